```python
import math
import jax, jax.numpy as jnp
from jax import lax
import numpy as np

D_MODEL = 1024
BATCH = 16
SEQ = 2048
DEPTH = 1

HEAD_DIM = 64
N_ATTN_HEADS = 12
ATTN_WIDTH = N_ATTN_HEADS * HEAD_DIM
CONV_WIDTH = D_MODEL - ATTN_WIDTH
IN_WIDTH = 3 * ATTN_WIDTH + 2 * CONV_WIDTH
CONV_KERNEL = 31
DILATED_CONFIGS = ((128, 1), (512, 4), (2048, 16))
ATTN_BLOCK = 128
REL_BUCKETS = 32
REL_MAX_DIST = 2048
D_FF = 2816
FFN_CONV_KERNEL = 3
ALPHA = (2 * DEPTH) ** 0.25
BETA = (8 * DEPTH) ** -0.25
LN_EPS = 1e-5
NEG_INF = -1e30

kernel_name = "hymba_dilated_conformer_convffn_deepnorm"


def _layer_norm(x, g, b):
    xf = x.astype(jnp.float32)
    mu = jnp.mean(xf, axis=-1, keepdims=True)
    var = jnp.mean(jnp.square(xf - mu), axis=-1, keepdims=True)
    y = (xf - mu) * lax.rsqrt(var + LN_EPS)
    return (y * g.astype(jnp.float32) + b.astype(jnp.float32)).astype(x.dtype)


def _rms_norm(x, g):
    xf = x.astype(jnp.float32)
    y = xf * lax.rsqrt(jnp.mean(jnp.square(xf), axis=-1, keepdims=True) + LN_EPS)
    return (y * g.astype(jnp.float32)).astype(x.dtype)


def _causal_dwconv(x, w, b):
    K, C = w.shape
    y = lax.conv_general_dilated(
        x, w[:, None, :].astype(x.dtype), window_strides=(1,), padding=[(K - 1, 0)],
        dimension_numbers=("NWC", "WIO", "NWC"), feature_group_count=C)
    return y + b.astype(x.dtype)


def _t5_bucket(dist):
    exact = REL_BUCKETS // 2
    d_f = jnp.maximum(dist, 1).astype(jnp.float32)
    large = exact + (jnp.log(d_f / exact) / math.log(REL_MAX_DIST / exact)
                     * (REL_BUCKETS - exact)).astype(jnp.int32)
    large = jnp.minimum(large, REL_BUCKETS - 1)
    return jnp.where(dist < exact, dist, large)


def _dilated_branch(q, k, v, rel_table, window, dilation):
    B, H, S, E = q.shape
    L = S // dilation
    nb = -(-L // ATTN_BLOCK)
    Lp = nb * ATTN_BLOCK
    max_steps = window // dilation
    scale = 1.0 / math.sqrt(E)

    def to_sub(t):
        t = t.reshape(B, H, L, dilation, E).transpose(0, 1, 3, 2, 4)
        t = jnp.pad(t, ((0, 0), (0, 0), (0, 0), (0, Lp - L), (0, 0)))
        return t.reshape(B, H, dilation, nb, ATTN_BLOCK, E)

    def with_prev(t):
        prev = jnp.pad(t, ((0, 0), (0, 0), (0, 0), (1, 0), (0, 0), (0, 0)))[:, :, :, :nb]
        return jnp.concatenate([prev, t], axis=4)

    qs = to_sub(q)
    kk = with_prev(to_sub(k))
    vv = with_prev(to_sub(v))
    s = jnp.einsum("bhrnqe,bhrnke->bhrnqk", qs, kk,
                   preferred_element_type=jnp.float32) * scale

    qi = jnp.arange(ATTN_BLOCK)[:, None]
    kj = jnp.arange(2 * ATTN_BLOCK)[None, :]
    steps = qi + ATTN_BLOCK - kj
    band = (steps >= 0) & (steps <= max_steps)
    has_prev = (jnp.arange(nb)[:, None, None] > 0) | (kj >= ATTN_BLOCK)[None]
    valid = band[None] & has_prev
    bucket = _t5_bucket(jnp.maximum(steps, 0) * dilation)
    bias = rel_table[bucket].astype(jnp.float32).transpose(2, 0, 1)
    s = s + bias[:, None, None]
    s = jnp.where(valid, s, NEG_INF)

    m = jnp.max(s, axis=-1)
    p = jnp.exp(s - m[..., None])
    l = jnp.sum(p, axis=-1)
    o = jnp.einsum("bhrnqk,bhrnke->bhrnqe", p, vv.astype(jnp.float32))

    def from_sub(t):
        tail = t.shape[5:]
        t = t.reshape((B, H, dilation, Lp) + tail)[:, :, :, :L]
        t = jnp.moveaxis(t, 2, 3)
        return t.reshape((B, H, S) + tail)

    return from_sub(o), from_sub(m), from_sub(l)


def _dilated_attention(q, k, v, rel_table):
    branches = [_dilated_branch(q, k, v, rel_table, w, d) for (w, d) in DILATED_CONFIGS]
    m_all = jnp.max(jnp.stack([b[1] for b in branches]), axis=0)
    num = 0.0
    den = 0.0
    for o_i, m_i, l_i in branches:
        c = jnp.exp(m_i - m_all)
        num = num + o_i * c[..., None]
        den = den + l_i * c
    return num / den[..., None]


def _layer(x, rel_table, w_in, b_in, conv_w, conv_b, conv_ln_g, conv_ln_b,
           attn_norm_g, conv_norm_g, w_out, ln1_g, ln1_b,
           w_up, ffn_conv_w, ffn_conv_b, w_down, ln2_g, ln2_b):
    B, S, _ = x.shape
    h = x @ w_in + b_in
    def heads(t):
        return t.reshape(B, S, N_ATTN_HEADS, HEAD_DIM).transpose(0, 2, 1, 3)
    q = heads(h[..., :ATTN_WIDTH])
    k = heads(h[..., ATTN_WIDTH:2 * ATTN_WIDTH])
    v = heads(h[..., 2 * ATTN_WIDTH:3 * ATTN_WIDTH])
    attn = _dilated_attention(q, k, v, rel_table)
    attn = attn.transpose(0, 2, 1, 3).reshape(B, S, ATTN_WIDTH).astype(x.dtype)

    a, g = jnp.split(h[..., 3 * ATTN_WIDTH:], 2, axis=-1)
    u = a * jax.nn.sigmoid(g)
    u = _causal_dwconv(u, conv_w, conv_b)
    u = jax.nn.silu(_layer_norm(u, conv_ln_g, conv_ln_b))

    mixed = jnp.concatenate([_rms_norm(attn, attn_norm_g), _rms_norm(u, conv_norm_g)], axis=-1)
    x = _layer_norm(ALPHA * x + mixed @ w_out, ln1_g, ln1_b)

    up = _causal_dwconv(x @ w_up, ffn_conv_w, ffn_conv_b)
    gate, val = jnp.split(up, 2, axis=-1)
    y = (jax.nn.silu(gate) * val) @ w_down
    return _layer_norm(ALPHA * x + y, ln2_g, ln2_b)


def _fwd_setup_inputs(seed: int = 0) -> dict:
    key = jax.random.key(seed)
    ks = jax.random.split(key, 20)
    f32 = jnp.float32
    nrm = lambda k, shape, s: jax.random.normal(k, shape, f32) * s
    w_in = nrm(ks[1], (DEPTH, D_MODEL, IN_WIDTH), D_MODEL ** -0.5)
    v_scale = jnp.ones((IN_WIDTH,), f32).at[2 * ATTN_WIDTH:3 * ATTN_WIDTH].set(BETA)
    w_in = w_in * v_scale
    return {
        "x": jax.random.normal(ks[0], (BATCH, SEQ, D_MODEL), f32),
        "rel_table": nrm(ks[2], (REL_BUCKETS, N_ATTN_HEADS), 0.5),
        "w_in": w_in,
        "b_in": nrm(ks[3], (DEPTH, IN_WIDTH), 0.02),
        "conv_w": nrm(ks[4], (DEPTH, CONV_KERNEL, CONV_WIDTH), CONV_KERNEL ** -0.5),
        "conv_b": nrm(ks[5], (DEPTH, CONV_WIDTH), 0.02),
        "conv_ln_g": 1.0 + nrm(ks[6], (DEPTH, CONV_WIDTH), 0.02),
        "conv_ln_b": nrm(ks[7], (DEPTH, CONV_WIDTH), 0.02),
        "attn_norm_g": 1.0 + nrm(ks[8], (DEPTH, ATTN_WIDTH), 0.02),
        "conv_norm_g": 1.0 + nrm(ks[9], (DEPTH, CONV_WIDTH), 0.02),
        "w_out": nrm(ks[10], (DEPTH, D_MODEL, D_MODEL), BETA * D_MODEL ** -0.5),
        "ln1_g": 1.0 + nrm(ks[11], (DEPTH, D_MODEL), 0.02),
        "ln1_b": nrm(ks[12], (DEPTH, D_MODEL), 0.02),
        "w_up": nrm(ks[13], (DEPTH, D_MODEL, 2 * D_FF), D_MODEL ** -0.5),
        "ffn_conv_w": nrm(ks[14], (DEPTH, FFN_CONV_KERNEL, 2 * D_FF), FFN_CONV_KERNEL ** -0.5),
        "ffn_conv_b": nrm(ks[15], (DEPTH, 2 * D_FF), 0.02),
        "w_down": nrm(ks[16], (DEPTH, D_FF, D_MODEL), BETA * D_FF ** -0.5),
        "ln2_g": 1.0 + nrm(ks[17], (DEPTH, D_MODEL), 0.02),
        "ln2_b": nrm(ks[18], (DEPTH, D_MODEL), 0.02),
    }


def _fwd_reference(x, rel_table, w_in, b_in, conv_w, conv_b, conv_ln_g, conv_ln_b,
              attn_norm_g, conv_norm_g, w_out, ln1_g, ln1_b,
              w_up, ffn_conv_w, ffn_conv_b, w_down, ln2_g, ln2_b):
    for i in range(DEPTH):
        x = _layer(x, rel_table, w_in[i], b_in[i], conv_w[i], conv_b[i], conv_ln_g[i],
                   conv_ln_b[i], attn_norm_g[i], conv_norm_g[i], w_out[i], ln1_g[i],
                   ln1_b[i], w_up[i], ffn_conv_w[i], ffn_conv_b[i], w_down[i],
                   ln2_g[i], ln2_b[i])
    return x


import jax as _jax
import jax.numpy as _jnp

TWIN_FORMAT = 'train_step'
FWD_PARAMS = ['x', 'rel_table', 'w_in', 'b_in', 'conv_w', 'conv_b', 'conv_ln_g', 'conv_ln_b', 'attn_norm_g', 'conv_norm_g', 'w_out', 'ln1_g', 'ln1_b', 'w_up', 'ffn_conv_w', 'ffn_conv_b', 'w_down', 'ln2_g', 'ln2_b']
TWIN_WEIGHTS = ['rel_table', 'w_in', 'b_in', 'conv_w', 'conv_b', 'conv_ln_g', 'conv_ln_b', 'attn_norm_g', 'conv_norm_g', 'w_out', 'ln1_g', 'ln1_b', 'w_up', 'ffn_conv_w', 'ffn_conv_b', 'w_down', 'ln2_g', 'ln2_b']
TWIN_DIFF_INPUT = 'x'
TWIN_INPUTS = ['x', 'rel_table', 'w_in', 'b_in', 'conv_w', 'conv_b', 'conv_ln_g', 'conv_ln_b', 'attn_norm_g', 'conv_norm_g', 'w_out', 'ln1_g', 'ln1_b', 'w_up', 'ffn_conv_w', 'ffn_conv_b', 'w_down', 'ln2_g', 'ln2_b', 'loss_target', 'm_rel_table', 'm_w_in', 'm_b_in', 'm_conv_w', 'm_conv_b', 'm_conv_ln_g', 'm_conv_ln_b', 'm_attn_norm_g', 'm_conv_norm_g', 'm_w_out', 'm_ln1_g', 'm_ln1_b', 'm_w_up', 'm_ffn_conv_w', 'm_ffn_conv_b', 'm_w_down', 'm_ln2_g', 'm_ln2_b', 'v_rel_table', 'v_w_in', 'v_b_in', 'v_conv_w', 'v_conv_b', 'v_conv_ln_g', 'v_conv_ln_b', 'v_attn_norm_g', 'v_conv_norm_g', 'v_w_out', 'v_ln1_g', 'v_ln1_b', 'v_w_up', 'v_ffn_conv_w', 'v_ffn_conv_b', 'v_w_down', 'v_ln2_g', 'v_ln2_b']
TWIN_OUTPUTS = ['loss', 'grad_x', 'grad_rel_table', 'grad_w_in', 'grad_b_in', 'grad_conv_w', 'grad_conv_b', 'grad_conv_ln_g', 'grad_conv_ln_b', 'grad_attn_norm_g', 'grad_conv_norm_g', 'grad_w_out', 'grad_ln1_g', 'grad_ln1_b', 'grad_w_up', 'grad_ffn_conv_w', 'grad_ffn_conv_b', 'grad_w_down', 'grad_ln2_g', 'grad_ln2_b', 'delta_rel_table', 'delta_w_in', 'delta_b_in', 'delta_conv_w', 'delta_conv_b', 'delta_conv_ln_g', 'delta_conv_ln_b', 'delta_attn_norm_g', 'delta_conv_norm_g', 'delta_w_out', 'delta_ln1_g', 'delta_ln1_b', 'delta_w_up', 'delta_ffn_conv_w', 'delta_ffn_conv_b', 'delta_w_down', 'delta_ln2_g', 'delta_ln2_b', 'new_m_rel_table', 'new_m_w_in', 'new_m_b_in', 'new_m_conv_w', 'new_m_conv_b', 'new_m_conv_ln_g', 'new_m_conv_ln_b', 'new_m_attn_norm_g', 'new_m_conv_norm_g', 'new_m_w_out', 'new_m_ln1_g', 'new_m_ln1_b', 'new_m_w_up', 'new_m_ffn_conv_w', 'new_m_ffn_conv_b', 'new_m_w_down', 'new_m_ln2_g', 'new_m_ln2_b', 'new_v_rel_table', 'new_v_w_in', 'new_v_b_in', 'new_v_conv_w', 'new_v_conv_b', 'new_v_conv_ln_g', 'new_v_conv_ln_b', 'new_v_attn_norm_g', 'new_v_conv_norm_g', 'new_v_w_out', 'new_v_ln1_g', 'new_v_ln1_b', 'new_v_w_up', 'new_v_ffn_conv_w', 'new_v_ffn_conv_b', 'new_v_w_down', 'new_v_ln2_g', 'new_v_ln2_b']
TWIN_LEAF_KINDS = {'loss': 'loss', 'grad_x': 'grad_x', 'grad_rel_table': 'grad_w', 'grad_w_in': 'grad_w', 'grad_b_in': 'grad_w', 'grad_conv_w': 'grad_w', 'grad_conv_b': 'grad_w', 'grad_conv_ln_g': 'grad_w', 'grad_conv_ln_b': 'grad_w', 'grad_attn_norm_g': 'grad_w', 'grad_conv_norm_g': 'grad_w', 'grad_w_out': 'grad_w', 'grad_ln1_g': 'grad_w', 'grad_ln1_b': 'grad_w', 'grad_w_up': 'grad_w', 'grad_ffn_conv_w': 'grad_w', 'grad_ffn_conv_b': 'grad_w', 'grad_w_down': 'grad_w', 'grad_ln2_g': 'grad_w', 'grad_ln2_b': 'grad_w', 'delta_rel_table': 'delta_w', 'delta_w_in': 'delta_w', 'delta_b_in': 'delta_w', 'delta_conv_w': 'delta_w', 'delta_conv_b': 'delta_w', 'delta_conv_ln_g': 'delta_w', 'delta_conv_ln_b': 'delta_w', 'delta_attn_norm_g': 'delta_w', 'delta_conv_norm_g': 'delta_w', 'delta_w_out': 'delta_w', 'delta_ln1_g': 'delta_w', 'delta_ln1_b': 'delta_w', 'delta_w_up': 'delta_w', 'delta_ffn_conv_w': 'delta_w', 'delta_ffn_conv_b': 'delta_w', 'delta_w_down': 'delta_w', 'delta_ln2_g': 'delta_w', 'delta_ln2_b': 'delta_w', 'new_m_rel_table': 'new_m', 'new_m_w_in': 'new_m', 'new_m_b_in': 'new_m', 'new_m_conv_w': 'new_m', 'new_m_conv_b': 'new_m', 'new_m_conv_ln_g': 'new_m', 'new_m_conv_ln_b': 'new_m', 'new_m_attn_norm_g': 'new_m', 'new_m_conv_norm_g': 'new_m', 'new_m_w_out': 'new_m', 'new_m_ln1_g': 'new_m', 'new_m_ln1_b': 'new_m', 'new_m_w_up': 'new_m', 'new_m_ffn_conv_w': 'new_m', 'new_m_ffn_conv_b': 'new_m', 'new_m_w_down': 'new_m', 'new_m_ln2_g': 'new_m', 'new_m_ln2_b': 'new_m', 'new_v_rel_table': 'new_v', 'new_v_w_in': 'new_v', 'new_v_b_in': 'new_v', 'new_v_conv_w': 'new_v', 'new_v_conv_b': 'new_v', 'new_v_conv_ln_g': 'new_v', 'new_v_conv_ln_b': 'new_v', 'new_v_attn_norm_g': 'new_v', 'new_v_conv_norm_g': 'new_v', 'new_v_w_out': 'new_v', 'new_v_ln1_g': 'new_v', 'new_v_ln1_b': 'new_v', 'new_v_w_up': 'new_v', 'new_v_ffn_conv_w': 'new_v', 'new_v_ffn_conv_b': 'new_v', 'new_v_w_down': 'new_v', 'new_v_ln2_g': 'new_v', 'new_v_ln2_b': 'new_v'}


def _forward(args):
    return _fwd_reference(*[args[k] for k in FWD_PARAMS])


def _output_shape():
    out = _jax.eval_shape(lambda: _forward(_fwd_setup_inputs(0)))
    return out.shape, out.dtype

N_MICROBATCH = 1
ADAM_LR = 0.001
ADAM_B1 = 0.9
ADAM_B2 = 0.999
ADAM_EPS = 1e-08
ADAM_WD = 0.01
ADAM_STEP = 10
PER_EXAMPLE_BATCH_AXIS = {'x': 0, 'loss_target': 0}
SHARED_INPUTS = []
_WEIGHT_DTYPES = {'rel_table': _jnp.float32, 'w_in': _jnp.float32, 'b_in': _jnp.float32, 'conv_w': _jnp.float32, 'conv_b': _jnp.float32, 'conv_ln_g': _jnp.float32, 'conv_ln_b': _jnp.float32, 'attn_norm_g': _jnp.float32, 'conv_norm_g': _jnp.float32, 'w_out': _jnp.float32, 'ln1_g': _jnp.float32, 'ln1_b': _jnp.float32, 'w_up': _jnp.float32, 'ffn_conv_w': _jnp.float32, 'ffn_conv_b': _jnp.float32, 'w_down': _jnp.float32, 'ln2_g': _jnp.float32, 'ln2_b': _jnp.float32}
MOMENT_SCALE = {'rel_table': 1.049101e-01, 'w_in': 9.535781e-02, 'b_in': 1.877331e+00, 'conv_w': 8.690848e-02, 'conv_b': 2.688997e-01, 'conv_ln_g': 1.258633e-01, 'conv_ln_b': 1.749835e-01, 'attn_norm_g': 8.718200e-02, 'conv_norm_g': 9.204590e-02, 'w_out': 1.568674e-01, 'ln1_g': 9.271859e-01, 'ln1_b': 4.562816e-01, 'w_up': 3.145693e-02, 'ffn_conv_w': 3.234273e-02, 'ffn_conv_b': 3.742938e-02, 'w_down': 8.706012e-02, 'ln2_g': 3.201772e+01, 'ln2_b': 4.159108e+00}


def _to_microbatches(a, axis):
    t = _jnp.moveaxis(a, axis, 0)
    t = t.reshape((N_MICROBATCH, t.shape[0] // N_MICROBATCH) + t.shape[1:])
    return _jnp.moveaxis(t, 1, axis + 1)


def setup_inputs(seed: int = 0) -> dict:
    inp = _fwd_setup_inputs(seed)
    key = _jax.random.fold_in(_jax.random.key(seed), 7919)
    shape, _ = _output_shape()
    out = dict(inp)
    out["loss_target"] = _jax.random.normal(_jax.random.fold_in(key, 0), shape, _jnp.float32)
    for i, name in enumerate(TWIN_WEIGHTS):
        w = inp[name].astype(_jnp.float32)
        if MOMENT_SCALE is None:
            s = _jnp.sqrt(_jnp.mean(_jnp.square(w)) + 1e-30)
        else:
            s = MOMENT_SCALE[name]
        km, kv = _jax.random.split(_jax.random.fold_in(key, i + 1))
        out[name] = w
        out["m_" + name] = s * _jax.random.normal(km, w.shape, _jnp.float32)
        out["v_" + name] = (s * s) * _jax.random.uniform(kv, w.shape, _jnp.float32, 0.5, 1.5)
    if N_MICROBATCH > 1:
        for name, axis in PER_EXAMPLE_BATCH_AXIS.items():
            out[name] = _to_microbatches(out[name], axis)
    return {'x': out['x'], 'rel_table': out['rel_table'], 'w_in': out['w_in'], 'b_in': out['b_in'], 'conv_w': out['conv_w'], 'conv_b': out['conv_b'], 'conv_ln_g': out['conv_ln_g'], 'conv_ln_b': out['conv_ln_b'], 'attn_norm_g': out['attn_norm_g'], 'conv_norm_g': out['conv_norm_g'], 'w_out': out['w_out'], 'ln1_g': out['ln1_g'], 'ln1_b': out['ln1_b'], 'w_up': out['w_up'], 'ffn_conv_w': out['ffn_conv_w'], 'ffn_conv_b': out['ffn_conv_b'], 'w_down': out['w_down'], 'ln2_g': out['ln2_g'], 'ln2_b': out['ln2_b'], 'loss_target': out['loss_target'], 'm_rel_table': out['m_rel_table'], 'm_w_in': out['m_w_in'], 'm_b_in': out['m_b_in'], 'm_conv_w': out['m_conv_w'], 'm_conv_b': out['m_conv_b'], 'm_conv_ln_g': out['m_conv_ln_g'], 'm_conv_ln_b': out['m_conv_ln_b'], 'm_attn_norm_g': out['m_attn_norm_g'], 'm_conv_norm_g': out['m_conv_norm_g'], 'm_w_out': out['m_w_out'], 'm_ln1_g': out['m_ln1_g'], 'm_ln1_b': out['m_ln1_b'], 'm_w_up': out['m_w_up'], 'm_ffn_conv_w': out['m_ffn_conv_w'], 'm_ffn_conv_b': out['m_ffn_conv_b'], 'm_w_down': out['m_w_down'], 'm_ln2_g': out['m_ln2_g'], 'm_ln2_b': out['m_ln2_b'], 'v_rel_table': out['v_rel_table'], 'v_w_in': out['v_w_in'], 'v_b_in': out['v_b_in'], 'v_conv_w': out['v_conv_w'], 'v_conv_b': out['v_conv_b'], 'v_conv_ln_g': out['v_conv_ln_g'], 'v_conv_ln_b': out['v_conv_ln_b'], 'v_attn_norm_g': out['v_attn_norm_g'], 'v_conv_norm_g': out['v_conv_norm_g'], 'v_w_out': out['v_w_out'], 'v_ln1_g': out['v_ln1_g'], 'v_ln1_b': out['v_ln1_b'], 'v_w_up': out['v_w_up'], 'v_ffn_conv_w': out['v_ffn_conv_w'], 'v_ffn_conv_b': out['v_ffn_conv_b'], 'v_w_down': out['v_w_down'], 'v_ln2_g': out['v_ln2_g'], 'v_ln2_b': out['v_ln2_b']}


def _loss(weights, diff, rest, loss_target):
    with _jax.named_scope("forward"):
        args = {**rest, TWIN_DIFF_INPUT: diff, **{k: w.astype(_WEIGHT_DTYPES[k]) for k, w in weights.items()}}
        y = _forward(args)
    with _jax.named_scope("loss_head"):
        err = _jnp.square(y.astype(_jnp.float32) - loss_target)
        return 0.5 * _jnp.sum(_jnp.mean(err, axis=-1)) if err.ndim else 0.5 * err


def _adamw(w, g, m, v):
    m = ADAM_B1 * m + (1.0 - ADAM_B1) * g
    v = ADAM_B2 * v + (1.0 - ADAM_B2) * _jnp.square(g)
    m_hat = m / (1.0 - ADAM_B1 ** ADAM_STEP)
    v_hat = v / (1.0 - ADAM_B2 ** ADAM_STEP)
    delta = -ADAM_LR * (m_hat / (_jnp.sqrt(v_hat) + ADAM_EPS) + ADAM_WD * w)
    return delta, m, v


def reference(x, rel_table, w_in, b_in, conv_w, conv_b, conv_ln_g, conv_ln_b, attn_norm_g, conv_norm_g, w_out, ln1_g, ln1_b, w_up, ffn_conv_w, ffn_conv_b, w_down, ln2_g, ln2_b, loss_target, m_rel_table, m_w_in, m_b_in, m_conv_w, m_conv_b, m_conv_ln_g, m_conv_ln_b, m_attn_norm_g, m_conv_norm_g, m_w_out, m_ln1_g, m_ln1_b, m_w_up, m_ffn_conv_w, m_ffn_conv_b, m_w_down, m_ln2_g, m_ln2_b, v_rel_table, v_w_in, v_b_in, v_conv_w, v_conv_b, v_conv_ln_g, v_conv_ln_b, v_attn_norm_g, v_conv_norm_g, v_w_out, v_ln1_g, v_ln1_b, v_w_up, v_ffn_conv_w, v_ffn_conv_b, v_w_down, v_ln2_g, v_ln2_b):
    given = dict(x=x, rel_table=rel_table, w_in=w_in, b_in=b_in, conv_w=conv_w, conv_b=conv_b, conv_ln_g=conv_ln_g, conv_ln_b=conv_ln_b, attn_norm_g=attn_norm_g, conv_norm_g=conv_norm_g, w_out=w_out, ln1_g=ln1_g, ln1_b=ln1_b, w_up=w_up, ffn_conv_w=ffn_conv_w, ffn_conv_b=ffn_conv_b, w_down=w_down, ln2_g=ln2_g, ln2_b=ln2_b, loss_target=loss_target, m_rel_table=m_rel_table, m_w_in=m_w_in, m_b_in=m_b_in, m_conv_w=m_conv_w, m_conv_b=m_conv_b, m_conv_ln_g=m_conv_ln_g, m_conv_ln_b=m_conv_ln_b, m_attn_norm_g=m_attn_norm_g, m_conv_norm_g=m_conv_norm_g, m_w_out=m_w_out, m_ln1_g=m_ln1_g, m_ln1_b=m_ln1_b, m_w_up=m_w_up, m_ffn_conv_w=m_ffn_conv_w, m_ffn_conv_b=m_ffn_conv_b, m_w_down=m_w_down, m_ln2_g=m_ln2_g, m_ln2_b=m_ln2_b, v_rel_table=v_rel_table, v_w_in=v_w_in, v_b_in=v_b_in, v_conv_w=v_conv_w, v_conv_b=v_conv_b, v_conv_ln_g=v_conv_ln_g, v_conv_ln_b=v_conv_ln_b, v_attn_norm_g=v_attn_norm_g, v_conv_norm_g=v_conv_norm_g, v_w_out=v_w_out, v_ln1_g=v_ln1_g, v_ln1_b=v_ln1_b, v_w_up=v_w_up, v_ffn_conv_w=v_ffn_conv_w, v_ffn_conv_b=v_ffn_conv_b, v_w_down=v_w_down, v_ln2_g=v_ln2_g, v_ln2_b=v_ln2_b)
    weights = {n: given[n] for n in TWIN_WEIGHTS}
    shared = {n: given[n] for n in SHARED_INPUTS}
    per_example = {n: given[n] for n in ['x']}
    grad_fn = _jax.value_and_grad(_loss, argnums=(0, 1))

    def one_microbatch(ex, loss_target):
        ex = dict(ex)
        diff = ex.pop(TWIN_DIFF_INPUT)
        return grad_fn(weights, diff, {**shared, **ex}, loss_target)

    if N_MICROBATCH == 1:
        loss, (grad_w, grad_x) = one_microbatch(per_example, given["loss_target"])
    else:
        def body(carry, xs):
            loss_sum, grad_sum = carry
            l_k, (gw_k, gx_k) = one_microbatch(xs[0], xs[1])
            with _jax.named_scope("update"):
                return (loss_sum + l_k, _jax.tree.map(_jnp.add, grad_sum, gw_k)), gx_k

        init = (_jnp.zeros((), _jnp.float32), _jax.tree.map(_jnp.zeros_like, weights))
        (loss, grad_w), grad_x = _jax.lax.scan(body, init, (per_example, given["loss_target"]))
    with _jax.named_scope("update"):
        delta_w, new_m, new_v = {}, {}, {}
        for n in TWIN_WEIGHTS:
            delta_w[n], new_m[n], new_v[n] = _adamw(weights[n], grad_w[n], given["m_" + n], given["v_" + n])
    return (loss, grad_x, *[grad_w[n] for n in TWIN_WEIGHTS], *[delta_w[n] for n in TWIN_WEIGHTS],
            *[new_m[n] for n in TWIN_WEIGHTS], *[new_v[n] for n in TWIN_WEIGHTS])
```

```python
import functools
import math

import numpy as np
import jax
import jax.numpy as jnp
from jax import lax
from jax.experimental import pallas as pl
from jax.experimental.pallas import tpu as pltpu

F32 = jnp.float32
BF16 = jnp.bfloat16
MESH = pl.DeviceIdType.MESH

HEAD_DIM = 64
LANES = 128
ATTN_BLOCK = 128
DILATED_CONFIGS = ((128, 1), (512, 4), (2048, 16))
CONV_KERNEL = 31
FFN_CONV_KERNEL = 3
REL_BUCKETS = 32
REL_MAX_DIST = 2048
DEPTH = 1
ALPHA = (2 * DEPTH) ** 0.25
LN_EPS = 1e-5
NEG_INF = -1e30
QK_SCALE = 1.0 / math.sqrt(HEAD_DIM)
ADAM_LR = 0.001
ADAM_B1 = 0.9
ADAM_B2 = 0.999
ADAM_EPS = 1e-08
ADAM_WD = 0.01
ADAM_STEP = 10
VMEM_LIMIT = 52 * 1024 * 1024
FFN_COLS = 256
N_SHARDS = 4
N_DEV = 8


def _params(n_axes):
    return pltpu.CompilerParams(dimension_semantics=("arbitrary",) * n_axes,
                                vmem_limit_bytes=VMEM_LIMIT)


def _matmul(a, b, *, mode, tm, tn, tk, outs, epilogue, extras=(), name):
    if mode == "tn":
        K, M = a.shape
        N = b.shape[1]
        a_spec = pl.BlockSpec((tk, tm), lambda i, j, k: (k, i))
        b_spec = pl.BlockSpec((tk, tn), lambda i, j, k: (k, j))
        dims = (((0,), (0,)), ((), ()))
    elif mode == "nt":
        M, K = a.shape
        N = b.shape[0]
        a_spec = pl.BlockSpec((tm, tk), lambda i, j, k: (i, k))
        b_spec = pl.BlockSpec((tn, tk), lambda i, j, k: (j, k))
        dims = (((1,), (1,)), ((), ()))
    else:
        M, K = a.shape
        N = b.shape[1]
        a_spec = pl.BlockSpec((tm, tk), lambda i, j, k: (i, k))
        b_spec = pl.BlockSpec((tk, tn), lambda i, j, k: (k, j))
        dims = (((1,), (0,)), ((), ()))
    assert M % tm == 0 and N % tn == 0 and K % tk == 0, (name, M, N, K, tm, tn, tk)
    nk = K // tk
    n_extra = len(extras)

    def body(a_ref, b_ref, *rest):
        extra_refs = rest[:n_extra]
        out_refs = rest[n_extra:n_extra + len(outs)]
        acc_ref = rest[-1]
        i, j, k = pl.program_id(0), pl.program_id(1), pl.program_id(2)
        part = lax.dot_general(a_ref[...].astype(BF16), b_ref[...].astype(BF16), dims,
                               preferred_element_type=F32)
        if nk == 1:
            epilogue(part, i, j, extra_refs, out_refs)
        else:
            @pl.when(k == 0)
            def _():
                acc_ref[...] = part

            @pl.when(k > 0)
            def _():
                acc_ref[...] += part

            @pl.when(k == nk - 1)
            def _():
                epilogue(acc_ref[...], i, j, extra_refs, out_refs)

    in_specs = [a_spec, b_spec] + [pl.BlockSpec(bs, im) for (_, bs, im) in extras]
    out_specs = [pl.BlockSpec(bs, im) for (_, _, bs, im) in outs]
    out_shape = [jax.ShapeDtypeStruct(s, d) for (s, d, _, _) in outs]
    res = pl.pallas_call(
        body, grid=(M // tm, N // tn, nk), in_specs=in_specs, out_specs=out_specs,
        out_shape=out_shape, scratch_shapes=[pltpu.VMEM((tm, tn), F32)],
        compiler_params=_params(3), name=name,
    )(a, b, *[e[0] for e in extras])
    return res


def _plain_out(M, N, tm, tn, dtype):
    return ((M, N), dtype, (tm, tn), lambda i, j, k: (i, j))


def _mm_plain(a, b, *, mode, tm, tn, tk, out_dtype, name, bias=None):
    if mode == "tn":
        M, N = a.shape[1], b.shape[1]
    elif mode == "nt":
        M, N = a.shape[0], b.shape[0]
    else:
        M, N = a.shape[0], b.shape[1]
    extras = []
    if bias is not None:
        extras.append((bias, (1, tn), lambda i, j, k: (0, j)))

    def epilogue(acc, i, j, extra_refs, out_refs):
        if bias is not None:
            acc = acc + extra_refs[0][...]
        out_refs[0][...] = acc.astype(out_dtype)

    return _matmul(a, b, mode=mode, tm=tm, tn=tn, tk=tk, outs=[_plain_out(M, N, tm, tn, out_dtype)],
                   epilogue=epilogue, extras=extras, name=name)[0]


def _row_tile(T, want):
    t = min(T, want)
    while T % t:
        t //= 2
    return t


def _col_tile(N, want):
    if N <= want:
        return N
    best = None
    for c in range(LANES, want + 1, LANES):
        if N % c == 0:
            best = c
    return best if best is not None else N


def _accumulate(ref, first, val):
    @pl.when(first)
    def _():
        ref[...] = val

    @pl.when(jnp.logical_not(first))
    def _():
        ref[...] += val


def _ln_fwd(z, g, b):
    mu = jnp.mean(z, axis=-1, keepdims=True)
    zc = z - mu
    var = jnp.mean(zc * zc, axis=-1, keepdims=True)
    r = lax.rsqrt(var + LN_EPS)
    xh = zc * r
    return xh * g + b, xh, r


def _ln_bwd(dy, xh, r, g):
    dxh = dy * g
    m1 = jnp.mean(dxh, axis=-1, keepdims=True)
    m2 = jnp.mean(dxh * xh, axis=-1, keepdims=True)
    return r * (dxh - m1 - xh * m2)


def _sigmoid(x):
    return 1.0 / (1.0 + jnp.exp(-x))


def _shift_down(x, s, row):
    if s == 0:
        return x
    return jnp.where(row >= s, pltpu.roll(x, s, 0), 0.0)


def _shift_up(x, s, row):
    if s == 0:
        return x
    n = x.shape[0]
    return jnp.where(row < n - s, pltpu.roll(x, n - s, 0), 0.0)


def _bucket_tables():
    exact = REL_BUCKETS // 2
    qi = np.arange(ATTN_BLOCK)[:, None]
    kj = np.arange(2 * ATTN_BLOCK)[None, :]
    steps = qi + ATTN_BLOCK - kj
    buckets, masks = [], []
    for window, dilation in DILATED_CONFIGS:
        max_steps = window // dilation
        band = (steps >= 0) & (steps <= max_steps)
        dist = np.maximum(steps, 0) * dilation
        d_f = np.maximum(dist, 1).astype(np.float32)
        large = exact + (np.log(d_f / np.float32(exact)) / np.float32(math.log(REL_MAX_DIST / exact))
                         * np.float32(REL_BUCKETS - exact)).astype(np.int32)
        large = np.minimum(large, REL_BUCKETS - 1)
        bucket = np.where(dist < exact, dist, large).astype(np.int32)
        buckets.append(bucket.reshape(1, -1))
        masks.append(np.where(band, 0.0, NEG_INF).astype(np.float32).reshape(1, -1))
    return np.stack(buckets), np.stack(masks)


def _split_hi_lo(x):
    hi = x.astype(BF16)
    lo = (x - hi.astype(F32)).astype(BF16)
    return hi, lo


def _bias_build(rel_table_t, bucket, mask):
    H = rel_table_t.shape[0]
    n = bucket.shape[-1]

    def body(t_ref, bkt_ref, mask_ref, o_ref):
        onehot = (lax.broadcasted_iota(jnp.int32, (REL_BUCKETS, n), 0) == bkt_ref[0]).astype(BF16)
        t = t_ref[...]
        t1 = t.astype(BF16)
        r1 = t - t1.astype(F32)
        t2 = r1.astype(BF16)
        t3 = (r1 - t2.astype(F32)).astype(BF16)
        acc = jnp.dot(t1, onehot, preferred_element_type=F32)
        acc = acc + jnp.dot(t2, onehot, preferred_element_type=F32)
        acc = acc + jnp.dot(t3, onehot, preferred_element_type=F32)
        o_ref[0] = acc + mask_ref[0]

    return pl.pallas_call(
        body, grid=(3,),
        in_specs=[pl.BlockSpec((H, REL_BUCKETS), lambda b: (0, 0)),
                  pl.BlockSpec((1, 1, n), lambda b: (b, 0, 0)),
                  pl.BlockSpec((1, 1, n), lambda b: (b, 0, 0))],
        out_specs=pl.BlockSpec((1, H, n), lambda b: (b, 0, 0)),
        out_shape=jax.ShapeDtypeStruct((3, H, n), F32),
        compiler_params=_params(1), name="bias_build",
    )(rel_table_t, bucket, mask)


def _rel_grad(dbias, bucket):
    H = dbias.shape[1]
    n = bucket.shape[-1]
    dims = (((1,), (1,)), ((), ()))

    def body(d_ref, bkt_ref, o_ref):
        b = pl.program_id(0)
        onehot = (lax.broadcasted_iota(jnp.int32, (REL_BUCKETS, n), 0) == bkt_ref[0]).astype(BF16)
        d = d_ref[0]
        d1 = d.astype(BF16)
        r1 = d - d1.astype(F32)
        d2 = r1.astype(BF16)
        d3 = (r1 - d2.astype(F32)).astype(BF16)
        acc = lax.dot_general(d1, onehot, dims, preferred_element_type=F32)
        acc = acc + lax.dot_general(d2, onehot, dims, preferred_element_type=F32)
        acc = acc + lax.dot_general(d3, onehot, dims, preferred_element_type=F32)
        _accumulate(o_ref, b == 0, acc)

    return pl.pallas_call(
        body, grid=(3,),
        in_specs=[pl.BlockSpec((1, H, n), lambda b: (b, 0, 0)),
                  pl.BlockSpec((1, 1, n), lambda b: (b, 0, 0))],
        out_specs=pl.BlockSpec((H, REL_BUCKETS), lambda b: (0, 0)),
        out_shape=jax.ShapeDtypeStruct((H, REL_BUCKETS), F32),
        compiler_params=_params(1), name="rel_grad",
    )(dbias, bucket)


def _attn_specs(B, S, AW, d):
    L = S // d
    HP = AW // LANES
    W3 = 3 * HP
    q_spec = pl.BlockSpec((1, L, LANES), lambda h, b, r: (b, 0, r * W3 + h))
    k_spec = pl.BlockSpec((1, L, LANES), lambda h, b, r: (b, 0, r * W3 + HP + h))
    v_spec = pl.BlockSpec((1, L, LANES), lambda h, b, r: (b, 0, r * W3 + 2 * HP + h))
    o_spec = pl.BlockSpec((1, L, LANES), lambda h, b, r: (b, 0, r * HP + h))
    bias_spec = pl.BlockSpec((2, ATTN_BLOCK, 2 * ATTN_BLOCK), lambda h, b, r: (h, 0, 0))
    return L, HP, q_spec, k_spec, v_spec, o_spec, bias_spec


def _attn_fwd(qkv, bias, B, S, AW, d, name):
    L, HP, q_spec, k_spec, v_spec, o_spec, bias_spec = _attn_specs(B, S, AW, d)
    nb = L // ATTN_BLOCK
    nt = (((1,), (1,)), ((), ()))

    def body(q_ref, k_ref, v_ref, b_ref, o_ref, lse_ref):
        head0 = lax.broadcasted_iota(jnp.int32, (1, LANES), 1) < HEAD_DIM

        def block(n, first):
            qs = pl.multiple_of(n * ATTN_BLOCK, ATTN_BLOCK)
            q = q_ref[0, pl.ds(qs, ATTN_BLOCK), :]
            if first:
                kk = k_ref[0, pl.ds(0, ATTN_BLOCK), :]
                vv = v_ref[0, pl.ds(0, ATTN_BLOCK), :]
            else:
                ks = pl.multiple_of(n * ATTN_BLOCK - ATTN_BLOCK, ATTN_BLOCK)
                kk = k_ref[0, pl.ds(ks, 2 * ATTN_BLOCK), :]
                vv = v_ref[0, pl.ds(ks, 2 * ATTN_BLOCK), :]
            outs, lses = [], []
            for e in range(2):
                msk = head0 if e == 0 else jnp.logical_not(head0)
                qe = jnp.where(msk, q, jnp.zeros_like(q))
                s = lax.dot_general(qe, kk, nt, preferred_element_type=F32) * QK_SCALE
                s = s + (b_ref[e, :, ATTN_BLOCK:] if first else b_ref[e])
                m = jnp.max(s, axis=-1, keepdims=True)
                p = jnp.exp(s - m)
                l = jnp.sum(p, axis=-1, keepdims=True)
                o = jnp.dot(p.astype(BF16), vv, preferred_element_type=F32)
                outs.append(o / l)
                lses.append(jnp.broadcast_to(m + jnp.log(l), (ATTN_BLOCK, LANES)))
            o_ref[0, pl.ds(qs, ATTN_BLOCK), :] = jnp.where(head0, outs[0], outs[1])
            lse_ref[0, pl.ds(qs, ATTN_BLOCK), :] = jnp.where(head0, lses[0], lses[1])

        block(0, True)
        if nb > 1:
            def loop(n, c):
                block(n, False)
                return c
            lax.fori_loop(1, nb, loop, 0)

    qv = qkv.reshape(B, L, d * 3 * AW)
    o, lse = pl.pallas_call(
        body, grid=(HP, B, d), in_specs=[q_spec, k_spec, v_spec, bias_spec],
        out_specs=[o_spec, o_spec],
        out_shape=[jax.ShapeDtypeStruct((B, L, d * AW), F32)] * 2,
        compiler_params=_params(3), name=name,
    )(qv, qv, qv, bias)
    return o.reshape(B * S, AW), lse.reshape(B * S, AW)


def _attn_bwd(qkv, do, lse, dd, bias, B, S, AW, d, name):
    L, HP, q_spec, k_spec, v_spec, o_spec, bias_spec = _attn_specs(B, S, AW, d)
    nb = L // ATTN_BLOCK
    nt = (((1,), (1,)), ((), ()))
    tn = (((0,), (0,)), ((), ()))

    def body(q_ref, k_ref, v_ref, do_ref, lse_ref, dd_ref, b_ref, dq_ref, dk_ref, dv_ref, db_ref):
        head0 = lax.broadcasted_iota(jnp.int32, (1, LANES), 1) < HEAD_DIM
        first_step = jnp.logical_and(pl.program_id(1) == 0, pl.program_id(2) == 0)

        @pl.when(first_step)
        def _():
            db_ref[...] = jnp.zeros_like(db_ref)

        dk_ref[...] = jnp.zeros_like(dk_ref)
        dv_ref[...] = jnp.zeros_like(dv_ref)

        def block(n, first):
            qs = pl.multiple_of(n * ATTN_BLOCK, ATTN_BLOCK)
            nkeys = ATTN_BLOCK if first else 2 * ATTN_BLOCK
            ks = 0 if first else pl.multiple_of(n * ATTN_BLOCK - ATTN_BLOCK, ATTN_BLOCK)
            q = q_ref[0, pl.ds(qs, ATTN_BLOCK), :]
            kk = k_ref[0, pl.ds(ks, nkeys), :]
            vv = v_ref[0, pl.ds(ks, nkeys), :]
            dout = do_ref[0, pl.ds(qs, ATTN_BLOCK), :]
            lse_b = lse_ref[0, pl.ds(qs, ATTN_BLOCK), :]
            dd_b = dd_ref[0, pl.ds(qs, ATTN_BLOCK), :]
            dq = jnp.zeros((ATTN_BLOCK, LANES), F32)
            dkk = jnp.zeros((nkeys, LANES), F32)
            dvv = jnp.zeros((nkeys, LANES), F32)
            for e in range(2):
                msk = head0 if e == 0 else jnp.logical_not(head0)
                c0 = e * HEAD_DIM
                qe = jnp.where(msk, q, jnp.zeros_like(q))
                doe = jnp.where(msk, dout, jnp.zeros_like(dout))
                kke = jnp.where(msk, kk, jnp.zeros_like(kk))
                s = lax.dot_general(qe, kk, nt, preferred_element_type=F32) * QK_SCALE
                s = s + (b_ref[e, :, ATTN_BLOCK:] if first else b_ref[e])
                p = jnp.exp(s - lse_b[:, c0:c0 + 1])
                dp = lax.dot_general(doe, vv, nt, preferred_element_type=F32)
                ds = p * (dp - dd_b[:, c0:c0 + 1])
                if first:
                    db_ref[e, :, ATTN_BLOCK:] += ds
                else:
                    db_ref[e] += ds
                dsb = (ds * QK_SCALE).astype(BF16)
                dq = dq + jnp.dot(dsb, kke, preferred_element_type=F32)
                dkk = dkk + lax.dot_general(dsb, qe, tn, preferred_element_type=F32)
                dvv = dvv + lax.dot_general(p.astype(BF16), doe, tn, preferred_element_type=F32)
            dq_ref[0, pl.ds(qs, ATTN_BLOCK), :] = dq
            dk_ref[0, pl.ds(ks, nkeys), :] += dkk
            dv_ref[0, pl.ds(ks, nkeys), :] += dvv

        block(0, True)
        if nb > 1:
            def loop(n, c):
                block(n, False)
                return c
            lax.fori_loop(1, nb, loop, 0)

    H = AW // HEAD_DIM
    qv = qkv.reshape(B, L, d * 3 * AW)
    view = lambda t: t.reshape(B, L, d * AW)
    dq, dk, dv, db = pl.pallas_call(
        body, grid=(HP, B, d),
        in_specs=[q_spec, k_spec, v_spec, o_spec, o_spec, o_spec, bias_spec],
        out_specs=[o_spec, o_spec, o_spec, bias_spec],
        out_shape=[jax.ShapeDtypeStruct((B, L, d * AW), F32)] * 3
        + [jax.ShapeDtypeStruct((H, ATTN_BLOCK, 2 * ATTN_BLOCK), F32)],
        compiler_params=_params(3), name=name,
    )(qv, qv, qv, view(do), view(lse), view(dd), bias)
    flat = lambda t: t.reshape(B * S, AW)
    return flat(dq), flat(dk), flat(dv), db


def _attn_combine(ons, lses, gain, tm):
    T, AW = ons[0].shape

    def body(o1, o2, o3, l1, l2, l3, g_ref, attn_ref, lse_ref, mix_ref, r_ref):
        la, lb, lc = l1[...], l2[...], l3[...]
        m = jnp.maximum(jnp.maximum(la, lb), lc)
        ea, eb, ec = jnp.exp(la - m), jnp.exp(lb - m), jnp.exp(lc - m)
        den = ea + eb + ec
        attn = (ea * o1[...] + eb * o2[...] + ec * o3[...]) / den
        attn_ref[...] = attn
        lse_ref[...] = m + jnp.log(den)
        r = lax.rsqrt(jnp.mean(attn * attn, axis=-1, keepdims=True) + LN_EPS)
        mix_ref[...] = (attn * r * g_ref[...]).astype(BF16)
        r_ref[...] = jnp.broadcast_to(r, (tm, LANES))

    row = pl.BlockSpec((tm, AW), lambda i: (i, 0))
    return pl.pallas_call(
        body, grid=(T // tm,),
        in_specs=[row] * 6 + [pl.BlockSpec((1, AW), lambda i: (0, 0))],
        out_specs=[row, row, row, pl.BlockSpec((tm, LANES), lambda i: (i, 0))],
        out_shape=[jax.ShapeDtypeStruct((T, AW), F32), jax.ShapeDtypeStruct((T, AW), F32),
                   jax.ShapeDtypeStruct((T, AW), BF16), jax.ShapeDtypeStruct((T, LANES), F32)],
        compiler_params=_params(1), name="attn_combine",
    )(*ons, *lses, gain)


def _attn_pre_bwd(dmixed, attn, rstd, gain, tm):
    T, AW = attn.shape
    ones_np = np.kron(np.eye(AW // HEAD_DIM, dtype=np.float32), np.ones((HEAD_DIM, HEAD_DIM), np.float32))
    ones_bd = jnp.asarray(ones_np, dtype=BF16)

    def body(dm_ref, a_ref, r_ref, g_ref, ones_ref, do_ref, dd_ref, dg_ref):
        i = pl.program_id(0)
        dm = dm_ref[...]
        a = a_ref[...]
        r = r_ref[:, 0:1]
        dxn = dm * g_ref[...]
        da = r * (dxn - a * (r * r) * jnp.mean(dxn * a, axis=-1, keepdims=True))
        do_ref[...] = da.astype(BF16)
        hi, lo = _split_hi_lo(da * a)
        dd_ref[...] = (jnp.dot(hi, ones_ref[...], preferred_element_type=F32)
                       + jnp.dot(lo, ones_ref[...], preferred_element_type=F32))
        _accumulate(dg_ref, i == 0, jnp.sum(dm * a * r, axis=0, keepdims=True))

    row = pl.BlockSpec((tm, AW), lambda i: (i, 0))
    vec = pl.BlockSpec((1, AW), lambda i: (0, 0))
    return pl.pallas_call(
        body, grid=(T // tm,),
        in_specs=[row, row, pl.BlockSpec((tm, LANES), lambda i: (i, 0)), vec,
                  pl.BlockSpec((AW, AW), lambda i: (0, 0))],
        out_specs=[row, row, vec],
        out_shape=[jax.ShapeDtypeStruct((T, AW), BF16), jax.ShapeDtypeStruct((T, AW), F32),
                   jax.ShapeDtypeStruct((1, AW), F32)],
        compiler_params=_params(1), name="attn_pre_bwd",
    )(dmixed, attn, rstd, gain, ones_bd)


def _conv_branch_fwd_math(a, g, w_ref, cb, lg, lb, row):
    sg = _sigmoid(g)
    u0 = a * sg
    uc = jnp.zeros_like(u0) + cb
    for k in range(CONV_KERNEL):
        uc = uc + w_ref[k:k + 1, :] * _shift_down(u0, CONV_KERNEL - 1 - k, row)
    ul, xh, r = _ln_fwd(uc, lg, lb)
    su = _sigmoid(ul)
    u = ul * su
    return sg, u0, ul, xh, r, su, u


def _conv_fwd(ag, conv_w, conv_b, ln_g, ln_b, norm_g, B, S, CW):
    def body(a_ref, g_ref, w_ref, cb_ref, lg_ref, lb_ref, ng_ref, o_ref):
        row = lax.broadcasted_iota(jnp.int32, (S, CW), 0)
        _, _, _, _, _, _, u = _conv_branch_fwd_math(a_ref[0], g_ref[0], w_ref, cb_ref[...], lg_ref[...],
                                                    lb_ref[...], row)
        rr = lax.rsqrt(jnp.mean(u * u, axis=-1, keepdims=True) + LN_EPS)
        o_ref[0] = (u * rr * ng_ref[...]).astype(BF16)

    vec = pl.BlockSpec((1, CW), lambda b: (0, 0))
    out = pl.pallas_call(
        body, grid=(B,),
        in_specs=[pl.BlockSpec((1, S, CW), lambda b: (b, 0, 0)), pl.BlockSpec((1, S, CW), lambda b: (b, 0, 1)),
                  pl.BlockSpec((CONV_KERNEL, CW), lambda b: (0, 0)), vec, vec, vec, vec],
        out_specs=pl.BlockSpec((1, S, CW), lambda b: (b, 0, 0)),
        out_shape=jax.ShapeDtypeStruct((B, S, CW), BF16),
        compiler_params=_params(1), name="conv_fwd",
    )(ag.reshape(B, S, 2 * CW), ag.reshape(B, S, 2 * CW), conv_w, conv_b, ln_g, ln_b, norm_g)
    return out.reshape(B * S, CW)


def _conv_bwd(ag, dmixed, conv_w, conv_b, ln_g, ln_b, norm_g, B, S, CW, D):
    AW = D - CW
    assert AW % CW == 0

    def body(a_ref, g_ref, dm_ref, w_ref, cb_ref, lg_ref, lb_ref, ng_ref,
             dag_ref, dw_ref, dcb_ref, dlg_ref, dlb_ref, dng_ref):
        b = pl.program_id(0)
        row = lax.broadcasted_iota(jnp.int32, (S, CW), 0)
        a, g = a_ref[0], g_ref[0]
        sg, u0, ul, xh, r, su, u = _conv_branch_fwd_math(a, g, w_ref, cb_ref[...], lg_ref[...], lb_ref[...], row)
        rr = lax.rsqrt(jnp.mean(u * u, axis=-1, keepdims=True) + LN_EPS)
        dm = dm_ref[0]
        dxn = dm * ng_ref[...]
        du = rr * (dxn - u * (rr * rr) * jnp.mean(dxn * u, axis=-1, keepdims=True))
        dul = du * su * (1.0 + ul * (1.0 - su))
        duc = _ln_bwd(dul, xh, r, lg_ref[...])
        first = b == 0
        _accumulate(dng_ref, first, jnp.sum(dm * u * rr, axis=0, keepdims=True))
        _accumulate(dlg_ref, first, jnp.sum(dul * xh, axis=0, keepdims=True))
        _accumulate(dlb_ref, first, jnp.sum(dul, axis=0, keepdims=True))
        _accumulate(dcb_ref, first, jnp.sum(duc, axis=0, keepdims=True))

        @pl.when(first)
        def _():
            dw_ref[...] = jnp.zeros_like(dw_ref)

        du0 = jnp.zeros_like(u0)
        for k in range(CONV_KERNEL):
            sh = CONV_KERNEL - 1 - k
            dw_ref[k:k + 1, :] += jnp.sum(duc * _shift_down(u0, sh, row), axis=0, keepdims=True)
            du0 = du0 + w_ref[k:k + 1, :] * _shift_up(duc, sh, row)
        dag_ref[0, :, :CW] = du0 * sg
        dag_ref[0, :, CW:] = du0 * a * sg * (1.0 - sg)

    vec = pl.BlockSpec((1, CW), lambda b: (0, 0))
    wspec = pl.BlockSpec((CONV_KERNEL, CW), lambda b: (0, 0))
    agv = ag.reshape(B, S, 2 * CW)
    res = pl.pallas_call(
        body, grid=(B,),
        in_specs=[pl.BlockSpec((1, S, CW), lambda b: (b, 0, 0)), pl.BlockSpec((1, S, CW), lambda b: (b, 0, 1)),
                  pl.BlockSpec((1, S, CW), lambda b: (b, 0, AW // CW)), wspec, vec, vec, vec, vec],
        out_specs=[pl.BlockSpec((1, S, 2 * CW), lambda b: (b, 0, 0)), wspec, vec, vec, vec, vec],
        out_shape=[jax.ShapeDtypeStruct((B, S, 2 * CW), F32), jax.ShapeDtypeStruct((CONV_KERNEL, CW), F32)]
        + [jax.ShapeDtypeStruct((1, CW), F32)] * 4,
        compiler_params=_params(1), name="conv_bwd",
    )(agv, agv, dmixed.reshape(B, S, D), conv_w, conv_b, ln_g, ln_b, norm_g)
    return (res[0].reshape(B * S, 2 * CW),) + tuple(res[1:])


def _ffn_conv(x, w_ref, bias, row):
    y = jnp.zeros_like(x) + bias
    for k in range(FFN_CONV_KERNEL):
        y = y + w_ref[k:k + 1, :] * _shift_down(x, FFN_CONV_KERNEL - 1 - k, row)
    return y


def _ffn_act(upre, cw, cb, B, S, DFF):
    tc = FFN_COLS
    nj = DFF // tc

    def body(u_ref, w_ref, b_ref, o_ref):
        row = lax.broadcasted_iota(jnp.int32, (S, 2 * tc), 0)
        up = _ffn_conv(u_ref[0], w_ref, b_ref[...], row)
        gate, val = up[:, :tc], up[:, tc:]
        o_ref[0] = (gate * _sigmoid(gate) * val).astype(BF16)

    out = pl.pallas_call(
        body, grid=(B, nj),
        in_specs=[pl.BlockSpec((1, S, 2 * tc), lambda b, j: (b, 0, j)),
                  pl.BlockSpec((FFN_CONV_KERNEL, 2 * tc), lambda b, j: (0, j)),
                  pl.BlockSpec((1, 2 * tc), lambda b, j: (0, j))],
        out_specs=pl.BlockSpec((1, S, tc), lambda b, j: (b, 0, j)),
        out_shape=jax.ShapeDtypeStruct((B, S, DFF), BF16),
        compiler_params=_params(2), name="ffn_act",
    )(upre.reshape(B, S, 2 * DFF), cw, cb)
    return out.reshape(B * S, DFF)


def _ffn_bwd(upre, dact, cw, cb, B, S, DFF):
    tc = FFN_COLS
    nj = DFF // tc

    def body(u_ref, da_ref, w_ref, b_ref, du_ref, dw_ref, db_ref):
        b = pl.program_id(1)
        row = lax.broadcasted_iota(jnp.int32, (S, 2 * tc), 0)
        upre_b = u_ref[0]
        up = _ffn_conv(upre_b, w_ref, b_ref[...], row)
        gate, val = up[:, :tc], up[:, tc:]
        sg = _sigmoid(gate)
        dact_b = da_ref[0]
        dgate = dact_b * val * sg * (1.0 + gate * (1.0 - sg))
        dval = dact_b * gate * sg
        dup = jnp.concatenate([dgate, dval], axis=1)
        first = b == 0
        _accumulate(db_ref, first, jnp.sum(dup, axis=0, keepdims=True))

        @pl.when(first)
        def _():
            dw_ref[...] = jnp.zeros_like(dw_ref)

        dupre = jnp.zeros_like(dup)
        for k in range(FFN_CONV_KERNEL):
            sh = FFN_CONV_KERNEL - 1 - k
            dw_ref[k:k + 1, :] += jnp.sum(dup * _shift_down(upre_b, sh, row), axis=0, keepdims=True)
            dupre = dupre + w_ref[k:k + 1, :] * _shift_up(dup, sh, row)
        du_ref[0] = dupre.astype(BF16)

    res = pl.pallas_call(
        body, grid=(nj, B),
        in_specs=[pl.BlockSpec((1, S, 2 * tc), lambda j, b: (b, 0, j)),
                  pl.BlockSpec((1, S, tc), lambda j, b: (b, 0, j)),
                  pl.BlockSpec((FFN_CONV_KERNEL, 2 * tc), lambda j, b: (0, j)),
                  pl.BlockSpec((1, 2 * tc), lambda j, b: (0, j))],
        out_specs=[pl.BlockSpec((1, S, 2 * tc), lambda j, b: (b, 0, j)),
                   pl.BlockSpec((FFN_CONV_KERNEL, 2 * tc), lambda j, b: (0, j)),
                   pl.BlockSpec((1, 2 * tc), lambda j, b: (0, j))],
        out_shape=[jax.ShapeDtypeStruct((B, S, 2 * DFF), BF16),
                   jax.ShapeDtypeStruct((FFN_CONV_KERNEL, 2 * DFF), F32),
                   jax.ShapeDtypeStruct((1, 2 * DFF), F32)],
        compiler_params=_params(2), name="ffn_bwd",
    )(upre.reshape(B, S, 2 * DFF), dact.reshape(B, S, DFF), cw, cb)
    return res[0].reshape(B * S, 2 * DFF), res[1], res[2]


def _interleave_cols(w, DFF):
    lead = w.shape[:-1]
    nj = DFF // FFN_COLS
    w = w.reshape(lead + (2, nj, FFN_COLS))
    return jnp.swapaxes(w, -3, -2).reshape(lead + (2 * DFF,))


def _deinterleave_cols(w, DFF):
    lead = w.shape[:-1]
    nj = DFF // FFN_COLS
    w = w.reshape(lead + (nj, 2, FFN_COLS))
    return jnp.swapaxes(w, -3, -2).reshape(lead + (2 * DFF,))


def _dh_sum(parts, dag, tm):
    T, AW = parts[0][0].shape
    CW2 = dag.shape[1]
    W = 3 * AW + CW2

    def body(*refs):
        ins = refs[:9]
        dag_ref, dh_ref, cs_ref = refs[9], refs[10], refs[11]
        i = pl.program_id(0)
        sums = []
        for c in range(3):
            tot = ins[c][...] + ins[3 + c][...] + ins[6 + c][...]
            dh_ref[:, c * AW:(c + 1) * AW] = tot.astype(BF16)
            sums.append(jnp.sum(tot, axis=0, keepdims=True))
        dg = dag_ref[...]
        dh_ref[:, 3 * AW:] = dg.astype(BF16)
        sums.append(jnp.sum(dg, axis=0, keepdims=True))
        _accumulate(cs_ref, i == 0, jnp.concatenate(sums, axis=1))

    row = pl.BlockSpec((tm, AW), lambda i: (i, 0))
    flat = [p[c] for p in parts for c in range(3)]
    return pl.pallas_call(
        body, grid=(T // tm,),
        in_specs=[row] * 9 + [pl.BlockSpec((tm, CW2), lambda i: (i, 0))],
        out_specs=[pl.BlockSpec((tm, W), lambda i: (i, 0)), pl.BlockSpec((1, W), lambda i: (0, 0))],
        out_shape=[jax.ShapeDtypeStruct((T, W), BF16), jax.ShapeDtypeStruct((1, W), F32)],
        compiler_params=_params(1), name="dh_sum",
    )(*flat, dag)


def _local_step(x, target, rel_table, w_in, b_in, conv_w, conv_b, conv_ln_g, conv_ln_b, attn_norm_g,
                conv_norm_g, w_out, ln1_g, ln1_b, w_up_il, ffn_cw_il, ffn_cb_il, w_down, ln2_g, ln2_b):
    B, S, D = x.shape
    T = B * S
    AW = attn_norm_g.shape[-1]
    CW = conv_norm_g.shape[-1]
    H = AW // HEAD_DIM
    DFF = w_down.shape[0]
    INW = 3 * AW + 2 * CW
    xf = x.reshape(T, D)
    tf = target.reshape(T, D)
    tm = _row_tile(T, 512)
    tm_s = _row_tile(T, 256)

    bucket_np, mask_np = _bucket_tables()
    bucket = jnp.asarray(bucket_np)
    band_mask = jnp.asarray(mask_np)
    bias_all = _bias_build(rel_table.T, bucket, band_mask).reshape(3, H, ATTN_BLOCK, 2 * ATTN_BLOCK)

    tn_qkv = _col_tile(3 * AW, 1152)
    qkv = _mm_plain(xf, w_in[:, :3 * AW], mode="nn", tm=tm, tn=tn_qkv, tk=D, out_dtype=BF16,
                    bias=b_in[:, :3 * AW], name="mm_qkv")
    ag = _mm_plain(xf, w_in[:, 3 * AW:], mode="nn", tm=tm, tn=2 * CW, tk=D, out_dtype=F32,
                   bias=b_in[:, 3 * AW:], name="mm_ag")

    ons, lses = [], []
    for bi, (_, d) in enumerate(DILATED_CONFIGS):
        o, l = _attn_fwd(qkv, bias_all[bi], B, S, AW, d, name="attn_fwd_d%d" % d)
        ons.append(o)
        lses.append(l)
    attn, lse, mixed_a, r_attn = _attn_combine(ons, lses, attn_norm_g, tm_s)
    mixed_c = _conv_fwd(ag, conv_w, conv_b, conv_ln_g, conv_ln_b, conv_norm_g, B, S, CW)
    mixed = jnp.concatenate([mixed_a, mixed_c], axis=1)

    def ln1_epilogue(acc, i, j, extra_refs, out_refs):
        x_ref, g_ref, b_ref = extra_refs
        x1, xh, r = _ln_fwd(acc + ALPHA * x_ref[...], g_ref[...], b_ref[...])
        out_refs[0][...] = x1
        out_refs[1][...] = x1.astype(BF16)
        out_refs[2][...] = xh
        out_refs[3][...] = jnp.broadcast_to(r, (tm_s, LANES))

    rowD = lambda i, j, k: (i, 0)
    vecD = lambda i, j, k: (0, 0)
    x1, x1b, xh1, r1 = _matmul(
        mixed, w_out, mode="nn", tm=tm_s, tn=D, tk=D,
        extras=[(xf, (tm_s, D), rowD), (ln1_g, (1, D), vecD), (ln1_b, (1, D), vecD)],
        outs=[((T, D), F32, (tm_s, D), rowD), ((T, D), BF16, (tm_s, D), rowD), ((T, D), F32, (tm_s, D), rowD),
              ((T, LANES), F32, (tm_s, LANES), rowD)],
        epilogue=ln1_epilogue, name="mm_out_ln1")

    tn_up = _col_tile(2 * DFF, 1408)
    upre = _mm_plain(x1b, w_up_il, mode="nn", tm=tm, tn=tn_up, tk=D, out_dtype=F32, name="mm_up")
    act = _ffn_act(upre, ffn_cw_il, ffn_cb_il, B, S, DFF)

    def ln2_epilogue(acc, i, j, extra_refs, out_refs):
        x1_ref, g_ref, b_ref, t_ref = extra_refs
        dz_ref, dzb_ref, loss_ref, dg_ref, db_ref = out_refs
        g = g_ref[...]
        y, xh, r = _ln_fwd(acc + ALPHA * x1_ref[...], g, b_ref[...])
        diff = y - t_ref[...]
        row_loss = jnp.sum(diff * diff, axis=1, keepdims=True)
        tile_loss = jnp.sum(row_loss, axis=0, keepdims=True) * (0.5 / D)
        dy = diff * (1.0 / D)
        dz = _ln_bwd(dy, xh, r, g)
        dz_ref[...] = dz
        dzb_ref[...] = dz.astype(BF16)
        first = i == 0
        _accumulate(loss_ref, first, jnp.broadcast_to(tile_loss, (1, LANES)))
        _accumulate(dg_ref, first, jnp.sum(dy * xh, axis=0, keepdims=True))
        _accumulate(db_ref, first, jnp.sum(dy, axis=0, keepdims=True))

    dz2, dz2b, loss_part, d_ln2_g, d_ln2_b = _matmul(
        act, w_down, mode="nn", tm=tm_s, tn=D, tk=DFF,
        extras=[(x1, (tm_s, D), rowD), (ln2_g, (1, D), vecD), (ln2_b, (1, D), vecD), (tf, (tm_s, D), rowD)],
        outs=[((T, D), F32, (tm_s, D), rowD), ((T, D), BF16, (tm_s, D), rowD),
              ((1, LANES), F32, (1, LANES), vecD), ((1, D), F32, (1, D), vecD), ((1, D), F32, (1, D), vecD)],
        epilogue=ln2_epilogue, name="mm_down_ln2_loss")

    tn_dff = _col_tile(DFF, 1408)
    dact = _mm_plain(dz2b, w_down, mode="nt", tm=tm, tn=tn_dff, tk=D, out_dtype=F32, name="mm_dact")
    dupre, d_ffn_cw_il, d_ffn_cb_il = _ffn_bwd(upre, dact, ffn_cw_il, ffn_cb_il, B, S, DFF)
    tk_t = _row_tile(T, 512)
    d_w_down = _mm_plain(act, dz2b, mode="tn", tm=tn_dff, tn=D, tk=tk_t, out_dtype=F32, name="mm_dw_down")
    d_w_up_il = _mm_plain(x1b, dupre, mode="tn", tm=D, tn=tn_up, tk=tk_t, out_dtype=F32, name="mm_dw_up")

    def ln1_bwd_epilogue(acc, i, j, extra_refs, out_refs):
        dz2_ref, xh_ref, r_ref, g_ref = extra_refs
        dz_ref, dzb_ref, dg_ref, db_ref = out_refs
        dx1 = acc + ALPHA * dz2_ref[...]
        xh = xh_ref[...]
        dz = _ln_bwd(dx1, xh, r_ref[:, 0:1], g_ref[...])
        dz_ref[...] = dz
        dzb_ref[...] = dz.astype(BF16)
        first = i == 0
        _accumulate(dg_ref, first, jnp.sum(dx1 * xh, axis=0, keepdims=True))
        _accumulate(db_ref, first, jnp.sum(dx1, axis=0, keepdims=True))

    dz1, dz1b, d_ln1_g, d_ln1_b = _matmul(
        dupre, w_up_il, mode="nt", tm=tm_s, tn=D, tk=_col_tile(2 * DFF, 2816),
        extras=[(dz2, (tm_s, D), rowD), (xh1, (tm_s, D), rowD), (r1, (tm_s, LANES), rowD), (ln1_g, (1, D), vecD)],
        outs=[((T, D), F32, (tm_s, D), rowD), ((T, D), BF16, (tm_s, D), rowD),
              ((1, D), F32, (1, D), vecD), ((1, D), F32, (1, D), vecD)],
        epilogue=ln1_bwd_epilogue, name="mm_dx1_ln1_bwd")

    dmixed = _mm_plain(dz1b, w_out, mode="nt", tm=tm, tn=D, tk=D, out_dtype=F32, name="mm_dmixed")
    d_w_out = _mm_plain(mixed, dz1b, mode="tn", tm=D, tn=D, tk=tk_t, out_dtype=F32, name="mm_dw_out")

    dattn, dd, d_attn_norm_g = _attn_pre_bwd(dmixed, attn, r_attn, attn_norm_g, tm_s)
    dag, d_conv_w, d_conv_b, d_conv_ln_g, d_conv_ln_b, d_conv_norm_g = _conv_bwd(
        ag, dmixed, conv_w, conv_b, conv_ln_g, conv_ln_b, conv_norm_g, B, S, CW, D)

    parts, dbias = [], []
    for bi, (_, d) in enumerate(DILATED_CONFIGS):
        dq, dk, dv, db = _attn_bwd(qkv, dattn, lse, dd, bias_all[bi], B, S, AW, d, name="attn_bwd_d%d" % d)
        parts.append((dq, dk, dv))
        dbias.append(db.reshape(H, ATTN_BLOCK * 2 * ATTN_BLOCK))
    d_rel_table = _rel_grad(jnp.stack(dbias), bucket).T
    dh, d_b_in = _dh_sum(parts, dag, tm_s)

    def gx_epilogue(acc, i, j, extra_refs, out_refs):
        out_refs[0][...] = acc + ALPHA * extra_refs[0][...]

    grad_x = _matmul(dh, w_in, mode="nt", tm=tm_s, tn=D, tk=INW,
                     extras=[(dz1, (tm_s, D), rowD)], outs=[((T, D), F32, (tm_s, D), rowD)],
                     epilogue=gx_epilogue, name="mm_grad_x")[0]
    d_w_in = _mm_plain(xf, dh, mode="tn", tm=D, tn=_col_tile(INW, 1408), tk=tk_t, out_dtype=F32, name="mm_dw_in")

    grads = dict(rel_table=d_rel_table, w_in=d_w_in, b_in=d_b_in, conv_w=d_conv_w, conv_b=d_conv_b,
                 conv_ln_g=d_conv_ln_g, conv_ln_b=d_conv_ln_b, attn_norm_g=d_attn_norm_g,
                 conv_norm_g=d_conv_norm_g, w_out=d_w_out, ln1_g=d_ln1_g, ln1_b=d_ln1_b, w_up_il=d_w_up_il,
                 ffn_conv_w_il=d_ffn_cw_il, ffn_conv_b_il=d_ffn_cb_il, w_down=d_w_down,
                 ln2_g=d_ln2_g, ln2_b=d_ln2_b)
    return loss_part, grad_x.reshape(B, S, D), grads


def _place():
    return lax.axis_index("x"), lax.axis_index("y"), lax.axis_index("c")


CHIP_FLIPS = ((1, 0), (0, 1), (1, 1))


def _flip(v, f):
    return 1 - v if f else v


HBM_SPEC = pl.BlockSpec(memory_space=pl.ANY)
VMEM_SPEC = pl.BlockSpec(memory_space=pltpu.VMEM)
COMM_PARAMS = pltpu.CompilerParams(vmem_limit_bytes=VMEM_LIMIT)


def _gather_weights(big, small):
    nb, ns = len(big), len(small)

    def body(*refs):
        big_in = refs[:nb]
        small_in = refs[nb:nb + ns]
        big_out = refs[nb + ns:2 * nb + ns]
        small_out = refs[2 * nb + ns:2 * nb + 2 * ns]
        stages = refs[2 * nb + 2 * ns:3 * nb + 2 * ns]
        send_sems, recv_sems, local_sems = refs[3 * nb + 2 * ns:]
        x, y, c = _place()
        s_me = 2 * x + y
        sibling = (x, y, 1 - c)
        started, local_copies = [], []
        for a in range(nb):
            rh = big[a].shape[0] // 2
            lo = pl.multiple_of(c * rh, 16)
            stages[a][...] = big_in[a][pl.ds(lo, rh), :].astype(BF16)
            mine = big_out[a].at[s_me, pl.ds(lo, rh), :]
            loc = pltpu.make_async_copy(stages[a], mine, local_sems.at[a])
            loc.start()
            local_copies.append(loc)
            targets = [sibling] + [(_flip(x, fx), _flip(y, fy), c) for fx, fy in CHIP_FLIPS]
            for k, to in enumerate(targets):
                cp = pltpu.make_async_remote_copy(stages[a], mine, send_sems.at[a * 7 + k],
                                                  recv_sems.at[a * 7 + k], device_id=to, device_id_type=MESH)
                cp.start()
                started.append(cp)
        for a in range(ns):
            mine = small_out[a].at[s_me]
            loc = pltpu.make_async_copy(small_in[a], mine, local_sems.at[nb + a])
            loc.start()
            local_copies.append(loc)
            for k, (fx, fy) in enumerate(CHIP_FLIPS):
                cp = pltpu.make_async_remote_copy(small_in[a], mine, send_sems.at[nb * 7 + a * 3 + k],
                                                  recv_sems.at[nb * 7 + a * 3 + k],
                                                  device_id=(_flip(x, fx), _flip(y, fy), c), device_id_type=MESH)
                cp.start()
                started.append(cp)
        for a in range(nb):
            rh = big[a].shape[0] // 2
            lo = pl.multiple_of(c * rh, 16)
            for k, (fx, fy) in enumerate(CHIP_FLIPS):
                s_from = 2 * _flip(x, fx) + _flip(y, fy)
                got = big_out[a].at[s_from, pl.ds(lo, rh), :]
                pltpu.make_async_remote_copy(got, got, send_sems.at[a * 7 + 1 + k], recv_sems.at[a * 7 + 1 + k],
                                             device_id=sibling, device_id_type=MESH).wait_recv()
                fwd = pltpu.make_async_remote_copy(got, got, send_sems.at[a * 7 + 4 + k],
                                                   recv_sems.at[a * 7 + 4 + k], device_id=sibling,
                                                   device_id_type=MESH)
                fwd.start()
                started.append(fwd)
        for a in range(nb):
            rh = big[a].shape[0] // 2
            lo_sib = pl.multiple_of((1 - c) * rh, 16)
            for k in (0, 4, 5, 6):
                any_rows = big_out[a].at[s_me, pl.ds(lo_sib, rh), :]
                pltpu.make_async_remote_copy(any_rows, any_rows, send_sems.at[a * 7 + k], recv_sems.at[a * 7 + k],
                                             device_id=sibling, device_id_type=MESH).wait_recv()
        for a in range(ns):
            for k in range(3):
                pltpu.make_async_remote_copy(small_in[a], small_out[a].at[s_me], send_sems.at[nb * 7 + a * 3 + k],
                                             recv_sems.at[nb * 7 + a * 3 + k], device_id=sibling,
                                             device_id_type=MESH).wait_recv()
        for cp in started:
            cp.wait_send()
        for cp in local_copies:
            cp.wait()

    n_sem = nb * 7 + ns * 3
    out_shape = ([jax.ShapeDtypeStruct((N_SHARDS,) + w.shape, BF16) for w in big]
                 + [jax.ShapeDtypeStruct((N_SHARDS,) + w.shape, F32) for w in small])
    res = pl.pallas_call(
        body, in_specs=[VMEM_SPEC] * nb + [HBM_SPEC] * ns, out_specs=[HBM_SPEC] * (nb + ns),
        out_shape=out_shape,
        scratch_shapes=[pltpu.VMEM((w.shape[0] // 2, w.shape[1]), BF16) for w in big]
        + [pltpu.SemaphoreType.DMA((n_sem,)), pltpu.SemaphoreType.DMA((n_sem,)),
           pltpu.SemaphoreType.DMA((nb + ns,))],
        compiler_params=COMM_PARAMS, name="gather_weights",
    )(*big, *small)
    return res[:nb], res[nb:]


def _sibling_exchange(grads):
    n = len(grads)

    def body(*refs):
        g_in = refs[:n]
        got = refs[n:2 * n]
        send_sems, recv_sems = refs[2 * n:]
        x, y, c = _place()
        cps = []
        for a in range(n):
            rh = grads[a].shape[1] // 2
            lo = pl.multiple_of((1 - c) * rh, 8)
            cp = pltpu.make_async_remote_copy(g_in[a].at[:, pl.ds(lo, rh), :], got[a], send_sems.at[a],
                                              recv_sems.at[a], device_id=(x, y, 1 - c), device_id_type=MESH)
            cp.start()
            cps.append(cp)
        for cp in cps:
            cp.wait()

    return pl.pallas_call(
        body, in_specs=[HBM_SPEC] * n, out_specs=[HBM_SPEC] * n,
        out_shape=[jax.ShapeDtypeStruct((N_SHARDS, g.shape[1] // 2, g.shape[2]), F32) for g in grads],
        scratch_shapes=[pltpu.SemaphoreType.DMA((n,)), pltpu.SemaphoreType.DMA((n,))],
        compiler_params=COMM_PARAMS, name="sibling_exchange",
    )(*grads)


def _chip_exchange(chip_parts, pack):
    n = len(chip_parts)

    def body(*refs):
        parts = refs[:n]
        pack_ref = refs[n]
        got = refs[n + 1:2 * n + 1]
        all_packs = refs[2 * n + 1]
        send_sems, recv_sems, local_sem = refs[2 * n + 2:]
        x, y, c = _place()
        me = 4 * x + 2 * y + c
        cps = []
        for a in range(n):
            for k, (fx, fy) in enumerate(CHIP_FLIPS):
                px, py = _flip(x, fx), _flip(y, fy)
                cp = pltpu.make_async_remote_copy(parts[a].at[2 * px + py], got[a].at[k], send_sems.at[a * 3 + k],
                                                  recv_sems.at[a * 3 + k], device_id=(px, py, c),
                                                  device_id_type=MESH)
                cp.start()
                cps.append(cp)
        loc = pltpu.make_async_copy(pack_ref, all_packs.at[me], local_sem)
        loc.start()
        for m in range(1, N_DEV):
            to = (_flip(x, m & 4), _flip(y, m & 2), _flip(c, m & 1))
            cp = pltpu.make_async_remote_copy(pack_ref, all_packs.at[me], send_sems.at[n * 3 + m - 1],
                                              recv_sems.at[n * 3 + m - 1], device_id=to, device_id_type=MESH)
            cp.start()
            cps.append(cp)
        for cp in cps:
            cp.wait()
        loc.wait()

    rs = pack.shape[0]
    res = pl.pallas_call(
        body, in_specs=[HBM_SPEC] * (n + 1), out_specs=[HBM_SPEC] * (n + 1),
        out_shape=[jax.ShapeDtypeStruct((3,) + p.shape[1:], BF16) for p in chip_parts]
        + [jax.ShapeDtypeStruct((N_DEV, rs, LANES), F32)],
        scratch_shapes=[pltpu.SemaphoreType.DMA((n * 3 + N_DEV - 1,)), pltpu.SemaphoreType.DMA((n * 3 + N_DEV - 1,)),
                        pltpu.SemaphoreType.DMA],
        compiler_params=COMM_PARAMS, name="chip_exchange",
    )(*chip_parts, pack)
    return res[:n], res[n]


def _sibling_assemble(halves):
    n = len(halves)

    def body(*refs):
        h_in = refs[:n]
        full = refs[n:2 * n]
        send_sems, recv_sems, local_sems = refs[2 * n:]
        x, y, c = _place()
        cps = []
        for a in range(n):
            rh = halves[a].shape[0]
            mine = full[a].at[pl.ds(pl.multiple_of(c * rh, 8), rh), :]
            loc = pltpu.make_async_copy(h_in[a], mine, local_sems.at[a])
            loc.start()
            cp = pltpu.make_async_remote_copy(h_in[a], mine, send_sems.at[a], recv_sems.at[a],
                                              device_id=(x, y, 1 - c), device_id_type=MESH)
            cp.start()
            cps += [loc, cp]
        for cp in cps:
            cp.wait()

    return pl.pallas_call(
        body, in_specs=[HBM_SPEC] * n, out_specs=[HBM_SPEC] * n,
        out_shape=[jax.ShapeDtypeStruct((2 * h.shape[0], h.shape[1]), F32) for h in halves],
        scratch_shapes=[pltpu.SemaphoreType.DMA((n,)), pltpu.SemaphoreType.DMA((n,)),
                        pltpu.SemaphoreType.DMA((n,))],
        compiler_params=COMM_PARAMS, name="sibling_assemble",
    )(*halves)


def _half_tile(rh, mult=16, want=256):
    best = None
    for t in range(mult, min(rh, want) + 1, mult):
        if rh % t == 0:
            best = t
    return best if best is not None else rh


def _pair_sum(g, sib, ids, name):
    _, R, C = g.shape
    rh = R // 2
    rt = _half_tile(rh)
    nt = rh // rt

    def body(ids_ref, g_ref, s_ref, o_ref):
        o_ref[...] = (g_ref[...] + s_ref[...]).astype(BF16)

    grid_spec = pltpu.PrefetchScalarGridSpec(
        num_scalar_prefetch=1, grid=(N_SHARDS, nt),
        in_specs=[pl.BlockSpec((1, rt, C), lambda s, i, ids: (s, ids[2] * nt + i, 0)),
                  pl.BlockSpec((1, rt, C), lambda s, i, ids: (s, i, 0))],
        out_specs=pl.BlockSpec((1, rt, C), lambda s, i, ids: (s, i, 0)))
    return pl.pallas_call(body, grid_spec=grid_spec, out_shape=jax.ShapeDtypeStruct((N_SHARDS, rh, C), BF16),
                          compiler_params=_params(2), name=name)(ids, g, sib)


def _final_sum(g, sib, got, ids, name):
    _, R, C = g.shape
    rh = R // 2
    rt = _half_tile(rh)
    nt = rh // rt

    def body(ids_ref, g_ref, s_ref, r_ref, o_ref):
        tot = g_ref[0] + s_ref[0]
        for k in range(3):
            tot = tot + r_ref[k].astype(F32)
        o_ref[...] = tot

    grid_spec = pltpu.PrefetchScalarGridSpec(
        num_scalar_prefetch=1, grid=(nt,),
        in_specs=[pl.BlockSpec((1, rt, C), lambda i, ids: (2 * ids[0] + ids[1], ids[2] * nt + i, 0)),
                  pl.BlockSpec((1, rt, C), lambda i, ids: (2 * ids[0] + ids[1], i, 0)),
                  pl.BlockSpec((3, rt, C), lambda i, ids: (0, i, 0))],
        out_specs=pl.BlockSpec((rt, C), lambda i, ids: (i, 0)))
    return pl.pallas_call(body, grid_spec=grid_spec, out_shape=jax.ShapeDtypeStruct((rh, C), F32),
                          compiler_params=_params(1), name=name)(ids, g, sib, got)


def _sum_packs(all_packs):
    def body(p_ref, o_ref):
        tot = p_ref[0]
        for i in range(1, N_DEV):
            tot = tot + p_ref[i]
        o_ref[...] = tot

    return pl.pallas_call(body, in_specs=[VMEM_SPEC], out_specs=VMEM_SPEC,
                          out_shape=jax.ShapeDtypeStruct(all_packs.shape[1:], F32), name="sum_packs")(all_packs)


def _adamw(w, g, m, v, name):
    R, C = w.shape
    rt = _half_tile(R, mult=8, want=256)

    def body(w_ref, g_ref, m_ref, v_ref, d_ref, nm_ref, nv_ref):
        gg = g_ref[...]
        nm = ADAM_B1 * m_ref[...] + (1.0 - ADAM_B1) * gg
        nv = ADAM_B2 * v_ref[...] + (1.0 - ADAM_B2) * (gg * gg)
        m_hat = nm / (1.0 - ADAM_B1 ** ADAM_STEP)
        v_hat = nv / (1.0 - ADAM_B2 ** ADAM_STEP)
        d_ref[...] = -ADAM_LR * (m_hat / (jnp.sqrt(v_hat) + ADAM_EPS) + ADAM_WD * w_ref[...])
        nm_ref[...] = nm
        nv_ref[...] = nv

    spec = pl.BlockSpec((rt, C), lambda i: (i, 0))
    return pl.pallas_call(body, grid=(R // rt,), in_specs=[spec] * 4, out_specs=[spec] * 3,
                          out_shape=[jax.ShapeDtypeStruct((R, C), F32)] * 3,
                          compiler_params=_params(1), name=name)(w, g, m, v)


def _pack(pieces):
    rows = []
    for p in pieces:
        flat = p.reshape(-1)
        pad = (-flat.shape[0]) % LANES
        if pad:
            flat = jnp.concatenate([flat, jnp.zeros((pad,), F32)])
        rows.append(flat.reshape(-1, LANES))
    total = sum(r.shape[0] for r in rows)
    pad_rows = (-total) % 8
    if pad_rows:
        rows.append(jnp.zeros((pad_rows, LANES), F32))
    return jnp.concatenate(rows, axis=0)


def _unpack(buf, shapes):
    out, r0 = [], 0
    for shp in shapes:
        n = int(np.prod(shp))
        nr = -(-n // LANES)
        out.append(buf[r0:r0 + nr].reshape(-1)[:n].reshape(shp))
        r0 += nr
    return out


SMALL_NAMES = ("rel_table", "b_in", "conv_w", "conv_b", "conv_ln_g", "conv_ln_b", "attn_norm_g", "conv_norm_g",
               "ln1_g", "ln1_b", "ffn_conv_w", "ffn_conv_b", "ln2_g", "ln2_b")
BIG_NAMES = ("w_in", "w_out", "w_up", "w_down")
WEIGHT_ORDER = ("rel_table", "w_in", "b_in", "conv_w", "conv_b", "conv_ln_g", "conv_ln_b", "attn_norm_g",
                "conv_norm_g", "w_out", "ln1_g", "ln1_b", "w_up", "ffn_conv_w", "ffn_conv_b", "w_down",
                "ln2_g", "ln2_b")


def kernel(x, rel_table, w_in, b_in, conv_w, conv_b, conv_ln_g, conv_ln_b, attn_norm_g, conv_norm_g, w_out, ln1_g, ln1_b, w_up, ffn_conv_w, ffn_conv_b, w_down, ln2_g, ln2_b, loss_target, m_rel_table, m_w_in, m_b_in, m_conv_w, m_conv_b, m_conv_ln_g, m_conv_ln_b, m_attn_norm_g, m_conv_norm_g, m_w_out, m_ln1_g, m_ln1_b, m_w_up, m_ffn_conv_w, m_ffn_conv_b, m_w_down, m_ln2_g, m_ln2_b, v_rel_table, v_w_in, v_b_in, v_conv_w, v_conv_b, v_conv_ln_g, v_conv_ln_b, v_attn_norm_g, v_conv_norm_g, v_w_out, v_ln1_g, v_ln1_b, v_w_up, v_ffn_conv_w, v_ffn_conv_b, v_w_down, v_ln2_g, v_ln2_b):
    args = dict(locals())
    weights = {n: args[n] for n in WEIGHT_ORDER}
    moms = {n: args["m_" + n] for n in WEIGHT_ORDER}
    vels = {n: args["v_" + n] for n in WEIGHT_ORDER}
    xi, yi, ci = _place()
    ids = jnp.stack([xi, yi, ci]).astype(jnp.int32)
    shard = 2 * xi + yi
    D = x.shape[-1]
    DFF = w_down.shape[1] * N_SHARDS
    CW = conv_norm_g.shape[-1]

    (g_in, g_out, g_up, g_down), (g_cw, g_fcw) = _gather_weights(
        [w_in[0], w_out[0], w_up[0], w_down[0]], [conv_w[0], ffn_conv_w[0]])
    cols = lambda t: jnp.transpose(t, (1, 0, 2)).reshape(t.shape[1], N_SHARDS * t.shape[2])
    w_in_f = cols(g_in)
    w_up_il = _interleave_cols(cols(g_up), DFF)
    w_out_f = g_out.reshape(D, D)
    w_down_f = g_down.reshape(DFF, D)
    conv_w_f = cols(g_cw)
    ffn_cw_il = _interleave_cols(cols(g_fcw), DFF)
    ffn_cb_il = _interleave_cols(ffn_conv_b, DFF)

    loss_part, grad_x, gl = _local_step(
        x, loss_target, rel_table, w_in_f, b_in, conv_w_f, conv_b, conv_ln_g, conv_ln_b, attn_norm_g, conv_norm_g,
        w_out_f, ln1_g, ln1_b, w_up_il, ffn_cw_il, ffn_cb_il, w_down_f, ln2_g, ln2_b)

    rows = lambda t: jnp.transpose(t.reshape(t.shape[0], N_SHARDS, t.shape[1] // N_SHARDS), (1, 0, 2))
    big_parts = [rows(gl["w_in"]), gl["w_out"].reshape(N_SHARDS, D // N_SHARDS, D),
                 rows(_deinterleave_cols(gl["w_up_il"], DFF)), gl["w_down"].reshape(N_SHARDS, DFF // N_SHARDS, D)]
    sib = _sibling_exchange(big_parts)
    chip_parts = [_pair_sum(g, s, ids, name="pair_sum_" + n) for g, s, n in zip(big_parts, sib, BIG_NAMES)]

    small_g = dict(gl)
    small_g["ffn_conv_w"] = _deinterleave_cols(gl["ffn_conv_w_il"], DFF)
    small_g["ffn_conv_b"] = _deinterleave_cols(gl["ffn_conv_b_il"], DFF)
    pack = _pack([loss_part] + [small_g[n] for n in SMALL_NAMES])
    got, all_packs = _chip_exchange(chip_parts, pack)
    halves = [_final_sum(g, s, r, ids, name="final_sum_" + n)
              for g, s, r, n in zip(big_parts, sib, got, BIG_NAMES)]
    big_grads = dict(zip(BIG_NAMES, _sibling_assemble(halves)))

    summed = _sum_packs(all_packs)
    full_shapes = {n: weights[n].shape for n in SMALL_NAMES}
    full_shapes["conv_w"] = (1, CONV_KERNEL, CW)
    full_shapes["ffn_conv_w"] = (1, FFN_CONV_KERNEL, 2 * DFF)
    un = _unpack(summed, [(1, LANES)] + [full_shapes[n] for n in SMALL_NAMES])
    loss = un[0][0, 0]
    small_grads = dict(zip(SMALL_NAMES, un[1:]))
    for n in ("conv_w", "ffn_conv_w"):
        width = weights[n].shape[-1]
        small_grads[n] = lax.dynamic_slice_in_dim(small_grads[n], shard * width, width, axis=2)

    grads, delta, new_m, new_v = {}, {}, {}, {}
    for n in BIG_NAMES:
        shp = weights[n].shape
        g2 = big_grads[n]
        d, nm, nv = _adamw(weights[n][0], g2, moms[n][0], vels[n][0], name="adamw_" + n)
        grads[n], delta[n], new_m[n], new_v[n] = (t.reshape(shp) for t in (g2, d, nm, nv))
    sp = lambda src: _pack([src[n] for n in SMALL_NAMES])
    d_s, nm_s, nv_s = _adamw(sp(weights), sp(small_grads), sp(moms), sp(vels), name="adamw_small")
    shapes = [weights[n].shape for n in SMALL_NAMES]
    for tgt, buf in ((delta, d_s), (new_m, nm_s), (new_v, nv_s)):
        tgt.update(zip(SMALL_NAMES, _unpack(buf, shapes)))
    grads.update(small_grads)

    return (loss, grad_x, *[grads[n] for n in WEIGHT_ORDER], *[delta[n] for n in WEIGHT_ORDER],
            *[new_m[n] for n in WEIGHT_ORDER], *[new_v[n] for n in WEIGHT_ORDER])
```

```python
import functools
import math

import numpy as np
import jax
import jax.numpy as jnp
from jax import lax
from jax.experimental import pallas as pl
from jax.experimental.pallas import tpu as pltpu

F32 = jnp.float32
BF16 = jnp.bfloat16
MESH = pl.DeviceIdType.MESH

HEAD_DIM = 64
LANES = 128
ATTN_BLOCK = 128
DILATED_CONFIGS = ((128, 1), (512, 4), (2048, 16))
CONV_KERNEL = 31
FFN_CONV_KERNEL = 3
REL_BUCKETS = 32
REL_MAX_DIST = 2048
DEPTH = 1
ALPHA = (2 * DEPTH) ** 0.25
LN_EPS = 1e-5
NEG_INF = -1e30
QK_SCALE = 1.0 / math.sqrt(HEAD_DIM)
ADAM_LR = 0.001
ADAM_B1 = 0.9
ADAM_B2 = 0.999
ADAM_EPS = 1e-08
ADAM_WD = 0.01
ADAM_STEP = 10
VMEM_LIMIT = 52 * 1024 * 1024
FFN_COLS = 256
N_SHARDS = 4
N_DEV = 8


def _params(n_axes):
    return pltpu.CompilerParams(dimension_semantics=("arbitrary",) * n_axes,
                                vmem_limit_bytes=VMEM_LIMIT)


MM_DIMS = {"nn": (((1,), (0,)), ((), ())), "nt": (((1,), (1,)), ((), ())), "tn": (((0,), (0,)), ((), ()))}


def _matmul_general(ins, part_fn, *, grid, tm, tn, outs, epilogue, extras=(), name):
    nk = grid[2]
    n_in, n_extra = len(ins), len(extras)

    def body(*refs):
        in_refs = refs[:n_in]
        rest = refs[n_in:]
        extra_refs = rest[:n_extra]
        out_refs = rest[n_extra:n_extra + len(outs)]
        acc_ref = rest[-1]
        i, j, k = pl.program_id(0), pl.program_id(1), pl.program_id(2)
        part = part_fn(in_refs, i, j, k)
        if nk == 1:
            epilogue(part, i, j, extra_refs, out_refs)
        else:
            @pl.when(k == 0)
            def _():
                acc_ref[...] = part

            @pl.when(k > 0)
            def _():
                acc_ref[...] += part

            @pl.when(k == nk - 1)
            def _():
                epilogue(acc_ref[...], i, j, extra_refs, out_refs)

    in_specs = [pl.BlockSpec(bs, im) for (_, bs, im) in list(ins) + list(extras)]
    out_specs = [pl.BlockSpec(bs, im) for (_, _, bs, im) in outs]
    out_shape = [jax.ShapeDtypeStruct(s, d) for (s, d, _, _) in outs]
    return pl.pallas_call(
        body, grid=grid, in_specs=in_specs, out_specs=out_specs,
        out_shape=out_shape, scratch_shapes=[pltpu.VMEM((tm, tn), F32)],
        compiler_params=_params(3), name=name,
    )(*[e[0] for e in ins], *[e[0] for e in extras])


def _dot(a, b, mode):
    return lax.dot_general(a.astype(BF16), b.astype(BF16), MM_DIMS[mode], preferred_element_type=F32)


def _matmul(a, b, *, mode, tm, tn, tk, outs, epilogue, extras=(), name):
    if mode == "tn":
        K, M = a.shape
        N = b.shape[1]
        ins = [(a, (tk, tm), lambda i, j, k: (k, i)), (b, (tk, tn), lambda i, j, k: (k, j))]
    elif mode == "nt":
        M, K = a.shape
        N = b.shape[0]
        ins = [(a, (tm, tk), lambda i, j, k: (i, k)), (b, (tn, tk), lambda i, j, k: (j, k))]
    else:
        M, K = a.shape
        N = b.shape[1]
        ins = [(a, (tm, tk), lambda i, j, k: (i, k)), (b, (tk, tn), lambda i, j, k: (k, j))]
    assert M % tm == 0 and N % tn == 0 and K % tk == 0, (name, M, N, K, tm, tn, tk)

    def part_fn(in_refs, i, j, k):
        return _dot(in_refs[0][...], in_refs[1][...], mode)

    return _matmul_general(ins, part_fn, grid=(M // tm, N // tn, K // tk), tm=tm, tn=tn, outs=outs,
                           epilogue=epilogue, extras=extras, name=name)


def _plain_out(M, N, tm, tn, dtype):
    return ((M, N), dtype, (tm, tn), lambda i, j, k: (i, j))


def _mm_plain(a, b, *, mode, tm, tn, tk, out_dtype, name, bias=None):
    if mode == "tn":
        M, N = a.shape[1], b.shape[1]
    elif mode == "nt":
        M, N = a.shape[0], b.shape[0]
    else:
        M, N = a.shape[0], b.shape[1]
    extras = []
    if bias is not None:
        extras.append((bias, (1, tn), lambda i, j, k: (0, j)))

    def epilogue(acc, i, j, extra_refs, out_refs):
        if bias is not None:
            acc = acc + extra_refs[0][...]
        out_refs[0][...] = acc.astype(out_dtype)

    return _matmul(a, b, mode=mode, tm=tm, tn=tn, tk=tk, outs=[_plain_out(M, N, tm, tn, out_dtype)],
                   epilogue=epilogue, extras=extras, name=name)[0]


def _row_tile(T, want):
    t = min(T, want)
    while T % t:
        t //= 2
    return t


def _col_tile(N, want):
    if N <= want:
        return N
    best = None
    for c in range(LANES, want + 1, LANES):
        if N % c == 0:
            best = c
    return best if best is not None else N


def _accumulate(ref, first, val):
    @pl.when(first)
    def _():
        ref[...] = val

    @pl.when(jnp.logical_not(first))
    def _():
        ref[...] += val


def _ln_fwd(z, g, b):
    mu = jnp.mean(z, axis=-1, keepdims=True)
    zc = z - mu
    var = jnp.mean(zc * zc, axis=-1, keepdims=True)
    r = lax.rsqrt(var + LN_EPS)
    xh = zc * r
    return xh * g + b, xh, r


def _ln_bwd(dy, xh, r, g):
    dxh = dy * g
    m1 = jnp.mean(dxh, axis=-1, keepdims=True)
    m2 = jnp.mean(dxh * xh, axis=-1, keepdims=True)
    return r * (dxh - m1 - xh * m2)


def _sigmoid(x):
    return 1.0 / (1.0 + jnp.exp(-x))


def _shift_down(x, s, row):
    if s == 0:
        return x
    return jnp.where(row >= s, pltpu.roll(x, s, 0), 0.0)


def _shift_up(x, s, row):
    if s == 0:
        return x
    n = x.shape[0]
    return jnp.where(row < n - s, pltpu.roll(x, n - s, 0), 0.0)


def _bucket_tables():
    exact = REL_BUCKETS // 2
    qi = np.arange(ATTN_BLOCK)[:, None]
    kj = np.arange(2 * ATTN_BLOCK)[None, :]
    steps = qi + ATTN_BLOCK - kj
    buckets, masks = [], []
    for window, dilation in DILATED_CONFIGS:
        max_steps = window // dilation
        band = (steps >= 0) & (steps <= max_steps)
        dist = np.maximum(steps, 0) * dilation
        d_f = np.maximum(dist, 1).astype(np.float32)
        large = exact + (np.log(d_f / np.float32(exact)) / np.float32(math.log(REL_MAX_DIST / exact))
                         * np.float32(REL_BUCKETS - exact)).astype(np.int32)
        large = np.minimum(large, REL_BUCKETS - 1)
        bucket = np.where(dist < exact, dist, large).astype(np.int32)
        buckets.append(bucket.reshape(1, -1))
        masks.append(np.where(band, 0.0, NEG_INF).astype(np.float32).reshape(1, -1))
    return np.stack(buckets), np.stack(masks)


def _split_hi_lo(x):
    hi = x.astype(BF16)
    lo = (x - hi.astype(F32)).astype(BF16)
    return hi, lo


def _bias_build(rel_table_t, bucket, mask):
    H = rel_table_t.shape[0]
    n = bucket.shape[-1]

    def body(t_ref, bkt_ref, mask_ref, o_ref):
        onehot = (lax.broadcasted_iota(jnp.int32, (REL_BUCKETS, n), 0) == bkt_ref[0]).astype(BF16)
        t = t_ref[...]
        t1 = t.astype(BF16)
        r1 = t - t1.astype(F32)
        t2 = r1.astype(BF16)
        t3 = (r1 - t2.astype(F32)).astype(BF16)
        acc = jnp.dot(t1, onehot, preferred_element_type=F32)
        acc = acc + jnp.dot(t2, onehot, preferred_element_type=F32)
        acc = acc + jnp.dot(t3, onehot, preferred_element_type=F32)
        o_ref[0] = acc + mask_ref[0]

    return pl.pallas_call(
        body, grid=(3,),
        in_specs=[pl.BlockSpec((H, REL_BUCKETS), lambda b: (0, 0)),
                  pl.BlockSpec((1, 1, n), lambda b: (b, 0, 0)),
                  pl.BlockSpec((1, 1, n), lambda b: (b, 0, 0))],
        out_specs=pl.BlockSpec((1, H, n), lambda b: (b, 0, 0)),
        out_shape=jax.ShapeDtypeStruct((3, H, n), F32),
        compiler_params=_params(1), name="bias_build",
    )(rel_table_t, bucket, mask)


def _rel_grad(dbias, bucket):
    H = dbias.shape[1]
    n = bucket.shape[-1]
    dims = (((1,), (1,)), ((), ()))

    def body(d_ref, bkt_ref, o_ref):
        b = pl.program_id(0)
        onehot = (lax.broadcasted_iota(jnp.int32, (REL_BUCKETS, n), 0) == bkt_ref[0]).astype(BF16)
        d = d_ref[0]
        d1 = d.astype(BF16)
        r1 = d - d1.astype(F32)
        d2 = r1.astype(BF16)
        d3 = (r1 - d2.astype(F32)).astype(BF16)
        acc = lax.dot_general(d1, onehot, dims, preferred_element_type=F32)
        acc = acc + lax.dot_general(d2, onehot, dims, preferred_element_type=F32)
        acc = acc + lax.dot_general(d3, onehot, dims, preferred_element_type=F32)
        _accumulate(o_ref, b == 0, acc)

    return pl.pallas_call(
        body, grid=(3,),
        in_specs=[pl.BlockSpec((1, H, n), lambda b: (b, 0, 0)),
                  pl.BlockSpec((1, 1, n), lambda b: (b, 0, 0))],
        out_specs=pl.BlockSpec((H, REL_BUCKETS), lambda b: (0, 0)),
        out_shape=jax.ShapeDtypeStruct((H, REL_BUCKETS), F32),
        compiler_params=_params(1), name="rel_grad",
    )(dbias, bucket)


def _attn_specs(B, S, AW, d):
    L = S // d
    HP = AW // LANES
    W3 = 3 * HP
    q_spec = pl.BlockSpec((1, L, LANES), lambda h, b, r: (b, 0, r * W3 + h))
    k_spec = pl.BlockSpec((1, L, LANES), lambda h, b, r: (b, 0, r * W3 + HP + h))
    v_spec = pl.BlockSpec((1, L, LANES), lambda h, b, r: (b, 0, r * W3 + 2 * HP + h))
    o_spec = pl.BlockSpec((1, L, LANES), lambda h, b, r: (b, 0, r * HP + h))
    bias_spec = pl.BlockSpec((2, ATTN_BLOCK, 2 * ATTN_BLOCK), lambda h, b, r: (h, 0, 0))
    return L, HP, q_spec, k_spec, v_spec, o_spec, bias_spec


def _attn_fwd(qkv, bias, B, S, AW, d, name):
    L, HP, q_spec, k_spec, v_spec, o_spec, bias_spec = _attn_specs(B, S, AW, d)
    nb = L // ATTN_BLOCK
    nt = (((1,), (1,)), ((), ()))

    def body(q_ref, k_ref, v_ref, b_ref, o_ref, lse_ref):
        head0 = lax.broadcasted_iota(jnp.int32, (1, LANES), 1) < HEAD_DIM

        def block(n, first):
            qs = pl.multiple_of(n * ATTN_BLOCK, ATTN_BLOCK)
            q = q_ref[0, pl.ds(qs, ATTN_BLOCK), :]
            if first:
                kk = k_ref[0, pl.ds(0, ATTN_BLOCK), :]
                vv = v_ref[0, pl.ds(0, ATTN_BLOCK), :]
            else:
                ks = pl.multiple_of(n * ATTN_BLOCK - ATTN_BLOCK, ATTN_BLOCK)
                kk = k_ref[0, pl.ds(ks, 2 * ATTN_BLOCK), :]
                vv = v_ref[0, pl.ds(ks, 2 * ATTN_BLOCK), :]
            outs, lses = [], []
            for e in range(2):
                msk = head0 if e == 0 else jnp.logical_not(head0)
                qe = jnp.where(msk, q, jnp.zeros_like(q))
                s = lax.dot_general(qe, kk, nt, preferred_element_type=F32) * QK_SCALE
                s = s + (b_ref[e, :, ATTN_BLOCK:] if first else b_ref[e])
                m = jnp.max(s, axis=-1, keepdims=True)
                p = jnp.exp(s - m)
                l = jnp.sum(p, axis=-1, keepdims=True)
                o = jnp.dot(p.astype(BF16), vv, preferred_element_type=F32)
                outs.append(o / l)
                lses.append(jnp.broadcast_to(m + jnp.log(l), (ATTN_BLOCK, LANES)))
            o_ref[0, pl.ds(qs, ATTN_BLOCK), :] = jnp.where(head0, outs[0], outs[1])
            lse_ref[0, pl.ds(qs, ATTN_BLOCK), :] = jnp.where(head0, lses[0], lses[1])

        block(0, True)
        if nb > 1:
            def loop(n, c):
                block(n, False)
                return c
            lax.fori_loop(1, nb, loop, 0)

    qv = qkv.reshape(B, L, d * 3 * AW)
    o, lse = pl.pallas_call(
        body, grid=(HP, B, d), in_specs=[q_spec, k_spec, v_spec, bias_spec],
        out_specs=[o_spec, o_spec],
        out_shape=[jax.ShapeDtypeStruct((B, L, d * AW), F32)] * 2,
        compiler_params=_params(3), name=name,
    )(qv, qv, qv, bias)
    return o.reshape(B * S, AW), lse.reshape(B * S, AW)


def _attn_bwd(qkv, do, lse, dd, bias, B, S, AW, d, name):
    L, HP, q_spec, k_spec, v_spec, o_spec, bias_spec = _attn_specs(B, S, AW, d)
    nb = L // ATTN_BLOCK
    nt = (((1,), (1,)), ((), ()))
    tn = (((0,), (0,)), ((), ()))

    def body(q_ref, k_ref, v_ref, do_ref, lse_ref, dd_ref, b_ref, dq_ref, dk_ref, dv_ref, db_ref):
        head0 = lax.broadcasted_iota(jnp.int32, (1, LANES), 1) < HEAD_DIM
        first_step = jnp.logical_and(pl.program_id(1) == 0, pl.program_id(2) == 0)

        @pl.when(first_step)
        def _():
            db_ref[...] = jnp.zeros_like(db_ref)

        dk_ref[...] = jnp.zeros_like(dk_ref)
        dv_ref[...] = jnp.zeros_like(dv_ref)

        def block(n, first):
            qs = pl.multiple_of(n * ATTN_BLOCK, ATTN_BLOCK)
            nkeys = ATTN_BLOCK if first else 2 * ATTN_BLOCK
            ks = 0 if first else pl.multiple_of(n * ATTN_BLOCK - ATTN_BLOCK, ATTN_BLOCK)
            q = q_ref[0, pl.ds(qs, ATTN_BLOCK), :]
            kk = k_ref[0, pl.ds(ks, nkeys), :]
            vv = v_ref[0, pl.ds(ks, nkeys), :]
            dout = do_ref[0, pl.ds(qs, ATTN_BLOCK), :]
            lse_b = lse_ref[0, pl.ds(qs, ATTN_BLOCK), :]
            dd_b = dd_ref[0, pl.ds(qs, ATTN_BLOCK), :]
            dq = jnp.zeros((ATTN_BLOCK, LANES), F32)
            dkk = jnp.zeros((nkeys, LANES), F32)
            dvv = jnp.zeros((nkeys, LANES), F32)
            for e in range(2):
                msk = head0 if e == 0 else jnp.logical_not(head0)
                c0 = e * HEAD_DIM
                qe = jnp.where(msk, q, jnp.zeros_like(q))
                doe = jnp.where(msk, dout, jnp.zeros_like(dout))
                kke = jnp.where(msk, kk, jnp.zeros_like(kk))
                s = lax.dot_general(qe, kk, nt, preferred_element_type=F32) * QK_SCALE
                s = s + (b_ref[e, :, ATTN_BLOCK:] if first else b_ref[e])
                p = jnp.exp(s - lse_b[:, c0:c0 + 1])
                dp = lax.dot_general(doe, vv, nt, preferred_element_type=F32)
                ds = p * (dp - dd_b[:, c0:c0 + 1])
                if first:
                    db_ref[e, :, ATTN_BLOCK:] += ds
                else:
                    db_ref[e] += ds
                dsb = (ds * QK_SCALE).astype(BF16)
                dq = dq + jnp.dot(dsb, kke, preferred_element_type=F32)
                dkk = dkk + lax.dot_general(dsb, qe, tn, preferred_element_type=F32)
                dvv = dvv + lax.dot_general(p.astype(BF16), doe, tn, preferred_element_type=F32)
            dq_ref[0, pl.ds(qs, ATTN_BLOCK), :] = dq
            dk_ref[0, pl.ds(ks, nkeys), :] += dkk
            dv_ref[0, pl.ds(ks, nkeys), :] += dvv

        block(0, True)
        if nb > 1:
            def loop(n, c):
                block(n, False)
                return c
            lax.fori_loop(1, nb, loop, 0)

    H = AW // HEAD_DIM
    qv = qkv.reshape(B, L, d * 3 * AW)
    view = lambda t: t.reshape(B, L, d * AW)
    dq, dk, dv, db = pl.pallas_call(
        body, grid=(HP, B, d),
        in_specs=[q_spec, k_spec, v_spec, o_spec, o_spec, o_spec, bias_spec],
        out_specs=[o_spec, o_spec, o_spec, bias_spec],
        out_shape=[jax.ShapeDtypeStruct((B, L, d * AW), F32)] * 3
        + [jax.ShapeDtypeStruct((H, ATTN_BLOCK, 2 * ATTN_BLOCK), F32)],
        compiler_params=_params(3), name=name,
    )(qv, qv, qv, view(do), view(lse), view(dd), bias)
    flat = lambda t: t.reshape(B * S, AW)
    return flat(dq), flat(dk), flat(dv), db


def _attn_combine(ons, lses, gain, tm):
    T, AW = ons[0].shape

    def body(o1, o2, o3, l1, l2, l3, g_ref, attn_ref, lse_ref, mix_ref, r_ref):
        la, lb, lc = l1[...], l2[...], l3[...]
        m = jnp.maximum(jnp.maximum(la, lb), lc)
        ea, eb, ec = jnp.exp(la - m), jnp.exp(lb - m), jnp.exp(lc - m)
        den = ea + eb + ec
        attn = (ea * o1[...] + eb * o2[...] + ec * o3[...]) / den
        attn_ref[...] = attn
        lse_ref[...] = m + jnp.log(den)
        r = lax.rsqrt(jnp.mean(attn * attn, axis=-1, keepdims=True) + LN_EPS)
        mix_ref[...] = (attn * r * g_ref[...]).astype(BF16)
        r_ref[...] = jnp.broadcast_to(r, (tm, LANES))

    row = pl.BlockSpec((tm, AW), lambda i: (i, 0))
    return pl.pallas_call(
        body, grid=(T // tm,),
        in_specs=[row] * 6 + [pl.BlockSpec((1, AW), lambda i: (0, 0))],
        out_specs=[row, row, row, pl.BlockSpec((tm, LANES), lambda i: (i, 0))],
        out_shape=[jax.ShapeDtypeStruct((T, AW), F32), jax.ShapeDtypeStruct((T, AW), F32),
                   jax.ShapeDtypeStruct((T, AW), BF16), jax.ShapeDtypeStruct((T, LANES), F32)],
        compiler_params=_params(1), name="attn_combine",
    )(*ons, *lses, gain)


def _to_sub(src_ref, stage_ref, dsts, S):
    stage_ref[...] = src_ref[0].astype(F32)
    for (_, d), dst in zip(DILATED_CONFIGS[1:], dsts):
        L = S // d
        for r in range(d):
            dst[r * L:(r + 1) * L, :] = stage_ref[pl.ds(r, L, stride=d), :].astype(dst.dtype)


def _branch_blocks(S, d, block):
    nb = S // d // ATTN_BLOCK

    def per_residue(r, c):
        block(r * nb, True)
        if nb > 1:
            def inner(n, c2):
                block(r * nb + n, False)
                return c2
            lax.fori_loop(1, nb, inner, 0)
        return c

    lax.fori_loop(0, d, per_residue, 0)


def _attention_fwd(qkv, bias_all, B, S, AW):
    HP = AW // LANES
    nt = MM_DIMS["nt"]

    def body(q_ref, k_ref, v_ref, b_ref, o_ref, lse_ref, stage, q4, q16, k4, k16, v4, v16, o1, l1, o4, l4, o16, l16):
        head0 = lax.broadcasted_iota(jnp.int32, (1, LANES), 1) < HEAD_DIM
        _to_sub(q_ref, stage, (q4, q16), S)
        _to_sub(k_ref, stage, (k4, k16), S)
        _to_sub(v_ref, stage, (v4, v16), S)
        srcs = ((q_ref.at[0], k_ref.at[0], v_ref.at[0], o1, l1), (q4, k4, v4, o4, l4), (q16, k16, v16, o16, l16))
        for bi, (_, d) in enumerate(DILATED_CONFIGS):
            qs_ref, ks_ref, vs_ref, od_ref, ld_ref = srcs[bi]

            def block(g, first, bi=bi, qs_ref=qs_ref, ks_ref=ks_ref, vs_ref=vs_ref, od_ref=od_ref, ld_ref=ld_ref):
                qs = pl.multiple_of(g * ATTN_BLOCK, ATTN_BLOCK)
                nkeys = ATTN_BLOCK if first else 2 * ATTN_BLOCK
                ks = qs if first else pl.multiple_of(qs - ATTN_BLOCK, ATTN_BLOCK)
                q = qs_ref[pl.ds(qs, ATTN_BLOCK), :]
                kk = ks_ref[pl.ds(ks, nkeys), :]
                vv = vs_ref[pl.ds(ks, nkeys), :]
                outs, lses = [], []
                for e in range(2):
                    msk = head0 if e == 0 else jnp.logical_not(head0)
                    qe = jnp.where(msk, q, jnp.zeros_like(q))
                    s = lax.dot_general(qe, kk, nt, preferred_element_type=F32) * QK_SCALE
                    s = s + (b_ref[bi, e, :, ATTN_BLOCK:] if first else b_ref[bi, e])
                    m = jnp.max(s, axis=-1, keepdims=True)
                    p = jnp.exp(s - m)
                    l = jnp.sum(p, axis=-1, keepdims=True)
                    o = jnp.dot(p.astype(BF16), vv, preferred_element_type=F32)
                    outs.append(o / l)
                    lses.append(jnp.broadcast_to(m + jnp.log(l), (ATTN_BLOCK, LANES)))
                od_ref[pl.ds(qs, ATTN_BLOCK), :] = jnp.where(head0, outs[0], outs[1])
                ld_ref[pl.ds(qs, ATTN_BLOCK), :] = jnp.where(head0, lses[0], lses[1])

            _branch_blocks(S, d, block)

        def natural(sub_ref, d):
            L = S // d
            for r in range(d):
                stage[pl.ds(r, L, stride=d), :] = sub_ref[r * L:(r + 1) * L, :]
            return stage[...]

        la = l1[...]
        lb = natural(l4, 4)
        lc = natural(l16, 16)
        m = jnp.maximum(jnp.maximum(la, lb), lc)
        ea, eb, ec = jnp.exp(la - m), jnp.exp(lb - m), jnp.exp(lc - m)
        den = ea + eb + ec
        lse_ref[0] = m + jnp.log(den)
        acc = ea * o1[...]
        acc = acc + eb * natural(o4, 4)
        acc = acc + ec * natural(o16, 16)
        o_ref[0] = acc / den

    blk = lambda off: pl.BlockSpec((1, S, LANES), lambda b, h: (b, 0, off + h))
    qv = qkv.reshape(B, S, 3 * AW)
    sub_b = pltpu.VMEM((S, LANES), BF16)
    sub_f = pltpu.VMEM((S, LANES), F32)
    o, lse = pl.pallas_call(
        body, grid=(B, HP),
        in_specs=[blk(0), blk(HP), blk(2 * HP),
                  pl.BlockSpec((3, 2, ATTN_BLOCK, 2 * ATTN_BLOCK), lambda b, h: (0, h, 0, 0))],
        out_specs=[blk(0), blk(0)],
        out_shape=[jax.ShapeDtypeStruct((B, S, AW), F32)] * 2,
        scratch_shapes=[sub_f] + [sub_b] * 6 + [sub_f] * 6,
        compiler_params=_params(2), name="attention_fwd",
    )(qv, qv, qv, bias_all)
    return o.reshape(B * S, AW), lse.reshape(B * S, AW)


def _attention_bwd(qkv, do, lse, dd, bias_all, B, S, AW):
    HP = AW // LANES
    H = AW // HEAD_DIM
    nt, tn = MM_DIMS["nt"], MM_DIMS["tn"]

    def body(q_ref, k_ref, v_ref, do_ref, lse_ref, dd_ref, b_ref,
             dq_ref, dk_ref, dv_ref, csq_ref, csk_ref, csv_ref, db_ref,
             stage, q4, q16, k4, k16, v4, v16, g4, g16, l4, l16, d4, d16,
             aq1, ak1, av1, aq4, ak4, av4, aq16, ak16, av16):
        head0 = lax.broadcasted_iota(jnp.int32, (1, LANES), 1) < HEAD_DIM
        first_b = pl.program_id(1) == 0

        @pl.when(first_b)
        def _():
            db_ref[...] = jnp.zeros_like(db_ref)

        _to_sub(q_ref, stage, (q4, q16), S)
        _to_sub(k_ref, stage, (k4, k16), S)
        _to_sub(v_ref, stage, (v4, v16), S)
        _to_sub(do_ref, stage, (g4, g16), S)
        _to_sub(lse_ref, stage, (l4, l16), S)
        _to_sub(dd_ref, stage, (d4, d16), S)
        for acc in (ak1, av1, ak4, av4, ak16, av16):
            acc[...] = jnp.zeros_like(acc)
        srcs = ((q_ref.at[0], k_ref.at[0], v_ref.at[0], do_ref.at[0], lse_ref.at[0], dd_ref.at[0], aq1, ak1, av1),
                (q4, k4, v4, g4, l4, d4, aq4, ak4, av4), (q16, k16, v16, g16, l16, d16, aq16, ak16, av16))
        for bi, (_, d) in enumerate(DILATED_CONFIGS):
            def block(g, first, bi=bi, refs=srcs[bi]):
                qs_ref, ks_ref, vs_ref, gs_ref, ls_ref, ds_ref, aq, ak, av = refs
                qs = pl.multiple_of(g * ATTN_BLOCK, ATTN_BLOCK)
                nkeys = ATTN_BLOCK if first else 2 * ATTN_BLOCK
                ks = qs if first else pl.multiple_of(qs - ATTN_BLOCK, ATTN_BLOCK)
                q = qs_ref[pl.ds(qs, ATTN_BLOCK), :]
                kk = ks_ref[pl.ds(ks, nkeys), :]
                vv = vs_ref[pl.ds(ks, nkeys), :]
                dout = gs_ref[pl.ds(qs, ATTN_BLOCK), :]
                lse_b = ls_ref[pl.ds(qs, ATTN_BLOCK), :]
                dd_b = ds_ref[pl.ds(qs, ATTN_BLOCK), :]
                dq = jnp.zeros((ATTN_BLOCK, LANES), F32)
                dkk = jnp.zeros((nkeys, LANES), F32)
                dvv = jnp.zeros((nkeys, LANES), F32)
                for e in range(2):
                    msk = head0 if e == 0 else jnp.logical_not(head0)
                    c0 = e * HEAD_DIM
                    qe = jnp.where(msk, q, jnp.zeros_like(q))
                    doe = jnp.where(msk, dout, jnp.zeros_like(dout))
                    kke = jnp.where(msk, kk, jnp.zeros_like(kk))
                    s = lax.dot_general(qe, kk, nt, preferred_element_type=F32) * QK_SCALE
                    s = s + (b_ref[bi, e, :, ATTN_BLOCK:] if first else b_ref[bi, e])
                    p = jnp.exp(s - lse_b[:, c0:c0 + 1])
                    dp = lax.dot_general(doe, vv, nt, preferred_element_type=F32)
                    ds = p * (dp - dd_b[:, c0:c0 + 1])
                    if first:
                        db_ref[bi, e, :, ATTN_BLOCK:] += ds
                    else:
                        db_ref[bi, e] += ds
                    dsb = (ds * QK_SCALE).astype(BF16)
                    dq = dq + jnp.dot(dsb, kke, preferred_element_type=F32)
                    dkk = dkk + lax.dot_general(dsb, qe, tn, preferred_element_type=F32)
                    dvv = dvv + lax.dot_general(p.astype(BF16), doe, tn, preferred_element_type=F32)
                aq[pl.ds(qs, ATTN_BLOCK), :] = dq
                ak[pl.ds(ks, nkeys), :] += dkk
                av[pl.ds(ks, nkeys), :] += dvv

            _branch_blocks(S, d, block)

        for a1, a4, a16, out_ref, cs_ref in ((aq1, aq4, aq16, dq_ref, csq_ref), (ak1, ak4, ak16, dk_ref, csk_ref),
                                             (av1, av4, av16, dv_ref, csv_ref)):
            stage[...] = a1[...]
            for d, sub in ((4, a4), (16, a16)):
                L = S // d
                for r in range(d):
                    stage[pl.ds(r, L, stride=d), :] += sub[r * L:(r + 1) * L, :]
            tot = stage[...]
            out_ref[0] = tot.astype(out_ref.dtype)
            _accumulate(cs_ref, first_b, jnp.sum(tot, axis=0, keepdims=True))

    blk = lambda off: pl.BlockSpec((1, S, LANES), lambda h, b: (b, 0, off + h))
    cs_spec = pl.BlockSpec((1, LANES), lambda h, b: (0, h))
    bias_spec = pl.BlockSpec((3, 2, ATTN_BLOCK, 2 * ATTN_BLOCK), lambda h, b: (0, h, 0, 0))
    qv = qkv.reshape(B, S, 3 * AW)
    view = lambda t: t.reshape(B, S, AW)
    sub_b = pltpu.VMEM((S, LANES), BF16)
    sub_f = pltpu.VMEM((S, LANES), F32)
    res = pl.pallas_call(
        body, grid=(HP, B),
        in_specs=[blk(0), blk(HP), blk(2 * HP), blk(0), blk(0), blk(0), bias_spec],
        out_specs=[blk(0), blk(0), blk(0), cs_spec, cs_spec, cs_spec, bias_spec],
        out_shape=[jax.ShapeDtypeStruct((B, S, AW), BF16)] * 3 + [jax.ShapeDtypeStruct((1, AW), F32)] * 3
        + [jax.ShapeDtypeStruct((3, H, ATTN_BLOCK, 2 * ATTN_BLOCK), F32)],
        scratch_shapes=[sub_f] + [sub_b] * 8 + [sub_f] * 4 + [sub_f] * 9,
        compiler_params=_params(2), name="attention_bwd",
    )(qv, qv, qv, view(do), view(lse), view(dd), bias_all)
    flat = lambda t: t.reshape(B * S, AW)
    return flat(res[0]), flat(res[1]), flat(res[2]), res[3], res[4], res[5], res[6]


def _attn_norm(attn, gain, tm):
    T, AW = attn.shape

    def body(a_ref, g_ref, mix_ref, r_ref):
        a = a_ref[...]
        r = lax.rsqrt(jnp.mean(a * a, axis=-1, keepdims=True) + LN_EPS)
        mix_ref[...] = (a * r * g_ref[...]).astype(BF16)
        r_ref[...] = jnp.broadcast_to(r, (tm, LANES))

    row = pl.BlockSpec((tm, AW), lambda i: (i, 0))
    return pl.pallas_call(
        body, grid=(T // tm,), in_specs=[row, pl.BlockSpec((1, AW), lambda i: (0, 0))],
        out_specs=[row, pl.BlockSpec((tm, LANES), lambda i: (i, 0))],
        out_shape=[jax.ShapeDtypeStruct((T, AW), BF16), jax.ShapeDtypeStruct((T, LANES), F32)],
        compiler_params=_params(1), name="attn_norm",
    )(attn, gain)


def _attn_pre_bwd(dmixed, attn, rstd, gain, tm):
    T, AW = attn.shape
    ones_np = np.kron(np.eye(AW // HEAD_DIM, dtype=np.float32), np.ones((HEAD_DIM, HEAD_DIM), np.float32))
    ones_bd = jnp.asarray(ones_np, dtype=BF16)

    def body(dm_ref, a_ref, r_ref, g_ref, ones_ref, do_ref, dd_ref, dg_ref):
        i = pl.program_id(0)
        dm = dm_ref[...]
        a = a_ref[...]
        r = r_ref[:, 0:1]
        dxn = dm * g_ref[...]
        da = r * (dxn - a * (r * r) * jnp.mean(dxn * a, axis=-1, keepdims=True))
        do_ref[...] = da.astype(BF16)
        hi, lo = _split_hi_lo(da * a)
        dd_ref[...] = (jnp.dot(hi, ones_ref[...], preferred_element_type=F32)
                       + jnp.dot(lo, ones_ref[...], preferred_element_type=F32))
        _accumulate(dg_ref, i == 0, jnp.sum(dm * a * r, axis=0, keepdims=True))

    row = pl.BlockSpec((tm, AW), lambda i: (i, 0))
    vec = pl.BlockSpec((1, AW), lambda i: (0, 0))
    return pl.pallas_call(
        body, grid=(T // tm,),
        in_specs=[row, row, pl.BlockSpec((tm, LANES), lambda i: (i, 0)), vec,
                  pl.BlockSpec((AW, AW), lambda i: (0, 0))],
        out_specs=[row, row, vec],
        out_shape=[jax.ShapeDtypeStruct((T, AW), BF16), jax.ShapeDtypeStruct((T, AW), F32),
                   jax.ShapeDtypeStruct((1, AW), F32)],
        compiler_params=_params(1), name="attn_pre_bwd",
    )(dmixed, attn, rstd, gain, ones_bd)


def _conv_branch_fwd_math(a, g, w_ref, cb, lg, lb, row):
    sg = _sigmoid(g)
    u0 = a * sg
    uc = jnp.zeros_like(u0) + cb
    for k in range(CONV_KERNEL):
        uc = uc + w_ref[k:k + 1, :] * _shift_down(u0, CONV_KERNEL - 1 - k, row)
    ul, xh, r = _ln_fwd(uc, lg, lb)
    su = _sigmoid(ul)
    u = ul * su
    return sg, u0, ul, xh, r, su, u


def _conv_fwd(ag, conv_w, conv_b, ln_g, ln_b, norm_g, B, S, CW):
    def body(a_ref, g_ref, w_ref, cb_ref, lg_ref, lb_ref, ng_ref, o_ref):
        row = lax.broadcasted_iota(jnp.int32, (S, CW), 0)
        _, _, _, _, _, _, u = _conv_branch_fwd_math(a_ref[0], g_ref[0], w_ref, cb_ref[...], lg_ref[...],
                                                    lb_ref[...], row)
        rr = lax.rsqrt(jnp.mean(u * u, axis=-1, keepdims=True) + LN_EPS)
        o_ref[0] = (u * rr * ng_ref[...]).astype(BF16)

    vec = pl.BlockSpec((1, CW), lambda b: (0, 0))
    out = pl.pallas_call(
        body, grid=(B,),
        in_specs=[pl.BlockSpec((1, S, CW), lambda b: (b, 0, 0)), pl.BlockSpec((1, S, CW), lambda b: (b, 0, 1)),
                  pl.BlockSpec((CONV_KERNEL, CW), lambda b: (0, 0)), vec, vec, vec, vec],
        out_specs=pl.BlockSpec((1, S, CW), lambda b: (b, 0, 0)),
        out_shape=jax.ShapeDtypeStruct((B, S, CW), BF16),
        compiler_params=_params(1), name="conv_fwd",
    )(ag.reshape(B, S, 2 * CW), ag.reshape(B, S, 2 * CW), conv_w, conv_b, ln_g, ln_b, norm_g)
    return out.reshape(B * S, CW)


def _conv_bwd(ag, dmixed, conv_w, conv_b, ln_g, ln_b, norm_g, B, S, CW, D):
    AW = D - CW
    assert AW % CW == 0

    def body(a_ref, g_ref, dm_ref, w_ref, cb_ref, lg_ref, lb_ref, ng_ref,
             dag_ref, dw_ref, dcb_ref, dlg_ref, dlb_ref, dng_ref):
        b = pl.program_id(0)
        row = lax.broadcasted_iota(jnp.int32, (S, CW), 0)
        a, g = a_ref[0], g_ref[0]
        sg, u0, ul, xh, r, su, u = _conv_branch_fwd_math(a, g, w_ref, cb_ref[...], lg_ref[...], lb_ref[...], row)
        rr = lax.rsqrt(jnp.mean(u * u, axis=-1, keepdims=True) + LN_EPS)
        dm = dm_ref[0]
        dxn = dm * ng_ref[...]
        du = rr * (dxn - u * (rr * rr) * jnp.mean(dxn * u, axis=-1, keepdims=True))
        dul = du * su * (1.0 + ul * (1.0 - su))
        duc = _ln_bwd(dul, xh, r, lg_ref[...])
        first = b == 0
        _accumulate(dng_ref, first, jnp.sum(dm * u * rr, axis=0, keepdims=True))
        _accumulate(dlg_ref, first, jnp.sum(dul * xh, axis=0, keepdims=True))
        _accumulate(dlb_ref, first, jnp.sum(dul, axis=0, keepdims=True))
        _accumulate(dcb_ref, first, jnp.sum(duc, axis=0, keepdims=True))

        @pl.when(first)
        def _():
            dw_ref[...] = jnp.zeros_like(dw_ref)

        du0 = jnp.zeros_like(u0)
        for k in range(CONV_KERNEL):
            sh = CONV_KERNEL - 1 - k
            dw_ref[k:k + 1, :] += jnp.sum(duc * _shift_down(u0, sh, row), axis=0, keepdims=True)
            du0 = du0 + w_ref[k:k + 1, :] * _shift_up(duc, sh, row)
        dag_ref[0, :, :CW] = du0 * sg
        dag_ref[0, :, CW:] = du0 * a * sg * (1.0 - sg)

    vec = pl.BlockSpec((1, CW), lambda b: (0, 0))
    wspec = pl.BlockSpec((CONV_KERNEL, CW), lambda b: (0, 0))
    agv = ag.reshape(B, S, 2 * CW)
    res = pl.pallas_call(
        body, grid=(B,),
        in_specs=[pl.BlockSpec((1, S, CW), lambda b: (b, 0, 0)), pl.BlockSpec((1, S, CW), lambda b: (b, 0, 1)),
                  pl.BlockSpec((1, S, CW), lambda b: (b, 0, AW // CW)), wspec, vec, vec, vec, vec],
        out_specs=[pl.BlockSpec((1, S, 2 * CW), lambda b: (b, 0, 0)), wspec, vec, vec, vec, vec],
        out_shape=[jax.ShapeDtypeStruct((B, S, 2 * CW), F32), jax.ShapeDtypeStruct((CONV_KERNEL, CW), F32)]
        + [jax.ShapeDtypeStruct((1, CW), F32)] * 4,
        compiler_params=_params(1), name="conv_bwd",
    )(agv, agv, dmixed.reshape(B, S, D), conv_w, conv_b, ln_g, ln_b, norm_g)
    return (res[0].reshape(B * S, 2 * CW),) + tuple(res[1:])


def _ffn_conv(x, w_ref, bias, row):
    y = jnp.zeros_like(x) + bias
    for k in range(FFN_CONV_KERNEL):
        y = y + w_ref[k:k + 1, :] * _shift_down(x, FFN_CONV_KERNEL - 1 - k, row)
    return y


def _ffn_specs(S, tc, nj, order):
    pick = (lambda b, j: (b, j)) if order == "bj" else (lambda j, b: (b, j))
    act = lambda off: pl.BlockSpec((1, S, tc), lambda *g: (pick(*g)[0], 0, off + pick(*g)[1]))
    cw = lambda off: pl.BlockSpec((FFN_CONV_KERNEL, tc), lambda *g: (0, off + pick(*g)[1]))
    cb = lambda off: pl.BlockSpec((1, tc), lambda *g: (0, off + pick(*g)[1]))
    return act, cw, cb


def _ffn_act(upre, cw, cb, B, S, DFF):
    tc = FFN_COLS
    nj = DFF // tc

    def body(ug_ref, uv_ref, wg_ref, wv_ref, bg_ref, bv_ref, o_ref):
        row = lax.broadcasted_iota(jnp.int32, (S, tc), 0)
        gate = _ffn_conv(ug_ref[0], wg_ref, bg_ref[...], row)
        val = _ffn_conv(uv_ref[0], wv_ref, bv_ref[...], row)
        o_ref[0] = (gate * _sigmoid(gate) * val).astype(BF16)

    act, cws, cbs = _ffn_specs(S, tc, nj, "bj")
    uv = upre.reshape(B, S, 2 * DFF)
    out = pl.pallas_call(
        body, grid=(B, nj), in_specs=[act(0), act(nj), cws(0), cws(nj), cbs(0), cbs(nj)], out_specs=act(0),
        out_shape=jax.ShapeDtypeStruct((B, S, DFF), BF16), compiler_params=_params(2), name="ffn_act",
    )(uv, uv, cw, cw, cb, cb)
    return out.reshape(B * S, DFF)


def _ffn_bwd(upre, dact, cw, cb, B, S, DFF):
    tc = FFN_COLS
    nj = DFF // tc

    def body(ug_ref, uv_ref, da_ref, wg_ref, wv_ref, bg_ref, bv_ref, dug_ref, duv_ref, dwg_ref, dwv_ref,
             dbg_ref, dbv_ref):
        first = pl.program_id(1) == 0
        row = lax.broadcasted_iota(jnp.int32, (S, tc), 0)
        ug, uv = ug_ref[0], uv_ref[0]
        gate = _ffn_conv(ug, wg_ref, bg_ref[...], row)
        val = _ffn_conv(uv, wv_ref, bv_ref[...], row)
        sg = _sigmoid(gate)
        dact_b = da_ref[0]
        dgate = dact_b * val * sg * (1.0 + gate * (1.0 - sg))
        dval = dact_b * gate * sg
        for dup, u, w_ref, du_ref, dw_ref, db_ref in ((dgate, ug, wg_ref, dug_ref, dwg_ref, dbg_ref),
                                                      (dval, uv, wv_ref, duv_ref, dwv_ref, dbv_ref)):
            _accumulate(db_ref, first, jnp.sum(dup, axis=0, keepdims=True))

            @pl.when(first)
            def _(dw_ref=dw_ref):
                dw_ref[...] = jnp.zeros_like(dw_ref)

            dupre = jnp.zeros_like(dup)
            for k in range(FFN_CONV_KERNEL):
                sh = FFN_CONV_KERNEL - 1 - k
                dw_ref[k:k + 1, :] += jnp.sum(dup * _shift_down(u, sh, row), axis=0, keepdims=True)
                dupre = dupre + w_ref[k:k + 1, :] * _shift_up(dup, sh, row)
            du_ref[0] = dupre.astype(BF16)

    act, cws, cbs = _ffn_specs(S, tc, nj, "jb")
    uv = upre.reshape(B, S, 2 * DFF)
    res = pl.pallas_call(
        body, grid=(nj, B),
        in_specs=[act(0), act(nj), act(0), cws(0), cws(nj), cbs(0), cbs(nj)],
        out_specs=[act(0), act(0), cws(0), cws(0), cbs(0), cbs(0)],
        out_shape=[jax.ShapeDtypeStruct((B, S, DFF), BF16)] * 2
        + [jax.ShapeDtypeStruct((FFN_CONV_KERNEL, DFF), F32)] * 2 + [jax.ShapeDtypeStruct((1, DFF), F32)] * 2,
        compiler_params=_params(2), name="ffn_bwd",
    )(uv, uv, dact.reshape(B, S, DFF), cw, cw, cb, cb)
    flat = lambda t: t.reshape(B * S, DFF)
    return (flat(res[0]), flat(res[1]), jnp.concatenate([res[2], res[3]], axis=1),
            jnp.concatenate([res[4], res[5]], axis=1))


def _dh_cat(dq, dk, dv, dag, tm):
    T, AW = dq.shape
    CW2 = dag.shape[1]
    W = 3 * AW + CW2

    def body(dq_ref, dk_ref, dv_ref, dag_ref, dh_ref, cs_ref):
        for c, ref in enumerate((dq_ref, dk_ref, dv_ref)):
            dh_ref[:, c * AW:(c + 1) * AW] = ref[...]
        dg = dag_ref[...]
        dh_ref[:, 3 * AW:] = dg.astype(BF16)
        _accumulate(cs_ref, pl.program_id(0) == 0, jnp.sum(dg, axis=0, keepdims=True))

    row = pl.BlockSpec((tm, AW), lambda i: (i, 0))
    return pl.pallas_call(
        body, grid=(T // tm,),
        in_specs=[row] * 3 + [pl.BlockSpec((tm, CW2), lambda i: (i, 0))],
        out_specs=[pl.BlockSpec((tm, W), lambda i: (i, 0)), pl.BlockSpec((1, CW2), lambda i: (0, 0))],
        out_shape=[jax.ShapeDtypeStruct((T, W), BF16), jax.ShapeDtypeStruct((1, CW2), F32)],
        compiler_params=_params(1), name="dh_cat",
    )(dq, dk, dv, dag)


def _local_step(x, target, rel_table, w_in, b_in, conv_w, conv_b, conv_ln_g, conv_ln_b, attn_norm_g,
                conv_norm_g, w_out, ln1_g, ln1_b, w_up_sh, ffn_cw, ffn_cb, w_down, ln2_g, ln2_b):
    B, S, D = x.shape
    T = B * S
    AW = attn_norm_g.shape[-1]
    CW = conv_norm_g.shape[-1]
    H = AW // HEAD_DIM
    DFF = w_down.shape[0]
    INW = 3 * AW + 2 * CW
    xf = x.reshape(T, D)
    tf = target.reshape(T, D)
    tm = _row_tile(T, 512)
    tm_s = _row_tile(T, 256)

    bucket_np, mask_np = _bucket_tables()
    bucket = jnp.asarray(bucket_np)
    band_mask = jnp.asarray(mask_np)
    bias_all = _bias_build(rel_table.T, bucket, band_mask).reshape(3, H, ATTN_BLOCK, 2 * ATTN_BLOCK)

    tn_qkv = _col_tile(3 * AW, 1152)
    qkv = _mm_plain(xf, w_in[:, :3 * AW], mode="nn", tm=tm, tn=tn_qkv, tk=D, out_dtype=BF16,
                    bias=b_in[:, :3 * AW], name="mm_qkv")
    ag = _mm_plain(xf, w_in[:, 3 * AW:], mode="nn", tm=tm, tn=2 * CW, tk=D, out_dtype=F32,
                   bias=b_in[:, 3 * AW:], name="mm_ag")

    attn, lse = _attention_fwd(qkv, bias_all, B, S, AW)
    mixed_a, r_attn = _attn_norm(attn, attn_norm_g, tm_s)
    mixed_c = _conv_fwd(ag, conv_w, conv_b, conv_ln_g, conv_ln_b, conv_norm_g, B, S, CW)
    mixed = jnp.concatenate([mixed_a, mixed_c], axis=1)

    def ln1_epilogue(acc, i, j, extra_refs, out_refs):
        x_ref, g_ref, b_ref = extra_refs
        x1, xh, r = _ln_fwd(acc + ALPHA * x_ref[...], g_ref[...], b_ref[...])
        out_refs[0][...] = x1
        out_refs[1][...] = x1.astype(BF16)
        out_refs[2][...] = xh
        out_refs[3][...] = jnp.broadcast_to(r, (tm_s, LANES))

    rowD = lambda i, j, k: (i, 0)
    vecD = lambda i, j, k: (0, 0)
    x1, x1b, xh1, r1 = _matmul(
        mixed, w_out, mode="nn", tm=tm_s, tn=D, tk=D,
        extras=[(xf, (tm_s, D), rowD), (ln1_g, (1, D), vecD), (ln1_b, (1, D), vecD)],
        outs=[((T, D), F32, (tm_s, D), rowD), ((T, D), BF16, (tm_s, D), rowD), ((T, D), F32, (tm_s, D), rowD),
              ((T, LANES), F32, (tm_s, LANES), rowD)],
        epilogue=ln1_epilogue, name="mm_out_ln1")

    NS, _, cs = w_up_sh.shape
    half = NS // 2

    def up_epilogue(acc, i, j, extra_refs, out_refs):
        out_refs[0][...] = acc

    upre = _matmul_general(
        [(x1b, (tm, D), lambda i, j, k: (i, 0)), (w_up_sh, (1, D, cs), lambda i, j, k: (j, 0, 0))],
        lambda refs, i, j, k: _dot(refs[0][...], refs[1][0], "nn"),
        grid=(T // tm, NS, 1), tm=tm, tn=cs, outs=[_plain_out(T, 2 * DFF, tm, cs, F32)],
        epilogue=up_epilogue, name="mm_up")[0]
    act = _ffn_act(upre, ffn_cw, ffn_cb, B, S, DFF)

    def ln2_epilogue(acc, i, j, extra_refs, out_refs):
        x1_ref, g_ref, b_ref, t_ref = extra_refs
        dz_ref, dzb_ref, loss_ref, dg_ref, db_ref = out_refs
        g = g_ref[...]
        y, xh, r = _ln_fwd(acc + ALPHA * x1_ref[...], g, b_ref[...])
        diff = y - t_ref[...]
        row_loss = jnp.sum(diff * diff, axis=1, keepdims=True)
        tile_loss = jnp.sum(row_loss, axis=0, keepdims=True) * (0.5 / D)
        dy = diff * (1.0 / D)
        dz = _ln_bwd(dy, xh, r, g)
        dz_ref[...] = dz
        dzb_ref[...] = dz.astype(BF16)
        first = i == 0
        _accumulate(loss_ref, first, jnp.broadcast_to(tile_loss, (1, LANES)))
        _accumulate(dg_ref, first, jnp.sum(dy * xh, axis=0, keepdims=True))
        _accumulate(db_ref, first, jnp.sum(dy, axis=0, keepdims=True))

    dz2, dz2b, loss_part, d_ln2_g, d_ln2_b = _matmul(
        act, w_down, mode="nn", tm=tm_s, tn=D, tk=DFF,
        extras=[(x1, (tm_s, D), rowD), (ln2_g, (1, D), vecD), (ln2_b, (1, D), vecD), (tf, (tm_s, D), rowD)],
        outs=[((T, D), F32, (tm_s, D), rowD), ((T, D), BF16, (tm_s, D), rowD),
              ((1, LANES), F32, (1, LANES), vecD), ((1, D), F32, (1, D), vecD), ((1, D), F32, (1, D), vecD)],
        epilogue=ln2_epilogue, name="mm_down_ln2_loss")

    tn_dff = _col_tile(DFF, 1408)
    dact = _mm_plain(dz2b, w_down, mode="nt", tm=tm, tn=tn_dff, tk=D, out_dtype=F32, name="mm_dact")
    dupre_g, dupre_v, d_ffn_cw, d_ffn_cb = _ffn_bwd(upre, dact, ffn_cw, ffn_cb, B, S, DFF)
    tk_t = _row_tile(T, 512)
    d_w_down = _mm_plain(act, dz2b, mode="tn", tm=tn_dff, tn=D, tk=tk_t, out_dtype=F32, name="mm_dw_down")

    def dw_up_epilogue(acc, i, j, extra_refs, out_refs):
        out_refs[0][0] = acc

    d_w_up_sh = _matmul_general(
        [(x1b, (tk_t, D), lambda i, j, k: (k, 0)),
         (dupre_g, (tk_t, cs), lambda i, j, k: (jnp.where(j < half, k, 0), jnp.minimum(j, half - 1))),
         (dupre_v, (tk_t, cs), lambda i, j, k: (jnp.where(j < half, 0, k), jnp.maximum(j - half, 0)))],
        lambda refs, i, j, k: _dot(refs[0][...], jnp.where(j < half, refs[1][...], refs[2][...]), "tn"),
        grid=(1, NS, T // tk_t), tm=D, tn=cs,
        outs=[((NS, D, cs), F32, (1, D, cs), lambda i, j, k: (j, 0, 0))],
        epilogue=dw_up_epilogue, name="mm_dw_up")[0]

    def ln1_bwd_epilogue(acc, i, j, extra_refs, out_refs):
        dz2_ref, xh_ref, r_ref, g_ref = extra_refs
        dz_ref, dzb_ref, dg_ref, db_ref = out_refs
        dx1 = acc + ALPHA * dz2_ref[...]
        xh = xh_ref[...]
        dz = _ln_bwd(dx1, xh, r_ref[:, 0:1], g_ref[...])
        dz_ref[...] = dz
        dzb_ref[...] = dz.astype(BF16)
        first = i == 0
        _accumulate(dg_ref, first, jnp.sum(dx1 * xh, axis=0, keepdims=True))
        _accumulate(db_ref, first, jnp.sum(dx1, axis=0, keepdims=True))

    dz1, dz1b, d_ln1_g, d_ln1_b = _matmul_general(
        [(dupre_g, (tm_s, cs), lambda i, j, k: (i, jnp.minimum(k, half - 1))),
         (dupre_v, (tm_s, cs), lambda i, j, k: (i, jnp.maximum(k - half, 0))),
         (w_up_sh, (1, D, cs), lambda i, j, k: (k, 0, 0))],
        lambda refs, i, j, k: _dot(jnp.where(k < half, refs[0][...], refs[1][...]), refs[2][0], "nt"),
        grid=(T // tm_s, 1, NS), tm=tm_s, tn=D,
        extras=[(dz2, (tm_s, D), rowD), (xh1, (tm_s, D), rowD), (r1, (tm_s, LANES), rowD), (ln1_g, (1, D), vecD)],
        outs=[((T, D), F32, (tm_s, D), rowD), ((T, D), BF16, (tm_s, D), rowD),
              ((1, D), F32, (1, D), vecD), ((1, D), F32, (1, D), vecD)],
        epilogue=ln1_bwd_epilogue, name="mm_dx1_ln1_bwd")

    dmixed = _mm_plain(dz1b, w_out, mode="nt", tm=tm, tn=D, tk=D, out_dtype=F32, name="mm_dmixed")
    d_w_out = _mm_plain(mixed, dz1b, mode="tn", tm=D, tn=D, tk=tk_t, out_dtype=F32, name="mm_dw_out")

    dattn, dd, d_attn_norm_g = _attn_pre_bwd(dmixed, attn, r_attn, attn_norm_g, tm_s)
    dag, d_conv_w, d_conv_b, d_conv_ln_g, d_conv_ln_b, d_conv_norm_g = _conv_bwd(
        ag, dmixed, conv_w, conv_b, conv_ln_g, conv_ln_b, conv_norm_g, B, S, CW, D)

    dq, dk, dv, csq, csk, csv, dbias = _attention_bwd(qkv, dattn, lse, dd, bias_all, B, S, AW)
    d_rel_table = _rel_grad(dbias.reshape(3, H, ATTN_BLOCK * 2 * ATTN_BLOCK), bucket).T
    dh, cs_ag = _dh_cat(dq, dk, dv, dag, tm_s)
    d_b_in = jnp.concatenate([csq, csk, csv, cs_ag], axis=1)

    def gx_epilogue(acc, i, j, extra_refs, out_refs):
        out_refs[0][...] = acc + ALPHA * extra_refs[0][...]

    grad_x = _matmul(dh, w_in, mode="nt", tm=tm_s, tn=D, tk=INW,
                     extras=[(dz1, (tm_s, D), rowD)], outs=[((T, D), F32, (tm_s, D), rowD)],
                     epilogue=gx_epilogue, name="mm_grad_x")[0]
    d_w_in = _mm_plain(xf, dh, mode="tn", tm=D, tn=_col_tile(INW, 1408), tk=tk_t, out_dtype=F32, name="mm_dw_in")

    grads = dict(rel_table=d_rel_table, w_in=d_w_in, b_in=d_b_in, conv_w=d_conv_w, conv_b=d_conv_b,
                 conv_ln_g=d_conv_ln_g, conv_ln_b=d_conv_ln_b, attn_norm_g=d_attn_norm_g,
                 conv_norm_g=d_conv_norm_g, w_out=d_w_out, ln1_g=d_ln1_g, ln1_b=d_ln1_b, w_up_sh=d_w_up_sh,
                 ffn_conv_w=d_ffn_cw, ffn_conv_b=d_ffn_cb, w_down=d_w_down,
                 ln2_g=d_ln2_g, ln2_b=d_ln2_b)
    return loss_part, grad_x.reshape(B, S, D), grads


def _place():
    return lax.axis_index("x"), lax.axis_index("y"), lax.axis_index("c")


CHIP_FLIPS = ((1, 0), (0, 1), (1, 1))


def _flip(v, f):
    return 1 - v if f else v


HBM_SPEC = pl.BlockSpec(memory_space=pl.ANY)
VMEM_SPEC = pl.BlockSpec(memory_space=pltpu.VMEM)
COMM_PARAMS = pltpu.CompilerParams(vmem_limit_bytes=VMEM_LIMIT)


def _gather_weights(big, small):
    nb, ns = len(big), len(small)

    def body(*refs):
        big_in = refs[:nb]
        small_in = refs[nb:nb + ns]
        big_out = refs[nb + ns:2 * nb + ns]
        small_out = refs[2 * nb + ns:2 * nb + 2 * ns]
        stages = refs[2 * nb + 2 * ns:3 * nb + 2 * ns]
        send_sems, recv_sems, local_sems = refs[3 * nb + 2 * ns:]
        x, y, c = _place()
        s_me = 2 * x + y
        sibling = (x, y, 1 - c)
        started, local_copies = [], []
        for a in range(nb):
            rh = big[a].shape[0] // 2
            lo = pl.multiple_of(c * rh, 16)
            stages[a][...] = big_in[a][pl.ds(lo, rh), :].astype(BF16)
            mine = big_out[a].at[s_me, pl.ds(lo, rh), :]
            loc = pltpu.make_async_copy(stages[a], mine, local_sems.at[a])
            loc.start()
            local_copies.append(loc)
            targets = [sibling] + [(_flip(x, fx), _flip(y, fy), c) for fx, fy in CHIP_FLIPS]
            for k, to in enumerate(targets):
                cp = pltpu.make_async_remote_copy(stages[a], mine, send_sems.at[a * 7 + k],
                                                  recv_sems.at[a * 7 + k], device_id=to, device_id_type=MESH)
                cp.start()
                started.append(cp)
        for a in range(ns):
            mine = small_out[a].at[s_me]
            loc = pltpu.make_async_copy(small_in[a], mine, local_sems.at[nb + a])
            loc.start()
            local_copies.append(loc)
            for k, (fx, fy) in enumerate(CHIP_FLIPS):
                cp = pltpu.make_async_remote_copy(small_in[a], mine, send_sems.at[nb * 7 + a * 3 + k],
                                                  recv_sems.at[nb * 7 + a * 3 + k],
                                                  device_id=(_flip(x, fx), _flip(y, fy), c), device_id_type=MESH)
                cp.start()
                started.append(cp)
        for a in range(nb):
            rh = big[a].shape[0] // 2
            lo = pl.multiple_of(c * rh, 16)
            for k, (fx, fy) in enumerate(CHIP_FLIPS):
                s_from = 2 * _flip(x, fx) + _flip(y, fy)
                got = big_out[a].at[s_from, pl.ds(lo, rh), :]
                pltpu.make_async_remote_copy(got, got, send_sems.at[a * 7 + 1 + k], recv_sems.at[a * 7 + 1 + k],
                                             device_id=sibling, device_id_type=MESH).wait_recv()
                fwd = pltpu.make_async_remote_copy(got, got, send_sems.at[a * 7 + 4 + k],
                                                   recv_sems.at[a * 7 + 4 + k], device_id=sibling,
                                                   device_id_type=MESH)
                fwd.start()
                started.append(fwd)
        for a in range(nb):
            rh = big[a].shape[0] // 2
            lo_sib = pl.multiple_of((1 - c) * rh, 16)
            for k in (0, 4, 5, 6):
                any_rows = big_out[a].at[s_me, pl.ds(lo_sib, rh), :]
                pltpu.make_async_remote_copy(any_rows, any_rows, send_sems.at[a * 7 + k], recv_sems.at[a * 7 + k],
                                             device_id=sibling, device_id_type=MESH).wait_recv()
        for a in range(ns):
            for k in range(3):
                pltpu.make_async_remote_copy(small_in[a], small_out[a].at[s_me], send_sems.at[nb * 7 + a * 3 + k],
                                             recv_sems.at[nb * 7 + a * 3 + k], device_id=sibling,
                                             device_id_type=MESH).wait_recv()
        for cp in started:
            cp.wait_send()
        for cp in local_copies:
            cp.wait()

    n_sem = nb * 7 + ns * 3
    out_shape = ([jax.ShapeDtypeStruct((N_SHARDS,) + w.shape, BF16) for w in big]
                 + [jax.ShapeDtypeStruct((N_SHARDS,) + w.shape, F32) for w in small])
    res = pl.pallas_call(
        body, in_specs=[VMEM_SPEC] * nb + [HBM_SPEC] * ns, out_specs=[HBM_SPEC] * (nb + ns),
        out_shape=out_shape,
        scratch_shapes=[pltpu.VMEM((w.shape[0] // 2, w.shape[1]), BF16) for w in big]
        + [pltpu.SemaphoreType.DMA((n_sem,)), pltpu.SemaphoreType.DMA((n_sem,)),
           pltpu.SemaphoreType.DMA((nb + ns,))],
        compiler_params=COMM_PARAMS, name="gather_weights",
    )(*big, *small)
    return res[:nb], res[nb:]


def _sibling_exchange(grads):
    n = len(grads)

    def body(*refs):
        g_in = refs[:n]
        got = refs[n:2 * n]
        send_sems, recv_sems = refs[2 * n:]
        x, y, c = _place()
        cps = []
        for a in range(n):
            rh = grads[a].shape[1] // 2
            lo = pl.multiple_of((1 - c) * rh, 8)
            cp = pltpu.make_async_remote_copy(g_in[a].at[:, pl.ds(lo, rh), :], got[a], send_sems.at[a],
                                              recv_sems.at[a], device_id=(x, y, 1 - c), device_id_type=MESH)
            cp.start()
            cps.append(cp)
        for cp in cps:
            cp.wait()

    return pl.pallas_call(
        body, in_specs=[HBM_SPEC] * n, out_specs=[HBM_SPEC] * n,
        out_shape=[jax.ShapeDtypeStruct((N_SHARDS, g.shape[1] // 2, g.shape[2]), F32) for g in grads],
        scratch_shapes=[pltpu.SemaphoreType.DMA((n,)), pltpu.SemaphoreType.DMA((n,))],
        compiler_params=COMM_PARAMS, name="sibling_exchange",
    )(*grads)


def _chip_exchange(chip_parts, pack):
    n = len(chip_parts)

    def body(*refs):
        parts = refs[:n]
        pack_ref = refs[n]
        got = refs[n + 1:2 * n + 1]
        all_packs = refs[2 * n + 1]
        send_sems, recv_sems, local_sem = refs[2 * n + 2:]
        x, y, c = _place()
        me = 4 * x + 2 * y + c
        cps = []
        for a in range(n):
            for k, (fx, fy) in enumerate(CHIP_FLIPS):
                px, py = _flip(x, fx), _flip(y, fy)
                cp = pltpu.make_async_remote_copy(parts[a].at[2 * px + py], got[a].at[k], send_sems.at[a * 3 + k],
                                                  recv_sems.at[a * 3 + k], device_id=(px, py, c),
                                                  device_id_type=MESH)
                cp.start()
                cps.append(cp)
        loc = pltpu.make_async_copy(pack_ref, all_packs.at[me], local_sem)
        loc.start()
        for m in range(1, N_DEV):
            to = (_flip(x, m & 4), _flip(y, m & 2), _flip(c, m & 1))
            cp = pltpu.make_async_remote_copy(pack_ref, all_packs.at[me], send_sems.at[n * 3 + m - 1],
                                              recv_sems.at[n * 3 + m - 1], device_id=to, device_id_type=MESH)
            cp.start()
            cps.append(cp)
        for cp in cps:
            cp.wait()
        loc.wait()

    rs = pack.shape[0]
    res = pl.pallas_call(
        body, in_specs=[HBM_SPEC] * (n + 1), out_specs=[HBM_SPEC] * (n + 1),
        out_shape=[jax.ShapeDtypeStruct((3,) + p.shape[1:], BF16) for p in chip_parts]
        + [jax.ShapeDtypeStruct((N_DEV, rs, LANES), F32)],
        scratch_shapes=[pltpu.SemaphoreType.DMA((n * 3 + N_DEV - 1,)), pltpu.SemaphoreType.DMA((n * 3 + N_DEV - 1,)),
                        pltpu.SemaphoreType.DMA],
        compiler_params=COMM_PARAMS, name="chip_exchange",
    )(*chip_parts, pack)
    return res[:n], res[n]


def _sibling_assemble(fulls):
    n = len(fulls)

    def body(*refs):
        full = refs[n:2 * n]
        send_sems, recv_sems = refs[2 * n:]
        x, y, c = _place()
        cps = []
        for a in range(n):
            rh = fulls[a].shape[0] // 2
            mine = full[a].at[pl.ds(pl.multiple_of(c * rh, 8), rh), :]
            cp = pltpu.make_async_remote_copy(mine, mine, send_sems.at[a], recv_sems.at[a],
                                              device_id=(x, y, 1 - c), device_id_type=MESH)
            cp.start()
            cps.append(cp)
        for cp in cps:
            cp.wait()

    return pl.pallas_call(
        body, in_specs=[HBM_SPEC] * n, out_specs=[HBM_SPEC] * n,
        out_shape=[jax.ShapeDtypeStruct(f.shape, F32) for f in fulls],
        input_output_aliases={a: a for a in range(n)},
        scratch_shapes=[pltpu.SemaphoreType.DMA((n,)), pltpu.SemaphoreType.DMA((n,))],
        compiler_params=COMM_PARAMS, name="sibling_assemble",
    )(*fulls)


def _half_tile(rh, mult=16, want=256):
    best = None
    for t in range(mult, min(rh, want) + 1, mult):
        if rh % t == 0:
            best = t
    return best if best is not None else rh


def _pair_sum(g, sib, ids, name):
    _, R, C = g.shape
    rh = R // 2
    rt = _half_tile(rh)
    nt = rh // rt

    def body(ids_ref, g_ref, s_ref, o_ref):
        o_ref[...] = (g_ref[...] + s_ref[...]).astype(BF16)

    grid_spec = pltpu.PrefetchScalarGridSpec(
        num_scalar_prefetch=1, grid=(N_SHARDS, nt),
        in_specs=[pl.BlockSpec((1, rt, C), lambda s, i, ids: (s, ids[2] * nt + i, 0)),
                  pl.BlockSpec((1, rt, C), lambda s, i, ids: (s, i, 0))],
        out_specs=pl.BlockSpec((1, rt, C), lambda s, i, ids: (s, i, 0)))
    return pl.pallas_call(body, grid_spec=grid_spec, out_shape=jax.ShapeDtypeStruct((N_SHARDS, rh, C), BF16),
                          compiler_params=_params(2), name=name)(ids, g, sib)


def _final_sum(g, sib, got, ids, name):
    _, R, C = g.shape
    rh = R // 2
    rt = _half_tile(rh)
    nt = rh // rt

    def body(ids_ref, g_ref, s_ref, r_ref, o_ref):
        tot = g_ref[0] + s_ref[0]
        for k in range(3):
            tot = tot + r_ref[k].astype(F32)
        o_ref[...] = tot

    grid_spec = pltpu.PrefetchScalarGridSpec(
        num_scalar_prefetch=1, grid=(nt,),
        in_specs=[pl.BlockSpec((1, rt, C), lambda i, ids: (2 * ids[0] + ids[1], ids[2] * nt + i, 0)),
                  pl.BlockSpec((1, rt, C), lambda i, ids: (2 * ids[0] + ids[1], i, 0)),
                  pl.BlockSpec((3, rt, C), lambda i, ids: (0, i, 0))],
        out_specs=pl.BlockSpec((rt, C), lambda i, ids: (ids[2] * nt + i, 0)))
    return pl.pallas_call(body, grid_spec=grid_spec, out_shape=jax.ShapeDtypeStruct((R, C), F32),
                          compiler_params=_params(1), name=name)(ids, g, sib, got)


def _sum_packs(all_packs):
    def body(p_ref, o_ref):
        tot = p_ref[0]
        for i in range(1, N_DEV):
            tot = tot + p_ref[i]
        o_ref[...] = tot

    return pl.pallas_call(body, in_specs=[VMEM_SPEC], out_specs=VMEM_SPEC,
                          out_shape=jax.ShapeDtypeStruct(all_packs.shape[1:], F32), name="sum_packs")(all_packs)


def _adamw(w, g, m, v, name):
    R, C = w.shape
    rt = _half_tile(R, mult=8, want=256)

    def body(w_ref, g_ref, m_ref, v_ref, d_ref, nm_ref, nv_ref):
        gg = g_ref[...]
        nm = ADAM_B1 * m_ref[...] + (1.0 - ADAM_B1) * gg
        nv = ADAM_B2 * v_ref[...] + (1.0 - ADAM_B2) * (gg * gg)
        m_hat = nm / (1.0 - ADAM_B1 ** ADAM_STEP)
        v_hat = nv / (1.0 - ADAM_B2 ** ADAM_STEP)
        d_ref[...] = -ADAM_LR * (m_hat / (jnp.sqrt(v_hat) + ADAM_EPS) + ADAM_WD * w_ref[...])
        nm_ref[...] = nm
        nv_ref[...] = nv

    spec = pl.BlockSpec((rt, C), lambda i: (i, 0))
    return pl.pallas_call(body, grid=(R // rt,), in_specs=[spec] * 4, out_specs=[spec] * 3,
                          out_shape=[jax.ShapeDtypeStruct((R, C), F32)] * 3,
                          compiler_params=_params(1), name=name)(w, g, m, v)


def _pack(pieces):
    rows = []
    for p in pieces:
        flat = p.reshape(-1)
        pad = (-flat.shape[0]) % LANES
        if pad:
            flat = jnp.concatenate([flat, jnp.zeros((pad,), F32)])
        rows.append(flat.reshape(-1, LANES))
    total = sum(r.shape[0] for r in rows)
    pad_rows = (-total) % 8
    if pad_rows:
        rows.append(jnp.zeros((pad_rows, LANES), F32))
    return jnp.concatenate(rows, axis=0)


def _unpack(buf, shapes):
    out, r0 = [], 0
    for shp in shapes:
        n = int(np.prod(shp))
        nr = -(-n // LANES)
        out.append(buf[r0:r0 + nr].reshape(-1)[:n].reshape(shp))
        r0 += nr
    return out


SMALL_NAMES = ("rel_table", "b_in", "conv_w", "conv_b", "conv_ln_g", "conv_ln_b", "attn_norm_g", "conv_norm_g",
               "ln1_g", "ln1_b", "ffn_conv_w", "ffn_conv_b", "ln2_g", "ln2_b")
BIG_NAMES = ("w_in", "w_out", "w_up", "w_down")
WEIGHT_ORDER = ("rel_table", "w_in", "b_in", "conv_w", "conv_b", "conv_ln_g", "conv_ln_b", "attn_norm_g",
                "conv_norm_g", "w_out", "ln1_g", "ln1_b", "w_up", "ffn_conv_w", "ffn_conv_b", "w_down",
                "ln2_g", "ln2_b")


def kernel(x, rel_table, w_in, b_in, conv_w, conv_b, conv_ln_g, conv_ln_b, attn_norm_g, conv_norm_g, w_out, ln1_g, ln1_b, w_up, ffn_conv_w, ffn_conv_b, w_down, ln2_g, ln2_b, loss_target, m_rel_table, m_w_in, m_b_in, m_conv_w, m_conv_b, m_conv_ln_g, m_conv_ln_b, m_attn_norm_g, m_conv_norm_g, m_w_out, m_ln1_g, m_ln1_b, m_w_up, m_ffn_conv_w, m_ffn_conv_b, m_w_down, m_ln2_g, m_ln2_b, v_rel_table, v_w_in, v_b_in, v_conv_w, v_conv_b, v_conv_ln_g, v_conv_ln_b, v_attn_norm_g, v_conv_norm_g, v_w_out, v_ln1_g, v_ln1_b, v_w_up, v_ffn_conv_w, v_ffn_conv_b, v_w_down, v_ln2_g, v_ln2_b):
    args = dict(locals())
    weights = {n: args[n] for n in WEIGHT_ORDER}
    moms = {n: args["m_" + n] for n in WEIGHT_ORDER}
    vels = {n: args["v_" + n] for n in WEIGHT_ORDER}
    xi, yi, ci = _place()
    ids = jnp.stack([xi, yi, ci]).astype(jnp.int32)
    shard = 2 * xi + yi
    D = x.shape[-1]
    DFF = w_down.shape[1] * N_SHARDS
    CW = conv_norm_g.shape[-1]

    (g_in, g_out, g_up, g_down), (g_cw, g_fcw) = _gather_weights(
        [w_in[0], w_out[0], w_up[0], w_down[0]], [conv_w[0], ffn_conv_w[0]])
    cols = lambda t: jnp.transpose(t, (1, 0, 2)).reshape(t.shape[1], N_SHARDS * t.shape[2])
    w_in_f = cols(g_in)
    w_out_f = g_out.reshape(D, D)
    w_down_f = g_down.reshape(DFF, D)
    conv_w_f = cols(g_cw)
    ffn_cw_f = cols(g_fcw)

    loss_part, grad_x, gl = _local_step(
        x, loss_target, rel_table, w_in_f, b_in, conv_w_f, conv_b, conv_ln_g, conv_ln_b, attn_norm_g, conv_norm_g,
        w_out_f, ln1_g, ln1_b, g_up, ffn_cw_f, ffn_conv_b, w_down_f, ln2_g, ln2_b)

    rows = lambda t: jnp.transpose(t.reshape(t.shape[0], N_SHARDS, t.shape[1] // N_SHARDS), (1, 0, 2))
    big_parts = [rows(gl["w_in"]), gl["w_out"].reshape(N_SHARDS, D // N_SHARDS, D),
                 gl["w_up_sh"], gl["w_down"].reshape(N_SHARDS, DFF // N_SHARDS, D)]
    sib = _sibling_exchange(big_parts)
    chip_parts = [_pair_sum(g, s, ids, name="pair_sum_" + n) for g, s, n in zip(big_parts, sib, BIG_NAMES)]

    pack = _pack([loss_part] + [gl[n] for n in SMALL_NAMES])
    got, all_packs = _chip_exchange(chip_parts, pack)
    fulls = [_final_sum(g, s, r, ids, name="final_sum_" + n)
             for g, s, r, n in zip(big_parts, sib, got, BIG_NAMES)]
    big_grads = dict(zip(BIG_NAMES, _sibling_assemble(fulls)))

    summed = _sum_packs(all_packs)
    full_shapes = {n: weights[n].shape for n in SMALL_NAMES}
    full_shapes["conv_w"] = (1, CONV_KERNEL, CW)
    full_shapes["ffn_conv_w"] = (1, FFN_CONV_KERNEL, 2 * DFF)
    un = _unpack(summed, [(1, LANES)] + [full_shapes[n] for n in SMALL_NAMES])
    loss = un[0][0, 0]
    small_grads = dict(zip(SMALL_NAMES, un[1:]))
    for n in ("conv_w", "ffn_conv_w"):
        width = weights[n].shape[-1]
        small_grads[n] = lax.dynamic_slice_in_dim(small_grads[n], shard * width, width, axis=2)

    grads, delta, new_m, new_v = {}, {}, {}, {}
    for n in BIG_NAMES:
        shp = weights[n].shape
        g2 = big_grads[n]
        d, nm, nv = _adamw(weights[n][0], g2, moms[n][0], vels[n][0], name="adamw_" + n)
        grads[n], delta[n], new_m[n], new_v[n] = (t.reshape(shp) for t in (g2, d, nm, nv))
    sp = lambda src: _pack([src[n] for n in SMALL_NAMES])
    d_s, nm_s, nv_s = _adamw(sp(weights), sp(small_grads), sp(moms), sp(vels), name="adamw_small")
    shapes = [weights[n].shape for n in SMALL_NAMES]
    for tgt, buf in ((delta, d_s), (new_m, nm_s), (new_v, nv_s)):
        tgt.update(zip(SMALL_NAMES, _unpack(buf, shapes)))
    grads.update(small_grads)

    return (loss, grad_x, *[grads[n] for n in WEIGHT_ORDER], *[delta[n] for n in WEIGHT_ORDER],
            *[new_m[n] for n in WEIGHT_ORDER], *[new_v[n] for n in WEIGHT_ORDER])
```

```python
import functools
import math

import numpy as np
import jax
import jax.numpy as jnp
from jax import lax
from jax.experimental import pallas as pl
from jax.experimental.pallas import tpu as pltpu

F32 = jnp.float32
BF16 = jnp.bfloat16
MESH = pl.DeviceIdType.MESH

HEAD_DIM = 64
LANES = 128
ATTN_BLOCK = 128
DILATED_CONFIGS = ((128, 1), (512, 4), (2048, 16))
CONV_KERNEL = 31
FFN_CONV_KERNEL = 3
REL_BUCKETS = 32
REL_MAX_DIST = 2048
DEPTH = 1
ALPHA = (2 * DEPTH) ** 0.25
LN_EPS = 1e-5
NEG_INF = -1e30
QK_SCALE = 1.0 / math.sqrt(HEAD_DIM)
ADAM_LR = 0.001
ADAM_B1 = 0.9
ADAM_B2 = 0.999
ADAM_EPS = 1e-08
ADAM_WD = 0.01
ADAM_STEP = 10
VMEM_LIMIT = 52 * 1024 * 1024
FFN_COLS = 256
N_SHARDS = 4
N_DEV = 8


def _params(n_axes):
    return pltpu.CompilerParams(dimension_semantics=("arbitrary",) * n_axes,
                                vmem_limit_bytes=VMEM_LIMIT)


MM_DIMS = {"nn": (((1,), (0,)), ((), ())), "nt": (((1,), (1,)), ((), ())), "tn": (((0,), (0,)), ((), ()))}


def _matmul_general(ins, part_fn, *, grid, tm, tn, outs, epilogue, extras=(), name):
    nk = grid[2]
    n_in, n_extra = len(ins), len(extras)

    def body(*refs):
        in_refs = refs[:n_in]
        rest = refs[n_in:]
        extra_refs = rest[:n_extra]
        out_refs = rest[n_extra:n_extra + len(outs)]
        acc_ref = rest[-1]
        i, j, k = pl.program_id(0), pl.program_id(1), pl.program_id(2)
        part = part_fn(in_refs, i, j, k)
        if nk == 1:
            epilogue(part, i, j, extra_refs, out_refs)
        else:
            @pl.when(k == 0)
            def _():
                acc_ref[...] = part

            @pl.when(k > 0)
            def _():
                acc_ref[...] += part

            @pl.when(k == nk - 1)
            def _():
                epilogue(acc_ref[...], i, j, extra_refs, out_refs)

    in_specs = [pl.BlockSpec(bs, im) for (_, bs, im) in list(ins) + list(extras)]
    out_specs = [pl.BlockSpec(bs, im) for (_, _, bs, im) in outs]
    out_shape = [jax.ShapeDtypeStruct(s, d) for (s, d, _, _) in outs]
    return pl.pallas_call(
        body, grid=grid, in_specs=in_specs, out_specs=out_specs,
        out_shape=out_shape, scratch_shapes=[pltpu.VMEM((tm, tn), F32)],
        compiler_params=_params(3), name=name,
    )(*[e[0] for e in ins], *[e[0] for e in extras])


def _dot(a, b, mode):
    return lax.dot_general(a.astype(BF16), b.astype(BF16), MM_DIMS[mode], preferred_element_type=F32)


def _matmul(a, b, *, mode, tm, tn, tk, outs, epilogue, extras=(), name):
    if mode == "tn":
        K, M = a.shape
        N = b.shape[1]
        ins = [(a, (tk, tm), lambda i, j, k: (k, i)), (b, (tk, tn), lambda i, j, k: (k, j))]
    elif mode == "nt":
        M, K = a.shape
        N = b.shape[0]
        ins = [(a, (tm, tk), lambda i, j, k: (i, k)), (b, (tn, tk), lambda i, j, k: (j, k))]
    else:
        M, K = a.shape
        N = b.shape[1]
        ins = [(a, (tm, tk), lambda i, j, k: (i, k)), (b, (tk, tn), lambda i, j, k: (k, j))]
    assert M % tm == 0 and N % tn == 0 and K % tk == 0, (name, M, N, K, tm, tn, tk)

    def part_fn(in_refs, i, j, k):
        return _dot(in_refs[0][...], in_refs[1][...], mode)

    return _matmul_general(ins, part_fn, grid=(M // tm, N // tn, K // tk), tm=tm, tn=tn, outs=outs,
                           epilogue=epilogue, extras=extras, name=name)


def _plain_out(M, N, tm, tn, dtype):
    return ((M, N), dtype, (tm, tn), lambda i, j, k: (i, j))


def _mm_plain(a, b, *, mode, tm, tn, tk, out_dtype, name, bias=None):
    if mode == "tn":
        M, N = a.shape[1], b.shape[1]
    elif mode == "nt":
        M, N = a.shape[0], b.shape[0]
    else:
        M, N = a.shape[0], b.shape[1]
    extras = []
    if bias is not None:
        extras.append((bias, (1, tn), lambda i, j, k: (0, j)))

    def epilogue(acc, i, j, extra_refs, out_refs):
        if bias is not None:
            acc = acc + extra_refs[0][...]
        out_refs[0][...] = acc.astype(out_dtype)

    return _matmul(a, b, mode=mode, tm=tm, tn=tn, tk=tk, outs=[_plain_out(M, N, tm, tn, out_dtype)],
                   epilogue=epilogue, extras=extras, name=name)[0]


def _row_tile(T, want):
    t = min(T, want)
    while T % t:
        t //= 2
    return t


def _col_tile(N, want):
    if N <= want:
        return N
    best = None
    for c in range(LANES, want + 1, LANES):
        if N % c == 0:
            best = c
    return best if best is not None else N


def _accumulate(ref, first, val):
    @pl.when(first)
    def _():
        ref[...] = val

    @pl.when(jnp.logical_not(first))
    def _():
        ref[...] += val


def _ln_fwd(z, g, b):
    mu = jnp.mean(z, axis=-1, keepdims=True)
    zc = z - mu
    var = jnp.mean(zc * zc, axis=-1, keepdims=True)
    r = lax.rsqrt(var + LN_EPS)
    xh = zc * r
    return xh * g + b, xh, r


def _ln_bwd(dy, xh, r, g):
    dxh = dy * g
    m1 = jnp.mean(dxh, axis=-1, keepdims=True)
    m2 = jnp.mean(dxh * xh, axis=-1, keepdims=True)
    return r * (dxh - m1 - xh * m2)


def _sigmoid(x):
    return 1.0 / (1.0 + jnp.exp(-x))


def _shift_down(x, s, row):
    if s == 0:
        return x
    return jnp.where(row >= s, pltpu.roll(x, s, 0), 0.0)


def _shift_up(x, s, row):
    if s == 0:
        return x
    n = x.shape[0]
    return jnp.where(row < n - s, pltpu.roll(x, n - s, 0), 0.0)


def _bucket_tables():
    exact = REL_BUCKETS // 2
    qi = np.arange(ATTN_BLOCK)[:, None]
    kj = np.arange(2 * ATTN_BLOCK)[None, :]
    steps = qi + ATTN_BLOCK - kj
    buckets, masks = [], []
    for window, dilation in DILATED_CONFIGS:
        max_steps = window // dilation
        band = (steps >= 0) & (steps <= max_steps)
        dist = np.maximum(steps, 0) * dilation
        d_f = np.maximum(dist, 1).astype(np.float32)
        large = exact + (np.log(d_f / np.float32(exact)) / np.float32(math.log(REL_MAX_DIST / exact))
                         * np.float32(REL_BUCKETS - exact)).astype(np.int32)
        large = np.minimum(large, REL_BUCKETS - 1)
        bucket = np.where(dist < exact, dist, large).astype(np.int32)
        buckets.append(bucket.reshape(1, -1))
        masks.append(np.where(band, 0.0, NEG_INF).astype(np.float32).reshape(1, -1))
    return np.stack(buckets), np.stack(masks)


def _split_hi_lo(x):
    hi = x.astype(BF16)
    lo = (x - hi.astype(F32)).astype(BF16)
    return hi, lo


def _bias_build(rel_table_t, bucket, mask):
    H = rel_table_t.shape[0]
    n = bucket.shape[-1]

    def body(t_ref, bkt_ref, mask_ref, o_ref):
        onehot = (lax.broadcasted_iota(jnp.int32, (REL_BUCKETS, n), 0) == bkt_ref[0]).astype(BF16)
        t = t_ref[...]
        t1 = t.astype(BF16)
        r1 = t - t1.astype(F32)
        t2 = r1.astype(BF16)
        t3 = (r1 - t2.astype(F32)).astype(BF16)
        acc = jnp.dot(t1, onehot, preferred_element_type=F32)
        acc = acc + jnp.dot(t2, onehot, preferred_element_type=F32)
        acc = acc + jnp.dot(t3, onehot, preferred_element_type=F32)
        o_ref[0] = acc + mask_ref[0]

    return pl.pallas_call(
        body, grid=(3,),
        in_specs=[pl.BlockSpec((H, REL_BUCKETS), lambda b: (0, 0)),
                  pl.BlockSpec((1, 1, n), lambda b: (b, 0, 0)),
                  pl.BlockSpec((1, 1, n), lambda b: (b, 0, 0))],
        out_specs=pl.BlockSpec((1, H, n), lambda b: (b, 0, 0)),
        out_shape=jax.ShapeDtypeStruct((3, H, n), F32),
        compiler_params=_params(1), name="bias_build",
    )(rel_table_t, bucket, mask)


def _rel_grad(dbias, bucket):
    H = dbias.shape[1]
    n = bucket.shape[-1]
    dims = (((1,), (1,)), ((), ()))

    def body(d_ref, bkt_ref, o_ref):
        b = pl.program_id(0)
        onehot = (lax.broadcasted_iota(jnp.int32, (REL_BUCKETS, n), 0) == bkt_ref[0]).astype(BF16)
        d = d_ref[0]
        d1 = d.astype(BF16)
        r1 = d - d1.astype(F32)
        d2 = r1.astype(BF16)
        d3 = (r1 - d2.astype(F32)).astype(BF16)
        acc = lax.dot_general(d1, onehot, dims, preferred_element_type=F32)
        acc = acc + lax.dot_general(d2, onehot, dims, preferred_element_type=F32)
        acc = acc + lax.dot_general(d3, onehot, dims, preferred_element_type=F32)
        _accumulate(o_ref, b == 0, acc)

    return pl.pallas_call(
        body, grid=(3,),
        in_specs=[pl.BlockSpec((1, H, n), lambda b: (b, 0, 0)),
                  pl.BlockSpec((1, 1, n), lambda b: (b, 0, 0))],
        out_specs=pl.BlockSpec((H, REL_BUCKETS), lambda b: (0, 0)),
        out_shape=jax.ShapeDtypeStruct((H, REL_BUCKETS), F32),
        compiler_params=_params(1), name="rel_grad",
    )(dbias, bucket)


def _attn_specs(B, S, AW, d):
    L = S // d
    HP = AW // LANES
    W3 = 3 * HP
    q_spec = pl.BlockSpec((1, L, LANES), lambda h, b, r: (b, 0, r * W3 + h))
    k_spec = pl.BlockSpec((1, L, LANES), lambda h, b, r: (b, 0, r * W3 + HP + h))
    v_spec = pl.BlockSpec((1, L, LANES), lambda h, b, r: (b, 0, r * W3 + 2 * HP + h))
    o_spec = pl.BlockSpec((1, L, LANES), lambda h, b, r: (b, 0, r * HP + h))
    bias_spec = pl.BlockSpec((2, ATTN_BLOCK, 2 * ATTN_BLOCK), lambda h, b, r: (h, 0, 0))
    return L, HP, q_spec, k_spec, v_spec, o_spec, bias_spec


def _attn_fwd(qkv, bias, B, S, AW, d, name):
    L, HP, q_spec, k_spec, v_spec, o_spec, bias_spec = _attn_specs(B, S, AW, d)
    nb = L // ATTN_BLOCK
    nt = (((1,), (1,)), ((), ()))

    def body(q_ref, k_ref, v_ref, b_ref, o_ref, lse_ref):
        head0 = lax.broadcasted_iota(jnp.int32, (1, LANES), 1) < HEAD_DIM

        def block(n, first):
            qs = pl.multiple_of(n * ATTN_BLOCK, ATTN_BLOCK)
            q = q_ref[0, pl.ds(qs, ATTN_BLOCK), :]
            if first:
                kk = k_ref[0, pl.ds(0, ATTN_BLOCK), :]
                vv = v_ref[0, pl.ds(0, ATTN_BLOCK), :]
            else:
                ks = pl.multiple_of(n * ATTN_BLOCK - ATTN_BLOCK, ATTN_BLOCK)
                kk = k_ref[0, pl.ds(ks, 2 * ATTN_BLOCK), :]
                vv = v_ref[0, pl.ds(ks, 2 * ATTN_BLOCK), :]
            outs, lses = [], []
            for e in range(2):
                msk = head0 if e == 0 else jnp.logical_not(head0)
                qe = jnp.where(msk, q, jnp.zeros_like(q))
                s = lax.dot_general(qe, kk, nt, preferred_element_type=F32) * QK_SCALE
                s = s + (b_ref[e, :, ATTN_BLOCK:] if first else b_ref[e])
                m = jnp.max(s, axis=-1, keepdims=True)
                p = jnp.exp(s - m)
                l = jnp.sum(p, axis=-1, keepdims=True)
                o = jnp.dot(p.astype(BF16), vv, preferred_element_type=F32)
                outs.append(o / l)
                lses.append(jnp.broadcast_to(m + jnp.log(l), (ATTN_BLOCK, LANES)))
            o_ref[0, pl.ds(qs, ATTN_BLOCK), :] = jnp.where(head0, outs[0], outs[1])
            lse_ref[0, pl.ds(qs, ATTN_BLOCK), :] = jnp.where(head0, lses[0], lses[1])

        block(0, True)
        if nb > 1:
            def loop(n, c):
                block(n, False)
                return c
            lax.fori_loop(1, nb, loop, 0)

    qv = qkv.reshape(B, L, d * 3 * AW)
    o, lse = pl.pallas_call(
        body, grid=(HP, B, d), in_specs=[q_spec, k_spec, v_spec, bias_spec],
        out_specs=[o_spec, o_spec],
        out_shape=[jax.ShapeDtypeStruct((B, L, d * AW), F32)] * 2,
        compiler_params=_params(3), name=name,
    )(qv, qv, qv, bias)
    return o.reshape(B * S, AW), lse.reshape(B * S, AW)


def _attn_bwd(qkv, do, lse, dd, bias, B, S, AW, d, name):
    L, HP, q_spec, k_spec, v_spec, o_spec, bias_spec = _attn_specs(B, S, AW, d)
    nb = L // ATTN_BLOCK
    nt = (((1,), (1,)), ((), ()))
    tn = (((0,), (0,)), ((), ()))

    def body(q_ref, k_ref, v_ref, do_ref, lse_ref, dd_ref, b_ref, dq_ref, dk_ref, dv_ref, db_ref):
        head0 = lax.broadcasted_iota(jnp.int32, (1, LANES), 1) < HEAD_DIM
        first_step = jnp.logical_and(pl.program_id(1) == 0, pl.program_id(2) == 0)

        @pl.when(first_step)
        def _():
            db_ref[...] = jnp.zeros_like(db_ref)

        dk_ref[...] = jnp.zeros_like(dk_ref)
        dv_ref[...] = jnp.zeros_like(dv_ref)

        def block(n, first):
            qs = pl.multiple_of(n * ATTN_BLOCK, ATTN_BLOCK)
            nkeys = ATTN_BLOCK if first else 2 * ATTN_BLOCK
            ks = 0 if first else pl.multiple_of(n * ATTN_BLOCK - ATTN_BLOCK, ATTN_BLOCK)
            q = q_ref[0, pl.ds(qs, ATTN_BLOCK), :]
            kk = k_ref[0, pl.ds(ks, nkeys), :]
            vv = v_ref[0, pl.ds(ks, nkeys), :]
            dout = do_ref[0, pl.ds(qs, ATTN_BLOCK), :]
            lse_b = lse_ref[0, pl.ds(qs, ATTN_BLOCK), :]
            dd_b = dd_ref[0, pl.ds(qs, ATTN_BLOCK), :]
            dq = jnp.zeros((ATTN_BLOCK, LANES), F32)
            dkk = jnp.zeros((nkeys, LANES), F32)
            dvv = jnp.zeros((nkeys, LANES), F32)
            for e in range(2):
                msk = head0 if e == 0 else jnp.logical_not(head0)
                c0 = e * HEAD_DIM
                qe = jnp.where(msk, q, jnp.zeros_like(q))
                doe = jnp.where(msk, dout, jnp.zeros_like(dout))
                kke = jnp.where(msk, kk, jnp.zeros_like(kk))
                s = lax.dot_general(qe, kk, nt, preferred_element_type=F32) * QK_SCALE
                s = s + (b_ref[e, :, ATTN_BLOCK:] if first else b_ref[e])
                p = jnp.exp(s - lse_b[:, c0:c0 + 1])
                dp = lax.dot_general(doe, vv, nt, preferred_element_type=F32)
                ds = p * (dp - dd_b[:, c0:c0 + 1])
                if first:
                    db_ref[e, :, ATTN_BLOCK:] += ds
                else:
                    db_ref[e] += ds
                dsb = (ds * QK_SCALE).astype(BF16)
                dq = dq + jnp.dot(dsb, kke, preferred_element_type=F32)
                dkk = dkk + lax.dot_general(dsb, qe, tn, preferred_element_type=F32)
                dvv = dvv + lax.dot_general(p.astype(BF16), doe, tn, preferred_element_type=F32)
            dq_ref[0, pl.ds(qs, ATTN_BLOCK), :] = dq
            dk_ref[0, pl.ds(ks, nkeys), :] += dkk
            dv_ref[0, pl.ds(ks, nkeys), :] += dvv

        block(0, True)
        if nb > 1:
            def loop(n, c):
                block(n, False)
                return c
            lax.fori_loop(1, nb, loop, 0)

    H = AW // HEAD_DIM
    qv = qkv.reshape(B, L, d * 3 * AW)
    view = lambda t: t.reshape(B, L, d * AW)
    dq, dk, dv, db = pl.pallas_call(
        body, grid=(HP, B, d),
        in_specs=[q_spec, k_spec, v_spec, o_spec, o_spec, o_spec, bias_spec],
        out_specs=[o_spec, o_spec, o_spec, bias_spec],
        out_shape=[jax.ShapeDtypeStruct((B, L, d * AW), F32)] * 3
        + [jax.ShapeDtypeStruct((H, ATTN_BLOCK, 2 * ATTN_BLOCK), F32)],
        compiler_params=_params(3), name=name,
    )(qv, qv, qv, view(do), view(lse), view(dd), bias)
    flat = lambda t: t.reshape(B * S, AW)
    return flat(dq), flat(dk), flat(dv), db


def _attn_combine(ons, lses, gain, tm):
    T, AW = ons[0].shape

    def body(o1, o2, o3, l1, l2, l3, g_ref, attn_ref, lse_ref, mix_ref, r_ref):
        la, lb, lc = l1[...], l2[...], l3[...]
        m = jnp.maximum(jnp.maximum(la, lb), lc)
        ea, eb, ec = jnp.exp(la - m), jnp.exp(lb - m), jnp.exp(lc - m)
        den = ea + eb + ec
        attn = (ea * o1[...] + eb * o2[...] + ec * o3[...]) / den
        attn_ref[...] = attn
        lse_ref[...] = m + jnp.log(den)
        r = lax.rsqrt(jnp.mean(attn * attn, axis=-1, keepdims=True) + LN_EPS)
        mix_ref[...] = (attn * r * g_ref[...]).astype(BF16)
        r_ref[...] = jnp.broadcast_to(r, (tm, LANES))

    row = pl.BlockSpec((tm, AW), lambda i: (i, 0))
    return pl.pallas_call(
        body, grid=(T // tm,),
        in_specs=[row] * 6 + [pl.BlockSpec((1, AW), lambda i: (0, 0))],
        out_specs=[row, row, row, pl.BlockSpec((tm, LANES), lambda i: (i, 0))],
        out_shape=[jax.ShapeDtypeStruct((T, AW), F32), jax.ShapeDtypeStruct((T, AW), F32),
                   jax.ShapeDtypeStruct((T, AW), BF16), jax.ShapeDtypeStruct((T, LANES), F32)],
        compiler_params=_params(1), name="attn_combine",
    )(*ons, *lses, gain)


def _to_sub(src_ref, stage_ref, dsts, S):
    stage_ref[...] = src_ref[0].astype(F32)
    for (_, d), dst in zip(DILATED_CONFIGS[1:], dsts):
        L = S // d
        for r in range(d):
            dst[r * L:(r + 1) * L, :] = stage_ref[pl.ds(r, L, stride=d), :].astype(dst.dtype)


def _branch_blocks(S, d, block):
    nb = S // d // ATTN_BLOCK
    inner_unroll = 3 if (nb - 1) % 3 == 0 else 1

    def per_residue(r, c):
        block(r * nb, True)
        if nb > 1:
            def inner(n, c2):
                block(r * nb + n, False)
                return c2
            lax.fori_loop(1, nb, inner, 0, unroll=inner_unroll)
        return c

    lax.fori_loop(0, d, per_residue, 0, unroll=4 if nb == 1 else 1)


def _attention_fwd(qkv, bias_all, B, S, AW):
    HP = AW // LANES
    nt = MM_DIMS["nt"]

    def body(q_ref, k_ref, v_ref, b_ref, o_ref, lse_ref, stage, q4, q16, k4, k16, v4, v16, o1, l1, o4, l4, o16, l16):
        head0 = lax.broadcasted_iota(jnp.int32, (1, LANES), 1) < HEAD_DIM
        _to_sub(q_ref, stage, (q4, q16), S)
        _to_sub(k_ref, stage, (k4, k16), S)
        _to_sub(v_ref, stage, (v4, v16), S)
        srcs = ((q_ref.at[0], k_ref.at[0], v_ref.at[0], o1, l1), (q4, k4, v4, o4, l4), (q16, k16, v16, o16, l16))
        for bi, (_, d) in enumerate(DILATED_CONFIGS):
            qs_ref, ks_ref, vs_ref, od_ref, ld_ref = srcs[bi]

            def block(g, first, bi=bi, qs_ref=qs_ref, ks_ref=ks_ref, vs_ref=vs_ref, od_ref=od_ref, ld_ref=ld_ref):
                qs = pl.multiple_of(g * ATTN_BLOCK, ATTN_BLOCK)
                nkeys = ATTN_BLOCK if first else 2 * ATTN_BLOCK
                ks = qs if first else pl.multiple_of(qs - ATTN_BLOCK, ATTN_BLOCK)
                q = qs_ref[pl.ds(qs, ATTN_BLOCK), :]
                kk = ks_ref[pl.ds(ks, nkeys), :]
                vv = vs_ref[pl.ds(ks, nkeys), :]
                outs, lses = [], []
                for e in range(2):
                    msk = head0 if e == 0 else jnp.logical_not(head0)
                    qe = jnp.where(msk, q * QK_SCALE, jnp.zeros_like(q))
                    s = lax.dot_general(qe, kk, nt, preferred_element_type=F32)
                    s = s + (b_ref[bi, e, :, ATTN_BLOCK:] if first else b_ref[bi, e])
                    m = jnp.max(s, axis=-1, keepdims=True)
                    p = jnp.exp(s - m)
                    l = jnp.sum(p, axis=-1, keepdims=True)
                    o = jnp.dot(p.astype(BF16), vv, preferred_element_type=F32)
                    outs.append(o / l)
                    lses.append(jnp.broadcast_to(m + jnp.log(l), (ATTN_BLOCK, LANES)))
                od_ref[pl.ds(qs, ATTN_BLOCK), :] = jnp.where(head0, outs[0], outs[1])
                ld_ref[pl.ds(qs, ATTN_BLOCK), :] = jnp.where(head0, lses[0], lses[1])

            _branch_blocks(S, d, block)

        def natural(sub_ref, d):
            L = S // d
            for r in range(d):
                stage[pl.ds(r, L, stride=d), :] = sub_ref[r * L:(r + 1) * L, :]
            return stage[...]

        la = l1[...]
        lb = natural(l4, 4)
        lc = natural(l16, 16)
        m = jnp.maximum(jnp.maximum(la, lb), lc)
        ea, eb, ec = jnp.exp(la - m), jnp.exp(lb - m), jnp.exp(lc - m)
        den = ea + eb + ec
        lse_ref[0] = m + jnp.log(den)
        acc = ea * o1[...]
        acc = acc + eb * natural(o4, 4)
        acc = acc + ec * natural(o16, 16)
        o_ref[0] = acc / den

    blk = lambda off: pl.BlockSpec((1, S, LANES), lambda b, h: (b, 0, off + h))
    qv = qkv.reshape(B, S, 3 * AW)
    sub_b = pltpu.VMEM((S, LANES), BF16)
    sub_f = pltpu.VMEM((S, LANES), F32)
    o, lse = pl.pallas_call(
        body, grid=(B, HP),
        in_specs=[blk(0), blk(HP), blk(2 * HP),
                  pl.BlockSpec((3, 2, ATTN_BLOCK, 2 * ATTN_BLOCK), lambda b, h: (0, h, 0, 0))],
        out_specs=[blk(0), blk(0)],
        out_shape=[jax.ShapeDtypeStruct((B, S, AW), F32)] * 2,
        scratch_shapes=[sub_f] + [sub_b] * 6 + [sub_f] * 6,
        compiler_params=_params(2), name="attention_fwd",
    )(qv, qv, qv, bias_all)
    return o.reshape(B * S, AW), lse.reshape(B * S, AW)


def _attention_bwd(qkv, do, lse, dd, bias_all, B, S, AW):
    HP = AW // LANES
    H = AW // HEAD_DIM
    nt, tn = MM_DIMS["nt"], MM_DIMS["tn"]

    def body(q_ref, k_ref, v_ref, do_ref, lse_ref, dd_ref, b_ref,
             dq_ref, dk_ref, dv_ref, csq_ref, csk_ref, csv_ref, db_ref,
             stage, q4, q16, k4, k16, v4, v16, g4, g16, l4, l16, d4, d16,
             aq1, ak1, av1, aq4, ak4, av4, aq16, ak16, av16):
        head0 = lax.broadcasted_iota(jnp.int32, (1, LANES), 1) < HEAD_DIM
        first_b = pl.program_id(1) == 0

        @pl.when(first_b)
        def _():
            db_ref[...] = jnp.zeros_like(db_ref)

        _to_sub(q_ref, stage, (q4, q16), S)
        _to_sub(k_ref, stage, (k4, k16), S)
        _to_sub(v_ref, stage, (v4, v16), S)
        _to_sub(do_ref, stage, (g4, g16), S)
        _to_sub(lse_ref, stage, (l4, l16), S)
        _to_sub(dd_ref, stage, (d4, d16), S)
        for acc in (ak1, av1, ak4, av4, ak16, av16):
            acc[...] = jnp.zeros_like(acc)
        srcs = ((q_ref.at[0], k_ref.at[0], v_ref.at[0], do_ref.at[0], lse_ref.at[0], dd_ref.at[0], aq1, ak1, av1),
                (q4, k4, v4, g4, l4, d4, aq4, ak4, av4), (q16, k16, v16, g16, l16, d16, aq16, ak16, av16))
        for bi, (_, d) in enumerate(DILATED_CONFIGS):
            def block(g, first, bi=bi, refs=srcs[bi]):
                qs_ref, ks_ref, vs_ref, gs_ref, ls_ref, ds_ref, aq, ak, av = refs
                qs = pl.multiple_of(g * ATTN_BLOCK, ATTN_BLOCK)
                nkeys = ATTN_BLOCK if first else 2 * ATTN_BLOCK
                ks = qs if first else pl.multiple_of(qs - ATTN_BLOCK, ATTN_BLOCK)
                q = qs_ref[pl.ds(qs, ATTN_BLOCK), :]
                kk = ks_ref[pl.ds(ks, nkeys), :]
                vv = vs_ref[pl.ds(ks, nkeys), :]
                dout = gs_ref[pl.ds(qs, ATTN_BLOCK), :]
                lse_b = ls_ref[pl.ds(qs, ATTN_BLOCK), :]
                dd_b = ds_ref[pl.ds(qs, ATTN_BLOCK), :]
                dq = jnp.zeros((ATTN_BLOCK, LANES), F32)
                dkk = jnp.zeros((nkeys, LANES), F32)
                dvv = jnp.zeros((nkeys, LANES), F32)
                for e in range(2):
                    msk = head0 if e == 0 else jnp.logical_not(head0)
                    c0 = e * HEAD_DIM
                    qe = jnp.where(msk, q * QK_SCALE, jnp.zeros_like(q))
                    doe = jnp.where(msk, dout, jnp.zeros_like(dout))
                    kke = jnp.where(msk, kk * QK_SCALE, jnp.zeros_like(kk))
                    s = lax.dot_general(qe, kk, nt, preferred_element_type=F32)
                    s = s + (b_ref[bi, e, :, ATTN_BLOCK:] if first else b_ref[bi, e])
                    p = jnp.exp(s - lse_b[:, c0:c0 + 1])
                    dp = lax.dot_general(doe, vv, nt, preferred_element_type=F32)
                    ds = p * (dp - dd_b[:, c0:c0 + 1])
                    if first:
                        db_ref[bi, e, :, ATTN_BLOCK:] += ds
                    else:
                        db_ref[bi, e] += ds
                    dsb = ds.astype(BF16)
                    dq = dq + jnp.dot(dsb, kke, preferred_element_type=F32)
                    dkk = dkk + lax.dot_general(dsb, qe, tn, preferred_element_type=F32)
                    dvv = dvv + lax.dot_general(p.astype(BF16), doe, tn, preferred_element_type=F32)
                aq[pl.ds(qs, ATTN_BLOCK), :] = dq
                ak[pl.ds(ks, nkeys), :] += dkk
                av[pl.ds(ks, nkeys), :] += dvv

            _branch_blocks(S, d, block)

        for a1, a4, a16, out_ref, cs_ref in ((aq1, aq4, aq16, dq_ref, csq_ref), (ak1, ak4, ak16, dk_ref, csk_ref),
                                             (av1, av4, av16, dv_ref, csv_ref)):
            stage[...] = a1[...]
            for d, sub in ((4, a4), (16, a16)):
                L = S // d
                for r in range(d):
                    stage[pl.ds(r, L, stride=d), :] += sub[r * L:(r + 1) * L, :]
            tot = stage[...]
            out_ref[0] = tot.astype(out_ref.dtype)
            _accumulate(cs_ref, first_b, jnp.sum(tot, axis=0, keepdims=True))

    blk = lambda off: pl.BlockSpec((1, S, LANES), lambda h, b: (b, 0, off + h))
    cs_spec = pl.BlockSpec((1, LANES), lambda h, b: (0, h))
    bias_spec = pl.BlockSpec((3, 2, ATTN_BLOCK, 2 * ATTN_BLOCK), lambda h, b: (0, h, 0, 0))
    qv = qkv.reshape(B, S, 3 * AW)
    view = lambda t: t.reshape(B, S, AW)
    sub_b = pltpu.VMEM((S, LANES), BF16)
    sub_f = pltpu.VMEM((S, LANES), F32)
    res = pl.pallas_call(
        body, grid=(HP, B),
        in_specs=[blk(0), blk(HP), blk(2 * HP), blk(0), blk(0), blk(0), bias_spec],
        out_specs=[blk(0), blk(0), blk(0), cs_spec, cs_spec, cs_spec, bias_spec],
        out_shape=[jax.ShapeDtypeStruct((B, S, AW), BF16)] * 3 + [jax.ShapeDtypeStruct((1, AW), F32)] * 3
        + [jax.ShapeDtypeStruct((3, H, ATTN_BLOCK, 2 * ATTN_BLOCK), F32)],
        scratch_shapes=[sub_f] + [sub_b] * 8 + [sub_f] * 4 + [sub_f] * 9,
        compiler_params=_params(2), name="attention_bwd",
    )(qv, qv, qv, view(do), view(lse), view(dd), bias_all)
    flat = lambda t: t.reshape(B * S, AW)
    return flat(res[0]), flat(res[1]), flat(res[2]), res[3], res[4], res[5], res[6]


def _attn_norm(attn, gain, tm):
    T, AW = attn.shape

    def body(a_ref, g_ref, mix_ref, r_ref):
        a = a_ref[...]
        r = lax.rsqrt(jnp.mean(a * a, axis=-1, keepdims=True) + LN_EPS)
        mix_ref[...] = (a * r * g_ref[...]).astype(BF16)
        r_ref[...] = jnp.broadcast_to(r, (tm, LANES))

    row = pl.BlockSpec((tm, AW), lambda i: (i, 0))
    return pl.pallas_call(
        body, grid=(T // tm,), in_specs=[row, pl.BlockSpec((1, AW), lambda i: (0, 0))],
        out_specs=[row, pl.BlockSpec((tm, LANES), lambda i: (i, 0))],
        out_shape=[jax.ShapeDtypeStruct((T, AW), BF16), jax.ShapeDtypeStruct((T, LANES), F32)],
        compiler_params=_params(1), name="attn_norm",
    )(attn, gain)


def _attn_pre_bwd(dmixed, attn, rstd, gain, tm):
    T, AW = attn.shape
    ones_np = np.kron(np.eye(AW // HEAD_DIM, dtype=np.float32), np.ones((HEAD_DIM, HEAD_DIM), np.float32))
    ones_bd = jnp.asarray(ones_np, dtype=BF16)

    def body(dm_ref, a_ref, r_ref, g_ref, ones_ref, do_ref, dd_ref, dg_ref):
        i = pl.program_id(0)
        dm = dm_ref[...]
        a = a_ref[...]
        r = r_ref[:, 0:1]
        dxn = dm * g_ref[...]
        da = r * (dxn - a * (r * r) * jnp.mean(dxn * a, axis=-1, keepdims=True))
        do_ref[...] = da.astype(BF16)
        hi, lo = _split_hi_lo(da * a)
        dd_ref[...] = (jnp.dot(hi, ones_ref[...], preferred_element_type=F32)
                       + jnp.dot(lo, ones_ref[...], preferred_element_type=F32))
        _accumulate(dg_ref, i == 0, jnp.sum(dm * a * r, axis=0, keepdims=True))

    row = pl.BlockSpec((tm, AW), lambda i: (i, 0))
    vec = pl.BlockSpec((1, AW), lambda i: (0, 0))
    return pl.pallas_call(
        body, grid=(T // tm,),
        in_specs=[row, row, pl.BlockSpec((tm, LANES), lambda i: (i, 0)), vec,
                  pl.BlockSpec((AW, AW), lambda i: (0, 0))],
        out_specs=[row, row, vec],
        out_shape=[jax.ShapeDtypeStruct((T, AW), BF16), jax.ShapeDtypeStruct((T, AW), F32),
                   jax.ShapeDtypeStruct((1, AW), F32)],
        compiler_params=_params(1), name="attn_pre_bwd",
    )(dmixed, attn, rstd, gain, ones_bd)


def _conv_branch_fwd_math(a, g, w_ref, cb, lg, lb, row):
    sg = _sigmoid(g)
    u0 = a * sg
    uc = jnp.zeros_like(u0) + cb
    for k in range(CONV_KERNEL):
        uc = uc + w_ref[k:k + 1, :] * _shift_down(u0, CONV_KERNEL - 1 - k, row)
    ul, xh, r = _ln_fwd(uc, lg, lb)
    su = _sigmoid(ul)
    u = ul * su
    return sg, u0, ul, xh, r, su, u


def _conv_fwd(ag, conv_w, conv_b, ln_g, ln_b, norm_g, B, S, CW):
    def body(a_ref, g_ref, w_ref, cb_ref, lg_ref, lb_ref, ng_ref, o_ref):
        row = lax.broadcasted_iota(jnp.int32, (S, CW), 0)
        _, _, _, _, _, _, u = _conv_branch_fwd_math(a_ref[0], g_ref[0], w_ref, cb_ref[...], lg_ref[...],
                                                    lb_ref[...], row)
        rr = lax.rsqrt(jnp.mean(u * u, axis=-1, keepdims=True) + LN_EPS)
        o_ref[0] = (u * rr * ng_ref[...]).astype(BF16)

    vec = pl.BlockSpec((1, CW), lambda b: (0, 0))
    out = pl.pallas_call(
        body, grid=(B,),
        in_specs=[pl.BlockSpec((1, S, CW), lambda b: (b, 0, 0)), pl.BlockSpec((1, S, CW), lambda b: (b, 0, 1)),
                  pl.BlockSpec((CONV_KERNEL, CW), lambda b: (0, 0)), vec, vec, vec, vec],
        out_specs=pl.BlockSpec((1, S, CW), lambda b: (b, 0, 0)),
        out_shape=jax.ShapeDtypeStruct((B, S, CW), BF16),
        compiler_params=_params(1), name="conv_fwd",
    )(ag.reshape(B, S, 2 * CW), ag.reshape(B, S, 2 * CW), conv_w, conv_b, ln_g, ln_b, norm_g)
    return out.reshape(B * S, CW)


def _conv_bwd(ag, dmixed, conv_w, conv_b, ln_g, ln_b, norm_g, B, S, CW, D):
    AW = D - CW
    assert AW % CW == 0

    def body(a_ref, g_ref, dm_ref, w_ref, cb_ref, lg_ref, lb_ref, ng_ref,
             dag_ref, dw_ref, dcb_ref, dlg_ref, dlb_ref, dng_ref):
        b = pl.program_id(0)
        row = lax.broadcasted_iota(jnp.int32, (S, CW), 0)
        a, g = a_ref[0], g_ref[0]
        sg, u0, ul, xh, r, su, u = _conv_branch_fwd_math(a, g, w_ref, cb_ref[...], lg_ref[...], lb_ref[...], row)
        rr = lax.rsqrt(jnp.mean(u * u, axis=-1, keepdims=True) + LN_EPS)
        dm = dm_ref[0]
        dxn = dm * ng_ref[...]
        du = rr * (dxn - u * (rr * rr) * jnp.mean(dxn * u, axis=-1, keepdims=True))
        dul = du * su * (1.0 + ul * (1.0 - su))
        duc = _ln_bwd(dul, xh, r, lg_ref[...])
        first = b == 0
        _accumulate(dng_ref, first, jnp.sum(dm * u * rr, axis=0, keepdims=True))
        _accumulate(dlg_ref, first, jnp.sum(dul * xh, axis=0, keepdims=True))
        _accumulate(dlb_ref, first, jnp.sum(dul, axis=0, keepdims=True))
        _accumulate(dcb_ref, first, jnp.sum(duc, axis=0, keepdims=True))

        @pl.when(first)
        def _():
            dw_ref[...] = jnp.zeros_like(dw_ref)

        du0 = jnp.zeros_like(u0)
        for k in range(CONV_KERNEL):
            sh = CONV_KERNEL - 1 - k
            dw_ref[k:k + 1, :] += jnp.sum(duc * _shift_down(u0, sh, row), axis=0, keepdims=True)
            du0 = du0 + w_ref[k:k + 1, :] * _shift_up(duc, sh, row)
        dag_ref[0, :, :CW] = du0 * sg
        dag_ref[0, :, CW:] = du0 * a * sg * (1.0 - sg)

    vec = pl.BlockSpec((1, CW), lambda b: (0, 0))
    wspec = pl.BlockSpec((CONV_KERNEL, CW), lambda b: (0, 0))
    agv = ag.reshape(B, S, 2 * CW)
    res = pl.pallas_call(
        body, grid=(B,),
        in_specs=[pl.BlockSpec((1, S, CW), lambda b: (b, 0, 0)), pl.BlockSpec((1, S, CW), lambda b: (b, 0, 1)),
                  pl.BlockSpec((1, S, CW), lambda b: (b, 0, AW // CW)), wspec, vec, vec, vec, vec],
        out_specs=[pl.BlockSpec((1, S, 2 * CW), lambda b: (b, 0, 0)), wspec, vec, vec, vec, vec],
        out_shape=[jax.ShapeDtypeStruct((B, S, 2 * CW), F32), jax.ShapeDtypeStruct((CONV_KERNEL, CW), F32)]
        + [jax.ShapeDtypeStruct((1, CW), F32)] * 4,
        compiler_params=_params(1), name="conv_bwd",
    )(agv, agv, dmixed.reshape(B, S, D), conv_w, conv_b, ln_g, ln_b, norm_g)
    return (res[0].reshape(B * S, 2 * CW),) + tuple(res[1:])


def _ffn_conv(x, w_ref, bias, row):
    y = jnp.zeros_like(x) + bias
    for k in range(FFN_CONV_KERNEL):
        y = y + w_ref[k:k + 1, :] * _shift_down(x, FFN_CONV_KERNEL - 1 - k, row)
    return y


def _ffn_specs(S, tc, nj, order):
    pick = (lambda b, j: (b, j)) if order == "bj" else (lambda j, b: (b, j))
    act = lambda off: pl.BlockSpec((1, S, tc), lambda *g: (pick(*g)[0], 0, off + pick(*g)[1]))
    cw = lambda off: pl.BlockSpec((FFN_CONV_KERNEL, tc), lambda *g: (0, off + pick(*g)[1]))
    cb = lambda off: pl.BlockSpec((1, tc), lambda *g: (0, off + pick(*g)[1]))
    return act, cw, cb


def _ffn_act(upre, cw, cb, B, S, DFF):
    tc = FFN_COLS
    nj = DFF // tc

    def body(ug_ref, uv_ref, wg_ref, wv_ref, bg_ref, bv_ref, o_ref):
        row = lax.broadcasted_iota(jnp.int32, (S, tc), 0)
        gate = _ffn_conv(ug_ref[0], wg_ref, bg_ref[...], row)
        val = _ffn_conv(uv_ref[0], wv_ref, bv_ref[...], row)
        o_ref[0] = (gate * _sigmoid(gate) * val).astype(BF16)

    act, cws, cbs = _ffn_specs(S, tc, nj, "bj")
    uv = upre.reshape(B, S, 2 * DFF)
    out = pl.pallas_call(
        body, grid=(B, nj), in_specs=[act(0), act(nj), cws(0), cws(nj), cbs(0), cbs(nj)], out_specs=act(0),
        out_shape=jax.ShapeDtypeStruct((B, S, DFF), BF16), compiler_params=_params(2), name="ffn_act",
    )(uv, uv, cw, cw, cb, cb)
    return out.reshape(B * S, DFF)


def _ffn_bwd(upre, dact, cw, cb, B, S, DFF):
    tc = FFN_COLS
    nj = DFF // tc

    def body(ug_ref, uv_ref, da_ref, wg_ref, wv_ref, bg_ref, bv_ref, dug_ref, duv_ref, dwg_ref, dwv_ref,
             dbg_ref, dbv_ref):
        first = pl.program_id(1) == 0
        row = lax.broadcasted_iota(jnp.int32, (S, tc), 0)
        ug, uv = ug_ref[0], uv_ref[0]
        gate = _ffn_conv(ug, wg_ref, bg_ref[...], row)
        val = _ffn_conv(uv, wv_ref, bv_ref[...], row)
        sg = _sigmoid(gate)
        dact_b = da_ref[0]
        dgate = dact_b * val * sg * (1.0 + gate * (1.0 - sg))
        dval = dact_b * gate * sg
        for dup, u, w_ref, du_ref, dw_ref, db_ref in ((dgate, ug, wg_ref, dug_ref, dwg_ref, dbg_ref),
                                                      (dval, uv, wv_ref, duv_ref, dwv_ref, dbv_ref)):
            _accumulate(db_ref, first, jnp.sum(dup, axis=0, keepdims=True))

            @pl.when(first)
            def _(dw_ref=dw_ref):
                dw_ref[...] = jnp.zeros_like(dw_ref)

            dupre = jnp.zeros_like(dup)
            for k in range(FFN_CONV_KERNEL):
                sh = FFN_CONV_KERNEL - 1 - k
                dw_ref[k:k + 1, :] += jnp.sum(dup * _shift_down(u, sh, row), axis=0, keepdims=True)
                dupre = dupre + w_ref[k:k + 1, :] * _shift_up(dup, sh, row)
            du_ref[0] = dupre.astype(BF16)

    act, cws, cbs = _ffn_specs(S, tc, nj, "jb")
    uv = upre.reshape(B, S, 2 * DFF)
    res = pl.pallas_call(
        body, grid=(nj, B),
        in_specs=[act(0), act(nj), act(0), cws(0), cws(nj), cbs(0), cbs(nj)],
        out_specs=[act(0), act(0), cws(0), cws(0), cbs(0), cbs(0)],
        out_shape=[jax.ShapeDtypeStruct((B, S, DFF), BF16)] * 2
        + [jax.ShapeDtypeStruct((FFN_CONV_KERNEL, DFF), F32)] * 2 + [jax.ShapeDtypeStruct((1, DFF), F32)] * 2,
        compiler_params=_params(2), name="ffn_bwd",
    )(uv, uv, dact.reshape(B, S, DFF), cw, cw, cb, cb)
    flat = lambda t: t.reshape(B * S, DFF)
    return (flat(res[0]), flat(res[1]), jnp.concatenate([res[2], res[3]], axis=1),
            jnp.concatenate([res[4], res[5]], axis=1))


def _dh_cat(dq, dk, dv, dag, tm):
    T, AW = dq.shape
    CW2 = dag.shape[1]
    W = 3 * AW + CW2

    def body(dq_ref, dk_ref, dv_ref, dag_ref, dh_ref, cs_ref):
        for c, ref in enumerate((dq_ref, dk_ref, dv_ref)):
            dh_ref[:, c * AW:(c + 1) * AW] = ref[...]
        dg = dag_ref[...]
        dh_ref[:, 3 * AW:] = dg.astype(BF16)
        _accumulate(cs_ref, pl.program_id(0) == 0, jnp.sum(dg, axis=0, keepdims=True))

    row = pl.BlockSpec((tm, AW), lambda i: (i, 0))
    return pl.pallas_call(
        body, grid=(T // tm,),
        in_specs=[row] * 3 + [pl.BlockSpec((tm, CW2), lambda i: (i, 0))],
        out_specs=[pl.BlockSpec((tm, W), lambda i: (i, 0)), pl.BlockSpec((1, CW2), lambda i: (0, 0))],
        out_shape=[jax.ShapeDtypeStruct((T, W), BF16), jax.ShapeDtypeStruct((1, CW2), F32)],
        compiler_params=_params(1), name="dh_cat",
    )(dq, dk, dv, dag)


def _local_step(x, target, rel_table, w_in, b_in, conv_w, conv_b, conv_ln_g, conv_ln_b, attn_norm_g,
                conv_norm_g, w_out, ln1_g, ln1_b, w_up_sh, ffn_cw, ffn_cb, w_down, ln2_g, ln2_b):
    B, S, D = x.shape
    T = B * S
    AW = attn_norm_g.shape[-1]
    CW = conv_norm_g.shape[-1]
    H = AW // HEAD_DIM
    DFF = w_down.shape[0]
    INW = 3 * AW + 2 * CW
    xf = x.reshape(T, D)
    tf = target.reshape(T, D)
    tm = _row_tile(T, 512)
    tm_s = _row_tile(T, 256)

    bucket_np, mask_np = _bucket_tables()
    bucket = jnp.asarray(bucket_np)
    band_mask = jnp.asarray(mask_np)
    bias_all = _bias_build(rel_table.T, bucket, band_mask).reshape(3, H, ATTN_BLOCK, 2 * ATTN_BLOCK)

    tn_qkv = _col_tile(3 * AW, 1152)
    qkv = _mm_plain(xf, w_in[:, :3 * AW], mode="nn", tm=tm, tn=tn_qkv, tk=D, out_dtype=BF16,
                    bias=b_in[:, :3 * AW], name="mm_qkv")
    ag = _mm_plain(xf, w_in[:, 3 * AW:], mode="nn", tm=tm, tn=2 * CW, tk=D, out_dtype=F32,
                   bias=b_in[:, 3 * AW:], name="mm_ag")

    attn, lse = _attention_fwd(qkv, bias_all, B, S, AW)
    mixed_a, r_attn = _attn_norm(attn, attn_norm_g, tm_s)
    mixed_c = _conv_fwd(ag, conv_w, conv_b, conv_ln_g, conv_ln_b, conv_norm_g, B, S, CW)
    mixed = jnp.concatenate([mixed_a, mixed_c], axis=1)

    def ln1_epilogue(acc, i, j, extra_refs, out_refs):
        x_ref, g_ref, b_ref = extra_refs
        x1, xh, r = _ln_fwd(acc + ALPHA * x_ref[...], g_ref[...], b_ref[...])
        out_refs[0][...] = x1
        out_refs[1][...] = x1.astype(BF16)
        out_refs[2][...] = xh
        out_refs[3][...] = jnp.broadcast_to(r, (tm_s, LANES))

    rowD = lambda i, j, k: (i, 0)
    vecD = lambda i, j, k: (0, 0)
    x1, x1b, xh1, r1 = _matmul(
        mixed, w_out, mode="nn", tm=tm_s, tn=D, tk=D,
        extras=[(xf, (tm_s, D), rowD), (ln1_g, (1, D), vecD), (ln1_b, (1, D), vecD)],
        outs=[((T, D), F32, (tm_s, D), rowD), ((T, D), BF16, (tm_s, D), rowD), ((T, D), F32, (tm_s, D), rowD),
              ((T, LANES), F32, (tm_s, LANES), rowD)],
        epilogue=ln1_epilogue, name="mm_out_ln1")

    NS, _, cs = w_up_sh.shape
    half = NS // 2

    def up_epilogue(acc, i, j, extra_refs, out_refs):
        out_refs[0][...] = acc

    upre = _matmul_general(
        [(x1b, (tm, D), lambda i, j, k: (i, 0)), (w_up_sh, (1, D, cs), lambda i, j, k: (j, 0, 0))],
        lambda refs, i, j, k: _dot(refs[0][...], refs[1][0], "nn"),
        grid=(T // tm, NS, 1), tm=tm, tn=cs, outs=[_plain_out(T, 2 * DFF, tm, cs, F32)],
        epilogue=up_epilogue, name="mm_up")[0]
    act = _ffn_act(upre, ffn_cw, ffn_cb, B, S, DFF)

    def ln2_epilogue(acc, i, j, extra_refs, out_refs):
        x1_ref, g_ref, b_ref, t_ref = extra_refs
        dz_ref, dzb_ref, loss_ref, dg_ref, db_ref = out_refs
        g = g_ref[...]
        y, xh, r = _ln_fwd(acc + ALPHA * x1_ref[...], g, b_ref[...])
        diff = y - t_ref[...]
        row_loss = jnp.sum(diff * diff, axis=1, keepdims=True)
        tile_loss = jnp.sum(row_loss, axis=0, keepdims=True) * (0.5 / D)
        dy = diff * (1.0 / D)
        dz = _ln_bwd(dy, xh, r, g)
        dz_ref[...] = dz
        dzb_ref[...] = dz.astype(BF16)
        first = i == 0
        _accumulate(loss_ref, first, jnp.broadcast_to(tile_loss, (1, LANES)))
        _accumulate(dg_ref, first, jnp.sum(dy * xh, axis=0, keepdims=True))
        _accumulate(db_ref, first, jnp.sum(dy, axis=0, keepdims=True))

    dz2, dz2b, loss_part, d_ln2_g, d_ln2_b = _matmul(
        act, w_down, mode="nn", tm=tm_s, tn=D, tk=DFF,
        extras=[(x1, (tm_s, D), rowD), (ln2_g, (1, D), vecD), (ln2_b, (1, D), vecD), (tf, (tm_s, D), rowD)],
        outs=[((T, D), F32, (tm_s, D), rowD), ((T, D), BF16, (tm_s, D), rowD),
              ((1, LANES), F32, (1, LANES), vecD), ((1, D), F32, (1, D), vecD), ((1, D), F32, (1, D), vecD)],
        epilogue=ln2_epilogue, name="mm_down_ln2_loss")

    tn_dff = _col_tile(DFF, 1408)
    dact = _mm_plain(dz2b, w_down, mode="nt", tm=tm, tn=tn_dff, tk=D, out_dtype=F32, name="mm_dact")
    dupre_g, dupre_v, d_ffn_cw, d_ffn_cb = _ffn_bwd(upre, dact, ffn_cw, ffn_cb, B, S, DFF)
    tk_t = _row_tile(T, 512)
    d_w_down = _mm_plain(act, dz2b, mode="tn", tm=tn_dff, tn=D, tk=tk_t, out_dtype=F32, name="mm_dw_down")

    def dw_up_epilogue(acc, i, j, extra_refs, out_refs):
        out_refs[0][0] = acc

    d_w_up_sh = _matmul_general(
        [(x1b, (tk_t, D), lambda i, j, k: (k, 0)),
         (dupre_g, (tk_t, cs), lambda i, j, k: (jnp.where(j < half, k, 0), jnp.minimum(j, half - 1))),
         (dupre_v, (tk_t, cs), lambda i, j, k: (jnp.where(j < half, 0, k), jnp.maximum(j - half, 0)))],
        lambda refs, i, j, k: _dot(refs[0][...], jnp.where(j < half, refs[1][...], refs[2][...]), "tn"),
        grid=(1, NS, T // tk_t), tm=D, tn=cs,
        outs=[((NS, D, cs), F32, (1, D, cs), lambda i, j, k: (j, 0, 0))],
        epilogue=dw_up_epilogue, name="mm_dw_up")[0]

    def ln1_bwd_epilogue(acc, i, j, extra_refs, out_refs):
        dz2_ref, xh_ref, r_ref, g_ref = extra_refs
        dz_ref, dzb_ref, dg_ref, db_ref = out_refs
        dx1 = acc + ALPHA * dz2_ref[...]
        xh = xh_ref[...]
        dz = _ln_bwd(dx1, xh, r_ref[:, 0:1], g_ref[...])
        dz_ref[...] = dz
        dzb_ref[...] = dz.astype(BF16)
        first = i == 0
        _accumulate(dg_ref, first, jnp.sum(dx1 * xh, axis=0, keepdims=True))
        _accumulate(db_ref, first, jnp.sum(dx1, axis=0, keepdims=True))

    dz1, dz1b, d_ln1_g, d_ln1_b = _matmul_general(
        [(dupre_g, (tm_s, cs), lambda i, j, k: (i, jnp.minimum(k, half - 1))),
         (dupre_v, (tm_s, cs), lambda i, j, k: (i, jnp.maximum(k - half, 0))),
         (w_up_sh, (1, D, cs), lambda i, j, k: (k, 0, 0))],
        lambda refs, i, j, k: _dot(jnp.where(k < half, refs[0][...], refs[1][...]), refs[2][0], "nt"),
        grid=(T // tm_s, 1, NS), tm=tm_s, tn=D,
        extras=[(dz2, (tm_s, D), rowD), (xh1, (tm_s, D), rowD), (r1, (tm_s, LANES), rowD), (ln1_g, (1, D), vecD)],
        outs=[((T, D), F32, (tm_s, D), rowD), ((T, D), BF16, (tm_s, D), rowD),
              ((1, D), F32, (1, D), vecD), ((1, D), F32, (1, D), vecD)],
        epilogue=ln1_bwd_epilogue, name="mm_dx1_ln1_bwd")

    dmixed = _mm_plain(dz1b, w_out, mode="nt", tm=tm, tn=D, tk=D, out_dtype=F32, name="mm_dmixed")
    d_w_out = _mm_plain(mixed, dz1b, mode="tn", tm=D, tn=D, tk=tk_t, out_dtype=F32, name="mm_dw_out")

    dattn, dd, d_attn_norm_g = _attn_pre_bwd(dmixed, attn, r_attn, attn_norm_g, tm_s)
    dag, d_conv_w, d_conv_b, d_conv_ln_g, d_conv_ln_b, d_conv_norm_g = _conv_bwd(
        ag, dmixed, conv_w, conv_b, conv_ln_g, conv_ln_b, conv_norm_g, B, S, CW, D)

    dq, dk, dv, csq, csk, csv, dbias = _attention_bwd(qkv, dattn, lse, dd, bias_all, B, S, AW)
    d_rel_table = _rel_grad(dbias.reshape(3, H, ATTN_BLOCK * 2 * ATTN_BLOCK), bucket).T
    dh, cs_ag = _dh_cat(dq, dk, dv, dag, tm_s)
    d_b_in = jnp.concatenate([csq, csk, csv, cs_ag], axis=1)

    def gx_epilogue(acc, i, j, extra_refs, out_refs):
        out_refs[0][...] = acc + ALPHA * extra_refs[0][...]

    grad_x = _matmul(dh, w_in, mode="nt", tm=tm_s, tn=D, tk=INW,
                     extras=[(dz1, (tm_s, D), rowD)], outs=[((T, D), F32, (tm_s, D), rowD)],
                     epilogue=gx_epilogue, name="mm_grad_x")[0]
    d_w_in = _mm_plain(xf, dh, mode="tn", tm=D, tn=_col_tile(INW, 1408), tk=tk_t, out_dtype=F32, name="mm_dw_in")

    grads = dict(rel_table=d_rel_table, w_in=d_w_in, b_in=d_b_in, conv_w=d_conv_w, conv_b=d_conv_b,
                 conv_ln_g=d_conv_ln_g, conv_ln_b=d_conv_ln_b, attn_norm_g=d_attn_norm_g,
                 conv_norm_g=d_conv_norm_g, w_out=d_w_out, ln1_g=d_ln1_g, ln1_b=d_ln1_b, w_up_sh=d_w_up_sh,
                 ffn_conv_w=d_ffn_cw, ffn_conv_b=d_ffn_cb, w_down=d_w_down,
                 ln2_g=d_ln2_g, ln2_b=d_ln2_b)
    return loss_part, grad_x.reshape(B, S, D), grads


def _place():
    return lax.axis_index("x"), lax.axis_index("y"), lax.axis_index("c")


CHIP_FLIPS = ((1, 0), (0, 1), (1, 1))


def _flip(v, f):
    return 1 - v if f else v


HBM_SPEC = pl.BlockSpec(memory_space=pl.ANY)
VMEM_SPEC = pl.BlockSpec(memory_space=pltpu.VMEM)
COMM_PARAMS = pltpu.CompilerParams(vmem_limit_bytes=VMEM_LIMIT)


def _gather_weights(big, small):
    nb, ns = len(big), len(small)

    def body(*refs):
        big_in = refs[:nb]
        small_in = refs[nb:nb + ns]
        big_out = refs[nb + ns:2 * nb + ns]
        small_out = refs[2 * nb + ns:2 * nb + 2 * ns]
        stages = refs[2 * nb + 2 * ns:3 * nb + 2 * ns]
        send_sems, recv_sems, local_sems = refs[3 * nb + 2 * ns:]
        x, y, c = _place()
        s_me = 2 * x + y
        sibling = (x, y, 1 - c)
        started, local_copies = [], []
        for a in range(nb):
            rh = big[a].shape[0] // 2
            lo = pl.multiple_of(c * rh, 16)
            stages[a][...] = big_in[a][pl.ds(lo, rh), :].astype(BF16)
            mine = big_out[a].at[s_me, pl.ds(lo, rh), :]
            loc = pltpu.make_async_copy(stages[a], mine, local_sems.at[a])
            loc.start()
            local_copies.append(loc)
            targets = [sibling] + [(_flip(x, fx), _flip(y, fy), c) for fx, fy in CHIP_FLIPS]
            for k, to in enumerate(targets):
                cp = pltpu.make_async_remote_copy(stages[a], mine, send_sems.at[a * 7 + k],
                                                  recv_sems.at[a * 7 + k], device_id=to, device_id_type=MESH)
                cp.start()
                started.append(cp)
        for a in range(ns):
            mine = small_out[a].at[s_me]
            loc = pltpu.make_async_copy(small_in[a], mine, local_sems.at[nb + a])
            loc.start()
            local_copies.append(loc)
            for k, (fx, fy) in enumerate(CHIP_FLIPS):
                cp = pltpu.make_async_remote_copy(small_in[a], mine, send_sems.at[nb * 7 + a * 3 + k],
                                                  recv_sems.at[nb * 7 + a * 3 + k],
                                                  device_id=(_flip(x, fx), _flip(y, fy), c), device_id_type=MESH)
                cp.start()
                started.append(cp)
        for a in range(nb):
            rh = big[a].shape[0] // 2
            lo = pl.multiple_of(c * rh, 16)
            for k, (fx, fy) in enumerate(CHIP_FLIPS):
                s_from = 2 * _flip(x, fx) + _flip(y, fy)
                got = big_out[a].at[s_from, pl.ds(lo, rh), :]
                pltpu.make_async_remote_copy(got, got, send_sems.at[a * 7 + 1 + k], recv_sems.at[a * 7 + 1 + k],
                                             device_id=sibling, device_id_type=MESH).wait_recv()
                fwd = pltpu.make_async_remote_copy(got, got, send_sems.at[a * 7 + 4 + k],
                                                   recv_sems.at[a * 7 + 4 + k], device_id=sibling,
                                                   device_id_type=MESH)
                fwd.start()
                started.append(fwd)
        for a in range(nb):
            rh = big[a].shape[0] // 2
            lo_sib = pl.multiple_of((1 - c) * rh, 16)
            for k in (0, 4, 5, 6):
                any_rows = big_out[a].at[s_me, pl.ds(lo_sib, rh), :]
                pltpu.make_async_remote_copy(any_rows, any_rows, send_sems.at[a * 7 + k], recv_sems.at[a * 7 + k],
                                             device_id=sibling, device_id_type=MESH).wait_recv()
        for a in range(ns):
            for k in range(3):
                pltpu.make_async_remote_copy(small_in[a], small_out[a].at[s_me], send_sems.at[nb * 7 + a * 3 + k],
                                             recv_sems.at[nb * 7 + a * 3 + k], device_id=sibling,
                                             device_id_type=MESH).wait_recv()
        for cp in started:
            cp.wait_send()
        for cp in local_copies:
            cp.wait()

    n_sem = nb * 7 + ns * 3
    out_shape = ([jax.ShapeDtypeStruct((N_SHARDS,) + w.shape, BF16) for w in big]
                 + [jax.ShapeDtypeStruct((N_SHARDS,) + w.shape, F32) for w in small])
    res = pl.pallas_call(
        body, in_specs=[VMEM_SPEC] * nb + [HBM_SPEC] * ns, out_specs=[HBM_SPEC] * (nb + ns),
        out_shape=out_shape,
        scratch_shapes=[pltpu.VMEM((w.shape[0] // 2, w.shape[1]), BF16) for w in big]
        + [pltpu.SemaphoreType.DMA((n_sem,)), pltpu.SemaphoreType.DMA((n_sem,)),
           pltpu.SemaphoreType.DMA((nb + ns,))],
        compiler_params=COMM_PARAMS, name="gather_weights",
    )(*big, *small)
    return res[:nb], res[nb:]


def _sibling_exchange(grads):
    n = len(grads)

    def body(*refs):
        g_in = refs[:n]
        got = refs[n:2 * n]
        send_sems, recv_sems = refs[2 * n:]
        x, y, c = _place()
        cps = []
        for a in range(n):
            rh = grads[a].shape[1] // 2
            lo = pl.multiple_of((1 - c) * rh, 8)
            cp = pltpu.make_async_remote_copy(g_in[a].at[:, pl.ds(lo, rh), :], got[a], send_sems.at[a],
                                              recv_sems.at[a], device_id=(x, y, 1 - c), device_id_type=MESH)
            cp.start()
            cps.append(cp)
        for cp in cps:
            cp.wait()

    return pl.pallas_call(
        body, in_specs=[HBM_SPEC] * n, out_specs=[HBM_SPEC] * n,
        out_shape=[jax.ShapeDtypeStruct((N_SHARDS, g.shape[1] // 2, g.shape[2]), F32) for g in grads],
        scratch_shapes=[pltpu.SemaphoreType.DMA((n,)), pltpu.SemaphoreType.DMA((n,))],
        compiler_params=COMM_PARAMS, name="sibling_exchange",
    )(*grads)


def _chip_exchange(chip_parts, pack):
    n = len(chip_parts)

    def body(*refs):
        parts = refs[:n]
        pack_ref = refs[n]
        got = refs[n + 1:2 * n + 1]
        all_packs = refs[2 * n + 1]
        send_sems, recv_sems, local_sem = refs[2 * n + 2:]
        x, y, c = _place()
        me = 4 * x + 2 * y + c
        cps = []
        for a in range(n):
            for k, (fx, fy) in enumerate(CHIP_FLIPS):
                px, py = _flip(x, fx), _flip(y, fy)
                cp = pltpu.make_async_remote_copy(parts[a].at[2 * px + py], got[a].at[k], send_sems.at[a * 3 + k],
                                                  recv_sems.at[a * 3 + k], device_id=(px, py, c),
                                                  device_id_type=MESH)
                cp.start()
                cps.append(cp)
        loc = pltpu.make_async_copy(pack_ref, all_packs.at[me], local_sem)
        loc.start()
        for m in range(1, N_DEV):
            to = (_flip(x, m & 4), _flip(y, m & 2), _flip(c, m & 1))
            cp = pltpu.make_async_remote_copy(pack_ref, all_packs.at[me], send_sems.at[n * 3 + m - 1],
                                              recv_sems.at[n * 3 + m - 1], device_id=to, device_id_type=MESH)
            cp.start()
            cps.append(cp)
        for cp in cps:
            cp.wait()
        loc.wait()

    rs = pack.shape[0]
    res = pl.pallas_call(
        body, in_specs=[HBM_SPEC] * (n + 1), out_specs=[HBM_SPEC] * (n + 1),
        out_shape=[jax.ShapeDtypeStruct((3,) + p.shape[1:], BF16) for p in chip_parts]
        + [jax.ShapeDtypeStruct((N_DEV, rs, LANES), F32)],
        scratch_shapes=[pltpu.SemaphoreType.DMA((n * 3 + N_DEV - 1,)), pltpu.SemaphoreType.DMA((n * 3 + N_DEV - 1,)),
                        pltpu.SemaphoreType.DMA],
        compiler_params=COMM_PARAMS, name="chip_exchange",
    )(*chip_parts, pack)
    return res[:n], res[n]


def _sibling_assemble(fulls):
    n = len(fulls)

    def body(*refs):
        full = refs[n:2 * n]
        send_sems, recv_sems = refs[2 * n:]
        x, y, c = _place()
        cps = []
        for a in range(n):
            rh = fulls[a].shape[0] // 2
            mine = full[a].at[pl.ds(pl.multiple_of(c * rh, 8), rh), :]
            cp = pltpu.make_async_remote_copy(mine, mine, send_sems.at[a], recv_sems.at[a],
                                              device_id=(x, y, 1 - c), device_id_type=MESH)
            cp.start()
            cps.append(cp)
        for cp in cps:
            cp.wait()

    return pl.pallas_call(
        body, in_specs=[HBM_SPEC] * n, out_specs=[HBM_SPEC] * n,
        out_shape=[jax.ShapeDtypeStruct(f.shape, F32) for f in fulls],
        input_output_aliases={a: a for a in range(n)},
        scratch_shapes=[pltpu.SemaphoreType.DMA((n,)), pltpu.SemaphoreType.DMA((n,))],
        compiler_params=COMM_PARAMS, name="sibling_assemble",
    )(*fulls)


def _half_tile(rh, mult=16, want=256):
    best = None
    for t in range(mult, min(rh, want) + 1, mult):
        if rh % t == 0:
            best = t
    return best if best is not None else rh


def _pair_sum(g, sib, ids, name):
    _, R, C = g.shape
    rh = R // 2
    rt = _half_tile(rh)
    nt = rh // rt

    def body(ids_ref, g_ref, s_ref, o_ref):
        o_ref[...] = (g_ref[...] + s_ref[...]).astype(BF16)

    grid_spec = pltpu.PrefetchScalarGridSpec(
        num_scalar_prefetch=1, grid=(N_SHARDS, nt),
        in_specs=[pl.BlockSpec((1, rt, C), lambda s, i, ids: (s, ids[2] * nt + i, 0)),
                  pl.BlockSpec((1, rt, C), lambda s, i, ids: (s, i, 0))],
        out_specs=pl.BlockSpec((1, rt, C), lambda s, i, ids: (s, i, 0)))
    return pl.pallas_call(body, grid_spec=grid_spec, out_shape=jax.ShapeDtypeStruct((N_SHARDS, rh, C), BF16),
                          compiler_params=_params(2), name=name)(ids, g, sib)


def _final_sum(g, sib, got, ids, name):
    _, R, C = g.shape
    rh = R // 2
    rt = _half_tile(rh)
    nt = rh // rt

    def body(ids_ref, g_ref, s_ref, r_ref, o_ref):
        tot = g_ref[0] + s_ref[0]
        for k in range(3):
            tot = tot + r_ref[k].astype(F32)
        o_ref[...] = tot

    grid_spec = pltpu.PrefetchScalarGridSpec(
        num_scalar_prefetch=1, grid=(nt,),
        in_specs=[pl.BlockSpec((1, rt, C), lambda i, ids: (2 * ids[0] + ids[1], ids[2] * nt + i, 0)),
                  pl.BlockSpec((1, rt, C), lambda i, ids: (2 * ids[0] + ids[1], i, 0)),
                  pl.BlockSpec((3, rt, C), lambda i, ids: (0, i, 0))],
        out_specs=pl.BlockSpec((rt, C), lambda i, ids: (ids[2] * nt + i, 0)))
    return pl.pallas_call(body, grid_spec=grid_spec, out_shape=jax.ShapeDtypeStruct((R, C), F32),
                          compiler_params=_params(1), name=name)(ids, g, sib, got)


def _sum_packs(all_packs):
    def body(p_ref, o_ref):
        tot = p_ref[0]
        for i in range(1, N_DEV):
            tot = tot + p_ref[i]
        o_ref[...] = tot

    return pl.pallas_call(body, in_specs=[VMEM_SPEC], out_specs=VMEM_SPEC,
                          out_shape=jax.ShapeDtypeStruct(all_packs.shape[1:], F32), name="sum_packs")(all_packs)


def _adamw(w, g, m, v, name):
    R, C = w.shape
    rt = _half_tile(R, mult=8, want=256)

    def body(w_ref, g_ref, m_ref, v_ref, d_ref, nm_ref, nv_ref):
        gg = g_ref[...]
        nm = ADAM_B1 * m_ref[...] + (1.0 - ADAM_B1) * gg
        nv = ADAM_B2 * v_ref[...] + (1.0 - ADAM_B2) * (gg * gg)
        m_hat = nm / (1.0 - ADAM_B1 ** ADAM_STEP)
        v_hat = nv / (1.0 - ADAM_B2 ** ADAM_STEP)
        d_ref[...] = -ADAM_LR * (m_hat / (jnp.sqrt(v_hat) + ADAM_EPS) + ADAM_WD * w_ref[...])
        nm_ref[...] = nm
        nv_ref[...] = nv

    spec = pl.BlockSpec((rt, C), lambda i: (i, 0))
    return pl.pallas_call(body, grid=(R // rt,), in_specs=[spec] * 4, out_specs=[spec] * 3,
                          out_shape=[jax.ShapeDtypeStruct((R, C), F32)] * 3,
                          compiler_params=_params(1), name=name)(w, g, m, v)


def _pack(pieces):
    rows = []
    for p in pieces:
        flat = p.reshape(-1)
        pad = (-flat.shape[0]) % LANES
        if pad:
            flat = jnp.concatenate([flat, jnp.zeros((pad,), F32)])
        rows.append(flat.reshape(-1, LANES))
    total = sum(r.shape[0] for r in rows)
    pad_rows = (-total) % 8
    if pad_rows:
        rows.append(jnp.zeros((pad_rows, LANES), F32))
    return jnp.concatenate(rows, axis=0)


def _unpack(buf, shapes):
    out, r0 = [], 0
    for shp in shapes:
        n = int(np.prod(shp))
        nr = -(-n // LANES)
        out.append(buf[r0:r0 + nr].reshape(-1)[:n].reshape(shp))
        r0 += nr
    return out


SMALL_NAMES = ("rel_table", "b_in", "conv_w", "conv_b", "conv_ln_g", "conv_ln_b", "attn_norm_g", "conv_norm_g",
               "ln1_g", "ln1_b", "ffn_conv_w", "ffn_conv_b", "ln2_g", "ln2_b")
BIG_NAMES = ("w_in", "w_out", "w_up", "w_down")
WEIGHT_ORDER = ("rel_table", "w_in", "b_in", "conv_w", "conv_b", "conv_ln_g", "conv_ln_b", "attn_norm_g",
                "conv_norm_g", "w_out", "ln1_g", "ln1_b", "w_up", "ffn_conv_w", "ffn_conv_b", "w_down",
                "ln2_g", "ln2_b")


def kernel(x, rel_table, w_in, b_in, conv_w, conv_b, conv_ln_g, conv_ln_b, attn_norm_g, conv_norm_g, w_out, ln1_g, ln1_b, w_up, ffn_conv_w, ffn_conv_b, w_down, ln2_g, ln2_b, loss_target, m_rel_table, m_w_in, m_b_in, m_conv_w, m_conv_b, m_conv_ln_g, m_conv_ln_b, m_attn_norm_g, m_conv_norm_g, m_w_out, m_ln1_g, m_ln1_b, m_w_up, m_ffn_conv_w, m_ffn_conv_b, m_w_down, m_ln2_g, m_ln2_b, v_rel_table, v_w_in, v_b_in, v_conv_w, v_conv_b, v_conv_ln_g, v_conv_ln_b, v_attn_norm_g, v_conv_norm_g, v_w_out, v_ln1_g, v_ln1_b, v_w_up, v_ffn_conv_w, v_ffn_conv_b, v_w_down, v_ln2_g, v_ln2_b):
    args = dict(locals())
    weights = {n: args[n] for n in WEIGHT_ORDER}
    moms = {n: args["m_" + n] for n in WEIGHT_ORDER}
    vels = {n: args["v_" + n] for n in WEIGHT_ORDER}
    xi, yi, ci = _place()
    ids = jnp.stack([xi, yi, ci]).astype(jnp.int32)
    shard = 2 * xi + yi
    D = x.shape[-1]
    DFF = w_down.shape[1] * N_SHARDS
    CW = conv_norm_g.shape[-1]

    (g_in, g_out, g_up, g_down), (g_cw, g_fcw) = _gather_weights(
        [w_in[0], w_out[0], w_up[0], w_down[0]], [conv_w[0], ffn_conv_w[0]])
    cols = lambda t: jnp.transpose(t, (1, 0, 2)).reshape(t.shape[1], N_SHARDS * t.shape[2])
    w_in_f = cols(g_in)
    w_out_f = g_out.reshape(D, D)
    w_down_f = g_down.reshape(DFF, D)
    conv_w_f = cols(g_cw)
    ffn_cw_f = cols(g_fcw)

    loss_part, grad_x, gl = _local_step(
        x, loss_target, rel_table, w_in_f, b_in, conv_w_f, conv_b, conv_ln_g, conv_ln_b, attn_norm_g, conv_norm_g,
        w_out_f, ln1_g, ln1_b, g_up, ffn_cw_f, ffn_conv_b, w_down_f, ln2_g, ln2_b)

    rows = lambda t: jnp.transpose(t.reshape(t.shape[0], N_SHARDS, t.shape[1] // N_SHARDS), (1, 0, 2))
    big_parts = [rows(gl["w_in"]), gl["w_out"].reshape(N_SHARDS, D // N_SHARDS, D),
                 gl["w_up_sh"], gl["w_down"].reshape(N_SHARDS, DFF // N_SHARDS, D)]
    sib = _sibling_exchange(big_parts)
    chip_parts = [_pair_sum(g, s, ids, name="pair_sum_" + n) for g, s, n in zip(big_parts, sib, BIG_NAMES)]

    pack = _pack([loss_part] + [gl[n] for n in SMALL_NAMES])
    got, all_packs = _chip_exchange(chip_parts, pack)
    fulls = [_final_sum(g, s, r, ids, name="final_sum_" + n)
             for g, s, r, n in zip(big_parts, sib, got, BIG_NAMES)]
    big_grads = dict(zip(BIG_NAMES, _sibling_assemble(fulls)))

    summed = _sum_packs(all_packs)
    full_shapes = {n: weights[n].shape for n in SMALL_NAMES}
    full_shapes["conv_w"] = (1, CONV_KERNEL, CW)
    full_shapes["ffn_conv_w"] = (1, FFN_CONV_KERNEL, 2 * DFF)
    un = _unpack(summed, [(1, LANES)] + [full_shapes[n] for n in SMALL_NAMES])
    loss = un[0][0, 0]
    small_grads = dict(zip(SMALL_NAMES, un[1:]))
    for n in ("conv_w", "ffn_conv_w"):
        width = weights[n].shape[-1]
        small_grads[n] = lax.dynamic_slice_in_dim(small_grads[n], shard * width, width, axis=2)

    grads, delta, new_m, new_v = {}, {}, {}, {}
    for n in BIG_NAMES:
        shp = weights[n].shape
        g2 = big_grads[n]
        d, nm, nv = _adamw(weights[n][0], g2, moms[n][0], vels[n][0], name="adamw_" + n)
        grads[n], delta[n], new_m[n], new_v[n] = (t.reshape(shp) for t in (g2, d, nm, nv))
    sp = lambda src: _pack([src[n] for n in SMALL_NAMES])
    d_s, nm_s, nv_s = _adamw(sp(weights), sp(small_grads), sp(moms), sp(vels), name="adamw_small")
    shapes = [weights[n].shape for n in SMALL_NAMES]
    for tgt, buf in ((delta, d_s), (new_m, nm_s), (new_v, nv_s)):
        tgt.update(zip(SMALL_NAMES, _unpack(buf, shapes)))
    grads.update(small_grads)

    return (loss, grad_x, *[grads[n] for n in WEIGHT_ORDER], *[delta[n] for n in WEIGHT_ORDER],
            *[new_m[n] for n in WEIGHT_ORDER], *[new_v[n] for n in WEIGHT_ORDER])
```

```python
import functools
import math

import numpy as np
import jax
import jax.numpy as jnp
from jax import lax
from jax.experimental import pallas as pl
from jax.experimental.pallas import tpu as pltpu

F32 = jnp.float32
BF16 = jnp.bfloat16
MESH = pl.DeviceIdType.MESH

HEAD_DIM = 64
LANES = 128
ATTN_BLOCK = 128
DILATED_CONFIGS = ((128, 1), (512, 4), (2048, 16))
CONV_KERNEL = 31
FFN_CONV_KERNEL = 3
REL_BUCKETS = 32
REL_MAX_DIST = 2048
DEPTH = 1
ALPHA = (2 * DEPTH) ** 0.25
LN_EPS = 1e-5
NEG_INF = -1e30
QK_SCALE = 1.0 / math.sqrt(HEAD_DIM)
ADAM_LR = 0.001
ADAM_B1 = 0.9
ADAM_B2 = 0.999
ADAM_EPS = 1e-08
ADAM_WD = 0.01
ADAM_STEP = 10
VMEM_LIMIT = 52 * 1024 * 1024
FFN_COLS = 256
N_SHARDS = 4
N_DEV = 8


def _params(n_axes):
    return pltpu.CompilerParams(dimension_semantics=("arbitrary",) * n_axes,
                                vmem_limit_bytes=VMEM_LIMIT)


MM_DIMS = {"nn": (((1,), (0,)), ((), ())), "nt": (((1,), (1,)), ((), ())), "tn": (((0,), (0,)), ((), ()))}


def _matmul_general(ins, part_fn, *, grid, tm, tn, outs, epilogue, extras=(), name):
    nk = grid[2]
    n_in, n_extra = len(ins), len(extras)

    def body(*refs):
        in_refs = refs[:n_in]
        rest = refs[n_in:]
        extra_refs = rest[:n_extra]
        out_refs = rest[n_extra:n_extra + len(outs)]
        acc_ref = rest[-1]
        i, j, k = pl.program_id(0), pl.program_id(1), pl.program_id(2)
        part = part_fn(in_refs, i, j, k)
        if nk == 1:
            epilogue(part, i, j, extra_refs, out_refs)
        else:
            @pl.when(k == 0)
            def _():
                acc_ref[...] = part

            @pl.when(k > 0)
            def _():
                acc_ref[...] += part

            @pl.when(k == nk - 1)
            def _():
                epilogue(acc_ref[...], i, j, extra_refs, out_refs)

    in_specs = [pl.BlockSpec(bs, im) for (_, bs, im) in list(ins) + list(extras)]
    out_specs = [pl.BlockSpec(bs, im) for (_, _, bs, im) in outs]
    out_shape = [jax.ShapeDtypeStruct(s, d) for (s, d, _, _) in outs]
    return pl.pallas_call(
        body, grid=grid, in_specs=in_specs, out_specs=out_specs,
        out_shape=out_shape, scratch_shapes=[pltpu.VMEM((tm, tn), F32)],
        compiler_params=_params(3), name=name,
    )(*[e[0] for e in ins], *[e[0] for e in extras])


def _dot(a, b, mode):
    return lax.dot_general(a.astype(BF16), b.astype(BF16), MM_DIMS[mode], preferred_element_type=F32)


def _matmul(a, b, *, mode, tm, tn, tk, outs, epilogue, extras=(), name):
    if mode == "tn":
        K, M = a.shape
        N = b.shape[1]
        ins = [(a, (tk, tm), lambda i, j, k: (k, i)), (b, (tk, tn), lambda i, j, k: (k, j))]
    elif mode == "nt":
        M, K = a.shape
        N = b.shape[0]
        ins = [(a, (tm, tk), lambda i, j, k: (i, k)), (b, (tn, tk), lambda i, j, k: (j, k))]
    else:
        M, K = a.shape
        N = b.shape[1]
        ins = [(a, (tm, tk), lambda i, j, k: (i, k)), (b, (tk, tn), lambda i, j, k: (k, j))]
    assert M % tm == 0 and N % tn == 0 and K % tk == 0, (name, M, N, K, tm, tn, tk)

    def part_fn(in_refs, i, j, k):
        return _dot(in_refs[0][...], in_refs[1][...], mode)

    return _matmul_general(ins, part_fn, grid=(M // tm, N // tn, K // tk), tm=tm, tn=tn, outs=outs,
                           epilogue=epilogue, extras=extras, name=name)


def _plain_out(M, N, tm, tn, dtype):
    return ((M, N), dtype, (tm, tn), lambda i, j, k: (i, j))


def _mm_plain(a, b, *, mode, tm, tn, tk, out_dtype, name, bias=None):
    if mode == "tn":
        M, N = a.shape[1], b.shape[1]
    elif mode == "nt":
        M, N = a.shape[0], b.shape[0]
    else:
        M, N = a.shape[0], b.shape[1]
    extras = []
    if bias is not None:
        extras.append((bias, (1, tn), lambda i, j, k: (0, j)))

    def epilogue(acc, i, j, extra_refs, out_refs):
        if bias is not None:
            acc = acc + extra_refs[0][...]
        out_refs[0][...] = acc.astype(out_dtype)

    return _matmul(a, b, mode=mode, tm=tm, tn=tn, tk=tk, outs=[_plain_out(M, N, tm, tn, out_dtype)],
                   epilogue=epilogue, extras=extras, name=name)[0]


def _row_tile(T, want):
    t = min(T, want)
    while T % t:
        t //= 2
    return t


def _col_tile(N, want):
    if N <= want:
        return N
    best = None
    for c in range(LANES, want + 1, LANES):
        if N % c == 0:
            best = c
    return best if best is not None else N


def _accumulate(ref, first, val):
    @pl.when(first)
    def _():
        ref[...] = val

    @pl.when(jnp.logical_not(first))
    def _():
        ref[...] += val


def _ln_fwd(z, g, b):
    mu = jnp.mean(z, axis=-1, keepdims=True)
    zc = z - mu
    var = jnp.mean(zc * zc, axis=-1, keepdims=True)
    r = lax.rsqrt(var + LN_EPS)
    xh = zc * r
    return xh * g + b, xh, r


def _ln_bwd(dy, xh, r, g):
    dxh = dy * g
    m1 = jnp.mean(dxh, axis=-1, keepdims=True)
    m2 = jnp.mean(dxh * xh, axis=-1, keepdims=True)
    return r * (dxh - m1 - xh * m2)


def _sigmoid(x):
    return 1.0 / (1.0 + jnp.exp(-x))


def _shift_down(x, s, row):
    if s == 0:
        return x
    return jnp.where(row >= s, pltpu.roll(x, s, 0), 0.0)


def _shift_up(x, s, row):
    if s == 0:
        return x
    n = x.shape[0]
    return jnp.where(row < n - s, pltpu.roll(x, n - s, 0), 0.0)


def _bucket_tables():
    exact = REL_BUCKETS // 2
    qi = np.arange(ATTN_BLOCK)[:, None]
    kj = np.arange(2 * ATTN_BLOCK)[None, :]
    steps = qi + ATTN_BLOCK - kj
    buckets, masks = [], []
    for window, dilation in DILATED_CONFIGS:
        max_steps = window // dilation
        band = (steps >= 0) & (steps <= max_steps)
        dist = np.maximum(steps, 0) * dilation
        d_f = np.maximum(dist, 1).astype(np.float32)
        large = exact + (np.log(d_f / np.float32(exact)) / np.float32(math.log(REL_MAX_DIST / exact))
                         * np.float32(REL_BUCKETS - exact)).astype(np.int32)
        large = np.minimum(large, REL_BUCKETS - 1)
        bucket = np.where(dist < exact, dist, large).astype(np.int32)
        buckets.append(bucket.reshape(1, -1))
        masks.append(np.where(band, 0.0, NEG_INF).astype(np.float32).reshape(1, -1))
    return np.stack(buckets), np.stack(masks)


def _split_hi_lo(x):
    hi = x.astype(BF16)
    lo = (x - hi.astype(F32)).astype(BF16)
    return hi, lo


def _bias_build(rel_table_t, bucket, mask):
    H = rel_table_t.shape[0]
    n = bucket.shape[-1]

    def body(t_ref, bkt_ref, mask_ref, o_ref):
        onehot = (lax.broadcasted_iota(jnp.int32, (REL_BUCKETS, n), 0) == bkt_ref[0]).astype(BF16)
        t = t_ref[...]
        t1 = t.astype(BF16)
        r1 = t - t1.astype(F32)
        t2 = r1.astype(BF16)
        t3 = (r1 - t2.astype(F32)).astype(BF16)
        acc = jnp.dot(t1, onehot, preferred_element_type=F32)
        acc = acc + jnp.dot(t2, onehot, preferred_element_type=F32)
        acc = acc + jnp.dot(t3, onehot, preferred_element_type=F32)
        o_ref[0] = acc + mask_ref[0]

    return pl.pallas_call(
        body, grid=(3,),
        in_specs=[pl.BlockSpec((H, REL_BUCKETS), lambda b: (0, 0)),
                  pl.BlockSpec((1, 1, n), lambda b: (b, 0, 0)),
                  pl.BlockSpec((1, 1, n), lambda b: (b, 0, 0))],
        out_specs=pl.BlockSpec((1, H, n), lambda b: (b, 0, 0)),
        out_shape=jax.ShapeDtypeStruct((3, H, n), F32),
        compiler_params=_params(1), name="bias_build",
    )(rel_table_t, bucket, mask)


def _rel_grad(dbias, bucket):
    H = dbias.shape[1]
    n = bucket.shape[-1]
    dims = (((1,), (1,)), ((), ()))

    def body(d_ref, bkt_ref, o_ref):
        b = pl.program_id(0)
        onehot = (lax.broadcasted_iota(jnp.int32, (REL_BUCKETS, n), 0) == bkt_ref[0]).astype(BF16)
        d = d_ref[0]
        d1 = d.astype(BF16)
        r1 = d - d1.astype(F32)
        d2 = r1.astype(BF16)
        d3 = (r1 - d2.astype(F32)).astype(BF16)
        acc = lax.dot_general(d1, onehot, dims, preferred_element_type=F32)
        acc = acc + lax.dot_general(d2, onehot, dims, preferred_element_type=F32)
        acc = acc + lax.dot_general(d3, onehot, dims, preferred_element_type=F32)
        _accumulate(o_ref, b == 0, acc)

    return pl.pallas_call(
        body, grid=(3,),
        in_specs=[pl.BlockSpec((1, H, n), lambda b: (b, 0, 0)),
                  pl.BlockSpec((1, 1, n), lambda b: (b, 0, 0))],
        out_specs=pl.BlockSpec((H, REL_BUCKETS), lambda b: (0, 0)),
        out_shape=jax.ShapeDtypeStruct((H, REL_BUCKETS), F32),
        compiler_params=_params(1), name="rel_grad",
    )(dbias, bucket)


def _attn_specs(B, S, AW, d):
    L = S // d
    HP = AW // LANES
    W3 = 3 * HP
    q_spec = pl.BlockSpec((1, L, LANES), lambda h, b, r: (b, 0, r * W3 + h))
    k_spec = pl.BlockSpec((1, L, LANES), lambda h, b, r: (b, 0, r * W3 + HP + h))
    v_spec = pl.BlockSpec((1, L, LANES), lambda h, b, r: (b, 0, r * W3 + 2 * HP + h))
    o_spec = pl.BlockSpec((1, L, LANES), lambda h, b, r: (b, 0, r * HP + h))
    bias_spec = pl.BlockSpec((2, ATTN_BLOCK, 2 * ATTN_BLOCK), lambda h, b, r: (h, 0, 0))
    return L, HP, q_spec, k_spec, v_spec, o_spec, bias_spec


def _attn_fwd(qkv, bias, B, S, AW, d, name):
    L, HP, q_spec, k_spec, v_spec, o_spec, bias_spec = _attn_specs(B, S, AW, d)
    nb = L // ATTN_BLOCK
    nt = (((1,), (1,)), ((), ()))

    def body(q_ref, k_ref, v_ref, b_ref, o_ref, lse_ref):
        head0 = lax.broadcasted_iota(jnp.int32, (1, LANES), 1) < HEAD_DIM

        def block(n, first):
            qs = pl.multiple_of(n * ATTN_BLOCK, ATTN_BLOCK)
            q = q_ref[0, pl.ds(qs, ATTN_BLOCK), :]
            if first:
                kk = k_ref[0, pl.ds(0, ATTN_BLOCK), :]
                vv = v_ref[0, pl.ds(0, ATTN_BLOCK), :]
            else:
                ks = pl.multiple_of(n * ATTN_BLOCK - ATTN_BLOCK, ATTN_BLOCK)
                kk = k_ref[0, pl.ds(ks, 2 * ATTN_BLOCK), :]
                vv = v_ref[0, pl.ds(ks, 2 * ATTN_BLOCK), :]
            outs, lses = [], []
            for e in range(2):
                msk = head0 if e == 0 else jnp.logical_not(head0)
                qe = jnp.where(msk, q, jnp.zeros_like(q))
                s = lax.dot_general(qe, kk, nt, preferred_element_type=F32) * QK_SCALE
                s = s + (b_ref[e, :, ATTN_BLOCK:] if first else b_ref[e])
                m = jnp.max(s, axis=-1, keepdims=True)
                p = jnp.exp(s - m)
                l = jnp.sum(p, axis=-1, keepdims=True)
                o = jnp.dot(p.astype(BF16), vv, preferred_element_type=F32)
                outs.append(o / l)
                lses.append(jnp.broadcast_to(m + jnp.log(l), (ATTN_BLOCK, LANES)))
            o_ref[0, pl.ds(qs, ATTN_BLOCK), :] = jnp.where(head0, outs[0], outs[1])
            lse_ref[0, pl.ds(qs, ATTN_BLOCK), :] = jnp.where(head0, lses[0], lses[1])

        block(0, True)
        if nb > 1:
            def loop(n, c):
                block(n, False)
                return c
            lax.fori_loop(1, nb, loop, 0)

    qv = qkv.reshape(B, L, d * 3 * AW)
    o, lse = pl.pallas_call(
        body, grid=(HP, B, d), in_specs=[q_spec, k_spec, v_spec, bias_spec],
        out_specs=[o_spec, o_spec],
        out_shape=[jax.ShapeDtypeStruct((B, L, d * AW), F32)] * 2,
        compiler_params=_params(3), name=name,
    )(qv, qv, qv, bias)
    return o.reshape(B * S, AW), lse.reshape(B * S, AW)


def _attn_bwd(qkv, do, lse, dd, bias, B, S, AW, d, name):
    L, HP, q_spec, k_spec, v_spec, o_spec, bias_spec = _attn_specs(B, S, AW, d)
    nb = L // ATTN_BLOCK
    nt = (((1,), (1,)), ((), ()))
    tn = (((0,), (0,)), ((), ()))

    def body(q_ref, k_ref, v_ref, do_ref, lse_ref, dd_ref, b_ref, dq_ref, dk_ref, dv_ref, db_ref):
        head0 = lax.broadcasted_iota(jnp.int32, (1, LANES), 1) < HEAD_DIM
        first_step = jnp.logical_and(pl.program_id(1) == 0, pl.program_id(2) == 0)

        @pl.when(first_step)
        def _():
            db_ref[...] = jnp.zeros_like(db_ref)

        dk_ref[...] = jnp.zeros_like(dk_ref)
        dv_ref[...] = jnp.zeros_like(dv_ref)

        def block(n, first):
            qs = pl.multiple_of(n * ATTN_BLOCK, ATTN_BLOCK)
            nkeys = ATTN_BLOCK if first else 2 * ATTN_BLOCK
            ks = 0 if first else pl.multiple_of(n * ATTN_BLOCK - ATTN_BLOCK, ATTN_BLOCK)
            q = q_ref[0, pl.ds(qs, ATTN_BLOCK), :]
            kk = k_ref[0, pl.ds(ks, nkeys), :]
            vv = v_ref[0, pl.ds(ks, nkeys), :]
            dout = do_ref[0, pl.ds(qs, ATTN_BLOCK), :]
            lse_b = lse_ref[0, pl.ds(qs, ATTN_BLOCK), :]
            dd_b = dd_ref[0, pl.ds(qs, ATTN_BLOCK), :]
            dq = jnp.zeros((ATTN_BLOCK, LANES), F32)
            dkk = jnp.zeros((nkeys, LANES), F32)
            dvv = jnp.zeros((nkeys, LANES), F32)
            for e in range(2):
                msk = head0 if e == 0 else jnp.logical_not(head0)
                c0 = e * HEAD_DIM
                qe = jnp.where(msk, q, jnp.zeros_like(q))
                doe = jnp.where(msk, dout, jnp.zeros_like(dout))
                kke = jnp.where(msk, kk, jnp.zeros_like(kk))
                s = lax.dot_general(qe, kk, nt, preferred_element_type=F32) * QK_SCALE
                s = s + (b_ref[e, :, ATTN_BLOCK:] if first else b_ref[e])
                p = jnp.exp(s - lse_b[:, c0:c0 + 1])
                dp = lax.dot_general(doe, vv, nt, preferred_element_type=F32)
                ds = p * (dp - dd_b[:, c0:c0 + 1])
                if first:
                    db_ref[e, :, ATTN_BLOCK:] += ds
                else:
                    db_ref[e] += ds
                dsb = (ds * QK_SCALE).astype(BF16)
                dq = dq + jnp.dot(dsb, kke, preferred_element_type=F32)
                dkk = dkk + lax.dot_general(dsb, qe, tn, preferred_element_type=F32)
                dvv = dvv + lax.dot_general(p.astype(BF16), doe, tn, preferred_element_type=F32)
            dq_ref[0, pl.ds(qs, ATTN_BLOCK), :] = dq
            dk_ref[0, pl.ds(ks, nkeys), :] += dkk
            dv_ref[0, pl.ds(ks, nkeys), :] += dvv

        block(0, True)
        if nb > 1:
            def loop(n, c):
                block(n, False)
                return c
            lax.fori_loop(1, nb, loop, 0)

    H = AW // HEAD_DIM
    qv = qkv.reshape(B, L, d * 3 * AW)
    view = lambda t: t.reshape(B, L, d * AW)
    dq, dk, dv, db = pl.pallas_call(
        body, grid=(HP, B, d),
        in_specs=[q_spec, k_spec, v_spec, o_spec, o_spec, o_spec, bias_spec],
        out_specs=[o_spec, o_spec, o_spec, bias_spec],
        out_shape=[jax.ShapeDtypeStruct((B, L, d * AW), F32)] * 3
        + [jax.ShapeDtypeStruct((H, ATTN_BLOCK, 2 * ATTN_BLOCK), F32)],
        compiler_params=_params(3), name=name,
    )(qv, qv, qv, view(do), view(lse), view(dd), bias)
    flat = lambda t: t.reshape(B * S, AW)
    return flat(dq), flat(dk), flat(dv), db


def _attn_combine(ons, lses, gain, tm):
    T, AW = ons[0].shape

    def body(o1, o2, o3, l1, l2, l3, g_ref, attn_ref, lse_ref, mix_ref, r_ref):
        la, lb, lc = l1[...], l2[...], l3[...]
        m = jnp.maximum(jnp.maximum(la, lb), lc)
        ea, eb, ec = jnp.exp(la - m), jnp.exp(lb - m), jnp.exp(lc - m)
        den = ea + eb + ec
        attn = (ea * o1[...] + eb * o2[...] + ec * o3[...]) / den
        attn_ref[...] = attn
        lse_ref[...] = m + jnp.log(den)
        r = lax.rsqrt(jnp.mean(attn * attn, axis=-1, keepdims=True) + LN_EPS)
        mix_ref[...] = (attn * r * g_ref[...]).astype(BF16)
        r_ref[...] = jnp.broadcast_to(r, (tm, LANES))

    row = pl.BlockSpec((tm, AW), lambda i: (i, 0))
    return pl.pallas_call(
        body, grid=(T // tm,),
        in_specs=[row] * 6 + [pl.BlockSpec((1, AW), lambda i: (0, 0))],
        out_specs=[row, row, row, pl.BlockSpec((tm, LANES), lambda i: (i, 0))],
        out_shape=[jax.ShapeDtypeStruct((T, AW), F32), jax.ShapeDtypeStruct((T, AW), F32),
                   jax.ShapeDtypeStruct((T, AW), BF16), jax.ShapeDtypeStruct((T, LANES), F32)],
        compiler_params=_params(1), name="attn_combine",
    )(*ons, *lses, gain)


def _to_sub(src_ref, stage_ref, dsts, S):
    stage_ref[...] = src_ref[0].astype(F32)
    for (_, d), dst in zip(DILATED_CONFIGS[1:], dsts):
        L = S // d
        for r in range(d):
            dst[r * L:(r + 1) * L, :] = stage_ref[pl.ds(r, L, stride=d), :].astype(dst.dtype)


def _branch_blocks(S, d, block):
    nb = S // d // ATTN_BLOCK
    inner_unroll = 3 if (nb - 1) % 3 == 0 else 1

    def per_residue(r, c):
        block(r * nb, True)
        if nb > 1:
            def inner(n, c2):
                block(r * nb + n, False)
                return c2
            lax.fori_loop(1, nb, inner, 0, unroll=inner_unroll)
        return c

    lax.fori_loop(0, d, per_residue, 0, unroll=4 if nb == 1 else 1)


def _attention_fwd(qkv, bias_all, B, S, AW):
    HP = AW // LANES
    nt = MM_DIMS["nt"]

    def body(q_ref, k_ref, v_ref, b_ref, o_ref, lse_ref, stage, q4, q16, k4, k16, v4, v16, o1, l1, o4, l4, o16, l16):
        head0 = lax.broadcasted_iota(jnp.int32, (1, LANES), 1) < HEAD_DIM
        _to_sub(q_ref, stage, (q4, q16), S)
        _to_sub(k_ref, stage, (k4, k16), S)
        _to_sub(v_ref, stage, (v4, v16), S)
        srcs = ((q_ref.at[0], k_ref.at[0], v_ref.at[0], o1, l1), (q4, k4, v4, o4, l4), (q16, k16, v16, o16, l16))
        for bi, (_, d) in enumerate(DILATED_CONFIGS):
            qs_ref, ks_ref, vs_ref, od_ref, ld_ref = srcs[bi]

            def block(g, first, bi=bi, qs_ref=qs_ref, ks_ref=ks_ref, vs_ref=vs_ref, od_ref=od_ref, ld_ref=ld_ref):
                qs = pl.multiple_of(g * ATTN_BLOCK, ATTN_BLOCK)
                nkeys = ATTN_BLOCK if first else 2 * ATTN_BLOCK
                ks = qs if first else pl.multiple_of(qs - ATTN_BLOCK, ATTN_BLOCK)
                q = qs_ref[pl.ds(qs, ATTN_BLOCK), :]
                kk = ks_ref[pl.ds(ks, nkeys), :]
                vv = vs_ref[pl.ds(ks, nkeys), :]
                outs, lses = [], []
                for e in range(2):
                    msk = head0 if e == 0 else jnp.logical_not(head0)
                    qe = jnp.where(msk, q * QK_SCALE, jnp.zeros_like(q))
                    s = lax.dot_general(qe, kk, nt, preferred_element_type=F32)
                    s = s + (b_ref[bi, e, :, ATTN_BLOCK:] if first else b_ref[bi, e])
                    m = jnp.max(s, axis=-1, keepdims=True)
                    p = jnp.exp(s - m)
                    l = jnp.sum(p, axis=-1, keepdims=True)
                    o = jnp.dot(p.astype(BF16), vv, preferred_element_type=F32)
                    outs.append(o / l)
                    lses.append(jnp.broadcast_to(m + jnp.log(l), (ATTN_BLOCK, LANES)))
                od_ref[pl.ds(qs, ATTN_BLOCK), :] = jnp.where(head0, outs[0], outs[1])
                ld_ref[pl.ds(qs, ATTN_BLOCK), :] = jnp.where(head0, lses[0], lses[1])

            _branch_blocks(S, d, block)

        def natural(sub_ref, d):
            L = S // d
            for r in range(d):
                stage[pl.ds(r, L, stride=d), :] = sub_ref[r * L:(r + 1) * L, :]
            return stage[...]

        la = l1[...]
        lb = natural(l4, 4)
        lc = natural(l16, 16)
        m = jnp.maximum(jnp.maximum(la, lb), lc)
        ea, eb, ec = jnp.exp(la - m), jnp.exp(lb - m), jnp.exp(lc - m)
        den = ea + eb + ec
        lse_ref[0] = m + jnp.log(den)
        acc = ea * o1[...]
        acc = acc + eb * natural(o4, 4)
        acc = acc + ec * natural(o16, 16)
        o_ref[0] = acc / den

    blk = lambda off: pl.BlockSpec((1, S, LANES), lambda b, h: (b, 0, off + h))
    qv = qkv.reshape(B, S, 3 * AW)
    sub_b = pltpu.VMEM((S, LANES), BF16)
    sub_f = pltpu.VMEM((S, LANES), F32)
    o, lse = pl.pallas_call(
        body, grid=(B, HP),
        in_specs=[blk(0), blk(HP), blk(2 * HP),
                  pl.BlockSpec((3, 2, ATTN_BLOCK, 2 * ATTN_BLOCK), lambda b, h: (0, h, 0, 0))],
        out_specs=[blk(0), blk(0)],
        out_shape=[jax.ShapeDtypeStruct((B, S, AW), F32)] * 2,
        scratch_shapes=[sub_f] + [sub_b] * 6 + [sub_f] * 6,
        compiler_params=_params(2), name="attention_fwd",
    )(qv, qv, qv, bias_all)
    return o.reshape(B * S, AW), lse.reshape(B * S, AW)


def _attention_bwd(qkv, do, lse, dd, bias_all, B, S, AW):
    HP = AW // LANES
    H = AW // HEAD_DIM
    nt, tn = MM_DIMS["nt"], MM_DIMS["tn"]

    def body(q_ref, k_ref, v_ref, do_ref, lse_ref, dd_ref, b_ref,
             dq_ref, dk_ref, dv_ref, csq_ref, csk_ref, csv_ref, db_ref,
             stage, q4, q16, k4, k16, v4, v16, g4, g16, l4, l16, d4, d16,
             aq1, ak1, av1, aq4, ak4, av4, aq16, ak16, av16):
        head0 = lax.broadcasted_iota(jnp.int32, (1, LANES), 1) < HEAD_DIM
        first_b = pl.program_id(1) == 0

        @pl.when(first_b)
        def _():
            db_ref[...] = jnp.zeros_like(db_ref)

        _to_sub(q_ref, stage, (q4, q16), S)
        _to_sub(k_ref, stage, (k4, k16), S)
        _to_sub(v_ref, stage, (v4, v16), S)
        _to_sub(do_ref, stage, (g4, g16), S)
        _to_sub(lse_ref, stage, (l4, l16), S)
        _to_sub(dd_ref, stage, (d4, d16), S)
        for acc in (ak1, av1, ak4, av4, ak16, av16):
            acc[...] = jnp.zeros_like(acc)
        srcs = ((q_ref.at[0], k_ref.at[0], v_ref.at[0], do_ref.at[0], lse_ref.at[0], dd_ref.at[0], aq1, ak1, av1),
                (q4, k4, v4, g4, l4, d4, aq4, ak4, av4), (q16, k16, v16, g16, l16, d16, aq16, ak16, av16))
        for bi, (_, d) in enumerate(DILATED_CONFIGS):
            def block(g, first, bi=bi, refs=srcs[bi]):
                qs_ref, ks_ref, vs_ref, gs_ref, ls_ref, ds_ref, aq, ak, av = refs
                qs = pl.multiple_of(g * ATTN_BLOCK, ATTN_BLOCK)
                nkeys = ATTN_BLOCK if first else 2 * ATTN_BLOCK
                ks = qs if first else pl.multiple_of(qs - ATTN_BLOCK, ATTN_BLOCK)
                q = qs_ref[pl.ds(qs, ATTN_BLOCK), :]
                kk = ks_ref[pl.ds(ks, nkeys), :]
                vv = vs_ref[pl.ds(ks, nkeys), :]
                dout = gs_ref[pl.ds(qs, ATTN_BLOCK), :]
                lse_b = ls_ref[pl.ds(qs, ATTN_BLOCK), :]
                dd_b = ds_ref[pl.ds(qs, ATTN_BLOCK), :]
                dq = jnp.zeros((ATTN_BLOCK, LANES), F32)
                dkk = jnp.zeros((nkeys, LANES), F32)
                dvv = jnp.zeros((nkeys, LANES), F32)
                for e in range(2):
                    msk = head0 if e == 0 else jnp.logical_not(head0)
                    c0 = e * HEAD_DIM
                    qe = jnp.where(msk, q * QK_SCALE, jnp.zeros_like(q))
                    doe = jnp.where(msk, dout, jnp.zeros_like(dout))
                    kke = jnp.where(msk, kk * QK_SCALE, jnp.zeros_like(kk))
                    s = lax.dot_general(qe, kk, nt, preferred_element_type=F32)
                    s = s + (b_ref[bi, e, :, ATTN_BLOCK:] if first else b_ref[bi, e])
                    p = jnp.exp(s - lse_b[:, c0:c0 + 1])
                    dp = lax.dot_general(doe, vv, nt, preferred_element_type=F32)
                    ds = p * (dp - dd_b[:, c0:c0 + 1])
                    if first:
                        db_ref[bi, e, :, ATTN_BLOCK:] += ds
                    else:
                        db_ref[bi, e] += ds
                    dsb = ds.astype(BF16)
                    dq = dq + jnp.dot(dsb, kke, preferred_element_type=F32)
                    dkk = dkk + lax.dot_general(dsb, qe, tn, preferred_element_type=F32)
                    dvv = dvv + lax.dot_general(p.astype(BF16), doe, tn, preferred_element_type=F32)
                aq[pl.ds(qs, ATTN_BLOCK), :] = dq
                ak[pl.ds(ks, nkeys), :] += dkk
                av[pl.ds(ks, nkeys), :] += dvv

            _branch_blocks(S, d, block)

        for a1, a4, a16, out_ref, cs_ref in ((aq1, aq4, aq16, dq_ref, csq_ref), (ak1, ak4, ak16, dk_ref, csk_ref),
                                             (av1, av4, av16, dv_ref, csv_ref)):
            stage[...] = a1[...]
            for d, sub in ((4, a4), (16, a16)):
                L = S // d
                for r in range(d):
                    stage[pl.ds(r, L, stride=d), :] += sub[r * L:(r + 1) * L, :]
            tot = stage[...]
            out_ref[0] = tot.astype(out_ref.dtype)
            _accumulate(cs_ref, first_b, jnp.sum(tot, axis=0, keepdims=True))

    blk = lambda off: pl.BlockSpec((1, S, LANES), lambda h, b: (b, 0, off + h))
    cs_spec = pl.BlockSpec((1, LANES), lambda h, b: (0, h))
    bias_spec = pl.BlockSpec((3, 2, ATTN_BLOCK, 2 * ATTN_BLOCK), lambda h, b: (0, h, 0, 0))
    qv = qkv.reshape(B, S, 3 * AW)
    view = lambda t: t.reshape(B, S, AW)
    sub_b = pltpu.VMEM((S, LANES), BF16)
    sub_f = pltpu.VMEM((S, LANES), F32)
    res = pl.pallas_call(
        body, grid=(HP, B),
        in_specs=[blk(0), blk(HP), blk(2 * HP), blk(0), blk(0), blk(0), bias_spec],
        out_specs=[blk(0), blk(0), blk(0), cs_spec, cs_spec, cs_spec, bias_spec],
        out_shape=[jax.ShapeDtypeStruct((B, S, AW), BF16)] * 3 + [jax.ShapeDtypeStruct((1, AW), F32)] * 3
        + [jax.ShapeDtypeStruct((3, H, ATTN_BLOCK, 2 * ATTN_BLOCK), F32)],
        scratch_shapes=[sub_f] + [sub_b] * 8 + [sub_f] * 4 + [sub_f] * 9,
        compiler_params=_params(2), name="attention_bwd",
    )(qv, qv, qv, view(do), view(lse), view(dd), bias_all)
    flat = lambda t: t.reshape(B * S, AW)
    return flat(res[0]), flat(res[1]), flat(res[2]), res[3], res[4], res[5], res[6]


def _regroup(src, stage, dst, d, S, off=0):
    if d == 1:
        dst[off:off + S, :] = src.astype(dst.dtype)
        return
    stage[...] = src.astype(F32)
    L = S // d
    for r in range(d):
        dst[off + r * L:off + (r + 1) * L, :] = stage[pl.ds(r, L, stride=d), :].astype(dst.dtype)


def _ungroup(sub_ref, off, nat_ref, d, S, add):
    L = S // d
    for r in range(d):
        rows = pl.ds(0, S) if d == 1 else pl.ds(r, L, stride=d)
        val = sub_ref[off + r * L:off + (r + 1) * L, :]
        if add:
            nat_ref[rows, :] += val
        else:
            nat_ref[rows, :] = val


def _branch_scores(qe, kc3, kp3, b_ref, bi, e, first3):
    s_cur = jnp.einsum("gqe,gke->gqk", qe, kc3, preferred_element_type=F32) + b_ref[bi, e, :, ATTN_BLOCK:]
    if kp3 is None:
        return s_cur, None
    s_prev = jnp.einsum("gqe,gke->gqk", qe, kp3, preferred_element_type=F32) + b_ref[bi, e, :, :ATTN_BLOCK]
    return s_cur, jnp.where(first3, NEG_INF, s_prev)


def _attention_fwd(qkv, bias_all, B, S, AW):
    HP = AW // LANES
    G = S // ATTN_BLOCK
    blk3 = (G, ATTN_BLOCK, LANES)

    def body(q_ref, k_ref, v_ref, b_ref, o_ref, lse_ref, stage, qs, ks, vs, ot, lt, on0, on1, on2, ln0, ln1, ln2):
        head0 = lax.broadcasted_iota(jnp.int32, (1, 1, LANES), 2) < HEAD_DIM
        g_idx = lax.broadcasted_iota(jnp.int32, (G, 1, 1), 0)
        ks[0:ATTN_BLOCK, :] = jnp.zeros((ATTN_BLOCK, LANES), BF16)
        vs[0:ATTN_BLOCK, :] = jnp.zeros((ATTN_BLOCK, LANES), BF16)
        nat_o, nat_l = (on0, on1, on2), (ln0, ln1, ln2)
        for bi, (_, d) in enumerate(DILATED_CONFIGS):
            nb = S // d // ATTN_BLOCK
            _regroup(q_ref[0], stage, qs, d, S)
            _regroup(k_ref[0], stage, ks, d, S, ATTN_BLOCK)
            _regroup(v_ref[0], stage, vs, d, S, ATTN_BLOCK)
            q3 = qs[...].reshape(blk3) * QK_SCALE
            kc3 = ks[ATTN_BLOCK:ATTN_BLOCK + S, :].reshape(blk3)
            vc3 = vs[ATTN_BLOCK:ATTN_BLOCK + S, :].reshape(blk3)
            kp3 = vp3 = first3 = None
            if nb > 1:
                kp3 = ks[0:S, :].reshape(blk3)
                vp3 = vs[0:S, :].reshape(blk3)
                first3 = (g_idx & (nb - 1)) == 0
            outs, lses = [], []
            for e in range(2):
                msk = head0 if e == 0 else jnp.logical_not(head0)
                qe = jnp.where(msk, q3, jnp.zeros_like(q3))
                s_cur, s_prev = _branch_scores(qe, kc3, kp3, b_ref, bi, e, first3)
                m = jnp.max(s_cur, axis=-1, keepdims=True)
                if s_prev is not None:
                    m = jnp.maximum(m, jnp.max(s_prev, axis=-1, keepdims=True))
                p = jnp.exp(s_cur - m)
                l = jnp.sum(p, axis=-1, keepdims=True)
                o = jnp.einsum("gqk,gke->gqe", p.astype(BF16), vc3, preferred_element_type=F32)
                if s_prev is not None:
                    p = jnp.exp(s_prev - m)
                    l = l + jnp.sum(p, axis=-1, keepdims=True)
                    o = o + jnp.einsum("gqk,gke->gqe", p.astype(BF16), vp3, preferred_element_type=F32)
                outs.append(o / l)
                lses.append(jnp.broadcast_to(m + jnp.log(l), blk3))
            ot[...] = jnp.where(head0, outs[0], outs[1]).reshape(S, LANES)
            lt[...] = jnp.where(head0, lses[0], lses[1]).reshape(S, LANES)
            _ungroup(ot, 0, nat_o[bi], d, S, add=False)
            _ungroup(lt, 0, nat_l[bi], d, S, add=False)

        la, lb, lc = ln0[...], ln1[...], ln2[...]
        m = jnp.maximum(jnp.maximum(la, lb), lc)
        ea, eb, ec = jnp.exp(la - m), jnp.exp(lb - m), jnp.exp(lc - m)
        den = ea + eb + ec
        lse_ref[0] = m + jnp.log(den)
        o_ref[0] = (ea * on0[...] + eb * on1[...] + ec * on2[...]) / den

    blk = lambda off: pl.BlockSpec((1, S, LANES), lambda b, h: (b, 0, off + h))
    qv = qkv.reshape(B, S, 3 * AW)
    sub_f = pltpu.VMEM((S, LANES), F32)
    pad_b = pltpu.VMEM((S + ATTN_BLOCK, LANES), BF16)
    o, lse = pl.pallas_call(
        body, grid=(B, HP),
        in_specs=[blk(0), blk(HP), blk(2 * HP),
                  pl.BlockSpec((3, 2, ATTN_BLOCK, 2 * ATTN_BLOCK), lambda b, h: (0, h, 0, 0))],
        out_specs=[blk(0), blk(0)],
        out_shape=[jax.ShapeDtypeStruct((B, S, AW), F32)] * 2,
        scratch_shapes=[sub_f, pltpu.VMEM((S, LANES), BF16), pad_b, pad_b] + [sub_f] * 8,
        compiler_params=_params(2), name="attention_fwd",
    )(qv, qv, qv, bias_all)
    return o.reshape(B * S, AW), lse.reshape(B * S, AW)


def _attention_bwd(qkv, do, lse, dd, bias_all, B, S, AW):
    HP = AW // LANES
    H = AW // HEAD_DIM
    G = S // ATTN_BLOCK
    blk3 = (G, ATTN_BLOCK, LANES)
    PAD = ATTN_BLOCK

    def body(q_ref, k_ref, v_ref, do_ref, lse_ref, dd_ref, b_ref,
             dq_ref, dk_ref, dv_ref, csq_ref, csk_ref, csv_ref, db_ref,
             stage, qs, ks, vs, gs, ls, ds_, tq, tk, tv, accq, acck, accv):
        head0 = lax.broadcasted_iota(jnp.int32, (1, 1, LANES), 2) < HEAD_DIM
        g_idx = lax.broadcasted_iota(jnp.int32, (G, 1, 1), 0)
        first_b = pl.program_id(1) == 0

        @pl.when(first_b)
        def _():
            db_ref[...] = jnp.zeros_like(db_ref)

        ks[0:PAD, :] = jnp.zeros((PAD, LANES), BF16)
        vs[0:PAD, :] = jnp.zeros((PAD, LANES), BF16)
        tk[0:PAD, :] = jnp.zeros((PAD, LANES), F32)
        tv[0:PAD, :] = jnp.zeros((PAD, LANES), F32)
        for bi, (_, d) in enumerate(DILATED_CONFIGS):
            nb = S // d // ATTN_BLOCK
            _regroup(q_ref[0], stage, qs, d, S)
            _regroup(k_ref[0], stage, ks, d, S, PAD)
            _regroup(v_ref[0], stage, vs, d, S, PAD)
            _regroup(do_ref[0], stage, gs, d, S)
            _regroup(lse_ref[0], stage, ls, d, S)
            _regroup(dd_ref[0], stage, ds_, d, S)
            q3 = qs[...].reshape(blk3) * QK_SCALE
            do3 = gs[...].reshape(blk3)
            lse3 = ls[...].reshape(blk3)
            dd3 = ds_[...].reshape(blk3)
            kc3 = ks[PAD:PAD + S, :].reshape(blk3)
            vc3 = vs[PAD:PAD + S, :].reshape(blk3)
            kp3 = vp3 = first3 = None
            if nb > 1:
                kp3 = ks[0:S, :].reshape(blk3)
                vp3 = vs[0:S, :].reshape(blk3)
                first3 = (g_idx & (nb - 1)) == 0
            dq = jnp.zeros(blk3, F32)
            dkc = jnp.zeros(blk3, F32)
            dvc = jnp.zeros(blk3, F32)
            dkp = jnp.zeros(blk3, F32)
            dvp = jnp.zeros(blk3, F32)
            for e in range(2):
                msk = head0 if e == 0 else jnp.logical_not(head0)
                c0 = e * HEAD_DIM
                qe = jnp.where(msk, q3, jnp.zeros_like(q3))
                doe = jnp.where(msk, do3, jnp.zeros_like(do3))
                lse_e = lse3[:, :, c0:c0 + 1]
                dd_e = dd3[:, :, c0:c0 + 1]
                s_cur, s_prev = _branch_scores(qe, kc3, kp3, b_ref, bi, e, first3)
                for s, k3, v3, cur in ((s_cur, kc3, vc3, True), (s_prev, kp3, vp3, False)):
                    if s is None:
                        continue
                    p = jnp.exp(s - lse_e)
                    dp = jnp.einsum("gqe,gke->gqk", doe, v3, preferred_element_type=F32)
                    dsc = p * (dp - dd_e)
                    if cur:
                        db_ref[bi, e, :, ATTN_BLOCK:] += jnp.sum(dsc, axis=0)
                    else:
                        db_ref[bi, e, :, :ATTN_BLOCK] += jnp.sum(dsc, axis=0)
                    dsb = dsc.astype(BF16)
                    ke = jnp.where(msk, k3 * QK_SCALE, jnp.zeros_like(k3))
                    dq = dq + jnp.einsum("gqk,gke->gqe", dsb, ke, preferred_element_type=F32)
                    dk_e = jnp.einsum("gqk,gqe->gke", dsb, qe, preferred_element_type=F32)
                    dv_e = jnp.einsum("gqk,gqe->gke", p.astype(BF16), doe, preferred_element_type=F32)
                    if cur:
                        dkc, dvc = dkc + dk_e, dvc + dv_e
                    else:
                        dkp, dvp = dkp + dk_e, dvp + dv_e
            tq[...] = dq.reshape(S, LANES)
            tk[PAD:PAD + S, :] = dkc.reshape(S, LANES)
            tv[PAD:PAD + S, :] = dvc.reshape(S, LANES)
            if nb > 1:
                tk[0:S, :] += dkp.reshape(S, LANES)
                tv[0:S, :] += dvp.reshape(S, LANES)
            _ungroup(tq, 0, accq, d, S, add=bi > 0)
            _ungroup(tk, PAD, acck, d, S, add=bi > 0)
            _ungroup(tv, PAD, accv, d, S, add=bi > 0)

        for acc, out_ref, cs_ref in ((accq, dq_ref, csq_ref), (acck, dk_ref, csk_ref), (accv, dv_ref, csv_ref)):
            tot = acc[...]
            out_ref[0] = tot.astype(out_ref.dtype)
            _accumulate(cs_ref, first_b, jnp.sum(tot, axis=0, keepdims=True))

    blk = lambda off: pl.BlockSpec((1, S, LANES), lambda h, b: (b, 0, off + h))
    cs_spec = pl.BlockSpec((1, LANES), lambda h, b: (0, h))
    bias_spec = pl.BlockSpec((3, 2, ATTN_BLOCK, 2 * ATTN_BLOCK), lambda h, b: (0, h, 0, 0))
    qv = qkv.reshape(B, S, 3 * AW)
    view = lambda t: t.reshape(B, S, AW)
    sub_b = pltpu.VMEM((S, LANES), BF16)
    sub_f = pltpu.VMEM((S, LANES), F32)
    pad_b = pltpu.VMEM((S + PAD, LANES), BF16)
    pad_f = pltpu.VMEM((S + PAD, LANES), F32)
    res = pl.pallas_call(
        body, grid=(HP, B),
        in_specs=[blk(0), blk(HP), blk(2 * HP), blk(0), blk(0), blk(0), bias_spec],
        out_specs=[blk(0), blk(0), blk(0), cs_spec, cs_spec, cs_spec, bias_spec],
        out_shape=[jax.ShapeDtypeStruct((B, S, AW), BF16)] * 3 + [jax.ShapeDtypeStruct((1, AW), F32)] * 3
        + [jax.ShapeDtypeStruct((3, H, ATTN_BLOCK, 2 * ATTN_BLOCK), F32)],
        scratch_shapes=[sub_f, sub_b, pad_b, pad_b, sub_b, sub_f, sub_f, sub_f, pad_f, pad_f, sub_f, sub_f, sub_f],
        compiler_params=_params(2), name="attention_bwd",
    )(qv, qv, qv, view(do), view(lse), view(dd), bias_all)
    flat = lambda t: t.reshape(B * S, AW)
    return flat(res[0]), flat(res[1]), flat(res[2]), res[3], res[4], res[5], res[6]


def _attn_norm(attn, gain, tm):
    T, AW = attn.shape

    def body(a_ref, g_ref, mix_ref, r_ref):
        a = a_ref[...]
        r = lax.rsqrt(jnp.mean(a * a, axis=-1, keepdims=True) + LN_EPS)
        mix_ref[...] = (a * r * g_ref[...]).astype(BF16)
        r_ref[...] = jnp.broadcast_to(r, (tm, LANES))

    row = pl.BlockSpec((tm, AW), lambda i: (i, 0))
    return pl.pallas_call(
        body, grid=(T // tm,), in_specs=[row, pl.BlockSpec((1, AW), lambda i: (0, 0))],
        out_specs=[row, pl.BlockSpec((tm, LANES), lambda i: (i, 0))],
        out_shape=[jax.ShapeDtypeStruct((T, AW), BF16), jax.ShapeDtypeStruct((T, LANES), F32)],
        compiler_params=_params(1), name="attn_norm",
    )(attn, gain)


def _attn_pre_bwd(dmixed, attn, rstd, gain, tm):
    T, AW = attn.shape
    ones_np = np.kron(np.eye(AW // HEAD_DIM, dtype=np.float32), np.ones((HEAD_DIM, HEAD_DIM), np.float32))
    ones_bd = jnp.asarray(ones_np, dtype=BF16)

    def body(dm_ref, a_ref, r_ref, g_ref, ones_ref, do_ref, dd_ref, dg_ref):
        i = pl.program_id(0)
        dm = dm_ref[...]
        a = a_ref[...]
        r = r_ref[:, 0:1]
        dxn = dm * g_ref[...]
        da = r * (dxn - a * (r * r) * jnp.mean(dxn * a, axis=-1, keepdims=True))
        do_ref[...] = da.astype(BF16)
        hi, lo = _split_hi_lo(da * a)
        dd_ref[...] = (jnp.dot(hi, ones_ref[...], preferred_element_type=F32)
                       + jnp.dot(lo, ones_ref[...], preferred_element_type=F32))
        _accumulate(dg_ref, i == 0, jnp.sum(dm * a * r, axis=0, keepdims=True))

    row = pl.BlockSpec((tm, AW), lambda i: (i, 0))
    vec = pl.BlockSpec((1, AW), lambda i: (0, 0))
    return pl.pallas_call(
        body, grid=(T // tm,),
        in_specs=[row, row, pl.BlockSpec((tm, LANES), lambda i: (i, 0)), vec,
                  pl.BlockSpec((AW, AW), lambda i: (0, 0))],
        out_specs=[row, row, vec],
        out_shape=[jax.ShapeDtypeStruct((T, AW), BF16), jax.ShapeDtypeStruct((T, AW), F32),
                   jax.ShapeDtypeStruct((1, AW), F32)],
        compiler_params=_params(1), name="attn_pre_bwd",
    )(dmixed, attn, rstd, gain, ones_bd)


def _conv_branch_fwd_math(a, g, w_ref, cb, lg, lb, row):
    sg = _sigmoid(g)
    u0 = a * sg
    uc = jnp.zeros_like(u0) + cb
    for k in range(CONV_KERNEL):
        uc = uc + w_ref[k:k + 1, :] * _shift_down(u0, CONV_KERNEL - 1 - k, row)
    ul, xh, r = _ln_fwd(uc, lg, lb)
    su = _sigmoid(ul)
    u = ul * su
    return sg, u0, ul, xh, r, su, u


def _conv_fwd(ag, conv_w, conv_b, ln_g, ln_b, norm_g, B, S, CW):
    def body(a_ref, g_ref, w_ref, cb_ref, lg_ref, lb_ref, ng_ref, o_ref):
        row = lax.broadcasted_iota(jnp.int32, (S, CW), 0)
        _, _, _, _, _, _, u = _conv_branch_fwd_math(a_ref[0], g_ref[0], w_ref, cb_ref[...], lg_ref[...],
                                                    lb_ref[...], row)
        rr = lax.rsqrt(jnp.mean(u * u, axis=-1, keepdims=True) + LN_EPS)
        o_ref[0] = (u * rr * ng_ref[...]).astype(BF16)

    vec = pl.BlockSpec((1, CW), lambda b: (0, 0))
    out = pl.pallas_call(
        body, grid=(B,),
        in_specs=[pl.BlockSpec((1, S, CW), lambda b: (b, 0, 0)), pl.BlockSpec((1, S, CW), lambda b: (b, 0, 1)),
                  pl.BlockSpec((CONV_KERNEL, CW), lambda b: (0, 0)), vec, vec, vec, vec],
        out_specs=pl.BlockSpec((1, S, CW), lambda b: (b, 0, 0)),
        out_shape=jax.ShapeDtypeStruct((B, S, CW), BF16),
        compiler_params=_params(1), name="conv_fwd",
    )(ag.reshape(B, S, 2 * CW), ag.reshape(B, S, 2 * CW), conv_w, conv_b, ln_g, ln_b, norm_g)
    return out.reshape(B * S, CW)


def _conv_bwd(ag, dmixed, conv_w, conv_b, ln_g, ln_b, norm_g, B, S, CW, D):
    AW = D - CW
    assert AW % CW == 0

    def body(a_ref, g_ref, dm_ref, w_ref, cb_ref, lg_ref, lb_ref, ng_ref,
             dag_ref, dw_ref, dcb_ref, dlg_ref, dlb_ref, dng_ref):
        b = pl.program_id(0)
        row = lax.broadcasted_iota(jnp.int32, (S, CW), 0)
        a, g = a_ref[0], g_ref[0]
        sg, u0, ul, xh, r, su, u = _conv_branch_fwd_math(a, g, w_ref, cb_ref[...], lg_ref[...], lb_ref[...], row)
        rr = lax.rsqrt(jnp.mean(u * u, axis=-1, keepdims=True) + LN_EPS)
        dm = dm_ref[0]
        dxn = dm * ng_ref[...]
        du = rr * (dxn - u * (rr * rr) * jnp.mean(dxn * u, axis=-1, keepdims=True))
        dul = du * su * (1.0 + ul * (1.0 - su))
        duc = _ln_bwd(dul, xh, r, lg_ref[...])
        first = b == 0
        _accumulate(dng_ref, first, jnp.sum(dm * u * rr, axis=0, keepdims=True))
        _accumulate(dlg_ref, first, jnp.sum(dul * xh, axis=0, keepdims=True))
        _accumulate(dlb_ref, first, jnp.sum(dul, axis=0, keepdims=True))
        _accumulate(dcb_ref, first, jnp.sum(duc, axis=0, keepdims=True))

        @pl.when(first)
        def _():
            dw_ref[...] = jnp.zeros_like(dw_ref)

        du0 = jnp.zeros_like(u0)
        for k in range(CONV_KERNEL):
            sh = CONV_KERNEL - 1 - k
            dw_ref[k:k + 1, :] += jnp.sum(duc * _shift_down(u0, sh, row), axis=0, keepdims=True)
            du0 = du0 + w_ref[k:k + 1, :] * _shift_up(duc, sh, row)
        dag_ref[0, :, :CW] = du0 * sg
        dag_ref[0, :, CW:] = du0 * a * sg * (1.0 - sg)

    vec = pl.BlockSpec((1, CW), lambda b: (0, 0))
    wspec = pl.BlockSpec((CONV_KERNEL, CW), lambda b: (0, 0))
    agv = ag.reshape(B, S, 2 * CW)
    res = pl.pallas_call(
        body, grid=(B,),
        in_specs=[pl.BlockSpec((1, S, CW), lambda b: (b, 0, 0)), pl.BlockSpec((1, S, CW), lambda b: (b, 0, 1)),
                  pl.BlockSpec((1, S, CW), lambda b: (b, 0, AW // CW)), wspec, vec, vec, vec, vec],
        out_specs=[pl.BlockSpec((1, S, 2 * CW), lambda b: (b, 0, 0)), wspec, vec, vec, vec, vec],
        out_shape=[jax.ShapeDtypeStruct((B, S, 2 * CW), F32), jax.ShapeDtypeStruct((CONV_KERNEL, CW), F32)]
        + [jax.ShapeDtypeStruct((1, CW), F32)] * 4,
        compiler_params=_params(1), name="conv_bwd",
    )(agv, agv, dmixed.reshape(B, S, D), conv_w, conv_b, ln_g, ln_b, norm_g)
    return (res[0].reshape(B * S, 2 * CW),) + tuple(res[1:])


def _ffn_conv(x, w_ref, bias, row):
    y = jnp.zeros_like(x) + bias
    for k in range(FFN_CONV_KERNEL):
        y = y + w_ref[k:k + 1, :] * _shift_down(x, FFN_CONV_KERNEL - 1 - k, row)
    return y


def _ffn_specs(S, tc, nj, order):
    pick = (lambda b, j: (b, j)) if order == "bj" else (lambda j, b: (b, j))
    act = lambda off: pl.BlockSpec((1, S, tc), lambda *g: (pick(*g)[0], 0, off + pick(*g)[1]))
    cw = lambda off: pl.BlockSpec((FFN_CONV_KERNEL, tc), lambda *g: (0, off + pick(*g)[1]))
    cb = lambda off: pl.BlockSpec((1, tc), lambda *g: (0, off + pick(*g)[1]))
    return act, cw, cb


def _ffn_act(upre, cw, cb, B, S, DFF):
    tc = FFN_COLS
    nj = DFF // tc

    def body(ug_ref, uv_ref, wg_ref, wv_ref, bg_ref, bv_ref, o_ref):
        row = lax.broadcasted_iota(jnp.int32, (S, tc), 0)
        gate = _ffn_conv(ug_ref[0], wg_ref, bg_ref[...], row)
        val = _ffn_conv(uv_ref[0], wv_ref, bv_ref[...], row)
        o_ref[0] = (gate * _sigmoid(gate) * val).astype(BF16)

    act, cws, cbs = _ffn_specs(S, tc, nj, "bj")
    uv = upre.reshape(B, S, 2 * DFF)
    out = pl.pallas_call(
        body, grid=(B, nj), in_specs=[act(0), act(nj), cws(0), cws(nj), cbs(0), cbs(nj)], out_specs=act(0),
        out_shape=jax.ShapeDtypeStruct((B, S, DFF), BF16), compiler_params=_params(2), name="ffn_act",
    )(uv, uv, cw, cw, cb, cb)
    return out.reshape(B * S, DFF)


def _ffn_bwd(upre, dact, cw, cb, B, S, DFF):
    tc = FFN_COLS
    nj = DFF // tc

    def body(ug_ref, uv_ref, da_ref, wg_ref, wv_ref, bg_ref, bv_ref, dug_ref, duv_ref, dwg_ref, dwv_ref,
             dbg_ref, dbv_ref):
        first = pl.program_id(1) == 0
        row = lax.broadcasted_iota(jnp.int32, (S, tc), 0)
        ug, uv = ug_ref[0], uv_ref[0]
        gate = _ffn_conv(ug, wg_ref, bg_ref[...], row)
        val = _ffn_conv(uv, wv_ref, bv_ref[...], row)
        sg = _sigmoid(gate)
        dact_b = da_ref[0]
        dgate = dact_b * val * sg * (1.0 + gate * (1.0 - sg))
        dval = dact_b * gate * sg
        for dup, u, w_ref, du_ref, dw_ref, db_ref in ((dgate, ug, wg_ref, dug_ref, dwg_ref, dbg_ref),
                                                      (dval, uv, wv_ref, duv_ref, dwv_ref, dbv_ref)):
            _accumulate(db_ref, first, jnp.sum(dup, axis=0, keepdims=True))

            @pl.when(first)
            def _(dw_ref=dw_ref):
                dw_ref[...] = jnp.zeros_like(dw_ref)

            dupre = jnp.zeros_like(dup)
            for k in range(FFN_CONV_KERNEL):
                sh = FFN_CONV_KERNEL - 1 - k
                dw_ref[k:k + 1, :] += jnp.sum(dup * _shift_down(u, sh, row), axis=0, keepdims=True)
                dupre = dupre + w_ref[k:k + 1, :] * _shift_up(dup, sh, row)
            du_ref[0] = dupre.astype(BF16)

    act, cws, cbs = _ffn_specs(S, tc, nj, "jb")
    uv = upre.reshape(B, S, 2 * DFF)
    res = pl.pallas_call(
        body, grid=(nj, B),
        in_specs=[act(0), act(nj), act(0), cws(0), cws(nj), cbs(0), cbs(nj)],
        out_specs=[act(0), act(0), cws(0), cws(0), cbs(0), cbs(0)],
        out_shape=[jax.ShapeDtypeStruct((B, S, DFF), BF16)] * 2
        + [jax.ShapeDtypeStruct((FFN_CONV_KERNEL, DFF), F32)] * 2 + [jax.ShapeDtypeStruct((1, DFF), F32)] * 2,
        compiler_params=_params(2), name="ffn_bwd",
    )(uv, uv, dact.reshape(B, S, DFF), cw, cw, cb, cb)
    flat = lambda t: t.reshape(B * S, DFF)
    return (flat(res[0]), flat(res[1]), jnp.concatenate([res[2], res[3]], axis=1),
            jnp.concatenate([res[4], res[5]], axis=1))


def _dh_cat(dq, dk, dv, dag, tm):
    T, AW = dq.shape
    CW2 = dag.shape[1]
    W = 3 * AW + CW2

    def body(dq_ref, dk_ref, dv_ref, dag_ref, dh_ref, cs_ref):
        for c, ref in enumerate((dq_ref, dk_ref, dv_ref)):
            dh_ref[:, c * AW:(c + 1) * AW] = ref[...]
        dg = dag_ref[...]
        dh_ref[:, 3 * AW:] = dg.astype(BF16)
        _accumulate(cs_ref, pl.program_id(0) == 0, jnp.sum(dg, axis=0, keepdims=True))

    row = pl.BlockSpec((tm, AW), lambda i: (i, 0))
    return pl.pallas_call(
        body, grid=(T // tm,),
        in_specs=[row] * 3 + [pl.BlockSpec((tm, CW2), lambda i: (i, 0))],
        out_specs=[pl.BlockSpec((tm, W), lambda i: (i, 0)), pl.BlockSpec((1, CW2), lambda i: (0, 0))],
        out_shape=[jax.ShapeDtypeStruct((T, W), BF16), jax.ShapeDtypeStruct((1, CW2), F32)],
        compiler_params=_params(1), name="dh_cat",
    )(dq, dk, dv, dag)


def _local_step(x, target, rel_table, w_in, b_in, conv_w, conv_b, conv_ln_g, conv_ln_b, attn_norm_g,
                conv_norm_g, w_out, ln1_g, ln1_b, w_up_sh, ffn_cw, ffn_cb, w_down, ln2_g, ln2_b):
    B, S, D = x.shape
    T = B * S
    AW = attn_norm_g.shape[-1]
    CW = conv_norm_g.shape[-1]
    H = AW // HEAD_DIM
    DFF = w_down.shape[0]
    INW = 3 * AW + 2 * CW
    xf = x.reshape(T, D)
    tf = target.reshape(T, D)
    tm = _row_tile(T, 512)
    tm_s = _row_tile(T, 256)

    bucket_np, mask_np = _bucket_tables()
    bucket = jnp.asarray(bucket_np)
    band_mask = jnp.asarray(mask_np)
    bias_all = _bias_build(rel_table.T, bucket, band_mask).reshape(3, H, ATTN_BLOCK, 2 * ATTN_BLOCK)

    tn_qkv = _col_tile(3 * AW, 1152)
    qkv = _mm_plain(xf, w_in[:, :3 * AW], mode="nn", tm=tm, tn=tn_qkv, tk=D, out_dtype=BF16,
                    bias=b_in[:, :3 * AW], name="mm_qkv")
    ag = _mm_plain(xf, w_in[:, 3 * AW:], mode="nn", tm=tm, tn=2 * CW, tk=D, out_dtype=F32,
                   bias=b_in[:, 3 * AW:], name="mm_ag")

    attn, lse = _attention_fwd(qkv, bias_all, B, S, AW)
    mixed_a, r_attn = _attn_norm(attn, attn_norm_g, tm_s)
    mixed_c = _conv_fwd(ag, conv_w, conv_b, conv_ln_g, conv_ln_b, conv_norm_g, B, S, CW)
    mixed = jnp.concatenate([mixed_a, mixed_c], axis=1)

    def ln1_epilogue(acc, i, j, extra_refs, out_refs):
        x_ref, g_ref, b_ref = extra_refs
        x1, xh, r = _ln_fwd(acc + ALPHA * x_ref[...], g_ref[...], b_ref[...])
        out_refs[0][...] = x1
        out_refs[1][...] = x1.astype(BF16)
        out_refs[2][...] = xh
        out_refs[3][...] = jnp.broadcast_to(r, (tm_s, LANES))

    rowD = lambda i, j, k: (i, 0)
    vecD = lambda i, j, k: (0, 0)
    x1, x1b, xh1, r1 = _matmul(
        mixed, w_out, mode="nn", tm=tm_s, tn=D, tk=D,
        extras=[(xf, (tm_s, D), rowD), (ln1_g, (1, D), vecD), (ln1_b, (1, D), vecD)],
        outs=[((T, D), F32, (tm_s, D), rowD), ((T, D), BF16, (tm_s, D), rowD), ((T, D), F32, (tm_s, D), rowD),
              ((T, LANES), F32, (tm_s, LANES), rowD)],
        epilogue=ln1_epilogue, name="mm_out_ln1")

    NS, _, cs = w_up_sh.shape
    half = NS // 2

    def up_epilogue(acc, i, j, extra_refs, out_refs):
        out_refs[0][...] = acc

    upre = _matmul_general(
        [(x1b, (tm, D), lambda i, j, k: (i, 0)), (w_up_sh, (1, D, cs), lambda i, j, k: (j, 0, 0))],
        lambda refs, i, j, k: _dot(refs[0][...], refs[1][0], "nn"),
        grid=(T // tm, NS, 1), tm=tm, tn=cs, outs=[_plain_out(T, 2 * DFF, tm, cs, F32)],
        epilogue=up_epilogue, name="mm_up")[0]
    act = _ffn_act(upre, ffn_cw, ffn_cb, B, S, DFF)

    def ln2_epilogue(acc, i, j, extra_refs, out_refs):
        x1_ref, g_ref, b_ref, t_ref = extra_refs
        dz_ref, dzb_ref, loss_ref, dg_ref, db_ref = out_refs
        g = g_ref[...]
        y, xh, r = _ln_fwd(acc + ALPHA * x1_ref[...], g, b_ref[...])
        diff = y - t_ref[...]
        row_loss = jnp.sum(diff * diff, axis=1, keepdims=True)
        tile_loss = jnp.sum(row_loss, axis=0, keepdims=True) * (0.5 / D)
        dy = diff * (1.0 / D)
        dz = _ln_bwd(dy, xh, r, g)
        dz_ref[...] = dz
        dzb_ref[...] = dz.astype(BF16)
        first = i == 0
        _accumulate(loss_ref, first, jnp.broadcast_to(tile_loss, (1, LANES)))
        _accumulate(dg_ref, first, jnp.sum(dy * xh, axis=0, keepdims=True))
        _accumulate(db_ref, first, jnp.sum(dy, axis=0, keepdims=True))

    dz2, dz2b, loss_part, d_ln2_g, d_ln2_b = _matmul(
        act, w_down, mode="nn", tm=tm_s, tn=D, tk=DFF,
        extras=[(x1, (tm_s, D), rowD), (ln2_g, (1, D), vecD), (ln2_b, (1, D), vecD), (tf, (tm_s, D), rowD)],
        outs=[((T, D), F32, (tm_s, D), rowD), ((T, D), BF16, (tm_s, D), rowD),
              ((1, LANES), F32, (1, LANES), vecD), ((1, D), F32, (1, D), vecD), ((1, D), F32, (1, D), vecD)],
        epilogue=ln2_epilogue, name="mm_down_ln2_loss")

    tn_dff = _col_tile(DFF, 1408)
    dact = _mm_plain(dz2b, w_down, mode="nt", tm=tm, tn=tn_dff, tk=D, out_dtype=F32, name="mm_dact")
    dupre_g, dupre_v, d_ffn_cw, d_ffn_cb = _ffn_bwd(upre, dact, ffn_cw, ffn_cb, B, S, DFF)
    tk_t = _row_tile(T, 512)
    d_w_down = _mm_plain(act, dz2b, mode="tn", tm=tn_dff, tn=D, tk=tk_t, out_dtype=F32, name="mm_dw_down")

    def dw_up_epilogue(acc, i, j, extra_refs, out_refs):
        out_refs[0][0] = acc

    d_w_up_sh = _matmul_general(
        [(x1b, (tk_t, D), lambda i, j, k: (k, 0)),
         (dupre_g, (tk_t, cs), lambda i, j, k: (jnp.where(j < half, k, 0), jnp.minimum(j, half - 1))),
         (dupre_v, (tk_t, cs), lambda i, j, k: (jnp.where(j < half, 0, k), jnp.maximum(j - half, 0)))],
        lambda refs, i, j, k: _dot(refs[0][...], jnp.where(j < half, refs[1][...], refs[2][...]), "tn"),
        grid=(1, NS, T // tk_t), tm=D, tn=cs,
        outs=[((NS, D, cs), F32, (1, D, cs), lambda i, j, k: (j, 0, 0))],
        epilogue=dw_up_epilogue, name="mm_dw_up")[0]

    def ln1_bwd_epilogue(acc, i, j, extra_refs, out_refs):
        dz2_ref, xh_ref, r_ref, g_ref = extra_refs
        dz_ref, dzb_ref, dg_ref, db_ref = out_refs
        dx1 = acc + ALPHA * dz2_ref[...]
        xh = xh_ref[...]
        dz = _ln_bwd(dx1, xh, r_ref[:, 0:1], g_ref[...])
        dz_ref[...] = dz
        dzb_ref[...] = dz.astype(BF16)
        first = i == 0
        _accumulate(dg_ref, first, jnp.sum(dx1 * xh, axis=0, keepdims=True))
        _accumulate(db_ref, first, jnp.sum(dx1, axis=0, keepdims=True))

    dz1, dz1b, d_ln1_g, d_ln1_b = _matmul_general(
        [(dupre_g, (tm_s, cs), lambda i, j, k: (i, jnp.minimum(k, half - 1))),
         (dupre_v, (tm_s, cs), lambda i, j, k: (i, jnp.maximum(k - half, 0))),
         (w_up_sh, (1, D, cs), lambda i, j, k: (k, 0, 0))],
        lambda refs, i, j, k: _dot(jnp.where(k < half, refs[0][...], refs[1][...]), refs[2][0], "nt"),
        grid=(T // tm_s, 1, NS), tm=tm_s, tn=D,
        extras=[(dz2, (tm_s, D), rowD), (xh1, (tm_s, D), rowD), (r1, (tm_s, LANES), rowD), (ln1_g, (1, D), vecD)],
        outs=[((T, D), F32, (tm_s, D), rowD), ((T, D), BF16, (tm_s, D), rowD),
              ((1, D), F32, (1, D), vecD), ((1, D), F32, (1, D), vecD)],
        epilogue=ln1_bwd_epilogue, name="mm_dx1_ln1_bwd")

    dmixed = _mm_plain(dz1b, w_out, mode="nt", tm=tm, tn=D, tk=D, out_dtype=F32, name="mm_dmixed")
    d_w_out = _mm_plain(mixed, dz1b, mode="tn", tm=D, tn=D, tk=tk_t, out_dtype=F32, name="mm_dw_out")

    dattn, dd, d_attn_norm_g = _attn_pre_bwd(dmixed, attn, r_attn, attn_norm_g, tm_s)
    dag, d_conv_w, d_conv_b, d_conv_ln_g, d_conv_ln_b, d_conv_norm_g = _conv_bwd(
        ag, dmixed, conv_w, conv_b, conv_ln_g, conv_ln_b, conv_norm_g, B, S, CW, D)

    dq, dk, dv, csq, csk, csv, dbias = _attention_bwd(qkv, dattn, lse, dd, bias_all, B, S, AW)
    d_rel_table = _rel_grad(dbias.reshape(3, H, ATTN_BLOCK * 2 * ATTN_BLOCK), bucket).T
    dh, cs_ag = _dh_cat(dq, dk, dv, dag, tm_s)
    d_b_in = jnp.concatenate([csq, csk, csv, cs_ag], axis=1)

    def gx_epilogue(acc, i, j, extra_refs, out_refs):
        out_refs[0][...] = acc + ALPHA * extra_refs[0][...]

    grad_x = _matmul(dh, w_in, mode="nt", tm=tm_s, tn=D, tk=INW,
                     extras=[(dz1, (tm_s, D), rowD)], outs=[((T, D), F32, (tm_s, D), rowD)],
                     epilogue=gx_epilogue, name="mm_grad_x")[0]
    d_w_in = _mm_plain(xf, dh, mode="tn", tm=D, tn=_col_tile(INW, 1408), tk=tk_t, out_dtype=F32, name="mm_dw_in")

    grads = dict(rel_table=d_rel_table, w_in=d_w_in, b_in=d_b_in, conv_w=d_conv_w, conv_b=d_conv_b,
                 conv_ln_g=d_conv_ln_g, conv_ln_b=d_conv_ln_b, attn_norm_g=d_attn_norm_g,
                 conv_norm_g=d_conv_norm_g, w_out=d_w_out, ln1_g=d_ln1_g, ln1_b=d_ln1_b, w_up_sh=d_w_up_sh,
                 ffn_conv_w=d_ffn_cw, ffn_conv_b=d_ffn_cb, w_down=d_w_down,
                 ln2_g=d_ln2_g, ln2_b=d_ln2_b)
    return loss_part, grad_x.reshape(B, S, D), grads


def _place():
    return lax.axis_index("x"), lax.axis_index("y"), lax.axis_index("c")


CHIP_FLIPS = ((1, 0), (0, 1), (1, 1))


def _flip(v, f):
    return 1 - v if f else v


HBM_SPEC = pl.BlockSpec(memory_space=pl.ANY)
VMEM_SPEC = pl.BlockSpec(memory_space=pltpu.VMEM)
COMM_PARAMS = pltpu.CompilerParams(vmem_limit_bytes=VMEM_LIMIT)


def _gather_weights(big, small):
    nb, ns = len(big), len(small)

    def body(*refs):
        big_in = refs[:nb]
        small_in = refs[nb:nb + ns]
        big_out = refs[nb + ns:2 * nb + ns]
        small_out = refs[2 * nb + ns:2 * nb + 2 * ns]
        stages = refs[2 * nb + 2 * ns:3 * nb + 2 * ns]
        send_sems, recv_sems, local_sems = refs[3 * nb + 2 * ns:]
        x, y, c = _place()
        s_me = 2 * x + y
        sibling = (x, y, 1 - c)
        started, local_copies = [], []
        for a in range(nb):
            rh = big[a].shape[0] // 2
            lo = pl.multiple_of(c * rh, 16)
            stages[a][...] = big_in[a][pl.ds(lo, rh), :].astype(BF16)
            mine = big_out[a].at[s_me, pl.ds(lo, rh), :]
            loc = pltpu.make_async_copy(stages[a], mine, local_sems.at[a])
            loc.start()
            local_copies.append(loc)
            targets = [sibling] + [(_flip(x, fx), _flip(y, fy), c) for fx, fy in CHIP_FLIPS]
            for k, to in enumerate(targets):
                cp = pltpu.make_async_remote_copy(stages[a], mine, send_sems.at[a * 7 + k],
                                                  recv_sems.at[a * 7 + k], device_id=to, device_id_type=MESH)
                cp.start()
                started.append(cp)
        for a in range(ns):
            mine = small_out[a].at[s_me]
            loc = pltpu.make_async_copy(small_in[a], mine, local_sems.at[nb + a])
            loc.start()
            local_copies.append(loc)
            for k, (fx, fy) in enumerate(CHIP_FLIPS):
                cp = pltpu.make_async_remote_copy(small_in[a], mine, send_sems.at[nb * 7 + a * 3 + k],
                                                  recv_sems.at[nb * 7 + a * 3 + k],
                                                  device_id=(_flip(x, fx), _flip(y, fy), c), device_id_type=MESH)
                cp.start()
                started.append(cp)
        for a in range(nb):
            rh = big[a].shape[0] // 2
            lo = pl.multiple_of(c * rh, 16)
            for k, (fx, fy) in enumerate(CHIP_FLIPS):
                s_from = 2 * _flip(x, fx) + _flip(y, fy)
                got = big_out[a].at[s_from, pl.ds(lo, rh), :]
                pltpu.make_async_remote_copy(got, got, send_sems.at[a * 7 + 1 + k], recv_sems.at[a * 7 + 1 + k],
                                             device_id=sibling, device_id_type=MESH).wait_recv()
                fwd = pltpu.make_async_remote_copy(got, got, send_sems.at[a * 7 + 4 + k],
                                                   recv_sems.at[a * 7 + 4 + k], device_id=sibling,
                                                   device_id_type=MESH)
                fwd.start()
                started.append(fwd)
        for a in range(nb):
            rh = big[a].shape[0] // 2
            lo_sib = pl.multiple_of((1 - c) * rh, 16)
            for k in (0, 4, 5, 6):
                any_rows = big_out[a].at[s_me, pl.ds(lo_sib, rh), :]
                pltpu.make_async_remote_copy(any_rows, any_rows, send_sems.at[a * 7 + k], recv_sems.at[a * 7 + k],
                                             device_id=sibling, device_id_type=MESH).wait_recv()
        for a in range(ns):
            for k in range(3):
                pltpu.make_async_remote_copy(small_in[a], small_out[a].at[s_me], send_sems.at[nb * 7 + a * 3 + k],
                                             recv_sems.at[nb * 7 + a * 3 + k], device_id=sibling,
                                             device_id_type=MESH).wait_recv()
        for cp in started:
            cp.wait_send()
        for cp in local_copies:
            cp.wait()

    n_sem = nb * 7 + ns * 3
    out_shape = ([jax.ShapeDtypeStruct((N_SHARDS,) + w.shape, BF16) for w in big]
                 + [jax.ShapeDtypeStruct((N_SHARDS,) + w.shape, F32) for w in small])
    res = pl.pallas_call(
        body, in_specs=[VMEM_SPEC] * nb + [HBM_SPEC] * ns, out_specs=[HBM_SPEC] * (nb + ns),
        out_shape=out_shape,
        scratch_shapes=[pltpu.VMEM((w.shape[0] // 2, w.shape[1]), BF16) for w in big]
        + [pltpu.SemaphoreType.DMA((n_sem,)), pltpu.SemaphoreType.DMA((n_sem,)),
           pltpu.SemaphoreType.DMA((nb + ns,))],
        compiler_params=COMM_PARAMS, name="gather_weights",
    )(*big, *small)
    return res[:nb], res[nb:]


def _sibling_exchange(grads):
    n = len(grads)

    def body(*refs):
        g_in = refs[:n]
        got = refs[n:2 * n]
        send_sems, recv_sems = refs[2 * n:]
        x, y, c = _place()
        cps = []
        for a in range(n):
            rh = grads[a].shape[1] // 2
            lo = pl.multiple_of((1 - c) * rh, 8)
            cp = pltpu.make_async_remote_copy(g_in[a].at[:, pl.ds(lo, rh), :], got[a], send_sems.at[a],
                                              recv_sems.at[a], device_id=(x, y, 1 - c), device_id_type=MESH)
            cp.start()
            cps.append(cp)
        for cp in cps:
            cp.wait()

    return pl.pallas_call(
        body, in_specs=[HBM_SPEC] * n, out_specs=[HBM_SPEC] * n,
        out_shape=[jax.ShapeDtypeStruct((N_SHARDS, g.shape[1] // 2, g.shape[2]), F32) for g in grads],
        scratch_shapes=[pltpu.SemaphoreType.DMA((n,)), pltpu.SemaphoreType.DMA((n,))],
        compiler_params=COMM_PARAMS, name="sibling_exchange",
    )(*grads)


def _chip_exchange(chip_parts, pack):
    n = len(chip_parts)

    def body(*refs):
        parts = refs[:n]
        pack_ref = refs[n]
        got = refs[n + 1:2 * n + 1]
        all_packs = refs[2 * n + 1]
        send_sems, recv_sems, local_sem = refs[2 * n + 2:]
        x, y, c = _place()
        me = 4 * x + 2 * y + c
        cps = []
        for a in range(n):
            for k, (fx, fy) in enumerate(CHIP_FLIPS):
                px, py = _flip(x, fx), _flip(y, fy)
                cp = pltpu.make_async_remote_copy(parts[a].at[2 * px + py], got[a].at[k], send_sems.at[a * 3 + k],
                                                  recv_sems.at[a * 3 + k], device_id=(px, py, c),
                                                  device_id_type=MESH)
                cp.start()
                cps.append(cp)
        loc = pltpu.make_async_copy(pack_ref, all_packs.at[me], local_sem)
        loc.start()
        for m in range(1, N_DEV):
            to = (_flip(x, m & 4), _flip(y, m & 2), _flip(c, m & 1))
            cp = pltpu.make_async_remote_copy(pack_ref, all_packs.at[me], send_sems.at[n * 3 + m - 1],
                                              recv_sems.at[n * 3 + m - 1], device_id=to, device_id_type=MESH)
            cp.start()
            cps.append(cp)
        for cp in cps:
            cp.wait()
        loc.wait()

    rs = pack.shape[0]
    res = pl.pallas_call(
        body, in_specs=[HBM_SPEC] * (n + 1), out_specs=[HBM_SPEC] * (n + 1),
        out_shape=[jax.ShapeDtypeStruct((3,) + p.shape[1:], BF16) for p in chip_parts]
        + [jax.ShapeDtypeStruct((N_DEV, rs, LANES), F32)],
        scratch_shapes=[pltpu.SemaphoreType.DMA((n * 3 + N_DEV - 1,)), pltpu.SemaphoreType.DMA((n * 3 + N_DEV - 1,)),
                        pltpu.SemaphoreType.DMA],
        compiler_params=COMM_PARAMS, name="chip_exchange",
    )(*chip_parts, pack)
    return res[:n], res[n]


def _sibling_assemble(fulls):
    n = len(fulls)

    def body(*refs):
        full = refs[n:2 * n]
        send_sems, recv_sems = refs[2 * n:]
        x, y, c = _place()
        cps = []
        for a in range(n):
            rh = fulls[a].shape[0] // 2
            mine = full[a].at[pl.ds(pl.multiple_of(c * rh, 8), rh), :]
            cp = pltpu.make_async_remote_copy(mine, mine, send_sems.at[a], recv_sems.at[a],
                                              device_id=(x, y, 1 - c), device_id_type=MESH)
            cp.start()
            cps.append(cp)
        for cp in cps:
            cp.wait()

    return pl.pallas_call(
        body, in_specs=[HBM_SPEC] * n, out_specs=[HBM_SPEC] * n,
        out_shape=[jax.ShapeDtypeStruct(f.shape, F32) for f in fulls],
        input_output_aliases={a: a for a in range(n)},
        scratch_shapes=[pltpu.SemaphoreType.DMA((n,)), pltpu.SemaphoreType.DMA((n,))],
        compiler_params=COMM_PARAMS, name="sibling_assemble",
    )(*fulls)


def _half_tile(rh, mult=16, want=256):
    best = None
    for t in range(mult, min(rh, want) + 1, mult):
        if rh % t == 0:
            best = t
    return best if best is not None else rh


def _pair_sum(g, sib, ids, name):
    _, R, C = g.shape
    rh = R // 2
    rt = _half_tile(rh)
    nt = rh // rt

    def body(ids_ref, g_ref, s_ref, o_ref):
        o_ref[...] = (g_ref[...] + s_ref[...]).astype(BF16)

    grid_spec = pltpu.PrefetchScalarGridSpec(
        num_scalar_prefetch=1, grid=(N_SHARDS, nt),
        in_specs=[pl.BlockSpec((1, rt, C), lambda s, i, ids: (s, ids[2] * nt + i, 0)),
                  pl.BlockSpec((1, rt, C), lambda s, i, ids: (s, i, 0))],
        out_specs=pl.BlockSpec((1, rt, C), lambda s, i, ids: (s, i, 0)))
    return pl.pallas_call(body, grid_spec=grid_spec, out_shape=jax.ShapeDtypeStruct((N_SHARDS, rh, C), BF16),
                          compiler_params=_params(2), name=name)(ids, g, sib)


def _final_sum(g, sib, got, ids, name):
    _, R, C = g.shape
    rh = R // 2
    rt = _half_tile(rh)
    nt = rh // rt

    def body(ids_ref, g_ref, s_ref, r_ref, o_ref):
        tot = g_ref[0] + s_ref[0]
        for k in range(3):
            tot = tot + r_ref[k].astype(F32)
        o_ref[...] = tot

    grid_spec = pltpu.PrefetchScalarGridSpec(
        num_scalar_prefetch=1, grid=(nt,),
        in_specs=[pl.BlockSpec((1, rt, C), lambda i, ids: (2 * ids[0] + ids[1], ids[2] * nt + i, 0)),
                  pl.BlockSpec((1, rt, C), lambda i, ids: (2 * ids[0] + ids[1], i, 0)),
                  pl.BlockSpec((3, rt, C), lambda i, ids: (0, i, 0))],
        out_specs=pl.BlockSpec((rt, C), lambda i, ids: (ids[2] * nt + i, 0)))
    return pl.pallas_call(body, grid_spec=grid_spec, out_shape=jax.ShapeDtypeStruct((R, C), F32),
                          compiler_params=_params(1), name=name)(ids, g, sib, got)


def _sum_packs(all_packs):
    def body(p_ref, o_ref):
        tot = p_ref[0]
        for i in range(1, N_DEV):
            tot = tot + p_ref[i]
        o_ref[...] = tot

    return pl.pallas_call(body, in_specs=[VMEM_SPEC], out_specs=VMEM_SPEC,
                          out_shape=jax.ShapeDtypeStruct(all_packs.shape[1:], F32), name="sum_packs")(all_packs)


def _adamw(w, g, m, v, name):
    R, C = w.shape
    rt = _half_tile(R, mult=8, want=256)

    def body(w_ref, g_ref, m_ref, v_ref, d_ref, nm_ref, nv_ref):
        gg = g_ref[...]
        nm = ADAM_B1 * m_ref[...] + (1.0 - ADAM_B1) * gg
        nv = ADAM_B2 * v_ref[...] + (1.0 - ADAM_B2) * (gg * gg)
        m_hat = nm / (1.0 - ADAM_B1 ** ADAM_STEP)
        v_hat = nv / (1.0 - ADAM_B2 ** ADAM_STEP)
        d_ref[...] = -ADAM_LR * (m_hat / (jnp.sqrt(v_hat) + ADAM_EPS) + ADAM_WD * w_ref[...])
        nm_ref[...] = nm
        nv_ref[...] = nv

    spec = pl.BlockSpec((rt, C), lambda i: (i, 0))
    return pl.pallas_call(body, grid=(R // rt,), in_specs=[spec] * 4, out_specs=[spec] * 3,
                          out_shape=[jax.ShapeDtypeStruct((R, C), F32)] * 3,
                          compiler_params=_params(1), name=name)(w, g, m, v)


def _pack(pieces):
    rows = []
    for p in pieces:
        flat = p.reshape(-1)
        pad = (-flat.shape[0]) % LANES
        if pad:
            flat = jnp.concatenate([flat, jnp.zeros((pad,), F32)])
        rows.append(flat.reshape(-1, LANES))
    total = sum(r.shape[0] for r in rows)
    pad_rows = (-total) % 8
    if pad_rows:
        rows.append(jnp.zeros((pad_rows, LANES), F32))
    return jnp.concatenate(rows, axis=0)


def _unpack(buf, shapes):
    out, r0 = [], 0
    for shp in shapes:
        n = int(np.prod(shp))
        nr = -(-n // LANES)
        out.append(buf[r0:r0 + nr].reshape(-1)[:n].reshape(shp))
        r0 += nr
    return out


SMALL_NAMES = ("rel_table", "b_in", "conv_w", "conv_b", "conv_ln_g", "conv_ln_b", "attn_norm_g", "conv_norm_g",
               "ln1_g", "ln1_b", "ffn_conv_w", "ffn_conv_b", "ln2_g", "ln2_b")
BIG_NAMES = ("w_in", "w_out", "w_up", "w_down")
WEIGHT_ORDER = ("rel_table", "w_in", "b_in", "conv_w", "conv_b", "conv_ln_g", "conv_ln_b", "attn_norm_g",
                "conv_norm_g", "w_out", "ln1_g", "ln1_b", "w_up", "ffn_conv_w", "ffn_conv_b", "w_down",
                "ln2_g", "ln2_b")


def kernel(x, rel_table, w_in, b_in, conv_w, conv_b, conv_ln_g, conv_ln_b, attn_norm_g, conv_norm_g, w_out, ln1_g, ln1_b, w_up, ffn_conv_w, ffn_conv_b, w_down, ln2_g, ln2_b, loss_target, m_rel_table, m_w_in, m_b_in, m_conv_w, m_conv_b, m_conv_ln_g, m_conv_ln_b, m_attn_norm_g, m_conv_norm_g, m_w_out, m_ln1_g, m_ln1_b, m_w_up, m_ffn_conv_w, m_ffn_conv_b, m_w_down, m_ln2_g, m_ln2_b, v_rel_table, v_w_in, v_b_in, v_conv_w, v_conv_b, v_conv_ln_g, v_conv_ln_b, v_attn_norm_g, v_conv_norm_g, v_w_out, v_ln1_g, v_ln1_b, v_w_up, v_ffn_conv_w, v_ffn_conv_b, v_w_down, v_ln2_g, v_ln2_b):
    args = dict(locals())
    weights = {n: args[n] for n in WEIGHT_ORDER}
    moms = {n: args["m_" + n] for n in WEIGHT_ORDER}
    vels = {n: args["v_" + n] for n in WEIGHT_ORDER}
    xi, yi, ci = _place()
    ids = jnp.stack([xi, yi, ci]).astype(jnp.int32)
    shard = 2 * xi + yi
    D = x.shape[-1]
    DFF = w_down.shape[1] * N_SHARDS
    CW = conv_norm_g.shape[-1]

    (g_in, g_out, g_up, g_down), (g_cw, g_fcw) = _gather_weights(
        [w_in[0], w_out[0], w_up[0], w_down[0]], [conv_w[0], ffn_conv_w[0]])
    cols = lambda t: jnp.transpose(t, (1, 0, 2)).reshape(t.shape[1], N_SHARDS * t.shape[2])
    w_in_f = cols(g_in)
    w_out_f = g_out.reshape(D, D)
    w_down_f = g_down.reshape(DFF, D)
    conv_w_f = cols(g_cw)
    ffn_cw_f = cols(g_fcw)

    loss_part, grad_x, gl = _local_step(
        x, loss_target, rel_table, w_in_f, b_in, conv_w_f, conv_b, conv_ln_g, conv_ln_b, attn_norm_g, conv_norm_g,
        w_out_f, ln1_g, ln1_b, g_up, ffn_cw_f, ffn_conv_b, w_down_f, ln2_g, ln2_b)

    rows = lambda t: jnp.transpose(t.reshape(t.shape[0], N_SHARDS, t.shape[1] // N_SHARDS), (1, 0, 2))
    big_parts = [rows(gl["w_in"]), gl["w_out"].reshape(N_SHARDS, D // N_SHARDS, D),
                 gl["w_up_sh"], gl["w_down"].reshape(N_SHARDS, DFF // N_SHARDS, D)]
    sib = _sibling_exchange(big_parts)
    chip_parts = [_pair_sum(g, s, ids, name="pair_sum_" + n) for g, s, n in zip(big_parts, sib, BIG_NAMES)]

    pack = _pack([loss_part] + [gl[n] for n in SMALL_NAMES])
    got, all_packs = _chip_exchange(chip_parts, pack)
    fulls = [_final_sum(g, s, r, ids, name="final_sum_" + n)
             for g, s, r, n in zip(big_parts, sib, got, BIG_NAMES)]
    big_grads = dict(zip(BIG_NAMES, _sibling_assemble(fulls)))

    summed = _sum_packs(all_packs)
    full_shapes = {n: weights[n].shape for n in SMALL_NAMES}
    full_shapes["conv_w"] = (1, CONV_KERNEL, CW)
    full_shapes["ffn_conv_w"] = (1, FFN_CONV_KERNEL, 2 * DFF)
    un = _unpack(summed, [(1, LANES)] + [full_shapes[n] for n in SMALL_NAMES])
    loss = un[0][0, 0]
    small_grads = dict(zip(SMALL_NAMES, un[1:]))
    for n in ("conv_w", "ffn_conv_w"):
        width = weights[n].shape[-1]
        small_grads[n] = lax.dynamic_slice_in_dim(small_grads[n], shard * width, width, axis=2)

    grads, delta, new_m, new_v = {}, {}, {}, {}
    for n in BIG_NAMES:
        shp = weights[n].shape
        g2 = big_grads[n]
        d, nm, nv = _adamw(weights[n][0], g2, moms[n][0], vels[n][0], name="adamw_" + n)
        grads[n], delta[n], new_m[n], new_v[n] = (t.reshape(shp) for t in (g2, d, nm, nv))
    sp = lambda src: _pack([src[n] for n in SMALL_NAMES])
    d_s, nm_s, nv_s = _adamw(sp(weights), sp(small_grads), sp(moms), sp(vels), name="adamw_small")
    shapes = [weights[n].shape for n in SMALL_NAMES]
    for tgt, buf in ((delta, d_s), (new_m, nm_s), (new_v, nv_s)):
        tgt.update(zip(SMALL_NAMES, _unpack(buf, shapes)))
    grads.update(small_grads)

    return (loss, grad_x, *[grads[n] for n in WEIGHT_ORDER], *[delta[n] for n in WEIGHT_ORDER],
            *[new_m[n] for n in WEIGHT_ORDER], *[new_v[n] for n in WEIGHT_ORDER])
```

```python
import functools
import math

import numpy as np
import jax
import jax.numpy as jnp
from jax import lax
from jax.experimental import pallas as pl
from jax.experimental.pallas import tpu as pltpu

F32 = jnp.float32
BF16 = jnp.bfloat16
MESH = pl.DeviceIdType.MESH

HEAD_DIM = 64
LANES = 128
ATTN_BLOCK = 128
DILATED_CONFIGS = ((128, 1), (512, 4), (2048, 16))
CONV_KERNEL = 31
FFN_CONV_KERNEL = 3
REL_BUCKETS = 32
REL_MAX_DIST = 2048
DEPTH = 1
ALPHA = (2 * DEPTH) ** 0.25
LN_EPS = 1e-5
NEG_INF = -1e30
QK_SCALE = 1.0 / math.sqrt(HEAD_DIM)
ADAM_LR = 0.001
ADAM_B1 = 0.9
ADAM_B2 = 0.999
ADAM_EPS = 1e-08
ADAM_WD = 0.01
ADAM_STEP = 10
VMEM_LIMIT = 52 * 1024 * 1024
FFN_COLS = 128
N_SHARDS = 4
N_DEV = 8


def _params(n_axes):
    return pltpu.CompilerParams(dimension_semantics=("arbitrary",) * n_axes,
                                vmem_limit_bytes=VMEM_LIMIT)


MM_DIMS = {"nn": (((1,), (0,)), ((), ())), "nt": (((1,), (1,)), ((), ())), "tn": (((0,), (0,)), ((), ()))}


def _matmul_general(ins, part_fn, *, grid, tm, tn, outs, epilogue, extras=(), name):
    nk = grid[2]
    n_in, n_extra = len(ins), len(extras)

    def body(*refs):
        in_refs = refs[:n_in]
        rest = refs[n_in:]
        extra_refs = rest[:n_extra]
        out_refs = rest[n_extra:n_extra + len(outs)]
        acc_ref = rest[-1]
        i, j, k = pl.program_id(0), pl.program_id(1), pl.program_id(2)
        part = part_fn(in_refs, i, j, k)
        if nk == 1:
            epilogue(part, i, j, extra_refs, out_refs)
        else:
            @pl.when(k == 0)
            def _():
                acc_ref[...] = part

            @pl.when(k > 0)
            def _():
                acc_ref[...] += part

            @pl.when(k == nk - 1)
            def _():
                epilogue(acc_ref[...], i, j, extra_refs, out_refs)

    in_specs = [pl.BlockSpec(bs, im) for (_, bs, im) in list(ins) + list(extras)]
    out_specs = [pl.BlockSpec(bs, im) for (_, _, bs, im) in outs]
    out_shape = [jax.ShapeDtypeStruct(s, d) for (s, d, _, _) in outs]
    return pl.pallas_call(
        body, grid=grid, in_specs=in_specs, out_specs=out_specs,
        out_shape=out_shape, scratch_shapes=[pltpu.VMEM((tm, tn), F32)],
        compiler_params=_params(3), name=name,
    )(*[e[0] for e in ins], *[e[0] for e in extras])


def _dot(a, b, mode):
    return lax.dot_general(a.astype(BF16), b.astype(BF16), MM_DIMS[mode], preferred_element_type=F32)


def _matmul(a, b, *, mode, tm, tn, tk, outs, epilogue, extras=(), name):
    if mode == "tn":
        K, M = a.shape
        N = b.shape[1]
        ins = [(a, (tk, tm), lambda i, j, k: (k, i)), (b, (tk, tn), lambda i, j, k: (k, j))]
    elif mode == "nt":
        M, K = a.shape
        N = b.shape[0]
        ins = [(a, (tm, tk), lambda i, j, k: (i, k)), (b, (tn, tk), lambda i, j, k: (j, k))]
    else:
        M, K = a.shape
        N = b.shape[1]
        ins = [(a, (tm, tk), lambda i, j, k: (i, k)), (b, (tk, tn), lambda i, j, k: (k, j))]
    assert M % tm == 0 and N % tn == 0 and K % tk == 0, (name, M, N, K, tm, tn, tk)

    def part_fn(in_refs, i, j, k):
        return _dot(in_refs[0][...], in_refs[1][...], mode)

    return _matmul_general(ins, part_fn, grid=(M // tm, N // tn, K // tk), tm=tm, tn=tn, outs=outs,
                           epilogue=epilogue, extras=extras, name=name)


def _plain_out(M, N, tm, tn, dtype):
    return ((M, N), dtype, (tm, tn), lambda i, j, k: (i, j))


def _mm_plain(a, b, *, mode, tm, tn, tk, out_dtype, name, bias=None):
    if mode == "tn":
        M, N = a.shape[1], b.shape[1]
    elif mode == "nt":
        M, N = a.shape[0], b.shape[0]
    else:
        M, N = a.shape[0], b.shape[1]
    extras = []
    if bias is not None:
        extras.append((bias, (1, tn), lambda i, j, k: (0, j)))

    def epilogue(acc, i, j, extra_refs, out_refs):
        if bias is not None:
            acc = acc + extra_refs[0][...]
        out_refs[0][...] = acc.astype(out_dtype)

    return _matmul(a, b, mode=mode, tm=tm, tn=tn, tk=tk, outs=[_plain_out(M, N, tm, tn, out_dtype)],
                   epilogue=epilogue, extras=extras, name=name)[0]


def _row_tile(T, want):
    t = min(T, want)
    while T % t:
        t //= 2
    return t


def _col_tile(N, want):
    if N <= want:
        return N
    best = None
    for c in range(LANES, want + 1, LANES):
        if N % c == 0:
            best = c
    return best if best is not None else N


def _accumulate(ref, first, val):
    @pl.when(first)
    def _():
        ref[...] = val

    @pl.when(jnp.logical_not(first))
    def _():
        ref[...] += val


def _ln_fwd(z, g, b):
    mu = jnp.mean(z, axis=-1, keepdims=True)
    zc = z - mu
    var = jnp.mean(zc * zc, axis=-1, keepdims=True)
    r = lax.rsqrt(var + LN_EPS)
    xh = zc * r
    return xh * g + b, xh, r


def _ln_bwd(dy, xh, r, g):
    dxh = dy * g
    m1 = jnp.mean(dxh, axis=-1, keepdims=True)
    m2 = jnp.mean(dxh * xh, axis=-1, keepdims=True)
    return r * (dxh - m1 - xh * m2)


def _sigmoid(x):
    return 1.0 / (1.0 + jnp.exp(-x))


def _shift_down(x, s, row):
    if s == 0:
        return x
    return jnp.where(row >= s, pltpu.roll(x, s, 0), 0.0)


def _shift_up(x, s, row):
    if s == 0:
        return x
    n = x.shape[0]
    return jnp.where(row < n - s, pltpu.roll(x, n - s, 0), 0.0)


def _bucket_tables():
    exact = REL_BUCKETS // 2
    qi = np.arange(ATTN_BLOCK)[:, None]
    kj = np.arange(2 * ATTN_BLOCK)[None, :]
    steps = qi + ATTN_BLOCK - kj
    buckets, masks = [], []
    for window, dilation in DILATED_CONFIGS:
        max_steps = window // dilation
        band = (steps >= 0) & (steps <= max_steps)
        dist = np.maximum(steps, 0) * dilation
        d_f = np.maximum(dist, 1).astype(np.float32)
        large = exact + (np.log(d_f / np.float32(exact)) / np.float32(math.log(REL_MAX_DIST / exact))
                         * np.float32(REL_BUCKETS - exact)).astype(np.int32)
        large = np.minimum(large, REL_BUCKETS - 1)
        bucket = np.where(dist < exact, dist, large).astype(np.int32)
        buckets.append(bucket.reshape(1, -1))
        masks.append(np.where(band, 0.0, NEG_INF).astype(np.float32).reshape(1, -1))
    return np.stack(buckets), np.stack(masks)


def _split_hi_lo(x):
    hi = x.astype(BF16)
    lo = (x - hi.astype(F32)).astype(BF16)
    return hi, lo


def _bias_build(rel_table_t, bucket, mask):
    H = rel_table_t.shape[0]
    n = bucket.shape[-1]

    def body(t_ref, bkt_ref, mask_ref, o_ref):
        onehot = (lax.broadcasted_iota(jnp.int32, (REL_BUCKETS, n), 0) == bkt_ref[0]).astype(BF16)
        t = t_ref[...]
        t1 = t.astype(BF16)
        r1 = t - t1.astype(F32)
        t2 = r1.astype(BF16)
        t3 = (r1 - t2.astype(F32)).astype(BF16)
        acc = jnp.dot(t1, onehot, preferred_element_type=F32)
        acc = acc + jnp.dot(t2, onehot, preferred_element_type=F32)
        acc = acc + jnp.dot(t3, onehot, preferred_element_type=F32)
        o_ref[0] = acc + mask_ref[0]

    return pl.pallas_call(
        body, grid=(3,),
        in_specs=[pl.BlockSpec((H, REL_BUCKETS), lambda b: (0, 0)),
                  pl.BlockSpec((1, 1, n), lambda b: (b, 0, 0)),
                  pl.BlockSpec((1, 1, n), lambda b: (b, 0, 0))],
        out_specs=pl.BlockSpec((1, H, n), lambda b: (b, 0, 0)),
        out_shape=jax.ShapeDtypeStruct((3, H, n), F32),
        compiler_params=_params(1), name="bias_build",
    )(rel_table_t, bucket, mask)


def _rel_grad(dbias, bucket):
    H = dbias.shape[1]
    n = bucket.shape[-1]
    dims = (((1,), (1,)), ((), ()))

    def body(d_ref, bkt_ref, o_ref):
        b = pl.program_id(0)
        onehot = (lax.broadcasted_iota(jnp.int32, (REL_BUCKETS, n), 0) == bkt_ref[0]).astype(BF16)
        d = d_ref[0]
        d1 = d.astype(BF16)
        r1 = d - d1.astype(F32)
        d2 = r1.astype(BF16)
        d3 = (r1 - d2.astype(F32)).astype(BF16)
        acc = lax.dot_general(d1, onehot, dims, preferred_element_type=F32)
        acc = acc + lax.dot_general(d2, onehot, dims, preferred_element_type=F32)
        acc = acc + lax.dot_general(d3, onehot, dims, preferred_element_type=F32)
        _accumulate(o_ref, b == 0, acc)

    return pl.pallas_call(
        body, grid=(3,),
        in_specs=[pl.BlockSpec((1, H, n), lambda b: (b, 0, 0)),
                  pl.BlockSpec((1, 1, n), lambda b: (b, 0, 0))],
        out_specs=pl.BlockSpec((H, REL_BUCKETS), lambda b: (0, 0)),
        out_shape=jax.ShapeDtypeStruct((H, REL_BUCKETS), F32),
        compiler_params=_params(1), name="rel_grad",
    )(dbias, bucket)


def _attn_specs(B, S, AW, d):
    L = S // d
    HP = AW // LANES
    W3 = 3 * HP
    q_spec = pl.BlockSpec((1, L, LANES), lambda h, b, r: (b, 0, r * W3 + h))
    k_spec = pl.BlockSpec((1, L, LANES), lambda h, b, r: (b, 0, r * W3 + HP + h))
    v_spec = pl.BlockSpec((1, L, LANES), lambda h, b, r: (b, 0, r * W3 + 2 * HP + h))
    o_spec = pl.BlockSpec((1, L, LANES), lambda h, b, r: (b, 0, r * HP + h))
    bias_spec = pl.BlockSpec((2, ATTN_BLOCK, 2 * ATTN_BLOCK), lambda h, b, r: (h, 0, 0))
    return L, HP, q_spec, k_spec, v_spec, o_spec, bias_spec


def _attn_fwd(qkv, bias, B, S, AW, d, name):
    L, HP, q_spec, k_spec, v_spec, o_spec, bias_spec = _attn_specs(B, S, AW, d)
    nb = L // ATTN_BLOCK
    nt = (((1,), (1,)), ((), ()))

    def body(q_ref, k_ref, v_ref, b_ref, o_ref, lse_ref):
        head0 = lax.broadcasted_iota(jnp.int32, (1, LANES), 1) < HEAD_DIM

        def block(n, first):
            qs = pl.multiple_of(n * ATTN_BLOCK, ATTN_BLOCK)
            q = q_ref[0, pl.ds(qs, ATTN_BLOCK), :]
            if first:
                kk = k_ref[0, pl.ds(0, ATTN_BLOCK), :]
                vv = v_ref[0, pl.ds(0, ATTN_BLOCK), :]
            else:
                ks = pl.multiple_of(n * ATTN_BLOCK - ATTN_BLOCK, ATTN_BLOCK)
                kk = k_ref[0, pl.ds(ks, 2 * ATTN_BLOCK), :]
                vv = v_ref[0, pl.ds(ks, 2 * ATTN_BLOCK), :]
            outs, lses = [], []
            for e in range(2):
                msk = head0 if e == 0 else jnp.logical_not(head0)
                qe = jnp.where(msk, q, jnp.zeros_like(q))
                s = lax.dot_general(qe, kk, nt, preferred_element_type=F32) * QK_SCALE
                s = s + (b_ref[e, :, ATTN_BLOCK:] if first else b_ref[e])
                m = jnp.max(s, axis=-1, keepdims=True)
                p = jnp.exp(s - m)
                l = jnp.sum(p, axis=-1, keepdims=True)
                o = jnp.dot(p.astype(BF16), vv, preferred_element_type=F32)
                outs.append(o / l)
                lses.append(jnp.broadcast_to(m + jnp.log(l), (ATTN_BLOCK, LANES)))
            o_ref[0, pl.ds(qs, ATTN_BLOCK), :] = jnp.where(head0, outs[0], outs[1])
            lse_ref[0, pl.ds(qs, ATTN_BLOCK), :] = jnp.where(head0, lses[0], lses[1])

        block(0, True)
        if nb > 1:
            def loop(n, c):
                block(n, False)
                return c
            lax.fori_loop(1, nb, loop, 0)

    qv = qkv.reshape(B, L, d * 3 * AW)
    o, lse = pl.pallas_call(
        body, grid=(HP, B, d), in_specs=[q_spec, k_spec, v_spec, bias_spec],
        out_specs=[o_spec, o_spec],
        out_shape=[jax.ShapeDtypeStruct((B, L, d * AW), F32)] * 2,
        compiler_params=_params(3), name=name,
    )(qv, qv, qv, bias)
    return o.reshape(B * S, AW), lse.reshape(B * S, AW)


def _attn_bwd(qkv, do, lse, dd, bias, B, S, AW, d, name):
    L, HP, q_spec, k_spec, v_spec, o_spec, bias_spec = _attn_specs(B, S, AW, d)
    nb = L // ATTN_BLOCK
    nt = (((1,), (1,)), ((), ()))
    tn = (((0,), (0,)), ((), ()))

    def body(q_ref, k_ref, v_ref, do_ref, lse_ref, dd_ref, b_ref, dq_ref, dk_ref, dv_ref, db_ref):
        head0 = lax.broadcasted_iota(jnp.int32, (1, LANES), 1) < HEAD_DIM
        first_step = jnp.logical_and(pl.program_id(1) == 0, pl.program_id(2) == 0)

        @pl.when(first_step)
        def _():
            db_ref[...] = jnp.zeros_like(db_ref)

        dk_ref[...] = jnp.zeros_like(dk_ref)
        dv_ref[...] = jnp.zeros_like(dv_ref)

        def block(n, first):
            qs = pl.multiple_of(n * ATTN_BLOCK, ATTN_BLOCK)
            nkeys = ATTN_BLOCK if first else 2 * ATTN_BLOCK
            ks = 0 if first else pl.multiple_of(n * ATTN_BLOCK - ATTN_BLOCK, ATTN_BLOCK)
            q = q_ref[0, pl.ds(qs, ATTN_BLOCK), :]
            kk = k_ref[0, pl.ds(ks, nkeys), :]
            vv = v_ref[0, pl.ds(ks, nkeys), :]
            dout = do_ref[0, pl.ds(qs, ATTN_BLOCK), :]
            lse_b = lse_ref[0, pl.ds(qs, ATTN_BLOCK), :]
            dd_b = dd_ref[0, pl.ds(qs, ATTN_BLOCK), :]
            dq = jnp.zeros((ATTN_BLOCK, LANES), F32)
            dkk = jnp.zeros((nkeys, LANES), F32)
            dvv = jnp.zeros((nkeys, LANES), F32)
            for e in range(2):
                msk = head0 if e == 0 else jnp.logical_not(head0)
                c0 = e * HEAD_DIM
                qe = jnp.where(msk, q, jnp.zeros_like(q))
                doe = jnp.where(msk, dout, jnp.zeros_like(dout))
                kke = jnp.where(msk, kk, jnp.zeros_like(kk))
                s = lax.dot_general(qe, kk, nt, preferred_element_type=F32) * QK_SCALE
                s = s + (b_ref[e, :, ATTN_BLOCK:] if first else b_ref[e])
                p = jnp.exp(s - lse_b[:, c0:c0 + 1])
                dp = lax.dot_general(doe, vv, nt, preferred_element_type=F32)
                ds = p * (dp - dd_b[:, c0:c0 + 1])
                if first:
                    db_ref[e, :, ATTN_BLOCK:] += ds
                else:
                    db_ref[e] += ds
                dsb = (ds * QK_SCALE).astype(BF16)
                dq = dq + jnp.dot(dsb, kke, preferred_element_type=F32)
                dkk = dkk + lax.dot_general(dsb, qe, tn, preferred_element_type=F32)
                dvv = dvv + lax.dot_general(p.astype(BF16), doe, tn, preferred_element_type=F32)
            dq_ref[0, pl.ds(qs, ATTN_BLOCK), :] = dq
            dk_ref[0, pl.ds(ks, nkeys), :] += dkk
            dv_ref[0, pl.ds(ks, nkeys), :] += dvv

        block(0, True)
        if nb > 1:
            def loop(n, c):
                block(n, False)
                return c
            lax.fori_loop(1, nb, loop, 0)

    H = AW // HEAD_DIM
    qv = qkv.reshape(B, L, d * 3 * AW)
    view = lambda t: t.reshape(B, L, d * AW)
    dq, dk, dv, db = pl.pallas_call(
        body, grid=(HP, B, d),
        in_specs=[q_spec, k_spec, v_spec, o_spec, o_spec, o_spec, bias_spec],
        out_specs=[o_spec, o_spec, o_spec, bias_spec],
        out_shape=[jax.ShapeDtypeStruct((B, L, d * AW), F32)] * 3
        + [jax.ShapeDtypeStruct((H, ATTN_BLOCK, 2 * ATTN_BLOCK), F32)],
        compiler_params=_params(3), name=name,
    )(qv, qv, qv, view(do), view(lse), view(dd), bias)
    flat = lambda t: t.reshape(B * S, AW)
    return flat(dq), flat(dk), flat(dv), db


def _attn_combine(ons, lses, gain, tm):
    T, AW = ons[0].shape

    def body(o1, o2, o3, l1, l2, l3, g_ref, attn_ref, lse_ref, mix_ref, r_ref):
        la, lb, lc = l1[...], l2[...], l3[...]
        m = jnp.maximum(jnp.maximum(la, lb), lc)
        ea, eb, ec = jnp.exp(la - m), jnp.exp(lb - m), jnp.exp(lc - m)
        den = ea + eb + ec
        attn = (ea * o1[...] + eb * o2[...] + ec * o3[...]) / den
        attn_ref[...] = attn
        lse_ref[...] = m + jnp.log(den)
        r = lax.rsqrt(jnp.mean(attn * attn, axis=-1, keepdims=True) + LN_EPS)
        mix_ref[...] = (attn * r * g_ref[...]).astype(BF16)
        r_ref[...] = jnp.broadcast_to(r, (tm, LANES))

    row = pl.BlockSpec((tm, AW), lambda i: (i, 0))
    return pl.pallas_call(
        body, grid=(T // tm,),
        in_specs=[row] * 6 + [pl.BlockSpec((1, AW), lambda i: (0, 0))],
        out_specs=[row, row, row, pl.BlockSpec((tm, LANES), lambda i: (i, 0))],
        out_shape=[jax.ShapeDtypeStruct((T, AW), F32), jax.ShapeDtypeStruct((T, AW), F32),
                   jax.ShapeDtypeStruct((T, AW), BF16), jax.ShapeDtypeStruct((T, LANES), F32)],
        compiler_params=_params(1), name="attn_combine",
    )(*ons, *lses, gain)


def _to_sub(src_ref, stage_ref, dsts, S):
    stage_ref[...] = src_ref[0].astype(F32)
    for (_, d), dst in zip(DILATED_CONFIGS[1:], dsts):
        L = S // d
        for r in range(d):
            dst[r * L:(r + 1) * L, :] = stage_ref[pl.ds(r, L, stride=d), :].astype(dst.dtype)


def _branch_blocks(S, d, block):
    nb = S // d // ATTN_BLOCK
    inner_unroll = 3 if (nb - 1) % 3 == 0 else 1

    def per_residue(r, c):
        block(r * nb, True)
        if nb > 1:
            def inner(n, c2):
                block(r * nb + n, False)
                return c2
            lax.fori_loop(1, nb, inner, 0, unroll=inner_unroll)
        return c

    lax.fori_loop(0, d, per_residue, 0, unroll=4 if nb == 1 else 1)


def _attention_fwd(qkv, bias_all, B, S, AW):
    HP = AW // LANES
    nt = MM_DIMS["nt"]

    def body(q_ref, k_ref, v_ref, b_ref, o_ref, lse_ref, stage, q4, q16, k4, k16, v4, v16, o1, l1, o4, l4, o16, l16):
        head0 = lax.broadcasted_iota(jnp.int32, (1, LANES), 1) < HEAD_DIM
        _to_sub(q_ref, stage, (q4, q16), S)
        _to_sub(k_ref, stage, (k4, k16), S)
        _to_sub(v_ref, stage, (v4, v16), S)
        srcs = ((q_ref.at[0], k_ref.at[0], v_ref.at[0], o1, l1), (q4, k4, v4, o4, l4), (q16, k16, v16, o16, l16))
        for bi, (_, d) in enumerate(DILATED_CONFIGS):
            qs_ref, ks_ref, vs_ref, od_ref, ld_ref = srcs[bi]

            def block(g, first, bi=bi, qs_ref=qs_ref, ks_ref=ks_ref, vs_ref=vs_ref, od_ref=od_ref, ld_ref=ld_ref):
                qs = pl.multiple_of(g * ATTN_BLOCK, ATTN_BLOCK)
                nkeys = ATTN_BLOCK if first else 2 * ATTN_BLOCK
                ks = qs if first else pl.multiple_of(qs - ATTN_BLOCK, ATTN_BLOCK)
                q = qs_ref[pl.ds(qs, ATTN_BLOCK), :]
                kk = ks_ref[pl.ds(ks, nkeys), :]
                vv = vs_ref[pl.ds(ks, nkeys), :]
                outs, lses = [], []
                for e in range(2):
                    msk = head0 if e == 0 else jnp.logical_not(head0)
                    qe = jnp.where(msk, q * QK_SCALE, jnp.zeros_like(q))
                    s = lax.dot_general(qe, kk, nt, preferred_element_type=F32)
                    s = s + (b_ref[bi, e, :, ATTN_BLOCK:] if first else b_ref[bi, e])
                    m = jnp.max(s, axis=-1, keepdims=True)
                    p = jnp.exp(s - m)
                    l = jnp.sum(p, axis=-1, keepdims=True)
                    o = jnp.dot(p.astype(BF16), vv, preferred_element_type=F32)
                    outs.append(o / l)
                    lses.append(jnp.broadcast_to(m + jnp.log(l), (ATTN_BLOCK, LANES)))
                od_ref[pl.ds(qs, ATTN_BLOCK), :] = jnp.where(head0, outs[0], outs[1])
                ld_ref[pl.ds(qs, ATTN_BLOCK), :] = jnp.where(head0, lses[0], lses[1])

            _branch_blocks(S, d, block)

        def natural(sub_ref, d):
            L = S // d
            for r in range(d):
                stage[pl.ds(r, L, stride=d), :] = sub_ref[r * L:(r + 1) * L, :]
            return stage[...]

        la = l1[...]
        lb = natural(l4, 4)
        lc = natural(l16, 16)
        m = jnp.maximum(jnp.maximum(la, lb), lc)
        ea, eb, ec = jnp.exp(la - m), jnp.exp(lb - m), jnp.exp(lc - m)
        den = ea + eb + ec
        lse_ref[0] = m + jnp.log(den)
        acc = ea * o1[...]
        acc = acc + eb * natural(o4, 4)
        acc = acc + ec * natural(o16, 16)
        o_ref[0] = acc / den

    blk = lambda off: pl.BlockSpec((1, S, LANES), lambda b, h: (b, 0, off + h))
    qv = qkv.reshape(B, S, 3 * AW)
    sub_b = pltpu.VMEM((S, LANES), BF16)
    sub_f = pltpu.VMEM((S, LANES), F32)
    o, lse = pl.pallas_call(
        body, grid=(B, HP),
        in_specs=[blk(0), blk(HP), blk(2 * HP),
                  pl.BlockSpec((3, 2, ATTN_BLOCK, 2 * ATTN_BLOCK), lambda b, h: (0, h, 0, 0))],
        out_specs=[blk(0), blk(0)],
        out_shape=[jax.ShapeDtypeStruct((B, S, AW), F32)] * 2,
        scratch_shapes=[sub_f] + [sub_b] * 6 + [sub_f] * 6,
        compiler_params=_params(2), name="attention_fwd",
    )(qv, qv, qv, bias_all)
    return o.reshape(B * S, AW), lse.reshape(B * S, AW)


def _attention_bwd(qkv, do, lse, dd, bias_all, B, S, AW):
    HP = AW // LANES
    H = AW // HEAD_DIM
    nt, tn = MM_DIMS["nt"], MM_DIMS["tn"]

    def body(q_ref, k_ref, v_ref, do_ref, lse_ref, dd_ref, b_ref,
             dq_ref, dk_ref, dv_ref, csq_ref, csk_ref, csv_ref, db_ref,
             stage, q4, q16, k4, k16, v4, v16, g4, g16, l4, l16, d4, d16,
             aq1, ak1, av1, aq4, ak4, av4, aq16, ak16, av16):
        head0 = lax.broadcasted_iota(jnp.int32, (1, LANES), 1) < HEAD_DIM
        first_b = pl.program_id(1) == 0

        @pl.when(first_b)
        def _():
            db_ref[...] = jnp.zeros_like(db_ref)

        _to_sub(q_ref, stage, (q4, q16), S)
        _to_sub(k_ref, stage, (k4, k16), S)
        _to_sub(v_ref, stage, (v4, v16), S)
        _to_sub(do_ref, stage, (g4, g16), S)
        _to_sub(lse_ref, stage, (l4, l16), S)
        _to_sub(dd_ref, stage, (d4, d16), S)
        for acc in (ak1, av1, ak4, av4, ak16, av16):
            acc[...] = jnp.zeros_like(acc)
        srcs = ((q_ref.at[0], k_ref.at[0], v_ref.at[0], do_ref.at[0], lse_ref.at[0], dd_ref.at[0], aq1, ak1, av1),
                (q4, k4, v4, g4, l4, d4, aq4, ak4, av4), (q16, k16, v16, g16, l16, d16, aq16, ak16, av16))
        for bi, (_, d) in enumerate(DILATED_CONFIGS):
            def block(g, first, bi=bi, refs=srcs[bi]):
                qs_ref, ks_ref, vs_ref, gs_ref, ls_ref, ds_ref, aq, ak, av = refs
                qs = pl.multiple_of(g * ATTN_BLOCK, ATTN_BLOCK)
                nkeys = ATTN_BLOCK if first else 2 * ATTN_BLOCK
                ks = qs if first else pl.multiple_of(qs - ATTN_BLOCK, ATTN_BLOCK)
                q = qs_ref[pl.ds(qs, ATTN_BLOCK), :]
                kk = ks_ref[pl.ds(ks, nkeys), :]
                vv = vs_ref[pl.ds(ks, nkeys), :]
                dout = gs_ref[pl.ds(qs, ATTN_BLOCK), :]
                lse_b = ls_ref[pl.ds(qs, ATTN_BLOCK), :]
                dd_b = ds_ref[pl.ds(qs, ATTN_BLOCK), :]
                dq = jnp.zeros((ATTN_BLOCK, LANES), F32)
                dkk = jnp.zeros((nkeys, LANES), F32)
                dvv = jnp.zeros((nkeys, LANES), F32)
                for e in range(2):
                    msk = head0 if e == 0 else jnp.logical_not(head0)
                    c0 = e * HEAD_DIM
                    qe = jnp.where(msk, q * QK_SCALE, jnp.zeros_like(q))
                    doe = jnp.where(msk, dout, jnp.zeros_like(dout))
                    kke = jnp.where(msk, kk * QK_SCALE, jnp.zeros_like(kk))
                    s = lax.dot_general(qe, kk, nt, preferred_element_type=F32)
                    s = s + (b_ref[bi, e, :, ATTN_BLOCK:] if first else b_ref[bi, e])
                    p = jnp.exp(s - lse_b[:, c0:c0 + 1])
                    dp = lax.dot_general(doe, vv, nt, preferred_element_type=F32)
                    ds = p * (dp - dd_b[:, c0:c0 + 1])
                    if first:
                        db_ref[bi, e, :, ATTN_BLOCK:] += ds
                    else:
                        db_ref[bi, e] += ds
                    dsb = ds.astype(BF16)
                    dq = dq + jnp.dot(dsb, kke, preferred_element_type=F32)
                    dkk = dkk + lax.dot_general(dsb, qe, tn, preferred_element_type=F32)
                    dvv = dvv + lax.dot_general(p.astype(BF16), doe, tn, preferred_element_type=F32)
                aq[pl.ds(qs, ATTN_BLOCK), :] = dq
                ak[pl.ds(ks, nkeys), :] += dkk
                av[pl.ds(ks, nkeys), :] += dvv

            _branch_blocks(S, d, block)

        for a1, a4, a16, out_ref, cs_ref in ((aq1, aq4, aq16, dq_ref, csq_ref), (ak1, ak4, ak16, dk_ref, csk_ref),
                                             (av1, av4, av16, dv_ref, csv_ref)):
            stage[...] = a1[...]
            for d, sub in ((4, a4), (16, a16)):
                L = S // d
                for r in range(d):
                    stage[pl.ds(r, L, stride=d), :] += sub[r * L:(r + 1) * L, :]
            tot = stage[...]
            out_ref[0] = tot.astype(out_ref.dtype)
            _accumulate(cs_ref, first_b, jnp.sum(tot, axis=0, keepdims=True))

    blk = lambda off: pl.BlockSpec((1, S, LANES), lambda h, b: (b, 0, off + h))
    cs_spec = pl.BlockSpec((1, LANES), lambda h, b: (0, h))
    bias_spec = pl.BlockSpec((3, 2, ATTN_BLOCK, 2 * ATTN_BLOCK), lambda h, b: (0, h, 0, 0))
    qv = qkv.reshape(B, S, 3 * AW)
    view = lambda t: t.reshape(B, S, AW)
    sub_b = pltpu.VMEM((S, LANES), BF16)
    sub_f = pltpu.VMEM((S, LANES), F32)
    res = pl.pallas_call(
        body, grid=(HP, B),
        in_specs=[blk(0), blk(HP), blk(2 * HP), blk(0), blk(0), blk(0), bias_spec],
        out_specs=[blk(0), blk(0), blk(0), cs_spec, cs_spec, cs_spec, bias_spec],
        out_shape=[jax.ShapeDtypeStruct((B, S, AW), BF16)] * 3 + [jax.ShapeDtypeStruct((1, AW), F32)] * 3
        + [jax.ShapeDtypeStruct((3, H, ATTN_BLOCK, 2 * ATTN_BLOCK), F32)],
        scratch_shapes=[sub_f] + [sub_b] * 8 + [sub_f] * 4 + [sub_f] * 9,
        compiler_params=_params(2), name="attention_bwd",
    )(qv, qv, qv, view(do), view(lse), view(dd), bias_all)
    flat = lambda t: t.reshape(B * S, AW)
    return flat(res[0]), flat(res[1]), flat(res[2]), res[3], res[4], res[5], res[6]


def _regroup(src, stage, dst, d, S, off=0):
    if d == 1:
        dst[off:off + S, :] = src.astype(dst.dtype)
        return
    stage[...] = src.astype(F32)
    L = S // d
    for r in range(d):
        dst[off + r * L:off + (r + 1) * L, :] = stage[pl.ds(r, L, stride=d), :].astype(dst.dtype)


def _ungroup(sub_ref, off, nat_ref, d, S, add):
    L = S // d
    for r in range(d):
        rows = pl.ds(0, S) if d == 1 else pl.ds(r, L, stride=d)
        val = sub_ref[off + r * L:off + (r + 1) * L, :]
        if add:
            nat_ref[rows, :] += val
        else:
            nat_ref[rows, :] = val


def _branch_scores(qe, kc3, kp3, b_ref, bi, e, first3):
    s_cur = jnp.einsum("gqe,gke->gqk", qe, kc3, preferred_element_type=F32) + b_ref[bi, e, :, ATTN_BLOCK:]
    if kp3 is None:
        return s_cur, None
    s_prev = jnp.einsum("gqe,gke->gqk", qe, kp3, preferred_element_type=F32) + b_ref[bi, e, :, :ATTN_BLOCK]
    return s_cur, jnp.where(first3, NEG_INF, s_prev)


def _attention_fwd(qkv, bias_all, B, S, AW):
    HP = AW // LANES
    G = S // ATTN_BLOCK
    blk3 = (G, ATTN_BLOCK, LANES)

    def body(q_ref, k_ref, v_ref, b_ref, o_ref, lse_ref, stage, qs, ks, vs, ot, lt, on0, on1, on2, ln0, ln1, ln2):
        head0 = lax.broadcasted_iota(jnp.int32, (1, 1, LANES), 2) < HEAD_DIM
        g_idx = lax.broadcasted_iota(jnp.int32, (G, 1, 1), 0)
        ks[0:ATTN_BLOCK, :] = jnp.zeros((ATTN_BLOCK, LANES), BF16)
        vs[0:ATTN_BLOCK, :] = jnp.zeros((ATTN_BLOCK, LANES), BF16)
        nat_o, nat_l = (on0, on1, on2), (ln0, ln1, ln2)
        for bi, (_, d) in enumerate(DILATED_CONFIGS):
            nb = S // d // ATTN_BLOCK
            _regroup(q_ref[0], stage, qs, d, S)
            _regroup(k_ref[0], stage, ks, d, S, ATTN_BLOCK)
            _regroup(v_ref[0], stage, vs, d, S, ATTN_BLOCK)
            q3 = qs[...].reshape(blk3) * QK_SCALE
            kc3 = ks[ATTN_BLOCK:ATTN_BLOCK + S, :].reshape(blk3)
            vc3 = vs[ATTN_BLOCK:ATTN_BLOCK + S, :].reshape(blk3)
            kp3 = vp3 = first3 = None
            if nb > 1:
                kp3 = ks[0:S, :].reshape(blk3)
                vp3 = vs[0:S, :].reshape(blk3)
                first3 = (g_idx & (nb - 1)) == 0
            outs, lses = [], []
            for e in range(2):
                msk = head0 if e == 0 else jnp.logical_not(head0)
                qe = jnp.where(msk, q3, jnp.zeros_like(q3))
                s_cur, s_prev = _branch_scores(qe, kc3, kp3, b_ref, bi, e, first3)
                m = jnp.max(s_cur, axis=-1, keepdims=True)
                if s_prev is not None:
                    m = jnp.maximum(m, jnp.max(s_prev, axis=-1, keepdims=True))
                p = jnp.exp(s_cur - m)
                l = jnp.sum(p, axis=-1, keepdims=True)
                o = jnp.einsum("gqk,gke->gqe", p.astype(BF16), vc3, preferred_element_type=F32)
                if s_prev is not None:
                    p = jnp.exp(s_prev - m)
                    l = l + jnp.sum(p, axis=-1, keepdims=True)
                    o = o + jnp.einsum("gqk,gke->gqe", p.astype(BF16), vp3, preferred_element_type=F32)
                outs.append(o / l)
                lses.append(jnp.broadcast_to(m + jnp.log(l), blk3))
            ot[...] = jnp.where(head0, outs[0], outs[1]).reshape(S, LANES)
            lt[...] = jnp.where(head0, lses[0], lses[1]).reshape(S, LANES)
            _ungroup(ot, 0, nat_o[bi], d, S, add=False)
            _ungroup(lt, 0, nat_l[bi], d, S, add=False)

        la, lb, lc = ln0[...], ln1[...], ln2[...]
        m = jnp.maximum(jnp.maximum(la, lb), lc)
        ea, eb, ec = jnp.exp(la - m), jnp.exp(lb - m), jnp.exp(lc - m)
        den = ea + eb + ec
        lse_ref[0] = m + jnp.log(den)
        o_ref[0] = (ea * on0[...] + eb * on1[...] + ec * on2[...]) / den

    blk = lambda off: pl.BlockSpec((1, S, LANES), lambda b, h: (b, 0, off + h))
    qv = qkv.reshape(B, S, 3 * AW)
    sub_f = pltpu.VMEM((S, LANES), F32)
    pad_b = pltpu.VMEM((S + ATTN_BLOCK, LANES), BF16)
    o, lse = pl.pallas_call(
        body, grid=(B, HP),
        in_specs=[blk(0), blk(HP), blk(2 * HP),
                  pl.BlockSpec((3, 2, ATTN_BLOCK, 2 * ATTN_BLOCK), lambda b, h: (0, h, 0, 0))],
        out_specs=[blk(0), blk(0)],
        out_shape=[jax.ShapeDtypeStruct((B, S, AW), F32)] * 2,
        scratch_shapes=[sub_f, pltpu.VMEM((S, LANES), BF16), pad_b, pad_b] + [sub_f] * 8,
        compiler_params=_params(2), name="attention_fwd",
    )(qv, qv, qv, bias_all)
    return o.reshape(B * S, AW), lse.reshape(B * S, AW)


def _attention_bwd(qkv, do, lse, dd, bias_all, B, S, AW):
    HP = AW // LANES
    H = AW // HEAD_DIM
    G = S // ATTN_BLOCK
    blk3 = (G, ATTN_BLOCK, LANES)
    PAD = ATTN_BLOCK

    def body(q_ref, k_ref, v_ref, do_ref, lse_ref, dd_ref, b_ref,
             dq_ref, dk_ref, dv_ref, csq_ref, csk_ref, csv_ref, db_ref,
             stage, qs, ks, vs, gs, ls, ds_, tq, tk, tv, accq, acck, accv):
        head0 = lax.broadcasted_iota(jnp.int32, (1, 1, LANES), 2) < HEAD_DIM
        g_idx = lax.broadcasted_iota(jnp.int32, (G, 1, 1), 0)
        first_b = pl.program_id(1) == 0

        @pl.when(first_b)
        def _():
            db_ref[...] = jnp.zeros_like(db_ref)

        ks[0:PAD, :] = jnp.zeros((PAD, LANES), BF16)
        vs[0:PAD, :] = jnp.zeros((PAD, LANES), BF16)
        tk[0:PAD, :] = jnp.zeros((PAD, LANES), F32)
        tv[0:PAD, :] = jnp.zeros((PAD, LANES), F32)
        for bi, (_, d) in enumerate(DILATED_CONFIGS):
            nb = S // d // ATTN_BLOCK
            _regroup(q_ref[0], stage, qs, d, S)
            _regroup(k_ref[0], stage, ks, d, S, PAD)
            _regroup(v_ref[0], stage, vs, d, S, PAD)
            _regroup(do_ref[0], stage, gs, d, S)
            _regroup(lse_ref[0], stage, ls, d, S)
            _regroup(dd_ref[0], stage, ds_, d, S)
            q3 = qs[...].reshape(blk3) * QK_SCALE
            do3 = gs[...].reshape(blk3)
            lse3 = ls[...].reshape(blk3)
            dd3 = ds_[...].reshape(blk3)
            kc3 = ks[PAD:PAD + S, :].reshape(blk3)
            vc3 = vs[PAD:PAD + S, :].reshape(blk3)
            kp3 = vp3 = first3 = None
            if nb > 1:
                kp3 = ks[0:S, :].reshape(blk3)
                vp3 = vs[0:S, :].reshape(blk3)
                first3 = (g_idx & (nb - 1)) == 0
            dq = jnp.zeros(blk3, F32)
            dkc = jnp.zeros(blk3, F32)
            dvc = jnp.zeros(blk3, F32)
            dkp = jnp.zeros(blk3, F32)
            dvp = jnp.zeros(blk3, F32)
            for e in range(2):
                msk = head0 if e == 0 else jnp.logical_not(head0)
                c0 = e * HEAD_DIM
                qe = jnp.where(msk, q3, jnp.zeros_like(q3))
                doe = jnp.where(msk, do3, jnp.zeros_like(do3))
                lse_e = lse3[:, :, c0:c0 + 1]
                dd_e = dd3[:, :, c0:c0 + 1]
                s_cur, s_prev = _branch_scores(qe, kc3, kp3, b_ref, bi, e, first3)
                for s, k3, v3, cur in ((s_cur, kc3, vc3, True), (s_prev, kp3, vp3, False)):
                    if s is None:
                        continue
                    p = jnp.exp(s - lse_e)
                    dp = jnp.einsum("gqe,gke->gqk", doe, v3, preferred_element_type=F32)
                    dsc = p * (dp - dd_e)
                    if cur:
                        db_ref[bi, e, :, ATTN_BLOCK:] += jnp.sum(dsc, axis=0)
                    else:
                        db_ref[bi, e, :, :ATTN_BLOCK] += jnp.sum(dsc, axis=0)
                    dsb = dsc.astype(BF16)
                    ke = jnp.where(msk, k3 * QK_SCALE, jnp.zeros_like(k3))
                    dq = dq + jnp.einsum("gqk,gke->gqe", dsb, ke, preferred_element_type=F32)
                    dk_e = jnp.einsum("gqk,gqe->gke", dsb, qe, preferred_element_type=F32)
                    dv_e = jnp.einsum("gqk,gqe->gke", p.astype(BF16), doe, preferred_element_type=F32)
                    if cur:
                        dkc, dvc = dkc + dk_e, dvc + dv_e
                    else:
                        dkp, dvp = dkp + dk_e, dvp + dv_e
            tq[...] = dq.reshape(S, LANES)
            tk[PAD:PAD + S, :] = dkc.reshape(S, LANES)
            tv[PAD:PAD + S, :] = dvc.reshape(S, LANES)
            if nb > 1:
                tk[0:S, :] += dkp.reshape(S, LANES)
                tv[0:S, :] += dvp.reshape(S, LANES)
            _ungroup(tq, 0, accq, d, S, add=bi > 0)
            _ungroup(tk, PAD, acck, d, S, add=bi > 0)
            _ungroup(tv, PAD, accv, d, S, add=bi > 0)

        for acc, out_ref, cs_ref in ((accq, dq_ref, csq_ref), (acck, dk_ref, csk_ref), (accv, dv_ref, csv_ref)):
            tot = acc[...]
            out_ref[0] = tot.astype(out_ref.dtype)
            _accumulate(cs_ref, first_b, jnp.sum(tot, axis=0, keepdims=True))

    blk = lambda off: pl.BlockSpec((1, S, LANES), lambda h, b: (b, 0, off + h))
    cs_spec = pl.BlockSpec((1, LANES), lambda h, b: (0, h))
    bias_spec = pl.BlockSpec((3, 2, ATTN_BLOCK, 2 * ATTN_BLOCK), lambda h, b: (0, h, 0, 0))
    qv = qkv.reshape(B, S, 3 * AW)
    view = lambda t: t.reshape(B, S, AW)
    sub_b = pltpu.VMEM((S, LANES), BF16)
    sub_f = pltpu.VMEM((S, LANES), F32)
    pad_b = pltpu.VMEM((S + PAD, LANES), BF16)
    pad_f = pltpu.VMEM((S + PAD, LANES), F32)
    res = pl.pallas_call(
        body, grid=(HP, B),
        in_specs=[blk(0), blk(HP), blk(2 * HP), blk(0), blk(0), blk(0), bias_spec],
        out_specs=[blk(0), blk(0), blk(0), cs_spec, cs_spec, cs_spec, bias_spec],
        out_shape=[jax.ShapeDtypeStruct((B, S, AW), BF16)] * 3 + [jax.ShapeDtypeStruct((1, AW), F32)] * 3
        + [jax.ShapeDtypeStruct((3, H, ATTN_BLOCK, 2 * ATTN_BLOCK), F32)],
        scratch_shapes=[sub_f, sub_b, pad_b, pad_b, sub_b, sub_f, sub_f, sub_f, pad_f, pad_f, sub_f, sub_f, sub_f],
        compiler_params=_params(2), name="attention_bwd",
    )(qv, qv, qv, view(do), view(lse), view(dd), bias_all)
    flat = lambda t: t.reshape(B * S, AW)
    return flat(res[0]), flat(res[1]), flat(res[2]), res[3], res[4], res[5], res[6]


def _attn_norm(attn, gain, tm):
    T, AW = attn.shape

    def body(a_ref, g_ref, mix_ref, r_ref):
        a = a_ref[...]
        r = lax.rsqrt(jnp.mean(a * a, axis=-1, keepdims=True) + LN_EPS)
        mix_ref[...] = (a * r * g_ref[...]).astype(BF16)
        r_ref[...] = jnp.broadcast_to(r, (tm, LANES))

    row = pl.BlockSpec((tm, AW), lambda i: (i, 0))
    return pl.pallas_call(
        body, grid=(T // tm,), in_specs=[row, pl.BlockSpec((1, AW), lambda i: (0, 0))],
        out_specs=[row, pl.BlockSpec((tm, LANES), lambda i: (i, 0))],
        out_shape=[jax.ShapeDtypeStruct((T, AW), BF16), jax.ShapeDtypeStruct((T, LANES), F32)],
        compiler_params=_params(1), name="attn_norm",
    )(attn, gain)


def _attn_pre_bwd(dmixed, attn, rstd, gain, tm):
    T, AW = attn.shape
    ones_np = np.kron(np.eye(AW // HEAD_DIM, dtype=np.float32), np.ones((HEAD_DIM, HEAD_DIM), np.float32))
    ones_bd = jnp.asarray(ones_np, dtype=BF16)

    def body(dm_ref, a_ref, r_ref, g_ref, ones_ref, do_ref, dd_ref, dg_ref):
        i = pl.program_id(0)
        dm = dm_ref[...]
        a = a_ref[...]
        r = r_ref[:, 0:1]
        dxn = dm * g_ref[...]
        da = r * (dxn - a * (r * r) * jnp.mean(dxn * a, axis=-1, keepdims=True))
        do_ref[...] = da.astype(BF16)
        hi, lo = _split_hi_lo(da * a)
        dd_ref[...] = (jnp.dot(hi, ones_ref[...], preferred_element_type=F32)
                       + jnp.dot(lo, ones_ref[...], preferred_element_type=F32))
        _accumulate(dg_ref, i == 0, jnp.sum(dm * a * r, axis=0, keepdims=True))

    row = pl.BlockSpec((tm, AW), lambda i: (i, 0))
    vec = pl.BlockSpec((1, AW), lambda i: (0, 0))
    return pl.pallas_call(
        body, grid=(T // tm,),
        in_specs=[row, row, pl.BlockSpec((tm, LANES), lambda i: (i, 0)), vec,
                  pl.BlockSpec((AW, AW), lambda i: (0, 0))],
        out_specs=[row, row, vec],
        out_shape=[jax.ShapeDtypeStruct((T, AW), BF16), jax.ShapeDtypeStruct((T, AW), F32),
                   jax.ShapeDtypeStruct((1, AW), F32)],
        compiler_params=_params(1), name="attn_pre_bwd",
    )(dmixed, attn, rstd, gain, ones_bd)


def _conv_branch_fwd_math(a, g, w_ref, cb, lg, lb, row):
    sg = _sigmoid(g)
    u0 = a * sg
    uc = jnp.zeros_like(u0) + cb
    for k in range(CONV_KERNEL):
        uc = uc + w_ref[k:k + 1, :] * _shift_down(u0, CONV_KERNEL - 1 - k, row)
    ul, xh, r = _ln_fwd(uc, lg, lb)
    su = _sigmoid(ul)
    u = ul * su
    return sg, u0, ul, xh, r, su, u


def _conv_fwd(ag, conv_w, conv_b, ln_g, ln_b, norm_g, B, S, CW):
    def body(a_ref, g_ref, w_ref, cb_ref, lg_ref, lb_ref, ng_ref, o_ref):
        row = lax.broadcasted_iota(jnp.int32, (S, CW), 0)
        _, _, _, _, _, _, u = _conv_branch_fwd_math(a_ref[0], g_ref[0], w_ref, cb_ref[...], lg_ref[...],
                                                    lb_ref[...], row)
        rr = lax.rsqrt(jnp.mean(u * u, axis=-1, keepdims=True) + LN_EPS)
        o_ref[0] = (u * rr * ng_ref[...]).astype(BF16)

    vec = pl.BlockSpec((1, CW), lambda b: (0, 0))
    out = pl.pallas_call(
        body, grid=(B,),
        in_specs=[pl.BlockSpec((1, S, CW), lambda b: (b, 0, 0)), pl.BlockSpec((1, S, CW), lambda b: (b, 0, 1)),
                  pl.BlockSpec((CONV_KERNEL, CW), lambda b: (0, 0)), vec, vec, vec, vec],
        out_specs=pl.BlockSpec((1, S, CW), lambda b: (b, 0, 0)),
        out_shape=jax.ShapeDtypeStruct((B, S, CW), BF16),
        compiler_params=_params(1), name="conv_fwd",
    )(ag.reshape(B, S, 2 * CW), ag.reshape(B, S, 2 * CW), conv_w, conv_b, ln_g, ln_b, norm_g)
    return out.reshape(B * S, CW)


def _conv_bwd(ag, dmixed, conv_w, conv_b, ln_g, ln_b, norm_g, B, S, CW, D):
    AW = D - CW
    assert AW % CW == 0

    def body(a_ref, g_ref, dm_ref, w_ref, cb_ref, lg_ref, lb_ref, ng_ref,
             dag_ref, dw_ref, dcb_ref, dlg_ref, dlb_ref, dng_ref):
        b = pl.program_id(0)
        row = lax.broadcasted_iota(jnp.int32, (S, CW), 0)
        a, g = a_ref[0], g_ref[0]
        sg, u0, ul, xh, r, su, u = _conv_branch_fwd_math(a, g, w_ref, cb_ref[...], lg_ref[...], lb_ref[...], row)
        rr = lax.rsqrt(jnp.mean(u * u, axis=-1, keepdims=True) + LN_EPS)
        dm = dm_ref[0]
        dxn = dm * ng_ref[...]
        du = rr * (dxn - u * (rr * rr) * jnp.mean(dxn * u, axis=-1, keepdims=True))
        dul = du * su * (1.0 + ul * (1.0 - su))
        duc = _ln_bwd(dul, xh, r, lg_ref[...])
        first = b == 0
        _accumulate(dng_ref, first, jnp.sum(dm * u * rr, axis=0, keepdims=True))
        _accumulate(dlg_ref, first, jnp.sum(dul * xh, axis=0, keepdims=True))
        _accumulate(dlb_ref, first, jnp.sum(dul, axis=0, keepdims=True))
        _accumulate(dcb_ref, first, jnp.sum(duc, axis=0, keepdims=True))

        @pl.when(first)
        def _():
            dw_ref[...] = jnp.zeros_like(dw_ref)

        du0 = jnp.zeros_like(u0)
        for k in range(CONV_KERNEL):
            sh = CONV_KERNEL - 1 - k
            dw_ref[k:k + 1, :] += jnp.sum(duc * _shift_down(u0, sh, row), axis=0, keepdims=True)
            du0 = du0 + w_ref[k:k + 1, :] * _shift_up(duc, sh, row)
        dag_ref[0, :, :CW] = du0 * sg
        dag_ref[0, :, CW:] = du0 * a * sg * (1.0 - sg)

    vec = pl.BlockSpec((1, CW), lambda b: (0, 0))
    wspec = pl.BlockSpec((CONV_KERNEL, CW), lambda b: (0, 0))
    agv = ag.reshape(B, S, 2 * CW)
    res = pl.pallas_call(
        body, grid=(B,),
        in_specs=[pl.BlockSpec((1, S, CW), lambda b: (b, 0, 0)), pl.BlockSpec((1, S, CW), lambda b: (b, 0, 1)),
                  pl.BlockSpec((1, S, CW), lambda b: (b, 0, AW // CW)), wspec, vec, vec, vec, vec],
        out_specs=[pl.BlockSpec((1, S, 2 * CW), lambda b: (b, 0, 0)), wspec, vec, vec, vec, vec],
        out_shape=[jax.ShapeDtypeStruct((B, S, 2 * CW), F32), jax.ShapeDtypeStruct((CONV_KERNEL, CW), F32)]
        + [jax.ShapeDtypeStruct((1, CW), F32)] * 4,
        compiler_params=_params(1), name="conv_bwd",
    )(agv, agv, dmixed.reshape(B, S, D), conv_w, conv_b, ln_g, ln_b, norm_g)
    return (res[0].reshape(B * S, 2 * CW),) + tuple(res[1:])


def _ffn_conv(x, w_ref, bias, row):
    y = jnp.zeros_like(x) + bias
    for k in range(FFN_CONV_KERNEL):
        y = y + w_ref[k:k + 1, :] * _shift_down(x, FFN_CONV_KERNEL - 1 - k, row)
    return y


def _ffn_specs(S, tc, nj, order):
    pick = (lambda b, j: (b, j)) if order == "bj" else (lambda j, b: (b, j))
    act = lambda off: pl.BlockSpec((1, S, tc), lambda *g: (pick(*g)[0], 0, off + pick(*g)[1]))
    cw = lambda off: pl.BlockSpec((FFN_CONV_KERNEL, tc), lambda *g: (0, off + pick(*g)[1]))
    cb = lambda off: pl.BlockSpec((1, tc), lambda *g: (0, off + pick(*g)[1]))
    return act, cw, cb


def _ffn_act(upre, cw, cb, B, S, DFF):
    tc = FFN_COLS
    nj = DFF // tc

    def body(ug_ref, uv_ref, wg_ref, wv_ref, bg_ref, bv_ref, o_ref):
        row = lax.broadcasted_iota(jnp.int32, (S, tc), 0)
        gate = _ffn_conv(ug_ref[0], wg_ref, bg_ref[...], row)
        val = _ffn_conv(uv_ref[0], wv_ref, bv_ref[...], row)
        o_ref[0] = (gate * _sigmoid(gate) * val).astype(BF16)

    act, cws, cbs = _ffn_specs(S, tc, nj, "bj")
    uv = upre.reshape(B, S, 2 * DFF)
    out = pl.pallas_call(
        body, grid=(B, nj), in_specs=[act(0), act(nj), cws(0), cws(nj), cbs(0), cbs(nj)], out_specs=act(0),
        out_shape=jax.ShapeDtypeStruct((B, S, DFF), BF16), compiler_params=_params(2), name="ffn_act",
    )(uv, uv, cw, cw, cb, cb)
    return out.reshape(B * S, DFF)


def _ffn_bwd(upre, dact, cw, cb, B, S, DFF):
    tc = FFN_COLS
    nj = DFF // tc

    def body(ug_ref, uv_ref, da_ref, wg_ref, wv_ref, bg_ref, bv_ref, dug_ref, duv_ref, dwg_ref, dwv_ref,
             dbg_ref, dbv_ref):
        first = pl.program_id(1) == 0
        row = lax.broadcasted_iota(jnp.int32, (S, tc), 0)
        ug, uv = ug_ref[0], uv_ref[0]
        gate = _ffn_conv(ug, wg_ref, bg_ref[...], row)
        val = _ffn_conv(uv, wv_ref, bv_ref[...], row)
        sg = _sigmoid(gate)
        dact_b = da_ref[0]
        dgate = dact_b * val * sg * (1.0 + gate * (1.0 - sg))
        dval = dact_b * gate * sg
        for dup, u, w_ref, du_ref, dw_ref, db_ref in ((dgate, ug, wg_ref, dug_ref, dwg_ref, dbg_ref),
                                                      (dval, uv, wv_ref, duv_ref, dwv_ref, dbv_ref)):
            _accumulate(db_ref, first, jnp.sum(dup, axis=0, keepdims=True))

            @pl.when(first)
            def _(dw_ref=dw_ref):
                dw_ref[...] = jnp.zeros_like(dw_ref)

            dupre = jnp.zeros_like(dup)
            for k in range(FFN_CONV_KERNEL):
                sh = FFN_CONV_KERNEL - 1 - k
                dw_ref[k:k + 1, :] += jnp.sum(dup * _shift_down(u, sh, row), axis=0, keepdims=True)
                dupre = dupre + w_ref[k:k + 1, :] * _shift_up(dup, sh, row)
            du_ref[0] = dupre.astype(BF16)

    act, cws, cbs = _ffn_specs(S, tc, nj, "jb")
    uv = upre.reshape(B, S, 2 * DFF)
    res = pl.pallas_call(
        body, grid=(nj, B),
        in_specs=[act(0), act(nj), act(0), cws(0), cws(nj), cbs(0), cbs(nj)],
        out_specs=[act(0), act(0), cws(0), cws(0), cbs(0), cbs(0)],
        out_shape=[jax.ShapeDtypeStruct((B, S, DFF), BF16)] * 2
        + [jax.ShapeDtypeStruct((FFN_CONV_KERNEL, DFF), F32)] * 2 + [jax.ShapeDtypeStruct((1, DFF), F32)] * 2,
        compiler_params=_params(2), name="ffn_bwd",
    )(uv, uv, dact.reshape(B, S, DFF), cw, cw, cb, cb)
    flat = lambda t: t.reshape(B * S, DFF)
    return (flat(res[0]), flat(res[1]), jnp.concatenate([res[2], res[3]], axis=1),
            jnp.concatenate([res[4], res[5]], axis=1))


def _w_up_block_spec(w_up_sh, tc, off, order):
    _, D, cs = w_up_sh.shape
    assert cs % tc == 0
    bps = cs // tc
    jj = (lambda b, j: j) if order == "bj" else (lambda j, b: j)
    return pl.BlockSpec((1, D, tc), lambda *g: ((off + jj(*g)) // bps, 0, (off + jj(*g)) % bps))


def _ffn_fwd_fused(x1b, w_up_sh, cw, cb, B, S, DFF):
    tc = FFN_COLS
    nj = DFF // tc
    D = x1b.shape[1]

    def body(x_ref, wg_ref, wv_ref, cwg_ref, cwv_ref, cbg_ref, cbv_ref, o_ref):
        w = jnp.concatenate([wg_ref[0], wv_ref[0]], axis=1)
        up = jnp.dot(x_ref[0], w, preferred_element_type=F32)
        row = lax.broadcasted_iota(jnp.int32, (S, tc), 0)
        gate = _ffn_conv(up[:, :tc], cwg_ref, cbg_ref[...], row)
        val = _ffn_conv(up[:, tc:], cwv_ref, cbv_ref[...], row)
        o_ref[0] = (gate * _sigmoid(gate) * val).astype(BF16)

    act, cws, cbs = _ffn_specs(S, tc, nj, "bj")
    out = pl.pallas_call(
        body, grid=(B, nj),
        in_specs=[pl.BlockSpec((1, S, D), lambda b, j: (b, 0, 0)),
                  _w_up_block_spec(w_up_sh, tc, 0, "bj"), _w_up_block_spec(w_up_sh, tc, nj, "bj"),
                  cws(0), cws(nj), cbs(0), cbs(nj)],
        out_specs=act(0), out_shape=jax.ShapeDtypeStruct((B, S, DFF), BF16),
        compiler_params=_params(2), name="ffn_fwd",
    )(x1b.reshape(B, S, D), w_up_sh, w_up_sh, cw, cw, cb, cb)
    return out.reshape(B * S, DFF)


def _ffn_bwd_fused(x1b, dz2b, w_up_sh, w_down, cw, cb, B, S, DFF):
    tc = FFN_COLS
    nj = DFF // tc
    D = x1b.shape[1]

    def body(x_ref, dz_ref, wg_ref, wv_ref, wd_ref, cwg_ref, cwv_ref, cbg_ref, cbv_ref,
             dug_ref, duv_ref, dwu_ref, dwd_ref, dcw_ref, dcb_ref):
        first = pl.program_id(1) == 0
        x = x_ref[0]
        dz = dz_ref[0]
        w = jnp.concatenate([wg_ref[0], wv_ref[0]], axis=1)
        up = jnp.dot(x, w, preferred_element_type=F32)
        row = lax.broadcasted_iota(jnp.int32, (S, tc), 0)
        ug, uv = up[:, :tc], up[:, tc:]
        gate = _ffn_conv(ug, cwg_ref, cbg_ref[...], row)
        val = _ffn_conv(uv, cwv_ref, cbv_ref[...], row)
        sg = _sigmoid(gate)
        act = (gate * sg * val).astype(BF16)
        dact = _dot(dz, wd_ref[...], "nt")
        dgate = dact * val * sg * (1.0 + gate * (1.0 - sg))
        dval = dact * gate * sg

        @pl.when(first)
        def _():
            dcw_ref[...] = jnp.zeros_like(dcw_ref)

        dupre = []
        for h, (dup, u, w_ref) in enumerate(((dgate, ug, cwg_ref), (dval, uv, cwv_ref))):
            _accumulate(dcb_ref.at[h], first, jnp.sum(dup, axis=0, keepdims=True))
            acc = jnp.zeros_like(dup)
            for k in range(FFN_CONV_KERNEL):
                sh = FFN_CONV_KERNEL - 1 - k
                dcw_ref[h, k:k + 1, :] += jnp.sum(dup * _shift_down(u, sh, row), axis=0, keepdims=True)
                acc = acc + w_ref[k:k + 1, :] * _shift_up(dup, sh, row)
            dupre.append(acc.astype(BF16))
        dug_ref[0] = dupre[0]
        duv_ref[0] = dupre[1]
        dw_t = _dot(jnp.concatenate(dupre, axis=1), x, "tn")
        _accumulate(dwu_ref.at[0], first, dw_t[:tc])
        _accumulate(dwu_ref.at[1], first, dw_t[tc:])
        _accumulate(dwd_ref, first, _dot(act, dz, "tn"))

    act_s, cws, cbs = _ffn_specs(S, tc, nj, "jb")
    seq = pl.BlockSpec((1, S, D), lambda j, b: (b, 0, 0))
    res = pl.pallas_call(
        body, grid=(nj, B),
        in_specs=[seq, seq, _w_up_block_spec(w_up_sh, tc, 0, "jb"), _w_up_block_spec(w_up_sh, tc, nj, "jb"),
                  pl.BlockSpec((tc, D), lambda j, b: (j, 0)), cws(0), cws(nj), cbs(0), cbs(nj)],
        out_specs=[act_s(0), act_s(0), pl.BlockSpec((2, tc, D), lambda j, b: (0, j, 0)),
                   pl.BlockSpec((tc, D), lambda j, b: (j, 0)),
                   pl.BlockSpec((2, FFN_CONV_KERNEL, tc), lambda j, b: (0, 0, j)),
                   pl.BlockSpec((2, 1, tc), lambda j, b: (0, 0, j))],
        out_shape=[jax.ShapeDtypeStruct((B, S, DFF), BF16)] * 2
        + [jax.ShapeDtypeStruct((2, DFF, D), F32), jax.ShapeDtypeStruct((DFF, D), F32),
           jax.ShapeDtypeStruct((2, FFN_CONV_KERNEL, DFF), F32), jax.ShapeDtypeStruct((2, 1, DFF), F32)],
        compiler_params=_params(2), name="ffn_bwd",
    )(x1b.reshape(B, S, D), dz2b.reshape(B, S, D), w_up_sh, w_up_sh, w_down, cw, cw, cb, cb)
    flat = lambda t: t.reshape(B * S, DFF)
    return flat(res[0]), flat(res[1]), res[2], res[3], res[4], res[5]


def _transpose(x, tr, name):
    R, C = x.shape

    def body(x_ref, o_ref):
        o_ref[...] = x_ref[...].T

    return pl.pallas_call(
        body, grid=(R // tr,), in_specs=[pl.BlockSpec((tr, C), lambda i: (i, 0))],
        out_specs=pl.BlockSpec((C, tr), lambda i: (0, i)), out_shape=jax.ShapeDtypeStruct((C, R), F32),
        compiler_params=_params(1), name=name)(x)


def _dh_cat(dq, dk, dv, dag, tm):
    T, AW = dq.shape
    CW2 = dag.shape[1]
    W = 3 * AW + CW2

    def body(dq_ref, dk_ref, dv_ref, dag_ref, dh_ref, cs_ref):
        for c, ref in enumerate((dq_ref, dk_ref, dv_ref)):
            dh_ref[:, c * AW:(c + 1) * AW] = ref[...]
        dg = dag_ref[...]
        dh_ref[:, 3 * AW:] = dg.astype(BF16)
        _accumulate(cs_ref, pl.program_id(0) == 0, jnp.sum(dg, axis=0, keepdims=True))

    row = pl.BlockSpec((tm, AW), lambda i: (i, 0))
    return pl.pallas_call(
        body, grid=(T // tm,),
        in_specs=[row] * 3 + [pl.BlockSpec((tm, CW2), lambda i: (i, 0))],
        out_specs=[pl.BlockSpec((tm, W), lambda i: (i, 0)), pl.BlockSpec((1, CW2), lambda i: (0, 0))],
        out_shape=[jax.ShapeDtypeStruct((T, W), BF16), jax.ShapeDtypeStruct((1, CW2), F32)],
        compiler_params=_params(1), name="dh_cat",
    )(dq, dk, dv, dag)


def _local_step(x, target, rel_table, w_in, b_in, conv_w, conv_b, conv_ln_g, conv_ln_b, attn_norm_g,
                conv_norm_g, w_out, ln1_g, ln1_b, w_up_sh, ffn_cw, ffn_cb, w_down, ln2_g, ln2_b):
    B, S, D = x.shape
    T = B * S
    AW = attn_norm_g.shape[-1]
    CW = conv_norm_g.shape[-1]
    H = AW // HEAD_DIM
    DFF = w_down.shape[0]
    INW = 3 * AW + 2 * CW
    xf = x.reshape(T, D)
    tf = target.reshape(T, D)
    tm = _row_tile(T, 512)
    tm_s = _row_tile(T, 256)

    bucket_np, mask_np = _bucket_tables()
    bucket = jnp.asarray(bucket_np)
    band_mask = jnp.asarray(mask_np)
    bias_all = _bias_build(rel_table.T, bucket, band_mask).reshape(3, H, ATTN_BLOCK, 2 * ATTN_BLOCK)

    tn_qkv = _col_tile(3 * AW, 1152)
    qkv = _mm_plain(xf, w_in[:, :3 * AW], mode="nn", tm=tm, tn=tn_qkv, tk=D, out_dtype=BF16,
                    bias=b_in[:, :3 * AW], name="mm_qkv")
    ag = _mm_plain(xf, w_in[:, 3 * AW:], mode="nn", tm=tm, tn=2 * CW, tk=D, out_dtype=F32,
                   bias=b_in[:, 3 * AW:], name="mm_ag")

    attn, lse = _attention_fwd(qkv, bias_all, B, S, AW)
    mixed_a, r_attn = _attn_norm(attn, attn_norm_g, tm_s)
    mixed_c = _conv_fwd(ag, conv_w, conv_b, conv_ln_g, conv_ln_b, conv_norm_g, B, S, CW)
    mixed = jnp.concatenate([mixed_a, mixed_c], axis=1)

    def ln1_epilogue(acc, i, j, extra_refs, out_refs):
        x_ref, g_ref, b_ref = extra_refs
        x1, xh, r = _ln_fwd(acc + ALPHA * x_ref[...], g_ref[...], b_ref[...])
        out_refs[0][...] = x1
        out_refs[1][...] = x1.astype(BF16)
        out_refs[2][...] = xh
        out_refs[3][...] = jnp.broadcast_to(r, (tm_s, LANES))

    rowD = lambda i, j, k: (i, 0)
    vecD = lambda i, j, k: (0, 0)
    x1, x1b, xh1, r1 = _matmul(
        mixed, w_out, mode="nn", tm=tm_s, tn=D, tk=D,
        extras=[(xf, (tm_s, D), rowD), (ln1_g, (1, D), vecD), (ln1_b, (1, D), vecD)],
        outs=[((T, D), F32, (tm_s, D), rowD), ((T, D), BF16, (tm_s, D), rowD), ((T, D), F32, (tm_s, D), rowD),
              ((T, LANES), F32, (tm_s, LANES), rowD)],
        epilogue=ln1_epilogue, name="mm_out_ln1")

    NS, _, cs = w_up_sh.shape
    half = NS // 2

    act = _ffn_fwd_fused(x1b, w_up_sh, ffn_cw, ffn_cb, B, S, DFF)

    def ln2_epilogue(acc, i, j, extra_refs, out_refs):
        x1_ref, g_ref, b_ref, t_ref = extra_refs
        dz_ref, dzb_ref, loss_ref, dg_ref, db_ref = out_refs
        g = g_ref[...]
        y, xh, r = _ln_fwd(acc + ALPHA * x1_ref[...], g, b_ref[...])
        diff = y - t_ref[...]
        row_loss = jnp.sum(diff * diff, axis=1, keepdims=True)
        tile_loss = jnp.sum(row_loss, axis=0, keepdims=True) * (0.5 / D)
        dy = diff * (1.0 / D)
        dz = _ln_bwd(dy, xh, r, g)
        dz_ref[...] = dz
        dzb_ref[...] = dz.astype(BF16)
        first = i == 0
        _accumulate(loss_ref, first, jnp.broadcast_to(tile_loss, (1, LANES)))
        _accumulate(dg_ref, first, jnp.sum(dy * xh, axis=0, keepdims=True))
        _accumulate(db_ref, first, jnp.sum(dy, axis=0, keepdims=True))

    dz2, dz2b, loss_part, d_ln2_g, d_ln2_b = _matmul(
        act, w_down, mode="nn", tm=tm_s, tn=D, tk=DFF,
        extras=[(x1, (tm_s, D), rowD), (ln2_g, (1, D), vecD), (ln2_b, (1, D), vecD), (tf, (tm_s, D), rowD)],
        outs=[((T, D), F32, (tm_s, D), rowD), ((T, D), BF16, (tm_s, D), rowD),
              ((1, LANES), F32, (1, LANES), vecD), ((1, D), F32, (1, D), vecD), ((1, D), F32, (1, D), vecD)],
        epilogue=ln2_epilogue, name="mm_down_ln2_loss")

    dupre_g, dupre_v, d_w_up_t, d_w_down, d_ffn_cw2, d_ffn_cb2 = _ffn_bwd_fused(
        x1b, dz2b, w_up_sh, w_down, ffn_cw, ffn_cb, B, S, DFF)
    d_w_up_t = d_w_up_t.reshape(NS, cs, D)
    d_ffn_cw = jnp.transpose(d_ffn_cw2, (1, 0, 2)).reshape(FFN_CONV_KERNEL, 2 * DFF)
    d_ffn_cb = d_ffn_cb2.reshape(1, 2 * DFF)
    tk_t = _row_tile(T, 512)

    def ln1_bwd_epilogue(acc, i, j, extra_refs, out_refs):
        dz2_ref, xh_ref, r_ref, g_ref = extra_refs
        dz_ref, dzb_ref, dg_ref, db_ref = out_refs
        dx1 = acc + ALPHA * dz2_ref[...]
        xh = xh_ref[...]
        dz = _ln_bwd(dx1, xh, r_ref[:, 0:1], g_ref[...])
        dz_ref[...] = dz
        dzb_ref[...] = dz.astype(BF16)
        first = i == 0
        _accumulate(dg_ref, first, jnp.sum(dx1 * xh, axis=0, keepdims=True))
        _accumulate(db_ref, first, jnp.sum(dx1, axis=0, keepdims=True))

    dz1, dz1b, d_ln1_g, d_ln1_b = _matmul_general(
        [(dupre_g, (tm, cs), lambda i, j, k: (i, jnp.minimum(k, half - 1))),
         (dupre_v, (tm, cs), lambda i, j, k: (i, jnp.maximum(k - half, 0))),
         (w_up_sh, (1, D, cs), lambda i, j, k: (k, 0, 0))],
        lambda refs, i, j, k: _dot(jnp.where(k < half, refs[0][...], refs[1][...]), refs[2][0], "nt"),
        grid=(T // tm, 1, NS), tm=tm, tn=D,
        extras=[(dz2, (tm, D), rowD), (xh1, (tm, D), rowD), (r1, (tm, LANES), rowD), (ln1_g, (1, D), vecD)],
        outs=[((T, D), F32, (tm, D), rowD), ((T, D), BF16, (tm, D), rowD),
              ((1, D), F32, (1, D), vecD), ((1, D), F32, (1, D), vecD)],
        epilogue=ln1_bwd_epilogue, name="mm_dx1_ln1_bwd")

    dmixed = _mm_plain(dz1b, w_out, mode="nt", tm=tm, tn=D, tk=D, out_dtype=F32, name="mm_dmixed")
    d_w_out = _mm_plain(mixed, dz1b, mode="tn", tm=D, tn=D, tk=tk_t, out_dtype=F32, name="mm_dw_out")

    dattn, dd, d_attn_norm_g = _attn_pre_bwd(dmixed, attn, r_attn, attn_norm_g, tm_s)
    dag, d_conv_w, d_conv_b, d_conv_ln_g, d_conv_ln_b, d_conv_norm_g = _conv_bwd(
        ag, dmixed, conv_w, conv_b, conv_ln_g, conv_ln_b, conv_norm_g, B, S, CW, D)

    dq, dk, dv, csq, csk, csv, dbias = _attention_bwd(qkv, dattn, lse, dd, bias_all, B, S, AW)
    d_rel_table = _rel_grad(dbias.reshape(3, H, ATTN_BLOCK * 2 * ATTN_BLOCK), bucket).T
    dh, cs_ag = _dh_cat(dq, dk, dv, dag, tm_s)
    d_b_in = jnp.concatenate([csq, csk, csv, cs_ag], axis=1)

    def gx_epilogue(acc, i, j, extra_refs, out_refs):
        out_refs[0][...] = acc + ALPHA * extra_refs[0][...]

    grad_x = _matmul(dh, w_in, mode="nt", tm=tm_s, tn=D, tk=INW,
                     extras=[(dz1, (tm_s, D), rowD)], outs=[((T, D), F32, (tm_s, D), rowD)],
                     epilogue=gx_epilogue, name="mm_grad_x")[0]
    d_w_in = _mm_plain(xf, dh, mode="tn", tm=D, tn=_col_tile(INW, 1408), tk=tk_t, out_dtype=F32, name="mm_dw_in")

    grads = dict(rel_table=d_rel_table, w_in=d_w_in, b_in=d_b_in, conv_w=d_conv_w, conv_b=d_conv_b,
                 conv_ln_g=d_conv_ln_g, conv_ln_b=d_conv_ln_b, attn_norm_g=d_attn_norm_g,
                 conv_norm_g=d_conv_norm_g, w_out=d_w_out, ln1_g=d_ln1_g, ln1_b=d_ln1_b, w_up_t=d_w_up_t,
                 ffn_conv_w=d_ffn_cw, ffn_conv_b=d_ffn_cb, w_down=d_w_down,
                 ln2_g=d_ln2_g, ln2_b=d_ln2_b)
    return loss_part, grad_x.reshape(B, S, D), grads


def _place():
    return lax.axis_index("x"), lax.axis_index("y"), lax.axis_index("c")


CHIP_FLIPS = ((1, 0), (0, 1), (1, 1))


def _flip(v, f):
    return 1 - v if f else v


HBM_SPEC = pl.BlockSpec(memory_space=pl.ANY)
VMEM_SPEC = pl.BlockSpec(memory_space=pltpu.VMEM)
COMM_PARAMS = pltpu.CompilerParams(vmem_limit_bytes=VMEM_LIMIT)


def _gather_weights(big, small):
    nb, ns = len(big), len(small)

    def body(*refs):
        big_in = refs[:nb]
        small_in = refs[nb:nb + ns]
        big_out = refs[nb + ns:2 * nb + ns]
        small_out = refs[2 * nb + ns:2 * nb + 2 * ns]
        stages = refs[2 * nb + 2 * ns:3 * nb + 2 * ns]
        send_sems, recv_sems, local_sems = refs[3 * nb + 2 * ns:]
        x, y, c = _place()
        s_me = 2 * x + y
        sibling = (x, y, 1 - c)
        started, local_copies = [], []
        for a in range(nb):
            rh = big[a].shape[0] // 2
            lo = pl.multiple_of(c * rh, 16)
            stages[a][...] = big_in[a][pl.ds(lo, rh), :].astype(BF16)
            mine = big_out[a].at[s_me, pl.ds(lo, rh), :]
            loc = pltpu.make_async_copy(stages[a], mine, local_sems.at[a])
            loc.start()
            local_copies.append(loc)
            targets = [sibling] + [(_flip(x, fx), _flip(y, fy), c) for fx, fy in CHIP_FLIPS]
            for k, to in enumerate(targets):
                cp = pltpu.make_async_remote_copy(stages[a], mine, send_sems.at[a * 7 + k],
                                                  recv_sems.at[a * 7 + k], device_id=to, device_id_type=MESH)
                cp.start()
                started.append(cp)
        for a in range(ns):
            mine = small_out[a].at[s_me]
            loc = pltpu.make_async_copy(small_in[a], mine, local_sems.at[nb + a])
            loc.start()
            local_copies.append(loc)
            for k, (fx, fy) in enumerate(CHIP_FLIPS):
                cp = pltpu.make_async_remote_copy(small_in[a], mine, send_sems.at[nb * 7 + a * 3 + k],
                                                  recv_sems.at[nb * 7 + a * 3 + k],
                                                  device_id=(_flip(x, fx), _flip(y, fy), c), device_id_type=MESH)
                cp.start()
                started.append(cp)
        for a in range(nb):
            rh = big[a].shape[0] // 2
            lo = pl.multiple_of(c * rh, 16)
            for k, (fx, fy) in enumerate(CHIP_FLIPS):
                s_from = 2 * _flip(x, fx) + _flip(y, fy)
                got = big_out[a].at[s_from, pl.ds(lo, rh), :]
                pltpu.make_async_remote_copy(got, got, send_sems.at[a * 7 + 1 + k], recv_sems.at[a * 7 + 1 + k],
                                             device_id=sibling, device_id_type=MESH).wait_recv()
                fwd = pltpu.make_async_remote_copy(got, got, send_sems.at[a * 7 + 4 + k],
                                                   recv_sems.at[a * 7 + 4 + k], device_id=sibling,
                                                   device_id_type=MESH)
                fwd.start()
                started.append(fwd)
        for a in range(nb):
            rh = big[a].shape[0] // 2
            lo_sib = pl.multiple_of((1 - c) * rh, 16)
            for k in (0, 4, 5, 6):
                any_rows = big_out[a].at[s_me, pl.ds(lo_sib, rh), :]
                pltpu.make_async_remote_copy(any_rows, any_rows, send_sems.at[a * 7 + k], recv_sems.at[a * 7 + k],
                                             device_id=sibling, device_id_type=MESH).wait_recv()
        for a in range(ns):
            for k in range(3):
                pltpu.make_async_remote_copy(small_in[a], small_out[a].at[s_me], send_sems.at[nb * 7 + a * 3 + k],
                                             recv_sems.at[nb * 7 + a * 3 + k], device_id=sibling,
                                             device_id_type=MESH).wait_recv()
        for cp in started:
            cp.wait_send()
        for cp in local_copies:
            cp.wait()

    n_sem = nb * 7 + ns * 3
    out_shape = ([jax.ShapeDtypeStruct((N_SHARDS,) + w.shape, BF16) for w in big]
                 + [jax.ShapeDtypeStruct((N_SHARDS,) + w.shape, F32) for w in small])
    res = pl.pallas_call(
        body, in_specs=[VMEM_SPEC] * nb + [HBM_SPEC] * ns, out_specs=[HBM_SPEC] * (nb + ns),
        out_shape=out_shape,
        scratch_shapes=[pltpu.VMEM((w.shape[0] // 2, w.shape[1]), BF16) for w in big]
        + [pltpu.SemaphoreType.DMA((n_sem,)), pltpu.SemaphoreType.DMA((n_sem,)),
           pltpu.SemaphoreType.DMA((nb + ns,))],
        compiler_params=COMM_PARAMS, name="gather_weights",
    )(*big, *small)
    return res[:nb], res[nb:]


def _sibling_exchange(grads):
    n = len(grads)

    def body(*refs):
        g_in = refs[:n]
        got = refs[n:2 * n]
        send_sems, recv_sems = refs[2 * n:]
        x, y, c = _place()
        cps = []
        for a in range(n):
            rh = grads[a].shape[1] // 2
            lo = pl.multiple_of((1 - c) * rh, 8)
            cp = pltpu.make_async_remote_copy(g_in[a].at[:, pl.ds(lo, rh), :], got[a], send_sems.at[a],
                                              recv_sems.at[a], device_id=(x, y, 1 - c), device_id_type=MESH)
            cp.start()
            cps.append(cp)
        for cp in cps:
            cp.wait()

    return pl.pallas_call(
        body, in_specs=[HBM_SPEC] * n, out_specs=[HBM_SPEC] * n,
        out_shape=[jax.ShapeDtypeStruct((N_SHARDS, g.shape[1] // 2, g.shape[2]), F32) for g in grads],
        scratch_shapes=[pltpu.SemaphoreType.DMA((n,)), pltpu.SemaphoreType.DMA((n,))],
        compiler_params=COMM_PARAMS, name="sibling_exchange",
    )(*grads)


def _chip_exchange(chip_parts, pack):
    n = len(chip_parts)

    def body(*refs):
        parts = refs[:n]
        pack_ref = refs[n]
        got = refs[n + 1:2 * n + 1]
        all_packs = refs[2 * n + 1]
        send_sems, recv_sems, local_sem = refs[2 * n + 2:]
        x, y, c = _place()
        me = 4 * x + 2 * y + c
        cps = []
        for a in range(n):
            for k, (fx, fy) in enumerate(CHIP_FLIPS):
                px, py = _flip(x, fx), _flip(y, fy)
                cp = pltpu.make_async_remote_copy(parts[a].at[2 * px + py], got[a].at[k], send_sems.at[a * 3 + k],
                                                  recv_sems.at[a * 3 + k], device_id=(px, py, c),
                                                  device_id_type=MESH)
                cp.start()
                cps.append(cp)
        loc = pltpu.make_async_copy(pack_ref, all_packs.at[me], local_sem)
        loc.start()
        for m in range(1, N_DEV):
            to = (_flip(x, m & 4), _flip(y, m & 2), _flip(c, m & 1))
            cp = pltpu.make_async_remote_copy(pack_ref, all_packs.at[me], send_sems.at[n * 3 + m - 1],
                                              recv_sems.at[n * 3 + m - 1], device_id=to, device_id_type=MESH)
            cp.start()
            cps.append(cp)
        for cp in cps:
            cp.wait()
        loc.wait()

    rs = pack.shape[0]
    res = pl.pallas_call(
        body, in_specs=[HBM_SPEC] * (n + 1), out_specs=[HBM_SPEC] * (n + 1),
        out_shape=[jax.ShapeDtypeStruct((3,) + p.shape[1:], BF16) for p in chip_parts]
        + [jax.ShapeDtypeStruct((N_DEV, rs, LANES), F32)],
        scratch_shapes=[pltpu.SemaphoreType.DMA((n * 3 + N_DEV - 1,)), pltpu.SemaphoreType.DMA((n * 3 + N_DEV - 1,)),
                        pltpu.SemaphoreType.DMA],
        compiler_params=COMM_PARAMS, name="chip_exchange",
    )(*chip_parts, pack)
    return res[:n], res[n]


def _sibling_assemble(fulls):
    n = len(fulls)

    def body(*refs):
        full = refs[n:2 * n]
        send_sems, recv_sems = refs[2 * n:]
        x, y, c = _place()
        cps = []
        for a in range(n):
            rh = fulls[a].shape[0] // 2
            mine = full[a].at[pl.ds(pl.multiple_of(c * rh, 8), rh), :]
            cp = pltpu.make_async_remote_copy(mine, mine, send_sems.at[a], recv_sems.at[a],
                                              device_id=(x, y, 1 - c), device_id_type=MESH)
            cp.start()
            cps.append(cp)
        for cp in cps:
            cp.wait()

    return pl.pallas_call(
        body, in_specs=[HBM_SPEC] * n, out_specs=[HBM_SPEC] * n,
        out_shape=[jax.ShapeDtypeStruct(f.shape, F32) for f in fulls],
        input_output_aliases={a: a for a in range(n)},
        scratch_shapes=[pltpu.SemaphoreType.DMA((n,)), pltpu.SemaphoreType.DMA((n,))],
        compiler_params=COMM_PARAMS, name="sibling_assemble",
    )(*fulls)


def _half_tile(rh, mult=16, want=256):
    best = None
    for t in range(mult, min(rh, want) + 1, mult):
        if rh % t == 0:
            best = t
    return best if best is not None else rh


def _pair_sum(g, sib, ids, name):
    _, R, C = g.shape
    rh = R // 2
    rt = _half_tile(rh)
    nt = rh // rt

    def body(ids_ref, g_ref, s_ref, o_ref):
        o_ref[...] = (g_ref[...] + s_ref[...]).astype(BF16)

    grid_spec = pltpu.PrefetchScalarGridSpec(
        num_scalar_prefetch=1, grid=(N_SHARDS, nt),
        in_specs=[pl.BlockSpec((1, rt, C), lambda s, i, ids: (s, ids[2] * nt + i, 0)),
                  pl.BlockSpec((1, rt, C), lambda s, i, ids: (s, i, 0))],
        out_specs=pl.BlockSpec((1, rt, C), lambda s, i, ids: (s, i, 0)))
    return pl.pallas_call(body, grid_spec=grid_spec, out_shape=jax.ShapeDtypeStruct((N_SHARDS, rh, C), BF16),
                          compiler_params=_params(2), name=name)(ids, g, sib)


def _final_sum(g, sib, got, ids, name):
    _, R, C = g.shape
    rh = R // 2
    rt = _half_tile(rh)
    nt = rh // rt

    def body(ids_ref, g_ref, s_ref, r_ref, o_ref):
        tot = g_ref[0] + s_ref[0]
        for k in range(3):
            tot = tot + r_ref[k].astype(F32)
        o_ref[...] = tot

    grid_spec = pltpu.PrefetchScalarGridSpec(
        num_scalar_prefetch=1, grid=(nt,),
        in_specs=[pl.BlockSpec((1, rt, C), lambda i, ids: (2 * ids[0] + ids[1], ids[2] * nt + i, 0)),
                  pl.BlockSpec((1, rt, C), lambda i, ids: (2 * ids[0] + ids[1], i, 0)),
                  pl.BlockSpec((3, rt, C), lambda i, ids: (0, i, 0))],
        out_specs=pl.BlockSpec((rt, C), lambda i, ids: (ids[2] * nt + i, 0)))
    return pl.pallas_call(body, grid_spec=grid_spec, out_shape=jax.ShapeDtypeStruct((R, C), F32),
                          compiler_params=_params(1), name=name)(ids, g, sib, got)


def _sum_packs(all_packs):
    def body(p_ref, o_ref):
        tot = p_ref[0]
        for i in range(1, N_DEV):
            tot = tot + p_ref[i]
        o_ref[...] = tot

    return pl.pallas_call(body, in_specs=[VMEM_SPEC], out_specs=VMEM_SPEC,
                          out_shape=jax.ShapeDtypeStruct(all_packs.shape[1:], F32), name="sum_packs")(all_packs)


def _adamw(w, g, m, v, name):
    R, C = w.shape
    rt = _half_tile(R, mult=8, want=256)

    def body(w_ref, g_ref, m_ref, v_ref, d_ref, nm_ref, nv_ref):
        gg = g_ref[...]
        nm = ADAM_B1 * m_ref[...] + (1.0 - ADAM_B1) * gg
        nv = ADAM_B2 * v_ref[...] + (1.0 - ADAM_B2) * (gg * gg)
        m_hat = nm / (1.0 - ADAM_B1 ** ADAM_STEP)
        v_hat = nv / (1.0 - ADAM_B2 ** ADAM_STEP)
        d_ref[...] = -ADAM_LR * (m_hat / (jnp.sqrt(v_hat) + ADAM_EPS) + ADAM_WD * w_ref[...])
        nm_ref[...] = nm
        nv_ref[...] = nv

    spec = pl.BlockSpec((rt, C), lambda i: (i, 0))
    return pl.pallas_call(body, grid=(R // rt,), in_specs=[spec] * 4, out_specs=[spec] * 3,
                          out_shape=[jax.ShapeDtypeStruct((R, C), F32)] * 3,
                          compiler_params=_params(1), name=name)(w, g, m, v)


def _pack(pieces):
    rows = []
    for p in pieces:
        flat = p.reshape(-1)
        pad = (-flat.shape[0]) % LANES
        if pad:
            flat = jnp.concatenate([flat, jnp.zeros((pad,), F32)])
        rows.append(flat.reshape(-1, LANES))
    total = sum(r.shape[0] for r in rows)
    pad_rows = (-total) % 8
    if pad_rows:
        rows.append(jnp.zeros((pad_rows, LANES), F32))
    return jnp.concatenate(rows, axis=0)


def _unpack(buf, shapes):
    out, r0 = [], 0
    for shp in shapes:
        n = int(np.prod(shp))
        nr = -(-n // LANES)
        out.append(buf[r0:r0 + nr].reshape(-1)[:n].reshape(shp))
        r0 += nr
    return out


SMALL_NAMES = ("rel_table", "b_in", "conv_w", "conv_b", "conv_ln_g", "conv_ln_b", "attn_norm_g", "conv_norm_g",
               "ln1_g", "ln1_b", "ffn_conv_w", "ffn_conv_b", "ln2_g", "ln2_b")
BIG_NAMES = ("w_in", "w_out", "w_up", "w_down")
WEIGHT_ORDER = ("rel_table", "w_in", "b_in", "conv_w", "conv_b", "conv_ln_g", "conv_ln_b", "attn_norm_g",
                "conv_norm_g", "w_out", "ln1_g", "ln1_b", "w_up", "ffn_conv_w", "ffn_conv_b", "w_down",
                "ln2_g", "ln2_b")


def kernel(x, rel_table, w_in, b_in, conv_w, conv_b, conv_ln_g, conv_ln_b, attn_norm_g, conv_norm_g, w_out, ln1_g, ln1_b, w_up, ffn_conv_w, ffn_conv_b, w_down, ln2_g, ln2_b, loss_target, m_rel_table, m_w_in, m_b_in, m_conv_w, m_conv_b, m_conv_ln_g, m_conv_ln_b, m_attn_norm_g, m_conv_norm_g, m_w_out, m_ln1_g, m_ln1_b, m_w_up, m_ffn_conv_w, m_ffn_conv_b, m_w_down, m_ln2_g, m_ln2_b, v_rel_table, v_w_in, v_b_in, v_conv_w, v_conv_b, v_conv_ln_g, v_conv_ln_b, v_attn_norm_g, v_conv_norm_g, v_w_out, v_ln1_g, v_ln1_b, v_w_up, v_ffn_conv_w, v_ffn_conv_b, v_w_down, v_ln2_g, v_ln2_b):
    args = dict(locals())
    weights = {n: args[n] for n in WEIGHT_ORDER}
    moms = {n: args["m_" + n] for n in WEIGHT_ORDER}
    vels = {n: args["v_" + n] for n in WEIGHT_ORDER}
    xi, yi, ci = _place()
    ids = jnp.stack([xi, yi, ci]).astype(jnp.int32)
    shard = 2 * xi + yi
    D = x.shape[-1]
    DFF = w_down.shape[1] * N_SHARDS
    CW = conv_norm_g.shape[-1]

    (g_in, g_out, g_up, g_down), (g_cw, g_fcw) = _gather_weights(
        [w_in[0], w_out[0], w_up[0], w_down[0]], [conv_w[0], ffn_conv_w[0]])
    cols = lambda t: jnp.transpose(t, (1, 0, 2)).reshape(t.shape[1], N_SHARDS * t.shape[2])
    w_in_f = cols(g_in)
    w_out_f = g_out.reshape(D, D)
    w_down_f = g_down.reshape(DFF, D)
    conv_w_f = cols(g_cw)
    ffn_cw_f = cols(g_fcw)

    loss_part, grad_x, gl = _local_step(
        x, loss_target, rel_table, w_in_f, b_in, conv_w_f, conv_b, conv_ln_g, conv_ln_b, attn_norm_g, conv_norm_g,
        w_out_f, ln1_g, ln1_b, g_up, ffn_cw_f, ffn_conv_b, w_down_f, ln2_g, ln2_b)

    rows = lambda t: jnp.transpose(t.reshape(t.shape[0], N_SHARDS, t.shape[1] // N_SHARDS), (1, 0, 2))
    big_parts = [rows(gl["w_in"]), gl["w_out"].reshape(N_SHARDS, D // N_SHARDS, D),
                 gl["w_up_t"], gl["w_down"].reshape(N_SHARDS, DFF // N_SHARDS, D)]
    sib = _sibling_exchange(big_parts)
    chip_parts = [_pair_sum(g, s, ids, name="pair_sum_" + n) for g, s, n in zip(big_parts, sib, BIG_NAMES)]

    pack = _pack([loss_part] + [gl[n] for n in SMALL_NAMES])
    got, all_packs = _chip_exchange(chip_parts, pack)
    fulls = [_final_sum(g, s, r, ids, name="final_sum_" + n)
             for g, s, r, n in zip(big_parts, sib, got, BIG_NAMES)]
    big_grads = dict(zip(BIG_NAMES, _sibling_assemble(fulls)))
    big_grads["w_up"] = _transpose(big_grads["w_up"], LANES, name="transpose_dw_up")

    summed = _sum_packs(all_packs)
    full_shapes = {n: weights[n].shape for n in SMALL_NAMES}
    full_shapes["conv_w"] = (1, CONV_KERNEL, CW)
    full_shapes["ffn_conv_w"] = (1, FFN_CONV_KERNEL, 2 * DFF)
    un = _unpack(summed, [(1, LANES)] + [full_shapes[n] for n in SMALL_NAMES])
    loss = un[0][0, 0]
    small_grads = dict(zip(SMALL_NAMES, un[1:]))
    for n in ("conv_w", "ffn_conv_w"):
        width = weights[n].shape[-1]
        small_grads[n] = lax.dynamic_slice_in_dim(small_grads[n], shard * width, width, axis=2)

    grads, delta, new_m, new_v = {}, {}, {}, {}
    for n in BIG_NAMES:
        shp = weights[n].shape
        g2 = big_grads[n]
        d, nm, nv = _adamw(weights[n][0], g2, moms[n][0], vels[n][0], name="adamw_" + n)
        grads[n], delta[n], new_m[n], new_v[n] = (t.reshape(shp) for t in (g2, d, nm, nv))
    sp = lambda src: _pack([src[n] for n in SMALL_NAMES])
    d_s, nm_s, nv_s = _adamw(sp(weights), sp(small_grads), sp(moms), sp(vels), name="adamw_small")
    shapes = [weights[n].shape for n in SMALL_NAMES]
    for tgt, buf in ((delta, d_s), (new_m, nm_s), (new_v, nv_s)):
        tgt.update(zip(SMALL_NAMES, _unpack(buf, shapes)))
    grads.update(small_grads)

    return (loss, grad_x, *[grads[n] for n in WEIGHT_ORDER], *[delta[n] for n in WEIGHT_ORDER],
            *[new_m[n] for n in WEIGHT_ORDER], *[new_v[n] for n in WEIGHT_ORDER])
```

```python
import functools
import math

import numpy as np
import jax
import jax.numpy as jnp
from jax import lax
from jax.experimental import pallas as pl
from jax.experimental.pallas import tpu as pltpu

F32 = jnp.float32
BF16 = jnp.bfloat16
MESH = pl.DeviceIdType.MESH

HEAD_DIM = 64
LANES = 128
ATTN_BLOCK = 128
DILATED_CONFIGS = ((128, 1), (512, 4), (2048, 16))
CONV_KERNEL = 31
FFN_CONV_KERNEL = 3
REL_BUCKETS = 32
REL_MAX_DIST = 2048
DEPTH = 1
ALPHA = (2 * DEPTH) ** 0.25
LN_EPS = 1e-5
NEG_INF = -1e30
QK_SCALE = 1.0 / math.sqrt(HEAD_DIM)
ADAM_LR = 0.001
ADAM_B1 = 0.9
ADAM_B2 = 0.999
ADAM_EPS = 1e-08
ADAM_WD = 0.01
ADAM_STEP = 10
VMEM_LIMIT = 52 * 1024 * 1024
FFN_COLS = 128
N_SHARDS = 4
N_DEV = 8


def _params(n_axes):
    return pltpu.CompilerParams(dimension_semantics=("arbitrary",) * n_axes,
                                vmem_limit_bytes=VMEM_LIMIT)


MM_DIMS = {"nn": (((1,), (0,)), ((), ())), "nt": (((1,), (1,)), ((), ())), "tn": (((0,), (0,)), ((), ()))}


class _Background:
    def __init__(self, in_arrays, out_shapes, aliases, n_sems, run, n_local=1):
        self.in_arrays, self.out_shapes, self.aliases = list(in_arrays), list(out_shapes), dict(aliases)
        self.n_sems, self.n_local, self.run = n_sems, n_local, run

    def scratch(self):
        return [pltpu.SemaphoreType.DMA((self.n_sems,)), pltpu.SemaphoreType.DMA((self.n_sems,)),
                pltpu.SemaphoreType.DMA((self.n_local,))]


def _hosted_call(body, bg, *, grid, in_specs, out_specs, out_shape, scratch_shapes, operands, name):
    n_in, n_out, n_scr = len(in_specs), len(out_specs), len(scratch_shapes)
    if bg is None:
        return pl.pallas_call(lambda *refs: body(refs, lambda post: None), grid=grid, in_specs=in_specs,
                              out_specs=out_specs, out_shape=out_shape, scratch_shapes=scratch_shapes,
                              compiler_params=_params(len(grid)), name=name)(*operands)
    nb_in, nb_out = len(bg.in_arrays), len(bg.out_shapes)
    n_steps = int(np.prod(grid))

    def full_body(*refs):
        own = refs[:n_in] + refs[n_in + nb_in:n_in + nb_in + n_out] \
            + refs[n_in + nb_in + n_out + nb_out:n_in + nb_in + n_out + nb_out + n_scr]
        bg_in = refs[n_in:n_in + nb_in]
        bg_out = refs[n_in + nb_in + n_out:n_in + nb_in + n_out + nb_out]
        sems = refs[n_in + nb_in + n_out + nb_out + n_scr:]
        step = pl.program_id(0)
        for ax in range(1, len(grid)):
            step = step * grid[ax] + pl.program_id(ax)

        def hook(post):
            bg.run(step, n_steps, bg_in, bg_out, *sems, post)

        body(own, hook)

    res = pl.pallas_call(
        full_body, grid=grid, in_specs=list(in_specs) + [HBM_SPEC] * nb_in,
        out_specs=list(out_specs) + [HBM_SPEC] * nb_out, out_shape=list(out_shape) + bg.out_shapes,
        input_output_aliases={n_in + a: n_out + o for a, o in bg.aliases.items()},
        scratch_shapes=list(scratch_shapes) + bg.scratch(), compiler_params=_params(len(grid)), name=name,
    )(*operands, *bg.in_arrays)
    return res


def _matmul_general(ins, part_fn, *, grid, tm, tn, outs, epilogue, extras=(), name, bg=None):
    nk = grid[2]
    n_in, n_extra = len(ins), len(extras)

    def body(refs, bg_hook):
        in_refs = refs[:n_in]
        rest = refs[n_in:]
        extra_refs = rest[:n_extra]
        out_refs = rest[n_extra:n_extra + len(outs)]
        acc_ref = rest[-1]
        i, j, k = pl.program_id(0), pl.program_id(1), pl.program_id(2)
        bg_hook(False)
        part = part_fn(in_refs, i, j, k)
        if nk == 1:
            epilogue(part, i, j, extra_refs, out_refs)
        else:
            @pl.when(k == 0)
            def _():
                acc_ref[...] = part

            @pl.when(k > 0)
            def _():
                acc_ref[...] += part

            @pl.when(k == nk - 1)
            def _():
                epilogue(acc_ref[...], i, j, extra_refs, out_refs)
        bg_hook(True)

    in_specs = [pl.BlockSpec(bs, im) for (_, bs, im) in list(ins) + list(extras)]
    out_specs = [pl.BlockSpec(bs, im) for (_, _, bs, im) in outs]
    out_shape = [jax.ShapeDtypeStruct(s, d) for (s, d, _, _) in outs]
    return _hosted_call(body, bg, grid=grid, in_specs=in_specs, out_specs=out_specs, out_shape=out_shape,
                        scratch_shapes=[pltpu.VMEM((tm, tn), F32)],
                        operands=[e[0] for e in ins] + [e[0] for e in extras], name=name)


def _dot(a, b, mode):
    return lax.dot_general(a.astype(BF16), b.astype(BF16), MM_DIMS[mode], preferred_element_type=F32)


def _matmul(a, b, *, mode, tm, tn, tk, outs, epilogue, extras=(), name, bg=None):
    if mode == "tn":
        K, M = a.shape
        N = b.shape[1]
        ins = [(a, (tk, tm), lambda i, j, k: (k, i)), (b, (tk, tn), lambda i, j, k: (k, j))]
    elif mode == "nt":
        M, K = a.shape
        N = b.shape[0]
        ins = [(a, (tm, tk), lambda i, j, k: (i, k)), (b, (tn, tk), lambda i, j, k: (j, k))]
    else:
        M, K = a.shape
        N = b.shape[1]
        ins = [(a, (tm, tk), lambda i, j, k: (i, k)), (b, (tk, tn), lambda i, j, k: (k, j))]
    assert M % tm == 0 and N % tn == 0 and K % tk == 0, (name, M, N, K, tm, tn, tk)

    def part_fn(in_refs, i, j, k):
        return _dot(in_refs[0][...], in_refs[1][...], mode)

    return _matmul_general(ins, part_fn, grid=(M // tm, N // tn, K // tk), tm=tm, tn=tn, outs=outs,
                           epilogue=epilogue, extras=extras, name=name, bg=bg)


def _plain_out(M, N, tm, tn, dtype):
    return ((M, N), dtype, (tm, tn), lambda i, j, k: (i, j))


def _mm_plain(a, b, *, mode, tm, tn, tk, out_dtype, name, bias=None):
    if mode == "tn":
        M, N = a.shape[1], b.shape[1]
    elif mode == "nt":
        M, N = a.shape[0], b.shape[0]
    else:
        M, N = a.shape[0], b.shape[1]
    extras = []
    if bias is not None:
        extras.append((bias, (1, tn), lambda i, j, k: (0, j)))

    def epilogue(acc, i, j, extra_refs, out_refs):
        if bias is not None:
            acc = acc + extra_refs[0][...]
        out_refs[0][...] = acc.astype(out_dtype)

    return _matmul(a, b, mode=mode, tm=tm, tn=tn, tk=tk, outs=[_plain_out(M, N, tm, tn, out_dtype)],
                   epilogue=epilogue, extras=extras, name=name)[0]


def _row_tile(T, want):
    t = min(T, want)
    while T % t:
        t //= 2
    return t


def _col_tile(N, want):
    if N <= want:
        return N
    best = None
    for c in range(LANES, want + 1, LANES):
        if N % c == 0:
            best = c
    return best if best is not None else N


def _accumulate(ref, first, val):
    @pl.when(first)
    def _():
        ref[...] = val

    @pl.when(jnp.logical_not(first))
    def _():
        ref[...] += val


def _ln_fwd(z, g, b):
    mu = jnp.mean(z, axis=-1, keepdims=True)
    zc = z - mu
    var = jnp.mean(zc * zc, axis=-1, keepdims=True)
    r = lax.rsqrt(var + LN_EPS)
    xh = zc * r
    return xh * g + b, xh, r


def _ln_bwd(dy, xh, r, g):
    dxh = dy * g
    m1 = jnp.mean(dxh, axis=-1, keepdims=True)
    m2 = jnp.mean(dxh * xh, axis=-1, keepdims=True)
    return r * (dxh - m1 - xh * m2)


def _sigmoid(x):
    return 1.0 / (1.0 + jnp.exp(-x))


def _shift_down(x, s, row):
    if s == 0:
        return x
    return jnp.where(row >= s, pltpu.roll(x, s, 0), 0.0)


def _shift_up(x, s, row):
    if s == 0:
        return x
    n = x.shape[0]
    return jnp.where(row < n - s, pltpu.roll(x, n - s, 0), 0.0)


def _bucket_tables():
    exact = REL_BUCKETS // 2
    qi = np.arange(ATTN_BLOCK)[:, None]
    kj = np.arange(2 * ATTN_BLOCK)[None, :]
    steps = qi + ATTN_BLOCK - kj
    buckets, masks = [], []
    for window, dilation in DILATED_CONFIGS:
        max_steps = window // dilation
        band = (steps >= 0) & (steps <= max_steps)
        dist = np.maximum(steps, 0) * dilation
        d_f = np.maximum(dist, 1).astype(np.float32)
        large = exact + (np.log(d_f / np.float32(exact)) / np.float32(math.log(REL_MAX_DIST / exact))
                         * np.float32(REL_BUCKETS - exact)).astype(np.int32)
        large = np.minimum(large, REL_BUCKETS - 1)
        bucket = np.where(dist < exact, dist, large).astype(np.int32)
        buckets.append(bucket.reshape(1, -1))
        masks.append(np.where(band, 0.0, NEG_INF).astype(np.float32).reshape(1, -1))
    return np.stack(buckets), np.stack(masks)


def _split_hi_lo(x):
    hi = x.astype(BF16)
    lo = (x - hi.astype(F32)).astype(BF16)
    return hi, lo


def _bias_build(rel_table_t, bucket, mask):
    H = rel_table_t.shape[0]
    n = bucket.shape[-1]

    def body(t_ref, bkt_ref, mask_ref, o_ref):
        onehot = (lax.broadcasted_iota(jnp.int32, (REL_BUCKETS, n), 0) == bkt_ref[0]).astype(BF16)
        t = t_ref[...]
        t1 = t.astype(BF16)
        r1 = t - t1.astype(F32)
        t2 = r1.astype(BF16)
        t3 = (r1 - t2.astype(F32)).astype(BF16)
        acc = jnp.dot(t1, onehot, preferred_element_type=F32)
        acc = acc + jnp.dot(t2, onehot, preferred_element_type=F32)
        acc = acc + jnp.dot(t3, onehot, preferred_element_type=F32)
        o_ref[0] = acc + mask_ref[0]

    return pl.pallas_call(
        body, grid=(3,),
        in_specs=[pl.BlockSpec((H, REL_BUCKETS), lambda b: (0, 0)),
                  pl.BlockSpec((1, 1, n), lambda b: (b, 0, 0)),
                  pl.BlockSpec((1, 1, n), lambda b: (b, 0, 0))],
        out_specs=pl.BlockSpec((1, H, n), lambda b: (b, 0, 0)),
        out_shape=jax.ShapeDtypeStruct((3, H, n), F32),
        compiler_params=_params(1), name="bias_build",
    )(rel_table_t, bucket, mask)


def _rel_grad(dbias, bucket):
    H = dbias.shape[1]
    n = bucket.shape[-1]
    dims = (((1,), (1,)), ((), ()))

    def body(d_ref, bkt_ref, o_ref):
        b = pl.program_id(0)
        onehot = (lax.broadcasted_iota(jnp.int32, (REL_BUCKETS, n), 0) == bkt_ref[0]).astype(BF16)
        d = d_ref[0]
        d1 = d.astype(BF16)
        r1 = d - d1.astype(F32)
        d2 = r1.astype(BF16)
        d3 = (r1 - d2.astype(F32)).astype(BF16)
        acc = lax.dot_general(d1, onehot, dims, preferred_element_type=F32)
        acc = acc + lax.dot_general(d2, onehot, dims, preferred_element_type=F32)
        acc = acc + lax.dot_general(d3, onehot, dims, preferred_element_type=F32)
        _accumulate(o_ref, b == 0, acc)

    return pl.pallas_call(
        body, grid=(3,),
        in_specs=[pl.BlockSpec((1, H, n), lambda b: (b, 0, 0)),
                  pl.BlockSpec((1, 1, n), lambda b: (b, 0, 0))],
        out_specs=pl.BlockSpec((H, REL_BUCKETS), lambda b: (0, 0)),
        out_shape=jax.ShapeDtypeStruct((H, REL_BUCKETS), F32),
        compiler_params=_params(1), name="rel_grad",
    )(dbias, bucket)


def _attn_specs(B, S, AW, d):
    L = S // d
    HP = AW // LANES
    W3 = 3 * HP
    q_spec = pl.BlockSpec((1, L, LANES), lambda h, b, r: (b, 0, r * W3 + h))
    k_spec = pl.BlockSpec((1, L, LANES), lambda h, b, r: (b, 0, r * W3 + HP + h))
    v_spec = pl.BlockSpec((1, L, LANES), lambda h, b, r: (b, 0, r * W3 + 2 * HP + h))
    o_spec = pl.BlockSpec((1, L, LANES), lambda h, b, r: (b, 0, r * HP + h))
    bias_spec = pl.BlockSpec((2, ATTN_BLOCK, 2 * ATTN_BLOCK), lambda h, b, r: (h, 0, 0))
    return L, HP, q_spec, k_spec, v_spec, o_spec, bias_spec


def _attn_fwd(qkv, bias, B, S, AW, d, name):
    L, HP, q_spec, k_spec, v_spec, o_spec, bias_spec = _attn_specs(B, S, AW, d)
    nb = L // ATTN_BLOCK
    nt = (((1,), (1,)), ((), ()))

    def body(q_ref, k_ref, v_ref, b_ref, o_ref, lse_ref):
        head0 = lax.broadcasted_iota(jnp.int32, (1, LANES), 1) < HEAD_DIM

        def block(n, first):
            qs = pl.multiple_of(n * ATTN_BLOCK, ATTN_BLOCK)
            q = q_ref[0, pl.ds(qs, ATTN_BLOCK), :]
            if first:
                kk = k_ref[0, pl.ds(0, ATTN_BLOCK), :]
                vv = v_ref[0, pl.ds(0, ATTN_BLOCK), :]
            else:
                ks = pl.multiple_of(n * ATTN_BLOCK - ATTN_BLOCK, ATTN_BLOCK)
                kk = k_ref[0, pl.ds(ks, 2 * ATTN_BLOCK), :]
                vv = v_ref[0, pl.ds(ks, 2 * ATTN_BLOCK), :]
            outs, lses = [], []
            for e in range(2):
                msk = head0 if e == 0 else jnp.logical_not(head0)
                qe = jnp.where(msk, q, jnp.zeros_like(q))
                s = lax.dot_general(qe, kk, nt, preferred_element_type=F32) * QK_SCALE
                s = s + (b_ref[e, :, ATTN_BLOCK:] if first else b_ref[e])
                m = jnp.max(s, axis=-1, keepdims=True)
                p = jnp.exp(s - m)
                l = jnp.sum(p, axis=-1, keepdims=True)
                o = jnp.dot(p.astype(BF16), vv, preferred_element_type=F32)
                outs.append(o / l)
                lses.append(jnp.broadcast_to(m + jnp.log(l), (ATTN_BLOCK, LANES)))
            o_ref[0, pl.ds(qs, ATTN_BLOCK), :] = jnp.where(head0, outs[0], outs[1])
            lse_ref[0, pl.ds(qs, ATTN_BLOCK), :] = jnp.where(head0, lses[0], lses[1])

        block(0, True)
        if nb > 1:
            def loop(n, c):
                block(n, False)
                return c
            lax.fori_loop(1, nb, loop, 0)

    qv = qkv.reshape(B, L, d * 3 * AW)
    o, lse = pl.pallas_call(
        body, grid=(HP, B, d), in_specs=[q_spec, k_spec, v_spec, bias_spec],
        out_specs=[o_spec, o_spec],
        out_shape=[jax.ShapeDtypeStruct((B, L, d * AW), F32)] * 2,
        compiler_params=_params(3), name=name,
    )(qv, qv, qv, bias)
    return o.reshape(B * S, AW), lse.reshape(B * S, AW)


def _attn_bwd(qkv, do, lse, dd, bias, B, S, AW, d, name):
    L, HP, q_spec, k_spec, v_spec, o_spec, bias_spec = _attn_specs(B, S, AW, d)
    nb = L // ATTN_BLOCK
    nt = (((1,), (1,)), ((), ()))
    tn = (((0,), (0,)), ((), ()))

    def body(q_ref, k_ref, v_ref, do_ref, lse_ref, dd_ref, b_ref, dq_ref, dk_ref, dv_ref, db_ref):
        head0 = lax.broadcasted_iota(jnp.int32, (1, LANES), 1) < HEAD_DIM
        first_step = jnp.logical_and(pl.program_id(1) == 0, pl.program_id(2) == 0)

        @pl.when(first_step)
        def _():
            db_ref[...] = jnp.zeros_like(db_ref)

        dk_ref[...] = jnp.zeros_like(dk_ref)
        dv_ref[...] = jnp.zeros_like(dv_ref)

        def block(n, first):
            qs = pl.multiple_of(n * ATTN_BLOCK, ATTN_BLOCK)
            nkeys = ATTN_BLOCK if first else 2 * ATTN_BLOCK
            ks = 0 if first else pl.multiple_of(n * ATTN_BLOCK - ATTN_BLOCK, ATTN_BLOCK)
            q = q_ref[0, pl.ds(qs, ATTN_BLOCK), :]
            kk = k_ref[0, pl.ds(ks, nkeys), :]
            vv = v_ref[0, pl.ds(ks, nkeys), :]
            dout = do_ref[0, pl.ds(qs, ATTN_BLOCK), :]
            lse_b = lse_ref[0, pl.ds(qs, ATTN_BLOCK), :]
            dd_b = dd_ref[0, pl.ds(qs, ATTN_BLOCK), :]
            dq = jnp.zeros((ATTN_BLOCK, LANES), F32)
            dkk = jnp.zeros((nkeys, LANES), F32)
            dvv = jnp.zeros((nkeys, LANES), F32)
            for e in range(2):
                msk = head0 if e == 0 else jnp.logical_not(head0)
                c0 = e * HEAD_DIM
                qe = jnp.where(msk, q, jnp.zeros_like(q))
                doe = jnp.where(msk, dout, jnp.zeros_like(dout))
                kke = jnp.where(msk, kk, jnp.zeros_like(kk))
                s = lax.dot_general(qe, kk, nt, preferred_element_type=F32) * QK_SCALE
                s = s + (b_ref[e, :, ATTN_BLOCK:] if first else b_ref[e])
                p = jnp.exp(s - lse_b[:, c0:c0 + 1])
                dp = lax.dot_general(doe, vv, nt, preferred_element_type=F32)
                ds = p * (dp - dd_b[:, c0:c0 + 1])
                if first:
                    db_ref[e, :, ATTN_BLOCK:] += ds
                else:
                    db_ref[e] += ds
                dsb = (ds * QK_SCALE).astype(BF16)
                dq = dq + jnp.dot(dsb, kke, preferred_element_type=F32)
                dkk = dkk + lax.dot_general(dsb, qe, tn, preferred_element_type=F32)
                dvv = dvv + lax.dot_general(p.astype(BF16), doe, tn, preferred_element_type=F32)
            dq_ref[0, pl.ds(qs, ATTN_BLOCK), :] = dq
            dk_ref[0, pl.ds(ks, nkeys), :] += dkk
            dv_ref[0, pl.ds(ks, nkeys), :] += dvv

        block(0, True)
        if nb > 1:
            def loop(n, c):
                block(n, False)
                return c
            lax.fori_loop(1, nb, loop, 0)

    H = AW // HEAD_DIM
    qv = qkv.reshape(B, L, d * 3 * AW)
    view = lambda t: t.reshape(B, L, d * AW)
    dq, dk, dv, db = pl.pallas_call(
        body, grid=(HP, B, d),
        in_specs=[q_spec, k_spec, v_spec, o_spec, o_spec, o_spec, bias_spec],
        out_specs=[o_spec, o_spec, o_spec, bias_spec],
        out_shape=[jax.ShapeDtypeStruct((B, L, d * AW), F32)] * 3
        + [jax.ShapeDtypeStruct((H, ATTN_BLOCK, 2 * ATTN_BLOCK), F32)],
        compiler_params=_params(3), name=name,
    )(qv, qv, qv, view(do), view(lse), view(dd), bias)
    flat = lambda t: t.reshape(B * S, AW)
    return flat(dq), flat(dk), flat(dv), db


def _attn_combine(ons, lses, gain, tm):
    T, AW = ons[0].shape

    def body(o1, o2, o3, l1, l2, l3, g_ref, attn_ref, lse_ref, mix_ref, r_ref):
        la, lb, lc = l1[...], l2[...], l3[...]
        m = jnp.maximum(jnp.maximum(la, lb), lc)
        ea, eb, ec = jnp.exp(la - m), jnp.exp(lb - m), jnp.exp(lc - m)
        den = ea + eb + ec
        attn = (ea * o1[...] + eb * o2[...] + ec * o3[...]) / den
        attn_ref[...] = attn
        lse_ref[...] = m + jnp.log(den)
        r = lax.rsqrt(jnp.mean(attn * attn, axis=-1, keepdims=True) + LN_EPS)
        mix_ref[...] = (attn * r * g_ref[...]).astype(BF16)
        r_ref[...] = jnp.broadcast_to(r, (tm, LANES))

    row = pl.BlockSpec((tm, AW), lambda i: (i, 0))
    return pl.pallas_call(
        body, grid=(T // tm,),
        in_specs=[row] * 6 + [pl.BlockSpec((1, AW), lambda i: (0, 0))],
        out_specs=[row, row, row, pl.BlockSpec((tm, LANES), lambda i: (i, 0))],
        out_shape=[jax.ShapeDtypeStruct((T, AW), F32), jax.ShapeDtypeStruct((T, AW), F32),
                   jax.ShapeDtypeStruct((T, AW), BF16), jax.ShapeDtypeStruct((T, LANES), F32)],
        compiler_params=_params(1), name="attn_combine",
    )(*ons, *lses, gain)


def _to_sub(src_ref, stage_ref, dsts, S):
    stage_ref[...] = src_ref[0].astype(F32)
    for (_, d), dst in zip(DILATED_CONFIGS[1:], dsts):
        L = S // d
        for r in range(d):
            dst[r * L:(r + 1) * L, :] = stage_ref[pl.ds(r, L, stride=d), :].astype(dst.dtype)


def _branch_blocks(S, d, block):
    nb = S // d // ATTN_BLOCK
    inner_unroll = 3 if (nb - 1) % 3 == 0 else 1

    def per_residue(r, c):
        block(r * nb, True)
        if nb > 1:
            def inner(n, c2):
                block(r * nb + n, False)
                return c2
            lax.fori_loop(1, nb, inner, 0, unroll=inner_unroll)
        return c

    lax.fori_loop(0, d, per_residue, 0, unroll=4 if nb == 1 else 1)


def _attention_fwd(qkv, bias_all, B, S, AW):
    HP = AW // LANES
    nt = MM_DIMS["nt"]

    def body(q_ref, k_ref, v_ref, b_ref, o_ref, lse_ref, stage, q4, q16, k4, k16, v4, v16, o1, l1, o4, l4, o16, l16):
        head0 = lax.broadcasted_iota(jnp.int32, (1, LANES), 1) < HEAD_DIM
        _to_sub(q_ref, stage, (q4, q16), S)
        _to_sub(k_ref, stage, (k4, k16), S)
        _to_sub(v_ref, stage, (v4, v16), S)
        srcs = ((q_ref.at[0], k_ref.at[0], v_ref.at[0], o1, l1), (q4, k4, v4, o4, l4), (q16, k16, v16, o16, l16))
        for bi, (_, d) in enumerate(DILATED_CONFIGS):
            qs_ref, ks_ref, vs_ref, od_ref, ld_ref = srcs[bi]

            def block(g, first, bi=bi, qs_ref=qs_ref, ks_ref=ks_ref, vs_ref=vs_ref, od_ref=od_ref, ld_ref=ld_ref):
                qs = pl.multiple_of(g * ATTN_BLOCK, ATTN_BLOCK)
                nkeys = ATTN_BLOCK if first else 2 * ATTN_BLOCK
                ks = qs if first else pl.multiple_of(qs - ATTN_BLOCK, ATTN_BLOCK)
                q = qs_ref[pl.ds(qs, ATTN_BLOCK), :]
                kk = ks_ref[pl.ds(ks, nkeys), :]
                vv = vs_ref[pl.ds(ks, nkeys), :]
                outs, lses = [], []
                for e in range(2):
                    msk = head0 if e == 0 else jnp.logical_not(head0)
                    qe = jnp.where(msk, q * QK_SCALE, jnp.zeros_like(q))
                    s = lax.dot_general(qe, kk, nt, preferred_element_type=F32)
                    s = s + (b_ref[bi, e, :, ATTN_BLOCK:] if first else b_ref[bi, e])
                    m = jnp.max(s, axis=-1, keepdims=True)
                    p = jnp.exp(s - m)
                    l = jnp.sum(p, axis=-1, keepdims=True)
                    o = jnp.dot(p.astype(BF16), vv, preferred_element_type=F32)
                    outs.append(o / l)
                    lses.append(jnp.broadcast_to(m + jnp.log(l), (ATTN_BLOCK, LANES)))
                od_ref[pl.ds(qs, ATTN_BLOCK), :] = jnp.where(head0, outs[0], outs[1])
                ld_ref[pl.ds(qs, ATTN_BLOCK), :] = jnp.where(head0, lses[0], lses[1])

            _branch_blocks(S, d, block)

        def natural(sub_ref, d):
            L = S // d
            for r in range(d):
                stage[pl.ds(r, L, stride=d), :] = sub_ref[r * L:(r + 1) * L, :]
            return stage[...]

        la = l1[...]
        lb = natural(l4, 4)
        lc = natural(l16, 16)
        m = jnp.maximum(jnp.maximum(la, lb), lc)
        ea, eb, ec = jnp.exp(la - m), jnp.exp(lb - m), jnp.exp(lc - m)
        den = ea + eb + ec
        lse_ref[0] = m + jnp.log(den)
        acc = ea * o1[...]
        acc = acc + eb * natural(o4, 4)
        acc = acc + ec * natural(o16, 16)
        o_ref[0] = acc / den

    blk = lambda off: pl.BlockSpec((1, S, LANES), lambda b, h: (b, 0, off + h))
    qv = qkv.reshape(B, S, 3 * AW)
    sub_b = pltpu.VMEM((S, LANES), BF16)
    sub_f = pltpu.VMEM((S, LANES), F32)
    o, lse = pl.pallas_call(
        body, grid=(B, HP),
        in_specs=[blk(0), blk(HP), blk(2 * HP),
                  pl.BlockSpec((3, 2, ATTN_BLOCK, 2 * ATTN_BLOCK), lambda b, h: (0, h, 0, 0))],
        out_specs=[blk(0), blk(0)],
        out_shape=[jax.ShapeDtypeStruct((B, S, AW), F32)] * 2,
        scratch_shapes=[sub_f] + [sub_b] * 6 + [sub_f] * 6,
        compiler_params=_params(2), name="attention_fwd",
    )(qv, qv, qv, bias_all)
    return o.reshape(B * S, AW), lse.reshape(B * S, AW)


def _attention_bwd(qkv, do, lse, dd, bias_all, B, S, AW):
    HP = AW // LANES
    H = AW // HEAD_DIM
    nt, tn = MM_DIMS["nt"], MM_DIMS["tn"]

    def body(q_ref, k_ref, v_ref, do_ref, lse_ref, dd_ref, b_ref,
             dq_ref, dk_ref, dv_ref, csq_ref, csk_ref, csv_ref, db_ref,
             stage, q4, q16, k4, k16, v4, v16, g4, g16, l4, l16, d4, d16,
             aq1, ak1, av1, aq4, ak4, av4, aq16, ak16, av16):
        head0 = lax.broadcasted_iota(jnp.int32, (1, LANES), 1) < HEAD_DIM
        first_b = pl.program_id(1) == 0

        @pl.when(first_b)
        def _():
            db_ref[...] = jnp.zeros_like(db_ref)

        _to_sub(q_ref, stage, (q4, q16), S)
        _to_sub(k_ref, stage, (k4, k16), S)
        _to_sub(v_ref, stage, (v4, v16), S)
        _to_sub(do_ref, stage, (g4, g16), S)
        _to_sub(lse_ref, stage, (l4, l16), S)
        _to_sub(dd_ref, stage, (d4, d16), S)
        for acc in (ak1, av1, ak4, av4, ak16, av16):
            acc[...] = jnp.zeros_like(acc)
        srcs = ((q_ref.at[0], k_ref.at[0], v_ref.at[0], do_ref.at[0], lse_ref.at[0], dd_ref.at[0], aq1, ak1, av1),
                (q4, k4, v4, g4, l4, d4, aq4, ak4, av4), (q16, k16, v16, g16, l16, d16, aq16, ak16, av16))
        for bi, (_, d) in enumerate(DILATED_CONFIGS):
            def block(g, first, bi=bi, refs=srcs[bi]):
                qs_ref, ks_ref, vs_ref, gs_ref, ls_ref, ds_ref, aq, ak, av = refs
                qs = pl.multiple_of(g * ATTN_BLOCK, ATTN_BLOCK)
                nkeys = ATTN_BLOCK if first else 2 * ATTN_BLOCK
                ks = qs if first else pl.multiple_of(qs - ATTN_BLOCK, ATTN_BLOCK)
                q = qs_ref[pl.ds(qs, ATTN_BLOCK), :]
                kk = ks_ref[pl.ds(ks, nkeys), :]
                vv = vs_ref[pl.ds(ks, nkeys), :]
                dout = gs_ref[pl.ds(qs, ATTN_BLOCK), :]
                lse_b = ls_ref[pl.ds(qs, ATTN_BLOCK), :]
                dd_b = ds_ref[pl.ds(qs, ATTN_BLOCK), :]
                dq = jnp.zeros((ATTN_BLOCK, LANES), F32)
                dkk = jnp.zeros((nkeys, LANES), F32)
                dvv = jnp.zeros((nkeys, LANES), F32)
                for e in range(2):
                    msk = head0 if e == 0 else jnp.logical_not(head0)
                    c0 = e * HEAD_DIM
                    qe = jnp.where(msk, q * QK_SCALE, jnp.zeros_like(q))
                    doe = jnp.where(msk, dout, jnp.zeros_like(dout))
                    kke = jnp.where(msk, kk * QK_SCALE, jnp.zeros_like(kk))
                    s = lax.dot_general(qe, kk, nt, preferred_element_type=F32)
                    s = s + (b_ref[bi, e, :, ATTN_BLOCK:] if first else b_ref[bi, e])
                    p = jnp.exp(s - lse_b[:, c0:c0 + 1])
                    dp = lax.dot_general(doe, vv, nt, preferred_element_type=F32)
                    ds = p * (dp - dd_b[:, c0:c0 + 1])
                    if first:
                        db_ref[bi, e, :, ATTN_BLOCK:] += ds
                    else:
                        db_ref[bi, e] += ds
                    dsb = ds.astype(BF16)
                    dq = dq + jnp.dot(dsb, kke, preferred_element_type=F32)
                    dkk = dkk + lax.dot_general(dsb, qe, tn, preferred_element_type=F32)
                    dvv = dvv + lax.dot_general(p.astype(BF16), doe, tn, preferred_element_type=F32)
                aq[pl.ds(qs, ATTN_BLOCK), :] = dq
                ak[pl.ds(ks, nkeys), :] += dkk
                av[pl.ds(ks, nkeys), :] += dvv

            _branch_blocks(S, d, block)

        for a1, a4, a16, out_ref, cs_ref in ((aq1, aq4, aq16, dq_ref, csq_ref), (ak1, ak4, ak16, dk_ref, csk_ref),
                                             (av1, av4, av16, dv_ref, csv_ref)):
            stage[...] = a1[...]
            for d, sub in ((4, a4), (16, a16)):
                L = S // d
                for r in range(d):
                    stage[pl.ds(r, L, stride=d), :] += sub[r * L:(r + 1) * L, :]
            tot = stage[...]
            out_ref[0] = tot.astype(out_ref.dtype)
            _accumulate(cs_ref, first_b, jnp.sum(tot, axis=0, keepdims=True))

    blk = lambda off: pl.BlockSpec((1, S, LANES), lambda h, b: (b, 0, off + h))
    cs_spec = pl.BlockSpec((1, LANES), lambda h, b: (0, h))
    bias_spec = pl.BlockSpec((3, 2, ATTN_BLOCK, 2 * ATTN_BLOCK), lambda h, b: (0, h, 0, 0))
    qv = qkv.reshape(B, S, 3 * AW)
    view = lambda t: t.reshape(B, S, AW)
    sub_b = pltpu.VMEM((S, LANES), BF16)
    sub_f = pltpu.VMEM((S, LANES), F32)
    res = pl.pallas_call(
        body, grid=(HP, B),
        in_specs=[blk(0), blk(HP), blk(2 * HP), blk(0), blk(0), blk(0), bias_spec],
        out_specs=[blk(0), blk(0), blk(0), cs_spec, cs_spec, cs_spec, bias_spec],
        out_shape=[jax.ShapeDtypeStruct((B, S, AW), BF16)] * 3 + [jax.ShapeDtypeStruct((1, AW), F32)] * 3
        + [jax.ShapeDtypeStruct((3, H, ATTN_BLOCK, 2 * ATTN_BLOCK), F32)],
        scratch_shapes=[sub_f] + [sub_b] * 8 + [sub_f] * 4 + [sub_f] * 9,
        compiler_params=_params(2), name="attention_bwd",
    )(qv, qv, qv, view(do), view(lse), view(dd), bias_all)
    flat = lambda t: t.reshape(B * S, AW)
    return flat(res[0]), flat(res[1]), flat(res[2]), res[3], res[4], res[5], res[6]


def _regroup(src, stage, dst, d, S, off=0):
    if d == 1:
        dst[off:off + S, :] = src.astype(dst.dtype)
        return
    stage[...] = src.astype(F32)
    L = S // d
    for r in range(d):
        dst[off + r * L:off + (r + 1) * L, :] = stage[pl.ds(r, L, stride=d), :].astype(dst.dtype)


def _ungroup(sub_ref, off, nat_ref, d, S, add):
    L = S // d
    for r in range(d):
        rows = pl.ds(0, S) if d == 1 else pl.ds(r, L, stride=d)
        val = sub_ref[off + r * L:off + (r + 1) * L, :]
        if add:
            nat_ref[rows, :] += val
        else:
            nat_ref[rows, :] = val


def _branch_scores(qe, kc3, kp3, b_ref, bi, e, first3):
    s_cur = jnp.einsum("gqe,gke->gqk", qe, kc3, preferred_element_type=F32) + b_ref[bi, e, :, ATTN_BLOCK:]
    if kp3 is None:
        return s_cur, None
    s_prev = jnp.einsum("gqe,gke->gqk", qe, kp3, preferred_element_type=F32) + b_ref[bi, e, :, :ATTN_BLOCK]
    return s_cur, jnp.where(first3, NEG_INF, s_prev)


def _attention_fwd(qkv, bias_all, B, S, AW, bg=None):
    HP = AW // LANES
    G = S // ATTN_BLOCK
    blk3 = (G, ATTN_BLOCK, LANES)

    def body(refs, bg_hook):
        q_ref, k_ref, v_ref, b_ref, o_ref, lse_ref, stage, qs, ks, vs, ot, lt, on0, on1, on2, ln0, ln1, ln2 = refs
        bg_hook(False)
        head0 = lax.broadcasted_iota(jnp.int32, (1, 1, LANES), 2) < HEAD_DIM
        g_idx = lax.broadcasted_iota(jnp.int32, (G, 1, 1), 0)
        ks[0:ATTN_BLOCK, :] = jnp.zeros((ATTN_BLOCK, LANES), BF16)
        vs[0:ATTN_BLOCK, :] = jnp.zeros((ATTN_BLOCK, LANES), BF16)
        nat_o, nat_l = (on0, on1, on2), (ln0, ln1, ln2)
        for bi, (_, d) in enumerate(DILATED_CONFIGS):
            nb = S // d // ATTN_BLOCK
            _regroup(q_ref[0], stage, qs, d, S)
            _regroup(k_ref[0], stage, ks, d, S, ATTN_BLOCK)
            _regroup(v_ref[0], stage, vs, d, S, ATTN_BLOCK)
            q3 = qs[...].reshape(blk3) * QK_SCALE
            kc3 = ks[ATTN_BLOCK:ATTN_BLOCK + S, :].reshape(blk3)
            vc3 = vs[ATTN_BLOCK:ATTN_BLOCK + S, :].reshape(blk3)
            kp3 = vp3 = first3 = None
            if nb > 1:
                kp3 = ks[0:S, :].reshape(blk3)
                vp3 = vs[0:S, :].reshape(blk3)
                first3 = (g_idx & (nb - 1)) == 0
            outs, lses = [], []
            for e in range(2):
                msk = head0 if e == 0 else jnp.logical_not(head0)
                qe = jnp.where(msk, q3, jnp.zeros_like(q3))
                s_cur, s_prev = _branch_scores(qe, kc3, kp3, b_ref, bi, e, first3)
                m = jnp.max(s_cur, axis=-1, keepdims=True)
                if s_prev is not None:
                    m = jnp.maximum(m, jnp.max(s_prev, axis=-1, keepdims=True))
                p = jnp.exp(s_cur - m)
                l = jnp.sum(p, axis=-1, keepdims=True)
                o = jnp.einsum("gqk,gke->gqe", p.astype(BF16), vc3, preferred_element_type=F32)
                if s_prev is not None:
                    p = jnp.exp(s_prev - m)
                    l = l + jnp.sum(p, axis=-1, keepdims=True)
                    o = o + jnp.einsum("gqk,gke->gqe", p.astype(BF16), vp3, preferred_element_type=F32)
                outs.append(o / l)
                lses.append(jnp.broadcast_to(m + jnp.log(l), blk3))
            ot[...] = jnp.where(head0, outs[0], outs[1]).reshape(S, LANES)
            lt[...] = jnp.where(head0, lses[0], lses[1]).reshape(S, LANES)
            _ungroup(ot, 0, nat_o[bi], d, S, add=False)
            _ungroup(lt, 0, nat_l[bi], d, S, add=False)

        la, lb, lc = ln0[...], ln1[...], ln2[...]
        m = jnp.maximum(jnp.maximum(la, lb), lc)
        ea, eb, ec = jnp.exp(la - m), jnp.exp(lb - m), jnp.exp(lc - m)
        den = ea + eb + ec
        lse_ref[0] = m + jnp.log(den)
        o_ref[0] = (ea * on0[...] + eb * on1[...] + ec * on2[...]) / den
        bg_hook(True)

    blk = lambda off: pl.BlockSpec((1, S, LANES), lambda b, h: (b, 0, off + h))
    qv = qkv.reshape(B, S, 3 * AW)
    sub_f = pltpu.VMEM((S, LANES), F32)
    pad_b = pltpu.VMEM((S + ATTN_BLOCK, LANES), BF16)
    res = _hosted_call(
        body, bg, grid=(B, HP),
        in_specs=[blk(0), blk(HP), blk(2 * HP),
                  pl.BlockSpec((3, 2, ATTN_BLOCK, 2 * ATTN_BLOCK), lambda b, h: (0, h, 0, 0))],
        out_specs=[blk(0), blk(0)],
        out_shape=[jax.ShapeDtypeStruct((B, S, AW), F32)] * 2,
        scratch_shapes=[sub_f, pltpu.VMEM((S, LANES), BF16), pad_b, pad_b] + [sub_f] * 8,
        operands=[qv, qv, qv, bias_all], name="attention_fwd")
    return (res[0].reshape(B * S, AW), res[1].reshape(B * S, AW)) + tuple(res[2:])


def _attention_bwd(qkv, do, lse, dd, bias_all, B, S, AW, bg=None):
    HP = AW // LANES
    H = AW // HEAD_DIM
    G = S // ATTN_BLOCK
    blk3 = (G, ATTN_BLOCK, LANES)
    PAD = ATTN_BLOCK

    def body(refs, bg_hook):
        (q_ref, k_ref, v_ref, do_ref, lse_ref, dd_ref, b_ref,
         dq_ref, dk_ref, dv_ref, csq_ref, csk_ref, csv_ref, db_ref,
         stage, qs, ks, vs, gs, ls, ds_, tq, tk, tv, accq, acck, accv) = refs
        bg_hook(False)
        head0 = lax.broadcasted_iota(jnp.int32, (1, 1, LANES), 2) < HEAD_DIM
        g_idx = lax.broadcasted_iota(jnp.int32, (G, 1, 1), 0)
        first_b = pl.program_id(1) == 0

        @pl.when(first_b)
        def _():
            db_ref[...] = jnp.zeros_like(db_ref)

        ks[0:PAD, :] = jnp.zeros((PAD, LANES), BF16)
        vs[0:PAD, :] = jnp.zeros((PAD, LANES), BF16)
        tk[0:PAD, :] = jnp.zeros((PAD, LANES), F32)
        tv[0:PAD, :] = jnp.zeros((PAD, LANES), F32)
        for bi, (_, d) in enumerate(DILATED_CONFIGS):
            nb = S // d // ATTN_BLOCK
            _regroup(q_ref[0], stage, qs, d, S)
            _regroup(k_ref[0], stage, ks, d, S, PAD)
            _regroup(v_ref[0], stage, vs, d, S, PAD)
            _regroup(do_ref[0], stage, gs, d, S)
            _regroup(lse_ref[0], stage, ls, d, S)
            _regroup(dd_ref[0], stage, ds_, d, S)
            q3 = qs[...].reshape(blk3) * QK_SCALE
            do3 = gs[...].reshape(blk3)
            lse3 = ls[...].reshape(blk3)
            dd3 = ds_[...].reshape(blk3)
            kc3 = ks[PAD:PAD + S, :].reshape(blk3)
            vc3 = vs[PAD:PAD + S, :].reshape(blk3)
            kp3 = vp3 = first3 = None
            if nb > 1:
                kp3 = ks[0:S, :].reshape(blk3)
                vp3 = vs[0:S, :].reshape(blk3)
                first3 = (g_idx & (nb - 1)) == 0
            dq = jnp.zeros(blk3, F32)
            dkc = jnp.zeros(blk3, F32)
            dvc = jnp.zeros(blk3, F32)
            dkp = jnp.zeros(blk3, F32)
            dvp = jnp.zeros(blk3, F32)
            for e in range(2):
                msk = head0 if e == 0 else jnp.logical_not(head0)
                c0 = e * HEAD_DIM
                qe = jnp.where(msk, q3, jnp.zeros_like(q3))
                doe = jnp.where(msk, do3, jnp.zeros_like(do3))
                lse_e = lse3[:, :, c0:c0 + 1]
                dd_e = dd3[:, :, c0:c0 + 1]
                s_cur, s_prev = _branch_scores(qe, kc3, kp3, b_ref, bi, e, first3)
                for s, k3, v3, cur in ((s_cur, kc3, vc3, True), (s_prev, kp3, vp3, False)):
                    if s is None:
                        continue
                    p = jnp.exp(s - lse_e)
                    dp = jnp.einsum("gqe,gke->gqk", doe, v3, preferred_element_type=F32)
                    dsc = p * (dp - dd_e)
                    if cur:
                        db_ref[bi, e, :, ATTN_BLOCK:] += jnp.sum(dsc, axis=0)
                    else:
                        db_ref[bi, e, :, :ATTN_BLOCK] += jnp.sum(dsc, axis=0)
                    dsb = dsc.astype(BF16)
                    ke = jnp.where(msk, k3 * QK_SCALE, jnp.zeros_like(k3))
                    dq = dq + jnp.einsum("gqk,gke->gqe", dsb, ke, preferred_element_type=F32)
                    dk_e = jnp.einsum("gqk,gqe->gke", dsb, qe, preferred_element_type=F32)
                    dv_e = jnp.einsum("gqk,gqe->gke", p.astype(BF16), doe, preferred_element_type=F32)
                    if cur:
                        dkc, dvc = dkc + dk_e, dvc + dv_e
                    else:
                        dkp, dvp = dkp + dk_e, dvp + dv_e
            tq[...] = dq.reshape(S, LANES)
            tk[PAD:PAD + S, :] = dkc.reshape(S, LANES)
            tv[PAD:PAD + S, :] = dvc.reshape(S, LANES)
            if nb > 1:
                tk[0:S, :] += dkp.reshape(S, LANES)
                tv[0:S, :] += dvp.reshape(S, LANES)
            _ungroup(tq, 0, accq, d, S, add=bi > 0)
            _ungroup(tk, PAD, acck, d, S, add=bi > 0)
            _ungroup(tv, PAD, accv, d, S, add=bi > 0)

        for acc, out_ref, cs_ref in ((accq, dq_ref, csq_ref), (acck, dk_ref, csk_ref), (accv, dv_ref, csv_ref)):
            tot = acc[...]
            out_ref[0] = tot.astype(out_ref.dtype)
            _accumulate(cs_ref, first_b, jnp.sum(tot, axis=0, keepdims=True))
        bg_hook(True)

    blk = lambda off: pl.BlockSpec((1, S, LANES), lambda h, b: (b, 0, off + h))
    cs_spec = pl.BlockSpec((1, LANES), lambda h, b: (0, h))
    bias_spec = pl.BlockSpec((3, 2, ATTN_BLOCK, 2 * ATTN_BLOCK), lambda h, b: (0, h, 0, 0))
    qv = qkv.reshape(B, S, 3 * AW)
    view = lambda t: t.reshape(B, S, AW)
    sub_b = pltpu.VMEM((S, LANES), BF16)
    sub_f = pltpu.VMEM((S, LANES), F32)
    pad_b = pltpu.VMEM((S + PAD, LANES), BF16)
    pad_f = pltpu.VMEM((S + PAD, LANES), F32)
    res = _hosted_call(
        body, bg, grid=(HP, B),
        in_specs=[blk(0), blk(HP), blk(2 * HP), blk(0), blk(0), blk(0), bias_spec],
        out_specs=[blk(0), blk(0), blk(0), cs_spec, cs_spec, cs_spec, bias_spec],
        out_shape=[jax.ShapeDtypeStruct((B, S, AW), BF16)] * 3 + [jax.ShapeDtypeStruct((1, AW), F32)] * 3
        + [jax.ShapeDtypeStruct((3, H, ATTN_BLOCK, 2 * ATTN_BLOCK), F32)],
        scratch_shapes=[sub_f, sub_b, pad_b, pad_b, sub_b, sub_f, sub_f, sub_f, pad_f, pad_f, sub_f, sub_f, sub_f],
        operands=[qv, qv, qv, view(do), view(lse), view(dd), bias_all], name="attention_bwd")
    flat = lambda t: t.reshape(B * S, AW)
    return (flat(res[0]), flat(res[1]), flat(res[2]), res[3], res[4], res[5], res[6]) + tuple(res[7:])


def _attn_norm(attn, gain, tm):
    T, AW = attn.shape

    def body(a_ref, g_ref, mix_ref, r_ref):
        a = a_ref[...]
        r = lax.rsqrt(jnp.mean(a * a, axis=-1, keepdims=True) + LN_EPS)
        mix_ref[...] = (a * r * g_ref[...]).astype(BF16)
        r_ref[...] = jnp.broadcast_to(r, (tm, LANES))

    row = pl.BlockSpec((tm, AW), lambda i: (i, 0))
    return pl.pallas_call(
        body, grid=(T // tm,), in_specs=[row, pl.BlockSpec((1, AW), lambda i: (0, 0))],
        out_specs=[row, pl.BlockSpec((tm, LANES), lambda i: (i, 0))],
        out_shape=[jax.ShapeDtypeStruct((T, AW), BF16), jax.ShapeDtypeStruct((T, LANES), F32)],
        compiler_params=_params(1), name="attn_norm",
    )(attn, gain)


def _attn_pre_bwd(dmixed, attn, rstd, gain, tm):
    T, AW = attn.shape
    ones_np = np.kron(np.eye(AW // HEAD_DIM, dtype=np.float32), np.ones((HEAD_DIM, HEAD_DIM), np.float32))
    ones_bd = jnp.asarray(ones_np, dtype=BF16)

    def body(dm_ref, a_ref, r_ref, g_ref, ones_ref, do_ref, dd_ref, dg_ref):
        i = pl.program_id(0)
        dm = dm_ref[...]
        a = a_ref[...]
        r = r_ref[:, 0:1]
        dxn = dm * g_ref[...]
        da = r * (dxn - a * (r * r) * jnp.mean(dxn * a, axis=-1, keepdims=True))
        do_ref[...] = da.astype(BF16)
        hi, lo = _split_hi_lo(da * a)
        dd_ref[...] = (jnp.dot(hi, ones_ref[...], preferred_element_type=F32)
                       + jnp.dot(lo, ones_ref[...], preferred_element_type=F32))
        _accumulate(dg_ref, i == 0, jnp.sum(dm * a * r, axis=0, keepdims=True))

    row = pl.BlockSpec((tm, AW), lambda i: (i, 0))
    vec = pl.BlockSpec((1, AW), lambda i: (0, 0))
    return pl.pallas_call(
        body, grid=(T // tm,),
        in_specs=[row, row, pl.BlockSpec((tm, LANES), lambda i: (i, 0)), vec,
                  pl.BlockSpec((AW, AW), lambda i: (0, 0))],
        out_specs=[row, row, vec],
        out_shape=[jax.ShapeDtypeStruct((T, AW), BF16), jax.ShapeDtypeStruct((T, AW), F32),
                   jax.ShapeDtypeStruct((1, AW), F32)],
        compiler_params=_params(1), name="attn_pre_bwd",
    )(dmixed, attn, rstd, gain, ones_bd)


def _conv_branch_fwd_math(a, g, w_ref, cb, lg, lb, row):
    sg = _sigmoid(g)
    u0 = a * sg
    uc = jnp.zeros_like(u0) + cb
    for k in range(CONV_KERNEL):
        uc = uc + w_ref[k:k + 1, :] * _shift_down(u0, CONV_KERNEL - 1 - k, row)
    ul, xh, r = _ln_fwd(uc, lg, lb)
    su = _sigmoid(ul)
    u = ul * su
    return sg, u0, ul, xh, r, su, u


def _conv_fwd(ag, conv_w, conv_b, ln_g, ln_b, norm_g, B, S, CW):
    def body(a_ref, g_ref, w_ref, cb_ref, lg_ref, lb_ref, ng_ref, o_ref):
        row = lax.broadcasted_iota(jnp.int32, (S, CW), 0)
        _, _, _, _, _, _, u = _conv_branch_fwd_math(a_ref[0], g_ref[0], w_ref, cb_ref[...], lg_ref[...],
                                                    lb_ref[...], row)
        rr = lax.rsqrt(jnp.mean(u * u, axis=-1, keepdims=True) + LN_EPS)
        o_ref[0] = (u * rr * ng_ref[...]).astype(BF16)

    vec = pl.BlockSpec((1, CW), lambda b: (0, 0))
    out = pl.pallas_call(
        body, grid=(B,),
        in_specs=[pl.BlockSpec((1, S, CW), lambda b: (b, 0, 0)), pl.BlockSpec((1, S, CW), lambda b: (b, 0, 1)),
                  pl.BlockSpec((CONV_KERNEL, CW), lambda b: (0, 0)), vec, vec, vec, vec],
        out_specs=pl.BlockSpec((1, S, CW), lambda b: (b, 0, 0)),
        out_shape=jax.ShapeDtypeStruct((B, S, CW), BF16),
        compiler_params=_params(1), name="conv_fwd",
    )(ag.reshape(B, S, 2 * CW), ag.reshape(B, S, 2 * CW), conv_w, conv_b, ln_g, ln_b, norm_g)
    return out.reshape(B * S, CW)


def _conv_bwd(ag, dmixed, conv_w, conv_b, ln_g, ln_b, norm_g, B, S, CW, D):
    AW = D - CW
    assert AW % CW == 0

    def body(a_ref, g_ref, dm_ref, w_ref, cb_ref, lg_ref, lb_ref, ng_ref,
             dag_ref, dw_ref, dcb_ref, dlg_ref, dlb_ref, dng_ref):
        b = pl.program_id(0)
        row = lax.broadcasted_iota(jnp.int32, (S, CW), 0)
        a, g = a_ref[0], g_ref[0]
        sg, u0, ul, xh, r, su, u = _conv_branch_fwd_math(a, g, w_ref, cb_ref[...], lg_ref[...], lb_ref[...], row)
        rr = lax.rsqrt(jnp.mean(u * u, axis=-1, keepdims=True) + LN_EPS)
        dm = dm_ref[0]
        dxn = dm * ng_ref[...]
        du = rr * (dxn - u * (rr * rr) * jnp.mean(dxn * u, axis=-1, keepdims=True))
        dul = du * su * (1.0 + ul * (1.0 - su))
        duc = _ln_bwd(dul, xh, r, lg_ref[...])
        first = b == 0
        _accumulate(dng_ref, first, jnp.sum(dm * u * rr, axis=0, keepdims=True))
        _accumulate(dlg_ref, first, jnp.sum(dul * xh, axis=0, keepdims=True))
        _accumulate(dlb_ref, first, jnp.sum(dul, axis=0, keepdims=True))
        _accumulate(dcb_ref, first, jnp.sum(duc, axis=0, keepdims=True))

        @pl.when(first)
        def _():
            dw_ref[...] = jnp.zeros_like(dw_ref)

        du0 = jnp.zeros_like(u0)
        for k in range(CONV_KERNEL):
            sh = CONV_KERNEL - 1 - k
            dw_ref[k:k + 1, :] += jnp.sum(duc * _shift_down(u0, sh, row), axis=0, keepdims=True)
            du0 = du0 + w_ref[k:k + 1, :] * _shift_up(duc, sh, row)
        dag_ref[0, :, :CW] = du0 * sg
        dag_ref[0, :, CW:] = du0 * a * sg * (1.0 - sg)

    vec = pl.BlockSpec((1, CW), lambda b: (0, 0))
    wspec = pl.BlockSpec((CONV_KERNEL, CW), lambda b: (0, 0))
    agv = ag.reshape(B, S, 2 * CW)
    res = pl.pallas_call(
        body, grid=(B,),
        in_specs=[pl.BlockSpec((1, S, CW), lambda b: (b, 0, 0)), pl.BlockSpec((1, S, CW), lambda b: (b, 0, 1)),
                  pl.BlockSpec((1, S, CW), lambda b: (b, 0, AW // CW)), wspec, vec, vec, vec, vec],
        out_specs=[pl.BlockSpec((1, S, 2 * CW), lambda b: (b, 0, 0)), wspec, vec, vec, vec, vec],
        out_shape=[jax.ShapeDtypeStruct((B, S, 2 * CW), F32), jax.ShapeDtypeStruct((CONV_KERNEL, CW), F32)]
        + [jax.ShapeDtypeStruct((1, CW), F32)] * 4,
        compiler_params=_params(1), name="conv_bwd",
    )(agv, agv, dmixed.reshape(B, S, D), conv_w, conv_b, ln_g, ln_b, norm_g)
    return (res[0].reshape(B * S, 2 * CW),) + tuple(res[1:])


def _ffn_conv(x, w_ref, bias, row):
    y = jnp.zeros_like(x) + bias
    for k in range(FFN_CONV_KERNEL):
        y = y + w_ref[k:k + 1, :] * _shift_down(x, FFN_CONV_KERNEL - 1 - k, row)
    return y


def _ffn_specs(S, tc, nj, order):
    pick = (lambda b, j: (b, j)) if order == "bj" else (lambda j, b: (b, j))
    act = lambda off: pl.BlockSpec((1, S, tc), lambda *g: (pick(*g)[0], 0, off + pick(*g)[1]))
    cw = lambda off: pl.BlockSpec((FFN_CONV_KERNEL, tc), lambda *g: (0, off + pick(*g)[1]))
    cb = lambda off: pl.BlockSpec((1, tc), lambda *g: (0, off + pick(*g)[1]))
    return act, cw, cb


def _ffn_act(upre, cw, cb, B, S, DFF):
    tc = FFN_COLS
    nj = DFF // tc

    def body(ug_ref, uv_ref, wg_ref, wv_ref, bg_ref, bv_ref, o_ref):
        row = lax.broadcasted_iota(jnp.int32, (S, tc), 0)
        gate = _ffn_conv(ug_ref[0], wg_ref, bg_ref[...], row)
        val = _ffn_conv(uv_ref[0], wv_ref, bv_ref[...], row)
        o_ref[0] = (gate * _sigmoid(gate) * val).astype(BF16)

    act, cws, cbs = _ffn_specs(S, tc, nj, "bj")
    uv = upre.reshape(B, S, 2 * DFF)
    out = pl.pallas_call(
        body, grid=(B, nj), in_specs=[act(0), act(nj), cws(0), cws(nj), cbs(0), cbs(nj)], out_specs=act(0),
        out_shape=jax.ShapeDtypeStruct((B, S, DFF), BF16), compiler_params=_params(2), name="ffn_act",
    )(uv, uv, cw, cw, cb, cb)
    return out.reshape(B * S, DFF)


def _ffn_bwd(upre, dact, cw, cb, B, S, DFF):
    tc = FFN_COLS
    nj = DFF // tc

    def body(ug_ref, uv_ref, da_ref, wg_ref, wv_ref, bg_ref, bv_ref, dug_ref, duv_ref, dwg_ref, dwv_ref,
             dbg_ref, dbv_ref):
        first = pl.program_id(1) == 0
        row = lax.broadcasted_iota(jnp.int32, (S, tc), 0)
        ug, uv = ug_ref[0], uv_ref[0]
        gate = _ffn_conv(ug, wg_ref, bg_ref[...], row)
        val = _ffn_conv(uv, wv_ref, bv_ref[...], row)
        sg = _sigmoid(gate)
        dact_b = da_ref[0]
        dgate = dact_b * val * sg * (1.0 + gate * (1.0 - sg))
        dval = dact_b * gate * sg
        for dup, u, w_ref, du_ref, dw_ref, db_ref in ((dgate, ug, wg_ref, dug_ref, dwg_ref, dbg_ref),
                                                      (dval, uv, wv_ref, duv_ref, dwv_ref, dbv_ref)):
            _accumulate(db_ref, first, jnp.sum(dup, axis=0, keepdims=True))

            @pl.when(first)
            def _(dw_ref=dw_ref):
                dw_ref[...] = jnp.zeros_like(dw_ref)

            dupre = jnp.zeros_like(dup)
            for k in range(FFN_CONV_KERNEL):
                sh = FFN_CONV_KERNEL - 1 - k
                dw_ref[k:k + 1, :] += jnp.sum(dup * _shift_down(u, sh, row), axis=0, keepdims=True)
                dupre = dupre + w_ref[k:k + 1, :] * _shift_up(dup, sh, row)
            du_ref[0] = dupre.astype(BF16)

    act, cws, cbs = _ffn_specs(S, tc, nj, "jb")
    uv = upre.reshape(B, S, 2 * DFF)
    res = pl.pallas_call(
        body, grid=(nj, B),
        in_specs=[act(0), act(nj), act(0), cws(0), cws(nj), cbs(0), cbs(nj)],
        out_specs=[act(0), act(0), cws(0), cws(0), cbs(0), cbs(0)],
        out_shape=[jax.ShapeDtypeStruct((B, S, DFF), BF16)] * 2
        + [jax.ShapeDtypeStruct((FFN_CONV_KERNEL, DFF), F32)] * 2 + [jax.ShapeDtypeStruct((1, DFF), F32)] * 2,
        compiler_params=_params(2), name="ffn_bwd",
    )(uv, uv, dact.reshape(B, S, DFF), cw, cw, cb, cb)
    flat = lambda t: t.reshape(B * S, DFF)
    return (flat(res[0]), flat(res[1]), jnp.concatenate([res[2], res[3]], axis=1),
            jnp.concatenate([res[4], res[5]], axis=1))


def _w_up_block_spec(w_up_sh, tc, off, order):
    _, D, cs = w_up_sh.shape
    assert cs % tc == 0
    bps = cs // tc
    jj = (lambda b, j: j) if order == "bj" else (lambda j, b: j)
    return pl.BlockSpec((1, D, tc), lambda *g: ((off + jj(*g)) // bps, 0, (off + jj(*g)) % bps))


def _ffn_fwd_fused(x1b, w_up_sh, cw, cb, B, S, DFF):
    tc = FFN_COLS
    nj = DFF // tc
    D = x1b.shape[1]

    def body(x_ref, wg_ref, wv_ref, cwg_ref, cwv_ref, cbg_ref, cbv_ref, o_ref):
        w = jnp.concatenate([wg_ref[0], wv_ref[0]], axis=1)
        up = jnp.dot(x_ref[0], w, preferred_element_type=F32)
        row = lax.broadcasted_iota(jnp.int32, (S, tc), 0)
        gate = _ffn_conv(up[:, :tc], cwg_ref, cbg_ref[...], row)
        val = _ffn_conv(up[:, tc:], cwv_ref, cbv_ref[...], row)
        o_ref[0] = (gate * _sigmoid(gate) * val).astype(BF16)

    act, cws, cbs = _ffn_specs(S, tc, nj, "bj")
    out = pl.pallas_call(
        body, grid=(B, nj),
        in_specs=[pl.BlockSpec((1, S, D), lambda b, j: (b, 0, 0)),
                  _w_up_block_spec(w_up_sh, tc, 0, "bj"), _w_up_block_spec(w_up_sh, tc, nj, "bj"),
                  cws(0), cws(nj), cbs(0), cbs(nj)],
        out_specs=act(0), out_shape=jax.ShapeDtypeStruct((B, S, DFF), BF16),
        compiler_params=_params(2), name="ffn_fwd",
    )(x1b.reshape(B, S, D), w_up_sh, w_up_sh, cw, cw, cb, cb)
    return out.reshape(B * S, DFF)


def _ffn_bwd_fused(x1b, dz2b, w_up_sh, w_down, cw, cb, B, S, DFF):
    tc = FFN_COLS
    nj = DFF // tc
    D = x1b.shape[1]

    def body(x_ref, dz_ref, wg_ref, wv_ref, wd_ref, cwg_ref, cwv_ref, cbg_ref, cbv_ref,
             dug_ref, duv_ref, dwu_ref, dwd_ref, dcw_ref, dcb_ref):
        first = pl.program_id(1) == 0
        x = x_ref[0]
        dz = dz_ref[0]
        w = jnp.concatenate([wg_ref[0], wv_ref[0]], axis=1)
        up = jnp.dot(x, w, preferred_element_type=F32)
        row = lax.broadcasted_iota(jnp.int32, (S, tc), 0)
        ug, uv = up[:, :tc], up[:, tc:]
        gate = _ffn_conv(ug, cwg_ref, cbg_ref[...], row)
        val = _ffn_conv(uv, cwv_ref, cbv_ref[...], row)
        sg = _sigmoid(gate)
        act = (gate * sg * val).astype(BF16)
        dact = _dot(dz, wd_ref[...], "nt")
        dgate = dact * val * sg * (1.0 + gate * (1.0 - sg))
        dval = dact * gate * sg

        @pl.when(first)
        def _():
            dcw_ref[...] = jnp.zeros_like(dcw_ref)

        dupre = []
        for h, (dup, u, w_ref) in enumerate(((dgate, ug, cwg_ref), (dval, uv, cwv_ref))):
            _accumulate(dcb_ref.at[h], first, jnp.sum(dup, axis=0, keepdims=True))
            acc = jnp.zeros_like(dup)
            for k in range(FFN_CONV_KERNEL):
                sh = FFN_CONV_KERNEL - 1 - k
                dcw_ref[h, k:k + 1, :] += jnp.sum(dup * _shift_down(u, sh, row), axis=0, keepdims=True)
                acc = acc + w_ref[k:k + 1, :] * _shift_up(dup, sh, row)
            dupre.append(acc.astype(BF16))
        dug_ref[0] = dupre[0]
        duv_ref[0] = dupre[1]
        dw_t = _dot(jnp.concatenate(dupre, axis=1), x, "tn")
        _accumulate(dwu_ref.at[0], first, dw_t[:tc])
        _accumulate(dwu_ref.at[1], first, dw_t[tc:])
        _accumulate(dwd_ref, first, _dot(act, dz, "tn"))

    act_s, cws, cbs = _ffn_specs(S, tc, nj, "jb")
    seq = pl.BlockSpec((1, S, D), lambda j, b: (b, 0, 0))
    res = pl.pallas_call(
        body, grid=(nj, B),
        in_specs=[seq, seq, _w_up_block_spec(w_up_sh, tc, 0, "jb"), _w_up_block_spec(w_up_sh, tc, nj, "jb"),
                  pl.BlockSpec((tc, D), lambda j, b: (j, 0)), cws(0), cws(nj), cbs(0), cbs(nj)],
        out_specs=[act_s(0), act_s(0), pl.BlockSpec((2, tc, D), lambda j, b: (0, j, 0)),
                   pl.BlockSpec((tc, D), lambda j, b: (j, 0)),
                   pl.BlockSpec((2, FFN_CONV_KERNEL, tc), lambda j, b: (0, 0, j)),
                   pl.BlockSpec((2, 1, tc), lambda j, b: (0, 0, j))],
        out_shape=[jax.ShapeDtypeStruct((B, S, DFF), BF16)] * 2
        + [jax.ShapeDtypeStruct((2, DFF, D), F32), jax.ShapeDtypeStruct((DFF, D), F32),
           jax.ShapeDtypeStruct((2, FFN_CONV_KERNEL, DFF), F32), jax.ShapeDtypeStruct((2, 1, DFF), F32)],
        compiler_params=_params(2), name="ffn_bwd",
    )(x1b.reshape(B, S, D), dz2b.reshape(B, S, D), w_up_sh, w_up_sh, w_down, cw, cw, cb, cb)
    flat = lambda t: t.reshape(B * S, DFF)
    return flat(res[0]), flat(res[1]), res[2], res[3], res[4], res[5]


def _transpose(x, tr, name):
    R, C = x.shape

    def body(x_ref, o_ref):
        o_ref[...] = x_ref[...].T

    return pl.pallas_call(
        body, grid=(R // tr,), in_specs=[pl.BlockSpec((tr, C), lambda i: (i, 0))],
        out_specs=pl.BlockSpec((C, tr), lambda i: (0, i)), out_shape=jax.ShapeDtypeStruct((C, R), F32),
        compiler_params=_params(1), name=name)(x)


def _dh_cat(dq, dk, dv, dag, tm):
    T, AW = dq.shape
    CW2 = dag.shape[1]
    W = 3 * AW + CW2

    def body(dq_ref, dk_ref, dv_ref, dag_ref, dh_ref, cs_ref):
        for c, ref in enumerate((dq_ref, dk_ref, dv_ref)):
            dh_ref[:, c * AW:(c + 1) * AW] = ref[...]
        dg = dag_ref[...]
        dh_ref[:, 3 * AW:] = dg.astype(BF16)
        _accumulate(cs_ref, pl.program_id(0) == 0, jnp.sum(dg, axis=0, keepdims=True))

    row = pl.BlockSpec((tm, AW), lambda i: (i, 0))
    return pl.pallas_call(
        body, grid=(T // tm,),
        in_specs=[row] * 3 + [pl.BlockSpec((tm, CW2), lambda i: (i, 0))],
        out_specs=[pl.BlockSpec((tm, W), lambda i: (i, 0)), pl.BlockSpec((1, CW2), lambda i: (0, 0))],
        out_shape=[jax.ShapeDtypeStruct((T, W), BF16), jax.ShapeDtypeStruct((1, CW2), F32)],
        compiler_params=_params(1), name="dh_cat",
    )(dq, dk, dv, dag)


def _local_step(x, target, rel_table, w_in, b_in, conv_w, conv_b, conv_ln_g, conv_ln_b, attn_norm_g,
                conv_norm_g, staged, ln1_g, ln1_b, ffn_cw, ffn_cb, ln2_g, ln2_b, ids):
    B, S, D = x.shape
    T = B * S
    AW = attn_norm_g.shape[-1]
    CW = conv_norm_g.shape[-1]
    H = AW // HEAD_DIM
    DFF = staged[2].shape[0] * staged[2].shape[1]
    INW = 3 * AW + 2 * CW
    xf = x.reshape(T, D)
    tf = target.reshape(T, D)
    tm = _row_tile(T, 512)
    tm_s = _row_tile(T, 256)

    bucket_np, mask_np = _bucket_tables()
    bucket = jnp.asarray(bucket_np)
    band_mask = jnp.asarray(mask_np)
    bias_all = _bias_build(rel_table.T, bucket, band_mask).reshape(3, H, ATTN_BLOCK, 2 * ATTN_BLOCK)

    tn_qkv = _col_tile(3 * AW, 1152)
    qkv = _mm_plain(xf, w_in[:, :3 * AW], mode="nn", tm=tm, tn=tn_qkv, tk=D, out_dtype=BF16,
                    bias=b_in[:, :3 * AW], name="mm_qkv")
    ag = _mm_plain(xf, w_in[:, 3 * AW:], mode="nn", tm=tm, tn=2 * CW, tk=D, out_dtype=F32,
                   bias=b_in[:, 3 * AW:], name="mm_ag")

    attn, lse, w_out_g, w_up_sh, w_down_g = _attention_fwd(qkv, bias_all, B, S, AW, bg=_bg_gather(staged))
    w_out = w_out_g.reshape(D, D)
    w_down = w_down_g.reshape(DFF, D)
    mixed_a, r_attn = _attn_norm(attn, attn_norm_g, tm_s)
    mixed_c = _conv_fwd(ag, conv_w, conv_b, conv_ln_g, conv_ln_b, conv_norm_g, B, S, CW)
    mixed = jnp.concatenate([mixed_a, mixed_c], axis=1)

    def ln1_epilogue(acc, i, j, extra_refs, out_refs):
        x_ref, g_ref, b_ref = extra_refs
        x1, xh, r = _ln_fwd(acc + ALPHA * x_ref[...], g_ref[...], b_ref[...])
        out_refs[0][...] = x1
        out_refs[1][...] = x1.astype(BF16)
        out_refs[2][...] = xh
        out_refs[3][...] = jnp.broadcast_to(r, (tm_s, LANES))

    rowD = lambda i, j, k: (i, 0)
    vecD = lambda i, j, k: (0, 0)
    x1, x1b, xh1, r1 = _matmul(
        mixed, w_out, mode="nn", tm=tm_s, tn=D, tk=D,
        extras=[(xf, (tm_s, D), rowD), (ln1_g, (1, D), vecD), (ln1_b, (1, D), vecD)],
        outs=[((T, D), F32, (tm_s, D), rowD), ((T, D), BF16, (tm_s, D), rowD), ((T, D), F32, (tm_s, D), rowD),
              ((T, LANES), F32, (tm_s, LANES), rowD)],
        epilogue=ln1_epilogue, name="mm_out_ln1")

    NS, _, cs = w_up_sh.shape
    half = NS // 2

    act = _ffn_fwd_fused(x1b, w_up_sh, ffn_cw, ffn_cb, B, S, DFF)

    def ln2_epilogue(acc, i, j, extra_refs, out_refs):
        x1_ref, g_ref, b_ref, t_ref = extra_refs
        dz_ref, dzb_ref, loss_ref, dg_ref, db_ref = out_refs
        g = g_ref[...]
        y, xh, r = _ln_fwd(acc + ALPHA * x1_ref[...], g, b_ref[...])
        diff = y - t_ref[...]
        row_loss = jnp.sum(diff * diff, axis=1, keepdims=True)
        tile_loss = jnp.sum(row_loss, axis=0, keepdims=True) * (0.5 / D)
        dy = diff * (1.0 / D)
        dz = _ln_bwd(dy, xh, r, g)
        dz_ref[...] = dz
        dzb_ref[...] = dz.astype(BF16)
        first = i == 0
        _accumulate(loss_ref, first, jnp.broadcast_to(tile_loss, (1, LANES)))
        _accumulate(dg_ref, first, jnp.sum(dy * xh, axis=0, keepdims=True))
        _accumulate(db_ref, first, jnp.sum(dy, axis=0, keepdims=True))

    dz2, dz2b, loss_part, d_ln2_g, d_ln2_b = _matmul(
        act, w_down, mode="nn", tm=tm_s, tn=D, tk=DFF,
        extras=[(x1, (tm_s, D), rowD), (ln2_g, (1, D), vecD), (ln2_b, (1, D), vecD), (tf, (tm_s, D), rowD)],
        outs=[((T, D), F32, (tm_s, D), rowD), ((T, D), BF16, (tm_s, D), rowD),
              ((1, LANES), F32, (1, LANES), vecD), ((1, D), F32, (1, D), vecD), ((1, D), F32, (1, D), vecD)],
        epilogue=ln2_epilogue, name="mm_down_ln2_loss")

    dupre_g, dupre_v, d_w_up_t, d_w_down, d_ffn_cw2, d_ffn_cb2 = _ffn_bwd_fused(
        x1b, dz2b, w_up_sh, w_down, ffn_cw, ffn_cb, B, S, DFF)
    d_w_up_t = d_w_up_t.reshape(NS, cs, D)
    d_ffn_cw = jnp.transpose(d_ffn_cw2, (1, 0, 2)).reshape(FFN_CONV_KERNEL, 2 * DFF)
    d_ffn_cb = d_ffn_cb2.reshape(1, 2 * DFF)
    tk_t = _row_tile(T, 512)

    def ln1_bwd_epilogue(acc, i, j, extra_refs, out_refs):
        dz2_ref, xh_ref, r_ref, g_ref = extra_refs
        dz_ref, dzb_ref, dg_ref, db_ref = out_refs
        dx1 = acc + ALPHA * dz2_ref[...]
        xh = xh_ref[...]
        dz = _ln_bwd(dx1, xh, r_ref[:, 0:1], g_ref[...])
        dz_ref[...] = dz
        dzb_ref[...] = dz.astype(BF16)
        first = i == 0
        _accumulate(dg_ref, first, jnp.sum(dx1 * xh, axis=0, keepdims=True))
        _accumulate(db_ref, first, jnp.sum(dx1, axis=0, keepdims=True))

    early = [d_w_up_t, d_w_down.reshape(NS, DFF // NS, D)]
    dz1, dz1b, d_ln1_g, d_ln1_b, *sib_e = _matmul_general(
        [(dupre_g, (tm, cs), lambda i, j, k: (i, jnp.minimum(k, half - 1))),
         (dupre_v, (tm, cs), lambda i, j, k: (i, jnp.maximum(k - half, 0))),
         (w_up_sh, (1, D, cs), lambda i, j, k: (k, 0, 0))],
        lambda refs, i, j, k: _dot(jnp.where(k < half, refs[0][...], refs[1][...]), refs[2][0], "nt"),
        grid=(T // tm, 1, NS), tm=tm, tn=D,
        extras=[(dz2, (tm, D), rowD), (xh1, (tm, D), rowD), (r1, (tm, LANES), rowD), (ln1_g, (1, D), vecD)],
        outs=[((T, D), F32, (tm, D), rowD), ((T, D), BF16, (tm, D), rowD),
              ((1, D), F32, (1, D), vecD), ((1, D), F32, (1, D), vecD)],
        epilogue=ln1_bwd_epilogue, name="mm_dx1_ln1_bwd", bg=_bg_sibling_exchange(early))
    chip_e = [_pair_sum(g, s, ids, name="pair_sum_" + n) for g, s, n in zip(early, sib_e, ("w_up", "w_down"))]

    dmixed = _mm_plain(dz1b, w_out, mode="nt", tm=tm, tn=D, tk=D, out_dtype=F32, name="mm_dmixed")
    d_w_out = _mm_plain(mixed, dz1b, mode="tn", tm=D, tn=D, tk=tk_t, out_dtype=F32, name="mm_dw_out")

    dattn, dd, d_attn_norm_g = _attn_pre_bwd(dmixed, attn, r_attn, attn_norm_g, tm_s)
    dag, d_conv_w, d_conv_b, d_conv_ln_g, d_conv_ln_b, d_conv_norm_g = _conv_bwd(
        ag, dmixed, conv_w, conv_b, conv_ln_g, conv_ln_b, conv_norm_g, B, S, CW, D)

    dq, dk, dv, csq, csk, csv, dbias, *got_e = _attention_bwd(qkv, dattn, lse, dd, bias_all, B, S, AW,
                                                              bg=_bg_chip_exchange(chip_e))
    full_up, full_down = [_final_sum(g, s, r, ids, name="final_sum_" + n)
                          for g, s, r, n in zip(early, sib_e, got_e, ("w_up", "w_down"))]
    d_rel_table = _rel_grad(dbias.reshape(3, H, ATTN_BLOCK * 2 * ATTN_BLOCK), bucket).T
    dh, cs_ag = _dh_cat(dq, dk, dv, dag, tm_s)
    d_b_in = jnp.concatenate([csq, csk, csv, cs_ag], axis=1)

    d_w_in = _mm_plain(xf, dh, mode="tn", tm=D, tn=_col_tile(INW, 1408), tk=tk_t, out_dtype=F32, name="mm_dw_in")
    late = [jnp.transpose(d_w_in.reshape(D, NS, INW // NS), (1, 0, 2)), d_w_out.reshape(NS, D // NS, D)]
    sib_l = _sibling_exchange(late)
    chip_l = [_pair_sum(g, s, ids, name="pair_sum_" + n) for g, s, n in zip(late, sib_l, ("w_in", "w_out"))]
    small = dict(rel_table=d_rel_table, b_in=d_b_in, conv_w=d_conv_w, conv_b=d_conv_b, conv_ln_g=d_conv_ln_g,
                 conv_ln_b=d_conv_ln_b, attn_norm_g=d_attn_norm_g, conv_norm_g=d_conv_norm_g, ln1_g=d_ln1_g,
                 ln1_b=d_ln1_b, ffn_conv_w=d_ffn_cw, ffn_conv_b=d_ffn_cb, ln2_g=d_ln2_g, ln2_b=d_ln2_b)
    pack = _pack([loss_part] + [small[n] for n in SMALL_NAMES])

    def gx_epilogue(acc, i, j, extra_refs, out_refs):
        out_refs[0][...] = acc + ALPHA * extra_refs[0][...]

    grad_x, got_in, got_out, all_packs = _matmul(
        dh, w_in, mode="nt", tm=tm_s, tn=D, tk=INW, extras=[(dz1, (tm_s, D), rowD)],
        outs=[((T, D), F32, (tm_s, D), rowD)], epilogue=gx_epilogue, name="mm_grad_x",
        bg=_bg_chip_exchange(chip_l, pack))
    full_in, full_out = [_final_sum(g, s, r, ids, name="final_sum_" + n)
                         for g, s, r, n in zip(late, sib_l, (got_in, got_out), ("w_in", "w_out"))]
    return grad_x.reshape(B, S, D), [full_in, full_out, full_up, full_down], all_packs


def _place():
    return lax.axis_index("x"), lax.axis_index("y"), lax.axis_index("c")


CHIP_FLIPS = ((1, 0), (0, 1), (1, 1))


def _flip(v, f):
    return 1 - v if f else v


HBM_SPEC = pl.BlockSpec(memory_space=pl.ANY)
VMEM_SPEC = pl.BlockSpec(memory_space=pltpu.VMEM)
COMM_PARAMS = pltpu.CompilerParams(vmem_limit_bytes=VMEM_LIMIT)


def _gather_weights(big, small):
    nb, ns = len(big), len(small)

    def body(*refs):
        big_in = refs[:nb]
        small_in = refs[nb:nb + ns]
        big_out = refs[nb + ns:2 * nb + ns]
        small_out = refs[2 * nb + ns:2 * nb + 2 * ns]
        stages = refs[2 * nb + 2 * ns:3 * nb + 2 * ns]
        send_sems, recv_sems, local_sems = refs[3 * nb + 2 * ns:]
        x, y, c = _place()
        s_me = 2 * x + y
        sibling = (x, y, 1 - c)
        started, local_copies = [], []
        for a in range(nb):
            rh = big[a].shape[0] // 2
            lo = pl.multiple_of(c * rh, 16)
            stages[a][...] = big_in[a][pl.ds(lo, rh), :].astype(BF16)
            mine = big_out[a].at[s_me, pl.ds(lo, rh), :]
            loc = pltpu.make_async_copy(stages[a], mine, local_sems.at[a])
            loc.start()
            local_copies.append(loc)
            targets = [sibling] + [(_flip(x, fx), _flip(y, fy), c) for fx, fy in CHIP_FLIPS]
            for k, to in enumerate(targets):
                cp = pltpu.make_async_remote_copy(stages[a], mine, send_sems.at[a * 7 + k],
                                                  recv_sems.at[a * 7 + k], device_id=to, device_id_type=MESH)
                cp.start()
                started.append(cp)
        for a in range(ns):
            mine = small_out[a].at[s_me]
            loc = pltpu.make_async_copy(small_in[a], mine, local_sems.at[nb + a])
            loc.start()
            local_copies.append(loc)
            for k, (fx, fy) in enumerate(CHIP_FLIPS):
                cp = pltpu.make_async_remote_copy(small_in[a], mine, send_sems.at[nb * 7 + a * 3 + k],
                                                  recv_sems.at[nb * 7 + a * 3 + k],
                                                  device_id=(_flip(x, fx), _flip(y, fy), c), device_id_type=MESH)
                cp.start()
                started.append(cp)
        for a in range(nb):
            rh = big[a].shape[0] // 2
            lo = pl.multiple_of(c * rh, 16)
            for k, (fx, fy) in enumerate(CHIP_FLIPS):
                s_from = 2 * _flip(x, fx) + _flip(y, fy)
                got = big_out[a].at[s_from, pl.ds(lo, rh), :]
                pltpu.make_async_remote_copy(got, got, send_sems.at[a * 7 + 1 + k], recv_sems.at[a * 7 + 1 + k],
                                             device_id=sibling, device_id_type=MESH).wait_recv()
                fwd = pltpu.make_async_remote_copy(got, got, send_sems.at[a * 7 + 4 + k],
                                                   recv_sems.at[a * 7 + 4 + k], device_id=sibling,
                                                   device_id_type=MESH)
                fwd.start()
                started.append(fwd)
        for a in range(nb):
            rh = big[a].shape[0] // 2
            lo_sib = pl.multiple_of((1 - c) * rh, 16)
            for k in (0, 4, 5, 6):
                any_rows = big_out[a].at[s_me, pl.ds(lo_sib, rh), :]
                pltpu.make_async_remote_copy(any_rows, any_rows, send_sems.at[a * 7 + k], recv_sems.at[a * 7 + k],
                                             device_id=sibling, device_id_type=MESH).wait_recv()
        for a in range(ns):
            for k in range(3):
                pltpu.make_async_remote_copy(small_in[a], small_out[a].at[s_me], send_sems.at[nb * 7 + a * 3 + k],
                                             recv_sems.at[nb * 7 + a * 3 + k], device_id=sibling,
                                             device_id_type=MESH).wait_recv()
        for cp in started:
            cp.wait_send()
        for cp in local_copies:
            cp.wait()

    n_sem = nb * 7 + ns * 3
    out_shape = ([jax.ShapeDtypeStruct((N_SHARDS,) + w.shape, BF16) for w in big]
                 + [jax.ShapeDtypeStruct((N_SHARDS,) + w.shape, F32) for w in small])
    res = pl.pallas_call(
        body, in_specs=[VMEM_SPEC] * nb + [HBM_SPEC] * ns, out_specs=[HBM_SPEC] * (nb + ns),
        out_shape=out_shape,
        scratch_shapes=[pltpu.VMEM((w.shape[0] // 2, w.shape[1]), BF16) for w in big]
        + [pltpu.SemaphoreType.DMA((n_sem,)), pltpu.SemaphoreType.DMA((n_sem,)),
           pltpu.SemaphoreType.DMA((nb + ns,))],
        compiler_params=COMM_PARAMS, name="gather_weights",
    )(*big, *small)
    return res[:nb], res[nb:]


def _sibling_exchange(grads):
    n = len(grads)

    def body(*refs):
        g_in = refs[:n]
        got = refs[n:2 * n]
        send_sems, recv_sems = refs[2 * n:]
        x, y, c = _place()
        cps = []
        for a in range(n):
            rh = grads[a].shape[1] // 2
            lo = pl.multiple_of((1 - c) * rh, 8)
            cp = pltpu.make_async_remote_copy(g_in[a].at[:, pl.ds(lo, rh), :], got[a], send_sems.at[a],
                                              recv_sems.at[a], device_id=(x, y, 1 - c), device_id_type=MESH)
            cp.start()
            cps.append(cp)
        for cp in cps:
            cp.wait()

    return pl.pallas_call(
        body, in_specs=[HBM_SPEC] * n, out_specs=[HBM_SPEC] * n,
        out_shape=[jax.ShapeDtypeStruct((N_SHARDS, g.shape[1] // 2, g.shape[2]), F32) for g in grads],
        scratch_shapes=[pltpu.SemaphoreType.DMA((n,)), pltpu.SemaphoreType.DMA((n,))],
        compiler_params=COMM_PARAMS, name="sibling_exchange",
    )(*grads)


def _chip_exchange(chip_parts, pack):
    n = len(chip_parts)

    def body(*refs):
        parts = refs[:n]
        pack_ref = refs[n]
        got = refs[n + 1:2 * n + 1]
        all_packs = refs[2 * n + 1]
        send_sems, recv_sems, local_sem = refs[2 * n + 2:]
        x, y, c = _place()
        me = 4 * x + 2 * y + c
        cps = []
        for a in range(n):
            for k, (fx, fy) in enumerate(CHIP_FLIPS):
                px, py = _flip(x, fx), _flip(y, fy)
                cp = pltpu.make_async_remote_copy(parts[a].at[2 * px + py], got[a].at[k], send_sems.at[a * 3 + k],
                                                  recv_sems.at[a * 3 + k], device_id=(px, py, c),
                                                  device_id_type=MESH)
                cp.start()
                cps.append(cp)
        loc = pltpu.make_async_copy(pack_ref, all_packs.at[me], local_sem)
        loc.start()
        for m in range(1, N_DEV):
            to = (_flip(x, m & 4), _flip(y, m & 2), _flip(c, m & 1))
            cp = pltpu.make_async_remote_copy(pack_ref, all_packs.at[me], send_sems.at[n * 3 + m - 1],
                                              recv_sems.at[n * 3 + m - 1], device_id=to, device_id_type=MESH)
            cp.start()
            cps.append(cp)
        for cp in cps:
            cp.wait()
        loc.wait()

    rs = pack.shape[0]
    res = pl.pallas_call(
        body, in_specs=[HBM_SPEC] * (n + 1), out_specs=[HBM_SPEC] * (n + 1),
        out_shape=[jax.ShapeDtypeStruct((3,) + p.shape[1:], BF16) for p in chip_parts]
        + [jax.ShapeDtypeStruct((N_DEV, rs, LANES), F32)],
        scratch_shapes=[pltpu.SemaphoreType.DMA((n * 3 + N_DEV - 1,)), pltpu.SemaphoreType.DMA((n * 3 + N_DEV - 1,)),
                        pltpu.SemaphoreType.DMA],
        compiler_params=COMM_PARAMS, name="chip_exchange",
    )(*chip_parts, pack)
    return res[:n], res[n]


def _sibling_assemble(fulls):
    n = len(fulls)

    def body(*refs):
        full = refs[n:2 * n]
        send_sems, recv_sems = refs[2 * n:]
        x, y, c = _place()
        cps = []
        for a in range(n):
            rh = fulls[a].shape[0] // 2
            mine = full[a].at[pl.ds(pl.multiple_of(c * rh, 8), rh), :]
            cp = pltpu.make_async_remote_copy(mine, mine, send_sems.at[a], recv_sems.at[a],
                                              device_id=(x, y, 1 - c), device_id_type=MESH)
            cp.start()
            cps.append(cp)
        for cp in cps:
            cp.wait()

    return pl.pallas_call(
        body, in_specs=[HBM_SPEC] * n, out_specs=[HBM_SPEC] * n,
        out_shape=[jax.ShapeDtypeStruct(f.shape, F32) for f in fulls],
        input_output_aliases={a: a for a in range(n)},
        scratch_shapes=[pltpu.SemaphoreType.DMA((n,)), pltpu.SemaphoreType.DMA((n,))],
        compiler_params=COMM_PARAMS, name="sibling_assemble",
    )(*fulls)


def _remote(ref_src, ref_dst, send_sems, recv_sems, k, to):
    return pltpu.make_async_remote_copy(ref_src, ref_dst, send_sems.at[k], recv_sems.at[k], device_id=to,
                                        device_id_type=MESH)


def _stage_half(w, ids, name):
    R, C = w.shape
    rh = R // 2
    rt = _half_tile(rh)
    nt = rh // rt

    def body(ids_ref, w_ref, o_ref):
        o_ref[0] = w_ref[...].astype(BF16)

    grid_spec = pltpu.PrefetchScalarGridSpec(
        num_scalar_prefetch=1, grid=(nt,),
        in_specs=[pl.BlockSpec((rt, C), lambda i, ids: (ids[2] * nt + i, 0))],
        out_specs=pl.BlockSpec((1, rt, C), lambda i, ids: (2 * ids[0] + ids[1], ids[2] * nt + i, 0)))
    return pl.pallas_call(body, grid_spec=grid_spec, out_shape=jax.ShapeDtypeStruct((N_SHARDS, R, C), BF16),
                          compiler_params=_params(1), name=name)(ids, w)


def _bg_gather(staged):
    n = len(staged)

    def run(step, n_steps, ins, outs, send_sems, recv_sems, local_sems, post):
        x, y, c = _place()
        s_me = 2 * x + y
        sibling = (x, y, 1 - c)
        chips = [(_flip(x, fx), _flip(y, fy)) for fx, fy in CHIP_FLIPS]

        def rows(a, s, half):
            rh = staged[a].shape[1] // 2
            return outs[a].at[s, pl.ds(pl.multiple_of(half * rh, 16), rh), :]

        def copy(a, k, ref, to):
            return _remote(ref, ref, send_sems, recv_sems, a * 7 + k, to)

        if not post:
            @pl.when(step == 0)
            def _():
                for a in range(n):
                    mine = rows(a, s_me, c)
                    copy(a, 0, mine, sibling).start()
                    for k, (px, py) in enumerate(chips):
                        copy(a, 1 + k, mine, (px, py, c)).start()

            @pl.when(step == n_steps // 2)
            def _():
                for a in range(n):
                    for k, (px, py) in enumerate(chips):
                        got = rows(a, 2 * px + py, c)
                        copy(a, 1 + k, got, sibling).wait_recv()
                        copy(a, 4 + k, got, sibling).start()
        else:
            @pl.when(step == n_steps - 1)
            def _():
                for a in range(n):
                    for k in (0, 4, 5, 6):
                        copy(a, k, rows(a, s_me, 1 - c), sibling).wait_recv()
                    for k in range(7):
                        copy(a, k, rows(a, s_me, c), sibling).wait_send()

    return _Background(staged, [jax.ShapeDtypeStruct(g.shape, g.dtype) for g in staged],
                       {a: a for a in range(n)}, 7 * n, run)


def _bg_sibling_exchange(grads):
    n = len(grads)

    def run(step, n_steps, ins, outs, send_sems, recv_sems, local_sems, post):
        x, y, c = _place()

        def copy(a):
            rh = grads[a].shape[1] // 2
            lo = pl.multiple_of((1 - c) * rh, 8)
            return _remote(ins[a].at[:, pl.ds(lo, rh), :], outs[a], send_sems, recv_sems, a, (x, y, 1 - c))

        if not post:
            @pl.when(step == 0)
            def _():
                for a in range(n):
                    copy(a).start()
        else:
            @pl.when(step == n_steps - 1)
            def _():
                for a in range(n):
                    copy(a).wait()

    return _Background(grads, [jax.ShapeDtypeStruct((N_SHARDS, g.shape[1] // 2, g.shape[2]), F32) for g in grads],
                       {}, n, run)


def _bg_chip_exchange(chip_parts, pack=None):
    n = len(chip_parts)

    def run(step, n_steps, ins, outs, send_sems, recv_sems, local_sems, post):
        x, y, c = _place()
        me = 4 * x + 2 * y + c

        def copies():
            cps = []
            for a in range(n):
                for k, (fx, fy) in enumerate(CHIP_FLIPS):
                    px, py = _flip(x, fx), _flip(y, fy)
                    cps.append(_remote(ins[a].at[2 * px + py], outs[a].at[k], send_sems, recv_sems, a * 3 + k,
                                       (px, py, c)))
            if pack is not None:
                for m in range(1, N_DEV):
                    to = (_flip(x, m & 4), _flip(y, m & 2), _flip(c, m & 1))
                    cps.append(_remote(ins[n], outs[n].at[me], send_sems, recv_sems, n * 3 + m - 1, to))
            return cps

        def local():
            return pltpu.make_async_copy(ins[n], outs[n].at[me], local_sems.at[0])

        if not post:
            @pl.when(step == 0)
            def _():
                for cp in copies():
                    cp.start()
                if pack is not None:
                    local().start()
        else:
            @pl.when(step == n_steps - 1)
            def _():
                for cp in copies():
                    cp.wait()
                if pack is not None:
                    local().wait()

    in_arrays = list(chip_parts) + ([pack] if pack is not None else [])
    out_shapes = [jax.ShapeDtypeStruct((3,) + p.shape[1:], BF16) for p in chip_parts]
    if pack is not None:
        out_shapes.append(jax.ShapeDtypeStruct((N_DEV, pack.shape[0], LANES), F32))
    return _Background(in_arrays, out_shapes, {}, n * 3 + N_DEV - 1, run)


def _half_tile(rh, mult=16, want=256):
    best = None
    for t in range(mult, min(rh, want) + 1, mult):
        if rh % t == 0:
            best = t
    return best if best is not None else rh


def _pair_sum(g, sib, ids, name):
    _, R, C = g.shape
    rh = R // 2
    rt = _half_tile(rh)
    nt = rh // rt

    def body(ids_ref, g_ref, s_ref, o_ref):
        o_ref[...] = (g_ref[...] + s_ref[...]).astype(BF16)

    grid_spec = pltpu.PrefetchScalarGridSpec(
        num_scalar_prefetch=1, grid=(N_SHARDS, nt),
        in_specs=[pl.BlockSpec((1, rt, C), lambda s, i, ids: (s, ids[2] * nt + i, 0)),
                  pl.BlockSpec((1, rt, C), lambda s, i, ids: (s, i, 0))],
        out_specs=pl.BlockSpec((1, rt, C), lambda s, i, ids: (s, i, 0)))
    return pl.pallas_call(body, grid_spec=grid_spec, out_shape=jax.ShapeDtypeStruct((N_SHARDS, rh, C), BF16),
                          compiler_params=_params(2), name=name)(ids, g, sib)


def _final_sum(g, sib, got, ids, name):
    _, R, C = g.shape
    rh = R // 2
    rt = _half_tile(rh)
    nt = rh // rt

    def body(ids_ref, g_ref, s_ref, r_ref, o_ref):
        tot = g_ref[0] + s_ref[0]
        for k in range(3):
            tot = tot + r_ref[k].astype(F32)
        o_ref[...] = tot

    grid_spec = pltpu.PrefetchScalarGridSpec(
        num_scalar_prefetch=1, grid=(nt,),
        in_specs=[pl.BlockSpec((1, rt, C), lambda i, ids: (2 * ids[0] + ids[1], ids[2] * nt + i, 0)),
                  pl.BlockSpec((1, rt, C), lambda i, ids: (2 * ids[0] + ids[1], i, 0)),
                  pl.BlockSpec((3, rt, C), lambda i, ids: (0, i, 0))],
        out_specs=pl.BlockSpec((rt, C), lambda i, ids: (ids[2] * nt + i, 0)))
    return pl.pallas_call(body, grid_spec=grid_spec, out_shape=jax.ShapeDtypeStruct((R, C), F32),
                          compiler_params=_params(1), name=name)(ids, g, sib, got)


def _sum_packs(all_packs):
    def body(p_ref, o_ref):
        tot = p_ref[0]
        for i in range(1, N_DEV):
            tot = tot + p_ref[i]
        o_ref[...] = tot

    return pl.pallas_call(body, in_specs=[VMEM_SPEC], out_specs=VMEM_SPEC,
                          out_shape=jax.ShapeDtypeStruct(all_packs.shape[1:], F32), name="sum_packs")(all_packs)


def _adamw(w, g, m, v, name):
    R, C = w.shape
    rt = _half_tile(R, mult=8, want=256)

    def body(w_ref, g_ref, m_ref, v_ref, d_ref, nm_ref, nv_ref):
        gg = g_ref[...]
        nm = ADAM_B1 * m_ref[...] + (1.0 - ADAM_B1) * gg
        nv = ADAM_B2 * v_ref[...] + (1.0 - ADAM_B2) * (gg * gg)
        m_hat = nm / (1.0 - ADAM_B1 ** ADAM_STEP)
        v_hat = nv / (1.0 - ADAM_B2 ** ADAM_STEP)
        d_ref[...] = -ADAM_LR * (m_hat / (jnp.sqrt(v_hat) + ADAM_EPS) + ADAM_WD * w_ref[...])
        nm_ref[...] = nm
        nv_ref[...] = nv

    spec = pl.BlockSpec((rt, C), lambda i: (i, 0))
    return pl.pallas_call(body, grid=(R // rt,), in_specs=[spec] * 4, out_specs=[spec] * 3,
                          out_shape=[jax.ShapeDtypeStruct((R, C), F32)] * 3,
                          compiler_params=_params(1), name=name)(w, g, m, v)


def _pack(pieces):
    rows = []
    for p in pieces:
        flat = p.reshape(-1)
        pad = (-flat.shape[0]) % LANES
        if pad:
            flat = jnp.concatenate([flat, jnp.zeros((pad,), F32)])
        rows.append(flat.reshape(-1, LANES))
    total = sum(r.shape[0] for r in rows)
    pad_rows = (-total) % 8
    if pad_rows:
        rows.append(jnp.zeros((pad_rows, LANES), F32))
    return jnp.concatenate(rows, axis=0)


def _unpack(buf, shapes):
    out, r0 = [], 0
    for shp in shapes:
        n = int(np.prod(shp))
        nr = -(-n // LANES)
        out.append(buf[r0:r0 + nr].reshape(-1)[:n].reshape(shp))
        r0 += nr
    return out


SMALL_NAMES = ("rel_table", "b_in", "conv_w", "conv_b", "conv_ln_g", "conv_ln_b", "attn_norm_g", "conv_norm_g",
               "ln1_g", "ln1_b", "ffn_conv_w", "ffn_conv_b", "ln2_g", "ln2_b")
BIG_NAMES = ("w_in", "w_out", "w_up", "w_down")
WEIGHT_ORDER = ("rel_table", "w_in", "b_in", "conv_w", "conv_b", "conv_ln_g", "conv_ln_b", "attn_norm_g",
                "conv_norm_g", "w_out", "ln1_g", "ln1_b", "w_up", "ffn_conv_w", "ffn_conv_b", "w_down",
                "ln2_g", "ln2_b")


def kernel(x, rel_table, w_in, b_in, conv_w, conv_b, conv_ln_g, conv_ln_b, attn_norm_g, conv_norm_g, w_out, ln1_g, ln1_b, w_up, ffn_conv_w, ffn_conv_b, w_down, ln2_g, ln2_b, loss_target, m_rel_table, m_w_in, m_b_in, m_conv_w, m_conv_b, m_conv_ln_g, m_conv_ln_b, m_attn_norm_g, m_conv_norm_g, m_w_out, m_ln1_g, m_ln1_b, m_w_up, m_ffn_conv_w, m_ffn_conv_b, m_w_down, m_ln2_g, m_ln2_b, v_rel_table, v_w_in, v_b_in, v_conv_w, v_conv_b, v_conv_ln_g, v_conv_ln_b, v_attn_norm_g, v_conv_norm_g, v_w_out, v_ln1_g, v_ln1_b, v_w_up, v_ffn_conv_w, v_ffn_conv_b, v_w_down, v_ln2_g, v_ln2_b):
    args = dict(locals())
    weights = {n: args[n] for n in WEIGHT_ORDER}
    moms = {n: args["m_" + n] for n in WEIGHT_ORDER}
    vels = {n: args["v_" + n] for n in WEIGHT_ORDER}
    xi, yi, ci = _place()
    ids = jnp.stack([xi, yi, ci]).astype(jnp.int32)
    shard = 2 * xi + yi
    D = x.shape[-1]
    DFF = w_down.shape[1] * N_SHARDS
    CW = conv_norm_g.shape[-1]

    (g_in,), (g_cw, g_fcw) = _gather_weights([w_in[0]], [conv_w[0], ffn_conv_w[0]])
    cols = lambda t: jnp.transpose(t, (1, 0, 2)).reshape(t.shape[1], N_SHARDS * t.shape[2])
    staged = [_stage_half(w[0], ids, name="stage_" + n) for w, n in ((w_out, "w_out"), (w_up, "w_up"),
                                                                     (w_down, "w_down"))]

    grad_x, fulls, all_packs = _local_step(
        x, loss_target, rel_table, cols(g_in), b_in, cols(g_cw), conv_b, conv_ln_g, conv_ln_b, attn_norm_g,
        conv_norm_g, staged, ln1_g, ln1_b, cols(g_fcw), ffn_conv_b, ln2_g, ln2_b, ids)
    big_grads = dict(zip(BIG_NAMES, _sibling_assemble(fulls)))
    big_grads["w_up"] = _transpose(big_grads["w_up"], LANES, name="transpose_dw_up")

    summed = _sum_packs(all_packs)
    full_shapes = {n: weights[n].shape for n in SMALL_NAMES}
    full_shapes["conv_w"] = (1, CONV_KERNEL, CW)
    full_shapes["ffn_conv_w"] = (1, FFN_CONV_KERNEL, 2 * DFF)
    un = _unpack(summed, [(1, LANES)] + [full_shapes[n] for n in SMALL_NAMES])
    loss = un[0][0, 0]
    small_grads = dict(zip(SMALL_NAMES, un[1:]))
    for n in ("conv_w", "ffn_conv_w"):
        width = weights[n].shape[-1]
        small_grads[n] = lax.dynamic_slice_in_dim(small_grads[n], shard * width, width, axis=2)

    grads, delta, new_m, new_v = {}, {}, {}, {}
    for n in BIG_NAMES:
        shp = weights[n].shape
        g2 = big_grads[n]
        d, nm, nv = _adamw(weights[n][0], g2, moms[n][0], vels[n][0], name="adamw_" + n)
        grads[n], delta[n], new_m[n], new_v[n] = (t.reshape(shp) for t in (g2, d, nm, nv))
    sp = lambda src: _pack([src[n] for n in SMALL_NAMES])
    d_s, nm_s, nv_s = _adamw(sp(weights), sp(small_grads), sp(moms), sp(vels), name="adamw_small")
    shapes = [weights[n].shape for n in SMALL_NAMES]
    for tgt, buf in ((delta, d_s), (new_m, nm_s), (new_v, nv_s)):
        tgt.update(zip(SMALL_NAMES, _unpack(buf, shapes)))
    grads.update(small_grads)

    return (loss, grad_x, *[grads[n] for n in WEIGHT_ORDER], *[delta[n] for n in WEIGHT_ORDER],
            *[new_m[n] for n in WEIGHT_ORDER], *[new_v[n] for n in WEIGHT_ORDER])
```

```python
import functools
import math

import numpy as np
import jax
import jax.numpy as jnp
from jax import lax
from jax.experimental import pallas as pl
from jax.experimental.pallas import tpu as pltpu

F32 = jnp.float32
BF16 = jnp.bfloat16
MESH = pl.DeviceIdType.MESH

HEAD_DIM = 64
LANES = 128
ATTN_BLOCK = 128
DILATED_CONFIGS = ((128, 1), (512, 4), (2048, 16))
CONV_KERNEL = 31
FFN_CONV_KERNEL = 3
REL_BUCKETS = 32
REL_MAX_DIST = 2048
DEPTH = 1
ALPHA = (2 * DEPTH) ** 0.25
LN_EPS = 1e-5
NEG_INF = -1e30
QK_SCALE = 1.0 / math.sqrt(HEAD_DIM)
ADAM_LR = 0.001
ADAM_B1 = 0.9
ADAM_B2 = 0.999
ADAM_EPS = 1e-08
ADAM_WD = 0.01
ADAM_STEP = 10
VMEM_LIMIT = 52 * 1024 * 1024
FFN_COLS = 128
N_SHARDS = 4
N_DEV = 8


def _params(n_axes):
    return pltpu.CompilerParams(dimension_semantics=("arbitrary",) * n_axes,
                                vmem_limit_bytes=VMEM_LIMIT)


MM_DIMS = {"nn": (((1,), (0,)), ((), ())), "nt": (((1,), (1,)), ((), ())), "tn": (((0,), (0,)), ((), ()))}


class _Background:
    def __init__(self, in_arrays, out_shapes, aliases, n_sems, run, n_local=1):
        self.in_arrays, self.out_shapes, self.aliases = list(in_arrays), list(out_shapes), dict(aliases)
        self.n_sems, self.n_local, self.run = n_sems, n_local, run

    def scratch(self):
        return [pltpu.SemaphoreType.DMA((self.n_sems,)), pltpu.SemaphoreType.DMA((self.n_sems,)),
                pltpu.SemaphoreType.DMA((self.n_local,))]


def _hosted_call(body, bg, *, grid, in_specs, out_specs, out_shape, scratch_shapes, operands, name):
    n_in, n_out, n_scr = len(in_specs), len(out_specs), len(scratch_shapes)
    if bg is None:
        return pl.pallas_call(lambda *refs: body(refs, lambda post: None), grid=grid, in_specs=in_specs,
                              out_specs=out_specs, out_shape=out_shape, scratch_shapes=scratch_shapes,
                              compiler_params=_params(len(grid)), name=name)(*operands)
    nb_in, nb_out = len(bg.in_arrays), len(bg.out_shapes)
    n_steps = int(np.prod(grid))

    def full_body(*refs):
        own = refs[:n_in] + refs[n_in + nb_in:n_in + nb_in + n_out] \
            + refs[n_in + nb_in + n_out + nb_out:n_in + nb_in + n_out + nb_out + n_scr]
        bg_in = refs[n_in:n_in + nb_in]
        bg_out = refs[n_in + nb_in + n_out:n_in + nb_in + n_out + nb_out]
        sems = refs[n_in + nb_in + n_out + nb_out + n_scr:]
        step = pl.program_id(0)
        for ax in range(1, len(grid)):
            step = step * grid[ax] + pl.program_id(ax)

        def hook(post):
            bg.run(step, n_steps, bg_in, bg_out, *sems, post)

        body(own, hook)

    res = pl.pallas_call(
        full_body, grid=grid, in_specs=list(in_specs) + [HBM_SPEC] * nb_in,
        out_specs=list(out_specs) + [HBM_SPEC] * nb_out, out_shape=list(out_shape) + bg.out_shapes,
        input_output_aliases={n_in + a: n_out + o for a, o in bg.aliases.items()},
        scratch_shapes=list(scratch_shapes) + bg.scratch(), compiler_params=_params(len(grid)), name=name,
    )(*operands, *bg.in_arrays)
    return res


def _matmul_general(ins, part_fn, *, grid, tm, tn, outs, epilogue, extras=(), name, bg=None):
    nk = grid[2]
    n_in, n_extra = len(ins), len(extras)

    def body(refs, bg_hook):
        in_refs = refs[:n_in]
        rest = refs[n_in:]
        extra_refs = rest[:n_extra]
        out_refs = rest[n_extra:n_extra + len(outs)]
        acc_ref = rest[-1]
        i, j, k = pl.program_id(0), pl.program_id(1), pl.program_id(2)
        bg_hook(False)
        part = part_fn(in_refs, i, j, k)
        if nk == 1:
            epilogue(part, i, j, extra_refs, out_refs)
        else:
            @pl.when(k == 0)
            def _():
                acc_ref[...] = part

            @pl.when(k > 0)
            def _():
                acc_ref[...] += part

            @pl.when(k == nk - 1)
            def _():
                epilogue(acc_ref[...], i, j, extra_refs, out_refs)
        bg_hook(True)

    in_specs = [pl.BlockSpec(bs, im) for (_, bs, im) in list(ins) + list(extras)]
    out_specs = [pl.BlockSpec(bs, im) for (_, _, bs, im) in outs]
    out_shape = [jax.ShapeDtypeStruct(s, d) for (s, d, _, _) in outs]
    return _hosted_call(body, bg, grid=grid, in_specs=in_specs, out_specs=out_specs, out_shape=out_shape,
                        scratch_shapes=[pltpu.VMEM((tm, tn), F32)],
                        operands=[e[0] for e in ins] + [e[0] for e in extras], name=name)


def _dot(a, b, mode):
    return lax.dot_general(a.astype(BF16), b.astype(BF16), MM_DIMS[mode], preferred_element_type=F32)


def _matmul(a, b, *, mode, tm, tn, tk, outs, epilogue, extras=(), name, bg=None):
    if mode == "tn":
        K, M = a.shape
        N = b.shape[1]
        ins = [(a, (tk, tm), lambda i, j, k: (k, i)), (b, (tk, tn), lambda i, j, k: (k, j))]
    elif mode == "nt":
        M, K = a.shape
        N = b.shape[0]
        ins = [(a, (tm, tk), lambda i, j, k: (i, k)), (b, (tn, tk), lambda i, j, k: (j, k))]
    else:
        M, K = a.shape
        N = b.shape[1]
        ins = [(a, (tm, tk), lambda i, j, k: (i, k)), (b, (tk, tn), lambda i, j, k: (k, j))]
    assert M % tm == 0 and N % tn == 0 and K % tk == 0, (name, M, N, K, tm, tn, tk)

    def part_fn(in_refs, i, j, k):
        return _dot(in_refs[0][...], in_refs[1][...], mode)

    return _matmul_general(ins, part_fn, grid=(M // tm, N // tn, K // tk), tm=tm, tn=tn, outs=outs,
                           epilogue=epilogue, extras=extras, name=name, bg=bg)


def _plain_out(M, N, tm, tn, dtype):
    return ((M, N), dtype, (tm, tn), lambda i, j, k: (i, j))


def _mm_plain(a, b, *, mode, tm, tn, tk, out_dtype, name, bias=None):
    if mode == "tn":
        M, N = a.shape[1], b.shape[1]
    elif mode == "nt":
        M, N = a.shape[0], b.shape[0]
    else:
        M, N = a.shape[0], b.shape[1]
    extras = []
    if bias is not None:
        extras.append((bias, (1, tn), lambda i, j, k: (0, j)))

    def epilogue(acc, i, j, extra_refs, out_refs):
        if bias is not None:
            acc = acc + extra_refs[0][...]
        out_refs[0][...] = acc.astype(out_dtype)

    return _matmul(a, b, mode=mode, tm=tm, tn=tn, tk=tk, outs=[_plain_out(M, N, tm, tn, out_dtype)],
                   epilogue=epilogue, extras=extras, name=name)[0]


def _row_tile(T, want):
    t = min(T, want)
    while T % t:
        t //= 2
    return t


def _col_tile(N, want):
    if N <= want:
        return N
    best = None
    for c in range(LANES, want + 1, LANES):
        if N % c == 0:
            best = c
    return best if best is not None else N


def _accumulate(ref, first, val):
    @pl.when(first)
    def _():
        ref[...] = val

    @pl.when(jnp.logical_not(first))
    def _():
        ref[...] += val


def _ln_fwd(z, g, b):
    mu = jnp.mean(z, axis=-1, keepdims=True)
    zc = z - mu
    var = jnp.mean(zc * zc, axis=-1, keepdims=True)
    r = lax.rsqrt(var + LN_EPS)
    xh = zc * r
    return xh * g + b, xh, r


def _ln_bwd(dy, xh, r, g):
    dxh = dy * g
    m1 = jnp.mean(dxh, axis=-1, keepdims=True)
    m2 = jnp.mean(dxh * xh, axis=-1, keepdims=True)
    return r * (dxh - m1 - xh * m2)


def _sigmoid(x):
    return 1.0 / (1.0 + jnp.exp(-x))


def _shift_down(x, s, row):
    if s == 0:
        return x
    rolled = pltpu.roll(x, s, 0)
    nfix = -(-s // 8) * 8
    head = jnp.where(row[:nfix] >= s, rolled[:nfix], 0.0)
    return jnp.concatenate([head, rolled[nfix:]], axis=0)


def _shift_up(x, s, row):
    if s == 0:
        return x
    n = x.shape[0]
    rolled = pltpu.roll(x, n - s, 0)
    nfix = -(-s // 8) * 8
    tail = jnp.where(row[n - nfix:] < n - s, rolled[n - nfix:], 0.0)
    return jnp.concatenate([rolled[:n - nfix], tail], axis=0)


def _bucket_tables():
    exact = REL_BUCKETS // 2
    qi = np.arange(ATTN_BLOCK)[:, None]
    kj = np.arange(2 * ATTN_BLOCK)[None, :]
    steps = qi + ATTN_BLOCK - kj
    buckets, masks = [], []
    for window, dilation in DILATED_CONFIGS:
        max_steps = window // dilation
        band = (steps >= 0) & (steps <= max_steps)
        dist = np.maximum(steps, 0) * dilation
        d_f = np.maximum(dist, 1).astype(np.float32)
        large = exact + (np.log(d_f / np.float32(exact)) / np.float32(math.log(REL_MAX_DIST / exact))
                         * np.float32(REL_BUCKETS - exact)).astype(np.int32)
        large = np.minimum(large, REL_BUCKETS - 1)
        bucket = np.where(dist < exact, dist, large).astype(np.int32)
        buckets.append(bucket.reshape(1, -1))
        masks.append(np.where(band, 0.0, NEG_INF).astype(np.float32).reshape(1, -1))
    return np.stack(buckets), np.stack(masks)


def _split_hi_lo(x):
    hi = x.astype(BF16)
    lo = (x - hi.astype(F32)).astype(BF16)
    return hi, lo


def _bias_build(rel_table_t, bucket, mask):
    H = rel_table_t.shape[0]
    n = bucket.shape[-1]

    def body(t_ref, bkt_ref, mask_ref, o_ref):
        onehot = (lax.broadcasted_iota(jnp.int32, (REL_BUCKETS, n), 0) == bkt_ref[0]).astype(BF16)
        t = t_ref[...]
        t1 = t.astype(BF16)
        r1 = t - t1.astype(F32)
        t2 = r1.astype(BF16)
        t3 = (r1 - t2.astype(F32)).astype(BF16)
        acc = jnp.dot(t1, onehot, preferred_element_type=F32)
        acc = acc + jnp.dot(t2, onehot, preferred_element_type=F32)
        acc = acc + jnp.dot(t3, onehot, preferred_element_type=F32)
        o_ref[0] = acc + mask_ref[0]

    return pl.pallas_call(
        body, grid=(3,),
        in_specs=[pl.BlockSpec((H, REL_BUCKETS), lambda b: (0, 0)),
                  pl.BlockSpec((1, 1, n), lambda b: (b, 0, 0)),
                  pl.BlockSpec((1, 1, n), lambda b: (b, 0, 0))],
        out_specs=pl.BlockSpec((1, H, n), lambda b: (b, 0, 0)),
        out_shape=jax.ShapeDtypeStruct((3, H, n), F32),
        compiler_params=_params(1), name="bias_build",
    )(rel_table_t, bucket, mask)


def _rel_grad(dbias, bucket):
    H = dbias.shape[1]
    n = bucket.shape[-1]
    dims = (((1,), (1,)), ((), ()))

    def body(d_ref, bkt_ref, o_ref):
        b = pl.program_id(0)
        onehot = (lax.broadcasted_iota(jnp.int32, (REL_BUCKETS, n), 0) == bkt_ref[0]).astype(BF16)
        d = d_ref[0]
        d1 = d.astype(BF16)
        r1 = d - d1.astype(F32)
        d2 = r1.astype(BF16)
        d3 = (r1 - d2.astype(F32)).astype(BF16)
        acc = lax.dot_general(d1, onehot, dims, preferred_element_type=F32)
        acc = acc + lax.dot_general(d2, onehot, dims, preferred_element_type=F32)
        acc = acc + lax.dot_general(d3, onehot, dims, preferred_element_type=F32)
        _accumulate(o_ref, b == 0, acc)

    return pl.pallas_call(
        body, grid=(3,),
        in_specs=[pl.BlockSpec((1, H, n), lambda b: (b, 0, 0)),
                  pl.BlockSpec((1, 1, n), lambda b: (b, 0, 0))],
        out_specs=pl.BlockSpec((H, REL_BUCKETS), lambda b: (0, 0)),
        out_shape=jax.ShapeDtypeStruct((H, REL_BUCKETS), F32),
        compiler_params=_params(1), name="rel_grad",
    )(dbias, bucket)


def _attn_specs(B, S, AW, d):
    L = S // d
    HP = AW // LANES
    W3 = 3 * HP
    q_spec = pl.BlockSpec((1, L, LANES), lambda h, b, r: (b, 0, r * W3 + h))
    k_spec = pl.BlockSpec((1, L, LANES), lambda h, b, r: (b, 0, r * W3 + HP + h))
    v_spec = pl.BlockSpec((1, L, LANES), lambda h, b, r: (b, 0, r * W3 + 2 * HP + h))
    o_spec = pl.BlockSpec((1, L, LANES), lambda h, b, r: (b, 0, r * HP + h))
    bias_spec = pl.BlockSpec((2, ATTN_BLOCK, 2 * ATTN_BLOCK), lambda h, b, r: (h, 0, 0))
    return L, HP, q_spec, k_spec, v_spec, o_spec, bias_spec


def _attn_fwd(qkv, bias, B, S, AW, d, name):
    L, HP, q_spec, k_spec, v_spec, o_spec, bias_spec = _attn_specs(B, S, AW, d)
    nb = L // ATTN_BLOCK
    nt = (((1,), (1,)), ((), ()))

    def body(q_ref, k_ref, v_ref, b_ref, o_ref, lse_ref):
        head0 = lax.broadcasted_iota(jnp.int32, (1, LANES), 1) < HEAD_DIM

        def block(n, first):
            qs = pl.multiple_of(n * ATTN_BLOCK, ATTN_BLOCK)
            q = q_ref[0, pl.ds(qs, ATTN_BLOCK), :]
            if first:
                kk = k_ref[0, pl.ds(0, ATTN_BLOCK), :]
                vv = v_ref[0, pl.ds(0, ATTN_BLOCK), :]
            else:
                ks = pl.multiple_of(n * ATTN_BLOCK - ATTN_BLOCK, ATTN_BLOCK)
                kk = k_ref[0, pl.ds(ks, 2 * ATTN_BLOCK), :]
                vv = v_ref[0, pl.ds(ks, 2 * ATTN_BLOCK), :]
            outs, lses = [], []
            for e in range(2):
                msk = head0 if e == 0 else jnp.logical_not(head0)
                qe = jnp.where(msk, q, jnp.zeros_like(q))
                s = lax.dot_general(qe, kk, nt, preferred_element_type=F32) * QK_SCALE
                s = s + (b_ref[e, :, ATTN_BLOCK:] if first else b_ref[e])
                m = jnp.max(s, axis=-1, keepdims=True)
                p = jnp.exp(s - m)
                l = jnp.sum(p, axis=-1, keepdims=True)
                o = jnp.dot(p.astype(BF16), vv, preferred_element_type=F32)
                outs.append(o / l)
                lses.append(jnp.broadcast_to(m + jnp.log(l), (ATTN_BLOCK, LANES)))
            o_ref[0, pl.ds(qs, ATTN_BLOCK), :] = jnp.where(head0, outs[0], outs[1])
            lse_ref[0, pl.ds(qs, ATTN_BLOCK), :] = jnp.where(head0, lses[0], lses[1])

        block(0, True)
        if nb > 1:
            def loop(n, c):
                block(n, False)
                return c
            lax.fori_loop(1, nb, loop, 0)

    qv = qkv.reshape(B, L, d * 3 * AW)
    o, lse = pl.pallas_call(
        body, grid=(HP, B, d), in_specs=[q_spec, k_spec, v_spec, bias_spec],
        out_specs=[o_spec, o_spec],
        out_shape=[jax.ShapeDtypeStruct((B, L, d * AW), F32)] * 2,
        compiler_params=_params(3), name=name,
    )(qv, qv, qv, bias)
    return o.reshape(B * S, AW), lse.reshape(B * S, AW)


def _attn_bwd(qkv, do, lse, dd, bias, B, S, AW, d, name):
    L, HP, q_spec, k_spec, v_spec, o_spec, bias_spec = _attn_specs(B, S, AW, d)
    nb = L // ATTN_BLOCK
    nt = (((1,), (1,)), ((), ()))
    tn = (((0,), (0,)), ((), ()))

    def body(q_ref, k_ref, v_ref, do_ref, lse_ref, dd_ref, b_ref, dq_ref, dk_ref, dv_ref, db_ref):
        head0 = lax.broadcasted_iota(jnp.int32, (1, LANES), 1) < HEAD_DIM
        first_step = jnp.logical_and(pl.program_id(1) == 0, pl.program_id(2) == 0)

        @pl.when(first_step)
        def _():
            db_ref[...] = jnp.zeros_like(db_ref)

        dk_ref[...] = jnp.zeros_like(dk_ref)
        dv_ref[...] = jnp.zeros_like(dv_ref)

        def block(n, first):
            qs = pl.multiple_of(n * ATTN_BLOCK, ATTN_BLOCK)
            nkeys = ATTN_BLOCK if first else 2 * ATTN_BLOCK
            ks = 0 if first else pl.multiple_of(n * ATTN_BLOCK - ATTN_BLOCK, ATTN_BLOCK)
            q = q_ref[0, pl.ds(qs, ATTN_BLOCK), :]
            kk = k_ref[0, pl.ds(ks, nkeys), :]
            vv = v_ref[0, pl.ds(ks, nkeys), :]
            dout = do_ref[0, pl.ds(qs, ATTN_BLOCK), :]
            lse_b = lse_ref[0, pl.ds(qs, ATTN_BLOCK), :]
            dd_b = dd_ref[0, pl.ds(qs, ATTN_BLOCK), :]
            dq = jnp.zeros((ATTN_BLOCK, LANES), F32)
            dkk = jnp.zeros((nkeys, LANES), F32)
            dvv = jnp.zeros((nkeys, LANES), F32)
            for e in range(2):
                msk = head0 if e == 0 else jnp.logical_not(head0)
                c0 = e * HEAD_DIM
                qe = jnp.where(msk, q, jnp.zeros_like(q))
                doe = jnp.where(msk, dout, jnp.zeros_like(dout))
                kke = jnp.where(msk, kk, jnp.zeros_like(kk))
                s = lax.dot_general(qe, kk, nt, preferred_element_type=F32) * QK_SCALE
                s = s + (b_ref[e, :, ATTN_BLOCK:] if first else b_ref[e])
                p = jnp.exp(s - lse_b[:, c0:c0 + 1])
                dp = lax.dot_general(doe, vv, nt, preferred_element_type=F32)
                ds = p * (dp - dd_b[:, c0:c0 + 1])
                if first:
                    db_ref[e, :, ATTN_BLOCK:] += ds
                else:
                    db_ref[e] += ds
                dsb = (ds * QK_SCALE).astype(BF16)
                dq = dq + jnp.dot(dsb, kke, preferred_element_type=F32)
                dkk = dkk + lax.dot_general(dsb, qe, tn, preferred_element_type=F32)
                dvv = dvv + lax.dot_general(p.astype(BF16), doe, tn, preferred_element_type=F32)
            dq_ref[0, pl.ds(qs, ATTN_BLOCK), :] = dq
            dk_ref[0, pl.ds(ks, nkeys), :] += dkk
            dv_ref[0, pl.ds(ks, nkeys), :] += dvv

        block(0, True)
        if nb > 1:
            def loop(n, c):
                block(n, False)
                return c
            lax.fori_loop(1, nb, loop, 0)

    H = AW // HEAD_DIM
    qv = qkv.reshape(B, L, d * 3 * AW)
    view = lambda t: t.reshape(B, L, d * AW)
    dq, dk, dv, db = pl.pallas_call(
        body, grid=(HP, B, d),
        in_specs=[q_spec, k_spec, v_spec, o_spec, o_spec, o_spec, bias_spec],
        out_specs=[o_spec, o_spec, o_spec, bias_spec],
        out_shape=[jax.ShapeDtypeStruct((B, L, d * AW), F32)] * 3
        + [jax.ShapeDtypeStruct((H, ATTN_BLOCK, 2 * ATTN_BLOCK), F32)],
        compiler_params=_params(3), name=name,
    )(qv, qv, qv, view(do), view(lse), view(dd), bias)
    flat = lambda t: t.reshape(B * S, AW)
    return flat(dq), flat(dk), flat(dv), db


def _attn_combine(ons, lses, gain, tm):
    T, AW = ons[0].shape

    def body(o1, o2, o3, l1, l2, l3, g_ref, attn_ref, lse_ref, mix_ref, r_ref):
        la, lb, lc = l1[...], l2[...], l3[...]
        m = jnp.maximum(jnp.maximum(la, lb), lc)
        ea, eb, ec = jnp.exp(la - m), jnp.exp(lb - m), jnp.exp(lc - m)
        den = ea + eb + ec
        attn = (ea * o1[...] + eb * o2[...] + ec * o3[...]) / den
        attn_ref[...] = attn
        lse_ref[...] = m + jnp.log(den)
        r = lax.rsqrt(jnp.mean(attn * attn, axis=-1, keepdims=True) + LN_EPS)
        mix_ref[...] = (attn * r * g_ref[...]).astype(BF16)
        r_ref[...] = jnp.broadcast_to(r, (tm, LANES))

    row = pl.BlockSpec((tm, AW), lambda i: (i, 0))
    return pl.pallas_call(
        body, grid=(T // tm,),
        in_specs=[row] * 6 + [pl.BlockSpec((1, AW), lambda i: (0, 0))],
        out_specs=[row, row, row, pl.BlockSpec((tm, LANES), lambda i: (i, 0))],
        out_shape=[jax.ShapeDtypeStruct((T, AW), F32), jax.ShapeDtypeStruct((T, AW), F32),
                   jax.ShapeDtypeStruct((T, AW), BF16), jax.ShapeDtypeStruct((T, LANES), F32)],
        compiler_params=_params(1), name="attn_combine",
    )(*ons, *lses, gain)


def _to_sub(src_ref, stage_ref, dsts, S):
    stage_ref[...] = src_ref[0].astype(F32)
    for (_, d), dst in zip(DILATED_CONFIGS[1:], dsts):
        L = S // d
        for r in range(d):
            dst[r * L:(r + 1) * L, :] = stage_ref[pl.ds(r, L, stride=d), :].astype(dst.dtype)


def _branch_blocks(S, d, block):
    nb = S // d // ATTN_BLOCK
    inner_unroll = 3 if (nb - 1) % 3 == 0 else 1

    def per_residue(r, c):
        block(r * nb, True)
        if nb > 1:
            def inner(n, c2):
                block(r * nb + n, False)
                return c2
            lax.fori_loop(1, nb, inner, 0, unroll=inner_unroll)
        return c

    lax.fori_loop(0, d, per_residue, 0, unroll=4 if nb == 1 else 1)


def _attention_fwd(qkv, bias_all, B, S, AW):
    HP = AW // LANES
    nt = MM_DIMS["nt"]

    def body(q_ref, k_ref, v_ref, b_ref, o_ref, lse_ref, stage, q4, q16, k4, k16, v4, v16, o1, l1, o4, l4, o16, l16):
        head0 = lax.broadcasted_iota(jnp.int32, (1, LANES), 1) < HEAD_DIM
        _to_sub(q_ref, stage, (q4, q16), S)
        _to_sub(k_ref, stage, (k4, k16), S)
        _to_sub(v_ref, stage, (v4, v16), S)
        srcs = ((q_ref.at[0], k_ref.at[0], v_ref.at[0], o1, l1), (q4, k4, v4, o4, l4), (q16, k16, v16, o16, l16))
        for bi, (_, d) in enumerate(DILATED_CONFIGS):
            qs_ref, ks_ref, vs_ref, od_ref, ld_ref = srcs[bi]

            def block(g, first, bi=bi, qs_ref=qs_ref, ks_ref=ks_ref, vs_ref=vs_ref, od_ref=od_ref, ld_ref=ld_ref):
                qs = pl.multiple_of(g * ATTN_BLOCK, ATTN_BLOCK)
                nkeys = ATTN_BLOCK if first else 2 * ATTN_BLOCK
                ks = qs if first else pl.multiple_of(qs - ATTN_BLOCK, ATTN_BLOCK)
                q = qs_ref[pl.ds(qs, ATTN_BLOCK), :]
                kk = ks_ref[pl.ds(ks, nkeys), :]
                vv = vs_ref[pl.ds(ks, nkeys), :]
                outs, lses = [], []
                for e in range(2):
                    msk = head0 if e == 0 else jnp.logical_not(head0)
                    qe = jnp.where(msk, q * QK_SCALE, jnp.zeros_like(q))
                    s = lax.dot_general(qe, kk, nt, preferred_element_type=F32)
                    s = s + (b_ref[bi, e, :, ATTN_BLOCK:] if first else b_ref[bi, e])
                    m = jnp.max(s, axis=-1, keepdims=True)
                    p = jnp.exp(s - m)
                    l = jnp.sum(p, axis=-1, keepdims=True)
                    o = jnp.dot(p.astype(BF16), vv, preferred_element_type=F32)
                    outs.append(o / l)
                    lses.append(jnp.broadcast_to(m + jnp.log(l), (ATTN_BLOCK, LANES)))
                od_ref[pl.ds(qs, ATTN_BLOCK), :] = jnp.where(head0, outs[0], outs[1])
                ld_ref[pl.ds(qs, ATTN_BLOCK), :] = jnp.where(head0, lses[0], lses[1])

            _branch_blocks(S, d, block)

        def natural(sub_ref, d):
            L = S // d
            for r in range(d):
                stage[pl.ds(r, L, stride=d), :] = sub_ref[r * L:(r + 1) * L, :]
            return stage[...]

        la = l1[...]
        lb = natural(l4, 4)
        lc = natural(l16, 16)
        m = jnp.maximum(jnp.maximum(la, lb), lc)
        ea, eb, ec = jnp.exp(la - m), jnp.exp(lb - m), jnp.exp(lc - m)
        den = ea + eb + ec
        lse_ref[0] = m + jnp.log(den)
        acc = ea * o1[...]
        acc = acc + eb * natural(o4, 4)
        acc = acc + ec * natural(o16, 16)
        o_ref[0] = acc / den

    blk = lambda off: pl.BlockSpec((1, S, LANES), lambda b, h: (b, 0, off + h))
    qv = qkv.reshape(B, S, 3 * AW)
    sub_b = pltpu.VMEM((S, LANES), BF16)
    sub_f = pltpu.VMEM((S, LANES), F32)
    o, lse = pl.pallas_call(
        body, grid=(B, HP),
        in_specs=[blk(0), blk(HP), blk(2 * HP),
                  pl.BlockSpec((3, 2, ATTN_BLOCK, 2 * ATTN_BLOCK), lambda b, h: (0, h, 0, 0))],
        out_specs=[blk(0), blk(0)],
        out_shape=[jax.ShapeDtypeStruct((B, S, AW), F32)] * 2,
        scratch_shapes=[sub_f] + [sub_b] * 6 + [sub_f] * 6,
        compiler_params=_params(2), name="attention_fwd",
    )(qv, qv, qv, bias_all)
    return o.reshape(B * S, AW), lse.reshape(B * S, AW)


def _attention_bwd(qkv, do, lse, dd, bias_all, B, S, AW):
    HP = AW // LANES
    H = AW // HEAD_DIM
    nt, tn = MM_DIMS["nt"], MM_DIMS["tn"]

    def body(q_ref, k_ref, v_ref, do_ref, lse_ref, dd_ref, b_ref,
             dq_ref, dk_ref, dv_ref, csq_ref, csk_ref, csv_ref, db_ref,
             stage, q4, q16, k4, k16, v4, v16, g4, g16, l4, l16, d4, d16,
             aq1, ak1, av1, aq4, ak4, av4, aq16, ak16, av16):
        head0 = lax.broadcasted_iota(jnp.int32, (1, LANES), 1) < HEAD_DIM
        first_b = pl.program_id(1) == 0

        @pl.when(first_b)
        def _():
            db_ref[...] = jnp.zeros_like(db_ref)

        _to_sub(q_ref, stage, (q4, q16), S)
        _to_sub(k_ref, stage, (k4, k16), S)
        _to_sub(v_ref, stage, (v4, v16), S)
        _to_sub(do_ref, stage, (g4, g16), S)
        _to_sub(lse_ref, stage, (l4, l16), S)
        _to_sub(dd_ref, stage, (d4, d16), S)
        for acc in (ak1, av1, ak4, av4, ak16, av16):
            acc[...] = jnp.zeros_like(acc)
        srcs = ((q_ref.at[0], k_ref.at[0], v_ref.at[0], do_ref.at[0], lse_ref.at[0], dd_ref.at[0], aq1, ak1, av1),
                (q4, k4, v4, g4, l4, d4, aq4, ak4, av4), (q16, k16, v16, g16, l16, d16, aq16, ak16, av16))
        for bi, (_, d) in enumerate(DILATED_CONFIGS):
            def block(g, first, bi=bi, refs=srcs[bi]):
                qs_ref, ks_ref, vs_ref, gs_ref, ls_ref, ds_ref, aq, ak, av = refs
                qs = pl.multiple_of(g * ATTN_BLOCK, ATTN_BLOCK)
                nkeys = ATTN_BLOCK if first else 2 * ATTN_BLOCK
                ks = qs if first else pl.multiple_of(qs - ATTN_BLOCK, ATTN_BLOCK)
                q = qs_ref[pl.ds(qs, ATTN_BLOCK), :]
                kk = ks_ref[pl.ds(ks, nkeys), :]
                vv = vs_ref[pl.ds(ks, nkeys), :]
                dout = gs_ref[pl.ds(qs, ATTN_BLOCK), :]
                lse_b = ls_ref[pl.ds(qs, ATTN_BLOCK), :]
                dd_b = ds_ref[pl.ds(qs, ATTN_BLOCK), :]
                dq = jnp.zeros((ATTN_BLOCK, LANES), F32)
                dkk = jnp.zeros((nkeys, LANES), F32)
                dvv = jnp.zeros((nkeys, LANES), F32)
                for e in range(2):
                    msk = head0 if e == 0 else jnp.logical_not(head0)
                    c0 = e * HEAD_DIM
                    qe = jnp.where(msk, q * QK_SCALE, jnp.zeros_like(q))
                    doe = jnp.where(msk, dout, jnp.zeros_like(dout))
                    kke = jnp.where(msk, kk * QK_SCALE, jnp.zeros_like(kk))
                    s = lax.dot_general(qe, kk, nt, preferred_element_type=F32)
                    s = s + (b_ref[bi, e, :, ATTN_BLOCK:] if first else b_ref[bi, e])
                    p = jnp.exp(s - lse_b[:, c0:c0 + 1])
                    dp = lax.dot_general(doe, vv, nt, preferred_element_type=F32)
                    ds = p * (dp - dd_b[:, c0:c0 + 1])
                    if first:
                        db_ref[bi, e, :, ATTN_BLOCK:] += ds
                    else:
                        db_ref[bi, e] += ds
                    dsb = ds.astype(BF16)
                    dq = dq + jnp.dot(dsb, kke, preferred_element_type=F32)
                    dkk = dkk + lax.dot_general(dsb, qe, tn, preferred_element_type=F32)
                    dvv = dvv + lax.dot_general(p.astype(BF16), doe, tn, preferred_element_type=F32)
                aq[pl.ds(qs, ATTN_BLOCK), :] = dq
                ak[pl.ds(ks, nkeys), :] += dkk
                av[pl.ds(ks, nkeys), :] += dvv

            _branch_blocks(S, d, block)

        for a1, a4, a16, out_ref, cs_ref in ((aq1, aq4, aq16, dq_ref, csq_ref), (ak1, ak4, ak16, dk_ref, csk_ref),
                                             (av1, av4, av16, dv_ref, csv_ref)):
            stage[...] = a1[...]
            for d, sub in ((4, a4), (16, a16)):
                L = S // d
                for r in range(d):
                    stage[pl.ds(r, L, stride=d), :] += sub[r * L:(r + 1) * L, :]
            tot = stage[...]
            out_ref[0] = tot.astype(out_ref.dtype)
            _accumulate(cs_ref, first_b, jnp.sum(tot, axis=0, keepdims=True))

    blk = lambda off: pl.BlockSpec((1, S, LANES), lambda h, b: (b, 0, off + h))
    cs_spec = pl.BlockSpec((1, LANES), lambda h, b: (0, h))
    bias_spec = pl.BlockSpec((3, 2, ATTN_BLOCK, 2 * ATTN_BLOCK), lambda h, b: (0, h, 0, 0))
    qv = qkv.reshape(B, S, 3 * AW)
    view = lambda t: t.reshape(B, S, AW)
    sub_b = pltpu.VMEM((S, LANES), BF16)
    sub_f = pltpu.VMEM((S, LANES), F32)
    res = pl.pallas_call(
        body, grid=(HP, B),
        in_specs=[blk(0), blk(HP), blk(2 * HP), blk(0), blk(0), blk(0), bias_spec],
        out_specs=[blk(0), blk(0), blk(0), cs_spec, cs_spec, cs_spec, bias_spec],
        out_shape=[jax.ShapeDtypeStruct((B, S, AW), BF16)] * 3 + [jax.ShapeDtypeStruct((1, AW), F32)] * 3
        + [jax.ShapeDtypeStruct((3, H, ATTN_BLOCK, 2 * ATTN_BLOCK), F32)],
        scratch_shapes=[sub_f] + [sub_b] * 8 + [sub_f] * 4 + [sub_f] * 9,
        compiler_params=_params(2), name="attention_bwd",
    )(qv, qv, qv, view(do), view(lse), view(dd), bias_all)
    flat = lambda t: t.reshape(B * S, AW)
    return flat(res[0]), flat(res[1]), flat(res[2]), res[3], res[4], res[5], res[6]


def _regroup(src, stage, dst, d, S, off=0):
    if d == 1:
        dst[off:off + S, :] = src.astype(dst.dtype)
        return
    stage[...] = src.astype(F32)
    L = S // d
    for r in range(d):
        dst[off + r * L:off + (r + 1) * L, :] = stage[pl.ds(r, L, stride=d), :].astype(dst.dtype)


def _ungroup(sub_ref, off, nat_ref, d, S, add):
    L = S // d
    for r in range(d):
        rows = pl.ds(0, S) if d == 1 else pl.ds(r, L, stride=d)
        val = sub_ref[off + r * L:off + (r + 1) * L, :]
        if add:
            nat_ref[rows, :] += val
        else:
            nat_ref[rows, :] = val


def _branch_scores(qe, kc3, kp3, b_ref, bi, e, first3):
    s_cur = jnp.einsum("gqe,gke->gqk", qe, kc3, preferred_element_type=F32) + b_ref[bi, e, :, ATTN_BLOCK:]
    if kp3 is None:
        return s_cur, None
    s_prev = jnp.einsum("gqe,gke->gqk", qe, kp3, preferred_element_type=F32) + b_ref[bi, e, :, :ATTN_BLOCK]
    return s_cur, jnp.where(first3, NEG_INF, s_prev)


def _attention_fwd(qkv, bias_all, B, S, AW, bg=None):
    HP = AW // LANES
    G = S // ATTN_BLOCK
    blk3 = (G, ATTN_BLOCK, LANES)

    def body(refs, bg_hook):
        q_ref, k_ref, v_ref, b_ref, o_ref, lse_ref, stage, qs, ks, vs, ot, lt, on0, on1, on2, ln0, ln1, ln2 = refs
        bg_hook(False)
        head0 = lax.broadcasted_iota(jnp.int32, (1, 1, LANES), 2) < HEAD_DIM
        g_idx = lax.broadcasted_iota(jnp.int32, (G, 1, 1), 0)
        ks[0:ATTN_BLOCK, :] = jnp.zeros((ATTN_BLOCK, LANES), BF16)
        vs[0:ATTN_BLOCK, :] = jnp.zeros((ATTN_BLOCK, LANES), BF16)
        nat_o, nat_l = (on0, on1, on2), (ln0, ln1, ln2)
        for bi, (_, d) in enumerate(DILATED_CONFIGS):
            nb = S // d // ATTN_BLOCK
            _regroup(q_ref[0], stage, qs, d, S)
            _regroup(k_ref[0], stage, ks, d, S, ATTN_BLOCK)
            _regroup(v_ref[0], stage, vs, d, S, ATTN_BLOCK)
            q3 = qs[...].reshape(blk3) * QK_SCALE
            kc3 = ks[ATTN_BLOCK:ATTN_BLOCK + S, :].reshape(blk3)
            vc3 = vs[ATTN_BLOCK:ATTN_BLOCK + S, :].reshape(blk3)
            kp3 = vp3 = first3 = None
            if nb > 1:
                kp3 = ks[0:S, :].reshape(blk3)
                vp3 = vs[0:S, :].reshape(blk3)
                first3 = (g_idx & (nb - 1)) == 0
            outs, lses = [], []
            for e in range(2):
                msk = head0 if e == 0 else jnp.logical_not(head0)
                qe = jnp.where(msk, q3, jnp.zeros_like(q3))
                s_cur, s_prev = _branch_scores(qe, kc3, kp3, b_ref, bi, e, first3)
                m = jnp.max(s_cur, axis=-1, keepdims=True)
                if s_prev is not None:
                    m = jnp.maximum(m, jnp.max(s_prev, axis=-1, keepdims=True))
                p = jnp.exp(s_cur - m)
                l = jnp.sum(p, axis=-1, keepdims=True)
                o = jnp.einsum("gqk,gke->gqe", p.astype(BF16), vc3, preferred_element_type=F32)
                if s_prev is not None:
                    p = jnp.exp(s_prev - m)
                    l = l + jnp.sum(p, axis=-1, keepdims=True)
                    o = o + jnp.einsum("gqk,gke->gqe", p.astype(BF16), vp3, preferred_element_type=F32)
                outs.append(o / l)
                lses.append(jnp.broadcast_to(m + jnp.log(l), blk3))
            ot[...] = jnp.where(head0, outs[0], outs[1]).reshape(S, LANES)
            lt[...] = jnp.where(head0, lses[0], lses[1]).reshape(S, LANES)
            _ungroup(ot, 0, nat_o[bi], d, S, add=False)
            _ungroup(lt, 0, nat_l[bi], d, S, add=False)

        la, lb, lc = ln0[...], ln1[...], ln2[...]
        m = jnp.maximum(jnp.maximum(la, lb), lc)
        ea, eb, ec = jnp.exp(la - m), jnp.exp(lb - m), jnp.exp(lc - m)
        den = ea + eb + ec
        lse_ref[0] = m + jnp.log(den)
        o_ref[0] = (ea * on0[...] + eb * on1[...] + ec * on2[...]) / den
        bg_hook(True)

    blk = lambda off: pl.BlockSpec((1, S, LANES), lambda b, h: (b, 0, off + h))
    qv = qkv.reshape(B, S, 3 * AW)
    sub_f = pltpu.VMEM((S, LANES), F32)
    pad_b = pltpu.VMEM((S + ATTN_BLOCK, LANES), BF16)
    res = _hosted_call(
        body, bg, grid=(B, HP),
        in_specs=[blk(0), blk(HP), blk(2 * HP),
                  pl.BlockSpec((3, 2, ATTN_BLOCK, 2 * ATTN_BLOCK), lambda b, h: (0, h, 0, 0))],
        out_specs=[blk(0), blk(0)],
        out_shape=[jax.ShapeDtypeStruct((B, S, AW), F32)] * 2,
        scratch_shapes=[sub_f, pltpu.VMEM((S, LANES), BF16), pad_b, pad_b] + [sub_f] * 8,
        operands=[qv, qv, qv, bias_all], name="attention_fwd")
    return (res[0].reshape(B * S, AW), res[1].reshape(B * S, AW)) + tuple(res[2:])


def _attention_bwd(qkv, do, lse, dd, bias_all, B, S, AW, bg=None):
    HP = AW // LANES
    H = AW // HEAD_DIM
    G = S // ATTN_BLOCK
    blk3 = (G, ATTN_BLOCK, LANES)
    PAD = ATTN_BLOCK

    def body(refs, bg_hook):
        (q_ref, k_ref, v_ref, do_ref, lse_ref, dd_ref, b_ref,
         dq_ref, dk_ref, dv_ref, csq_ref, csk_ref, csv_ref, db_ref,
         stage, qs, ks, vs, gs, ls, ds_, tq, tk, tv, accq, acck, accv) = refs
        bg_hook(False)
        head0 = lax.broadcasted_iota(jnp.int32, (1, 1, LANES), 2) < HEAD_DIM
        g_idx = lax.broadcasted_iota(jnp.int32, (G, 1, 1), 0)
        first_b = pl.program_id(1) == 0

        @pl.when(first_b)
        def _():
            db_ref[...] = jnp.zeros_like(db_ref)

        ks[0:PAD, :] = jnp.zeros((PAD, LANES), BF16)
        vs[0:PAD, :] = jnp.zeros((PAD, LANES), BF16)
        tk[0:PAD, :] = jnp.zeros((PAD, LANES), F32)
        tv[0:PAD, :] = jnp.zeros((PAD, LANES), F32)
        for bi, (_, d) in enumerate(DILATED_CONFIGS):
            nb = S // d // ATTN_BLOCK
            _regroup(q_ref[0], stage, qs, d, S)
            _regroup(k_ref[0], stage, ks, d, S, PAD)
            _regroup(v_ref[0], stage, vs, d, S, PAD)
            _regroup(do_ref[0], stage, gs, d, S)
            _regroup(lse_ref[0], stage, ls, d, S)
            _regroup(dd_ref[0], stage, ds_, d, S)
            q3 = qs[...].reshape(blk3) * QK_SCALE
            do3 = gs[...].reshape(blk3)
            lse3 = ls[...].reshape(blk3)
            dd3 = ds_[...].reshape(blk3)
            kc3 = ks[PAD:PAD + S, :].reshape(blk3)
            vc3 = vs[PAD:PAD + S, :].reshape(blk3)
            kp3 = vp3 = first3 = None
            if nb > 1:
                kp3 = ks[0:S, :].reshape(blk3)
                vp3 = vs[0:S, :].reshape(blk3)
                first3 = (g_idx & (nb - 1)) == 0
            dq = jnp.zeros(blk3, F32)
            dkc = jnp.zeros(blk3, F32)
            dvc = jnp.zeros(blk3, F32)
            dkp = jnp.zeros(blk3, F32)
            dvp = jnp.zeros(blk3, F32)
            for e in range(2):
                msk = head0 if e == 0 else jnp.logical_not(head0)
                c0 = e * HEAD_DIM
                qe = jnp.where(msk, q3, jnp.zeros_like(q3))
                doe = jnp.where(msk, do3, jnp.zeros_like(do3))
                lse_e = lse3[:, :, c0:c0 + 1]
                dd_e = dd3[:, :, c0:c0 + 1]
                s_cur, s_prev = _branch_scores(qe, kc3, kp3, b_ref, bi, e, first3)
                for s, k3, v3, cur in ((s_cur, kc3, vc3, True), (s_prev, kp3, vp3, False)):
                    if s is None:
                        continue
                    p = jnp.exp(s - lse_e)
                    dp = jnp.einsum("gqe,gke->gqk", doe, v3, preferred_element_type=F32)
                    dsc = p * (dp - dd_e)
                    if cur:
                        db_ref[bi, e, :, ATTN_BLOCK:] += jnp.sum(dsc, axis=0)
                    else:
                        db_ref[bi, e, :, :ATTN_BLOCK] += jnp.sum(dsc, axis=0)
                    dsb = dsc.astype(BF16)
                    ke = jnp.where(msk, k3 * QK_SCALE, jnp.zeros_like(k3))
                    dq = dq + jnp.einsum("gqk,gke->gqe", dsb, ke, preferred_element_type=F32)
                    dk_e = jnp.einsum("gqk,gqe->gke", dsb, qe, preferred_element_type=F32)
                    dv_e = jnp.einsum("gqk,gqe->gke", p.astype(BF16), doe, preferred_element_type=F32)
                    if cur:
                        dkc, dvc = dkc + dk_e, dvc + dv_e
                    else:
                        dkp, dvp = dkp + dk_e, dvp + dv_e
            tq[...] = dq.reshape(S, LANES)
            tk[PAD:PAD + S, :] = dkc.reshape(S, LANES)
            tv[PAD:PAD + S, :] = dvc.reshape(S, LANES)
            if nb > 1:
                tk[0:S, :] += dkp.reshape(S, LANES)
                tv[0:S, :] += dvp.reshape(S, LANES)
            _ungroup(tq, 0, accq, d, S, add=bi > 0)
            _ungroup(tk, PAD, acck, d, S, add=bi > 0)
            _ungroup(tv, PAD, accv, d, S, add=bi > 0)

        for acc, out_ref, cs_ref in ((accq, dq_ref, csq_ref), (acck, dk_ref, csk_ref), (accv, dv_ref, csv_ref)):
            tot = acc[...]
            out_ref[0] = tot.astype(out_ref.dtype)
            _accumulate(cs_ref, first_b, jnp.sum(tot, axis=0, keepdims=True))
        bg_hook(True)

    blk = lambda off: pl.BlockSpec((1, S, LANES), lambda h, b: (b, 0, off + h))
    cs_spec = pl.BlockSpec((1, LANES), lambda h, b: (0, h))
    bias_spec = pl.BlockSpec((3, 2, ATTN_BLOCK, 2 * ATTN_BLOCK), lambda h, b: (0, h, 0, 0))
    qv = qkv.reshape(B, S, 3 * AW)
    view = lambda t: t.reshape(B, S, AW)
    sub_b = pltpu.VMEM((S, LANES), BF16)
    sub_f = pltpu.VMEM((S, LANES), F32)
    pad_b = pltpu.VMEM((S + PAD, LANES), BF16)
    pad_f = pltpu.VMEM((S + PAD, LANES), F32)
    res = _hosted_call(
        body, bg, grid=(HP, B),
        in_specs=[blk(0), blk(HP), blk(2 * HP), blk(0), blk(0), blk(0), bias_spec],
        out_specs=[blk(0), blk(0), blk(0), cs_spec, cs_spec, cs_spec, bias_spec],
        out_shape=[jax.ShapeDtypeStruct((B, S, AW), BF16)] * 3 + [jax.ShapeDtypeStruct((1, AW), F32)] * 3
        + [jax.ShapeDtypeStruct((3, H, ATTN_BLOCK, 2 * ATTN_BLOCK), F32)],
        scratch_shapes=[sub_f, sub_b, pad_b, pad_b, sub_b, sub_f, sub_f, sub_f, pad_f, pad_f, sub_f, sub_f, sub_f],
        operands=[qv, qv, qv, view(do), view(lse), view(dd), bias_all], name="attention_bwd")
    flat = lambda t: t.reshape(B * S, AW)
    return (flat(res[0]), flat(res[1]), flat(res[2]), res[3], res[4], res[5], res[6]) + tuple(res[7:])


def _attn_norm(attn, gain, tm):
    T, AW = attn.shape

    def body(a_ref, g_ref, mix_ref, r_ref):
        a = a_ref[...]
        r = lax.rsqrt(jnp.mean(a * a, axis=-1, keepdims=True) + LN_EPS)
        mix_ref[...] = (a * r * g_ref[...]).astype(BF16)
        r_ref[...] = jnp.broadcast_to(r, (tm, LANES))

    row = pl.BlockSpec((tm, AW), lambda i: (i, 0))
    return pl.pallas_call(
        body, grid=(T // tm,), in_specs=[row, pl.BlockSpec((1, AW), lambda i: (0, 0))],
        out_specs=[row, pl.BlockSpec((tm, LANES), lambda i: (i, 0))],
        out_shape=[jax.ShapeDtypeStruct((T, AW), BF16), jax.ShapeDtypeStruct((T, LANES), F32)],
        compiler_params=_params(1), name="attn_norm",
    )(attn, gain)


def _attn_pre_bwd(dmixed, attn, rstd, gain, tm):
    T, AW = attn.shape
    ones_np = np.kron(np.eye(AW // HEAD_DIM, dtype=np.float32), np.ones((HEAD_DIM, HEAD_DIM), np.float32))
    ones_bd = jnp.asarray(ones_np, dtype=BF16)

    def body(dm_ref, a_ref, r_ref, g_ref, ones_ref, do_ref, dd_ref, dg_ref):
        i = pl.program_id(0)
        dm = dm_ref[...]
        a = a_ref[...]
        r = r_ref[:, 0:1]
        dxn = dm * g_ref[...]
        da = r * (dxn - a * (r * r) * jnp.mean(dxn * a, axis=-1, keepdims=True))
        do_ref[...] = da.astype(BF16)
        hi, lo = _split_hi_lo(da * a)
        dd_ref[...] = (jnp.dot(hi, ones_ref[...], preferred_element_type=F32)
                       + jnp.dot(lo, ones_ref[...], preferred_element_type=F32))
        _accumulate(dg_ref, i == 0, jnp.sum(dm * a * r, axis=0, keepdims=True))

    row = pl.BlockSpec((tm, AW), lambda i: (i, 0))
    vec = pl.BlockSpec((1, AW), lambda i: (0, 0))
    return pl.pallas_call(
        body, grid=(T // tm,),
        in_specs=[row, row, pl.BlockSpec((tm, LANES), lambda i: (i, 0)), vec,
                  pl.BlockSpec((AW, AW), lambda i: (0, 0))],
        out_specs=[row, row, vec],
        out_shape=[jax.ShapeDtypeStruct((T, AW), BF16), jax.ShapeDtypeStruct((T, AW), F32),
                   jax.ShapeDtypeStruct((1, AW), F32)],
        compiler_params=_params(1), name="attn_pre_bwd",
    )(dmixed, attn, rstd, gain, ones_bd)


def _conv_branch_fwd_math(a, g, w_ref, cb, lg, lb, row):
    sg = _sigmoid(g)
    u0 = a * sg
    uc = jnp.zeros_like(u0) + cb
    for k in range(CONV_KERNEL):
        uc = uc + w_ref[k:k + 1, :] * _shift_down(u0, CONV_KERNEL - 1 - k, row)
    ul, xh, r = _ln_fwd(uc, lg, lb)
    su = _sigmoid(ul)
    u = ul * su
    return sg, u0, ul, xh, r, su, u


def _conv_fwd(ag, conv_w, conv_b, ln_g, ln_b, norm_g, B, S, CW):
    def body(a_ref, g_ref, w_ref, cb_ref, lg_ref, lb_ref, ng_ref, o_ref):
        row = lax.broadcasted_iota(jnp.int32, (S, CW), 0)
        _, _, _, _, _, _, u = _conv_branch_fwd_math(a_ref[0], g_ref[0], w_ref, cb_ref[...], lg_ref[...],
                                                    lb_ref[...], row)
        rr = lax.rsqrt(jnp.mean(u * u, axis=-1, keepdims=True) + LN_EPS)
        o_ref[0] = (u * rr * ng_ref[...]).astype(BF16)

    vec = pl.BlockSpec((1, CW), lambda b: (0, 0))
    out = pl.pallas_call(
        body, grid=(B,),
        in_specs=[pl.BlockSpec((1, S, CW), lambda b: (b, 0, 0)), pl.BlockSpec((1, S, CW), lambda b: (b, 0, 1)),
                  pl.BlockSpec((CONV_KERNEL, CW), lambda b: (0, 0)), vec, vec, vec, vec],
        out_specs=pl.BlockSpec((1, S, CW), lambda b: (b, 0, 0)),
        out_shape=jax.ShapeDtypeStruct((B, S, CW), BF16),
        compiler_params=_params(1), name="conv_fwd",
    )(ag.reshape(B, S, 2 * CW), ag.reshape(B, S, 2 * CW), conv_w, conv_b, ln_g, ln_b, norm_g)
    return out.reshape(B * S, CW)


def _conv_bwd(ag, dmixed, conv_w, conv_b, ln_g, ln_b, norm_g, B, S, CW, D):
    AW = D - CW
    assert AW % CW == 0

    def body(a_ref, g_ref, dm_ref, w_ref, cb_ref, lg_ref, lb_ref, ng_ref,
             dag_ref, dw_ref, dcb_ref, dlg_ref, dlb_ref, dng_ref):
        b = pl.program_id(0)
        row = lax.broadcasted_iota(jnp.int32, (S, CW), 0)
        a, g = a_ref[0], g_ref[0]
        sg, u0, ul, xh, r, su, u = _conv_branch_fwd_math(a, g, w_ref, cb_ref[...], lg_ref[...], lb_ref[...], row)
        rr = lax.rsqrt(jnp.mean(u * u, axis=-1, keepdims=True) + LN_EPS)
        dm = dm_ref[0]
        dxn = dm * ng_ref[...]
        du = rr * (dxn - u * (rr * rr) * jnp.mean(dxn * u, axis=-1, keepdims=True))
        dul = du * su * (1.0 + ul * (1.0 - su))
        duc = _ln_bwd(dul, xh, r, lg_ref[...])
        first = b == 0
        _accumulate(dng_ref, first, jnp.sum(dm * u * rr, axis=0, keepdims=True))
        _accumulate(dlg_ref, first, jnp.sum(dul * xh, axis=0, keepdims=True))
        _accumulate(dlb_ref, first, jnp.sum(dul, axis=0, keepdims=True))
        _accumulate(dcb_ref, first, jnp.sum(duc, axis=0, keepdims=True))

        @pl.when(first)
        def _():
            dw_ref[...] = jnp.zeros_like(dw_ref)

        du0 = jnp.zeros_like(u0)
        for k in range(CONV_KERNEL):
            sh = CONV_KERNEL - 1 - k
            dw_ref[k:k + 1, :] += jnp.sum(duc * _shift_down(u0, sh, row), axis=0, keepdims=True)
            du0 = du0 + w_ref[k:k + 1, :] * _shift_up(duc, sh, row)
        dag_ref[0, :, :CW] = du0 * sg
        dag_ref[0, :, CW:] = du0 * a * sg * (1.0 - sg)

    vec = pl.BlockSpec((1, CW), lambda b: (0, 0))
    wspec = pl.BlockSpec((CONV_KERNEL, CW), lambda b: (0, 0))
    agv = ag.reshape(B, S, 2 * CW)
    res = pl.pallas_call(
        body, grid=(B,),
        in_specs=[pl.BlockSpec((1, S, CW), lambda b: (b, 0, 0)), pl.BlockSpec((1, S, CW), lambda b: (b, 0, 1)),
                  pl.BlockSpec((1, S, CW), lambda b: (b, 0, AW // CW)), wspec, vec, vec, vec, vec],
        out_specs=[pl.BlockSpec((1, S, 2 * CW), lambda b: (b, 0, 0)), wspec, vec, vec, vec, vec],
        out_shape=[jax.ShapeDtypeStruct((B, S, 2 * CW), F32), jax.ShapeDtypeStruct((CONV_KERNEL, CW), F32)]
        + [jax.ShapeDtypeStruct((1, CW), F32)] * 4,
        compiler_params=_params(1), name="conv_bwd",
    )(agv, agv, dmixed.reshape(B, S, D), conv_w, conv_b, ln_g, ln_b, norm_g)
    return (res[0].reshape(B * S, 2 * CW),) + tuple(res[1:])


def _ffn_conv(x, w_ref, bias, row):
    y = jnp.zeros_like(x) + bias
    for k in range(FFN_CONV_KERNEL):
        y = y + w_ref[k:k + 1, :] * _shift_down(x, FFN_CONV_KERNEL - 1 - k, row)
    return y


def _ffn_specs(S, tc, nj, order):
    pick = (lambda b, j: (b, j)) if order == "bj" else (lambda j, b: (b, j))
    act = lambda off: pl.BlockSpec((1, S, tc), lambda *g: (pick(*g)[0], 0, off + pick(*g)[1]))
    cw = lambda off: pl.BlockSpec((FFN_CONV_KERNEL, tc), lambda *g: (0, off + pick(*g)[1]))
    cb = lambda off: pl.BlockSpec((1, tc), lambda *g: (0, off + pick(*g)[1]))
    return act, cw, cb


def _ffn_act(upre, cw, cb, B, S, DFF):
    tc = FFN_COLS
    nj = DFF // tc

    def body(ug_ref, uv_ref, wg_ref, wv_ref, bg_ref, bv_ref, o_ref):
        row = lax.broadcasted_iota(jnp.int32, (S, tc), 0)
        gate = _ffn_conv(ug_ref[0], wg_ref, bg_ref[...], row)
        val = _ffn_conv(uv_ref[0], wv_ref, bv_ref[...], row)
        o_ref[0] = (gate * _sigmoid(gate) * val).astype(BF16)

    act, cws, cbs = _ffn_specs(S, tc, nj, "bj")
    uv = upre.reshape(B, S, 2 * DFF)
    out = pl.pallas_call(
        body, grid=(B, nj), in_specs=[act(0), act(nj), cws(0), cws(nj), cbs(0), cbs(nj)], out_specs=act(0),
        out_shape=jax.ShapeDtypeStruct((B, S, DFF), BF16), compiler_params=_params(2), name="ffn_act",
    )(uv, uv, cw, cw, cb, cb)
    return out.reshape(B * S, DFF)


def _ffn_bwd(upre, dact, cw, cb, B, S, DFF):
    tc = FFN_COLS
    nj = DFF // tc

    def body(ug_ref, uv_ref, da_ref, wg_ref, wv_ref, bg_ref, bv_ref, dug_ref, duv_ref, dwg_ref, dwv_ref,
             dbg_ref, dbv_ref):
        first = pl.program_id(1) == 0
        row = lax.broadcasted_iota(jnp.int32, (S, tc), 0)
        ug, uv = ug_ref[0], uv_ref[0]
        gate = _ffn_conv(ug, wg_ref, bg_ref[...], row)
        val = _ffn_conv(uv, wv_ref, bv_ref[...], row)
        sg = _sigmoid(gate)
        dact_b = da_ref[0]
        dgate = dact_b * val * sg * (1.0 + gate * (1.0 - sg))
        dval = dact_b * gate * sg
        for dup, u, w_ref, du_ref, dw_ref, db_ref in ((dgate, ug, wg_ref, dug_ref, dwg_ref, dbg_ref),
                                                      (dval, uv, wv_ref, duv_ref, dwv_ref, dbv_ref)):
            _accumulate(db_ref, first, jnp.sum(dup, axis=0, keepdims=True))

            @pl.when(first)
            def _(dw_ref=dw_ref):
                dw_ref[...] = jnp.zeros_like(dw_ref)

            dupre = jnp.zeros_like(dup)
            for k in range(FFN_CONV_KERNEL):
                sh = FFN_CONV_KERNEL - 1 - k
                dw_ref[k:k + 1, :] += jnp.sum(dup * _shift_down(u, sh, row), axis=0, keepdims=True)
                dupre = dupre + w_ref[k:k + 1, :] * _shift_up(dup, sh, row)
            du_ref[0] = dupre.astype(BF16)

    act, cws, cbs = _ffn_specs(S, tc, nj, "jb")
    uv = upre.reshape(B, S, 2 * DFF)
    res = pl.pallas_call(
        body, grid=(nj, B),
        in_specs=[act(0), act(nj), act(0), cws(0), cws(nj), cbs(0), cbs(nj)],
        out_specs=[act(0), act(0), cws(0), cws(0), cbs(0), cbs(0)],
        out_shape=[jax.ShapeDtypeStruct((B, S, DFF), BF16)] * 2
        + [jax.ShapeDtypeStruct((FFN_CONV_KERNEL, DFF), F32)] * 2 + [jax.ShapeDtypeStruct((1, DFF), F32)] * 2,
        compiler_params=_params(2), name="ffn_bwd",
    )(uv, uv, dact.reshape(B, S, DFF), cw, cw, cb, cb)
    flat = lambda t: t.reshape(B * S, DFF)
    return (flat(res[0]), flat(res[1]), jnp.concatenate([res[2], res[3]], axis=1),
            jnp.concatenate([res[4], res[5]], axis=1))


FFN_HALO = 16


def _half_sequences(S):
    if S < 8 * FFN_HALO:
        return [(0, S, 0, S)]
    h = S // 2
    return [(0, h + FFN_HALO, 0, h), (h - FFN_HALO, S, FFN_HALO, h)]


def _w_up_block_spec(w_up_sh, tc, off):
    _, D, cs = w_up_sh.shape
    assert cs % tc == 0
    bps = cs // tc
    return pl.BlockSpec((1, D, tc), lambda j: ((off + j) // bps, 0, (off + j) % bps))


def _ffn_fwd_fused(x1b, w_up_sh, cw, cb, B, S, DFF):
    tc = FFN_COLS
    nj = DFF // tc
    D = x1b.shape[1]

    def body(x_ref, wg_ref, wv_ref, cwg_ref, cwv_ref, cbg_ref, cbv_ref, o_ref):
        w = jnp.concatenate([wg_ref[0], wv_ref[0]], axis=1)
        row = lax.broadcasted_iota(jnp.int32, (S, tc), 0)
        for b in range(B):
            up = jnp.dot(x_ref[b], w, preferred_element_type=F32)
            gate = _ffn_conv(up[:, :tc], cwg_ref, cbg_ref[...], row)
            val = _ffn_conv(up[:, tc:], cwv_ref, cbv_ref[...], row)
            o_ref[b] = (gate * _sigmoid(gate) * val).astype(BF16)

    cws = lambda off: pl.BlockSpec((FFN_CONV_KERNEL, tc), lambda j: (0, off + j))
    cbs = lambda off: pl.BlockSpec((1, tc), lambda j: (0, off + j))
    out = pl.pallas_call(
        body, grid=(nj,),
        in_specs=[pl.BlockSpec((B, S, D), lambda j: (0, 0, 0), pipeline_mode=pl.Buffered(1)),
                  _w_up_block_spec(w_up_sh, tc, 0), _w_up_block_spec(w_up_sh, tc, nj),
                  cws(0), cws(nj), cbs(0), cbs(nj)],
        out_specs=pl.BlockSpec((B, S, tc), lambda j: (0, 0, j)), out_shape=jax.ShapeDtypeStruct((B, S, DFF), BF16),
        compiler_params=_params(1), name="ffn_fwd",
    )(x1b.reshape(B, S, D), w_up_sh, w_up_sh, cw, cw, cb, cb)
    return out.reshape(B * S, DFF)


def _ffn_bwd_fused(x1b, dz2b, w_up_sh, w_down, cw, cb, B, S, DFF):
    tc = FFN_COLS
    nj = DFF // tc
    D = x1b.shape[1]

    def body(x_ref, dz_ref, wg_ref, wv_ref, wd_ref, cwg_ref, cwv_ref, cbg_ref, cbv_ref,
             dug_ref, duv_ref, dwu_ref, dwd_ref, dcw_ref, dcb_ref):
        first = pl.program_id(1) == 0
        w = jnp.concatenate([wg_ref[0], wv_ref[0]], axis=1)
        dw_t = dwd = None
        dcb = [None, None]
        dcw = [[None] * FFN_CONV_KERNEL, [None] * FFN_CONV_KERNEL]
        add = lambda old, new: new if old is None else old + new
        for lo, hi, o0, on in _half_sequences(S):
            n = hi - lo
            own = slice(o0, o0 + on)
            row = lax.broadcasted_iota(jnp.int32, (n, tc), 0)
            x = x_ref[0, lo:hi, :]
            dz = dz_ref[0, lo:hi, :]
            up = jnp.dot(x, w, preferred_element_type=F32)
            ug, uv = up[:, :tc], up[:, tc:]
            gate = _ffn_conv(ug, cwg_ref, cbg_ref[...], row)
            val = _ffn_conv(uv, cwv_ref, cbv_ref[...], row)
            sg = _sigmoid(gate)
            act = (gate * sg * val).astype(BF16)
            dact = _dot(dz, wd_ref[...], "nt")
            dgate = dact * val * sg * (1.0 + gate * (1.0 - sg))
            dval = dact * gate * sg
            dupre = []
            for h, (dup, u, w_ref) in enumerate(((dgate, ug, cwg_ref), (dval, uv, cwv_ref))):
                dcb[h] = add(dcb[h], jnp.sum(dup[own], axis=0, keepdims=True))
                acc = jnp.zeros_like(dup)
                for k in range(FFN_CONV_KERNEL):
                    sh = FFN_CONV_KERNEL - 1 - k
                    dcw[h][k] = add(dcw[h][k], jnp.sum((dup * _shift_down(u, sh, row))[own], axis=0, keepdims=True))
                    acc = acc + w_ref[k:k + 1, :] * _shift_up(dup, sh, row)
                dupre.append(acc.astype(BF16)[own])
            dug_ref[0, lo + o0:lo + o0 + on, :] = dupre[0]
            duv_ref[0, lo + o0:lo + o0 + on, :] = dupre[1]
            dw_t = add(dw_t, _dot(jnp.concatenate(dupre, axis=1), x[own], "tn"))
            dwd = add(dwd, _dot(act[own], dz[own], "tn"))
        _accumulate(dwu_ref.at[0], first, dw_t[:tc])
        _accumulate(dwu_ref.at[1], first, dw_t[tc:])
        _accumulate(dwd_ref, first, dwd)
        for h in range(2):
            _accumulate(dcb_ref.at[h], first, dcb[h])
            for k in range(FFN_CONV_KERNEL):
                _accumulate(dcw_ref.at[h, pl.ds(k, 1), :], first, dcw[h][k])

    act_s, cws, cbs = _ffn_specs(S, tc, nj, "jb")
    seq = pl.BlockSpec((1, S, D), lambda j, b: (b, 0, 0))
    wspec = lambda off: pl.BlockSpec((1, D, tc), lambda j, b: ((off + j) // (w_up_sh.shape[2] // tc), 0,
                                                               (off + j) % (w_up_sh.shape[2] // tc)))
    res = pl.pallas_call(
        body, grid=(nj, B),
        in_specs=[seq, seq, wspec(0), wspec(nj),
                  pl.BlockSpec((tc, D), lambda j, b: (j, 0)), cws(0), cws(nj), cbs(0), cbs(nj)],
        out_specs=[act_s(0), act_s(0), pl.BlockSpec((2, tc, D), lambda j, b: (0, j, 0)),
                   pl.BlockSpec((tc, D), lambda j, b: (j, 0)),
                   pl.BlockSpec((2, FFN_CONV_KERNEL, tc), lambda j, b: (0, 0, j)),
                   pl.BlockSpec((2, 1, tc), lambda j, b: (0, 0, j))],
        out_shape=[jax.ShapeDtypeStruct((B, S, DFF), BF16)] * 2
        + [jax.ShapeDtypeStruct((2, DFF, D), F32), jax.ShapeDtypeStruct((DFF, D), F32),
           jax.ShapeDtypeStruct((2, FFN_CONV_KERNEL, DFF), F32), jax.ShapeDtypeStruct((2, 1, DFF), F32)],
        compiler_params=_params(2), name="ffn_bwd",
    )(x1b.reshape(B, S, D), dz2b.reshape(B, S, D), w_up_sh, w_up_sh, w_down, cw, cw, cb, cb)
    flat = lambda t: t.reshape(B * S, DFF)
    return flat(res[0]), flat(res[1]), res[2], res[3], res[4], res[5]


def _transpose(x, tr, name):
    R, C = x.shape

    def body(x_ref, o_ref):
        o_ref[...] = x_ref[...].T

    return pl.pallas_call(
        body, grid=(R // tr,), in_specs=[pl.BlockSpec((tr, C), lambda i: (i, 0))],
        out_specs=pl.BlockSpec((C, tr), lambda i: (0, i)), out_shape=jax.ShapeDtypeStruct((C, R), F32),
        compiler_params=_params(1), name=name)(x)


def _dh_cat(dq, dk, dv, dag, tm):
    T, AW = dq.shape
    CW2 = dag.shape[1]
    W = 3 * AW + CW2

    def body(dq_ref, dk_ref, dv_ref, dag_ref, dh_ref, cs_ref):
        for c, ref in enumerate((dq_ref, dk_ref, dv_ref)):
            dh_ref[:, c * AW:(c + 1) * AW] = ref[...]
        dg = dag_ref[...]
        dh_ref[:, 3 * AW:] = dg.astype(BF16)
        _accumulate(cs_ref, pl.program_id(0) == 0, jnp.sum(dg, axis=0, keepdims=True))

    row = pl.BlockSpec((tm, AW), lambda i: (i, 0))
    return pl.pallas_call(
        body, grid=(T // tm,),
        in_specs=[row] * 3 + [pl.BlockSpec((tm, CW2), lambda i: (i, 0))],
        out_specs=[pl.BlockSpec((tm, W), lambda i: (i, 0)), pl.BlockSpec((1, CW2), lambda i: (0, 0))],
        out_shape=[jax.ShapeDtypeStruct((T, W), BF16), jax.ShapeDtypeStruct((1, CW2), F32)],
        compiler_params=_params(1), name="dh_cat",
    )(dq, dk, dv, dag)


def _local_step(x, target, rel_table, w_in, b_in, conv_w, conv_b, conv_ln_g, conv_ln_b, attn_norm_g,
                conv_norm_g, staged, ln1_g, ln1_b, ffn_cw, ffn_cb, ln2_g, ln2_b, ids):
    B, S, D = x.shape
    T = B * S
    AW = attn_norm_g.shape[-1]
    CW = conv_norm_g.shape[-1]
    H = AW // HEAD_DIM
    DFF = staged[2].shape[0] * staged[2].shape[1]
    INW = 3 * AW + 2 * CW
    xf = x.reshape(T, D)
    tf = target.reshape(T, D)
    tm = _row_tile(T, 512)
    tm_s = _row_tile(T, 256)

    bucket_np, mask_np = _bucket_tables()
    bucket = jnp.asarray(bucket_np)
    band_mask = jnp.asarray(mask_np)
    bias_all = _bias_build(rel_table.T, bucket, band_mask).reshape(3, H, ATTN_BLOCK, 2 * ATTN_BLOCK)

    tn_qkv = _col_tile(3 * AW, 1152)
    qkv = _mm_plain(xf, w_in[:, :3 * AW], mode="nn", tm=tm, tn=tn_qkv, tk=D, out_dtype=BF16,
                    bias=b_in[:, :3 * AW], name="mm_qkv")
    ag = _mm_plain(xf, w_in[:, 3 * AW:], mode="nn", tm=tm, tn=2 * CW, tk=D, out_dtype=F32,
                   bias=b_in[:, 3 * AW:], name="mm_ag")

    attn, lse, w_out_g, w_up_sh, w_down_g = _attention_fwd(qkv, bias_all, B, S, AW, bg=_bg_gather(staged))
    w_out = w_out_g.reshape(D, D)
    w_down = w_down_g.reshape(DFF, D)
    mixed_a, r_attn = _attn_norm(attn, attn_norm_g, tm_s)
    mixed_c = _conv_fwd(ag, conv_w, conv_b, conv_ln_g, conv_ln_b, conv_norm_g, B, S, CW)
    mixed = jnp.concatenate([mixed_a, mixed_c], axis=1)

    def ln1_epilogue(acc, i, j, extra_refs, out_refs):
        x_ref, g_ref, b_ref = extra_refs
        x1, xh, r = _ln_fwd(acc + ALPHA * x_ref[...], g_ref[...], b_ref[...])
        out_refs[0][...] = x1
        out_refs[1][...] = x1.astype(BF16)
        out_refs[2][...] = xh
        out_refs[3][...] = jnp.broadcast_to(r, (tm_s, LANES))

    rowD = lambda i, j, k: (i, 0)
    vecD = lambda i, j, k: (0, 0)
    x1, x1b, xh1, r1 = _matmul(
        mixed, w_out, mode="nn", tm=tm_s, tn=D, tk=D,
        extras=[(xf, (tm_s, D), rowD), (ln1_g, (1, D), vecD), (ln1_b, (1, D), vecD)],
        outs=[((T, D), F32, (tm_s, D), rowD), ((T, D), BF16, (tm_s, D), rowD), ((T, D), F32, (tm_s, D), rowD),
              ((T, LANES), F32, (tm_s, LANES), rowD)],
        epilogue=ln1_epilogue, name="mm_out_ln1")

    NS, _, cs = w_up_sh.shape
    half = NS // 2

    act = _ffn_fwd_fused(x1b, w_up_sh, ffn_cw, ffn_cb, B, S, DFF)

    def ln2_epilogue(acc, i, j, extra_refs, out_refs):
        x1_ref, g_ref, b_ref, t_ref = extra_refs
        dz_ref, dzb_ref, loss_ref, dg_ref, db_ref = out_refs
        g = g_ref[...]
        y, xh, r = _ln_fwd(acc + ALPHA * x1_ref[...], g, b_ref[...])
        diff = y - t_ref[...]
        row_loss = jnp.sum(diff * diff, axis=1, keepdims=True)
        tile_loss = jnp.sum(row_loss, axis=0, keepdims=True) * (0.5 / D)
        dy = diff * (1.0 / D)
        dz = _ln_bwd(dy, xh, r, g)
        dz_ref[...] = dz
        dzb_ref[...] = dz.astype(BF16)
        first = i == 0
        _accumulate(loss_ref, first, jnp.broadcast_to(tile_loss, (1, LANES)))
        _accumulate(dg_ref, first, jnp.sum(dy * xh, axis=0, keepdims=True))
        _accumulate(db_ref, first, jnp.sum(dy, axis=0, keepdims=True))

    dz2, dz2b, loss_part, d_ln2_g, d_ln2_b = _matmul(
        act, w_down, mode="nn", tm=tm_s, tn=D, tk=DFF,
        extras=[(x1, (tm_s, D), rowD), (ln2_g, (1, D), vecD), (ln2_b, (1, D), vecD), (tf, (tm_s, D), rowD)],
        outs=[((T, D), F32, (tm_s, D), rowD), ((T, D), BF16, (tm_s, D), rowD),
              ((1, LANES), F32, (1, LANES), vecD), ((1, D), F32, (1, D), vecD), ((1, D), F32, (1, D), vecD)],
        epilogue=ln2_epilogue, name="mm_down_ln2_loss")

    dupre_g, dupre_v, d_w_up_t, d_w_down, d_ffn_cw2, d_ffn_cb2 = _ffn_bwd_fused(
        x1b, dz2b, w_up_sh, w_down, ffn_cw, ffn_cb, B, S, DFF)
    d_w_up_t = d_w_up_t.reshape(NS, cs, D)
    d_ffn_cw = jnp.transpose(d_ffn_cw2, (1, 0, 2)).reshape(FFN_CONV_KERNEL, 2 * DFF)
    d_ffn_cb = d_ffn_cb2.reshape(1, 2 * DFF)
    tk_t = _row_tile(T, 512)

    def ln1_bwd_epilogue(acc, i, j, extra_refs, out_refs):
        dz2_ref, xh_ref, r_ref, g_ref = extra_refs
        dz_ref, dzb_ref, dg_ref, db_ref = out_refs
        dx1 = acc + ALPHA * dz2_ref[...]
        xh = xh_ref[...]
        dz = _ln_bwd(dx1, xh, r_ref[:, 0:1], g_ref[...])
        dz_ref[...] = dz
        dzb_ref[...] = dz.astype(BF16)
        first = i == 0
        _accumulate(dg_ref, first, jnp.sum(dx1 * xh, axis=0, keepdims=True))
        _accumulate(db_ref, first, jnp.sum(dx1, axis=0, keepdims=True))

    early = [d_w_up_t, d_w_down.reshape(NS, DFF // NS, D)]
    dz1, dz1b, d_ln1_g, d_ln1_b, *sib_e = _matmul_general(
        [(dupre_g, (tm, cs), lambda i, j, k: (i, jnp.minimum(k, half - 1))),
         (dupre_v, (tm, cs), lambda i, j, k: (i, jnp.maximum(k - half, 0))),
         (w_up_sh, (1, D, cs), lambda i, j, k: (k, 0, 0))],
        lambda refs, i, j, k: _dot(jnp.where(k < half, refs[0][...], refs[1][...]), refs[2][0], "nt"),
        grid=(T // tm, 1, NS), tm=tm, tn=D,
        extras=[(dz2, (tm, D), rowD), (xh1, (tm, D), rowD), (r1, (tm, LANES), rowD), (ln1_g, (1, D), vecD)],
        outs=[((T, D), F32, (tm, D), rowD), ((T, D), BF16, (tm, D), rowD),
              ((1, D), F32, (1, D), vecD), ((1, D), F32, (1, D), vecD)],
        epilogue=ln1_bwd_epilogue, name="mm_dx1_ln1_bwd", bg=_bg_sibling_exchange(early))
    chip_e = [_pair_sum(g, s, ids, name="pair_sum_" + n) for g, s, n in zip(early, sib_e, ("w_up", "w_down"))]

    dmixed = _mm_plain(dz1b, w_out, mode="nt", tm=tm, tn=D, tk=D, out_dtype=F32, name="mm_dmixed")
    d_w_out = _mm_plain(mixed, dz1b, mode="tn", tm=D, tn=D, tk=tk_t, out_dtype=F32, name="mm_dw_out")

    dattn, dd, d_attn_norm_g = _attn_pre_bwd(dmixed, attn, r_attn, attn_norm_g, tm_s)
    dag, d_conv_w, d_conv_b, d_conv_ln_g, d_conv_ln_b, d_conv_norm_g = _conv_bwd(
        ag, dmixed, conv_w, conv_b, conv_ln_g, conv_ln_b, conv_norm_g, B, S, CW, D)

    dq, dk, dv, csq, csk, csv, dbias, *got_e = _attention_bwd(qkv, dattn, lse, dd, bias_all, B, S, AW,
                                                              bg=_bg_chip_exchange(chip_e))
    full_up, full_down = [_final_sum(g, s, r, ids, name="final_sum_" + n)
                          for g, s, r, n in zip(early, sib_e, got_e, ("w_up", "w_down"))]
    d_rel_table = _rel_grad(dbias.reshape(3, H, ATTN_BLOCK * 2 * ATTN_BLOCK), bucket).T
    dh, cs_ag = _dh_cat(dq, dk, dv, dag, tm_s)
    d_b_in = jnp.concatenate([csq, csk, csv, cs_ag], axis=1)

    d_w_in = _mm_plain(xf, dh, mode="tn", tm=D, tn=_col_tile(INW, 1408), tk=tk_t, out_dtype=F32, name="mm_dw_in")
    late = [jnp.transpose(d_w_in.reshape(D, NS, INW // NS), (1, 0, 2)), d_w_out.reshape(NS, D // NS, D)]
    sib_l = _sibling_exchange(late)
    chip_l = [_pair_sum(g, s, ids, name="pair_sum_" + n) for g, s, n in zip(late, sib_l, ("w_in", "w_out"))]
    small = dict(rel_table=d_rel_table, b_in=d_b_in, conv_w=d_conv_w, conv_b=d_conv_b, conv_ln_g=d_conv_ln_g,
                 conv_ln_b=d_conv_ln_b, attn_norm_g=d_attn_norm_g, conv_norm_g=d_conv_norm_g, ln1_g=d_ln1_g,
                 ln1_b=d_ln1_b, ffn_conv_w=d_ffn_cw, ffn_conv_b=d_ffn_cb, ln2_g=d_ln2_g, ln2_b=d_ln2_b)
    pack = _pack([loss_part] + [small[n] for n in SMALL_NAMES])

    def gx_epilogue(acc, i, j, extra_refs, out_refs):
        out_refs[0][...] = acc + ALPHA * extra_refs[0][...]

    grad_x, got_in, got_out, all_packs = _matmul(
        dh, w_in, mode="nt", tm=tm_s, tn=D, tk=INW, extras=[(dz1, (tm_s, D), rowD)],
        outs=[((T, D), F32, (tm_s, D), rowD)], epilogue=gx_epilogue, name="mm_grad_x",
        bg=_bg_chip_exchange(chip_l, pack))
    full_in, full_out = [_final_sum(g, s, r, ids, name="final_sum_" + n)
                         for g, s, r, n in zip(late, sib_l, (got_in, got_out), ("w_in", "w_out"))]
    return grad_x.reshape(B, S, D), [full_in, full_out, full_up, full_down], all_packs


def _place():
    return lax.axis_index("x"), lax.axis_index("y"), lax.axis_index("c")


CHIP_FLIPS = ((1, 0), (0, 1), (1, 1))


def _flip(v, f):
    return 1 - v if f else v


HBM_SPEC = pl.BlockSpec(memory_space=pl.ANY)
VMEM_SPEC = pl.BlockSpec(memory_space=pltpu.VMEM)
COMM_PARAMS = pltpu.CompilerParams(vmem_limit_bytes=VMEM_LIMIT)


def _gather_weights(big, small):
    nb, ns = len(big), len(small)

    def body(*refs):
        big_in = refs[:nb]
        small_in = refs[nb:nb + ns]
        big_out = refs[nb + ns:2 * nb + ns]
        small_out = refs[2 * nb + ns:2 * nb + 2 * ns]
        stages = refs[2 * nb + 2 * ns:3 * nb + 2 * ns]
        send_sems, recv_sems, local_sems = refs[3 * nb + 2 * ns:]
        x, y, c = _place()
        s_me = 2 * x + y
        sibling = (x, y, 1 - c)
        started, local_copies = [], []
        for a in range(nb):
            rh = big[a].shape[0] // 2
            lo = pl.multiple_of(c * rh, 16)
            stages[a][...] = big_in[a][pl.ds(lo, rh), :].astype(BF16)
            mine = big_out[a].at[s_me, pl.ds(lo, rh), :]
            loc = pltpu.make_async_copy(stages[a], mine, local_sems.at[a])
            loc.start()
            local_copies.append(loc)
            targets = [sibling] + [(_flip(x, fx), _flip(y, fy), c) for fx, fy in CHIP_FLIPS]
            for k, to in enumerate(targets):
                cp = pltpu.make_async_remote_copy(stages[a], mine, send_sems.at[a * 7 + k],
                                                  recv_sems.at[a * 7 + k], device_id=to, device_id_type=MESH)
                cp.start()
                started.append(cp)
        for a in range(ns):
            mine = small_out[a].at[s_me]
            loc = pltpu.make_async_copy(small_in[a], mine, local_sems.at[nb + a])
            loc.start()
            local_copies.append(loc)
            for k, (fx, fy) in enumerate(CHIP_FLIPS):
                cp = pltpu.make_async_remote_copy(small_in[a], mine, send_sems.at[nb * 7 + a * 3 + k],
                                                  recv_sems.at[nb * 7 + a * 3 + k],
                                                  device_id=(_flip(x, fx), _flip(y, fy), c), device_id_type=MESH)
                cp.start()
                started.append(cp)
        for a in range(nb):
            rh = big[a].shape[0] // 2
            lo = pl.multiple_of(c * rh, 16)
            for k, (fx, fy) in enumerate(CHIP_FLIPS):
                s_from = 2 * _flip(x, fx) + _flip(y, fy)
                got = big_out[a].at[s_from, pl.ds(lo, rh), :]
                pltpu.make_async_remote_copy(got, got, send_sems.at[a * 7 + 1 + k], recv_sems.at[a * 7 + 1 + k],
                                             device_id=sibling, device_id_type=MESH).wait_recv()
                fwd = pltpu.make_async_remote_copy(got, got, send_sems.at[a * 7 + 4 + k],
                                                   recv_sems.at[a * 7 + 4 + k], device_id=sibling,
                                                   device_id_type=MESH)
                fwd.start()
                started.append(fwd)
        for a in range(nb):
            rh = big[a].shape[0] // 2
            lo_sib = pl.multiple_of((1 - c) * rh, 16)
            for k in (0, 4, 5, 6):
                any_rows = big_out[a].at[s_me, pl.ds(lo_sib, rh), :]
                pltpu.make_async_remote_copy(any_rows, any_rows, send_sems.at[a * 7 + k], recv_sems.at[a * 7 + k],
                                             device_id=sibling, device_id_type=MESH).wait_recv()
        for a in range(ns):
            for k in range(3):
                pltpu.make_async_remote_copy(small_in[a], small_out[a].at[s_me], send_sems.at[nb * 7 + a * 3 + k],
                                             recv_sems.at[nb * 7 + a * 3 + k], device_id=sibling,
                                             device_id_type=MESH).wait_recv()
        for cp in started:
            cp.wait_send()
        for cp in local_copies:
            cp.wait()

    n_sem = nb * 7 + ns * 3
    out_shape = ([jax.ShapeDtypeStruct((N_SHARDS,) + w.shape, BF16) for w in big]
                 + [jax.ShapeDtypeStruct((N_SHARDS,) + w.shape, F32) for w in small])
    res = pl.pallas_call(
        body, in_specs=[VMEM_SPEC] * nb + [HBM_SPEC] * ns, out_specs=[HBM_SPEC] * (nb + ns),
        out_shape=out_shape,
        scratch_shapes=[pltpu.VMEM((w.shape[0] // 2, w.shape[1]), BF16) for w in big]
        + [pltpu.SemaphoreType.DMA((n_sem,)), pltpu.SemaphoreType.DMA((n_sem,)),
           pltpu.SemaphoreType.DMA((nb + ns,))],
        compiler_params=COMM_PARAMS, name="gather_weights",
    )(*big, *small)
    return res[:nb], res[nb:]


def _sibling_exchange(grads):
    n = len(grads)

    def body(*refs):
        g_in = refs[:n]
        got = refs[n:2 * n]
        send_sems, recv_sems = refs[2 * n:]
        x, y, c = _place()
        cps = []
        for a in range(n):
            rh = grads[a].shape[1] // 2
            lo = pl.multiple_of((1 - c) * rh, 8)
            cp = pltpu.make_async_remote_copy(g_in[a].at[:, pl.ds(lo, rh), :], got[a], send_sems.at[a],
                                              recv_sems.at[a], device_id=(x, y, 1 - c), device_id_type=MESH)
            cp.start()
            cps.append(cp)
        for cp in cps:
            cp.wait()

    return pl.pallas_call(
        body, in_specs=[HBM_SPEC] * n, out_specs=[HBM_SPEC] * n,
        out_shape=[jax.ShapeDtypeStruct((N_SHARDS, g.shape[1] // 2, g.shape[2]), F32) for g in grads],
        scratch_shapes=[pltpu.SemaphoreType.DMA((n,)), pltpu.SemaphoreType.DMA((n,))],
        compiler_params=COMM_PARAMS, name="sibling_exchange",
    )(*grads)


def _chip_exchange(chip_parts, pack):
    n = len(chip_parts)

    def body(*refs):
        parts = refs[:n]
        pack_ref = refs[n]
        got = refs[n + 1:2 * n + 1]
        all_packs = refs[2 * n + 1]
        send_sems, recv_sems, local_sem = refs[2 * n + 2:]
        x, y, c = _place()
        me = 4 * x + 2 * y + c
        cps = []
        for a in range(n):
            for k, (fx, fy) in enumerate(CHIP_FLIPS):
                px, py = _flip(x, fx), _flip(y, fy)
                cp = pltpu.make_async_remote_copy(parts[a].at[2 * px + py], got[a].at[k], send_sems.at[a * 3 + k],
                                                  recv_sems.at[a * 3 + k], device_id=(px, py, c),
                                                  device_id_type=MESH)
                cp.start()
                cps.append(cp)
        loc = pltpu.make_async_copy(pack_ref, all_packs.at[me], local_sem)
        loc.start()
        for m in range(1, N_DEV):
            to = (_flip(x, m & 4), _flip(y, m & 2), _flip(c, m & 1))
            cp = pltpu.make_async_remote_copy(pack_ref, all_packs.at[me], send_sems.at[n * 3 + m - 1],
                                              recv_sems.at[n * 3 + m - 1], device_id=to, device_id_type=MESH)
            cp.start()
            cps.append(cp)
        for cp in cps:
            cp.wait()
        loc.wait()

    rs = pack.shape[0]
    res = pl.pallas_call(
        body, in_specs=[HBM_SPEC] * (n + 1), out_specs=[HBM_SPEC] * (n + 1),
        out_shape=[jax.ShapeDtypeStruct((3,) + p.shape[1:], BF16) for p in chip_parts]
        + [jax.ShapeDtypeStruct((N_DEV, rs, LANES), F32)],
        scratch_shapes=[pltpu.SemaphoreType.DMA((n * 3 + N_DEV - 1,)), pltpu.SemaphoreType.DMA((n * 3 + N_DEV - 1,)),
                        pltpu.SemaphoreType.DMA],
        compiler_params=COMM_PARAMS, name="chip_exchange",
    )(*chip_parts, pack)
    return res[:n], res[n]


def _sibling_assemble(fulls):
    n = len(fulls)

    def body(*refs):
        full = refs[n:2 * n]
        send_sems, recv_sems = refs[2 * n:]
        x, y, c = _place()
        cps = []
        for a in range(n):
            rh = fulls[a].shape[0] // 2
            mine = full[a].at[pl.ds(pl.multiple_of(c * rh, 8), rh), :]
            cp = pltpu.make_async_remote_copy(mine, mine, send_sems.at[a], recv_sems.at[a],
                                              device_id=(x, y, 1 - c), device_id_type=MESH)
            cp.start()
            cps.append(cp)
        for cp in cps:
            cp.wait()

    return pl.pallas_call(
        body, in_specs=[HBM_SPEC] * n, out_specs=[HBM_SPEC] * n,
        out_shape=[jax.ShapeDtypeStruct(f.shape, F32) for f in fulls],
        input_output_aliases={a: a for a in range(n)},
        scratch_shapes=[pltpu.SemaphoreType.DMA((n,)), pltpu.SemaphoreType.DMA((n,))],
        compiler_params=COMM_PARAMS, name="sibling_assemble",
    )(*fulls)


def _remote(ref_src, ref_dst, send_sems, recv_sems, k, to):
    return pltpu.make_async_remote_copy(ref_src, ref_dst, send_sems.at[k], recv_sems.at[k], device_id=to,
                                        device_id_type=MESH)


def _stage_half(w, ids, name):
    R, C = w.shape
    rh = R // 2
    rt = _half_tile(rh)
    nt = rh // rt

    def body(ids_ref, w_ref, o_ref):
        o_ref[0] = w_ref[...].astype(BF16)

    grid_spec = pltpu.PrefetchScalarGridSpec(
        num_scalar_prefetch=1, grid=(nt,),
        in_specs=[pl.BlockSpec((rt, C), lambda i, ids: (ids[2] * nt + i, 0))],
        out_specs=pl.BlockSpec((1, rt, C), lambda i, ids: (2 * ids[0] + ids[1], ids[2] * nt + i, 0)))
    return pl.pallas_call(body, grid_spec=grid_spec, out_shape=jax.ShapeDtypeStruct((N_SHARDS, R, C), BF16),
                          compiler_params=_params(1), name=name)(ids, w)


def _bg_gather(staged):
    n = len(staged)

    def run(step, n_steps, ins, outs, send_sems, recv_sems, local_sems, post):
        x, y, c = _place()
        s_me = 2 * x + y
        sibling = (x, y, 1 - c)
        chips = [(_flip(x, fx), _flip(y, fy)) for fx, fy in CHIP_FLIPS]

        def rows(a, s, half):
            rh = staged[a].shape[1] // 2
            return outs[a].at[s, pl.ds(pl.multiple_of(half * rh, 16), rh), :]

        def copy(a, k, ref, to):
            return _remote(ref, ref, send_sems, recv_sems, a * 7 + k, to)

        if not post:
            @pl.when(step == 0)
            def _():
                for a in range(n):
                    mine = rows(a, s_me, c)
                    copy(a, 0, mine, sibling).start()
                    for k, (px, py) in enumerate(chips):
                        copy(a, 1 + k, mine, (px, py, c)).start()

            @pl.when(step == max(n_steps - 2, 0))
            def _():
                for a in range(n):
                    for k, (px, py) in enumerate(chips):
                        got = rows(a, 2 * px + py, c)
                        copy(a, 1 + k, got, sibling).wait_recv()
                        copy(a, 4 + k, got, sibling).start()
        else:
            @pl.when(step == n_steps - 1)
            def _():
                for a in range(n):
                    for k in (0, 4, 5, 6):
                        copy(a, k, rows(a, s_me, 1 - c), sibling).wait_recv()
                    for k in range(7):
                        copy(a, k, rows(a, s_me, c), sibling).wait_send()

    return _Background(staged, [jax.ShapeDtypeStruct(g.shape, g.dtype) for g in staged],
                       {a: a for a in range(n)}, 7 * n, run)


def _bg_sibling_exchange(grads):
    n = len(grads)

    def run(step, n_steps, ins, outs, send_sems, recv_sems, local_sems, post):
        x, y, c = _place()

        def copy(a):
            rh = grads[a].shape[1] // 2
            lo = pl.multiple_of((1 - c) * rh, 8)
            return _remote(ins[a].at[:, pl.ds(lo, rh), :], outs[a], send_sems, recv_sems, a, (x, y, 1 - c))

        if not post:
            @pl.when(step == 0)
            def _():
                for a in range(n):
                    copy(a).start()
        else:
            @pl.when(step == n_steps - 1)
            def _():
                for a in range(n):
                    copy(a).wait()

    return _Background(grads, [jax.ShapeDtypeStruct((N_SHARDS, g.shape[1] // 2, g.shape[2]), F32) for g in grads],
                       {}, n, run)


def _bg_chip_exchange(chip_parts, pack=None):
    n = len(chip_parts)

    def run(step, n_steps, ins, outs, send_sems, recv_sems, local_sems, post):
        x, y, c = _place()
        me = 4 * x + 2 * y + c

        def copies():
            cps = []
            for a in range(n):
                for k, (fx, fy) in enumerate(CHIP_FLIPS):
                    px, py = _flip(x, fx), _flip(y, fy)
                    cps.append(_remote(ins[a].at[2 * px + py], outs[a].at[k], send_sems, recv_sems, a * 3 + k,
                                       (px, py, c)))
            if pack is not None:
                for m in range(1, N_DEV):
                    to = (_flip(x, m & 4), _flip(y, m & 2), _flip(c, m & 1))
                    cps.append(_remote(ins[n], outs[n].at[me], send_sems, recv_sems, n * 3 + m - 1, to))
            return cps

        def local():
            return pltpu.make_async_copy(ins[n], outs[n].at[me], local_sems.at[0])

        if not post:
            @pl.when(step == 0)
            def _():
                for cp in copies():
                    cp.start()
                if pack is not None:
                    local().start()
        else:
            @pl.when(step == n_steps - 1)
            def _():
                for cp in copies():
                    cp.wait()
                if pack is not None:
                    local().wait()

    in_arrays = list(chip_parts) + ([pack] if pack is not None else [])
    out_shapes = [jax.ShapeDtypeStruct((3,) + p.shape[1:], BF16) for p in chip_parts]
    if pack is not None:
        out_shapes.append(jax.ShapeDtypeStruct((N_DEV, pack.shape[0], LANES), F32))
    return _Background(in_arrays, out_shapes, {}, n * 3 + N_DEV - 1, run)


def _half_tile(rh, mult=16, want=256):
    best = None
    for t in range(mult, min(rh, want) + 1, mult):
        if rh % t == 0:
            best = t
    return best if best is not None else rh


def _pair_sum(g, sib, ids, name):
    _, R, C = g.shape
    rh = R // 2
    rt = _half_tile(rh)
    nt = rh // rt

    def body(ids_ref, g_ref, s_ref, o_ref):
        o_ref[...] = (g_ref[...] + s_ref[...]).astype(BF16)

    grid_spec = pltpu.PrefetchScalarGridSpec(
        num_scalar_prefetch=1, grid=(N_SHARDS, nt),
        in_specs=[pl.BlockSpec((1, rt, C), lambda s, i, ids: (s, ids[2] * nt + i, 0)),
                  pl.BlockSpec((1, rt, C), lambda s, i, ids: (s, i, 0))],
        out_specs=pl.BlockSpec((1, rt, C), lambda s, i, ids: (s, i, 0)))
    return pl.pallas_call(body, grid_spec=grid_spec, out_shape=jax.ShapeDtypeStruct((N_SHARDS, rh, C), BF16),
                          compiler_params=_params(2), name=name)(ids, g, sib)


def _final_sum(g, sib, got, ids, name):
    _, R, C = g.shape
    rh = R // 2
    rt = _half_tile(rh)
    nt = rh // rt

    def body(ids_ref, g_ref, s_ref, r_ref, o_ref):
        tot = g_ref[0] + s_ref[0]
        for k in range(3):
            tot = tot + r_ref[k].astype(F32)
        o_ref[...] = tot

    grid_spec = pltpu.PrefetchScalarGridSpec(
        num_scalar_prefetch=1, grid=(nt,),
        in_specs=[pl.BlockSpec((1, rt, C), lambda i, ids: (2 * ids[0] + ids[1], ids[2] * nt + i, 0)),
                  pl.BlockSpec((1, rt, C), lambda i, ids: (2 * ids[0] + ids[1], i, 0)),
                  pl.BlockSpec((3, rt, C), lambda i, ids: (0, i, 0))],
        out_specs=pl.BlockSpec((rt, C), lambda i, ids: (ids[2] * nt + i, 0)))
    return pl.pallas_call(body, grid_spec=grid_spec, out_shape=jax.ShapeDtypeStruct((R, C), F32),
                          compiler_params=_params(1), name=name)(ids, g, sib, got)


def _sum_packs(all_packs):
    def body(p_ref, o_ref):
        tot = p_ref[0]
        for i in range(1, N_DEV):
            tot = tot + p_ref[i]
        o_ref[...] = tot

    return pl.pallas_call(body, in_specs=[VMEM_SPEC], out_specs=VMEM_SPEC,
                          out_shape=jax.ShapeDtypeStruct(all_packs.shape[1:], F32), name="sum_packs")(all_packs)


def _adamw(w, g, m, v, name):
    R, C = w.shape
    rt = _half_tile(R, mult=8, want=256)

    def body(w_ref, g_ref, m_ref, v_ref, d_ref, nm_ref, nv_ref):
        gg = g_ref[...]
        nm = ADAM_B1 * m_ref[...] + (1.0 - ADAM_B1) * gg
        nv = ADAM_B2 * v_ref[...] + (1.0 - ADAM_B2) * (gg * gg)
        m_hat = nm / (1.0 - ADAM_B1 ** ADAM_STEP)
        v_hat = nv / (1.0 - ADAM_B2 ** ADAM_STEP)
        d_ref[...] = -ADAM_LR * (m_hat / (jnp.sqrt(v_hat) + ADAM_EPS) + ADAM_WD * w_ref[...])
        nm_ref[...] = nm
        nv_ref[...] = nv

    spec = pl.BlockSpec((rt, C), lambda i: (i, 0))
    return pl.pallas_call(body, grid=(R // rt,), in_specs=[spec] * 4, out_specs=[spec] * 3,
                          out_shape=[jax.ShapeDtypeStruct((R, C), F32)] * 3,
                          compiler_params=_params(1), name=name)(w, g, m, v)


def _pack(pieces):
    rows = []
    for p in pieces:
        flat = p.reshape(-1)
        pad = (-flat.shape[0]) % LANES
        if pad:
            flat = jnp.concatenate([flat, jnp.zeros((pad,), F32)])
        rows.append(flat.reshape(-1, LANES))
    total = sum(r.shape[0] for r in rows)
    pad_rows = (-total) % 8
    if pad_rows:
        rows.append(jnp.zeros((pad_rows, LANES), F32))
    return jnp.concatenate(rows, axis=0)


def _unpack(buf, shapes):
    out, r0 = [], 0
    for shp in shapes:
        n = int(np.prod(shp))
        nr = -(-n // LANES)
        out.append(buf[r0:r0 + nr].reshape(-1)[:n].reshape(shp))
        r0 += nr
    return out


SMALL_NAMES = ("rel_table", "b_in", "conv_w", "conv_b", "conv_ln_g", "conv_ln_b", "attn_norm_g", "conv_norm_g",
               "ln1_g", "ln1_b", "ffn_conv_w", "ffn_conv_b", "ln2_g", "ln2_b")
BIG_NAMES = ("w_in", "w_out", "w_up", "w_down")
WEIGHT_ORDER = ("rel_table", "w_in", "b_in", "conv_w", "conv_b", "conv_ln_g", "conv_ln_b", "attn_norm_g",
                "conv_norm_g", "w_out", "ln1_g", "ln1_b", "w_up", "ffn_conv_w", "ffn_conv_b", "w_down",
                "ln2_g", "ln2_b")


def kernel(x, rel_table, w_in, b_in, conv_w, conv_b, conv_ln_g, conv_ln_b, attn_norm_g, conv_norm_g, w_out, ln1_g, ln1_b, w_up, ffn_conv_w, ffn_conv_b, w_down, ln2_g, ln2_b, loss_target, m_rel_table, m_w_in, m_b_in, m_conv_w, m_conv_b, m_conv_ln_g, m_conv_ln_b, m_attn_norm_g, m_conv_norm_g, m_w_out, m_ln1_g, m_ln1_b, m_w_up, m_ffn_conv_w, m_ffn_conv_b, m_w_down, m_ln2_g, m_ln2_b, v_rel_table, v_w_in, v_b_in, v_conv_w, v_conv_b, v_conv_ln_g, v_conv_ln_b, v_attn_norm_g, v_conv_norm_g, v_w_out, v_ln1_g, v_ln1_b, v_w_up, v_ffn_conv_w, v_ffn_conv_b, v_w_down, v_ln2_g, v_ln2_b):
    args = dict(locals())
    weights = {n: args[n] for n in WEIGHT_ORDER}
    moms = {n: args["m_" + n] for n in WEIGHT_ORDER}
    vels = {n: args["v_" + n] for n in WEIGHT_ORDER}
    xi, yi, ci = _place()
    ids = jnp.stack([xi, yi, ci]).astype(jnp.int32)
    shard = 2 * xi + yi
    D = x.shape[-1]
    DFF = w_down.shape[1] * N_SHARDS
    CW = conv_norm_g.shape[-1]

    (g_in,), (g_cw, g_fcw) = _gather_weights([w_in[0]], [conv_w[0], ffn_conv_w[0]])
    cols = lambda t: jnp.transpose(t, (1, 0, 2)).reshape(t.shape[1], N_SHARDS * t.shape[2])
    staged = [_stage_half(w[0], ids, name="stage_" + n) for w, n in ((w_out, "w_out"), (w_up, "w_up"),
                                                                     (w_down, "w_down"))]

    grad_x, fulls, all_packs = _local_step(
        x, loss_target, rel_table, cols(g_in), b_in, cols(g_cw), conv_b, conv_ln_g, conv_ln_b, attn_norm_g,
        conv_norm_g, staged, ln1_g, ln1_b, cols(g_fcw), ffn_conv_b, ln2_g, ln2_b, ids)
    big_grads = dict(zip(BIG_NAMES, _sibling_assemble(fulls)))
    big_grads["w_up"] = _transpose(big_grads["w_up"], LANES, name="transpose_dw_up")

    summed = _sum_packs(all_packs)
    full_shapes = {n: weights[n].shape for n in SMALL_NAMES}
    full_shapes["conv_w"] = (1, CONV_KERNEL, CW)
    full_shapes["ffn_conv_w"] = (1, FFN_CONV_KERNEL, 2 * DFF)
    un = _unpack(summed, [(1, LANES)] + [full_shapes[n] for n in SMALL_NAMES])
    loss = un[0][0, 0]
    small_grads = dict(zip(SMALL_NAMES, un[1:]))
    for n in ("conv_w", "ffn_conv_w"):
        width = weights[n].shape[-1]
        small_grads[n] = lax.dynamic_slice_in_dim(small_grads[n], shard * width, width, axis=2)

    grads, delta, new_m, new_v = {}, {}, {}, {}
    for n in BIG_NAMES:
        shp = weights[n].shape
        g2 = big_grads[n]
        d, nm, nv = _adamw(weights[n][0], g2, moms[n][0], vels[n][0], name="adamw_" + n)
        grads[n], delta[n], new_m[n], new_v[n] = (t.reshape(shp) for t in (g2, d, nm, nv))
    sp = lambda src: _pack([src[n] for n in SMALL_NAMES])
    d_s, nm_s, nv_s = _adamw(sp(weights), sp(small_grads), sp(moms), sp(vels), name="adamw_small")
    shapes = [weights[n].shape for n in SMALL_NAMES]
    for tgt, buf in ((delta, d_s), (new_m, nm_s), (new_v, nv_s)):
        tgt.update(zip(SMALL_NAMES, _unpack(buf, shapes)))
    grads.update(small_grads)

    return (loss, grad_x, *[grads[n] for n in WEIGHT_ORDER], *[delta[n] for n in WEIGHT_ORDER],
            *[new_m[n] for n in WEIGHT_ORDER], *[new_v[n] for n in WEIGHT_ORDER])
```

```python
import functools
import math

import numpy as np
import jax
import jax.numpy as jnp
from jax import lax
from jax.experimental import pallas as pl
from jax.experimental.pallas import tpu as pltpu

F32 = jnp.float32
BF16 = jnp.bfloat16
MESH = pl.DeviceIdType.MESH

HEAD_DIM = 64
LANES = 128
ATTN_BLOCK = 128
DILATED_CONFIGS = ((128, 1), (512, 4), (2048, 16))
CONV_KERNEL = 31
FFN_CONV_KERNEL = 3
REL_BUCKETS = 32
REL_MAX_DIST = 2048
DEPTH = 1
ALPHA = (2 * DEPTH) ** 0.25
LN_EPS = 1e-5
NEG_INF = -1e30
QK_SCALE = 1.0 / math.sqrt(HEAD_DIM)
ADAM_LR = 0.001
ADAM_B1 = 0.9
ADAM_B2 = 0.999
ADAM_EPS = 1e-08
ADAM_WD = 0.01
ADAM_STEP = 10
VMEM_LIMIT = 52 * 1024 * 1024
FFN_COLS = 128
N_SHARDS = 4
N_DEV = 8


def _params(n_axes):
    return pltpu.CompilerParams(dimension_semantics=("arbitrary",) * n_axes,
                                vmem_limit_bytes=VMEM_LIMIT)


MM_DIMS = {"nn": (((1,), (0,)), ((), ())), "nt": (((1,), (1,)), ((), ())), "tn": (((0,), (0,)), ((), ()))}


class _Background:
    def __init__(self, in_arrays, out_shapes, aliases, n_sems, run, n_local=1):
        self.in_arrays, self.out_shapes, self.aliases = list(in_arrays), list(out_shapes), dict(aliases)
        self.n_sems, self.n_local, self.run = n_sems, n_local, run

    def scratch(self):
        return [pltpu.SemaphoreType.DMA((self.n_sems,)), pltpu.SemaphoreType.DMA((self.n_sems,)),
                pltpu.SemaphoreType.DMA((self.n_local,))]


def _hosted_call(body, bg, *, grid, in_specs, out_specs, out_shape, scratch_shapes, operands, name):
    n_in, n_out, n_scr = len(in_specs), len(out_specs), len(scratch_shapes)
    if bg is None:
        return pl.pallas_call(lambda *refs: body(refs, lambda post: None), grid=grid, in_specs=in_specs,
                              out_specs=out_specs, out_shape=out_shape, scratch_shapes=scratch_shapes,
                              compiler_params=_params(len(grid)), name=name)(*operands)
    nb_in, nb_out = len(bg.in_arrays), len(bg.out_shapes)
    n_steps = int(np.prod(grid))

    def full_body(*refs):
        own = refs[:n_in] + refs[n_in + nb_in:n_in + nb_in + n_out] \
            + refs[n_in + nb_in + n_out + nb_out:n_in + nb_in + n_out + nb_out + n_scr]
        bg_in = refs[n_in:n_in + nb_in]
        bg_out = refs[n_in + nb_in + n_out:n_in + nb_in + n_out + nb_out]
        sems = refs[n_in + nb_in + n_out + nb_out + n_scr:]
        step = pl.program_id(0)
        for ax in range(1, len(grid)):
            step = step * grid[ax] + pl.program_id(ax)

        def hook(post):
            bg.run(step, n_steps, bg_in, bg_out, *sems, post)

        body(own, hook)

    res = pl.pallas_call(
        full_body, grid=grid, in_specs=list(in_specs) + [HBM_SPEC] * nb_in,
        out_specs=list(out_specs) + [HBM_SPEC] * nb_out, out_shape=list(out_shape) + bg.out_shapes,
        input_output_aliases={n_in + a: n_out + o for a, o in bg.aliases.items()},
        scratch_shapes=list(scratch_shapes) + bg.scratch(), compiler_params=_params(len(grid)), name=name,
    )(*operands, *bg.in_arrays)
    return res


def _matmul_general(ins, part_fn, *, grid, tm, tn, outs, epilogue, extras=(), name, bg=None):
    nk = grid[2]
    n_in, n_extra = len(ins), len(extras)

    def body(refs, bg_hook):
        in_refs = refs[:n_in]
        rest = refs[n_in:]
        extra_refs = rest[:n_extra]
        out_refs = rest[n_extra:n_extra + len(outs)]
        acc_ref = rest[-1]
        i, j, k = pl.program_id(0), pl.program_id(1), pl.program_id(2)
        bg_hook(False)
        part = part_fn(in_refs, i, j, k)
        if nk == 1:
            epilogue(part, i, j, extra_refs, out_refs)
        else:
            @pl.when(k == 0)
            def _():
                acc_ref[...] = part

            @pl.when(k > 0)
            def _():
                acc_ref[...] += part

            @pl.when(k == nk - 1)
            def _():
                epilogue(acc_ref[...], i, j, extra_refs, out_refs)
        bg_hook(True)

    in_specs = [pl.BlockSpec(bs, im) for (_, bs, im) in list(ins) + list(extras)]
    out_specs = [pl.BlockSpec(bs, im) for (_, _, bs, im) in outs]
    out_shape = [jax.ShapeDtypeStruct(s, d) for (s, d, _, _) in outs]
    return _hosted_call(body, bg, grid=grid, in_specs=in_specs, out_specs=out_specs, out_shape=out_shape,
                        scratch_shapes=[pltpu.VMEM((tm, tn), F32)],
                        operands=[e[0] for e in ins] + [e[0] for e in extras], name=name)


def _dot(a, b, mode):
    return lax.dot_general(a.astype(BF16), b.astype(BF16), MM_DIMS[mode], preferred_element_type=F32)


def _matmul(a, b, *, mode, tm, tn, tk, outs, epilogue, extras=(), name, bg=None):
    if mode == "tn":
        K, M = a.shape
        N = b.shape[1]
        ins = [(a, (tk, tm), lambda i, j, k: (k, i)), (b, (tk, tn), lambda i, j, k: (k, j))]
    elif mode == "nt":
        M, K = a.shape
        N = b.shape[0]
        ins = [(a, (tm, tk), lambda i, j, k: (i, k)), (b, (tn, tk), lambda i, j, k: (j, k))]
    else:
        M, K = a.shape
        N = b.shape[1]
        ins = [(a, (tm, tk), lambda i, j, k: (i, k)), (b, (tk, tn), lambda i, j, k: (k, j))]
    assert M % tm == 0 and N % tn == 0 and K % tk == 0, (name, M, N, K, tm, tn, tk)

    def part_fn(in_refs, i, j, k):
        return _dot(in_refs[0][...], in_refs[1][...], mode)

    return _matmul_general(ins, part_fn, grid=(M // tm, N // tn, K // tk), tm=tm, tn=tn, outs=outs,
                           epilogue=epilogue, extras=extras, name=name, bg=bg)


def _plain_out(M, N, tm, tn, dtype):
    return ((M, N), dtype, (tm, tn), lambda i, j, k: (i, j))


def _mm_plain(a, b, *, mode, tm, tn, tk, out_dtype, name, bias=None, bg=None):
    if mode == "tn":
        M, N = a.shape[1], b.shape[1]
    elif mode == "nt":
        M, N = a.shape[0], b.shape[0]
    else:
        M, N = a.shape[0], b.shape[1]
    extras = []
    if bias is not None:
        extras.append((bias, (1, tn), lambda i, j, k: (0, j)))

    def epilogue(acc, i, j, extra_refs, out_refs):
        if bias is not None:
            acc = acc + extra_refs[0][...]
        out_refs[0][...] = acc.astype(out_dtype)

    res = _matmul(a, b, mode=mode, tm=tm, tn=tn, tk=tk, outs=[_plain_out(M, N, tm, tn, out_dtype)],
                  epilogue=epilogue, extras=extras, name=name, bg=bg)
    return res[0] if bg is None else res


def _row_tile(T, want):
    t = min(T, want)
    while T % t:
        t //= 2
    return t


def _col_tile(N, want):
    if N <= want:
        return N
    best = None
    for c in range(LANES, want + 1, LANES):
        if N % c == 0:
            best = c
    return best if best is not None else N


def _accumulate(ref, first, val):
    @pl.when(first)
    def _():
        ref[...] = val

    @pl.when(jnp.logical_not(first))
    def _():
        ref[...] += val


def _ln_fwd(z, g, b):
    mu = jnp.mean(z, axis=-1, keepdims=True)
    zc = z - mu
    var = jnp.mean(zc * zc, axis=-1, keepdims=True)
    r = lax.rsqrt(var + LN_EPS)
    xh = zc * r
    return xh * g + b, xh, r


def _ln_bwd(dy, xh, r, g):
    dxh = dy * g
    m1 = jnp.mean(dxh, axis=-1, keepdims=True)
    m2 = jnp.mean(dxh * xh, axis=-1, keepdims=True)
    return r * (dxh - m1 - xh * m2)


def _sigmoid(x):
    return 1.0 / (1.0 + jnp.exp(-x))


def _shift_down(x, s, row):
    if s == 0:
        return x
    rolled = pltpu.roll(x, s, 0)
    nfix = -(-s // 8) * 8
    head = jnp.where(row[:nfix] >= s, rolled[:nfix], 0.0)
    return jnp.concatenate([head, rolled[nfix:]], axis=0)


def _shift_up(x, s, row):
    if s == 0:
        return x
    n = x.shape[0]
    rolled = pltpu.roll(x, n - s, 0)
    nfix = -(-s // 8) * 8
    tail = jnp.where(row[n - nfix:] < n - s, rolled[n - nfix:], 0.0)
    return jnp.concatenate([rolled[:n - nfix], tail], axis=0)


def _bucket_tables():
    exact = REL_BUCKETS // 2
    qi = np.arange(ATTN_BLOCK)[:, None]
    kj = np.arange(2 * ATTN_BLOCK)[None, :]
    steps = qi + ATTN_BLOCK - kj
    buckets, masks = [], []
    for window, dilation in DILATED_CONFIGS:
        max_steps = window // dilation
        band = (steps >= 0) & (steps <= max_steps)
        dist = np.maximum(steps, 0) * dilation
        d_f = np.maximum(dist, 1).astype(np.float32)
        large = exact + (np.log(d_f / np.float32(exact)) / np.float32(math.log(REL_MAX_DIST / exact))
                         * np.float32(REL_BUCKETS - exact)).astype(np.int32)
        large = np.minimum(large, REL_BUCKETS - 1)
        bucket = np.where(dist < exact, dist, large).astype(np.int32)
        buckets.append(bucket.reshape(1, -1))
        masks.append(np.where(band, 0.0, NEG_INF).astype(np.float32).reshape(1, -1))
    return np.stack(buckets), np.stack(masks)


def _split_hi_lo(x):
    hi = x.astype(BF16)
    lo = (x - hi.astype(F32)).astype(BF16)
    return hi, lo


def _bias_build(rel_table_t, bucket, mask):
    H = rel_table_t.shape[0]
    n = bucket.shape[-1]

    def body(t_ref, bkt_ref, mask_ref, o_ref):
        onehot = (lax.broadcasted_iota(jnp.int32, (REL_BUCKETS, n), 0) == bkt_ref[0]).astype(BF16)
        t = t_ref[...]
        t1 = t.astype(BF16)
        r1 = t - t1.astype(F32)
        t2 = r1.astype(BF16)
        t3 = (r1 - t2.astype(F32)).astype(BF16)
        acc = jnp.dot(t1, onehot, preferred_element_type=F32)
        acc = acc + jnp.dot(t2, onehot, preferred_element_type=F32)
        acc = acc + jnp.dot(t3, onehot, preferred_element_type=F32)
        o_ref[0] = acc + mask_ref[0]

    return pl.pallas_call(
        body, grid=(3,),
        in_specs=[pl.BlockSpec((H, REL_BUCKETS), lambda b: (0, 0)),
                  pl.BlockSpec((1, 1, n), lambda b: (b, 0, 0)),
                  pl.BlockSpec((1, 1, n), lambda b: (b, 0, 0))],
        out_specs=pl.BlockSpec((1, H, n), lambda b: (b, 0, 0)),
        out_shape=jax.ShapeDtypeStruct((3, H, n), F32),
        compiler_params=_params(1), name="bias_build",
    )(rel_table_t, bucket, mask)


def _rel_grad(dbias, bucket):
    H = dbias.shape[1]
    n = bucket.shape[-1]
    dims = (((1,), (1,)), ((), ()))

    def body(d_ref, bkt_ref, o_ref):
        b = pl.program_id(0)
        onehot = (lax.broadcasted_iota(jnp.int32, (REL_BUCKETS, n), 0) == bkt_ref[0]).astype(BF16)
        d = d_ref[0]
        d1 = d.astype(BF16)
        r1 = d - d1.astype(F32)
        d2 = r1.astype(BF16)
        d3 = (r1 - d2.astype(F32)).astype(BF16)
        acc = lax.dot_general(d1, onehot, dims, preferred_element_type=F32)
        acc = acc + lax.dot_general(d2, onehot, dims, preferred_element_type=F32)
        acc = acc + lax.dot_general(d3, onehot, dims, preferred_element_type=F32)
        _accumulate(o_ref, b == 0, acc)

    return pl.pallas_call(
        body, grid=(3,),
        in_specs=[pl.BlockSpec((1, H, n), lambda b: (b, 0, 0)),
                  pl.BlockSpec((1, 1, n), lambda b: (b, 0, 0))],
        out_specs=pl.BlockSpec((H, REL_BUCKETS), lambda b: (0, 0)),
        out_shape=jax.ShapeDtypeStruct((H, REL_BUCKETS), F32),
        compiler_params=_params(1), name="rel_grad",
    )(dbias, bucket)


def _attn_specs(B, S, AW, d):
    L = S // d
    HP = AW // LANES
    W3 = 3 * HP
    q_spec = pl.BlockSpec((1, L, LANES), lambda h, b, r: (b, 0, r * W3 + h))
    k_spec = pl.BlockSpec((1, L, LANES), lambda h, b, r: (b, 0, r * W3 + HP + h))
    v_spec = pl.BlockSpec((1, L, LANES), lambda h, b, r: (b, 0, r * W3 + 2 * HP + h))
    o_spec = pl.BlockSpec((1, L, LANES), lambda h, b, r: (b, 0, r * HP + h))
    bias_spec = pl.BlockSpec((2, ATTN_BLOCK, 2 * ATTN_BLOCK), lambda h, b, r: (h, 0, 0))
    return L, HP, q_spec, k_spec, v_spec, o_spec, bias_spec


def _attn_fwd(qkv, bias, B, S, AW, d, name):
    L, HP, q_spec, k_spec, v_spec, o_spec, bias_spec = _attn_specs(B, S, AW, d)
    nb = L // ATTN_BLOCK
    nt = (((1,), (1,)), ((), ()))

    def body(q_ref, k_ref, v_ref, b_ref, o_ref, lse_ref):
        head0 = lax.broadcasted_iota(jnp.int32, (1, LANES), 1) < HEAD_DIM

        def block(n, first):
            qs = pl.multiple_of(n * ATTN_BLOCK, ATTN_BLOCK)
            q = q_ref[0, pl.ds(qs, ATTN_BLOCK), :]
            if first:
                kk = k_ref[0, pl.ds(0, ATTN_BLOCK), :]
                vv = v_ref[0, pl.ds(0, ATTN_BLOCK), :]
            else:
                ks = pl.multiple_of(n * ATTN_BLOCK - ATTN_BLOCK, ATTN_BLOCK)
                kk = k_ref[0, pl.ds(ks, 2 * ATTN_BLOCK), :]
                vv = v_ref[0, pl.ds(ks, 2 * ATTN_BLOCK), :]
            outs, lses = [], []
            for e in range(2):
                msk = head0 if e == 0 else jnp.logical_not(head0)
                qe = jnp.where(msk, q, jnp.zeros_like(q))
                s = lax.dot_general(qe, kk, nt, preferred_element_type=F32) * QK_SCALE
                s = s + (b_ref[e, :, ATTN_BLOCK:] if first else b_ref[e])
                m = jnp.max(s, axis=-1, keepdims=True)
                p = jnp.exp(s - m)
                l = jnp.sum(p, axis=-1, keepdims=True)
                o = jnp.dot(p.astype(BF16), vv, preferred_element_type=F32)
                outs.append(o / l)
                lses.append(jnp.broadcast_to(m + jnp.log(l), (ATTN_BLOCK, LANES)))
            o_ref[0, pl.ds(qs, ATTN_BLOCK), :] = jnp.where(head0, outs[0], outs[1])
            lse_ref[0, pl.ds(qs, ATTN_BLOCK), :] = jnp.where(head0, lses[0], lses[1])

        block(0, True)
        if nb > 1:
            def loop(n, c):
                block(n, False)
                return c
            lax.fori_loop(1, nb, loop, 0)

    qv = qkv.reshape(B, L, d * 3 * AW)
    o, lse = pl.pallas_call(
        body, grid=(HP, B, d), in_specs=[q_spec, k_spec, v_spec, bias_spec],
        out_specs=[o_spec, o_spec],
        out_shape=[jax.ShapeDtypeStruct((B, L, d * AW), F32)] * 2,
        compiler_params=_params(3), name=name,
    )(qv, qv, qv, bias)
    return o.reshape(B * S, AW), lse.reshape(B * S, AW)


def _attn_bwd(qkv, do, lse, dd, bias, B, S, AW, d, name):
    L, HP, q_spec, k_spec, v_spec, o_spec, bias_spec = _attn_specs(B, S, AW, d)
    nb = L // ATTN_BLOCK
    nt = (((1,), (1,)), ((), ()))
    tn = (((0,), (0,)), ((), ()))

    def body(q_ref, k_ref, v_ref, do_ref, lse_ref, dd_ref, b_ref, dq_ref, dk_ref, dv_ref, db_ref):
        head0 = lax.broadcasted_iota(jnp.int32, (1, LANES), 1) < HEAD_DIM
        first_step = jnp.logical_and(pl.program_id(1) == 0, pl.program_id(2) == 0)

        @pl.when(first_step)
        def _():
            db_ref[...] = jnp.zeros_like(db_ref)

        dk_ref[...] = jnp.zeros_like(dk_ref)
        dv_ref[...] = jnp.zeros_like(dv_ref)

        def block(n, first):
            qs = pl.multiple_of(n * ATTN_BLOCK, ATTN_BLOCK)
            nkeys = ATTN_BLOCK if first else 2 * ATTN_BLOCK
            ks = 0 if first else pl.multiple_of(n * ATTN_BLOCK - ATTN_BLOCK, ATTN_BLOCK)
            q = q_ref[0, pl.ds(qs, ATTN_BLOCK), :]
            kk = k_ref[0, pl.ds(ks, nkeys), :]
            vv = v_ref[0, pl.ds(ks, nkeys), :]
            dout = do_ref[0, pl.ds(qs, ATTN_BLOCK), :]
            lse_b = lse_ref[0, pl.ds(qs, ATTN_BLOCK), :]
            dd_b = dd_ref[0, pl.ds(qs, ATTN_BLOCK), :]
            dq = jnp.zeros((ATTN_BLOCK, LANES), F32)
            dkk = jnp.zeros((nkeys, LANES), F32)
            dvv = jnp.zeros((nkeys, LANES), F32)
            for e in range(2):
                msk = head0 if e == 0 else jnp.logical_not(head0)
                c0 = e * HEAD_DIM
                qe = jnp.where(msk, q, jnp.zeros_like(q))
                doe = jnp.where(msk, dout, jnp.zeros_like(dout))
                kke = jnp.where(msk, kk, jnp.zeros_like(kk))
                s = lax.dot_general(qe, kk, nt, preferred_element_type=F32) * QK_SCALE
                s = s + (b_ref[e, :, ATTN_BLOCK:] if first else b_ref[e])
                p = jnp.exp(s - lse_b[:, c0:c0 + 1])
                dp = lax.dot_general(doe, vv, nt, preferred_element_type=F32)
                ds = p * (dp - dd_b[:, c0:c0 + 1])
                if first:
                    db_ref[e, :, ATTN_BLOCK:] += ds
                else:
                    db_ref[e] += ds
                dsb = (ds * QK_SCALE).astype(BF16)
                dq = dq + jnp.dot(dsb, kke, preferred_element_type=F32)
                dkk = dkk + lax.dot_general(dsb, qe, tn, preferred_element_type=F32)
                dvv = dvv + lax.dot_general(p.astype(BF16), doe, tn, preferred_element_type=F32)
            dq_ref[0, pl.ds(qs, ATTN_BLOCK), :] = dq
            dk_ref[0, pl.ds(ks, nkeys), :] += dkk
            dv_ref[0, pl.ds(ks, nkeys), :] += dvv

        block(0, True)
        if nb > 1:
            def loop(n, c):
                block(n, False)
                return c
            lax.fori_loop(1, nb, loop, 0)

    H = AW // HEAD_DIM
    qv = qkv.reshape(B, L, d * 3 * AW)
    view = lambda t: t.reshape(B, L, d * AW)
    dq, dk, dv, db = pl.pallas_call(
        body, grid=(HP, B, d),
        in_specs=[q_spec, k_spec, v_spec, o_spec, o_spec, o_spec, bias_spec],
        out_specs=[o_spec, o_spec, o_spec, bias_spec],
        out_shape=[jax.ShapeDtypeStruct((B, L, d * AW), F32)] * 3
        + [jax.ShapeDtypeStruct((H, ATTN_BLOCK, 2 * ATTN_BLOCK), F32)],
        compiler_params=_params(3), name=name,
    )(qv, qv, qv, view(do), view(lse), view(dd), bias)
    flat = lambda t: t.reshape(B * S, AW)
    return flat(dq), flat(dk), flat(dv), db


def _attn_combine(ons, lses, gain, tm):
    T, AW = ons[0].shape

    def body(o1, o2, o3, l1, l2, l3, g_ref, attn_ref, lse_ref, mix_ref, r_ref):
        la, lb, lc = l1[...], l2[...], l3[...]
        m = jnp.maximum(jnp.maximum(la, lb), lc)
        ea, eb, ec = jnp.exp(la - m), jnp.exp(lb - m), jnp.exp(lc - m)
        den = ea + eb + ec
        attn = (ea * o1[...] + eb * o2[...] + ec * o3[...]) / den
        attn_ref[...] = attn
        lse_ref[...] = m + jnp.log(den)
        r = lax.rsqrt(jnp.mean(attn * attn, axis=-1, keepdims=True) + LN_EPS)
        mix_ref[...] = (attn * r * g_ref[...]).astype(BF16)
        r_ref[...] = jnp.broadcast_to(r, (tm, LANES))

    row = pl.BlockSpec((tm, AW), lambda i: (i, 0))
    return pl.pallas_call(
        body, grid=(T // tm,),
        in_specs=[row] * 6 + [pl.BlockSpec((1, AW), lambda i: (0, 0))],
        out_specs=[row, row, row, pl.BlockSpec((tm, LANES), lambda i: (i, 0))],
        out_shape=[jax.ShapeDtypeStruct((T, AW), F32), jax.ShapeDtypeStruct((T, AW), F32),
                   jax.ShapeDtypeStruct((T, AW), BF16), jax.ShapeDtypeStruct((T, LANES), F32)],
        compiler_params=_params(1), name="attn_combine",
    )(*ons, *lses, gain)


def _to_sub(src_ref, stage_ref, dsts, S):
    stage_ref[...] = src_ref[0].astype(F32)
    for (_, d), dst in zip(DILATED_CONFIGS[1:], dsts):
        L = S // d
        for r in range(d):
            dst[r * L:(r + 1) * L, :] = stage_ref[pl.ds(r, L, stride=d), :].astype(dst.dtype)


def _branch_blocks(S, d, block):
    nb = S // d // ATTN_BLOCK
    inner_unroll = 3 if (nb - 1) % 3 == 0 else 1

    def per_residue(r, c):
        block(r * nb, True)
        if nb > 1:
            def inner(n, c2):
                block(r * nb + n, False)
                return c2
            lax.fori_loop(1, nb, inner, 0, unroll=inner_unroll)
        return c

    lax.fori_loop(0, d, per_residue, 0, unroll=4 if nb == 1 else 1)


def _attention_fwd(qkv, bias_all, B, S, AW):
    HP = AW // LANES
    nt = MM_DIMS["nt"]

    def body(q_ref, k_ref, v_ref, b_ref, o_ref, lse_ref, stage, q4, q16, k4, k16, v4, v16, o1, l1, o4, l4, o16, l16):
        head0 = lax.broadcasted_iota(jnp.int32, (1, LANES), 1) < HEAD_DIM
        _to_sub(q_ref, stage, (q4, q16), S)
        _to_sub(k_ref, stage, (k4, k16), S)
        _to_sub(v_ref, stage, (v4, v16), S)
        srcs = ((q_ref.at[0], k_ref.at[0], v_ref.at[0], o1, l1), (q4, k4, v4, o4, l4), (q16, k16, v16, o16, l16))
        for bi, (_, d) in enumerate(DILATED_CONFIGS):
            qs_ref, ks_ref, vs_ref, od_ref, ld_ref = srcs[bi]

            def block(g, first, bi=bi, qs_ref=qs_ref, ks_ref=ks_ref, vs_ref=vs_ref, od_ref=od_ref, ld_ref=ld_ref):
                qs = pl.multiple_of(g * ATTN_BLOCK, ATTN_BLOCK)
                nkeys = ATTN_BLOCK if first else 2 * ATTN_BLOCK
                ks = qs if first else pl.multiple_of(qs - ATTN_BLOCK, ATTN_BLOCK)
                q = qs_ref[pl.ds(qs, ATTN_BLOCK), :]
                kk = ks_ref[pl.ds(ks, nkeys), :]
                vv = vs_ref[pl.ds(ks, nkeys), :]
                outs, lses = [], []
                for e in range(2):
                    msk = head0 if e == 0 else jnp.logical_not(head0)
                    qe = jnp.where(msk, q * QK_SCALE, jnp.zeros_like(q))
                    s = lax.dot_general(qe, kk, nt, preferred_element_type=F32)
                    s = s + (b_ref[bi, e, :, ATTN_BLOCK:] if first else b_ref[bi, e])
                    m = jnp.max(s, axis=-1, keepdims=True)
                    p = jnp.exp(s - m)
                    l = jnp.sum(p, axis=-1, keepdims=True)
                    o = jnp.dot(p.astype(BF16), vv, preferred_element_type=F32)
                    outs.append(o / l)
                    lses.append(jnp.broadcast_to(m + jnp.log(l), (ATTN_BLOCK, LANES)))
                od_ref[pl.ds(qs, ATTN_BLOCK), :] = jnp.where(head0, outs[0], outs[1])
                ld_ref[pl.ds(qs, ATTN_BLOCK), :] = jnp.where(head0, lses[0], lses[1])

            _branch_blocks(S, d, block)

        def natural(sub_ref, d):
            L = S // d
            for r in range(d):
                stage[pl.ds(r, L, stride=d), :] = sub_ref[r * L:(r + 1) * L, :]
            return stage[...]

        la = l1[...]
        lb = natural(l4, 4)
        lc = natural(l16, 16)
        m = jnp.maximum(jnp.maximum(la, lb), lc)
        ea, eb, ec = jnp.exp(la - m), jnp.exp(lb - m), jnp.exp(lc - m)
        den = ea + eb + ec
        lse_ref[0] = m + jnp.log(den)
        acc = ea * o1[...]
        acc = acc + eb * natural(o4, 4)
        acc = acc + ec * natural(o16, 16)
        o_ref[0] = acc / den

    blk = lambda off: pl.BlockSpec((1, S, LANES), lambda b, h: (b, 0, off + h))
    qv = qkv.reshape(B, S, 3 * AW)
    sub_b = pltpu.VMEM((S, LANES), BF16)
    sub_f = pltpu.VMEM((S, LANES), F32)
    o, lse = pl.pallas_call(
        body, grid=(B, HP),
        in_specs=[blk(0), blk(HP), blk(2 * HP),
                  pl.BlockSpec((3, 2, ATTN_BLOCK, 2 * ATTN_BLOCK), lambda b, h: (0, h, 0, 0))],
        out_specs=[blk(0), blk(0)],
        out_shape=[jax.ShapeDtypeStruct((B, S, AW), F32)] * 2,
        scratch_shapes=[sub_f] + [sub_b] * 6 + [sub_f] * 6,
        compiler_params=_params(2), name="attention_fwd",
    )(qv, qv, qv, bias_all)
    return o.reshape(B * S, AW), lse.reshape(B * S, AW)


def _attention_bwd(qkv, do, lse, dd, bias_all, B, S, AW):
    HP = AW // LANES
    H = AW // HEAD_DIM
    nt, tn = MM_DIMS["nt"], MM_DIMS["tn"]

    def body(q_ref, k_ref, v_ref, do_ref, lse_ref, dd_ref, b_ref,
             dq_ref, dk_ref, dv_ref, csq_ref, csk_ref, csv_ref, db_ref,
             stage, q4, q16, k4, k16, v4, v16, g4, g16, l4, l16, d4, d16,
             aq1, ak1, av1, aq4, ak4, av4, aq16, ak16, av16):
        head0 = lax.broadcasted_iota(jnp.int32, (1, LANES), 1) < HEAD_DIM
        first_b = pl.program_id(1) == 0

        @pl.when(first_b)
        def _():
            db_ref[...] = jnp.zeros_like(db_ref)

        _to_sub(q_ref, stage, (q4, q16), S)
        _to_sub(k_ref, stage, (k4, k16), S)
        _to_sub(v_ref, stage, (v4, v16), S)
        _to_sub(do_ref, stage, (g4, g16), S)
        _to_sub(lse_ref, stage, (l4, l16), S)
        _to_sub(dd_ref, stage, (d4, d16), S)
        for acc in (ak1, av1, ak4, av4, ak16, av16):
            acc[...] = jnp.zeros_like(acc)
        srcs = ((q_ref.at[0], k_ref.at[0], v_ref.at[0], do_ref.at[0], lse_ref.at[0], dd_ref.at[0], aq1, ak1, av1),
                (q4, k4, v4, g4, l4, d4, aq4, ak4, av4), (q16, k16, v16, g16, l16, d16, aq16, ak16, av16))
        for bi, (_, d) in enumerate(DILATED_CONFIGS):
            def block(g, first, bi=bi, refs=srcs[bi]):
                qs_ref, ks_ref, vs_ref, gs_ref, ls_ref, ds_ref, aq, ak, av = refs
                qs = pl.multiple_of(g * ATTN_BLOCK, ATTN_BLOCK)
                nkeys = ATTN_BLOCK if first else 2 * ATTN_BLOCK
                ks = qs if first else pl.multiple_of(qs - ATTN_BLOCK, ATTN_BLOCK)
                q = qs_ref[pl.ds(qs, ATTN_BLOCK), :]
                kk = ks_ref[pl.ds(ks, nkeys), :]
                vv = vs_ref[pl.ds(ks, nkeys), :]
                dout = gs_ref[pl.ds(qs, ATTN_BLOCK), :]
                lse_b = ls_ref[pl.ds(qs, ATTN_BLOCK), :]
                dd_b = ds_ref[pl.ds(qs, ATTN_BLOCK), :]
                dq = jnp.zeros((ATTN_BLOCK, LANES), F32)
                dkk = jnp.zeros((nkeys, LANES), F32)
                dvv = jnp.zeros((nkeys, LANES), F32)
                for e in range(2):
                    msk = head0 if e == 0 else jnp.logical_not(head0)
                    c0 = e * HEAD_DIM
                    qe = jnp.where(msk, q * QK_SCALE, jnp.zeros_like(q))
                    doe = jnp.where(msk, dout, jnp.zeros_like(dout))
                    kke = jnp.where(msk, kk * QK_SCALE, jnp.zeros_like(kk))
                    s = lax.dot_general(qe, kk, nt, preferred_element_type=F32)
                    s = s + (b_ref[bi, e, :, ATTN_BLOCK:] if first else b_ref[bi, e])
                    p = jnp.exp(s - lse_b[:, c0:c0 + 1])
                    dp = lax.dot_general(doe, vv, nt, preferred_element_type=F32)
                    ds = p * (dp - dd_b[:, c0:c0 + 1])
                    if first:
                        db_ref[bi, e, :, ATTN_BLOCK:] += ds
                    else:
                        db_ref[bi, e] += ds
                    dsb = ds.astype(BF16)
                    dq = dq + jnp.dot(dsb, kke, preferred_element_type=F32)
                    dkk = dkk + lax.dot_general(dsb, qe, tn, preferred_element_type=F32)
                    dvv = dvv + lax.dot_general(p.astype(BF16), doe, tn, preferred_element_type=F32)
                aq[pl.ds(qs, ATTN_BLOCK), :] = dq
                ak[pl.ds(ks, nkeys), :] += dkk
                av[pl.ds(ks, nkeys), :] += dvv

            _branch_blocks(S, d, block)

        for a1, a4, a16, out_ref, cs_ref in ((aq1, aq4, aq16, dq_ref, csq_ref), (ak1, ak4, ak16, dk_ref, csk_ref),
                                             (av1, av4, av16, dv_ref, csv_ref)):
            stage[...] = a1[...]
            for d, sub in ((4, a4), (16, a16)):
                L = S // d
                for r in range(d):
                    stage[pl.ds(r, L, stride=d), :] += sub[r * L:(r + 1) * L, :]
            tot = stage[...]
            out_ref[0] = tot.astype(out_ref.dtype)
            _accumulate(cs_ref, first_b, jnp.sum(tot, axis=0, keepdims=True))

    blk = lambda off: pl.BlockSpec((1, S, LANES), lambda h, b: (b, 0, off + h))
    cs_spec = pl.BlockSpec((1, LANES), lambda h, b: (0, h))
    bias_spec = pl.BlockSpec((3, 2, ATTN_BLOCK, 2 * ATTN_BLOCK), lambda h, b: (0, h, 0, 0))
    qv = qkv.reshape(B, S, 3 * AW)
    view = lambda t: t.reshape(B, S, AW)
    sub_b = pltpu.VMEM((S, LANES), BF16)
    sub_f = pltpu.VMEM((S, LANES), F32)
    res = pl.pallas_call(
        body, grid=(HP, B),
        in_specs=[blk(0), blk(HP), blk(2 * HP), blk(0), blk(0), blk(0), bias_spec],
        out_specs=[blk(0), blk(0), blk(0), cs_spec, cs_spec, cs_spec, bias_spec],
        out_shape=[jax.ShapeDtypeStruct((B, S, AW), BF16)] * 3 + [jax.ShapeDtypeStruct((1, AW), F32)] * 3
        + [jax.ShapeDtypeStruct((3, H, ATTN_BLOCK, 2 * ATTN_BLOCK), F32)],
        scratch_shapes=[sub_f] + [sub_b] * 8 + [sub_f] * 4 + [sub_f] * 9,
        compiler_params=_params(2), name="attention_bwd",
    )(qv, qv, qv, view(do), view(lse), view(dd), bias_all)
    flat = lambda t: t.reshape(B * S, AW)
    return flat(res[0]), flat(res[1]), flat(res[2]), res[3], res[4], res[5], res[6]


def _regroup(src, stage, dst, d, S, off=0):
    if d == 1:
        dst[off:off + S, :] = src.astype(dst.dtype)
        return
    stage[...] = src.astype(F32)
    L = S // d
    for r in range(d):
        dst[off + r * L:off + (r + 1) * L, :] = stage[pl.ds(r, L, stride=d), :].astype(dst.dtype)


def _ungroup(sub_ref, off, nat_ref, d, S, add):
    L = S // d
    for r in range(d):
        rows = pl.ds(0, S) if d == 1 else pl.ds(r, L, stride=d)
        val = sub_ref[off + r * L:off + (r + 1) * L, :]
        if add:
            nat_ref[rows, :] += val
        else:
            nat_ref[rows, :] = val


def _branch_keys(ks, vs, S, nb, g_idx):
    blk3 = (S // ATTN_BLOCK, ATTN_BLOCK, LANES)
    kc3 = ks[ATTN_BLOCK:ATTN_BLOCK + S, :].reshape(blk3)
    vc3 = vs[ATTN_BLOCK:ATTN_BLOCK + S, :].reshape(blk3)
    if nb == 1:
        return kc3, vc3, None
    kk3 = jnp.concatenate([ks[0:S, :].reshape(blk3), kc3], axis=1)
    vv3 = jnp.concatenate([vs[0:S, :].reshape(blk3), vc3], axis=1)
    col = lax.broadcasted_iota(jnp.int32, (1, 1, 2 * ATTN_BLOCK), 2)
    dead = jnp.logical_and((g_idx & (nb - 1)) == 0, col < ATTN_BLOCK)
    return kk3, vv3, dead


def _branch_scores(qe, kk3, b_ref, bi, e, dead):
    s = jnp.einsum("gqe,gke->gqk", qe, kk3, preferred_element_type=F32)
    if dead is None:
        return s + b_ref[bi, e, :, ATTN_BLOCK:]
    return jnp.where(dead, NEG_INF, s + b_ref[bi, e])


def _attention_fwd(qkv, bias_all, B, S, AW, bg=None):
    HP = AW // LANES
    G = S // ATTN_BLOCK
    blk3 = (G, ATTN_BLOCK, LANES)

    def body(refs, bg_hook):
        q_ref, k_ref, v_ref, b_ref, o_ref, lse_ref, stage, qs, ks, vs, ot, lt, on0, on1, on2, ln0, ln1, ln2 = refs
        bg_hook(False)
        head0 = lax.broadcasted_iota(jnp.int32, (1, 1, LANES), 2) < HEAD_DIM
        g_idx = lax.broadcasted_iota(jnp.int32, (G, 1, 1), 0)
        ks[0:ATTN_BLOCK, :] = jnp.zeros((ATTN_BLOCK, LANES), BF16)
        vs[0:ATTN_BLOCK, :] = jnp.zeros((ATTN_BLOCK, LANES), BF16)
        nat_o, nat_l = (on0, on1, on2), (ln0, ln1, ln2)
        for bi, (_, d) in enumerate(DILATED_CONFIGS):
            nb = S // d // ATTN_BLOCK
            _regroup(q_ref[0], stage, qs, d, S)
            _regroup(k_ref[0], stage, ks, d, S, ATTN_BLOCK)
            _regroup(v_ref[0], stage, vs, d, S, ATTN_BLOCK)
            q3 = qs[...].reshape(blk3) * QK_SCALE
            kk3, vv3, dead = _branch_keys(ks, vs, S, nb, g_idx)
            outs, lses = [], []
            for e in range(2):
                msk = head0 if e == 0 else jnp.logical_not(head0)
                qe = jnp.where(msk, q3, jnp.zeros_like(q3))
                s = _branch_scores(qe, kk3, b_ref, bi, e, dead)
                m = jnp.max(s, axis=-1, keepdims=True)
                p = jnp.exp(s - m)
                l = jnp.sum(p, axis=-1, keepdims=True)
                o = jnp.einsum("gqk,gke->gqe", p.astype(BF16), vv3, preferred_element_type=F32)
                outs.append(o / l)
                lses.append(jnp.broadcast_to(m + jnp.log(l), blk3))
            ot[...] = jnp.where(head0, outs[0], outs[1]).reshape(S, LANES)
            lt[...] = jnp.where(head0, lses[0], lses[1]).reshape(S, LANES)
            _ungroup(ot, 0, nat_o[bi], d, S, add=False)
            _ungroup(lt, 0, nat_l[bi], d, S, add=False)

        la, lb, lc = ln0[...], ln1[...], ln2[...]
        m = jnp.maximum(jnp.maximum(la, lb), lc)
        ea, eb, ec = jnp.exp(la - m), jnp.exp(lb - m), jnp.exp(lc - m)
        den = ea + eb + ec
        lse_ref[0] = m + jnp.log(den)
        o_ref[0] = (ea * on0[...] + eb * on1[...] + ec * on2[...]) / den
        bg_hook(True)

    blk = lambda off: pl.BlockSpec((1, S, LANES), lambda b, h: (b, 0, off + h))
    qv = qkv.reshape(B, S, 3 * AW)
    sub_f = pltpu.VMEM((S, LANES), F32)
    pad_b = pltpu.VMEM((S + ATTN_BLOCK, LANES), BF16)
    res = _hosted_call(
        body, bg, grid=(B, HP),
        in_specs=[blk(0), blk(HP), blk(2 * HP),
                  pl.BlockSpec((3, 2, ATTN_BLOCK, 2 * ATTN_BLOCK), lambda b, h: (0, h, 0, 0))],
        out_specs=[blk(0), blk(0)],
        out_shape=[jax.ShapeDtypeStruct((B, S, AW), F32)] * 2,
        scratch_shapes=[sub_f, pltpu.VMEM((S, LANES), BF16), pad_b, pad_b] + [sub_f] * 8,
        operands=[qv, qv, qv, bias_all], name="attention_fwd")
    return (res[0].reshape(B * S, AW), res[1].reshape(B * S, AW)) + tuple(res[2:])


def _attention_bwd(qkv, do, lse, dd, bias_all, B, S, AW, bg=None):
    HP = AW // LANES
    H = AW // HEAD_DIM
    G = S // ATTN_BLOCK
    blk3 = (G, ATTN_BLOCK, LANES)
    PAD = ATTN_BLOCK

    def body(refs, bg_hook):
        (q_ref, k_ref, v_ref, do_ref, lse_ref, dd_ref, b_ref,
         dq_ref, dk_ref, dv_ref, csq_ref, csk_ref, csv_ref, db_ref,
         stage, qs, ks, vs, gs, ls, ds_, tq, tk, tv, accq, acck, accv) = refs
        bg_hook(False)
        head0 = lax.broadcasted_iota(jnp.int32, (1, 1, LANES), 2) < HEAD_DIM
        g_idx = lax.broadcasted_iota(jnp.int32, (G, 1, 1), 0)
        first_b = pl.program_id(1) == 0

        @pl.when(first_b)
        def _():
            db_ref[...] = jnp.zeros_like(db_ref)

        ks[0:PAD, :] = jnp.zeros((PAD, LANES), BF16)
        vs[0:PAD, :] = jnp.zeros((PAD, LANES), BF16)
        tk[0:PAD, :] = jnp.zeros((PAD, LANES), F32)
        tv[0:PAD, :] = jnp.zeros((PAD, LANES), F32)
        for bi, (_, d) in enumerate(DILATED_CONFIGS):
            nb = S // d // ATTN_BLOCK
            _regroup(q_ref[0], stage, qs, d, S)
            _regroup(k_ref[0], stage, ks, d, S, PAD)
            _regroup(v_ref[0], stage, vs, d, S, PAD)
            _regroup(do_ref[0], stage, gs, d, S)
            _regroup(lse_ref[0], stage, ls, d, S)
            _regroup(dd_ref[0], stage, ds_, d, S)
            q3 = qs[...].reshape(blk3) * QK_SCALE
            do3 = gs[...].reshape(blk3)
            lse3 = ls[...].reshape(blk3)
            dd3 = ds_[...].reshape(blk3)
            kk3, vv3, dead = _branch_keys(ks, vs, S, nb, g_idx)
            dq = jnp.zeros(blk3, F32)
            dkk = jnp.zeros(kk3.shape, F32)
            dvv = jnp.zeros(kk3.shape, F32)
            for e in range(2):
                msk = head0 if e == 0 else jnp.logical_not(head0)
                c0 = e * HEAD_DIM
                qe = jnp.where(msk, q3, jnp.zeros_like(q3))
                doe = jnp.where(msk, do3, jnp.zeros_like(do3))
                ke = jnp.where(msk, kk3 * QK_SCALE, jnp.zeros_like(kk3))
                s = _branch_scores(qe, kk3, b_ref, bi, e, dead)
                p = jnp.exp(s - lse3[:, :, c0:c0 + 1])
                dp = jnp.einsum("gqe,gke->gqk", doe, vv3, preferred_element_type=F32)
                dsc = p * (dp - dd3[:, :, c0:c0 + 1])
                if dead is None:
                    db_ref[bi, e, :, ATTN_BLOCK:] += jnp.sum(dsc, axis=0)
                else:
                    db_ref[bi, e] += jnp.sum(dsc, axis=0)
                dsb = dsc.astype(BF16)
                dq = dq + jnp.einsum("gqk,gke->gqe", dsb, ke, preferred_element_type=F32)
                dkk = dkk + jnp.einsum("gqk,gqe->gke", dsb, qe, preferred_element_type=F32)
                dvv = dvv + jnp.einsum("gqk,gqe->gke", p.astype(BF16), doe, preferred_element_type=F32)
            tq[...] = dq.reshape(S, LANES)
            if dead is None:
                tk[PAD:PAD + S, :] = dkk.reshape(S, LANES)
                tv[PAD:PAD + S, :] = dvv.reshape(S, LANES)
            else:
                tk[PAD:PAD + S, :] = dkk[:, ATTN_BLOCK:, :].reshape(S, LANES)
                tv[PAD:PAD + S, :] = dvv[:, ATTN_BLOCK:, :].reshape(S, LANES)
                tk[0:S, :] += dkk[:, :ATTN_BLOCK, :].reshape(S, LANES)
                tv[0:S, :] += dvv[:, :ATTN_BLOCK, :].reshape(S, LANES)
            _ungroup(tq, 0, accq, d, S, add=bi > 0)
            _ungroup(tk, PAD, acck, d, S, add=bi > 0)
            _ungroup(tv, PAD, accv, d, S, add=bi > 0)

        for acc, out_ref, cs_ref in ((accq, dq_ref, csq_ref), (acck, dk_ref, csk_ref), (accv, dv_ref, csv_ref)):
            tot = acc[...]
            out_ref[0] = tot.astype(out_ref.dtype)
            _accumulate(cs_ref, first_b, jnp.sum(tot, axis=0, keepdims=True))
        bg_hook(True)

    blk = lambda off: pl.BlockSpec((1, S, LANES), lambda h, b: (b, 0, off + h))
    cs_spec = pl.BlockSpec((1, LANES), lambda h, b: (0, h))
    bias_spec = pl.BlockSpec((3, 2, ATTN_BLOCK, 2 * ATTN_BLOCK), lambda h, b: (0, h, 0, 0))
    qv = qkv.reshape(B, S, 3 * AW)
    view = lambda t: t.reshape(B, S, AW)
    sub_b = pltpu.VMEM((S, LANES), BF16)
    sub_f = pltpu.VMEM((S, LANES), F32)
    pad_b = pltpu.VMEM((S + PAD, LANES), BF16)
    pad_f = pltpu.VMEM((S + PAD, LANES), F32)
    res = _hosted_call(
        body, bg, grid=(HP, B),
        in_specs=[blk(0), blk(HP), blk(2 * HP), blk(0), blk(0), blk(0), bias_spec],
        out_specs=[blk(0), blk(0), blk(0), cs_spec, cs_spec, cs_spec, bias_spec],
        out_shape=[jax.ShapeDtypeStruct((B, S, AW), BF16)] * 3 + [jax.ShapeDtypeStruct((1, AW), F32)] * 3
        + [jax.ShapeDtypeStruct((3, H, ATTN_BLOCK, 2 * ATTN_BLOCK), F32)],
        scratch_shapes=[sub_f, sub_b, pad_b, pad_b, sub_b, sub_f, sub_f, sub_f, pad_f, pad_f, sub_f, sub_f, sub_f],
        operands=[qv, qv, qv, view(do), view(lse), view(dd), bias_all], name="attention_bwd")
    flat = lambda t: t.reshape(B * S, AW)
    return (flat(res[0]), flat(res[1]), flat(res[2]), res[3], res[4], res[5], res[6]) + tuple(res[7:])


def _attn_norm(attn, gain, tm):
    T, AW = attn.shape

    def body(a_ref, g_ref, mix_ref, r_ref):
        a = a_ref[...]
        r = lax.rsqrt(jnp.mean(a * a, axis=-1, keepdims=True) + LN_EPS)
        mix_ref[...] = (a * r * g_ref[...]).astype(BF16)
        r_ref[...] = jnp.broadcast_to(r, (tm, LANES))

    row = pl.BlockSpec((tm, AW), lambda i: (i, 0))
    return pl.pallas_call(
        body, grid=(T // tm,), in_specs=[row, pl.BlockSpec((1, AW), lambda i: (0, 0))],
        out_specs=[row, pl.BlockSpec((tm, LANES), lambda i: (i, 0))],
        out_shape=[jax.ShapeDtypeStruct((T, AW), BF16), jax.ShapeDtypeStruct((T, LANES), F32)],
        compiler_params=_params(1), name="attn_norm",
    )(attn, gain)


def _attn_pre_bwd(dmixed, attn, rstd, gain, tm):
    T, AW = attn.shape
    ones_np = np.kron(np.eye(AW // HEAD_DIM, dtype=np.float32), np.ones((HEAD_DIM, HEAD_DIM), np.float32))
    ones_bd = jnp.asarray(ones_np, dtype=BF16)

    def body(dm_ref, a_ref, r_ref, g_ref, ones_ref, do_ref, dd_ref, dg_ref):
        i = pl.program_id(0)
        dm = dm_ref[...]
        a = a_ref[...]
        r = r_ref[:, 0:1]
        dxn = dm * g_ref[...]
        da = r * (dxn - a * (r * r) * jnp.mean(dxn * a, axis=-1, keepdims=True))
        do_ref[...] = da.astype(BF16)
        hi, lo = _split_hi_lo(da * a)
        dd_ref[...] = (jnp.dot(hi, ones_ref[...], preferred_element_type=F32)
                       + jnp.dot(lo, ones_ref[...], preferred_element_type=F32))
        _accumulate(dg_ref, i == 0, jnp.sum(dm * a * r, axis=0, keepdims=True))

    row = pl.BlockSpec((tm, AW), lambda i: (i, 0))
    vec = pl.BlockSpec((1, AW), lambda i: (0, 0))
    return pl.pallas_call(
        body, grid=(T // tm,),
        in_specs=[row, row, pl.BlockSpec((tm, LANES), lambda i: (i, 0)), vec,
                  pl.BlockSpec((AW, AW), lambda i: (0, 0))],
        out_specs=[row, row, vec],
        out_shape=[jax.ShapeDtypeStruct((T, AW), BF16), jax.ShapeDtypeStruct((T, AW), F32),
                   jax.ShapeDtypeStruct((1, AW), F32)],
        compiler_params=_params(1), name="attn_pre_bwd",
    )(dmixed, attn, rstd, gain, ones_bd)


def _conv_branch_fwd_math(a, g, w_ref, cb, lg, lb, row):
    sg = _sigmoid(g)
    u0 = a * sg
    uc = jnp.zeros_like(u0) + cb
    for k in range(CONV_KERNEL):
        uc = uc + w_ref[k:k + 1, :] * _shift_down(u0, CONV_KERNEL - 1 - k, row)
    ul, xh, r = _ln_fwd(uc, lg, lb)
    su = _sigmoid(ul)
    u = ul * su
    return sg, u0, ul, xh, r, su, u


def _conv_fwd(ag, conv_w, conv_b, ln_g, ln_b, norm_g, B, S, CW):
    def body(a_ref, g_ref, w_ref, cb_ref, lg_ref, lb_ref, ng_ref, o_ref):
        row = lax.broadcasted_iota(jnp.int32, (S, CW), 0)
        _, _, _, _, _, _, u = _conv_branch_fwd_math(a_ref[0], g_ref[0], w_ref, cb_ref[...], lg_ref[...],
                                                    lb_ref[...], row)
        rr = lax.rsqrt(jnp.mean(u * u, axis=-1, keepdims=True) + LN_EPS)
        o_ref[0] = (u * rr * ng_ref[...]).astype(BF16)

    vec = pl.BlockSpec((1, CW), lambda b: (0, 0))
    out = pl.pallas_call(
        body, grid=(B,),
        in_specs=[pl.BlockSpec((1, S, CW), lambda b: (b, 0, 0)), pl.BlockSpec((1, S, CW), lambda b: (b, 0, 1)),
                  pl.BlockSpec((CONV_KERNEL, CW), lambda b: (0, 0)), vec, vec, vec, vec],
        out_specs=pl.BlockSpec((1, S, CW), lambda b: (b, 0, 0)),
        out_shape=jax.ShapeDtypeStruct((B, S, CW), BF16),
        compiler_params=_params(1), name="conv_fwd",
    )(ag.reshape(B, S, 2 * CW), ag.reshape(B, S, 2 * CW), conv_w, conv_b, ln_g, ln_b, norm_g)
    return out.reshape(B * S, CW)


def _conv_bwd(ag, dmixed, conv_w, conv_b, ln_g, ln_b, norm_g, B, S, CW, D):
    AW = D - CW
    assert AW % CW == 0

    def body(a_ref, g_ref, dm_ref, w_ref, cb_ref, lg_ref, lb_ref, ng_ref,
             dag_ref, dw_ref, dcb_ref, dlg_ref, dlb_ref, dng_ref):
        b = pl.program_id(0)
        row = lax.broadcasted_iota(jnp.int32, (S, CW), 0)
        a, g = a_ref[0], g_ref[0]
        sg, u0, ul, xh, r, su, u = _conv_branch_fwd_math(a, g, w_ref, cb_ref[...], lg_ref[...], lb_ref[...], row)
        rr = lax.rsqrt(jnp.mean(u * u, axis=-1, keepdims=True) + LN_EPS)
        dm = dm_ref[0]
        dxn = dm * ng_ref[...]
        du = rr * (dxn - u * (rr * rr) * jnp.mean(dxn * u, axis=-1, keepdims=True))
        dul = du * su * (1.0 + ul * (1.0 - su))
        duc = _ln_bwd(dul, xh, r, lg_ref[...])
        first = b == 0
        _accumulate(dng_ref, first, jnp.sum(dm * u * rr, axis=0, keepdims=True))
        _accumulate(dlg_ref, first, jnp.sum(dul * xh, axis=0, keepdims=True))
        _accumulate(dlb_ref, first, jnp.sum(dul, axis=0, keepdims=True))
        _accumulate(dcb_ref, first, jnp.sum(duc, axis=0, keepdims=True))

        @pl.when(first)
        def _():
            dw_ref[...] = jnp.zeros_like(dw_ref)

        du0 = jnp.zeros_like(u0)
        for k in range(CONV_KERNEL):
            sh = CONV_KERNEL - 1 - k
            dw_ref[k:k + 1, :] += jnp.sum(duc * _shift_down(u0, sh, row), axis=0, keepdims=True)
            du0 = du0 + w_ref[k:k + 1, :] * _shift_up(duc, sh, row)
        dag_ref[0, :, :CW] = du0 * sg
        dag_ref[0, :, CW:] = du0 * a * sg * (1.0 - sg)

    vec = pl.BlockSpec((1, CW), lambda b: (0, 0))
    wspec = pl.BlockSpec((CONV_KERNEL, CW), lambda b: (0, 0))
    agv = ag.reshape(B, S, 2 * CW)
    res = pl.pallas_call(
        body, grid=(B,),
        in_specs=[pl.BlockSpec((1, S, CW), lambda b: (b, 0, 0)), pl.BlockSpec((1, S, CW), lambda b: (b, 0, 1)),
                  pl.BlockSpec((1, S, CW), lambda b: (b, 0, AW // CW)), wspec, vec, vec, vec, vec],
        out_specs=[pl.BlockSpec((1, S, 2 * CW), lambda b: (b, 0, 0)), wspec, vec, vec, vec, vec],
        out_shape=[jax.ShapeDtypeStruct((B, S, 2 * CW), F32), jax.ShapeDtypeStruct((CONV_KERNEL, CW), F32)]
        + [jax.ShapeDtypeStruct((1, CW), F32)] * 4,
        compiler_params=_params(1), name="conv_bwd",
    )(agv, agv, dmixed.reshape(B, S, D), conv_w, conv_b, ln_g, ln_b, norm_g)
    return (res[0].reshape(B * S, 2 * CW),) + tuple(res[1:])


def _ffn_conv(x, w_ref, bias, row):
    y = jnp.zeros_like(x) + bias
    for k in range(FFN_CONV_KERNEL):
        y = y + w_ref[k:k + 1, :] * _shift_down(x, FFN_CONV_KERNEL - 1 - k, row)
    return y


def _ffn_specs(S, tc, nj, order):
    pick = (lambda b, j: (b, j)) if order == "bj" else (lambda j, b: (b, j))
    act = lambda off: pl.BlockSpec((1, S, tc), lambda *g: (pick(*g)[0], 0, off + pick(*g)[1]))
    cw = lambda off: pl.BlockSpec((FFN_CONV_KERNEL, tc), lambda *g: (0, off + pick(*g)[1]))
    cb = lambda off: pl.BlockSpec((1, tc), lambda *g: (0, off + pick(*g)[1]))
    return act, cw, cb


def _ffn_act(upre, cw, cb, B, S, DFF):
    tc = FFN_COLS
    nj = DFF // tc

    def body(ug_ref, uv_ref, wg_ref, wv_ref, bg_ref, bv_ref, o_ref):
        row = lax.broadcasted_iota(jnp.int32, (S, tc), 0)
        gate = _ffn_conv(ug_ref[0], wg_ref, bg_ref[...], row)
        val = _ffn_conv(uv_ref[0], wv_ref, bv_ref[...], row)
        o_ref[0] = (gate * _sigmoid(gate) * val).astype(BF16)

    act, cws, cbs = _ffn_specs(S, tc, nj, "bj")
    uv = upre.reshape(B, S, 2 * DFF)
    out = pl.pallas_call(
        body, grid=(B, nj), in_specs=[act(0), act(nj), cws(0), cws(nj), cbs(0), cbs(nj)], out_specs=act(0),
        out_shape=jax.ShapeDtypeStruct((B, S, DFF), BF16), compiler_params=_params(2), name="ffn_act",
    )(uv, uv, cw, cw, cb, cb)
    return out.reshape(B * S, DFF)


def _ffn_bwd(upre, dact, cw, cb, B, S, DFF):
    tc = FFN_COLS
    nj = DFF // tc

    def body(ug_ref, uv_ref, da_ref, wg_ref, wv_ref, bg_ref, bv_ref, dug_ref, duv_ref, dwg_ref, dwv_ref,
             dbg_ref, dbv_ref):
        first = pl.program_id(1) == 0
        row = lax.broadcasted_iota(jnp.int32, (S, tc), 0)
        ug, uv = ug_ref[0], uv_ref[0]
        gate = _ffn_conv(ug, wg_ref, bg_ref[...], row)
        val = _ffn_conv(uv, wv_ref, bv_ref[...], row)
        sg = _sigmoid(gate)
        dact_b = da_ref[0]
        dgate = dact_b * val * sg * (1.0 + gate * (1.0 - sg))
        dval = dact_b * gate * sg
        for dup, u, w_ref, du_ref, dw_ref, db_ref in ((dgate, ug, wg_ref, dug_ref, dwg_ref, dbg_ref),
                                                      (dval, uv, wv_ref, duv_ref, dwv_ref, dbv_ref)):
            _accumulate(db_ref, first, jnp.sum(dup, axis=0, keepdims=True))

            @pl.when(first)
            def _(dw_ref=dw_ref):
                dw_ref[...] = jnp.zeros_like(dw_ref)

            dupre = jnp.zeros_like(dup)
            for k in range(FFN_CONV_KERNEL):
                sh = FFN_CONV_KERNEL - 1 - k
                dw_ref[k:k + 1, :] += jnp.sum(dup * _shift_down(u, sh, row), axis=0, keepdims=True)
                dupre = dupre + w_ref[k:k + 1, :] * _shift_up(dup, sh, row)
            du_ref[0] = dupre.astype(BF16)

    act, cws, cbs = _ffn_specs(S, tc, nj, "jb")
    uv = upre.reshape(B, S, 2 * DFF)
    res = pl.pallas_call(
        body, grid=(nj, B),
        in_specs=[act(0), act(nj), act(0), cws(0), cws(nj), cbs(0), cbs(nj)],
        out_specs=[act(0), act(0), cws(0), cws(0), cbs(0), cbs(0)],
        out_shape=[jax.ShapeDtypeStruct((B, S, DFF), BF16)] * 2
        + [jax.ShapeDtypeStruct((FFN_CONV_KERNEL, DFF), F32)] * 2 + [jax.ShapeDtypeStruct((1, DFF), F32)] * 2,
        compiler_params=_params(2), name="ffn_bwd",
    )(uv, uv, dact.reshape(B, S, DFF), cw, cw, cb, cb)
    flat = lambda t: t.reshape(B * S, DFF)
    return (flat(res[0]), flat(res[1]), jnp.concatenate([res[2], res[3]], axis=1),
            jnp.concatenate([res[4], res[5]], axis=1))


FFN_HALO = 16


def _half_sequences(S):
    if S < 8 * FFN_HALO:
        return [(0, S, 0, S)]
    h = S // 2
    return [(0, h + FFN_HALO, 0, h), (h - FFN_HALO, S, FFN_HALO, h)]


def _w_up_block_spec(w_up_sh, tc, off):
    _, D, cs = w_up_sh.shape
    assert cs % tc == 0
    bps = cs // tc
    return pl.BlockSpec((1, D, tc), lambda j: ((off + j) // bps, 0, (off + j) % bps))


def _ffn_fwd_fused(x1b, w_up_sh, cw, cb, B, S, DFF):
    tc = FFN_COLS
    nj = DFF // tc
    D = x1b.shape[1]

    def body(x_ref, wg_ref, wv_ref, cwg_ref, cwv_ref, cbg_ref, cbv_ref, o_ref):
        w = jnp.concatenate([wg_ref[0], wv_ref[0]], axis=1)
        row = lax.broadcasted_iota(jnp.int32, (S, tc), 0)
        for b in range(B):
            up = jnp.dot(x_ref[b], w, preferred_element_type=F32)
            gate = _ffn_conv(up[:, :tc], cwg_ref, cbg_ref[...], row)
            val = _ffn_conv(up[:, tc:], cwv_ref, cbv_ref[...], row)
            o_ref[b] = (gate * _sigmoid(gate) * val).astype(BF16)

    cws = lambda off: pl.BlockSpec((FFN_CONV_KERNEL, tc), lambda j: (0, off + j))
    cbs = lambda off: pl.BlockSpec((1, tc), lambda j: (0, off + j))
    out = pl.pallas_call(
        body, grid=(nj,),
        in_specs=[pl.BlockSpec((B, S, D), lambda j: (0, 0, 0), pipeline_mode=pl.Buffered(1)),
                  _w_up_block_spec(w_up_sh, tc, 0), _w_up_block_spec(w_up_sh, tc, nj),
                  cws(0), cws(nj), cbs(0), cbs(nj)],
        out_specs=pl.BlockSpec((B, S, tc), lambda j: (0, 0, j)), out_shape=jax.ShapeDtypeStruct((B, S, DFF), BF16),
        compiler_params=_params(1), name="ffn_fwd",
    )(x1b.reshape(B, S, D), w_up_sh, w_up_sh, cw, cw, cb, cb)
    return out.reshape(B * S, DFF)


def _ffn_bwd_fused(x1b, dz2b, w_up_sh, w_down, cw, cb, B, S, DFF):
    tc = FFN_COLS
    nj = DFF // tc
    D = x1b.shape[1]

    def body(x_ref, dz_ref, wg_ref, wv_ref, wd_ref, cwg_ref, cwv_ref, cbg_ref, cbv_ref,
             dug_ref, duv_ref, dwu_ref, dwd_ref, dcw_ref, dcb_ref):
        first = pl.program_id(1) == 0
        w = jnp.concatenate([wg_ref[0], wv_ref[0]], axis=1)
        dw_t = dwd = None
        dcb = [None, None]
        dcw = [[None] * FFN_CONV_KERNEL, [None] * FFN_CONV_KERNEL]
        add = lambda old, new: new if old is None else old + new
        for lo, hi, o0, on in _half_sequences(S):
            n = hi - lo
            own = slice(o0, o0 + on)
            row = lax.broadcasted_iota(jnp.int32, (n, tc), 0)
            x = x_ref[0, lo:hi, :]
            dz = dz_ref[0, lo:hi, :]
            up = jnp.dot(x, w, preferred_element_type=F32)
            ug, uv = up[:, :tc], up[:, tc:]
            gate = _ffn_conv(ug, cwg_ref, cbg_ref[...], row)
            val = _ffn_conv(uv, cwv_ref, cbv_ref[...], row)
            sg = _sigmoid(gate)
            act = (gate * sg * val).astype(BF16)
            dact = _dot(dz, wd_ref[...], "nt")
            dgate = dact * val * sg * (1.0 + gate * (1.0 - sg))
            dval = dact * gate * sg
            dupre = []
            for h, (dup, u, w_ref) in enumerate(((dgate, ug, cwg_ref), (dval, uv, cwv_ref))):
                dcb[h] = add(dcb[h], jnp.sum(dup[own], axis=0, keepdims=True))
                acc = jnp.zeros_like(dup)
                for k in range(FFN_CONV_KERNEL):
                    sh = FFN_CONV_KERNEL - 1 - k
                    dcw[h][k] = add(dcw[h][k], jnp.sum((dup * _shift_down(u, sh, row))[own], axis=0, keepdims=True))
                    acc = acc + w_ref[k:k + 1, :] * _shift_up(dup, sh, row)
                dupre.append(acc.astype(BF16)[own])
            dug_ref[0, lo + o0:lo + o0 + on, :] = dupre[0]
            duv_ref[0, lo + o0:lo + o0 + on, :] = dupre[1]
            dw_t = add(dw_t, _dot(jnp.concatenate(dupre, axis=1), x[own], "tn"))
            dwd = add(dwd, _dot(act[own], dz[own], "tn"))
        _accumulate(dwu_ref.at[0], first, dw_t[:tc])
        _accumulate(dwu_ref.at[1], first, dw_t[tc:])
        _accumulate(dwd_ref, first, dwd)
        for h in range(2):
            _accumulate(dcb_ref.at[h], first, dcb[h])
            for k in range(FFN_CONV_KERNEL):
                _accumulate(dcw_ref.at[k, pl.ds(h, 1), :], first, dcw[h][k])

    act_s, cws, cbs = _ffn_specs(S, tc, nj, "jb")
    seq = pl.BlockSpec((1, S, D), lambda j, b: (b, 0, 0))
    wspec = lambda off: pl.BlockSpec((1, D, tc), lambda j, b: ((off + j) // (w_up_sh.shape[2] // tc), 0,
                                                               (off + j) % (w_up_sh.shape[2] // tc)))
    res = pl.pallas_call(
        body, grid=(nj, B),
        in_specs=[seq, seq, wspec(0), wspec(nj),
                  pl.BlockSpec((tc, D), lambda j, b: (j, 0)), cws(0), cws(nj), cbs(0), cbs(nj)],
        out_specs=[act_s(0), act_s(0), pl.BlockSpec((2, tc, D), lambda j, b: (0, j, 0)),
                   pl.BlockSpec((tc, D), lambda j, b: (j, 0)),
                   pl.BlockSpec((FFN_CONV_KERNEL, 2, tc), lambda j, b: (0, 0, j)),
                   pl.BlockSpec((2, 1, tc), lambda j, b: (0, 0, j))],
        out_shape=[jax.ShapeDtypeStruct((B, S, DFF), BF16)] * 2
        + [jax.ShapeDtypeStruct((2, DFF, D), F32), jax.ShapeDtypeStruct((DFF, D), F32),
           jax.ShapeDtypeStruct((FFN_CONV_KERNEL, 2, DFF), F32), jax.ShapeDtypeStruct((2, 1, DFF), F32)],
        compiler_params=_params(2), name="ffn_bwd",
    )(x1b.reshape(B, S, D), dz2b.reshape(B, S, D), w_up_sh, w_up_sh, w_down, cw, cw, cb, cb)
    flat = lambda t: t.reshape(B * S, DFF)
    return flat(res[0]), flat(res[1]), res[2], res[3], res[4], res[5]


def _transpose(x, name):
    R, C = x.shape
    tr = LANES if R % LANES == 0 else R

    def body(x_ref, o_ref):
        o_ref[...] = x_ref[...].T

    return pl.pallas_call(
        body, grid=(R // tr,), in_specs=[pl.BlockSpec((tr, C), lambda i: (i, 0))],
        out_specs=pl.BlockSpec((C, tr), lambda i: (0, i)), out_shape=jax.ShapeDtypeStruct((C, R), F32),
        compiler_params=_params(1), name=name)(x)


def _dh_cat(dq, dk, dv, dag, tm):
    T, AW = dq.shape
    CW2 = dag.shape[1]
    W = 3 * AW + CW2

    def body(dq_ref, dk_ref, dv_ref, dag_ref, dh_ref, cs_ref):
        for c, ref in enumerate((dq_ref, dk_ref, dv_ref)):
            dh_ref[:, c * AW:(c + 1) * AW] = ref[...]
        dg = dag_ref[...]
        dh_ref[:, 3 * AW:] = dg.astype(BF16)
        _accumulate(cs_ref, pl.program_id(0) == 0, jnp.sum(dg, axis=0, keepdims=True))

    row = pl.BlockSpec((tm, AW), lambda i: (i, 0))
    return pl.pallas_call(
        body, grid=(T // tm,),
        in_specs=[row] * 3 + [pl.BlockSpec((tm, CW2), lambda i: (i, 0))],
        out_specs=[pl.BlockSpec((tm, W), lambda i: (i, 0)), pl.BlockSpec((1, CW2), lambda i: (0, 0))],
        out_shape=[jax.ShapeDtypeStruct((T, W), BF16), jax.ShapeDtypeStruct((1, CW2), F32)],
        compiler_params=_params(1), name="dh_cat",
    )(dq, dk, dv, dag)


def _local_step(x, target, rel_table, w_in, b_in, conv_w, conv_b, conv_ln_g, conv_ln_b, attn_norm_g,
                conv_norm_g, staged, ln1_g, ln1_b, ffn_cw, ffn_cb, ln2_g, ln2_b, ids):
    B, S, D = x.shape
    T = B * S
    AW = attn_norm_g.shape[-1]
    CW = conv_norm_g.shape[-1]
    H = AW // HEAD_DIM
    DFF = staged[2].shape[0] * staged[2].shape[1]
    INW = 3 * AW + 2 * CW
    xf = x.reshape(T, D)
    tf = target.reshape(T, D)
    tm = _row_tile(T, 512)
    tm_s = _row_tile(T, 256)

    bucket_np, mask_np = _bucket_tables()
    bucket = jnp.asarray(bucket_np)
    band_mask = jnp.asarray(mask_np)
    bias_all = _bias_build(rel_table.T, bucket, band_mask).reshape(3, H, ATTN_BLOCK, 2 * ATTN_BLOCK)

    tn_qkv = _col_tile(3 * AW, 1152)
    qkv = _mm_plain(xf, w_in[:, :3 * AW], mode="nn", tm=tm, tn=tn_qkv, tk=D, out_dtype=BF16,
                    bias=b_in[:, :3 * AW], name="mm_qkv")
    ag = _mm_plain(xf, w_in[:, 3 * AW:], mode="nn", tm=tm, tn=2 * CW, tk=D, out_dtype=F32,
                   bias=b_in[:, 3 * AW:], name="mm_ag")

    attn, lse, w_out_g, w_up_sh, w_down_g = _attention_fwd(qkv, bias_all, B, S, AW, bg=_bg_gather(staged))
    w_out = w_out_g.reshape(D, D)
    w_down = w_down_g.reshape(DFF, D)
    mixed_a, r_attn = _attn_norm(attn, attn_norm_g, tm_s)
    mixed_c = _conv_fwd(ag, conv_w, conv_b, conv_ln_g, conv_ln_b, conv_norm_g, B, S, CW)
    mixed = jnp.concatenate([mixed_a, mixed_c], axis=1)

    def ln1_epilogue(acc, i, j, extra_refs, out_refs):
        x_ref, g_ref, b_ref = extra_refs
        x1, xh, r = _ln_fwd(acc + ALPHA * x_ref[...], g_ref[...], b_ref[...])
        out_refs[0][...] = x1
        out_refs[1][...] = x1.astype(BF16)
        out_refs[2][...] = xh
        out_refs[3][...] = jnp.broadcast_to(r, (tm_s, LANES))

    rowD = lambda i, j, k: (i, 0)
    vecD = lambda i, j, k: (0, 0)
    x1, x1b, xh1, r1 = _matmul(
        mixed, w_out, mode="nn", tm=tm_s, tn=D, tk=D,
        extras=[(xf, (tm_s, D), rowD), (ln1_g, (1, D), vecD), (ln1_b, (1, D), vecD)],
        outs=[((T, D), F32, (tm_s, D), rowD), ((T, D), BF16, (tm_s, D), rowD), ((T, D), F32, (tm_s, D), rowD),
              ((T, LANES), F32, (tm_s, LANES), rowD)],
        epilogue=ln1_epilogue, name="mm_out_ln1")

    NS, _, cs = w_up_sh.shape
    half = NS // 2

    act = _ffn_fwd_fused(x1b, w_up_sh, ffn_cw, ffn_cb, B, S, DFF)

    def ln2_epilogue(acc, i, j, extra_refs, out_refs):
        x1_ref, g_ref, b_ref, t_ref = extra_refs
        dz_ref, dzb_ref, loss_ref, dg_ref, db_ref = out_refs
        g = g_ref[...]
        y, xh, r = _ln_fwd(acc + ALPHA * x1_ref[...], g, b_ref[...])
        diff = y - t_ref[...]
        row_loss = jnp.sum(diff * diff, axis=1, keepdims=True)
        tile_loss = jnp.sum(row_loss, axis=0, keepdims=True) * (0.5 / D)
        dy = diff * (1.0 / D)
        dz = _ln_bwd(dy, xh, r, g)
        dz_ref[...] = dz
        dzb_ref[...] = dz.astype(BF16)
        first = i == 0
        _accumulate(loss_ref, first, jnp.broadcast_to(tile_loss, (1, LANES)))
        _accumulate(dg_ref, first, jnp.sum(dy * xh, axis=0, keepdims=True))
        _accumulate(db_ref, first, jnp.sum(dy, axis=0, keepdims=True))

    dz2, dz2b, loss_part, d_ln2_g, d_ln2_b = _matmul(
        act, w_down, mode="nn", tm=tm_s, tn=D, tk=DFF,
        extras=[(x1, (tm_s, D), rowD), (ln2_g, (1, D), vecD), (ln2_b, (1, D), vecD), (tf, (tm_s, D), rowD)],
        outs=[((T, D), F32, (tm_s, D), rowD), ((T, D), BF16, (tm_s, D), rowD),
              ((1, LANES), F32, (1, LANES), vecD), ((1, D), F32, (1, D), vecD), ((1, D), F32, (1, D), vecD)],
        epilogue=ln2_epilogue, name="mm_down_ln2_loss")

    dupre_g, dupre_v, d_w_up_t, d_w_down, d_ffn_cw2, d_ffn_cb2 = _ffn_bwd_fused(
        x1b, dz2b, w_up_sh, w_down, ffn_cw, ffn_cb, B, S, DFF)
    d_w_up_t = d_w_up_t.reshape(NS, cs, D)
    d_ffn_cw = d_ffn_cw2.reshape(FFN_CONV_KERNEL, 2 * DFF)
    d_ffn_cb = d_ffn_cb2.reshape(1, 2 * DFF)
    tk_t = _row_tile(T, 512)

    def ln1_bwd_epilogue(acc, i, j, extra_refs, out_refs):
        dz2_ref, xh_ref, r_ref, g_ref = extra_refs
        dz_ref, dzb_ref, dg_ref, db_ref = out_refs
        dx1 = acc + ALPHA * dz2_ref[...]
        xh = xh_ref[...]
        dz = _ln_bwd(dx1, xh, r_ref[:, 0:1], g_ref[...])
        dz_ref[...] = dz
        dzb_ref[...] = dz.astype(BF16)
        first = i == 0
        _accumulate(dg_ref, first, jnp.sum(dx1 * xh, axis=0, keepdims=True))
        _accumulate(db_ref, first, jnp.sum(dx1, axis=0, keepdims=True))

    early = [d_w_up_t, d_w_down.reshape(NS, DFF // NS, D)]
    dz1, dz1b, d_ln1_g, d_ln1_b, *sib_e = _matmul_general(
        [(dupre_g, (tm, cs), lambda i, j, k: (i, jnp.minimum(k, half - 1))),
         (dupre_v, (tm, cs), lambda i, j, k: (i, jnp.maximum(k - half, 0))),
         (w_up_sh, (1, D, cs), lambda i, j, k: (k, 0, 0))],
        lambda refs, i, j, k: _dot(jnp.where(k < half, refs[0][...], refs[1][...]), refs[2][0], "nt"),
        grid=(T // tm, 1, NS), tm=tm, tn=D,
        extras=[(dz2, (tm, D), rowD), (xh1, (tm, D), rowD), (r1, (tm, LANES), rowD), (ln1_g, (1, D), vecD)],
        outs=[((T, D), F32, (tm, D), rowD), ((T, D), BF16, (tm, D), rowD),
              ((1, D), F32, (1, D), vecD), ((1, D), F32, (1, D), vecD)],
        epilogue=ln1_bwd_epilogue, name="mm_dx1_ln1_bwd", bg=_bg_sibling_exchange(early))
    chip_e = [_pair_sum(g, s, ids, name="pair_sum_" + n) for g, s, n in zip(early, sib_e, ("w_up", "w_down"))]

    d_w_out = _mm_plain(mixed, dz1b, mode="tn", tm=D, tn=D, tk=tk_t, out_dtype=F32, name="mm_dw_out")
    early.append(d_w_out.reshape(NS, D // NS, D))
    dmixed, sib_out = _mm_plain(dz1b, w_out, mode="nt", tm=tm, tn=D, tk=D, out_dtype=F32, name="mm_dmixed",
                                bg=_bg_sibling_exchange(early[2:]))
    sib_e.append(sib_out)
    chip_e.append(_pair_sum(early[2], sib_out, ids, name="pair_sum_w_out"))

    dattn, dd, d_attn_norm_g = _attn_pre_bwd(dmixed, attn, r_attn, attn_norm_g, tm_s)
    dag, d_conv_w, d_conv_b, d_conv_ln_g, d_conv_ln_b, d_conv_norm_g = _conv_bwd(
        ag, dmixed, conv_w, conv_b, conv_ln_g, conv_ln_b, conv_norm_g, B, S, CW, D)

    dq, dk, dv, csq, csk, csv, dbias, *got_e = _attention_bwd(qkv, dattn, lse, dd, bias_all, B, S, AW,
                                                              bg=_bg_chip_exchange(chip_e))
    full_up, full_down, full_out = [_final_sum(g, s, r, ids, name="final_sum_" + n)
                                    for g, s, r, n in zip(early, sib_e, got_e, ("w_up", "w_down", "w_out"))]
    d_rel_table = _rel_grad(dbias.reshape(3, H, ATTN_BLOCK * 2 * ATTN_BLOCK), bucket).T
    dh, cs_ag = _dh_cat(dq, dk, dv, dag, tm_s)
    d_b_in = jnp.concatenate([csq, csk, csv, cs_ag], axis=1)

    d_w_in_t = _mm_plain(dh, xf, mode="tn", tm=_col_tile(INW, 1408), tn=D, tk=tk_t, out_dtype=F32, name="mm_dw_in")
    late = [d_w_in_t.reshape(NS, INW // NS, D)]
    sib_l = _sibling_exchange(late)
    chip_l = [_pair_sum(late[0], sib_l[0], ids, name="pair_sum_w_in")]
    small = dict(rel_table=d_rel_table, b_in=d_b_in, conv_w=d_conv_w, conv_b=d_conv_b, conv_ln_g=d_conv_ln_g,
                 conv_ln_b=d_conv_ln_b, attn_norm_g=d_attn_norm_g, conv_norm_g=d_conv_norm_g, ln1_g=d_ln1_g,
                 ln1_b=d_ln1_b, ffn_conv_w=d_ffn_cw, ffn_conv_b=d_ffn_cb, ln2_g=d_ln2_g, ln2_b=d_ln2_b)
    pack = _pack([loss_part] + [small[n] for n in SMALL_NAMES])

    def gx_epilogue(acc, i, j, extra_refs, out_refs):
        out_refs[0][...] = acc + ALPHA * extra_refs[0][...]

    grad_x, got_in, all_packs = _matmul(
        dh, w_in, mode="nt", tm=tm_s, tn=D, tk=INW, extras=[(dz1, (tm_s, D), rowD)],
        outs=[((T, D), F32, (tm_s, D), rowD)], epilogue=gx_epilogue, name="mm_grad_x",
        bg=_bg_chip_exchange(chip_l, pack))
    full_in = _final_sum(late[0], sib_l[0], got_in, ids, name="final_sum_w_in")
    return grad_x.reshape(B, S, D), [full_in, full_out, full_up, full_down], all_packs


def _place():
    return lax.axis_index("x"), lax.axis_index("y"), lax.axis_index("c")


CHIP_FLIPS = ((1, 0), (0, 1), (1, 1))


def _flip(v, f):
    return 1 - v if f else v


HBM_SPEC = pl.BlockSpec(memory_space=pl.ANY)
VMEM_SPEC = pl.BlockSpec(memory_space=pltpu.VMEM)
COMM_PARAMS = pltpu.CompilerParams(vmem_limit_bytes=VMEM_LIMIT)


def _gather_weights(big, small):
    nb, ns = len(big), len(small)

    def body(*refs):
        big_in = refs[:nb]
        small_in = refs[nb:nb + ns]
        big_out = refs[nb + ns:2 * nb + ns]
        small_out = refs[2 * nb + ns:2 * nb + 2 * ns]
        stages = refs[2 * nb + 2 * ns:3 * nb + 2 * ns]
        send_sems, recv_sems, local_sems = refs[3 * nb + 2 * ns:]
        x, y, c = _place()
        s_me = 2 * x + y
        sibling = (x, y, 1 - c)
        started, local_copies = [], []
        for a in range(nb):
            rh = big[a].shape[0] // 2
            lo = pl.multiple_of(c * rh, 16)
            stages[a][...] = big_in[a][pl.ds(lo, rh), :].astype(BF16)
            mine = big_out[a].at[s_me, pl.ds(lo, rh), :]
            loc = pltpu.make_async_copy(stages[a], mine, local_sems.at[a])
            loc.start()
            local_copies.append(loc)
            targets = [sibling] + [(_flip(x, fx), _flip(y, fy), c) for fx, fy in CHIP_FLIPS]
            for k, to in enumerate(targets):
                cp = pltpu.make_async_remote_copy(stages[a], mine, send_sems.at[a * 7 + k],
                                                  recv_sems.at[a * 7 + k], device_id=to, device_id_type=MESH)
                cp.start()
                started.append(cp)
        for a in range(ns):
            mine = small_out[a].at[s_me]
            loc = pltpu.make_async_copy(small_in[a], mine, local_sems.at[nb + a])
            loc.start()
            local_copies.append(loc)
            for k, (fx, fy) in enumerate(CHIP_FLIPS):
                cp = pltpu.make_async_remote_copy(small_in[a], mine, send_sems.at[nb * 7 + a * 3 + k],
                                                  recv_sems.at[nb * 7 + a * 3 + k],
                                                  device_id=(_flip(x, fx), _flip(y, fy), c), device_id_type=MESH)
                cp.start()
                started.append(cp)
        for a in range(nb):
            rh = big[a].shape[0] // 2
            lo = pl.multiple_of(c * rh, 16)
            for k, (fx, fy) in enumerate(CHIP_FLIPS):
                s_from = 2 * _flip(x, fx) + _flip(y, fy)
                got = big_out[a].at[s_from, pl.ds(lo, rh), :]
                pltpu.make_async_remote_copy(got, got, send_sems.at[a * 7 + 1 + k], recv_sems.at[a * 7 + 1 + k],
                                             device_id=sibling, device_id_type=MESH).wait_recv()
                fwd = pltpu.make_async_remote_copy(got, got, send_sems.at[a * 7 + 4 + k],
                                                   recv_sems.at[a * 7 + 4 + k], device_id=sibling,
                                                   device_id_type=MESH)
                fwd.start()
                started.append(fwd)
        for a in range(nb):
            rh = big[a].shape[0] // 2
            lo_sib = pl.multiple_of((1 - c) * rh, 16)
            for k in (0, 4, 5, 6):
                any_rows = big_out[a].at[s_me, pl.ds(lo_sib, rh), :]
                pltpu.make_async_remote_copy(any_rows, any_rows, send_sems.at[a * 7 + k], recv_sems.at[a * 7 + k],
                                             device_id=sibling, device_id_type=MESH).wait_recv()
        for a in range(ns):
            for k in range(3):
                pltpu.make_async_remote_copy(small_in[a], small_out[a].at[s_me], send_sems.at[nb * 7 + a * 3 + k],
                                             recv_sems.at[nb * 7 + a * 3 + k], device_id=sibling,
                                             device_id_type=MESH).wait_recv()
        for cp in started:
            cp.wait_send()
        for cp in local_copies:
            cp.wait()

    n_sem = nb * 7 + ns * 3
    out_shape = ([jax.ShapeDtypeStruct((N_SHARDS,) + w.shape, BF16) for w in big]
                 + [jax.ShapeDtypeStruct((N_SHARDS,) + w.shape, F32) for w in small])
    res = pl.pallas_call(
        body, in_specs=[VMEM_SPEC] * nb + [HBM_SPEC] * ns, out_specs=[HBM_SPEC] * (nb + ns),
        out_shape=out_shape,
        scratch_shapes=[pltpu.VMEM((w.shape[0] // 2, w.shape[1]), BF16) for w in big]
        + [pltpu.SemaphoreType.DMA((n_sem,)), pltpu.SemaphoreType.DMA((n_sem,)),
           pltpu.SemaphoreType.DMA((nb + ns,))],
        compiler_params=COMM_PARAMS, name="gather_weights",
    )(*big, *small)
    return res[:nb], res[nb:]


def _sibling_exchange(grads):
    n = len(grads)

    def body(*refs):
        g_in = refs[:n]
        got = refs[n:2 * n]
        send_sems, recv_sems = refs[2 * n:]
        x, y, c = _place()
        cps = []
        for a in range(n):
            rh = grads[a].shape[1] // 2
            lo = pl.multiple_of((1 - c) * rh, 8)
            cp = pltpu.make_async_remote_copy(g_in[a].at[:, pl.ds(lo, rh), :], got[a], send_sems.at[a],
                                              recv_sems.at[a], device_id=(x, y, 1 - c), device_id_type=MESH)
            cp.start()
            cps.append(cp)
        for cp in cps:
            cp.wait()

    return pl.pallas_call(
        body, in_specs=[HBM_SPEC] * n, out_specs=[HBM_SPEC] * n,
        out_shape=[jax.ShapeDtypeStruct((N_SHARDS, g.shape[1] // 2, g.shape[2]), F32) for g in grads],
        scratch_shapes=[pltpu.SemaphoreType.DMA((n,)), pltpu.SemaphoreType.DMA((n,))],
        compiler_params=COMM_PARAMS, name="sibling_exchange",
    )(*grads)


def _chip_exchange(chip_parts, pack):
    n = len(chip_parts)

    def body(*refs):
        parts = refs[:n]
        pack_ref = refs[n]
        got = refs[n + 1:2 * n + 1]
        all_packs = refs[2 * n + 1]
        send_sems, recv_sems, local_sem = refs[2 * n + 2:]
        x, y, c = _place()
        me = 4 * x + 2 * y + c
        cps = []
        for a in range(n):
            for k, (fx, fy) in enumerate(CHIP_FLIPS):
                px, py = _flip(x, fx), _flip(y, fy)
                cp = pltpu.make_async_remote_copy(parts[a].at[2 * px + py], got[a].at[k], send_sems.at[a * 3 + k],
                                                  recv_sems.at[a * 3 + k], device_id=(px, py, c),
                                                  device_id_type=MESH)
                cp.start()
                cps.append(cp)
        loc = pltpu.make_async_copy(pack_ref, all_packs.at[me], local_sem)
        loc.start()
        for m in range(1, N_DEV):
            to = (_flip(x, m & 4), _flip(y, m & 2), _flip(c, m & 1))
            cp = pltpu.make_async_remote_copy(pack_ref, all_packs.at[me], send_sems.at[n * 3 + m - 1],
                                              recv_sems.at[n * 3 + m - 1], device_id=to, device_id_type=MESH)
            cp.start()
            cps.append(cp)
        for cp in cps:
            cp.wait()
        loc.wait()

    rs = pack.shape[0]
    res = pl.pallas_call(
        body, in_specs=[HBM_SPEC] * (n + 1), out_specs=[HBM_SPEC] * (n + 1),
        out_shape=[jax.ShapeDtypeStruct((3,) + p.shape[1:], BF16) for p in chip_parts]
        + [jax.ShapeDtypeStruct((N_DEV, rs, LANES), F32)],
        scratch_shapes=[pltpu.SemaphoreType.DMA((n * 3 + N_DEV - 1,)), pltpu.SemaphoreType.DMA((n * 3 + N_DEV - 1,)),
                        pltpu.SemaphoreType.DMA],
        compiler_params=COMM_PARAMS, name="chip_exchange",
    )(*chip_parts, pack)
    return res[:n], res[n]


def _sibling_assemble(fulls):
    n = len(fulls)

    def body(*refs):
        full = refs[n:2 * n]
        send_sems, recv_sems = refs[2 * n:]
        x, y, c = _place()
        cps = []
        for a in range(n):
            rh = fulls[a].shape[0] // 2
            mine = full[a].at[pl.ds(pl.multiple_of(c * rh, 8), rh), :]
            cp = pltpu.make_async_remote_copy(mine, mine, send_sems.at[a], recv_sems.at[a],
                                              device_id=(x, y, 1 - c), device_id_type=MESH)
            cp.start()
            cps.append(cp)
        for cp in cps:
            cp.wait()

    return pl.pallas_call(
        body, in_specs=[HBM_SPEC] * n, out_specs=[HBM_SPEC] * n,
        out_shape=[jax.ShapeDtypeStruct(f.shape, F32) for f in fulls],
        input_output_aliases={a: a for a in range(n)},
        scratch_shapes=[pltpu.SemaphoreType.DMA((n,)), pltpu.SemaphoreType.DMA((n,))],
        compiler_params=COMM_PARAMS, name="sibling_assemble",
    )(*fulls)


def _remote(ref_src, ref_dst, send_sems, recv_sems, k, to):
    return pltpu.make_async_remote_copy(ref_src, ref_dst, send_sems.at[k], recv_sems.at[k], device_id=to,
                                        device_id_type=MESH)


def _stage_half(w, ids, name):
    R, C = w.shape
    rh = R // 2
    rt = _half_tile(rh)
    nt = rh // rt

    def body(ids_ref, w_ref, o_ref):
        o_ref[0] = w_ref[...].astype(BF16)

    grid_spec = pltpu.PrefetchScalarGridSpec(
        num_scalar_prefetch=1, grid=(nt,),
        in_specs=[pl.BlockSpec((rt, C), lambda i, ids: (ids[2] * nt + i, 0))],
        out_specs=pl.BlockSpec((1, rt, C), lambda i, ids: (2 * ids[0] + ids[1], ids[2] * nt + i, 0)))
    return pl.pallas_call(body, grid_spec=grid_spec, out_shape=jax.ShapeDtypeStruct((N_SHARDS, R, C), BF16),
                          compiler_params=_params(1), name=name)(ids, w)


def _bg_gather(staged):
    n = len(staged)

    def run(step, n_steps, ins, outs, send_sems, recv_sems, local_sems, post):
        x, y, c = _place()
        s_me = 2 * x + y
        sibling = (x, y, 1 - c)
        chips = [(_flip(x, fx), _flip(y, fy)) for fx, fy in CHIP_FLIPS]

        def rows(a, s, half):
            rh = staged[a].shape[1] // 2
            return outs[a].at[s, pl.ds(pl.multiple_of(half * rh, 16), rh), :]

        def copy(a, k, ref, to):
            return _remote(ref, ref, send_sems, recv_sems, a * 7 + k, to)

        if not post:
            @pl.when(step == 0)
            def _():
                for a in range(n):
                    mine = rows(a, s_me, c)
                    copy(a, 0, mine, sibling).start()
                    for k, (px, py) in enumerate(chips):
                        copy(a, 1 + k, mine, (px, py, c)).start()

            @pl.when(step == max(n_steps - 2, 0))
            def _():
                for a in range(n):
                    for k, (px, py) in enumerate(chips):
                        got = rows(a, 2 * px + py, c)
                        copy(a, 1 + k, got, sibling).wait_recv()
                        copy(a, 4 + k, got, sibling).start()
        else:
            @pl.when(step == n_steps - 1)
            def _():
                for a in range(n):
                    for k in (0, 4, 5, 6):
                        copy(a, k, rows(a, s_me, 1 - c), sibling).wait_recv()
                    for k in range(7):
                        copy(a, k, rows(a, s_me, c), sibling).wait_send()

    return _Background(staged, [jax.ShapeDtypeStruct(g.shape, g.dtype) for g in staged],
                       {a: a for a in range(n)}, 7 * n, run)


def _bg_sibling_exchange(grads):
    n = len(grads)

    def run(step, n_steps, ins, outs, send_sems, recv_sems, local_sems, post):
        x, y, c = _place()

        def copy(a):
            rh = grads[a].shape[1] // 2
            lo = pl.multiple_of((1 - c) * rh, 8)
            return _remote(ins[a].at[:, pl.ds(lo, rh), :], outs[a], send_sems, recv_sems, a, (x, y, 1 - c))

        if not post:
            @pl.when(step == 0)
            def _():
                for a in range(n):
                    copy(a).start()
        else:
            @pl.when(step == n_steps - 1)
            def _():
                for a in range(n):
                    copy(a).wait()

    return _Background(grads, [jax.ShapeDtypeStruct((N_SHARDS, g.shape[1] // 2, g.shape[2]), F32) for g in grads],
                       {}, n, run)


def _bg_chip_exchange(chip_parts, pack=None):
    n = len(chip_parts)

    def run(step, n_steps, ins, outs, send_sems, recv_sems, local_sems, post):
        x, y, c = _place()
        me = 4 * x + 2 * y + c

        def copies():
            cps = []
            for a in range(n):
                for k, (fx, fy) in enumerate(CHIP_FLIPS):
                    px, py = _flip(x, fx), _flip(y, fy)
                    cps.append(_remote(ins[a].at[2 * px + py], outs[a].at[k], send_sems, recv_sems, a * 3 + k,
                                       (px, py, c)))
            if pack is not None:
                for m in range(1, N_DEV):
                    to = (_flip(x, m & 4), _flip(y, m & 2), _flip(c, m & 1))
                    cps.append(_remote(ins[n], outs[n].at[me], send_sems, recv_sems, n * 3 + m - 1, to))
            return cps

        def local():
            return pltpu.make_async_copy(ins[n], outs[n].at[me], local_sems.at[0])

        if not post:
            @pl.when(step == 0)
            def _():
                for cp in copies():
                    cp.start()
                if pack is not None:
                    local().start()
        else:
            @pl.when(step == n_steps - 1)
            def _():
                for cp in copies():
                    cp.wait()
                if pack is not None:
                    local().wait()

    in_arrays = list(chip_parts) + ([pack] if pack is not None else [])
    out_shapes = [jax.ShapeDtypeStruct((3,) + p.shape[1:], BF16) for p in chip_parts]
    if pack is not None:
        out_shapes.append(jax.ShapeDtypeStruct((N_DEV, pack.shape[0], LANES), F32))
    return _Background(in_arrays, out_shapes, {}, n * 3 + N_DEV - 1, run)


def _half_tile(rh, mult=16, want=256):
    best = None
    for t in range(mult, min(rh, want) + 1, mult):
        if rh % t == 0:
            best = t
    return best if best is not None else rh


def _pair_sum(g, sib, ids, name):
    _, R, C = g.shape
    rh = R // 2
    rt = _half_tile(rh)
    nt = rh // rt

    def body(ids_ref, g_ref, s_ref, o_ref):
        o_ref[...] = (g_ref[...] + s_ref[...]).astype(BF16)

    grid_spec = pltpu.PrefetchScalarGridSpec(
        num_scalar_prefetch=1, grid=(N_SHARDS, nt),
        in_specs=[pl.BlockSpec((1, rt, C), lambda s, i, ids: (s, ids[2] * nt + i, 0)),
                  pl.BlockSpec((1, rt, C), lambda s, i, ids: (s, i, 0))],
        out_specs=pl.BlockSpec((1, rt, C), lambda s, i, ids: (s, i, 0)))
    return pl.pallas_call(body, grid_spec=grid_spec, out_shape=jax.ShapeDtypeStruct((N_SHARDS, rh, C), BF16),
                          compiler_params=_params(2), name=name)(ids, g, sib)


def _final_sum(g, sib, got, ids, name):
    _, R, C = g.shape
    rh = R // 2
    rt = _half_tile(rh)
    nt = rh // rt

    def body(ids_ref, g_ref, s_ref, r_ref, o_ref):
        tot = g_ref[0] + s_ref[0]
        for k in range(3):
            tot = tot + r_ref[k].astype(F32)
        o_ref[...] = tot

    grid_spec = pltpu.PrefetchScalarGridSpec(
        num_scalar_prefetch=1, grid=(nt,),
        in_specs=[pl.BlockSpec((1, rt, C), lambda i, ids: (2 * ids[0] + ids[1], ids[2] * nt + i, 0)),
                  pl.BlockSpec((1, rt, C), lambda i, ids: (2 * ids[0] + ids[1], i, 0)),
                  pl.BlockSpec((3, rt, C), lambda i, ids: (0, i, 0))],
        out_specs=pl.BlockSpec((rt, C), lambda i, ids: (ids[2] * nt + i, 0)))
    return pl.pallas_call(body, grid_spec=grid_spec, out_shape=jax.ShapeDtypeStruct((R, C), F32),
                          compiler_params=_params(1), name=name)(ids, g, sib, got)


def _sum_packs(all_packs):
    def body(p_ref, o_ref):
        tot = p_ref[0]
        for i in range(1, N_DEV):
            tot = tot + p_ref[i]
        o_ref[...] = tot

    return pl.pallas_call(body, in_specs=[VMEM_SPEC], out_specs=VMEM_SPEC,
                          out_shape=jax.ShapeDtypeStruct(all_packs.shape[1:], F32), name="sum_packs")(all_packs)


def _adamw(w, g, m, v, name):
    R, C = w.shape
    rt = _half_tile(R, mult=8, want=256)

    def body(w_ref, g_ref, m_ref, v_ref, d_ref, nm_ref, nv_ref):
        gg = g_ref[...]
        nm = ADAM_B1 * m_ref[...] + (1.0 - ADAM_B1) * gg
        nv = ADAM_B2 * v_ref[...] + (1.0 - ADAM_B2) * (gg * gg)
        m_hat = nm / (1.0 - ADAM_B1 ** ADAM_STEP)
        v_hat = nv / (1.0 - ADAM_B2 ** ADAM_STEP)
        d_ref[...] = -ADAM_LR * (m_hat / (jnp.sqrt(v_hat) + ADAM_EPS) + ADAM_WD * w_ref[...])
        nm_ref[...] = nm
        nv_ref[...] = nv

    spec = pl.BlockSpec((rt, C), lambda i: (i, 0))
    return pl.pallas_call(body, grid=(R // rt,), in_specs=[spec] * 4, out_specs=[spec] * 3,
                          out_shape=[jax.ShapeDtypeStruct((R, C), F32)] * 3,
                          compiler_params=_params(1), name=name)(w, g, m, v)


def _pack(pieces):
    rows = []
    for p in pieces:
        flat = p.reshape(-1)
        pad = (-flat.shape[0]) % LANES
        if pad:
            flat = jnp.concatenate([flat, jnp.zeros((pad,), F32)])
        rows.append(flat.reshape(-1, LANES))
    total = sum(r.shape[0] for r in rows)
    pad_rows = (-total) % 8
    if pad_rows:
        rows.append(jnp.zeros((pad_rows, LANES), F32))
    return jnp.concatenate(rows, axis=0)


def _unpack(buf, shapes):
    out, r0 = [], 0
    for shp in shapes:
        n = int(np.prod(shp))
        nr = -(-n // LANES)
        out.append(buf[r0:r0 + nr].reshape(-1)[:n].reshape(shp))
        r0 += nr
    return out


SMALL_NAMES = ("rel_table", "b_in", "conv_w", "conv_b", "conv_ln_g", "conv_ln_b", "attn_norm_g", "conv_norm_g",
               "ln1_g", "ln1_b", "ffn_conv_w", "ffn_conv_b", "ln2_g", "ln2_b")
BIG_NAMES = ("w_in", "w_out", "w_up", "w_down")
WEIGHT_ORDER = ("rel_table", "w_in", "b_in", "conv_w", "conv_b", "conv_ln_g", "conv_ln_b", "attn_norm_g",
                "conv_norm_g", "w_out", "ln1_g", "ln1_b", "w_up", "ffn_conv_w", "ffn_conv_b", "w_down",
                "ln2_g", "ln2_b")


def kernel(x, rel_table, w_in, b_in, conv_w, conv_b, conv_ln_g, conv_ln_b, attn_norm_g, conv_norm_g, w_out, ln1_g, ln1_b, w_up, ffn_conv_w, ffn_conv_b, w_down, ln2_g, ln2_b, loss_target, m_rel_table, m_w_in, m_b_in, m_conv_w, m_conv_b, m_conv_ln_g, m_conv_ln_b, m_attn_norm_g, m_conv_norm_g, m_w_out, m_ln1_g, m_ln1_b, m_w_up, m_ffn_conv_w, m_ffn_conv_b, m_w_down, m_ln2_g, m_ln2_b, v_rel_table, v_w_in, v_b_in, v_conv_w, v_conv_b, v_conv_ln_g, v_conv_ln_b, v_attn_norm_g, v_conv_norm_g, v_w_out, v_ln1_g, v_ln1_b, v_w_up, v_ffn_conv_w, v_ffn_conv_b, v_w_down, v_ln2_g, v_ln2_b):
    args = dict(locals())
    weights = {n: args[n] for n in WEIGHT_ORDER}
    moms = {n: args["m_" + n] for n in WEIGHT_ORDER}
    vels = {n: args["v_" + n] for n in WEIGHT_ORDER}
    xi, yi, ci = _place()
    ids = jnp.stack([xi, yi, ci]).astype(jnp.int32)
    shard = 2 * xi + yi
    D = x.shape[-1]
    DFF = w_down.shape[1] * N_SHARDS
    CW = conv_norm_g.shape[-1]

    (g_in,), (g_cw, g_fcw) = _gather_weights([w_in[0]], [conv_w[0], ffn_conv_w[0]])
    cols = lambda t: jnp.transpose(t, (1, 0, 2)).reshape(t.shape[1], N_SHARDS * t.shape[2])
    staged = [_stage_half(w[0], ids, name="stage_" + n) for w, n in ((w_out, "w_out"), (w_up, "w_up"),
                                                                     (w_down, "w_down"))]

    grad_x, fulls, all_packs = _local_step(
        x, loss_target, rel_table, cols(g_in), b_in, cols(g_cw), conv_b, conv_ln_g, conv_ln_b, attn_norm_g,
        conv_norm_g, staged, ln1_g, ln1_b, cols(g_fcw), ffn_conv_b, ln2_g, ln2_b, ids)
    big_grads = dict(zip(BIG_NAMES, _sibling_assemble(fulls)))
    for n in ("w_in", "w_up"):
        big_grads[n] = _transpose(big_grads[n], name="transpose_d" + n)

    summed = _sum_packs(all_packs)
    full_shapes = {n: weights[n].shape for n in SMALL_NAMES}
    full_shapes["conv_w"] = (1, CONV_KERNEL, CW)
    full_shapes["ffn_conv_w"] = (1, FFN_CONV_KERNEL, 2 * DFF)
    un = _unpack(summed, [(1, LANES)] + [full_shapes[n] for n in SMALL_NAMES])
    loss = un[0][0, 0]
    small_grads = dict(zip(SMALL_NAMES, un[1:]))
    for n in ("conv_w", "ffn_conv_w"):
        width = weights[n].shape[-1]
        small_grads[n] = lax.dynamic_slice_in_dim(small_grads[n], shard * width, width, axis=2)

    grads, delta, new_m, new_v = {}, {}, {}, {}
    for n in BIG_NAMES:
        shp = weights[n].shape
        g2 = big_grads[n]
        d, nm, nv = _adamw(weights[n][0], g2, moms[n][0], vels[n][0], name="adamw_" + n)
        grads[n], delta[n], new_m[n], new_v[n] = (t.reshape(shp) for t in (g2, d, nm, nv))
    sp = lambda src: _pack([src[n] for n in SMALL_NAMES])
    d_s, nm_s, nv_s = _adamw(sp(weights), sp(small_grads), sp(moms), sp(vels), name="adamw_small")
    shapes = [weights[n].shape for n in SMALL_NAMES]
    for tgt, buf in ((delta, d_s), (new_m, nm_s), (new_v, nv_s)):
        tgt.update(zip(SMALL_NAMES, _unpack(buf, shapes)))
    grads.update(small_grads)

    return (loss, grad_x, *[grads[n] for n in WEIGHT_ORDER], *[delta[n] for n in WEIGHT_ORDER],
            *[new_m[n] for n in WEIGHT_ORDER], *[new_v[n] for n in WEIGHT_ORDER])
```

```python
import functools
import math

import numpy as np
import jax
import jax.numpy as jnp
from jax import lax
from jax.experimental import pallas as pl
from jax.experimental.pallas import tpu as pltpu

F32 = jnp.float32
BF16 = jnp.bfloat16
MESH = pl.DeviceIdType.MESH

HEAD_DIM = 64
LANES = 128
ATTN_BLOCK = 128
DILATED_CONFIGS = ((128, 1), (512, 4), (2048, 16))
CONV_KERNEL = 31
FFN_CONV_KERNEL = 3
REL_BUCKETS = 32
REL_MAX_DIST = 2048
DEPTH = 1
ALPHA = (2 * DEPTH) ** 0.25
LN_EPS = 1e-5
NEG_INF = -1e30
QK_SCALE = 1.0 / math.sqrt(HEAD_DIM)
ADAM_LR = 0.001
ADAM_B1 = 0.9
ADAM_B2 = 0.999
ADAM_EPS = 1e-08
ADAM_WD = 0.01
ADAM_STEP = 10
VMEM_LIMIT = 52 * 1024 * 1024
FFN_COLS = 128
N_SHARDS = 4
N_DEV = 8


def _params(n_axes):
    return pltpu.CompilerParams(dimension_semantics=("arbitrary",) * n_axes,
                                vmem_limit_bytes=VMEM_LIMIT)


MM_DIMS = {"nn": (((1,), (0,)), ((), ())), "nt": (((1,), (1,)), ((), ())), "tn": (((0,), (0,)), ((), ()))}


class _Background:
    def __init__(self, in_arrays, out_shapes, aliases, n_sems, run, n_local=1):
        self.in_arrays, self.out_shapes, self.aliases = list(in_arrays), list(out_shapes), dict(aliases)
        self.n_sems, self.n_local, self.run = n_sems, n_local, run

    def scratch(self):
        return [pltpu.SemaphoreType.DMA((self.n_sems,)), pltpu.SemaphoreType.DMA((self.n_sems,)),
                pltpu.SemaphoreType.DMA((self.n_local,))]


def _hosted_call(body, bg, *, grid, in_specs, out_specs, out_shape, scratch_shapes, operands, name):
    n_in, n_out, n_scr = len(in_specs), len(out_specs), len(scratch_shapes)
    if bg is None:
        return pl.pallas_call(lambda *refs: body(refs, lambda post: None), grid=grid, in_specs=in_specs,
                              out_specs=out_specs, out_shape=out_shape, scratch_shapes=scratch_shapes,
                              compiler_params=_params(len(grid)), name=name)(*operands)
    nb_in, nb_out = len(bg.in_arrays), len(bg.out_shapes)
    n_steps = int(np.prod(grid))

    def full_body(*refs):
        own = refs[:n_in] + refs[n_in + nb_in:n_in + nb_in + n_out] \
            + refs[n_in + nb_in + n_out + nb_out:n_in + nb_in + n_out + nb_out + n_scr]
        bg_in = refs[n_in:n_in + nb_in]
        bg_out = refs[n_in + nb_in + n_out:n_in + nb_in + n_out + nb_out]
        sems = refs[n_in + nb_in + n_out + nb_out + n_scr:]
        step = pl.program_id(0)
        for ax in range(1, len(grid)):
            step = step * grid[ax] + pl.program_id(ax)

        def hook(post):
            bg.run(step, n_steps, bg_in, bg_out, *sems, post)

        body(own, hook)

    res = pl.pallas_call(
        full_body, grid=grid, in_specs=list(in_specs) + [HBM_SPEC] * nb_in,
        out_specs=list(out_specs) + [HBM_SPEC] * nb_out, out_shape=list(out_shape) + bg.out_shapes,
        input_output_aliases={n_in + a: n_out + o for a, o in bg.aliases.items()},
        scratch_shapes=list(scratch_shapes) + bg.scratch(), compiler_params=_params(len(grid)), name=name,
    )(*operands, *bg.in_arrays)
    return res


def _matmul_general(ins, part_fn, *, grid, tm, tn, outs, epilogue, extras=(), name, bg=None):
    nk = grid[2]
    n_in, n_extra = len(ins), len(extras)

    def body(refs, bg_hook):
        in_refs = refs[:n_in]
        rest = refs[n_in:]
        extra_refs = rest[:n_extra]
        out_refs = rest[n_extra:n_extra + len(outs)]
        acc_ref = rest[-1]
        i, j, k = pl.program_id(0), pl.program_id(1), pl.program_id(2)
        bg_hook(False)
        part = part_fn(in_refs, i, j, k)
        if nk == 1:
            epilogue(part, i, j, extra_refs, out_refs)
        else:
            @pl.when(k == 0)
            def _():
                acc_ref[...] = part

            @pl.when(k > 0)
            def _():
                acc_ref[...] += part

            @pl.when(k == nk - 1)
            def _():
                epilogue(acc_ref[...], i, j, extra_refs, out_refs)
        bg_hook(True)

    in_specs = [pl.BlockSpec(bs, im) for (_, bs, im) in list(ins) + list(extras)]
    out_specs = [pl.BlockSpec(bs, im) for (_, _, bs, im) in outs]
    out_shape = [jax.ShapeDtypeStruct(s, d) for (s, d, _, _) in outs]
    return _hosted_call(body, bg, grid=grid, in_specs=in_specs, out_specs=out_specs, out_shape=out_shape,
                        scratch_shapes=[pltpu.VMEM((tm, tn), F32)],
                        operands=[e[0] for e in ins] + [e[0] for e in extras], name=name)


def _dot(a, b, mode):
    return lax.dot_general(a.astype(BF16), b.astype(BF16), MM_DIMS[mode], preferred_element_type=F32)


def _matmul(a, b, *, mode, tm, tn, tk, outs, epilogue, extras=(), name, bg=None):
    if mode == "tn":
        K, M = a.shape
        N = b.shape[1]
        ins = [(a, (tk, tm), lambda i, j, k: (k, i)), (b, (tk, tn), lambda i, j, k: (k, j))]
    elif mode == "nt":
        M, K = a.shape
        N = b.shape[0]
        ins = [(a, (tm, tk), lambda i, j, k: (i, k)), (b, (tn, tk), lambda i, j, k: (j, k))]
    else:
        M, K = a.shape
        N = b.shape[1]
        ins = [(a, (tm, tk), lambda i, j, k: (i, k)), (b, (tk, tn), lambda i, j, k: (k, j))]
    assert M % tm == 0 and N % tn == 0 and K % tk == 0, (name, M, N, K, tm, tn, tk)

    def part_fn(in_refs, i, j, k):
        return _dot(in_refs[0][...], in_refs[1][...], mode)

    return _matmul_general(ins, part_fn, grid=(M // tm, N // tn, K // tk), tm=tm, tn=tn, outs=outs,
                           epilogue=epilogue, extras=extras, name=name, bg=bg)


def _plain_out(M, N, tm, tn, dtype):
    return ((M, N), dtype, (tm, tn), lambda i, j, k: (i, j))


def _mm_plain(a, b, *, mode, tm, tn, tk, out_dtype, name, bias=None, bg=None):
    if mode == "tn":
        M, N = a.shape[1], b.shape[1]
    elif mode == "nt":
        M, N = a.shape[0], b.shape[0]
    else:
        M, N = a.shape[0], b.shape[1]
    extras = []
    if bias is not None:
        extras.append((bias, (1, tn), lambda i, j, k: (0, j)))

    def epilogue(acc, i, j, extra_refs, out_refs):
        if bias is not None:
            acc = acc + extra_refs[0][...]
        out_refs[0][...] = acc.astype(out_dtype)

    res = _matmul(a, b, mode=mode, tm=tm, tn=tn, tk=tk, outs=[_plain_out(M, N, tm, tn, out_dtype)],
                  epilogue=epilogue, extras=extras, name=name, bg=bg)
    return res[0] if bg is None else res


def _row_tile(T, want):
    t = min(T, want)
    while T % t:
        t //= 2
    return t


def _col_tile(N, want):
    if N <= want:
        return N
    best = None
    for c in range(LANES, want + 1, LANES):
        if N % c == 0:
            best = c
    return best if best is not None else N


def _accumulate(ref, first, val):
    @pl.when(first)
    def _():
        ref[...] = val

    @pl.when(jnp.logical_not(first))
    def _():
        ref[...] += val


def _ln_fwd(z, g, b):
    mu = jnp.mean(z, axis=-1, keepdims=True)
    zc = z - mu
    var = jnp.mean(zc * zc, axis=-1, keepdims=True)
    r = lax.rsqrt(var + LN_EPS)
    xh = zc * r
    return xh * g + b, xh, r


def _ln_bwd(dy, xh, r, g):
    dxh = dy * g
    m1 = jnp.mean(dxh, axis=-1, keepdims=True)
    m2 = jnp.mean(dxh * xh, axis=-1, keepdims=True)
    return r * (dxh - m1 - xh * m2)


def _sigmoid(x):
    return 1.0 / (1.0 + jnp.exp(-x))


def _shift_down(x, s, row):
    if s == 0:
        return x
    rolled = pltpu.roll(x, s, 0)
    nfix = -(-s // 8) * 8
    head = jnp.where(row[:nfix] >= s, rolled[:nfix], 0.0)
    return jnp.concatenate([head, rolled[nfix:]], axis=0)


def _shift_up(x, s, row):
    if s == 0:
        return x
    n = x.shape[0]
    rolled = pltpu.roll(x, n - s, 0)
    nfix = -(-s // 8) * 8
    tail = jnp.where(row[n - nfix:] < n - s, rolled[n - nfix:], 0.0)
    return jnp.concatenate([rolled[:n - nfix], tail], axis=0)


def _bucket_tables():
    exact = REL_BUCKETS // 2
    qi = np.arange(ATTN_BLOCK)[:, None]
    kj = np.arange(2 * ATTN_BLOCK)[None, :]
    steps = qi + ATTN_BLOCK - kj
    buckets, masks = [], []
    for window, dilation in DILATED_CONFIGS:
        max_steps = window // dilation
        band = (steps >= 0) & (steps <= max_steps)
        dist = np.maximum(steps, 0) * dilation
        d_f = np.maximum(dist, 1).astype(np.float32)
        large = exact + (np.log(d_f / np.float32(exact)) / np.float32(math.log(REL_MAX_DIST / exact))
                         * np.float32(REL_BUCKETS - exact)).astype(np.int32)
        large = np.minimum(large, REL_BUCKETS - 1)
        bucket = np.where(dist < exact, dist, large).astype(np.int32)
        buckets.append(bucket.reshape(1, -1))
        masks.append(np.where(band, 0.0, NEG_INF).astype(np.float32).reshape(1, -1))
    return np.stack(buckets), np.stack(masks)


def _split_hi_lo(x):
    hi = x.astype(BF16)
    lo = (x - hi.astype(F32)).astype(BF16)
    return hi, lo


def _bias_build(rel_table_t, bucket, mask):
    H = rel_table_t.shape[0]
    n = bucket.shape[-1]

    def body(t_ref, bkt_ref, mask_ref, o_ref):
        onehot = (lax.broadcasted_iota(jnp.int32, (REL_BUCKETS, n), 0) == bkt_ref[0]).astype(BF16)
        t = t_ref[...]
        t1 = t.astype(BF16)
        r1 = t - t1.astype(F32)
        t2 = r1.astype(BF16)
        t3 = (r1 - t2.astype(F32)).astype(BF16)
        acc = jnp.dot(t1, onehot, preferred_element_type=F32)
        acc = acc + jnp.dot(t2, onehot, preferred_element_type=F32)
        acc = acc + jnp.dot(t3, onehot, preferred_element_type=F32)
        o_ref[0] = acc + mask_ref[0]

    return pl.pallas_call(
        body, grid=(3,),
        in_specs=[pl.BlockSpec((H, REL_BUCKETS), lambda b: (0, 0)),
                  pl.BlockSpec((1, 1, n), lambda b: (b, 0, 0)),
                  pl.BlockSpec((1, 1, n), lambda b: (b, 0, 0))],
        out_specs=pl.BlockSpec((1, H, n), lambda b: (b, 0, 0)),
        out_shape=jax.ShapeDtypeStruct((3, H, n), F32),
        compiler_params=_params(1), name="bias_build",
    )(rel_table_t, bucket, mask)


def _rel_grad(dbias, bucket):
    H = dbias.shape[1]
    n = bucket.shape[-1]
    dims = (((1,), (1,)), ((), ()))

    def body(d_ref, bkt_ref, o_ref):
        b = pl.program_id(0)
        onehot = (lax.broadcasted_iota(jnp.int32, (REL_BUCKETS, n), 0) == bkt_ref[0]).astype(BF16)
        d = d_ref[0]
        d1 = d.astype(BF16)
        r1 = d - d1.astype(F32)
        d2 = r1.astype(BF16)
        d3 = (r1 - d2.astype(F32)).astype(BF16)
        acc = lax.dot_general(d1, onehot, dims, preferred_element_type=F32)
        acc = acc + lax.dot_general(d2, onehot, dims, preferred_element_type=F32)
        acc = acc + lax.dot_general(d3, onehot, dims, preferred_element_type=F32)
        _accumulate(o_ref, b == 0, acc)

    return pl.pallas_call(
        body, grid=(3,),
        in_specs=[pl.BlockSpec((1, H, n), lambda b: (b, 0, 0)),
                  pl.BlockSpec((1, 1, n), lambda b: (b, 0, 0))],
        out_specs=pl.BlockSpec((H, REL_BUCKETS), lambda b: (0, 0)),
        out_shape=jax.ShapeDtypeStruct((H, REL_BUCKETS), F32),
        compiler_params=_params(1), name="rel_grad",
    )(dbias, bucket)


def _attn_specs(B, S, AW, d):
    L = S // d
    HP = AW // LANES
    W3 = 3 * HP
    q_spec = pl.BlockSpec((1, L, LANES), lambda h, b, r: (b, 0, r * W3 + h))
    k_spec = pl.BlockSpec((1, L, LANES), lambda h, b, r: (b, 0, r * W3 + HP + h))
    v_spec = pl.BlockSpec((1, L, LANES), lambda h, b, r: (b, 0, r * W3 + 2 * HP + h))
    o_spec = pl.BlockSpec((1, L, LANES), lambda h, b, r: (b, 0, r * HP + h))
    bias_spec = pl.BlockSpec((2, ATTN_BLOCK, 2 * ATTN_BLOCK), lambda h, b, r: (h, 0, 0))
    return L, HP, q_spec, k_spec, v_spec, o_spec, bias_spec


def _attn_fwd(qkv, bias, B, S, AW, d, name):
    L, HP, q_spec, k_spec, v_spec, o_spec, bias_spec = _attn_specs(B, S, AW, d)
    nb = L // ATTN_BLOCK
    nt = (((1,), (1,)), ((), ()))

    def body(q_ref, k_ref, v_ref, b_ref, o_ref, lse_ref):
        head0 = lax.broadcasted_iota(jnp.int32, (1, LANES), 1) < HEAD_DIM

        def block(n, first):
            qs = pl.multiple_of(n * ATTN_BLOCK, ATTN_BLOCK)
            q = q_ref[0, pl.ds(qs, ATTN_BLOCK), :]
            if first:
                kk = k_ref[0, pl.ds(0, ATTN_BLOCK), :]
                vv = v_ref[0, pl.ds(0, ATTN_BLOCK), :]
            else:
                ks = pl.multiple_of(n * ATTN_BLOCK - ATTN_BLOCK, ATTN_BLOCK)
                kk = k_ref[0, pl.ds(ks, 2 * ATTN_BLOCK), :]
                vv = v_ref[0, pl.ds(ks, 2 * ATTN_BLOCK), :]
            outs, lses = [], []
            for e in range(2):
                msk = head0 if e == 0 else jnp.logical_not(head0)
                qe = jnp.where(msk, q, jnp.zeros_like(q))
                s = lax.dot_general(qe, kk, nt, preferred_element_type=F32) * QK_SCALE
                s = s + (b_ref[e, :, ATTN_BLOCK:] if first else b_ref[e])
                m = jnp.max(s, axis=-1, keepdims=True)
                p = jnp.exp(s - m)
                l = jnp.sum(p, axis=-1, keepdims=True)
                o = jnp.dot(p.astype(BF16), vv, preferred_element_type=F32)
                outs.append(o / l)
                lses.append(jnp.broadcast_to(m + jnp.log(l), (ATTN_BLOCK, LANES)))
            o_ref[0, pl.ds(qs, ATTN_BLOCK), :] = jnp.where(head0, outs[0], outs[1])
            lse_ref[0, pl.ds(qs, ATTN_BLOCK), :] = jnp.where(head0, lses[0], lses[1])

        block(0, True)
        if nb > 1:
            def loop(n, c):
                block(n, False)
                return c
            lax.fori_loop(1, nb, loop, 0)

    qv = qkv.reshape(B, L, d * 3 * AW)
    o, lse = pl.pallas_call(
        body, grid=(HP, B, d), in_specs=[q_spec, k_spec, v_spec, bias_spec],
        out_specs=[o_spec, o_spec],
        out_shape=[jax.ShapeDtypeStruct((B, L, d * AW), F32)] * 2,
        compiler_params=_params(3), name=name,
    )(qv, qv, qv, bias)
    return o.reshape(B * S, AW), lse.reshape(B * S, AW)


def _attn_bwd(qkv, do, lse, dd, bias, B, S, AW, d, name):
    L, HP, q_spec, k_spec, v_spec, o_spec, bias_spec = _attn_specs(B, S, AW, d)
    nb = L // ATTN_BLOCK
    nt = (((1,), (1,)), ((), ()))
    tn = (((0,), (0,)), ((), ()))

    def body(q_ref, k_ref, v_ref, do_ref, lse_ref, dd_ref, b_ref, dq_ref, dk_ref, dv_ref, db_ref):
        head0 = lax.broadcasted_iota(jnp.int32, (1, LANES), 1) < HEAD_DIM
        first_step = jnp.logical_and(pl.program_id(1) == 0, pl.program_id(2) == 0)

        @pl.when(first_step)
        def _():
            db_ref[...] = jnp.zeros_like(db_ref)

        dk_ref[...] = jnp.zeros_like(dk_ref)
        dv_ref[...] = jnp.zeros_like(dv_ref)

        def block(n, first):
            qs = pl.multiple_of(n * ATTN_BLOCK, ATTN_BLOCK)
            nkeys = ATTN_BLOCK if first else 2 * ATTN_BLOCK
            ks = 0 if first else pl.multiple_of(n * ATTN_BLOCK - ATTN_BLOCK, ATTN_BLOCK)
            q = q_ref[0, pl.ds(qs, ATTN_BLOCK), :]
            kk = k_ref[0, pl.ds(ks, nkeys), :]
            vv = v_ref[0, pl.ds(ks, nkeys), :]
            dout = do_ref[0, pl.ds(qs, ATTN_BLOCK), :]
            lse_b = lse_ref[0, pl.ds(qs, ATTN_BLOCK), :]
            dd_b = dd_ref[0, pl.ds(qs, ATTN_BLOCK), :]
            dq = jnp.zeros((ATTN_BLOCK, LANES), F32)
            dkk = jnp.zeros((nkeys, LANES), F32)
            dvv = jnp.zeros((nkeys, LANES), F32)
            for e in range(2):
                msk = head0 if e == 0 else jnp.logical_not(head0)
                c0 = e * HEAD_DIM
                qe = jnp.where(msk, q, jnp.zeros_like(q))
                doe = jnp.where(msk, dout, jnp.zeros_like(dout))
                kke = jnp.where(msk, kk, jnp.zeros_like(kk))
                s = lax.dot_general(qe, kk, nt, preferred_element_type=F32) * QK_SCALE
                s = s + (b_ref[e, :, ATTN_BLOCK:] if first else b_ref[e])
                p = jnp.exp(s - lse_b[:, c0:c0 + 1])
                dp = lax.dot_general(doe, vv, nt, preferred_element_type=F32)
                ds = p * (dp - dd_b[:, c0:c0 + 1])
                if first:
                    db_ref[e, :, ATTN_BLOCK:] += ds
                else:
                    db_ref[e] += ds
                dsb = (ds * QK_SCALE).astype(BF16)
                dq = dq + jnp.dot(dsb, kke, preferred_element_type=F32)
                dkk = dkk + lax.dot_general(dsb, qe, tn, preferred_element_type=F32)
                dvv = dvv + lax.dot_general(p.astype(BF16), doe, tn, preferred_element_type=F32)
            dq_ref[0, pl.ds(qs, ATTN_BLOCK), :] = dq
            dk_ref[0, pl.ds(ks, nkeys), :] += dkk
            dv_ref[0, pl.ds(ks, nkeys), :] += dvv

        block(0, True)
        if nb > 1:
            def loop(n, c):
                block(n, False)
                return c
            lax.fori_loop(1, nb, loop, 0)

    H = AW // HEAD_DIM
    qv = qkv.reshape(B, L, d * 3 * AW)
    view = lambda t: t.reshape(B, L, d * AW)
    dq, dk, dv, db = pl.pallas_call(
        body, grid=(HP, B, d),
        in_specs=[q_spec, k_spec, v_spec, o_spec, o_spec, o_spec, bias_spec],
        out_specs=[o_spec, o_spec, o_spec, bias_spec],
        out_shape=[jax.ShapeDtypeStruct((B, L, d * AW), F32)] * 3
        + [jax.ShapeDtypeStruct((H, ATTN_BLOCK, 2 * ATTN_BLOCK), F32)],
        compiler_params=_params(3), name=name,
    )(qv, qv, qv, view(do), view(lse), view(dd), bias)
    flat = lambda t: t.reshape(B * S, AW)
    return flat(dq), flat(dk), flat(dv), db


def _attn_combine(ons, lses, gain, tm):
    T, AW = ons[0].shape

    def body(o1, o2, o3, l1, l2, l3, g_ref, attn_ref, lse_ref, mix_ref, r_ref):
        la, lb, lc = l1[...], l2[...], l3[...]
        m = jnp.maximum(jnp.maximum(la, lb), lc)
        ea, eb, ec = jnp.exp(la - m), jnp.exp(lb - m), jnp.exp(lc - m)
        den = ea + eb + ec
        attn = (ea * o1[...] + eb * o2[...] + ec * o3[...]) / den
        attn_ref[...] = attn
        lse_ref[...] = m + jnp.log(den)
        r = lax.rsqrt(jnp.mean(attn * attn, axis=-1, keepdims=True) + LN_EPS)
        mix_ref[...] = (attn * r * g_ref[...]).astype(BF16)
        r_ref[...] = jnp.broadcast_to(r, (tm, LANES))

    row = pl.BlockSpec((tm, AW), lambda i: (i, 0))
    return pl.pallas_call(
        body, grid=(T // tm,),
        in_specs=[row] * 6 + [pl.BlockSpec((1, AW), lambda i: (0, 0))],
        out_specs=[row, row, row, pl.BlockSpec((tm, LANES), lambda i: (i, 0))],
        out_shape=[jax.ShapeDtypeStruct((T, AW), F32), jax.ShapeDtypeStruct((T, AW), F32),
                   jax.ShapeDtypeStruct((T, AW), BF16), jax.ShapeDtypeStruct((T, LANES), F32)],
        compiler_params=_params(1), name="attn_combine",
    )(*ons, *lses, gain)


def _to_sub(src_ref, stage_ref, dsts, S):
    stage_ref[...] = src_ref[0].astype(F32)
    for (_, d), dst in zip(DILATED_CONFIGS[1:], dsts):
        L = S // d
        for r in range(d):
            dst[r * L:(r + 1) * L, :] = stage_ref[pl.ds(r, L, stride=d), :].astype(dst.dtype)


def _branch_blocks(S, d, block):
    nb = S // d // ATTN_BLOCK
    inner_unroll = 3 if (nb - 1) % 3 == 0 else 1

    def per_residue(r, c):
        block(r * nb, True)
        if nb > 1:
            def inner(n, c2):
                block(r * nb + n, False)
                return c2
            lax.fori_loop(1, nb, inner, 0, unroll=inner_unroll)
        return c

    lax.fori_loop(0, d, per_residue, 0, unroll=4 if nb == 1 else 1)


def _attention_fwd(qkv, bias_all, B, S, AW):
    HP = AW // LANES
    nt = MM_DIMS["nt"]

    def body(q_ref, k_ref, v_ref, b_ref, o_ref, lse_ref, stage, q4, q16, k4, k16, v4, v16, o1, l1, o4, l4, o16, l16):
        head0 = lax.broadcasted_iota(jnp.int32, (1, LANES), 1) < HEAD_DIM
        _to_sub(q_ref, stage, (q4, q16), S)
        _to_sub(k_ref, stage, (k4, k16), S)
        _to_sub(v_ref, stage, (v4, v16), S)
        srcs = ((q_ref.at[0], k_ref.at[0], v_ref.at[0], o1, l1), (q4, k4, v4, o4, l4), (q16, k16, v16, o16, l16))
        for bi, (_, d) in enumerate(DILATED_CONFIGS):
            qs_ref, ks_ref, vs_ref, od_ref, ld_ref = srcs[bi]

            def block(g, first, bi=bi, qs_ref=qs_ref, ks_ref=ks_ref, vs_ref=vs_ref, od_ref=od_ref, ld_ref=ld_ref):
                qs = pl.multiple_of(g * ATTN_BLOCK, ATTN_BLOCK)
                nkeys = ATTN_BLOCK if first else 2 * ATTN_BLOCK
                ks = qs if first else pl.multiple_of(qs - ATTN_BLOCK, ATTN_BLOCK)
                q = qs_ref[pl.ds(qs, ATTN_BLOCK), :]
                kk = ks_ref[pl.ds(ks, nkeys), :]
                vv = vs_ref[pl.ds(ks, nkeys), :]
                outs, lses = [], []
                for e in range(2):
                    msk = head0 if e == 0 else jnp.logical_not(head0)
                    qe = jnp.where(msk, q * QK_SCALE, jnp.zeros_like(q))
                    s = lax.dot_general(qe, kk, nt, preferred_element_type=F32)
                    s = s + (b_ref[bi, e, :, ATTN_BLOCK:] if first else b_ref[bi, e])
                    m = jnp.max(s, axis=-1, keepdims=True)
                    p = jnp.exp(s - m)
                    l = jnp.sum(p, axis=-1, keepdims=True)
                    o = jnp.dot(p.astype(BF16), vv, preferred_element_type=F32)
                    outs.append(o / l)
                    lses.append(jnp.broadcast_to(m + jnp.log(l), (ATTN_BLOCK, LANES)))
                od_ref[pl.ds(qs, ATTN_BLOCK), :] = jnp.where(head0, outs[0], outs[1])
                ld_ref[pl.ds(qs, ATTN_BLOCK), :] = jnp.where(head0, lses[0], lses[1])

            _branch_blocks(S, d, block)

        def natural(sub_ref, d):
            L = S // d
            for r in range(d):
                stage[pl.ds(r, L, stride=d), :] = sub_ref[r * L:(r + 1) * L, :]
            return stage[...]

        la = l1[...]
        lb = natural(l4, 4)
        lc = natural(l16, 16)
        m = jnp.maximum(jnp.maximum(la, lb), lc)
        ea, eb, ec = jnp.exp(la - m), jnp.exp(lb - m), jnp.exp(lc - m)
        den = ea + eb + ec
        lse_ref[0] = m + jnp.log(den)
        acc = ea * o1[...]
        acc = acc + eb * natural(o4, 4)
        acc = acc + ec * natural(o16, 16)
        o_ref[0] = acc / den

    blk = lambda off: pl.BlockSpec((1, S, LANES), lambda b, h: (b, 0, off + h))
    qv = qkv.reshape(B, S, 3 * AW)
    sub_b = pltpu.VMEM((S, LANES), BF16)
    sub_f = pltpu.VMEM((S, LANES), F32)
    o, lse = pl.pallas_call(
        body, grid=(B, HP),
        in_specs=[blk(0), blk(HP), blk(2 * HP),
                  pl.BlockSpec((3, 2, ATTN_BLOCK, 2 * ATTN_BLOCK), lambda b, h: (0, h, 0, 0))],
        out_specs=[blk(0), blk(0)],
        out_shape=[jax.ShapeDtypeStruct((B, S, AW), F32)] * 2,
        scratch_shapes=[sub_f] + [sub_b] * 6 + [sub_f] * 6,
        compiler_params=_params(2), name="attention_fwd",
    )(qv, qv, qv, bias_all)
    return o.reshape(B * S, AW), lse.reshape(B * S, AW)


def _attention_bwd(qkv, do, lse, dd, bias_all, B, S, AW):
    HP = AW // LANES
    H = AW // HEAD_DIM
    nt, tn = MM_DIMS["nt"], MM_DIMS["tn"]

    def body(q_ref, k_ref, v_ref, do_ref, lse_ref, dd_ref, b_ref,
             dq_ref, dk_ref, dv_ref, csq_ref, csk_ref, csv_ref, db_ref,
             stage, q4, q16, k4, k16, v4, v16, g4, g16, l4, l16, d4, d16,
             aq1, ak1, av1, aq4, ak4, av4, aq16, ak16, av16):
        head0 = lax.broadcasted_iota(jnp.int32, (1, LANES), 1) < HEAD_DIM
        first_b = pl.program_id(1) == 0

        @pl.when(first_b)
        def _():
            db_ref[...] = jnp.zeros_like(db_ref)

        _to_sub(q_ref, stage, (q4, q16), S)
        _to_sub(k_ref, stage, (k4, k16), S)
        _to_sub(v_ref, stage, (v4, v16), S)
        _to_sub(do_ref, stage, (g4, g16), S)
        _to_sub(lse_ref, stage, (l4, l16), S)
        _to_sub(dd_ref, stage, (d4, d16), S)
        for acc in (ak1, av1, ak4, av4, ak16, av16):
            acc[...] = jnp.zeros_like(acc)
        srcs = ((q_ref.at[0], k_ref.at[0], v_ref.at[0], do_ref.at[0], lse_ref.at[0], dd_ref.at[0], aq1, ak1, av1),
                (q4, k4, v4, g4, l4, d4, aq4, ak4, av4), (q16, k16, v16, g16, l16, d16, aq16, ak16, av16))
        for bi, (_, d) in enumerate(DILATED_CONFIGS):
            def block(g, first, bi=bi, refs=srcs[bi]):
                qs_ref, ks_ref, vs_ref, gs_ref, ls_ref, ds_ref, aq, ak, av = refs
                qs = pl.multiple_of(g * ATTN_BLOCK, ATTN_BLOCK)
                nkeys = ATTN_BLOCK if first else 2 * ATTN_BLOCK
                ks = qs if first else pl.multiple_of(qs - ATTN_BLOCK, ATTN_BLOCK)
                q = qs_ref[pl.ds(qs, ATTN_BLOCK), :]
                kk = ks_ref[pl.ds(ks, nkeys), :]
                vv = vs_ref[pl.ds(ks, nkeys), :]
                dout = gs_ref[pl.ds(qs, ATTN_BLOCK), :]
                lse_b = ls_ref[pl.ds(qs, ATTN_BLOCK), :]
                dd_b = ds_ref[pl.ds(qs, ATTN_BLOCK), :]
                dq = jnp.zeros((ATTN_BLOCK, LANES), F32)
                dkk = jnp.zeros((nkeys, LANES), F32)
                dvv = jnp.zeros((nkeys, LANES), F32)
                for e in range(2):
                    msk = head0 if e == 0 else jnp.logical_not(head0)
                    c0 = e * HEAD_DIM
                    qe = jnp.where(msk, q * QK_SCALE, jnp.zeros_like(q))
                    doe = jnp.where(msk, dout, jnp.zeros_like(dout))
                    kke = jnp.where(msk, kk * QK_SCALE, jnp.zeros_like(kk))
                    s = lax.dot_general(qe, kk, nt, preferred_element_type=F32)
                    s = s + (b_ref[bi, e, :, ATTN_BLOCK:] if first else b_ref[bi, e])
                    p = jnp.exp(s - lse_b[:, c0:c0 + 1])
                    dp = lax.dot_general(doe, vv, nt, preferred_element_type=F32)
                    ds = p * (dp - dd_b[:, c0:c0 + 1])
                    if first:
                        db_ref[bi, e, :, ATTN_BLOCK:] += ds
                    else:
                        db_ref[bi, e] += ds
                    dsb = ds.astype(BF16)
                    dq = dq + jnp.dot(dsb, kke, preferred_element_type=F32)
                    dkk = dkk + lax.dot_general(dsb, qe, tn, preferred_element_type=F32)
                    dvv = dvv + lax.dot_general(p.astype(BF16), doe, tn, preferred_element_type=F32)
                aq[pl.ds(qs, ATTN_BLOCK), :] = dq
                ak[pl.ds(ks, nkeys), :] += dkk
                av[pl.ds(ks, nkeys), :] += dvv

            _branch_blocks(S, d, block)

        for a1, a4, a16, out_ref, cs_ref in ((aq1, aq4, aq16, dq_ref, csq_ref), (ak1, ak4, ak16, dk_ref, csk_ref),
                                             (av1, av4, av16, dv_ref, csv_ref)):
            stage[...] = a1[...]
            for d, sub in ((4, a4), (16, a16)):
                L = S // d
                for r in range(d):
                    stage[pl.ds(r, L, stride=d), :] += sub[r * L:(r + 1) * L, :]
            tot = stage[...]
            out_ref[0] = tot.astype(out_ref.dtype)
            _accumulate(cs_ref, first_b, jnp.sum(tot, axis=0, keepdims=True))

    blk = lambda off: pl.BlockSpec((1, S, LANES), lambda h, b: (b, 0, off + h))
    cs_spec = pl.BlockSpec((1, LANES), lambda h, b: (0, h))
    bias_spec = pl.BlockSpec((3, 2, ATTN_BLOCK, 2 * ATTN_BLOCK), lambda h, b: (0, h, 0, 0))
    qv = qkv.reshape(B, S, 3 * AW)
    view = lambda t: t.reshape(B, S, AW)
    sub_b = pltpu.VMEM((S, LANES), BF16)
    sub_f = pltpu.VMEM((S, LANES), F32)
    res = pl.pallas_call(
        body, grid=(HP, B),
        in_specs=[blk(0), blk(HP), blk(2 * HP), blk(0), blk(0), blk(0), bias_spec],
        out_specs=[blk(0), blk(0), blk(0), cs_spec, cs_spec, cs_spec, bias_spec],
        out_shape=[jax.ShapeDtypeStruct((B, S, AW), BF16)] * 3 + [jax.ShapeDtypeStruct((1, AW), F32)] * 3
        + [jax.ShapeDtypeStruct((3, H, ATTN_BLOCK, 2 * ATTN_BLOCK), F32)],
        scratch_shapes=[sub_f] + [sub_b] * 8 + [sub_f] * 4 + [sub_f] * 9,
        compiler_params=_params(2), name="attention_bwd",
    )(qv, qv, qv, view(do), view(lse), view(dd), bias_all)
    flat = lambda t: t.reshape(B * S, AW)
    return flat(res[0]), flat(res[1]), flat(res[2]), res[3], res[4], res[5], res[6]


def _regroup(src, stage, dst, d, S, off=0):
    if d == 1:
        dst[off:off + S, :] = src.astype(dst.dtype)
        return
    stage[...] = src.astype(F32)
    L = S // d
    for r in range(d):
        dst[off + r * L:off + (r + 1) * L, :] = stage[pl.ds(r, L, stride=d), :].astype(dst.dtype)


def _ungroup(sub_ref, off, nat_ref, d, S, add):
    L = S // d
    for r in range(d):
        rows = pl.ds(0, S) if d == 1 else pl.ds(r, L, stride=d)
        val = sub_ref[off + r * L:off + (r + 1) * L, :]
        if add:
            nat_ref[rows, :] += val
        else:
            nat_ref[rows, :] = val


def _branch_keys(ks, vs, S, nb, g_idx):
    blk3 = (S // ATTN_BLOCK, ATTN_BLOCK, LANES)
    kc3 = ks[ATTN_BLOCK:ATTN_BLOCK + S, :].reshape(blk3)
    vc3 = vs[ATTN_BLOCK:ATTN_BLOCK + S, :].reshape(blk3)
    if nb == 1:
        return kc3, vc3, None
    kk3 = jnp.concatenate([ks[0:S, :].reshape(blk3), kc3], axis=1)
    vv3 = jnp.concatenate([vs[0:S, :].reshape(blk3), vc3], axis=1)
    col = lax.broadcasted_iota(jnp.int32, (1, 1, 2 * ATTN_BLOCK), 2)
    dead = jnp.logical_and((g_idx & (nb - 1)) == 0, col < ATTN_BLOCK)
    return kk3, vv3, dead


def _branch_scores(qe, kk3, b_ref, bi, e, dead):
    s = jnp.einsum("gqe,gke->gqk", qe, kk3, preferred_element_type=F32)
    if dead is None:
        return s + b_ref[bi, e, :, ATTN_BLOCK:]
    return jnp.where(dead, NEG_INF, s + b_ref[bi, e])


def _attention_fwd(qkv, bias_all, B, S, AW, bg=None):
    HP = AW // LANES
    G = S // ATTN_BLOCK
    blk3 = (G, ATTN_BLOCK, LANES)

    def body(refs, bg_hook):
        q_ref, k_ref, v_ref, b_ref, o_ref, lse_ref, stage, qs, ks, vs, ot, lt, on0, on1, on2, ln0, ln1, ln2 = refs
        bg_hook(False)
        head0 = lax.broadcasted_iota(jnp.int32, (1, 1, LANES), 2) < HEAD_DIM
        g_idx = lax.broadcasted_iota(jnp.int32, (G, 1, 1), 0)
        ks[0:ATTN_BLOCK, :] = jnp.zeros((ATTN_BLOCK, LANES), BF16)
        vs[0:ATTN_BLOCK, :] = jnp.zeros((ATTN_BLOCK, LANES), BF16)
        nat_o, nat_l = (on0, on1, on2), (ln0, ln1, ln2)
        for bi, (_, d) in enumerate(DILATED_CONFIGS):
            nb = S // d // ATTN_BLOCK
            _regroup(q_ref[0], stage, qs, d, S)
            _regroup(k_ref[0], stage, ks, d, S, ATTN_BLOCK)
            _regroup(v_ref[0], stage, vs, d, S, ATTN_BLOCK)
            q3 = qs[...].reshape(blk3) * QK_SCALE
            kk3, vv3, dead = _branch_keys(ks, vs, S, nb, g_idx)
            outs, lses = [], []
            for e in range(2):
                msk = head0 if e == 0 else jnp.logical_not(head0)
                qe = jnp.where(msk, q3, jnp.zeros_like(q3))
                s = _branch_scores(qe, kk3, b_ref, bi, e, dead)
                m = jnp.max(s, axis=-1, keepdims=True)
                p = jnp.exp(s - m)
                l = jnp.sum(p, axis=-1, keepdims=True)
                o = jnp.einsum("gqk,gke->gqe", p.astype(BF16), vv3, preferred_element_type=F32)
                outs.append(o / l)
                lses.append(jnp.broadcast_to(m + jnp.log(l), blk3))
            ot[...] = jnp.where(head0, outs[0], outs[1]).reshape(S, LANES)
            lt[...] = jnp.where(head0, lses[0], lses[1]).reshape(S, LANES)
            _ungroup(ot, 0, nat_o[bi], d, S, add=False)
            _ungroup(lt, 0, nat_l[bi], d, S, add=False)

        la, lb, lc = ln0[...], ln1[...], ln2[...]
        m = jnp.maximum(jnp.maximum(la, lb), lc)
        ea, eb, ec = jnp.exp(la - m), jnp.exp(lb - m), jnp.exp(lc - m)
        den = ea + eb + ec
        lse_ref[0] = m + jnp.log(den)
        o_ref[0] = (ea * on0[...] + eb * on1[...] + ec * on2[...]) / den
        bg_hook(True)

    blk = lambda off: pl.BlockSpec((1, S, LANES), lambda b, h: (b, 0, off + h))
    qv = qkv.reshape(B, S, 3 * AW)
    sub_f = pltpu.VMEM((S, LANES), F32)
    pad_b = pltpu.VMEM((S + ATTN_BLOCK, LANES), BF16)
    res = _hosted_call(
        body, bg, grid=(B, HP),
        in_specs=[blk(0), blk(HP), blk(2 * HP),
                  pl.BlockSpec((3, 2, ATTN_BLOCK, 2 * ATTN_BLOCK), lambda b, h: (0, h, 0, 0))],
        out_specs=[blk(0), blk(0)],
        out_shape=[jax.ShapeDtypeStruct((B, S, AW), F32)] * 2,
        scratch_shapes=[sub_f, pltpu.VMEM((S, LANES), BF16), pad_b, pad_b] + [sub_f] * 8,
        operands=[qv, qv, qv, bias_all], name="attention_fwd")
    return (res[0].reshape(B * S, AW), res[1].reshape(B * S, AW)) + tuple(res[2:])


def _attention_bwd(qkv, do, lse, dd, bias_all, B, S, AW, bg=None):
    HP = AW // LANES
    H = AW // HEAD_DIM
    G = S // ATTN_BLOCK
    blk3 = (G, ATTN_BLOCK, LANES)
    PAD = ATTN_BLOCK

    def body(refs, bg_hook):
        (q_ref, k_ref, v_ref, do_ref, lse_ref, dd_ref, b_ref,
         dq_ref, dk_ref, dv_ref, csq_ref, csk_ref, csv_ref, db_ref,
         stage, qs, ks, vs, gs, ls, ds_, tq, tk, tv, accq, acck, accv) = refs
        bg_hook(False)
        head0 = lax.broadcasted_iota(jnp.int32, (1, 1, LANES), 2) < HEAD_DIM
        g_idx = lax.broadcasted_iota(jnp.int32, (G, 1, 1), 0)
        first_b = pl.program_id(1) == 0

        @pl.when(first_b)
        def _():
            db_ref[...] = jnp.zeros_like(db_ref)

        ks[0:PAD, :] = jnp.zeros((PAD, LANES), BF16)
        vs[0:PAD, :] = jnp.zeros((PAD, LANES), BF16)
        tk[0:PAD, :] = jnp.zeros((PAD, LANES), F32)
        tv[0:PAD, :] = jnp.zeros((PAD, LANES), F32)
        for bi, (_, d) in enumerate(DILATED_CONFIGS):
            nb = S // d // ATTN_BLOCK
            _regroup(q_ref[0], stage, qs, d, S)
            _regroup(k_ref[0], stage, ks, d, S, PAD)
            _regroup(v_ref[0], stage, vs, d, S, PAD)
            _regroup(do_ref[0], stage, gs, d, S)
            _regroup(lse_ref[0], stage, ls, d, S)
            _regroup(dd_ref[0], stage, ds_, d, S)
            q3 = qs[...].reshape(blk3) * QK_SCALE
            do3 = gs[...].reshape(blk3)
            lse3 = ls[...].reshape(blk3)
            dd3 = ds_[...].reshape(blk3)
            kk3, vv3, dead = _branch_keys(ks, vs, S, nb, g_idx)
            dq = jnp.zeros(blk3, F32)
            dkk = jnp.zeros(kk3.shape, F32)
            dvv = jnp.zeros(kk3.shape, F32)
            for e in range(2):
                msk = head0 if e == 0 else jnp.logical_not(head0)
                c0 = e * HEAD_DIM
                qe = jnp.where(msk, q3, jnp.zeros_like(q3))
                doe = jnp.where(msk, do3, jnp.zeros_like(do3))
                ke = jnp.where(msk, kk3 * QK_SCALE, jnp.zeros_like(kk3))
                s = _branch_scores(qe, kk3, b_ref, bi, e, dead)
                p = jnp.exp(s - lse3[:, :, c0:c0 + 1])
                dp = jnp.einsum("gqe,gke->gqk", doe, vv3, preferred_element_type=F32)
                dsc = p * (dp - dd3[:, :, c0:c0 + 1])
                if dead is None:
                    db_ref[bi, e, :, ATTN_BLOCK:] += jnp.sum(dsc, axis=0)
                else:
                    db_ref[bi, e] += jnp.sum(dsc, axis=0)
                dsb = dsc.astype(BF16)
                dq = dq + jnp.einsum("gqk,gke->gqe", dsb, ke, preferred_element_type=F32)
                dkk = dkk + jnp.einsum("gqk,gqe->gke", dsb, qe, preferred_element_type=F32)
                dvv = dvv + jnp.einsum("gqk,gqe->gke", p.astype(BF16), doe, preferred_element_type=F32)
            tq[...] = dq.reshape(S, LANES)
            if dead is None:
                tk[PAD:PAD + S, :] = dkk.reshape(S, LANES)
                tv[PAD:PAD + S, :] = dvv.reshape(S, LANES)
            else:
                tk[PAD:PAD + S, :] = dkk[:, ATTN_BLOCK:, :].reshape(S, LANES)
                tv[PAD:PAD + S, :] = dvv[:, ATTN_BLOCK:, :].reshape(S, LANES)
                tk[0:S, :] += dkk[:, :ATTN_BLOCK, :].reshape(S, LANES)
                tv[0:S, :] += dvv[:, :ATTN_BLOCK, :].reshape(S, LANES)
            _ungroup(tq, 0, accq, d, S, add=bi > 0)
            _ungroup(tk, PAD, acck, d, S, add=bi > 0)
            _ungroup(tv, PAD, accv, d, S, add=bi > 0)

        for acc, out_ref, cs_ref in ((accq, dq_ref, csq_ref), (acck, dk_ref, csk_ref), (accv, dv_ref, csv_ref)):
            tot = acc[...]
            out_ref[0] = tot.astype(out_ref.dtype)
            _accumulate(cs_ref, first_b, jnp.sum(tot, axis=0, keepdims=True))
        bg_hook(True)

    blk = lambda off: pl.BlockSpec((1, S, LANES), lambda h, b: (b, 0, off + h))
    cs_spec = pl.BlockSpec((1, LANES), lambda h, b: (0, h))
    bias_spec = pl.BlockSpec((3, 2, ATTN_BLOCK, 2 * ATTN_BLOCK), lambda h, b: (0, h, 0, 0))
    qv = qkv.reshape(B, S, 3 * AW)
    view = lambda t: t.reshape(B, S, AW)
    sub_b = pltpu.VMEM((S, LANES), BF16)
    sub_f = pltpu.VMEM((S, LANES), F32)
    pad_b = pltpu.VMEM((S + PAD, LANES), BF16)
    pad_f = pltpu.VMEM((S + PAD, LANES), F32)
    res = _hosted_call(
        body, bg, grid=(HP, B),
        in_specs=[blk(0), blk(HP), blk(2 * HP), blk(0), blk(0), blk(0), bias_spec],
        out_specs=[blk(0), blk(0), blk(0), cs_spec, cs_spec, cs_spec, bias_spec],
        out_shape=[jax.ShapeDtypeStruct((B, S, AW), BF16)] * 3 + [jax.ShapeDtypeStruct((1, AW), F32)] * 3
        + [jax.ShapeDtypeStruct((3, H, ATTN_BLOCK, 2 * ATTN_BLOCK), F32)],
        scratch_shapes=[sub_f, sub_b, pad_b, pad_b, sub_b, sub_f, sub_f, sub_f, pad_f, pad_f, sub_f, sub_f, sub_f],
        operands=[qv, qv, qv, view(do), view(lse), view(dd), bias_all], name="attention_bwd")
    flat = lambda t: t.reshape(B * S, AW)
    return (flat(res[0]), flat(res[1]), flat(res[2]), res[3], res[4], res[5], res[6]) + tuple(res[7:])


def _attn_norm(attn, gain, tm):
    T, AW = attn.shape

    def body(a_ref, g_ref, mix_ref, r_ref):
        a = a_ref[...]
        r = lax.rsqrt(jnp.mean(a * a, axis=-1, keepdims=True) + LN_EPS)
        mix_ref[...] = (a * r * g_ref[...]).astype(BF16)
        r_ref[...] = jnp.broadcast_to(r, (tm, LANES))

    row = pl.BlockSpec((tm, AW), lambda i: (i, 0))
    return pl.pallas_call(
        body, grid=(T // tm,), in_specs=[row, pl.BlockSpec((1, AW), lambda i: (0, 0))],
        out_specs=[row, pl.BlockSpec((tm, LANES), lambda i: (i, 0))],
        out_shape=[jax.ShapeDtypeStruct((T, AW), BF16), jax.ShapeDtypeStruct((T, LANES), F32)],
        compiler_params=_params(1), name="attn_norm",
    )(attn, gain)


def _attn_pre_bwd(dmixed, attn, rstd, gain, tm):
    T, AW = attn.shape
    ones_np = np.kron(np.eye(AW // HEAD_DIM, dtype=np.float32), np.ones((HEAD_DIM, HEAD_DIM), np.float32))
    ones_bd = jnp.asarray(ones_np, dtype=BF16)

    def body(dm_ref, a_ref, r_ref, g_ref, ones_ref, do_ref, dd_ref, dg_ref):
        i = pl.program_id(0)
        dm = dm_ref[...]
        a = a_ref[...]
        r = r_ref[:, 0:1]
        dxn = dm * g_ref[...]
        da = r * (dxn - a * (r * r) * jnp.mean(dxn * a, axis=-1, keepdims=True))
        do_ref[...] = da.astype(BF16)
        hi, lo = _split_hi_lo(da * a)
        dd_ref[...] = (jnp.dot(hi, ones_ref[...], preferred_element_type=F32)
                       + jnp.dot(lo, ones_ref[...], preferred_element_type=F32))
        _accumulate(dg_ref, i == 0, jnp.sum(dm * a * r, axis=0, keepdims=True))

    row = pl.BlockSpec((tm, AW), lambda i: (i, 0))
    vec = pl.BlockSpec((1, AW), lambda i: (0, 0))
    return pl.pallas_call(
        body, grid=(T // tm,),
        in_specs=[row, row, pl.BlockSpec((tm, LANES), lambda i: (i, 0)), vec,
                  pl.BlockSpec((AW, AW), lambda i: (0, 0))],
        out_specs=[row, row, vec],
        out_shape=[jax.ShapeDtypeStruct((T, AW), BF16), jax.ShapeDtypeStruct((T, AW), F32),
                   jax.ShapeDtypeStruct((1, AW), F32)],
        compiler_params=_params(1), name="attn_pre_bwd",
    )(dmixed, attn, rstd, gain, ones_bd)


def _conv_branch_fwd_math(a, g, w_ref, cb, lg, lb, row):
    sg = _sigmoid(g)
    u0 = a * sg
    uc = jnp.zeros_like(u0) + cb
    for k in range(CONV_KERNEL):
        uc = uc + w_ref[k:k + 1, :] * _shift_down(u0, CONV_KERNEL - 1 - k, row)
    ul, xh, r = _ln_fwd(uc, lg, lb)
    su = _sigmoid(ul)
    u = ul * su
    return sg, u0, ul, xh, r, su, u


def _conv_fwd(ag, conv_w, conv_b, ln_g, ln_b, norm_g, B, S, CW):
    def body(a_ref, g_ref, w_ref, cb_ref, lg_ref, lb_ref, ng_ref, o_ref):
        row = lax.broadcasted_iota(jnp.int32, (S, CW), 0)
        _, _, _, _, _, _, u = _conv_branch_fwd_math(a_ref[0], g_ref[0], w_ref, cb_ref[...], lg_ref[...],
                                                    lb_ref[...], row)
        rr = lax.rsqrt(jnp.mean(u * u, axis=-1, keepdims=True) + LN_EPS)
        o_ref[0] = (u * rr * ng_ref[...]).astype(BF16)

    vec = pl.BlockSpec((1, CW), lambda b: (0, 0))
    out = pl.pallas_call(
        body, grid=(B,),
        in_specs=[pl.BlockSpec((1, S, CW), lambda b: (b, 0, 0)), pl.BlockSpec((1, S, CW), lambda b: (b, 0, 1)),
                  pl.BlockSpec((CONV_KERNEL, CW), lambda b: (0, 0)), vec, vec, vec, vec],
        out_specs=pl.BlockSpec((1, S, CW), lambda b: (b, 0, 0)),
        out_shape=jax.ShapeDtypeStruct((B, S, CW), BF16),
        compiler_params=_params(1), name="conv_fwd",
    )(ag.reshape(B, S, 2 * CW), ag.reshape(B, S, 2 * CW), conv_w, conv_b, ln_g, ln_b, norm_g)
    return out.reshape(B * S, CW)


def _conv_bwd(ag, dmixed, conv_w, conv_b, ln_g, ln_b, norm_g, B, S, CW, D):
    AW = D - CW
    assert AW % CW == 0

    def body(a_ref, g_ref, dm_ref, w_ref, cb_ref, lg_ref, lb_ref, ng_ref,
             dag_ref, dw_ref, dcb_ref, dlg_ref, dlb_ref, dng_ref):
        b = pl.program_id(0)
        row = lax.broadcasted_iota(jnp.int32, (S, CW), 0)
        a, g = a_ref[0], g_ref[0]
        sg, u0, ul, xh, r, su, u = _conv_branch_fwd_math(a, g, w_ref, cb_ref[...], lg_ref[...], lb_ref[...], row)
        rr = lax.rsqrt(jnp.mean(u * u, axis=-1, keepdims=True) + LN_EPS)
        dm = dm_ref[0]
        dxn = dm * ng_ref[...]
        du = rr * (dxn - u * (rr * rr) * jnp.mean(dxn * u, axis=-1, keepdims=True))
        dul = du * su * (1.0 + ul * (1.0 - su))
        duc = _ln_bwd(dul, xh, r, lg_ref[...])
        first = b == 0
        _accumulate(dng_ref, first, jnp.sum(dm * u * rr, axis=0, keepdims=True))
        _accumulate(dlg_ref, first, jnp.sum(dul * xh, axis=0, keepdims=True))
        _accumulate(dlb_ref, first, jnp.sum(dul, axis=0, keepdims=True))
        _accumulate(dcb_ref, first, jnp.sum(duc, axis=0, keepdims=True))

        @pl.when(first)
        def _():
            dw_ref[...] = jnp.zeros_like(dw_ref)

        du0 = jnp.zeros_like(u0)
        for k in range(CONV_KERNEL):
            sh = CONV_KERNEL - 1 - k
            dw_ref[k:k + 1, :] += jnp.sum(duc * _shift_down(u0, sh, row), axis=0, keepdims=True)
            du0 = du0 + w_ref[k:k + 1, :] * _shift_up(duc, sh, row)
        dag_ref[0, :, :CW] = du0 * sg
        dag_ref[0, :, CW:] = du0 * a * sg * (1.0 - sg)

    vec = pl.BlockSpec((1, CW), lambda b: (0, 0))
    wspec = pl.BlockSpec((CONV_KERNEL, CW), lambda b: (0, 0))
    agv = ag.reshape(B, S, 2 * CW)
    res = pl.pallas_call(
        body, grid=(B,),
        in_specs=[pl.BlockSpec((1, S, CW), lambda b: (b, 0, 0)), pl.BlockSpec((1, S, CW), lambda b: (b, 0, 1)),
                  pl.BlockSpec((1, S, CW), lambda b: (b, 0, AW // CW)), wspec, vec, vec, vec, vec],
        out_specs=[pl.BlockSpec((1, S, 2 * CW), lambda b: (b, 0, 0)), wspec, vec, vec, vec, vec],
        out_shape=[jax.ShapeDtypeStruct((B, S, 2 * CW), F32), jax.ShapeDtypeStruct((CONV_KERNEL, CW), F32)]
        + [jax.ShapeDtypeStruct((1, CW), F32)] * 4,
        compiler_params=_params(1), name="conv_bwd",
    )(agv, agv, dmixed.reshape(B, S, D), conv_w, conv_b, ln_g, ln_b, norm_g)
    return (res[0].reshape(B * S, 2 * CW),) + tuple(res[1:])


def _ffn_conv(x, w_ref, bias, row):
    y = jnp.zeros_like(x) + bias
    for k in range(FFN_CONV_KERNEL):
        y = y + w_ref[k:k + 1, :] * _shift_down(x, FFN_CONV_KERNEL - 1 - k, row)
    return y


def _ffn_specs(S, tc, nj, order):
    pick = (lambda b, j: (b, j)) if order == "bj" else (lambda j, b: (b, j))
    act = lambda off: pl.BlockSpec((1, S, tc), lambda *g: (pick(*g)[0], 0, off + pick(*g)[1]))
    cw = lambda off: pl.BlockSpec((FFN_CONV_KERNEL, tc), lambda *g: (0, off + pick(*g)[1]))
    cb = lambda off: pl.BlockSpec((1, tc), lambda *g: (0, off + pick(*g)[1]))
    return act, cw, cb


def _ffn_act(upre, cw, cb, B, S, DFF):
    tc = FFN_COLS
    nj = DFF // tc

    def body(ug_ref, uv_ref, wg_ref, wv_ref, bg_ref, bv_ref, o_ref):
        row = lax.broadcasted_iota(jnp.int32, (S, tc), 0)
        gate = _ffn_conv(ug_ref[0], wg_ref, bg_ref[...], row)
        val = _ffn_conv(uv_ref[0], wv_ref, bv_ref[...], row)
        o_ref[0] = (gate * _sigmoid(gate) * val).astype(BF16)

    act, cws, cbs = _ffn_specs(S, tc, nj, "bj")
    uv = upre.reshape(B, S, 2 * DFF)
    out = pl.pallas_call(
        body, grid=(B, nj), in_specs=[act(0), act(nj), cws(0), cws(nj), cbs(0), cbs(nj)], out_specs=act(0),
        out_shape=jax.ShapeDtypeStruct((B, S, DFF), BF16), compiler_params=_params(2), name="ffn_act",
    )(uv, uv, cw, cw, cb, cb)
    return out.reshape(B * S, DFF)


def _ffn_bwd(upre, dact, cw, cb, B, S, DFF):
    tc = FFN_COLS
    nj = DFF // tc

    def body(ug_ref, uv_ref, da_ref, wg_ref, wv_ref, bg_ref, bv_ref, dug_ref, duv_ref, dwg_ref, dwv_ref,
             dbg_ref, dbv_ref):
        first = pl.program_id(1) == 0
        row = lax.broadcasted_iota(jnp.int32, (S, tc), 0)
        ug, uv = ug_ref[0], uv_ref[0]
        gate = _ffn_conv(ug, wg_ref, bg_ref[...], row)
        val = _ffn_conv(uv, wv_ref, bv_ref[...], row)
        sg = _sigmoid(gate)
        dact_b = da_ref[0]
        dgate = dact_b * val * sg * (1.0 + gate * (1.0 - sg))
        dval = dact_b * gate * sg
        for dup, u, w_ref, du_ref, dw_ref, db_ref in ((dgate, ug, wg_ref, dug_ref, dwg_ref, dbg_ref),
                                                      (dval, uv, wv_ref, duv_ref, dwv_ref, dbv_ref)):
            _accumulate(db_ref, first, jnp.sum(dup, axis=0, keepdims=True))

            @pl.when(first)
            def _(dw_ref=dw_ref):
                dw_ref[...] = jnp.zeros_like(dw_ref)

            dupre = jnp.zeros_like(dup)
            for k in range(FFN_CONV_KERNEL):
                sh = FFN_CONV_KERNEL - 1 - k
                dw_ref[k:k + 1, :] += jnp.sum(dup * _shift_down(u, sh, row), axis=0, keepdims=True)
                dupre = dupre + w_ref[k:k + 1, :] * _shift_up(dup, sh, row)
            du_ref[0] = dupre.astype(BF16)

    act, cws, cbs = _ffn_specs(S, tc, nj, "jb")
    uv = upre.reshape(B, S, 2 * DFF)
    res = pl.pallas_call(
        body, grid=(nj, B),
        in_specs=[act(0), act(nj), act(0), cws(0), cws(nj), cbs(0), cbs(nj)],
        out_specs=[act(0), act(0), cws(0), cws(0), cbs(0), cbs(0)],
        out_shape=[jax.ShapeDtypeStruct((B, S, DFF), BF16)] * 2
        + [jax.ShapeDtypeStruct((FFN_CONV_KERNEL, DFF), F32)] * 2 + [jax.ShapeDtypeStruct((1, DFF), F32)] * 2,
        compiler_params=_params(2), name="ffn_bwd",
    )(uv, uv, dact.reshape(B, S, DFF), cw, cw, cb, cb)
    flat = lambda t: t.reshape(B * S, DFF)
    return (flat(res[0]), flat(res[1]), jnp.concatenate([res[2], res[3]], axis=1),
            jnp.concatenate([res[4], res[5]], axis=1))


FFN_HALO = 16


def _half_sequences(S):
    if S < 8 * FFN_HALO:
        return [(0, S, 0, S)]
    h = S // 2
    return [(0, h + FFN_HALO, 0, h), (h - FFN_HALO, S, FFN_HALO, h)]


def _w_up_block_spec(w_up_sh, tc, off):
    _, D, cs = w_up_sh.shape
    assert cs % tc == 0
    bps = cs // tc
    return pl.BlockSpec((1, D, tc), lambda j: ((off + j) // bps, 0, (off + j) % bps))


def _ffn_fwd_fused(x1b, w_up_sh, cw, cb, B, S, DFF):
    tc = FFN_COLS
    nj = DFF // tc
    D = x1b.shape[1]

    def body(x_ref, wg_ref, wv_ref, cwg_ref, cwv_ref, cbg_ref, cbv_ref, o_ref, up_ref):
        w = jnp.concatenate([wg_ref[0], wv_ref[0]], axis=1)
        row = lax.broadcasted_iota(jnp.int32, (S, tc), 0)
        for b in range(B):
            up = jnp.dot(x_ref[b], w, preferred_element_type=F32)
            up_ref[b] = up
            gate = _ffn_conv(up[:, :tc], cwg_ref, cbg_ref[...], row)
            val = _ffn_conv(up[:, tc:], cwv_ref, cbv_ref[...], row)
            o_ref[b] = (gate * _sigmoid(gate) * val).astype(BF16)

    cws = lambda off: pl.BlockSpec((FFN_CONV_KERNEL, tc), lambda j: (0, off + j))
    cbs = lambda off: pl.BlockSpec((1, tc), lambda j: (0, off + j))
    act, upre = pl.pallas_call(
        body, grid=(nj,),
        in_specs=[pl.BlockSpec((B, S, D), lambda j: (0, 0, 0), pipeline_mode=pl.Buffered(1)),
                  _w_up_block_spec(w_up_sh, tc, 0), _w_up_block_spec(w_up_sh, tc, nj),
                  cws(0), cws(nj), cbs(0), cbs(nj)],
        out_specs=[pl.BlockSpec((B, S, tc), lambda j: (0, 0, j)), pl.BlockSpec((B, S, 2 * tc), lambda j: (0, 0, j))],
        out_shape=[jax.ShapeDtypeStruct((B, S, DFF), BF16), jax.ShapeDtypeStruct((B, S, 2 * DFF), F32)],
        compiler_params=_params(1), name="ffn_fwd",
    )(x1b.reshape(B, S, D), w_up_sh, w_up_sh, cw, cw, cb, cb)
    return act.reshape(B * S, DFF), upre


def _ffn_bwd_fused(x1b, dz2b, upre, w_down, cw, cb, B, S, DFF):
    tc = FFN_COLS
    nj = DFF // tc
    D = x1b.shape[1]

    def body(x_ref, dz_ref, up_ref, wd_ref, cwg_ref, cwv_ref, cbg_ref, cbv_ref,
             dug_ref, duv_ref, dwu_ref, dwd_ref, dcw_ref, dcb_ref):
        first = pl.program_id(1) == 0
        dw_t = dwd = None
        dcb = [None, None]
        dcw = [[None] * FFN_CONV_KERNEL, [None] * FFN_CONV_KERNEL]
        add = lambda old, new: new if old is None else old + new
        for lo, hi, o0, on in _half_sequences(S):
            n = hi - lo
            own = slice(o0, o0 + on)
            row = lax.broadcasted_iota(jnp.int32, (n, tc), 0)
            x = x_ref[0, lo:hi, :]
            dz = dz_ref[0, lo:hi, :]
            ug, uv = up_ref[0, lo:hi, :tc], up_ref[0, lo:hi, tc:]
            gate = _ffn_conv(ug, cwg_ref, cbg_ref[...], row)
            val = _ffn_conv(uv, cwv_ref, cbv_ref[...], row)
            sg = _sigmoid(gate)
            act = (gate * sg * val).astype(BF16)
            dact = _dot(dz, wd_ref[...], "nt")
            dgate = dact * val * sg * (1.0 + gate * (1.0 - sg))
            dval = dact * gate * sg
            dupre = []
            for h, (dup, u, w_ref) in enumerate(((dgate, ug, cwg_ref), (dval, uv, cwv_ref))):
                dcb[h] = add(dcb[h], jnp.sum(dup[own], axis=0, keepdims=True))
                acc = jnp.zeros_like(dup)
                for k in range(FFN_CONV_KERNEL):
                    sh = FFN_CONV_KERNEL - 1 - k
                    dcw[h][k] = add(dcw[h][k], jnp.sum((dup * _shift_down(u, sh, row))[own], axis=0, keepdims=True))
                    acc = acc + w_ref[k:k + 1, :] * _shift_up(dup, sh, row)
                dupre.append(acc.astype(BF16)[own])
            dug_ref[0, lo + o0:lo + o0 + on, :] = dupre[0]
            duv_ref[0, lo + o0:lo + o0 + on, :] = dupre[1]
            dw_t = add(dw_t, _dot(jnp.concatenate(dupre, axis=1), x[own], "tn"))
            dwd = add(dwd, _dot(act[own], dz[own], "tn"))
        _accumulate(dwu_ref.at[0], first, dw_t[:tc])
        _accumulate(dwu_ref.at[1], first, dw_t[tc:])
        _accumulate(dwd_ref, first, dwd)
        for h in range(2):
            _accumulate(dcb_ref.at[h], first, dcb[h])
            for k in range(FFN_CONV_KERNEL):
                _accumulate(dcw_ref.at[k, pl.ds(h, 1), :], first, dcw[h][k])

    act_s, cws, cbs = _ffn_specs(S, tc, nj, "jb")
    seq = pl.BlockSpec((1, S, D), lambda j, b: (b, 0, 0))
    res = pl.pallas_call(
        body, grid=(nj, B),
        in_specs=[seq, seq, pl.BlockSpec((1, S, 2 * tc), lambda j, b: (b, 0, j)),
                  pl.BlockSpec((tc, D), lambda j, b: (j, 0)), cws(0), cws(nj), cbs(0), cbs(nj)],
        out_specs=[act_s(0), act_s(0), pl.BlockSpec((2, tc, D), lambda j, b: (0, j, 0)),
                   pl.BlockSpec((tc, D), lambda j, b: (j, 0)),
                   pl.BlockSpec((FFN_CONV_KERNEL, 2, tc), lambda j, b: (0, 0, j)),
                   pl.BlockSpec((2, 1, tc), lambda j, b: (0, 0, j))],
        out_shape=[jax.ShapeDtypeStruct((B, S, DFF), BF16)] * 2
        + [jax.ShapeDtypeStruct((2, DFF, D), F32), jax.ShapeDtypeStruct((DFF, D), F32),
           jax.ShapeDtypeStruct((FFN_CONV_KERNEL, 2, DFF), F32), jax.ShapeDtypeStruct((2, 1, DFF), F32)],
        compiler_params=_params(2), name="ffn_bwd",
    )(x1b.reshape(B, S, D), dz2b.reshape(B, S, D), upre, w_down, cw, cw, cb, cb)
    flat = lambda t: t.reshape(B * S, DFF)
    return flat(res[0]), flat(res[1]), res[2], res[3], res[4], res[5]


def _transpose(x, name):
    R, C = x.shape
    tr = LANES if R % LANES == 0 else R

    def body(x_ref, o_ref):
        o_ref[...] = x_ref[...].T

    return pl.pallas_call(
        body, grid=(R // tr,), in_specs=[pl.BlockSpec((tr, C), lambda i: (i, 0))],
        out_specs=pl.BlockSpec((C, tr), lambda i: (0, i)), out_shape=jax.ShapeDtypeStruct((C, R), F32),
        compiler_params=_params(1), name=name)(x)


def _dh_cat(dq, dk, dv, dag, tm):
    T, AW = dq.shape
    CW2 = dag.shape[1]
    W = 3 * AW + CW2

    def body(dq_ref, dk_ref, dv_ref, dag_ref, dh_ref, cs_ref):
        for c, ref in enumerate((dq_ref, dk_ref, dv_ref)):
            dh_ref[:, c * AW:(c + 1) * AW] = ref[...]
        dg = dag_ref[...]
        dh_ref[:, 3 * AW:] = dg.astype(BF16)
        _accumulate(cs_ref, pl.program_id(0) == 0, jnp.sum(dg, axis=0, keepdims=True))

    row = pl.BlockSpec((tm, AW), lambda i: (i, 0))
    return pl.pallas_call(
        body, grid=(T // tm,),
        in_specs=[row] * 3 + [pl.BlockSpec((tm, CW2), lambda i: (i, 0))],
        out_specs=[pl.BlockSpec((tm, W), lambda i: (i, 0)), pl.BlockSpec((1, CW2), lambda i: (0, 0))],
        out_shape=[jax.ShapeDtypeStruct((T, W), BF16), jax.ShapeDtypeStruct((1, CW2), F32)],
        compiler_params=_params(1), name="dh_cat",
    )(dq, dk, dv, dag)


def _local_step(x, target, rel_table, w_in, b_in, conv_w, conv_b, conv_ln_g, conv_ln_b, attn_norm_g,
                conv_norm_g, staged, ln1_g, ln1_b, ffn_cw, ffn_cb, ln2_g, ln2_b, ids):
    B, S, D = x.shape
    T = B * S
    AW = attn_norm_g.shape[-1]
    CW = conv_norm_g.shape[-1]
    H = AW // HEAD_DIM
    DFF = staged[2].shape[0] * staged[2].shape[1]
    INW = 3 * AW + 2 * CW
    xf = x.reshape(T, D)
    tf = target.reshape(T, D)
    tm = _row_tile(T, 512)
    tm_s = _row_tile(T, 256)

    bucket_np, mask_np = _bucket_tables()
    bucket = jnp.asarray(bucket_np)
    band_mask = jnp.asarray(mask_np)
    bias_all = _bias_build(rel_table.T, bucket, band_mask).reshape(3, H, ATTN_BLOCK, 2 * ATTN_BLOCK)

    tn_qkv = _col_tile(3 * AW, 1152)
    qkv = _mm_plain(xf, w_in[:, :3 * AW], mode="nn", tm=tm, tn=tn_qkv, tk=D, out_dtype=BF16,
                    bias=b_in[:, :3 * AW], name="mm_qkv")
    ag = _mm_plain(xf, w_in[:, 3 * AW:], mode="nn", tm=tm, tn=2 * CW, tk=D, out_dtype=F32,
                   bias=b_in[:, 3 * AW:], name="mm_ag")

    attn, lse, w_out_g, w_up_sh, w_down_g = _attention_fwd(qkv, bias_all, B, S, AW, bg=_bg_gather(staged))
    w_out = w_out_g.reshape(D, D)
    w_down = w_down_g.reshape(DFF, D)
    mixed_a, r_attn = _attn_norm(attn, attn_norm_g, tm_s)
    mixed_c = _conv_fwd(ag, conv_w, conv_b, conv_ln_g, conv_ln_b, conv_norm_g, B, S, CW)
    mixed = jnp.concatenate([mixed_a, mixed_c], axis=1)

    def ln1_epilogue(acc, i, j, extra_refs, out_refs):
        x_ref, g_ref, b_ref = extra_refs
        x1, xh, r = _ln_fwd(acc + ALPHA * x_ref[...], g_ref[...], b_ref[...])
        out_refs[0][...] = x1
        out_refs[1][...] = x1.astype(BF16)
        out_refs[2][...] = xh
        out_refs[3][...] = jnp.broadcast_to(r, (tm_s, LANES))

    rowD = lambda i, j, k: (i, 0)
    vecD = lambda i, j, k: (0, 0)
    x1, x1b, xh1, r1 = _matmul(
        mixed, w_out, mode="nn", tm=tm_s, tn=D, tk=D,
        extras=[(xf, (tm_s, D), rowD), (ln1_g, (1, D), vecD), (ln1_b, (1, D), vecD)],
        outs=[((T, D), F32, (tm_s, D), rowD), ((T, D), BF16, (tm_s, D), rowD), ((T, D), F32, (tm_s, D), rowD),
              ((T, LANES), F32, (tm_s, LANES), rowD)],
        epilogue=ln1_epilogue, name="mm_out_ln1")

    NS, _, cs = w_up_sh.shape
    half = NS // 2

    act, upre = _ffn_fwd_fused(x1b, w_up_sh, ffn_cw, ffn_cb, B, S, DFF)

    def ln2_epilogue(acc, i, j, extra_refs, out_refs):
        x1_ref, g_ref, b_ref, t_ref = extra_refs
        dz_ref, dzb_ref, loss_ref, dg_ref, db_ref = out_refs
        g = g_ref[...]
        y, xh, r = _ln_fwd(acc + ALPHA * x1_ref[...], g, b_ref[...])
        diff = y - t_ref[...]
        row_loss = jnp.sum(diff * diff, axis=1, keepdims=True)
        tile_loss = jnp.sum(row_loss, axis=0, keepdims=True) * (0.5 / D)
        dy = diff * (1.0 / D)
        dz = _ln_bwd(dy, xh, r, g)
        dz_ref[...] = dz
        dzb_ref[...] = dz.astype(BF16)
        first = i == 0
        _accumulate(loss_ref, first, jnp.broadcast_to(tile_loss, (1, LANES)))
        _accumulate(dg_ref, first, jnp.sum(dy * xh, axis=0, keepdims=True))
        _accumulate(db_ref, first, jnp.sum(dy, axis=0, keepdims=True))

    dz2, dz2b, loss_part, d_ln2_g, d_ln2_b = _matmul(
        act, w_down, mode="nn", tm=tm_s, tn=D, tk=DFF,
        extras=[(x1, (tm_s, D), rowD), (ln2_g, (1, D), vecD), (ln2_b, (1, D), vecD), (tf, (tm_s, D), rowD)],
        outs=[((T, D), F32, (tm_s, D), rowD), ((T, D), BF16, (tm_s, D), rowD),
              ((1, LANES), F32, (1, LANES), vecD), ((1, D), F32, (1, D), vecD), ((1, D), F32, (1, D), vecD)],
        epilogue=ln2_epilogue, name="mm_down_ln2_loss")

    dupre_g, dupre_v, d_w_up_t, d_w_down, d_ffn_cw2, d_ffn_cb2 = _ffn_bwd_fused(
        x1b, dz2b, upre, w_down, ffn_cw, ffn_cb, B, S, DFF)
    d_w_up_t = d_w_up_t.reshape(NS, cs, D)
    d_ffn_cw = d_ffn_cw2.reshape(FFN_CONV_KERNEL, 2 * DFF)
    d_ffn_cb = d_ffn_cb2.reshape(1, 2 * DFF)
    tk_t = _row_tile(T, 512)

    def ln1_bwd_epilogue(acc, i, j, extra_refs, out_refs):
        dz2_ref, xh_ref, r_ref, g_ref = extra_refs
        dz_ref, dzb_ref, dg_ref, db_ref = out_refs
        dx1 = acc + ALPHA * dz2_ref[...]
        xh = xh_ref[...]
        dz = _ln_bwd(dx1, xh, r_ref[:, 0:1], g_ref[...])
        dz_ref[...] = dz
        dzb_ref[...] = dz.astype(BF16)
        first = i == 0
        _accumulate(dg_ref, first, jnp.sum(dx1 * xh, axis=0, keepdims=True))
        _accumulate(db_ref, first, jnp.sum(dx1, axis=0, keepdims=True))

    early = [d_w_up_t, d_w_down.reshape(NS, DFF // NS, D)]
    dz1, dz1b, d_ln1_g, d_ln1_b, *sib_e = _matmul_general(
        [(dupre_g, (tm, cs), lambda i, j, k: (i, jnp.minimum(k, half - 1))),
         (dupre_v, (tm, cs), lambda i, j, k: (i, jnp.maximum(k - half, 0))),
         (w_up_sh, (1, D, cs), lambda i, j, k: (k, 0, 0))],
        lambda refs, i, j, k: _dot(jnp.where(k < half, refs[0][...], refs[1][...]), refs[2][0], "nt"),
        grid=(T // tm, 1, NS), tm=tm, tn=D,
        extras=[(dz2, (tm, D), rowD), (xh1, (tm, D), rowD), (r1, (tm, LANES), rowD), (ln1_g, (1, D), vecD)],
        outs=[((T, D), F32, (tm, D), rowD), ((T, D), BF16, (tm, D), rowD),
              ((1, D), F32, (1, D), vecD), ((1, D), F32, (1, D), vecD)],
        epilogue=ln1_bwd_epilogue, name="mm_dx1_ln1_bwd", bg=_bg_sibling_exchange(early))
    chip_e = [_pair_sum(g, s, ids, name="pair_sum_" + n) for g, s, n in zip(early, sib_e, ("w_up", "w_down"))]

    d_w_out = _mm_plain(mixed, dz1b, mode="tn", tm=D, tn=D, tk=tk_t, out_dtype=F32, name="mm_dw_out")
    early.append(d_w_out.reshape(NS, D // NS, D))
    dmixed, sib_out = _mm_plain(dz1b, w_out, mode="nt", tm=tm, tn=D, tk=D, out_dtype=F32, name="mm_dmixed",
                                bg=_bg_sibling_exchange(early[2:]))
    sib_e.append(sib_out)
    chip_e.append(_pair_sum(early[2], sib_out, ids, name="pair_sum_w_out"))

    dattn, dd, d_attn_norm_g = _attn_pre_bwd(dmixed, attn, r_attn, attn_norm_g, tm_s)
    dag, d_conv_w, d_conv_b, d_conv_ln_g, d_conv_ln_b, d_conv_norm_g = _conv_bwd(
        ag, dmixed, conv_w, conv_b, conv_ln_g, conv_ln_b, conv_norm_g, B, S, CW, D)

    dq, dk, dv, csq, csk, csv, dbias, *got_e = _attention_bwd(qkv, dattn, lse, dd, bias_all, B, S, AW,
                                                              bg=_bg_chip_exchange(chip_e))
    full_up, full_down, full_out = [_final_sum(g, s, r, ids, name="final_sum_" + n)
                                    for g, s, r, n in zip(early, sib_e, got_e, ("w_up", "w_down", "w_out"))]
    d_rel_table = _rel_grad(dbias.reshape(3, H, ATTN_BLOCK * 2 * ATTN_BLOCK), bucket).T
    dh, cs_ag = _dh_cat(dq, dk, dv, dag, tm_s)
    d_b_in = jnp.concatenate([csq, csk, csv, cs_ag], axis=1)

    d_w_in_t = _mm_plain(dh, xf, mode="tn", tm=_col_tile(INW, 1408), tn=D, tk=tk_t, out_dtype=F32, name="mm_dw_in")
    late = [d_w_in_t.reshape(NS, INW // NS, D)]
    sib_l = _sibling_exchange(late)
    chip_l = [_pair_sum(late[0], sib_l[0], ids, name="pair_sum_w_in")]
    small = dict(rel_table=d_rel_table, b_in=d_b_in, conv_w=d_conv_w, conv_b=d_conv_b, conv_ln_g=d_conv_ln_g,
                 conv_ln_b=d_conv_ln_b, attn_norm_g=d_attn_norm_g, conv_norm_g=d_conv_norm_g, ln1_g=d_ln1_g,
                 ln1_b=d_ln1_b, ffn_conv_w=d_ffn_cw, ffn_conv_b=d_ffn_cb, ln2_g=d_ln2_g, ln2_b=d_ln2_b)
    pack = _pack([loss_part] + [small[n] for n in SMALL_NAMES])

    def gx_epilogue(acc, i, j, extra_refs, out_refs):
        out_refs[0][...] = acc + ALPHA * extra_refs[0][...]

    grad_x, got_in, all_packs = _matmul(
        dh, w_in, mode="nt", tm=tm_s, tn=D, tk=INW, extras=[(dz1, (tm_s, D), rowD)],
        outs=[((T, D), F32, (tm_s, D), rowD)], epilogue=gx_epilogue, name="mm_grad_x",
        bg=_bg_chip_exchange(chip_l, pack))
    full_in = _final_sum(late[0], sib_l[0], got_in, ids, name="final_sum_w_in")
    return grad_x.reshape(B, S, D), [full_in, full_out, full_up, full_down], all_packs


def _place():
    return lax.axis_index("x"), lax.axis_index("y"), lax.axis_index("c")


CHIP_FLIPS = ((1, 0), (0, 1), (1, 1))


def _flip(v, f):
    return 1 - v if f else v


HBM_SPEC = pl.BlockSpec(memory_space=pl.ANY)
VMEM_SPEC = pl.BlockSpec(memory_space=pltpu.VMEM)
COMM_PARAMS = pltpu.CompilerParams(vmem_limit_bytes=VMEM_LIMIT)


def _gather_weights(big, small):
    nb, ns = len(big), len(small)

    def body(*refs):
        big_in = refs[:nb]
        small_in = refs[nb:nb + ns]
        big_out = refs[nb + ns:2 * nb + ns]
        small_out = refs[2 * nb + ns:2 * nb + 2 * ns]
        stages = refs[2 * nb + 2 * ns:3 * nb + 2 * ns]
        send_sems, recv_sems, local_sems = refs[3 * nb + 2 * ns:]
        x, y, c = _place()
        s_me = 2 * x + y
        sibling = (x, y, 1 - c)
        started, local_copies = [], []
        for a in range(nb):
            rh = big[a].shape[0] // 2
            lo = pl.multiple_of(c * rh, 16)
            stages[a][...] = big_in[a][pl.ds(lo, rh), :].astype(BF16)
            mine = big_out[a].at[s_me, pl.ds(lo, rh), :]
            loc = pltpu.make_async_copy(stages[a], mine, local_sems.at[a])
            loc.start()
            local_copies.append(loc)
            targets = [sibling] + [(_flip(x, fx), _flip(y, fy), c) for fx, fy in CHIP_FLIPS]
            for k, to in enumerate(targets):
                cp = pltpu.make_async_remote_copy(stages[a], mine, send_sems.at[a * 7 + k],
                                                  recv_sems.at[a * 7 + k], device_id=to, device_id_type=MESH)
                cp.start()
                started.append(cp)
        for a in range(ns):
            mine = small_out[a].at[s_me]
            loc = pltpu.make_async_copy(small_in[a], mine, local_sems.at[nb + a])
            loc.start()
            local_copies.append(loc)
            for k, (fx, fy) in enumerate(CHIP_FLIPS):
                cp = pltpu.make_async_remote_copy(small_in[a], mine, send_sems.at[nb * 7 + a * 3 + k],
                                                  recv_sems.at[nb * 7 + a * 3 + k],
                                                  device_id=(_flip(x, fx), _flip(y, fy), c), device_id_type=MESH)
                cp.start()
                started.append(cp)
        for a in range(nb):
            rh = big[a].shape[0] // 2
            lo = pl.multiple_of(c * rh, 16)
            for k, (fx, fy) in enumerate(CHIP_FLIPS):
                s_from = 2 * _flip(x, fx) + _flip(y, fy)
                got = big_out[a].at[s_from, pl.ds(lo, rh), :]
                pltpu.make_async_remote_copy(got, got, send_sems.at[a * 7 + 1 + k], recv_sems.at[a * 7 + 1 + k],
                                             device_id=sibling, device_id_type=MESH).wait_recv()
                fwd = pltpu.make_async_remote_copy(got, got, send_sems.at[a * 7 + 4 + k],
                                                   recv_sems.at[a * 7 + 4 + k], device_id=sibling,
                                                   device_id_type=MESH)
                fwd.start()
                started.append(fwd)
        for a in range(nb):
            rh = big[a].shape[0] // 2
            lo_sib = pl.multiple_of((1 - c) * rh, 16)
            for k in (0, 4, 5, 6):
                any_rows = big_out[a].at[s_me, pl.ds(lo_sib, rh), :]
                pltpu.make_async_remote_copy(any_rows, any_rows, send_sems.at[a * 7 + k], recv_sems.at[a * 7 + k],
                                             device_id=sibling, device_id_type=MESH).wait_recv()
        for a in range(ns):
            for k in range(3):
                pltpu.make_async_remote_copy(small_in[a], small_out[a].at[s_me], send_sems.at[nb * 7 + a * 3 + k],
                                             recv_sems.at[nb * 7 + a * 3 + k], device_id=sibling,
                                             device_id_type=MESH).wait_recv()
        for cp in started:
            cp.wait_send()
        for cp in local_copies:
            cp.wait()

    n_sem = nb * 7 + ns * 3
    out_shape = ([jax.ShapeDtypeStruct((N_SHARDS,) + w.shape, BF16) for w in big]
                 + [jax.ShapeDtypeStruct((N_SHARDS,) + w.shape, F32) for w in small])
    res = pl.pallas_call(
        body, in_specs=[VMEM_SPEC] * nb + [HBM_SPEC] * ns, out_specs=[HBM_SPEC] * (nb + ns),
        out_shape=out_shape,
        scratch_shapes=[pltpu.VMEM((w.shape[0] // 2, w.shape[1]), BF16) for w in big]
        + [pltpu.SemaphoreType.DMA((n_sem,)), pltpu.SemaphoreType.DMA((n_sem,)),
           pltpu.SemaphoreType.DMA((nb + ns,))],
        compiler_params=COMM_PARAMS, name="gather_weights",
    )(*big, *small)
    return res[:nb], res[nb:]


def _sibling_exchange(grads):
    n = len(grads)

    def body(*refs):
        g_in = refs[:n]
        got = refs[n:2 * n]
        send_sems, recv_sems = refs[2 * n:]
        x, y, c = _place()
        cps = []
        for a in range(n):
            rh = grads[a].shape[1] // 2
            lo = pl.multiple_of((1 - c) * rh, 8)
            cp = pltpu.make_async_remote_copy(g_in[a].at[:, pl.ds(lo, rh), :], got[a], send_sems.at[a],
                                              recv_sems.at[a], device_id=(x, y, 1 - c), device_id_type=MESH)
            cp.start()
            cps.append(cp)
        for cp in cps:
            cp.wait()

    return pl.pallas_call(
        body, in_specs=[HBM_SPEC] * n, out_specs=[HBM_SPEC] * n,
        out_shape=[jax.ShapeDtypeStruct((N_SHARDS, g.shape[1] // 2, g.shape[2]), F32) for g in grads],
        scratch_shapes=[pltpu.SemaphoreType.DMA((n,)), pltpu.SemaphoreType.DMA((n,))],
        compiler_params=COMM_PARAMS, name="sibling_exchange",
    )(*grads)


def _chip_exchange(chip_parts, pack):
    n = len(chip_parts)

    def body(*refs):
        parts = refs[:n]
        pack_ref = refs[n]
        got = refs[n + 1:2 * n + 1]
        all_packs = refs[2 * n + 1]
        send_sems, recv_sems, local_sem = refs[2 * n + 2:]
        x, y, c = _place()
        me = 4 * x + 2 * y + c
        cps = []
        for a in range(n):
            for k, (fx, fy) in enumerate(CHIP_FLIPS):
                px, py = _flip(x, fx), _flip(y, fy)
                cp = pltpu.make_async_remote_copy(parts[a].at[2 * px + py], got[a].at[k], send_sems.at[a * 3 + k],
                                                  recv_sems.at[a * 3 + k], device_id=(px, py, c),
                                                  device_id_type=MESH)
                cp.start()
                cps.append(cp)
        loc = pltpu.make_async_copy(pack_ref, all_packs.at[me], local_sem)
        loc.start()
        for m in range(1, N_DEV):
            to = (_flip(x, m & 4), _flip(y, m & 2), _flip(c, m & 1))
            cp = pltpu.make_async_remote_copy(pack_ref, all_packs.at[me], send_sems.at[n * 3 + m - 1],
                                              recv_sems.at[n * 3 + m - 1], device_id=to, device_id_type=MESH)
            cp.start()
            cps.append(cp)
        for cp in cps:
            cp.wait()
        loc.wait()

    rs = pack.shape[0]
    res = pl.pallas_call(
        body, in_specs=[HBM_SPEC] * (n + 1), out_specs=[HBM_SPEC] * (n + 1),
        out_shape=[jax.ShapeDtypeStruct((3,) + p.shape[1:], BF16) for p in chip_parts]
        + [jax.ShapeDtypeStruct((N_DEV, rs, LANES), F32)],
        scratch_shapes=[pltpu.SemaphoreType.DMA((n * 3 + N_DEV - 1,)), pltpu.SemaphoreType.DMA((n * 3 + N_DEV - 1,)),
                        pltpu.SemaphoreType.DMA],
        compiler_params=COMM_PARAMS, name="chip_exchange",
    )(*chip_parts, pack)
    return res[:n], res[n]


def _sibling_assemble(fulls):
    n = len(fulls)

    def body(*refs):
        full = refs[n:2 * n]
        send_sems, recv_sems = refs[2 * n:]
        x, y, c = _place()
        cps = []
        for a in range(n):
            rh = fulls[a].shape[0] // 2
            mine = full[a].at[pl.ds(pl.multiple_of(c * rh, 8), rh), :]
            cp = pltpu.make_async_remote_copy(mine, mine, send_sems.at[a], recv_sems.at[a],
                                              device_id=(x, y, 1 - c), device_id_type=MESH)
            cp.start()
            cps.append(cp)
        for cp in cps:
            cp.wait()

    return pl.pallas_call(
        body, in_specs=[HBM_SPEC] * n, out_specs=[HBM_SPEC] * n,
        out_shape=[jax.ShapeDtypeStruct(f.shape, F32) for f in fulls],
        input_output_aliases={a: a for a in range(n)},
        scratch_shapes=[pltpu.SemaphoreType.DMA((n,)), pltpu.SemaphoreType.DMA((n,))],
        compiler_params=COMM_PARAMS, name="sibling_assemble",
    )(*fulls)


def _remote(ref_src, ref_dst, send_sems, recv_sems, k, to):
    return pltpu.make_async_remote_copy(ref_src, ref_dst, send_sems.at[k], recv_sems.at[k], device_id=to,
                                        device_id_type=MESH)


def _stage_half(w, ids, name):
    R, C = w.shape
    rh = R // 2
    rt = _half_tile(rh)
    nt = rh // rt

    def body(ids_ref, w_ref, o_ref):
        o_ref[0] = w_ref[...].astype(BF16)

    grid_spec = pltpu.PrefetchScalarGridSpec(
        num_scalar_prefetch=1, grid=(nt,),
        in_specs=[pl.BlockSpec((rt, C), lambda i, ids: (ids[2] * nt + i, 0))],
        out_specs=pl.BlockSpec((1, rt, C), lambda i, ids: (2 * ids[0] + ids[1], ids[2] * nt + i, 0)))
    return pl.pallas_call(body, grid_spec=grid_spec, out_shape=jax.ShapeDtypeStruct((N_SHARDS, R, C), BF16),
                          compiler_params=_params(1), name=name)(ids, w)


def _bg_gather(staged):
    n = len(staged)

    def run(step, n_steps, ins, outs, send_sems, recv_sems, local_sems, post):
        x, y, c = _place()
        s_me = 2 * x + y
        sibling = (x, y, 1 - c)
        chips = [(_flip(x, fx), _flip(y, fy)) for fx, fy in CHIP_FLIPS]

        def rows(a, s, half):
            rh = staged[a].shape[1] // 2
            return outs[a].at[s, pl.ds(pl.multiple_of(half * rh, 16), rh), :]

        def copy(a, k, ref, to):
            return _remote(ref, ref, send_sems, recv_sems, a * 7 + k, to)

        if not post:
            @pl.when(step == 0)
            def _():
                for a in range(n):
                    mine = rows(a, s_me, c)
                    copy(a, 0, mine, sibling).start()
                    for k, (px, py) in enumerate(chips):
                        copy(a, 1 + k, mine, (px, py, c)).start()

            @pl.when(step == max(n_steps - 2, 0))
            def _():
                for a in range(n):
                    for k, (px, py) in enumerate(chips):
                        got = rows(a, 2 * px + py, c)
                        copy(a, 1 + k, got, sibling).wait_recv()
                        copy(a, 4 + k, got, sibling).start()
        else:
            @pl.when(step == n_steps - 1)
            def _():
                for a in range(n):
                    for k in (0, 4, 5, 6):
                        copy(a, k, rows(a, s_me, 1 - c), sibling).wait_recv()
                    for k in range(7):
                        copy(a, k, rows(a, s_me, c), sibling).wait_send()

    return _Background(staged, [jax.ShapeDtypeStruct(g.shape, g.dtype) for g in staged],
                       {a: a for a in range(n)}, 7 * n, run)


def _bg_sibling_exchange(grads):
    n = len(grads)

    def run(step, n_steps, ins, outs, send_sems, recv_sems, local_sems, post):
        x, y, c = _place()

        def copy(a):
            rh = grads[a].shape[1] // 2
            lo = pl.multiple_of((1 - c) * rh, 8)
            return _remote(ins[a].at[:, pl.ds(lo, rh), :], outs[a], send_sems, recv_sems, a, (x, y, 1 - c))

        if not post:
            @pl.when(step == 0)
            def _():
                for a in range(n):
                    copy(a).start()
        else:
            @pl.when(step == n_steps - 1)
            def _():
                for a in range(n):
                    copy(a).wait()

    return _Background(grads, [jax.ShapeDtypeStruct((N_SHARDS, g.shape[1] // 2, g.shape[2]), F32) for g in grads],
                       {}, n, run)


def _bg_chip_exchange(chip_parts, pack=None):
    n = len(chip_parts)

    def run(step, n_steps, ins, outs, send_sems, recv_sems, local_sems, post):
        x, y, c = _place()
        me = 4 * x + 2 * y + c

        def copies():
            cps = []
            for a in range(n):
                for k, (fx, fy) in enumerate(CHIP_FLIPS):
                    px, py = _flip(x, fx), _flip(y, fy)
                    cps.append(_remote(ins[a].at[2 * px + py], outs[a].at[k], send_sems, recv_sems, a * 3 + k,
                                       (px, py, c)))
            if pack is not None:
                for m in range(1, N_DEV):
                    to = (_flip(x, m & 4), _flip(y, m & 2), _flip(c, m & 1))
                    cps.append(_remote(ins[n], outs[n].at[me], send_sems, recv_sems, n * 3 + m - 1, to))
            return cps

        def local():
            return pltpu.make_async_copy(ins[n], outs[n].at[me], local_sems.at[0])

        if not post:
            @pl.when(step == 0)
            def _():
                for cp in copies():
                    cp.start()
                if pack is not None:
                    local().start()
        else:
            @pl.when(step == n_steps - 1)
            def _():
                for cp in copies():
                    cp.wait()
                if pack is not None:
                    local().wait()

    in_arrays = list(chip_parts) + ([pack] if pack is not None else [])
    out_shapes = [jax.ShapeDtypeStruct((3,) + p.shape[1:], BF16) for p in chip_parts]
    if pack is not None:
        out_shapes.append(jax.ShapeDtypeStruct((N_DEV, pack.shape[0], LANES), F32))
    return _Background(in_arrays, out_shapes, {}, n * 3 + N_DEV - 1, run)


def _half_tile(rh, mult=16, want=256):
    best = None
    for t in range(mult, min(rh, want) + 1, mult):
        if rh % t == 0:
            best = t
    return best if best is not None else rh


def _pair_sum(g, sib, ids, name):
    _, R, C = g.shape
    rh = R // 2
    rt = _half_tile(rh)
    nt = rh // rt

    def body(ids_ref, g_ref, s_ref, o_ref):
        o_ref[...] = (g_ref[...] + s_ref[...]).astype(BF16)

    grid_spec = pltpu.PrefetchScalarGridSpec(
        num_scalar_prefetch=1, grid=(N_SHARDS, nt),
        in_specs=[pl.BlockSpec((1, rt, C), lambda s, i, ids: (s, ids[2] * nt + i, 0)),
                  pl.BlockSpec((1, rt, C), lambda s, i, ids: (s, i, 0))],
        out_specs=pl.BlockSpec((1, rt, C), lambda s, i, ids: (s, i, 0)))
    return pl.pallas_call(body, grid_spec=grid_spec, out_shape=jax.ShapeDtypeStruct((N_SHARDS, rh, C), BF16),
                          compiler_params=_params(2), name=name)(ids, g, sib)


def _final_sum(g, sib, got, ids, name):
    _, R, C = g.shape
    rh = R // 2
    rt = _half_tile(rh)
    nt = rh // rt

    def body(ids_ref, g_ref, s_ref, r_ref, o_ref):
        tot = g_ref[0] + s_ref[0]
        for k in range(3):
            tot = tot + r_ref[k].astype(F32)
        o_ref[...] = tot

    grid_spec = pltpu.PrefetchScalarGridSpec(
        num_scalar_prefetch=1, grid=(nt,),
        in_specs=[pl.BlockSpec((1, rt, C), lambda i, ids: (2 * ids[0] + ids[1], ids[2] * nt + i, 0)),
                  pl.BlockSpec((1, rt, C), lambda i, ids: (2 * ids[0] + ids[1], i, 0)),
                  pl.BlockSpec((3, rt, C), lambda i, ids: (0, i, 0))],
        out_specs=pl.BlockSpec((rt, C), lambda i, ids: (ids[2] * nt + i, 0)))
    return pl.pallas_call(body, grid_spec=grid_spec, out_shape=jax.ShapeDtypeStruct((R, C), F32),
                          compiler_params=_params(1), name=name)(ids, g, sib, got)


def _sum_packs(all_packs):
    def body(p_ref, o_ref):
        tot = p_ref[0]
        for i in range(1, N_DEV):
            tot = tot + p_ref[i]
        o_ref[...] = tot

    return pl.pallas_call(body, in_specs=[VMEM_SPEC], out_specs=VMEM_SPEC,
                          out_shape=jax.ShapeDtypeStruct(all_packs.shape[1:], F32), name="sum_packs")(all_packs)


def _adamw(w, g, m, v, name):
    R, C = w.shape
    rt = _half_tile(R, mult=8, want=256)

    def body(w_ref, g_ref, m_ref, v_ref, d_ref, nm_ref, nv_ref):
        gg = g_ref[...]
        nm = ADAM_B1 * m_ref[...] + (1.0 - ADAM_B1) * gg
        nv = ADAM_B2 * v_ref[...] + (1.0 - ADAM_B2) * (gg * gg)
        m_hat = nm / (1.0 - ADAM_B1 ** ADAM_STEP)
        v_hat = nv / (1.0 - ADAM_B2 ** ADAM_STEP)
        d_ref[...] = -ADAM_LR * (m_hat / (jnp.sqrt(v_hat) + ADAM_EPS) + ADAM_WD * w_ref[...])
        nm_ref[...] = nm
        nv_ref[...] = nv

    spec = pl.BlockSpec((rt, C), lambda i: (i, 0))
    return pl.pallas_call(body, grid=(R // rt,), in_specs=[spec] * 4, out_specs=[spec] * 3,
                          out_shape=[jax.ShapeDtypeStruct((R, C), F32)] * 3,
                          compiler_params=_params(1), name=name)(w, g, m, v)


def _adamw_update(w, g, m, v):
    nm = ADAM_B1 * m + (1.0 - ADAM_B1) * g
    nv = ADAM_B2 * v + (1.0 - ADAM_B2) * (g * g)
    m_hat = nm / (1.0 - ADAM_B1 ** ADAM_STEP)
    v_hat = nv / (1.0 - ADAM_B2 ** ADAM_STEP)
    return -ADAM_LR * (m_hat / (jnp.sqrt(v_hat) + ADAM_EPS) + ADAM_WD * w), nm, nv


def _adamw_many(ws, gs, ms, vs, name):
    n = len(ws)

    def body(*refs):
        for i in range(n):
            d, nm, nv = _adamw_update(refs[i][...], refs[n + i][...], refs[2 * n + i][...], refs[3 * n + i][...])
            refs[4 * n + i][...] = d
            refs[5 * n + i][...] = nm
            refs[6 * n + i][...] = nv

    return pl.pallas_call(body, in_specs=[VMEM_SPEC] * (4 * n), out_specs=[VMEM_SPEC] * (3 * n),
                          out_shape=[jax.ShapeDtypeStruct(w.shape, F32) for w in ws] * 3, name=name,
                          )(*ws, *gs, *ms, *vs)


def _pack(pieces):
    rows = []
    for p in pieces:
        flat = p.reshape(-1)
        pad = (-flat.shape[0]) % LANES
        if pad:
            flat = jnp.concatenate([flat, jnp.zeros((pad,), F32)])
        rows.append(flat.reshape(-1, LANES))
    total = sum(r.shape[0] for r in rows)
    pad_rows = (-total) % 8
    if pad_rows:
        rows.append(jnp.zeros((pad_rows, LANES), F32))
    return jnp.concatenate(rows, axis=0)


def _unpack(buf, shapes):
    out, r0 = [], 0
    for shp in shapes:
        n = int(np.prod(shp))
        nr = -(-n // LANES)
        out.append(buf[r0:r0 + nr].reshape(-1)[:n].reshape(shp))
        r0 += nr
    return out


SMALL_NAMES = ("rel_table", "b_in", "conv_w", "conv_b", "conv_ln_g", "conv_ln_b", "attn_norm_g", "conv_norm_g",
               "ln1_g", "ln1_b", "ffn_conv_w", "ffn_conv_b", "ln2_g", "ln2_b")
BIG_NAMES = ("w_in", "w_out", "w_up", "w_down")
WEIGHT_ORDER = ("rel_table", "w_in", "b_in", "conv_w", "conv_b", "conv_ln_g", "conv_ln_b", "attn_norm_g",
                "conv_norm_g", "w_out", "ln1_g", "ln1_b", "w_up", "ffn_conv_w", "ffn_conv_b", "w_down",
                "ln2_g", "ln2_b")


def kernel(x, rel_table, w_in, b_in, conv_w, conv_b, conv_ln_g, conv_ln_b, attn_norm_g, conv_norm_g, w_out, ln1_g, ln1_b, w_up, ffn_conv_w, ffn_conv_b, w_down, ln2_g, ln2_b, loss_target, m_rel_table, m_w_in, m_b_in, m_conv_w, m_conv_b, m_conv_ln_g, m_conv_ln_b, m_attn_norm_g, m_conv_norm_g, m_w_out, m_ln1_g, m_ln1_b, m_w_up, m_ffn_conv_w, m_ffn_conv_b, m_w_down, m_ln2_g, m_ln2_b, v_rel_table, v_w_in, v_b_in, v_conv_w, v_conv_b, v_conv_ln_g, v_conv_ln_b, v_attn_norm_g, v_conv_norm_g, v_w_out, v_ln1_g, v_ln1_b, v_w_up, v_ffn_conv_w, v_ffn_conv_b, v_w_down, v_ln2_g, v_ln2_b):
    args = dict(locals())
    weights = {n: args[n] for n in WEIGHT_ORDER}
    moms = {n: args["m_" + n] for n in WEIGHT_ORDER}
    vels = {n: args["v_" + n] for n in WEIGHT_ORDER}
    xi, yi, ci = _place()
    ids = jnp.stack([xi, yi, ci]).astype(jnp.int32)
    shard = 2 * xi + yi
    D = x.shape[-1]
    DFF = w_down.shape[1] * N_SHARDS
    CW = conv_norm_g.shape[-1]

    (g_in,), (g_cw, g_fcw) = _gather_weights([w_in[0]], [conv_w[0], ffn_conv_w[0]])
    cols = lambda t: jnp.transpose(t, (1, 0, 2)).reshape(t.shape[1], N_SHARDS * t.shape[2])
    staged = [_stage_half(w[0], ids, name="stage_" + n) for w, n in ((w_out, "w_out"), (w_up, "w_up"),
                                                                     (w_down, "w_down"))]

    grad_x, fulls, all_packs = _local_step(
        x, loss_target, rel_table, cols(g_in), b_in, cols(g_cw), conv_b, conv_ln_g, conv_ln_b, attn_norm_g,
        conv_norm_g, staged, ln1_g, ln1_b, cols(g_fcw), ffn_conv_b, ln2_g, ln2_b, ids)
    big_grads = dict(zip(BIG_NAMES, _sibling_assemble(fulls)))
    for n in ("w_in", "w_up"):
        big_grads[n] = _transpose(big_grads[n], name="transpose_d" + n)

    summed = _sum_packs(all_packs)
    full_shapes = {n: weights[n].shape for n in SMALL_NAMES}
    full_shapes["conv_w"] = (1, CONV_KERNEL, CW)
    full_shapes["ffn_conv_w"] = (1, FFN_CONV_KERNEL, 2 * DFF)
    un = _unpack(summed, [(1, LANES)] + [full_shapes[n] for n in SMALL_NAMES])
    loss = un[0][0, 0]
    small_grads = dict(zip(SMALL_NAMES, un[1:]))
    for n in ("conv_w", "ffn_conv_w"):
        width = weights[n].shape[-1]
        small_grads[n] = lax.dynamic_slice_in_dim(small_grads[n], shard * width, width, axis=2)

    grads, delta, new_m, new_v = {}, {}, {}, {}
    for n in BIG_NAMES:
        shp = weights[n].shape
        g2 = big_grads[n]
        d, nm, nv = _adamw(weights[n][0], g2, moms[n][0], vels[n][0], name="adamw_" + n)
        grads[n], delta[n], new_m[n], new_v[n] = (t.reshape(shp) for t in (g2, d, nm, nv))
    pick = lambda src: [src[n] for n in SMALL_NAMES]
    small_out = _adamw_many(pick(weights), pick(small_grads), pick(moms), pick(vels), name="adamw_small")
    ns = len(SMALL_NAMES)
    for tgt, part in ((delta, small_out[:ns]), (new_m, small_out[ns:2 * ns]), (new_v, small_out[2 * ns:])):
        tgt.update(zip(SMALL_NAMES, part))
    grads.update(small_grads)

    return (loss, grad_x, *[grads[n] for n in WEIGHT_ORDER], *[delta[n] for n in WEIGHT_ORDER],
            *[new_m[n] for n in WEIGHT_ORDER], *[new_v[n] for n in WEIGHT_ORDER])
```

```python
import functools
import math

import numpy as np
import jax
import jax.numpy as jnp
from jax import lax
from jax.experimental import pallas as pl
from jax.experimental.pallas import tpu as pltpu

F32 = jnp.float32
BF16 = jnp.bfloat16
MESH = pl.DeviceIdType.MESH

HEAD_DIM = 64
LANES = 128
ATTN_BLOCK = 128
DILATED_CONFIGS = ((128, 1), (512, 4), (2048, 16))
CONV_KERNEL = 31
FFN_CONV_KERNEL = 3
REL_BUCKETS = 32
REL_MAX_DIST = 2048
DEPTH = 1
ALPHA = (2 * DEPTH) ** 0.25
LN_EPS = 1e-5
NEG_INF = -1e30
QK_SCALE = 1.0 / math.sqrt(HEAD_DIM)
ADAM_LR = 0.001
ADAM_B1 = 0.9
ADAM_B2 = 0.999
ADAM_EPS = 1e-08
ADAM_WD = 0.01
ADAM_STEP = 10
VMEM_LIMIT = 52 * 1024 * 1024
FFN_COLS = 128
N_SHARDS = 4
N_DEV = 8


def _params(n_axes):
    return pltpu.CompilerParams(dimension_semantics=("arbitrary",) * n_axes,
                                vmem_limit_bytes=VMEM_LIMIT)


MM_DIMS = {"nn": (((1,), (0,)), ((), ())), "nt": (((1,), (1,)), ((), ())), "tn": (((0,), (0,)), ((), ()))}


class _Background:
    def __init__(self, in_arrays, out_shapes, aliases, n_sems, run, n_local=1):
        self.in_arrays, self.out_shapes, self.aliases = list(in_arrays), list(out_shapes), dict(aliases)
        self.n_sems, self.n_local, self.run = n_sems, n_local, run

    def scratch(self):
        return [pltpu.SemaphoreType.DMA((self.n_sems,)), pltpu.SemaphoreType.DMA((self.n_sems,)),
                pltpu.SemaphoreType.DMA((self.n_local,))]


def _hosted_call(body, bg, *, grid, in_specs, out_specs, out_shape, scratch_shapes, operands, name):
    n_in, n_out, n_scr = len(in_specs), len(out_specs), len(scratch_shapes)
    if bg is None:
        return pl.pallas_call(lambda *refs: body(refs, lambda post: None), grid=grid, in_specs=in_specs,
                              out_specs=out_specs, out_shape=out_shape, scratch_shapes=scratch_shapes,
                              compiler_params=_params(len(grid)), name=name)(*operands)
    nb_in, nb_out = len(bg.in_arrays), len(bg.out_shapes)
    n_steps = int(np.prod(grid))

    def full_body(*refs):
        own = refs[:n_in] + refs[n_in + nb_in:n_in + nb_in + n_out] \
            + refs[n_in + nb_in + n_out + nb_out:n_in + nb_in + n_out + nb_out + n_scr]
        bg_in = refs[n_in:n_in + nb_in]
        bg_out = refs[n_in + nb_in + n_out:n_in + nb_in + n_out + nb_out]
        sems = refs[n_in + nb_in + n_out + nb_out + n_scr:]
        step = pl.program_id(0)
        for ax in range(1, len(grid)):
            step = step * grid[ax] + pl.program_id(ax)

        def hook(post):
            bg.run(step, n_steps, bg_in, bg_out, *sems, post)

        body(own, hook)

    res = pl.pallas_call(
        full_body, grid=grid, in_specs=list(in_specs) + [HBM_SPEC] * nb_in,
        out_specs=list(out_specs) + [HBM_SPEC] * nb_out, out_shape=list(out_shape) + bg.out_shapes,
        input_output_aliases={n_in + a: n_out + o for a, o in bg.aliases.items()},
        scratch_shapes=list(scratch_shapes) + bg.scratch(), compiler_params=_params(len(grid)), name=name,
    )(*operands, *bg.in_arrays)
    return res


def _matmul_general(ins, part_fn, *, grid, tm, tn, outs, epilogue, extras=(), name, bg=None):
    nk = grid[2]
    n_in, n_extra = len(ins), len(extras)

    def body(refs, bg_hook):
        in_refs = refs[:n_in]
        rest = refs[n_in:]
        extra_refs = rest[:n_extra]
        out_refs = rest[n_extra:n_extra + len(outs)]
        acc_ref = rest[-1]
        i, j, k = pl.program_id(0), pl.program_id(1), pl.program_id(2)
        bg_hook(False)
        part = part_fn(in_refs, i, j, k)
        if nk == 1:
            epilogue(part, i, j, extra_refs, out_refs)
        else:
            @pl.when(k == 0)
            def _():
                acc_ref[...] = part

            @pl.when(k > 0)
            def _():
                acc_ref[...] += part

            @pl.when(k == nk - 1)
            def _():
                epilogue(acc_ref[...], i, j, extra_refs, out_refs)
        bg_hook(True)

    in_specs = [pl.BlockSpec(bs, im) for (_, bs, im) in list(ins) + list(extras)]
    out_specs = [pl.BlockSpec(bs, im) for (_, _, bs, im) in outs]
    out_shape = [jax.ShapeDtypeStruct(s, d) for (s, d, _, _) in outs]
    return _hosted_call(body, bg, grid=grid, in_specs=in_specs, out_specs=out_specs, out_shape=out_shape,
                        scratch_shapes=[pltpu.VMEM((tm, tn), F32)],
                        operands=[e[0] for e in ins] + [e[0] for e in extras], name=name)


def _dot(a, b, mode):
    return lax.dot_general(a.astype(BF16), b.astype(BF16), MM_DIMS[mode], preferred_element_type=F32)


def _matmul(a, b, *, mode, tm, tn, tk, outs, epilogue, extras=(), name, bg=None):
    if mode == "tn":
        K, M = a.shape
        N = b.shape[1]
        ins = [(a, (tk, tm), lambda i, j, k: (k, i)), (b, (tk, tn), lambda i, j, k: (k, j))]
    elif mode == "nt":
        M, K = a.shape
        N = b.shape[0]
        ins = [(a, (tm, tk), lambda i, j, k: (i, k)), (b, (tn, tk), lambda i, j, k: (j, k))]
    else:
        M, K = a.shape
        N = b.shape[1]
        ins = [(a, (tm, tk), lambda i, j, k: (i, k)), (b, (tk, tn), lambda i, j, k: (k, j))]
    assert M % tm == 0 and N % tn == 0 and K % tk == 0, (name, M, N, K, tm, tn, tk)

    def part_fn(in_refs, i, j, k):
        return _dot(in_refs[0][...], in_refs[1][...], mode)

    return _matmul_general(ins, part_fn, grid=(M // tm, N // tn, K // tk), tm=tm, tn=tn, outs=outs,
                           epilogue=epilogue, extras=extras, name=name, bg=bg)


def _plain_out(M, N, tm, tn, dtype):
    return ((M, N), dtype, (tm, tn), lambda i, j, k: (i, j))


def _mm_plain(a, b, *, mode, tm, tn, tk, out_dtype, name, bias=None, bg=None):
    if mode == "tn":
        M, N = a.shape[1], b.shape[1]
    elif mode == "nt":
        M, N = a.shape[0], b.shape[0]
    else:
        M, N = a.shape[0], b.shape[1]
    extras = []
    if bias is not None:
        extras.append((bias, (1, tn), lambda i, j, k: (0, j)))

    def epilogue(acc, i, j, extra_refs, out_refs):
        if bias is not None:
            acc = acc + extra_refs[0][...]
        out_refs[0][...] = acc.astype(out_dtype)

    res = _matmul(a, b, mode=mode, tm=tm, tn=tn, tk=tk, outs=[_plain_out(M, N, tm, tn, out_dtype)],
                  epilogue=epilogue, extras=extras, name=name, bg=bg)
    return res[0] if bg is None else res


def _row_tile(T, want):
    t = min(T, want)
    while T % t:
        t //= 2
    return t


def _col_tile(N, want):
    if N <= want:
        return N
    best = None
    for c in range(LANES, want + 1, LANES):
        if N % c == 0:
            best = c
    return best if best is not None else N


def _accumulate(ref, first, val):
    @pl.when(first)
    def _():
        ref[...] = val

    @pl.when(jnp.logical_not(first))
    def _():
        ref[...] += val


def _ln_fwd(z, g, b):
    mu = jnp.mean(z, axis=-1, keepdims=True)
    zc = z - mu
    var = jnp.mean(zc * zc, axis=-1, keepdims=True)
    r = lax.rsqrt(var + LN_EPS)
    xh = zc * r
    return xh * g + b, xh, r


def _ln_bwd(dy, xh, r, g):
    dxh = dy * g
    m1 = jnp.mean(dxh, axis=-1, keepdims=True)
    m2 = jnp.mean(dxh * xh, axis=-1, keepdims=True)
    return r * (dxh - m1 - xh * m2)


def _sigmoid(x):
    return 1.0 / (1.0 + jnp.exp(-x))


def _shift_down(x, s, row):
    if s == 0:
        return x
    rolled = pltpu.roll(x, s, 0)
    nfix = -(-s // 8) * 8
    head = jnp.where(row[:nfix] >= s, rolled[:nfix], 0.0)
    return jnp.concatenate([head, rolled[nfix:]], axis=0)


def _shift_up(x, s, row):
    if s == 0:
        return x
    n = x.shape[0]
    rolled = pltpu.roll(x, n - s, 0)
    nfix = -(-s // 8) * 8
    tail = jnp.where(row[n - nfix:] < n - s, rolled[n - nfix:], 0.0)
    return jnp.concatenate([rolled[:n - nfix], tail], axis=0)


def _bucket_tables():
    exact = REL_BUCKETS // 2
    qi = np.arange(ATTN_BLOCK)[:, None]
    kj = np.arange(2 * ATTN_BLOCK)[None, :]
    steps = qi + ATTN_BLOCK - kj
    buckets, masks = [], []
    for window, dilation in DILATED_CONFIGS:
        max_steps = window // dilation
        band = (steps >= 0) & (steps <= max_steps)
        dist = np.maximum(steps, 0) * dilation
        d_f = np.maximum(dist, 1).astype(np.float32)
        large = exact + (np.log(d_f / np.float32(exact)) / np.float32(math.log(REL_MAX_DIST / exact))
                         * np.float32(REL_BUCKETS - exact)).astype(np.int32)
        large = np.minimum(large, REL_BUCKETS - 1)
        bucket = np.where(dist < exact, dist, large).astype(np.int32)
        buckets.append(bucket.reshape(1, -1))
        masks.append(np.where(band, 0.0, NEG_INF).astype(np.float32).reshape(1, -1))
    return np.stack(buckets), np.stack(masks)


def _split_hi_lo(x):
    hi = x.astype(BF16)
    lo = (x - hi.astype(F32)).astype(BF16)
    return hi, lo


def _bias_build(rel_table_t, bucket, mask):
    H = rel_table_t.shape[0]
    n = bucket.shape[-1]

    def body(t_ref, bkt_ref, mask_ref, o_ref):
        onehot = (lax.broadcasted_iota(jnp.int32, (REL_BUCKETS, n), 0) == bkt_ref[0]).astype(BF16)
        t = t_ref[...]
        t1 = t.astype(BF16)
        r1 = t - t1.astype(F32)
        t2 = r1.astype(BF16)
        t3 = (r1 - t2.astype(F32)).astype(BF16)
        acc = jnp.dot(t1, onehot, preferred_element_type=F32)
        acc = acc + jnp.dot(t2, onehot, preferred_element_type=F32)
        acc = acc + jnp.dot(t3, onehot, preferred_element_type=F32)
        o_ref[0] = acc + mask_ref[0]

    return pl.pallas_call(
        body, grid=(3,),
        in_specs=[pl.BlockSpec((H, REL_BUCKETS), lambda b: (0, 0)),
                  pl.BlockSpec((1, 1, n), lambda b: (b, 0, 0)),
                  pl.BlockSpec((1, 1, n), lambda b: (b, 0, 0))],
        out_specs=pl.BlockSpec((1, H, n), lambda b: (b, 0, 0)),
        out_shape=jax.ShapeDtypeStruct((3, H, n), F32),
        compiler_params=_params(1), name="bias_build",
    )(rel_table_t, bucket, mask)


def _rel_grad(dbias, bucket):
    H = dbias.shape[1]
    n = bucket.shape[-1]
    dims = (((1,), (1,)), ((), ()))

    def body(d_ref, bkt_ref, o_ref):
        b = pl.program_id(0)
        onehot = (lax.broadcasted_iota(jnp.int32, (REL_BUCKETS, n), 0) == bkt_ref[0]).astype(BF16)
        d = d_ref[0]
        d1 = d.astype(BF16)
        r1 = d - d1.astype(F32)
        d2 = r1.astype(BF16)
        d3 = (r1 - d2.astype(F32)).astype(BF16)
        acc = lax.dot_general(d1, onehot, dims, preferred_element_type=F32)
        acc = acc + lax.dot_general(d2, onehot, dims, preferred_element_type=F32)
        acc = acc + lax.dot_general(d3, onehot, dims, preferred_element_type=F32)
        _accumulate(o_ref, b == 0, acc)

    return pl.pallas_call(
        body, grid=(3,),
        in_specs=[pl.BlockSpec((1, H, n), lambda b: (b, 0, 0)),
                  pl.BlockSpec((1, 1, n), lambda b: (b, 0, 0))],
        out_specs=pl.BlockSpec((H, REL_BUCKETS), lambda b: (0, 0)),
        out_shape=jax.ShapeDtypeStruct((H, REL_BUCKETS), F32),
        compiler_params=_params(1), name="rel_grad",
    )(dbias, bucket)


def _attn_specs(B, S, AW, d):
    L = S // d
    HP = AW // LANES
    W3 = 3 * HP
    q_spec = pl.BlockSpec((1, L, LANES), lambda h, b, r: (b, 0, r * W3 + h))
    k_spec = pl.BlockSpec((1, L, LANES), lambda h, b, r: (b, 0, r * W3 + HP + h))
    v_spec = pl.BlockSpec((1, L, LANES), lambda h, b, r: (b, 0, r * W3 + 2 * HP + h))
    o_spec = pl.BlockSpec((1, L, LANES), lambda h, b, r: (b, 0, r * HP + h))
    bias_spec = pl.BlockSpec((2, ATTN_BLOCK, 2 * ATTN_BLOCK), lambda h, b, r: (h, 0, 0))
    return L, HP, q_spec, k_spec, v_spec, o_spec, bias_spec


def _attn_fwd(qkv, bias, B, S, AW, d, name):
    L, HP, q_spec, k_spec, v_spec, o_spec, bias_spec = _attn_specs(B, S, AW, d)
    nb = L // ATTN_BLOCK
    nt = (((1,), (1,)), ((), ()))

    def body(q_ref, k_ref, v_ref, b_ref, o_ref, lse_ref):
        head0 = lax.broadcasted_iota(jnp.int32, (1, LANES), 1) < HEAD_DIM

        def block(n, first):
            qs = pl.multiple_of(n * ATTN_BLOCK, ATTN_BLOCK)
            q = q_ref[0, pl.ds(qs, ATTN_BLOCK), :]
            if first:
                kk = k_ref[0, pl.ds(0, ATTN_BLOCK), :]
                vv = v_ref[0, pl.ds(0, ATTN_BLOCK), :]
            else:
                ks = pl.multiple_of(n * ATTN_BLOCK - ATTN_BLOCK, ATTN_BLOCK)
                kk = k_ref[0, pl.ds(ks, 2 * ATTN_BLOCK), :]
                vv = v_ref[0, pl.ds(ks, 2 * ATTN_BLOCK), :]
            outs, lses = [], []
            for e in range(2):
                msk = head0 if e == 0 else jnp.logical_not(head0)
                qe = jnp.where(msk, q, jnp.zeros_like(q))
                s = lax.dot_general(qe, kk, nt, preferred_element_type=F32) * QK_SCALE
                s = s + (b_ref[e, :, ATTN_BLOCK:] if first else b_ref[e])
                m = jnp.max(s, axis=-1, keepdims=True)
                p = jnp.exp(s - m)
                l = jnp.sum(p, axis=-1, keepdims=True)
                o = jnp.dot(p.astype(BF16), vv, preferred_element_type=F32)
                outs.append(o / l)
                lses.append(jnp.broadcast_to(m + jnp.log(l), (ATTN_BLOCK, LANES)))
            o_ref[0, pl.ds(qs, ATTN_BLOCK), :] = jnp.where(head0, outs[0], outs[1])
            lse_ref[0, pl.ds(qs, ATTN_BLOCK), :] = jnp.where(head0, lses[0], lses[1])

        block(0, True)
        if nb > 1:
            def loop(n, c):
                block(n, False)
                return c
            lax.fori_loop(1, nb, loop, 0)

    qv = qkv.reshape(B, L, d * 3 * AW)
    o, lse = pl.pallas_call(
        body, grid=(HP, B, d), in_specs=[q_spec, k_spec, v_spec, bias_spec],
        out_specs=[o_spec, o_spec],
        out_shape=[jax.ShapeDtypeStruct((B, L, d * AW), F32)] * 2,
        compiler_params=_params(3), name=name,
    )(qv, qv, qv, bias)
    return o.reshape(B * S, AW), lse.reshape(B * S, AW)


def _attn_bwd(qkv, do, lse, dd, bias, B, S, AW, d, name):
    L, HP, q_spec, k_spec, v_spec, o_spec, bias_spec = _attn_specs(B, S, AW, d)
    nb = L // ATTN_BLOCK
    nt = (((1,), (1,)), ((), ()))
    tn = (((0,), (0,)), ((), ()))

    def body(q_ref, k_ref, v_ref, do_ref, lse_ref, dd_ref, b_ref, dq_ref, dk_ref, dv_ref, db_ref):
        head0 = lax.broadcasted_iota(jnp.int32, (1, LANES), 1) < HEAD_DIM
        first_step = jnp.logical_and(pl.program_id(1) == 0, pl.program_id(2) == 0)

        @pl.when(first_step)
        def _():
            db_ref[...] = jnp.zeros_like(db_ref)

        dk_ref[...] = jnp.zeros_like(dk_ref)
        dv_ref[...] = jnp.zeros_like(dv_ref)

        def block(n, first):
            qs = pl.multiple_of(n * ATTN_BLOCK, ATTN_BLOCK)
            nkeys = ATTN_BLOCK if first else 2 * ATTN_BLOCK
            ks = 0 if first else pl.multiple_of(n * ATTN_BLOCK - ATTN_BLOCK, ATTN_BLOCK)
            q = q_ref[0, pl.ds(qs, ATTN_BLOCK), :]
            kk = k_ref[0, pl.ds(ks, nkeys), :]
            vv = v_ref[0, pl.ds(ks, nkeys), :]
            dout = do_ref[0, pl.ds(qs, ATTN_BLOCK), :]
            lse_b = lse_ref[0, pl.ds(qs, ATTN_BLOCK), :]
            dd_b = dd_ref[0, pl.ds(qs, ATTN_BLOCK), :]
            dq = jnp.zeros((ATTN_BLOCK, LANES), F32)
            dkk = jnp.zeros((nkeys, LANES), F32)
            dvv = jnp.zeros((nkeys, LANES), F32)
            for e in range(2):
                msk = head0 if e == 0 else jnp.logical_not(head0)
                c0 = e * HEAD_DIM
                qe = jnp.where(msk, q, jnp.zeros_like(q))
                doe = jnp.where(msk, dout, jnp.zeros_like(dout))
                kke = jnp.where(msk, kk, jnp.zeros_like(kk))
                s = lax.dot_general(qe, kk, nt, preferred_element_type=F32) * QK_SCALE
                s = s + (b_ref[e, :, ATTN_BLOCK:] if first else b_ref[e])
                p = jnp.exp(s - lse_b[:, c0:c0 + 1])
                dp = lax.dot_general(doe, vv, nt, preferred_element_type=F32)
                ds = p * (dp - dd_b[:, c0:c0 + 1])
                if first:
                    db_ref[e, :, ATTN_BLOCK:] += ds
                else:
                    db_ref[e] += ds
                dsb = (ds * QK_SCALE).astype(BF16)
                dq = dq + jnp.dot(dsb, kke, preferred_element_type=F32)
                dkk = dkk + lax.dot_general(dsb, qe, tn, preferred_element_type=F32)
                dvv = dvv + lax.dot_general(p.astype(BF16), doe, tn, preferred_element_type=F32)
            dq_ref[0, pl.ds(qs, ATTN_BLOCK), :] = dq
            dk_ref[0, pl.ds(ks, nkeys), :] += dkk
            dv_ref[0, pl.ds(ks, nkeys), :] += dvv

        block(0, True)
        if nb > 1:
            def loop(n, c):
                block(n, False)
                return c
            lax.fori_loop(1, nb, loop, 0)

    H = AW // HEAD_DIM
    qv = qkv.reshape(B, L, d * 3 * AW)
    view = lambda t: t.reshape(B, L, d * AW)
    dq, dk, dv, db = pl.pallas_call(
        body, grid=(HP, B, d),
        in_specs=[q_spec, k_spec, v_spec, o_spec, o_spec, o_spec, bias_spec],
        out_specs=[o_spec, o_spec, o_spec, bias_spec],
        out_shape=[jax.ShapeDtypeStruct((B, L, d * AW), F32)] * 3
        + [jax.ShapeDtypeStruct((H, ATTN_BLOCK, 2 * ATTN_BLOCK), F32)],
        compiler_params=_params(3), name=name,
    )(qv, qv, qv, view(do), view(lse), view(dd), bias)
    flat = lambda t: t.reshape(B * S, AW)
    return flat(dq), flat(dk), flat(dv), db


def _attn_combine(ons, lses, gain, tm):
    T, AW = ons[0].shape

    def body(o1, o2, o3, l1, l2, l3, g_ref, attn_ref, lse_ref, mix_ref, r_ref):
        la, lb, lc = l1[...], l2[...], l3[...]
        m = jnp.maximum(jnp.maximum(la, lb), lc)
        ea, eb, ec = jnp.exp(la - m), jnp.exp(lb - m), jnp.exp(lc - m)
        den = ea + eb + ec
        attn = (ea * o1[...] + eb * o2[...] + ec * o3[...]) / den
        attn_ref[...] = attn
        lse_ref[...] = m + jnp.log(den)
        r = lax.rsqrt(jnp.mean(attn * attn, axis=-1, keepdims=True) + LN_EPS)
        mix_ref[...] = (attn * r * g_ref[...]).astype(BF16)
        r_ref[...] = jnp.broadcast_to(r, (tm, LANES))

    row = pl.BlockSpec((tm, AW), lambda i: (i, 0))
    return pl.pallas_call(
        body, grid=(T // tm,),
        in_specs=[row] * 6 + [pl.BlockSpec((1, AW), lambda i: (0, 0))],
        out_specs=[row, row, row, pl.BlockSpec((tm, LANES), lambda i: (i, 0))],
        out_shape=[jax.ShapeDtypeStruct((T, AW), F32), jax.ShapeDtypeStruct((T, AW), F32),
                   jax.ShapeDtypeStruct((T, AW), BF16), jax.ShapeDtypeStruct((T, LANES), F32)],
        compiler_params=_params(1), name="attn_combine",
    )(*ons, *lses, gain)


def _to_sub(src_ref, stage_ref, dsts, S):
    stage_ref[...] = src_ref[0].astype(F32)
    for (_, d), dst in zip(DILATED_CONFIGS[1:], dsts):
        L = S // d
        for r in range(d):
            dst[r * L:(r + 1) * L, :] = stage_ref[pl.ds(r, L, stride=d), :].astype(dst.dtype)


def _branch_blocks(S, d, block):
    nb = S // d // ATTN_BLOCK
    inner_unroll = 3 if (nb - 1) % 3 == 0 else 1

    def per_residue(r, c):
        block(r * nb, True)
        if nb > 1:
            def inner(n, c2):
                block(r * nb + n, False)
                return c2
            lax.fori_loop(1, nb, inner, 0, unroll=inner_unroll)
        return c

    lax.fori_loop(0, d, per_residue, 0, unroll=4 if nb == 1 else 1)


def _attention_fwd(qkv, bias_all, B, S, AW):
    HP = AW // LANES
    nt = MM_DIMS["nt"]

    def body(q_ref, k_ref, v_ref, b_ref, o_ref, lse_ref, stage, q4, q16, k4, k16, v4, v16, o1, l1, o4, l4, o16, l16):
        head0 = lax.broadcasted_iota(jnp.int32, (1, LANES), 1) < HEAD_DIM
        _to_sub(q_ref, stage, (q4, q16), S)
        _to_sub(k_ref, stage, (k4, k16), S)
        _to_sub(v_ref, stage, (v4, v16), S)
        srcs = ((q_ref.at[0], k_ref.at[0], v_ref.at[0], o1, l1), (q4, k4, v4, o4, l4), (q16, k16, v16, o16, l16))
        for bi, (_, d) in enumerate(DILATED_CONFIGS):
            qs_ref, ks_ref, vs_ref, od_ref, ld_ref = srcs[bi]

            def block(g, first, bi=bi, qs_ref=qs_ref, ks_ref=ks_ref, vs_ref=vs_ref, od_ref=od_ref, ld_ref=ld_ref):
                qs = pl.multiple_of(g * ATTN_BLOCK, ATTN_BLOCK)
                nkeys = ATTN_BLOCK if first else 2 * ATTN_BLOCK
                ks = qs if first else pl.multiple_of(qs - ATTN_BLOCK, ATTN_BLOCK)
                q = qs_ref[pl.ds(qs, ATTN_BLOCK), :]
                kk = ks_ref[pl.ds(ks, nkeys), :]
                vv = vs_ref[pl.ds(ks, nkeys), :]
                outs, lses = [], []
                for e in range(2):
                    msk = head0 if e == 0 else jnp.logical_not(head0)
                    qe = jnp.where(msk, q * QK_SCALE, jnp.zeros_like(q))
                    s = lax.dot_general(qe, kk, nt, preferred_element_type=F32)
                    s = s + (b_ref[bi, e, :, ATTN_BLOCK:] if first else b_ref[bi, e])
                    m = jnp.max(s, axis=-1, keepdims=True)
                    p = jnp.exp(s - m)
                    l = jnp.sum(p, axis=-1, keepdims=True)
                    o = jnp.dot(p.astype(BF16), vv, preferred_element_type=F32)
                    outs.append(o / l)
                    lses.append(jnp.broadcast_to(m + jnp.log(l), (ATTN_BLOCK, LANES)))
                od_ref[pl.ds(qs, ATTN_BLOCK), :] = jnp.where(head0, outs[0], outs[1])
                ld_ref[pl.ds(qs, ATTN_BLOCK), :] = jnp.where(head0, lses[0], lses[1])

            _branch_blocks(S, d, block)

        def natural(sub_ref, d):
            L = S // d
            for r in range(d):
                stage[pl.ds(r, L, stride=d), :] = sub_ref[r * L:(r + 1) * L, :]
            return stage[...]

        la = l1[...]
        lb = natural(l4, 4)
        lc = natural(l16, 16)
        m = jnp.maximum(jnp.maximum(la, lb), lc)
        ea, eb, ec = jnp.exp(la - m), jnp.exp(lb - m), jnp.exp(lc - m)
        den = ea + eb + ec
        lse_ref[0] = m + jnp.log(den)
        acc = ea * o1[...]
        acc = acc + eb * natural(o4, 4)
        acc = acc + ec * natural(o16, 16)
        o_ref[0] = acc / den

    blk = lambda off: pl.BlockSpec((1, S, LANES), lambda b, h: (b, 0, off + h))
    qv = qkv.reshape(B, S, 3 * AW)
    sub_b = pltpu.VMEM((S, LANES), BF16)
    sub_f = pltpu.VMEM((S, LANES), F32)
    o, lse = pl.pallas_call(
        body, grid=(B, HP),
        in_specs=[blk(0), blk(HP), blk(2 * HP),
                  pl.BlockSpec((3, 2, ATTN_BLOCK, 2 * ATTN_BLOCK), lambda b, h: (0, h, 0, 0))],
        out_specs=[blk(0), blk(0)],
        out_shape=[jax.ShapeDtypeStruct((B, S, AW), F32)] * 2,
        scratch_shapes=[sub_f] + [sub_b] * 6 + [sub_f] * 6,
        compiler_params=_params(2), name="attention_fwd",
    )(qv, qv, qv, bias_all)
    return o.reshape(B * S, AW), lse.reshape(B * S, AW)


def _attention_bwd(qkv, do, lse, dd, bias_all, B, S, AW):
    HP = AW // LANES
    H = AW // HEAD_DIM
    nt, tn = MM_DIMS["nt"], MM_DIMS["tn"]

    def body(q_ref, k_ref, v_ref, do_ref, lse_ref, dd_ref, b_ref,
             dq_ref, dk_ref, dv_ref, csq_ref, csk_ref, csv_ref, db_ref,
             stage, q4, q16, k4, k16, v4, v16, g4, g16, l4, l16, d4, d16,
             aq1, ak1, av1, aq4, ak4, av4, aq16, ak16, av16):
        head0 = lax.broadcasted_iota(jnp.int32, (1, LANES), 1) < HEAD_DIM
        first_b = pl.program_id(1) == 0

        @pl.when(first_b)
        def _():
            db_ref[...] = jnp.zeros_like(db_ref)

        _to_sub(q_ref, stage, (q4, q16), S)
        _to_sub(k_ref, stage, (k4, k16), S)
        _to_sub(v_ref, stage, (v4, v16), S)
        _to_sub(do_ref, stage, (g4, g16), S)
        _to_sub(lse_ref, stage, (l4, l16), S)
        _to_sub(dd_ref, stage, (d4, d16), S)
        for acc in (ak1, av1, ak4, av4, ak16, av16):
            acc[...] = jnp.zeros_like(acc)
        srcs = ((q_ref.at[0], k_ref.at[0], v_ref.at[0], do_ref.at[0], lse_ref.at[0], dd_ref.at[0], aq1, ak1, av1),
                (q4, k4, v4, g4, l4, d4, aq4, ak4, av4), (q16, k16, v16, g16, l16, d16, aq16, ak16, av16))
        for bi, (_, d) in enumerate(DILATED_CONFIGS):
            def block(g, first, bi=bi, refs=srcs[bi]):
                qs_ref, ks_ref, vs_ref, gs_ref, ls_ref, ds_ref, aq, ak, av = refs
                qs = pl.multiple_of(g * ATTN_BLOCK, ATTN_BLOCK)
                nkeys = ATTN_BLOCK if first else 2 * ATTN_BLOCK
                ks = qs if first else pl.multiple_of(qs - ATTN_BLOCK, ATTN_BLOCK)
                q = qs_ref[pl.ds(qs, ATTN_BLOCK), :]
                kk = ks_ref[pl.ds(ks, nkeys), :]
                vv = vs_ref[pl.ds(ks, nkeys), :]
                dout = gs_ref[pl.ds(qs, ATTN_BLOCK), :]
                lse_b = ls_ref[pl.ds(qs, ATTN_BLOCK), :]
                dd_b = ds_ref[pl.ds(qs, ATTN_BLOCK), :]
                dq = jnp.zeros((ATTN_BLOCK, LANES), F32)
                dkk = jnp.zeros((nkeys, LANES), F32)
                dvv = jnp.zeros((nkeys, LANES), F32)
                for e in range(2):
                    msk = head0 if e == 0 else jnp.logical_not(head0)
                    c0 = e * HEAD_DIM
                    qe = jnp.where(msk, q * QK_SCALE, jnp.zeros_like(q))
                    doe = jnp.where(msk, dout, jnp.zeros_like(dout))
                    kke = jnp.where(msk, kk * QK_SCALE, jnp.zeros_like(kk))
                    s = lax.dot_general(qe, kk, nt, preferred_element_type=F32)
                    s = s + (b_ref[bi, e, :, ATTN_BLOCK:] if first else b_ref[bi, e])
                    p = jnp.exp(s - lse_b[:, c0:c0 + 1])
                    dp = lax.dot_general(doe, vv, nt, preferred_element_type=F32)
                    ds = p * (dp - dd_b[:, c0:c0 + 1])
                    if first:
                        db_ref[bi, e, :, ATTN_BLOCK:] += ds
                    else:
                        db_ref[bi, e] += ds
                    dsb = ds.astype(BF16)
                    dq = dq + jnp.dot(dsb, kke, preferred_element_type=F32)
                    dkk = dkk + lax.dot_general(dsb, qe, tn, preferred_element_type=F32)
                    dvv = dvv + lax.dot_general(p.astype(BF16), doe, tn, preferred_element_type=F32)
                aq[pl.ds(qs, ATTN_BLOCK), :] = dq
                ak[pl.ds(ks, nkeys), :] += dkk
                av[pl.ds(ks, nkeys), :] += dvv

            _branch_blocks(S, d, block)

        for a1, a4, a16, out_ref, cs_ref in ((aq1, aq4, aq16, dq_ref, csq_ref), (ak1, ak4, ak16, dk_ref, csk_ref),
                                             (av1, av4, av16, dv_ref, csv_ref)):
            stage[...] = a1[...]
            for d, sub in ((4, a4), (16, a16)):
                L = S // d
                for r in range(d):
                    stage[pl.ds(r, L, stride=d), :] += sub[r * L:(r + 1) * L, :]
            tot = stage[...]
            out_ref[0] = tot.astype(out_ref.dtype)
            _accumulate(cs_ref, first_b, jnp.sum(tot, axis=0, keepdims=True))

    blk = lambda off: pl.BlockSpec((1, S, LANES), lambda h, b: (b, 0, off + h))
    cs_spec = pl.BlockSpec((1, LANES), lambda h, b: (0, h))
    bias_spec = pl.BlockSpec((3, 2, ATTN_BLOCK, 2 * ATTN_BLOCK), lambda h, b: (0, h, 0, 0))
    qv = qkv.reshape(B, S, 3 * AW)
    view = lambda t: t.reshape(B, S, AW)
    sub_b = pltpu.VMEM((S, LANES), BF16)
    sub_f = pltpu.VMEM((S, LANES), F32)
    res = pl.pallas_call(
        body, grid=(HP, B),
        in_specs=[blk(0), blk(HP), blk(2 * HP), blk(0), blk(0), blk(0), bias_spec],
        out_specs=[blk(0), blk(0), blk(0), cs_spec, cs_spec, cs_spec, bias_spec],
        out_shape=[jax.ShapeDtypeStruct((B, S, AW), BF16)] * 3 + [jax.ShapeDtypeStruct((1, AW), F32)] * 3
        + [jax.ShapeDtypeStruct((3, H, ATTN_BLOCK, 2 * ATTN_BLOCK), F32)],
        scratch_shapes=[sub_f] + [sub_b] * 8 + [sub_f] * 4 + [sub_f] * 9,
        compiler_params=_params(2), name="attention_bwd",
    )(qv, qv, qv, view(do), view(lse), view(dd), bias_all)
    flat = lambda t: t.reshape(B * S, AW)
    return flat(res[0]), flat(res[1]), flat(res[2]), res[3], res[4], res[5], res[6]


def _regroup(src, stage, dst, d, S, off=0):
    if d == 1:
        dst[off:off + S, :] = src.astype(dst.dtype)
        return
    stage[...] = src.astype(F32)
    L = S // d
    for r in range(d):
        dst[off + r * L:off + (r + 1) * L, :] = stage[pl.ds(r, L, stride=d), :].astype(dst.dtype)


def _ungroup(sub_ref, off, nat_ref, d, S, add):
    L = S // d
    for r in range(d):
        rows = pl.ds(0, S) if d == 1 else pl.ds(r, L, stride=d)
        val = sub_ref[off + r * L:off + (r + 1) * L, :]
        if add:
            nat_ref[rows, :] += val
        else:
            nat_ref[rows, :] = val


def _branch_keys(ks, vs, S, nb, g_idx):
    blk3 = (S // ATTN_BLOCK, ATTN_BLOCK, LANES)
    kc3 = ks[ATTN_BLOCK:ATTN_BLOCK + S, :].reshape(blk3)
    vc3 = vs[ATTN_BLOCK:ATTN_BLOCK + S, :].reshape(blk3)
    if nb == 1:
        return kc3, vc3, None
    kk3 = jnp.concatenate([ks[0:S, :].reshape(blk3), kc3], axis=1)
    vv3 = jnp.concatenate([vs[0:S, :].reshape(blk3), vc3], axis=1)
    col = lax.broadcasted_iota(jnp.int32, (1, 1, 2 * ATTN_BLOCK), 2)
    dead = jnp.logical_and((g_idx & (nb - 1)) == 0, col < ATTN_BLOCK)
    return kk3, vv3, dead


def _branch_scores(qe, kk3, b_ref, bi, e, dead):
    s = jnp.einsum("gqe,gke->gqk", qe, kk3, preferred_element_type=F32)
    if dead is None:
        return s + b_ref[bi, e, :, ATTN_BLOCK:]
    return jnp.where(dead, NEG_INF, s + b_ref[bi, e])


def _attention_fwd(qkv, bias_all, B, S, AW, bg=None):
    HP = AW // LANES
    G = S // ATTN_BLOCK
    blk3 = (G, ATTN_BLOCK, LANES)

    def body(refs, bg_hook):
        q_ref, k_ref, v_ref, b_ref, o_ref, lse_ref, stage, qs, ks, vs, ot, lt, on0, on1, on2, ln0, ln1, ln2 = refs
        bg_hook(False)
        head0 = lax.broadcasted_iota(jnp.int32, (1, 1, LANES), 2) < HEAD_DIM
        g_idx = lax.broadcasted_iota(jnp.int32, (G, 1, 1), 0)
        ks[0:ATTN_BLOCK, :] = jnp.zeros((ATTN_BLOCK, LANES), BF16)
        vs[0:ATTN_BLOCK, :] = jnp.zeros((ATTN_BLOCK, LANES), BF16)
        nat_o, nat_l = (on0, on1, on2), (ln0, ln1, ln2)
        for bi, (_, d) in enumerate(DILATED_CONFIGS):
            nb = S // d // ATTN_BLOCK
            _regroup(q_ref[0], stage, qs, d, S)
            _regroup(k_ref[0], stage, ks, d, S, ATTN_BLOCK)
            _regroup(v_ref[0], stage, vs, d, S, ATTN_BLOCK)
            q3 = qs[...].reshape(blk3) * QK_SCALE
            kk3, vv3, dead = _branch_keys(ks, vs, S, nb, g_idx)
            outs, lses = [], []
            for e in range(2):
                msk = head0 if e == 0 else jnp.logical_not(head0)
                qe = jnp.where(msk, q3, jnp.zeros_like(q3))
                s = _branch_scores(qe, kk3, b_ref, bi, e, dead)
                m = jnp.max(s, axis=-1, keepdims=True)
                p = jnp.exp(s - m)
                l = jnp.sum(p, axis=-1, keepdims=True)
                o = jnp.einsum("gqk,gke->gqe", p.astype(BF16), vv3, preferred_element_type=F32)
                outs.append(o / l)
                lses.append(jnp.broadcast_to(m + jnp.log(l), blk3))
            ot[...] = jnp.where(head0, outs[0], outs[1]).reshape(S, LANES)
            lt[...] = jnp.where(head0, lses[0], lses[1]).reshape(S, LANES)
            _ungroup(ot, 0, nat_o[bi], d, S, add=False)
            _ungroup(lt, 0, nat_l[bi], d, S, add=False)

        la, lb, lc = ln0[...], ln1[...], ln2[...]
        m = jnp.maximum(jnp.maximum(la, lb), lc)
        ea, eb, ec = jnp.exp(la - m), jnp.exp(lb - m), jnp.exp(lc - m)
        den = ea + eb + ec
        lse_ref[0] = m + jnp.log(den)
        o_ref[0] = (ea * on0[...] + eb * on1[...] + ec * on2[...]) / den
        bg_hook(True)

    blk = lambda off: pl.BlockSpec((1, S, LANES), lambda b, h: (b, 0, off + h))
    qv = qkv.reshape(B, S, 3 * AW)
    sub_f = pltpu.VMEM((S, LANES), F32)
    pad_b = pltpu.VMEM((S + ATTN_BLOCK, LANES), BF16)
    res = _hosted_call(
        body, bg, grid=(B, HP),
        in_specs=[blk(0), blk(HP), blk(2 * HP),
                  pl.BlockSpec((3, 2, ATTN_BLOCK, 2 * ATTN_BLOCK), lambda b, h: (0, h, 0, 0))],
        out_specs=[blk(0), blk(0)],
        out_shape=[jax.ShapeDtypeStruct((B, S, AW), F32)] * 2,
        scratch_shapes=[sub_f, pltpu.VMEM((S, LANES), BF16), pad_b, pad_b] + [sub_f] * 8,
        operands=[qv, qv, qv, bias_all], name="attention_fwd")
    return (res[0].reshape(B * S, AW), res[1].reshape(B * S, AW)) + tuple(res[2:])


def _attention_bwd(qkv, do, lse, dd, bias_all, B, S, AW, bg=None):
    HP = AW // LANES
    H = AW // HEAD_DIM
    G = S // ATTN_BLOCK
    blk3 = (G, ATTN_BLOCK, LANES)
    PAD = ATTN_BLOCK

    def body(refs, bg_hook):
        (q_ref, k_ref, v_ref, do_ref, lse_ref, dd_ref, b_ref,
         dq_ref, dk_ref, dv_ref, csq_ref, csk_ref, csv_ref, db_ref,
         stage, qs, ks, vs, gs, ls, ds_, tq, tk, tv, accq, acck, accv) = refs
        bg_hook(False)
        head0 = lax.broadcasted_iota(jnp.int32, (1, 1, LANES), 2) < HEAD_DIM
        g_idx = lax.broadcasted_iota(jnp.int32, (G, 1, 1), 0)
        first_b = pl.program_id(1) == 0

        @pl.when(first_b)
        def _():
            db_ref[...] = jnp.zeros_like(db_ref)

        ks[0:PAD, :] = jnp.zeros((PAD, LANES), BF16)
        vs[0:PAD, :] = jnp.zeros((PAD, LANES), BF16)
        tk[0:PAD, :] = jnp.zeros((PAD, LANES), F32)
        tv[0:PAD, :] = jnp.zeros((PAD, LANES), F32)
        for bi, (_, d) in enumerate(DILATED_CONFIGS):
            nb = S // d // ATTN_BLOCK
            _regroup(q_ref[0], stage, qs, d, S)
            _regroup(k_ref[0], stage, ks, d, S, PAD)
            _regroup(v_ref[0], stage, vs, d, S, PAD)
            _regroup(do_ref[0], stage, gs, d, S)
            _regroup(lse_ref[0], stage, ls, d, S)
            _regroup(dd_ref[0], stage, ds_, d, S)
            q3 = qs[...].reshape(blk3) * QK_SCALE
            do3 = gs[...].reshape(blk3)
            lse3 = ls[...].reshape(blk3)
            dd3 = ds_[...].reshape(blk3)
            kk3, vv3, dead = _branch_keys(ks, vs, S, nb, g_idx)
            dq = jnp.zeros(blk3, F32)
            dkk = jnp.zeros(kk3.shape, F32)
            dvv = jnp.zeros(kk3.shape, F32)
            for e in range(2):
                msk = head0 if e == 0 else jnp.logical_not(head0)
                c0 = e * HEAD_DIM
                qe = jnp.where(msk, q3, jnp.zeros_like(q3))
                doe = jnp.where(msk, do3, jnp.zeros_like(do3))
                ke = jnp.where(msk, kk3 * QK_SCALE, jnp.zeros_like(kk3))
                s = _branch_scores(qe, kk3, b_ref, bi, e, dead)
                p = jnp.exp(s - lse3[:, :, c0:c0 + 1])
                dp = jnp.einsum("gqe,gke->gqk", doe, vv3, preferred_element_type=F32)
                dsc = p * (dp - dd3[:, :, c0:c0 + 1])
                if dead is None:
                    db_ref[bi, e, :, ATTN_BLOCK:] += jnp.sum(dsc, axis=0)
                else:
                    db_ref[bi, e] += jnp.sum(dsc, axis=0)
                dsb = dsc.astype(BF16)
                dq = dq + jnp.einsum("gqk,gke->gqe", dsb, ke, preferred_element_type=F32)
                dkk = dkk + jnp.einsum("gqk,gqe->gke", dsb, qe, preferred_element_type=F32)
                dvv = dvv + jnp.einsum("gqk,gqe->gke", p.astype(BF16), doe, preferred_element_type=F32)
            tq[...] = dq.reshape(S, LANES)
            if dead is None:
                tk[PAD:PAD + S, :] = dkk.reshape(S, LANES)
                tv[PAD:PAD + S, :] = dvv.reshape(S, LANES)
            else:
                tk[PAD:PAD + S, :] = dkk[:, ATTN_BLOCK:, :].reshape(S, LANES)
                tv[PAD:PAD + S, :] = dvv[:, ATTN_BLOCK:, :].reshape(S, LANES)
                tk[0:S, :] += dkk[:, :ATTN_BLOCK, :].reshape(S, LANES)
                tv[0:S, :] += dvv[:, :ATTN_BLOCK, :].reshape(S, LANES)
            _ungroup(tq, 0, accq, d, S, add=bi > 0)
            _ungroup(tk, PAD, acck, d, S, add=bi > 0)
            _ungroup(tv, PAD, accv, d, S, add=bi > 0)

        for acc, out_ref, cs_ref in ((accq, dq_ref, csq_ref), (acck, dk_ref, csk_ref), (accv, dv_ref, csv_ref)):
            tot = acc[...]
            out_ref[0] = tot.astype(out_ref.dtype)
            _accumulate(cs_ref, first_b, jnp.sum(tot, axis=0, keepdims=True))
        bg_hook(True)

    blk = lambda off: pl.BlockSpec((1, S, LANES), lambda h, b: (b, 0, off + h))
    cs_spec = pl.BlockSpec((1, LANES), lambda h, b: (0, h))
    bias_spec = pl.BlockSpec((3, 2, ATTN_BLOCK, 2 * ATTN_BLOCK), lambda h, b: (0, h, 0, 0))
    qv = qkv.reshape(B, S, 3 * AW)
    view = lambda t: t.reshape(B, S, AW)
    sub_b = pltpu.VMEM((S, LANES), BF16)
    sub_f = pltpu.VMEM((S, LANES), F32)
    pad_b = pltpu.VMEM((S + PAD, LANES), BF16)
    pad_f = pltpu.VMEM((S + PAD, LANES), F32)
    res = _hosted_call(
        body, bg, grid=(HP, B),
        in_specs=[blk(0), blk(HP), blk(2 * HP), blk(0), blk(0), blk(0), bias_spec],
        out_specs=[blk(0), blk(0), blk(0), cs_spec, cs_spec, cs_spec, bias_spec],
        out_shape=[jax.ShapeDtypeStruct((B, S, AW), BF16)] * 3 + [jax.ShapeDtypeStruct((1, AW), F32)] * 3
        + [jax.ShapeDtypeStruct((3, H, ATTN_BLOCK, 2 * ATTN_BLOCK), F32)],
        scratch_shapes=[sub_f, sub_b, pad_b, pad_b, sub_b, sub_f, sub_f, sub_f, pad_f, pad_f, sub_f, sub_f, sub_f],
        operands=[qv, qv, qv, view(do), view(lse), view(dd), bias_all], name="attention_bwd")
    flat = lambda t: t.reshape(B * S, AW)
    return (flat(res[0]), flat(res[1]), flat(res[2]), res[3], res[4], res[5], res[6]) + tuple(res[7:])


def _attn_norm(attn, gain, tm):
    T, AW = attn.shape

    def body(a_ref, g_ref, mix_ref, r_ref):
        a = a_ref[...]
        r = lax.rsqrt(jnp.mean(a * a, axis=-1, keepdims=True) + LN_EPS)
        mix_ref[...] = (a * r * g_ref[...]).astype(BF16)
        r_ref[...] = jnp.broadcast_to(r, (tm, LANES))

    row = pl.BlockSpec((tm, AW), lambda i: (i, 0))
    return pl.pallas_call(
        body, grid=(T // tm,), in_specs=[row, pl.BlockSpec((1, AW), lambda i: (0, 0))],
        out_specs=[row, pl.BlockSpec((tm, LANES), lambda i: (i, 0))],
        out_shape=[jax.ShapeDtypeStruct((T, AW), BF16), jax.ShapeDtypeStruct((T, LANES), F32)],
        compiler_params=_params(1), name="attn_norm",
    )(attn, gain)


def _attn_pre_bwd(dmixed, attn, rstd, gain, tm):
    T, AW = attn.shape
    ones_np = np.kron(np.eye(AW // HEAD_DIM, dtype=np.float32), np.ones((HEAD_DIM, HEAD_DIM), np.float32))
    ones_bd = jnp.asarray(ones_np, dtype=BF16)

    def body(dm_ref, a_ref, r_ref, g_ref, ones_ref, do_ref, dd_ref, dg_ref):
        i = pl.program_id(0)
        dm = dm_ref[...]
        a = a_ref[...]
        r = r_ref[:, 0:1]
        dxn = dm * g_ref[...]
        da = r * (dxn - a * (r * r) * jnp.mean(dxn * a, axis=-1, keepdims=True))
        do_ref[...] = da.astype(BF16)
        hi, lo = _split_hi_lo(da * a)
        dd_ref[...] = (jnp.dot(hi, ones_ref[...], preferred_element_type=F32)
                       + jnp.dot(lo, ones_ref[...], preferred_element_type=F32))
        _accumulate(dg_ref, i == 0, jnp.sum(dm * a * r, axis=0, keepdims=True))

    row = pl.BlockSpec((tm, AW), lambda i: (i, 0))
    vec = pl.BlockSpec((1, AW), lambda i: (0, 0))
    return pl.pallas_call(
        body, grid=(T // tm,),
        in_specs=[row, row, pl.BlockSpec((tm, LANES), lambda i: (i, 0)), vec,
                  pl.BlockSpec((AW, AW), lambda i: (0, 0))],
        out_specs=[row, row, vec],
        out_shape=[jax.ShapeDtypeStruct((T, AW), BF16), jax.ShapeDtypeStruct((T, AW), F32),
                   jax.ShapeDtypeStruct((1, AW), F32)],
        compiler_params=_params(1), name="attn_pre_bwd",
    )(dmixed, attn, rstd, gain, ones_bd)


class _RowShifts:
    def __init__(self, x, row, up):
        self.x, self.row, self.up, self.base = x, row, up, {0: x}

    def __call__(self, s):
        x = self.x
        n, c = x.shape
        r, whole = s % 8, s - s % 8
        if r not in self.base:
            if self.up:
                rolled = pltpu.roll(x, n - r, 0)
                tail = jnp.where(self.row[n - 8:] < n - r, rolled[n - 8:], 0.0)
                self.base[r] = jnp.concatenate([rolled[:n - 8], tail], axis=0)
            else:
                rolled = pltpu.roll(x, r, 0)
                head = jnp.where(self.row[:8] >= r, rolled[:8], 0.0)
                self.base[r] = jnp.concatenate([head, rolled[8:]], axis=0)
        y = self.base[r]
        if whole == 0:
            return y
        pad = jnp.zeros((whole, c), x.dtype)
        if self.up:
            return jnp.concatenate([y[whole:], pad], axis=0)
        return jnp.concatenate([pad, y[:n - whole]], axis=0)


def _conv_branch_fwd_math(a, g, w_ref, cb, lg, lb, row):
    sg = _sigmoid(g)
    u0 = a * sg
    u0_down = _RowShifts(u0, row, up=False)
    uc = jnp.zeros_like(u0) + cb
    for k in range(CONV_KERNEL):
        uc = uc + w_ref[k:k + 1, :] * u0_down(CONV_KERNEL - 1 - k)
    ul, xh, r = _ln_fwd(uc, lg, lb)
    su = _sigmoid(ul)
    u = ul * su
    return sg, u0_down, ul, xh, r, su, u


def _conv_fwd(ag, conv_w, conv_b, ln_g, ln_b, norm_g, B, S, CW):
    def body(a_ref, g_ref, w_ref, cb_ref, lg_ref, lb_ref, ng_ref, o_ref):
        row = lax.broadcasted_iota(jnp.int32, (S, CW), 0)
        _, _, _, _, _, _, u = _conv_branch_fwd_math(a_ref[0], g_ref[0], w_ref, cb_ref[...], lg_ref[...],
                                                    lb_ref[...], row)
        rr = lax.rsqrt(jnp.mean(u * u, axis=-1, keepdims=True) + LN_EPS)
        o_ref[0] = (u * rr * ng_ref[...]).astype(BF16)

    vec = pl.BlockSpec((1, CW), lambda b: (0, 0))
    out = pl.pallas_call(
        body, grid=(B,),
        in_specs=[pl.BlockSpec((1, S, CW), lambda b: (b, 0, 0)), pl.BlockSpec((1, S, CW), lambda b: (b, 0, 1)),
                  pl.BlockSpec((CONV_KERNEL, CW), lambda b: (0, 0)), vec, vec, vec, vec],
        out_specs=pl.BlockSpec((1, S, CW), lambda b: (b, 0, 0)),
        out_shape=jax.ShapeDtypeStruct((B, S, CW), BF16),
        compiler_params=_params(1), name="conv_fwd",
    )(ag.reshape(B, S, 2 * CW), ag.reshape(B, S, 2 * CW), conv_w, conv_b, ln_g, ln_b, norm_g)
    return out.reshape(B * S, CW)


def _conv_bwd(ag, dmixed, conv_w, conv_b, ln_g, ln_b, norm_g, B, S, CW, D):
    AW = D - CW
    assert AW % CW == 0

    def body(a_ref, g_ref, dm_ref, w_ref, cb_ref, lg_ref, lb_ref, ng_ref,
             dag_ref, dw_ref, dcb_ref, dlg_ref, dlb_ref, dng_ref):
        b = pl.program_id(0)
        row = lax.broadcasted_iota(jnp.int32, (S, CW), 0)
        a, g = a_ref[0], g_ref[0]
        sg, u0_down, ul, xh, r, su, u = _conv_branch_fwd_math(a, g, w_ref, cb_ref[...], lg_ref[...], lb_ref[...], row)
        rr = lax.rsqrt(jnp.mean(u * u, axis=-1, keepdims=True) + LN_EPS)
        dm = dm_ref[0]
        dxn = dm * ng_ref[...]
        du = rr * (dxn - u * (rr * rr) * jnp.mean(dxn * u, axis=-1, keepdims=True))
        dul = du * su * (1.0 + ul * (1.0 - su))
        duc = _ln_bwd(dul, xh, r, lg_ref[...])
        first = b == 0
        _accumulate(dng_ref, first, jnp.sum(dm * u * rr, axis=0, keepdims=True))
        _accumulate(dlg_ref, first, jnp.sum(dul * xh, axis=0, keepdims=True))
        _accumulate(dlb_ref, first, jnp.sum(dul, axis=0, keepdims=True))
        _accumulate(dcb_ref, first, jnp.sum(duc, axis=0, keepdims=True))

        @pl.when(first)
        def _():
            dw_ref[...] = jnp.zeros_like(dw_ref)

        duc_up = _RowShifts(duc, row, up=True)
        du0 = jnp.zeros_like(duc)
        for k in range(CONV_KERNEL):
            sh = CONV_KERNEL - 1 - k
            dw_ref[k:k + 1, :] += jnp.sum(duc * u0_down(sh), axis=0, keepdims=True)
            du0 = du0 + w_ref[k:k + 1, :] * duc_up(sh)
        dag_ref[0, :, :CW] = du0 * sg
        dag_ref[0, :, CW:] = du0 * a * sg * (1.0 - sg)

    vec = pl.BlockSpec((1, CW), lambda b: (0, 0))
    wspec = pl.BlockSpec((CONV_KERNEL, CW), lambda b: (0, 0))
    agv = ag.reshape(B, S, 2 * CW)
    res = pl.pallas_call(
        body, grid=(B,),
        in_specs=[pl.BlockSpec((1, S, CW), lambda b: (b, 0, 0)), pl.BlockSpec((1, S, CW), lambda b: (b, 0, 1)),
                  pl.BlockSpec((1, S, CW), lambda b: (b, 0, AW // CW)), wspec, vec, vec, vec, vec],
        out_specs=[pl.BlockSpec((1, S, 2 * CW), lambda b: (b, 0, 0)), wspec, vec, vec, vec, vec],
        out_shape=[jax.ShapeDtypeStruct((B, S, 2 * CW), F32), jax.ShapeDtypeStruct((CONV_KERNEL, CW), F32)]
        + [jax.ShapeDtypeStruct((1, CW), F32)] * 4,
        compiler_params=_params(1), name="conv_bwd",
    )(agv, agv, dmixed.reshape(B, S, D), conv_w, conv_b, ln_g, ln_b, norm_g)
    return (res[0].reshape(B * S, 2 * CW),) + tuple(res[1:])


def _ffn_conv(x, w_ref, bias, row):
    y = jnp.zeros_like(x) + bias
    for k in range(FFN_CONV_KERNEL):
        y = y + w_ref[k:k + 1, :] * _shift_down(x, FFN_CONV_KERNEL - 1 - k, row)
    return y


def _ffn_specs(S, tc, nj, order):
    pick = (lambda b, j: (b, j)) if order == "bj" else (lambda j, b: (b, j))
    act = lambda off: pl.BlockSpec((1, S, tc), lambda *g: (pick(*g)[0], 0, off + pick(*g)[1]))
    cw = lambda off: pl.BlockSpec((FFN_CONV_KERNEL, tc), lambda *g: (0, off + pick(*g)[1]))
    cb = lambda off: pl.BlockSpec((1, tc), lambda *g: (0, off + pick(*g)[1]))
    return act, cw, cb


def _ffn_act(upre, cw, cb, B, S, DFF):
    tc = FFN_COLS
    nj = DFF // tc

    def body(ug_ref, uv_ref, wg_ref, wv_ref, bg_ref, bv_ref, o_ref):
        row = lax.broadcasted_iota(jnp.int32, (S, tc), 0)
        gate = _ffn_conv(ug_ref[0], wg_ref, bg_ref[...], row)
        val = _ffn_conv(uv_ref[0], wv_ref, bv_ref[...], row)
        o_ref[0] = (gate * _sigmoid(gate) * val).astype(BF16)

    act, cws, cbs = _ffn_specs(S, tc, nj, "bj")
    uv = upre.reshape(B, S, 2 * DFF)
    out = pl.pallas_call(
        body, grid=(B, nj), in_specs=[act(0), act(nj), cws(0), cws(nj), cbs(0), cbs(nj)], out_specs=act(0),
        out_shape=jax.ShapeDtypeStruct((B, S, DFF), BF16), compiler_params=_params(2), name="ffn_act",
    )(uv, uv, cw, cw, cb, cb)
    return out.reshape(B * S, DFF)


def _ffn_bwd(upre, dact, cw, cb, B, S, DFF):
    tc = FFN_COLS
    nj = DFF // tc

    def body(ug_ref, uv_ref, da_ref, wg_ref, wv_ref, bg_ref, bv_ref, dug_ref, duv_ref, dwg_ref, dwv_ref,
             dbg_ref, dbv_ref):
        first = pl.program_id(1) == 0
        row = lax.broadcasted_iota(jnp.int32, (S, tc), 0)
        ug, uv = ug_ref[0], uv_ref[0]
        gate = _ffn_conv(ug, wg_ref, bg_ref[...], row)
        val = _ffn_conv(uv, wv_ref, bv_ref[...], row)
        sg = _sigmoid(gate)
        dact_b = da_ref[0]
        dgate = dact_b * val * sg * (1.0 + gate * (1.0 - sg))
        dval = dact_b * gate * sg
        for dup, u, w_ref, du_ref, dw_ref, db_ref in ((dgate, ug, wg_ref, dug_ref, dwg_ref, dbg_ref),
                                                      (dval, uv, wv_ref, duv_ref, dwv_ref, dbv_ref)):
            _accumulate(db_ref, first, jnp.sum(dup, axis=0, keepdims=True))

            @pl.when(first)
            def _(dw_ref=dw_ref):
                dw_ref[...] = jnp.zeros_like(dw_ref)

            dupre = jnp.zeros_like(dup)
            for k in range(FFN_CONV_KERNEL):
                sh = FFN_CONV_KERNEL - 1 - k
                dw_ref[k:k + 1, :] += jnp.sum(dup * _shift_down(u, sh, row), axis=0, keepdims=True)
                dupre = dupre + w_ref[k:k + 1, :] * _shift_up(dup, sh, row)
            du_ref[0] = dupre.astype(BF16)

    act, cws, cbs = _ffn_specs(S, tc, nj, "jb")
    uv = upre.reshape(B, S, 2 * DFF)
    res = pl.pallas_call(
        body, grid=(nj, B),
        in_specs=[act(0), act(nj), act(0), cws(0), cws(nj), cbs(0), cbs(nj)],
        out_specs=[act(0), act(0), cws(0), cws(0), cbs(0), cbs(0)],
        out_shape=[jax.ShapeDtypeStruct((B, S, DFF), BF16)] * 2
        + [jax.ShapeDtypeStruct((FFN_CONV_KERNEL, DFF), F32)] * 2 + [jax.ShapeDtypeStruct((1, DFF), F32)] * 2,
        compiler_params=_params(2), name="ffn_bwd",
    )(uv, uv, dact.reshape(B, S, DFF), cw, cw, cb, cb)
    flat = lambda t: t.reshape(B * S, DFF)
    return (flat(res[0]), flat(res[1]), jnp.concatenate([res[2], res[3]], axis=1),
            jnp.concatenate([res[4], res[5]], axis=1))


FFN_HALO = 16


def _half_sequences(S):
    if S < 8 * FFN_HALO:
        return [(0, S, 0, S)]
    h = S // 2
    return [(0, h + FFN_HALO, 0, h), (h - FFN_HALO, S, FFN_HALO, h)]


def _w_up_block_spec(w_up_sh, tc, off):
    _, D, cs = w_up_sh.shape
    assert cs % tc == 0
    bps = cs // tc
    return pl.BlockSpec((1, D, tc), lambda j: ((off + j) // bps, 0, (off + j) % bps))


def _ffn_fwd_fused(x1b, w_up_sh, cw, cb, B, S, DFF):
    tc = FFN_COLS
    nj = DFF // tc
    D = x1b.shape[1]

    def body(x_ref, wg_ref, wv_ref, cwg_ref, cwv_ref, cbg_ref, cbv_ref, o_ref, up_ref):
        w = jnp.concatenate([wg_ref[0], wv_ref[0]], axis=1)
        row = lax.broadcasted_iota(jnp.int32, (S, tc), 0)
        for b in range(B):
            up = jnp.dot(x_ref[b], w, preferred_element_type=F32)
            up_ref[b] = up
            gate = _ffn_conv(up[:, :tc], cwg_ref, cbg_ref[...], row)
            val = _ffn_conv(up[:, tc:], cwv_ref, cbv_ref[...], row)
            o_ref[b] = (gate * _sigmoid(gate) * val).astype(BF16)

    cws = lambda off: pl.BlockSpec((FFN_CONV_KERNEL, tc), lambda j: (0, off + j))
    cbs = lambda off: pl.BlockSpec((1, tc), lambda j: (0, off + j))
    act, upre = pl.pallas_call(
        body, grid=(nj,),
        in_specs=[pl.BlockSpec((B, S, D), lambda j: (0, 0, 0), pipeline_mode=pl.Buffered(1)),
                  _w_up_block_spec(w_up_sh, tc, 0), _w_up_block_spec(w_up_sh, tc, nj),
                  cws(0), cws(nj), cbs(0), cbs(nj)],
        out_specs=[pl.BlockSpec((B, S, tc), lambda j: (0, 0, j)), pl.BlockSpec((B, S, 2 * tc), lambda j: (0, 0, j))],
        out_shape=[jax.ShapeDtypeStruct((B, S, DFF), BF16), jax.ShapeDtypeStruct((B, S, 2 * DFF), F32)],
        compiler_params=_params(1), name="ffn_fwd",
    )(x1b.reshape(B, S, D), w_up_sh, w_up_sh, cw, cw, cb, cb)
    return act.reshape(B * S, DFF), upre


def _ffn_bwd_fused(x1b, dz2b, upre, w_down, cw, cb, B, S, DFF):
    tc = FFN_COLS
    nj = DFF // tc
    D = x1b.shape[1]

    def body(x_ref, dz_ref, up_ref, wd_ref, cwg_ref, cwv_ref, cbg_ref, cbv_ref,
             dug_ref, duv_ref, dwu_ref, dwd_ref, dcw_ref, dcb_ref):
        first = pl.program_id(1) == 0
        dw_t = dwd = None
        dcb = [None, None]
        dcw = [[None] * FFN_CONV_KERNEL, [None] * FFN_CONV_KERNEL]
        add = lambda old, new: new if old is None else old + new
        for lo, hi, o0, on in _half_sequences(S):
            n = hi - lo
            own = slice(o0, o0 + on)
            row = lax.broadcasted_iota(jnp.int32, (n, tc), 0)
            x = x_ref[0, lo:hi, :]
            dz = dz_ref[0, lo:hi, :]
            ug, uv = up_ref[0, lo:hi, :tc], up_ref[0, lo:hi, tc:]
            gate = _ffn_conv(ug, cwg_ref, cbg_ref[...], row)
            val = _ffn_conv(uv, cwv_ref, cbv_ref[...], row)
            sg = _sigmoid(gate)
            act = (gate * sg * val).astype(BF16)
            dact = _dot(dz, wd_ref[...], "nt")
            dgate = dact * val * sg * (1.0 + gate * (1.0 - sg))
            dval = dact * gate * sg
            dupre = []
            for h, (dup, u, w_ref) in enumerate(((dgate, ug, cwg_ref), (dval, uv, cwv_ref))):
                dcb[h] = add(dcb[h], jnp.sum(dup[own], axis=0, keepdims=True))
                acc = jnp.zeros_like(dup)
                for k in range(FFN_CONV_KERNEL):
                    sh = FFN_CONV_KERNEL - 1 - k
                    dcw[h][k] = add(dcw[h][k], jnp.sum((dup * _shift_down(u, sh, row))[own], axis=0, keepdims=True))
                    acc = acc + w_ref[k:k + 1, :] * _shift_up(dup, sh, row)
                dupre.append(acc.astype(BF16)[own])
            dug_ref[0, lo + o0:lo + o0 + on, :] = dupre[0]
            duv_ref[0, lo + o0:lo + o0 + on, :] = dupre[1]
            dw_t = add(dw_t, _dot(jnp.concatenate(dupre, axis=1), x[own], "tn"))
            dwd = add(dwd, _dot(act[own], dz[own], "tn"))
        _accumulate(dwu_ref.at[0], first, dw_t[:tc])
        _accumulate(dwu_ref.at[1], first, dw_t[tc:])
        _accumulate(dwd_ref, first, dwd)
        for h in range(2):
            _accumulate(dcb_ref.at[h], first, dcb[h])
            for k in range(FFN_CONV_KERNEL):
                _accumulate(dcw_ref.at[k, pl.ds(h, 1), :], first, dcw[h][k])

    act_s, cws, cbs = _ffn_specs(S, tc, nj, "jb")
    seq = pl.BlockSpec((1, S, D), lambda j, b: (b, 0, 0))
    res = pl.pallas_call(
        body, grid=(nj, B),
        in_specs=[seq, seq, pl.BlockSpec((1, S, 2 * tc), lambda j, b: (b, 0, j)),
                  pl.BlockSpec((tc, D), lambda j, b: (j, 0)), cws(0), cws(nj), cbs(0), cbs(nj)],
        out_specs=[act_s(0), act_s(0), pl.BlockSpec((2, tc, D), lambda j, b: (0, j, 0)),
                   pl.BlockSpec((tc, D), lambda j, b: (j, 0)),
                   pl.BlockSpec((FFN_CONV_KERNEL, 2, tc), lambda j, b: (0, 0, j)),
                   pl.BlockSpec((2, 1, tc), lambda j, b: (0, 0, j))],
        out_shape=[jax.ShapeDtypeStruct((B, S, DFF), BF16)] * 2
        + [jax.ShapeDtypeStruct((2, DFF, D), F32), jax.ShapeDtypeStruct((DFF, D), F32),
           jax.ShapeDtypeStruct((FFN_CONV_KERNEL, 2, DFF), F32), jax.ShapeDtypeStruct((2, 1, DFF), F32)],
        compiler_params=_params(2), name="ffn_bwd",
    )(x1b.reshape(B, S, D), dz2b.reshape(B, S, D), upre, w_down, cw, cw, cb, cb)
    flat = lambda t: t.reshape(B * S, DFF)
    return flat(res[0]), flat(res[1]), res[2], res[3], res[4], res[5]


def _transpose(x, name):
    R, C = x.shape
    tr = LANES if R % LANES == 0 else R

    def body(x_ref, o_ref):
        o_ref[...] = x_ref[...].T

    return pl.pallas_call(
        body, grid=(R // tr,), in_specs=[pl.BlockSpec((tr, C), lambda i: (i, 0))],
        out_specs=pl.BlockSpec((C, tr), lambda i: (0, i)), out_shape=jax.ShapeDtypeStruct((C, R), F32),
        compiler_params=_params(1), name=name)(x)


def _dh_cat(dq, dk, dv, dag, tm):
    T, AW = dq.shape
    CW2 = dag.shape[1]
    W = 3 * AW + CW2

    def body(dq_ref, dk_ref, dv_ref, dag_ref, dh_ref, cs_ref):
        for c, ref in enumerate((dq_ref, dk_ref, dv_ref)):
            dh_ref[:, c * AW:(c + 1) * AW] = ref[...]
        dg = dag_ref[...]
        dh_ref[:, 3 * AW:] = dg.astype(BF16)
        _accumulate(cs_ref, pl.program_id(0) == 0, jnp.sum(dg, axis=0, keepdims=True))

    row = pl.BlockSpec((tm, AW), lambda i: (i, 0))
    return pl.pallas_call(
        body, grid=(T // tm,),
        in_specs=[row] * 3 + [pl.BlockSpec((tm, CW2), lambda i: (i, 0))],
        out_specs=[pl.BlockSpec((tm, W), lambda i: (i, 0)), pl.BlockSpec((1, CW2), lambda i: (0, 0))],
        out_shape=[jax.ShapeDtypeStruct((T, W), BF16), jax.ShapeDtypeStruct((1, CW2), F32)],
        compiler_params=_params(1), name="dh_cat",
    )(dq, dk, dv, dag)


def _local_step(x, target, rel_table, w_in, b_in, conv_w, conv_b, conv_ln_g, conv_ln_b, attn_norm_g,
                conv_norm_g, staged, ln1_g, ln1_b, ffn_cw, ffn_cb, ln2_g, ln2_b, ids):
    B, S, D = x.shape
    T = B * S
    AW = attn_norm_g.shape[-1]
    CW = conv_norm_g.shape[-1]
    H = AW // HEAD_DIM
    DFF = staged[2].shape[0] * staged[2].shape[1]
    INW = 3 * AW + 2 * CW
    xf = x.reshape(T, D)
    tf = target.reshape(T, D)
    tm = _row_tile(T, 512)
    tm_s = _row_tile(T, 256)

    bucket_np, mask_np = _bucket_tables()
    bucket = jnp.asarray(bucket_np)
    band_mask = jnp.asarray(mask_np)
    bias_all = _bias_build(rel_table.T, bucket, band_mask).reshape(3, H, ATTN_BLOCK, 2 * ATTN_BLOCK)

    tn_qkv = _col_tile(3 * AW, 1152)
    qkv = _mm_plain(xf, w_in[:, :3 * AW], mode="nn", tm=tm, tn=tn_qkv, tk=D, out_dtype=BF16,
                    bias=b_in[:, :3 * AW], name="mm_qkv")
    ag = _mm_plain(xf, w_in[:, 3 * AW:], mode="nn", tm=tm, tn=2 * CW, tk=D, out_dtype=F32,
                   bias=b_in[:, 3 * AW:], name="mm_ag")

    attn, lse, w_out_g, w_up_sh, w_down_g = _attention_fwd(qkv, bias_all, B, S, AW, bg=_bg_gather(staged))
    w_out = w_out_g.reshape(D, D)
    w_down = w_down_g.reshape(DFF, D)
    mixed_a, r_attn = _attn_norm(attn, attn_norm_g, tm_s)
    mixed_c = _conv_fwd(ag, conv_w, conv_b, conv_ln_g, conv_ln_b, conv_norm_g, B, S, CW)
    mixed = jnp.concatenate([mixed_a, mixed_c], axis=1)

    def ln1_epilogue(acc, i, j, extra_refs, out_refs):
        x_ref, g_ref, b_ref = extra_refs
        x1, xh, r = _ln_fwd(acc + ALPHA * x_ref[...], g_ref[...], b_ref[...])
        out_refs[0][...] = x1
        out_refs[1][...] = x1.astype(BF16)
        out_refs[2][...] = xh
        out_refs[3][...] = jnp.broadcast_to(r, (tm_s, LANES))

    rowD = lambda i, j, k: (i, 0)
    vecD = lambda i, j, k: (0, 0)
    x1, x1b, xh1, r1 = _matmul(
        mixed, w_out, mode="nn", tm=tm_s, tn=D, tk=D,
        extras=[(xf, (tm_s, D), rowD), (ln1_g, (1, D), vecD), (ln1_b, (1, D), vecD)],
        outs=[((T, D), F32, (tm_s, D), rowD), ((T, D), BF16, (tm_s, D), rowD), ((T, D), F32, (tm_s, D), rowD),
              ((T, LANES), F32, (tm_s, LANES), rowD)],
        epilogue=ln1_epilogue, name="mm_out_ln1")

    NS, _, cs = w_up_sh.shape
    half = NS // 2

    act, upre = _ffn_fwd_fused(x1b, w_up_sh, ffn_cw, ffn_cb, B, S, DFF)

    def ln2_epilogue(acc, i, j, extra_refs, out_refs):
        x1_ref, g_ref, b_ref, t_ref = extra_refs
        dz_ref, dzb_ref, loss_ref, dg_ref, db_ref = out_refs
        g = g_ref[...]
        y, xh, r = _ln_fwd(acc + ALPHA * x1_ref[...], g, b_ref[...])
        diff = y - t_ref[...]
        row_loss = jnp.sum(diff * diff, axis=1, keepdims=True)
        tile_loss = jnp.sum(row_loss, axis=0, keepdims=True) * (0.5 / D)
        dy = diff * (1.0 / D)
        dz = _ln_bwd(dy, xh, r, g)
        dz_ref[...] = dz
        dzb_ref[...] = dz.astype(BF16)
        first = i == 0
        _accumulate(loss_ref, first, jnp.broadcast_to(tile_loss, (1, LANES)))
        _accumulate(dg_ref, first, jnp.sum(dy * xh, axis=0, keepdims=True))
        _accumulate(db_ref, first, jnp.sum(dy, axis=0, keepdims=True))

    dz2, dz2b, loss_part, d_ln2_g, d_ln2_b = _matmul(
        act, w_down, mode="nn", tm=tm_s, tn=D, tk=DFF,
        extras=[(x1, (tm_s, D), rowD), (ln2_g, (1, D), vecD), (ln2_b, (1, D), vecD), (tf, (tm_s, D), rowD)],
        outs=[((T, D), F32, (tm_s, D), rowD), ((T, D), BF16, (tm_s, D), rowD),
              ((1, LANES), F32, (1, LANES), vecD), ((1, D), F32, (1, D), vecD), ((1, D), F32, (1, D), vecD)],
        epilogue=ln2_epilogue, name="mm_down_ln2_loss")

    dupre_g, dupre_v, d_w_up_t, d_w_down, d_ffn_cw2, d_ffn_cb2 = _ffn_bwd_fused(
        x1b, dz2b, upre, w_down, ffn_cw, ffn_cb, B, S, DFF)
    d_w_up_t = d_w_up_t.reshape(NS, cs, D)
    d_ffn_cw = d_ffn_cw2.reshape(FFN_CONV_KERNEL, 2 * DFF)
    d_ffn_cb = d_ffn_cb2.reshape(1, 2 * DFF)
    tk_t = _row_tile(T, 512)

    def ln1_bwd_epilogue(acc, i, j, extra_refs, out_refs):
        dz2_ref, xh_ref, r_ref, g_ref = extra_refs
        dz_ref, dzb_ref, dg_ref, db_ref = out_refs
        dx1 = acc + ALPHA * dz2_ref[...]
        xh = xh_ref[...]
        dz = _ln_bwd(dx1, xh, r_ref[:, 0:1], g_ref[...])
        dz_ref[...] = dz
        dzb_ref[...] = dz.astype(BF16)
        first = i == 0
        _accumulate(dg_ref, first, jnp.sum(dx1 * xh, axis=0, keepdims=True))
        _accumulate(db_ref, first, jnp.sum(dx1, axis=0, keepdims=True))

    early = [d_w_up_t, d_w_down.reshape(NS, DFF // NS, D)]
    dz1, dz1b, d_ln1_g, d_ln1_b, *sib_e = _matmul_general(
        [(dupre_g, (tm, cs), lambda i, j, k: (i, jnp.minimum(k, half - 1))),
         (dupre_v, (tm, cs), lambda i, j, k: (i, jnp.maximum(k - half, 0))),
         (w_up_sh, (1, D, cs), lambda i, j, k: (k, 0, 0))],
        lambda refs, i, j, k: _dot(jnp.where(k < half, refs[0][...], refs[1][...]), refs[2][0], "nt"),
        grid=(T // tm, 1, NS), tm=tm, tn=D,
        extras=[(dz2, (tm, D), rowD), (xh1, (tm, D), rowD), (r1, (tm, LANES), rowD), (ln1_g, (1, D), vecD)],
        outs=[((T, D), F32, (tm, D), rowD), ((T, D), BF16, (tm, D), rowD),
              ((1, D), F32, (1, D), vecD), ((1, D), F32, (1, D), vecD)],
        epilogue=ln1_bwd_epilogue, name="mm_dx1_ln1_bwd", bg=_bg_sibling_exchange(early))
    chip_e = [_pair_sum(g, s, ids, name="pair_sum_" + n) for g, s, n in zip(early, sib_e, ("w_up", "w_down"))]

    d_w_out = _mm_plain(mixed, dz1b, mode="tn", tm=D, tn=D, tk=tk_t, out_dtype=F32, name="mm_dw_out")
    early.append(d_w_out.reshape(NS, D // NS, D))
    dmixed, sib_out = _mm_plain(dz1b, w_out, mode="nt", tm=tm, tn=D, tk=D, out_dtype=F32, name="mm_dmixed",
                                bg=_bg_sibling_exchange(early[2:]))
    sib_e.append(sib_out)
    chip_e.append(_pair_sum(early[2], sib_out, ids, name="pair_sum_w_out"))

    dattn, dd, d_attn_norm_g = _attn_pre_bwd(dmixed, attn, r_attn, attn_norm_g, tm_s)
    dag, d_conv_w, d_conv_b, d_conv_ln_g, d_conv_ln_b, d_conv_norm_g = _conv_bwd(
        ag, dmixed, conv_w, conv_b, conv_ln_g, conv_ln_b, conv_norm_g, B, S, CW, D)

    dq, dk, dv, csq, csk, csv, dbias, *got_e = _attention_bwd(qkv, dattn, lse, dd, bias_all, B, S, AW,
                                                              bg=_bg_chip_exchange(chip_e))
    full_up, full_down, full_out = [_final_sum(g, s, r, ids, name="final_sum_" + n)
                                    for g, s, r, n in zip(early, sib_e, got_e, ("w_up", "w_down", "w_out"))]
    d_rel_table = _rel_grad(dbias.reshape(3, H, ATTN_BLOCK * 2 * ATTN_BLOCK), bucket).T
    dh, cs_ag = _dh_cat(dq, dk, dv, dag, tm_s)
    d_b_in = jnp.concatenate([csq, csk, csv, cs_ag], axis=1)

    d_w_in_t = _mm_plain(dh, xf, mode="tn", tm=_col_tile(INW, 1408), tn=D, tk=tk_t, out_dtype=F32, name="mm_dw_in")
    late = [d_w_in_t.reshape(NS, INW // NS, D)]
    sib_l = _sibling_exchange(late)
    chip_l = [_pair_sum(late[0], sib_l[0], ids, name="pair_sum_w_in")]
    small = dict(rel_table=d_rel_table, b_in=d_b_in, conv_w=d_conv_w, conv_b=d_conv_b, conv_ln_g=d_conv_ln_g,
                 conv_ln_b=d_conv_ln_b, attn_norm_g=d_attn_norm_g, conv_norm_g=d_conv_norm_g, ln1_g=d_ln1_g,
                 ln1_b=d_ln1_b, ffn_conv_w=d_ffn_cw, ffn_conv_b=d_ffn_cb, ln2_g=d_ln2_g, ln2_b=d_ln2_b)
    pack = _pack([loss_part] + [small[n] for n in SMALL_NAMES])

    def gx_epilogue(acc, i, j, extra_refs, out_refs):
        out_refs[0][...] = acc + ALPHA * extra_refs[0][...]

    grad_x, got_in, all_packs = _matmul(
        dh, w_in, mode="nt", tm=tm_s, tn=D, tk=INW, extras=[(dz1, (tm_s, D), rowD)],
        outs=[((T, D), F32, (tm_s, D), rowD)], epilogue=gx_epilogue, name="mm_grad_x",
        bg=_bg_chip_exchange(chip_l, pack))
    full_in = _final_sum(late[0], sib_l[0], got_in, ids, name="final_sum_w_in")
    return grad_x.reshape(B, S, D), [full_in, full_out, full_up, full_down], all_packs


def _place():
    return lax.axis_index("x"), lax.axis_index("y"), lax.axis_index("c")


CHIP_FLIPS = ((1, 0), (0, 1), (1, 1))


def _flip(v, f):
    return 1 - v if f else v


HBM_SPEC = pl.BlockSpec(memory_space=pl.ANY)
VMEM_SPEC = pl.BlockSpec(memory_space=pltpu.VMEM)
COMM_PARAMS = pltpu.CompilerParams(vmem_limit_bytes=VMEM_LIMIT)


def _gather_weights(big, small):
    nb, ns = len(big), len(small)

    def body(*refs):
        big_in = refs[:nb]
        small_in = refs[nb:nb + ns]
        big_out = refs[nb + ns:2 * nb + ns]
        small_out = refs[2 * nb + ns:2 * nb + 2 * ns]
        stages = refs[2 * nb + 2 * ns:3 * nb + 2 * ns]
        send_sems, recv_sems, local_sems = refs[3 * nb + 2 * ns:]
        x, y, c = _place()
        s_me = 2 * x + y
        sibling = (x, y, 1 - c)
        started, local_copies = [], []
        for a in range(nb):
            rh = big[a].shape[0] // 2
            lo = pl.multiple_of(c * rh, 16)
            stages[a][...] = big_in[a][pl.ds(lo, rh), :].astype(BF16)
            mine = big_out[a].at[s_me, pl.ds(lo, rh), :]
            loc = pltpu.make_async_copy(stages[a], mine, local_sems.at[a])
            loc.start()
            local_copies.append(loc)
            targets = [sibling] + [(_flip(x, fx), _flip(y, fy), c) for fx, fy in CHIP_FLIPS]
            for k, to in enumerate(targets):
                cp = pltpu.make_async_remote_copy(stages[a], mine, send_sems.at[a * 7 + k],
                                                  recv_sems.at[a * 7 + k], device_id=to, device_id_type=MESH)
                cp.start()
                started.append(cp)
        for a in range(ns):
            mine = small_out[a].at[s_me]
            loc = pltpu.make_async_copy(small_in[a], mine, local_sems.at[nb + a])
            loc.start()
            local_copies.append(loc)
            for k, (fx, fy) in enumerate(CHIP_FLIPS):
                cp = pltpu.make_async_remote_copy(small_in[a], mine, send_sems.at[nb * 7 + a * 3 + k],
                                                  recv_sems.at[nb * 7 + a * 3 + k],
                                                  device_id=(_flip(x, fx), _flip(y, fy), c), device_id_type=MESH)
                cp.start()
                started.append(cp)
        for a in range(nb):
            rh = big[a].shape[0] // 2
            lo = pl.multiple_of(c * rh, 16)
            for k, (fx, fy) in enumerate(CHIP_FLIPS):
                s_from = 2 * _flip(x, fx) + _flip(y, fy)
                got = big_out[a].at[s_from, pl.ds(lo, rh), :]
                pltpu.make_async_remote_copy(got, got, send_sems.at[a * 7 + 1 + k], recv_sems.at[a * 7 + 1 + k],
                                             device_id=sibling, device_id_type=MESH).wait_recv()
                fwd = pltpu.make_async_remote_copy(got, got, send_sems.at[a * 7 + 4 + k],
                                                   recv_sems.at[a * 7 + 4 + k], device_id=sibling,
                                                   device_id_type=MESH)
                fwd.start()
                started.append(fwd)
        for a in range(nb):
            rh = big[a].shape[0] // 2
            lo_sib = pl.multiple_of((1 - c) * rh, 16)
            for k in (0, 4, 5, 6):
                any_rows = big_out[a].at[s_me, pl.ds(lo_sib, rh), :]
                pltpu.make_async_remote_copy(any_rows, any_rows, send_sems.at[a * 7 + k], recv_sems.at[a * 7 + k],
                                             device_id=sibling, device_id_type=MESH).wait_recv()
        for a in range(ns):
            for k in range(3):
                pltpu.make_async_remote_copy(small_in[a], small_out[a].at[s_me], send_sems.at[nb * 7 + a * 3 + k],
                                             recv_sems.at[nb * 7 + a * 3 + k], device_id=sibling,
                                             device_id_type=MESH).wait_recv()
        for cp in started:
            cp.wait_send()
        for cp in local_copies:
            cp.wait()

    n_sem = nb * 7 + ns * 3
    out_shape = ([jax.ShapeDtypeStruct((N_SHARDS,) + w.shape, BF16) for w in big]
                 + [jax.ShapeDtypeStruct((N_SHARDS,) + w.shape, F32) for w in small])
    res = pl.pallas_call(
        body, in_specs=[VMEM_SPEC] * nb + [HBM_SPEC] * ns, out_specs=[HBM_SPEC] * (nb + ns),
        out_shape=out_shape,
        scratch_shapes=[pltpu.VMEM((w.shape[0] // 2, w.shape[1]), BF16) for w in big]
        + [pltpu.SemaphoreType.DMA((n_sem,)), pltpu.SemaphoreType.DMA((n_sem,)),
           pltpu.SemaphoreType.DMA((nb + ns,))],
        compiler_params=COMM_PARAMS, name="gather_weights",
    )(*big, *small)
    return res[:nb], res[nb:]


def _sibling_exchange(grads):
    n = len(grads)

    def body(*refs):
        g_in = refs[:n]
        got = refs[n:2 * n]
        send_sems, recv_sems = refs[2 * n:]
        x, y, c = _place()
        cps = []
        for a in range(n):
            rh = grads[a].shape[1] // 2
            lo = pl.multiple_of((1 - c) * rh, 8)
            cp = pltpu.make_async_remote_copy(g_in[a].at[:, pl.ds(lo, rh), :], got[a], send_sems.at[a],
                                              recv_sems.at[a], device_id=(x, y, 1 - c), device_id_type=MESH)
            cp.start()
            cps.append(cp)
        for cp in cps:
            cp.wait()

    return pl.pallas_call(
        body, in_specs=[HBM_SPEC] * n, out_specs=[HBM_SPEC] * n,
        out_shape=[jax.ShapeDtypeStruct((N_SHARDS, g.shape[1] // 2, g.shape[2]), F32) for g in grads],
        scratch_shapes=[pltpu.SemaphoreType.DMA((n,)), pltpu.SemaphoreType.DMA((n,))],
        compiler_params=COMM_PARAMS, name="sibling_exchange",
    )(*grads)


def _chip_exchange(chip_parts, pack):
    n = len(chip_parts)

    def body(*refs):
        parts = refs[:n]
        pack_ref = refs[n]
        got = refs[n + 1:2 * n + 1]
        all_packs = refs[2 * n + 1]
        send_sems, recv_sems, local_sem = refs[2 * n + 2:]
        x, y, c = _place()
        me = 4 * x + 2 * y + c
        cps = []
        for a in range(n):
            for k, (fx, fy) in enumerate(CHIP_FLIPS):
                px, py = _flip(x, fx), _flip(y, fy)
                cp = pltpu.make_async_remote_copy(parts[a].at[2 * px + py], got[a].at[k], send_sems.at[a * 3 + k],
                                                  recv_sems.at[a * 3 + k], device_id=(px, py, c),
                                                  device_id_type=MESH)
                cp.start()
                cps.append(cp)
        loc = pltpu.make_async_copy(pack_ref, all_packs.at[me], local_sem)
        loc.start()
        for m in range(1, N_DEV):
            to = (_flip(x, m & 4), _flip(y, m & 2), _flip(c, m & 1))
            cp = pltpu.make_async_remote_copy(pack_ref, all_packs.at[me], send_sems.at[n * 3 + m - 1],
                                              recv_sems.at[n * 3 + m - 1], device_id=to, device_id_type=MESH)
            cp.start()
            cps.append(cp)
        for cp in cps:
            cp.wait()
        loc.wait()

    rs = pack.shape[0]
    res = pl.pallas_call(
        body, in_specs=[HBM_SPEC] * (n + 1), out_specs=[HBM_SPEC] * (n + 1),
        out_shape=[jax.ShapeDtypeStruct((3,) + p.shape[1:], BF16) for p in chip_parts]
        + [jax.ShapeDtypeStruct((N_DEV, rs, LANES), F32)],
        scratch_shapes=[pltpu.SemaphoreType.DMA((n * 3 + N_DEV - 1,)), pltpu.SemaphoreType.DMA((n * 3 + N_DEV - 1,)),
                        pltpu.SemaphoreType.DMA],
        compiler_params=COMM_PARAMS, name="chip_exchange",
    )(*chip_parts, pack)
    return res[:n], res[n]


def _sibling_assemble(fulls):
    n = len(fulls)

    def body(*refs):
        full = refs[n:2 * n]
        send_sems, recv_sems = refs[2 * n:]
        x, y, c = _place()
        cps = []
        for a in range(n):
            rh = fulls[a].shape[0] // 2
            mine = full[a].at[pl.ds(pl.multiple_of(c * rh, 8), rh), :]
            cp = pltpu.make_async_remote_copy(mine, mine, send_sems.at[a], recv_sems.at[a],
                                              device_id=(x, y, 1 - c), device_id_type=MESH)
            cp.start()
            cps.append(cp)
        for cp in cps:
            cp.wait()

    return pl.pallas_call(
        body, in_specs=[HBM_SPEC] * n, out_specs=[HBM_SPEC] * n,
        out_shape=[jax.ShapeDtypeStruct(f.shape, F32) for f in fulls],
        input_output_aliases={a: a for a in range(n)},
        scratch_shapes=[pltpu.SemaphoreType.DMA((n,)), pltpu.SemaphoreType.DMA((n,))],
        compiler_params=COMM_PARAMS, name="sibling_assemble",
    )(*fulls)


def _remote(ref_src, ref_dst, send_sems, recv_sems, k, to):
    return pltpu.make_async_remote_copy(ref_src, ref_dst, send_sems.at[k], recv_sems.at[k], device_id=to,
                                        device_id_type=MESH)


def _stage_half(w, ids, name):
    R, C = w.shape
    rh = R // 2
    rt = _half_tile(rh)
    nt = rh // rt

    def body(ids_ref, w_ref, o_ref):
        o_ref[0] = w_ref[...].astype(BF16)

    grid_spec = pltpu.PrefetchScalarGridSpec(
        num_scalar_prefetch=1, grid=(nt,),
        in_specs=[pl.BlockSpec((rt, C), lambda i, ids: (ids[2] * nt + i, 0))],
        out_specs=pl.BlockSpec((1, rt, C), lambda i, ids: (2 * ids[0] + ids[1], ids[2] * nt + i, 0)))
    return pl.pallas_call(body, grid_spec=grid_spec, out_shape=jax.ShapeDtypeStruct((N_SHARDS, R, C), BF16),
                          compiler_params=_params(1), name=name)(ids, w)


def _bg_gather(staged):
    n = len(staged)

    def run(step, n_steps, ins, outs, send_sems, recv_sems, local_sems, post):
        x, y, c = _place()
        s_me = 2 * x + y
        sibling = (x, y, 1 - c)
        chips = [(_flip(x, fx), _flip(y, fy)) for fx, fy in CHIP_FLIPS]

        def rows(a, s, half):
            rh = staged[a].shape[1] // 2
            return outs[a].at[s, pl.ds(pl.multiple_of(half * rh, 16), rh), :]

        def copy(a, k, ref, to):
            return _remote(ref, ref, send_sems, recv_sems, a * 7 + k, to)

        if not post:
            @pl.when(step == 0)
            def _():
                for a in range(n):
                    mine = rows(a, s_me, c)
                    copy(a, 0, mine, sibling).start()
                    for k, (px, py) in enumerate(chips):
                        copy(a, 1 + k, mine, (px, py, c)).start()

            @pl.when(step == max(n_steps - 2, 0))
            def _():
                for a in range(n):
                    for k, (px, py) in enumerate(chips):
                        got = rows(a, 2 * px + py, c)
                        copy(a, 1 + k, got, sibling).wait_recv()
                        copy(a, 4 + k, got, sibling).start()
        else:
            @pl.when(step == n_steps - 1)
            def _():
                for a in range(n):
                    for k in (0, 4, 5, 6):
                        copy(a, k, rows(a, s_me, 1 - c), sibling).wait_recv()
                    for k in range(7):
                        copy(a, k, rows(a, s_me, c), sibling).wait_send()

    return _Background(staged, [jax.ShapeDtypeStruct(g.shape, g.dtype) for g in staged],
                       {a: a for a in range(n)}, 7 * n, run)


def _bg_sibling_exchange(grads):
    n = len(grads)

    def run(step, n_steps, ins, outs, send_sems, recv_sems, local_sems, post):
        x, y, c = _place()

        def copy(a):
            rh = grads[a].shape[1] // 2
            lo = pl.multiple_of((1 - c) * rh, 8)
            return _remote(ins[a].at[:, pl.ds(lo, rh), :], outs[a], send_sems, recv_sems, a, (x, y, 1 - c))

        if not post:
            @pl.when(step == 0)
            def _():
                for a in range(n):
                    copy(a).start()
        else:
            @pl.when(step == n_steps - 1)
            def _():
                for a in range(n):
                    copy(a).wait()

    return _Background(grads, [jax.ShapeDtypeStruct((N_SHARDS, g.shape[1] // 2, g.shape[2]), F32) for g in grads],
                       {}, n, run)


def _bg_chip_exchange(chip_parts, pack=None):
    n = len(chip_parts)

    def run(step, n_steps, ins, outs, send_sems, recv_sems, local_sems, post):
        x, y, c = _place()
        me = 4 * x + 2 * y + c

        def copies():
            cps = []
            for a in range(n):
                for k, (fx, fy) in enumerate(CHIP_FLIPS):
                    px, py = _flip(x, fx), _flip(y, fy)
                    cps.append(_remote(ins[a].at[2 * px + py], outs[a].at[k], send_sems, recv_sems, a * 3 + k,
                                       (px, py, c)))
            if pack is not None:
                for m in range(1, N_DEV):
                    to = (_flip(x, m & 4), _flip(y, m & 2), _flip(c, m & 1))
                    cps.append(_remote(ins[n], outs[n].at[me], send_sems, recv_sems, n * 3 + m - 1, to))
            return cps

        def local():
            return pltpu.make_async_copy(ins[n], outs[n].at[me], local_sems.at[0])

        if not post:
            @pl.when(step == 0)
            def _():
                for cp in copies():
                    cp.start()
                if pack is not None:
                    local().start()
        else:
            @pl.when(step == n_steps - 1)
            def _():
                for cp in copies():
                    cp.wait()
                if pack is not None:
                    local().wait()

    in_arrays = list(chip_parts) + ([pack] if pack is not None else [])
    out_shapes = [jax.ShapeDtypeStruct((3,) + p.shape[1:], BF16) for p in chip_parts]
    if pack is not None:
        out_shapes.append(jax.ShapeDtypeStruct((N_DEV, pack.shape[0], LANES), F32))
    return _Background(in_arrays, out_shapes, {}, n * 3 + N_DEV - 1, run)


def _half_tile(rh, mult=16, want=256):
    best = None
    for t in range(mult, min(rh, want) + 1, mult):
        if rh % t == 0:
            best = t
    return best if best is not None else rh


def _pair_sum(g, sib, ids, name):
    _, R, C = g.shape
    rh = R // 2
    rt = _half_tile(rh)
    nt = rh // rt

    def body(ids_ref, g_ref, s_ref, o_ref):
        o_ref[...] = (g_ref[...] + s_ref[...]).astype(BF16)

    grid_spec = pltpu.PrefetchScalarGridSpec(
        num_scalar_prefetch=1, grid=(N_SHARDS, nt),
        in_specs=[pl.BlockSpec((1, rt, C), lambda s, i, ids: (s, ids[2] * nt + i, 0)),
                  pl.BlockSpec((1, rt, C), lambda s, i, ids: (s, i, 0))],
        out_specs=pl.BlockSpec((1, rt, C), lambda s, i, ids: (s, i, 0)))
    return pl.pallas_call(body, grid_spec=grid_spec, out_shape=jax.ShapeDtypeStruct((N_SHARDS, rh, C), BF16),
                          compiler_params=_params(2), name=name)(ids, g, sib)


def _final_sum(g, sib, got, ids, name):
    _, R, C = g.shape
    rh = R // 2
    rt = _half_tile(rh)
    nt = rh // rt

    def body(ids_ref, g_ref, s_ref, r_ref, o_ref):
        tot = g_ref[0] + s_ref[0]
        for k in range(3):
            tot = tot + r_ref[k].astype(F32)
        o_ref[...] = tot

    grid_spec = pltpu.PrefetchScalarGridSpec(
        num_scalar_prefetch=1, grid=(nt,),
        in_specs=[pl.BlockSpec((1, rt, C), lambda i, ids: (2 * ids[0] + ids[1], ids[2] * nt + i, 0)),
                  pl.BlockSpec((1, rt, C), lambda i, ids: (2 * ids[0] + ids[1], i, 0)),
                  pl.BlockSpec((3, rt, C), lambda i, ids: (0, i, 0))],
        out_specs=pl.BlockSpec((rt, C), lambda i, ids: (ids[2] * nt + i, 0)))
    return pl.pallas_call(body, grid_spec=grid_spec, out_shape=jax.ShapeDtypeStruct((R, C), F32),
                          compiler_params=_params(1), name=name)(ids, g, sib, got)


def _sum_packs(all_packs):
    def body(p_ref, o_ref):
        tot = p_ref[0]
        for i in range(1, N_DEV):
            tot = tot + p_ref[i]
        o_ref[...] = tot

    return pl.pallas_call(body, in_specs=[VMEM_SPEC], out_specs=VMEM_SPEC,
                          out_shape=jax.ShapeDtypeStruct(all_packs.shape[1:], F32), name="sum_packs")(all_packs)


def _adamw(w, g, m, v, name):
    R, C = w.shape
    rt = _half_tile(R, mult=8, want=256)

    def body(w_ref, g_ref, m_ref, v_ref, d_ref, nm_ref, nv_ref):
        gg = g_ref[...]
        nm = ADAM_B1 * m_ref[...] + (1.0 - ADAM_B1) * gg
        nv = ADAM_B2 * v_ref[...] + (1.0 - ADAM_B2) * (gg * gg)
        m_hat = nm / (1.0 - ADAM_B1 ** ADAM_STEP)
        v_hat = nv / (1.0 - ADAM_B2 ** ADAM_STEP)
        d_ref[...] = -ADAM_LR * (m_hat / (jnp.sqrt(v_hat) + ADAM_EPS) + ADAM_WD * w_ref[...])
        nm_ref[...] = nm
        nv_ref[...] = nv

    spec = pl.BlockSpec((rt, C), lambda i: (i, 0))
    return pl.pallas_call(body, grid=(R // rt,), in_specs=[spec] * 4, out_specs=[spec] * 3,
                          out_shape=[jax.ShapeDtypeStruct((R, C), F32)] * 3,
                          compiler_params=_params(1), name=name)(w, g, m, v)


def _adamw_update(w, g, m, v):
    nm = ADAM_B1 * m + (1.0 - ADAM_B1) * g
    nv = ADAM_B2 * v + (1.0 - ADAM_B2) * (g * g)
    m_hat = nm / (1.0 - ADAM_B1 ** ADAM_STEP)
    v_hat = nv / (1.0 - ADAM_B2 ** ADAM_STEP)
    return -ADAM_LR * (m_hat / (jnp.sqrt(v_hat) + ADAM_EPS) + ADAM_WD * w), nm, nv


def _adamw_many(ws, gs, ms, vs, name):
    n = len(ws)

    def body(*refs):
        for i in range(n):
            d, nm, nv = _adamw_update(refs[i][...], refs[n + i][...], refs[2 * n + i][...], refs[3 * n + i][...])
            refs[4 * n + i][...] = d
            refs[5 * n + i][...] = nm
            refs[6 * n + i][...] = nv

    return pl.pallas_call(body, in_specs=[VMEM_SPEC] * (4 * n), out_specs=[VMEM_SPEC] * (3 * n),
                          out_shape=[jax.ShapeDtypeStruct(w.shape, F32) for w in ws] * 3, name=name,
                          )(*ws, *gs, *ms, *vs)


def _pack(pieces):
    rows = []
    for p in pieces:
        flat = p.reshape(-1)
        pad = (-flat.shape[0]) % LANES
        if pad:
            flat = jnp.concatenate([flat, jnp.zeros((pad,), F32)])
        rows.append(flat.reshape(-1, LANES))
    total = sum(r.shape[0] for r in rows)
    pad_rows = (-total) % 8
    if pad_rows:
        rows.append(jnp.zeros((pad_rows, LANES), F32))
    return jnp.concatenate(rows, axis=0)


def _unpack(buf, shapes):
    out, r0 = [], 0
    for shp in shapes:
        n = int(np.prod(shp))
        nr = -(-n // LANES)
        out.append(buf[r0:r0 + nr].reshape(-1)[:n].reshape(shp))
        r0 += nr
    return out


SMALL_NAMES = ("rel_table", "b_in", "conv_w", "conv_b", "conv_ln_g", "conv_ln_b", "attn_norm_g", "conv_norm_g",
               "ln1_g", "ln1_b", "ffn_conv_w", "ffn_conv_b", "ln2_g", "ln2_b")
BIG_NAMES = ("w_in", "w_out", "w_up", "w_down")
WEIGHT_ORDER = ("rel_table", "w_in", "b_in", "conv_w", "conv_b", "conv_ln_g", "conv_ln_b", "attn_norm_g",
                "conv_norm_g", "w_out", "ln1_g", "ln1_b", "w_up", "ffn_conv_w", "ffn_conv_b", "w_down",
                "ln2_g", "ln2_b")


def kernel(x, rel_table, w_in, b_in, conv_w, conv_b, conv_ln_g, conv_ln_b, attn_norm_g, conv_norm_g, w_out, ln1_g, ln1_b, w_up, ffn_conv_w, ffn_conv_b, w_down, ln2_g, ln2_b, loss_target, m_rel_table, m_w_in, m_b_in, m_conv_w, m_conv_b, m_conv_ln_g, m_conv_ln_b, m_attn_norm_g, m_conv_norm_g, m_w_out, m_ln1_g, m_ln1_b, m_w_up, m_ffn_conv_w, m_ffn_conv_b, m_w_down, m_ln2_g, m_ln2_b, v_rel_table, v_w_in, v_b_in, v_conv_w, v_conv_b, v_conv_ln_g, v_conv_ln_b, v_attn_norm_g, v_conv_norm_g, v_w_out, v_ln1_g, v_ln1_b, v_w_up, v_ffn_conv_w, v_ffn_conv_b, v_w_down, v_ln2_g, v_ln2_b):
    args = dict(locals())
    weights = {n: args[n] for n in WEIGHT_ORDER}
    moms = {n: args["m_" + n] for n in WEIGHT_ORDER}
    vels = {n: args["v_" + n] for n in WEIGHT_ORDER}
    xi, yi, ci = _place()
    ids = jnp.stack([xi, yi, ci]).astype(jnp.int32)
    shard = 2 * xi + yi
    D = x.shape[-1]
    DFF = w_down.shape[1] * N_SHARDS
    CW = conv_norm_g.shape[-1]

    (g_in,), (g_cw, g_fcw) = _gather_weights([w_in[0]], [conv_w[0], ffn_conv_w[0]])
    cols = lambda t: jnp.transpose(t, (1, 0, 2)).reshape(t.shape[1], N_SHARDS * t.shape[2])
    staged = [_stage_half(w[0], ids, name="stage_" + n) for w, n in ((w_out, "w_out"), (w_up, "w_up"),
                                                                     (w_down, "w_down"))]

    grad_x, fulls, all_packs = _local_step(
        x, loss_target, rel_table, cols(g_in), b_in, cols(g_cw), conv_b, conv_ln_g, conv_ln_b, attn_norm_g,
        conv_norm_g, staged, ln1_g, ln1_b, cols(g_fcw), ffn_conv_b, ln2_g, ln2_b, ids)
    big_grads = dict(zip(BIG_NAMES, _sibling_assemble(fulls)))
    for n in ("w_in", "w_up"):
        big_grads[n] = _transpose(big_grads[n], name="transpose_d" + n)

    summed = _sum_packs(all_packs)
    full_shapes = {n: weights[n].shape for n in SMALL_NAMES}
    full_shapes["conv_w"] = (1, CONV_KERNEL, CW)
    full_shapes["ffn_conv_w"] = (1, FFN_CONV_KERNEL, 2 * DFF)
    un = _unpack(summed, [(1, LANES)] + [full_shapes[n] for n in SMALL_NAMES])
    loss = un[0][0, 0]
    small_grads = dict(zip(SMALL_NAMES, un[1:]))
    for n in ("conv_w", "ffn_conv_w"):
        width = weights[n].shape[-1]
        small_grads[n] = lax.dynamic_slice_in_dim(small_grads[n], shard * width, width, axis=2)

    grads, delta, new_m, new_v = {}, {}, {}, {}
    for n in BIG_NAMES:
        shp = weights[n].shape
        g2 = big_grads[n]
        d, nm, nv = _adamw(weights[n][0], g2, moms[n][0], vels[n][0], name="adamw_" + n)
        grads[n], delta[n], new_m[n], new_v[n] = (t.reshape(shp) for t in (g2, d, nm, nv))
    pick = lambda src: [src[n] for n in SMALL_NAMES]
    small_out = _adamw_many(pick(weights), pick(small_grads), pick(moms), pick(vels), name="adamw_small")
    ns = len(SMALL_NAMES)
    for tgt, part in ((delta, small_out[:ns]), (new_m, small_out[ns:2 * ns]), (new_v, small_out[2 * ns:])):
        tgt.update(zip(SMALL_NAMES, part))
    grads.update(small_grads)

    return (loss, grad_x, *[grads[n] for n in WEIGHT_ORDER], *[delta[n] for n in WEIGHT_ORDER],
            *[new_m[n] for n in WEIGHT_ORDER], *[new_v[n] for n in WEIGHT_ORDER])
```

```python
import functools
import math

import numpy as np
import jax
import jax.numpy as jnp
from jax import lax
from jax.experimental import pallas as pl
from jax.experimental.pallas import tpu as pltpu

F32 = jnp.float32
BF16 = jnp.bfloat16
MESH = pl.DeviceIdType.MESH

HEAD_DIM = 64
LANES = 128
ATTN_BLOCK = 128
DILATED_CONFIGS = ((128, 1), (512, 4), (2048, 16))
CONV_KERNEL = 31
FFN_CONV_KERNEL = 3
REL_BUCKETS = 32
REL_MAX_DIST = 2048
DEPTH = 1
ALPHA = (2 * DEPTH) ** 0.25
LN_EPS = 1e-5
NEG_INF = -1e30
QK_SCALE = 1.0 / math.sqrt(HEAD_DIM)
ADAM_LR = 0.001
ADAM_B1 = 0.9
ADAM_B2 = 0.999
ADAM_EPS = 1e-08
ADAM_WD = 0.01
ADAM_STEP = 10
VMEM_LIMIT = 52 * 1024 * 1024
FFN_COLS = 128
N_SHARDS = 4
N_DEV = 8


def _params(n_axes):
    return pltpu.CompilerParams(dimension_semantics=("arbitrary",) * n_axes,
                                vmem_limit_bytes=VMEM_LIMIT)


MM_DIMS = {"nn": (((1,), (0,)), ((), ())), "nt": (((1,), (1,)), ((), ())), "tn": (((0,), (0,)), ((), ()))}


class _Background:
    def __init__(self, in_arrays, out_shapes, aliases, n_sems, run, n_local=1):
        self.in_arrays, self.out_shapes, self.aliases = list(in_arrays), list(out_shapes), dict(aliases)
        self.n_sems, self.n_local, self.run = n_sems, n_local, run

    def scratch(self):
        return [pltpu.SemaphoreType.DMA((self.n_sems,)), pltpu.SemaphoreType.DMA((self.n_sems,)),
                pltpu.SemaphoreType.DMA((self.n_local,))]


def _hosted_call(body, bg, *, grid, in_specs, out_specs, out_shape, scratch_shapes, operands, name):
    n_in, n_out, n_scr = len(in_specs), len(out_specs), len(scratch_shapes)
    if bg is None:
        return pl.pallas_call(lambda *refs: body(refs, lambda post: None), grid=grid, in_specs=in_specs,
                              out_specs=out_specs, out_shape=out_shape, scratch_shapes=scratch_shapes,
                              compiler_params=_params(len(grid)), name=name)(*operands)
    nb_in, nb_out = len(bg.in_arrays), len(bg.out_shapes)
    n_steps = int(np.prod(grid))

    def full_body(*refs):
        own = refs[:n_in] + refs[n_in + nb_in:n_in + nb_in + n_out] \
            + refs[n_in + nb_in + n_out + nb_out:n_in + nb_in + n_out + nb_out + n_scr]
        bg_in = refs[n_in:n_in + nb_in]
        bg_out = refs[n_in + nb_in + n_out:n_in + nb_in + n_out + nb_out]
        sems = refs[n_in + nb_in + n_out + nb_out + n_scr:]
        step = pl.program_id(0)
        for ax in range(1, len(grid)):
            step = step * grid[ax] + pl.program_id(ax)

        def hook(post):
            bg.run(step, n_steps, bg_in, bg_out, *sems, post)

        body(own, hook)

    res = pl.pallas_call(
        full_body, grid=grid, in_specs=list(in_specs) + [HBM_SPEC] * nb_in,
        out_specs=list(out_specs) + [HBM_SPEC] * nb_out, out_shape=list(out_shape) + bg.out_shapes,
        input_output_aliases={n_in + a: n_out + o for a, o in bg.aliases.items()},
        scratch_shapes=list(scratch_shapes) + bg.scratch(), compiler_params=_params(len(grid)), name=name,
    )(*operands, *bg.in_arrays)
    return res


def _matmul_general(ins, part_fn, *, grid, tm, tn, outs, epilogue, extras=(), name, bg=None):
    nk = grid[2]
    n_in, n_extra = len(ins), len(extras)

    def body(refs, bg_hook):
        in_refs = refs[:n_in]
        rest = refs[n_in:]
        extra_refs = rest[:n_extra]
        out_refs = rest[n_extra:n_extra + len(outs)]
        acc_ref = rest[-1]
        i, j, k = pl.program_id(0), pl.program_id(1), pl.program_id(2)
        bg_hook(False)
        part = part_fn(in_refs, i, j, k)
        if nk == 1:
            epilogue(part, i, j, extra_refs, out_refs)
        else:
            @pl.when(k == 0)
            def _():
                acc_ref[...] = part

            @pl.when(k > 0)
            def _():
                acc_ref[...] += part

            @pl.when(k == nk - 1)
            def _():
                epilogue(acc_ref[...], i, j, extra_refs, out_refs)
        bg_hook(True)

    in_specs = [pl.BlockSpec(bs, im) for (_, bs, im) in list(ins) + list(extras)]
    out_specs = [pl.BlockSpec(bs, im) for (_, _, bs, im) in outs]
    out_shape = [jax.ShapeDtypeStruct(s, d) for (s, d, _, _) in outs]
    return _hosted_call(body, bg, grid=grid, in_specs=in_specs, out_specs=out_specs, out_shape=out_shape,
                        scratch_shapes=[pltpu.VMEM((tm, tn), F32)],
                        operands=[e[0] for e in ins] + [e[0] for e in extras], name=name)


def _dot(a, b, mode):
    return lax.dot_general(a.astype(BF16), b.astype(BF16), MM_DIMS[mode], preferred_element_type=F32)


def _matmul(a, b, *, mode, tm, tn, tk, outs, epilogue, extras=(), name, bg=None):
    if mode == "tn":
        K, M = a.shape
        N = b.shape[1]
        ins = [(a, (tk, tm), lambda i, j, k: (k, i)), (b, (tk, tn), lambda i, j, k: (k, j))]
    elif mode == "nt":
        M, K = a.shape
        N = b.shape[0]
        ins = [(a, (tm, tk), lambda i, j, k: (i, k)), (b, (tn, tk), lambda i, j, k: (j, k))]
    else:
        M, K = a.shape
        N = b.shape[1]
        ins = [(a, (tm, tk), lambda i, j, k: (i, k)), (b, (tk, tn), lambda i, j, k: (k, j))]
    assert M % tm == 0 and N % tn == 0 and K % tk == 0, (name, M, N, K, tm, tn, tk)

    def part_fn(in_refs, i, j, k):
        return _dot(in_refs[0][...], in_refs[1][...], mode)

    return _matmul_general(ins, part_fn, grid=(M // tm, N // tn, K // tk), tm=tm, tn=tn, outs=outs,
                           epilogue=epilogue, extras=extras, name=name, bg=bg)


def _plain_out(M, N, tm, tn, dtype):
    return ((M, N), dtype, (tm, tn), lambda i, j, k: (i, j))


def _mm_plain(a, b, *, mode, tm, tn, tk, out_dtype, name, bias=None, bg=None):
    if mode == "tn":
        M, N = a.shape[1], b.shape[1]
    elif mode == "nt":
        M, N = a.shape[0], b.shape[0]
    else:
        M, N = a.shape[0], b.shape[1]
    extras = []
    if bias is not None:
        extras.append((bias, (1, tn), lambda i, j, k: (0, j)))

    def epilogue(acc, i, j, extra_refs, out_refs):
        if bias is not None:
            acc = acc + extra_refs[0][...]
        out_refs[0][...] = acc.astype(out_dtype)

    res = _matmul(a, b, mode=mode, tm=tm, tn=tn, tk=tk, outs=[_plain_out(M, N, tm, tn, out_dtype)],
                  epilogue=epilogue, extras=extras, name=name, bg=bg)
    return res[0] if bg is None else res


def _row_tile(T, want):
    t = min(T, want)
    while T % t:
        t //= 2
    return t


def _col_tile(N, want):
    if N <= want:
        return N
    best = None
    for c in range(LANES, want + 1, LANES):
        if N % c == 0:
            best = c
    return best if best is not None else N


def _accumulate(ref, first, val):
    @pl.when(first)
    def _():
        ref[...] = val

    @pl.when(jnp.logical_not(first))
    def _():
        ref[...] += val


def _ln_fwd(z, g, b):
    mu = jnp.mean(z, axis=-1, keepdims=True)
    zc = z - mu
    var = jnp.mean(zc * zc, axis=-1, keepdims=True)
    r = lax.rsqrt(var + LN_EPS)
    xh = zc * r
    return xh * g + b, xh, r


def _ln_bwd(dy, xh, r, g):
    dxh = dy * g
    m1 = jnp.mean(dxh, axis=-1, keepdims=True)
    m2 = jnp.mean(dxh * xh, axis=-1, keepdims=True)
    return r * (dxh - m1 - xh * m2)


def _sigmoid(x):
    return 1.0 / (1.0 + jnp.exp(-x))


def _shift_down(x, s, row):
    if s == 0:
        return x
    rolled = pltpu.roll(x, s, 0)
    nfix = -(-s // 8) * 8
    head = jnp.where(row[:nfix] >= s, rolled[:nfix], 0.0)
    return jnp.concatenate([head, rolled[nfix:]], axis=0)


def _shift_up(x, s, row):
    if s == 0:
        return x
    n = x.shape[0]
    rolled = pltpu.roll(x, n - s, 0)
    nfix = -(-s // 8) * 8
    tail = jnp.where(row[n - nfix:] < n - s, rolled[n - nfix:], 0.0)
    return jnp.concatenate([rolled[:n - nfix], tail], axis=0)


def _bucket_tables():
    exact = REL_BUCKETS // 2
    qi = np.arange(ATTN_BLOCK)[:, None]
    kj = np.arange(2 * ATTN_BLOCK)[None, :]
    steps = qi + ATTN_BLOCK - kj
    buckets, masks = [], []
    for window, dilation in DILATED_CONFIGS:
        max_steps = window // dilation
        band = (steps >= 0) & (steps <= max_steps)
        dist = np.maximum(steps, 0) * dilation
        d_f = np.maximum(dist, 1).astype(np.float32)
        large = exact + (np.log(d_f / np.float32(exact)) / np.float32(math.log(REL_MAX_DIST / exact))
                         * np.float32(REL_BUCKETS - exact)).astype(np.int32)
        large = np.minimum(large, REL_BUCKETS - 1)
        bucket = np.where(dist < exact, dist, large).astype(np.int32)
        buckets.append(bucket.reshape(1, -1))
        masks.append(np.where(band, 0.0, NEG_INF).astype(np.float32).reshape(1, -1))
    return np.stack(buckets), np.stack(masks)


def _split_hi_lo(x):
    hi = x.astype(BF16)
    lo = (x - hi.astype(F32)).astype(BF16)
    return hi, lo


def _bias_build(rel_table_t, bucket, mask):
    H = rel_table_t.shape[0]
    n = bucket.shape[-1]

    def body(t_ref, bkt_ref, mask_ref, o_ref):
        onehot = (lax.broadcasted_iota(jnp.int32, (REL_BUCKETS, n), 0) == bkt_ref[0]).astype(BF16)
        t = t_ref[...]
        t1 = t.astype(BF16)
        r1 = t - t1.astype(F32)
        t2 = r1.astype(BF16)
        t3 = (r1 - t2.astype(F32)).astype(BF16)
        acc = jnp.dot(t1, onehot, preferred_element_type=F32)
        acc = acc + jnp.dot(t2, onehot, preferred_element_type=F32)
        acc = acc + jnp.dot(t3, onehot, preferred_element_type=F32)
        o_ref[0] = acc + mask_ref[0]

    return pl.pallas_call(
        body, grid=(3,),
        in_specs=[pl.BlockSpec((H, REL_BUCKETS), lambda b: (0, 0)),
                  pl.BlockSpec((1, 1, n), lambda b: (b, 0, 0)),
                  pl.BlockSpec((1, 1, n), lambda b: (b, 0, 0))],
        out_specs=pl.BlockSpec((1, H, n), lambda b: (b, 0, 0)),
        out_shape=jax.ShapeDtypeStruct((3, H, n), F32),
        compiler_params=_params(1), name="bias_build",
    )(rel_table_t, bucket, mask)


def _rel_grad(dbias, bucket):
    H = dbias.shape[1]
    n = bucket.shape[-1]
    dims = (((1,), (1,)), ((), ()))

    def body(d_ref, bkt_ref, o_ref):
        b = pl.program_id(0)
        onehot = (lax.broadcasted_iota(jnp.int32, (REL_BUCKETS, n), 0) == bkt_ref[0]).astype(BF16)
        d = d_ref[0]
        d1 = d.astype(BF16)
        r1 = d - d1.astype(F32)
        d2 = r1.astype(BF16)
        d3 = (r1 - d2.astype(F32)).astype(BF16)
        acc = lax.dot_general(d1, onehot, dims, preferred_element_type=F32)
        acc = acc + lax.dot_general(d2, onehot, dims, preferred_element_type=F32)
        acc = acc + lax.dot_general(d3, onehot, dims, preferred_element_type=F32)
        _accumulate(o_ref, b == 0, acc)

    return pl.pallas_call(
        body, grid=(3,),
        in_specs=[pl.BlockSpec((1, H, n), lambda b: (b, 0, 0)),
                  pl.BlockSpec((1, 1, n), lambda b: (b, 0, 0))],
        out_specs=pl.BlockSpec((H, REL_BUCKETS), lambda b: (0, 0)),
        out_shape=jax.ShapeDtypeStruct((H, REL_BUCKETS), F32),
        compiler_params=_params(1), name="rel_grad",
    )(dbias, bucket)


def _attn_specs(B, S, AW, d):
    L = S // d
    HP = AW // LANES
    W3 = 3 * HP
    q_spec = pl.BlockSpec((1, L, LANES), lambda h, b, r: (b, 0, r * W3 + h))
    k_spec = pl.BlockSpec((1, L, LANES), lambda h, b, r: (b, 0, r * W3 + HP + h))
    v_spec = pl.BlockSpec((1, L, LANES), lambda h, b, r: (b, 0, r * W3 + 2 * HP + h))
    o_spec = pl.BlockSpec((1, L, LANES), lambda h, b, r: (b, 0, r * HP + h))
    bias_spec = pl.BlockSpec((2, ATTN_BLOCK, 2 * ATTN_BLOCK), lambda h, b, r: (h, 0, 0))
    return L, HP, q_spec, k_spec, v_spec, o_spec, bias_spec


def _attn_fwd(qkv, bias, B, S, AW, d, name):
    L, HP, q_spec, k_spec, v_spec, o_spec, bias_spec = _attn_specs(B, S, AW, d)
    nb = L // ATTN_BLOCK
    nt = (((1,), (1,)), ((), ()))

    def body(q_ref, k_ref, v_ref, b_ref, o_ref, lse_ref):
        head0 = lax.broadcasted_iota(jnp.int32, (1, LANES), 1) < HEAD_DIM

        def block(n, first):
            qs = pl.multiple_of(n * ATTN_BLOCK, ATTN_BLOCK)
            q = q_ref[0, pl.ds(qs, ATTN_BLOCK), :]
            if first:
                kk = k_ref[0, pl.ds(0, ATTN_BLOCK), :]
                vv = v_ref[0, pl.ds(0, ATTN_BLOCK), :]
            else:
                ks = pl.multiple_of(n * ATTN_BLOCK - ATTN_BLOCK, ATTN_BLOCK)
                kk = k_ref[0, pl.ds(ks, 2 * ATTN_BLOCK), :]
                vv = v_ref[0, pl.ds(ks, 2 * ATTN_BLOCK), :]
            outs, lses = [], []
            for e in range(2):
                msk = head0 if e == 0 else jnp.logical_not(head0)
                qe = jnp.where(msk, q, jnp.zeros_like(q))
                s = lax.dot_general(qe, kk, nt, preferred_element_type=F32) * QK_SCALE
                s = s + (b_ref[e, :, ATTN_BLOCK:] if first else b_ref[e])
                m = jnp.max(s, axis=-1, keepdims=True)
                p = jnp.exp(s - m)
                l = jnp.sum(p, axis=-1, keepdims=True)
                o = jnp.dot(p.astype(BF16), vv, preferred_element_type=F32)
                outs.append(o / l)
                lses.append(jnp.broadcast_to(m + jnp.log(l), (ATTN_BLOCK, LANES)))
            o_ref[0, pl.ds(qs, ATTN_BLOCK), :] = jnp.where(head0, outs[0], outs[1])
            lse_ref[0, pl.ds(qs, ATTN_BLOCK), :] = jnp.where(head0, lses[0], lses[1])

        block(0, True)
        if nb > 1:
            def loop(n, c):
                block(n, False)
                return c
            lax.fori_loop(1, nb, loop, 0)

    qv = qkv.reshape(B, L, d * 3 * AW)
    o, lse = pl.pallas_call(
        body, grid=(HP, B, d), in_specs=[q_spec, k_spec, v_spec, bias_spec],
        out_specs=[o_spec, o_spec],
        out_shape=[jax.ShapeDtypeStruct((B, L, d * AW), F32)] * 2,
        compiler_params=_params(3), name=name,
    )(qv, qv, qv, bias)
    return o.reshape(B * S, AW), lse.reshape(B * S, AW)


def _attn_bwd(qkv, do, lse, dd, bias, B, S, AW, d, name):
    L, HP, q_spec, k_spec, v_spec, o_spec, bias_spec = _attn_specs(B, S, AW, d)
    nb = L // ATTN_BLOCK
    nt = (((1,), (1,)), ((), ()))
    tn = (((0,), (0,)), ((), ()))

    def body(q_ref, k_ref, v_ref, do_ref, lse_ref, dd_ref, b_ref, dq_ref, dk_ref, dv_ref, db_ref):
        head0 = lax.broadcasted_iota(jnp.int32, (1, LANES), 1) < HEAD_DIM
        first_step = jnp.logical_and(pl.program_id(1) == 0, pl.program_id(2) == 0)

        @pl.when(first_step)
        def _():
            db_ref[...] = jnp.zeros_like(db_ref)

        dk_ref[...] = jnp.zeros_like(dk_ref)
        dv_ref[...] = jnp.zeros_like(dv_ref)

        def block(n, first):
            qs = pl.multiple_of(n * ATTN_BLOCK, ATTN_BLOCK)
            nkeys = ATTN_BLOCK if first else 2 * ATTN_BLOCK
            ks = 0 if first else pl.multiple_of(n * ATTN_BLOCK - ATTN_BLOCK, ATTN_BLOCK)
            q = q_ref[0, pl.ds(qs, ATTN_BLOCK), :]
            kk = k_ref[0, pl.ds(ks, nkeys), :]
            vv = v_ref[0, pl.ds(ks, nkeys), :]
            dout = do_ref[0, pl.ds(qs, ATTN_BLOCK), :]
            lse_b = lse_ref[0, pl.ds(qs, ATTN_BLOCK), :]
            dd_b = dd_ref[0, pl.ds(qs, ATTN_BLOCK), :]
            dq = jnp.zeros((ATTN_BLOCK, LANES), F32)
            dkk = jnp.zeros((nkeys, LANES), F32)
            dvv = jnp.zeros((nkeys, LANES), F32)
            for e in range(2):
                msk = head0 if e == 0 else jnp.logical_not(head0)
                c0 = e * HEAD_DIM
                qe = jnp.where(msk, q, jnp.zeros_like(q))
                doe = jnp.where(msk, dout, jnp.zeros_like(dout))
                kke = jnp.where(msk, kk, jnp.zeros_like(kk))
                s = lax.dot_general(qe, kk, nt, preferred_element_type=F32) * QK_SCALE
                s = s + (b_ref[e, :, ATTN_BLOCK:] if first else b_ref[e])
                p = jnp.exp(s - lse_b[:, c0:c0 + 1])
                dp = lax.dot_general(doe, vv, nt, preferred_element_type=F32)
                ds = p * (dp - dd_b[:, c0:c0 + 1])
                if first:
                    db_ref[e, :, ATTN_BLOCK:] += ds
                else:
                    db_ref[e] += ds
                dsb = (ds * QK_SCALE).astype(BF16)
                dq = dq + jnp.dot(dsb, kke, preferred_element_type=F32)
                dkk = dkk + lax.dot_general(dsb, qe, tn, preferred_element_type=F32)
                dvv = dvv + lax.dot_general(p.astype(BF16), doe, tn, preferred_element_type=F32)
            dq_ref[0, pl.ds(qs, ATTN_BLOCK), :] = dq
            dk_ref[0, pl.ds(ks, nkeys), :] += dkk
            dv_ref[0, pl.ds(ks, nkeys), :] += dvv

        block(0, True)
        if nb > 1:
            def loop(n, c):
                block(n, False)
                return c
            lax.fori_loop(1, nb, loop, 0)

    H = AW // HEAD_DIM
    qv = qkv.reshape(B, L, d * 3 * AW)
    view = lambda t: t.reshape(B, L, d * AW)
    dq, dk, dv, db = pl.pallas_call(
        body, grid=(HP, B, d),
        in_specs=[q_spec, k_spec, v_spec, o_spec, o_spec, o_spec, bias_spec],
        out_specs=[o_spec, o_spec, o_spec, bias_spec],
        out_shape=[jax.ShapeDtypeStruct((B, L, d * AW), F32)] * 3
        + [jax.ShapeDtypeStruct((H, ATTN_BLOCK, 2 * ATTN_BLOCK), F32)],
        compiler_params=_params(3), name=name,
    )(qv, qv, qv, view(do), view(lse), view(dd), bias)
    flat = lambda t: t.reshape(B * S, AW)
    return flat(dq), flat(dk), flat(dv), db


def _attn_combine(ons, lses, gain, tm):
    T, AW = ons[0].shape

    def body(o1, o2, o3, l1, l2, l3, g_ref, attn_ref, lse_ref, mix_ref, r_ref):
        la, lb, lc = l1[...], l2[...], l3[...]
        m = jnp.maximum(jnp.maximum(la, lb), lc)
        ea, eb, ec = jnp.exp(la - m), jnp.exp(lb - m), jnp.exp(lc - m)
        den = ea + eb + ec
        attn = (ea * o1[...] + eb * o2[...] + ec * o3[...]) / den
        attn_ref[...] = attn
        lse_ref[...] = m + jnp.log(den)
        r = lax.rsqrt(jnp.mean(attn * attn, axis=-1, keepdims=True) + LN_EPS)
        mix_ref[...] = (attn * r * g_ref[...]).astype(BF16)
        r_ref[...] = jnp.broadcast_to(r, (tm, LANES))

    row = pl.BlockSpec((tm, AW), lambda i: (i, 0))
    return pl.pallas_call(
        body, grid=(T // tm,),
        in_specs=[row] * 6 + [pl.BlockSpec((1, AW), lambda i: (0, 0))],
        out_specs=[row, row, row, pl.BlockSpec((tm, LANES), lambda i: (i, 0))],
        out_shape=[jax.ShapeDtypeStruct((T, AW), F32), jax.ShapeDtypeStruct((T, AW), F32),
                   jax.ShapeDtypeStruct((T, AW), BF16), jax.ShapeDtypeStruct((T, LANES), F32)],
        compiler_params=_params(1), name="attn_combine",
    )(*ons, *lses, gain)


def _to_sub(src_ref, stage_ref, dsts, S):
    stage_ref[...] = src_ref[0].astype(F32)
    for (_, d), dst in zip(DILATED_CONFIGS[1:], dsts):
        L = S // d
        for r in range(d):
            dst[r * L:(r + 1) * L, :] = stage_ref[pl.ds(r, L, stride=d), :].astype(dst.dtype)


def _branch_blocks(S, d, block):
    nb = S // d // ATTN_BLOCK
    inner_unroll = 3 if (nb - 1) % 3 == 0 else 1

    def per_residue(r, c):
        block(r * nb, True)
        if nb > 1:
            def inner(n, c2):
                block(r * nb + n, False)
                return c2
            lax.fori_loop(1, nb, inner, 0, unroll=inner_unroll)
        return c

    lax.fori_loop(0, d, per_residue, 0, unroll=4 if nb == 1 else 1)


def _attention_fwd(qkv, bias_all, B, S, AW):
    HP = AW // LANES
    nt = MM_DIMS["nt"]

    def body(q_ref, k_ref, v_ref, b_ref, o_ref, lse_ref, stage, q4, q16, k4, k16, v4, v16, o1, l1, o4, l4, o16, l16):
        head0 = lax.broadcasted_iota(jnp.int32, (1, LANES), 1) < HEAD_DIM
        _to_sub(q_ref, stage, (q4, q16), S)
        _to_sub(k_ref, stage, (k4, k16), S)
        _to_sub(v_ref, stage, (v4, v16), S)
        srcs = ((q_ref.at[0], k_ref.at[0], v_ref.at[0], o1, l1), (q4, k4, v4, o4, l4), (q16, k16, v16, o16, l16))
        for bi, (_, d) in enumerate(DILATED_CONFIGS):
            qs_ref, ks_ref, vs_ref, od_ref, ld_ref = srcs[bi]

            def block(g, first, bi=bi, qs_ref=qs_ref, ks_ref=ks_ref, vs_ref=vs_ref, od_ref=od_ref, ld_ref=ld_ref):
                qs = pl.multiple_of(g * ATTN_BLOCK, ATTN_BLOCK)
                nkeys = ATTN_BLOCK if first else 2 * ATTN_BLOCK
                ks = qs if first else pl.multiple_of(qs - ATTN_BLOCK, ATTN_BLOCK)
                q = qs_ref[pl.ds(qs, ATTN_BLOCK), :]
                kk = ks_ref[pl.ds(ks, nkeys), :]
                vv = vs_ref[pl.ds(ks, nkeys), :]
                outs, lses = [], []
                for e in range(2):
                    msk = head0 if e == 0 else jnp.logical_not(head0)
                    qe = jnp.where(msk, q * QK_SCALE, jnp.zeros_like(q))
                    s = lax.dot_general(qe, kk, nt, preferred_element_type=F32)
                    s = s + (b_ref[bi, e, :, ATTN_BLOCK:] if first else b_ref[bi, e])
                    m = jnp.max(s, axis=-1, keepdims=True)
                    p = jnp.exp(s - m)
                    l = jnp.sum(p, axis=-1, keepdims=True)
                    o = jnp.dot(p.astype(BF16), vv, preferred_element_type=F32)
                    outs.append(o / l)
                    lses.append(jnp.broadcast_to(m + jnp.log(l), (ATTN_BLOCK, LANES)))
                od_ref[pl.ds(qs, ATTN_BLOCK), :] = jnp.where(head0, outs[0], outs[1])
                ld_ref[pl.ds(qs, ATTN_BLOCK), :] = jnp.where(head0, lses[0], lses[1])

            _branch_blocks(S, d, block)

        def natural(sub_ref, d):
            L = S // d
            for r in range(d):
                stage[pl.ds(r, L, stride=d), :] = sub_ref[r * L:(r + 1) * L, :]
            return stage[...]

        la = l1[...]
        lb = natural(l4, 4)
        lc = natural(l16, 16)
        m = jnp.maximum(jnp.maximum(la, lb), lc)
        ea, eb, ec = jnp.exp(la - m), jnp.exp(lb - m), jnp.exp(lc - m)
        den = ea + eb + ec
        lse_ref[0] = m + jnp.log(den)
        acc = ea * o1[...]
        acc = acc + eb * natural(o4, 4)
        acc = acc + ec * natural(o16, 16)
        o_ref[0] = acc / den

    blk = lambda off: pl.BlockSpec((1, S, LANES), lambda b, h: (b, 0, off + h))
    qv = qkv.reshape(B, S, 3 * AW)
    sub_b = pltpu.VMEM((S, LANES), BF16)
    sub_f = pltpu.VMEM((S, LANES), F32)
    o, lse = pl.pallas_call(
        body, grid=(B, HP),
        in_specs=[blk(0), blk(HP), blk(2 * HP),
                  pl.BlockSpec((3, 2, ATTN_BLOCK, 2 * ATTN_BLOCK), lambda b, h: (0, h, 0, 0))],
        out_specs=[blk(0), blk(0)],
        out_shape=[jax.ShapeDtypeStruct((B, S, AW), F32)] * 2,
        scratch_shapes=[sub_f] + [sub_b] * 6 + [sub_f] * 6,
        compiler_params=_params(2), name="attention_fwd",
    )(qv, qv, qv, bias_all)
    return o.reshape(B * S, AW), lse.reshape(B * S, AW)


def _attention_bwd(qkv, do, lse, dd, bias_all, B, S, AW):
    HP = AW // LANES
    H = AW // HEAD_DIM
    nt, tn = MM_DIMS["nt"], MM_DIMS["tn"]

    def body(q_ref, k_ref, v_ref, do_ref, lse_ref, dd_ref, b_ref,
             dq_ref, dk_ref, dv_ref, csq_ref, csk_ref, csv_ref, db_ref,
             stage, q4, q16, k4, k16, v4, v16, g4, g16, l4, l16, d4, d16,
             aq1, ak1, av1, aq4, ak4, av4, aq16, ak16, av16):
        head0 = lax.broadcasted_iota(jnp.int32, (1, LANES), 1) < HEAD_DIM
        first_b = pl.program_id(1) == 0

        @pl.when(first_b)
        def _():
            db_ref[...] = jnp.zeros_like(db_ref)

        _to_sub(q_ref, stage, (q4, q16), S)
        _to_sub(k_ref, stage, (k4, k16), S)
        _to_sub(v_ref, stage, (v4, v16), S)
        _to_sub(do_ref, stage, (g4, g16), S)
        _to_sub(lse_ref, stage, (l4, l16), S)
        _to_sub(dd_ref, stage, (d4, d16), S)
        for acc in (ak1, av1, ak4, av4, ak16, av16):
            acc[...] = jnp.zeros_like(acc)
        srcs = ((q_ref.at[0], k_ref.at[0], v_ref.at[0], do_ref.at[0], lse_ref.at[0], dd_ref.at[0], aq1, ak1, av1),
                (q4, k4, v4, g4, l4, d4, aq4, ak4, av4), (q16, k16, v16, g16, l16, d16, aq16, ak16, av16))
        for bi, (_, d) in enumerate(DILATED_CONFIGS):
            def block(g, first, bi=bi, refs=srcs[bi]):
                qs_ref, ks_ref, vs_ref, gs_ref, ls_ref, ds_ref, aq, ak, av = refs
                qs = pl.multiple_of(g * ATTN_BLOCK, ATTN_BLOCK)
                nkeys = ATTN_BLOCK if first else 2 * ATTN_BLOCK
                ks = qs if first else pl.multiple_of(qs - ATTN_BLOCK, ATTN_BLOCK)
                q = qs_ref[pl.ds(qs, ATTN_BLOCK), :]
                kk = ks_ref[pl.ds(ks, nkeys), :]
                vv = vs_ref[pl.ds(ks, nkeys), :]
                dout = gs_ref[pl.ds(qs, ATTN_BLOCK), :]
                lse_b = ls_ref[pl.ds(qs, ATTN_BLOCK), :]
                dd_b = ds_ref[pl.ds(qs, ATTN_BLOCK), :]
                dq = jnp.zeros((ATTN_BLOCK, LANES), F32)
                dkk = jnp.zeros((nkeys, LANES), F32)
                dvv = jnp.zeros((nkeys, LANES), F32)
                for e in range(2):
                    msk = head0 if e == 0 else jnp.logical_not(head0)
                    c0 = e * HEAD_DIM
                    qe = jnp.where(msk, q * QK_SCALE, jnp.zeros_like(q))
                    doe = jnp.where(msk, dout, jnp.zeros_like(dout))
                    kke = jnp.where(msk, kk * QK_SCALE, jnp.zeros_like(kk))
                    s = lax.dot_general(qe, kk, nt, preferred_element_type=F32)
                    s = s + (b_ref[bi, e, :, ATTN_BLOCK:] if first else b_ref[bi, e])
                    p = jnp.exp(s - lse_b[:, c0:c0 + 1])
                    dp = lax.dot_general(doe, vv, nt, preferred_element_type=F32)
                    ds = p * (dp - dd_b[:, c0:c0 + 1])
                    if first:
                        db_ref[bi, e, :, ATTN_BLOCK:] += ds
                    else:
                        db_ref[bi, e] += ds
                    dsb = ds.astype(BF16)
                    dq = dq + jnp.dot(dsb, kke, preferred_element_type=F32)
                    dkk = dkk + lax.dot_general(dsb, qe, tn, preferred_element_type=F32)
                    dvv = dvv + lax.dot_general(p.astype(BF16), doe, tn, preferred_element_type=F32)
                aq[pl.ds(qs, ATTN_BLOCK), :] = dq
                ak[pl.ds(ks, nkeys), :] += dkk
                av[pl.ds(ks, nkeys), :] += dvv

            _branch_blocks(S, d, block)

        for a1, a4, a16, out_ref, cs_ref in ((aq1, aq4, aq16, dq_ref, csq_ref), (ak1, ak4, ak16, dk_ref, csk_ref),
                                             (av1, av4, av16, dv_ref, csv_ref)):
            stage[...] = a1[...]
            for d, sub in ((4, a4), (16, a16)):
                L = S // d
                for r in range(d):
                    stage[pl.ds(r, L, stride=d), :] += sub[r * L:(r + 1) * L, :]
            tot = stage[...]
            out_ref[0] = tot.astype(out_ref.dtype)
            _accumulate(cs_ref, first_b, jnp.sum(tot, axis=0, keepdims=True))

    blk = lambda off: pl.BlockSpec((1, S, LANES), lambda h, b: (b, 0, off + h))
    cs_spec = pl.BlockSpec((1, LANES), lambda h, b: (0, h))
    bias_spec = pl.BlockSpec((3, 2, ATTN_BLOCK, 2 * ATTN_BLOCK), lambda h, b: (0, h, 0, 0))
    qv = qkv.reshape(B, S, 3 * AW)
    view = lambda t: t.reshape(B, S, AW)
    sub_b = pltpu.VMEM((S, LANES), BF16)
    sub_f = pltpu.VMEM((S, LANES), F32)
    res = pl.pallas_call(
        body, grid=(HP, B),
        in_specs=[blk(0), blk(HP), blk(2 * HP), blk(0), blk(0), blk(0), bias_spec],
        out_specs=[blk(0), blk(0), blk(0), cs_spec, cs_spec, cs_spec, bias_spec],
        out_shape=[jax.ShapeDtypeStruct((B, S, AW), BF16)] * 3 + [jax.ShapeDtypeStruct((1, AW), F32)] * 3
        + [jax.ShapeDtypeStruct((3, H, ATTN_BLOCK, 2 * ATTN_BLOCK), F32)],
        scratch_shapes=[sub_f] + [sub_b] * 8 + [sub_f] * 4 + [sub_f] * 9,
        compiler_params=_params(2), name="attention_bwd",
    )(qv, qv, qv, view(do), view(lse), view(dd), bias_all)
    flat = lambda t: t.reshape(B * S, AW)
    return flat(res[0]), flat(res[1]), flat(res[2]), res[3], res[4], res[5], res[6]


def _regroup(src, stage, dst, d, S, off=0):
    if d == 1:
        dst[off:off + S, :] = src.astype(dst.dtype)
        return
    stage[...] = src.astype(F32)
    L = S // d
    for r in range(d):
        dst[off + r * L:off + (r + 1) * L, :] = stage[pl.ds(r, L, stride=d), :].astype(dst.dtype)


def _ungroup(sub_ref, off, nat_ref, d, S, add):
    L = S // d
    for r in range(d):
        rows = pl.ds(0, S) if d == 1 else pl.ds(r, L, stride=d)
        val = sub_ref[off + r * L:off + (r + 1) * L, :]
        if add:
            nat_ref[rows, :] += val
        else:
            nat_ref[rows, :] = val


def _branch_keys(ks, vs, S, nb, g_idx):
    blk3 = (S // ATTN_BLOCK, ATTN_BLOCK, LANES)
    kc3 = ks[ATTN_BLOCK:ATTN_BLOCK + S, :].reshape(blk3)
    vc3 = vs[ATTN_BLOCK:ATTN_BLOCK + S, :].reshape(blk3)
    if nb == 1:
        return kc3, vc3, None
    kk3 = jnp.concatenate([ks[0:S, :].reshape(blk3), kc3], axis=1)
    vv3 = jnp.concatenate([vs[0:S, :].reshape(blk3), vc3], axis=1)
    col = lax.broadcasted_iota(jnp.int32, (1, 1, 2 * ATTN_BLOCK), 2)
    dead = jnp.logical_and((g_idx & (nb - 1)) == 0, col < ATTN_BLOCK)
    return kk3, vv3, dead


def _branch_scores(qe, kk3, b_ref, bi, e, dead):
    s = jnp.einsum("gqe,gke->gqk", qe, kk3, preferred_element_type=F32)
    if dead is None:
        return s + b_ref[bi, e, :, ATTN_BLOCK:]
    return jnp.where(dead, NEG_INF, s + b_ref[bi, e])


def _attention_fwd(qkv, bias_all, B, S, AW, bg=None):
    HP = AW // LANES
    G = S // ATTN_BLOCK
    blk3 = (G, ATTN_BLOCK, LANES)

    def body(refs, bg_hook):
        q_ref, k_ref, v_ref, b_ref, o_ref, lse_ref, stage, qs, ks, vs, ot, lt, on0, on1, on2, ln0, ln1, ln2 = refs
        bg_hook(False)
        head0 = lax.broadcasted_iota(jnp.int32, (1, 1, LANES), 2) < HEAD_DIM
        g_idx = lax.broadcasted_iota(jnp.int32, (G, 1, 1), 0)
        ks[0:ATTN_BLOCK, :] = jnp.zeros((ATTN_BLOCK, LANES), BF16)
        vs[0:ATTN_BLOCK, :] = jnp.zeros((ATTN_BLOCK, LANES), BF16)
        nat_o, nat_l = (on0, on1, on2), (ln0, ln1, ln2)
        for bi, (_, d) in enumerate(DILATED_CONFIGS):
            nb = S // d // ATTN_BLOCK
            _regroup(q_ref[0], stage, qs, d, S)
            _regroup(k_ref[0], stage, ks, d, S, ATTN_BLOCK)
            _regroup(v_ref[0], stage, vs, d, S, ATTN_BLOCK)
            q3 = qs[...].reshape(blk3) * QK_SCALE
            kk3, vv3, dead = _branch_keys(ks, vs, S, nb, g_idx)
            outs, lses = [], []
            for e in range(2):
                msk = head0 if e == 0 else jnp.logical_not(head0)
                qe = jnp.where(msk, q3, jnp.zeros_like(q3))
                s = _branch_scores(qe, kk3, b_ref, bi, e, dead)
                m = jnp.max(s, axis=-1, keepdims=True)
                p = jnp.exp(s - m)
                l = jnp.sum(p, axis=-1, keepdims=True)
                o = jnp.einsum("gqk,gke->gqe", p.astype(BF16), vv3, preferred_element_type=F32)
                outs.append(o / l)
                lses.append(jnp.broadcast_to(m + jnp.log(l), blk3))
            ot[...] = jnp.where(head0, outs[0], outs[1]).reshape(S, LANES)
            lt[...] = jnp.where(head0, lses[0], lses[1]).reshape(S, LANES)
            _ungroup(ot, 0, nat_o[bi], d, S, add=False)
            _ungroup(lt, 0, nat_l[bi], d, S, add=False)

        la, lb, lc = ln0[...], ln1[...], ln2[...]
        m = jnp.maximum(jnp.maximum(la, lb), lc)
        ea, eb, ec = jnp.exp(la - m), jnp.exp(lb - m), jnp.exp(lc - m)
        den = ea + eb + ec
        lse_ref[0] = m + jnp.log(den)
        o_ref[0] = (ea * on0[...] + eb * on1[...] + ec * on2[...]) / den
        bg_hook(True)

    blk = lambda off: pl.BlockSpec((1, S, LANES), lambda b, h: (b, 0, off + h))
    qv = qkv.reshape(B, S, 3 * AW)
    sub_f = pltpu.VMEM((S, LANES), F32)
    pad_b = pltpu.VMEM((S + ATTN_BLOCK, LANES), BF16)
    res = _hosted_call(
        body, bg, grid=(B, HP),
        in_specs=[blk(0), blk(HP), blk(2 * HP),
                  pl.BlockSpec((3, 2, ATTN_BLOCK, 2 * ATTN_BLOCK), lambda b, h: (0, h, 0, 0))],
        out_specs=[blk(0), blk(0)],
        out_shape=[jax.ShapeDtypeStruct((B, S, AW), F32)] * 2,
        scratch_shapes=[sub_f, pltpu.VMEM((S, LANES), BF16), pad_b, pad_b] + [sub_f] * 8,
        operands=[qv, qv, qv, bias_all], name="attention_fwd")
    return (res[0].reshape(B * S, AW), res[1].reshape(B * S, AW)) + tuple(res[2:])


def _attention_bwd(qkv, do, lse, dd, bias_all, B, S, AW, bg=None):
    HP = AW // LANES
    H = AW // HEAD_DIM
    G = S // ATTN_BLOCK
    blk3 = (G, ATTN_BLOCK, LANES)
    PAD = ATTN_BLOCK

    def body(refs, bg_hook):
        (q_ref, k_ref, v_ref, do_ref, lse_ref, dd_ref, b_ref,
         dq_ref, dk_ref, dv_ref, csq_ref, csk_ref, csv_ref, db_ref,
         stage, qs, ks, vs, gs, ls, ds_, tq, tk, tv, accq, acck, accv) = refs
        bg_hook(False)
        head0 = lax.broadcasted_iota(jnp.int32, (1, 1, LANES), 2) < HEAD_DIM
        g_idx = lax.broadcasted_iota(jnp.int32, (G, 1, 1), 0)
        first_b = pl.program_id(1) == 0

        @pl.when(first_b)
        def _():
            db_ref[...] = jnp.zeros_like(db_ref)

        ks[0:PAD, :] = jnp.zeros((PAD, LANES), BF16)
        vs[0:PAD, :] = jnp.zeros((PAD, LANES), BF16)
        tk[0:PAD, :] = jnp.zeros((PAD, LANES), F32)
        tv[0:PAD, :] = jnp.zeros((PAD, LANES), F32)
        for bi, (_, d) in enumerate(DILATED_CONFIGS):
            nb = S // d // ATTN_BLOCK
            _regroup(q_ref[0], stage, qs, d, S)
            _regroup(k_ref[0], stage, ks, d, S, PAD)
            _regroup(v_ref[0], stage, vs, d, S, PAD)
            _regroup(do_ref[0], stage, gs, d, S)
            _regroup(lse_ref[0], stage, ls, d, S)
            _regroup(dd_ref[0], stage, ds_, d, S)
            q3 = qs[...].reshape(blk3) * QK_SCALE
            do3 = gs[...].reshape(blk3)
            lse3 = ls[...].reshape(blk3)
            dd3 = ds_[...].reshape(blk3)
            kk3, vv3, dead = _branch_keys(ks, vs, S, nb, g_idx)
            dq = jnp.zeros(blk3, F32)
            dkk = jnp.zeros(kk3.shape, F32)
            dvv = jnp.zeros(kk3.shape, F32)
            for e in range(2):
                msk = head0 if e == 0 else jnp.logical_not(head0)
                c0 = e * HEAD_DIM
                qe = jnp.where(msk, q3, jnp.zeros_like(q3))
                doe = jnp.where(msk, do3, jnp.zeros_like(do3))
                ke = jnp.where(msk, kk3 * QK_SCALE, jnp.zeros_like(kk3))
                s = _branch_scores(qe, kk3, b_ref, bi, e, dead)
                p = jnp.exp(s - lse3[:, :, c0:c0 + 1])
                dp = jnp.einsum("gqe,gke->gqk", doe, vv3, preferred_element_type=F32)
                dsc = p * (dp - dd3[:, :, c0:c0 + 1])
                if dead is None:
                    db_ref[bi, e, :, ATTN_BLOCK:] += jnp.sum(dsc, axis=0)
                else:
                    db_ref[bi, e] += jnp.sum(dsc, axis=0)
                dsb = dsc.astype(BF16)
                dq = dq + jnp.einsum("gqk,gke->gqe", dsb, ke, preferred_element_type=F32)
                dkk = dkk + jnp.einsum("gqk,gqe->gke", dsb, qe, preferred_element_type=F32)
                dvv = dvv + jnp.einsum("gqk,gqe->gke", p.astype(BF16), doe, preferred_element_type=F32)
            tq[...] = dq.reshape(S, LANES)
            if dead is None:
                tk[PAD:PAD + S, :] = dkk.reshape(S, LANES)
                tv[PAD:PAD + S, :] = dvv.reshape(S, LANES)
            else:
                tk[PAD:PAD + S, :] = dkk[:, ATTN_BLOCK:, :].reshape(S, LANES)
                tv[PAD:PAD + S, :] = dvv[:, ATTN_BLOCK:, :].reshape(S, LANES)
                tk[0:S, :] += dkk[:, :ATTN_BLOCK, :].reshape(S, LANES)
                tv[0:S, :] += dvv[:, :ATTN_BLOCK, :].reshape(S, LANES)
            _ungroup(tq, 0, accq, d, S, add=bi > 0)
            _ungroup(tk, PAD, acck, d, S, add=bi > 0)
            _ungroup(tv, PAD, accv, d, S, add=bi > 0)

        for acc, out_ref, cs_ref in ((accq, dq_ref, csq_ref), (acck, dk_ref, csk_ref), (accv, dv_ref, csv_ref)):
            tot = acc[...]
            out_ref[0] = tot.astype(out_ref.dtype)
            _accumulate(cs_ref, first_b, jnp.sum(tot, axis=0, keepdims=True))
        bg_hook(True)

    blk = lambda off: pl.BlockSpec((1, S, LANES), lambda h, b: (b, 0, off + h))
    cs_spec = pl.BlockSpec((1, LANES), lambda h, b: (0, h))
    bias_spec = pl.BlockSpec((3, 2, ATTN_BLOCK, 2 * ATTN_BLOCK), lambda h, b: (0, h, 0, 0))
    qv = qkv.reshape(B, S, 3 * AW)
    view = lambda t: t.reshape(B, S, AW)
    sub_b = pltpu.VMEM((S, LANES), BF16)
    sub_f = pltpu.VMEM((S, LANES), F32)
    pad_b = pltpu.VMEM((S + PAD, LANES), BF16)
    pad_f = pltpu.VMEM((S + PAD, LANES), F32)
    res = _hosted_call(
        body, bg, grid=(HP, B),
        in_specs=[blk(0), blk(HP), blk(2 * HP), blk(0), blk(0), blk(0), bias_spec],
        out_specs=[blk(0), blk(0), blk(0), cs_spec, cs_spec, cs_spec, bias_spec],
        out_shape=[jax.ShapeDtypeStruct((B, S, AW), BF16)] * 3 + [jax.ShapeDtypeStruct((1, AW), F32)] * 3
        + [jax.ShapeDtypeStruct((3, H, ATTN_BLOCK, 2 * ATTN_BLOCK), F32)],
        scratch_shapes=[sub_f, sub_b, pad_b, pad_b, sub_b, sub_f, sub_f, sub_f, pad_f, pad_f, sub_f, sub_f, sub_f],
        operands=[qv, qv, qv, view(do), view(lse), view(dd), bias_all], name="attention_bwd")
    flat = lambda t: t.reshape(B * S, AW)
    return (flat(res[0]), flat(res[1]), flat(res[2]), res[3], res[4], res[5], res[6]) + tuple(res[7:])


def _attn_norm(attn, gain, tm):
    T, AW = attn.shape

    def body(a_ref, g_ref, mix_ref, r_ref):
        a = a_ref[...]
        r = lax.rsqrt(jnp.mean(a * a, axis=-1, keepdims=True) + LN_EPS)
        mix_ref[...] = (a * r * g_ref[...]).astype(BF16)
        r_ref[...] = jnp.broadcast_to(r, (tm, LANES))

    row = pl.BlockSpec((tm, AW), lambda i: (i, 0))
    return pl.pallas_call(
        body, grid=(T // tm,), in_specs=[row, pl.BlockSpec((1, AW), lambda i: (0, 0))],
        out_specs=[row, pl.BlockSpec((tm, LANES), lambda i: (i, 0))],
        out_shape=[jax.ShapeDtypeStruct((T, AW), BF16), jax.ShapeDtypeStruct((T, LANES), F32)],
        compiler_params=_params(1), name="attn_norm",
    )(attn, gain)


def _attn_pre_bwd(dmixed, attn, rstd, gain, tm):
    T, AW = attn.shape
    ones_np = np.kron(np.eye(AW // HEAD_DIM, dtype=np.float32), np.ones((HEAD_DIM, HEAD_DIM), np.float32))
    ones_bd = jnp.asarray(ones_np, dtype=BF16)

    def body(dm_ref, a_ref, r_ref, g_ref, ones_ref, do_ref, dd_ref, dg_ref):
        i = pl.program_id(0)
        dm = dm_ref[...]
        a = a_ref[...]
        r = r_ref[:, 0:1]
        dxn = dm * g_ref[...]
        da = r * (dxn - a * (r * r) * jnp.mean(dxn * a, axis=-1, keepdims=True))
        do_ref[...] = da.astype(BF16)
        hi, lo = _split_hi_lo(da * a)
        dd_ref[...] = (jnp.dot(hi, ones_ref[...], preferred_element_type=F32)
                       + jnp.dot(lo, ones_ref[...], preferred_element_type=F32))
        _accumulate(dg_ref, i == 0, jnp.sum(dm * a * r, axis=0, keepdims=True))

    row = pl.BlockSpec((tm, AW), lambda i: (i, 0))
    vec = pl.BlockSpec((1, AW), lambda i: (0, 0))
    return pl.pallas_call(
        body, grid=(T // tm,),
        in_specs=[row, row, pl.BlockSpec((tm, LANES), lambda i: (i, 0)), vec,
                  pl.BlockSpec((AW, AW), lambda i: (0, 0))],
        out_specs=[row, row, vec],
        out_shape=[jax.ShapeDtypeStruct((T, AW), BF16), jax.ShapeDtypeStruct((T, AW), F32),
                   jax.ShapeDtypeStruct((1, AW), F32)],
        compiler_params=_params(1), name="attn_pre_bwd",
    )(dmixed, attn, rstd, gain, ones_bd)


class _RowShifts:
    def __init__(self, x, row, up):
        self.x, self.row, self.up, self.base = x, row, up, {0: x}

    def __call__(self, s):
        x = self.x
        n, c = x.shape
        r, whole = s % 8, s - s % 8
        if r not in self.base:
            if self.up:
                rolled = pltpu.roll(x, n - r, 0)
                tail = jnp.where(self.row[n - 8:] < n - r, rolled[n - 8:], 0.0)
                self.base[r] = jnp.concatenate([rolled[:n - 8], tail], axis=0)
            else:
                rolled = pltpu.roll(x, r, 0)
                head = jnp.where(self.row[:8] >= r, rolled[:8], 0.0)
                self.base[r] = jnp.concatenate([head, rolled[8:]], axis=0)
        y = self.base[r]
        if whole == 0:
            return y
        pad = jnp.zeros((whole, c), x.dtype)
        if self.up:
            return jnp.concatenate([y[whole:], pad], axis=0)
        return jnp.concatenate([pad, y[:n - whole]], axis=0)


def _conv_branch_fwd_math(a, g, w_ref, cb, lg, lb, row):
    sg = _sigmoid(g)
    u0 = a * sg
    u0_down = _RowShifts(u0, row, up=False)
    uc = jnp.zeros_like(u0) + cb
    for k in range(CONV_KERNEL):
        uc = uc + w_ref[k:k + 1, :] * u0_down(CONV_KERNEL - 1 - k)
    ul, xh, r = _ln_fwd(uc, lg, lb)
    su = _sigmoid(ul)
    u = ul * su
    return sg, u0_down, ul, xh, r, su, u


def _conv_fwd(ag, conv_w, conv_b, ln_g, ln_b, norm_g, B, S, CW):
    def body(a_ref, g_ref, w_ref, cb_ref, lg_ref, lb_ref, ng_ref, o_ref):
        row = lax.broadcasted_iota(jnp.int32, (S, CW), 0)
        _, _, _, _, _, _, u = _conv_branch_fwd_math(a_ref[0], g_ref[0], w_ref, cb_ref[...], lg_ref[...],
                                                    lb_ref[...], row)
        rr = lax.rsqrt(jnp.mean(u * u, axis=-1, keepdims=True) + LN_EPS)
        o_ref[0] = (u * rr * ng_ref[...]).astype(BF16)

    vec = pl.BlockSpec((1, CW), lambda b: (0, 0))
    out = pl.pallas_call(
        body, grid=(B,),
        in_specs=[pl.BlockSpec((1, S, CW), lambda b: (b, 0, 0)), pl.BlockSpec((1, S, CW), lambda b: (b, 0, 1)),
                  pl.BlockSpec((CONV_KERNEL, CW), lambda b: (0, 0)), vec, vec, vec, vec],
        out_specs=pl.BlockSpec((1, S, CW), lambda b: (b, 0, 0)),
        out_shape=jax.ShapeDtypeStruct((B, S, CW), BF16),
        compiler_params=_params(1), name="conv_fwd",
    )(ag.reshape(B, S, 2 * CW), ag.reshape(B, S, 2 * CW), conv_w, conv_b, ln_g, ln_b, norm_g)
    return out.reshape(B * S, CW)


def _conv_bwd(ag, dmixed, conv_w, conv_b, ln_g, ln_b, norm_g, B, S, CW, D):
    AW = D - CW
    assert AW % CW == 0

    def body(a_ref, g_ref, dm_ref, w_ref, cb_ref, lg_ref, lb_ref, ng_ref,
             dag_ref, dw_ref, dcb_ref, dlg_ref, dlb_ref, dng_ref):
        b = pl.program_id(0)
        row = lax.broadcasted_iota(jnp.int32, (S, CW), 0)
        a, g = a_ref[0], g_ref[0]
        sg, u0_down, ul, xh, r, su, u = _conv_branch_fwd_math(a, g, w_ref, cb_ref[...], lg_ref[...], lb_ref[...], row)
        rr = lax.rsqrt(jnp.mean(u * u, axis=-1, keepdims=True) + LN_EPS)
        dm = dm_ref[0]
        dxn = dm * ng_ref[...]
        du = rr * (dxn - u * (rr * rr) * jnp.mean(dxn * u, axis=-1, keepdims=True))
        dul = du * su * (1.0 + ul * (1.0 - su))
        duc = _ln_bwd(dul, xh, r, lg_ref[...])
        first = b == 0
        _accumulate(dng_ref, first, jnp.sum(dm * u * rr, axis=0, keepdims=True))
        _accumulate(dlg_ref, first, jnp.sum(dul * xh, axis=0, keepdims=True))
        _accumulate(dlb_ref, first, jnp.sum(dul, axis=0, keepdims=True))
        _accumulate(dcb_ref, first, jnp.sum(duc, axis=0, keepdims=True))

        @pl.when(first)
        def _():
            dw_ref[...] = jnp.zeros_like(dw_ref)

        duc_up = _RowShifts(duc, row, up=True)
        du0 = jnp.zeros_like(duc)
        for k in range(CONV_KERNEL):
            sh = CONV_KERNEL - 1 - k
            dw_ref[k:k + 1, :] += jnp.sum(duc * u0_down(sh), axis=0, keepdims=True)
            du0 = du0 + w_ref[k:k + 1, :] * duc_up(sh)
        dag_ref[0, :, :CW] = du0 * sg
        dag_ref[0, :, CW:] = du0 * a * sg * (1.0 - sg)

    vec = pl.BlockSpec((1, CW), lambda b: (0, 0))
    wspec = pl.BlockSpec((CONV_KERNEL, CW), lambda b: (0, 0))
    agv = ag.reshape(B, S, 2 * CW)
    res = pl.pallas_call(
        body, grid=(B,),
        in_specs=[pl.BlockSpec((1, S, CW), lambda b: (b, 0, 0)), pl.BlockSpec((1, S, CW), lambda b: (b, 0, 1)),
                  pl.BlockSpec((1, S, CW), lambda b: (b, 0, AW // CW)), wspec, vec, vec, vec, vec],
        out_specs=[pl.BlockSpec((1, S, 2 * CW), lambda b: (b, 0, 0)), wspec, vec, vec, vec, vec],
        out_shape=[jax.ShapeDtypeStruct((B, S, 2 * CW), F32), jax.ShapeDtypeStruct((CONV_KERNEL, CW), F32)]
        + [jax.ShapeDtypeStruct((1, CW), F32)] * 4,
        compiler_params=_params(1), name="conv_bwd",
    )(agv, agv, dmixed.reshape(B, S, D), conv_w, conv_b, ln_g, ln_b, norm_g)
    return (res[0].reshape(B * S, 2 * CW),) + tuple(res[1:])


def _ffn_conv(x, w_ref, bias, row):
    y = jnp.zeros_like(x) + bias
    for k in range(FFN_CONV_KERNEL):
        y = y + w_ref[k:k + 1, :] * _shift_down(x, FFN_CONV_KERNEL - 1 - k, row)
    return y


def _ffn_specs(S, tc, nj, order):
    pick = (lambda b, j: (b, j)) if order == "bj" else (lambda j, b: (b, j))
    act = lambda off: pl.BlockSpec((1, S, tc), lambda *g: (pick(*g)[0], 0, off + pick(*g)[1]))
    cw = lambda off: pl.BlockSpec((FFN_CONV_KERNEL, tc), lambda *g: (0, off + pick(*g)[1]))
    cb = lambda off: pl.BlockSpec((1, tc), lambda *g: (0, off + pick(*g)[1]))
    return act, cw, cb


def _ffn_act(upre, cw, cb, B, S, DFF):
    tc = FFN_COLS
    nj = DFF // tc

    def body(ug_ref, uv_ref, wg_ref, wv_ref, bg_ref, bv_ref, o_ref):
        row = lax.broadcasted_iota(jnp.int32, (S, tc), 0)
        gate = _ffn_conv(ug_ref[0], wg_ref, bg_ref[...], row)
        val = _ffn_conv(uv_ref[0], wv_ref, bv_ref[...], row)
        o_ref[0] = (gate * _sigmoid(gate) * val).astype(BF16)

    act, cws, cbs = _ffn_specs(S, tc, nj, "bj")
    uv = upre.reshape(B, S, 2 * DFF)
    out = pl.pallas_call(
        body, grid=(B, nj), in_specs=[act(0), act(nj), cws(0), cws(nj), cbs(0), cbs(nj)], out_specs=act(0),
        out_shape=jax.ShapeDtypeStruct((B, S, DFF), BF16), compiler_params=_params(2), name="ffn_act",
    )(uv, uv, cw, cw, cb, cb)
    return out.reshape(B * S, DFF)


def _ffn_bwd(upre, dact, cw, cb, B, S, DFF):
    tc = FFN_COLS
    nj = DFF // tc

    def body(ug_ref, uv_ref, da_ref, wg_ref, wv_ref, bg_ref, bv_ref, dug_ref, duv_ref, dwg_ref, dwv_ref,
             dbg_ref, dbv_ref):
        first = pl.program_id(1) == 0
        row = lax.broadcasted_iota(jnp.int32, (S, tc), 0)
        ug, uv = ug_ref[0], uv_ref[0]
        gate = _ffn_conv(ug, wg_ref, bg_ref[...], row)
        val = _ffn_conv(uv, wv_ref, bv_ref[...], row)
        sg = _sigmoid(gate)
        dact_b = da_ref[0]
        dgate = dact_b * val * sg * (1.0 + gate * (1.0 - sg))
        dval = dact_b * gate * sg
        for dup, u, w_ref, du_ref, dw_ref, db_ref in ((dgate, ug, wg_ref, dug_ref, dwg_ref, dbg_ref),
                                                      (dval, uv, wv_ref, duv_ref, dwv_ref, dbv_ref)):
            _accumulate(db_ref, first, jnp.sum(dup, axis=0, keepdims=True))

            @pl.when(first)
            def _(dw_ref=dw_ref):
                dw_ref[...] = jnp.zeros_like(dw_ref)

            dupre = jnp.zeros_like(dup)
            for k in range(FFN_CONV_KERNEL):
                sh = FFN_CONV_KERNEL - 1 - k
                dw_ref[k:k + 1, :] += jnp.sum(dup * _shift_down(u, sh, row), axis=0, keepdims=True)
                dupre = dupre + w_ref[k:k + 1, :] * _shift_up(dup, sh, row)
            du_ref[0] = dupre.astype(BF16)

    act, cws, cbs = _ffn_specs(S, tc, nj, "jb")
    uv = upre.reshape(B, S, 2 * DFF)
    res = pl.pallas_call(
        body, grid=(nj, B),
        in_specs=[act(0), act(nj), act(0), cws(0), cws(nj), cbs(0), cbs(nj)],
        out_specs=[act(0), act(0), cws(0), cws(0), cbs(0), cbs(0)],
        out_shape=[jax.ShapeDtypeStruct((B, S, DFF), BF16)] * 2
        + [jax.ShapeDtypeStruct((FFN_CONV_KERNEL, DFF), F32)] * 2 + [jax.ShapeDtypeStruct((1, DFF), F32)] * 2,
        compiler_params=_params(2), name="ffn_bwd",
    )(uv, uv, dact.reshape(B, S, DFF), cw, cw, cb, cb)
    flat = lambda t: t.reshape(B * S, DFF)
    return (flat(res[0]), flat(res[1]), jnp.concatenate([res[2], res[3]], axis=1),
            jnp.concatenate([res[4], res[5]], axis=1))


FFN_HALO = 16


def _half_sequences(S):
    if S < 8 * FFN_HALO:
        return [(0, S, 0, S)]
    h = S // 2
    return [(0, h + FFN_HALO, 0, h), (h - FFN_HALO, S, FFN_HALO, h)]


def _w_up_block_spec(w_up_sh, tc, off):
    _, D, cs = w_up_sh.shape
    assert cs % tc == 0
    bps = cs // tc
    return pl.BlockSpec((1, D, tc), lambda j: ((off + j) // bps, 0, (off + j) % bps))


def _ffn_fwd_fused(x1b, w_up_sh, cw, cb, B, S, DFF):
    tc = FFN_COLS
    nj = DFF // tc
    D = x1b.shape[1]

    def body(x_ref, wg_ref, wv_ref, cwg_ref, cwv_ref, cbg_ref, cbv_ref, o_ref, up_ref):
        w = jnp.concatenate([wg_ref[0], wv_ref[0]], axis=1)
        row = lax.broadcasted_iota(jnp.int32, (S, tc), 0)
        for b in range(B):
            up = jnp.dot(x_ref[b], w, preferred_element_type=F32)
            up_ref[b] = up
            gate = _ffn_conv(up[:, :tc], cwg_ref, cbg_ref[...], row)
            val = _ffn_conv(up[:, tc:], cwv_ref, cbv_ref[...], row)
            o_ref[b] = (gate * _sigmoid(gate) * val).astype(BF16)

    cws = lambda off: pl.BlockSpec((FFN_CONV_KERNEL, tc), lambda j: (0, off + j))
    cbs = lambda off: pl.BlockSpec((1, tc), lambda j: (0, off + j))
    act, upre = pl.pallas_call(
        body, grid=(nj,),
        in_specs=[pl.BlockSpec((B, S, D), lambda j: (0, 0, 0), pipeline_mode=pl.Buffered(1)),
                  _w_up_block_spec(w_up_sh, tc, 0), _w_up_block_spec(w_up_sh, tc, nj),
                  cws(0), cws(nj), cbs(0), cbs(nj)],
        out_specs=[pl.BlockSpec((B, S, tc), lambda j: (0, 0, j)), pl.BlockSpec((B, S, 2 * tc), lambda j: (0, 0, j))],
        out_shape=[jax.ShapeDtypeStruct((B, S, DFF), BF16), jax.ShapeDtypeStruct((B, S, 2 * DFF), F32)],
        compiler_params=_params(1), name="ffn_fwd",
    )(x1b.reshape(B, S, D), w_up_sh, w_up_sh, cw, cw, cb, cb)
    return act.reshape(B * S, DFF), upre


def _ffn_bwd_fused(x1b, dz2b, upre, w_down, cw, cb, B, S, DFF):
    tc = FFN_COLS
    nj = DFF // tc
    D = x1b.shape[1]

    def body(x_ref, dz_ref, up_ref, wd_ref, cwg_ref, cwv_ref, cbg_ref, cbv_ref,
             dug_ref, duv_ref, dwu_ref, dwd_ref, dcw_ref, dcb_ref):
        first = pl.program_id(1) == 0
        dw_t = dwd = None
        dcb = [None, None]
        dcw = [[None] * FFN_CONV_KERNEL, [None] * FFN_CONV_KERNEL]
        add = lambda old, new: new if old is None else old + new
        for lo, hi, o0, on in _half_sequences(S):
            n = hi - lo
            own = slice(o0, o0 + on)
            row = lax.broadcasted_iota(jnp.int32, (n, tc), 0)
            x = x_ref[0, lo:hi, :]
            dz = dz_ref[0, lo:hi, :]
            ug, uv = up_ref[0, lo:hi, :tc], up_ref[0, lo:hi, tc:]
            gate = _ffn_conv(ug, cwg_ref, cbg_ref[...], row)
            val = _ffn_conv(uv, cwv_ref, cbv_ref[...], row)
            sg = _sigmoid(gate)
            act = (gate * sg * val).astype(BF16)
            dact = _dot(dz, wd_ref[...], "nt")
            dgate = dact * val * sg * (1.0 + gate * (1.0 - sg))
            dval = dact * gate * sg
            dupre = []
            for h, (dup, u, w_ref) in enumerate(((dgate, ug, cwg_ref), (dval, uv, cwv_ref))):
                dcb[h] = add(dcb[h], jnp.sum(dup[own], axis=0, keepdims=True))
                acc = jnp.zeros_like(dup)
                for k in range(FFN_CONV_KERNEL):
                    sh = FFN_CONV_KERNEL - 1 - k
                    dcw[h][k] = add(dcw[h][k], jnp.sum((dup * _shift_down(u, sh, row))[own], axis=0, keepdims=True))
                    acc = acc + w_ref[k:k + 1, :] * _shift_up(dup, sh, row)
                dupre.append(acc.astype(BF16)[own])
            dug_ref[0, lo + o0:lo + o0 + on, :] = dupre[0]
            duv_ref[0, lo + o0:lo + o0 + on, :] = dupre[1]
            dw_t = add(dw_t, _dot(jnp.concatenate(dupre, axis=1), x[own], "tn"))
            dwd = add(dwd, _dot(act[own], dz[own], "tn"))
        _accumulate(dwu_ref.at[0], first, dw_t[:tc])
        _accumulate(dwu_ref.at[1], first, dw_t[tc:])
        _accumulate(dwd_ref, first, dwd)
        for h in range(2):
            _accumulate(dcb_ref.at[h], first, dcb[h])
            for k in range(FFN_CONV_KERNEL):
                _accumulate(dcw_ref.at[k, pl.ds(h, 1), :], first, dcw[h][k])

    act_s, cws, cbs = _ffn_specs(S, tc, nj, "jb")
    seq = pl.BlockSpec((1, S, D), lambda j, b: (b, 0, 0))
    res = pl.pallas_call(
        body, grid=(nj, B),
        in_specs=[seq, seq, pl.BlockSpec((1, S, 2 * tc), lambda j, b: (b, 0, j)),
                  pl.BlockSpec((tc, D), lambda j, b: (j, 0)), cws(0), cws(nj), cbs(0), cbs(nj)],
        out_specs=[act_s(0), act_s(0), pl.BlockSpec((2, tc, D), lambda j, b: (0, j, 0)),
                   pl.BlockSpec((tc, D), lambda j, b: (j, 0)),
                   pl.BlockSpec((FFN_CONV_KERNEL, 2, tc), lambda j, b: (0, 0, j)),
                   pl.BlockSpec((2, 1, tc), lambda j, b: (0, 0, j))],
        out_shape=[jax.ShapeDtypeStruct((B, S, DFF), BF16)] * 2
        + [jax.ShapeDtypeStruct((2, DFF, D), F32), jax.ShapeDtypeStruct((DFF, D), F32),
           jax.ShapeDtypeStruct((FFN_CONV_KERNEL, 2, DFF), F32), jax.ShapeDtypeStruct((2, 1, DFF), F32)],
        compiler_params=_params(2), name="ffn_bwd",
    )(x1b.reshape(B, S, D), dz2b.reshape(B, S, D), upre, w_down, cw, cw, cb, cb)
    flat = lambda t: t.reshape(B * S, DFF)
    return flat(res[0]), flat(res[1]), res[2], res[3], res[4], res[5]


def _ffn_bwd_seq(b, x1b3, dz2b3, upre, w_up_sh, w_down, cw, cb, prev, S, DFF):
    tc = FFN_COLS
    nj = DFF // tc
    B, _, D = x1b3.shape
    n_prev = 0 if prev is None else 5

    def body(*refs):
        (x_ref, dz_ref, up_ref, wg_ref, wv_ref, wd_ref, cwg_ref, cwv_ref, cbg_ref, cbv_ref) = refs[:10]
        prev_refs = refs[10:10 + n_prev]
        dx_hbm, dwu_ref, dwd_ref, dcw_ref, dcb_ref, acc_ref, sem = refs[10 + n_prev:]
        j = pl.program_id(0)

        @pl.when(j == 0)
        def _():
            acc_ref[...] = jnp.zeros_like(acc_ref)

        wcat = jnp.concatenate([wg_ref[0], wv_ref[0]], axis=1)
        dw_t = dwd = None
        dcb = [None, None]
        dcw = [[None] * FFN_CONV_KERNEL, [None] * FFN_CONV_KERNEL]
        add = lambda old, new: new if old is None else old + new
        for lo, hi, o0, on in _half_sequences(S):
            n = hi - lo
            own = slice(o0, o0 + on)
            row = lax.broadcasted_iota(jnp.int32, (n, tc), 0)
            x = x_ref[0, lo:hi, :]
            dz = dz_ref[0, lo:hi, :]
            ug, uv = up_ref[0, lo:hi, :tc], up_ref[0, lo:hi, tc:]
            gate = _ffn_conv(ug, cwg_ref, cbg_ref[...], row)
            val = _ffn_conv(uv, cwv_ref, cbv_ref[...], row)
            sg = _sigmoid(gate)
            act = (gate * sg * val).astype(BF16)
            dact = _dot(dz, wd_ref[...], "nt")
            dgate = dact * val * sg * (1.0 + gate * (1.0 - sg))
            dval = dact * gate * sg
            dupre = []
            for h, (dup, u, w_ref) in enumerate(((dgate, ug, cwg_ref), (dval, uv, cwv_ref))):
                dcb[h] = add(dcb[h], jnp.sum(dup[own], axis=0, keepdims=True))
                acc = jnp.zeros_like(dup)
                for k in range(FFN_CONV_KERNEL):
                    sh = FFN_CONV_KERNEL - 1 - k
                    dcw[h][k] = add(dcw[h][k], jnp.sum((dup * _shift_down(u, sh, row))[own], axis=0, keepdims=True))
                    acc = acc + w_ref[k:k + 1, :] * _shift_up(dup, sh, row)
                dupre.append(acc.astype(BF16)[own])
            dupre_cat = jnp.concatenate(dupre, axis=1)
            dw_t = add(dw_t, _dot(dupre_cat, x[own], "tn"))
            dwd = add(dwd, _dot(act[own], dz[own], "tn"))
            acc_ref[lo + o0:lo + o0 + on, :] += _dot(dupre_cat, wcat, "nt")
        if n_prev:
            _, pwu_ref, pwd_ref, pcw_ref, pcb_ref = prev_refs
            dwu_ref[0] = pwu_ref[0] + dw_t[:tc]
            dwu_ref[1] = pwu_ref[1] + dw_t[tc:]
            dwd_ref[...] = pwd_ref[...] + dwd
        else:
            dwu_ref[0] = dw_t[:tc]
            dwu_ref[1] = dw_t[tc:]
            dwd_ref[...] = dwd
        for h in range(2):
            dcb_ref[h] = dcb[h] + pcb_ref[h] if n_prev else dcb[h]
            for k in range(FFN_CONV_KERNEL):
                dcw_ref[k, h:h + 1, :] = dcw[h][k] + pcw_ref[k, h:h + 1, :] if n_prev else dcw[h][k]

        @pl.when(j == nj - 1)
        def _():
            out = pltpu.make_async_copy(acc_ref, dx_hbm.at[b], sem)
            out.start()
            out.wait()

    bps = w_up_sh.shape[2] // tc
    wspec = lambda off: pl.BlockSpec((1, D, tc), lambda j: ((off + j) // bps, 0, (off + j) % bps))
    cws = lambda off: pl.BlockSpec((FFN_CONV_KERNEL, tc), lambda j: (0, off + j))
    cbs = lambda off: pl.BlockSpec((1, tc), lambda j: (0, off + j))
    seq = pl.BlockSpec((1, S, D), lambda j: (b, 0, 0), pipeline_mode=pl.Buffered(1))
    part_specs = [pl.BlockSpec((2, tc, D), lambda j: (0, j, 0)), pl.BlockSpec((tc, D), lambda j: (j, 0)),
                  pl.BlockSpec((FFN_CONV_KERNEL, 2, tc), lambda j: (0, 0, j)), pl.BlockSpec((2, 1, tc), lambda j: (0, 0, j))]
    part_shapes = [jax.ShapeDtypeStruct((2, DFF, D), F32), jax.ShapeDtypeStruct((DFF, D), F32),
                   jax.ShapeDtypeStruct((FFN_CONV_KERNEL, 2, DFF), F32), jax.ShapeDtypeStruct((2, 1, DFF), F32)]
    in_specs = [seq, seq, pl.BlockSpec((1, S, 2 * tc), lambda j: (b, 0, j)), wspec(0), wspec(nj),
                pl.BlockSpec((tc, D), lambda j: (j, 0)), cws(0), cws(nj), cbs(0), cbs(nj)]
    operands = [x1b3, dz2b3, upre, w_up_sh, w_up_sh, w_down, cw, cw, cb, cb]
    aliases = {}
    if n_prev:
        in_specs += [HBM_SPEC] + part_specs
        operands += list(prev)
        aliases = {10 + i: i for i in range(5)}
    return pl.pallas_call(
        body, grid=(nj,), in_specs=in_specs, out_specs=[HBM_SPEC] + part_specs,
        out_shape=[jax.ShapeDtypeStruct((B, S, D), F32)] + part_shapes, input_output_aliases=aliases,
        scratch_shapes=[pltpu.VMEM((S, D), F32), pltpu.SemaphoreType.DMA],
        compiler_params=_params(1), name="ffn_bwd_seq%d" % b,
    )(*operands)


def _dx1_ln1_bwd(dupre_g, dupre_v, w_up_sh, dz2, xh1, r1, ln1_g, tm, bg):
    T, D = dz2.shape
    NS, _, cs = w_up_sh.shape
    half = NS // 2
    DFF = dupre_g.shape[1]

    def body(refs, bg_hook):
        dug_ref, duv_ref, w_ref, dz2_ref, xh_ref, r_ref, g_ref, dz_ref, dzb_ref, dg_ref, db_ref = refs
        bg_hook(False)
        first = pl.program_id(0) == 0
        acc = ALPHA * dz2_ref[...]
        for k in range(NS):
            src = dug_ref if k < half else duv_ref
            c0 = (k % half) * cs
            acc = acc + _dot(src[:, c0:c0 + cs], w_ref[k], "nt")
        dx1 = acc
        xh = xh_ref[...]
        dz = _ln_bwd(dx1, xh, r_ref[:, 0:1], g_ref[...])
        dz_ref[...] = dz
        dzb_ref[...] = dz.astype(BF16)
        _accumulate(dg_ref, first, jnp.sum(dx1 * xh, axis=0, keepdims=True))
        _accumulate(db_ref, first, jnp.sum(dx1, axis=0, keepdims=True))
        bg_hook(True)

    row = pl.BlockSpec((tm, D), lambda i: (i, 0))
    vec = pl.BlockSpec((1, D), lambda i: (0, 0))
    du = pl.BlockSpec((tm, DFF), lambda i: (i, 0))
    return _hosted_call(
        body, bg, grid=(T // tm,),
        in_specs=[du, du, pl.BlockSpec((NS, D, cs), lambda i: (0, 0, 0), pipeline_mode=pl.Buffered(1)),
                  row, row, pl.BlockSpec((tm, LANES), lambda i: (i, 0)), vec],
        out_specs=[row, row, vec, vec],
        out_shape=[jax.ShapeDtypeStruct((T, D), F32), jax.ShapeDtypeStruct((T, D), BF16),
                   jax.ShapeDtypeStruct((1, D), F32), jax.ShapeDtypeStruct((1, D), F32)],
        scratch_shapes=[], operands=[dupre_g, dupre_v, w_up_sh, dz2, xh1, r1, ln1_g], name="mm_dx1_ln1_bwd")


def _transpose(x, name):
    R, C = x.shape
    tr = LANES if R % LANES == 0 else R

    def body(x_ref, o_ref):
        o_ref[...] = x_ref[...].T

    return pl.pallas_call(
        body, grid=(R // tr,), in_specs=[pl.BlockSpec((tr, C), lambda i: (i, 0))],
        out_specs=pl.BlockSpec((C, tr), lambda i: (0, i)), out_shape=jax.ShapeDtypeStruct((C, R), F32),
        compiler_params=_params(1), name=name)(x)


def _dh_cat(dq, dk, dv, dag, tm):
    T, AW = dq.shape
    CW2 = dag.shape[1]
    W = 3 * AW + CW2

    def body(dq_ref, dk_ref, dv_ref, dag_ref, dh_ref, cs_ref):
        for c, ref in enumerate((dq_ref, dk_ref, dv_ref)):
            dh_ref[:, c * AW:(c + 1) * AW] = ref[...]
        dg = dag_ref[...]
        dh_ref[:, 3 * AW:] = dg.astype(BF16)
        _accumulate(cs_ref, pl.program_id(0) == 0, jnp.sum(dg, axis=0, keepdims=True))

    row = pl.BlockSpec((tm, AW), lambda i: (i, 0))
    return pl.pallas_call(
        body, grid=(T // tm,),
        in_specs=[row] * 3 + [pl.BlockSpec((tm, CW2), lambda i: (i, 0))],
        out_specs=[pl.BlockSpec((tm, W), lambda i: (i, 0)), pl.BlockSpec((1, CW2), lambda i: (0, 0))],
        out_shape=[jax.ShapeDtypeStruct((T, W), BF16), jax.ShapeDtypeStruct((1, CW2), F32)],
        compiler_params=_params(1), name="dh_cat",
    )(dq, dk, dv, dag)


def _local_step(x, target, rel_table, w_in, b_in, conv_w, conv_b, conv_ln_g, conv_ln_b, attn_norm_g,
                conv_norm_g, staged, ln1_g, ln1_b, ffn_cw, ffn_cb, ln2_g, ln2_b, ids):
    B, S, D = x.shape
    T = B * S
    AW = attn_norm_g.shape[-1]
    CW = conv_norm_g.shape[-1]
    H = AW // HEAD_DIM
    DFF = staged[2].shape[0] * staged[2].shape[1]
    INW = 3 * AW + 2 * CW
    xf = x.reshape(T, D)
    tf = target.reshape(T, D)
    tm = _row_tile(T, 512)
    tm_s = _row_tile(T, 256)

    bucket_np, mask_np = _bucket_tables()
    bucket = jnp.asarray(bucket_np)
    band_mask = jnp.asarray(mask_np)
    bias_all = _bias_build(rel_table.T, bucket, band_mask).reshape(3, H, ATTN_BLOCK, 2 * ATTN_BLOCK)

    tn_qkv = _col_tile(3 * AW, 1152)
    qkv = _mm_plain(xf, w_in[:, :3 * AW], mode="nn", tm=tm, tn=tn_qkv, tk=D, out_dtype=BF16,
                    bias=b_in[:, :3 * AW], name="mm_qkv")
    ag = _mm_plain(xf, w_in[:, 3 * AW:], mode="nn", tm=tm, tn=2 * CW, tk=D, out_dtype=F32,
                   bias=b_in[:, 3 * AW:], name="mm_ag")

    attn, lse, w_out_g, w_up_sh, w_down_g = _attention_fwd(qkv, bias_all, B, S, AW, bg=_bg_gather(staged))
    w_out = w_out_g.reshape(D, D)
    w_down = w_down_g.reshape(DFF, D)
    mixed_a, r_attn = _attn_norm(attn, attn_norm_g, tm_s)
    mixed_c = _conv_fwd(ag, conv_w, conv_b, conv_ln_g, conv_ln_b, conv_norm_g, B, S, CW)
    mixed = jnp.concatenate([mixed_a, mixed_c], axis=1)

    def ln1_epilogue(acc, i, j, extra_refs, out_refs):
        x_ref, g_ref, b_ref = extra_refs
        x1, xh, r = _ln_fwd(acc + ALPHA * x_ref[...], g_ref[...], b_ref[...])
        out_refs[0][...] = x1
        out_refs[1][...] = x1.astype(BF16)
        out_refs[2][...] = xh
        out_refs[3][...] = jnp.broadcast_to(r, (tm_s, LANES))

    rowD = lambda i, j, k: (i, 0)
    vecD = lambda i, j, k: (0, 0)
    x1, x1b, xh1, r1 = _matmul(
        mixed, w_out, mode="nn", tm=tm_s, tn=D, tk=D,
        extras=[(xf, (tm_s, D), rowD), (ln1_g, (1, D), vecD), (ln1_b, (1, D), vecD)],
        outs=[((T, D), F32, (tm_s, D), rowD), ((T, D), BF16, (tm_s, D), rowD), ((T, D), F32, (tm_s, D), rowD),
              ((T, LANES), F32, (tm_s, LANES), rowD)],
        epilogue=ln1_epilogue, name="mm_out_ln1")

    NS, _, cs = w_up_sh.shape
    half = NS // 2

    act, upre = _ffn_fwd_fused(x1b, w_up_sh, ffn_cw, ffn_cb, B, S, DFF)

    def ln2_epilogue(acc, i, j, extra_refs, out_refs):
        x1_ref, g_ref, b_ref, t_ref = extra_refs
        dz_ref, dzb_ref, loss_ref, dg_ref, db_ref = out_refs
        g = g_ref[...]
        y, xh, r = _ln_fwd(acc + ALPHA * x1_ref[...], g, b_ref[...])
        diff = y - t_ref[...]
        row_loss = jnp.sum(diff * diff, axis=1, keepdims=True)
        tile_loss = jnp.sum(row_loss, axis=0, keepdims=True) * (0.5 / D)
        dy = diff * (1.0 / D)
        dz = _ln_bwd(dy, xh, r, g)
        dz_ref[...] = dz
        dzb_ref[...] = dz.astype(BF16)
        first = i == 0
        _accumulate(loss_ref, first, jnp.broadcast_to(tile_loss, (1, LANES)))
        _accumulate(dg_ref, first, jnp.sum(dy * xh, axis=0, keepdims=True))
        _accumulate(db_ref, first, jnp.sum(dy, axis=0, keepdims=True))

    dz2, dz2b, loss_part, d_ln2_g, d_ln2_b = _matmul(
        act, w_down, mode="nn", tm=tm, tn=D, tk=DFF,
        extras=[(x1, (tm, D), rowD), (ln2_g, (1, D), vecD), (ln2_b, (1, D), vecD), (tf, (tm, D), rowD)],
        outs=[((T, D), F32, (tm, D), rowD), ((T, D), BF16, (tm, D), rowD),
              ((1, LANES), F32, (1, LANES), vecD), ((1, D), F32, (1, D), vecD), ((1, D), F32, (1, D), vecD)],
        epilogue=ln2_epilogue, name="mm_down_ln2_loss")

    dupre_g, dupre_v, d_w_up_t, d_w_down, d_ffn_cw2, d_ffn_cb2 = _ffn_bwd_fused(
        x1b, dz2b, upre, w_down, ffn_cw, ffn_cb, B, S, DFF)
    d_w_up_t = d_w_up_t.reshape(NS, cs, D)
    d_ffn_cw = d_ffn_cw2.reshape(FFN_CONV_KERNEL, 2 * DFF)
    d_ffn_cb = d_ffn_cb2.reshape(1, 2 * DFF)
    tk_t = _row_tile(T, 512)

    early = [d_w_up_t, d_w_down.reshape(NS, DFF // NS, D)]
    dz1, dz1b, d_ln1_g, d_ln1_b, *sib_e = _dx1_ln1_bwd(dupre_g, dupre_v, w_up_sh, dz2, xh1, r1, ln1_g, tm,
                                                       bg=_bg_sibling_exchange(early))
    chip_e = [_pair_sum(g, s, ids, name="pair_sum_" + n) for g, s, n in zip(early, sib_e, ("w_up", "w_down"))]

    d_w_out = _mm_plain(mixed, dz1b, mode="tn", tm=D, tn=D, tk=tk_t, out_dtype=F32, name="mm_dw_out")
    early.append(d_w_out.reshape(NS, D // NS, D))
    dmixed, sib_out = _mm_plain(dz1b, w_out, mode="nt", tm=tm, tn=D, tk=D, out_dtype=F32, name="mm_dmixed",
                                bg=_bg_sibling_exchange(early[2:]))
    sib_e.append(sib_out)
    chip_e.append(_pair_sum(early[2], sib_out, ids, name="pair_sum_w_out"))

    dattn, dd, d_attn_norm_g = _attn_pre_bwd(dmixed, attn, r_attn, attn_norm_g, tm_s)
    dag, d_conv_w, d_conv_b, d_conv_ln_g, d_conv_ln_b, d_conv_norm_g = _conv_bwd(
        ag, dmixed, conv_w, conv_b, conv_ln_g, conv_ln_b, conv_norm_g, B, S, CW, D)

    dq, dk, dv, csq, csk, csv, dbias, *got_e = _attention_bwd(qkv, dattn, lse, dd, bias_all, B, S, AW,
                                                              bg=_bg_chip_exchange(chip_e))
    full_up, full_down, full_out = [_final_sum(g, s, r, ids, name="final_sum_" + n)
                                    for g, s, r, n in zip(early, sib_e, got_e, ("w_up", "w_down", "w_out"))]
    d_rel_table = _rel_grad(dbias.reshape(3, H, ATTN_BLOCK * 2 * ATTN_BLOCK), bucket).T
    dh, cs_ag = _dh_cat(dq, dk, dv, dag, tm_s)
    d_b_in = jnp.concatenate([csq, csk, csv, cs_ag], axis=1)

    d_w_in_t = _mm_plain(dh, xf, mode="tn", tm=_col_tile(INW, 1408), tn=D, tk=tk_t, out_dtype=F32, name="mm_dw_in")
    late = [d_w_in_t.reshape(NS, INW // NS, D)]
    sib_l = _sibling_exchange(late)
    chip_l = [_pair_sum(late[0], sib_l[0], ids, name="pair_sum_w_in")]
    small = dict(rel_table=d_rel_table, b_in=d_b_in, conv_w=d_conv_w, conv_b=d_conv_b, conv_ln_g=d_conv_ln_g,
                 conv_ln_b=d_conv_ln_b, attn_norm_g=d_attn_norm_g, conv_norm_g=d_conv_norm_g, ln1_g=d_ln1_g,
                 ln1_b=d_ln1_b, ffn_conv_w=d_ffn_cw, ffn_conv_b=d_ffn_cb, ln2_g=d_ln2_g, ln2_b=d_ln2_b)
    pack = _pack([loss_part] + [small[n] for n in SMALL_NAMES])

    def gx_epilogue(acc, i, j, extra_refs, out_refs):
        out_refs[0][...] = acc + ALPHA * extra_refs[0][...]

    grad_x, got_in, all_packs = _matmul(
        dh, w_in, mode="nt", tm=tm, tn=D, tk=INW, extras=[(dz1, (tm, D), rowD)],
        outs=[((T, D), F32, (tm, D), rowD)], epilogue=gx_epilogue, name="mm_grad_x",
        bg=_bg_chip_exchange(chip_l, pack))
    full_in = _final_sum(late[0], sib_l[0], got_in, ids, name="final_sum_w_in")
    return grad_x.reshape(B, S, D), [full_in, full_out, full_up, full_down], all_packs


def _place():
    return lax.axis_index("x"), lax.axis_index("y"), lax.axis_index("c")


CHIP_FLIPS = ((1, 0), (0, 1), (1, 1))


def _flip(v, f):
    return 1 - v if f else v


HBM_SPEC = pl.BlockSpec(memory_space=pl.ANY)
VMEM_SPEC = pl.BlockSpec(memory_space=pltpu.VMEM)
COMM_PARAMS = pltpu.CompilerParams(vmem_limit_bytes=VMEM_LIMIT)


def _gather_weights(big, small):
    nb, ns = len(big), len(small)

    def body(*refs):
        big_in = refs[:nb]
        small_in = refs[nb:nb + ns]
        big_out = refs[nb + ns:2 * nb + ns]
        small_out = refs[2 * nb + ns:2 * nb + 2 * ns]
        stages = refs[2 * nb + 2 * ns:3 * nb + 2 * ns]
        send_sems, recv_sems, local_sems = refs[3 * nb + 2 * ns:]
        x, y, c = _place()
        s_me = 2 * x + y
        sibling = (x, y, 1 - c)
        started, local_copies = [], []
        for a in range(nb):
            rh = big[a].shape[0] // 2
            lo = pl.multiple_of(c * rh, 16)
            stages[a][...] = big_in[a][pl.ds(lo, rh), :].astype(BF16)
            mine = big_out[a].at[s_me, pl.ds(lo, rh), :]
            loc = pltpu.make_async_copy(stages[a], mine, local_sems.at[a])
            loc.start()
            local_copies.append(loc)
            targets = [sibling] + [(_flip(x, fx), _flip(y, fy), c) for fx, fy in CHIP_FLIPS]
            for k, to in enumerate(targets):
                cp = pltpu.make_async_remote_copy(stages[a], mine, send_sems.at[a * 7 + k],
                                                  recv_sems.at[a * 7 + k], device_id=to, device_id_type=MESH)
                cp.start()
                started.append(cp)
        for a in range(ns):
            mine = small_out[a].at[s_me]
            loc = pltpu.make_async_copy(small_in[a], mine, local_sems.at[nb + a])
            loc.start()
            local_copies.append(loc)
            for k, (fx, fy) in enumerate(CHIP_FLIPS):
                cp = pltpu.make_async_remote_copy(small_in[a], mine, send_sems.at[nb * 7 + a * 3 + k],
                                                  recv_sems.at[nb * 7 + a * 3 + k],
                                                  device_id=(_flip(x, fx), _flip(y, fy), c), device_id_type=MESH)
                cp.start()
                started.append(cp)
        for a in range(nb):
            rh = big[a].shape[0] // 2
            lo = pl.multiple_of(c * rh, 16)
            for k, (fx, fy) in enumerate(CHIP_FLIPS):
                s_from = 2 * _flip(x, fx) + _flip(y, fy)
                got = big_out[a].at[s_from, pl.ds(lo, rh), :]
                pltpu.make_async_remote_copy(got, got, send_sems.at[a * 7 + 1 + k], recv_sems.at[a * 7 + 1 + k],
                                             device_id=sibling, device_id_type=MESH).wait_recv()
                fwd = pltpu.make_async_remote_copy(got, got, send_sems.at[a * 7 + 4 + k],
                                                   recv_sems.at[a * 7 + 4 + k], device_id=sibling,
                                                   device_id_type=MESH)
                fwd.start()
                started.append(fwd)
        for a in range(nb):
            rh = big[a].shape[0] // 2
            lo_sib = pl.multiple_of((1 - c) * rh, 16)
            for k in (0, 4, 5, 6):
                any_rows = big_out[a].at[s_me, pl.ds(lo_sib, rh), :]
                pltpu.make_async_remote_copy(any_rows, any_rows, send_sems.at[a * 7 + k], recv_sems.at[a * 7 + k],
                                             device_id=sibling, device_id_type=MESH).wait_recv()
        for a in range(ns):
            for k in range(3):
                pltpu.make_async_remote_copy(small_in[a], small_out[a].at[s_me], send_sems.at[nb * 7 + a * 3 + k],
                                             recv_sems.at[nb * 7 + a * 3 + k], device_id=sibling,
                                             device_id_type=MESH).wait_recv()
        for cp in started:
            cp.wait_send()
        for cp in local_copies:
            cp.wait()

    n_sem = nb * 7 + ns * 3
    out_shape = ([jax.ShapeDtypeStruct((N_SHARDS,) + w.shape, BF16) for w in big]
                 + [jax.ShapeDtypeStruct((N_SHARDS,) + w.shape, F32) for w in small])
    res = pl.pallas_call(
        body, in_specs=[VMEM_SPEC] * nb + [HBM_SPEC] * ns, out_specs=[HBM_SPEC] * (nb + ns),
        out_shape=out_shape,
        scratch_shapes=[pltpu.VMEM((w.shape[0] // 2, w.shape[1]), BF16) for w in big]
        + [pltpu.SemaphoreType.DMA((n_sem,)), pltpu.SemaphoreType.DMA((n_sem,)),
           pltpu.SemaphoreType.DMA((nb + ns,))],
        compiler_params=COMM_PARAMS, name="gather_weights",
    )(*big, *small)
    return res[:nb], res[nb:]


def _sibling_exchange(grads):
    n = len(grads)

    def body(*refs):
        g_in = refs[:n]
        got = refs[n:2 * n]
        send_sems, recv_sems = refs[2 * n:]
        x, y, c = _place()
        cps = []
        for a in range(n):
            rh = grads[a].shape[1] // 2
            lo = pl.multiple_of((1 - c) * rh, 8)
            cp = pltpu.make_async_remote_copy(g_in[a].at[:, pl.ds(lo, rh), :], got[a], send_sems.at[a],
                                              recv_sems.at[a], device_id=(x, y, 1 - c), device_id_type=MESH)
            cp.start()
            cps.append(cp)
        for cp in cps:
            cp.wait()

    return pl.pallas_call(
        body, in_specs=[HBM_SPEC] * n, out_specs=[HBM_SPEC] * n,
        out_shape=[jax.ShapeDtypeStruct((N_SHARDS, g.shape[1] // 2, g.shape[2]), F32) for g in grads],
        scratch_shapes=[pltpu.SemaphoreType.DMA((n,)), pltpu.SemaphoreType.DMA((n,))],
        compiler_params=COMM_PARAMS, name="sibling_exchange",
    )(*grads)


def _chip_exchange(chip_parts, pack):
    n = len(chip_parts)

    def body(*refs):
        parts = refs[:n]
        pack_ref = refs[n]
        got = refs[n + 1:2 * n + 1]
        all_packs = refs[2 * n + 1]
        send_sems, recv_sems, local_sem = refs[2 * n + 2:]
        x, y, c = _place()
        me = 4 * x + 2 * y + c
        cps = []
        for a in range(n):
            for k, (fx, fy) in enumerate(CHIP_FLIPS):
                px, py = _flip(x, fx), _flip(y, fy)
                cp = pltpu.make_async_remote_copy(parts[a].at[2 * px + py], got[a].at[k], send_sems.at[a * 3 + k],
                                                  recv_sems.at[a * 3 + k], device_id=(px, py, c),
                                                  device_id_type=MESH)
                cp.start()
                cps.append(cp)
        loc = pltpu.make_async_copy(pack_ref, all_packs.at[me], local_sem)
        loc.start()
        for m in range(1, N_DEV):
            to = (_flip(x, m & 4), _flip(y, m & 2), _flip(c, m & 1))
            cp = pltpu.make_async_remote_copy(pack_ref, all_packs.at[me], send_sems.at[n * 3 + m - 1],
                                              recv_sems.at[n * 3 + m - 1], device_id=to, device_id_type=MESH)
            cp.start()
            cps.append(cp)
        for cp in cps:
            cp.wait()
        loc.wait()

    rs = pack.shape[0]
    res = pl.pallas_call(
        body, in_specs=[HBM_SPEC] * (n + 1), out_specs=[HBM_SPEC] * (n + 1),
        out_shape=[jax.ShapeDtypeStruct((3,) + p.shape[1:], BF16) for p in chip_parts]
        + [jax.ShapeDtypeStruct((N_DEV, rs, LANES), F32)],
        scratch_shapes=[pltpu.SemaphoreType.DMA((n * 3 + N_DEV - 1,)), pltpu.SemaphoreType.DMA((n * 3 + N_DEV - 1,)),
                        pltpu.SemaphoreType.DMA],
        compiler_params=COMM_PARAMS, name="chip_exchange",
    )(*chip_parts, pack)
    return res[:n], res[n]


def _sibling_assemble(fulls):
    n = len(fulls)

    def body(*refs):
        full = refs[n:2 * n]
        send_sems, recv_sems = refs[2 * n:]
        x, y, c = _place()
        cps = []
        for a in range(n):
            rh = fulls[a].shape[0] // 2
            mine = full[a].at[pl.ds(pl.multiple_of(c * rh, 8), rh), :]
            cp = pltpu.make_async_remote_copy(mine, mine, send_sems.at[a], recv_sems.at[a],
                                              device_id=(x, y, 1 - c), device_id_type=MESH)
            cp.start()
            cps.append(cp)
        for cp in cps:
            cp.wait()

    return pl.pallas_call(
        body, in_specs=[HBM_SPEC] * n, out_specs=[HBM_SPEC] * n,
        out_shape=[jax.ShapeDtypeStruct(f.shape, F32) for f in fulls],
        input_output_aliases={a: a for a in range(n)},
        scratch_shapes=[pltpu.SemaphoreType.DMA((n,)), pltpu.SemaphoreType.DMA((n,))],
        compiler_params=COMM_PARAMS, name="sibling_assemble",
    )(*fulls)


def _remote(ref_src, ref_dst, send_sems, recv_sems, k, to):
    return pltpu.make_async_remote_copy(ref_src, ref_dst, send_sems.at[k], recv_sems.at[k], device_id=to,
                                        device_id_type=MESH)


def _stage_half(w, ids, name):
    R, C = w.shape
    rh = R // 2
    rt = _half_tile(rh)
    nt = rh // rt

    def body(ids_ref, w_ref, o_ref):
        o_ref[0] = w_ref[...].astype(BF16)

    grid_spec = pltpu.PrefetchScalarGridSpec(
        num_scalar_prefetch=1, grid=(nt,),
        in_specs=[pl.BlockSpec((rt, C), lambda i, ids: (ids[2] * nt + i, 0))],
        out_specs=pl.BlockSpec((1, rt, C), lambda i, ids: (2 * ids[0] + ids[1], ids[2] * nt + i, 0)))
    return pl.pallas_call(body, grid_spec=grid_spec, out_shape=jax.ShapeDtypeStruct((N_SHARDS, R, C), BF16),
                          compiler_params=_params(1), name=name)(ids, w)


def _bg_gather(staged):
    n = len(staged)

    def run(step, n_steps, ins, outs, send_sems, recv_sems, local_sems, post):
        x, y, c = _place()
        s_me = 2 * x + y
        sibling = (x, y, 1 - c)
        chips = [(_flip(x, fx), _flip(y, fy)) for fx, fy in CHIP_FLIPS]

        def rows(a, s, half):
            rh = staged[a].shape[1] // 2
            return outs[a].at[s, pl.ds(pl.multiple_of(half * rh, 16), rh), :]

        def copy(a, k, ref, to):
            return _remote(ref, ref, send_sems, recv_sems, a * 7 + k, to)

        if not post:
            @pl.when(step == 0)
            def _():
                for a in range(n):
                    mine = rows(a, s_me, c)
                    copy(a, 0, mine, sibling).start()
                    for k, (px, py) in enumerate(chips):
                        copy(a, 1 + k, mine, (px, py, c)).start()

            @pl.when(step == max(n_steps - 2, 0))
            def _():
                for a in range(n):
                    for k, (px, py) in enumerate(chips):
                        got = rows(a, 2 * px + py, c)
                        copy(a, 1 + k, got, sibling).wait_recv()
                        copy(a, 4 + k, got, sibling).start()
        else:
            @pl.when(step == n_steps - 1)
            def _():
                for a in range(n):
                    for k in (0, 4, 5, 6):
                        copy(a, k, rows(a, s_me, 1 - c), sibling).wait_recv()
                    for k in range(7):
                        copy(a, k, rows(a, s_me, c), sibling).wait_send()

    return _Background(staged, [jax.ShapeDtypeStruct(g.shape, g.dtype) for g in staged],
                       {a: a for a in range(n)}, 7 * n, run)


def _bg_sibling_exchange(grads):
    n = len(grads)

    def run(step, n_steps, ins, outs, send_sems, recv_sems, local_sems, post):
        x, y, c = _place()

        def copy(a):
            rh = grads[a].shape[1] // 2
            lo = pl.multiple_of((1 - c) * rh, 8)
            return _remote(ins[a].at[:, pl.ds(lo, rh), :], outs[a], send_sems, recv_sems, a, (x, y, 1 - c))

        if not post:
            @pl.when(step == 0)
            def _():
                for a in range(n):
                    copy(a).start()
        else:
            @pl.when(step == n_steps - 1)
            def _():
                for a in range(n):
                    copy(a).wait()

    return _Background(grads, [jax.ShapeDtypeStruct((N_SHARDS, g.shape[1] // 2, g.shape[2]), F32) for g in grads],
                       {}, n, run)


def _bg_chip_exchange(chip_parts, pack=None):
    n = len(chip_parts)

    def run(step, n_steps, ins, outs, send_sems, recv_sems, local_sems, post):
        x, y, c = _place()
        me = 4 * x + 2 * y + c

        def copies():
            cps = []
            for a in range(n):
                for k, (fx, fy) in enumerate(CHIP_FLIPS):
                    px, py = _flip(x, fx), _flip(y, fy)
                    cps.append(_remote(ins[a].at[2 * px + py], outs[a].at[k], send_sems, recv_sems, a * 3 + k,
                                       (px, py, c)))
            if pack is not None:
                for m in range(1, N_DEV):
                    to = (_flip(x, m & 4), _flip(y, m & 2), _flip(c, m & 1))
                    cps.append(_remote(ins[n], outs[n].at[me], send_sems, recv_sems, n * 3 + m - 1, to))
            return cps

        def local():
            return pltpu.make_async_copy(ins[n], outs[n].at[me], local_sems.at[0])

        if not post:
            @pl.when(step == 0)
            def _():
                for cp in copies():
                    cp.start()
                if pack is not None:
                    local().start()
        else:
            @pl.when(step == n_steps - 1)
            def _():
                for cp in copies():
                    cp.wait()
                if pack is not None:
                    local().wait()

    in_arrays = list(chip_parts) + ([pack] if pack is not None else [])
    out_shapes = [jax.ShapeDtypeStruct((3,) + p.shape[1:], BF16) for p in chip_parts]
    if pack is not None:
        out_shapes.append(jax.ShapeDtypeStruct((N_DEV, pack.shape[0], LANES), F32))
    return _Background(in_arrays, out_shapes, {}, n * 3 + N_DEV - 1, run)


def _half_tile(rh, mult=16, want=256):
    best = None
    for t in range(mult, min(rh, want) + 1, mult):
        if rh % t == 0:
            best = t
    return best if best is not None else rh


def _pair_sum(g, sib, ids, name):
    _, R, C = g.shape
    rh = R // 2
    rt = _half_tile(rh)
    nt = rh // rt

    def body(ids_ref, g_ref, s_ref, o_ref):
        o_ref[...] = (g_ref[...] + s_ref[...]).astype(BF16)

    grid_spec = pltpu.PrefetchScalarGridSpec(
        num_scalar_prefetch=1, grid=(N_SHARDS, nt),
        in_specs=[pl.BlockSpec((1, rt, C), lambda s, i, ids: (s, ids[2] * nt + i, 0)),
                  pl.BlockSpec((1, rt, C), lambda s, i, ids: (s, i, 0))],
        out_specs=pl.BlockSpec((1, rt, C), lambda s, i, ids: (s, i, 0)))
    return pl.pallas_call(body, grid_spec=grid_spec, out_shape=jax.ShapeDtypeStruct((N_SHARDS, rh, C), BF16),
                          compiler_params=_params(2), name=name)(ids, g, sib)


def _final_sum(g, sib, got, ids, name):
    _, R, C = g.shape
    rh = R // 2
    rt = _half_tile(rh)
    nt = rh // rt

    def body(ids_ref, g_ref, s_ref, r_ref, o_ref):
        tot = g_ref[0] + s_ref[0]
        for k in range(3):
            tot = tot + r_ref[k].astype(F32)
        o_ref[...] = tot

    grid_spec = pltpu.PrefetchScalarGridSpec(
        num_scalar_prefetch=1, grid=(nt,),
        in_specs=[pl.BlockSpec((1, rt, C), lambda i, ids: (2 * ids[0] + ids[1], ids[2] * nt + i, 0)),
                  pl.BlockSpec((1, rt, C), lambda i, ids: (2 * ids[0] + ids[1], i, 0)),
                  pl.BlockSpec((3, rt, C), lambda i, ids: (0, i, 0))],
        out_specs=pl.BlockSpec((rt, C), lambda i, ids: (ids[2] * nt + i, 0)))
    return pl.pallas_call(body, grid_spec=grid_spec, out_shape=jax.ShapeDtypeStruct((R, C), F32),
                          compiler_params=_params(1), name=name)(ids, g, sib, got)


def _sum_packs(all_packs):
    def body(p_ref, o_ref):
        tot = p_ref[0]
        for i in range(1, N_DEV):
            tot = tot + p_ref[i]
        o_ref[...] = tot

    return pl.pallas_call(body, in_specs=[VMEM_SPEC], out_specs=VMEM_SPEC,
                          out_shape=jax.ShapeDtypeStruct(all_packs.shape[1:], F32), name="sum_packs")(all_packs)


def _adamw(w, g, m, v, name):
    R, C = w.shape
    rt = _half_tile(R, mult=8, want=256)

    def body(w_ref, g_ref, m_ref, v_ref, d_ref, nm_ref, nv_ref):
        gg = g_ref[...]
        nm = ADAM_B1 * m_ref[...] + (1.0 - ADAM_B1) * gg
        nv = ADAM_B2 * v_ref[...] + (1.0 - ADAM_B2) * (gg * gg)
        m_hat = nm / (1.0 - ADAM_B1 ** ADAM_STEP)
        v_hat = nv / (1.0 - ADAM_B2 ** ADAM_STEP)
        d_ref[...] = -ADAM_LR * (m_hat / (jnp.sqrt(v_hat) + ADAM_EPS) + ADAM_WD * w_ref[...])
        nm_ref[...] = nm
        nv_ref[...] = nv

    spec = pl.BlockSpec((rt, C), lambda i: (i, 0))
    return pl.pallas_call(body, grid=(R // rt,), in_specs=[spec] * 4, out_specs=[spec] * 3,
                          out_shape=[jax.ShapeDtypeStruct((R, C), F32)] * 3,
                          compiler_params=_params(1), name=name)(w, g, m, v)


def _adamw_update(w, g, m, v):
    nm = ADAM_B1 * m + (1.0 - ADAM_B1) * g
    nv = ADAM_B2 * v + (1.0 - ADAM_B2) * (g * g)
    m_hat = nm / (1.0 - ADAM_B1 ** ADAM_STEP)
    v_hat = nv / (1.0 - ADAM_B2 ** ADAM_STEP)
    return -ADAM_LR * (m_hat / (jnp.sqrt(v_hat) + ADAM_EPS) + ADAM_WD * w), nm, nv


def _adamw_many(ws, gs, ms, vs, name):
    n = len(ws)

    def body(*refs):
        for i in range(n):
            d, nm, nv = _adamw_update(refs[i][...], refs[n + i][...], refs[2 * n + i][...], refs[3 * n + i][...])
            refs[4 * n + i][...] = d
            refs[5 * n + i][...] = nm
            refs[6 * n + i][...] = nv

    return pl.pallas_call(body, in_specs=[VMEM_SPEC] * (4 * n), out_specs=[VMEM_SPEC] * (3 * n),
                          out_shape=[jax.ShapeDtypeStruct(w.shape, F32) for w in ws] * 3, name=name,
                          )(*ws, *gs, *ms, *vs)


def _pack(pieces):
    rows = []
    for p in pieces:
        flat = p.reshape(-1)
        pad = (-flat.shape[0]) % LANES
        if pad:
            flat = jnp.concatenate([flat, jnp.zeros((pad,), F32)])
        rows.append(flat.reshape(-1, LANES))
    total = sum(r.shape[0] for r in rows)
    pad_rows = (-total) % 8
    if pad_rows:
        rows.append(jnp.zeros((pad_rows, LANES), F32))
    return jnp.concatenate(rows, axis=0)


def _unpack(buf, shapes):
    out, r0 = [], 0
    for shp in shapes:
        n = int(np.prod(shp))
        nr = -(-n // LANES)
        out.append(buf[r0:r0 + nr].reshape(-1)[:n].reshape(shp))
        r0 += nr
    return out


SMALL_NAMES = ("rel_table", "b_in", "conv_w", "conv_b", "conv_ln_g", "conv_ln_b", "attn_norm_g", "conv_norm_g",
               "ln1_g", "ln1_b", "ffn_conv_w", "ffn_conv_b", "ln2_g", "ln2_b")
BIG_NAMES = ("w_in", "w_out", "w_up", "w_down")
WEIGHT_ORDER = ("rel_table", "w_in", "b_in", "conv_w", "conv_b", "conv_ln_g", "conv_ln_b", "attn_norm_g",
                "conv_norm_g", "w_out", "ln1_g", "ln1_b", "w_up", "ffn_conv_w", "ffn_conv_b", "w_down",
                "ln2_g", "ln2_b")


def kernel(x, rel_table, w_in, b_in, conv_w, conv_b, conv_ln_g, conv_ln_b, attn_norm_g, conv_norm_g, w_out, ln1_g, ln1_b, w_up, ffn_conv_w, ffn_conv_b, w_down, ln2_g, ln2_b, loss_target, m_rel_table, m_w_in, m_b_in, m_conv_w, m_conv_b, m_conv_ln_g, m_conv_ln_b, m_attn_norm_g, m_conv_norm_g, m_w_out, m_ln1_g, m_ln1_b, m_w_up, m_ffn_conv_w, m_ffn_conv_b, m_w_down, m_ln2_g, m_ln2_b, v_rel_table, v_w_in, v_b_in, v_conv_w, v_conv_b, v_conv_ln_g, v_conv_ln_b, v_attn_norm_g, v_conv_norm_g, v_w_out, v_ln1_g, v_ln1_b, v_w_up, v_ffn_conv_w, v_ffn_conv_b, v_w_down, v_ln2_g, v_ln2_b):
    args = dict(locals())
    weights = {n: args[n] for n in WEIGHT_ORDER}
    moms = {n: args["m_" + n] for n in WEIGHT_ORDER}
    vels = {n: args["v_" + n] for n in WEIGHT_ORDER}
    xi, yi, ci = _place()
    ids = jnp.stack([xi, yi, ci]).astype(jnp.int32)
    shard = 2 * xi + yi
    D = x.shape[-1]
    DFF = w_down.shape[1] * N_SHARDS
    CW = conv_norm_g.shape[-1]

    (g_in,), (g_cw, g_fcw) = _gather_weights([w_in[0]], [conv_w[0], ffn_conv_w[0]])
    cols = lambda t: jnp.transpose(t, (1, 0, 2)).reshape(t.shape[1], N_SHARDS * t.shape[2])
    staged = [_stage_half(w[0], ids, name="stage_" + n) for w, n in ((w_out, "w_out"), (w_up, "w_up"),
                                                                     (w_down, "w_down"))]

    grad_x, fulls, all_packs = _local_step(
        x, loss_target, rel_table, cols(g_in), b_in, cols(g_cw), conv_b, conv_ln_g, conv_ln_b, attn_norm_g,
        conv_norm_g, staged, ln1_g, ln1_b, cols(g_fcw), ffn_conv_b, ln2_g, ln2_b, ids)
    big_grads = dict(zip(BIG_NAMES, _sibling_assemble(fulls)))
    for n in ("w_in", "w_up"):
        big_grads[n] = _transpose(big_grads[n], name="transpose_d" + n)

    summed = _sum_packs(all_packs)
    full_shapes = {n: weights[n].shape for n in SMALL_NAMES}
    full_shapes["conv_w"] = (1, CONV_KERNEL, CW)
    full_shapes["ffn_conv_w"] = (1, FFN_CONV_KERNEL, 2 * DFF)
    un = _unpack(summed, [(1, LANES)] + [full_shapes[n] for n in SMALL_NAMES])
    loss = un[0][0, 0]
    small_grads = dict(zip(SMALL_NAMES, un[1:]))
    for n in ("conv_w", "ffn_conv_w"):
        width = weights[n].shape[-1]
        small_grads[n] = lax.dynamic_slice_in_dim(small_grads[n], shard * width, width, axis=2)

    grads, delta, new_m, new_v = {}, {}, {}, {}
    for n in BIG_NAMES:
        shp = weights[n].shape
        g2 = big_grads[n]
        d, nm, nv = _adamw(weights[n][0], g2, moms[n][0], vels[n][0], name="adamw_" + n)
        grads[n], delta[n], new_m[n], new_v[n] = (t.reshape(shp) for t in (g2, d, nm, nv))
    pick = lambda src: [src[n] for n in SMALL_NAMES]
    small_out = _adamw_many(pick(weights), pick(small_grads), pick(moms), pick(vels), name="adamw_small")
    ns = len(SMALL_NAMES)
    for tgt, part in ((delta, small_out[:ns]), (new_m, small_out[ns:2 * ns]), (new_v, small_out[2 * ns:])):
        tgt.update(zip(SMALL_NAMES, part))
    grads.update(small_grads)

    return (loss, grad_x, *[grads[n] for n in WEIGHT_ORDER], *[delta[n] for n in WEIGHT_ORDER],
            *[new_m[n] for n in WEIGHT_ORDER], *[new_v[n] for n in WEIGHT_ORDER])
```

```python
import functools
import math

import numpy as np
import jax
import jax.numpy as jnp
from jax import lax
from jax.experimental import pallas as pl
from jax.experimental.pallas import tpu as pltpu

F32 = jnp.float32
BF16 = jnp.bfloat16
MESH = pl.DeviceIdType.MESH

HEAD_DIM = 64
LANES = 128
ATTN_BLOCK = 128
DILATED_CONFIGS = ((128, 1), (512, 4), (2048, 16))
CONV_KERNEL = 31
FFN_CONV_KERNEL = 3
REL_BUCKETS = 32
REL_MAX_DIST = 2048
DEPTH = 1
ALPHA = (2 * DEPTH) ** 0.25
LN_EPS = 1e-5
NEG_INF = -1e30
QK_SCALE = 1.0 / math.sqrt(HEAD_DIM)
ADAM_LR = 0.001
ADAM_B1 = 0.9
ADAM_B2 = 0.999
ADAM_EPS = 1e-08
ADAM_WD = 0.01
ADAM_STEP = 10
VMEM_LIMIT = 52 * 1024 * 1024
FFN_COLS = 128
N_SHARDS = 4
N_DEV = 8


def _params(n_axes):
    return pltpu.CompilerParams(dimension_semantics=("arbitrary",) * n_axes,
                                vmem_limit_bytes=VMEM_LIMIT)


MM_DIMS = {"nn": (((1,), (0,)), ((), ())), "nt": (((1,), (1,)), ((), ())), "tn": (((0,), (0,)), ((), ()))}


class _Background:
    def __init__(self, in_arrays, out_shapes, aliases, n_sems, run, n_local=1):
        self.in_arrays, self.out_shapes, self.aliases = list(in_arrays), list(out_shapes), dict(aliases)
        self.n_sems, self.n_local, self.run = n_sems, n_local, run

    def scratch(self):
        return [pltpu.SemaphoreType.DMA((self.n_sems,)), pltpu.SemaphoreType.DMA((self.n_sems,)),
                pltpu.SemaphoreType.DMA((self.n_local,))]


def _hosted_call(body, bg, *, grid, in_specs, out_specs, out_shape, scratch_shapes, operands, name):
    n_in, n_out, n_scr = len(in_specs), len(out_specs), len(scratch_shapes)
    if bg is None:
        return pl.pallas_call(lambda *refs: body(refs, lambda post: None), grid=grid, in_specs=in_specs,
                              out_specs=out_specs, out_shape=out_shape, scratch_shapes=scratch_shapes,
                              compiler_params=_params(len(grid)), name=name)(*operands)
    nb_in, nb_out = len(bg.in_arrays), len(bg.out_shapes)
    n_steps = int(np.prod(grid))

    def full_body(*refs):
        own = refs[:n_in] + refs[n_in + nb_in:n_in + nb_in + n_out] \
            + refs[n_in + nb_in + n_out + nb_out:n_in + nb_in + n_out + nb_out + n_scr]
        bg_in = refs[n_in:n_in + nb_in]
        bg_out = refs[n_in + nb_in + n_out:n_in + nb_in + n_out + nb_out]
        sems = refs[n_in + nb_in + n_out + nb_out + n_scr:]
        step = pl.program_id(0)
        for ax in range(1, len(grid)):
            step = step * grid[ax] + pl.program_id(ax)

        def hook(post):
            bg.run(step, n_steps, bg_in, bg_out, *sems, post)

        body(own, hook)

    res = pl.pallas_call(
        full_body, grid=grid, in_specs=list(in_specs) + [HBM_SPEC] * nb_in,
        out_specs=list(out_specs) + [HBM_SPEC] * nb_out, out_shape=list(out_shape) + bg.out_shapes,
        input_output_aliases={n_in + a: n_out + o for a, o in bg.aliases.items()},
        scratch_shapes=list(scratch_shapes) + bg.scratch(), compiler_params=_params(len(grid)), name=name,
    )(*operands, *bg.in_arrays)
    return res


def _matmul_general(ins, part_fn, *, grid, tm, tn, outs, epilogue, extras=(), name, bg=None):
    nk = grid[2]
    n_in, n_extra = len(ins), len(extras)

    def body(refs, bg_hook):
        in_refs = refs[:n_in]
        rest = refs[n_in:]
        extra_refs = rest[:n_extra]
        out_refs = rest[n_extra:n_extra + len(outs)]
        acc_ref = rest[-1]
        i, j, k = pl.program_id(0), pl.program_id(1), pl.program_id(2)
        bg_hook(False)
        part = part_fn(in_refs, i, j, k)
        if nk == 1:
            epilogue(part, i, j, extra_refs, out_refs)
        else:
            @pl.when(k == 0)
            def _():
                acc_ref[...] = part

            @pl.when(k > 0)
            def _():
                acc_ref[...] += part

            @pl.when(k == nk - 1)
            def _():
                epilogue(acc_ref[...], i, j, extra_refs, out_refs)
        bg_hook(True)

    in_specs = [pl.BlockSpec(bs, im) for (_, bs, im) in list(ins) + list(extras)]
    out_specs = [pl.BlockSpec(bs, im) for (_, _, bs, im) in outs]
    out_shape = [jax.ShapeDtypeStruct(s, d) for (s, d, _, _) in outs]
    return _hosted_call(body, bg, grid=grid, in_specs=in_specs, out_specs=out_specs, out_shape=out_shape,
                        scratch_shapes=[pltpu.VMEM((tm, tn), F32)],
                        operands=[e[0] for e in ins] + [e[0] for e in extras], name=name)


def _dot(a, b, mode):
    return lax.dot_general(a.astype(BF16), b.astype(BF16), MM_DIMS[mode], preferred_element_type=F32)


def _matmul(a, b, *, mode, tm, tn, tk, outs, epilogue, extras=(), name, bg=None):
    if mode == "tn":
        K, M = a.shape
        N = b.shape[1]
        ins = [(a, (tk, tm), lambda i, j, k: (k, i)), (b, (tk, tn), lambda i, j, k: (k, j))]
    elif mode == "nt":
        M, K = a.shape
        N = b.shape[0]
        ins = [(a, (tm, tk), lambda i, j, k: (i, k)), (b, (tn, tk), lambda i, j, k: (j, k))]
    else:
        M, K = a.shape
        N = b.shape[1]
        ins = [(a, (tm, tk), lambda i, j, k: (i, k)), (b, (tk, tn), lambda i, j, k: (k, j))]
    assert M % tm == 0 and N % tn == 0 and K % tk == 0, (name, M, N, K, tm, tn, tk)

    def part_fn(in_refs, i, j, k):
        return _dot(in_refs[0][...], in_refs[1][...], mode)

    return _matmul_general(ins, part_fn, grid=(M // tm, N // tn, K // tk), tm=tm, tn=tn, outs=outs,
                           epilogue=epilogue, extras=extras, name=name, bg=bg)


def _plain_out(M, N, tm, tn, dtype):
    return ((M, N), dtype, (tm, tn), lambda i, j, k: (i, j))


def _mm_plain(a, b, *, mode, tm, tn, tk, out_dtype, name, bias=None, bg=None):
    if mode == "tn":
        M, N = a.shape[1], b.shape[1]
    elif mode == "nt":
        M, N = a.shape[0], b.shape[0]
    else:
        M, N = a.shape[0], b.shape[1]
    extras = []
    if bias is not None:
        extras.append((bias, (1, tn), lambda i, j, k: (0, j)))

    def epilogue(acc, i, j, extra_refs, out_refs):
        if bias is not None:
            acc = acc + extra_refs[0][...]
        out_refs[0][...] = acc.astype(out_dtype)

    res = _matmul(a, b, mode=mode, tm=tm, tn=tn, tk=tk, outs=[_plain_out(M, N, tm, tn, out_dtype)],
                  epilogue=epilogue, extras=extras, name=name, bg=bg)
    return res[0] if bg is None else res


def _row_tile(T, want):
    t = min(T, want)
    while T % t:
        t //= 2
    return t


def _col_tile(N, want):
    if N <= want:
        return N
    best = None
    for c in range(LANES, want + 1, LANES):
        if N % c == 0:
            best = c
    return best if best is not None else N


def _accumulate(ref, first, val):
    @pl.when(first)
    def _():
        ref[...] = val

    @pl.when(jnp.logical_not(first))
    def _():
        ref[...] += val


def _ln_fwd(z, g, b):
    mu = jnp.mean(z, axis=-1, keepdims=True)
    zc = z - mu
    var = jnp.mean(zc * zc, axis=-1, keepdims=True)
    r = lax.rsqrt(var + LN_EPS)
    xh = zc * r
    return xh * g + b, xh, r


def _ln_bwd(dy, xh, r, g):
    dxh = dy * g
    m1 = jnp.mean(dxh, axis=-1, keepdims=True)
    m2 = jnp.mean(dxh * xh, axis=-1, keepdims=True)
    return r * (dxh - m1 - xh * m2)


def _sigmoid(x):
    return 1.0 / (1.0 + jnp.exp(-x))


def _shift_down(x, s, row):
    if s == 0:
        return x
    rolled = pltpu.roll(x, s, 0)
    nfix = -(-s // 8) * 8
    head = jnp.where(row[:nfix] >= s, rolled[:nfix], 0.0)
    return jnp.concatenate([head, rolled[nfix:]], axis=0)


def _shift_up(x, s, row):
    if s == 0:
        return x
    n = x.shape[0]
    rolled = pltpu.roll(x, n - s, 0)
    nfix = -(-s // 8) * 8
    tail = jnp.where(row[n - nfix:] < n - s, rolled[n - nfix:], 0.0)
    return jnp.concatenate([rolled[:n - nfix], tail], axis=0)


def _bucket_tables():
    exact = REL_BUCKETS // 2
    qi = np.arange(ATTN_BLOCK)[:, None]
    kj = np.arange(2 * ATTN_BLOCK)[None, :]
    steps = qi + ATTN_BLOCK - kj
    buckets, masks = [], []
    for window, dilation in DILATED_CONFIGS:
        max_steps = window // dilation
        band = (steps >= 0) & (steps <= max_steps)
        dist = np.maximum(steps, 0) * dilation
        d_f = np.maximum(dist, 1).astype(np.float32)
        large = exact + (np.log(d_f / np.float32(exact)) / np.float32(math.log(REL_MAX_DIST / exact))
                         * np.float32(REL_BUCKETS - exact)).astype(np.int32)
        large = np.minimum(large, REL_BUCKETS - 1)
        bucket = np.where(dist < exact, dist, large).astype(np.int32)
        buckets.append(bucket.reshape(1, -1))
        masks.append(np.where(band, 0.0, NEG_INF).astype(np.float32).reshape(1, -1))
    return np.stack(buckets), np.stack(masks)


def _split_hi_lo(x):
    hi = x.astype(BF16)
    lo = (x - hi.astype(F32)).astype(BF16)
    return hi, lo


def _bias_build(rel_table_t, bucket, mask):
    H = rel_table_t.shape[0]
    n = bucket.shape[-1]

    def body(t_ref, bkt_ref, mask_ref, o_ref):
        onehot = (lax.broadcasted_iota(jnp.int32, (REL_BUCKETS, n), 0) == bkt_ref[0]).astype(BF16)
        t = t_ref[...]
        t1 = t.astype(BF16)
        r1 = t - t1.astype(F32)
        t2 = r1.astype(BF16)
        t3 = (r1 - t2.astype(F32)).astype(BF16)
        acc = jnp.dot(t1, onehot, preferred_element_type=F32)
        acc = acc + jnp.dot(t2, onehot, preferred_element_type=F32)
        acc = acc + jnp.dot(t3, onehot, preferred_element_type=F32)
        o_ref[0] = acc + mask_ref[0]

    return pl.pallas_call(
        body, grid=(3,),
        in_specs=[pl.BlockSpec((H, REL_BUCKETS), lambda b: (0, 0)),
                  pl.BlockSpec((1, 1, n), lambda b: (b, 0, 0)),
                  pl.BlockSpec((1, 1, n), lambda b: (b, 0, 0))],
        out_specs=pl.BlockSpec((1, H, n), lambda b: (b, 0, 0)),
        out_shape=jax.ShapeDtypeStruct((3, H, n), F32),
        compiler_params=_params(1), name="bias_build",
    )(rel_table_t, bucket, mask)


def _rel_grad(dbias, bucket):
    H = dbias.shape[1]
    n = bucket.shape[-1]
    dims = (((1,), (1,)), ((), ()))

    def body(d_ref, bkt_ref, o_ref):
        b = pl.program_id(0)
        onehot = (lax.broadcasted_iota(jnp.int32, (REL_BUCKETS, n), 0) == bkt_ref[0]).astype(BF16)
        d = d_ref[0]
        d1 = d.astype(BF16)
        r1 = d - d1.astype(F32)
        d2 = r1.astype(BF16)
        d3 = (r1 - d2.astype(F32)).astype(BF16)
        acc = lax.dot_general(d1, onehot, dims, preferred_element_type=F32)
        acc = acc + lax.dot_general(d2, onehot, dims, preferred_element_type=F32)
        acc = acc + lax.dot_general(d3, onehot, dims, preferred_element_type=F32)
        _accumulate(o_ref, b == 0, acc)

    return pl.pallas_call(
        body, grid=(3,),
        in_specs=[pl.BlockSpec((1, H, n), lambda b: (b, 0, 0)),
                  pl.BlockSpec((1, 1, n), lambda b: (b, 0, 0))],
        out_specs=pl.BlockSpec((H, REL_BUCKETS), lambda b: (0, 0)),
        out_shape=jax.ShapeDtypeStruct((H, REL_BUCKETS), F32),
        compiler_params=_params(1), name="rel_grad",
    )(dbias, bucket)


def _attn_specs(B, S, AW, d):
    L = S // d
    HP = AW // LANES
    W3 = 3 * HP
    q_spec = pl.BlockSpec((1, L, LANES), lambda h, b, r: (b, 0, r * W3 + h))
    k_spec = pl.BlockSpec((1, L, LANES), lambda h, b, r: (b, 0, r * W3 + HP + h))
    v_spec = pl.BlockSpec((1, L, LANES), lambda h, b, r: (b, 0, r * W3 + 2 * HP + h))
    o_spec = pl.BlockSpec((1, L, LANES), lambda h, b, r: (b, 0, r * HP + h))
    bias_spec = pl.BlockSpec((2, ATTN_BLOCK, 2 * ATTN_BLOCK), lambda h, b, r: (h, 0, 0))
    return L, HP, q_spec, k_spec, v_spec, o_spec, bias_spec


def _attn_fwd(qkv, bias, B, S, AW, d, name):
    L, HP, q_spec, k_spec, v_spec, o_spec, bias_spec = _attn_specs(B, S, AW, d)
    nb = L // ATTN_BLOCK
    nt = (((1,), (1,)), ((), ()))

    def body(q_ref, k_ref, v_ref, b_ref, o_ref, lse_ref):
        head0 = lax.broadcasted_iota(jnp.int32, (1, LANES), 1) < HEAD_DIM

        def block(n, first):
            qs = pl.multiple_of(n * ATTN_BLOCK, ATTN_BLOCK)
            q = q_ref[0, pl.ds(qs, ATTN_BLOCK), :]
            if first:
                kk = k_ref[0, pl.ds(0, ATTN_BLOCK), :]
                vv = v_ref[0, pl.ds(0, ATTN_BLOCK), :]
            else:
                ks = pl.multiple_of(n * ATTN_BLOCK - ATTN_BLOCK, ATTN_BLOCK)
                kk = k_ref[0, pl.ds(ks, 2 * ATTN_BLOCK), :]
                vv = v_ref[0, pl.ds(ks, 2 * ATTN_BLOCK), :]
            outs, lses = [], []
            for e in range(2):
                msk = head0 if e == 0 else jnp.logical_not(head0)
                qe = jnp.where(msk, q, jnp.zeros_like(q))
                s = lax.dot_general(qe, kk, nt, preferred_element_type=F32) * QK_SCALE
                s = s + (b_ref[e, :, ATTN_BLOCK:] if first else b_ref[e])
                m = jnp.max(s, axis=-1, keepdims=True)
                p = jnp.exp(s - m)
                l = jnp.sum(p, axis=-1, keepdims=True)
                o = jnp.dot(p.astype(BF16), vv, preferred_element_type=F32)
                outs.append(o / l)
                lses.append(jnp.broadcast_to(m + jnp.log(l), (ATTN_BLOCK, LANES)))
            o_ref[0, pl.ds(qs, ATTN_BLOCK), :] = jnp.where(head0, outs[0], outs[1])
            lse_ref[0, pl.ds(qs, ATTN_BLOCK), :] = jnp.where(head0, lses[0], lses[1])

        block(0, True)
        if nb > 1:
            def loop(n, c):
                block(n, False)
                return c
            lax.fori_loop(1, nb, loop, 0)

    qv = qkv.reshape(B, L, d * 3 * AW)
    o, lse = pl.pallas_call(
        body, grid=(HP, B, d), in_specs=[q_spec, k_spec, v_spec, bias_spec],
        out_specs=[o_spec, o_spec],
        out_shape=[jax.ShapeDtypeStruct((B, L, d * AW), F32)] * 2,
        compiler_params=_params(3), name=name,
    )(qv, qv, qv, bias)
    return o.reshape(B * S, AW), lse.reshape(B * S, AW)


def _attn_bwd(qkv, do, lse, dd, bias, B, S, AW, d, name):
    L, HP, q_spec, k_spec, v_spec, o_spec, bias_spec = _attn_specs(B, S, AW, d)
    nb = L // ATTN_BLOCK
    nt = (((1,), (1,)), ((), ()))
    tn = (((0,), (0,)), ((), ()))

    def body(q_ref, k_ref, v_ref, do_ref, lse_ref, dd_ref, b_ref, dq_ref, dk_ref, dv_ref, db_ref):
        head0 = lax.broadcasted_iota(jnp.int32, (1, LANES), 1) < HEAD_DIM
        first_step = jnp.logical_and(pl.program_id(1) == 0, pl.program_id(2) == 0)

        @pl.when(first_step)
        def _():
            db_ref[...] = jnp.zeros_like(db_ref)

        dk_ref[...] = jnp.zeros_like(dk_ref)
        dv_ref[...] = jnp.zeros_like(dv_ref)

        def block(n, first):
            qs = pl.multiple_of(n * ATTN_BLOCK, ATTN_BLOCK)
            nkeys = ATTN_BLOCK if first else 2 * ATTN_BLOCK
            ks = 0 if first else pl.multiple_of(n * ATTN_BLOCK - ATTN_BLOCK, ATTN_BLOCK)
            q = q_ref[0, pl.ds(qs, ATTN_BLOCK), :]
            kk = k_ref[0, pl.ds(ks, nkeys), :]
            vv = v_ref[0, pl.ds(ks, nkeys), :]
            dout = do_ref[0, pl.ds(qs, ATTN_BLOCK), :]
            lse_b = lse_ref[0, pl.ds(qs, ATTN_BLOCK), :]
            dd_b = dd_ref[0, pl.ds(qs, ATTN_BLOCK), :]
            dq = jnp.zeros((ATTN_BLOCK, LANES), F32)
            dkk = jnp.zeros((nkeys, LANES), F32)
            dvv = jnp.zeros((nkeys, LANES), F32)
            for e in range(2):
                msk = head0 if e == 0 else jnp.logical_not(head0)
                c0 = e * HEAD_DIM
                qe = jnp.where(msk, q, jnp.zeros_like(q))
                doe = jnp.where(msk, dout, jnp.zeros_like(dout))
                kke = jnp.where(msk, kk, jnp.zeros_like(kk))
                s = lax.dot_general(qe, kk, nt, preferred_element_type=F32) * QK_SCALE
                s = s + (b_ref[e, :, ATTN_BLOCK:] if first else b_ref[e])
                p = jnp.exp(s - lse_b[:, c0:c0 + 1])
                dp = lax.dot_general(doe, vv, nt, preferred_element_type=F32)
                ds = p * (dp - dd_b[:, c0:c0 + 1])
                if first:
                    db_ref[e, :, ATTN_BLOCK:] += ds
                else:
                    db_ref[e] += ds
                dsb = (ds * QK_SCALE).astype(BF16)
                dq = dq + jnp.dot(dsb, kke, preferred_element_type=F32)
                dkk = dkk + lax.dot_general(dsb, qe, tn, preferred_element_type=F32)
                dvv = dvv + lax.dot_general(p.astype(BF16), doe, tn, preferred_element_type=F32)
            dq_ref[0, pl.ds(qs, ATTN_BLOCK), :] = dq
            dk_ref[0, pl.ds(ks, nkeys), :] += dkk
            dv_ref[0, pl.ds(ks, nkeys), :] += dvv

        block(0, True)
        if nb > 1:
            def loop(n, c):
                block(n, False)
                return c
            lax.fori_loop(1, nb, loop, 0)

    H = AW // HEAD_DIM
    qv = qkv.reshape(B, L, d * 3 * AW)
    view = lambda t: t.reshape(B, L, d * AW)
    dq, dk, dv, db = pl.pallas_call(
        body, grid=(HP, B, d),
        in_specs=[q_spec, k_spec, v_spec, o_spec, o_spec, o_spec, bias_spec],
        out_specs=[o_spec, o_spec, o_spec, bias_spec],
        out_shape=[jax.ShapeDtypeStruct((B, L, d * AW), F32)] * 3
        + [jax.ShapeDtypeStruct((H, ATTN_BLOCK, 2 * ATTN_BLOCK), F32)],
        compiler_params=_params(3), name=name,
    )(qv, qv, qv, view(do), view(lse), view(dd), bias)
    flat = lambda t: t.reshape(B * S, AW)
    return flat(dq), flat(dk), flat(dv), db


def _attn_combine(ons, lses, gain, tm):
    T, AW = ons[0].shape

    def body(o1, o2, o3, l1, l2, l3, g_ref, attn_ref, lse_ref, mix_ref, r_ref):
        la, lb, lc = l1[...], l2[...], l3[...]
        m = jnp.maximum(jnp.maximum(la, lb), lc)
        ea, eb, ec = jnp.exp(la - m), jnp.exp(lb - m), jnp.exp(lc - m)
        den = ea + eb + ec
        attn = (ea * o1[...] + eb * o2[...] + ec * o3[...]) / den
        attn_ref[...] = attn
        lse_ref[...] = m + jnp.log(den)
        r = lax.rsqrt(jnp.mean(attn * attn, axis=-1, keepdims=True) + LN_EPS)
        mix_ref[...] = (attn * r * g_ref[...]).astype(BF16)
        r_ref[...] = jnp.broadcast_to(r, (tm, LANES))

    row = pl.BlockSpec((tm, AW), lambda i: (i, 0))
    return pl.pallas_call(
        body, grid=(T // tm,),
        in_specs=[row] * 6 + [pl.BlockSpec((1, AW), lambda i: (0, 0))],
        out_specs=[row, row, row, pl.BlockSpec((tm, LANES), lambda i: (i, 0))],
        out_shape=[jax.ShapeDtypeStruct((T, AW), F32), jax.ShapeDtypeStruct((T, AW), F32),
                   jax.ShapeDtypeStruct((T, AW), BF16), jax.ShapeDtypeStruct((T, LANES), F32)],
        compiler_params=_params(1), name="attn_combine",
    )(*ons, *lses, gain)


def _to_sub(src_ref, stage_ref, dsts, S):
    stage_ref[...] = src_ref[0].astype(F32)
    for (_, d), dst in zip(DILATED_CONFIGS[1:], dsts):
        L = S // d
        for r in range(d):
            dst[r * L:(r + 1) * L, :] = stage_ref[pl.ds(r, L, stride=d), :].astype(dst.dtype)


def _branch_blocks(S, d, block):
    nb = S // d // ATTN_BLOCK
    inner_unroll = 3 if (nb - 1) % 3 == 0 else 1

    def per_residue(r, c):
        block(r * nb, True)
        if nb > 1:
            def inner(n, c2):
                block(r * nb + n, False)
                return c2
            lax.fori_loop(1, nb, inner, 0, unroll=inner_unroll)
        return c

    lax.fori_loop(0, d, per_residue, 0, unroll=4 if nb == 1 else 1)


def _attention_fwd(qkv, bias_all, B, S, AW):
    HP = AW // LANES
    nt = MM_DIMS["nt"]

    def body(q_ref, k_ref, v_ref, b_ref, o_ref, lse_ref, stage, q4, q16, k4, k16, v4, v16, o1, l1, o4, l4, o16, l16):
        head0 = lax.broadcasted_iota(jnp.int32, (1, LANES), 1) < HEAD_DIM
        _to_sub(q_ref, stage, (q4, q16), S)
        _to_sub(k_ref, stage, (k4, k16), S)
        _to_sub(v_ref, stage, (v4, v16), S)
        srcs = ((q_ref.at[0], k_ref.at[0], v_ref.at[0], o1, l1), (q4, k4, v4, o4, l4), (q16, k16, v16, o16, l16))
        for bi, (_, d) in enumerate(DILATED_CONFIGS):
            qs_ref, ks_ref, vs_ref, od_ref, ld_ref = srcs[bi]

            def block(g, first, bi=bi, qs_ref=qs_ref, ks_ref=ks_ref, vs_ref=vs_ref, od_ref=od_ref, ld_ref=ld_ref):
                qs = pl.multiple_of(g * ATTN_BLOCK, ATTN_BLOCK)
                nkeys = ATTN_BLOCK if first else 2 * ATTN_BLOCK
                ks = qs if first else pl.multiple_of(qs - ATTN_BLOCK, ATTN_BLOCK)
                q = qs_ref[pl.ds(qs, ATTN_BLOCK), :]
                kk = ks_ref[pl.ds(ks, nkeys), :]
                vv = vs_ref[pl.ds(ks, nkeys), :]
                outs, lses = [], []
                for e in range(2):
                    msk = head0 if e == 0 else jnp.logical_not(head0)
                    qe = jnp.where(msk, q * QK_SCALE, jnp.zeros_like(q))
                    s = lax.dot_general(qe, kk, nt, preferred_element_type=F32)
                    s = s + (b_ref[bi, e, :, ATTN_BLOCK:] if first else b_ref[bi, e])
                    m = jnp.max(s, axis=-1, keepdims=True)
                    p = jnp.exp(s - m)
                    l = jnp.sum(p, axis=-1, keepdims=True)
                    o = jnp.dot(p.astype(BF16), vv, preferred_element_type=F32)
                    outs.append(o / l)
                    lses.append(jnp.broadcast_to(m + jnp.log(l), (ATTN_BLOCK, LANES)))
                od_ref[pl.ds(qs, ATTN_BLOCK), :] = jnp.where(head0, outs[0], outs[1])
                ld_ref[pl.ds(qs, ATTN_BLOCK), :] = jnp.where(head0, lses[0], lses[1])

            _branch_blocks(S, d, block)

        def natural(sub_ref, d):
            L = S // d
            for r in range(d):
                stage[pl.ds(r, L, stride=d), :] = sub_ref[r * L:(r + 1) * L, :]
            return stage[...]

        la = l1[...]
        lb = natural(l4, 4)
        lc = natural(l16, 16)
        m = jnp.maximum(jnp.maximum(la, lb), lc)
        ea, eb, ec = jnp.exp(la - m), jnp.exp(lb - m), jnp.exp(lc - m)
        den = ea + eb + ec
        lse_ref[0] = m + jnp.log(den)
        acc = ea * o1[...]
        acc = acc + eb * natural(o4, 4)
        acc = acc + ec * natural(o16, 16)
        o_ref[0] = acc / den

    blk = lambda off: pl.BlockSpec((1, S, LANES), lambda b, h: (b, 0, off + h))
    qv = qkv.reshape(B, S, 3 * AW)
    sub_b = pltpu.VMEM((S, LANES), BF16)
    sub_f = pltpu.VMEM((S, LANES), F32)
    o, lse = pl.pallas_call(
        body, grid=(B, HP),
        in_specs=[blk(0), blk(HP), blk(2 * HP),
                  pl.BlockSpec((3, 2, ATTN_BLOCK, 2 * ATTN_BLOCK), lambda b, h: (0, h, 0, 0))],
        out_specs=[blk(0), blk(0)],
        out_shape=[jax.ShapeDtypeStruct((B, S, AW), F32)] * 2,
        scratch_shapes=[sub_f] + [sub_b] * 6 + [sub_f] * 6,
        compiler_params=_params(2), name="attention_fwd",
    )(qv, qv, qv, bias_all)
    return o.reshape(B * S, AW), lse.reshape(B * S, AW)


def _attention_bwd(qkv, do, lse, dd, bias_all, B, S, AW):
    HP = AW // LANES
    H = AW // HEAD_DIM
    nt, tn = MM_DIMS["nt"], MM_DIMS["tn"]

    def body(q_ref, k_ref, v_ref, do_ref, lse_ref, dd_ref, b_ref,
             dq_ref, dk_ref, dv_ref, csq_ref, csk_ref, csv_ref, db_ref,
             stage, q4, q16, k4, k16, v4, v16, g4, g16, l4, l16, d4, d16,
             aq1, ak1, av1, aq4, ak4, av4, aq16, ak16, av16):
        head0 = lax.broadcasted_iota(jnp.int32, (1, LANES), 1) < HEAD_DIM
        first_b = pl.program_id(1) == 0

        @pl.when(first_b)
        def _():
            db_ref[...] = jnp.zeros_like(db_ref)

        _to_sub(q_ref, stage, (q4, q16), S)
        _to_sub(k_ref, stage, (k4, k16), S)
        _to_sub(v_ref, stage, (v4, v16), S)
        _to_sub(do_ref, stage, (g4, g16), S)
        _to_sub(lse_ref, stage, (l4, l16), S)
        _to_sub(dd_ref, stage, (d4, d16), S)
        for acc in (ak1, av1, ak4, av4, ak16, av16):
            acc[...] = jnp.zeros_like(acc)
        srcs = ((q_ref.at[0], k_ref.at[0], v_ref.at[0], do_ref.at[0], lse_ref.at[0], dd_ref.at[0], aq1, ak1, av1),
                (q4, k4, v4, g4, l4, d4, aq4, ak4, av4), (q16, k16, v16, g16, l16, d16, aq16, ak16, av16))
        for bi, (_, d) in enumerate(DILATED_CONFIGS):
            def block(g, first, bi=bi, refs=srcs[bi]):
                qs_ref, ks_ref, vs_ref, gs_ref, ls_ref, ds_ref, aq, ak, av = refs
                qs = pl.multiple_of(g * ATTN_BLOCK, ATTN_BLOCK)
                nkeys = ATTN_BLOCK if first else 2 * ATTN_BLOCK
                ks = qs if first else pl.multiple_of(qs - ATTN_BLOCK, ATTN_BLOCK)
                q = qs_ref[pl.ds(qs, ATTN_BLOCK), :]
                kk = ks_ref[pl.ds(ks, nkeys), :]
                vv = vs_ref[pl.ds(ks, nkeys), :]
                dout = gs_ref[pl.ds(qs, ATTN_BLOCK), :]
                lse_b = ls_ref[pl.ds(qs, ATTN_BLOCK), :]
                dd_b = ds_ref[pl.ds(qs, ATTN_BLOCK), :]
                dq = jnp.zeros((ATTN_BLOCK, LANES), F32)
                dkk = jnp.zeros((nkeys, LANES), F32)
                dvv = jnp.zeros((nkeys, LANES), F32)
                for e in range(2):
                    msk = head0 if e == 0 else jnp.logical_not(head0)
                    c0 = e * HEAD_DIM
                    qe = jnp.where(msk, q * QK_SCALE, jnp.zeros_like(q))
                    doe = jnp.where(msk, dout, jnp.zeros_like(dout))
                    kke = jnp.where(msk, kk * QK_SCALE, jnp.zeros_like(kk))
                    s = lax.dot_general(qe, kk, nt, preferred_element_type=F32)
                    s = s + (b_ref[bi, e, :, ATTN_BLOCK:] if first else b_ref[bi, e])
                    p = jnp.exp(s - lse_b[:, c0:c0 + 1])
                    dp = lax.dot_general(doe, vv, nt, preferred_element_type=F32)
                    ds = p * (dp - dd_b[:, c0:c0 + 1])
                    if first:
                        db_ref[bi, e, :, ATTN_BLOCK:] += ds
                    else:
                        db_ref[bi, e] += ds
                    dsb = ds.astype(BF16)
                    dq = dq + jnp.dot(dsb, kke, preferred_element_type=F32)
                    dkk = dkk + lax.dot_general(dsb, qe, tn, preferred_element_type=F32)
                    dvv = dvv + lax.dot_general(p.astype(BF16), doe, tn, preferred_element_type=F32)
                aq[pl.ds(qs, ATTN_BLOCK), :] = dq
                ak[pl.ds(ks, nkeys), :] += dkk
                av[pl.ds(ks, nkeys), :] += dvv

            _branch_blocks(S, d, block)

        for a1, a4, a16, out_ref, cs_ref in ((aq1, aq4, aq16, dq_ref, csq_ref), (ak1, ak4, ak16, dk_ref, csk_ref),
                                             (av1, av4, av16, dv_ref, csv_ref)):
            stage[...] = a1[...]
            for d, sub in ((4, a4), (16, a16)):
                L = S // d
                for r in range(d):
                    stage[pl.ds(r, L, stride=d), :] += sub[r * L:(r + 1) * L, :]
            tot = stage[...]
            out_ref[0] = tot.astype(out_ref.dtype)
            _accumulate(cs_ref, first_b, jnp.sum(tot, axis=0, keepdims=True))

    blk = lambda off: pl.BlockSpec((1, S, LANES), lambda h, b: (b, 0, off + h))
    cs_spec = pl.BlockSpec((1, LANES), lambda h, b: (0, h))
    bias_spec = pl.BlockSpec((3, 2, ATTN_BLOCK, 2 * ATTN_BLOCK), lambda h, b: (0, h, 0, 0))
    qv = qkv.reshape(B, S, 3 * AW)
    view = lambda t: t.reshape(B, S, AW)
    sub_b = pltpu.VMEM((S, LANES), BF16)
    sub_f = pltpu.VMEM((S, LANES), F32)
    res = pl.pallas_call(
        body, grid=(HP, B),
        in_specs=[blk(0), blk(HP), blk(2 * HP), blk(0), blk(0), blk(0), bias_spec],
        out_specs=[blk(0), blk(0), blk(0), cs_spec, cs_spec, cs_spec, bias_spec],
        out_shape=[jax.ShapeDtypeStruct((B, S, AW), BF16)] * 3 + [jax.ShapeDtypeStruct((1, AW), F32)] * 3
        + [jax.ShapeDtypeStruct((3, H, ATTN_BLOCK, 2 * ATTN_BLOCK), F32)],
        scratch_shapes=[sub_f] + [sub_b] * 8 + [sub_f] * 4 + [sub_f] * 9,
        compiler_params=_params(2), name="attention_bwd",
    )(qv, qv, qv, view(do), view(lse), view(dd), bias_all)
    flat = lambda t: t.reshape(B * S, AW)
    return flat(res[0]), flat(res[1]), flat(res[2]), res[3], res[4], res[5], res[6]


def _regroup(src, stage, dst, d, S, off=0):
    if d == 1:
        dst[off:off + S, :] = src.astype(dst.dtype)
        return
    stage[...] = src.astype(F32)
    L = S // d
    for r in range(d):
        dst[off + r * L:off + (r + 1) * L, :] = stage[pl.ds(r, L, stride=d), :].astype(dst.dtype)


def _ungroup(sub_ref, off, nat_ref, d, S, add):
    L = S // d
    for r in range(d):
        rows = pl.ds(0, S) if d == 1 else pl.ds(r, L, stride=d)
        val = sub_ref[off + r * L:off + (r + 1) * L, :]
        if add:
            nat_ref[rows, :] += val
        else:
            nat_ref[rows, :] = val


def _branch_keys(ks, vs, S, nb, g_idx):
    blk3 = (S // ATTN_BLOCK, ATTN_BLOCK, LANES)
    kc3 = ks[ATTN_BLOCK:ATTN_BLOCK + S, :].reshape(blk3)
    vc3 = vs[ATTN_BLOCK:ATTN_BLOCK + S, :].reshape(blk3)
    if nb == 1:
        return kc3, vc3, None
    kk3 = jnp.concatenate([ks[0:S, :].reshape(blk3), kc3], axis=1)
    vv3 = jnp.concatenate([vs[0:S, :].reshape(blk3), vc3], axis=1)
    col = lax.broadcasted_iota(jnp.int32, (1, 1, 2 * ATTN_BLOCK), 2)
    dead = jnp.logical_and((g_idx & (nb - 1)) == 0, col < ATTN_BLOCK)
    return kk3, vv3, dead


def _branch_scores(qe, kk3, b_ref, bi, e, dead):
    s = jnp.einsum("gqe,gke->gqk", qe, kk3, preferred_element_type=F32)
    if dead is None:
        return s + b_ref[bi, e, :, ATTN_BLOCK:]
    return jnp.where(dead, NEG_INF, s + b_ref[bi, e])


def _attention_fwd(qkv, bias_all, B, S, AW, bg=None):
    HP = AW // LANES
    G = S // ATTN_BLOCK
    blk3 = (G, ATTN_BLOCK, LANES)

    def body(refs, bg_hook):
        q_ref, k_ref, v_ref, b_ref, o_ref, lse_ref, stage, qs, ks, vs, ot, lt, on0, on1, on2, ln0, ln1, ln2 = refs
        bg_hook(False)
        head0 = lax.broadcasted_iota(jnp.int32, (1, 1, LANES), 2) < HEAD_DIM
        g_idx = lax.broadcasted_iota(jnp.int32, (G, 1, 1), 0)
        ks[0:ATTN_BLOCK, :] = jnp.zeros((ATTN_BLOCK, LANES), BF16)
        vs[0:ATTN_BLOCK, :] = jnp.zeros((ATTN_BLOCK, LANES), BF16)
        nat_o, nat_l = (on0, on1, on2), (ln0, ln1, ln2)
        for bi, (_, d) in enumerate(DILATED_CONFIGS):
            nb = S // d // ATTN_BLOCK
            _regroup(q_ref[0], stage, qs, d, S)
            _regroup(k_ref[0], stage, ks, d, S, ATTN_BLOCK)
            _regroup(v_ref[0], stage, vs, d, S, ATTN_BLOCK)
            q3 = qs[...].reshape(blk3) * QK_SCALE
            kk3, vv3, dead = _branch_keys(ks, vs, S, nb, g_idx)
            outs, lses = [], []
            for e in range(2):
                msk = head0 if e == 0 else jnp.logical_not(head0)
                qe = jnp.where(msk, q3, jnp.zeros_like(q3))
                s = _branch_scores(qe, kk3, b_ref, bi, e, dead)
                m = jnp.max(s, axis=-1, keepdims=True)
                p = jnp.exp(s - m)
                l = jnp.sum(p, axis=-1, keepdims=True)
                o = jnp.einsum("gqk,gke->gqe", p.astype(BF16), vv3, preferred_element_type=F32)
                outs.append(o / l)
                lses.append(jnp.broadcast_to(m + jnp.log(l), blk3))
            ot[...] = jnp.where(head0, outs[0], outs[1]).reshape(S, LANES)
            lt[...] = jnp.where(head0, lses[0], lses[1]).reshape(S, LANES)
            _ungroup(ot, 0, nat_o[bi], d, S, add=False)
            _ungroup(lt, 0, nat_l[bi], d, S, add=False)

        la, lb, lc = ln0[...], ln1[...], ln2[...]
        m = jnp.maximum(jnp.maximum(la, lb), lc)
        ea, eb, ec = jnp.exp(la - m), jnp.exp(lb - m), jnp.exp(lc - m)
        den = ea + eb + ec
        lse_ref[0] = m + jnp.log(den)
        o_ref[0] = (ea * on0[...] + eb * on1[...] + ec * on2[...]) / den
        bg_hook(True)

    blk = lambda off: pl.BlockSpec((1, S, LANES), lambda b, h: (b, 0, off + h))
    qv = qkv.reshape(B, S, 3 * AW)
    sub_f = pltpu.VMEM((S, LANES), F32)
    pad_b = pltpu.VMEM((S + ATTN_BLOCK, LANES), BF16)
    res = _hosted_call(
        body, bg, grid=(B, HP),
        in_specs=[blk(0), blk(HP), blk(2 * HP),
                  pl.BlockSpec((3, 2, ATTN_BLOCK, 2 * ATTN_BLOCK), lambda b, h: (0, h, 0, 0))],
        out_specs=[blk(0), blk(0)],
        out_shape=[jax.ShapeDtypeStruct((B, S, AW), F32)] * 2,
        scratch_shapes=[sub_f, pltpu.VMEM((S, LANES), BF16), pad_b, pad_b] + [sub_f] * 8,
        operands=[qv, qv, qv, bias_all], name="attention_fwd")
    return (res[0].reshape(B * S, AW), res[1].reshape(B * S, AW)) + tuple(res[2:])


def _attention_bwd(qkv, do, lse, dd, bias_all, B, S, AW, bg=None):
    HP = AW // LANES
    H = AW // HEAD_DIM
    G = S // ATTN_BLOCK
    blk3 = (G, ATTN_BLOCK, LANES)
    PAD = ATTN_BLOCK

    def body(refs, bg_hook):
        (q_ref, k_ref, v_ref, do_ref, lse_ref, dd_ref, b_ref,
         dq_ref, dk_ref, dv_ref, csq_ref, csk_ref, csv_ref, db_ref,
         stage, qs, ks, vs, gs, ls, ds_, tq, tk, tv, accq, acck, accv) = refs
        bg_hook(False)
        head0 = lax.broadcasted_iota(jnp.int32, (1, 1, LANES), 2) < HEAD_DIM
        g_idx = lax.broadcasted_iota(jnp.int32, (G, 1, 1), 0)
        first_b = pl.program_id(1) == 0

        @pl.when(first_b)
        def _():
            db_ref[...] = jnp.zeros_like(db_ref)

        ks[0:PAD, :] = jnp.zeros((PAD, LANES), BF16)
        vs[0:PAD, :] = jnp.zeros((PAD, LANES), BF16)
        tk[0:PAD, :] = jnp.zeros((PAD, LANES), F32)
        tv[0:PAD, :] = jnp.zeros((PAD, LANES), F32)
        for bi, (_, d) in enumerate(DILATED_CONFIGS):
            nb = S // d // ATTN_BLOCK
            _regroup(q_ref[0], stage, qs, d, S)
            _regroup(k_ref[0], stage, ks, d, S, PAD)
            _regroup(v_ref[0], stage, vs, d, S, PAD)
            _regroup(do_ref[0], stage, gs, d, S)
            _regroup(lse_ref[0], stage, ls, d, S)
            _regroup(dd_ref[0], stage, ds_, d, S)
            q3 = qs[...].reshape(blk3) * QK_SCALE
            do3 = gs[...].reshape(blk3)
            lse3 = ls[...].reshape(blk3)
            dd3 = ds_[...].reshape(blk3)
            kk3, vv3, dead = _branch_keys(ks, vs, S, nb, g_idx)
            dq = jnp.zeros(blk3, F32)
            dkk = jnp.zeros(kk3.shape, F32)
            dvv = jnp.zeros(kk3.shape, F32)
            for e in range(2):
                msk = head0 if e == 0 else jnp.logical_not(head0)
                c0 = e * HEAD_DIM
                qe = jnp.where(msk, q3, jnp.zeros_like(q3))
                doe = jnp.where(msk, do3, jnp.zeros_like(do3))
                ke = jnp.where(msk, kk3 * QK_SCALE, jnp.zeros_like(kk3))
                s = _branch_scores(qe, kk3, b_ref, bi, e, dead)
                p = jnp.exp(s - lse3[:, :, c0:c0 + 1])
                dp = jnp.einsum("gqe,gke->gqk", doe, vv3, preferred_element_type=F32)
                dsc = p * (dp - dd3[:, :, c0:c0 + 1])
                if dead is None:
                    db_ref[bi, e, :, ATTN_BLOCK:] += jnp.sum(dsc, axis=0)
                else:
                    db_ref[bi, e] += jnp.sum(dsc, axis=0)
                dsb = dsc.astype(BF16)
                dq = dq + jnp.einsum("gqk,gke->gqe", dsb, ke, preferred_element_type=F32)
                dkk = dkk + jnp.einsum("gqk,gqe->gke", dsb, qe, preferred_element_type=F32)
                dvv = dvv + jnp.einsum("gqk,gqe->gke", p.astype(BF16), doe, preferred_element_type=F32)
            tq[...] = dq.reshape(S, LANES)
            if dead is None:
                tk[PAD:PAD + S, :] = dkk.reshape(S, LANES)
                tv[PAD:PAD + S, :] = dvv.reshape(S, LANES)
            else:
                tk[PAD:PAD + S, :] = dkk[:, ATTN_BLOCK:, :].reshape(S, LANES)
                tv[PAD:PAD + S, :] = dvv[:, ATTN_BLOCK:, :].reshape(S, LANES)
                tk[0:S, :] += dkk[:, :ATTN_BLOCK, :].reshape(S, LANES)
                tv[0:S, :] += dvv[:, :ATTN_BLOCK, :].reshape(S, LANES)
            _ungroup(tq, 0, accq, d, S, add=bi > 0)
            _ungroup(tk, PAD, acck, d, S, add=bi > 0)
            _ungroup(tv, PAD, accv, d, S, add=bi > 0)

        for acc, out_ref, cs_ref in ((accq, dq_ref, csq_ref), (acck, dk_ref, csk_ref), (accv, dv_ref, csv_ref)):
            tot = acc[...]
            out_ref[0] = tot.astype(out_ref.dtype)
            _accumulate(cs_ref, first_b, jnp.sum(tot, axis=0, keepdims=True))
        bg_hook(True)

    blk = lambda off: pl.BlockSpec((1, S, LANES), lambda h, b: (b, 0, off + h))
    cs_spec = pl.BlockSpec((1, LANES), lambda h, b: (0, h))
    bias_spec = pl.BlockSpec((3, 2, ATTN_BLOCK, 2 * ATTN_BLOCK), lambda h, b: (0, h, 0, 0))
    qv = qkv.reshape(B, S, 3 * AW)
    view = lambda t: t.reshape(B, S, AW)
    sub_b = pltpu.VMEM((S, LANES), BF16)
    sub_f = pltpu.VMEM((S, LANES), F32)
    pad_b = pltpu.VMEM((S + PAD, LANES), BF16)
    pad_f = pltpu.VMEM((S + PAD, LANES), F32)
    res = _hosted_call(
        body, bg, grid=(HP, B),
        in_specs=[blk(0), blk(HP), blk(2 * HP), blk(0), blk(0), blk(0), bias_spec],
        out_specs=[blk(0), blk(0), blk(0), cs_spec, cs_spec, cs_spec, bias_spec],
        out_shape=[jax.ShapeDtypeStruct((B, S, AW), BF16)] * 3 + [jax.ShapeDtypeStruct((1, AW), F32)] * 3
        + [jax.ShapeDtypeStruct((3, H, ATTN_BLOCK, 2 * ATTN_BLOCK), F32)],
        scratch_shapes=[sub_f, sub_b, pad_b, pad_b, sub_b, sub_f, sub_f, sub_f, pad_f, pad_f, sub_f, sub_f, sub_f],
        operands=[qv, qv, qv, view(do), view(lse), view(dd), bias_all], name="attention_bwd")
    flat = lambda t: t.reshape(B * S, AW)
    return (flat(res[0]), flat(res[1]), flat(res[2]), res[3], res[4], res[5], res[6]) + tuple(res[7:])


def _attn_norm(attn, gain, tm):
    T, AW = attn.shape

    def body(a_ref, g_ref, mix_ref, r_ref):
        a = a_ref[...]
        r = lax.rsqrt(jnp.mean(a * a, axis=-1, keepdims=True) + LN_EPS)
        mix_ref[...] = (a * r * g_ref[...]).astype(BF16)
        r_ref[...] = jnp.broadcast_to(r, (tm, LANES))

    row = pl.BlockSpec((tm, AW), lambda i: (i, 0))
    return pl.pallas_call(
        body, grid=(T // tm,), in_specs=[row, pl.BlockSpec((1, AW), lambda i: (0, 0))],
        out_specs=[row, pl.BlockSpec((tm, LANES), lambda i: (i, 0))],
        out_shape=[jax.ShapeDtypeStruct((T, AW), BF16), jax.ShapeDtypeStruct((T, LANES), F32)],
        compiler_params=_params(1), name="attn_norm",
    )(attn, gain)


def _attn_pre_bwd(dmixed, attn, rstd, gain, tm):
    T, AW = attn.shape
    ones_np = np.kron(np.eye(AW // HEAD_DIM, dtype=np.float32), np.ones((HEAD_DIM, HEAD_DIM), np.float32))
    ones_bd = jnp.asarray(ones_np, dtype=BF16)

    def body(dm_ref, a_ref, r_ref, g_ref, ones_ref, do_ref, dd_ref, dg_ref):
        i = pl.program_id(0)
        dm = dm_ref[...]
        a = a_ref[...]
        r = r_ref[:, 0:1]
        dxn = dm * g_ref[...]
        da = r * (dxn - a * (r * r) * jnp.mean(dxn * a, axis=-1, keepdims=True))
        do_ref[...] = da.astype(BF16)
        hi, lo = _split_hi_lo(da * a)
        dd_ref[...] = (jnp.dot(hi, ones_ref[...], preferred_element_type=F32)
                       + jnp.dot(lo, ones_ref[...], preferred_element_type=F32))
        _accumulate(dg_ref, i == 0, jnp.sum(dm * a * r, axis=0, keepdims=True))

    row = pl.BlockSpec((tm, AW), lambda i: (i, 0))
    vec = pl.BlockSpec((1, AW), lambda i: (0, 0))
    return pl.pallas_call(
        body, grid=(T // tm,),
        in_specs=[row, row, pl.BlockSpec((tm, LANES), lambda i: (i, 0)), vec,
                  pl.BlockSpec((AW, AW), lambda i: (0, 0))],
        out_specs=[row, row, vec],
        out_shape=[jax.ShapeDtypeStruct((T, AW), BF16), jax.ShapeDtypeStruct((T, AW), F32),
                   jax.ShapeDtypeStruct((1, AW), F32)],
        compiler_params=_params(1), name="attn_pre_bwd",
    )(dmixed, attn, rstd, gain, ones_bd)


class _RowShifts:
    def __init__(self, x, row, up):
        self.x, self.row, self.up, self.base = x, row, up, {0: x}

    def __call__(self, s):
        x = self.x
        n, c = x.shape
        r, whole = s % 8, s - s % 8
        if r not in self.base:
            if self.up:
                rolled = pltpu.roll(x, n - r, 0)
                tail = jnp.where(self.row[n - 8:] < n - r, rolled[n - 8:], 0.0)
                self.base[r] = jnp.concatenate([rolled[:n - 8], tail], axis=0)
            else:
                rolled = pltpu.roll(x, r, 0)
                head = jnp.where(self.row[:8] >= r, rolled[:8], 0.0)
                self.base[r] = jnp.concatenate([head, rolled[8:]], axis=0)
        y = self.base[r]
        if whole == 0:
            return y
        pad = jnp.zeros((whole, c), x.dtype)
        if self.up:
            return jnp.concatenate([y[whole:], pad], axis=0)
        return jnp.concatenate([pad, y[:n - whole]], axis=0)


def _conv_branch_fwd_math(a, g, w_ref, cb, lg, lb, row):
    sg = _sigmoid(g)
    u0 = a * sg
    u0_down = _RowShifts(u0, row, up=False)
    uc = jnp.zeros_like(u0) + cb
    for k in range(CONV_KERNEL):
        uc = uc + w_ref[k:k + 1, :] * u0_down(CONV_KERNEL - 1 - k)
    ul, xh, r = _ln_fwd(uc, lg, lb)
    su = _sigmoid(ul)
    u = ul * su
    return sg, u0_down, ul, xh, r, su, u


def _conv_fwd(ag, conv_w, conv_b, ln_g, ln_b, norm_g, B, S, CW):
    def body(a_ref, g_ref, w_ref, cb_ref, lg_ref, lb_ref, ng_ref, o_ref):
        row = lax.broadcasted_iota(jnp.int32, (S, CW), 0)
        _, _, _, _, _, _, u = _conv_branch_fwd_math(a_ref[0], g_ref[0], w_ref, cb_ref[...], lg_ref[...],
                                                    lb_ref[...], row)
        rr = lax.rsqrt(jnp.mean(u * u, axis=-1, keepdims=True) + LN_EPS)
        o_ref[0] = (u * rr * ng_ref[...]).astype(BF16)

    vec = pl.BlockSpec((1, CW), lambda b: (0, 0))
    out = pl.pallas_call(
        body, grid=(B,),
        in_specs=[pl.BlockSpec((1, S, CW), lambda b: (b, 0, 0)), pl.BlockSpec((1, S, CW), lambda b: (b, 0, 1)),
                  pl.BlockSpec((CONV_KERNEL, CW), lambda b: (0, 0)), vec, vec, vec, vec],
        out_specs=pl.BlockSpec((1, S, CW), lambda b: (b, 0, 0)),
        out_shape=jax.ShapeDtypeStruct((B, S, CW), BF16),
        compiler_params=_params(1), name="conv_fwd",
    )(ag.reshape(B, S, 2 * CW), ag.reshape(B, S, 2 * CW), conv_w, conv_b, ln_g, ln_b, norm_g)
    return out.reshape(B * S, CW)


def _conv_bwd(ag, dmc, conv_w, conv_b, ln_g, ln_b, norm_g, B, S, CW):
    def body(a_ref, g_ref, dm_ref, w_ref, cb_ref, lg_ref, lb_ref, ng_ref,
             dag_ref, dw_ref, dcb_ref, dlg_ref, dlb_ref, dng_ref):
        b = pl.program_id(0)
        row = lax.broadcasted_iota(jnp.int32, (S, CW), 0)
        a, g = a_ref[0], g_ref[0]
        sg, u0_down, ul, xh, r, su, u = _conv_branch_fwd_math(a, g, w_ref, cb_ref[...], lg_ref[...], lb_ref[...], row)
        rr = lax.rsqrt(jnp.mean(u * u, axis=-1, keepdims=True) + LN_EPS)
        dm = dm_ref[0]
        dxn = dm * ng_ref[...]
        du = rr * (dxn - u * (rr * rr) * jnp.mean(dxn * u, axis=-1, keepdims=True))
        dul = du * su * (1.0 + ul * (1.0 - su))
        duc = _ln_bwd(dul, xh, r, lg_ref[...])
        first = b == 0
        _accumulate(dng_ref, first, jnp.sum(dm * u * rr, axis=0, keepdims=True))
        _accumulate(dlg_ref, first, jnp.sum(dul * xh, axis=0, keepdims=True))
        _accumulate(dlb_ref, first, jnp.sum(dul, axis=0, keepdims=True))
        _accumulate(dcb_ref, first, jnp.sum(duc, axis=0, keepdims=True))

        @pl.when(first)
        def _():
            dw_ref[...] = jnp.zeros_like(dw_ref)

        duc_up = _RowShifts(duc, row, up=True)
        du0 = jnp.zeros_like(duc)
        for k in range(CONV_KERNEL):
            sh = CONV_KERNEL - 1 - k
            dw_ref[k:k + 1, :] += jnp.sum(duc * u0_down(sh), axis=0, keepdims=True)
            du0 = du0 + w_ref[k:k + 1, :] * duc_up(sh)
        dag_ref[0, :, :CW] = du0 * sg
        dag_ref[0, :, CW:] = du0 * a * sg * (1.0 - sg)

    vec = pl.BlockSpec((1, CW), lambda b: (0, 0))
    wspec = pl.BlockSpec((CONV_KERNEL, CW), lambda b: (0, 0))
    agv = ag.reshape(B, S, 2 * CW)
    res = pl.pallas_call(
        body, grid=(B,),
        in_specs=[pl.BlockSpec((1, S, CW), lambda b: (b, 0, 0)), pl.BlockSpec((1, S, CW), lambda b: (b, 0, 1)),
                  pl.BlockSpec((1, S, CW), lambda b: (b, 0, 0)), wspec, vec, vec, vec, vec],
        out_specs=[pl.BlockSpec((1, S, 2 * CW), lambda b: (b, 0, 0)), wspec, vec, vec, vec, vec],
        out_shape=[jax.ShapeDtypeStruct((B, S, 2 * CW), F32), jax.ShapeDtypeStruct((CONV_KERNEL, CW), F32)]
        + [jax.ShapeDtypeStruct((1, CW), F32)] * 4,
        compiler_params=_params(1), name="conv_bwd",
    )(agv, agv, dmc.reshape(B, S, CW), conv_w, conv_b, ln_g, ln_b, norm_g)
    return (res[0].reshape(B * S, 2 * CW),) + tuple(res[1:])


def _ffn_conv(x, w_ref, bias, row):
    down = x if isinstance(x, _RowShifts) else _RowShifts(x, row, up=False)
    y = jnp.zeros_like(down.x) + bias
    for k in range(FFN_CONV_KERNEL):
        y = y + w_ref[k:k + 1, :] * down(FFN_CONV_KERNEL - 1 - k)
    return y


def _ffn_specs(S, tc, nj, order):
    pick = (lambda b, j: (b, j)) if order == "bj" else (lambda j, b: (b, j))
    act = lambda off: pl.BlockSpec((1, S, tc), lambda *g: (pick(*g)[0], 0, off + pick(*g)[1]))
    cw = lambda off: pl.BlockSpec((FFN_CONV_KERNEL, tc), lambda *g: (0, off + pick(*g)[1]))
    cb = lambda off: pl.BlockSpec((1, tc), lambda *g: (0, off + pick(*g)[1]))
    return act, cw, cb


def _ffn_act(upre, cw, cb, B, S, DFF):
    tc = FFN_COLS
    nj = DFF // tc

    def body(ug_ref, uv_ref, wg_ref, wv_ref, bg_ref, bv_ref, o_ref):
        row = lax.broadcasted_iota(jnp.int32, (S, tc), 0)
        gate = _ffn_conv(ug_ref[0], wg_ref, bg_ref[...], row)
        val = _ffn_conv(uv_ref[0], wv_ref, bv_ref[...], row)
        o_ref[0] = (gate * _sigmoid(gate) * val).astype(BF16)

    act, cws, cbs = _ffn_specs(S, tc, nj, "bj")
    uv = upre.reshape(B, S, 2 * DFF)
    out = pl.pallas_call(
        body, grid=(B, nj), in_specs=[act(0), act(nj), cws(0), cws(nj), cbs(0), cbs(nj)], out_specs=act(0),
        out_shape=jax.ShapeDtypeStruct((B, S, DFF), BF16), compiler_params=_params(2), name="ffn_act",
    )(uv, uv, cw, cw, cb, cb)
    return out.reshape(B * S, DFF)


def _ffn_bwd(upre, dact, cw, cb, B, S, DFF):
    tc = FFN_COLS
    nj = DFF // tc

    def body(ug_ref, uv_ref, da_ref, wg_ref, wv_ref, bg_ref, bv_ref, dug_ref, duv_ref, dwg_ref, dwv_ref,
             dbg_ref, dbv_ref):
        first = pl.program_id(1) == 0
        row = lax.broadcasted_iota(jnp.int32, (S, tc), 0)
        ug, uv = ug_ref[0], uv_ref[0]
        gate = _ffn_conv(ug, wg_ref, bg_ref[...], row)
        val = _ffn_conv(uv, wv_ref, bv_ref[...], row)
        sg = _sigmoid(gate)
        dact_b = da_ref[0]
        dgate = dact_b * val * sg * (1.0 + gate * (1.0 - sg))
        dval = dact_b * gate * sg
        for dup, u, w_ref, du_ref, dw_ref, db_ref in ((dgate, ug, wg_ref, dug_ref, dwg_ref, dbg_ref),
                                                      (dval, uv, wv_ref, duv_ref, dwv_ref, dbv_ref)):
            _accumulate(db_ref, first, jnp.sum(dup, axis=0, keepdims=True))

            @pl.when(first)
            def _(dw_ref=dw_ref):
                dw_ref[...] = jnp.zeros_like(dw_ref)

            dupre = jnp.zeros_like(dup)
            for k in range(FFN_CONV_KERNEL):
                sh = FFN_CONV_KERNEL - 1 - k
                dw_ref[k:k + 1, :] += jnp.sum(dup * _shift_down(u, sh, row), axis=0, keepdims=True)
                dupre = dupre + w_ref[k:k + 1, :] * _shift_up(dup, sh, row)
            du_ref[0] = dupre.astype(BF16)

    act, cws, cbs = _ffn_specs(S, tc, nj, "jb")
    uv = upre.reshape(B, S, 2 * DFF)
    res = pl.pallas_call(
        body, grid=(nj, B),
        in_specs=[act(0), act(nj), act(0), cws(0), cws(nj), cbs(0), cbs(nj)],
        out_specs=[act(0), act(0), cws(0), cws(0), cbs(0), cbs(0)],
        out_shape=[jax.ShapeDtypeStruct((B, S, DFF), BF16)] * 2
        + [jax.ShapeDtypeStruct((FFN_CONV_KERNEL, DFF), F32)] * 2 + [jax.ShapeDtypeStruct((1, DFF), F32)] * 2,
        compiler_params=_params(2), name="ffn_bwd",
    )(uv, uv, dact.reshape(B, S, DFF), cw, cw, cb, cb)
    flat = lambda t: t.reshape(B * S, DFF)
    return (flat(res[0]), flat(res[1]), jnp.concatenate([res[2], res[3]], axis=1),
            jnp.concatenate([res[4], res[5]], axis=1))


FFN_HALO = 16


def _half_sequences(S):
    if S < 8 * FFN_HALO:
        return [(0, S, 0, S)]
    h = S // 2
    return [(0, h + FFN_HALO, 0, h), (h - FFN_HALO, S, FFN_HALO, h)]


def _w_up_block_spec(w_up_sh, tc, off):
    _, D, cs = w_up_sh.shape
    assert cs % tc == 0
    bps = cs // tc
    return pl.BlockSpec((1, D, tc), lambda j: ((off + j) // bps, 0, (off + j) % bps))


def _ffn_fwd_fused(x1b, w_up_sh, cw, cb, B, S, DFF):
    tc = FFN_COLS
    nj = DFF // tc
    D = x1b.shape[1]

    def body(x_ref, wg_ref, wv_ref, cwg_ref, cwv_ref, cbg_ref, cbv_ref, o_ref, up_ref):
        w = jnp.concatenate([wg_ref[0], wv_ref[0]], axis=1)
        row = lax.broadcasted_iota(jnp.int32, (S, tc), 0)
        for b in range(B):
            up = jnp.dot(x_ref[b], w, preferred_element_type=F32)
            up_ref[b] = up
            gate = _ffn_conv(up[:, :tc], cwg_ref, cbg_ref[...], row)
            val = _ffn_conv(up[:, tc:], cwv_ref, cbv_ref[...], row)
            o_ref[b] = (gate * _sigmoid(gate) * val).astype(BF16)

    cws = lambda off: pl.BlockSpec((FFN_CONV_KERNEL, tc), lambda j: (0, off + j))
    cbs = lambda off: pl.BlockSpec((1, tc), lambda j: (0, off + j))
    act, upre = pl.pallas_call(
        body, grid=(nj,),
        in_specs=[pl.BlockSpec((B, S, D), lambda j: (0, 0, 0), pipeline_mode=pl.Buffered(1)),
                  _w_up_block_spec(w_up_sh, tc, 0), _w_up_block_spec(w_up_sh, tc, nj),
                  cws(0), cws(nj), cbs(0), cbs(nj)],
        out_specs=[pl.BlockSpec((B, S, tc), lambda j: (0, 0, j)), pl.BlockSpec((B, S, 2 * tc), lambda j: (0, 0, j))],
        out_shape=[jax.ShapeDtypeStruct((B, S, DFF), BF16), jax.ShapeDtypeStruct((B, S, 2 * DFF), F32)],
        compiler_params=_params(1), name="ffn_fwd",
    )(x1b.reshape(B, S, D), w_up_sh, w_up_sh, cw, cw, cb, cb)
    return act.reshape(B * S, DFF), upre


def _ffn_bwd_fused(x1b, dz2b, upre, w_down, cw, cb, B, S, DFF):
    tc = FFN_COLS
    nj = DFF // tc
    D = x1b.shape[1]

    def body(x_ref, dz_ref, up_ref, wd_ref, cwg_ref, cwv_ref, cbg_ref, cbv_ref,
             dug_ref, duv_ref, dwu_ref, dwd_ref, dcw_ref, dcb_ref):
        first = pl.program_id(1) == 0
        dw_t = dwd = None
        dcb = [None, None]
        dcw = [[None] * FFN_CONV_KERNEL, [None] * FFN_CONV_KERNEL]
        add = lambda old, new: new if old is None else old + new
        for lo, hi, o0, on in _half_sequences(S):
            n = hi - lo
            own = slice(o0, o0 + on)
            row = lax.broadcasted_iota(jnp.int32, (n, tc), 0)
            x = x_ref[0, lo:hi, :]
            dz = dz_ref[0, lo:hi, :]
            ug = _RowShifts(up_ref[0, lo:hi, :tc], row, up=False)
            uv = _RowShifts(up_ref[0, lo:hi, tc:], row, up=False)
            gate = _ffn_conv(ug, cwg_ref, cbg_ref[...], row)
            val = _ffn_conv(uv, cwv_ref, cbv_ref[...], row)
            sg = _sigmoid(gate)
            act = (gate * sg * val).astype(BF16)
            dact = _dot(dz, wd_ref[...], "nt")
            dgate = dact * val * sg * (1.0 + gate * (1.0 - sg))
            dval = dact * gate * sg
            dupre = []
            for h, (dup, u_down, w_ref) in enumerate(((dgate, ug, cwg_ref), (dval, uv, cwv_ref))):
                dcb[h] = add(dcb[h], jnp.sum(dup[own], axis=0, keepdims=True))
                dup_up = _RowShifts(dup, row, up=True)
                acc = jnp.zeros_like(dup)
                for k in range(FFN_CONV_KERNEL):
                    sh = FFN_CONV_KERNEL - 1 - k
                    dcw[h][k] = add(dcw[h][k], jnp.sum((dup * u_down(sh))[own], axis=0, keepdims=True))
                    acc = acc + w_ref[k:k + 1, :] * dup_up(sh)
                dupre.append(acc.astype(BF16)[own])
            dug_ref[0, lo + o0:lo + o0 + on, :] = dupre[0]
            duv_ref[0, lo + o0:lo + o0 + on, :] = dupre[1]
            dw_t = add(dw_t, _dot(jnp.concatenate(dupre, axis=1), x[own], "tn"))
            dwd = add(dwd, _dot(act[own], dz[own], "tn"))
        _accumulate(dwu_ref.at[0], first, dw_t[:tc])
        _accumulate(dwu_ref.at[1], first, dw_t[tc:])
        _accumulate(dwd_ref, first, dwd)
        for h in range(2):
            _accumulate(dcb_ref.at[h], first, dcb[h])
            for k in range(FFN_CONV_KERNEL):
                _accumulate(dcw_ref.at[k, pl.ds(h, 1), :], first, dcw[h][k])

    act_s, cws, cbs = _ffn_specs(S, tc, nj, "jb")
    seq = pl.BlockSpec((1, S, D), lambda j, b: (b, 0, 0))
    res = pl.pallas_call(
        body, grid=(nj, B),
        in_specs=[seq, seq, pl.BlockSpec((1, S, 2 * tc), lambda j, b: (b, 0, j)),
                  pl.BlockSpec((tc, D), lambda j, b: (j, 0)), cws(0), cws(nj), cbs(0), cbs(nj)],
        out_specs=[act_s(0), act_s(0), pl.BlockSpec((2, tc, D), lambda j, b: (0, j, 0)),
                   pl.BlockSpec((tc, D), lambda j, b: (j, 0)),
                   pl.BlockSpec((FFN_CONV_KERNEL, 2, tc), lambda j, b: (0, 0, j)),
                   pl.BlockSpec((2, 1, tc), lambda j, b: (0, 0, j))],
        out_shape=[jax.ShapeDtypeStruct((B, S, DFF), BF16)] * 2
        + [jax.ShapeDtypeStruct((2, DFF, D), F32), jax.ShapeDtypeStruct((DFF, D), F32),
           jax.ShapeDtypeStruct((FFN_CONV_KERNEL, 2, DFF), F32), jax.ShapeDtypeStruct((2, 1, DFF), F32)],
        compiler_params=_params(2), name="ffn_bwd",
    )(x1b.reshape(B, S, D), dz2b.reshape(B, S, D), upre, w_down, cw, cw, cb, cb)
    flat = lambda t: t.reshape(B * S, DFF)
    return flat(res[0]), flat(res[1]), res[2], res[3], res[4], res[5]


def _ffn_bwd_seq(b, x1b3, dz2b3, upre, w_up_sh, w_down, cw, cb, prev, S, DFF):
    tc = FFN_COLS
    nj = DFF // tc
    B, _, D = x1b3.shape
    n_prev = 0 if prev is None else 5

    def body(*refs):
        (x_ref, dz_ref, up_ref, wg_ref, wv_ref, wd_ref, cwg_ref, cwv_ref, cbg_ref, cbv_ref) = refs[:10]
        prev_refs = refs[10:10 + n_prev]
        dx_hbm, dwu_ref, dwd_ref, dcw_ref, dcb_ref, acc_ref, sem = refs[10 + n_prev:]
        j = pl.program_id(0)

        @pl.when(j == 0)
        def _():
            acc_ref[...] = jnp.zeros_like(acc_ref)

        wcat = jnp.concatenate([wg_ref[0], wv_ref[0]], axis=1)
        dw_t = dwd = None
        dcb = [None, None]
        dcw = [[None] * FFN_CONV_KERNEL, [None] * FFN_CONV_KERNEL]
        add = lambda old, new: new if old is None else old + new
        for lo, hi, o0, on in _half_sequences(S):
            n = hi - lo
            own = slice(o0, o0 + on)
            row = lax.broadcasted_iota(jnp.int32, (n, tc), 0)
            x = x_ref[0, lo:hi, :]
            dz = dz_ref[0, lo:hi, :]
            ug, uv = up_ref[0, lo:hi, :tc], up_ref[0, lo:hi, tc:]
            gate = _ffn_conv(ug, cwg_ref, cbg_ref[...], row)
            val = _ffn_conv(uv, cwv_ref, cbv_ref[...], row)
            sg = _sigmoid(gate)
            act = (gate * sg * val).astype(BF16)
            dact = _dot(dz, wd_ref[...], "nt")
            dgate = dact * val * sg * (1.0 + gate * (1.0 - sg))
            dval = dact * gate * sg
            dupre = []
            for h, (dup, u, w_ref) in enumerate(((dgate, ug, cwg_ref), (dval, uv, cwv_ref))):
                dcb[h] = add(dcb[h], jnp.sum(dup[own], axis=0, keepdims=True))
                acc = jnp.zeros_like(dup)
                for k in range(FFN_CONV_KERNEL):
                    sh = FFN_CONV_KERNEL - 1 - k
                    dcw[h][k] = add(dcw[h][k], jnp.sum((dup * _shift_down(u, sh, row))[own], axis=0, keepdims=True))
                    acc = acc + w_ref[k:k + 1, :] * _shift_up(dup, sh, row)
                dupre.append(acc.astype(BF16)[own])
            dupre_cat = jnp.concatenate(dupre, axis=1)
            dw_t = add(dw_t, _dot(dupre_cat, x[own], "tn"))
            dwd = add(dwd, _dot(act[own], dz[own], "tn"))
            acc_ref[lo + o0:lo + o0 + on, :] += _dot(dupre_cat, wcat, "nt")
        if n_prev:
            _, pwu_ref, pwd_ref, pcw_ref, pcb_ref = prev_refs
            dwu_ref[0] = pwu_ref[0] + dw_t[:tc]
            dwu_ref[1] = pwu_ref[1] + dw_t[tc:]
            dwd_ref[...] = pwd_ref[...] + dwd
        else:
            dwu_ref[0] = dw_t[:tc]
            dwu_ref[1] = dw_t[tc:]
            dwd_ref[...] = dwd
        for h in range(2):
            dcb_ref[h] = dcb[h] + pcb_ref[h] if n_prev else dcb[h]
            for k in range(FFN_CONV_KERNEL):
                dcw_ref[k, h:h + 1, :] = dcw[h][k] + pcw_ref[k, h:h + 1, :] if n_prev else dcw[h][k]

        @pl.when(j == nj - 1)
        def _():
            out = pltpu.make_async_copy(acc_ref, dx_hbm.at[b], sem)
            out.start()
            out.wait()

    bps = w_up_sh.shape[2] // tc
    wspec = lambda off: pl.BlockSpec((1, D, tc), lambda j: ((off + j) // bps, 0, (off + j) % bps))
    cws = lambda off: pl.BlockSpec((FFN_CONV_KERNEL, tc), lambda j: (0, off + j))
    cbs = lambda off: pl.BlockSpec((1, tc), lambda j: (0, off + j))
    seq = pl.BlockSpec((1, S, D), lambda j: (b, 0, 0), pipeline_mode=pl.Buffered(1))
    part_specs = [pl.BlockSpec((2, tc, D), lambda j: (0, j, 0)), pl.BlockSpec((tc, D), lambda j: (j, 0)),
                  pl.BlockSpec((FFN_CONV_KERNEL, 2, tc), lambda j: (0, 0, j)), pl.BlockSpec((2, 1, tc), lambda j: (0, 0, j))]
    part_shapes = [jax.ShapeDtypeStruct((2, DFF, D), F32), jax.ShapeDtypeStruct((DFF, D), F32),
                   jax.ShapeDtypeStruct((FFN_CONV_KERNEL, 2, DFF), F32), jax.ShapeDtypeStruct((2, 1, DFF), F32)]
    in_specs = [seq, seq, pl.BlockSpec((1, S, 2 * tc), lambda j: (b, 0, j)), wspec(0), wspec(nj),
                pl.BlockSpec((tc, D), lambda j: (j, 0)), cws(0), cws(nj), cbs(0), cbs(nj)]
    operands = [x1b3, dz2b3, upre, w_up_sh, w_up_sh, w_down, cw, cw, cb, cb]
    aliases = {}
    if n_prev:
        in_specs += [HBM_SPEC] + part_specs
        operands += list(prev)
        aliases = {10 + i: i for i in range(5)}
    return pl.pallas_call(
        body, grid=(nj,), in_specs=in_specs, out_specs=[HBM_SPEC] + part_specs,
        out_shape=[jax.ShapeDtypeStruct((B, S, D), F32)] + part_shapes, input_output_aliases=aliases,
        scratch_shapes=[pltpu.VMEM((S, D), F32), pltpu.SemaphoreType.DMA],
        compiler_params=_params(1), name="ffn_bwd_seq%d" % b,
    )(*operands)


def _dx1_ln1_bwd(dupre_g, dupre_v, w_up_sh, dz2, xh1, r1, ln1_g, tm, bg):
    T, D = dz2.shape
    NS, _, cs = w_up_sh.shape
    half = NS // 2
    DFF = dupre_g.shape[1]

    def body(refs, bg_hook):
        dug_ref, duv_ref, w_ref, dz2_ref, xh_ref, r_ref, g_ref, dz_ref, dzb_ref, dg_ref, db_ref = refs
        bg_hook(False)
        first = pl.program_id(0) == 0
        acc = ALPHA * dz2_ref[...]
        for k in range(NS):
            src = dug_ref if k < half else duv_ref
            c0 = (k % half) * cs
            acc = acc + _dot(src[:, c0:c0 + cs], w_ref[k], "nt")
        dx1 = acc
        xh = xh_ref[...]
        dz = _ln_bwd(dx1, xh, r_ref[:, 0:1], g_ref[...])
        dz_ref[...] = dz
        dzb_ref[...] = dz.astype(BF16)
        _accumulate(dg_ref, first, jnp.sum(dx1 * xh, axis=0, keepdims=True))
        _accumulate(db_ref, first, jnp.sum(dx1, axis=0, keepdims=True))
        bg_hook(True)

    row = pl.BlockSpec((tm, D), lambda i: (i, 0))
    vec = pl.BlockSpec((1, D), lambda i: (0, 0))
    du = pl.BlockSpec((tm, DFF), lambda i: (i, 0))
    return _hosted_call(
        body, bg, grid=(T // tm,),
        in_specs=[du, du, pl.BlockSpec((NS, D, cs), lambda i: (0, 0, 0), pipeline_mode=pl.Buffered(1)),
                  row, row, pl.BlockSpec((tm, LANES), lambda i: (i, 0)), vec],
        out_specs=[row, row, vec, vec],
        out_shape=[jax.ShapeDtypeStruct((T, D), F32), jax.ShapeDtypeStruct((T, D), BF16),
                   jax.ShapeDtypeStruct((1, D), F32), jax.ShapeDtypeStruct((1, D), F32)],
        scratch_shapes=[], operands=[dupre_g, dupre_v, w_up_sh, dz2, xh1, r1, ln1_g], name="mm_dx1_ln1_bwd")


def _transpose(x, name):
    R, C = x.shape
    tr = LANES if R % LANES == 0 else R

    def body(x_ref, o_ref):
        o_ref[...] = x_ref[...].T

    return pl.pallas_call(
        body, grid=(R // tr,), in_specs=[pl.BlockSpec((tr, C), lambda i: (i, 0))],
        out_specs=pl.BlockSpec((C, tr), lambda i: (0, i)), out_shape=jax.ShapeDtypeStruct((C, R), F32),
        compiler_params=_params(1), name=name)(x)


def _dh_cat(dq, dk, dv, dag, tm):
    T, AW = dq.shape
    CW2 = dag.shape[1]
    W = 3 * AW + CW2

    def body(dq_ref, dk_ref, dv_ref, dag_ref, dh_ref, cs_ref):
        for c, ref in enumerate((dq_ref, dk_ref, dv_ref)):
            dh_ref[:, c * AW:(c + 1) * AW] = ref[...]
        dg = dag_ref[...]
        dh_ref[:, 3 * AW:] = dg.astype(BF16)
        _accumulate(cs_ref, pl.program_id(0) == 0, jnp.sum(dg, axis=0, keepdims=True))

    row = pl.BlockSpec((tm, AW), lambda i: (i, 0))
    return pl.pallas_call(
        body, grid=(T // tm,),
        in_specs=[row] * 3 + [pl.BlockSpec((tm, CW2), lambda i: (i, 0))],
        out_specs=[pl.BlockSpec((tm, W), lambda i: (i, 0)), pl.BlockSpec((1, CW2), lambda i: (0, 0))],
        out_shape=[jax.ShapeDtypeStruct((T, W), BF16), jax.ShapeDtypeStruct((1, CW2), F32)],
        compiler_params=_params(1), name="dh_cat",
    )(dq, dk, dv, dag)


def _local_step(x, target, rel_table, w_in, b_in, conv_w, conv_b, conv_ln_g, conv_ln_b, attn_norm_g,
                conv_norm_g, staged, ln1_g, ln1_b, ffn_cw, ffn_cb, ln2_g, ln2_b, ids):
    B, S, D = x.shape
    T = B * S
    AW = attn_norm_g.shape[-1]
    CW = conv_norm_g.shape[-1]
    H = AW // HEAD_DIM
    DFF = staged[2].shape[0] * staged[2].shape[1]
    INW = 3 * AW + 2 * CW
    xf = x.reshape(T, D)
    tf = target.reshape(T, D)
    tm = _row_tile(T, 512)
    tm_s = _row_tile(T, 256)

    bucket_np, mask_np = _bucket_tables()
    bucket = jnp.asarray(bucket_np)
    band_mask = jnp.asarray(mask_np)
    bias_all = _bias_build(rel_table.T, bucket, band_mask).reshape(3, H, ATTN_BLOCK, 2 * ATTN_BLOCK)

    tn_qkv = _col_tile(3 * AW, 1152)
    qkv = _mm_plain(xf, w_in[:, :3 * AW], mode="nn", tm=tm, tn=tn_qkv, tk=D, out_dtype=BF16,
                    bias=b_in[:, :3 * AW], name="mm_qkv")
    ag = _mm_plain(xf, w_in[:, 3 * AW:], mode="nn", tm=tm, tn=2 * CW, tk=D, out_dtype=F32,
                   bias=b_in[:, 3 * AW:], name="mm_ag")

    attn, lse, w_out_g, w_up_sh, w_down_g = _attention_fwd(qkv, bias_all, B, S, AW, bg=_bg_gather(staged))
    w_out = w_out_g.reshape(D, D)
    w_down = w_down_g.reshape(DFF, D)
    mixed_a, r_attn = _attn_norm(attn, attn_norm_g, tm_s)
    mixed_c = _conv_fwd(ag, conv_w, conv_b, conv_ln_g, conv_ln_b, conv_norm_g, B, S, CW)
    mixed = jnp.concatenate([mixed_a, mixed_c], axis=1)

    def ln1_epilogue(acc, i, j, extra_refs, out_refs):
        x_ref, g_ref, b_ref = extra_refs
        x1, xh, r = _ln_fwd(acc + ALPHA * x_ref[...], g_ref[...], b_ref[...])
        out_refs[0][...] = x1
        out_refs[1][...] = x1.astype(BF16)
        out_refs[2][...] = xh
        out_refs[3][...] = jnp.broadcast_to(r, (tm_s, LANES))

    rowD = lambda i, j, k: (i, 0)
    vecD = lambda i, j, k: (0, 0)
    x1, x1b, xh1, r1 = _matmul(
        mixed, w_out, mode="nn", tm=tm_s, tn=D, tk=D,
        extras=[(xf, (tm_s, D), rowD), (ln1_g, (1, D), vecD), (ln1_b, (1, D), vecD)],
        outs=[((T, D), F32, (tm_s, D), rowD), ((T, D), BF16, (tm_s, D), rowD), ((T, D), F32, (tm_s, D), rowD),
              ((T, LANES), F32, (tm_s, LANES), rowD)],
        epilogue=ln1_epilogue, name="mm_out_ln1")

    NS, _, cs = w_up_sh.shape
    half = NS // 2

    act, upre = _ffn_fwd_fused(x1b, w_up_sh, ffn_cw, ffn_cb, B, S, DFF)

    def ln2_epilogue(acc, i, j, extra_refs, out_refs):
        x1_ref, g_ref, b_ref, t_ref = extra_refs
        dz_ref, dzb_ref, loss_ref, dg_ref, db_ref = out_refs
        g = g_ref[...]
        y, xh, r = _ln_fwd(acc + ALPHA * x1_ref[...], g, b_ref[...])
        diff = y - t_ref[...]
        row_loss = jnp.sum(diff * diff, axis=1, keepdims=True)
        tile_loss = jnp.sum(row_loss, axis=0, keepdims=True) * (0.5 / D)
        dy = diff * (1.0 / D)
        dz = _ln_bwd(dy, xh, r, g)
        dz_ref[...] = dz
        dzb_ref[...] = dz.astype(BF16)
        first = i == 0
        _accumulate(loss_ref, first, jnp.broadcast_to(tile_loss, (1, LANES)))
        _accumulate(dg_ref, first, jnp.sum(dy * xh, axis=0, keepdims=True))
        _accumulate(db_ref, first, jnp.sum(dy, axis=0, keepdims=True))

    dz2, dz2b, loss_part, d_ln2_g, d_ln2_b = _matmul(
        act, w_down, mode="nn", tm=tm, tn=D, tk=DFF,
        extras=[(x1, (tm, D), rowD), (ln2_g, (1, D), vecD), (ln2_b, (1, D), vecD), (tf, (tm, D), rowD)],
        outs=[((T, D), F32, (tm, D), rowD), ((T, D), BF16, (tm, D), rowD),
              ((1, LANES), F32, (1, LANES), vecD), ((1, D), F32, (1, D), vecD), ((1, D), F32, (1, D), vecD)],
        epilogue=ln2_epilogue, name="mm_down_ln2_loss")

    dupre_g, dupre_v, d_w_up_t, d_w_down, d_ffn_cw2, d_ffn_cb2 = _ffn_bwd_fused(
        x1b, dz2b, upre, w_down, ffn_cw, ffn_cb, B, S, DFF)
    d_w_up_t = d_w_up_t.reshape(NS, cs, D)
    d_ffn_cw = d_ffn_cw2.reshape(FFN_CONV_KERNEL, 2 * DFF)
    d_ffn_cb = d_ffn_cb2.reshape(1, 2 * DFF)
    tk_t = _row_tile(T, 512)

    early = [d_w_up_t, d_w_down.reshape(NS, DFF // NS, D)]
    dz1, dz1b, d_ln1_g, d_ln1_b, *sib_e = _dx1_ln1_bwd(dupre_g, dupre_v, w_up_sh, dz2, xh1, r1, ln1_g, tm,
                                                       bg=_bg_sibling_exchange(early))
    chip_e = [_pair_sum(g, s, ids, name="pair_sum_" + n) for g, s, n in zip(early, sib_e, ("w_up", "w_down"))]

    d_w_out = _mm_plain(mixed, dz1b, mode="tn", tm=D, tn=D, tk=tk_t, out_dtype=F32, name="mm_dw_out")
    early.append(d_w_out.reshape(NS, D // NS, D))
    ones_bd = jnp.asarray(np.kron(np.eye(H, dtype=np.float32), np.ones((HEAD_DIM, HEAD_DIM), np.float32)), dtype=BF16)

    def dmixed_epilogue(acc, i, j, extra_refs, out_refs):
        a_ref, r_ref, g_ref, ones_ref = extra_refs
        do_ref, dd_ref, dmc_ref, dg_ref = out_refs
        dm = acc[:, :AW]
        dmc_ref[...] = acc[:, AW:]
        a = a_ref[...]
        r = r_ref[:, 0:1]
        dxn = dm * g_ref[...]
        da = r * (dxn - a * (r * r) * jnp.mean(dxn * a, axis=-1, keepdims=True))
        do_ref[...] = da.astype(BF16)
        hi, lo = _split_hi_lo(da * a)
        dd_ref[...] = (jnp.dot(hi, ones_ref[...], preferred_element_type=F32)
                       + jnp.dot(lo, ones_ref[...], preferred_element_type=F32))
        _accumulate(dg_ref, i == 0, jnp.sum(dm * a * r, axis=0, keepdims=True))

    dattn, dd, dmc, d_attn_norm_g, sib_out = _matmul(
        dz1b, w_out, mode="nt", tm=tm, tn=D, tk=D,
        extras=[(attn, (tm, AW), rowD), (r_attn, (tm, LANES), rowD), (attn_norm_g, (1, AW), vecD),
                (ones_bd, (AW, AW), vecD)],
        outs=[((T, AW), BF16, (tm, AW), rowD), ((T, AW), F32, (tm, AW), rowD), ((T, CW), F32, (tm, CW), rowD),
              ((1, AW), F32, (1, AW), vecD)],
        epilogue=dmixed_epilogue, name="mm_dmixed", bg=_bg_sibling_exchange(early[2:]))
    sib_e.append(sib_out)
    chip_e.append(_pair_sum(early[2], sib_out, ids, name="pair_sum_w_out"))

    dag, d_conv_w, d_conv_b, d_conv_ln_g, d_conv_ln_b, d_conv_norm_g = _conv_bwd(
        ag, dmc, conv_w, conv_b, conv_ln_g, conv_ln_b, conv_norm_g, B, S, CW)

    dq, dk, dv, csq, csk, csv, dbias, *got_e = _attention_bwd(qkv, dattn, lse, dd, bias_all, B, S, AW,
                                                              bg=_bg_chip_exchange(chip_e))
    full_up, full_down, full_out = [_final_sum(g, s, r, ids, name="final_sum_" + n)
                                    for g, s, r, n in zip(early, sib_e, got_e, ("w_up", "w_down", "w_out"))]
    d_rel_table = _rel_grad(dbias.reshape(3, H, ATTN_BLOCK * 2 * ATTN_BLOCK), bucket).T
    dh, cs_ag = _dh_cat(dq, dk, dv, dag, tm_s)
    d_b_in = jnp.concatenate([csq, csk, csv, cs_ag], axis=1)

    d_w_in_t = _mm_plain(dh, xf, mode="tn", tm=_col_tile(INW, 1408), tn=D, tk=tk_t, out_dtype=F32, name="mm_dw_in")
    late = [d_w_in_t.reshape(NS, INW // NS, D)]
    sib_l = _sibling_exchange(late)
    chip_l = [_pair_sum(late[0], sib_l[0], ids, name="pair_sum_w_in")]
    small = dict(rel_table=d_rel_table, b_in=d_b_in, conv_w=d_conv_w, conv_b=d_conv_b, conv_ln_g=d_conv_ln_g,
                 conv_ln_b=d_conv_ln_b, attn_norm_g=d_attn_norm_g, conv_norm_g=d_conv_norm_g, ln1_g=d_ln1_g,
                 ln1_b=d_ln1_b, ffn_conv_w=d_ffn_cw, ffn_conv_b=d_ffn_cb, ln2_g=d_ln2_g, ln2_b=d_ln2_b)
    pack = _pack([loss_part] + [small[n] for n in SMALL_NAMES])

    def gx_epilogue(acc, i, j, extra_refs, out_refs):
        out_refs[0][...] = acc + ALPHA * extra_refs[0][...]

    grad_x, got_in, all_packs = _matmul(
        dh, w_in, mode="nt", tm=tm, tn=D, tk=INW, extras=[(dz1, (tm, D), rowD)],
        outs=[((T, D), F32, (tm, D), rowD)], epilogue=gx_epilogue, name="mm_grad_x",
        bg=_bg_chip_exchange(chip_l, pack))
    full_in = _final_sum(late[0], sib_l[0], got_in, ids, name="final_sum_w_in")
    return grad_x.reshape(B, S, D), [full_in, full_out, full_up, full_down], all_packs


def _place():
    return lax.axis_index("x"), lax.axis_index("y"), lax.axis_index("c")


CHIP_FLIPS = ((1, 0), (0, 1), (1, 1))


def _flip(v, f):
    return 1 - v if f else v


HBM_SPEC = pl.BlockSpec(memory_space=pl.ANY)
VMEM_SPEC = pl.BlockSpec(memory_space=pltpu.VMEM)
COMM_PARAMS = pltpu.CompilerParams(vmem_limit_bytes=VMEM_LIMIT)


def _gather_weights(big, small):
    nb, ns = len(big), len(small)

    def body(*refs):
        big_in = refs[:nb]
        small_in = refs[nb:nb + ns]
        big_out = refs[nb + ns:2 * nb + ns]
        small_out = refs[2 * nb + ns:2 * nb + 2 * ns]
        stages = refs[2 * nb + 2 * ns:3 * nb + 2 * ns]
        send_sems, recv_sems, local_sems = refs[3 * nb + 2 * ns:]
        x, y, c = _place()
        s_me = 2 * x + y
        sibling = (x, y, 1 - c)
        started, local_copies = [], []
        for a in range(nb):
            rh = big[a].shape[0] // 2
            lo = pl.multiple_of(c * rh, 16)
            stages[a][...] = big_in[a][pl.ds(lo, rh), :].astype(BF16)
            mine = big_out[a].at[s_me, pl.ds(lo, rh), :]
            loc = pltpu.make_async_copy(stages[a], mine, local_sems.at[a])
            loc.start()
            local_copies.append(loc)
            targets = [sibling] + [(_flip(x, fx), _flip(y, fy), c) for fx, fy in CHIP_FLIPS]
            for k, to in enumerate(targets):
                cp = pltpu.make_async_remote_copy(stages[a], mine, send_sems.at[a * 7 + k],
                                                  recv_sems.at[a * 7 + k], device_id=to, device_id_type=MESH)
                cp.start()
                started.append(cp)
        for a in range(ns):
            mine = small_out[a].at[s_me]
            loc = pltpu.make_async_copy(small_in[a], mine, local_sems.at[nb + a])
            loc.start()
            local_copies.append(loc)
            for k, (fx, fy) in enumerate(CHIP_FLIPS):
                cp = pltpu.make_async_remote_copy(small_in[a], mine, send_sems.at[nb * 7 + a * 3 + k],
                                                  recv_sems.at[nb * 7 + a * 3 + k],
                                                  device_id=(_flip(x, fx), _flip(y, fy), c), device_id_type=MESH)
                cp.start()
                started.append(cp)
        for a in range(nb):
            rh = big[a].shape[0] // 2
            lo = pl.multiple_of(c * rh, 16)
            for k, (fx, fy) in enumerate(CHIP_FLIPS):
                s_from = 2 * _flip(x, fx) + _flip(y, fy)
                got = big_out[a].at[s_from, pl.ds(lo, rh), :]
                pltpu.make_async_remote_copy(got, got, send_sems.at[a * 7 + 1 + k], recv_sems.at[a * 7 + 1 + k],
                                             device_id=sibling, device_id_type=MESH).wait_recv()
                fwd = pltpu.make_async_remote_copy(got, got, send_sems.at[a * 7 + 4 + k],
                                                   recv_sems.at[a * 7 + 4 + k], device_id=sibling,
                                                   device_id_type=MESH)
                fwd.start()
                started.append(fwd)
        for a in range(nb):
            rh = big[a].shape[0] // 2
            lo_sib = pl.multiple_of((1 - c) * rh, 16)
            for k in (0, 4, 5, 6):
                any_rows = big_out[a].at[s_me, pl.ds(lo_sib, rh), :]
                pltpu.make_async_remote_copy(any_rows, any_rows, send_sems.at[a * 7 + k], recv_sems.at[a * 7 + k],
                                             device_id=sibling, device_id_type=MESH).wait_recv()
        for a in range(ns):
            for k in range(3):
                pltpu.make_async_remote_copy(small_in[a], small_out[a].at[s_me], send_sems.at[nb * 7 + a * 3 + k],
                                             recv_sems.at[nb * 7 + a * 3 + k], device_id=sibling,
                                             device_id_type=MESH).wait_recv()
        for cp in started:
            cp.wait_send()
        for cp in local_copies:
            cp.wait()

    n_sem = nb * 7 + ns * 3
    out_shape = ([jax.ShapeDtypeStruct((N_SHARDS,) + w.shape, BF16) for w in big]
                 + [jax.ShapeDtypeStruct((N_SHARDS,) + w.shape, F32) for w in small])
    res = pl.pallas_call(
        body, in_specs=[VMEM_SPEC] * nb + [HBM_SPEC] * ns, out_specs=[HBM_SPEC] * (nb + ns),
        out_shape=out_shape,
        scratch_shapes=[pltpu.VMEM((w.shape[0] // 2, w.shape[1]), BF16) for w in big]
        + [pltpu.SemaphoreType.DMA((n_sem,)), pltpu.SemaphoreType.DMA((n_sem,)),
           pltpu.SemaphoreType.DMA((nb + ns,))],
        compiler_params=COMM_PARAMS, name="gather_weights",
    )(*big, *small)
    return res[:nb], res[nb:]


def _sibling_exchange(grads):
    n = len(grads)

    def body(*refs):
        g_in = refs[:n]
        got = refs[n:2 * n]
        send_sems, recv_sems = refs[2 * n:]
        x, y, c = _place()
        cps = []
        for a in range(n):
            rh = grads[a].shape[1] // 2
            lo = pl.multiple_of((1 - c) * rh, 8)
            cp = pltpu.make_async_remote_copy(g_in[a].at[:, pl.ds(lo, rh), :], got[a], send_sems.at[a],
                                              recv_sems.at[a], device_id=(x, y, 1 - c), device_id_type=MESH)
            cp.start()
            cps.append(cp)
        for cp in cps:
            cp.wait()

    return pl.pallas_call(
        body, in_specs=[HBM_SPEC] * n, out_specs=[HBM_SPEC] * n,
        out_shape=[jax.ShapeDtypeStruct((N_SHARDS, g.shape[1] // 2, g.shape[2]), F32) for g in grads],
        scratch_shapes=[pltpu.SemaphoreType.DMA((n,)), pltpu.SemaphoreType.DMA((n,))],
        compiler_params=COMM_PARAMS, name="sibling_exchange",
    )(*grads)


def _chip_exchange(chip_parts, pack):
    n = len(chip_parts)

    def body(*refs):
        parts = refs[:n]
        pack_ref = refs[n]
        got = refs[n + 1:2 * n + 1]
        all_packs = refs[2 * n + 1]
        send_sems, recv_sems, local_sem = refs[2 * n + 2:]
        x, y, c = _place()
        me = 4 * x + 2 * y + c
        cps = []
        for a in range(n):
            for k, (fx, fy) in enumerate(CHIP_FLIPS):
                px, py = _flip(x, fx), _flip(y, fy)
                cp = pltpu.make_async_remote_copy(parts[a].at[2 * px + py], got[a].at[k], send_sems.at[a * 3 + k],
                                                  recv_sems.at[a * 3 + k], device_id=(px, py, c),
                                                  device_id_type=MESH)
                cp.start()
                cps.append(cp)
        loc = pltpu.make_async_copy(pack_ref, all_packs.at[me], local_sem)
        loc.start()
        for m in range(1, N_DEV):
            to = (_flip(x, m & 4), _flip(y, m & 2), _flip(c, m & 1))
            cp = pltpu.make_async_remote_copy(pack_ref, all_packs.at[me], send_sems.at[n * 3 + m - 1],
                                              recv_sems.at[n * 3 + m - 1], device_id=to, device_id_type=MESH)
            cp.start()
            cps.append(cp)
        for cp in cps:
            cp.wait()
        loc.wait()

    rs = pack.shape[0]
    res = pl.pallas_call(
        body, in_specs=[HBM_SPEC] * (n + 1), out_specs=[HBM_SPEC] * (n + 1),
        out_shape=[jax.ShapeDtypeStruct((3,) + p.shape[1:], BF16) for p in chip_parts]
        + [jax.ShapeDtypeStruct((N_DEV, rs, LANES), F32)],
        scratch_shapes=[pltpu.SemaphoreType.DMA((n * 3 + N_DEV - 1,)), pltpu.SemaphoreType.DMA((n * 3 + N_DEV - 1,)),
                        pltpu.SemaphoreType.DMA],
        compiler_params=COMM_PARAMS, name="chip_exchange",
    )(*chip_parts, pack)
    return res[:n], res[n]


def _sibling_assemble(fulls):
    n = len(fulls)

    def body(*refs):
        full = refs[n:2 * n]
        send_sems, recv_sems = refs[2 * n:]
        x, y, c = _place()
        cps = []
        for a in range(n):
            rh = fulls[a].shape[0] // 2
            mine = full[a].at[pl.ds(pl.multiple_of(c * rh, 8), rh), :]
            cp = pltpu.make_async_remote_copy(mine, mine, send_sems.at[a], recv_sems.at[a],
                                              device_id=(x, y, 1 - c), device_id_type=MESH)
            cp.start()
            cps.append(cp)
        for cp in cps:
            cp.wait()

    return pl.pallas_call(
        body, in_specs=[HBM_SPEC] * n, out_specs=[HBM_SPEC] * n,
        out_shape=[jax.ShapeDtypeStruct(f.shape, F32) for f in fulls],
        input_output_aliases={a: a for a in range(n)},
        scratch_shapes=[pltpu.SemaphoreType.DMA((n,)), pltpu.SemaphoreType.DMA((n,))],
        compiler_params=COMM_PARAMS, name="sibling_assemble",
    )(*fulls)


def _remote(ref_src, ref_dst, send_sems, recv_sems, k, to):
    return pltpu.make_async_remote_copy(ref_src, ref_dst, send_sems.at[k], recv_sems.at[k], device_id=to,
                                        device_id_type=MESH)


def _stage_half(w, ids, name):
    R, C = w.shape
    rh = R // 2
    rt = _half_tile(rh)
    nt = rh // rt

    def body(ids_ref, w_ref, o_ref):
        o_ref[0] = w_ref[...].astype(BF16)

    grid_spec = pltpu.PrefetchScalarGridSpec(
        num_scalar_prefetch=1, grid=(nt,),
        in_specs=[pl.BlockSpec((rt, C), lambda i, ids: (ids[2] * nt + i, 0))],
        out_specs=pl.BlockSpec((1, rt, C), lambda i, ids: (2 * ids[0] + ids[1], ids[2] * nt + i, 0)))
    return pl.pallas_call(body, grid_spec=grid_spec, out_shape=jax.ShapeDtypeStruct((N_SHARDS, R, C), BF16),
                          compiler_params=_params(1), name=name)(ids, w)


def _bg_gather(staged):
    n = len(staged)

    def run(step, n_steps, ins, outs, send_sems, recv_sems, local_sems, post):
        x, y, c = _place()
        s_me = 2 * x + y
        sibling = (x, y, 1 - c)
        chips = [(_flip(x, fx), _flip(y, fy)) for fx, fy in CHIP_FLIPS]

        def rows(a, s, half):
            rh = staged[a].shape[1] // 2
            return outs[a].at[s, pl.ds(pl.multiple_of(half * rh, 16), rh), :]

        def copy(a, k, ref, to):
            return _remote(ref, ref, send_sems, recv_sems, a * 7 + k, to)

        if not post:
            @pl.when(step == 0)
            def _():
                for a in range(n):
                    mine = rows(a, s_me, c)
                    copy(a, 0, mine, sibling).start()
                    for k, (px, py) in enumerate(chips):
                        copy(a, 1 + k, mine, (px, py, c)).start()

            @pl.when(step == max(n_steps - 2, 0))
            def _():
                for a in range(n):
                    for k, (px, py) in enumerate(chips):
                        got = rows(a, 2 * px + py, c)
                        copy(a, 1 + k, got, sibling).wait_recv()
                        copy(a, 4 + k, got, sibling).start()
        else:
            @pl.when(step == n_steps - 1)
            def _():
                for a in range(n):
                    for k in (0, 4, 5, 6):
                        copy(a, k, rows(a, s_me, 1 - c), sibling).wait_recv()
                    for k in range(7):
                        copy(a, k, rows(a, s_me, c), sibling).wait_send()

    return _Background(staged, [jax.ShapeDtypeStruct(g.shape, g.dtype) for g in staged],
                       {a: a for a in range(n)}, 7 * n, run)


def _bg_sibling_exchange(grads):
    n = len(grads)

    def run(step, n_steps, ins, outs, send_sems, recv_sems, local_sems, post):
        x, y, c = _place()

        def copy(a):
            rh = grads[a].shape[1] // 2
            lo = pl.multiple_of((1 - c) * rh, 8)
            return _remote(ins[a].at[:, pl.ds(lo, rh), :], outs[a], send_sems, recv_sems, a, (x, y, 1 - c))

        if not post:
            @pl.when(step == 0)
            def _():
                for a in range(n):
                    copy(a).start()
        else:
            @pl.when(step == n_steps - 1)
            def _():
                for a in range(n):
                    copy(a).wait()

    return _Background(grads, [jax.ShapeDtypeStruct((N_SHARDS, g.shape[1] // 2, g.shape[2]), F32) for g in grads],
                       {}, n, run)


def _bg_chip_exchange(chip_parts, pack=None):
    n = len(chip_parts)

    def run(step, n_steps, ins, outs, send_sems, recv_sems, local_sems, post):
        x, y, c = _place()
        me = 4 * x + 2 * y + c

        def copies():
            cps = []
            for a in range(n):
                for k, (fx, fy) in enumerate(CHIP_FLIPS):
                    px, py = _flip(x, fx), _flip(y, fy)
                    cps.append(_remote(ins[a].at[2 * px + py], outs[a].at[k], send_sems, recv_sems, a * 3 + k,
                                       (px, py, c)))
            if pack is not None:
                for m in range(1, N_DEV):
                    to = (_flip(x, m & 4), _flip(y, m & 2), _flip(c, m & 1))
                    cps.append(_remote(ins[n], outs[n].at[me], send_sems, recv_sems, n * 3 + m - 1, to))
            return cps

        def local():
            return pltpu.make_async_copy(ins[n], outs[n].at[me], local_sems.at[0])

        if not post:
            @pl.when(step == 0)
            def _():
                for cp in copies():
                    cp.start()
                if pack is not None:
                    local().start()
        else:
            @pl.when(step == n_steps - 1)
            def _():
                for cp in copies():
                    cp.wait()
                if pack is not None:
                    local().wait()

    in_arrays = list(chip_parts) + ([pack] if pack is not None else [])
    out_shapes = [jax.ShapeDtypeStruct((3,) + p.shape[1:], BF16) for p in chip_parts]
    if pack is not None:
        out_shapes.append(jax.ShapeDtypeStruct((N_DEV, pack.shape[0], LANES), F32))
    return _Background(in_arrays, out_shapes, {}, n * 3 + N_DEV - 1, run)


def _half_tile(rh, mult=16, want=256):
    best = None
    for t in range(mult, min(rh, want) + 1, mult):
        if rh % t == 0:
            best = t
    return best if best is not None else rh


def _pair_sum(g, sib, ids, name):
    _, R, C = g.shape
    rh = R // 2
    rt = _half_tile(rh)
    nt = rh // rt

    def body(ids_ref, g_ref, s_ref, o_ref):
        o_ref[...] = (g_ref[...] + s_ref[...]).astype(BF16)

    grid_spec = pltpu.PrefetchScalarGridSpec(
        num_scalar_prefetch=1, grid=(N_SHARDS, nt),
        in_specs=[pl.BlockSpec((1, rt, C), lambda s, i, ids: (s, ids[2] * nt + i, 0)),
                  pl.BlockSpec((1, rt, C), lambda s, i, ids: (s, i, 0))],
        out_specs=pl.BlockSpec((1, rt, C), lambda s, i, ids: (s, i, 0)))
    return pl.pallas_call(body, grid_spec=grid_spec, out_shape=jax.ShapeDtypeStruct((N_SHARDS, rh, C), BF16),
                          compiler_params=_params(2), name=name)(ids, g, sib)


def _final_sum(g, sib, got, ids, name):
    _, R, C = g.shape
    rh = R // 2
    rt = _half_tile(rh)
    nt = rh // rt

    def body(ids_ref, g_ref, s_ref, r_ref, o_ref):
        tot = g_ref[0] + s_ref[0]
        for k in range(3):
            tot = tot + r_ref[k].astype(F32)
        o_ref[...] = tot

    grid_spec = pltpu.PrefetchScalarGridSpec(
        num_scalar_prefetch=1, grid=(nt,),
        in_specs=[pl.BlockSpec((1, rt, C), lambda i, ids: (2 * ids[0] + ids[1], ids[2] * nt + i, 0)),
                  pl.BlockSpec((1, rt, C), lambda i, ids: (2 * ids[0] + ids[1], i, 0)),
                  pl.BlockSpec((3, rt, C), lambda i, ids: (0, i, 0))],
        out_specs=pl.BlockSpec((rt, C), lambda i, ids: (ids[2] * nt + i, 0)))
    return pl.pallas_call(body, grid_spec=grid_spec, out_shape=jax.ShapeDtypeStruct((R, C), F32),
                          compiler_params=_params(1), name=name)(ids, g, sib, got)


def _sum_packs(all_packs):
    def body(p_ref, o_ref):
        tot = p_ref[0]
        for i in range(1, N_DEV):
            tot = tot + p_ref[i]
        o_ref[...] = tot

    return pl.pallas_call(body, in_specs=[VMEM_SPEC], out_specs=VMEM_SPEC,
                          out_shape=jax.ShapeDtypeStruct(all_packs.shape[1:], F32), name="sum_packs")(all_packs)


def _adamw(w, g, m, v, name):
    R, C = w.shape
    rt = _half_tile(R, mult=8, want=256)

    def body(w_ref, g_ref, m_ref, v_ref, d_ref, nm_ref, nv_ref):
        gg = g_ref[...]
        nm = ADAM_B1 * m_ref[...] + (1.0 - ADAM_B1) * gg
        nv = ADAM_B2 * v_ref[...] + (1.0 - ADAM_B2) * (gg * gg)
        m_hat = nm / (1.0 - ADAM_B1 ** ADAM_STEP)
        v_hat = nv / (1.0 - ADAM_B2 ** ADAM_STEP)
        d_ref[...] = -ADAM_LR * (m_hat / (jnp.sqrt(v_hat) + ADAM_EPS) + ADAM_WD * w_ref[...])
        nm_ref[...] = nm
        nv_ref[...] = nv

    spec = pl.BlockSpec((rt, C), lambda i: (i, 0))
    return pl.pallas_call(body, grid=(R // rt,), in_specs=[spec] * 4, out_specs=[spec] * 3,
                          out_shape=[jax.ShapeDtypeStruct((R, C), F32)] * 3,
                          compiler_params=_params(1), name=name)(w, g, m, v)


def _adamw_update(w, g, m, v):
    nm = ADAM_B1 * m + (1.0 - ADAM_B1) * g
    nv = ADAM_B2 * v + (1.0 - ADAM_B2) * (g * g)
    m_hat = nm / (1.0 - ADAM_B1 ** ADAM_STEP)
    v_hat = nv / (1.0 - ADAM_B2 ** ADAM_STEP)
    return -ADAM_LR * (m_hat / (jnp.sqrt(v_hat) + ADAM_EPS) + ADAM_WD * w), nm, nv


def _adamw_many(ws, gs, ms, vs, name):
    n = len(ws)

    def body(*refs):
        for i in range(n):
            d, nm, nv = _adamw_update(refs[i][...], refs[n + i][...], refs[2 * n + i][...], refs[3 * n + i][...])
            refs[4 * n + i][...] = d
            refs[5 * n + i][...] = nm
            refs[6 * n + i][...] = nv

    return pl.pallas_call(body, in_specs=[VMEM_SPEC] * (4 * n), out_specs=[VMEM_SPEC] * (3 * n),
                          out_shape=[jax.ShapeDtypeStruct(w.shape, F32) for w in ws] * 3, name=name,
                          )(*ws, *gs, *ms, *vs)


def _pack(pieces):
    rows = []
    for p in pieces:
        flat = p.reshape(-1)
        pad = (-flat.shape[0]) % LANES
        if pad:
            flat = jnp.concatenate([flat, jnp.zeros((pad,), F32)])
        rows.append(flat.reshape(-1, LANES))
    total = sum(r.shape[0] for r in rows)
    pad_rows = (-total) % 8
    if pad_rows:
        rows.append(jnp.zeros((pad_rows, LANES), F32))
    return jnp.concatenate(rows, axis=0)


def _unpack(buf, shapes):
    out, r0 = [], 0
    for shp in shapes:
        n = int(np.prod(shp))
        nr = -(-n // LANES)
        out.append(buf[r0:r0 + nr].reshape(-1)[:n].reshape(shp))
        r0 += nr
    return out


SMALL_NAMES = ("rel_table", "b_in", "conv_w", "conv_b", "conv_ln_g", "conv_ln_b", "attn_norm_g", "conv_norm_g",
               "ln1_g", "ln1_b", "ffn_conv_w", "ffn_conv_b", "ln2_g", "ln2_b")
BIG_NAMES = ("w_in", "w_out", "w_up", "w_down")
WEIGHT_ORDER = ("rel_table", "w_in", "b_in", "conv_w", "conv_b", "conv_ln_g", "conv_ln_b", "attn_norm_g",
                "conv_norm_g", "w_out", "ln1_g", "ln1_b", "w_up", "ffn_conv_w", "ffn_conv_b", "w_down",
                "ln2_g", "ln2_b")


def kernel(x, rel_table, w_in, b_in, conv_w, conv_b, conv_ln_g, conv_ln_b, attn_norm_g, conv_norm_g, w_out, ln1_g, ln1_b, w_up, ffn_conv_w, ffn_conv_b, w_down, ln2_g, ln2_b, loss_target, m_rel_table, m_w_in, m_b_in, m_conv_w, m_conv_b, m_conv_ln_g, m_conv_ln_b, m_attn_norm_g, m_conv_norm_g, m_w_out, m_ln1_g, m_ln1_b, m_w_up, m_ffn_conv_w, m_ffn_conv_b, m_w_down, m_ln2_g, m_ln2_b, v_rel_table, v_w_in, v_b_in, v_conv_w, v_conv_b, v_conv_ln_g, v_conv_ln_b, v_attn_norm_g, v_conv_norm_g, v_w_out, v_ln1_g, v_ln1_b, v_w_up, v_ffn_conv_w, v_ffn_conv_b, v_w_down, v_ln2_g, v_ln2_b):
    args = dict(locals())
    weights = {n: args[n] for n in WEIGHT_ORDER}
    moms = {n: args["m_" + n] for n in WEIGHT_ORDER}
    vels = {n: args["v_" + n] for n in WEIGHT_ORDER}
    xi, yi, ci = _place()
    ids = jnp.stack([xi, yi, ci]).astype(jnp.int32)
    shard = 2 * xi + yi
    D = x.shape[-1]
    DFF = w_down.shape[1] * N_SHARDS
    CW = conv_norm_g.shape[-1]

    (g_in,), (g_cw, g_fcw) = _gather_weights([w_in[0]], [conv_w[0], ffn_conv_w[0]])
    cols = lambda t: jnp.transpose(t, (1, 0, 2)).reshape(t.shape[1], N_SHARDS * t.shape[2])
    staged = [_stage_half(w[0], ids, name="stage_" + n) for w, n in ((w_out, "w_out"), (w_up, "w_up"),
                                                                     (w_down, "w_down"))]

    grad_x, fulls, all_packs = _local_step(
        x, loss_target, rel_table, cols(g_in), b_in, cols(g_cw), conv_b, conv_ln_g, conv_ln_b, attn_norm_g,
        conv_norm_g, staged, ln1_g, ln1_b, cols(g_fcw), ffn_conv_b, ln2_g, ln2_b, ids)
    big_grads = dict(zip(BIG_NAMES, _sibling_assemble(fulls)))
    for n in ("w_in", "w_up"):
        big_grads[n] = _transpose(big_grads[n], name="transpose_d" + n)

    summed = _sum_packs(all_packs)
    full_shapes = {n: weights[n].shape for n in SMALL_NAMES}
    full_shapes["conv_w"] = (1, CONV_KERNEL, CW)
    full_shapes["ffn_conv_w"] = (1, FFN_CONV_KERNEL, 2 * DFF)
    un = _unpack(summed, [(1, LANES)] + [full_shapes[n] for n in SMALL_NAMES])
    loss = un[0][0, 0]
    small_grads = dict(zip(SMALL_NAMES, un[1:]))
    for n in ("conv_w", "ffn_conv_w"):
        width = weights[n].shape[-1]
        small_grads[n] = lax.dynamic_slice_in_dim(small_grads[n], shard * width, width, axis=2)

    grads, delta, new_m, new_v = {}, {}, {}, {}
    for n in BIG_NAMES:
        shp = weights[n].shape
        g2 = big_grads[n]
        d, nm, nv = _adamw(weights[n][0], g2, moms[n][0], vels[n][0], name="adamw_" + n)
        grads[n], delta[n], new_m[n], new_v[n] = (t.reshape(shp) for t in (g2, d, nm, nv))
    pick = lambda src: [src[n] for n in SMALL_NAMES]
    small_out = _adamw_many(pick(weights), pick(small_grads), pick(moms), pick(vels), name="adamw_small")
    ns = len(SMALL_NAMES)
    for tgt, part in ((delta, small_out[:ns]), (new_m, small_out[ns:2 * ns]), (new_v, small_out[2 * ns:])):
        tgt.update(zip(SMALL_NAMES, part))
    grads.update(small_grads)

    return (loss, grad_x, *[grads[n] for n in WEIGHT_ORDER], *[delta[n] for n in WEIGHT_ORDER],
            *[new_m[n] for n in WEIGHT_ORDER], *[new_v[n] for n in WEIGHT_ORDER])
```

```python
import functools
import math

import numpy as np
import jax
import jax.numpy as jnp
from jax import lax
from jax.experimental import pallas as pl
from jax.experimental.pallas import tpu as pltpu

F32 = jnp.float32
BF16 = jnp.bfloat16
MESH = pl.DeviceIdType.MESH

HEAD_DIM = 64
LANES = 128
ATTN_BLOCK = 128
DILATED_CONFIGS = ((128, 1), (512, 4), (2048, 16))
CONV_KERNEL = 31
FFN_CONV_KERNEL = 3
REL_BUCKETS = 32
REL_MAX_DIST = 2048
DEPTH = 1
ALPHA = (2 * DEPTH) ** 0.25
LN_EPS = 1e-5
NEG_INF = -1e30
QK_SCALE = 1.0 / math.sqrt(HEAD_DIM)
ADAM_LR = 0.001
ADAM_B1 = 0.9
ADAM_B2 = 0.999
ADAM_EPS = 1e-08
ADAM_WD = 0.01
ADAM_STEP = 10
VMEM_LIMIT = 52 * 1024 * 1024
FFN_COLS = 128
N_SHARDS = 4
N_DEV = 8


def _params(n_axes):
    return pltpu.CompilerParams(dimension_semantics=("arbitrary",) * n_axes,
                                vmem_limit_bytes=VMEM_LIMIT)


MM_DIMS = {"nn": (((1,), (0,)), ((), ())), "nt": (((1,), (1,)), ((), ())), "tn": (((0,), (0,)), ((), ()))}


class _Background:
    def __init__(self, in_arrays, out_shapes, aliases, n_sems, run, n_local=1):
        self.in_arrays, self.out_shapes, self.aliases = list(in_arrays), list(out_shapes), dict(aliases)
        self.n_sems, self.n_local, self.run = n_sems, n_local, run

    def scratch(self):
        return [pltpu.SemaphoreType.DMA((self.n_sems,)), pltpu.SemaphoreType.DMA((self.n_sems,)),
                pltpu.SemaphoreType.DMA((self.n_local,))]


def _hosted_call(body, bg, *, grid, in_specs, out_specs, out_shape, scratch_shapes, operands, name):
    n_in, n_out, n_scr = len(in_specs), len(out_specs), len(scratch_shapes)
    if bg is None:
        return pl.pallas_call(lambda *refs: body(refs, lambda post: None), grid=grid, in_specs=in_specs,
                              out_specs=out_specs, out_shape=out_shape, scratch_shapes=scratch_shapes,
                              compiler_params=_params(len(grid)), name=name)(*operands)
    nb_in, nb_out = len(bg.in_arrays), len(bg.out_shapes)
    n_steps = int(np.prod(grid))

    def full_body(*refs):
        own = refs[:n_in] + refs[n_in + nb_in:n_in + nb_in + n_out] \
            + refs[n_in + nb_in + n_out + nb_out:n_in + nb_in + n_out + nb_out + n_scr]
        bg_in = refs[n_in:n_in + nb_in]
        bg_out = refs[n_in + nb_in + n_out:n_in + nb_in + n_out + nb_out]
        sems = refs[n_in + nb_in + n_out + nb_out + n_scr:]
        step = pl.program_id(0)
        for ax in range(1, len(grid)):
            step = step * grid[ax] + pl.program_id(ax)

        def hook(post):
            bg.run(step, n_steps, bg_in, bg_out, *sems, post)

        body(own, hook)

    res = pl.pallas_call(
        full_body, grid=grid, in_specs=list(in_specs) + [HBM_SPEC] * nb_in,
        out_specs=list(out_specs) + [HBM_SPEC] * nb_out, out_shape=list(out_shape) + bg.out_shapes,
        input_output_aliases={n_in + a: n_out + o for a, o in bg.aliases.items()},
        scratch_shapes=list(scratch_shapes) + bg.scratch(), compiler_params=_params(len(grid)), name=name,
    )(*operands, *bg.in_arrays)
    return res


def _matmul_general(ins, part_fn, *, grid, tm, tn, outs, epilogue, extras=(), name, bg=None):
    nk = grid[2]
    n_in, n_extra = len(ins), len(extras)

    def body(refs, bg_hook):
        in_refs = refs[:n_in]
        rest = refs[n_in:]
        extra_refs = rest[:n_extra]
        out_refs = rest[n_extra:n_extra + len(outs)]
        acc_ref = rest[-1]
        i, j, k = pl.program_id(0), pl.program_id(1), pl.program_id(2)
        bg_hook(False)
        part = part_fn(in_refs, i, j, k)
        if nk == 1:
            epilogue(part, i, j, extra_refs, out_refs)
        else:
            @pl.when(k == 0)
            def _():
                acc_ref[...] = part

            @pl.when(k > 0)
            def _():
                acc_ref[...] += part

            @pl.when(k == nk - 1)
            def _():
                epilogue(acc_ref[...], i, j, extra_refs, out_refs)
        bg_hook(True)

    in_specs = [pl.BlockSpec(bs, im) for (_, bs, im) in list(ins) + list(extras)]
    out_specs = [pl.BlockSpec(bs, im) for (_, _, bs, im) in outs]
    out_shape = [jax.ShapeDtypeStruct(s, d) for (s, d, _, _) in outs]
    return _hosted_call(body, bg, grid=grid, in_specs=in_specs, out_specs=out_specs, out_shape=out_shape,
                        scratch_shapes=[pltpu.VMEM((tm, tn), F32)],
                        operands=[e[0] for e in ins] + [e[0] for e in extras], name=name)


def _dot(a, b, mode):
    return lax.dot_general(a.astype(BF16), b.astype(BF16), MM_DIMS[mode], preferred_element_type=F32)


def _matmul(a, b, *, mode, tm, tn, tk, outs, epilogue, extras=(), name, bg=None):
    if mode == "tn":
        K, M = a.shape
        N = b.shape[1]
        ins = [(a, (tk, tm), lambda i, j, k: (k, i)), (b, (tk, tn), lambda i, j, k: (k, j))]
    elif mode == "nt":
        M, K = a.shape
        N = b.shape[0]
        ins = [(a, (tm, tk), lambda i, j, k: (i, k)), (b, (tn, tk), lambda i, j, k: (j, k))]
    else:
        M, K = a.shape
        N = b.shape[1]
        ins = [(a, (tm, tk), lambda i, j, k: (i, k)), (b, (tk, tn), lambda i, j, k: (k, j))]
    assert M % tm == 0 and N % tn == 0 and K % tk == 0, (name, M, N, K, tm, tn, tk)

    def part_fn(in_refs, i, j, k):
        return _dot(in_refs[0][...], in_refs[1][...], mode)

    return _matmul_general(ins, part_fn, grid=(M // tm, N // tn, K // tk), tm=tm, tn=tn, outs=outs,
                           epilogue=epilogue, extras=extras, name=name, bg=bg)


def _plain_out(M, N, tm, tn, dtype):
    return ((M, N), dtype, (tm, tn), lambda i, j, k: (i, j))


def _mm_plain(a, b, *, mode, tm, tn, tk, out_dtype, name, bias=None, bg=None):
    if mode == "tn":
        M, N = a.shape[1], b.shape[1]
    elif mode == "nt":
        M, N = a.shape[0], b.shape[0]
    else:
        M, N = a.shape[0], b.shape[1]
    extras = []
    if bias is not None:
        extras.append((bias, (1, tn), lambda i, j, k: (0, j)))

    def epilogue(acc, i, j, extra_refs, out_refs):
        if bias is not None:
            acc = acc + extra_refs[0][...]
        out_refs[0][...] = acc.astype(out_dtype)

    res = _matmul(a, b, mode=mode, tm=tm, tn=tn, tk=tk, outs=[_plain_out(M, N, tm, tn, out_dtype)],
                  epilogue=epilogue, extras=extras, name=name, bg=bg)
    return res[0] if bg is None else res


def _row_tile(T, want):
    t = min(T, want)
    while T % t:
        t //= 2
    return t


def _col_tile(N, want):
    if N <= want:
        return N
    best = None
    for c in range(LANES, want + 1, LANES):
        if N % c == 0:
            best = c
    return best if best is not None else N


def _accumulate(ref, first, val):
    @pl.when(first)
    def _():
        ref[...] = val

    @pl.when(jnp.logical_not(first))
    def _():
        ref[...] += val


def _ln_fwd(z, g, b):
    mu = jnp.mean(z, axis=-1, keepdims=True)
    zc = z - mu
    var = jnp.mean(zc * zc, axis=-1, keepdims=True)
    r = lax.rsqrt(var + LN_EPS)
    xh = zc * r
    return xh * g + b, xh, r


def _ln_bwd(dy, xh, r, g):
    dxh = dy * g
    m1 = jnp.mean(dxh, axis=-1, keepdims=True)
    m2 = jnp.mean(dxh * xh, axis=-1, keepdims=True)
    return r * (dxh - m1 - xh * m2)


def _sigmoid(x):
    return 1.0 / (1.0 + jnp.exp(-x))


def _shift_down(x, s, row):
    if s == 0:
        return x
    rolled = pltpu.roll(x, s, 0)
    nfix = -(-s // 8) * 8
    head = jnp.where(row[:nfix] >= s, rolled[:nfix], 0.0)
    return jnp.concatenate([head, rolled[nfix:]], axis=0)


def _shift_up(x, s, row):
    if s == 0:
        return x
    n = x.shape[0]
    rolled = pltpu.roll(x, n - s, 0)
    nfix = -(-s // 8) * 8
    tail = jnp.where(row[n - nfix:] < n - s, rolled[n - nfix:], 0.0)
    return jnp.concatenate([rolled[:n - nfix], tail], axis=0)


def _bucket_tables():
    exact = REL_BUCKETS // 2
    qi = np.arange(ATTN_BLOCK)[:, None]
    kj = np.arange(2 * ATTN_BLOCK)[None, :]
    steps = qi + ATTN_BLOCK - kj
    buckets, masks = [], []
    for window, dilation in DILATED_CONFIGS:
        max_steps = window // dilation
        band = (steps >= 0) & (steps <= max_steps)
        dist = np.maximum(steps, 0) * dilation
        d_f = np.maximum(dist, 1).astype(np.float32)
        large = exact + (np.log(d_f / np.float32(exact)) / np.float32(math.log(REL_MAX_DIST / exact))
                         * np.float32(REL_BUCKETS - exact)).astype(np.int32)
        large = np.minimum(large, REL_BUCKETS - 1)
        bucket = np.where(dist < exact, dist, large).astype(np.int32)
        buckets.append(bucket.reshape(1, -1))
        masks.append(np.where(band, 0.0, NEG_INF).astype(np.float32).reshape(1, -1))
    return np.stack(buckets), np.stack(masks)


def _split_hi_lo(x):
    hi = x.astype(BF16)
    lo = (x - hi.astype(F32)).astype(BF16)
    return hi, lo


def _bias_build(rel_table_t, bucket, mask):
    H = rel_table_t.shape[0]
    n = bucket.shape[-1]

    def body(t_ref, bkt_ref, mask_ref, o_ref):
        onehot = (lax.broadcasted_iota(jnp.int32, (REL_BUCKETS, n), 0) == bkt_ref[0]).astype(BF16)
        t = t_ref[...]
        t1 = t.astype(BF16)
        r1 = t - t1.astype(F32)
        t2 = r1.astype(BF16)
        t3 = (r1 - t2.astype(F32)).astype(BF16)
        acc = jnp.dot(t1, onehot, preferred_element_type=F32)
        acc = acc + jnp.dot(t2, onehot, preferred_element_type=F32)
        acc = acc + jnp.dot(t3, onehot, preferred_element_type=F32)
        o_ref[0] = acc + mask_ref[0]

    return pl.pallas_call(
        body, grid=(3,),
        in_specs=[pl.BlockSpec((H, REL_BUCKETS), lambda b: (0, 0)),
                  pl.BlockSpec((1, 1, n), lambda b: (b, 0, 0)),
                  pl.BlockSpec((1, 1, n), lambda b: (b, 0, 0))],
        out_specs=pl.BlockSpec((1, H, n), lambda b: (b, 0, 0)),
        out_shape=jax.ShapeDtypeStruct((3, H, n), F32),
        compiler_params=_params(1), name="bias_build",
    )(rel_table_t, bucket, mask)


def _rel_grad(dbias, bucket):
    H = dbias.shape[1]
    n = bucket.shape[-1]
    dims = (((1,), (1,)), ((), ()))

    def body(d_ref, bkt_ref, o_ref):
        b = pl.program_id(0)
        onehot = (lax.broadcasted_iota(jnp.int32, (REL_BUCKETS, n), 0) == bkt_ref[0]).astype(BF16)
        d = d_ref[0]
        d1 = d.astype(BF16)
        r1 = d - d1.astype(F32)
        d2 = r1.astype(BF16)
        d3 = (r1 - d2.astype(F32)).astype(BF16)
        acc = lax.dot_general(d1, onehot, dims, preferred_element_type=F32)
        acc = acc + lax.dot_general(d2, onehot, dims, preferred_element_type=F32)
        acc = acc + lax.dot_general(d3, onehot, dims, preferred_element_type=F32)
        _accumulate(o_ref, b == 0, acc)

    return pl.pallas_call(
        body, grid=(3,),
        in_specs=[pl.BlockSpec((1, H, n), lambda b: (b, 0, 0)),
                  pl.BlockSpec((1, 1, n), lambda b: (b, 0, 0))],
        out_specs=pl.BlockSpec((H, REL_BUCKETS), lambda b: (0, 0)),
        out_shape=jax.ShapeDtypeStruct((H, REL_BUCKETS), F32),
        compiler_params=_params(1), name="rel_grad",
    )(dbias, bucket)


def _attn_specs(B, S, AW, d):
    L = S // d
    HP = AW // LANES
    W3 = 3 * HP
    q_spec = pl.BlockSpec((1, L, LANES), lambda h, b, r: (b, 0, r * W3 + h))
    k_spec = pl.BlockSpec((1, L, LANES), lambda h, b, r: (b, 0, r * W3 + HP + h))
    v_spec = pl.BlockSpec((1, L, LANES), lambda h, b, r: (b, 0, r * W3 + 2 * HP + h))
    o_spec = pl.BlockSpec((1, L, LANES), lambda h, b, r: (b, 0, r * HP + h))
    bias_spec = pl.BlockSpec((2, ATTN_BLOCK, 2 * ATTN_BLOCK), lambda h, b, r: (h, 0, 0))
    return L, HP, q_spec, k_spec, v_spec, o_spec, bias_spec


def _attn_fwd(qkv, bias, B, S, AW, d, name):
    L, HP, q_spec, k_spec, v_spec, o_spec, bias_spec = _attn_specs(B, S, AW, d)
    nb = L // ATTN_BLOCK
    nt = (((1,), (1,)), ((), ()))

    def body(q_ref, k_ref, v_ref, b_ref, o_ref, lse_ref):
        head0 = lax.broadcasted_iota(jnp.int32, (1, LANES), 1) < HEAD_DIM

        def block(n, first):
            qs = pl.multiple_of(n * ATTN_BLOCK, ATTN_BLOCK)
            q = q_ref[0, pl.ds(qs, ATTN_BLOCK), :]
            if first:
                kk = k_ref[0, pl.ds(0, ATTN_BLOCK), :]
                vv = v_ref[0, pl.ds(0, ATTN_BLOCK), :]
            else:
                ks = pl.multiple_of(n * ATTN_BLOCK - ATTN_BLOCK, ATTN_BLOCK)
                kk = k_ref[0, pl.ds(ks, 2 * ATTN_BLOCK), :]
                vv = v_ref[0, pl.ds(ks, 2 * ATTN_BLOCK), :]
            outs, lses = [], []
            for e in range(2):
                msk = head0 if e == 0 else jnp.logical_not(head0)
                qe = jnp.where(msk, q, jnp.zeros_like(q))
                s = lax.dot_general(qe, kk, nt, preferred_element_type=F32) * QK_SCALE
                s = s + (b_ref[e, :, ATTN_BLOCK:] if first else b_ref[e])
                m = jnp.max(s, axis=-1, keepdims=True)
                p = jnp.exp(s - m)
                l = jnp.sum(p, axis=-1, keepdims=True)
                o = jnp.dot(p.astype(BF16), vv, preferred_element_type=F32)
                outs.append(o / l)
                lses.append(jnp.broadcast_to(m + jnp.log(l), (ATTN_BLOCK, LANES)))
            o_ref[0, pl.ds(qs, ATTN_BLOCK), :] = jnp.where(head0, outs[0], outs[1])
            lse_ref[0, pl.ds(qs, ATTN_BLOCK), :] = jnp.where(head0, lses[0], lses[1])

        block(0, True)
        if nb > 1:
            def loop(n, c):
                block(n, False)
                return c
            lax.fori_loop(1, nb, loop, 0)

    qv = qkv.reshape(B, L, d * 3 * AW)
    o, lse = pl.pallas_call(
        body, grid=(HP, B, d), in_specs=[q_spec, k_spec, v_spec, bias_spec],
        out_specs=[o_spec, o_spec],
        out_shape=[jax.ShapeDtypeStruct((B, L, d * AW), F32)] * 2,
        compiler_params=_params(3), name=name,
    )(qv, qv, qv, bias)
    return o.reshape(B * S, AW), lse.reshape(B * S, AW)


def _attn_bwd(qkv, do, lse, dd, bias, B, S, AW, d, name):
    L, HP, q_spec, k_spec, v_spec, o_spec, bias_spec = _attn_specs(B, S, AW, d)
    nb = L // ATTN_BLOCK
    nt = (((1,), (1,)), ((), ()))
    tn = (((0,), (0,)), ((), ()))

    def body(q_ref, k_ref, v_ref, do_ref, lse_ref, dd_ref, b_ref, dq_ref, dk_ref, dv_ref, db_ref):
        head0 = lax.broadcasted_iota(jnp.int32, (1, LANES), 1) < HEAD_DIM
        first_step = jnp.logical_and(pl.program_id(1) == 0, pl.program_id(2) == 0)

        @pl.when(first_step)
        def _():
            db_ref[...] = jnp.zeros_like(db_ref)

        dk_ref[...] = jnp.zeros_like(dk_ref)
        dv_ref[...] = jnp.zeros_like(dv_ref)

        def block(n, first):
            qs = pl.multiple_of(n * ATTN_BLOCK, ATTN_BLOCK)
            nkeys = ATTN_BLOCK if first else 2 * ATTN_BLOCK
            ks = 0 if first else pl.multiple_of(n * ATTN_BLOCK - ATTN_BLOCK, ATTN_BLOCK)
            q = q_ref[0, pl.ds(qs, ATTN_BLOCK), :]
            kk = k_ref[0, pl.ds(ks, nkeys), :]
            vv = v_ref[0, pl.ds(ks, nkeys), :]
            dout = do_ref[0, pl.ds(qs, ATTN_BLOCK), :]
            lse_b = lse_ref[0, pl.ds(qs, ATTN_BLOCK), :]
            dd_b = dd_ref[0, pl.ds(qs, ATTN_BLOCK), :]
            dq = jnp.zeros((ATTN_BLOCK, LANES), F32)
            dkk = jnp.zeros((nkeys, LANES), F32)
            dvv = jnp.zeros((nkeys, LANES), F32)
            for e in range(2):
                msk = head0 if e == 0 else jnp.logical_not(head0)
                c0 = e * HEAD_DIM
                qe = jnp.where(msk, q, jnp.zeros_like(q))
                doe = jnp.where(msk, dout, jnp.zeros_like(dout))
                kke = jnp.where(msk, kk, jnp.zeros_like(kk))
                s = lax.dot_general(qe, kk, nt, preferred_element_type=F32) * QK_SCALE
                s = s + (b_ref[e, :, ATTN_BLOCK:] if first else b_ref[e])
                p = jnp.exp(s - lse_b[:, c0:c0 + 1])
                dp = lax.dot_general(doe, vv, nt, preferred_element_type=F32)
                ds = p * (dp - dd_b[:, c0:c0 + 1])
                if first:
                    db_ref[e, :, ATTN_BLOCK:] += ds
                else:
                    db_ref[e] += ds
                dsb = (ds * QK_SCALE).astype(BF16)
                dq = dq + jnp.dot(dsb, kke, preferred_element_type=F32)
                dkk = dkk + lax.dot_general(dsb, qe, tn, preferred_element_type=F32)
                dvv = dvv + lax.dot_general(p.astype(BF16), doe, tn, preferred_element_type=F32)
            dq_ref[0, pl.ds(qs, ATTN_BLOCK), :] = dq
            dk_ref[0, pl.ds(ks, nkeys), :] += dkk
            dv_ref[0, pl.ds(ks, nkeys), :] += dvv

        block(0, True)
        if nb > 1:
            def loop(n, c):
                block(n, False)
                return c
            lax.fori_loop(1, nb, loop, 0)

    H = AW // HEAD_DIM
    qv = qkv.reshape(B, L, d * 3 * AW)
    view = lambda t: t.reshape(B, L, d * AW)
    dq, dk, dv, db = pl.pallas_call(
        body, grid=(HP, B, d),
        in_specs=[q_spec, k_spec, v_spec, o_spec, o_spec, o_spec, bias_spec],
        out_specs=[o_spec, o_spec, o_spec, bias_spec],
        out_shape=[jax.ShapeDtypeStruct((B, L, d * AW), F32)] * 3
        + [jax.ShapeDtypeStruct((H, ATTN_BLOCK, 2 * ATTN_BLOCK), F32)],
        compiler_params=_params(3), name=name,
    )(qv, qv, qv, view(do), view(lse), view(dd), bias)
    flat = lambda t: t.reshape(B * S, AW)
    return flat(dq), flat(dk), flat(dv), db


def _attn_combine(ons, lses, gain, tm):
    T, AW = ons[0].shape

    def body(o1, o2, o3, l1, l2, l3, g_ref, attn_ref, lse_ref, mix_ref, r_ref):
        la, lb, lc = l1[...], l2[...], l3[...]
        m = jnp.maximum(jnp.maximum(la, lb), lc)
        ea, eb, ec = jnp.exp(la - m), jnp.exp(lb - m), jnp.exp(lc - m)
        den = ea + eb + ec
        attn = (ea * o1[...] + eb * o2[...] + ec * o3[...]) / den
        attn_ref[...] = attn
        lse_ref[...] = m + jnp.log(den)
        r = lax.rsqrt(jnp.mean(attn * attn, axis=-1, keepdims=True) + LN_EPS)
        mix_ref[...] = (attn * r * g_ref[...]).astype(BF16)
        r_ref[...] = jnp.broadcast_to(r, (tm, LANES))

    row = pl.BlockSpec((tm, AW), lambda i: (i, 0))
    return pl.pallas_call(
        body, grid=(T // tm,),
        in_specs=[row] * 6 + [pl.BlockSpec((1, AW), lambda i: (0, 0))],
        out_specs=[row, row, row, pl.BlockSpec((tm, LANES), lambda i: (i, 0))],
        out_shape=[jax.ShapeDtypeStruct((T, AW), F32), jax.ShapeDtypeStruct((T, AW), F32),
                   jax.ShapeDtypeStruct((T, AW), BF16), jax.ShapeDtypeStruct((T, LANES), F32)],
        compiler_params=_params(1), name="attn_combine",
    )(*ons, *lses, gain)


def _to_sub(src_ref, stage_ref, dsts, S):
    stage_ref[...] = src_ref[0].astype(F32)
    for (_, d), dst in zip(DILATED_CONFIGS[1:], dsts):
        L = S // d
        for r in range(d):
            dst[r * L:(r + 1) * L, :] = stage_ref[pl.ds(r, L, stride=d), :].astype(dst.dtype)


def _branch_blocks(S, d, block):
    nb = S // d // ATTN_BLOCK
    inner_unroll = 3 if (nb - 1) % 3 == 0 else 1

    def per_residue(r, c):
        block(r * nb, True)
        if nb > 1:
            def inner(n, c2):
                block(r * nb + n, False)
                return c2
            lax.fori_loop(1, nb, inner, 0, unroll=inner_unroll)
        return c

    lax.fori_loop(0, d, per_residue, 0, unroll=4 if nb == 1 else 1)


def _attention_fwd(qkv, bias_all, B, S, AW):
    HP = AW // LANES
    nt = MM_DIMS["nt"]

    def body(q_ref, k_ref, v_ref, b_ref, o_ref, lse_ref, stage, q4, q16, k4, k16, v4, v16, o1, l1, o4, l4, o16, l16):
        head0 = lax.broadcasted_iota(jnp.int32, (1, LANES), 1) < HEAD_DIM
        _to_sub(q_ref, stage, (q4, q16), S)
        _to_sub(k_ref, stage, (k4, k16), S)
        _to_sub(v_ref, stage, (v4, v16), S)
        srcs = ((q_ref.at[0], k_ref.at[0], v_ref.at[0], o1, l1), (q4, k4, v4, o4, l4), (q16, k16, v16, o16, l16))
        for bi, (_, d) in enumerate(DILATED_CONFIGS):
            qs_ref, ks_ref, vs_ref, od_ref, ld_ref = srcs[bi]

            def block(g, first, bi=bi, qs_ref=qs_ref, ks_ref=ks_ref, vs_ref=vs_ref, od_ref=od_ref, ld_ref=ld_ref):
                qs = pl.multiple_of(g * ATTN_BLOCK, ATTN_BLOCK)
                nkeys = ATTN_BLOCK if first else 2 * ATTN_BLOCK
                ks = qs if first else pl.multiple_of(qs - ATTN_BLOCK, ATTN_BLOCK)
                q = qs_ref[pl.ds(qs, ATTN_BLOCK), :]
                kk = ks_ref[pl.ds(ks, nkeys), :]
                vv = vs_ref[pl.ds(ks, nkeys), :]
                outs, lses = [], []
                for e in range(2):
                    msk = head0 if e == 0 else jnp.logical_not(head0)
                    qe = jnp.where(msk, q * QK_SCALE, jnp.zeros_like(q))
                    s = lax.dot_general(qe, kk, nt, preferred_element_type=F32)
                    s = s + (b_ref[bi, e, :, ATTN_BLOCK:] if first else b_ref[bi, e])
                    m = jnp.max(s, axis=-1, keepdims=True)
                    p = jnp.exp(s - m)
                    l = jnp.sum(p, axis=-1, keepdims=True)
                    o = jnp.dot(p.astype(BF16), vv, preferred_element_type=F32)
                    outs.append(o / l)
                    lses.append(jnp.broadcast_to(m + jnp.log(l), (ATTN_BLOCK, LANES)))
                od_ref[pl.ds(qs, ATTN_BLOCK), :] = jnp.where(head0, outs[0], outs[1])
                ld_ref[pl.ds(qs, ATTN_BLOCK), :] = jnp.where(head0, lses[0], lses[1])

            _branch_blocks(S, d, block)

        def natural(sub_ref, d):
            L = S // d
            for r in range(d):
                stage[pl.ds(r, L, stride=d), :] = sub_ref[r * L:(r + 1) * L, :]
            return stage[...]

        la = l1[...]
        lb = natural(l4, 4)
        lc = natural(l16, 16)
        m = jnp.maximum(jnp.maximum(la, lb), lc)
        ea, eb, ec = jnp.exp(la - m), jnp.exp(lb - m), jnp.exp(lc - m)
        den = ea + eb + ec
        lse_ref[0] = m + jnp.log(den)
        acc = ea * o1[...]
        acc = acc + eb * natural(o4, 4)
        acc = acc + ec * natural(o16, 16)
        o_ref[0] = acc / den

    blk = lambda off: pl.BlockSpec((1, S, LANES), lambda b, h: (b, 0, off + h))
    qv = qkv.reshape(B, S, 3 * AW)
    sub_b = pltpu.VMEM((S, LANES), BF16)
    sub_f = pltpu.VMEM((S, LANES), F32)
    o, lse = pl.pallas_call(
        body, grid=(B, HP),
        in_specs=[blk(0), blk(HP), blk(2 * HP),
                  pl.BlockSpec((3, 2, ATTN_BLOCK, 2 * ATTN_BLOCK), lambda b, h: (0, h, 0, 0))],
        out_specs=[blk(0), blk(0)],
        out_shape=[jax.ShapeDtypeStruct((B, S, AW), F32)] * 2,
        scratch_shapes=[sub_f] + [sub_b] * 6 + [sub_f] * 6,
        compiler_params=_params(2), name="attention_fwd",
    )(qv, qv, qv, bias_all)
    return o.reshape(B * S, AW), lse.reshape(B * S, AW)


def _attention_bwd(qkv, do, lse, dd, bias_all, B, S, AW):
    HP = AW // LANES
    H = AW // HEAD_DIM
    nt, tn = MM_DIMS["nt"], MM_DIMS["tn"]

    def body(q_ref, k_ref, v_ref, do_ref, lse_ref, dd_ref, b_ref,
             dq_ref, dk_ref, dv_ref, csq_ref, csk_ref, csv_ref, db_ref,
             stage, q4, q16, k4, k16, v4, v16, g4, g16, l4, l16, d4, d16,
             aq1, ak1, av1, aq4, ak4, av4, aq16, ak16, av16):
        head0 = lax.broadcasted_iota(jnp.int32, (1, LANES), 1) < HEAD_DIM
        first_b = pl.program_id(1) == 0

        @pl.when(first_b)
        def _():
            db_ref[...] = jnp.zeros_like(db_ref)

        _to_sub(q_ref, stage, (q4, q16), S)
        _to_sub(k_ref, stage, (k4, k16), S)
        _to_sub(v_ref, stage, (v4, v16), S)
        _to_sub(do_ref, stage, (g4, g16), S)
        _to_sub(lse_ref, stage, (l4, l16), S)
        _to_sub(dd_ref, stage, (d4, d16), S)
        for acc in (ak1, av1, ak4, av4, ak16, av16):
            acc[...] = jnp.zeros_like(acc)
        srcs = ((q_ref.at[0], k_ref.at[0], v_ref.at[0], do_ref.at[0], lse_ref.at[0], dd_ref.at[0], aq1, ak1, av1),
                (q4, k4, v4, g4, l4, d4, aq4, ak4, av4), (q16, k16, v16, g16, l16, d16, aq16, ak16, av16))
        for bi, (_, d) in enumerate(DILATED_CONFIGS):
            def block(g, first, bi=bi, refs=srcs[bi]):
                qs_ref, ks_ref, vs_ref, gs_ref, ls_ref, ds_ref, aq, ak, av = refs
                qs = pl.multiple_of(g * ATTN_BLOCK, ATTN_BLOCK)
                nkeys = ATTN_BLOCK if first else 2 * ATTN_BLOCK
                ks = qs if first else pl.multiple_of(qs - ATTN_BLOCK, ATTN_BLOCK)
                q = qs_ref[pl.ds(qs, ATTN_BLOCK), :]
                kk = ks_ref[pl.ds(ks, nkeys), :]
                vv = vs_ref[pl.ds(ks, nkeys), :]
                dout = gs_ref[pl.ds(qs, ATTN_BLOCK), :]
                lse_b = ls_ref[pl.ds(qs, ATTN_BLOCK), :]
                dd_b = ds_ref[pl.ds(qs, ATTN_BLOCK), :]
                dq = jnp.zeros((ATTN_BLOCK, LANES), F32)
                dkk = jnp.zeros((nkeys, LANES), F32)
                dvv = jnp.zeros((nkeys, LANES), F32)
                for e in range(2):
                    msk = head0 if e == 0 else jnp.logical_not(head0)
                    c0 = e * HEAD_DIM
                    qe = jnp.where(msk, q * QK_SCALE, jnp.zeros_like(q))
                    doe = jnp.where(msk, dout, jnp.zeros_like(dout))
                    kke = jnp.where(msk, kk * QK_SCALE, jnp.zeros_like(kk))
                    s = lax.dot_general(qe, kk, nt, preferred_element_type=F32)
                    s = s + (b_ref[bi, e, :, ATTN_BLOCK:] if first else b_ref[bi, e])
                    p = jnp.exp(s - lse_b[:, c0:c0 + 1])
                    dp = lax.dot_general(doe, vv, nt, preferred_element_type=F32)
                    ds = p * (dp - dd_b[:, c0:c0 + 1])
                    if first:
                        db_ref[bi, e, :, ATTN_BLOCK:] += ds
                    else:
                        db_ref[bi, e] += ds
                    dsb = ds.astype(BF16)
                    dq = dq + jnp.dot(dsb, kke, preferred_element_type=F32)
                    dkk = dkk + lax.dot_general(dsb, qe, tn, preferred_element_type=F32)
                    dvv = dvv + lax.dot_general(p.astype(BF16), doe, tn, preferred_element_type=F32)
                aq[pl.ds(qs, ATTN_BLOCK), :] = dq
                ak[pl.ds(ks, nkeys), :] += dkk
                av[pl.ds(ks, nkeys), :] += dvv

            _branch_blocks(S, d, block)

        for a1, a4, a16, out_ref, cs_ref in ((aq1, aq4, aq16, dq_ref, csq_ref), (ak1, ak4, ak16, dk_ref, csk_ref),
                                             (av1, av4, av16, dv_ref, csv_ref)):
            stage[...] = a1[...]
            for d, sub in ((4, a4), (16, a16)):
                L = S // d
                for r in range(d):
                    stage[pl.ds(r, L, stride=d), :] += sub[r * L:(r + 1) * L, :]
            tot = stage[...]
            out_ref[0] = tot.astype(out_ref.dtype)
            _accumulate(cs_ref, first_b, jnp.sum(tot, axis=0, keepdims=True))

    blk = lambda off: pl.BlockSpec((1, S, LANES), lambda h, b: (b, 0, off + h))
    cs_spec = pl.BlockSpec((1, LANES), lambda h, b: (0, h))
    bias_spec = pl.BlockSpec((3, 2, ATTN_BLOCK, 2 * ATTN_BLOCK), lambda h, b: (0, h, 0, 0))
    qv = qkv.reshape(B, S, 3 * AW)
    view = lambda t: t.reshape(B, S, AW)
    sub_b = pltpu.VMEM((S, LANES), BF16)
    sub_f = pltpu.VMEM((S, LANES), F32)
    res = pl.pallas_call(
        body, grid=(HP, B),
        in_specs=[blk(0), blk(HP), blk(2 * HP), blk(0), blk(0), blk(0), bias_spec],
        out_specs=[blk(0), blk(0), blk(0), cs_spec, cs_spec, cs_spec, bias_spec],
        out_shape=[jax.ShapeDtypeStruct((B, S, AW), BF16)] * 3 + [jax.ShapeDtypeStruct((1, AW), F32)] * 3
        + [jax.ShapeDtypeStruct((3, H, ATTN_BLOCK, 2 * ATTN_BLOCK), F32)],
        scratch_shapes=[sub_f] + [sub_b] * 8 + [sub_f] * 4 + [sub_f] * 9,
        compiler_params=_params(2), name="attention_bwd",
    )(qv, qv, qv, view(do), view(lse), view(dd), bias_all)
    flat = lambda t: t.reshape(B * S, AW)
    return flat(res[0]), flat(res[1]), flat(res[2]), res[3], res[4], res[5], res[6]


def _regroup(src, stage, dst, d, S, off=0):
    if d == 1:
        dst[off:off + S, :] = src.astype(dst.dtype)
        return
    stage[...] = src.astype(F32)
    L = S // d
    for r in range(d):
        dst[off + r * L:off + (r + 1) * L, :] = stage[pl.ds(r, L, stride=d), :].astype(dst.dtype)


def _ungroup(sub_ref, off, nat_ref, d, S, add):
    L = S // d
    for r in range(d):
        rows = pl.ds(0, S) if d == 1 else pl.ds(r, L, stride=d)
        val = sub_ref[off + r * L:off + (r + 1) * L, :]
        if add:
            nat_ref[rows, :] += val
        else:
            nat_ref[rows, :] = val


def _branch_keys(ks, vs, S, nb, g_idx):
    blk3 = (S // ATTN_BLOCK, ATTN_BLOCK, LANES)
    kc3 = ks[ATTN_BLOCK:ATTN_BLOCK + S, :].reshape(blk3)
    vc3 = vs[ATTN_BLOCK:ATTN_BLOCK + S, :].reshape(blk3)
    if nb == 1:
        return kc3, vc3, None
    kk3 = jnp.concatenate([ks[0:S, :].reshape(blk3), kc3], axis=1)
    vv3 = jnp.concatenate([vs[0:S, :].reshape(blk3), vc3], axis=1)
    col = lax.broadcasted_iota(jnp.int32, (1, 1, 2 * ATTN_BLOCK), 2)
    dead = jnp.logical_and((g_idx & (nb - 1)) == 0, col < ATTN_BLOCK)
    return kk3, vv3, dead


def _branch_scores(qe, kk3, b_ref, bi, e, dead):
    s = jnp.einsum("gqe,gke->gqk", qe, kk3, preferred_element_type=F32)
    if dead is None:
        return s + b_ref[bi, e, :, ATTN_BLOCK:]
    return jnp.where(dead, NEG_INF, s + b_ref[bi, e])


def _attention_fwd(qkv, bias_all, B, S, AW, bg=None):
    HP = AW // LANES
    G = S // ATTN_BLOCK
    blk3 = (G, ATTN_BLOCK, LANES)

    def body(refs, bg_hook):
        q_ref, k_ref, v_ref, b_ref, o_ref, lse_ref, stage, qs, ks, vs, ot, lt, on0, on1, on2, ln0, ln1, ln2 = refs
        bg_hook(False)
        head0 = lax.broadcasted_iota(jnp.int32, (1, 1, LANES), 2) < HEAD_DIM
        g_idx = lax.broadcasted_iota(jnp.int32, (G, 1, 1), 0)
        ks[0:ATTN_BLOCK, :] = jnp.zeros((ATTN_BLOCK, LANES), BF16)
        vs[0:ATTN_BLOCK, :] = jnp.zeros((ATTN_BLOCK, LANES), BF16)
        nat_o, nat_l = (on0, on1, on2), (ln0, ln1, ln2)
        for bi, (_, d) in enumerate(DILATED_CONFIGS):
            nb = S // d // ATTN_BLOCK
            _regroup(q_ref[0], stage, qs, d, S)
            _regroup(k_ref[0], stage, ks, d, S, ATTN_BLOCK)
            _regroup(v_ref[0], stage, vs, d, S, ATTN_BLOCK)
            q3 = qs[...].reshape(blk3) * QK_SCALE
            kk3, vv3, dead = _branch_keys(ks, vs, S, nb, g_idx)
            outs, lses = [], []
            for e in range(2):
                msk = head0 if e == 0 else jnp.logical_not(head0)
                qe = jnp.where(msk, q3, jnp.zeros_like(q3))
                s = _branch_scores(qe, kk3, b_ref, bi, e, dead)
                m = jnp.max(s, axis=-1, keepdims=True)
                p = jnp.exp(s - m)
                l = jnp.sum(p, axis=-1, keepdims=True)
                o = jnp.einsum("gqk,gke->gqe", p.astype(BF16), vv3, preferred_element_type=F32)
                outs.append(o / l)
                lses.append(jnp.broadcast_to(m + jnp.log(l), blk3))
            ot[...] = jnp.where(head0, outs[0], outs[1]).reshape(S, LANES)
            lt[...] = jnp.where(head0, lses[0], lses[1]).reshape(S, LANES)
            _ungroup(ot, 0, nat_o[bi], d, S, add=False)
            _ungroup(lt, 0, nat_l[bi], d, S, add=False)

        la, lb, lc = ln0[...], ln1[...], ln2[...]
        m = jnp.maximum(jnp.maximum(la, lb), lc)
        ea, eb, ec = jnp.exp(la - m), jnp.exp(lb - m), jnp.exp(lc - m)
        den = ea + eb + ec
        lse_ref[0] = m + jnp.log(den)
        o_ref[0] = (ea * on0[...] + eb * on1[...] + ec * on2[...]) / den
        bg_hook(True)

    blk = lambda off: pl.BlockSpec((1, S, LANES), lambda b, h: (b, 0, off + h))
    qv = qkv.reshape(B, S, 3 * AW)
    sub_f = pltpu.VMEM((S, LANES), F32)
    pad_b = pltpu.VMEM((S + ATTN_BLOCK, LANES), BF16)
    res = _hosted_call(
        body, bg, grid=(B, HP),
        in_specs=[blk(0), blk(HP), blk(2 * HP),
                  pl.BlockSpec((3, 2, ATTN_BLOCK, 2 * ATTN_BLOCK), lambda b, h: (0, h, 0, 0))],
        out_specs=[blk(0), blk(0)],
        out_shape=[jax.ShapeDtypeStruct((B, S, AW), F32)] * 2,
        scratch_shapes=[sub_f, pltpu.VMEM((S, LANES), BF16), pad_b, pad_b] + [sub_f] * 8,
        operands=[qv, qv, qv, bias_all], name="attention_fwd")
    return (res[0].reshape(B * S, AW), res[1].reshape(B * S, AW)) + tuple(res[2:])


def _attention_bwd(qkv, do, lse, dd, bias_all, B, S, AW, bg=None):
    HP = AW // LANES
    H = AW // HEAD_DIM
    G = S // ATTN_BLOCK
    blk3 = (G, ATTN_BLOCK, LANES)
    PAD = ATTN_BLOCK

    def body(refs, bg_hook):
        (q_ref, k_ref, v_ref, do_ref, lse_ref, dd_ref, b_ref,
         dq_ref, dk_ref, dv_ref, csq_ref, csk_ref, csv_ref, db_ref,
         stage, qs, ks, vs, gs, ls, ds_, tq, tk, tv, accq, acck, accv) = refs
        bg_hook(False)
        head0 = lax.broadcasted_iota(jnp.int32, (1, 1, LANES), 2) < HEAD_DIM
        g_idx = lax.broadcasted_iota(jnp.int32, (G, 1, 1), 0)
        first_b = pl.program_id(1) == 0

        @pl.when(first_b)
        def _():
            db_ref[...] = jnp.zeros_like(db_ref)

        ks[0:PAD, :] = jnp.zeros((PAD, LANES), BF16)
        vs[0:PAD, :] = jnp.zeros((PAD, LANES), BF16)
        tk[0:PAD, :] = jnp.zeros((PAD, LANES), F32)
        tv[0:PAD, :] = jnp.zeros((PAD, LANES), F32)
        for bi, (_, d) in enumerate(DILATED_CONFIGS):
            nb = S // d // ATTN_BLOCK
            _regroup(q_ref[0], stage, qs, d, S)
            _regroup(k_ref[0], stage, ks, d, S, PAD)
            _regroup(v_ref[0], stage, vs, d, S, PAD)
            _regroup(do_ref[0], stage, gs, d, S)
            _regroup(lse_ref[0], stage, ls, d, S)
            _regroup(dd_ref[0], stage, ds_, d, S)
            q3 = qs[...].reshape(blk3) * QK_SCALE
            do3 = gs[...].reshape(blk3)
            lse3 = ls[...].reshape(blk3)
            dd3 = ds_[...].reshape(blk3)
            kk3, vv3, dead = _branch_keys(ks, vs, S, nb, g_idx)
            dq = jnp.zeros(blk3, F32)
            dkk = jnp.zeros(kk3.shape, F32)
            dvv = jnp.zeros(kk3.shape, F32)
            for e in range(2):
                msk = head0 if e == 0 else jnp.logical_not(head0)
                c0 = e * HEAD_DIM
                qe = jnp.where(msk, q3, jnp.zeros_like(q3))
                doe = jnp.where(msk, do3, jnp.zeros_like(do3))
                ke = jnp.where(msk, kk3 * QK_SCALE, jnp.zeros_like(kk3))
                s = _branch_scores(qe, kk3, b_ref, bi, e, dead)
                p = jnp.exp(s - lse3[:, :, c0:c0 + 1])
                dp = jnp.einsum("gqe,gke->gqk", doe, vv3, preferred_element_type=F32)
                dsc = p * (dp - dd3[:, :, c0:c0 + 1])
                if dead is None:
                    db_ref[bi, e, :, ATTN_BLOCK:] += jnp.sum(dsc, axis=0)
                else:
                    db_ref[bi, e] += jnp.sum(dsc, axis=0)
                dsb = dsc.astype(BF16)
                dq = dq + jnp.einsum("gqk,gke->gqe", dsb, ke, preferred_element_type=F32)
                dkk = dkk + jnp.einsum("gqk,gqe->gke", dsb, qe, preferred_element_type=F32)
                dvv = dvv + jnp.einsum("gqk,gqe->gke", p.astype(BF16), doe, preferred_element_type=F32)
            tq[...] = dq.reshape(S, LANES)
            if dead is None:
                tk[PAD:PAD + S, :] = dkk.reshape(S, LANES)
                tv[PAD:PAD + S, :] = dvv.reshape(S, LANES)
            else:
                tk[PAD:PAD + S, :] = dkk[:, ATTN_BLOCK:, :].reshape(S, LANES)
                tv[PAD:PAD + S, :] = dvv[:, ATTN_BLOCK:, :].reshape(S, LANES)
                tk[0:S, :] += dkk[:, :ATTN_BLOCK, :].reshape(S, LANES)
                tv[0:S, :] += dvv[:, :ATTN_BLOCK, :].reshape(S, LANES)
            _ungroup(tq, 0, accq, d, S, add=bi > 0)
            _ungroup(tk, PAD, acck, d, S, add=bi > 0)
            _ungroup(tv, PAD, accv, d, S, add=bi > 0)

        for acc, out_ref, cs_ref in ((accq, dq_ref, csq_ref), (acck, dk_ref, csk_ref), (accv, dv_ref, csv_ref)):
            tot = acc[...]
            out_ref[0] = tot.astype(out_ref.dtype)
            _accumulate(cs_ref, first_b, jnp.sum(tot, axis=0, keepdims=True))
        bg_hook(True)

    blk = lambda off: pl.BlockSpec((1, S, LANES), lambda h, b: (b, 0, off + h))
    cs_spec = pl.BlockSpec((1, LANES), lambda h, b: (0, h))
    bias_spec = pl.BlockSpec((3, 2, ATTN_BLOCK, 2 * ATTN_BLOCK), lambda h, b: (0, h, 0, 0))
    qv = qkv.reshape(B, S, 3 * AW)
    view = lambda t: t.reshape(B, S, AW)
    sub_b = pltpu.VMEM((S, LANES), BF16)
    sub_f = pltpu.VMEM((S, LANES), F32)
    pad_b = pltpu.VMEM((S + PAD, LANES), BF16)
    pad_f = pltpu.VMEM((S + PAD, LANES), F32)
    res = _hosted_call(
        body, bg, grid=(HP, B),
        in_specs=[blk(0), blk(HP), blk(2 * HP), blk(0), blk(0), blk(0), bias_spec],
        out_specs=[blk(0), blk(0), blk(0), cs_spec, cs_spec, cs_spec, bias_spec],
        out_shape=[jax.ShapeDtypeStruct((B, S, AW), BF16)] * 3 + [jax.ShapeDtypeStruct((1, AW), F32)] * 3
        + [jax.ShapeDtypeStruct((3, H, ATTN_BLOCK, 2 * ATTN_BLOCK), F32)],
        scratch_shapes=[sub_f, sub_b, pad_b, pad_b, sub_b, sub_f, sub_f, sub_f, pad_f, pad_f, sub_f, sub_f, sub_f],
        operands=[qv, qv, qv, view(do), view(lse), view(dd), bias_all], name="attention_bwd")
    flat = lambda t: t.reshape(B * S, AW)
    return (flat(res[0]), flat(res[1]), flat(res[2]), res[3], res[4], res[5], res[6]) + tuple(res[7:])


def _attn_norm(attn, gain, tm):
    T, AW = attn.shape

    def body(a_ref, g_ref, mix_ref, r_ref):
        a = a_ref[...]
        r = lax.rsqrt(jnp.mean(a * a, axis=-1, keepdims=True) + LN_EPS)
        mix_ref[...] = (a * r * g_ref[...]).astype(BF16)
        r_ref[...] = jnp.broadcast_to(r, (tm, LANES))

    row = pl.BlockSpec((tm, AW), lambda i: (i, 0))
    return pl.pallas_call(
        body, grid=(T // tm,), in_specs=[row, pl.BlockSpec((1, AW), lambda i: (0, 0))],
        out_specs=[row, pl.BlockSpec((tm, LANES), lambda i: (i, 0))],
        out_shape=[jax.ShapeDtypeStruct((T, AW), BF16), jax.ShapeDtypeStruct((T, LANES), F32)],
        compiler_params=_params(1), name="attn_norm",
    )(attn, gain)


def _attn_pre_bwd(dmixed, attn, rstd, gain, tm):
    T, AW = attn.shape
    ones_np = np.kron(np.eye(AW // HEAD_DIM, dtype=np.float32), np.ones((HEAD_DIM, HEAD_DIM), np.float32))
    ones_bd = jnp.asarray(ones_np, dtype=BF16)

    def body(dm_ref, a_ref, r_ref, g_ref, ones_ref, do_ref, dd_ref, dg_ref):
        i = pl.program_id(0)
        dm = dm_ref[...]
        a = a_ref[...]
        r = r_ref[:, 0:1]
        dxn = dm * g_ref[...]
        da = r * (dxn - a * (r * r) * jnp.mean(dxn * a, axis=-1, keepdims=True))
        do_ref[...] = da.astype(BF16)
        hi, lo = _split_hi_lo(da * a)
        dd_ref[...] = (jnp.dot(hi, ones_ref[...], preferred_element_type=F32)
                       + jnp.dot(lo, ones_ref[...], preferred_element_type=F32))
        _accumulate(dg_ref, i == 0, jnp.sum(dm * a * r, axis=0, keepdims=True))

    row = pl.BlockSpec((tm, AW), lambda i: (i, 0))
    vec = pl.BlockSpec((1, AW), lambda i: (0, 0))
    return pl.pallas_call(
        body, grid=(T // tm,),
        in_specs=[row, row, pl.BlockSpec((tm, LANES), lambda i: (i, 0)), vec,
                  pl.BlockSpec((AW, AW), lambda i: (0, 0))],
        out_specs=[row, row, vec],
        out_shape=[jax.ShapeDtypeStruct((T, AW), BF16), jax.ShapeDtypeStruct((T, AW), F32),
                   jax.ShapeDtypeStruct((1, AW), F32)],
        compiler_params=_params(1), name="attn_pre_bwd",
    )(dmixed, attn, rstd, gain, ones_bd)


class _RowShifts:
    def __init__(self, x, row, up):
        self.x, self.row, self.up, self.base = x, row, up, {0: x}

    def __call__(self, s):
        x = self.x
        n, c = x.shape
        r, whole = s % 8, s - s % 8
        if r not in self.base:
            if self.up:
                rolled = pltpu.roll(x, n - r, 0)
                tail = jnp.where(self.row[n - 8:] < n - r, rolled[n - 8:], 0.0)
                self.base[r] = jnp.concatenate([rolled[:n - 8], tail], axis=0)
            else:
                rolled = pltpu.roll(x, r, 0)
                head = jnp.where(self.row[:8] >= r, rolled[:8], 0.0)
                self.base[r] = jnp.concatenate([head, rolled[8:]], axis=0)
        y = self.base[r]
        if whole == 0:
            return y
        pad = jnp.zeros((whole, c), x.dtype)
        if self.up:
            return jnp.concatenate([y[whole:], pad], axis=0)
        return jnp.concatenate([pad, y[:n - whole]], axis=0)


def _conv_branch_fwd_math(a, g, w_ref, cb, lg, lb, row):
    sg = _sigmoid(g)
    u0 = a * sg
    u0_down = _RowShifts(u0, row, up=False)
    uc = jnp.zeros_like(u0) + cb
    for k in range(CONV_KERNEL):
        uc = uc + w_ref[k:k + 1, :] * u0_down(CONV_KERNEL - 1 - k)
    ul, xh, r = _ln_fwd(uc, lg, lb)
    su = _sigmoid(ul)
    u = ul * su
    return sg, u0_down, ul, xh, r, su, u


def _conv_fwd(ag, conv_w, conv_b, ln_g, ln_b, norm_g, B, S, CW):
    def body(a_ref, g_ref, w_ref, cb_ref, lg_ref, lb_ref, ng_ref, o_ref):
        row = lax.broadcasted_iota(jnp.int32, (S, CW), 0)
        _, _, _, _, _, _, u = _conv_branch_fwd_math(a_ref[0], g_ref[0], w_ref, cb_ref[...], lg_ref[...],
                                                    lb_ref[...], row)
        rr = lax.rsqrt(jnp.mean(u * u, axis=-1, keepdims=True) + LN_EPS)
        o_ref[0] = (u * rr * ng_ref[...]).astype(BF16)

    vec = pl.BlockSpec((1, CW), lambda b: (0, 0))
    out = pl.pallas_call(
        body, grid=(B,),
        in_specs=[pl.BlockSpec((1, S, CW), lambda b: (b, 0, 0)), pl.BlockSpec((1, S, CW), lambda b: (b, 0, 1)),
                  pl.BlockSpec((CONV_KERNEL, CW), lambda b: (0, 0)), vec, vec, vec, vec],
        out_specs=pl.BlockSpec((1, S, CW), lambda b: (b, 0, 0)),
        out_shape=jax.ShapeDtypeStruct((B, S, CW), BF16),
        compiler_params=_params(1), name="conv_fwd",
    )(ag.reshape(B, S, 2 * CW), ag.reshape(B, S, 2 * CW), conv_w, conv_b, ln_g, ln_b, norm_g)
    return out.reshape(B * S, CW)


def _conv_bwd(ag, dmc, conv_w, conv_b, ln_g, ln_b, norm_g, B, S, CW):
    def body(a_ref, g_ref, dm_ref, w_ref, cb_ref, lg_ref, lb_ref, ng_ref,
             dag_ref, dw_ref, dcb_ref, dlg_ref, dlb_ref, dng_ref):
        b = pl.program_id(0)
        row = lax.broadcasted_iota(jnp.int32, (S, CW), 0)
        a, g = a_ref[0], g_ref[0]
        sg, u0_down, ul, xh, r, su, u = _conv_branch_fwd_math(a, g, w_ref, cb_ref[...], lg_ref[...], lb_ref[...], row)
        rr = lax.rsqrt(jnp.mean(u * u, axis=-1, keepdims=True) + LN_EPS)
        dm = dm_ref[0]
        dxn = dm * ng_ref[...]
        du = rr * (dxn - u * (rr * rr) * jnp.mean(dxn * u, axis=-1, keepdims=True))
        dul = du * su * (1.0 + ul * (1.0 - su))
        duc = _ln_bwd(dul, xh, r, lg_ref[...])
        first = b == 0
        _accumulate(dng_ref, first, jnp.sum(dm * u * rr, axis=0, keepdims=True))
        _accumulate(dlg_ref, first, jnp.sum(dul * xh, axis=0, keepdims=True))
        _accumulate(dlb_ref, first, jnp.sum(dul, axis=0, keepdims=True))
        _accumulate(dcb_ref, first, jnp.sum(duc, axis=0, keepdims=True))

        @pl.when(first)
        def _():
            dw_ref[...] = jnp.zeros_like(dw_ref)

        duc_up = _RowShifts(duc, row, up=True)
        du0 = jnp.zeros_like(duc)
        for k in range(CONV_KERNEL):
            sh = CONV_KERNEL - 1 - k
            dw_ref[k:k + 1, :] += jnp.sum(duc * u0_down(sh), axis=0, keepdims=True)
            du0 = du0 + w_ref[k:k + 1, :] * duc_up(sh)
        dag_ref[0, :, :CW] = du0 * sg
        dag_ref[0, :, CW:] = du0 * a * sg * (1.0 - sg)

    vec = pl.BlockSpec((1, CW), lambda b: (0, 0))
    wspec = pl.BlockSpec((CONV_KERNEL, CW), lambda b: (0, 0))
    agv = ag.reshape(B, S, 2 * CW)
    res = pl.pallas_call(
        body, grid=(B,),
        in_specs=[pl.BlockSpec((1, S, CW), lambda b: (b, 0, 0)), pl.BlockSpec((1, S, CW), lambda b: (b, 0, 1)),
                  pl.BlockSpec((1, S, CW), lambda b: (b, 0, 0)), wspec, vec, vec, vec, vec],
        out_specs=[pl.BlockSpec((1, S, 2 * CW), lambda b: (b, 0, 0)), wspec, vec, vec, vec, vec],
        out_shape=[jax.ShapeDtypeStruct((B, S, 2 * CW), F32), jax.ShapeDtypeStruct((CONV_KERNEL, CW), F32)]
        + [jax.ShapeDtypeStruct((1, CW), F32)] * 4,
        compiler_params=_params(1), name="conv_bwd",
    )(agv, agv, dmc.reshape(B, S, CW), conv_w, conv_b, ln_g, ln_b, norm_g)
    return (res[0].reshape(B * S, 2 * CW),) + tuple(res[1:])


def _ffn_conv(x, w_ref, bias, row):
    down = x if isinstance(x, _RowShifts) else _RowShifts(x, row, up=False)
    y = jnp.zeros_like(down.x) + bias
    for k in range(FFN_CONV_KERNEL):
        y = y + w_ref[k:k + 1, :] * down(FFN_CONV_KERNEL - 1 - k)
    return y


def _ffn_specs(S, tc, nj, order):
    pick = (lambda b, j: (b, j)) if order == "bj" else (lambda j, b: (b, j))
    act = lambda off: pl.BlockSpec((1, S, tc), lambda *g: (pick(*g)[0], 0, off + pick(*g)[1]))
    cw = lambda off: pl.BlockSpec((FFN_CONV_KERNEL, tc), lambda *g: (0, off + pick(*g)[1]))
    cb = lambda off: pl.BlockSpec((1, tc), lambda *g: (0, off + pick(*g)[1]))
    return act, cw, cb


def _ffn_act(upre, cw, cb, B, S, DFF):
    tc = FFN_COLS
    nj = DFF // tc

    def body(ug_ref, uv_ref, wg_ref, wv_ref, bg_ref, bv_ref, o_ref):
        row = lax.broadcasted_iota(jnp.int32, (S, tc), 0)
        gate = _ffn_conv(ug_ref[0], wg_ref, bg_ref[...], row)
        val = _ffn_conv(uv_ref[0], wv_ref, bv_ref[...], row)
        o_ref[0] = (gate * _sigmoid(gate) * val).astype(BF16)

    act, cws, cbs = _ffn_specs(S, tc, nj, "bj")
    uv = upre.reshape(B, S, 2 * DFF)
    out = pl.pallas_call(
        body, grid=(B, nj), in_specs=[act(0), act(nj), cws(0), cws(nj), cbs(0), cbs(nj)], out_specs=act(0),
        out_shape=jax.ShapeDtypeStruct((B, S, DFF), BF16), compiler_params=_params(2), name="ffn_act",
    )(uv, uv, cw, cw, cb, cb)
    return out.reshape(B * S, DFF)


def _ffn_bwd(upre, dact, cw, cb, B, S, DFF):
    tc = FFN_COLS
    nj = DFF // tc

    def body(ug_ref, uv_ref, da_ref, wg_ref, wv_ref, bg_ref, bv_ref, dug_ref, duv_ref, dwg_ref, dwv_ref,
             dbg_ref, dbv_ref):
        first = pl.program_id(1) == 0
        row = lax.broadcasted_iota(jnp.int32, (S, tc), 0)
        ug, uv = ug_ref[0], uv_ref[0]
        gate = _ffn_conv(ug, wg_ref, bg_ref[...], row)
        val = _ffn_conv(uv, wv_ref, bv_ref[...], row)
        sg = _sigmoid(gate)
        dact_b = da_ref[0]
        dgate = dact_b * val * sg * (1.0 + gate * (1.0 - sg))
        dval = dact_b * gate * sg
        for dup, u, w_ref, du_ref, dw_ref, db_ref in ((dgate, ug, wg_ref, dug_ref, dwg_ref, dbg_ref),
                                                      (dval, uv, wv_ref, duv_ref, dwv_ref, dbv_ref)):
            _accumulate(db_ref, first, jnp.sum(dup, axis=0, keepdims=True))

            @pl.when(first)
            def _(dw_ref=dw_ref):
                dw_ref[...] = jnp.zeros_like(dw_ref)

            dupre = jnp.zeros_like(dup)
            for k in range(FFN_CONV_KERNEL):
                sh = FFN_CONV_KERNEL - 1 - k
                dw_ref[k:k + 1, :] += jnp.sum(dup * _shift_down(u, sh, row), axis=0, keepdims=True)
                dupre = dupre + w_ref[k:k + 1, :] * _shift_up(dup, sh, row)
            du_ref[0] = dupre.astype(BF16)

    act, cws, cbs = _ffn_specs(S, tc, nj, "jb")
    uv = upre.reshape(B, S, 2 * DFF)
    res = pl.pallas_call(
        body, grid=(nj, B),
        in_specs=[act(0), act(nj), act(0), cws(0), cws(nj), cbs(0), cbs(nj)],
        out_specs=[act(0), act(0), cws(0), cws(0), cbs(0), cbs(0)],
        out_shape=[jax.ShapeDtypeStruct((B, S, DFF), BF16)] * 2
        + [jax.ShapeDtypeStruct((FFN_CONV_KERNEL, DFF), F32)] * 2 + [jax.ShapeDtypeStruct((1, DFF), F32)] * 2,
        compiler_params=_params(2), name="ffn_bwd",
    )(uv, uv, dact.reshape(B, S, DFF), cw, cw, cb, cb)
    flat = lambda t: t.reshape(B * S, DFF)
    return (flat(res[0]), flat(res[1]), jnp.concatenate([res[2], res[3]], axis=1),
            jnp.concatenate([res[4], res[5]], axis=1))


FFN_HALO = 16


def _half_sequences(S):
    if S < 8 * FFN_HALO:
        return [(0, S, 0, S)]
    h = S // 2
    return [(0, h + FFN_HALO, 0, h), (h - FFN_HALO, S, FFN_HALO, h)]


def _w_up_block_spec(w_up_sh, tc, off):
    _, D, cs = w_up_sh.shape
    assert cs % tc == 0
    bps = cs // tc
    return pl.BlockSpec((1, D, tc), lambda j: ((off + j) // bps, 0, (off + j) % bps))


def _ffn_fwd_fused(x1b, w_up_sh, cw, cb, B, S, DFF):
    tc = FFN_COLS
    nj = DFF // tc
    D = x1b.shape[1]

    def body(x_ref, wg_ref, wv_ref, cwg_ref, cwv_ref, cbg_ref, cbv_ref, o_ref, up_ref):
        w = jnp.concatenate([wg_ref[0], wv_ref[0]], axis=1)
        for b in range(B):
            for lo, hi, o0, on in _half_sequences(S):
                row = lax.broadcasted_iota(jnp.int32, (hi - lo, tc), 0)
                up = jnp.dot(x_ref[b, lo:hi, :], w, preferred_element_type=F32)
                up_ref[b, lo + o0:lo + o0 + on, :] = up[o0:o0 + on]
                gate = _ffn_conv(up[:, :tc], cwg_ref, cbg_ref[...], row)
                val = _ffn_conv(up[:, tc:], cwv_ref, cbv_ref[...], row)
                o_ref[b, lo + o0:lo + o0 + on, :] = (gate * _sigmoid(gate) * val).astype(BF16)[o0:o0 + on]

    cws = lambda off: pl.BlockSpec((FFN_CONV_KERNEL, tc), lambda j: (0, off + j))
    cbs = lambda off: pl.BlockSpec((1, tc), lambda j: (0, off + j))
    act, upre = pl.pallas_call(
        body, grid=(nj,),
        in_specs=[pl.BlockSpec((B, S, D), lambda j: (0, 0, 0), pipeline_mode=pl.Buffered(1)),
                  _w_up_block_spec(w_up_sh, tc, 0), _w_up_block_spec(w_up_sh, tc, nj),
                  cws(0), cws(nj), cbs(0), cbs(nj)],
        out_specs=[pl.BlockSpec((B, S, tc), lambda j: (0, 0, j)), pl.BlockSpec((B, S, 2 * tc), lambda j: (0, 0, j))],
        out_shape=[jax.ShapeDtypeStruct((B, S, DFF), BF16), jax.ShapeDtypeStruct((B, S, 2 * DFF), F32)],
        compiler_params=_params(1), name="ffn_fwd",
    )(x1b.reshape(B, S, D), w_up_sh, w_up_sh, cw, cw, cb, cb)
    return act.reshape(B * S, DFF), upre


def _ffn_bwd_fused(x1b, dz2b, upre, w_down, cw, cb, B, S, DFF):
    tc = FFN_COLS
    nj = DFF // tc
    D = x1b.shape[1]

    def body(x_ref, dz_ref, up_ref, wd_ref, cwg_ref, cwv_ref, cbg_ref, cbv_ref,
             dug_ref, duv_ref, dwu_ref, dwd_ref, dcw_ref, dcb_ref):
        first = pl.program_id(1) == 0
        dw_t = dwd = None
        dcb = [None, None]
        dcw = [[None] * FFN_CONV_KERNEL, [None] * FFN_CONV_KERNEL]
        add = lambda old, new: new if old is None else old + new
        for lo, hi, o0, on in _half_sequences(S):
            n = hi - lo
            own = slice(o0, o0 + on)
            row = lax.broadcasted_iota(jnp.int32, (n, tc), 0)
            x = x_ref[0, lo:hi, :]
            dz = dz_ref[0, lo:hi, :]
            ug = _RowShifts(up_ref[0, lo:hi, :tc], row, up=False)
            uv = _RowShifts(up_ref[0, lo:hi, tc:], row, up=False)
            gate = _ffn_conv(ug, cwg_ref, cbg_ref[...], row)
            val = _ffn_conv(uv, cwv_ref, cbv_ref[...], row)
            sg = _sigmoid(gate)
            act = (gate * sg * val).astype(BF16)
            dact = _dot(dz, wd_ref[...], "nt")
            dgate = dact * val * sg * (1.0 + gate * (1.0 - sg))
            dval = dact * gate * sg
            dupre = []
            for h, (dup, u_down, w_ref) in enumerate(((dgate, ug, cwg_ref), (dval, uv, cwv_ref))):
                dcb[h] = add(dcb[h], jnp.sum(dup[own], axis=0, keepdims=True))
                dup_up = _RowShifts(dup, row, up=True)
                acc = jnp.zeros_like(dup)
                for k in range(FFN_CONV_KERNEL):
                    sh = FFN_CONV_KERNEL - 1 - k
                    dcw[h][k] = add(dcw[h][k], jnp.sum((dup * u_down(sh))[own], axis=0, keepdims=True))
                    acc = acc + w_ref[k:k + 1, :] * dup_up(sh)
                dupre.append(acc.astype(BF16)[own])
            dug_ref[0, lo + o0:lo + o0 + on, :] = dupre[0]
            duv_ref[0, lo + o0:lo + o0 + on, :] = dupre[1]
            dw_t = add(dw_t, _dot(jnp.concatenate(dupre, axis=1), x[own], "tn"))
            dwd = add(dwd, _dot(act[own], dz[own], "tn"))
        _accumulate(dwu_ref.at[0], first, dw_t[:tc])
        _accumulate(dwu_ref.at[1], first, dw_t[tc:])
        _accumulate(dwd_ref, first, dwd)
        for h in range(2):
            _accumulate(dcb_ref.at[h], first, dcb[h])
            for k in range(FFN_CONV_KERNEL):
                _accumulate(dcw_ref.at[k, pl.ds(h, 1), :], first, dcw[h][k])

    act_s, cws, cbs = _ffn_specs(S, tc, nj, "jb")
    seq = pl.BlockSpec((1, S, D), lambda j, b: (b, 0, 0))
    res = pl.pallas_call(
        body, grid=(nj, B),
        in_specs=[seq, seq, pl.BlockSpec((1, S, 2 * tc), lambda j, b: (b, 0, j)),
                  pl.BlockSpec((tc, D), lambda j, b: (j, 0)), cws(0), cws(nj), cbs(0), cbs(nj)],
        out_specs=[act_s(0), act_s(0), pl.BlockSpec((2, tc, D), lambda j, b: (0, j, 0)),
                   pl.BlockSpec((tc, D), lambda j, b: (j, 0)),
                   pl.BlockSpec((FFN_CONV_KERNEL, 2, tc), lambda j, b: (0, 0, j)),
                   pl.BlockSpec((2, 1, tc), lambda j, b: (0, 0, j))],
        out_shape=[jax.ShapeDtypeStruct((B, S, DFF), BF16)] * 2
        + [jax.ShapeDtypeStruct((2, DFF, D), F32), jax.ShapeDtypeStruct((DFF, D), F32),
           jax.ShapeDtypeStruct((FFN_CONV_KERNEL, 2, DFF), F32), jax.ShapeDtypeStruct((2, 1, DFF), F32)],
        compiler_params=_params(2), name="ffn_bwd",
    )(x1b.reshape(B, S, D), dz2b.reshape(B, S, D), upre, w_down, cw, cw, cb, cb)
    flat = lambda t: t.reshape(B * S, DFF)
    return flat(res[0]), flat(res[1]), res[2], res[3], res[4], res[5]


def _ffn_bwd_seq(b, x1b3, dz2b3, upre, w_up_sh, w_down, cw, cb, prev, S, DFF):
    tc = FFN_COLS
    nj = DFF // tc
    B, _, D = x1b3.shape
    n_prev = 0 if prev is None else 5

    def body(*refs):
        (x_ref, dz_ref, up_ref, wg_ref, wv_ref, wd_ref, cwg_ref, cwv_ref, cbg_ref, cbv_ref) = refs[:10]
        prev_refs = refs[10:10 + n_prev]
        dx_hbm, dwu_ref, dwd_ref, dcw_ref, dcb_ref, acc_ref, sem = refs[10 + n_prev:]
        j = pl.program_id(0)

        @pl.when(j == 0)
        def _():
            acc_ref[...] = jnp.zeros_like(acc_ref)

        wcat = jnp.concatenate([wg_ref[0], wv_ref[0]], axis=1)
        dw_t = dwd = None
        dcb = [None, None]
        dcw = [[None] * FFN_CONV_KERNEL, [None] * FFN_CONV_KERNEL]
        add = lambda old, new: new if old is None else old + new
        for lo, hi, o0, on in _half_sequences(S):
            n = hi - lo
            own = slice(o0, o0 + on)
            row = lax.broadcasted_iota(jnp.int32, (n, tc), 0)
            x = x_ref[0, lo:hi, :]
            dz = dz_ref[0, lo:hi, :]
            ug, uv = up_ref[0, lo:hi, :tc], up_ref[0, lo:hi, tc:]
            gate = _ffn_conv(ug, cwg_ref, cbg_ref[...], row)
            val = _ffn_conv(uv, cwv_ref, cbv_ref[...], row)
            sg = _sigmoid(gate)
            act = (gate * sg * val).astype(BF16)
            dact = _dot(dz, wd_ref[...], "nt")
            dgate = dact * val * sg * (1.0 + gate * (1.0 - sg))
            dval = dact * gate * sg
            dupre = []
            for h, (dup, u, w_ref) in enumerate(((dgate, ug, cwg_ref), (dval, uv, cwv_ref))):
                dcb[h] = add(dcb[h], jnp.sum(dup[own], axis=0, keepdims=True))
                acc = jnp.zeros_like(dup)
                for k in range(FFN_CONV_KERNEL):
                    sh = FFN_CONV_KERNEL - 1 - k
                    dcw[h][k] = add(dcw[h][k], jnp.sum((dup * _shift_down(u, sh, row))[own], axis=0, keepdims=True))
                    acc = acc + w_ref[k:k + 1, :] * _shift_up(dup, sh, row)
                dupre.append(acc.astype(BF16)[own])
            dupre_cat = jnp.concatenate(dupre, axis=1)
            dw_t = add(dw_t, _dot(dupre_cat, x[own], "tn"))
            dwd = add(dwd, _dot(act[own], dz[own], "tn"))
            acc_ref[lo + o0:lo + o0 + on, :] += _dot(dupre_cat, wcat, "nt")
        if n_prev:
            _, pwu_ref, pwd_ref, pcw_ref, pcb_ref = prev_refs
            dwu_ref[0] = pwu_ref[0] + dw_t[:tc]
            dwu_ref[1] = pwu_ref[1] + dw_t[tc:]
            dwd_ref[...] = pwd_ref[...] + dwd
        else:
            dwu_ref[0] = dw_t[:tc]
            dwu_ref[1] = dw_t[tc:]
            dwd_ref[...] = dwd
        for h in range(2):
            dcb_ref[h] = dcb[h] + pcb_ref[h] if n_prev else dcb[h]
            for k in range(FFN_CONV_KERNEL):
                dcw_ref[k, h:h + 1, :] = dcw[h][k] + pcw_ref[k, h:h + 1, :] if n_prev else dcw[h][k]

        @pl.when(j == nj - 1)
        def _():
            out = pltpu.make_async_copy(acc_ref, dx_hbm.at[b], sem)
            out.start()
            out.wait()

    bps = w_up_sh.shape[2] // tc
    wspec = lambda off: pl.BlockSpec((1, D, tc), lambda j: ((off + j) // bps, 0, (off + j) % bps))
    cws = lambda off: pl.BlockSpec((FFN_CONV_KERNEL, tc), lambda j: (0, off + j))
    cbs = lambda off: pl.BlockSpec((1, tc), lambda j: (0, off + j))
    seq = pl.BlockSpec((1, S, D), lambda j: (b, 0, 0), pipeline_mode=pl.Buffered(1))
    part_specs = [pl.BlockSpec((2, tc, D), lambda j: (0, j, 0)), pl.BlockSpec((tc, D), lambda j: (j, 0)),
                  pl.BlockSpec((FFN_CONV_KERNEL, 2, tc), lambda j: (0, 0, j)), pl.BlockSpec((2, 1, tc), lambda j: (0, 0, j))]
    part_shapes = [jax.ShapeDtypeStruct((2, DFF, D), F32), jax.ShapeDtypeStruct((DFF, D), F32),
                   jax.ShapeDtypeStruct((FFN_CONV_KERNEL, 2, DFF), F32), jax.ShapeDtypeStruct((2, 1, DFF), F32)]
    in_specs = [seq, seq, pl.BlockSpec((1, S, 2 * tc), lambda j: (b, 0, j)), wspec(0), wspec(nj),
                pl.BlockSpec((tc, D), lambda j: (j, 0)), cws(0), cws(nj), cbs(0), cbs(nj)]
    operands = [x1b3, dz2b3, upre, w_up_sh, w_up_sh, w_down, cw, cw, cb, cb]
    aliases = {}
    if n_prev:
        in_specs += [HBM_SPEC] + part_specs
        operands += list(prev)
        aliases = {10 + i: i for i in range(5)}
    return pl.pallas_call(
        body, grid=(nj,), in_specs=in_specs, out_specs=[HBM_SPEC] + part_specs,
        out_shape=[jax.ShapeDtypeStruct((B, S, D), F32)] + part_shapes, input_output_aliases=aliases,
        scratch_shapes=[pltpu.VMEM((S, D), F32), pltpu.SemaphoreType.DMA],
        compiler_params=_params(1), name="ffn_bwd_seq%d" % b,
    )(*operands)


def _dx1_ln1_bwd(dupre_g, dupre_v, w_up_sh, dz2, xh1, r1, ln1_g, tm, bg):
    T, D = dz2.shape
    NS, _, cs = w_up_sh.shape
    half = NS // 2
    DFF = dupre_g.shape[1]

    def body(refs, bg_hook):
        dug_ref, duv_ref, w_ref, dz2_ref, xh_ref, r_ref, g_ref, dz_ref, dzb_ref, dg_ref, db_ref = refs
        bg_hook(False)
        first = pl.program_id(0) == 0
        acc = ALPHA * dz2_ref[...]
        for k in range(NS):
            src = dug_ref if k < half else duv_ref
            c0 = (k % half) * cs
            acc = acc + _dot(src[:, c0:c0 + cs], w_ref[k], "nt")
        dx1 = acc
        xh = xh_ref[...]
        dz = _ln_bwd(dx1, xh, r_ref[:, 0:1], g_ref[...])
        dz_ref[...] = dz
        dzb_ref[...] = dz.astype(BF16)
        _accumulate(dg_ref, first, jnp.sum(dx1 * xh, axis=0, keepdims=True))
        _accumulate(db_ref, first, jnp.sum(dx1, axis=0, keepdims=True))
        bg_hook(True)

    row = pl.BlockSpec((tm, D), lambda i: (i, 0))
    vec = pl.BlockSpec((1, D), lambda i: (0, 0))
    du = pl.BlockSpec((tm, DFF), lambda i: (i, 0))
    return _hosted_call(
        body, bg, grid=(T // tm,),
        in_specs=[du, du, pl.BlockSpec((NS, D, cs), lambda i: (0, 0, 0), pipeline_mode=pl.Buffered(1)),
                  row, row, pl.BlockSpec((tm, LANES), lambda i: (i, 0)), vec],
        out_specs=[row, row, vec, vec],
        out_shape=[jax.ShapeDtypeStruct((T, D), F32), jax.ShapeDtypeStruct((T, D), BF16),
                   jax.ShapeDtypeStruct((1, D), F32), jax.ShapeDtypeStruct((1, D), F32)],
        scratch_shapes=[], operands=[dupre_g, dupre_v, w_up_sh, dz2, xh1, r1, ln1_g], name="mm_dx1_ln1_bwd")


def _transpose(x, name):
    R, C = x.shape
    tr = LANES if R % LANES == 0 else R

    def body(x_ref, o_ref):
        o_ref[...] = x_ref[...].T

    return pl.pallas_call(
        body, grid=(R // tr,), in_specs=[pl.BlockSpec((tr, C), lambda i: (i, 0))],
        out_specs=pl.BlockSpec((C, tr), lambda i: (0, i)), out_shape=jax.ShapeDtypeStruct((C, R), F32),
        compiler_params=_params(1), name=name)(x)


def _dh_cat(dq, dk, dv, dag, tm):
    T, AW = dq.shape
    CW2 = dag.shape[1]
    W = 3 * AW + CW2

    def body(dq_ref, dk_ref, dv_ref, dag_ref, dh_ref, cs_ref):
        for c, ref in enumerate((dq_ref, dk_ref, dv_ref)):
            dh_ref[:, c * AW:(c + 1) * AW] = ref[...]
        dg = dag_ref[...]
        dh_ref[:, 3 * AW:] = dg.astype(BF16)
        _accumulate(cs_ref, pl.program_id(0) == 0, jnp.sum(dg, axis=0, keepdims=True))

    row = pl.BlockSpec((tm, AW), lambda i: (i, 0))
    return pl.pallas_call(
        body, grid=(T // tm,),
        in_specs=[row] * 3 + [pl.BlockSpec((tm, CW2), lambda i: (i, 0))],
        out_specs=[pl.BlockSpec((tm, W), lambda i: (i, 0)), pl.BlockSpec((1, CW2), lambda i: (0, 0))],
        out_shape=[jax.ShapeDtypeStruct((T, W), BF16), jax.ShapeDtypeStruct((1, CW2), F32)],
        compiler_params=_params(1), name="dh_cat",
    )(dq, dk, dv, dag)


def _local_step(x, target, rel_table, w_in, b_in, conv_w, conv_b, conv_ln_g, conv_ln_b, attn_norm_g,
                conv_norm_g, staged, ln1_g, ln1_b, ffn_cw, ffn_cb, ln2_g, ln2_b, ids):
    B, S, D = x.shape
    T = B * S
    AW = attn_norm_g.shape[-1]
    CW = conv_norm_g.shape[-1]
    H = AW // HEAD_DIM
    DFF = staged[2].shape[0] * staged[2].shape[1]
    INW = 3 * AW + 2 * CW
    xf = x.reshape(T, D)
    tf = target.reshape(T, D)
    tm = _row_tile(T, 512)
    tm_s = _row_tile(T, 256)

    bucket_np, mask_np = _bucket_tables()
    bucket = jnp.asarray(bucket_np)
    band_mask = jnp.asarray(mask_np)
    bias_all = _bias_build(rel_table.T, bucket, band_mask).reshape(3, H, ATTN_BLOCK, 2 * ATTN_BLOCK)

    tn_qkv = _col_tile(3 * AW, 1152)
    qkv = _mm_plain(xf, w_in[:, :3 * AW], mode="nn", tm=tm, tn=tn_qkv, tk=D, out_dtype=BF16,
                    bias=b_in[:, :3 * AW], name="mm_qkv")
    ag = _mm_plain(xf, w_in[:, 3 * AW:], mode="nn", tm=tm, tn=2 * CW, tk=D, out_dtype=F32,
                   bias=b_in[:, 3 * AW:], name="mm_ag")

    attn, lse, w_out_g, w_up_sh, w_down_g = _attention_fwd(qkv, bias_all, B, S, AW, bg=_bg_gather(staged))
    w_out = w_out_g.reshape(D, D)
    w_down = w_down_g.reshape(DFF, D)
    mixed_c = _conv_fwd(ag, conv_w, conv_b, conv_ln_g, conv_ln_b, conv_norm_g, B, S, CW)

    def attn_rstd(a):
        return lax.rsqrt(jnp.mean(a * a, axis=-1, keepdims=True) + LN_EPS)

    def mixed_rows(attn_ref, mc_ref, gain_ref):
        a = attn_ref[...]
        return jnp.concatenate([(a * attn_rstd(a) * gain_ref[...]).astype(BF16), mc_ref[...]], axis=1)

    def ln1_epilogue(acc, i, j, extra_refs, out_refs):
        x_ref, g_ref, b_ref, a_ref = extra_refs
        x1, xh, r = _ln_fwd(acc + ALPHA * x_ref[...], g_ref[...], b_ref[...])
        out_refs[0][...] = x1
        out_refs[1][...] = x1.astype(BF16)
        out_refs[2][...] = xh
        out_refs[3][...] = jnp.broadcast_to(r, (tm_s, LANES))
        out_refs[4][...] = jnp.broadcast_to(attn_rstd(a_ref[...]), (tm_s, LANES))

    rowD = lambda i, j, k: (i, 0)
    vecD = lambda i, j, k: (0, 0)
    x1, x1b, xh1, r1, r_attn = _matmul_general(
        [(attn, (tm_s, AW), rowD), (mixed_c, (tm_s, CW), rowD), (attn_norm_g, (1, AW), vecD), (w_out, (D, D), vecD)],
        lambda refs, i, j, k: _dot(mixed_rows(refs[0], refs[1], refs[2]), refs[3][...], "nn"),
        grid=(T // tm_s, 1, 1), tm=tm_s, tn=D,
        extras=[(xf, (tm_s, D), rowD), (ln1_g, (1, D), vecD), (ln1_b, (1, D), vecD), (attn, (tm_s, AW), rowD)],
        outs=[((T, D), F32, (tm_s, D), rowD), ((T, D), BF16, (tm_s, D), rowD), ((T, D), F32, (tm_s, D), rowD),
              ((T, LANES), F32, (tm_s, LANES), rowD), ((T, LANES), F32, (tm_s, LANES), rowD)],
        epilogue=ln1_epilogue, name="mm_out_ln1")

    NS, _, cs = w_up_sh.shape
    half = NS // 2

    act, upre = _ffn_fwd_fused(x1b, w_up_sh, ffn_cw, ffn_cb, B, S, DFF)

    def ln2_epilogue(acc, i, j, extra_refs, out_refs):
        x1_ref, g_ref, b_ref, t_ref = extra_refs
        dz_ref, dzb_ref, loss_ref, dg_ref, db_ref = out_refs
        g = g_ref[...]
        y, xh, r = _ln_fwd(acc + ALPHA * x1_ref[...], g, b_ref[...])
        diff = y - t_ref[...]
        row_loss = jnp.sum(diff * diff, axis=1, keepdims=True)
        tile_loss = jnp.sum(row_loss, axis=0, keepdims=True) * (0.5 / D)
        dy = diff * (1.0 / D)
        dz = _ln_bwd(dy, xh, r, g)
        dz_ref[...] = dz
        dzb_ref[...] = dz.astype(BF16)
        first = i == 0
        _accumulate(loss_ref, first, jnp.broadcast_to(tile_loss, (1, LANES)))
        _accumulate(dg_ref, first, jnp.sum(dy * xh, axis=0, keepdims=True))
        _accumulate(db_ref, first, jnp.sum(dy, axis=0, keepdims=True))

    dz2, dz2b, loss_part, d_ln2_g, d_ln2_b = _matmul(
        act, w_down, mode="nn", tm=tm, tn=D, tk=DFF,
        extras=[(x1, (tm, D), rowD), (ln2_g, (1, D), vecD), (ln2_b, (1, D), vecD), (tf, (tm, D), rowD)],
        outs=[((T, D), F32, (tm, D), rowD), ((T, D), BF16, (tm, D), rowD),
              ((1, LANES), F32, (1, LANES), vecD), ((1, D), F32, (1, D), vecD), ((1, D), F32, (1, D), vecD)],
        epilogue=ln2_epilogue, name="mm_down_ln2_loss")

    dupre_g, dupre_v, d_w_up_t, d_w_down, d_ffn_cw2, d_ffn_cb2 = _ffn_bwd_fused(
        x1b, dz2b, upre, w_down, ffn_cw, ffn_cb, B, S, DFF)
    d_w_up_t = d_w_up_t.reshape(NS, cs, D)
    d_ffn_cw = d_ffn_cw2.reshape(FFN_CONV_KERNEL, 2 * DFF)
    d_ffn_cb = d_ffn_cb2.reshape(1, 2 * DFF)
    tk_t = _row_tile(T, 512)

    early = [d_w_up_t, d_w_down.reshape(NS, DFF // NS, D)]
    dz1, dz1b, d_ln1_g, d_ln1_b, *sib_e = _dx1_ln1_bwd(dupre_g, dupre_v, w_up_sh, dz2, xh1, r1, ln1_g, tm,
                                                       bg=_bg_sibling_exchange(early))
    chip_e = [_pair_sum(g, s, ids, name="pair_sum_" + n) for g, s, n in zip(early, sib_e, ("w_up", "w_down"))]

    def dw_out_epilogue(acc, i, j, extra_refs, out_refs):
        out_refs[0][...] = acc

    d_w_out = _matmul_general(
        [(attn, (tk_t, AW), lambda i, j, k: (k, 0)), (mixed_c, (tk_t, CW), lambda i, j, k: (k, 0)),
         (attn_norm_g, (1, AW), vecD), (dz1b, (tk_t, D), lambda i, j, k: (k, 0))],
        lambda refs, i, j, k: _dot(mixed_rows(refs[0], refs[1], refs[2]), refs[3][...], "tn"),
        grid=(1, 1, T // tk_t), tm=D, tn=D, outs=[_plain_out(D, D, D, D, F32)],
        epilogue=dw_out_epilogue, name="mm_dw_out")[0]
    early.append(d_w_out.reshape(NS, D // NS, D))
    ones_bd = jnp.asarray(np.kron(np.eye(H, dtype=np.float32), np.ones((HEAD_DIM, HEAD_DIM), np.float32)), dtype=BF16)

    def dmixed_epilogue(acc, i, j, extra_refs, out_refs):
        a_ref, r_ref, g_ref, ones_ref = extra_refs
        do_ref, dd_ref, dmc_ref, dg_ref = out_refs
        dm = acc[:, :AW]
        dmc_ref[...] = acc[:, AW:]
        a = a_ref[...]
        r = r_ref[:, 0:1]
        dxn = dm * g_ref[...]
        da = r * (dxn - a * (r * r) * jnp.mean(dxn * a, axis=-1, keepdims=True))
        do_ref[...] = da.astype(BF16)
        hi, lo = _split_hi_lo(da * a)
        dd_ref[...] = (jnp.dot(hi, ones_ref[...], preferred_element_type=F32)
                       + jnp.dot(lo, ones_ref[...], preferred_element_type=F32))
        _accumulate(dg_ref, i == 0, jnp.sum(dm * a * r, axis=0, keepdims=True))

    dattn, dd, dmc, d_attn_norm_g, sib_out = _matmul(
        dz1b, w_out, mode="nt", tm=tm, tn=D, tk=D,
        extras=[(attn, (tm, AW), rowD), (r_attn, (tm, LANES), rowD), (attn_norm_g, (1, AW), vecD),
                (ones_bd, (AW, AW), vecD)],
        outs=[((T, AW), BF16, (tm, AW), rowD), ((T, AW), F32, (tm, AW), rowD), ((T, CW), F32, (tm, CW), rowD),
              ((1, AW), F32, (1, AW), vecD)],
        epilogue=dmixed_epilogue, name="mm_dmixed", bg=_bg_sibling_exchange(early[2:]))
    sib_e.append(sib_out)
    chip_e.append(_pair_sum(early[2], sib_out, ids, name="pair_sum_w_out"))

    dag, d_conv_w, d_conv_b, d_conv_ln_g, d_conv_ln_b, d_conv_norm_g = _conv_bwd(
        ag, dmc, conv_w, conv_b, conv_ln_g, conv_ln_b, conv_norm_g, B, S, CW)

    dq, dk, dv, csq, csk, csv, dbias, *got_e = _attention_bwd(qkv, dattn, lse, dd, bias_all, B, S, AW,
                                                              bg=_bg_chip_exchange(chip_e))
    full_up, full_down, full_out = [_final_sum(g, s, r, ids, name="final_sum_" + n)
                                    for g, s, r, n in zip(early, sib_e, got_e, ("w_up", "w_down", "w_out"))]
    d_rel_table = _rel_grad(dbias.reshape(3, H, ATTN_BLOCK * 2 * ATTN_BLOCK), bucket).T
    dh, cs_ag = _dh_cat(dq, dk, dv, dag, tm_s)
    d_b_in = jnp.concatenate([csq, csk, csv, cs_ag], axis=1)

    d_w_in_t = _mm_plain(dh, xf, mode="tn", tm=_col_tile(INW, 1408), tn=D, tk=tk_t, out_dtype=F32, name="mm_dw_in")
    late = [d_w_in_t.reshape(NS, INW // NS, D)]
    sib_l = _sibling_exchange(late)
    chip_l = [_pair_sum(late[0], sib_l[0], ids, name="pair_sum_w_in")]
    small = dict(rel_table=d_rel_table, b_in=d_b_in, conv_w=d_conv_w, conv_b=d_conv_b, conv_ln_g=d_conv_ln_g,
                 conv_ln_b=d_conv_ln_b, attn_norm_g=d_attn_norm_g, conv_norm_g=d_conv_norm_g, ln1_g=d_ln1_g,
                 ln1_b=d_ln1_b, ffn_conv_w=d_ffn_cw, ffn_conv_b=d_ffn_cb, ln2_g=d_ln2_g, ln2_b=d_ln2_b)
    pack = _pack([loss_part] + [small[n] for n in SMALL_NAMES])

    def gx_epilogue(acc, i, j, extra_refs, out_refs):
        out_refs[0][...] = acc + ALPHA * extra_refs[0][...]

    grad_x, got_in, all_packs = _matmul(
        dh, w_in, mode="nt", tm=tm, tn=D, tk=INW, extras=[(dz1, (tm, D), rowD)],
        outs=[((T, D), F32, (tm, D), rowD)], epilogue=gx_epilogue, name="mm_grad_x",
        bg=_bg_chip_exchange(chip_l, pack))
    full_in = _final_sum(late[0], sib_l[0], got_in, ids, name="final_sum_w_in")
    return grad_x.reshape(B, S, D), [full_in, full_out, full_up, full_down], all_packs


def _place():
    return lax.axis_index("x"), lax.axis_index("y"), lax.axis_index("c")


CHIP_FLIPS = ((1, 0), (0, 1), (1, 1))


def _flip(v, f):
    return 1 - v if f else v


HBM_SPEC = pl.BlockSpec(memory_space=pl.ANY)
VMEM_SPEC = pl.BlockSpec(memory_space=pltpu.VMEM)
COMM_PARAMS = pltpu.CompilerParams(vmem_limit_bytes=VMEM_LIMIT)


def _gather_weights(big, small):
    nb, ns = len(big), len(small)

    def body(*refs):
        big_in = refs[:nb]
        small_in = refs[nb:nb + ns]
        big_out = refs[nb + ns:2 * nb + ns]
        small_out = refs[2 * nb + ns:2 * nb + 2 * ns]
        stages = refs[2 * nb + 2 * ns:3 * nb + 2 * ns]
        send_sems, recv_sems, local_sems = refs[3 * nb + 2 * ns:]
        x, y, c = _place()
        s_me = 2 * x + y
        sibling = (x, y, 1 - c)
        started, local_copies = [], []
        for a in range(nb):
            rh = big[a].shape[0] // 2
            lo = pl.multiple_of(c * rh, 16)
            stages[a][...] = big_in[a][pl.ds(lo, rh), :].astype(BF16)
            mine = big_out[a].at[s_me, pl.ds(lo, rh), :]
            loc = pltpu.make_async_copy(stages[a], mine, local_sems.at[a])
            loc.start()
            local_copies.append(loc)
            targets = [sibling] + [(_flip(x, fx), _flip(y, fy), c) for fx, fy in CHIP_FLIPS]
            for k, to in enumerate(targets):
                cp = pltpu.make_async_remote_copy(stages[a], mine, send_sems.at[a * 7 + k],
                                                  recv_sems.at[a * 7 + k], device_id=to, device_id_type=MESH)
                cp.start()
                started.append(cp)
        for a in range(ns):
            mine = small_out[a].at[s_me]
            loc = pltpu.make_async_copy(small_in[a], mine, local_sems.at[nb + a])
            loc.start()
            local_copies.append(loc)
            for k, (fx, fy) in enumerate(CHIP_FLIPS):
                cp = pltpu.make_async_remote_copy(small_in[a], mine, send_sems.at[nb * 7 + a * 3 + k],
                                                  recv_sems.at[nb * 7 + a * 3 + k],
                                                  device_id=(_flip(x, fx), _flip(y, fy), c), device_id_type=MESH)
                cp.start()
                started.append(cp)
        for a in range(nb):
            rh = big[a].shape[0] // 2
            lo = pl.multiple_of(c * rh, 16)
            for k, (fx, fy) in enumerate(CHIP_FLIPS):
                s_from = 2 * _flip(x, fx) + _flip(y, fy)
                got = big_out[a].at[s_from, pl.ds(lo, rh), :]
                pltpu.make_async_remote_copy(got, got, send_sems.at[a * 7 + 1 + k], recv_sems.at[a * 7 + 1 + k],
                                             device_id=sibling, device_id_type=MESH).wait_recv()
                fwd = pltpu.make_async_remote_copy(got, got, send_sems.at[a * 7 + 4 + k],
                                                   recv_sems.at[a * 7 + 4 + k], device_id=sibling,
                                                   device_id_type=MESH)
                fwd.start()
                started.append(fwd)
        for a in range(nb):
            rh = big[a].shape[0] // 2
            lo_sib = pl.multiple_of((1 - c) * rh, 16)
            for k in (0, 4, 5, 6):
                any_rows = big_out[a].at[s_me, pl.ds(lo_sib, rh), :]
                pltpu.make_async_remote_copy(any_rows, any_rows, send_sems.at[a * 7 + k], recv_sems.at[a * 7 + k],
                                             device_id=sibling, device_id_type=MESH).wait_recv()
        for a in range(ns):
            for k in range(3):
                pltpu.make_async_remote_copy(small_in[a], small_out[a].at[s_me], send_sems.at[nb * 7 + a * 3 + k],
                                             recv_sems.at[nb * 7 + a * 3 + k], device_id=sibling,
                                             device_id_type=MESH).wait_recv()
        for cp in started:
            cp.wait_send()
        for cp in local_copies:
            cp.wait()

    n_sem = nb * 7 + ns * 3
    out_shape = ([jax.ShapeDtypeStruct((N_SHARDS,) + w.shape, BF16) for w in big]
                 + [jax.ShapeDtypeStruct((N_SHARDS,) + w.shape, F32) for w in small])
    res = pl.pallas_call(
        body, in_specs=[VMEM_SPEC] * nb + [HBM_SPEC] * ns, out_specs=[HBM_SPEC] * (nb + ns),
        out_shape=out_shape,
        scratch_shapes=[pltpu.VMEM((w.shape[0] // 2, w.shape[1]), BF16) for w in big]
        + [pltpu.SemaphoreType.DMA((n_sem,)), pltpu.SemaphoreType.DMA((n_sem,)),
           pltpu.SemaphoreType.DMA((nb + ns,))],
        compiler_params=COMM_PARAMS, name="gather_weights",
    )(*big, *small)
    return res[:nb], res[nb:]


def _sibling_exchange(grads):
    n = len(grads)

    def body(*refs):
        g_in = refs[:n]
        got = refs[n:2 * n]
        send_sems, recv_sems = refs[2 * n:]
        x, y, c = _place()
        cps = []
        for a in range(n):
            rh = grads[a].shape[1] // 2
            lo = pl.multiple_of((1 - c) * rh, 8)
            cp = pltpu.make_async_remote_copy(g_in[a].at[:, pl.ds(lo, rh), :], got[a], send_sems.at[a],
                                              recv_sems.at[a], device_id=(x, y, 1 - c), device_id_type=MESH)
            cp.start()
            cps.append(cp)
        for cp in cps:
            cp.wait()

    return pl.pallas_call(
        body, in_specs=[HBM_SPEC] * n, out_specs=[HBM_SPEC] * n,
        out_shape=[jax.ShapeDtypeStruct((N_SHARDS, g.shape[1] // 2, g.shape[2]), F32) for g in grads],
        scratch_shapes=[pltpu.SemaphoreType.DMA((n,)), pltpu.SemaphoreType.DMA((n,))],
        compiler_params=COMM_PARAMS, name="sibling_exchange",
    )(*grads)


def _chip_exchange(chip_parts, pack):
    n = len(chip_parts)

    def body(*refs):
        parts = refs[:n]
        pack_ref = refs[n]
        got = refs[n + 1:2 * n + 1]
        all_packs = refs[2 * n + 1]
        send_sems, recv_sems, local_sem = refs[2 * n + 2:]
        x, y, c = _place()
        me = 4 * x + 2 * y + c
        cps = []
        for a in range(n):
            for k, (fx, fy) in enumerate(CHIP_FLIPS):
                px, py = _flip(x, fx), _flip(y, fy)
                cp = pltpu.make_async_remote_copy(parts[a].at[2 * px + py], got[a].at[k], send_sems.at[a * 3 + k],
                                                  recv_sems.at[a * 3 + k], device_id=(px, py, c),
                                                  device_id_type=MESH)
                cp.start()
                cps.append(cp)
        loc = pltpu.make_async_copy(pack_ref, all_packs.at[me], local_sem)
        loc.start()
        for m in range(1, N_DEV):
            to = (_flip(x, m & 4), _flip(y, m & 2), _flip(c, m & 1))
            cp = pltpu.make_async_remote_copy(pack_ref, all_packs.at[me], send_sems.at[n * 3 + m - 1],
                                              recv_sems.at[n * 3 + m - 1], device_id=to, device_id_type=MESH)
            cp.start()
            cps.append(cp)
        for cp in cps:
            cp.wait()
        loc.wait()

    rs = pack.shape[0]
    res = pl.pallas_call(
        body, in_specs=[HBM_SPEC] * (n + 1), out_specs=[HBM_SPEC] * (n + 1),
        out_shape=[jax.ShapeDtypeStruct((3,) + p.shape[1:], BF16) for p in chip_parts]
        + [jax.ShapeDtypeStruct((N_DEV, rs, LANES), F32)],
        scratch_shapes=[pltpu.SemaphoreType.DMA((n * 3 + N_DEV - 1,)), pltpu.SemaphoreType.DMA((n * 3 + N_DEV - 1,)),
                        pltpu.SemaphoreType.DMA],
        compiler_params=COMM_PARAMS, name="chip_exchange",
    )(*chip_parts, pack)
    return res[:n], res[n]


def _sibling_assemble(fulls):
    n = len(fulls)

    def body(*refs):
        full = refs[n:2 * n]
        send_sems, recv_sems = refs[2 * n:]
        x, y, c = _place()
        cps = []
        for a in range(n):
            rh = fulls[a].shape[0] // 2
            mine = full[a].at[pl.ds(pl.multiple_of(c * rh, 8), rh), :]
            cp = pltpu.make_async_remote_copy(mine, mine, send_sems.at[a], recv_sems.at[a],
                                              device_id=(x, y, 1 - c), device_id_type=MESH)
            cp.start()
            cps.append(cp)
        for cp in cps:
            cp.wait()

    return pl.pallas_call(
        body, in_specs=[HBM_SPEC] * n, out_specs=[HBM_SPEC] * n,
        out_shape=[jax.ShapeDtypeStruct(f.shape, F32) for f in fulls],
        input_output_aliases={a: a for a in range(n)},
        scratch_shapes=[pltpu.SemaphoreType.DMA((n,)), pltpu.SemaphoreType.DMA((n,))],
        compiler_params=COMM_PARAMS, name="sibling_assemble",
    )(*fulls)


def _remote(ref_src, ref_dst, send_sems, recv_sems, k, to):
    return pltpu.make_async_remote_copy(ref_src, ref_dst, send_sems.at[k], recv_sems.at[k], device_id=to,
                                        device_id_type=MESH)


def _stage_half(w, ids, name):
    R, C = w.shape
    rh = R // 2
    rt = _half_tile(rh)
    nt = rh // rt

    def body(ids_ref, w_ref, o_ref):
        o_ref[0] = w_ref[...].astype(BF16)

    grid_spec = pltpu.PrefetchScalarGridSpec(
        num_scalar_prefetch=1, grid=(nt,),
        in_specs=[pl.BlockSpec((rt, C), lambda i, ids: (ids[2] * nt + i, 0))],
        out_specs=pl.BlockSpec((1, rt, C), lambda i, ids: (2 * ids[0] + ids[1], ids[2] * nt + i, 0)))
    return pl.pallas_call(body, grid_spec=grid_spec, out_shape=jax.ShapeDtypeStruct((N_SHARDS, R, C), BF16),
                          compiler_params=_params(1), name=name)(ids, w)


def _bg_gather(staged):
    n = len(staged)

    def run(step, n_steps, ins, outs, send_sems, recv_sems, local_sems, post):
        x, y, c = _place()
        s_me = 2 * x + y
        sibling = (x, y, 1 - c)
        chips = [(_flip(x, fx), _flip(y, fy)) for fx, fy in CHIP_FLIPS]

        def rows(a, s, half):
            rh = staged[a].shape[1] // 2
            return outs[a].at[s, pl.ds(pl.multiple_of(half * rh, 16), rh), :]

        def copy(a, k, ref, to):
            return _remote(ref, ref, send_sems, recv_sems, a * 7 + k, to)

        if not post:
            @pl.when(step == 0)
            def _():
                for a in range(n):
                    mine = rows(a, s_me, c)
                    copy(a, 0, mine, sibling).start()
                    for k, (px, py) in enumerate(chips):
                        copy(a, 1 + k, mine, (px, py, c)).start()

            @pl.when(step == max(n_steps - 2, 0))
            def _():
                for a in range(n):
                    for k, (px, py) in enumerate(chips):
                        got = rows(a, 2 * px + py, c)
                        copy(a, 1 + k, got, sibling).wait_recv()
                        copy(a, 4 + k, got, sibling).start()
        else:
            @pl.when(step == n_steps - 1)
            def _():
                for a in range(n):
                    for k in (0, 4, 5, 6):
                        copy(a, k, rows(a, s_me, 1 - c), sibling).wait_recv()
                    for k in range(7):
                        copy(a, k, rows(a, s_me, c), sibling).wait_send()

    return _Background(staged, [jax.ShapeDtypeStruct(g.shape, g.dtype) for g in staged],
                       {a: a for a in range(n)}, 7 * n, run)


def _bg_sibling_exchange(grads):
    n = len(grads)

    def run(step, n_steps, ins, outs, send_sems, recv_sems, local_sems, post):
        x, y, c = _place()

        def copy(a):
            rh = grads[a].shape[1] // 2
            lo = pl.multiple_of((1 - c) * rh, 8)
            return _remote(ins[a].at[:, pl.ds(lo, rh), :], outs[a], send_sems, recv_sems, a, (x, y, 1 - c))

        if not post:
            @pl.when(step == 0)
            def _():
                for a in range(n):
                    copy(a).start()
        else:
            @pl.when(step == n_steps - 1)
            def _():
                for a in range(n):
                    copy(a).wait()

    return _Background(grads, [jax.ShapeDtypeStruct((N_SHARDS, g.shape[1] // 2, g.shape[2]), F32) for g in grads],
                       {}, n, run)


def _bg_chip_exchange(chip_parts, pack=None):
    n = len(chip_parts)

    def run(step, n_steps, ins, outs, send_sems, recv_sems, local_sems, post):
        x, y, c = _place()
        me = 4 * x + 2 * y + c

        def copies():
            cps = []
            for a in range(n):
                for k, (fx, fy) in enumerate(CHIP_FLIPS):
                    px, py = _flip(x, fx), _flip(y, fy)
                    cps.append(_remote(ins[a].at[2 * px + py], outs[a].at[k], send_sems, recv_sems, a * 3 + k,
                                       (px, py, c)))
            if pack is not None:
                for m in range(1, N_DEV):
                    to = (_flip(x, m & 4), _flip(y, m & 2), _flip(c, m & 1))
                    cps.append(_remote(ins[n], outs[n].at[me], send_sems, recv_sems, n * 3 + m - 1, to))
            return cps

        def local():
            return pltpu.make_async_copy(ins[n], outs[n].at[me], local_sems.at[0])

        if not post:
            @pl.when(step == 0)
            def _():
                for cp in copies():
                    cp.start()
                if pack is not None:
                    local().start()
        else:
            @pl.when(step == n_steps - 1)
            def _():
                for cp in copies():
                    cp.wait()
                if pack is not None:
                    local().wait()

    in_arrays = list(chip_parts) + ([pack] if pack is not None else [])
    out_shapes = [jax.ShapeDtypeStruct((3,) + p.shape[1:], BF16) for p in chip_parts]
    if pack is not None:
        out_shapes.append(jax.ShapeDtypeStruct((N_DEV, pack.shape[0], LANES), F32))
    return _Background(in_arrays, out_shapes, {}, n * 3 + N_DEV - 1, run)


def _half_tile(rh, mult=16, want=256):
    best = None
    for t in range(mult, min(rh, want) + 1, mult):
        if rh % t == 0:
            best = t
    return best if best is not None else rh


def _pair_sum(g, sib, ids, name):
    _, R, C = g.shape
    rh = R // 2
    rt = _half_tile(rh)
    nt = rh // rt

    def body(ids_ref, g_ref, s_ref, o_ref):
        o_ref[...] = (g_ref[...] + s_ref[...]).astype(BF16)

    grid_spec = pltpu.PrefetchScalarGridSpec(
        num_scalar_prefetch=1, grid=(N_SHARDS, nt),
        in_specs=[pl.BlockSpec((1, rt, C), lambda s, i, ids: (s, ids[2] * nt + i, 0)),
                  pl.BlockSpec((1, rt, C), lambda s, i, ids: (s, i, 0))],
        out_specs=pl.BlockSpec((1, rt, C), lambda s, i, ids: (s, i, 0)))
    return pl.pallas_call(body, grid_spec=grid_spec, out_shape=jax.ShapeDtypeStruct((N_SHARDS, rh, C), BF16),
                          compiler_params=_params(2), name=name)(ids, g, sib)


def _final_sum(g, sib, got, ids, name):
    _, R, C = g.shape
    rh = R // 2
    rt = _half_tile(rh)
    nt = rh // rt

    def body(ids_ref, g_ref, s_ref, r_ref, o_ref):
        tot = g_ref[0] + s_ref[0]
        for k in range(3):
            tot = tot + r_ref[k].astype(F32)
        o_ref[...] = tot

    grid_spec = pltpu.PrefetchScalarGridSpec(
        num_scalar_prefetch=1, grid=(nt,),
        in_specs=[pl.BlockSpec((1, rt, C), lambda i, ids: (2 * ids[0] + ids[1], ids[2] * nt + i, 0)),
                  pl.BlockSpec((1, rt, C), lambda i, ids: (2 * ids[0] + ids[1], i, 0)),
                  pl.BlockSpec((3, rt, C), lambda i, ids: (0, i, 0))],
        out_specs=pl.BlockSpec((rt, C), lambda i, ids: (ids[2] * nt + i, 0)))
    return pl.pallas_call(body, grid_spec=grid_spec, out_shape=jax.ShapeDtypeStruct((R, C), F32),
                          compiler_params=_params(1), name=name)(ids, g, sib, got)


def _sum_packs(all_packs):
    def body(p_ref, o_ref):
        tot = p_ref[0]
        for i in range(1, N_DEV):
            tot = tot + p_ref[i]
        o_ref[...] = tot

    return pl.pallas_call(body, in_specs=[VMEM_SPEC], out_specs=VMEM_SPEC,
                          out_shape=jax.ShapeDtypeStruct(all_packs.shape[1:], F32), name="sum_packs")(all_packs)


def _adamw(w, g, m, v, name):
    R, C = w.shape
    rt = _half_tile(R, mult=8, want=256)

    def body(w_ref, g_ref, m_ref, v_ref, d_ref, nm_ref, nv_ref):
        gg = g_ref[...]
        nm = ADAM_B1 * m_ref[...] + (1.0 - ADAM_B1) * gg
        nv = ADAM_B2 * v_ref[...] + (1.0 - ADAM_B2) * (gg * gg)
        m_hat = nm / (1.0 - ADAM_B1 ** ADAM_STEP)
        v_hat = nv / (1.0 - ADAM_B2 ** ADAM_STEP)
        d_ref[...] = -ADAM_LR * (m_hat / (jnp.sqrt(v_hat) + ADAM_EPS) + ADAM_WD * w_ref[...])
        nm_ref[...] = nm
        nv_ref[...] = nv

    spec = pl.BlockSpec((rt, C), lambda i: (i, 0))
    return pl.pallas_call(body, grid=(R // rt,), in_specs=[spec] * 4, out_specs=[spec] * 3,
                          out_shape=[jax.ShapeDtypeStruct((R, C), F32)] * 3,
                          compiler_params=_params(1), name=name)(w, g, m, v)


def _adamw_update(w, g, m, v):
    nm = ADAM_B1 * m + (1.0 - ADAM_B1) * g
    nv = ADAM_B2 * v + (1.0 - ADAM_B2) * (g * g)
    m_hat = nm / (1.0 - ADAM_B1 ** ADAM_STEP)
    v_hat = nv / (1.0 - ADAM_B2 ** ADAM_STEP)
    return -ADAM_LR * (m_hat / (jnp.sqrt(v_hat) + ADAM_EPS) + ADAM_WD * w), nm, nv


def _adamw_many(ws, gs, ms, vs, name):
    n = len(ws)

    def body(*refs):
        for i in range(n):
            d, nm, nv = _adamw_update(refs[i][...], refs[n + i][...], refs[2 * n + i][...], refs[3 * n + i][...])
            refs[4 * n + i][...] = d
            refs[5 * n + i][...] = nm
            refs[6 * n + i][...] = nv

    return pl.pallas_call(body, in_specs=[VMEM_SPEC] * (4 * n), out_specs=[VMEM_SPEC] * (3 * n),
                          out_shape=[jax.ShapeDtypeStruct(w.shape, F32) for w in ws] * 3, name=name,
                          )(*ws, *gs, *ms, *vs)


def _pack(pieces):
    rows = []
    for p in pieces:
        flat = p.reshape(-1)
        pad = (-flat.shape[0]) % LANES
        if pad:
            flat = jnp.concatenate([flat, jnp.zeros((pad,), F32)])
        rows.append(flat.reshape(-1, LANES))
    total = sum(r.shape[0] for r in rows)
    pad_rows = (-total) % 8
    if pad_rows:
        rows.append(jnp.zeros((pad_rows, LANES), F32))
    return jnp.concatenate(rows, axis=0)


def _unpack(buf, shapes):
    out, r0 = [], 0
    for shp in shapes:
        n = int(np.prod(shp))
        nr = -(-n // LANES)
        out.append(buf[r0:r0 + nr].reshape(-1)[:n].reshape(shp))
        r0 += nr
    return out


SMALL_NAMES = ("rel_table", "b_in", "conv_w", "conv_b", "conv_ln_g", "conv_ln_b", "attn_norm_g", "conv_norm_g",
               "ln1_g", "ln1_b", "ffn_conv_w", "ffn_conv_b", "ln2_g", "ln2_b")
BIG_NAMES = ("w_in", "w_out", "w_up", "w_down")
WEIGHT_ORDER = ("rel_table", "w_in", "b_in", "conv_w", "conv_b", "conv_ln_g", "conv_ln_b", "attn_norm_g",
                "conv_norm_g", "w_out", "ln1_g", "ln1_b", "w_up", "ffn_conv_w", "ffn_conv_b", "w_down",
                "ln2_g", "ln2_b")


def kernel(x, rel_table, w_in, b_in, conv_w, conv_b, conv_ln_g, conv_ln_b, attn_norm_g, conv_norm_g, w_out, ln1_g, ln1_b, w_up, ffn_conv_w, ffn_conv_b, w_down, ln2_g, ln2_b, loss_target, m_rel_table, m_w_in, m_b_in, m_conv_w, m_conv_b, m_conv_ln_g, m_conv_ln_b, m_attn_norm_g, m_conv_norm_g, m_w_out, m_ln1_g, m_ln1_b, m_w_up, m_ffn_conv_w, m_ffn_conv_b, m_w_down, m_ln2_g, m_ln2_b, v_rel_table, v_w_in, v_b_in, v_conv_w, v_conv_b, v_conv_ln_g, v_conv_ln_b, v_attn_norm_g, v_conv_norm_g, v_w_out, v_ln1_g, v_ln1_b, v_w_up, v_ffn_conv_w, v_ffn_conv_b, v_w_down, v_ln2_g, v_ln2_b):
    args = dict(locals())
    weights = {n: args[n] for n in WEIGHT_ORDER}
    moms = {n: args["m_" + n] for n in WEIGHT_ORDER}
    vels = {n: args["v_" + n] for n in WEIGHT_ORDER}
    xi, yi, ci = _place()
    ids = jnp.stack([xi, yi, ci]).astype(jnp.int32)
    shard = 2 * xi + yi
    D = x.shape[-1]
    DFF = w_down.shape[1] * N_SHARDS
    CW = conv_norm_g.shape[-1]

    (g_in,), (g_cw, g_fcw) = _gather_weights([w_in[0]], [conv_w[0], ffn_conv_w[0]])
    cols = lambda t: jnp.transpose(t, (1, 0, 2)).reshape(t.shape[1], N_SHARDS * t.shape[2])
    staged = [_stage_half(w[0], ids, name="stage_" + n) for w, n in ((w_out, "w_out"), (w_up, "w_up"),
                                                                     (w_down, "w_down"))]

    grad_x, fulls, all_packs = _local_step(
        x, loss_target, rel_table, cols(g_in), b_in, cols(g_cw), conv_b, conv_ln_g, conv_ln_b, attn_norm_g,
        conv_norm_g, staged, ln1_g, ln1_b, cols(g_fcw), ffn_conv_b, ln2_g, ln2_b, ids)
    big_grads = dict(zip(BIG_NAMES, _sibling_assemble(fulls)))
    for n in ("w_in", "w_up"):
        big_grads[n] = _transpose(big_grads[n], name="transpose_d" + n)

    summed = _sum_packs(all_packs)
    full_shapes = {n: weights[n].shape for n in SMALL_NAMES}
    full_shapes["conv_w"] = (1, CONV_KERNEL, CW)
    full_shapes["ffn_conv_w"] = (1, FFN_CONV_KERNEL, 2 * DFF)
    un = _unpack(summed, [(1, LANES)] + [full_shapes[n] for n in SMALL_NAMES])
    loss = un[0][0, 0]
    small_grads = dict(zip(SMALL_NAMES, un[1:]))
    for n in ("conv_w", "ffn_conv_w"):
        width = weights[n].shape[-1]
        small_grads[n] = lax.dynamic_slice_in_dim(small_grads[n], shard * width, width, axis=2)

    grads, delta, new_m, new_v = {}, {}, {}, {}
    for n in BIG_NAMES:
        shp = weights[n].shape
        g2 = big_grads[n]
        d, nm, nv = _adamw(weights[n][0], g2, moms[n][0], vels[n][0], name="adamw_" + n)
        grads[n], delta[n], new_m[n], new_v[n] = (t.reshape(shp) for t in (g2, d, nm, nv))
    pick = lambda src: [src[n] for n in SMALL_NAMES]
    small_out = _adamw_many(pick(weights), pick(small_grads), pick(moms), pick(vels), name="adamw_small")
    ns = len(SMALL_NAMES)
    for tgt, part in ((delta, small_out[:ns]), (new_m, small_out[ns:2 * ns]), (new_v, small_out[2 * ns:])):
        tgt.update(zip(SMALL_NAMES, part))
    grads.update(small_grads)

    return (loss, grad_x, *[grads[n] for n in WEIGHT_ORDER], *[delta[n] for n in WEIGHT_ORDER],
            *[new_m[n] for n in WEIGHT_ORDER], *[new_v[n] for n in WEIGHT_ORDER])
```

```python
import functools
import math

import numpy as np
import jax
import jax.numpy as jnp
from jax import lax
from jax.experimental import pallas as pl
from jax.experimental.pallas import tpu as pltpu

F32 = jnp.float32
BF16 = jnp.bfloat16
MESH = pl.DeviceIdType.MESH

HEAD_DIM = 64
LANES = 128
ATTN_BLOCK = 128
DILATED_CONFIGS = ((128, 1), (512, 4), (2048, 16))
CONV_KERNEL = 31
FFN_CONV_KERNEL = 3
REL_BUCKETS = 32
REL_MAX_DIST = 2048
DEPTH = 1
ALPHA = (2 * DEPTH) ** 0.25
LN_EPS = 1e-5
NEG_INF = -1e30
QK_SCALE = 1.0 / math.sqrt(HEAD_DIM)
ADAM_LR = 0.001
ADAM_B1 = 0.9
ADAM_B2 = 0.999
ADAM_EPS = 1e-08
ADAM_WD = 0.01
ADAM_STEP = 10
VMEM_LIMIT = 52 * 1024 * 1024
FFN_COLS = 128
N_SHARDS = 4
N_DEV = 8


def _params(n_axes):
    return pltpu.CompilerParams(dimension_semantics=("arbitrary",) * n_axes,
                                vmem_limit_bytes=VMEM_LIMIT)


MM_DIMS = {"nn": (((1,), (0,)), ((), ())), "nt": (((1,), (1,)), ((), ())), "tn": (((0,), (0,)), ((), ()))}


class _Background:
    def __init__(self, in_arrays, out_shapes, aliases, n_sems, run, n_local=1):
        self.in_arrays, self.out_shapes, self.aliases = list(in_arrays), list(out_shapes), dict(aliases)
        self.n_sems, self.n_local, self.run = n_sems, n_local, run

    def scratch(self):
        return [pltpu.SemaphoreType.DMA((self.n_sems,)), pltpu.SemaphoreType.DMA((self.n_sems,)),
                pltpu.SemaphoreType.DMA((self.n_local,))]


def _hosted_call(body, bg, *, grid, in_specs, out_specs, out_shape, scratch_shapes, operands, name):
    n_in, n_out, n_scr = len(in_specs), len(out_specs), len(scratch_shapes)
    if bg is None:
        return pl.pallas_call(lambda *refs: body(refs, lambda post: None), grid=grid, in_specs=in_specs,
                              out_specs=out_specs, out_shape=out_shape, scratch_shapes=scratch_shapes,
                              compiler_params=_params(len(grid)), name=name)(*operands)
    nb_in, nb_out = len(bg.in_arrays), len(bg.out_shapes)
    n_steps = int(np.prod(grid))

    def full_body(*refs):
        own = refs[:n_in] + refs[n_in + nb_in:n_in + nb_in + n_out] \
            + refs[n_in + nb_in + n_out + nb_out:n_in + nb_in + n_out + nb_out + n_scr]
        bg_in = refs[n_in:n_in + nb_in]
        bg_out = refs[n_in + nb_in + n_out:n_in + nb_in + n_out + nb_out]
        sems = refs[n_in + nb_in + n_out + nb_out + n_scr:]
        step = pl.program_id(0)
        for ax in range(1, len(grid)):
            step = step * grid[ax] + pl.program_id(ax)

        def hook(post):
            bg.run(step, n_steps, bg_in, bg_out, *sems, post)

        body(own, hook)

    res = pl.pallas_call(
        full_body, grid=grid, in_specs=list(in_specs) + [HBM_SPEC] * nb_in,
        out_specs=list(out_specs) + [HBM_SPEC] * nb_out, out_shape=list(out_shape) + bg.out_shapes,
        input_output_aliases={n_in + a: n_out + o for a, o in bg.aliases.items()},
        scratch_shapes=list(scratch_shapes) + bg.scratch(), compiler_params=_params(len(grid)), name=name,
    )(*operands, *bg.in_arrays)
    return res


def _matmul_general(ins, part_fn, *, grid, tm, tn, outs, epilogue, extras=(), name, bg=None):
    nk = grid[2]
    n_in, n_extra = len(ins), len(extras)

    def body(refs, bg_hook):
        in_refs = refs[:n_in]
        rest = refs[n_in:]
        extra_refs = rest[:n_extra]
        out_refs = rest[n_extra:n_extra + len(outs)]
        acc_ref = rest[-1]
        i, j, k = pl.program_id(0), pl.program_id(1), pl.program_id(2)
        bg_hook(False)
        part = part_fn(in_refs, i, j, k)
        if nk == 1:
            epilogue(part, i, j, extra_refs, out_refs)
        else:
            @pl.when(k == 0)
            def _():
                acc_ref[...] = part

            @pl.when(k > 0)
            def _():
                acc_ref[...] += part

            @pl.when(k == nk - 1)
            def _():
                epilogue(acc_ref[...], i, j, extra_refs, out_refs)
        bg_hook(True)

    in_specs = [pl.BlockSpec(bs, im) for (_, bs, im) in list(ins) + list(extras)]
    out_specs = [pl.BlockSpec(bs, im) for (_, _, bs, im) in outs]
    out_shape = [jax.ShapeDtypeStruct(s, d) for (s, d, _, _) in outs]
    return _hosted_call(body, bg, grid=grid, in_specs=in_specs, out_specs=out_specs, out_shape=out_shape,
                        scratch_shapes=[pltpu.VMEM((tm, tn), F32)],
                        operands=[e[0] for e in ins] + [e[0] for e in extras], name=name)


def _dot(a, b, mode):
    return lax.dot_general(a.astype(BF16), b.astype(BF16), MM_DIMS[mode], preferred_element_type=F32)


def _matmul(a, b, *, mode, tm, tn, tk, outs, epilogue, extras=(), name, bg=None):
    if mode == "tn":
        K, M = a.shape
        N = b.shape[1]
        ins = [(a, (tk, tm), lambda i, j, k: (k, i)), (b, (tk, tn), lambda i, j, k: (k, j))]
    elif mode == "nt":
        M, K = a.shape
        N = b.shape[0]
        ins = [(a, (tm, tk), lambda i, j, k: (i, k)), (b, (tn, tk), lambda i, j, k: (j, k))]
    else:
        M, K = a.shape
        N = b.shape[1]
        ins = [(a, (tm, tk), lambda i, j, k: (i, k)), (b, (tk, tn), lambda i, j, k: (k, j))]
    assert M % tm == 0 and N % tn == 0 and K % tk == 0, (name, M, N, K, tm, tn, tk)

    def part_fn(in_refs, i, j, k):
        return _dot(in_refs[0][...], in_refs[1][...], mode)

    return _matmul_general(ins, part_fn, grid=(M // tm, N // tn, K // tk), tm=tm, tn=tn, outs=outs,
                           epilogue=epilogue, extras=extras, name=name, bg=bg)


def _plain_out(M, N, tm, tn, dtype):
    return ((M, N), dtype, (tm, tn), lambda i, j, k: (i, j))


def _mm_plain(a, b, *, mode, tm, tn, tk, out_dtype, name, bias=None, bg=None):
    if mode == "tn":
        M, N = a.shape[1], b.shape[1]
    elif mode == "nt":
        M, N = a.shape[0], b.shape[0]
    else:
        M, N = a.shape[0], b.shape[1]
    extras = []
    if bias is not None:
        extras.append((bias, (1, tn), lambda i, j, k: (0, j)))

    def epilogue(acc, i, j, extra_refs, out_refs):
        if bias is not None:
            acc = acc + extra_refs[0][...]
        out_refs[0][...] = acc.astype(out_dtype)

    res = _matmul(a, b, mode=mode, tm=tm, tn=tn, tk=tk, outs=[_plain_out(M, N, tm, tn, out_dtype)],
                  epilogue=epilogue, extras=extras, name=name, bg=bg)
    return res[0] if bg is None else res


def _row_tile(T, want):
    t = min(T, want)
    while T % t:
        t //= 2
    return t


def _col_tile(N, want):
    if N <= want:
        return N
    best = None
    for c in range(LANES, want + 1, LANES):
        if N % c == 0:
            best = c
    return best if best is not None else N


def _accumulate(ref, first, val):
    @pl.when(first)
    def _():
        ref[...] = val

    @pl.when(jnp.logical_not(first))
    def _():
        ref[...] += val


def _ln_fwd(z, g, b):
    mu = jnp.mean(z, axis=-1, keepdims=True)
    zc = z - mu
    var = jnp.mean(zc * zc, axis=-1, keepdims=True)
    r = lax.rsqrt(var + LN_EPS)
    xh = zc * r
    return xh * g + b, xh, r


def _ln_bwd(dy, xh, r, g):
    dxh = dy * g
    m1 = jnp.mean(dxh, axis=-1, keepdims=True)
    m2 = jnp.mean(dxh * xh, axis=-1, keepdims=True)
    return r * (dxh - m1 - xh * m2)


def _sigmoid(x):
    return 1.0 / (1.0 + jnp.exp(-x))


def _shift_down(x, s, row):
    if s == 0:
        return x
    rolled = pltpu.roll(x, s, 0)
    nfix = -(-s // 8) * 8
    head = jnp.where(row[:nfix] >= s, rolled[:nfix], 0.0)
    return jnp.concatenate([head, rolled[nfix:]], axis=0)


def _shift_up(x, s, row):
    if s == 0:
        return x
    n = x.shape[0]
    rolled = pltpu.roll(x, n - s, 0)
    nfix = -(-s // 8) * 8
    tail = jnp.where(row[n - nfix:] < n - s, rolled[n - nfix:], 0.0)
    return jnp.concatenate([rolled[:n - nfix], tail], axis=0)


def _bucket_tables():
    exact = REL_BUCKETS // 2
    qi = np.arange(ATTN_BLOCK)[:, None]
    kj = np.arange(2 * ATTN_BLOCK)[None, :]
    steps = qi + ATTN_BLOCK - kj
    buckets, masks = [], []
    for window, dilation in DILATED_CONFIGS:
        max_steps = window // dilation
        band = (steps >= 0) & (steps <= max_steps)
        dist = np.maximum(steps, 0) * dilation
        d_f = np.maximum(dist, 1).astype(np.float32)
        large = exact + (np.log(d_f / np.float32(exact)) / np.float32(math.log(REL_MAX_DIST / exact))
                         * np.float32(REL_BUCKETS - exact)).astype(np.int32)
        large = np.minimum(large, REL_BUCKETS - 1)
        bucket = np.where(dist < exact, dist, large).astype(np.int32)
        buckets.append(bucket.reshape(1, -1))
        masks.append(np.where(band, 0.0, NEG_INF).astype(np.float32).reshape(1, -1))
    return np.stack(buckets), np.stack(masks)


def _split_hi_lo(x):
    hi = x.astype(BF16)
    lo = (x - hi.astype(F32)).astype(BF16)
    return hi, lo


def _bias_build(rel_table_t, bucket, mask):
    H = rel_table_t.shape[0]
    n = bucket.shape[-1]

    def body(t_ref, bkt_ref, mask_ref, o_ref):
        onehot = (lax.broadcasted_iota(jnp.int32, (REL_BUCKETS, n), 0) == bkt_ref[0]).astype(BF16)
        t = t_ref[...]
        t1 = t.astype(BF16)
        r1 = t - t1.astype(F32)
        t2 = r1.astype(BF16)
        t3 = (r1 - t2.astype(F32)).astype(BF16)
        acc = jnp.dot(t1, onehot, preferred_element_type=F32)
        acc = acc + jnp.dot(t2, onehot, preferred_element_type=F32)
        acc = acc + jnp.dot(t3, onehot, preferred_element_type=F32)
        o_ref[0] = acc + mask_ref[0]

    return pl.pallas_call(
        body, grid=(3,),
        in_specs=[pl.BlockSpec((H, REL_BUCKETS), lambda b: (0, 0)),
                  pl.BlockSpec((1, 1, n), lambda b: (b, 0, 0)),
                  pl.BlockSpec((1, 1, n), lambda b: (b, 0, 0))],
        out_specs=pl.BlockSpec((1, H, n), lambda b: (b, 0, 0)),
        out_shape=jax.ShapeDtypeStruct((3, H, n), F32),
        compiler_params=_params(1), name="bias_build",
    )(rel_table_t, bucket, mask)


def _rel_grad(dbias, bucket):
    H = dbias.shape[1]
    n = bucket.shape[-1]
    dims = (((1,), (1,)), ((), ()))

    def body(d_ref, bkt_ref, o_ref):
        b = pl.program_id(0)
        onehot = (lax.broadcasted_iota(jnp.int32, (REL_BUCKETS, n), 0) == bkt_ref[0]).astype(BF16)
        d = d_ref[0]
        d1 = d.astype(BF16)
        r1 = d - d1.astype(F32)
        d2 = r1.astype(BF16)
        d3 = (r1 - d2.astype(F32)).astype(BF16)
        acc = lax.dot_general(d1, onehot, dims, preferred_element_type=F32)
        acc = acc + lax.dot_general(d2, onehot, dims, preferred_element_type=F32)
        acc = acc + lax.dot_general(d3, onehot, dims, preferred_element_type=F32)
        _accumulate(o_ref, b == 0, acc)

    return pl.pallas_call(
        body, grid=(3,),
        in_specs=[pl.BlockSpec((1, H, n), lambda b: (b, 0, 0)),
                  pl.BlockSpec((1, 1, n), lambda b: (b, 0, 0))],
        out_specs=pl.BlockSpec((H, REL_BUCKETS), lambda b: (0, 0)),
        out_shape=jax.ShapeDtypeStruct((H, REL_BUCKETS), F32),
        compiler_params=_params(1), name="rel_grad",
    )(dbias, bucket)


def _attn_specs(B, S, AW, d):
    L = S // d
    HP = AW // LANES
    W3 = 3 * HP
    q_spec = pl.BlockSpec((1, L, LANES), lambda h, b, r: (b, 0, r * W3 + h))
    k_spec = pl.BlockSpec((1, L, LANES), lambda h, b, r: (b, 0, r * W3 + HP + h))
    v_spec = pl.BlockSpec((1, L, LANES), lambda h, b, r: (b, 0, r * W3 + 2 * HP + h))
    o_spec = pl.BlockSpec((1, L, LANES), lambda h, b, r: (b, 0, r * HP + h))
    bias_spec = pl.BlockSpec((2, ATTN_BLOCK, 2 * ATTN_BLOCK), lambda h, b, r: (h, 0, 0))
    return L, HP, q_spec, k_spec, v_spec, o_spec, bias_spec


def _attn_fwd(qkv, bias, B, S, AW, d, name):
    L, HP, q_spec, k_spec, v_spec, o_spec, bias_spec = _attn_specs(B, S, AW, d)
    nb = L // ATTN_BLOCK
    nt = (((1,), (1,)), ((), ()))

    def body(q_ref, k_ref, v_ref, b_ref, o_ref, lse_ref):
        head0 = lax.broadcasted_iota(jnp.int32, (1, LANES), 1) < HEAD_DIM

        def block(n, first):
            qs = pl.multiple_of(n * ATTN_BLOCK, ATTN_BLOCK)
            q = q_ref[0, pl.ds(qs, ATTN_BLOCK), :]
            if first:
                kk = k_ref[0, pl.ds(0, ATTN_BLOCK), :]
                vv = v_ref[0, pl.ds(0, ATTN_BLOCK), :]
            else:
                ks = pl.multiple_of(n * ATTN_BLOCK - ATTN_BLOCK, ATTN_BLOCK)
                kk = k_ref[0, pl.ds(ks, 2 * ATTN_BLOCK), :]
                vv = v_ref[0, pl.ds(ks, 2 * ATTN_BLOCK), :]
            outs, lses = [], []
            for e in range(2):
                msk = head0 if e == 0 else jnp.logical_not(head0)
                qe = jnp.where(msk, q, jnp.zeros_like(q))
                s = lax.dot_general(qe, kk, nt, preferred_element_type=F32) * QK_SCALE
                s = s + (b_ref[e, :, ATTN_BLOCK:] if first else b_ref[e])
                m = jnp.max(s, axis=-1, keepdims=True)
                p = jnp.exp(s - m)
                l = jnp.sum(p, axis=-1, keepdims=True)
                o = jnp.dot(p.astype(BF16), vv, preferred_element_type=F32)
                outs.append(o / l)
                lses.append(jnp.broadcast_to(m + jnp.log(l), (ATTN_BLOCK, LANES)))
            o_ref[0, pl.ds(qs, ATTN_BLOCK), :] = jnp.where(head0, outs[0], outs[1])
            lse_ref[0, pl.ds(qs, ATTN_BLOCK), :] = jnp.where(head0, lses[0], lses[1])

        block(0, True)
        if nb > 1:
            def loop(n, c):
                block(n, False)
                return c
            lax.fori_loop(1, nb, loop, 0)

    qv = qkv.reshape(B, L, d * 3 * AW)
    o, lse = pl.pallas_call(
        body, grid=(HP, B, d), in_specs=[q_spec, k_spec, v_spec, bias_spec],
        out_specs=[o_spec, o_spec],
        out_shape=[jax.ShapeDtypeStruct((B, L, d * AW), F32)] * 2,
        compiler_params=_params(3), name=name,
    )(qv, qv, qv, bias)
    return o.reshape(B * S, AW), lse.reshape(B * S, AW)


def _attn_bwd(qkv, do, lse, dd, bias, B, S, AW, d, name):
    L, HP, q_spec, k_spec, v_spec, o_spec, bias_spec = _attn_specs(B, S, AW, d)
    nb = L // ATTN_BLOCK
    nt = (((1,), (1,)), ((), ()))
    tn = (((0,), (0,)), ((), ()))

    def body(q_ref, k_ref, v_ref, do_ref, lse_ref, dd_ref, b_ref, dq_ref, dk_ref, dv_ref, db_ref):
        head0 = lax.broadcasted_iota(jnp.int32, (1, LANES), 1) < HEAD_DIM
        first_step = jnp.logical_and(pl.program_id(1) == 0, pl.program_id(2) == 0)

        @pl.when(first_step)
        def _():
            db_ref[...] = jnp.zeros_like(db_ref)

        dk_ref[...] = jnp.zeros_like(dk_ref)
        dv_ref[...] = jnp.zeros_like(dv_ref)

        def block(n, first):
            qs = pl.multiple_of(n * ATTN_BLOCK, ATTN_BLOCK)
            nkeys = ATTN_BLOCK if first else 2 * ATTN_BLOCK
            ks = 0 if first else pl.multiple_of(n * ATTN_BLOCK - ATTN_BLOCK, ATTN_BLOCK)
            q = q_ref[0, pl.ds(qs, ATTN_BLOCK), :]
            kk = k_ref[0, pl.ds(ks, nkeys), :]
            vv = v_ref[0, pl.ds(ks, nkeys), :]
            dout = do_ref[0, pl.ds(qs, ATTN_BLOCK), :]
            lse_b = lse_ref[0, pl.ds(qs, ATTN_BLOCK), :]
            dd_b = dd_ref[0, pl.ds(qs, ATTN_BLOCK), :]
            dq = jnp.zeros((ATTN_BLOCK, LANES), F32)
            dkk = jnp.zeros((nkeys, LANES), F32)
            dvv = jnp.zeros((nkeys, LANES), F32)
            for e in range(2):
                msk = head0 if e == 0 else jnp.logical_not(head0)
                c0 = e * HEAD_DIM
                qe = jnp.where(msk, q, jnp.zeros_like(q))
                doe = jnp.where(msk, dout, jnp.zeros_like(dout))
                kke = jnp.where(msk, kk, jnp.zeros_like(kk))
                s = lax.dot_general(qe, kk, nt, preferred_element_type=F32) * QK_SCALE
                s = s + (b_ref[e, :, ATTN_BLOCK:] if first else b_ref[e])
                p = jnp.exp(s - lse_b[:, c0:c0 + 1])
                dp = lax.dot_general(doe, vv, nt, preferred_element_type=F32)
                ds = p * (dp - dd_b[:, c0:c0 + 1])
                if first:
                    db_ref[e, :, ATTN_BLOCK:] += ds
                else:
                    db_ref[e] += ds
                dsb = (ds * QK_SCALE).astype(BF16)
                dq = dq + jnp.dot(dsb, kke, preferred_element_type=F32)
                dkk = dkk + lax.dot_general(dsb, qe, tn, preferred_element_type=F32)
                dvv = dvv + lax.dot_general(p.astype(BF16), doe, tn, preferred_element_type=F32)
            dq_ref[0, pl.ds(qs, ATTN_BLOCK), :] = dq
            dk_ref[0, pl.ds(ks, nkeys), :] += dkk
            dv_ref[0, pl.ds(ks, nkeys), :] += dvv

        block(0, True)
        if nb > 1:
            def loop(n, c):
                block(n, False)
                return c
            lax.fori_loop(1, nb, loop, 0)

    H = AW // HEAD_DIM
    qv = qkv.reshape(B, L, d * 3 * AW)
    view = lambda t: t.reshape(B, L, d * AW)
    dq, dk, dv, db = pl.pallas_call(
        body, grid=(HP, B, d),
        in_specs=[q_spec, k_spec, v_spec, o_spec, o_spec, o_spec, bias_spec],
        out_specs=[o_spec, o_spec, o_spec, bias_spec],
        out_shape=[jax.ShapeDtypeStruct((B, L, d * AW), F32)] * 3
        + [jax.ShapeDtypeStruct((H, ATTN_BLOCK, 2 * ATTN_BLOCK), F32)],
        compiler_params=_params(3), name=name,
    )(qv, qv, qv, view(do), view(lse), view(dd), bias)
    flat = lambda t: t.reshape(B * S, AW)
    return flat(dq), flat(dk), flat(dv), db


def _attn_combine(ons, lses, gain, tm):
    T, AW = ons[0].shape

    def body(o1, o2, o3, l1, l2, l3, g_ref, attn_ref, lse_ref, mix_ref, r_ref):
        la, lb, lc = l1[...], l2[...], l3[...]
        m = jnp.maximum(jnp.maximum(la, lb), lc)
        ea, eb, ec = jnp.exp(la - m), jnp.exp(lb - m), jnp.exp(lc - m)
        den = ea + eb + ec
        attn = (ea * o1[...] + eb * o2[...] + ec * o3[...]) / den
        attn_ref[...] = attn
        lse_ref[...] = m + jnp.log(den)
        r = lax.rsqrt(jnp.mean(attn * attn, axis=-1, keepdims=True) + LN_EPS)
        mix_ref[...] = (attn * r * g_ref[...]).astype(BF16)
        r_ref[...] = jnp.broadcast_to(r, (tm, LANES))

    row = pl.BlockSpec((tm, AW), lambda i: (i, 0))
    return pl.pallas_call(
        body, grid=(T // tm,),
        in_specs=[row] * 6 + [pl.BlockSpec((1, AW), lambda i: (0, 0))],
        out_specs=[row, row, row, pl.BlockSpec((tm, LANES), lambda i: (i, 0))],
        out_shape=[jax.ShapeDtypeStruct((T, AW), F32), jax.ShapeDtypeStruct((T, AW), F32),
                   jax.ShapeDtypeStruct((T, AW), BF16), jax.ShapeDtypeStruct((T, LANES), F32)],
        compiler_params=_params(1), name="attn_combine",
    )(*ons, *lses, gain)


def _to_sub(src_ref, stage_ref, dsts, S):
    stage_ref[...] = src_ref[0].astype(F32)
    for (_, d), dst in zip(DILATED_CONFIGS[1:], dsts):
        L = S // d
        for r in range(d):
            dst[r * L:(r + 1) * L, :] = stage_ref[pl.ds(r, L, stride=d), :].astype(dst.dtype)


def _branch_blocks(S, d, block):
    nb = S // d // ATTN_BLOCK
    inner_unroll = 3 if (nb - 1) % 3 == 0 else 1

    def per_residue(r, c):
        block(r * nb, True)
        if nb > 1:
            def inner(n, c2):
                block(r * nb + n, False)
                return c2
            lax.fori_loop(1, nb, inner, 0, unroll=inner_unroll)
        return c

    lax.fori_loop(0, d, per_residue, 0, unroll=4 if nb == 1 else 1)


def _attention_fwd(qkv, bias_all, B, S, AW):
    HP = AW // LANES
    nt = MM_DIMS["nt"]

    def body(q_ref, k_ref, v_ref, b_ref, o_ref, lse_ref, stage, q4, q16, k4, k16, v4, v16, o1, l1, o4, l4, o16, l16):
        head0 = lax.broadcasted_iota(jnp.int32, (1, LANES), 1) < HEAD_DIM
        _to_sub(q_ref, stage, (q4, q16), S)
        _to_sub(k_ref, stage, (k4, k16), S)
        _to_sub(v_ref, stage, (v4, v16), S)
        srcs = ((q_ref.at[0], k_ref.at[0], v_ref.at[0], o1, l1), (q4, k4, v4, o4, l4), (q16, k16, v16, o16, l16))
        for bi, (_, d) in enumerate(DILATED_CONFIGS):
            qs_ref, ks_ref, vs_ref, od_ref, ld_ref = srcs[bi]

            def block(g, first, bi=bi, qs_ref=qs_ref, ks_ref=ks_ref, vs_ref=vs_ref, od_ref=od_ref, ld_ref=ld_ref):
                qs = pl.multiple_of(g * ATTN_BLOCK, ATTN_BLOCK)
                nkeys = ATTN_BLOCK if first else 2 * ATTN_BLOCK
                ks = qs if first else pl.multiple_of(qs - ATTN_BLOCK, ATTN_BLOCK)
                q = qs_ref[pl.ds(qs, ATTN_BLOCK), :]
                kk = ks_ref[pl.ds(ks, nkeys), :]
                vv = vs_ref[pl.ds(ks, nkeys), :]
                outs, lses = [], []
                for e in range(2):
                    msk = head0 if e == 0 else jnp.logical_not(head0)
                    qe = jnp.where(msk, q * QK_SCALE, jnp.zeros_like(q))
                    s = lax.dot_general(qe, kk, nt, preferred_element_type=F32)
                    s = s + (b_ref[bi, e, :, ATTN_BLOCK:] if first else b_ref[bi, e])
                    m = jnp.max(s, axis=-1, keepdims=True)
                    p = jnp.exp(s - m)
                    l = jnp.sum(p, axis=-1, keepdims=True)
                    o = jnp.dot(p.astype(BF16), vv, preferred_element_type=F32)
                    outs.append(o / l)
                    lses.append(jnp.broadcast_to(m + jnp.log(l), (ATTN_BLOCK, LANES)))
                od_ref[pl.ds(qs, ATTN_BLOCK), :] = jnp.where(head0, outs[0], outs[1])
                ld_ref[pl.ds(qs, ATTN_BLOCK), :] = jnp.where(head0, lses[0], lses[1])

            _branch_blocks(S, d, block)

        def natural(sub_ref, d):
            L = S // d
            for r in range(d):
                stage[pl.ds(r, L, stride=d), :] = sub_ref[r * L:(r + 1) * L, :]
            return stage[...]

        la = l1[...]
        lb = natural(l4, 4)
        lc = natural(l16, 16)
        m = jnp.maximum(jnp.maximum(la, lb), lc)
        ea, eb, ec = jnp.exp(la - m), jnp.exp(lb - m), jnp.exp(lc - m)
        den = ea + eb + ec
        lse_ref[0] = m + jnp.log(den)
        acc = ea * o1[...]
        acc = acc + eb * natural(o4, 4)
        acc = acc + ec * natural(o16, 16)
        o_ref[0] = acc / den

    blk = lambda off: pl.BlockSpec((1, S, LANES), lambda b, h: (b, 0, off + h))
    qv = qkv.reshape(B, S, 3 * AW)
    sub_b = pltpu.VMEM((S, LANES), BF16)
    sub_f = pltpu.VMEM((S, LANES), F32)
    o, lse = pl.pallas_call(
        body, grid=(B, HP),
        in_specs=[blk(0), blk(HP), blk(2 * HP),
                  pl.BlockSpec((3, 2, ATTN_BLOCK, 2 * ATTN_BLOCK), lambda b, h: (0, h, 0, 0))],
        out_specs=[blk(0), blk(0)],
        out_shape=[jax.ShapeDtypeStruct((B, S, AW), F32)] * 2,
        scratch_shapes=[sub_f] + [sub_b] * 6 + [sub_f] * 6,
        compiler_params=_params(2), name="attention_fwd",
    )(qv, qv, qv, bias_all)
    return o.reshape(B * S, AW), lse.reshape(B * S, AW)


def _attention_bwd(qkv, do, lse, dd, bias_all, B, S, AW):
    HP = AW // LANES
    H = AW // HEAD_DIM
    nt, tn = MM_DIMS["nt"], MM_DIMS["tn"]

    def body(q_ref, k_ref, v_ref, do_ref, lse_ref, dd_ref, b_ref,
             dq_ref, dk_ref, dv_ref, csq_ref, csk_ref, csv_ref, db_ref,
             stage, q4, q16, k4, k16, v4, v16, g4, g16, l4, l16, d4, d16,
             aq1, ak1, av1, aq4, ak4, av4, aq16, ak16, av16):
        head0 = lax.broadcasted_iota(jnp.int32, (1, LANES), 1) < HEAD_DIM
        first_b = pl.program_id(1) == 0

        @pl.when(first_b)
        def _():
            db_ref[...] = jnp.zeros_like(db_ref)

        _to_sub(q_ref, stage, (q4, q16), S)
        _to_sub(k_ref, stage, (k4, k16), S)
        _to_sub(v_ref, stage, (v4, v16), S)
        _to_sub(do_ref, stage, (g4, g16), S)
        _to_sub(lse_ref, stage, (l4, l16), S)
        _to_sub(dd_ref, stage, (d4, d16), S)
        for acc in (ak1, av1, ak4, av4, ak16, av16):
            acc[...] = jnp.zeros_like(acc)
        srcs = ((q_ref.at[0], k_ref.at[0], v_ref.at[0], do_ref.at[0], lse_ref.at[0], dd_ref.at[0], aq1, ak1, av1),
                (q4, k4, v4, g4, l4, d4, aq4, ak4, av4), (q16, k16, v16, g16, l16, d16, aq16, ak16, av16))
        for bi, (_, d) in enumerate(DILATED_CONFIGS):
            def block(g, first, bi=bi, refs=srcs[bi]):
                qs_ref, ks_ref, vs_ref, gs_ref, ls_ref, ds_ref, aq, ak, av = refs
                qs = pl.multiple_of(g * ATTN_BLOCK, ATTN_BLOCK)
                nkeys = ATTN_BLOCK if first else 2 * ATTN_BLOCK
                ks = qs if first else pl.multiple_of(qs - ATTN_BLOCK, ATTN_BLOCK)
                q = qs_ref[pl.ds(qs, ATTN_BLOCK), :]
                kk = ks_ref[pl.ds(ks, nkeys), :]
                vv = vs_ref[pl.ds(ks, nkeys), :]
                dout = gs_ref[pl.ds(qs, ATTN_BLOCK), :]
                lse_b = ls_ref[pl.ds(qs, ATTN_BLOCK), :]
                dd_b = ds_ref[pl.ds(qs, ATTN_BLOCK), :]
                dq = jnp.zeros((ATTN_BLOCK, LANES), F32)
                dkk = jnp.zeros((nkeys, LANES), F32)
                dvv = jnp.zeros((nkeys, LANES), F32)
                for e in range(2):
                    msk = head0 if e == 0 else jnp.logical_not(head0)
                    c0 = e * HEAD_DIM
                    qe = jnp.where(msk, q * QK_SCALE, jnp.zeros_like(q))
                    doe = jnp.where(msk, dout, jnp.zeros_like(dout))
                    kke = jnp.where(msk, kk * QK_SCALE, jnp.zeros_like(kk))
                    s = lax.dot_general(qe, kk, nt, preferred_element_type=F32)
                    s = s + (b_ref[bi, e, :, ATTN_BLOCK:] if first else b_ref[bi, e])
                    p = jnp.exp(s - lse_b[:, c0:c0 + 1])
                    dp = lax.dot_general(doe, vv, nt, preferred_element_type=F32)
                    ds = p * (dp - dd_b[:, c0:c0 + 1])
                    if first:
                        db_ref[bi, e, :, ATTN_BLOCK:] += ds
                    else:
                        db_ref[bi, e] += ds
                    dsb = ds.astype(BF16)
                    dq = dq + jnp.dot(dsb, kke, preferred_element_type=F32)
                    dkk = dkk + lax.dot_general(dsb, qe, tn, preferred_element_type=F32)
                    dvv = dvv + lax.dot_general(p.astype(BF16), doe, tn, preferred_element_type=F32)
                aq[pl.ds(qs, ATTN_BLOCK), :] = dq
                ak[pl.ds(ks, nkeys), :] += dkk
                av[pl.ds(ks, nkeys), :] += dvv

            _branch_blocks(S, d, block)

        for a1, a4, a16, out_ref, cs_ref in ((aq1, aq4, aq16, dq_ref, csq_ref), (ak1, ak4, ak16, dk_ref, csk_ref),
                                             (av1, av4, av16, dv_ref, csv_ref)):
            stage[...] = a1[...]
            for d, sub in ((4, a4), (16, a16)):
                L = S // d
                for r in range(d):
                    stage[pl.ds(r, L, stride=d), :] += sub[r * L:(r + 1) * L, :]
            tot = stage[...]
            out_ref[0] = tot.astype(out_ref.dtype)
            _accumulate(cs_ref, first_b, jnp.sum(tot, axis=0, keepdims=True))

    blk = lambda off: pl.BlockSpec((1, S, LANES), lambda h, b: (b, 0, off + h))
    cs_spec = pl.BlockSpec((1, LANES), lambda h, b: (0, h))
    bias_spec = pl.BlockSpec((3, 2, ATTN_BLOCK, 2 * ATTN_BLOCK), lambda h, b: (0, h, 0, 0))
    qv = qkv.reshape(B, S, 3 * AW)
    view = lambda t: t.reshape(B, S, AW)
    sub_b = pltpu.VMEM((S, LANES), BF16)
    sub_f = pltpu.VMEM((S, LANES), F32)
    res = pl.pallas_call(
        body, grid=(HP, B),
        in_specs=[blk(0), blk(HP), blk(2 * HP), blk(0), blk(0), blk(0), bias_spec],
        out_specs=[blk(0), blk(0), blk(0), cs_spec, cs_spec, cs_spec, bias_spec],
        out_shape=[jax.ShapeDtypeStruct((B, S, AW), BF16)] * 3 + [jax.ShapeDtypeStruct((1, AW), F32)] * 3
        + [jax.ShapeDtypeStruct((3, H, ATTN_BLOCK, 2 * ATTN_BLOCK), F32)],
        scratch_shapes=[sub_f] + [sub_b] * 8 + [sub_f] * 4 + [sub_f] * 9,
        compiler_params=_params(2), name="attention_bwd",
    )(qv, qv, qv, view(do), view(lse), view(dd), bias_all)
    flat = lambda t: t.reshape(B * S, AW)
    return flat(res[0]), flat(res[1]), flat(res[2]), res[3], res[4], res[5], res[6]


def _regroup(src, stage, dst, d, S, off=0):
    if d == 1:
        dst[off:off + S, :] = src.astype(dst.dtype)
        return
    stage[...] = src.astype(F32)
    L = S // d
    for r in range(d):
        dst[off + r * L:off + (r + 1) * L, :] = stage[pl.ds(r, L, stride=d), :].astype(dst.dtype)


def _ungroup(sub_ref, off, nat_ref, d, S, add):
    L = S // d
    for r in range(d):
        rows = pl.ds(0, S) if d == 1 else pl.ds(r, L, stride=d)
        val = sub_ref[off + r * L:off + (r + 1) * L, :]
        if add:
            nat_ref[rows, :] += val
        else:
            nat_ref[rows, :] = val


def _branch_keys(ks, vs, S, nb, g_idx):
    blk3 = (S // ATTN_BLOCK, ATTN_BLOCK, LANES)
    kc3 = ks[ATTN_BLOCK:ATTN_BLOCK + S, :].reshape(blk3)
    vc3 = vs[ATTN_BLOCK:ATTN_BLOCK + S, :].reshape(blk3)
    if nb == 1:
        return kc3, vc3, None
    kk3 = jnp.concatenate([ks[0:S, :].reshape(blk3), kc3], axis=1)
    vv3 = jnp.concatenate([vs[0:S, :].reshape(blk3), vc3], axis=1)
    col = lax.broadcasted_iota(jnp.int32, (1, 1, 2 * ATTN_BLOCK), 2)
    dead = jnp.logical_and((g_idx & (nb - 1)) == 0, col < ATTN_BLOCK)
    return kk3, vv3, dead


def _branch_scores(qe, kk3, b_ref, bi, e, dead):
    s = jnp.einsum("gqe,gke->gqk", qe, kk3, preferred_element_type=F32)
    if dead is None:
        return s + b_ref[bi, e, :, ATTN_BLOCK:]
    return jnp.where(dead, NEG_INF, s + b_ref[bi, e])


def _attention_fwd(qkv, bias_all, B, S, AW, bg=None):
    HP = AW // LANES
    G = S // ATTN_BLOCK
    blk3 = (G, ATTN_BLOCK, LANES)

    def body(refs, bg_hook):
        q_ref, k_ref, v_ref, b_ref, o_ref, lse_ref, stage, qs, ks, vs, ot, lt, on0, on1, on2, ln0, ln1, ln2 = refs
        bg_hook(False)
        head0 = lax.broadcasted_iota(jnp.int32, (1, 1, LANES), 2) < HEAD_DIM
        g_idx = lax.broadcasted_iota(jnp.int32, (G, 1, 1), 0)
        ks[0:ATTN_BLOCK, :] = jnp.zeros((ATTN_BLOCK, LANES), BF16)
        vs[0:ATTN_BLOCK, :] = jnp.zeros((ATTN_BLOCK, LANES), BF16)
        nat_o, nat_l = (on0, on1, on2), (ln0, ln1, ln2)
        for bi, (_, d) in enumerate(DILATED_CONFIGS):
            nb = S // d // ATTN_BLOCK
            _regroup(q_ref[0], stage, qs, d, S)
            _regroup(k_ref[0], stage, ks, d, S, ATTN_BLOCK)
            _regroup(v_ref[0], stage, vs, d, S, ATTN_BLOCK)
            q3 = qs[...].reshape(blk3) * QK_SCALE
            kk3, vv3, dead = _branch_keys(ks, vs, S, nb, g_idx)
            outs, lses = [], []
            for e in range(2):
                msk = head0 if e == 0 else jnp.logical_not(head0)
                qe = jnp.where(msk, q3, jnp.zeros_like(q3))
                s = _branch_scores(qe, kk3, b_ref, bi, e, dead)
                m = jnp.max(s, axis=-1, keepdims=True)
                p = jnp.exp(s - m)
                l = jnp.sum(p, axis=-1, keepdims=True)
                o = jnp.einsum("gqk,gke->gqe", p.astype(BF16), vv3, preferred_element_type=F32)
                outs.append(o / l)
                lses.append(jnp.broadcast_to(m + jnp.log(l), blk3))
            ot[...] = jnp.where(head0, outs[0], outs[1]).reshape(S, LANES)
            lt[...] = jnp.where(head0, lses[0], lses[1]).reshape(S, LANES)
            _ungroup(ot, 0, nat_o[bi], d, S, add=False)
            _ungroup(lt, 0, nat_l[bi], d, S, add=False)

        la, lb, lc = ln0[...], ln1[...], ln2[...]
        m = jnp.maximum(jnp.maximum(la, lb), lc)
        ea, eb, ec = jnp.exp(la - m), jnp.exp(lb - m), jnp.exp(lc - m)
        den = ea + eb + ec
        lse_ref[0] = m + jnp.log(den)
        o_ref[0] = (ea * on0[...] + eb * on1[...] + ec * on2[...]) / den
        bg_hook(True)

    blk = lambda off: pl.BlockSpec((1, S, LANES), lambda b, h: (b, 0, off + h))
    qv = qkv.reshape(B, S, 3 * AW)
    sub_f = pltpu.VMEM((S, LANES), F32)
    pad_b = pltpu.VMEM((S + ATTN_BLOCK, LANES), BF16)
    res = _hosted_call(
        body, bg, grid=(B, HP),
        in_specs=[blk(0), blk(HP), blk(2 * HP),
                  pl.BlockSpec((3, 2, ATTN_BLOCK, 2 * ATTN_BLOCK), lambda b, h: (0, h, 0, 0))],
        out_specs=[blk(0), blk(0)],
        out_shape=[jax.ShapeDtypeStruct((B, S, AW), F32)] * 2,
        scratch_shapes=[sub_f, pltpu.VMEM((S, LANES), BF16), pad_b, pad_b] + [sub_f] * 8,
        operands=[qv, qv, qv, bias_all], name="attention_fwd")
    return (res[0].reshape(B * S, AW), res[1].reshape(B * S, AW)) + tuple(res[2:])


def _attention_bwd(qkv, do, lse, dd, bias_all, B, S, AW, bg=None):
    HP = AW // LANES
    H = AW // HEAD_DIM
    G = S // ATTN_BLOCK
    blk3 = (G, ATTN_BLOCK, LANES)
    PAD = ATTN_BLOCK

    def body(refs, bg_hook):
        (q_ref, k_ref, v_ref, do_ref, lse_ref, dd_ref, b_ref,
         dq_ref, dk_ref, dv_ref, csq_ref, csk_ref, csv_ref, db_ref,
         stage, qs, ks, vs, gs, ls, ds_, tq, tk, tv, accq, acck, accv) = refs
        bg_hook(False)
        head0 = lax.broadcasted_iota(jnp.int32, (1, 1, LANES), 2) < HEAD_DIM
        g_idx = lax.broadcasted_iota(jnp.int32, (G, 1, 1), 0)
        first_b = pl.program_id(1) == 0

        @pl.when(first_b)
        def _():
            db_ref[...] = jnp.zeros_like(db_ref)

        ks[0:PAD, :] = jnp.zeros((PAD, LANES), BF16)
        vs[0:PAD, :] = jnp.zeros((PAD, LANES), BF16)
        tk[0:PAD, :] = jnp.zeros((PAD, LANES), F32)
        tv[0:PAD, :] = jnp.zeros((PAD, LANES), F32)
        for bi, (_, d) in enumerate(DILATED_CONFIGS):
            nb = S // d // ATTN_BLOCK
            _regroup(q_ref[0], stage, qs, d, S)
            _regroup(k_ref[0], stage, ks, d, S, PAD)
            _regroup(v_ref[0], stage, vs, d, S, PAD)
            _regroup(do_ref[0], stage, gs, d, S)
            _regroup(lse_ref[0], stage, ls, d, S)
            _regroup(dd_ref[0], stage, ds_, d, S)
            q3 = qs[...].reshape(blk3) * QK_SCALE
            do3 = gs[...].reshape(blk3)
            lse3 = ls[...].reshape(blk3)
            dd3 = ds_[...].reshape(blk3)
            kk3, vv3, dead = _branch_keys(ks, vs, S, nb, g_idx)
            dq = jnp.zeros(blk3, F32)
            dkk = jnp.zeros(kk3.shape, F32)
            dvv = jnp.zeros(kk3.shape, F32)
            for e in range(2):
                msk = head0 if e == 0 else jnp.logical_not(head0)
                c0 = e * HEAD_DIM
                qe = jnp.where(msk, q3, jnp.zeros_like(q3))
                doe = jnp.where(msk, do3, jnp.zeros_like(do3))
                ke = jnp.where(msk, kk3 * QK_SCALE, jnp.zeros_like(kk3))
                s = _branch_scores(qe, kk3, b_ref, bi, e, dead)
                p = jnp.exp(s - lse3[:, :, c0:c0 + 1])
                dp = jnp.einsum("gqe,gke->gqk", doe, vv3, preferred_element_type=F32)
                dsc = p * (dp - dd3[:, :, c0:c0 + 1])
                if dead is None:
                    db_ref[bi, e, :, ATTN_BLOCK:] += jnp.sum(dsc, axis=0)
                else:
                    db_ref[bi, e] += jnp.sum(dsc, axis=0)
                dsb = dsc.astype(BF16)
                dq = dq + jnp.einsum("gqk,gke->gqe", dsb, ke, preferred_element_type=F32)
                dkk = dkk + jnp.einsum("gqk,gqe->gke", dsb, qe, preferred_element_type=F32)
                dvv = dvv + jnp.einsum("gqk,gqe->gke", p.astype(BF16), doe, preferred_element_type=F32)
            tq[...] = dq.reshape(S, LANES)
            if dead is None:
                tk[PAD:PAD + S, :] = dkk.reshape(S, LANES)
                tv[PAD:PAD + S, :] = dvv.reshape(S, LANES)
            else:
                tk[PAD:PAD + S, :] = dkk[:, ATTN_BLOCK:, :].reshape(S, LANES)
                tv[PAD:PAD + S, :] = dvv[:, ATTN_BLOCK:, :].reshape(S, LANES)
                tk[0:S, :] += dkk[:, :ATTN_BLOCK, :].reshape(S, LANES)
                tv[0:S, :] += dvv[:, :ATTN_BLOCK, :].reshape(S, LANES)
            _ungroup(tq, 0, accq, d, S, add=bi > 0)
            _ungroup(tk, PAD, acck, d, S, add=bi > 0)
            _ungroup(tv, PAD, accv, d, S, add=bi > 0)

        for acc, out_ref, cs_ref in ((accq, dq_ref, csq_ref), (acck, dk_ref, csk_ref), (accv, dv_ref, csv_ref)):
            tot = acc[...]
            out_ref[0] = tot.astype(out_ref.dtype)
            _accumulate(cs_ref, first_b, jnp.sum(tot, axis=0, keepdims=True))
        bg_hook(True)

    blk = lambda off: pl.BlockSpec((1, S, LANES), lambda h, b: (b, 0, off + h))
    cs_spec = pl.BlockSpec((1, LANES), lambda h, b: (0, h))
    bias_spec = pl.BlockSpec((3, 2, ATTN_BLOCK, 2 * ATTN_BLOCK), lambda h, b: (0, h, 0, 0))
    qv = qkv.reshape(B, S, 3 * AW)
    view = lambda t: t.reshape(B, S, AW)
    sub_b = pltpu.VMEM((S, LANES), BF16)
    sub_f = pltpu.VMEM((S, LANES), F32)
    pad_b = pltpu.VMEM((S + PAD, LANES), BF16)
    pad_f = pltpu.VMEM((S + PAD, LANES), F32)
    res = _hosted_call(
        body, bg, grid=(HP, B),
        in_specs=[blk(0), blk(HP), blk(2 * HP), blk(0), blk(0), blk(0), bias_spec],
        out_specs=[blk(0), blk(0), blk(0), cs_spec, cs_spec, cs_spec, bias_spec],
        out_shape=[jax.ShapeDtypeStruct((B, S, AW), BF16)] * 3 + [jax.ShapeDtypeStruct((1, AW), F32)] * 3
        + [jax.ShapeDtypeStruct((3, H, ATTN_BLOCK, 2 * ATTN_BLOCK), F32)],
        scratch_shapes=[sub_f, sub_b, pad_b, pad_b, sub_b, sub_f, sub_f, sub_f, pad_f, pad_f, sub_f, sub_f, sub_f],
        operands=[qv, qv, qv, view(do), view(lse), view(dd), bias_all], name="attention_bwd")
    flat = lambda t: t.reshape(B * S, AW)
    return (flat(res[0]), flat(res[1]), flat(res[2]), res[3], res[4], res[5], res[6]) + tuple(res[7:])


def _attn_norm(attn, gain, tm):
    T, AW = attn.shape

    def body(a_ref, g_ref, mix_ref, r_ref):
        a = a_ref[...]
        r = lax.rsqrt(jnp.mean(a * a, axis=-1, keepdims=True) + LN_EPS)
        mix_ref[...] = (a * r * g_ref[...]).astype(BF16)
        r_ref[...] = jnp.broadcast_to(r, (tm, LANES))

    row = pl.BlockSpec((tm, AW), lambda i: (i, 0))
    return pl.pallas_call(
        body, grid=(T // tm,), in_specs=[row, pl.BlockSpec((1, AW), lambda i: (0, 0))],
        out_specs=[row, pl.BlockSpec((tm, LANES), lambda i: (i, 0))],
        out_shape=[jax.ShapeDtypeStruct((T, AW), BF16), jax.ShapeDtypeStruct((T, LANES), F32)],
        compiler_params=_params(1), name="attn_norm",
    )(attn, gain)


def _attn_pre_bwd(dmixed, attn, rstd, gain, tm):
    T, AW = attn.shape
    ones_np = np.kron(np.eye(AW // HEAD_DIM, dtype=np.float32), np.ones((HEAD_DIM, HEAD_DIM), np.float32))
    ones_bd = jnp.asarray(ones_np, dtype=BF16)

    def body(dm_ref, a_ref, r_ref, g_ref, ones_ref, do_ref, dd_ref, dg_ref):
        i = pl.program_id(0)
        dm = dm_ref[...]
        a = a_ref[...]
        r = r_ref[:, 0:1]
        dxn = dm * g_ref[...]
        da = r * (dxn - a * (r * r) * jnp.mean(dxn * a, axis=-1, keepdims=True))
        do_ref[...] = da.astype(BF16)
        hi, lo = _split_hi_lo(da * a)
        dd_ref[...] = (jnp.dot(hi, ones_ref[...], preferred_element_type=F32)
                       + jnp.dot(lo, ones_ref[...], preferred_element_type=F32))
        _accumulate(dg_ref, i == 0, jnp.sum(dm * a * r, axis=0, keepdims=True))

    row = pl.BlockSpec((tm, AW), lambda i: (i, 0))
    vec = pl.BlockSpec((1, AW), lambda i: (0, 0))
    return pl.pallas_call(
        body, grid=(T // tm,),
        in_specs=[row, row, pl.BlockSpec((tm, LANES), lambda i: (i, 0)), vec,
                  pl.BlockSpec((AW, AW), lambda i: (0, 0))],
        out_specs=[row, row, vec],
        out_shape=[jax.ShapeDtypeStruct((T, AW), BF16), jax.ShapeDtypeStruct((T, AW), F32),
                   jax.ShapeDtypeStruct((1, AW), F32)],
        compiler_params=_params(1), name="attn_pre_bwd",
    )(dmixed, attn, rstd, gain, ones_bd)


class _RowShifts:
    def __init__(self, x, row, up):
        self.x, self.row, self.up, self.base = x, row, up, {0: x}

    def __call__(self, s):
        x = self.x
        n, c = x.shape
        r, whole = s % 8, s - s % 8
        if r not in self.base:
            if self.up:
                rolled = pltpu.roll(x, n - r, 0)
                tail = jnp.where(self.row[n - 8:] < n - r, rolled[n - 8:], 0.0)
                self.base[r] = jnp.concatenate([rolled[:n - 8], tail], axis=0)
            else:
                rolled = pltpu.roll(x, r, 0)
                head = jnp.where(self.row[:8] >= r, rolled[:8], 0.0)
                self.base[r] = jnp.concatenate([head, rolled[8:]], axis=0)
        y = self.base[r]
        if whole == 0:
            return y
        pad = jnp.zeros((whole, c), x.dtype)
        if self.up:
            return jnp.concatenate([y[whole:], pad], axis=0)
        return jnp.concatenate([pad, y[:n - whole]], axis=0)


def _conv_branch_fwd_math(a, g, w_ref, cb, lg, lb, row):
    sg = _sigmoid(g)
    u0 = a * sg
    u0_down = _RowShifts(u0, row, up=False)
    uc = jnp.zeros_like(u0) + cb
    for k in range(CONV_KERNEL):
        uc = uc + w_ref[k:k + 1, :] * u0_down(CONV_KERNEL - 1 - k)
    ul, xh, r = _ln_fwd(uc, lg, lb)
    su = _sigmoid(ul)
    u = ul * su
    return sg, u0_down, ul, xh, r, su, u


def _conv_fwd(ag, conv_w, conv_b, ln_g, ln_b, norm_g, B, S, CW):
    def body(a_ref, g_ref, w_ref, cb_ref, lg_ref, lb_ref, ng_ref, o_ref):
        row = lax.broadcasted_iota(jnp.int32, (S, CW), 0)
        _, _, _, _, _, _, u = _conv_branch_fwd_math(a_ref[0], g_ref[0], w_ref, cb_ref[...], lg_ref[...],
                                                    lb_ref[...], row)
        rr = lax.rsqrt(jnp.mean(u * u, axis=-1, keepdims=True) + LN_EPS)
        o_ref[0] = (u * rr * ng_ref[...]).astype(BF16)

    vec = pl.BlockSpec((1, CW), lambda b: (0, 0))
    out = pl.pallas_call(
        body, grid=(B,),
        in_specs=[pl.BlockSpec((1, S, CW), lambda b: (b, 0, 0)), pl.BlockSpec((1, S, CW), lambda b: (b, 0, 1)),
                  pl.BlockSpec((CONV_KERNEL, CW), lambda b: (0, 0)), vec, vec, vec, vec],
        out_specs=pl.BlockSpec((1, S, CW), lambda b: (b, 0, 0)),
        out_shape=jax.ShapeDtypeStruct((B, S, CW), BF16),
        compiler_params=_params(1), name="conv_fwd",
    )(ag.reshape(B, S, 2 * CW), ag.reshape(B, S, 2 * CW), conv_w, conv_b, ln_g, ln_b, norm_g)
    return out.reshape(B * S, CW)


def _conv_bwd(ag, dmc, conv_w, conv_b, ln_g, ln_b, norm_g, B, S, CW):
    def body(a_ref, g_ref, dm_ref, w_ref, cb_ref, lg_ref, lb_ref, ng_ref,
             dag_ref, dw_ref, dcb_ref, dlg_ref, dlb_ref, dng_ref):
        b = pl.program_id(0)
        row = lax.broadcasted_iota(jnp.int32, (S, CW), 0)
        a, g = a_ref[0], g_ref[0]
        sg, u0_down, ul, xh, r, su, u = _conv_branch_fwd_math(a, g, w_ref, cb_ref[...], lg_ref[...], lb_ref[...], row)
        rr = lax.rsqrt(jnp.mean(u * u, axis=-1, keepdims=True) + LN_EPS)
        dm = dm_ref[0]
        dxn = dm * ng_ref[...]
        du = rr * (dxn - u * (rr * rr) * jnp.mean(dxn * u, axis=-1, keepdims=True))
        dul = du * su * (1.0 + ul * (1.0 - su))
        duc = _ln_bwd(dul, xh, r, lg_ref[...])
        first = b == 0
        _accumulate(dng_ref, first, jnp.sum(dm * u * rr, axis=0, keepdims=True))
        _accumulate(dlg_ref, first, jnp.sum(dul * xh, axis=0, keepdims=True))
        _accumulate(dlb_ref, first, jnp.sum(dul, axis=0, keepdims=True))
        _accumulate(dcb_ref, first, jnp.sum(duc, axis=0, keepdims=True))

        @pl.when(first)
        def _():
            dw_ref[...] = jnp.zeros_like(dw_ref)

        duc_up = _RowShifts(duc, row, up=True)
        du0 = jnp.zeros_like(duc)
        for k in range(CONV_KERNEL):
            sh = CONV_KERNEL - 1 - k
            dw_ref[k:k + 1, :] += jnp.sum(duc * u0_down(sh), axis=0, keepdims=True)
            du0 = du0 + w_ref[k:k + 1, :] * duc_up(sh)
        dag_ref[0, :, :CW] = du0 * sg
        dag_ref[0, :, CW:] = du0 * a * sg * (1.0 - sg)

    vec = pl.BlockSpec((1, CW), lambda b: (0, 0))
    wspec = pl.BlockSpec((CONV_KERNEL, CW), lambda b: (0, 0))
    agv = ag.reshape(B, S, 2 * CW)
    res = pl.pallas_call(
        body, grid=(B,),
        in_specs=[pl.BlockSpec((1, S, CW), lambda b: (b, 0, 0)), pl.BlockSpec((1, S, CW), lambda b: (b, 0, 1)),
                  pl.BlockSpec((1, S, CW), lambda b: (b, 0, 0)), wspec, vec, vec, vec, vec],
        out_specs=[pl.BlockSpec((1, S, 2 * CW), lambda b: (b, 0, 0)), wspec, vec, vec, vec, vec],
        out_shape=[jax.ShapeDtypeStruct((B, S, 2 * CW), F32), jax.ShapeDtypeStruct((CONV_KERNEL, CW), F32)]
        + [jax.ShapeDtypeStruct((1, CW), F32)] * 4,
        compiler_params=_params(1), name="conv_bwd",
    )(agv, agv, dmc.reshape(B, S, CW), conv_w, conv_b, ln_g, ln_b, norm_g)
    return (res[0].reshape(B * S, 2 * CW),) + tuple(res[1:])


def _ffn_conv(x, w_ref, bias, row):
    down = x if isinstance(x, _RowShifts) else _RowShifts(x, row, up=False)
    y = jnp.zeros_like(down.x) + bias
    for k in range(FFN_CONV_KERNEL):
        y = y + w_ref[k:k + 1, :] * down(FFN_CONV_KERNEL - 1 - k)
    return y


def _ffn_specs(S, tc, nj, order):
    pick = (lambda b, j: (b, j)) if order == "bj" else (lambda j, b: (b, j))
    act = lambda off: pl.BlockSpec((1, S, tc), lambda *g: (pick(*g)[0], 0, off + pick(*g)[1]))
    cw = lambda off: pl.BlockSpec((FFN_CONV_KERNEL, tc), lambda *g: (0, off + pick(*g)[1]))
    cb = lambda off: pl.BlockSpec((1, tc), lambda *g: (0, off + pick(*g)[1]))
    return act, cw, cb


def _ffn_act(upre, cw, cb, B, S, DFF):
    tc = FFN_COLS
    nj = DFF // tc

    def body(ug_ref, uv_ref, wg_ref, wv_ref, bg_ref, bv_ref, o_ref):
        row = lax.broadcasted_iota(jnp.int32, (S, tc), 0)
        gate = _ffn_conv(ug_ref[0], wg_ref, bg_ref[...], row)
        val = _ffn_conv(uv_ref[0], wv_ref, bv_ref[...], row)
        o_ref[0] = (gate * _sigmoid(gate) * val).astype(BF16)

    act, cws, cbs = _ffn_specs(S, tc, nj, "bj")
    uv = upre.reshape(B, S, 2 * DFF)
    out = pl.pallas_call(
        body, grid=(B, nj), in_specs=[act(0), act(nj), cws(0), cws(nj), cbs(0), cbs(nj)], out_specs=act(0),
        out_shape=jax.ShapeDtypeStruct((B, S, DFF), BF16), compiler_params=_params(2), name="ffn_act",
    )(uv, uv, cw, cw, cb, cb)
    return out.reshape(B * S, DFF)


def _ffn_bwd(upre, dact, cw, cb, B, S, DFF):
    tc = FFN_COLS
    nj = DFF // tc

    def body(ug_ref, uv_ref, da_ref, wg_ref, wv_ref, bg_ref, bv_ref, dug_ref, duv_ref, dwg_ref, dwv_ref,
             dbg_ref, dbv_ref):
        first = pl.program_id(1) == 0
        row = lax.broadcasted_iota(jnp.int32, (S, tc), 0)
        ug, uv = ug_ref[0], uv_ref[0]
        gate = _ffn_conv(ug, wg_ref, bg_ref[...], row)
        val = _ffn_conv(uv, wv_ref, bv_ref[...], row)
        sg = _sigmoid(gate)
        dact_b = da_ref[0]
        dgate = dact_b * val * sg * (1.0 + gate * (1.0 - sg))
        dval = dact_b * gate * sg
        for dup, u, w_ref, du_ref, dw_ref, db_ref in ((dgate, ug, wg_ref, dug_ref, dwg_ref, dbg_ref),
                                                      (dval, uv, wv_ref, duv_ref, dwv_ref, dbv_ref)):
            _accumulate(db_ref, first, jnp.sum(dup, axis=0, keepdims=True))

            @pl.when(first)
            def _(dw_ref=dw_ref):
                dw_ref[...] = jnp.zeros_like(dw_ref)

            dupre = jnp.zeros_like(dup)
            for k in range(FFN_CONV_KERNEL):
                sh = FFN_CONV_KERNEL - 1 - k
                dw_ref[k:k + 1, :] += jnp.sum(dup * _shift_down(u, sh, row), axis=0, keepdims=True)
                dupre = dupre + w_ref[k:k + 1, :] * _shift_up(dup, sh, row)
            du_ref[0] = dupre.astype(BF16)

    act, cws, cbs = _ffn_specs(S, tc, nj, "jb")
    uv = upre.reshape(B, S, 2 * DFF)
    res = pl.pallas_call(
        body, grid=(nj, B),
        in_specs=[act(0), act(nj), act(0), cws(0), cws(nj), cbs(0), cbs(nj)],
        out_specs=[act(0), act(0), cws(0), cws(0), cbs(0), cbs(0)],
        out_shape=[jax.ShapeDtypeStruct((B, S, DFF), BF16)] * 2
        + [jax.ShapeDtypeStruct((FFN_CONV_KERNEL, DFF), F32)] * 2 + [jax.ShapeDtypeStruct((1, DFF), F32)] * 2,
        compiler_params=_params(2), name="ffn_bwd",
    )(uv, uv, dact.reshape(B, S, DFF), cw, cw, cb, cb)
    flat = lambda t: t.reshape(B * S, DFF)
    return (flat(res[0]), flat(res[1]), jnp.concatenate([res[2], res[3]], axis=1),
            jnp.concatenate([res[4], res[5]], axis=1))


FFN_HALO = 16


def _half_sequences(S):
    if S < 8 * FFN_HALO:
        return [(0, S, 0, S)]
    h = S // 2
    return [(0, h + FFN_HALO, 0, h), (h - FFN_HALO, S, FFN_HALO, h)]


def _w_up_block_spec(w_up_sh, tc, off):
    _, D, cs = w_up_sh.shape
    assert cs % tc == 0
    bps = cs // tc
    return pl.BlockSpec((1, D, tc), lambda j: ((off + j) // bps, 0, (off + j) % bps))


def _ffn_fwd_fused(x1b, w_up_sh, cw, cb, B, S, DFF):
    tc = FFN_COLS
    nj = DFF // tc
    D = x1b.shape[1]

    def body(x_ref, wg_ref, wv_ref, cwg_ref, cwv_ref, cbg_ref, cbv_ref, o_ref, up_ref):
        w = jnp.concatenate([wg_ref[0], wv_ref[0]], axis=1)
        for b in range(B):
            for lo, hi, o0, on in _half_sequences(S):
                row = lax.broadcasted_iota(jnp.int32, (hi - lo, tc), 0)
                up = jnp.dot(x_ref[b, lo:hi, :], w, preferred_element_type=F32)
                up_ref[b, lo + o0:lo + o0 + on, :] = up[o0:o0 + on]
                gate = _ffn_conv(up[:, :tc], cwg_ref, cbg_ref[...], row)
                val = _ffn_conv(up[:, tc:], cwv_ref, cbv_ref[...], row)
                o_ref[b, lo + o0:lo + o0 + on, :] = (gate * _sigmoid(gate) * val).astype(BF16)[o0:o0 + on]

    cws = lambda off: pl.BlockSpec((FFN_CONV_KERNEL, tc), lambda j: (0, off + j))
    cbs = lambda off: pl.BlockSpec((1, tc), lambda j: (0, off + j))
    act, upre = pl.pallas_call(
        body, grid=(nj,),
        in_specs=[pl.BlockSpec((B, S, D), lambda j: (0, 0, 0), pipeline_mode=pl.Buffered(1)),
                  _w_up_block_spec(w_up_sh, tc, 0), _w_up_block_spec(w_up_sh, tc, nj),
                  cws(0), cws(nj), cbs(0), cbs(nj)],
        out_specs=[pl.BlockSpec((B, S, tc), lambda j: (0, 0, j)), pl.BlockSpec((B, S, 2 * tc), lambda j: (0, 0, j))],
        out_shape=[jax.ShapeDtypeStruct((B, S, DFF), BF16), jax.ShapeDtypeStruct((B, S, 2 * DFF), F32)],
        compiler_params=_params(1), name="ffn_fwd",
    )(x1b.reshape(B, S, D), w_up_sh, w_up_sh, cw, cw, cb, cb)
    return act.reshape(B * S, DFF), upre


def _ffn_bwd_fused(x1b, dz2b, upre, w_down, cw, cb, B, S, DFF):
    tc = FFN_COLS
    nj = DFF // tc
    D = x1b.shape[1]

    def body(x_ref, dz_ref, up_ref, wd_ref, cwg_ref, cwv_ref, cbg_ref, cbv_ref,
             dug_ref, duv_ref, dwu_ref, dwd_ref, dcw_ref, dcb_ref):
        first = pl.program_id(1) == 0
        dw_t = dwd = None
        dcb = [None, None]
        dcw = [[None] * FFN_CONV_KERNEL, [None] * FFN_CONV_KERNEL]
        add = lambda old, new: new if old is None else old + new
        for lo, hi, o0, on in _half_sequences(S):
            n = hi - lo
            own = slice(o0, o0 + on)
            row = lax.broadcasted_iota(jnp.int32, (n, tc), 0)
            x = x_ref[0, lo:hi, :]
            dz = dz_ref[0, lo:hi, :]
            ug = _RowShifts(up_ref[0, lo:hi, :tc], row, up=False)
            uv = _RowShifts(up_ref[0, lo:hi, tc:], row, up=False)
            gate = _ffn_conv(ug, cwg_ref, cbg_ref[...], row)
            val = _ffn_conv(uv, cwv_ref, cbv_ref[...], row)
            sg = _sigmoid(gate)
            act = (gate * sg * val).astype(BF16)
            dact = _dot(dz, wd_ref[...], "nt")
            dgate = dact * val * sg * (1.0 + gate * (1.0 - sg))
            dval = dact * gate * sg
            dupre = []
            for h, (dup, u_down, w_ref) in enumerate(((dgate, ug, cwg_ref), (dval, uv, cwv_ref))):
                dcb[h] = add(dcb[h], jnp.sum(dup[own], axis=0, keepdims=True))
                dup_up = _RowShifts(dup, row, up=True)
                acc = jnp.zeros_like(dup)
                for k in range(FFN_CONV_KERNEL):
                    sh = FFN_CONV_KERNEL - 1 - k
                    dcw[h][k] = add(dcw[h][k], jnp.sum((dup * u_down(sh))[own], axis=0, keepdims=True))
                    acc = acc + w_ref[k:k + 1, :] * dup_up(sh)
                dupre.append(acc.astype(BF16)[own])
            dug_ref[0, lo + o0:lo + o0 + on, :] = dupre[0]
            duv_ref[0, lo + o0:lo + o0 + on, :] = dupre[1]
            dw_t = add(dw_t, _dot(jnp.concatenate(dupre, axis=1), x[own], "tn"))
            dwd = add(dwd, _dot(act[own], dz[own], "tn"))
        _accumulate(dwu_ref.at[0], first, dw_t[:tc])
        _accumulate(dwu_ref.at[1], first, dw_t[tc:])
        _accumulate(dwd_ref, first, dwd)
        for h in range(2):
            _accumulate(dcb_ref.at[h], first, dcb[h])
            for k in range(FFN_CONV_KERNEL):
                _accumulate(dcw_ref.at[k, pl.ds(h, 1), :], first, dcw[h][k])

    act_s, cws, cbs = _ffn_specs(S, tc, nj, "jb")
    seq = pl.BlockSpec((1, S, D), lambda j, b: (b, 0, 0))
    res = pl.pallas_call(
        body, grid=(nj, B),
        in_specs=[seq, seq, pl.BlockSpec((1, S, 2 * tc), lambda j, b: (b, 0, j)),
                  pl.BlockSpec((tc, D), lambda j, b: (j, 0)), cws(0), cws(nj), cbs(0), cbs(nj)],
        out_specs=[act_s(0), act_s(0), pl.BlockSpec((2, tc, D), lambda j, b: (0, j, 0)),
                   pl.BlockSpec((tc, D), lambda j, b: (j, 0)),
                   pl.BlockSpec((FFN_CONV_KERNEL, 2, tc), lambda j, b: (0, 0, j)),
                   pl.BlockSpec((2, 1, tc), lambda j, b: (0, 0, j))],
        out_shape=[jax.ShapeDtypeStruct((B, S, DFF), BF16)] * 2
        + [jax.ShapeDtypeStruct((2, DFF, D), F32), jax.ShapeDtypeStruct((DFF, D), F32),
           jax.ShapeDtypeStruct((FFN_CONV_KERNEL, 2, DFF), F32), jax.ShapeDtypeStruct((2, 1, DFF), F32)],
        compiler_params=_params(2), name="ffn_bwd",
    )(x1b.reshape(B, S, D), dz2b.reshape(B, S, D), upre, w_down, cw, cw, cb, cb)
    flat = lambda t: t.reshape(B * S, DFF)
    return flat(res[0]), flat(res[1]), res[2], res[3], res[4], res[5]


def _ffn_bwd_seq(b, x1b3, dz2b3, upre, w_up_sh, w_down, cw, cb, prev, S, DFF):
    tc = FFN_COLS
    nj = DFF // tc
    B, _, D = x1b3.shape
    n_prev = 0 if prev is None else 5

    def body(*refs):
        (x_ref, dz_ref, up_ref, wg_ref, wv_ref, wd_ref, cwg_ref, cwv_ref, cbg_ref, cbv_ref) = refs[:10]
        prev_refs = refs[10:10 + n_prev]
        dx_hbm, dwu_ref, dwd_ref, dcw_ref, dcb_ref, acc_ref, sem = refs[10 + n_prev:]
        j = pl.program_id(0)

        @pl.when(j == 0)
        def _():
            acc_ref[...] = jnp.zeros_like(acc_ref)

        wcat = jnp.concatenate([wg_ref[0], wv_ref[0]], axis=1)
        dw_t = dwd = None
        dcb = [None, None]
        dcw = [[None] * FFN_CONV_KERNEL, [None] * FFN_CONV_KERNEL]
        add = lambda old, new: new if old is None else old + new
        for lo, hi, o0, on in _half_sequences(S):
            n = hi - lo
            own = slice(o0, o0 + on)
            row = lax.broadcasted_iota(jnp.int32, (n, tc), 0)
            x = x_ref[0, lo:hi, :]
            dz = dz_ref[0, lo:hi, :]
            ug, uv = up_ref[0, lo:hi, :tc], up_ref[0, lo:hi, tc:]
            gate = _ffn_conv(ug, cwg_ref, cbg_ref[...], row)
            val = _ffn_conv(uv, cwv_ref, cbv_ref[...], row)
            sg = _sigmoid(gate)
            act = (gate * sg * val).astype(BF16)
            dact = _dot(dz, wd_ref[...], "nt")
            dgate = dact * val * sg * (1.0 + gate * (1.0 - sg))
            dval = dact * gate * sg
            dupre = []
            for h, (dup, u, w_ref) in enumerate(((dgate, ug, cwg_ref), (dval, uv, cwv_ref))):
                dcb[h] = add(dcb[h], jnp.sum(dup[own], axis=0, keepdims=True))
                acc = jnp.zeros_like(dup)
                for k in range(FFN_CONV_KERNEL):
                    sh = FFN_CONV_KERNEL - 1 - k
                    dcw[h][k] = add(dcw[h][k], jnp.sum((dup * _shift_down(u, sh, row))[own], axis=0, keepdims=True))
                    acc = acc + w_ref[k:k + 1, :] * _shift_up(dup, sh, row)
                dupre.append(acc.astype(BF16)[own])
            dupre_cat = jnp.concatenate(dupre, axis=1)
            dw_t = add(dw_t, _dot(dupre_cat, x[own], "tn"))
            dwd = add(dwd, _dot(act[own], dz[own], "tn"))
            acc_ref[lo + o0:lo + o0 + on, :] += _dot(dupre_cat, wcat, "nt")
        if n_prev:
            _, pwu_ref, pwd_ref, pcw_ref, pcb_ref = prev_refs
            dwu_ref[0] = pwu_ref[0] + dw_t[:tc]
            dwu_ref[1] = pwu_ref[1] + dw_t[tc:]
            dwd_ref[...] = pwd_ref[...] + dwd
        else:
            dwu_ref[0] = dw_t[:tc]
            dwu_ref[1] = dw_t[tc:]
            dwd_ref[...] = dwd
        for h in range(2):
            dcb_ref[h] = dcb[h] + pcb_ref[h] if n_prev else dcb[h]
            for k in range(FFN_CONV_KERNEL):
                dcw_ref[k, h:h + 1, :] = dcw[h][k] + pcw_ref[k, h:h + 1, :] if n_prev else dcw[h][k]

        @pl.when(j == nj - 1)
        def _():
            out = pltpu.make_async_copy(acc_ref, dx_hbm.at[b], sem)
            out.start()
            out.wait()

    bps = w_up_sh.shape[2] // tc
    wspec = lambda off: pl.BlockSpec((1, D, tc), lambda j: ((off + j) // bps, 0, (off + j) % bps))
    cws = lambda off: pl.BlockSpec((FFN_CONV_KERNEL, tc), lambda j: (0, off + j))
    cbs = lambda off: pl.BlockSpec((1, tc), lambda j: (0, off + j))
    seq = pl.BlockSpec((1, S, D), lambda j: (b, 0, 0), pipeline_mode=pl.Buffered(1))
    part_specs = [pl.BlockSpec((2, tc, D), lambda j: (0, j, 0)), pl.BlockSpec((tc, D), lambda j: (j, 0)),
                  pl.BlockSpec((FFN_CONV_KERNEL, 2, tc), lambda j: (0, 0, j)), pl.BlockSpec((2, 1, tc), lambda j: (0, 0, j))]
    part_shapes = [jax.ShapeDtypeStruct((2, DFF, D), F32), jax.ShapeDtypeStruct((DFF, D), F32),
                   jax.ShapeDtypeStruct((FFN_CONV_KERNEL, 2, DFF), F32), jax.ShapeDtypeStruct((2, 1, DFF), F32)]
    in_specs = [seq, seq, pl.BlockSpec((1, S, 2 * tc), lambda j: (b, 0, j)), wspec(0), wspec(nj),
                pl.BlockSpec((tc, D), lambda j: (j, 0)), cws(0), cws(nj), cbs(0), cbs(nj)]
    operands = [x1b3, dz2b3, upre, w_up_sh, w_up_sh, w_down, cw, cw, cb, cb]
    aliases = {}
    if n_prev:
        in_specs += [HBM_SPEC] + part_specs
        operands += list(prev)
        aliases = {10 + i: i for i in range(5)}
    return pl.pallas_call(
        body, grid=(nj,), in_specs=in_specs, out_specs=[HBM_SPEC] + part_specs,
        out_shape=[jax.ShapeDtypeStruct((B, S, D), F32)] + part_shapes, input_output_aliases=aliases,
        scratch_shapes=[pltpu.VMEM((S, D), F32), pltpu.SemaphoreType.DMA],
        compiler_params=_params(1), name="ffn_bwd_seq%d" % b,
    )(*operands)


def _dx1_ln1_bwd(dupre_g, dupre_v, w_up_sh, dz2, xh1, r1, ln1_g, tm, bg):
    T, D = dz2.shape
    NS, _, cs = w_up_sh.shape
    half = NS // 2
    DFF = dupre_g.shape[1]

    def body(refs, bg_hook):
        dug_ref, duv_ref, w_ref, dz2_ref, xh_ref, r_ref, g_ref, dz_ref, dzb_ref, dg_ref, db_ref = refs
        bg_hook(False)
        first = pl.program_id(0) == 0
        acc = ALPHA * dz2_ref[...]
        for k in range(NS):
            src = dug_ref if k < half else duv_ref
            c0 = (k % half) * cs
            acc = acc + _dot(src[:, c0:c0 + cs], w_ref[k], "nt")
        dx1 = acc
        xh = xh_ref[...]
        dz = _ln_bwd(dx1, xh, r_ref[:, 0:1], g_ref[...])
        dz_ref[...] = dz
        dzb_ref[...] = dz.astype(BF16)
        _accumulate(dg_ref, first, jnp.sum(dx1 * xh, axis=0, keepdims=True))
        _accumulate(db_ref, first, jnp.sum(dx1, axis=0, keepdims=True))
        bg_hook(True)

    row = pl.BlockSpec((tm, D), lambda i: (i, 0))
    vec = pl.BlockSpec((1, D), lambda i: (0, 0))
    du = pl.BlockSpec((tm, DFF), lambda i: (i, 0))
    return _hosted_call(
        body, bg, grid=(T // tm,),
        in_specs=[du, du, pl.BlockSpec((NS, D, cs), lambda i: (0, 0, 0), pipeline_mode=pl.Buffered(1)),
                  row, row, pl.BlockSpec((tm, LANES), lambda i: (i, 0)), vec],
        out_specs=[row, row, vec, vec],
        out_shape=[jax.ShapeDtypeStruct((T, D), F32), jax.ShapeDtypeStruct((T, D), BF16),
                   jax.ShapeDtypeStruct((1, D), F32), jax.ShapeDtypeStruct((1, D), F32)],
        scratch_shapes=[], operands=[dupre_g, dupre_v, w_up_sh, dz2, xh1, r1, ln1_g], name="mm_dx1_ln1_bwd")


def _transpose(x, name):
    R, C = x.shape
    tr = LANES if R % LANES == 0 else R

    def body(x_ref, o_ref):
        o_ref[...] = x_ref[...].T

    return pl.pallas_call(
        body, grid=(R // tr,), in_specs=[pl.BlockSpec((tr, C), lambda i: (i, 0))],
        out_specs=pl.BlockSpec((C, tr), lambda i: (0, i)), out_shape=jax.ShapeDtypeStruct((C, R), F32),
        compiler_params=_params(1), name=name)(x)


def _dh_cat(dq, dk, dv, dag, tm):
    T, AW = dq.shape
    CW2 = dag.shape[1]
    W = 3 * AW + CW2

    def body(dq_ref, dk_ref, dv_ref, dag_ref, dh_ref, cs_ref):
        for c, ref in enumerate((dq_ref, dk_ref, dv_ref)):
            dh_ref[:, c * AW:(c + 1) * AW] = ref[...]
        dg = dag_ref[...]
        dh_ref[:, 3 * AW:] = dg.astype(BF16)
        _accumulate(cs_ref, pl.program_id(0) == 0, jnp.sum(dg, axis=0, keepdims=True))

    row = pl.BlockSpec((tm, AW), lambda i: (i, 0))
    return pl.pallas_call(
        body, grid=(T // tm,),
        in_specs=[row] * 3 + [pl.BlockSpec((tm, CW2), lambda i: (i, 0))],
        out_specs=[pl.BlockSpec((tm, W), lambda i: (i, 0)), pl.BlockSpec((1, CW2), lambda i: (0, 0))],
        out_shape=[jax.ShapeDtypeStruct((T, W), BF16), jax.ShapeDtypeStruct((1, CW2), F32)],
        compiler_params=_params(1), name="dh_cat",
    )(dq, dk, dv, dag)


def _local_step(x, target, rel_table, w_in_t, b_in, conv_w, conv_b, conv_ln_g, conv_ln_b, attn_norm_g,
                conv_norm_g, staged, ln1_g, ln1_b, ffn_cw, ffn_cb, ln2_g, ln2_b, ids):
    B, S, D = x.shape
    T = B * S
    AW = attn_norm_g.shape[-1]
    CW = conv_norm_g.shape[-1]
    H = AW // HEAD_DIM
    DFF = staged[2].shape[0] * staged[2].shape[1]
    INW = 3 * AW + 2 * CW
    xf = x.reshape(T, D)
    tf = target.reshape(T, D)
    tm = _row_tile(T, 512)
    tm_s = _row_tile(T, 256)

    bucket_np, mask_np = _bucket_tables()
    bucket = jnp.asarray(bucket_np)
    band_mask = jnp.asarray(mask_np)
    bias_all = _bias_build(rel_table.T, bucket, band_mask).reshape(3, H, ATTN_BLOCK, 2 * ATTN_BLOCK)

    def in_proj(n0, n, tn, out_dtype, name):
        assert n0 % tn == 0 and n % tn == 0

        def epilogue(acc, i, j, extra_refs, out_refs):
            out_refs[0][...] = (acc + extra_refs[0][...]).astype(out_dtype)

        return _matmul_general(
            [(xf, (tm, D), lambda i, j, k: (i, 0)), (w_in_t, (tn, D), lambda i, j, k: (n0 // tn + j, 0))],
            lambda refs, i, j, k: _dot(refs[0][...], refs[1][...], "nt"),
            grid=(T // tm, n // tn, 1), tm=tm, tn=tn, extras=[(b_in, (1, tn), lambda i, j, k: (0, n0 // tn + j))],
            outs=[_plain_out(T, n, tm, tn, out_dtype)], epilogue=epilogue, name=name)[0]

    qkv = in_proj(0, 3 * AW, _col_tile(3 * AW, 1152), BF16, "mm_qkv")
    ag = in_proj(3 * AW, 2 * CW, math.gcd(3 * AW, 2 * CW), F32, "mm_ag")

    attn, lse, w_out_g, w_up_sh, w_down_g = _attention_fwd(qkv, bias_all, B, S, AW, bg=_bg_gather(staged))
    w_out = w_out_g.reshape(D, D)
    w_down = w_down_g.reshape(DFF, D)
    mixed_c = _conv_fwd(ag, conv_w, conv_b, conv_ln_g, conv_ln_b, conv_norm_g, B, S, CW)

    def attn_rstd(a):
        return lax.rsqrt(jnp.mean(a * a, axis=-1, keepdims=True) + LN_EPS)

    def mixed_rows(attn_ref, mc_ref, gain_ref):
        a = attn_ref[...]
        return jnp.concatenate([(a * attn_rstd(a) * gain_ref[...]).astype(BF16), mc_ref[...]], axis=1)

    def ln1_epilogue(acc, i, j, extra_refs, out_refs):
        x_ref, g_ref, b_ref, a_ref = extra_refs
        x1, xh, r = _ln_fwd(acc + ALPHA * x_ref[...], g_ref[...], b_ref[...])
        out_refs[0][...] = x1
        out_refs[1][...] = x1.astype(BF16)
        out_refs[2][...] = xh
        out_refs[3][...] = jnp.broadcast_to(r, (tm_s, LANES))
        out_refs[4][...] = jnp.broadcast_to(attn_rstd(a_ref[...]), (tm_s, LANES))

    rowD = lambda i, j, k: (i, 0)
    vecD = lambda i, j, k: (0, 0)
    x1, x1b, xh1, r1, r_attn = _matmul_general(
        [(attn, (tm_s, AW), rowD), (mixed_c, (tm_s, CW), rowD), (attn_norm_g, (1, AW), vecD), (w_out, (D, D), vecD)],
        lambda refs, i, j, k: _dot(mixed_rows(refs[0], refs[1], refs[2]), refs[3][...], "nn"),
        grid=(T // tm_s, 1, 1), tm=tm_s, tn=D,
        extras=[(xf, (tm_s, D), rowD), (ln1_g, (1, D), vecD), (ln1_b, (1, D), vecD), (attn, (tm_s, AW), rowD)],
        outs=[((T, D), F32, (tm_s, D), rowD), ((T, D), BF16, (tm_s, D), rowD), ((T, D), F32, (tm_s, D), rowD),
              ((T, LANES), F32, (tm_s, LANES), rowD), ((T, LANES), F32, (tm_s, LANES), rowD)],
        epilogue=ln1_epilogue, name="mm_out_ln1")

    NS, _, cs = w_up_sh.shape
    half = NS // 2

    act, upre = _ffn_fwd_fused(x1b, w_up_sh, ffn_cw, ffn_cb, B, S, DFF)

    def ln2_epilogue(acc, i, j, extra_refs, out_refs):
        x1_ref, g_ref, b_ref, t_ref = extra_refs
        dz_ref, dzb_ref, loss_ref, dg_ref, db_ref = out_refs
        g = g_ref[...]
        y, xh, r = _ln_fwd(acc + ALPHA * x1_ref[...], g, b_ref[...])
        diff = y - t_ref[...]
        row_loss = jnp.sum(diff * diff, axis=1, keepdims=True)
        tile_loss = jnp.sum(row_loss, axis=0, keepdims=True) * (0.5 / D)
        dy = diff * (1.0 / D)
        dz = _ln_bwd(dy, xh, r, g)
        dz_ref[...] = dz
        dzb_ref[...] = dz.astype(BF16)
        first = i == 0
        _accumulate(loss_ref, first, jnp.broadcast_to(tile_loss, (1, LANES)))
        _accumulate(dg_ref, first, jnp.sum(dy * xh, axis=0, keepdims=True))
        _accumulate(db_ref, first, jnp.sum(dy, axis=0, keepdims=True))

    dz2, dz2b, loss_part, d_ln2_g, d_ln2_b = _matmul(
        act, w_down, mode="nn", tm=tm, tn=D, tk=DFF,
        extras=[(x1, (tm, D), rowD), (ln2_g, (1, D), vecD), (ln2_b, (1, D), vecD), (tf, (tm, D), rowD)],
        outs=[((T, D), F32, (tm, D), rowD), ((T, D), BF16, (tm, D), rowD),
              ((1, LANES), F32, (1, LANES), vecD), ((1, D), F32, (1, D), vecD), ((1, D), F32, (1, D), vecD)],
        epilogue=ln2_epilogue, name="mm_down_ln2_loss")

    dupre_g, dupre_v, d_w_up_t, d_w_down, d_ffn_cw2, d_ffn_cb2 = _ffn_bwd_fused(
        x1b, dz2b, upre, w_down, ffn_cw, ffn_cb, B, S, DFF)
    d_w_up_t = d_w_up_t.reshape(NS, cs, D)
    d_ffn_cw = d_ffn_cw2.reshape(FFN_CONV_KERNEL, 2 * DFF)
    d_ffn_cb = d_ffn_cb2.reshape(1, 2 * DFF)
    tk_t = _row_tile(T, 512)

    early = [d_w_up_t, d_w_down.reshape(NS, DFF // NS, D)]
    dz1, dz1b, d_ln1_g, d_ln1_b, *sib_e = _dx1_ln1_bwd(dupre_g, dupre_v, w_up_sh, dz2, xh1, r1, ln1_g, tm,
                                                       bg=_bg_sibling_exchange(early))
    chip_e = [_pair_sum(g, s, ids, name="pair_sum_" + n) for g, s, n in zip(early, sib_e, ("w_up", "w_down"))]

    def dw_out_epilogue(acc, i, j, extra_refs, out_refs):
        out_refs[0][...] = acc

    d_w_out = _matmul_general(
        [(attn, (tk_t, AW), lambda i, j, k: (k, 0)), (mixed_c, (tk_t, CW), lambda i, j, k: (k, 0)),
         (attn_norm_g, (1, AW), vecD), (dz1b, (tk_t, D), lambda i, j, k: (k, 0))],
        lambda refs, i, j, k: _dot(mixed_rows(refs[0], refs[1], refs[2]), refs[3][...], "tn"),
        grid=(1, 1, T // tk_t), tm=D, tn=D, outs=[_plain_out(D, D, D, D, F32)],
        epilogue=dw_out_epilogue, name="mm_dw_out")[0]
    early.append(d_w_out.reshape(NS, D // NS, D))
    ones_bd = jnp.asarray(np.kron(np.eye(H, dtype=np.float32), np.ones((HEAD_DIM, HEAD_DIM), np.float32)), dtype=BF16)

    def dmixed_epilogue(acc, i, j, extra_refs, out_refs):
        a_ref, r_ref, g_ref, ones_ref = extra_refs
        do_ref, dd_ref, dmc_ref, dg_ref = out_refs
        dm = acc[:, :AW]
        dmc_ref[...] = acc[:, AW:]
        a = a_ref[...]
        r = r_ref[:, 0:1]
        dxn = dm * g_ref[...]
        da = r * (dxn - a * (r * r) * jnp.mean(dxn * a, axis=-1, keepdims=True))
        do_ref[...] = da.astype(BF16)
        hi, lo = _split_hi_lo(da * a)
        dd_ref[...] = (jnp.dot(hi, ones_ref[...], preferred_element_type=F32)
                       + jnp.dot(lo, ones_ref[...], preferred_element_type=F32))
        _accumulate(dg_ref, i == 0, jnp.sum(dm * a * r, axis=0, keepdims=True))

    dattn, dd, dmc, d_attn_norm_g, sib_out = _matmul(
        dz1b, w_out, mode="nt", tm=tm, tn=D, tk=D,
        extras=[(attn, (tm, AW), rowD), (r_attn, (tm, LANES), rowD), (attn_norm_g, (1, AW), vecD),
                (ones_bd, (AW, AW), vecD)],
        outs=[((T, AW), BF16, (tm, AW), rowD), ((T, AW), F32, (tm, AW), rowD), ((T, CW), F32, (tm, CW), rowD),
              ((1, AW), F32, (1, AW), vecD)],
        epilogue=dmixed_epilogue, name="mm_dmixed", bg=_bg_sibling_exchange(early[2:]))
    sib_e.append(sib_out)
    chip_e.append(_pair_sum(early[2], sib_out, ids, name="pair_sum_w_out"))

    dag, d_conv_w, d_conv_b, d_conv_ln_g, d_conv_ln_b, d_conv_norm_g = _conv_bwd(
        ag, dmc, conv_w, conv_b, conv_ln_g, conv_ln_b, conv_norm_g, B, S, CW)

    dq, dk, dv, csq, csk, csv, dbias, *got_e = _attention_bwd(qkv, dattn, lse, dd, bias_all, B, S, AW,
                                                              bg=_bg_chip_exchange(chip_e))
    full_up, full_down, full_out = [_final_sum(g, s, r, ids, name="final_sum_" + n)
                                    for g, s, r, n in zip(early, sib_e, got_e, ("w_up", "w_down", "w_out"))]
    d_rel_table = _rel_grad(dbias.reshape(3, H, ATTN_BLOCK * 2 * ATTN_BLOCK), bucket).T
    dh, cs_ag = _dh_cat(dq, dk, dv, dag, tm_s)
    d_b_in = jnp.concatenate([csq, csk, csv, cs_ag], axis=1)

    d_w_in_t = _mm_plain(dh, xf, mode="tn", tm=_col_tile(INW, 1408), tn=D, tk=tk_t, out_dtype=F32, name="mm_dw_in")
    late = [d_w_in_t.reshape(NS, INW // NS, D)]
    sib_l = _sibling_exchange(late)
    chip_l = [_pair_sum(late[0], sib_l[0], ids, name="pair_sum_w_in")]
    small = dict(rel_table=d_rel_table, b_in=d_b_in, conv_w=d_conv_w, conv_b=d_conv_b, conv_ln_g=d_conv_ln_g,
                 conv_ln_b=d_conv_ln_b, attn_norm_g=d_attn_norm_g, conv_norm_g=d_conv_norm_g, ln1_g=d_ln1_g,
                 ln1_b=d_ln1_b, ffn_conv_w=d_ffn_cw, ffn_conv_b=d_ffn_cb, ln2_g=d_ln2_g, ln2_b=d_ln2_b)
    pack = _pack([loss_part] + [small[n] for n in SMALL_NAMES])

    def gx_epilogue(acc, i, j, extra_refs, out_refs):
        out_refs[0][...] = acc + ALPHA * extra_refs[0][...]

    grad_x, got_in, all_packs = _matmul(
        dh, w_in_t, mode="nn", tm=tm, tn=D, tk=INW, extras=[(dz1, (tm, D), rowD)],
        outs=[((T, D), F32, (tm, D), rowD)], epilogue=gx_epilogue, name="mm_grad_x",
        bg=_bg_chip_exchange(chip_l, pack))
    full_in = _final_sum(late[0], sib_l[0], got_in, ids, name="final_sum_w_in")
    return grad_x.reshape(B, S, D), [full_in, full_out, full_up, full_down], all_packs


def _place():
    return lax.axis_index("x"), lax.axis_index("y"), lax.axis_index("c")


CHIP_FLIPS = ((1, 0), (0, 1), (1, 1))


def _flip(v, f):
    return 1 - v if f else v


HBM_SPEC = pl.BlockSpec(memory_space=pl.ANY)
VMEM_SPEC = pl.BlockSpec(memory_space=pltpu.VMEM)
COMM_PARAMS = pltpu.CompilerParams(vmem_limit_bytes=VMEM_LIMIT)


def _gather_weights(big, small):
    nb, ns = len(big), len(small)

    def body(*refs):
        big_in = refs[:nb]
        small_in = refs[nb:nb + ns]
        big_out = refs[nb + ns:2 * nb + ns]
        small_out = refs[2 * nb + ns:2 * nb + 2 * ns]
        stages = refs[2 * nb + 2 * ns:3 * nb + 2 * ns]
        send_sems, recv_sems, local_sems = refs[3 * nb + 2 * ns:]
        x, y, c = _place()
        s_me = 2 * x + y
        sibling = (x, y, 1 - c)
        started, local_copies = [], []
        for a in range(nb):
            rh = big[a].shape[0] // 2
            lo = pl.multiple_of(c * rh, 16)
            stages[a][...] = big_in[a][pl.ds(lo, rh), :].astype(BF16)
            mine = big_out[a].at[s_me, pl.ds(lo, rh), :]
            loc = pltpu.make_async_copy(stages[a], mine, local_sems.at[a])
            loc.start()
            local_copies.append(loc)
            targets = [sibling] + [(_flip(x, fx), _flip(y, fy), c) for fx, fy in CHIP_FLIPS]
            for k, to in enumerate(targets):
                cp = pltpu.make_async_remote_copy(stages[a], mine, send_sems.at[a * 7 + k],
                                                  recv_sems.at[a * 7 + k], device_id=to, device_id_type=MESH)
                cp.start()
                started.append(cp)
        for a in range(ns):
            mine = small_out[a].at[s_me]
            loc = pltpu.make_async_copy(small_in[a], mine, local_sems.at[nb + a])
            loc.start()
            local_copies.append(loc)
            for k, (fx, fy) in enumerate(CHIP_FLIPS):
                cp = pltpu.make_async_remote_copy(small_in[a], mine, send_sems.at[nb * 7 + a * 3 + k],
                                                  recv_sems.at[nb * 7 + a * 3 + k],
                                                  device_id=(_flip(x, fx), _flip(y, fy), c), device_id_type=MESH)
                cp.start()
                started.append(cp)
        for a in range(nb):
            rh = big[a].shape[0] // 2
            lo = pl.multiple_of(c * rh, 16)
            for k, (fx, fy) in enumerate(CHIP_FLIPS):
                s_from = 2 * _flip(x, fx) + _flip(y, fy)
                got = big_out[a].at[s_from, pl.ds(lo, rh), :]
                pltpu.make_async_remote_copy(got, got, send_sems.at[a * 7 + 1 + k], recv_sems.at[a * 7 + 1 + k],
                                             device_id=sibling, device_id_type=MESH).wait_recv()
                fwd = pltpu.make_async_remote_copy(got, got, send_sems.at[a * 7 + 4 + k],
                                                   recv_sems.at[a * 7 + 4 + k], device_id=sibling,
                                                   device_id_type=MESH)
                fwd.start()
                started.append(fwd)
        for a in range(nb):
            rh = big[a].shape[0] // 2
            lo_sib = pl.multiple_of((1 - c) * rh, 16)
            for k in (0, 4, 5, 6):
                any_rows = big_out[a].at[s_me, pl.ds(lo_sib, rh), :]
                pltpu.make_async_remote_copy(any_rows, any_rows, send_sems.at[a * 7 + k], recv_sems.at[a * 7 + k],
                                             device_id=sibling, device_id_type=MESH).wait_recv()
        for a in range(ns):
            for k in range(3):
                pltpu.make_async_remote_copy(small_in[a], small_out[a].at[s_me], send_sems.at[nb * 7 + a * 3 + k],
                                             recv_sems.at[nb * 7 + a * 3 + k], device_id=sibling,
                                             device_id_type=MESH).wait_recv()
        for cp in started:
            cp.wait_send()
        for cp in local_copies:
            cp.wait()

    n_sem = nb * 7 + ns * 3
    out_shape = ([jax.ShapeDtypeStruct((N_SHARDS,) + w.shape, BF16) for w in big]
                 + [jax.ShapeDtypeStruct((N_SHARDS,) + w.shape, F32) for w in small])
    res = pl.pallas_call(
        body, in_specs=[VMEM_SPEC] * nb + [HBM_SPEC] * ns, out_specs=[HBM_SPEC] * (nb + ns),
        out_shape=out_shape,
        scratch_shapes=[pltpu.VMEM((w.shape[0] // 2, w.shape[1]), BF16) for w in big]
        + [pltpu.SemaphoreType.DMA((n_sem,)), pltpu.SemaphoreType.DMA((n_sem,)),
           pltpu.SemaphoreType.DMA((nb + ns,))],
        compiler_params=COMM_PARAMS, name="gather_weights",
    )(*big, *small)
    return res[:nb], res[nb:]


def _sibling_exchange(grads):
    n = len(grads)

    def body(*refs):
        g_in = refs[:n]
        got = refs[n:2 * n]
        send_sems, recv_sems = refs[2 * n:]
        x, y, c = _place()
        cps = []
        for a in range(n):
            rh = grads[a].shape[1] // 2
            lo = pl.multiple_of((1 - c) * rh, 8)
            cp = pltpu.make_async_remote_copy(g_in[a].at[:, pl.ds(lo, rh), :], got[a], send_sems.at[a],
                                              recv_sems.at[a], device_id=(x, y, 1 - c), device_id_type=MESH)
            cp.start()
            cps.append(cp)
        for cp in cps:
            cp.wait()

    return pl.pallas_call(
        body, in_specs=[HBM_SPEC] * n, out_specs=[HBM_SPEC] * n,
        out_shape=[jax.ShapeDtypeStruct((N_SHARDS, g.shape[1] // 2, g.shape[2]), F32) for g in grads],
        scratch_shapes=[pltpu.SemaphoreType.DMA((n,)), pltpu.SemaphoreType.DMA((n,))],
        compiler_params=COMM_PARAMS, name="sibling_exchange",
    )(*grads)


def _chip_exchange(chip_parts, pack):
    n = len(chip_parts)

    def body(*refs):
        parts = refs[:n]
        pack_ref = refs[n]
        got = refs[n + 1:2 * n + 1]
        all_packs = refs[2 * n + 1]
        send_sems, recv_sems, local_sem = refs[2 * n + 2:]
        x, y, c = _place()
        me = 4 * x + 2 * y + c
        cps = []
        for a in range(n):
            for k, (fx, fy) in enumerate(CHIP_FLIPS):
                px, py = _flip(x, fx), _flip(y, fy)
                cp = pltpu.make_async_remote_copy(parts[a].at[2 * px + py], got[a].at[k], send_sems.at[a * 3 + k],
                                                  recv_sems.at[a * 3 + k], device_id=(px, py, c),
                                                  device_id_type=MESH)
                cp.start()
                cps.append(cp)
        loc = pltpu.make_async_copy(pack_ref, all_packs.at[me], local_sem)
        loc.start()
        for m in range(1, N_DEV):
            to = (_flip(x, m & 4), _flip(y, m & 2), _flip(c, m & 1))
            cp = pltpu.make_async_remote_copy(pack_ref, all_packs.at[me], send_sems.at[n * 3 + m - 1],
                                              recv_sems.at[n * 3 + m - 1], device_id=to, device_id_type=MESH)
            cp.start()
            cps.append(cp)
        for cp in cps:
            cp.wait()
        loc.wait()

    rs = pack.shape[0]
    res = pl.pallas_call(
        body, in_specs=[HBM_SPEC] * (n + 1), out_specs=[HBM_SPEC] * (n + 1),
        out_shape=[jax.ShapeDtypeStruct((3,) + p.shape[1:], BF16) for p in chip_parts]
        + [jax.ShapeDtypeStruct((N_DEV, rs, LANES), F32)],
        scratch_shapes=[pltpu.SemaphoreType.DMA((n * 3 + N_DEV - 1,)), pltpu.SemaphoreType.DMA((n * 3 + N_DEV - 1,)),
                        pltpu.SemaphoreType.DMA],
        compiler_params=COMM_PARAMS, name="chip_exchange",
    )(*chip_parts, pack)
    return res[:n], res[n]


def _sibling_assemble(fulls):
    n = len(fulls)

    def body(*refs):
        full = refs[n:2 * n]
        send_sems, recv_sems = refs[2 * n:]
        x, y, c = _place()
        cps = []
        for a in range(n):
            rh = fulls[a].shape[0] // 2
            mine = full[a].at[pl.ds(pl.multiple_of(c * rh, 8), rh), :]
            cp = pltpu.make_async_remote_copy(mine, mine, send_sems.at[a], recv_sems.at[a],
                                              device_id=(x, y, 1 - c), device_id_type=MESH)
            cp.start()
            cps.append(cp)
        for cp in cps:
            cp.wait()

    return pl.pallas_call(
        body, in_specs=[HBM_SPEC] * n, out_specs=[HBM_SPEC] * n,
        out_shape=[jax.ShapeDtypeStruct(f.shape, F32) for f in fulls],
        input_output_aliases={a: a for a in range(n)},
        scratch_shapes=[pltpu.SemaphoreType.DMA((n,)), pltpu.SemaphoreType.DMA((n,))],
        compiler_params=COMM_PARAMS, name="sibling_assemble",
    )(*fulls)


def _remote(ref_src, ref_dst, send_sems, recv_sems, k, to):
    return pltpu.make_async_remote_copy(ref_src, ref_dst, send_sems.at[k], recv_sems.at[k], device_id=to,
                                        device_id_type=MESH)


def _stage_half(w, ids, name):
    R, C = w.shape
    rh = R // 2
    rt = _half_tile(rh)
    nt = rh // rt

    def body(ids_ref, w_ref, o_ref):
        o_ref[0] = w_ref[...].astype(BF16)

    grid_spec = pltpu.PrefetchScalarGridSpec(
        num_scalar_prefetch=1, grid=(nt,),
        in_specs=[pl.BlockSpec((rt, C), lambda i, ids: (ids[2] * nt + i, 0))],
        out_specs=pl.BlockSpec((1, rt, C), lambda i, ids: (2 * ids[0] + ids[1], ids[2] * nt + i, 0)))
    return pl.pallas_call(body, grid_spec=grid_spec, out_shape=jax.ShapeDtypeStruct((N_SHARDS, R, C), BF16),
                          compiler_params=_params(1), name=name)(ids, w)


def _bg_gather(staged):
    n = len(staged)

    def run(step, n_steps, ins, outs, send_sems, recv_sems, local_sems, post):
        x, y, c = _place()
        s_me = 2 * x + y
        sibling = (x, y, 1 - c)
        chips = [(_flip(x, fx), _flip(y, fy)) for fx, fy in CHIP_FLIPS]

        def rows(a, s, half):
            rh = staged[a].shape[1] // 2
            return outs[a].at[s, pl.ds(pl.multiple_of(half * rh, 16), rh), :]

        def copy(a, k, ref, to):
            return _remote(ref, ref, send_sems, recv_sems, a * 7 + k, to)

        if not post:
            @pl.when(step == 0)
            def _():
                for a in range(n):
                    mine = rows(a, s_me, c)
                    copy(a, 0, mine, sibling).start()
                    for k, (px, py) in enumerate(chips):
                        copy(a, 1 + k, mine, (px, py, c)).start()

            @pl.when(step == max(n_steps - 2, 0))
            def _():
                for a in range(n):
                    for k, (px, py) in enumerate(chips):
                        got = rows(a, 2 * px + py, c)
                        copy(a, 1 + k, got, sibling).wait_recv()
                        copy(a, 4 + k, got, sibling).start()
        else:
            @pl.when(step == n_steps - 1)
            def _():
                for a in range(n):
                    for k in (0, 4, 5, 6):
                        copy(a, k, rows(a, s_me, 1 - c), sibling).wait_recv()
                    for k in range(7):
                        copy(a, k, rows(a, s_me, c), sibling).wait_send()

    return _Background(staged, [jax.ShapeDtypeStruct(g.shape, g.dtype) for g in staged],
                       {a: a for a in range(n)}, 7 * n, run)


def _bg_sibling_exchange(grads):
    n = len(grads)

    def run(step, n_steps, ins, outs, send_sems, recv_sems, local_sems, post):
        x, y, c = _place()

        def copy(a):
            rh = grads[a].shape[1] // 2
            lo = pl.multiple_of((1 - c) * rh, 8)
            return _remote(ins[a].at[:, pl.ds(lo, rh), :], outs[a], send_sems, recv_sems, a, (x, y, 1 - c))

        if not post:
            @pl.when(step == 0)
            def _():
                for a in range(n):
                    copy(a).start()
        else:
            @pl.when(step == n_steps - 1)
            def _():
                for a in range(n):
                    copy(a).wait()

    return _Background(grads, [jax.ShapeDtypeStruct((N_SHARDS, g.shape[1] // 2, g.shape[2]), F32) for g in grads],
                       {}, n, run)


def _bg_chip_exchange(chip_parts, pack=None):
    n = len(chip_parts)

    def run(step, n_steps, ins, outs, send_sems, recv_sems, local_sems, post):
        x, y, c = _place()
        me = 4 * x + 2 * y + c

        def copies():
            cps = []
            for a in range(n):
                for k, (fx, fy) in enumerate(CHIP_FLIPS):
                    px, py = _flip(x, fx), _flip(y, fy)
                    cps.append(_remote(ins[a].at[2 * px + py], outs[a].at[k], send_sems, recv_sems, a * 3 + k,
                                       (px, py, c)))
            if pack is not None:
                for m in range(1, N_DEV):
                    to = (_flip(x, m & 4), _flip(y, m & 2), _flip(c, m & 1))
                    cps.append(_remote(ins[n], outs[n].at[me], send_sems, recv_sems, n * 3 + m - 1, to))
            return cps

        def local():
            return pltpu.make_async_copy(ins[n], outs[n].at[me], local_sems.at[0])

        if not post:
            @pl.when(step == 0)
            def _():
                for cp in copies():
                    cp.start()
                if pack is not None:
                    local().start()
        else:
            @pl.when(step == n_steps - 1)
            def _():
                for cp in copies():
                    cp.wait()
                if pack is not None:
                    local().wait()

    in_arrays = list(chip_parts) + ([pack] if pack is not None else [])
    out_shapes = [jax.ShapeDtypeStruct((3,) + p.shape[1:], BF16) for p in chip_parts]
    if pack is not None:
        out_shapes.append(jax.ShapeDtypeStruct((N_DEV, pack.shape[0], LANES), F32))
    return _Background(in_arrays, out_shapes, {}, n * 3 + N_DEV - 1, run)


def _half_tile(rh, mult=16, want=256):
    best = None
    for t in range(mult, min(rh, want) + 1, mult):
        if rh % t == 0:
            best = t
    return best if best is not None else rh


def _pair_sum(g, sib, ids, name):
    _, R, C = g.shape
    rh = R // 2
    rt = _half_tile(rh)
    nt = rh // rt

    def body(ids_ref, g_ref, s_ref, o_ref):
        o_ref[...] = (g_ref[...] + s_ref[...]).astype(BF16)

    grid_spec = pltpu.PrefetchScalarGridSpec(
        num_scalar_prefetch=1, grid=(N_SHARDS, nt),
        in_specs=[pl.BlockSpec((1, rt, C), lambda s, i, ids: (s, ids[2] * nt + i, 0)),
                  pl.BlockSpec((1, rt, C), lambda s, i, ids: (s, i, 0))],
        out_specs=pl.BlockSpec((1, rt, C), lambda s, i, ids: (s, i, 0)))
    return pl.pallas_call(body, grid_spec=grid_spec, out_shape=jax.ShapeDtypeStruct((N_SHARDS, rh, C), BF16),
                          compiler_params=_params(2), name=name)(ids, g, sib)


def _final_sum(g, sib, got, ids, name):
    _, R, C = g.shape
    rh = R // 2
    rt = _half_tile(rh)
    nt = rh // rt

    def body(ids_ref, g_ref, s_ref, r_ref, o_ref):
        tot = g_ref[0] + s_ref[0]
        for k in range(3):
            tot = tot + r_ref[k].astype(F32)
        o_ref[...] = tot

    grid_spec = pltpu.PrefetchScalarGridSpec(
        num_scalar_prefetch=1, grid=(nt,),
        in_specs=[pl.BlockSpec((1, rt, C), lambda i, ids: (2 * ids[0] + ids[1], ids[2] * nt + i, 0)),
                  pl.BlockSpec((1, rt, C), lambda i, ids: (2 * ids[0] + ids[1], i, 0)),
                  pl.BlockSpec((3, rt, C), lambda i, ids: (0, i, 0))],
        out_specs=pl.BlockSpec((rt, C), lambda i, ids: (ids[2] * nt + i, 0)))
    return pl.pallas_call(body, grid_spec=grid_spec, out_shape=jax.ShapeDtypeStruct((R, C), F32),
                          compiler_params=_params(1), name=name)(ids, g, sib, got)


def _sum_packs(all_packs):
    def body(p_ref, o_ref):
        tot = p_ref[0]
        for i in range(1, N_DEV):
            tot = tot + p_ref[i]
        o_ref[...] = tot

    return pl.pallas_call(body, in_specs=[VMEM_SPEC], out_specs=VMEM_SPEC,
                          out_shape=jax.ShapeDtypeStruct(all_packs.shape[1:], F32), name="sum_packs")(all_packs)


def _adamw(w, g, m, v, name):
    R, C = w.shape
    rt = _half_tile(R, mult=8, want=256)

    def body(w_ref, g_ref, m_ref, v_ref, d_ref, nm_ref, nv_ref):
        gg = g_ref[...]
        nm = ADAM_B1 * m_ref[...] + (1.0 - ADAM_B1) * gg
        nv = ADAM_B2 * v_ref[...] + (1.0 - ADAM_B2) * (gg * gg)
        m_hat = nm / (1.0 - ADAM_B1 ** ADAM_STEP)
        v_hat = nv / (1.0 - ADAM_B2 ** ADAM_STEP)
        d_ref[...] = -ADAM_LR * (m_hat / (jnp.sqrt(v_hat) + ADAM_EPS) + ADAM_WD * w_ref[...])
        nm_ref[...] = nm
        nv_ref[...] = nv

    spec = pl.BlockSpec((rt, C), lambda i: (i, 0))
    return pl.pallas_call(body, grid=(R // rt,), in_specs=[spec] * 4, out_specs=[spec] * 3,
                          out_shape=[jax.ShapeDtypeStruct((R, C), F32)] * 3,
                          compiler_params=_params(1), name=name)(w, g, m, v)


def _adamw_update(w, g, m, v):
    nm = ADAM_B1 * m + (1.0 - ADAM_B1) * g
    nv = ADAM_B2 * v + (1.0 - ADAM_B2) * (g * g)
    m_hat = nm / (1.0 - ADAM_B1 ** ADAM_STEP)
    v_hat = nv / (1.0 - ADAM_B2 ** ADAM_STEP)
    return -ADAM_LR * (m_hat / (jnp.sqrt(v_hat) + ADAM_EPS) + ADAM_WD * w), nm, nv


def _adamw_many(ws, gs, ms, vs, name):
    n = len(ws)

    def body(*refs):
        for i in range(n):
            d, nm, nv = _adamw_update(refs[i][...], refs[n + i][...], refs[2 * n + i][...], refs[3 * n + i][...])
            refs[4 * n + i][...] = d
            refs[5 * n + i][...] = nm
            refs[6 * n + i][...] = nv

    return pl.pallas_call(body, in_specs=[VMEM_SPEC] * (4 * n), out_specs=[VMEM_SPEC] * (3 * n),
                          out_shape=[jax.ShapeDtypeStruct(w.shape, F32) for w in ws] * 3, name=name,
                          )(*ws, *gs, *ms, *vs)


def _pack(pieces):
    rows = []
    for p in pieces:
        flat = p.reshape(-1)
        pad = (-flat.shape[0]) % LANES
        if pad:
            flat = jnp.concatenate([flat, jnp.zeros((pad,), F32)])
        rows.append(flat.reshape(-1, LANES))
    total = sum(r.shape[0] for r in rows)
    pad_rows = (-total) % 8
    if pad_rows:
        rows.append(jnp.zeros((pad_rows, LANES), F32))
    return jnp.concatenate(rows, axis=0)


def _unpack(buf, shapes):
    out, r0 = [], 0
    for shp in shapes:
        n = int(np.prod(shp))
        nr = -(-n // LANES)
        out.append(buf[r0:r0 + nr].reshape(-1)[:n].reshape(shp))
        r0 += nr
    return out


SMALL_NAMES = ("rel_table", "b_in", "conv_w", "conv_b", "conv_ln_g", "conv_ln_b", "attn_norm_g", "conv_norm_g",
               "ln1_g", "ln1_b", "ffn_conv_w", "ffn_conv_b", "ln2_g", "ln2_b")
BIG_NAMES = ("w_in", "w_out", "w_up", "w_down")
WEIGHT_ORDER = ("rel_table", "w_in", "b_in", "conv_w", "conv_b", "conv_ln_g", "conv_ln_b", "attn_norm_g",
                "conv_norm_g", "w_out", "ln1_g", "ln1_b", "w_up", "ffn_conv_w", "ffn_conv_b", "w_down",
                "ln2_g", "ln2_b")


def kernel(x, rel_table, w_in, b_in, conv_w, conv_b, conv_ln_g, conv_ln_b, attn_norm_g, conv_norm_g, w_out, ln1_g, ln1_b, w_up, ffn_conv_w, ffn_conv_b, w_down, ln2_g, ln2_b, loss_target, m_rel_table, m_w_in, m_b_in, m_conv_w, m_conv_b, m_conv_ln_g, m_conv_ln_b, m_attn_norm_g, m_conv_norm_g, m_w_out, m_ln1_g, m_ln1_b, m_w_up, m_ffn_conv_w, m_ffn_conv_b, m_w_down, m_ln2_g, m_ln2_b, v_rel_table, v_w_in, v_b_in, v_conv_w, v_conv_b, v_conv_ln_g, v_conv_ln_b, v_attn_norm_g, v_conv_norm_g, v_w_out, v_ln1_g, v_ln1_b, v_w_up, v_ffn_conv_w, v_ffn_conv_b, v_w_down, v_ln2_g, v_ln2_b):
    args = dict(locals())
    weights = {n: args[n] for n in WEIGHT_ORDER}
    moms = {n: args["m_" + n] for n in WEIGHT_ORDER}
    vels = {n: args["v_" + n] for n in WEIGHT_ORDER}
    xi, yi, ci = _place()
    ids = jnp.stack([xi, yi, ci]).astype(jnp.int32)
    shard = 2 * xi + yi
    D = x.shape[-1]
    DFF = w_down.shape[1] * N_SHARDS
    CW = conv_norm_g.shape[-1]

    tr = lambda t: jnp.transpose(t[0])
    (g_in,), (g_cw, g_fcw) = _gather_weights([tr(w_in)], [conv_w[0], ffn_conv_w[0]])
    cols = lambda t: jnp.transpose(t, (1, 0, 2)).reshape(t.shape[1], N_SHARDS * t.shape[2])
    staged = [_stage_half(w[0], ids, name="stage_" + n) for w, n in ((w_out, "w_out"), (w_up, "w_up"),
                                                                     (w_down, "w_down"))]

    grad_x, fulls, all_packs = _local_step(
        x, loss_target, rel_table, g_in.reshape(-1, D), b_in, cols(g_cw), conv_b, conv_ln_g, conv_ln_b, attn_norm_g,
        conv_norm_g, staged, ln1_g, ln1_b, cols(g_fcw), ffn_conv_b, ln2_g, ln2_b, ids)
    big_grads = dict(zip(BIG_NAMES, _sibling_assemble(fulls)))
    big_grads["w_up"] = _transpose(big_grads["w_up"], name="transpose_dw_up")

    summed = _sum_packs(all_packs)
    full_shapes = {n: weights[n].shape for n in SMALL_NAMES}
    full_shapes["conv_w"] = (1, CONV_KERNEL, CW)
    full_shapes["ffn_conv_w"] = (1, FFN_CONV_KERNEL, 2 * DFF)
    un = _unpack(summed, [(1, LANES)] + [full_shapes[n] for n in SMALL_NAMES])
    loss = un[0][0, 0]
    small_grads = dict(zip(SMALL_NAMES, un[1:]))
    for n in ("conv_w", "ffn_conv_w"):
        width = weights[n].shape[-1]
        small_grads[n] = lax.dynamic_slice_in_dim(small_grads[n], shard * width, width, axis=2)

    grads, delta, new_m, new_v = {}, {}, {}, {}
    for n in BIG_NAMES:
        shp = weights[n].shape
        g2 = big_grads[n]
        if n == "w_in":
            res = (g2,) + tuple(_adamw(tr(weights[n]), g2, tr(moms[n]), tr(vels[n]), name="adamw_" + n))
            res = [jnp.transpose(t) for t in res]
        else:
            res = (g2,) + tuple(_adamw(weights[n][0], g2, moms[n][0], vels[n][0], name="adamw_" + n))
        grads[n], delta[n], new_m[n], new_v[n] = (t.reshape(shp) for t in res)
    pick = lambda src: [src[n] for n in SMALL_NAMES]
    small_out = _adamw_many(pick(weights), pick(small_grads), pick(moms), pick(vels), name="adamw_small")
    ns = len(SMALL_NAMES)
    for tgt, part in ((delta, small_out[:ns]), (new_m, small_out[ns:2 * ns]), (new_v, small_out[2 * ns:])):
        tgt.update(zip(SMALL_NAMES, part))
    grads.update(small_grads)

    return (loss, grad_x, *[grads[n] for n in WEIGHT_ORDER], *[delta[n] for n in WEIGHT_ORDER],
            *[new_m[n] for n in WEIGHT_ORDER], *[new_v[n] for n in WEIGHT_ORDER])
```

```python
import functools
import math

import numpy as np
import jax
import jax.numpy as jnp
from jax import lax
from jax.experimental import pallas as pl
from jax.experimental.pallas import tpu as pltpu

F32 = jnp.float32
BF16 = jnp.bfloat16
MESH = pl.DeviceIdType.MESH

HEAD_DIM = 64
LANES = 128
ATTN_BLOCK = 128
DILATED_CONFIGS = ((128, 1), (512, 4), (2048, 16))
CONV_KERNEL = 31
FFN_CONV_KERNEL = 3
REL_BUCKETS = 32
REL_MAX_DIST = 2048
DEPTH = 1
ALPHA = (2 * DEPTH) ** 0.25
LN_EPS = 1e-5
NEG_INF = -1e30
QK_SCALE = 1.0 / math.sqrt(HEAD_DIM)
ADAM_LR = 0.001
ADAM_B1 = 0.9
ADAM_B2 = 0.999
ADAM_EPS = 1e-08
ADAM_WD = 0.01
ADAM_STEP = 10
VMEM_LIMIT = 52 * 1024 * 1024
FFN_COLS = 128
N_SHARDS = 4
N_DEV = 8


def _params(n_axes):
    return pltpu.CompilerParams(dimension_semantics=("arbitrary",) * n_axes,
                                vmem_limit_bytes=VMEM_LIMIT)


MM_DIMS = {"nn": (((1,), (0,)), ((), ())), "nt": (((1,), (1,)), ((), ())), "tn": (((0,), (0,)), ((), ()))}


class _Background:
    def __init__(self, in_arrays, out_shapes, aliases, n_sems, run, n_local=1):
        self.in_arrays, self.out_shapes, self.aliases = list(in_arrays), list(out_shapes), dict(aliases)
        self.n_sems, self.n_local, self.run = n_sems, n_local, run

    def scratch(self):
        return [pltpu.SemaphoreType.DMA((self.n_sems,)), pltpu.SemaphoreType.DMA((self.n_sems,)),
                pltpu.SemaphoreType.DMA((self.n_local,))]


def _hosted_call(body, bg, *, grid, in_specs, out_specs, out_shape, scratch_shapes, operands, name):
    n_in, n_out, n_scr = len(in_specs), len(out_specs), len(scratch_shapes)
    if bg is None:
        return pl.pallas_call(lambda *refs: body(refs, lambda post: None), grid=grid, in_specs=in_specs,
                              out_specs=out_specs, out_shape=out_shape, scratch_shapes=scratch_shapes,
                              compiler_params=_params(len(grid)), name=name)(*operands)
    nb_in, nb_out = len(bg.in_arrays), len(bg.out_shapes)
    n_steps = int(np.prod(grid))

    def full_body(*refs):
        own = refs[:n_in] + refs[n_in + nb_in:n_in + nb_in + n_out] \
            + refs[n_in + nb_in + n_out + nb_out:n_in + nb_in + n_out + nb_out + n_scr]
        bg_in = refs[n_in:n_in + nb_in]
        bg_out = refs[n_in + nb_in + n_out:n_in + nb_in + n_out + nb_out]
        sems = refs[n_in + nb_in + n_out + nb_out + n_scr:]
        step = pl.program_id(0)
        for ax in range(1, len(grid)):
            step = step * grid[ax] + pl.program_id(ax)

        def hook(post):
            bg.run(step, n_steps, bg_in, bg_out, *sems, post)

        body(own, hook)

    res = pl.pallas_call(
        full_body, grid=grid, in_specs=list(in_specs) + [HBM_SPEC] * nb_in,
        out_specs=list(out_specs) + [HBM_SPEC] * nb_out, out_shape=list(out_shape) + bg.out_shapes,
        input_output_aliases={n_in + a: n_out + o for a, o in bg.aliases.items()},
        scratch_shapes=list(scratch_shapes) + bg.scratch(), compiler_params=_params(len(grid)), name=name,
    )(*operands, *bg.in_arrays)
    return res


def _matmul_general(ins, part_fn, *, grid, tm, tn, outs, epilogue, extras=(), name, bg=None):
    nk = grid[2]
    n_in, n_extra = len(ins), len(extras)

    def body(refs, bg_hook):
        in_refs = refs[:n_in]
        rest = refs[n_in:]
        extra_refs = rest[:n_extra]
        out_refs = rest[n_extra:n_extra + len(outs)]
        acc_ref = rest[-1]
        i, j, k = pl.program_id(0), pl.program_id(1), pl.program_id(2)
        bg_hook(False)
        part = part_fn(in_refs, i, j, k)
        if nk == 1:
            epilogue(part, i, j, extra_refs, out_refs)
        else:
            @pl.when(k == 0)
            def _():
                acc_ref[...] = part

            @pl.when(k > 0)
            def _():
                acc_ref[...] += part

            @pl.when(k == nk - 1)
            def _():
                epilogue(acc_ref[...], i, j, extra_refs, out_refs)
        bg_hook(True)

    in_specs = [pl.BlockSpec(bs, im) for (_, bs, im) in list(ins) + list(extras)]
    out_specs = [pl.BlockSpec(bs, im) for (_, _, bs, im) in outs]
    out_shape = [jax.ShapeDtypeStruct(s, d) for (s, d, _, _) in outs]
    return _hosted_call(body, bg, grid=grid, in_specs=in_specs, out_specs=out_specs, out_shape=out_shape,
                        scratch_shapes=[pltpu.VMEM((tm, tn), F32)],
                        operands=[e[0] for e in ins] + [e[0] for e in extras], name=name)


def _dot(a, b, mode):
    return lax.dot_general(a.astype(BF16), b.astype(BF16), MM_DIMS[mode], preferred_element_type=F32)


def _matmul(a, b, *, mode, tm, tn, tk, outs, epilogue, extras=(), name, bg=None):
    if mode == "tn":
        K, M = a.shape
        N = b.shape[1]
        ins = [(a, (tk, tm), lambda i, j, k: (k, i)), (b, (tk, tn), lambda i, j, k: (k, j))]
    elif mode == "nt":
        M, K = a.shape
        N = b.shape[0]
        ins = [(a, (tm, tk), lambda i, j, k: (i, k)), (b, (tn, tk), lambda i, j, k: (j, k))]
    else:
        M, K = a.shape
        N = b.shape[1]
        ins = [(a, (tm, tk), lambda i, j, k: (i, k)), (b, (tk, tn), lambda i, j, k: (k, j))]
    assert M % tm == 0 and N % tn == 0 and K % tk == 0, (name, M, N, K, tm, tn, tk)

    def part_fn(in_refs, i, j, k):
        return _dot(in_refs[0][...], in_refs[1][...], mode)

    return _matmul_general(ins, part_fn, grid=(M // tm, N // tn, K // tk), tm=tm, tn=tn, outs=outs,
                           epilogue=epilogue, extras=extras, name=name, bg=bg)


def _plain_out(M, N, tm, tn, dtype):
    return ((M, N), dtype, (tm, tn), lambda i, j, k: (i, j))


def _mm_plain(a, b, *, mode, tm, tn, tk, out_dtype, name, bias=None, bg=None):
    if mode == "tn":
        M, N = a.shape[1], b.shape[1]
    elif mode == "nt":
        M, N = a.shape[0], b.shape[0]
    else:
        M, N = a.shape[0], b.shape[1]
    extras = []
    if bias is not None:
        extras.append((bias, (1, tn), lambda i, j, k: (0, j)))

    def epilogue(acc, i, j, extra_refs, out_refs):
        if bias is not None:
            acc = acc + extra_refs[0][...]
        out_refs[0][...] = acc.astype(out_dtype)

    res = _matmul(a, b, mode=mode, tm=tm, tn=tn, tk=tk, outs=[_plain_out(M, N, tm, tn, out_dtype)],
                  epilogue=epilogue, extras=extras, name=name, bg=bg)
    return res[0] if bg is None else res


def _row_tile(T, want):
    t = min(T, want)
    while T % t:
        t //= 2
    return t


def _col_tile(N, want):
    if N <= want:
        return N
    best = None
    for c in range(LANES, want + 1, LANES):
        if N % c == 0:
            best = c
    return best if best is not None else N


def _accumulate(ref, first, val):
    @pl.when(first)
    def _():
        ref[...] = val

    @pl.when(jnp.logical_not(first))
    def _():
        ref[...] += val


def _ln_fwd(z, g, b):
    mu = jnp.mean(z, axis=-1, keepdims=True)
    zc = z - mu
    var = jnp.mean(zc * zc, axis=-1, keepdims=True)
    r = lax.rsqrt(var + LN_EPS)
    xh = zc * r
    return xh * g + b, xh, r


def _ln_bwd(dy, xh, r, g):
    dxh = dy * g
    m1 = jnp.mean(dxh, axis=-1, keepdims=True)
    m2 = jnp.mean(dxh * xh, axis=-1, keepdims=True)
    return r * (dxh - m1 - xh * m2)


def _sigmoid(x):
    return 1.0 / (1.0 + jnp.exp(-x))


def _shift_down(x, s, row):
    if s == 0:
        return x
    rolled = pltpu.roll(x, s, 0)
    nfix = -(-s // 8) * 8
    head = jnp.where(row[:nfix] >= s, rolled[:nfix], 0.0)
    return jnp.concatenate([head, rolled[nfix:]], axis=0)


def _shift_up(x, s, row):
    if s == 0:
        return x
    n = x.shape[0]
    rolled = pltpu.roll(x, n - s, 0)
    nfix = -(-s // 8) * 8
    tail = jnp.where(row[n - nfix:] < n - s, rolled[n - nfix:], 0.0)
    return jnp.concatenate([rolled[:n - nfix], tail], axis=0)


def _bucket_tables():
    exact = REL_BUCKETS // 2
    qi = np.arange(ATTN_BLOCK)[:, None]
    kj = np.arange(2 * ATTN_BLOCK)[None, :]
    steps = qi + ATTN_BLOCK - kj
    buckets, masks = [], []
    for window, dilation in DILATED_CONFIGS:
        max_steps = window // dilation
        band = (steps >= 0) & (steps <= max_steps)
        dist = np.maximum(steps, 0) * dilation
        d_f = np.maximum(dist, 1).astype(np.float32)
        large = exact + (np.log(d_f / np.float32(exact)) / np.float32(math.log(REL_MAX_DIST / exact))
                         * np.float32(REL_BUCKETS - exact)).astype(np.int32)
        large = np.minimum(large, REL_BUCKETS - 1)
        bucket = np.where(dist < exact, dist, large).astype(np.int32)
        buckets.append(bucket.reshape(1, -1))
        masks.append(np.where(band, 0.0, NEG_INF).astype(np.float32).reshape(1, -1))
    return np.stack(buckets), np.stack(masks)


def _split_hi_lo(x):
    hi = x.astype(BF16)
    lo = (x - hi.astype(F32)).astype(BF16)
    return hi, lo


def _bias_build(rel_table_t, bucket, mask):
    H = rel_table_t.shape[0]
    n = bucket.shape[-1]

    def body(t_ref, bkt_ref, mask_ref, o_ref):
        onehot = (lax.broadcasted_iota(jnp.int32, (REL_BUCKETS, n), 0) == bkt_ref[0]).astype(BF16)
        t = t_ref[...]
        t1 = t.astype(BF16)
        r1 = t - t1.astype(F32)
        t2 = r1.astype(BF16)
        t3 = (r1 - t2.astype(F32)).astype(BF16)
        acc = jnp.dot(t1, onehot, preferred_element_type=F32)
        acc = acc + jnp.dot(t2, onehot, preferred_element_type=F32)
        acc = acc + jnp.dot(t3, onehot, preferred_element_type=F32)
        o_ref[0] = acc + mask_ref[0]

    return pl.pallas_call(
        body, grid=(3,),
        in_specs=[pl.BlockSpec((H, REL_BUCKETS), lambda b: (0, 0)),
                  pl.BlockSpec((1, 1, n), lambda b: (b, 0, 0)),
                  pl.BlockSpec((1, 1, n), lambda b: (b, 0, 0))],
        out_specs=pl.BlockSpec((1, H, n), lambda b: (b, 0, 0)),
        out_shape=jax.ShapeDtypeStruct((3, H, n), F32),
        compiler_params=_params(1), name="bias_build",
    )(rel_table_t, bucket, mask)


def _rel_grad(dbias, bucket):
    H = dbias.shape[1]
    n = bucket.shape[-1]
    dims = (((1,), (1,)), ((), ()))

    def body(d_ref, bkt_ref, o_ref):
        b = pl.program_id(0)
        onehot = (lax.broadcasted_iota(jnp.int32, (REL_BUCKETS, n), 0) == bkt_ref[0]).astype(BF16)
        d = d_ref[0]
        d1 = d.astype(BF16)
        r1 = d - d1.astype(F32)
        d2 = r1.astype(BF16)
        d3 = (r1 - d2.astype(F32)).astype(BF16)
        acc = lax.dot_general(d1, onehot, dims, preferred_element_type=F32)
        acc = acc + lax.dot_general(d2, onehot, dims, preferred_element_type=F32)
        acc = acc + lax.dot_general(d3, onehot, dims, preferred_element_type=F32)
        _accumulate(o_ref, b == 0, acc)

    return pl.pallas_call(
        body, grid=(3,),
        in_specs=[pl.BlockSpec((1, H, n), lambda b: (b, 0, 0)),
                  pl.BlockSpec((1, 1, n), lambda b: (b, 0, 0))],
        out_specs=pl.BlockSpec((H, REL_BUCKETS), lambda b: (0, 0)),
        out_shape=jax.ShapeDtypeStruct((H, REL_BUCKETS), F32),
        compiler_params=_params(1), name="rel_grad",
    )(dbias, bucket)


def _attn_specs(B, S, AW, d):
    L = S // d
    HP = AW // LANES
    W3 = 3 * HP
    q_spec = pl.BlockSpec((1, L, LANES), lambda h, b, r: (b, 0, r * W3 + h))
    k_spec = pl.BlockSpec((1, L, LANES), lambda h, b, r: (b, 0, r * W3 + HP + h))
    v_spec = pl.BlockSpec((1, L, LANES), lambda h, b, r: (b, 0, r * W3 + 2 * HP + h))
    o_spec = pl.BlockSpec((1, L, LANES), lambda h, b, r: (b, 0, r * HP + h))
    bias_spec = pl.BlockSpec((2, ATTN_BLOCK, 2 * ATTN_BLOCK), lambda h, b, r: (h, 0, 0))
    return L, HP, q_spec, k_spec, v_spec, o_spec, bias_spec


def _attn_fwd(qkv, bias, B, S, AW, d, name):
    L, HP, q_spec, k_spec, v_spec, o_spec, bias_spec = _attn_specs(B, S, AW, d)
    nb = L // ATTN_BLOCK
    nt = (((1,), (1,)), ((), ()))

    def body(q_ref, k_ref, v_ref, b_ref, o_ref, lse_ref):
        head0 = lax.broadcasted_iota(jnp.int32, (1, LANES), 1) < HEAD_DIM

        def block(n, first):
            qs = pl.multiple_of(n * ATTN_BLOCK, ATTN_BLOCK)
            q = q_ref[0, pl.ds(qs, ATTN_BLOCK), :]
            if first:
                kk = k_ref[0, pl.ds(0, ATTN_BLOCK), :]
                vv = v_ref[0, pl.ds(0, ATTN_BLOCK), :]
            else:
                ks = pl.multiple_of(n * ATTN_BLOCK - ATTN_BLOCK, ATTN_BLOCK)
                kk = k_ref[0, pl.ds(ks, 2 * ATTN_BLOCK), :]
                vv = v_ref[0, pl.ds(ks, 2 * ATTN_BLOCK), :]
            outs, lses = [], []
            for e in range(2):
                msk = head0 if e == 0 else jnp.logical_not(head0)
                qe = jnp.where(msk, q, jnp.zeros_like(q))
                s = lax.dot_general(qe, kk, nt, preferred_element_type=F32) * QK_SCALE
                s = s + (b_ref[e, :, ATTN_BLOCK:] if first else b_ref[e])
                m = jnp.max(s, axis=-1, keepdims=True)
                p = jnp.exp(s - m)
                l = jnp.sum(p, axis=-1, keepdims=True)
                o = jnp.dot(p.astype(BF16), vv, preferred_element_type=F32)
                outs.append(o / l)
                lses.append(jnp.broadcast_to(m + jnp.log(l), (ATTN_BLOCK, LANES)))
            o_ref[0, pl.ds(qs, ATTN_BLOCK), :] = jnp.where(head0, outs[0], outs[1])
            lse_ref[0, pl.ds(qs, ATTN_BLOCK), :] = jnp.where(head0, lses[0], lses[1])

        block(0, True)
        if nb > 1:
            def loop(n, c):
                block(n, False)
                return c
            lax.fori_loop(1, nb, loop, 0)

    qv = qkv.reshape(B, L, d * 3 * AW)
    o, lse = pl.pallas_call(
        body, grid=(HP, B, d), in_specs=[q_spec, k_spec, v_spec, bias_spec],
        out_specs=[o_spec, o_spec],
        out_shape=[jax.ShapeDtypeStruct((B, L, d * AW), F32)] * 2,
        compiler_params=_params(3), name=name,
    )(qv, qv, qv, bias)
    return o.reshape(B * S, AW), lse.reshape(B * S, AW)


def _attn_bwd(qkv, do, lse, dd, bias, B, S, AW, d, name):
    L, HP, q_spec, k_spec, v_spec, o_spec, bias_spec = _attn_specs(B, S, AW, d)
    nb = L // ATTN_BLOCK
    nt = (((1,), (1,)), ((), ()))
    tn = (((0,), (0,)), ((), ()))

    def body(q_ref, k_ref, v_ref, do_ref, lse_ref, dd_ref, b_ref, dq_ref, dk_ref, dv_ref, db_ref):
        head0 = lax.broadcasted_iota(jnp.int32, (1, LANES), 1) < HEAD_DIM
        first_step = jnp.logical_and(pl.program_id(1) == 0, pl.program_id(2) == 0)

        @pl.when(first_step)
        def _():
            db_ref[...] = jnp.zeros_like(db_ref)

        dk_ref[...] = jnp.zeros_like(dk_ref)
        dv_ref[...] = jnp.zeros_like(dv_ref)

        def block(n, first):
            qs = pl.multiple_of(n * ATTN_BLOCK, ATTN_BLOCK)
            nkeys = ATTN_BLOCK if first else 2 * ATTN_BLOCK
            ks = 0 if first else pl.multiple_of(n * ATTN_BLOCK - ATTN_BLOCK, ATTN_BLOCK)
            q = q_ref[0, pl.ds(qs, ATTN_BLOCK), :]
            kk = k_ref[0, pl.ds(ks, nkeys), :]
            vv = v_ref[0, pl.ds(ks, nkeys), :]
            dout = do_ref[0, pl.ds(qs, ATTN_BLOCK), :]
            lse_b = lse_ref[0, pl.ds(qs, ATTN_BLOCK), :]
            dd_b = dd_ref[0, pl.ds(qs, ATTN_BLOCK), :]
            dq = jnp.zeros((ATTN_BLOCK, LANES), F32)
            dkk = jnp.zeros((nkeys, LANES), F32)
            dvv = jnp.zeros((nkeys, LANES), F32)
            for e in range(2):
                msk = head0 if e == 0 else jnp.logical_not(head0)
                c0 = e * HEAD_DIM
                qe = jnp.where(msk, q, jnp.zeros_like(q))
                doe = jnp.where(msk, dout, jnp.zeros_like(dout))
                kke = jnp.where(msk, kk, jnp.zeros_like(kk))
                s = lax.dot_general(qe, kk, nt, preferred_element_type=F32) * QK_SCALE
                s = s + (b_ref[e, :, ATTN_BLOCK:] if first else b_ref[e])
                p = jnp.exp(s - lse_b[:, c0:c0 + 1])
                dp = lax.dot_general(doe, vv, nt, preferred_element_type=F32)
                ds = p * (dp - dd_b[:, c0:c0 + 1])
                if first:
                    db_ref[e, :, ATTN_BLOCK:] += ds
                else:
                    db_ref[e] += ds
                dsb = (ds * QK_SCALE).astype(BF16)
                dq = dq + jnp.dot(dsb, kke, preferred_element_type=F32)
                dkk = dkk + lax.dot_general(dsb, qe, tn, preferred_element_type=F32)
                dvv = dvv + lax.dot_general(p.astype(BF16), doe, tn, preferred_element_type=F32)
            dq_ref[0, pl.ds(qs, ATTN_BLOCK), :] = dq
            dk_ref[0, pl.ds(ks, nkeys), :] += dkk
            dv_ref[0, pl.ds(ks, nkeys), :] += dvv

        block(0, True)
        if nb > 1:
            def loop(n, c):
                block(n, False)
                return c
            lax.fori_loop(1, nb, loop, 0)

    H = AW // HEAD_DIM
    qv = qkv.reshape(B, L, d * 3 * AW)
    view = lambda t: t.reshape(B, L, d * AW)
    dq, dk, dv, db = pl.pallas_call(
        body, grid=(HP, B, d),
        in_specs=[q_spec, k_spec, v_spec, o_spec, o_spec, o_spec, bias_spec],
        out_specs=[o_spec, o_spec, o_spec, bias_spec],
        out_shape=[jax.ShapeDtypeStruct((B, L, d * AW), F32)] * 3
        + [jax.ShapeDtypeStruct((H, ATTN_BLOCK, 2 * ATTN_BLOCK), F32)],
        compiler_params=_params(3), name=name,
    )(qv, qv, qv, view(do), view(lse), view(dd), bias)
    flat = lambda t: t.reshape(B * S, AW)
    return flat(dq), flat(dk), flat(dv), db


def _attn_combine(ons, lses, gain, tm):
    T, AW = ons[0].shape

    def body(o1, o2, o3, l1, l2, l3, g_ref, attn_ref, lse_ref, mix_ref, r_ref):
        la, lb, lc = l1[...], l2[...], l3[...]
        m = jnp.maximum(jnp.maximum(la, lb), lc)
        ea, eb, ec = jnp.exp(la - m), jnp.exp(lb - m), jnp.exp(lc - m)
        den = ea + eb + ec
        attn = (ea * o1[...] + eb * o2[...] + ec * o3[...]) / den
        attn_ref[...] = attn
        lse_ref[...] = m + jnp.log(den)
        r = lax.rsqrt(jnp.mean(attn * attn, axis=-1, keepdims=True) + LN_EPS)
        mix_ref[...] = (attn * r * g_ref[...]).astype(BF16)
        r_ref[...] = jnp.broadcast_to(r, (tm, LANES))

    row = pl.BlockSpec((tm, AW), lambda i: (i, 0))
    return pl.pallas_call(
        body, grid=(T // tm,),
        in_specs=[row] * 6 + [pl.BlockSpec((1, AW), lambda i: (0, 0))],
        out_specs=[row, row, row, pl.BlockSpec((tm, LANES), lambda i: (i, 0))],
        out_shape=[jax.ShapeDtypeStruct((T, AW), F32), jax.ShapeDtypeStruct((T, AW), F32),
                   jax.ShapeDtypeStruct((T, AW), BF16), jax.ShapeDtypeStruct((T, LANES), F32)],
        compiler_params=_params(1), name="attn_combine",
    )(*ons, *lses, gain)


def _to_sub(src_ref, stage_ref, dsts, S):
    stage_ref[...] = src_ref[0].astype(F32)
    for (_, d), dst in zip(DILATED_CONFIGS[1:], dsts):
        L = S // d
        for r in range(d):
            dst[r * L:(r + 1) * L, :] = stage_ref[pl.ds(r, L, stride=d), :].astype(dst.dtype)


def _branch_blocks(S, d, block):
    nb = S // d // ATTN_BLOCK
    inner_unroll = 3 if (nb - 1) % 3 == 0 else 1

    def per_residue(r, c):
        block(r * nb, True)
        if nb > 1:
            def inner(n, c2):
                block(r * nb + n, False)
                return c2
            lax.fori_loop(1, nb, inner, 0, unroll=inner_unroll)
        return c

    lax.fori_loop(0, d, per_residue, 0, unroll=4 if nb == 1 else 1)


def _attention_fwd(qkv, bias_all, B, S, AW):
    HP = AW // LANES
    nt = MM_DIMS["nt"]

    def body(q_ref, k_ref, v_ref, b_ref, o_ref, lse_ref, stage, q4, q16, k4, k16, v4, v16, o1, l1, o4, l4, o16, l16):
        head0 = lax.broadcasted_iota(jnp.int32, (1, LANES), 1) < HEAD_DIM
        _to_sub(q_ref, stage, (q4, q16), S)
        _to_sub(k_ref, stage, (k4, k16), S)
        _to_sub(v_ref, stage, (v4, v16), S)
        srcs = ((q_ref.at[0], k_ref.at[0], v_ref.at[0], o1, l1), (q4, k4, v4, o4, l4), (q16, k16, v16, o16, l16))
        for bi, (_, d) in enumerate(DILATED_CONFIGS):
            qs_ref, ks_ref, vs_ref, od_ref, ld_ref = srcs[bi]

            def block(g, first, bi=bi, qs_ref=qs_ref, ks_ref=ks_ref, vs_ref=vs_ref, od_ref=od_ref, ld_ref=ld_ref):
                qs = pl.multiple_of(g * ATTN_BLOCK, ATTN_BLOCK)
                nkeys = ATTN_BLOCK if first else 2 * ATTN_BLOCK
                ks = qs if first else pl.multiple_of(qs - ATTN_BLOCK, ATTN_BLOCK)
                q = qs_ref[pl.ds(qs, ATTN_BLOCK), :]
                kk = ks_ref[pl.ds(ks, nkeys), :]
                vv = vs_ref[pl.ds(ks, nkeys), :]
                outs, lses = [], []
                for e in range(2):
                    msk = head0 if e == 0 else jnp.logical_not(head0)
                    qe = jnp.where(msk, q * QK_SCALE, jnp.zeros_like(q))
                    s = lax.dot_general(qe, kk, nt, preferred_element_type=F32)
                    s = s + (b_ref[bi, e, :, ATTN_BLOCK:] if first else b_ref[bi, e])
                    m = jnp.max(s, axis=-1, keepdims=True)
                    p = jnp.exp(s - m)
                    l = jnp.sum(p, axis=-1, keepdims=True)
                    o = jnp.dot(p.astype(BF16), vv, preferred_element_type=F32)
                    outs.append(o / l)
                    lses.append(jnp.broadcast_to(m + jnp.log(l), (ATTN_BLOCK, LANES)))
                od_ref[pl.ds(qs, ATTN_BLOCK), :] = jnp.where(head0, outs[0], outs[1])
                ld_ref[pl.ds(qs, ATTN_BLOCK), :] = jnp.where(head0, lses[0], lses[1])

            _branch_blocks(S, d, block)

        def natural(sub_ref, d):
            L = S // d
            for r in range(d):
                stage[pl.ds(r, L, stride=d), :] = sub_ref[r * L:(r + 1) * L, :]
            return stage[...]

        la = l1[...]
        lb = natural(l4, 4)
        lc = natural(l16, 16)
        m = jnp.maximum(jnp.maximum(la, lb), lc)
        ea, eb, ec = jnp.exp(la - m), jnp.exp(lb - m), jnp.exp(lc - m)
        den = ea + eb + ec
        lse_ref[0] = m + jnp.log(den)
        acc = ea * o1[...]
        acc = acc + eb * natural(o4, 4)
        acc = acc + ec * natural(o16, 16)
        o_ref[0] = acc / den

    blk = lambda off: pl.BlockSpec((1, S, LANES), lambda b, h: (b, 0, off + h))
    qv = qkv.reshape(B, S, 3 * AW)
    sub_b = pltpu.VMEM((S, LANES), BF16)
    sub_f = pltpu.VMEM((S, LANES), F32)
    o, lse = pl.pallas_call(
        body, grid=(B, HP),
        in_specs=[blk(0), blk(HP), blk(2 * HP),
                  pl.BlockSpec((3, 2, ATTN_BLOCK, 2 * ATTN_BLOCK), lambda b, h: (0, h, 0, 0))],
        out_specs=[blk(0), blk(0)],
        out_shape=[jax.ShapeDtypeStruct((B, S, AW), F32)] * 2,
        scratch_shapes=[sub_f] + [sub_b] * 6 + [sub_f] * 6,
        compiler_params=_params(2), name="attention_fwd",
    )(qv, qv, qv, bias_all)
    return o.reshape(B * S, AW), lse.reshape(B * S, AW)


def _attention_bwd(qkv, do, lse, dd, bias_all, B, S, AW):
    HP = AW // LANES
    H = AW // HEAD_DIM
    nt, tn = MM_DIMS["nt"], MM_DIMS["tn"]

    def body(q_ref, k_ref, v_ref, do_ref, lse_ref, dd_ref, b_ref,
             dq_ref, dk_ref, dv_ref, csq_ref, csk_ref, csv_ref, db_ref,
             stage, q4, q16, k4, k16, v4, v16, g4, g16, l4, l16, d4, d16,
             aq1, ak1, av1, aq4, ak4, av4, aq16, ak16, av16):
        head0 = lax.broadcasted_iota(jnp.int32, (1, LANES), 1) < HEAD_DIM
        first_b = pl.program_id(1) == 0

        @pl.when(first_b)
        def _():
            db_ref[...] = jnp.zeros_like(db_ref)

        _to_sub(q_ref, stage, (q4, q16), S)
        _to_sub(k_ref, stage, (k4, k16), S)
        _to_sub(v_ref, stage, (v4, v16), S)
        _to_sub(do_ref, stage, (g4, g16), S)
        _to_sub(lse_ref, stage, (l4, l16), S)
        _to_sub(dd_ref, stage, (d4, d16), S)
        for acc in (ak1, av1, ak4, av4, ak16, av16):
            acc[...] = jnp.zeros_like(acc)
        srcs = ((q_ref.at[0], k_ref.at[0], v_ref.at[0], do_ref.at[0], lse_ref.at[0], dd_ref.at[0], aq1, ak1, av1),
                (q4, k4, v4, g4, l4, d4, aq4, ak4, av4), (q16, k16, v16, g16, l16, d16, aq16, ak16, av16))
        for bi, (_, d) in enumerate(DILATED_CONFIGS):
            def block(g, first, bi=bi, refs=srcs[bi]):
                qs_ref, ks_ref, vs_ref, gs_ref, ls_ref, ds_ref, aq, ak, av = refs
                qs = pl.multiple_of(g * ATTN_BLOCK, ATTN_BLOCK)
                nkeys = ATTN_BLOCK if first else 2 * ATTN_BLOCK
                ks = qs if first else pl.multiple_of(qs - ATTN_BLOCK, ATTN_BLOCK)
                q = qs_ref[pl.ds(qs, ATTN_BLOCK), :]
                kk = ks_ref[pl.ds(ks, nkeys), :]
                vv = vs_ref[pl.ds(ks, nkeys), :]
                dout = gs_ref[pl.ds(qs, ATTN_BLOCK), :]
                lse_b = ls_ref[pl.ds(qs, ATTN_BLOCK), :]
                dd_b = ds_ref[pl.ds(qs, ATTN_BLOCK), :]
                dq = jnp.zeros((ATTN_BLOCK, LANES), F32)
                dkk = jnp.zeros((nkeys, LANES), F32)
                dvv = jnp.zeros((nkeys, LANES), F32)
                for e in range(2):
                    msk = head0 if e == 0 else jnp.logical_not(head0)
                    c0 = e * HEAD_DIM
                    qe = jnp.where(msk, q * QK_SCALE, jnp.zeros_like(q))
                    doe = jnp.where(msk, dout, jnp.zeros_like(dout))
                    kke = jnp.where(msk, kk * QK_SCALE, jnp.zeros_like(kk))
                    s = lax.dot_general(qe, kk, nt, preferred_element_type=F32)
                    s = s + (b_ref[bi, e, :, ATTN_BLOCK:] if first else b_ref[bi, e])
                    p = jnp.exp(s - lse_b[:, c0:c0 + 1])
                    dp = lax.dot_general(doe, vv, nt, preferred_element_type=F32)
                    ds = p * (dp - dd_b[:, c0:c0 + 1])
                    if first:
                        db_ref[bi, e, :, ATTN_BLOCK:] += ds
                    else:
                        db_ref[bi, e] += ds
                    dsb = ds.astype(BF16)
                    dq = dq + jnp.dot(dsb, kke, preferred_element_type=F32)
                    dkk = dkk + lax.dot_general(dsb, qe, tn, preferred_element_type=F32)
                    dvv = dvv + lax.dot_general(p.astype(BF16), doe, tn, preferred_element_type=F32)
                aq[pl.ds(qs, ATTN_BLOCK), :] = dq
                ak[pl.ds(ks, nkeys), :] += dkk
                av[pl.ds(ks, nkeys), :] += dvv

            _branch_blocks(S, d, block)

        for a1, a4, a16, out_ref, cs_ref in ((aq1, aq4, aq16, dq_ref, csq_ref), (ak1, ak4, ak16, dk_ref, csk_ref),
                                             (av1, av4, av16, dv_ref, csv_ref)):
            stage[...] = a1[...]
            for d, sub in ((4, a4), (16, a16)):
                L = S // d
                for r in range(d):
                    stage[pl.ds(r, L, stride=d), :] += sub[r * L:(r + 1) * L, :]
            tot = stage[...]
            out_ref[0] = tot.astype(out_ref.dtype)
            _accumulate(cs_ref, first_b, jnp.sum(tot, axis=0, keepdims=True))

    blk = lambda off: pl.BlockSpec((1, S, LANES), lambda h, b: (b, 0, off + h))
    cs_spec = pl.BlockSpec((1, LANES), lambda h, b: (0, h))
    bias_spec = pl.BlockSpec((3, 2, ATTN_BLOCK, 2 * ATTN_BLOCK), lambda h, b: (0, h, 0, 0))
    qv = qkv.reshape(B, S, 3 * AW)
    view = lambda t: t.reshape(B, S, AW)
    sub_b = pltpu.VMEM((S, LANES), BF16)
    sub_f = pltpu.VMEM((S, LANES), F32)
    res = pl.pallas_call(
        body, grid=(HP, B),
        in_specs=[blk(0), blk(HP), blk(2 * HP), blk(0), blk(0), blk(0), bias_spec],
        out_specs=[blk(0), blk(0), blk(0), cs_spec, cs_spec, cs_spec, bias_spec],
        out_shape=[jax.ShapeDtypeStruct((B, S, AW), BF16)] * 3 + [jax.ShapeDtypeStruct((1, AW), F32)] * 3
        + [jax.ShapeDtypeStruct((3, H, ATTN_BLOCK, 2 * ATTN_BLOCK), F32)],
        scratch_shapes=[sub_f] + [sub_b] * 8 + [sub_f] * 4 + [sub_f] * 9,
        compiler_params=_params(2), name="attention_bwd",
    )(qv, qv, qv, view(do), view(lse), view(dd), bias_all)
    flat = lambda t: t.reshape(B * S, AW)
    return flat(res[0]), flat(res[1]), flat(res[2]), res[3], res[4], res[5], res[6]


def _regroup(src, stage, dst, d, S, off=0):
    if d == 1:
        dst[off:off + S, :] = src.astype(dst.dtype)
        return
    stage[...] = src.astype(F32)
    L = S // d
    for r in range(d):
        dst[off + r * L:off + (r + 1) * L, :] = stage[pl.ds(r, L, stride=d), :].astype(dst.dtype)


def _ungroup(sub_ref, off, nat_ref, d, S, add):
    L = S // d
    for r in range(d):
        rows = pl.ds(0, S) if d == 1 else pl.ds(r, L, stride=d)
        val = sub_ref[off + r * L:off + (r + 1) * L, :]
        if add:
            nat_ref[rows, :] += val
        else:
            nat_ref[rows, :] = val


def _branch_keys(ks, vs, S, nb, g_idx):
    blk3 = (S // ATTN_BLOCK, ATTN_BLOCK, LANES)
    kc3 = ks[ATTN_BLOCK:ATTN_BLOCK + S, :].reshape(blk3)
    vc3 = vs[ATTN_BLOCK:ATTN_BLOCK + S, :].reshape(blk3)
    if nb == 1:
        return kc3, vc3, None
    kk3 = jnp.concatenate([ks[0:S, :].reshape(blk3), kc3], axis=1)
    vv3 = jnp.concatenate([vs[0:S, :].reshape(blk3), vc3], axis=1)
    col = lax.broadcasted_iota(jnp.int32, (1, 1, 2 * ATTN_BLOCK), 2)
    dead = jnp.logical_and((g_idx & (nb - 1)) == 0, col < ATTN_BLOCK)
    return kk3, vv3, dead


def _branch_scores(qe, kk3, b_ref, bi, e, dead):
    s = jnp.einsum("gqe,gke->gqk", qe, kk3, preferred_element_type=F32)
    if dead is None:
        return s + b_ref[bi, e, :, ATTN_BLOCK:]
    return jnp.where(dead, NEG_INF, s + b_ref[bi, e])


def _attention_fwd(qkv, bias_all, B, S, AW, bg=None):
    HP = AW // LANES
    G = S // ATTN_BLOCK
    blk3 = (G, ATTN_BLOCK, LANES)

    def body(refs, bg_hook):
        q_ref, k_ref, v_ref, b_ref, o_ref, lse_ref, stage, qs, ks, vs, ot, lt, on0, on1, on2, ln0, ln1, ln2 = refs
        bg_hook(False)
        head0 = lax.broadcasted_iota(jnp.int32, (1, 1, LANES), 2) < HEAD_DIM
        g_idx = lax.broadcasted_iota(jnp.int32, (G, 1, 1), 0)
        ks[0:ATTN_BLOCK, :] = jnp.zeros((ATTN_BLOCK, LANES), BF16)
        vs[0:ATTN_BLOCK, :] = jnp.zeros((ATTN_BLOCK, LANES), BF16)
        nat_o, nat_l = (on0, on1, on2), (ln0, ln1, ln2)
        for bi, (_, d) in enumerate(DILATED_CONFIGS):
            nb = S // d // ATTN_BLOCK
            _regroup(q_ref[0], stage, qs, d, S)
            _regroup(k_ref[0], stage, ks, d, S, ATTN_BLOCK)
            _regroup(v_ref[0], stage, vs, d, S, ATTN_BLOCK)
            q3 = qs[...].reshape(blk3) * QK_SCALE
            kk3, vv3, dead = _branch_keys(ks, vs, S, nb, g_idx)
            outs, lses = [], []
            for e in range(2):
                msk = head0 if e == 0 else jnp.logical_not(head0)
                qe = jnp.where(msk, q3, jnp.zeros_like(q3))
                s = _branch_scores(qe, kk3, b_ref, bi, e, dead)
                m = jnp.max(s, axis=-1, keepdims=True)
                p = jnp.exp(s - m)
                l = jnp.sum(p, axis=-1, keepdims=True)
                o = jnp.einsum("gqk,gke->gqe", p.astype(BF16), vv3, preferred_element_type=F32)
                outs.append(o / l)
                lses.append(jnp.broadcast_to(m + jnp.log(l), blk3))
            ot[...] = jnp.where(head0, outs[0], outs[1]).reshape(S, LANES)
            lt[...] = jnp.where(head0, lses[0], lses[1]).reshape(S, LANES)
            _ungroup(ot, 0, nat_o[bi], d, S, add=False)
            _ungroup(lt, 0, nat_l[bi], d, S, add=False)

        la, lb, lc = ln0[...], ln1[...], ln2[...]
        m = jnp.maximum(jnp.maximum(la, lb), lc)
        ea, eb, ec = jnp.exp(la - m), jnp.exp(lb - m), jnp.exp(lc - m)
        den = ea + eb + ec
        lse_ref[0] = m + jnp.log(den)
        o_ref[0] = (ea * on0[...] + eb * on1[...] + ec * on2[...]) / den
        bg_hook(True)

    blk = lambda off: pl.BlockSpec((1, S, LANES), lambda b, h: (b, 0, off + h))
    qv = qkv.reshape(B, S, 3 * AW)
    sub_f = pltpu.VMEM((S, LANES), F32)
    pad_b = pltpu.VMEM((S + ATTN_BLOCK, LANES), BF16)
    res = _hosted_call(
        body, bg, grid=(B, HP),
        in_specs=[blk(0), blk(HP), blk(2 * HP),
                  pl.BlockSpec((3, 2, ATTN_BLOCK, 2 * ATTN_BLOCK), lambda b, h: (0, h, 0, 0))],
        out_specs=[blk(0), blk(0)],
        out_shape=[jax.ShapeDtypeStruct((B, S, AW), F32)] * 2,
        scratch_shapes=[sub_f, pltpu.VMEM((S, LANES), BF16), pad_b, pad_b] + [sub_f] * 8,
        operands=[qv, qv, qv, bias_all], name="attention_fwd")
    return (res[0].reshape(B * S, AW), res[1].reshape(B * S, AW)) + tuple(res[2:])


def _attention_bwd(qkv, do, lse, dd, bias_all, B, S, AW, bg=None):
    HP = AW // LANES
    H = AW // HEAD_DIM
    G = S // ATTN_BLOCK
    blk3 = (G, ATTN_BLOCK, LANES)
    PAD = ATTN_BLOCK

    def body(refs, bg_hook):
        (q_ref, k_ref, v_ref, do_ref, lse_ref, dd_ref, b_ref,
         dq_ref, dk_ref, dv_ref, csq_ref, csk_ref, csv_ref, db_ref,
         stage, qs, ks, vs, gs, ls, ds_, tq, tk, tv, accq, acck, accv) = refs
        bg_hook(False)
        head0 = lax.broadcasted_iota(jnp.int32, (1, 1, LANES), 2) < HEAD_DIM
        g_idx = lax.broadcasted_iota(jnp.int32, (G, 1, 1), 0)
        first_b = pl.program_id(1) == 0

        @pl.when(first_b)
        def _():
            db_ref[...] = jnp.zeros_like(db_ref)

        ks[0:PAD, :] = jnp.zeros((PAD, LANES), BF16)
        vs[0:PAD, :] = jnp.zeros((PAD, LANES), BF16)
        tk[0:PAD, :] = jnp.zeros((PAD, LANES), F32)
        tv[0:PAD, :] = jnp.zeros((PAD, LANES), F32)
        for bi, (_, d) in enumerate(DILATED_CONFIGS):
            nb = S // d // ATTN_BLOCK
            _regroup(q_ref[0], stage, qs, d, S)
            _regroup(k_ref[0], stage, ks, d, S, PAD)
            _regroup(v_ref[0], stage, vs, d, S, PAD)
            _regroup(do_ref[0], stage, gs, d, S)
            _regroup(lse_ref[0], stage, ls, d, S)
            _regroup(dd_ref[0], stage, ds_, d, S)
            q3 = qs[...].reshape(blk3) * QK_SCALE
            do3 = gs[...].reshape(blk3)
            lse3 = ls[...].reshape(blk3)
            dd3 = ds_[...].reshape(blk3)
            kk3, vv3, dead = _branch_keys(ks, vs, S, nb, g_idx)
            dq = jnp.zeros(blk3, F32)
            dkk = jnp.zeros(kk3.shape, F32)
            dvv = jnp.zeros(kk3.shape, F32)
            for e in range(2):
                msk = head0 if e == 0 else jnp.logical_not(head0)
                c0 = e * HEAD_DIM
                qe = jnp.where(msk, q3, jnp.zeros_like(q3))
                doe = jnp.where(msk, do3, jnp.zeros_like(do3))
                ke = jnp.where(msk, kk3 * QK_SCALE, jnp.zeros_like(kk3))
                s = _branch_scores(qe, kk3, b_ref, bi, e, dead)
                p = jnp.exp(s - lse3[:, :, c0:c0 + 1])
                dp = jnp.einsum("gqe,gke->gqk", doe, vv3, preferred_element_type=F32)
                dsc = p * (dp - dd3[:, :, c0:c0 + 1])
                if dead is None:
                    db_ref[bi, e, :, ATTN_BLOCK:] += jnp.sum(dsc, axis=0)
                else:
                    db_ref[bi, e] += jnp.sum(dsc, axis=0)
                dsb = dsc.astype(BF16)
                dq = dq + jnp.einsum("gqk,gke->gqe", dsb, ke, preferred_element_type=F32)
                dkk = dkk + jnp.einsum("gqk,gqe->gke", dsb, qe, preferred_element_type=F32)
                dvv = dvv + jnp.einsum("gqk,gqe->gke", p.astype(BF16), doe, preferred_element_type=F32)
            tq[...] = dq.reshape(S, LANES)
            if dead is None:
                tk[PAD:PAD + S, :] = dkk.reshape(S, LANES)
                tv[PAD:PAD + S, :] = dvv.reshape(S, LANES)
            else:
                tk[PAD:PAD + S, :] = dkk[:, ATTN_BLOCK:, :].reshape(S, LANES)
                tv[PAD:PAD + S, :] = dvv[:, ATTN_BLOCK:, :].reshape(S, LANES)
                tk[0:S, :] += dkk[:, :ATTN_BLOCK, :].reshape(S, LANES)
                tv[0:S, :] += dvv[:, :ATTN_BLOCK, :].reshape(S, LANES)
            _ungroup(tq, 0, accq, d, S, add=bi > 0)
            _ungroup(tk, PAD, acck, d, S, add=bi > 0)
            _ungroup(tv, PAD, accv, d, S, add=bi > 0)

        for acc, out_ref, cs_ref in ((accq, dq_ref, csq_ref), (acck, dk_ref, csk_ref), (accv, dv_ref, csv_ref)):
            tot = acc[...]
            out_ref[0] = tot.astype(out_ref.dtype)
            _accumulate(cs_ref, first_b, jnp.sum(tot, axis=0, keepdims=True))
        bg_hook(True)

    blk = lambda off: pl.BlockSpec((1, S, LANES), lambda h, b: (b, 0, off + h))
    cs_spec = pl.BlockSpec((1, LANES), lambda h, b: (0, h))
    bias_spec = pl.BlockSpec((3, 2, ATTN_BLOCK, 2 * ATTN_BLOCK), lambda h, b: (0, h, 0, 0))
    qv = qkv.reshape(B, S, 3 * AW)
    view = lambda t: t.reshape(B, S, AW)
    sub_b = pltpu.VMEM((S, LANES), BF16)
    sub_f = pltpu.VMEM((S, LANES), F32)
    pad_b = pltpu.VMEM((S + PAD, LANES), BF16)
    pad_f = pltpu.VMEM((S + PAD, LANES), F32)
    res = _hosted_call(
        body, bg, grid=(HP, B),
        in_specs=[blk(0), blk(HP), blk(2 * HP), blk(0), blk(0), blk(0), bias_spec],
        out_specs=[blk(0), blk(0), blk(0), cs_spec, cs_spec, cs_spec, bias_spec],
        out_shape=[jax.ShapeDtypeStruct((B, S, AW), BF16)] * 3 + [jax.ShapeDtypeStruct((1, AW), F32)] * 3
        + [jax.ShapeDtypeStruct((3, H, ATTN_BLOCK, 2 * ATTN_BLOCK), F32)],
        scratch_shapes=[sub_f, sub_b, pad_b, pad_b, sub_b, sub_f, sub_f, sub_f, pad_f, pad_f, sub_f, sub_f, sub_f],
        operands=[qv, qv, qv, view(do), view(lse), view(dd), bias_all], name="attention_bwd")
    flat = lambda t: t.reshape(B * S, AW)
    return (flat(res[0]), flat(res[1]), flat(res[2]), res[3], res[4], res[5], res[6]) + tuple(res[7:])


def _attn_norm(attn, gain, tm):
    T, AW = attn.shape

    def body(a_ref, g_ref, mix_ref, r_ref):
        a = a_ref[...]
        r = lax.rsqrt(jnp.mean(a * a, axis=-1, keepdims=True) + LN_EPS)
        mix_ref[...] = (a * r * g_ref[...]).astype(BF16)
        r_ref[...] = jnp.broadcast_to(r, (tm, LANES))

    row = pl.BlockSpec((tm, AW), lambda i: (i, 0))
    return pl.pallas_call(
        body, grid=(T // tm,), in_specs=[row, pl.BlockSpec((1, AW), lambda i: (0, 0))],
        out_specs=[row, pl.BlockSpec((tm, LANES), lambda i: (i, 0))],
        out_shape=[jax.ShapeDtypeStruct((T, AW), BF16), jax.ShapeDtypeStruct((T, LANES), F32)],
        compiler_params=_params(1), name="attn_norm",
    )(attn, gain)


def _attn_pre_bwd(dmixed, attn, rstd, gain, tm):
    T, AW = attn.shape
    ones_np = np.kron(np.eye(AW // HEAD_DIM, dtype=np.float32), np.ones((HEAD_DIM, HEAD_DIM), np.float32))
    ones_bd = jnp.asarray(ones_np, dtype=BF16)

    def body(dm_ref, a_ref, r_ref, g_ref, ones_ref, do_ref, dd_ref, dg_ref):
        i = pl.program_id(0)
        dm = dm_ref[...]
        a = a_ref[...]
        r = r_ref[:, 0:1]
        dxn = dm * g_ref[...]
        da = r * (dxn - a * (r * r) * jnp.mean(dxn * a, axis=-1, keepdims=True))
        do_ref[...] = da.astype(BF16)
        hi, lo = _split_hi_lo(da * a)
        dd_ref[...] = (jnp.dot(hi, ones_ref[...], preferred_element_type=F32)
                       + jnp.dot(lo, ones_ref[...], preferred_element_type=F32))
        _accumulate(dg_ref, i == 0, jnp.sum(dm * a * r, axis=0, keepdims=True))

    row = pl.BlockSpec((tm, AW), lambda i: (i, 0))
    vec = pl.BlockSpec((1, AW), lambda i: (0, 0))
    return pl.pallas_call(
        body, grid=(T // tm,),
        in_specs=[row, row, pl.BlockSpec((tm, LANES), lambda i: (i, 0)), vec,
                  pl.BlockSpec((AW, AW), lambda i: (0, 0))],
        out_specs=[row, row, vec],
        out_shape=[jax.ShapeDtypeStruct((T, AW), BF16), jax.ShapeDtypeStruct((T, AW), F32),
                   jax.ShapeDtypeStruct((1, AW), F32)],
        compiler_params=_params(1), name="attn_pre_bwd",
    )(dmixed, attn, rstd, gain, ones_bd)


class _RowShifts:
    def __init__(self, x, row, up):
        self.x, self.row, self.up, self.base = x, row, up, {0: x}

    def __call__(self, s):
        x = self.x
        n, c = x.shape
        r, whole = s % 8, s - s % 8
        if r not in self.base:
            if self.up:
                rolled = pltpu.roll(x, n - r, 0)
                tail = jnp.where(self.row[n - 8:] < n - r, rolled[n - 8:], 0.0)
                self.base[r] = jnp.concatenate([rolled[:n - 8], tail], axis=0)
            else:
                rolled = pltpu.roll(x, r, 0)
                head = jnp.where(self.row[:8] >= r, rolled[:8], 0.0)
                self.base[r] = jnp.concatenate([head, rolled[8:]], axis=0)
        y = self.base[r]
        if whole == 0:
            return y
        pad = jnp.zeros((whole, c), x.dtype)
        if self.up:
            return jnp.concatenate([y[whole:], pad], axis=0)
        return jnp.concatenate([pad, y[:n - whole]], axis=0)


def _conv_branch_fwd_math(a, g, w_ref, cb, lg, lb, row):
    sg = _sigmoid(g)
    u0 = a * sg
    u0_down = _RowShifts(u0, row, up=False)
    uc = jnp.zeros_like(u0) + cb
    for k in range(CONV_KERNEL):
        uc = uc + w_ref[k:k + 1, :] * u0_down(CONV_KERNEL - 1 - k)
    ul, xh, r = _ln_fwd(uc, lg, lb)
    su = _sigmoid(ul)
    u = ul * su
    return sg, u0_down, ul, xh, r, su, u


def _conv_fwd(ag, conv_w, conv_b, ln_g, ln_b, norm_g, B, S, CW):
    def body(a_ref, g_ref, w_ref, cb_ref, lg_ref, lb_ref, ng_ref, o_ref):
        row = lax.broadcasted_iota(jnp.int32, (S, CW), 0)
        _, _, _, _, _, _, u = _conv_branch_fwd_math(a_ref[0], g_ref[0], w_ref, cb_ref[...], lg_ref[...],
                                                    lb_ref[...], row)
        rr = lax.rsqrt(jnp.mean(u * u, axis=-1, keepdims=True) + LN_EPS)
        o_ref[0] = (u * rr * ng_ref[...]).astype(BF16)

    vec = pl.BlockSpec((1, CW), lambda b: (0, 0))
    out = pl.pallas_call(
        body, grid=(B,),
        in_specs=[pl.BlockSpec((1, S, CW), lambda b: (b, 0, 0)), pl.BlockSpec((1, S, CW), lambda b: (b, 0, 1)),
                  pl.BlockSpec((CONV_KERNEL, CW), lambda b: (0, 0)), vec, vec, vec, vec],
        out_specs=pl.BlockSpec((1, S, CW), lambda b: (b, 0, 0)),
        out_shape=jax.ShapeDtypeStruct((B, S, CW), BF16),
        compiler_params=_params(1), name="conv_fwd",
    )(ag.reshape(B, S, 2 * CW), ag.reshape(B, S, 2 * CW), conv_w, conv_b, ln_g, ln_b, norm_g)
    return out.reshape(B * S, CW)


def _conv_bwd(ag, dmc, conv_w, conv_b, ln_g, ln_b, norm_g, B, S, CW):
    def body(a_ref, g_ref, dm_ref, w_ref, cb_ref, lg_ref, lb_ref, ng_ref,
             dag_ref, dw_ref, dcb_ref, dlg_ref, dlb_ref, dng_ref):
        b = pl.program_id(0)
        row = lax.broadcasted_iota(jnp.int32, (S, CW), 0)
        a, g = a_ref[0], g_ref[0]
        sg, u0_down, ul, xh, r, su, u = _conv_branch_fwd_math(a, g, w_ref, cb_ref[...], lg_ref[...], lb_ref[...], row)
        rr = lax.rsqrt(jnp.mean(u * u, axis=-1, keepdims=True) + LN_EPS)
        dm = dm_ref[0]
        dxn = dm * ng_ref[...]
        du = rr * (dxn - u * (rr * rr) * jnp.mean(dxn * u, axis=-1, keepdims=True))
        dul = du * su * (1.0 + ul * (1.0 - su))
        duc = _ln_bwd(dul, xh, r, lg_ref[...])
        first = b == 0
        _accumulate(dng_ref, first, jnp.sum(dm * u * rr, axis=0, keepdims=True))
        _accumulate(dlg_ref, first, jnp.sum(dul * xh, axis=0, keepdims=True))
        _accumulate(dlb_ref, first, jnp.sum(dul, axis=0, keepdims=True))
        _accumulate(dcb_ref, first, jnp.sum(duc, axis=0, keepdims=True))

        @pl.when(first)
        def _():
            dw_ref[...] = jnp.zeros_like(dw_ref)

        duc_up = _RowShifts(duc, row, up=True)
        du0 = jnp.zeros_like(duc)
        for k in range(CONV_KERNEL):
            sh = CONV_KERNEL - 1 - k
            dw_ref[k:k + 1, :] += jnp.sum(duc * u0_down(sh), axis=0, keepdims=True)
            du0 = du0 + w_ref[k:k + 1, :] * duc_up(sh)
        dag_ref[0, :, :CW] = du0 * sg
        dag_ref[0, :, CW:] = du0 * a * sg * (1.0 - sg)

    vec = pl.BlockSpec((1, CW), lambda b: (0, 0))
    wspec = pl.BlockSpec((CONV_KERNEL, CW), lambda b: (0, 0))
    agv = ag.reshape(B, S, 2 * CW)
    res = pl.pallas_call(
        body, grid=(B,),
        in_specs=[pl.BlockSpec((1, S, CW), lambda b: (b, 0, 0)), pl.BlockSpec((1, S, CW), lambda b: (b, 0, 1)),
                  pl.BlockSpec((1, S, CW), lambda b: (b, 0, 0)), wspec, vec, vec, vec, vec],
        out_specs=[pl.BlockSpec((1, S, 2 * CW), lambda b: (b, 0, 0)), wspec, vec, vec, vec, vec],
        out_shape=[jax.ShapeDtypeStruct((B, S, 2 * CW), F32), jax.ShapeDtypeStruct((CONV_KERNEL, CW), F32)]
        + [jax.ShapeDtypeStruct((1, CW), F32)] * 4,
        compiler_params=_params(1), name="conv_bwd",
    )(agv, agv, dmc.reshape(B, S, CW), conv_w, conv_b, ln_g, ln_b, norm_g)
    return (res[0].reshape(B * S, 2 * CW),) + tuple(res[1:])


def _ffn_conv(x, w_ref, bias, row):
    down = x if isinstance(x, _RowShifts) else _RowShifts(x, row, up=False)
    y = jnp.zeros_like(down.x) + bias
    for k in range(FFN_CONV_KERNEL):
        y = y + w_ref[k:k + 1, :] * down(FFN_CONV_KERNEL - 1 - k)
    return y


def _ffn_specs(S, tc, nj, order):
    pick = (lambda b, j: (b, j)) if order == "bj" else (lambda j, b: (b, j))
    act = lambda off: pl.BlockSpec((1, S, tc), lambda *g: (pick(*g)[0], 0, off + pick(*g)[1]))
    cw = lambda off: pl.BlockSpec((FFN_CONV_KERNEL, tc), lambda *g: (0, off + pick(*g)[1]))
    cb = lambda off: pl.BlockSpec((1, tc), lambda *g: (0, off + pick(*g)[1]))
    return act, cw, cb


def _ffn_act(upre, cw, cb, B, S, DFF):
    tc = FFN_COLS
    nj = DFF // tc

    def body(ug_ref, uv_ref, wg_ref, wv_ref, bg_ref, bv_ref, o_ref):
        row = lax.broadcasted_iota(jnp.int32, (S, tc), 0)
        gate = _ffn_conv(ug_ref[0], wg_ref, bg_ref[...], row)
        val = _ffn_conv(uv_ref[0], wv_ref, bv_ref[...], row)
        o_ref[0] = (gate * _sigmoid(gate) * val).astype(BF16)

    act, cws, cbs = _ffn_specs(S, tc, nj, "bj")
    uv = upre.reshape(B, S, 2 * DFF)
    out = pl.pallas_call(
        body, grid=(B, nj), in_specs=[act(0), act(nj), cws(0), cws(nj), cbs(0), cbs(nj)], out_specs=act(0),
        out_shape=jax.ShapeDtypeStruct((B, S, DFF), BF16), compiler_params=_params(2), name="ffn_act",
    )(uv, uv, cw, cw, cb, cb)
    return out.reshape(B * S, DFF)


def _ffn_bwd(upre, dact, cw, cb, B, S, DFF):
    tc = FFN_COLS
    nj = DFF // tc

    def body(ug_ref, uv_ref, da_ref, wg_ref, wv_ref, bg_ref, bv_ref, dug_ref, duv_ref, dwg_ref, dwv_ref,
             dbg_ref, dbv_ref):
        first = pl.program_id(1) == 0
        row = lax.broadcasted_iota(jnp.int32, (S, tc), 0)
        ug, uv = ug_ref[0], uv_ref[0]
        gate = _ffn_conv(ug, wg_ref, bg_ref[...], row)
        val = _ffn_conv(uv, wv_ref, bv_ref[...], row)
        sg = _sigmoid(gate)
        dact_b = da_ref[0]
        dgate = dact_b * val * sg * (1.0 + gate * (1.0 - sg))
        dval = dact_b * gate * sg
        for dup, u, w_ref, du_ref, dw_ref, db_ref in ((dgate, ug, wg_ref, dug_ref, dwg_ref, dbg_ref),
                                                      (dval, uv, wv_ref, duv_ref, dwv_ref, dbv_ref)):
            _accumulate(db_ref, first, jnp.sum(dup, axis=0, keepdims=True))

            @pl.when(first)
            def _(dw_ref=dw_ref):
                dw_ref[...] = jnp.zeros_like(dw_ref)

            dupre = jnp.zeros_like(dup)
            for k in range(FFN_CONV_KERNEL):
                sh = FFN_CONV_KERNEL - 1 - k
                dw_ref[k:k + 1, :] += jnp.sum(dup * _shift_down(u, sh, row), axis=0, keepdims=True)
                dupre = dupre + w_ref[k:k + 1, :] * _shift_up(dup, sh, row)
            du_ref[0] = dupre.astype(BF16)

    act, cws, cbs = _ffn_specs(S, tc, nj, "jb")
    uv = upre.reshape(B, S, 2 * DFF)
    res = pl.pallas_call(
        body, grid=(nj, B),
        in_specs=[act(0), act(nj), act(0), cws(0), cws(nj), cbs(0), cbs(nj)],
        out_specs=[act(0), act(0), cws(0), cws(0), cbs(0), cbs(0)],
        out_shape=[jax.ShapeDtypeStruct((B, S, DFF), BF16)] * 2
        + [jax.ShapeDtypeStruct((FFN_CONV_KERNEL, DFF), F32)] * 2 + [jax.ShapeDtypeStruct((1, DFF), F32)] * 2,
        compiler_params=_params(2), name="ffn_bwd",
    )(uv, uv, dact.reshape(B, S, DFF), cw, cw, cb, cb)
    flat = lambda t: t.reshape(B * S, DFF)
    return (flat(res[0]), flat(res[1]), jnp.concatenate([res[2], res[3]], axis=1),
            jnp.concatenate([res[4], res[5]], axis=1))


FFN_HALO = 16


def _half_sequences(S):
    if S < 8 * FFN_HALO:
        return [(0, S, 0, S)]
    h = S // 2
    return [(0, h + FFN_HALO, 0, h), (h - FFN_HALO, S, FFN_HALO, h)]


def _w_up_block_spec(w_up_sh, tc, off):
    _, D, cs = w_up_sh.shape
    assert cs % tc == 0
    bps = cs // tc
    return pl.BlockSpec((1, D, tc), lambda j: ((off + j) // bps, 0, (off + j) % bps))


def _ffn_fwd_fused(x1b, w_up_sh, cw, cb, B, S, DFF):
    tc = FFN_COLS
    nj = DFF // tc
    D = x1b.shape[1]

    def body(x_ref, wg_ref, wv_ref, cwg_ref, cwv_ref, cbg_ref, cbv_ref, o_ref, up_ref):
        w = jnp.concatenate([wg_ref[0], wv_ref[0]], axis=1)
        for b in range(B):
            for lo, hi, o0, on in _half_sequences(S):
                row = lax.broadcasted_iota(jnp.int32, (hi - lo, tc), 0)
                up = jnp.dot(x_ref[b, lo:hi, :], w, preferred_element_type=F32)
                up_ref[b, lo + o0:lo + o0 + on, :] = up[o0:o0 + on]
                gate = _ffn_conv(up[:, :tc], cwg_ref, cbg_ref[...], row)
                val = _ffn_conv(up[:, tc:], cwv_ref, cbv_ref[...], row)
                o_ref[b, lo + o0:lo + o0 + on, :] = (gate * _sigmoid(gate) * val).astype(BF16)[o0:o0 + on]

    cws = lambda off: pl.BlockSpec((FFN_CONV_KERNEL, tc), lambda j: (0, off + j))
    cbs = lambda off: pl.BlockSpec((1, tc), lambda j: (0, off + j))
    act, upre = pl.pallas_call(
        body, grid=(nj,),
        in_specs=[pl.BlockSpec((B, S, D), lambda j: (0, 0, 0), pipeline_mode=pl.Buffered(1)),
                  _w_up_block_spec(w_up_sh, tc, 0), _w_up_block_spec(w_up_sh, tc, nj),
                  cws(0), cws(nj), cbs(0), cbs(nj)],
        out_specs=[pl.BlockSpec((B, S, tc), lambda j: (0, 0, j)), pl.BlockSpec((B, S, 2 * tc), lambda j: (0, 0, j))],
        out_shape=[jax.ShapeDtypeStruct((B, S, DFF), BF16), jax.ShapeDtypeStruct((B, S, 2 * DFF), F32)],
        compiler_params=_params(1), name="ffn_fwd",
    )(x1b.reshape(B, S, D), w_up_sh, w_up_sh, cw, cw, cb, cb)
    return act.reshape(B * S, DFF), upre


def _ffn_bwd_fused(x1b, dz2b, upre, w_down, cw, cb, B, S, DFF):
    tc = FFN_COLS
    nj = DFF // tc
    D = x1b.shape[1]

    def body(x_ref, dz_ref, up_ref, wd_ref, cwg_ref, cwv_ref, cbg_ref, cbv_ref,
             dug_ref, duv_ref, dwu_ref, dwd_ref, dcw_ref, dcb_ref):
        first = pl.program_id(1) == 0
        dw_t = dwd = None
        dcb = [None, None]
        dcw = [[None] * FFN_CONV_KERNEL, [None] * FFN_CONV_KERNEL]
        add = lambda old, new: new if old is None else old + new
        for lo, hi, o0, on in _half_sequences(S):
            n = hi - lo
            own = slice(o0, o0 + on)
            row = lax.broadcasted_iota(jnp.int32, (n, tc), 0)
            x = x_ref[0, lo:hi, :]
            dz = dz_ref[0, lo:hi, :]
            ug = _RowShifts(up_ref[0, lo:hi, :tc], row, up=False)
            uv = _RowShifts(up_ref[0, lo:hi, tc:], row, up=False)
            gate = _ffn_conv(ug, cwg_ref, cbg_ref[...], row)
            val = _ffn_conv(uv, cwv_ref, cbv_ref[...], row)
            sg = _sigmoid(gate)
            act = (gate * sg * val).astype(BF16)
            dact = _dot(dz, wd_ref[...], "nt")
            dgate = dact * val * sg * (1.0 + gate * (1.0 - sg))
            dval = dact * gate * sg
            dupre = []
            for h, (dup, u_down, w_ref) in enumerate(((dgate, ug, cwg_ref), (dval, uv, cwv_ref))):
                dcb[h] = add(dcb[h], jnp.sum(dup[own], axis=0, keepdims=True))
                dup_up = _RowShifts(dup, row, up=True)
                acc = jnp.zeros_like(dup)
                for k in range(FFN_CONV_KERNEL):
                    sh = FFN_CONV_KERNEL - 1 - k
                    dcw[h][k] = add(dcw[h][k], jnp.sum((dup * u_down(sh))[own], axis=0, keepdims=True))
                    acc = acc + w_ref[k:k + 1, :] * dup_up(sh)
                dupre.append(acc.astype(BF16)[own])
            dug_ref[0, lo + o0:lo + o0 + on, :] = dupre[0]
            duv_ref[0, lo + o0:lo + o0 + on, :] = dupre[1]
            dw_t = add(dw_t, _dot(jnp.concatenate(dupre, axis=1), x[own], "tn"))
            dwd = add(dwd, _dot(act[own], dz[own], "tn"))
        _accumulate(dwu_ref.at[0], first, dw_t[:tc])
        _accumulate(dwu_ref.at[1], first, dw_t[tc:])
        _accumulate(dwd_ref, first, dwd)
        for h in range(2):
            _accumulate(dcb_ref.at[h], first, dcb[h])
            for k in range(FFN_CONV_KERNEL):
                _accumulate(dcw_ref.at[k, pl.ds(h, 1), :], first, dcw[h][k])

    act_s, cws, cbs = _ffn_specs(S, tc, nj, "jb")
    seq = pl.BlockSpec((1, S, D), lambda j, b: (b, 0, 0))
    res = pl.pallas_call(
        body, grid=(nj, B),
        in_specs=[seq, seq, pl.BlockSpec((1, S, 2 * tc), lambda j, b: (b, 0, j)),
                  pl.BlockSpec((tc, D), lambda j, b: (j, 0)), cws(0), cws(nj), cbs(0), cbs(nj)],
        out_specs=[act_s(0), act_s(0), pl.BlockSpec((2, tc, D), lambda j, b: (0, j, 0)),
                   pl.BlockSpec((tc, D), lambda j, b: (j, 0)),
                   pl.BlockSpec((FFN_CONV_KERNEL, 2, tc), lambda j, b: (0, 0, j)),
                   pl.BlockSpec((2, 1, tc), lambda j, b: (0, 0, j))],
        out_shape=[jax.ShapeDtypeStruct((B, S, DFF), BF16)] * 2
        + [jax.ShapeDtypeStruct((2, DFF, D), F32), jax.ShapeDtypeStruct((DFF, D), F32),
           jax.ShapeDtypeStruct((FFN_CONV_KERNEL, 2, DFF), F32), jax.ShapeDtypeStruct((2, 1, DFF), F32)],
        compiler_params=_params(2), name="ffn_bwd",
    )(x1b.reshape(B, S, D), dz2b.reshape(B, S, D), upre, w_down, cw, cw, cb, cb)
    flat = lambda t: t.reshape(B * S, DFF)
    return flat(res[0]), flat(res[1]), res[2], res[3], res[4], res[5]


def _ffn_bwd_seq(b, x1b3, dz2b3, upre, w_up_sh, w_down, cw, cb, prev, S, DFF):
    tc = FFN_COLS
    nj = DFF // tc
    B, _, D = x1b3.shape
    n_prev = 0 if prev is None else 5

    def body(*refs):
        (x_ref, dz_ref, up_ref, wg_ref, wv_ref, wd_ref, cwg_ref, cwv_ref, cbg_ref, cbv_ref) = refs[:10]
        prev_refs = refs[10:10 + n_prev]
        dx_hbm, dwu_ref, dwd_ref, dcw_ref, dcb_ref, acc_ref, sem = refs[10 + n_prev:]
        j = pl.program_id(0)

        @pl.when(j == 0)
        def _():
            acc_ref[...] = jnp.zeros_like(acc_ref)

        wcat = jnp.concatenate([wg_ref[0], wv_ref[0]], axis=1)
        dw_t = dwd = None
        dcb = [None, None]
        dcw = [[None] * FFN_CONV_KERNEL, [None] * FFN_CONV_KERNEL]
        add = lambda old, new: new if old is None else old + new
        for lo, hi, o0, on in _half_sequences(S):
            n = hi - lo
            own = slice(o0, o0 + on)
            row = lax.broadcasted_iota(jnp.int32, (n, tc), 0)
            x = x_ref[0, lo:hi, :]
            dz = dz_ref[0, lo:hi, :]
            ug, uv = up_ref[0, lo:hi, :tc], up_ref[0, lo:hi, tc:]
            gate = _ffn_conv(ug, cwg_ref, cbg_ref[...], row)
            val = _ffn_conv(uv, cwv_ref, cbv_ref[...], row)
            sg = _sigmoid(gate)
            act = (gate * sg * val).astype(BF16)
            dact = _dot(dz, wd_ref[...], "nt")
            dgate = dact * val * sg * (1.0 + gate * (1.0 - sg))
            dval = dact * gate * sg
            dupre = []
            for h, (dup, u, w_ref) in enumerate(((dgate, ug, cwg_ref), (dval, uv, cwv_ref))):
                dcb[h] = add(dcb[h], jnp.sum(dup[own], axis=0, keepdims=True))
                acc = jnp.zeros_like(dup)
                for k in range(FFN_CONV_KERNEL):
                    sh = FFN_CONV_KERNEL - 1 - k
                    dcw[h][k] = add(dcw[h][k], jnp.sum((dup * _shift_down(u, sh, row))[own], axis=0, keepdims=True))
                    acc = acc + w_ref[k:k + 1, :] * _shift_up(dup, sh, row)
                dupre.append(acc.astype(BF16)[own])
            dupre_cat = jnp.concatenate(dupre, axis=1)
            dw_t = add(dw_t, _dot(dupre_cat, x[own], "tn"))
            dwd = add(dwd, _dot(act[own], dz[own], "tn"))
            acc_ref[lo + o0:lo + o0 + on, :] += _dot(dupre_cat, wcat, "nt")
        if n_prev:
            _, pwu_ref, pwd_ref, pcw_ref, pcb_ref = prev_refs
            dwu_ref[0] = pwu_ref[0] + dw_t[:tc]
            dwu_ref[1] = pwu_ref[1] + dw_t[tc:]
            dwd_ref[...] = pwd_ref[...] + dwd
        else:
            dwu_ref[0] = dw_t[:tc]
            dwu_ref[1] = dw_t[tc:]
            dwd_ref[...] = dwd
        for h in range(2):
            dcb_ref[h] = dcb[h] + pcb_ref[h] if n_prev else dcb[h]
            for k in range(FFN_CONV_KERNEL):
                dcw_ref[k, h:h + 1, :] = dcw[h][k] + pcw_ref[k, h:h + 1, :] if n_prev else dcw[h][k]

        @pl.when(j == nj - 1)
        def _():
            out = pltpu.make_async_copy(acc_ref, dx_hbm.at[b], sem)
            out.start()
            out.wait()

    bps = w_up_sh.shape[2] // tc
    wspec = lambda off: pl.BlockSpec((1, D, tc), lambda j: ((off + j) // bps, 0, (off + j) % bps))
    cws = lambda off: pl.BlockSpec((FFN_CONV_KERNEL, tc), lambda j: (0, off + j))
    cbs = lambda off: pl.BlockSpec((1, tc), lambda j: (0, off + j))
    seq = pl.BlockSpec((1, S, D), lambda j: (b, 0, 0), pipeline_mode=pl.Buffered(1))
    part_specs = [pl.BlockSpec((2, tc, D), lambda j: (0, j, 0)), pl.BlockSpec((tc, D), lambda j: (j, 0)),
                  pl.BlockSpec((FFN_CONV_KERNEL, 2, tc), lambda j: (0, 0, j)), pl.BlockSpec((2, 1, tc), lambda j: (0, 0, j))]
    part_shapes = [jax.ShapeDtypeStruct((2, DFF, D), F32), jax.ShapeDtypeStruct((DFF, D), F32),
                   jax.ShapeDtypeStruct((FFN_CONV_KERNEL, 2, DFF), F32), jax.ShapeDtypeStruct((2, 1, DFF), F32)]
    in_specs = [seq, seq, pl.BlockSpec((1, S, 2 * tc), lambda j: (b, 0, j)), wspec(0), wspec(nj),
                pl.BlockSpec((tc, D), lambda j: (j, 0)), cws(0), cws(nj), cbs(0), cbs(nj)]
    operands = [x1b3, dz2b3, upre, w_up_sh, w_up_sh, w_down, cw, cw, cb, cb]
    aliases = {}
    if n_prev:
        in_specs += [HBM_SPEC] + part_specs
        operands += list(prev)
        aliases = {10 + i: i for i in range(5)}
    return pl.pallas_call(
        body, grid=(nj,), in_specs=in_specs, out_specs=[HBM_SPEC] + part_specs,
        out_shape=[jax.ShapeDtypeStruct((B, S, D), F32)] + part_shapes, input_output_aliases=aliases,
        scratch_shapes=[pltpu.VMEM((S, D), F32), pltpu.SemaphoreType.DMA],
        compiler_params=_params(1), name="ffn_bwd_seq%d" % b,
    )(*operands)


def _dx1_ln1_bwd(dupre_g, dupre_v, w_up_sh, dz2, xh1, r1, ln1_g, tm, bg):
    T, D = dz2.shape
    NS, _, cs = w_up_sh.shape
    half = NS // 2
    DFF = dupre_g.shape[1]

    def body(refs, bg_hook):
        dug_ref, duv_ref, w_ref, dz2_ref, xh_ref, r_ref, g_ref, dz_ref, dzb_ref, dg_ref, db_ref = refs
        bg_hook(False)
        first = pl.program_id(0) == 0
        acc = ALPHA * dz2_ref[...]
        for k in range(NS):
            src = dug_ref if k < half else duv_ref
            c0 = (k % half) * cs
            acc = acc + _dot(src[:, c0:c0 + cs], w_ref[k], "nt")
        dx1 = acc
        xh = xh_ref[...]
        dz = _ln_bwd(dx1, xh, r_ref[:, 0:1], g_ref[...])
        dz_ref[...] = dz
        dzb_ref[...] = dz.astype(BF16)
        _accumulate(dg_ref, first, jnp.sum(dx1 * xh, axis=0, keepdims=True))
        _accumulate(db_ref, first, jnp.sum(dx1, axis=0, keepdims=True))
        bg_hook(True)

    row = pl.BlockSpec((tm, D), lambda i: (i, 0))
    vec = pl.BlockSpec((1, D), lambda i: (0, 0))
    du = pl.BlockSpec((tm, DFF), lambda i: (i, 0))
    return _hosted_call(
        body, bg, grid=(T // tm,),
        in_specs=[du, du, pl.BlockSpec((NS, D, cs), lambda i: (0, 0, 0), pipeline_mode=pl.Buffered(1)),
                  row, row, pl.BlockSpec((tm, LANES), lambda i: (i, 0)), vec],
        out_specs=[row, row, vec, vec],
        out_shape=[jax.ShapeDtypeStruct((T, D), F32), jax.ShapeDtypeStruct((T, D), BF16),
                   jax.ShapeDtypeStruct((1, D), F32), jax.ShapeDtypeStruct((1, D), F32)],
        scratch_shapes=[], operands=[dupre_g, dupre_v, w_up_sh, dz2, xh1, r1, ln1_g], name="mm_dx1_ln1_bwd")


def _transpose(x, name):
    R, C = x.shape
    tr = LANES if R % LANES == 0 else R

    def body(x_ref, o_ref):
        o_ref[...] = x_ref[...].T

    return pl.pallas_call(
        body, grid=(R // tr,), in_specs=[pl.BlockSpec((tr, C), lambda i: (i, 0))],
        out_specs=pl.BlockSpec((C, tr), lambda i: (0, i)), out_shape=jax.ShapeDtypeStruct((C, R), F32),
        compiler_params=_params(1), name=name)(x)


def _dh_cat(dq, dk, dv, dag, tm):
    T, AW = dq.shape
    CW2 = dag.shape[1]
    W = 3 * AW + CW2

    def body(dq_ref, dk_ref, dv_ref, dag_ref, dh_ref, cs_ref):
        for c, ref in enumerate((dq_ref, dk_ref, dv_ref)):
            dh_ref[:, c * AW:(c + 1) * AW] = ref[...]
        dg = dag_ref[...]
        dh_ref[:, 3 * AW:] = dg.astype(BF16)
        _accumulate(cs_ref, pl.program_id(0) == 0, jnp.sum(dg, axis=0, keepdims=True))

    row = pl.BlockSpec((tm, AW), lambda i: (i, 0))
    return pl.pallas_call(
        body, grid=(T // tm,),
        in_specs=[row] * 3 + [pl.BlockSpec((tm, CW2), lambda i: (i, 0))],
        out_specs=[pl.BlockSpec((tm, W), lambda i: (i, 0)), pl.BlockSpec((1, CW2), lambda i: (0, 0))],
        out_shape=[jax.ShapeDtypeStruct((T, W), BF16), jax.ShapeDtypeStruct((1, CW2), F32)],
        compiler_params=_params(1), name="dh_cat",
    )(dq, dk, dv, dag)


def _local_step(x, target, rel_table, w_in_t, b_in, conv_w, conv_b, conv_ln_g, conv_ln_b, attn_norm_g,
                conv_norm_g, staged, ln1_g, ln1_b, ffn_cw, ffn_cb, ln2_g, ln2_b, ids):
    B, S, D = x.shape
    T = B * S
    AW = attn_norm_g.shape[-1]
    CW = conv_norm_g.shape[-1]
    H = AW // HEAD_DIM
    DFF = staged[2].shape[0] * staged[2].shape[1]
    INW = 3 * AW + 2 * CW
    xf = x.reshape(T, D)
    tf = target.reshape(T, D)
    tm = _row_tile(T, 512)
    tm_s = _row_tile(T, 256)

    bucket_np, mask_np = _bucket_tables()
    bucket = jnp.asarray(bucket_np)
    band_mask = jnp.asarray(mask_np)
    bias_all = _bias_build(rel_table.T, bucket, band_mask).reshape(3, H, ATTN_BLOCK, 2 * ATTN_BLOCK)

    def in_proj(n0, n, tn, rows, out_dtype, name):
        assert n0 % tn == 0 and n % tn == 0

        def epilogue(acc, i, j, extra_refs, out_refs):
            out_refs[0][...] = (acc + extra_refs[0][...]).astype(out_dtype)

        return _matmul_general(
            [(xf, (rows, D), lambda i, j, k: (i, 0)), (w_in_t, (tn, D), lambda i, j, k: (n0 // tn + j, 0))],
            lambda refs, i, j, k: _dot(refs[0][...], refs[1][...], "nt"),
            grid=(T // rows, n // tn, 1), tm=rows, tn=tn,
            extras=[(b_in, (1, tn), lambda i, j, k: (0, n0 // tn + j))],
            outs=[_plain_out(T, n, rows, tn, out_dtype)], epilogue=epilogue, name=name)[0]

    qkv = in_proj(0, 3 * AW, _col_tile(3 * AW, 1152), tm, BF16, "mm_qkv")
    ag = in_proj(3 * AW, 2 * CW, math.gcd(3 * AW, 2 * CW), _row_tile(T, 1024), F32, "mm_ag")

    attn, lse, w_out_g, w_up_sh, w_down_g = _attention_fwd(qkv, bias_all, B, S, AW, bg=_bg_gather(staged))
    w_out = w_out_g.reshape(D, D)
    w_down = w_down_g.reshape(DFF, D)
    mixed_c = _conv_fwd(ag, conv_w, conv_b, conv_ln_g, conv_ln_b, conv_norm_g, B, S, CW)

    def attn_rstd(a):
        return lax.rsqrt(jnp.mean(a * a, axis=-1, keepdims=True) + LN_EPS)

    def mixed_rows(attn_ref, mc_ref, gain_ref):
        a = attn_ref[...]
        return jnp.concatenate([(a * attn_rstd(a) * gain_ref[...]).astype(BF16), mc_ref[...]], axis=1)

    def ln1_epilogue(acc, i, j, extra_refs, out_refs):
        x_ref, g_ref, b_ref, a_ref = extra_refs
        x1, xh, r = _ln_fwd(acc + ALPHA * x_ref[...], g_ref[...], b_ref[...])
        out_refs[0][...] = x1
        out_refs[1][...] = x1.astype(BF16)
        out_refs[2][...] = xh
        out_refs[3][...] = jnp.broadcast_to(r, (tm_s, LANES))
        out_refs[4][...] = jnp.broadcast_to(attn_rstd(a_ref[...]), (tm_s, LANES))

    rowD = lambda i, j, k: (i, 0)
    vecD = lambda i, j, k: (0, 0)
    x1, x1b, xh1, r1, r_attn = _matmul_general(
        [(attn, (tm_s, AW), rowD), (mixed_c, (tm_s, CW), rowD), (attn_norm_g, (1, AW), vecD), (w_out, (D, D), vecD)],
        lambda refs, i, j, k: _dot(mixed_rows(refs[0], refs[1], refs[2]), refs[3][...], "nn"),
        grid=(T // tm_s, 1, 1), tm=tm_s, tn=D,
        extras=[(xf, (tm_s, D), rowD), (ln1_g, (1, D), vecD), (ln1_b, (1, D), vecD), (attn, (tm_s, AW), rowD)],
        outs=[((T, D), F32, (tm_s, D), rowD), ((T, D), BF16, (tm_s, D), rowD), ((T, D), F32, (tm_s, D), rowD),
              ((T, LANES), F32, (tm_s, LANES), rowD), ((T, LANES), F32, (tm_s, LANES), rowD)],
        epilogue=ln1_epilogue, name="mm_out_ln1")

    NS, _, cs = w_up_sh.shape
    half = NS // 2

    act, upre = _ffn_fwd_fused(x1b, w_up_sh, ffn_cw, ffn_cb, B, S, DFF)

    def ln2_epilogue(acc, i, j, extra_refs, out_refs):
        x1_ref, g_ref, b_ref, t_ref = extra_refs
        dz_ref, dzb_ref, loss_ref, dg_ref, db_ref = out_refs
        g = g_ref[...]
        y, xh, r = _ln_fwd(acc + ALPHA * x1_ref[...], g, b_ref[...])
        diff = y - t_ref[...]
        row_loss = jnp.sum(diff * diff, axis=1, keepdims=True)
        tile_loss = jnp.sum(row_loss, axis=0, keepdims=True) * (0.5 / D)
        dy = diff * (1.0 / D)
        dz = _ln_bwd(dy, xh, r, g)
        dz_ref[...] = dz
        dzb_ref[...] = dz.astype(BF16)
        first = i == 0
        _accumulate(loss_ref, first, jnp.broadcast_to(tile_loss, (1, LANES)))
        _accumulate(dg_ref, first, jnp.sum(dy * xh, axis=0, keepdims=True))
        _accumulate(db_ref, first, jnp.sum(dy, axis=0, keepdims=True))

    dz2, dz2b, loss_part, d_ln2_g, d_ln2_b = _matmul(
        act, w_down, mode="nn", tm=tm, tn=D, tk=DFF,
        extras=[(x1, (tm, D), rowD), (ln2_g, (1, D), vecD), (ln2_b, (1, D), vecD), (tf, (tm, D), rowD)],
        outs=[((T, D), F32, (tm, D), rowD), ((T, D), BF16, (tm, D), rowD),
              ((1, LANES), F32, (1, LANES), vecD), ((1, D), F32, (1, D), vecD), ((1, D), F32, (1, D), vecD)],
        epilogue=ln2_epilogue, name="mm_down_ln2_loss")

    dupre_g, dupre_v, d_w_up_t, d_w_down, d_ffn_cw2, d_ffn_cb2 = _ffn_bwd_fused(
        x1b, dz2b, upre, w_down, ffn_cw, ffn_cb, B, S, DFF)
    d_w_up_t = d_w_up_t.reshape(NS, cs, D)
    d_ffn_cw = d_ffn_cw2.reshape(FFN_CONV_KERNEL, 2 * DFF)
    d_ffn_cb = d_ffn_cb2.reshape(1, 2 * DFF)
    tk_t = _row_tile(T, 512)

    early = [d_w_up_t, d_w_down.reshape(NS, DFF // NS, D)]
    dz1, dz1b, d_ln1_g, d_ln1_b, *sib_e = _dx1_ln1_bwd(dupre_g, dupre_v, w_up_sh, dz2, xh1, r1, ln1_g, tm,
                                                       bg=_bg_sibling_exchange(early))
    chip_e = [_pair_sum(g, s, ids, name="pair_sum_" + n) for g, s, n in zip(early, sib_e, ("w_up", "w_down"))]

    def dw_out_epilogue(acc, i, j, extra_refs, out_refs):
        out_refs[0][...] = acc

    d_w_out = _matmul_general(
        [(attn, (tk_t, AW), lambda i, j, k: (k, 0)), (mixed_c, (tk_t, CW), lambda i, j, k: (k, 0)),
         (attn_norm_g, (1, AW), vecD), (dz1b, (tk_t, D), lambda i, j, k: (k, 0))],
        lambda refs, i, j, k: _dot(mixed_rows(refs[0], refs[1], refs[2]), refs[3][...], "tn"),
        grid=(1, 1, T // tk_t), tm=D, tn=D, outs=[_plain_out(D, D, D, D, F32)],
        epilogue=dw_out_epilogue, name="mm_dw_out")[0]
    early.append(d_w_out.reshape(NS, D // NS, D))
    def dmixed_epilogue(acc, i, j, extra_refs, out_refs):
        a_ref, r_ref, g_ref = extra_refs
        do_ref, dd_ref, dmc_ref, dg_ref = out_refs
        head_of = lambda axis: lax.broadcasted_iota(jnp.int32, (AW, AW), axis) // HEAD_DIM
        same_head = (head_of(0) == head_of(1)).astype(BF16)
        dm = acc[:, :AW]
        dmc_ref[...] = acc[:, AW:]
        a = a_ref[...]
        r = r_ref[:, 0:1]
        dxn = dm * g_ref[...]
        da = r * (dxn - a * (r * r) * jnp.mean(dxn * a, axis=-1, keepdims=True))
        do_ref[...] = da.astype(BF16)
        hi, lo = _split_hi_lo(da * a)
        dd_ref[...] = (jnp.dot(hi, same_head, preferred_element_type=F32)
                       + jnp.dot(lo, same_head, preferred_element_type=F32))
        _accumulate(dg_ref, i == 0, jnp.sum(dm * a * r, axis=0, keepdims=True))

    dattn, dd, dmc, d_attn_norm_g, sib_out = _matmul(
        dz1b, w_out, mode="nt", tm=tm, tn=D, tk=D,
        extras=[(attn, (tm, AW), rowD), (r_attn, (tm, LANES), rowD), (attn_norm_g, (1, AW), vecD)],
        outs=[((T, AW), BF16, (tm, AW), rowD), ((T, AW), F32, (tm, AW), rowD), ((T, CW), F32, (tm, CW), rowD),
              ((1, AW), F32, (1, AW), vecD)],
        epilogue=dmixed_epilogue, name="mm_dmixed", bg=_bg_sibling_exchange(early[2:]))
    sib_e.append(sib_out)
    chip_e.append(_pair_sum(early[2], sib_out, ids, name="pair_sum_w_out"))

    dag, d_conv_w, d_conv_b, d_conv_ln_g, d_conv_ln_b, d_conv_norm_g = _conv_bwd(
        ag, dmc, conv_w, conv_b, conv_ln_g, conv_ln_b, conv_norm_g, B, S, CW)

    dq, dk, dv, csq, csk, csv, dbias, *got_e = _attention_bwd(qkv, dattn, lse, dd, bias_all, B, S, AW,
                                                              bg=_bg_chip_exchange(chip_e))
    full_up, full_down, full_out = [_final_sum(g, s, r, ids, name="final_sum_" + n)
                                    for g, s, r, n in zip(early, sib_e, got_e, ("w_up", "w_down", "w_out"))]
    d_rel_table = _rel_grad(dbias.reshape(3, H, ATTN_BLOCK * 2 * ATTN_BLOCK), bucket).T
    dh, cs_ag = _dh_cat(dq, dk, dv, dag, tm_s)
    d_b_in = jnp.concatenate([csq, csk, csv, cs_ag], axis=1)

    d_w_in_t = _mm_plain(dh, xf, mode="tn", tm=_col_tile(INW, 1408), tn=D, tk=tk_t, out_dtype=F32, name="mm_dw_in")
    late = [d_w_in_t.reshape(NS, INW // NS, D)]
    sib_l = _sibling_exchange(late)
    chip_l = [_pair_sum(late[0], sib_l[0], ids, name="pair_sum_w_in")]
    small = dict(rel_table=d_rel_table, b_in=d_b_in, conv_w=d_conv_w, conv_b=d_conv_b, conv_ln_g=d_conv_ln_g,
                 conv_ln_b=d_conv_ln_b, attn_norm_g=d_attn_norm_g, conv_norm_g=d_conv_norm_g, ln1_g=d_ln1_g,
                 ln1_b=d_ln1_b, ffn_conv_w=d_ffn_cw, ffn_conv_b=d_ffn_cb, ln2_g=d_ln2_g, ln2_b=d_ln2_b)
    pack = _pack([loss_part] + [small[n] for n in SMALL_NAMES])

    def gx_epilogue(acc, i, j, extra_refs, out_refs):
        out_refs[0][...] = acc + ALPHA * extra_refs[0][...]

    grad_x, got_in, all_packs = _matmul(
        dh, w_in_t, mode="nn", tm=tm, tn=D, tk=INW, extras=[(dz1, (tm, D), rowD)],
        outs=[((T, D), F32, (tm, D), rowD)], epilogue=gx_epilogue, name="mm_grad_x",
        bg=_bg_chip_exchange(chip_l, pack))
    full_in = _final_sum(late[0], sib_l[0], got_in, ids, name="final_sum_w_in")
    return grad_x.reshape(B, S, D), [full_in, full_out, full_up, full_down], all_packs


def _place():
    return lax.axis_index("x"), lax.axis_index("y"), lax.axis_index("c")


CHIP_FLIPS = ((1, 0), (0, 1), (1, 1))


def _flip(v, f):
    return 1 - v if f else v


HBM_SPEC = pl.BlockSpec(memory_space=pl.ANY)
VMEM_SPEC = pl.BlockSpec(memory_space=pltpu.VMEM)
COMM_PARAMS = pltpu.CompilerParams(vmem_limit_bytes=VMEM_LIMIT)


def _gather_weights(big, small):
    nb, ns = len(big), len(small)

    def body(*refs):
        big_in = refs[:nb]
        small_in = refs[nb:nb + ns]
        big_out = refs[nb + ns:2 * nb + ns]
        small_out = refs[2 * nb + ns:2 * nb + 2 * ns]
        stages = refs[2 * nb + 2 * ns:3 * nb + 2 * ns]
        send_sems, recv_sems, local_sems = refs[3 * nb + 2 * ns:]
        x, y, c = _place()
        s_me = 2 * x + y
        sibling = (x, y, 1 - c)
        started, local_copies = [], []
        for a in range(nb):
            rh = big[a].shape[0] // 2
            lo = pl.multiple_of(c * rh, 16)
            stages[a][...] = big_in[a][pl.ds(lo, rh), :].astype(BF16)
            mine = big_out[a].at[s_me, pl.ds(lo, rh), :]
            loc = pltpu.make_async_copy(stages[a], mine, local_sems.at[a])
            loc.start()
            local_copies.append(loc)
            targets = [sibling] + [(_flip(x, fx), _flip(y, fy), c) for fx, fy in CHIP_FLIPS]
            for k, to in enumerate(targets):
                cp = pltpu.make_async_remote_copy(stages[a], mine, send_sems.at[a * 7 + k],
                                                  recv_sems.at[a * 7 + k], device_id=to, device_id_type=MESH)
                cp.start()
                started.append(cp)
        for a in range(ns):
            mine = small_out[a].at[s_me]
            loc = pltpu.make_async_copy(small_in[a], mine, local_sems.at[nb + a])
            loc.start()
            local_copies.append(loc)
            for k, (fx, fy) in enumerate(CHIP_FLIPS):
                cp = pltpu.make_async_remote_copy(small_in[a], mine, send_sems.at[nb * 7 + a * 3 + k],
                                                  recv_sems.at[nb * 7 + a * 3 + k],
                                                  device_id=(_flip(x, fx), _flip(y, fy), c), device_id_type=MESH)
                cp.start()
                started.append(cp)
        for a in range(nb):
            rh = big[a].shape[0] // 2
            lo = pl.multiple_of(c * rh, 16)
            for k, (fx, fy) in enumerate(CHIP_FLIPS):
                s_from = 2 * _flip(x, fx) + _flip(y, fy)
                got = big_out[a].at[s_from, pl.ds(lo, rh), :]
                pltpu.make_async_remote_copy(got, got, send_sems.at[a * 7 + 1 + k], recv_sems.at[a * 7 + 1 + k],
                                             device_id=sibling, device_id_type=MESH).wait_recv()
                fwd = pltpu.make_async_remote_copy(got, got, send_sems.at[a * 7 + 4 + k],
                                                   recv_sems.at[a * 7 + 4 + k], device_id=sibling,
                                                   device_id_type=MESH)
                fwd.start()
                started.append(fwd)
        for a in range(nb):
            rh = big[a].shape[0] // 2
            lo_sib = pl.multiple_of((1 - c) * rh, 16)
            for k in (0, 4, 5, 6):
                any_rows = big_out[a].at[s_me, pl.ds(lo_sib, rh), :]
                pltpu.make_async_remote_copy(any_rows, any_rows, send_sems.at[a * 7 + k], recv_sems.at[a * 7 + k],
                                             device_id=sibling, device_id_type=MESH).wait_recv()
        for a in range(ns):
            for k in range(3):
                pltpu.make_async_remote_copy(small_in[a], small_out[a].at[s_me], send_sems.at[nb * 7 + a * 3 + k],
                                             recv_sems.at[nb * 7 + a * 3 + k], device_id=sibling,
                                             device_id_type=MESH).wait_recv()
        for cp in started:
            cp.wait_send()
        for cp in local_copies:
            cp.wait()

    n_sem = nb * 7 + ns * 3
    out_shape = ([jax.ShapeDtypeStruct((N_SHARDS,) + w.shape, BF16) for w in big]
                 + [jax.ShapeDtypeStruct((N_SHARDS,) + w.shape, F32) for w in small])
    res = pl.pallas_call(
        body, in_specs=[VMEM_SPEC] * nb + [HBM_SPEC] * ns, out_specs=[HBM_SPEC] * (nb + ns),
        out_shape=out_shape,
        scratch_shapes=[pltpu.VMEM((w.shape[0] // 2, w.shape[1]), BF16) for w in big]
        + [pltpu.SemaphoreType.DMA((n_sem,)), pltpu.SemaphoreType.DMA((n_sem,)),
           pltpu.SemaphoreType.DMA((nb + ns,))],
        compiler_params=COMM_PARAMS, name="gather_weights",
    )(*big, *small)
    return res[:nb], res[nb:]


def _sibling_exchange(grads):
    n = len(grads)

    def body(*refs):
        g_in = refs[:n]
        got = refs[n:2 * n]
        send_sems, recv_sems = refs[2 * n:]
        x, y, c = _place()
        cps = []
        for a in range(n):
            rh = grads[a].shape[1] // 2
            lo = pl.multiple_of((1 - c) * rh, 8)
            cp = pltpu.make_async_remote_copy(g_in[a].at[:, pl.ds(lo, rh), :], got[a], send_sems.at[a],
                                              recv_sems.at[a], device_id=(x, y, 1 - c), device_id_type=MESH)
            cp.start()
            cps.append(cp)
        for cp in cps:
            cp.wait()

    return pl.pallas_call(
        body, in_specs=[HBM_SPEC] * n, out_specs=[HBM_SPEC] * n,
        out_shape=[jax.ShapeDtypeStruct((N_SHARDS, g.shape[1] // 2, g.shape[2]), F32) for g in grads],
        scratch_shapes=[pltpu.SemaphoreType.DMA((n,)), pltpu.SemaphoreType.DMA((n,))],
        compiler_params=COMM_PARAMS, name="sibling_exchange",
    )(*grads)


def _chip_exchange(chip_parts, pack):
    n = len(chip_parts)

    def body(*refs):
        parts = refs[:n]
        pack_ref = refs[n]
        got = refs[n + 1:2 * n + 1]
        all_packs = refs[2 * n + 1]
        send_sems, recv_sems, local_sem = refs[2 * n + 2:]
        x, y, c = _place()
        me = 4 * x + 2 * y + c
        cps = []
        for a in range(n):
            for k, (fx, fy) in enumerate(CHIP_FLIPS):
                px, py = _flip(x, fx), _flip(y, fy)
                cp = pltpu.make_async_remote_copy(parts[a].at[2 * px + py], got[a].at[k], send_sems.at[a * 3 + k],
                                                  recv_sems.at[a * 3 + k], device_id=(px, py, c),
                                                  device_id_type=MESH)
                cp.start()
                cps.append(cp)
        loc = pltpu.make_async_copy(pack_ref, all_packs.at[me], local_sem)
        loc.start()
        for m in range(1, N_DEV):
            to = (_flip(x, m & 4), _flip(y, m & 2), _flip(c, m & 1))
            cp = pltpu.make_async_remote_copy(pack_ref, all_packs.at[me], send_sems.at[n * 3 + m - 1],
                                              recv_sems.at[n * 3 + m - 1], device_id=to, device_id_type=MESH)
            cp.start()
            cps.append(cp)
        for cp in cps:
            cp.wait()
        loc.wait()

    rs = pack.shape[0]
    res = pl.pallas_call(
        body, in_specs=[HBM_SPEC] * (n + 1), out_specs=[HBM_SPEC] * (n + 1),
        out_shape=[jax.ShapeDtypeStruct((3,) + p.shape[1:], BF16) for p in chip_parts]
        + [jax.ShapeDtypeStruct((N_DEV, rs, LANES), F32)],
        scratch_shapes=[pltpu.SemaphoreType.DMA((n * 3 + N_DEV - 1,)), pltpu.SemaphoreType.DMA((n * 3 + N_DEV - 1,)),
                        pltpu.SemaphoreType.DMA],
        compiler_params=COMM_PARAMS, name="chip_exchange",
    )(*chip_parts, pack)
    return res[:n], res[n]


def _sibling_assemble(fulls):
    n = len(fulls)

    def body(*refs):
        full = refs[n:2 * n]
        send_sems, recv_sems = refs[2 * n:]
        x, y, c = _place()
        cps = []
        for a in range(n):
            rh = fulls[a].shape[0] // 2
            mine = full[a].at[pl.ds(pl.multiple_of(c * rh, 8), rh), :]
            cp = pltpu.make_async_remote_copy(mine, mine, send_sems.at[a], recv_sems.at[a],
                                              device_id=(x, y, 1 - c), device_id_type=MESH)
            cp.start()
            cps.append(cp)
        for cp in cps:
            cp.wait()

    return pl.pallas_call(
        body, in_specs=[HBM_SPEC] * n, out_specs=[HBM_SPEC] * n,
        out_shape=[jax.ShapeDtypeStruct(f.shape, F32) for f in fulls],
        input_output_aliases={a: a for a in range(n)},
        scratch_shapes=[pltpu.SemaphoreType.DMA((n,)), pltpu.SemaphoreType.DMA((n,))],
        compiler_params=COMM_PARAMS, name="sibling_assemble",
    )(*fulls)


def _remote(ref_src, ref_dst, send_sems, recv_sems, k, to):
    return pltpu.make_async_remote_copy(ref_src, ref_dst, send_sems.at[k], recv_sems.at[k], device_id=to,
                                        device_id_type=MESH)


def _stage_half(w, ids, name):
    R, C = w.shape
    rh = R // 2
    rt = _half_tile(rh)
    nt = rh // rt

    def body(ids_ref, w_ref, o_ref):
        o_ref[0] = w_ref[...].astype(BF16)

    grid_spec = pltpu.PrefetchScalarGridSpec(
        num_scalar_prefetch=1, grid=(nt,),
        in_specs=[pl.BlockSpec((rt, C), lambda i, ids: (ids[2] * nt + i, 0))],
        out_specs=pl.BlockSpec((1, rt, C), lambda i, ids: (2 * ids[0] + ids[1], ids[2] * nt + i, 0)))
    return pl.pallas_call(body, grid_spec=grid_spec, out_shape=jax.ShapeDtypeStruct((N_SHARDS, R, C), BF16),
                          compiler_params=_params(1), name=name)(ids, w)


def _bg_gather(staged):
    n = len(staged)

    def run(step, n_steps, ins, outs, send_sems, recv_sems, local_sems, post):
        x, y, c = _place()
        s_me = 2 * x + y
        sibling = (x, y, 1 - c)
        chips = [(_flip(x, fx), _flip(y, fy)) for fx, fy in CHIP_FLIPS]

        def rows(a, s, half):
            rh = staged[a].shape[1] // 2
            return outs[a].at[s, pl.ds(pl.multiple_of(half * rh, 16), rh), :]

        def copy(a, k, ref, to):
            return _remote(ref, ref, send_sems, recv_sems, a * 7 + k, to)

        if not post:
            @pl.when(step == 0)
            def _():
                for a in range(n):
                    mine = rows(a, s_me, c)
                    copy(a, 0, mine, sibling).start()
                    for k, (px, py) in enumerate(chips):
                        copy(a, 1 + k, mine, (px, py, c)).start()

            @pl.when(step == max(n_steps - 2, 0))
            def _():
                for a in range(n):
                    for k, (px, py) in enumerate(chips):
                        got = rows(a, 2 * px + py, c)
                        copy(a, 1 + k, got, sibling).wait_recv()
                        copy(a, 4 + k, got, sibling).start()
        else:
            @pl.when(step == n_steps - 1)
            def _():
                for a in range(n):
                    for k in (0, 4, 5, 6):
                        copy(a, k, rows(a, s_me, 1 - c), sibling).wait_recv()
                    for k in range(7):
                        copy(a, k, rows(a, s_me, c), sibling).wait_send()

    return _Background(staged, [jax.ShapeDtypeStruct(g.shape, g.dtype) for g in staged],
                       {a: a for a in range(n)}, 7 * n, run)


def _bg_sibling_exchange(grads):
    n = len(grads)

    def run(step, n_steps, ins, outs, send_sems, recv_sems, local_sems, post):
        x, y, c = _place()

        def copy(a):
            rh = grads[a].shape[1] // 2
            lo = pl.multiple_of((1 - c) * rh, 8)
            return _remote(ins[a].at[:, pl.ds(lo, rh), :], outs[a], send_sems, recv_sems, a, (x, y, 1 - c))

        if not post:
            @pl.when(step == 0)
            def _():
                for a in range(n):
                    copy(a).start()
        else:
            @pl.when(step == n_steps - 1)
            def _():
                for a in range(n):
                    copy(a).wait()

    return _Background(grads, [jax.ShapeDtypeStruct((N_SHARDS, g.shape[1] // 2, g.shape[2]), F32) for g in grads],
                       {}, n, run)


def _bg_chip_exchange(chip_parts, pack=None):
    n = len(chip_parts)

    def run(step, n_steps, ins, outs, send_sems, recv_sems, local_sems, post):
        x, y, c = _place()
        me = 4 * x + 2 * y + c

        def copies():
            cps = []
            for a in range(n):
                for k, (fx, fy) in enumerate(CHIP_FLIPS):
                    px, py = _flip(x, fx), _flip(y, fy)
                    cps.append(_remote(ins[a].at[2 * px + py], outs[a].at[k], send_sems, recv_sems, a * 3 + k,
                                       (px, py, c)))
            if pack is not None:
                for m in range(1, N_DEV):
                    to = (_flip(x, m & 4), _flip(y, m & 2), _flip(c, m & 1))
                    cps.append(_remote(ins[n], outs[n].at[me], send_sems, recv_sems, n * 3 + m - 1, to))
            return cps

        def local():
            return pltpu.make_async_copy(ins[n], outs[n].at[me], local_sems.at[0])

        if not post:
            @pl.when(step == 0)
            def _():
                for cp in copies():
                    cp.start()
                if pack is not None:
                    local().start()
        else:
            @pl.when(step == n_steps - 1)
            def _():
                for cp in copies():
                    cp.wait()
                if pack is not None:
                    local().wait()

    in_arrays = list(chip_parts) + ([pack] if pack is not None else [])
    out_shapes = [jax.ShapeDtypeStruct((3,) + p.shape[1:], BF16) for p in chip_parts]
    if pack is not None:
        out_shapes.append(jax.ShapeDtypeStruct((N_DEV, pack.shape[0], LANES), F32))
    return _Background(in_arrays, out_shapes, {}, n * 3 + N_DEV - 1, run)


def _half_tile(rh, mult=16, want=256):
    best = None
    for t in range(mult, min(rh, want) + 1, mult):
        if rh % t == 0:
            best = t
    return best if best is not None else rh


def _pair_sum(g, sib, ids, name):
    _, R, C = g.shape
    rh = R // 2
    rt = _half_tile(rh)
    nt = rh // rt

    def body(ids_ref, g_ref, s_ref, o_ref):
        o_ref[...] = (g_ref[...] + s_ref[...]).astype(BF16)

    grid_spec = pltpu.PrefetchScalarGridSpec(
        num_scalar_prefetch=1, grid=(N_SHARDS, nt),
        in_specs=[pl.BlockSpec((1, rt, C), lambda s, i, ids: (s, ids[2] * nt + i, 0)),
                  pl.BlockSpec((1, rt, C), lambda s, i, ids: (s, i, 0))],
        out_specs=pl.BlockSpec((1, rt, C), lambda s, i, ids: (s, i, 0)))
    return pl.pallas_call(body, grid_spec=grid_spec, out_shape=jax.ShapeDtypeStruct((N_SHARDS, rh, C), BF16),
                          compiler_params=_params(2), name=name)(ids, g, sib)


def _final_sum(g, sib, got, ids, name):
    _, R, C = g.shape
    rh = R // 2
    rt = _half_tile(rh)
    nt = rh // rt

    def body(ids_ref, g_ref, s_ref, r_ref, o_ref):
        tot = g_ref[0] + s_ref[0]
        for k in range(3):
            tot = tot + r_ref[k].astype(F32)
        o_ref[...] = tot

    grid_spec = pltpu.PrefetchScalarGridSpec(
        num_scalar_prefetch=1, grid=(nt,),
        in_specs=[pl.BlockSpec((1, rt, C), lambda i, ids: (2 * ids[0] + ids[1], ids[2] * nt + i, 0)),
                  pl.BlockSpec((1, rt, C), lambda i, ids: (2 * ids[0] + ids[1], i, 0)),
                  pl.BlockSpec((3, rt, C), lambda i, ids: (0, i, 0))],
        out_specs=pl.BlockSpec((rt, C), lambda i, ids: (ids[2] * nt + i, 0)))
    return pl.pallas_call(body, grid_spec=grid_spec, out_shape=jax.ShapeDtypeStruct((R, C), F32),
                          compiler_params=_params(1), name=name)(ids, g, sib, got)


def _sum_packs(all_packs):
    def body(p_ref, o_ref):
        tot = p_ref[0]
        for i in range(1, N_DEV):
            tot = tot + p_ref[i]
        o_ref[...] = tot

    return pl.pallas_call(body, in_specs=[VMEM_SPEC], out_specs=VMEM_SPEC,
                          out_shape=jax.ShapeDtypeStruct(all_packs.shape[1:], F32), name="sum_packs")(all_packs)


def _adamw(w, g, m, v, name):
    R, C = w.shape
    rt = _half_tile(R, mult=8, want=256)

    def body(w_ref, g_ref, m_ref, v_ref, d_ref, nm_ref, nv_ref):
        gg = g_ref[...]
        nm = ADAM_B1 * m_ref[...] + (1.0 - ADAM_B1) * gg
        nv = ADAM_B2 * v_ref[...] + (1.0 - ADAM_B2) * (gg * gg)
        m_hat = nm / (1.0 - ADAM_B1 ** ADAM_STEP)
        v_hat = nv / (1.0 - ADAM_B2 ** ADAM_STEP)
        d_ref[...] = -ADAM_LR * (m_hat / (jnp.sqrt(v_hat) + ADAM_EPS) + ADAM_WD * w_ref[...])
        nm_ref[...] = nm
        nv_ref[...] = nv

    spec = pl.BlockSpec((rt, C), lambda i: (i, 0))
    return pl.pallas_call(body, grid=(R // rt,), in_specs=[spec] * 4, out_specs=[spec] * 3,
                          out_shape=[jax.ShapeDtypeStruct((R, C), F32)] * 3,
                          compiler_params=_params(1), name=name)(w, g, m, v)


def _adamw_update(w, g, m, v):
    nm = ADAM_B1 * m + (1.0 - ADAM_B1) * g
    nv = ADAM_B2 * v + (1.0 - ADAM_B2) * (g * g)
    m_hat = nm / (1.0 - ADAM_B1 ** ADAM_STEP)
    v_hat = nv / (1.0 - ADAM_B2 ** ADAM_STEP)
    return -ADAM_LR * (m_hat / (jnp.sqrt(v_hat) + ADAM_EPS) + ADAM_WD * w), nm, nv


def _adamw_many(ws, gs, ms, vs, name):
    n = len(ws)

    def body(*refs):
        for i in range(n):
            d, nm, nv = _adamw_update(refs[i][...], refs[n + i][...], refs[2 * n + i][...], refs[3 * n + i][...])
            refs[4 * n + i][...] = d
            refs[5 * n + i][...] = nm
            refs[6 * n + i][...] = nv

    return pl.pallas_call(body, in_specs=[VMEM_SPEC] * (4 * n), out_specs=[VMEM_SPEC] * (3 * n),
                          out_shape=[jax.ShapeDtypeStruct(w.shape, F32) for w in ws] * 3, name=name,
                          )(*ws, *gs, *ms, *vs)


def _pack(pieces):
    rows = []
    for p in pieces:
        flat = p.reshape(-1)
        pad = (-flat.shape[0]) % LANES
        if pad:
            flat = jnp.concatenate([flat, jnp.zeros((pad,), F32)])
        rows.append(flat.reshape(-1, LANES))
    total = sum(r.shape[0] for r in rows)
    pad_rows = (-total) % 8
    if pad_rows:
        rows.append(jnp.zeros((pad_rows, LANES), F32))
    return jnp.concatenate(rows, axis=0)


def _unpack(buf, shapes):
    out, r0 = [], 0
    for shp in shapes:
        n = int(np.prod(shp))
        nr = -(-n // LANES)
        out.append(buf[r0:r0 + nr].reshape(-1)[:n].reshape(shp))
        r0 += nr
    return out


SMALL_NAMES = ("rel_table", "b_in", "conv_w", "conv_b", "conv_ln_g", "conv_ln_b", "attn_norm_g", "conv_norm_g",
               "ln1_g", "ln1_b", "ffn_conv_w", "ffn_conv_b", "ln2_g", "ln2_b")
BIG_NAMES = ("w_in", "w_out", "w_up", "w_down")
WEIGHT_ORDER = ("rel_table", "w_in", "b_in", "conv_w", "conv_b", "conv_ln_g", "conv_ln_b", "attn_norm_g",
                "conv_norm_g", "w_out", "ln1_g", "ln1_b", "w_up", "ffn_conv_w", "ffn_conv_b", "w_down",
                "ln2_g", "ln2_b")


def kernel(x, rel_table, w_in, b_in, conv_w, conv_b, conv_ln_g, conv_ln_b, attn_norm_g, conv_norm_g, w_out, ln1_g, ln1_b, w_up, ffn_conv_w, ffn_conv_b, w_down, ln2_g, ln2_b, loss_target, m_rel_table, m_w_in, m_b_in, m_conv_w, m_conv_b, m_conv_ln_g, m_conv_ln_b, m_attn_norm_g, m_conv_norm_g, m_w_out, m_ln1_g, m_ln1_b, m_w_up, m_ffn_conv_w, m_ffn_conv_b, m_w_down, m_ln2_g, m_ln2_b, v_rel_table, v_w_in, v_b_in, v_conv_w, v_conv_b, v_conv_ln_g, v_conv_ln_b, v_attn_norm_g, v_conv_norm_g, v_w_out, v_ln1_g, v_ln1_b, v_w_up, v_ffn_conv_w, v_ffn_conv_b, v_w_down, v_ln2_g, v_ln2_b):
    args = dict(locals())
    weights = {n: args[n] for n in WEIGHT_ORDER}
    moms = {n: args["m_" + n] for n in WEIGHT_ORDER}
    vels = {n: args["v_" + n] for n in WEIGHT_ORDER}
    xi, yi, ci = _place()
    ids = jnp.stack([xi, yi, ci]).astype(jnp.int32)
    shard = 2 * xi + yi
    D = x.shape[-1]
    DFF = w_down.shape[1] * N_SHARDS
    CW = conv_norm_g.shape[-1]

    tr = lambda t: jnp.transpose(t[0])
    (g_in,), (g_cw, g_fcw) = _gather_weights([tr(w_in)], [conv_w[0], ffn_conv_w[0]])
    cols = lambda t: jnp.transpose(t, (1, 0, 2)).reshape(t.shape[1], N_SHARDS * t.shape[2])
    staged = [_stage_half(w[0], ids, name="stage_" + n) for w, n in ((w_out, "w_out"), (w_up, "w_up"),
                                                                     (w_down, "w_down"))]

    grad_x, fulls, all_packs = _local_step(
        x, loss_target, rel_table, g_in.reshape(-1, D), b_in, cols(g_cw), conv_b, conv_ln_g, conv_ln_b, attn_norm_g,
        conv_norm_g, staged, ln1_g, ln1_b, cols(g_fcw), ffn_conv_b, ln2_g, ln2_b, ids)
    big_grads = dict(zip(BIG_NAMES, _sibling_assemble(fulls)))
    big_grads["w_up"] = _transpose(big_grads["w_up"], name="transpose_dw_up")

    summed = _sum_packs(all_packs)
    full_shapes = {n: weights[n].shape for n in SMALL_NAMES}
    full_shapes["conv_w"] = (1, CONV_KERNEL, CW)
    full_shapes["ffn_conv_w"] = (1, FFN_CONV_KERNEL, 2 * DFF)
    un = _unpack(summed, [(1, LANES)] + [full_shapes[n] for n in SMALL_NAMES])
    loss = un[0][0, 0]
    small_grads = dict(zip(SMALL_NAMES, un[1:]))
    for n in ("conv_w", "ffn_conv_w"):
        width = weights[n].shape[-1]
        small_grads[n] = lax.dynamic_slice_in_dim(small_grads[n], shard * width, width, axis=2)

    grads, delta, new_m, new_v = {}, {}, {}, {}
    for n in BIG_NAMES:
        shp = weights[n].shape
        g2 = big_grads[n]
        if n == "w_in":
            res = (g2,) + tuple(_adamw(tr(weights[n]), g2, tr(moms[n]), tr(vels[n]), name="adamw_" + n))
            res = [jnp.transpose(t) for t in res]
        else:
            res = (g2,) + tuple(_adamw(weights[n][0], g2, moms[n][0], vels[n][0], name="adamw_" + n))
        grads[n], delta[n], new_m[n], new_v[n] = (t.reshape(shp) for t in res)
    pick = lambda src: [src[n] for n in SMALL_NAMES]
    small_out = _adamw_many(pick(weights), pick(small_grads), pick(moms), pick(vels), name="adamw_small")
    ns = len(SMALL_NAMES)
    for tgt, part in ((delta, small_out[:ns]), (new_m, small_out[ns:2 * ns]), (new_v, small_out[2 * ns:])):
        tgt.update(zip(SMALL_NAMES, part))
    grads.update(small_grads)

    return (loss, grad_x, *[grads[n] for n in WEIGHT_ORDER], *[delta[n] for n in WEIGHT_ORDER],
            *[new_m[n] for n in WEIGHT_ORDER], *[new_v[n] for n in WEIGHT_ORDER])
```

```python
import math

import numpy as np
import jax
import jax.numpy as jnp
from jax import lax
from jax.experimental import pallas as pl
from jax.experimental.pallas import tpu as pltpu

F32 = jnp.float32
BF16 = jnp.bfloat16
MESH = pl.DeviceIdType.MESH

HEAD_DIM = 64
LANES = 128
ATTN_BLOCK = 128
DILATED_CONFIGS = ((128, 1), (512, 4), (2048, 16))
CONV_KERNEL = 31
FFN_CONV_KERNEL = 3
REL_BUCKETS = 32
REL_MAX_DIST = 2048
DEPTH = 1
ALPHA = (2 * DEPTH) ** 0.25
LN_EPS = 1e-5
NEG_INF = -1e30
QK_SCALE = 1.0 / math.sqrt(HEAD_DIM)
ADAM_LR = 0.001
ADAM_B1 = 0.9
ADAM_B2 = 0.999
ADAM_EPS = 1e-08
ADAM_WD = 0.01
ADAM_STEP = 10
VMEM_LIMIT = 52 * 1024 * 1024
FFN_COLS = 128
N_SHARDS = 4
N_DEV = 8


def _params(n_axes):
    return pltpu.CompilerParams(dimension_semantics=("arbitrary",) * n_axes,
                                vmem_limit_bytes=VMEM_LIMIT)


MM_DIMS = {"nn": (((1,), (0,)), ((), ())), "nt": (((1,), (1,)), ((), ())), "tn": (((0,), (0,)), ((), ()))}


class _Background:
    def __init__(self, in_arrays, out_shapes, aliases, n_sems, run, n_local=1):
        self.in_arrays, self.out_shapes, self.aliases = list(in_arrays), list(out_shapes), dict(aliases)
        self.n_sems, self.n_local, self.run = n_sems, n_local, run

    def scratch(self):
        return [pltpu.SemaphoreType.DMA((self.n_sems,)), pltpu.SemaphoreType.DMA((self.n_sems,)),
                pltpu.SemaphoreType.DMA((self.n_local,))]


def _hosted_call(body, bg, *, grid, in_specs, out_specs, out_shape, scratch_shapes, operands, name):
    n_in, n_out, n_scr = len(in_specs), len(out_specs), len(scratch_shapes)
    if bg is None:
        return pl.pallas_call(lambda *refs: body(refs, lambda post: None), grid=grid, in_specs=in_specs,
                              out_specs=out_specs, out_shape=out_shape, scratch_shapes=scratch_shapes,
                              compiler_params=_params(len(grid)), name=name)(*operands)
    nb_in, nb_out = len(bg.in_arrays), len(bg.out_shapes)
    n_steps = int(np.prod(grid))

    def full_body(*refs):
        own = refs[:n_in] + refs[n_in + nb_in:n_in + nb_in + n_out] \
            + refs[n_in + nb_in + n_out + nb_out:n_in + nb_in + n_out + nb_out + n_scr]
        bg_in = refs[n_in:n_in + nb_in]
        bg_out = refs[n_in + nb_in + n_out:n_in + nb_in + n_out + nb_out]
        sems = refs[n_in + nb_in + n_out + nb_out + n_scr:]
        step = pl.program_id(0)
        for ax in range(1, len(grid)):
            step = step * grid[ax] + pl.program_id(ax)

        def hook(post):
            bg.run(step, n_steps, bg_in, bg_out, *sems, post)

        body(own, hook)

    res = pl.pallas_call(
        full_body, grid=grid, in_specs=list(in_specs) + [HBM_SPEC] * nb_in,
        out_specs=list(out_specs) + [HBM_SPEC] * nb_out, out_shape=list(out_shape) + bg.out_shapes,
        input_output_aliases={n_in + a: n_out + o for a, o in bg.aliases.items()},
        scratch_shapes=list(scratch_shapes) + bg.scratch(), compiler_params=_params(len(grid)), name=name,
    )(*operands, *bg.in_arrays)
    return res


def _matmul_general(ins, part_fn, *, grid, tm, tn, outs, epilogue, extras=(), name, bg=None):
    nk = grid[2]
    n_in, n_extra = len(ins), len(extras)

    def body(refs, bg_hook):
        in_refs = refs[:n_in]
        rest = refs[n_in:]
        extra_refs = rest[:n_extra]
        out_refs = rest[n_extra:n_extra + len(outs)]
        acc_ref = rest[-1]
        i, j, k = pl.program_id(0), pl.program_id(1), pl.program_id(2)
        bg_hook(False)
        part = part_fn(in_refs, i, j, k)
        if nk == 1:
            epilogue(part, i, j, extra_refs, out_refs)
        else:
            @pl.when(k == 0)
            def _():
                acc_ref[...] = part

            @pl.when(k > 0)
            def _():
                acc_ref[...] += part

            @pl.when(k == nk - 1)
            def _():
                epilogue(acc_ref[...], i, j, extra_refs, out_refs)
        bg_hook(True)

    in_specs = [pl.BlockSpec(bs, im) for (_, bs, im) in list(ins) + list(extras)]
    out_specs = [pl.BlockSpec(bs, im) for (_, _, bs, im) in outs]
    out_shape = [jax.ShapeDtypeStruct(s, d) for (s, d, _, _) in outs]
    return _hosted_call(body, bg, grid=grid, in_specs=in_specs, out_specs=out_specs, out_shape=out_shape,
                        scratch_shapes=[pltpu.VMEM((tm, tn), F32)],
                        operands=[e[0] for e in ins] + [e[0] for e in extras], name=name)


def _dot(a, b, mode):
    return lax.dot_general(a.astype(BF16), b.astype(BF16), MM_DIMS[mode], preferred_element_type=F32)


def _matmul(a, b, *, mode, tm, tn, tk, outs, epilogue, extras=(), name, bg=None):
    if mode == "tn":
        K, M = a.shape
        N = b.shape[1]
        ins = [(a, (tk, tm), lambda i, j, k: (k, i)), (b, (tk, tn), lambda i, j, k: (k, j))]
    elif mode == "nt":
        M, K = a.shape
        N = b.shape[0]
        ins = [(a, (tm, tk), lambda i, j, k: (i, k)), (b, (tn, tk), lambda i, j, k: (j, k))]
    else:
        M, K = a.shape
        N = b.shape[1]
        ins = [(a, (tm, tk), lambda i, j, k: (i, k)), (b, (tk, tn), lambda i, j, k: (k, j))]
    assert M % tm == 0 and N % tn == 0 and K % tk == 0, (name, M, N, K, tm, tn, tk)

    def part_fn(in_refs, i, j, k):
        return _dot(in_refs[0][...], in_refs[1][...], mode)

    return _matmul_general(ins, part_fn, grid=(M // tm, N // tn, K // tk), tm=tm, tn=tn, outs=outs,
                           epilogue=epilogue, extras=extras, name=name, bg=bg)


def _plain_out(M, N, tm, tn, dtype):
    return ((M, N), dtype, (tm, tn), lambda i, j, k: (i, j))


def _mm_plain(a, b, *, mode, tm, tn, tk, out_dtype, name, bias=None, bg=None):
    if mode == "tn":
        M, N = a.shape[1], b.shape[1]
    elif mode == "nt":
        M, N = a.shape[0], b.shape[0]
    else:
        M, N = a.shape[0], b.shape[1]
    extras = []
    if bias is not None:
        extras.append((bias, (1, tn), lambda i, j, k: (0, j)))

    def epilogue(acc, i, j, extra_refs, out_refs):
        if bias is not None:
            acc = acc + extra_refs[0][...]
        out_refs[0][...] = acc.astype(out_dtype)

    res = _matmul(a, b, mode=mode, tm=tm, tn=tn, tk=tk, outs=[_plain_out(M, N, tm, tn, out_dtype)],
                  epilogue=epilogue, extras=extras, name=name, bg=bg)
    return res[0] if bg is None else res


def _row_tile(T, want):
    t = min(T, want)
    while T % t:
        t //= 2
    return t


def _col_tile(N, want):
    if N <= want:
        return N
    best = None
    for c in range(LANES, want + 1, LANES):
        if N % c == 0:
            best = c
    return best if best is not None else N


def _accumulate(ref, first, val):
    @pl.when(first)
    def _():
        ref[...] = val

    @pl.when(jnp.logical_not(first))
    def _():
        ref[...] += val


def _ln_fwd(z, g, b):
    mu = jnp.mean(z, axis=-1, keepdims=True)
    zc = z - mu
    var = jnp.mean(zc * zc, axis=-1, keepdims=True)
    r = lax.rsqrt(var + LN_EPS)
    xh = zc * r
    return xh * g + b, xh, r


def _ln_bwd(dy, xh, r, g):
    dxh = dy * g
    m1 = jnp.mean(dxh, axis=-1, keepdims=True)
    m2 = jnp.mean(dxh * xh, axis=-1, keepdims=True)
    return r * (dxh - m1 - xh * m2)


def _sigmoid(x):
    return 1.0 / (1.0 + jnp.exp(-x))


def _bucket_tables():
    exact = REL_BUCKETS // 2
    qi = np.arange(ATTN_BLOCK)[:, None]
    kj = np.arange(2 * ATTN_BLOCK)[None, :]
    steps = qi + ATTN_BLOCK - kj
    buckets, masks = [], []
    for window, dilation in DILATED_CONFIGS:
        max_steps = window // dilation
        band = (steps >= 0) & (steps <= max_steps)
        dist = np.maximum(steps, 0) * dilation
        d_f = np.maximum(dist, 1).astype(np.float32)
        large = exact + (np.log(d_f / np.float32(exact)) / np.float32(math.log(REL_MAX_DIST / exact))
                         * np.float32(REL_BUCKETS - exact)).astype(np.int32)
        large = np.minimum(large, REL_BUCKETS - 1)
        bucket = np.where(dist < exact, dist, large).astype(np.int32)
        buckets.append(bucket.reshape(1, -1))
        masks.append(np.where(band, 0.0, NEG_INF).astype(np.float32).reshape(1, -1))
    return np.stack(buckets), np.stack(masks)


def _split_hi_lo(x):
    hi = x.astype(BF16)
    lo = (x - hi.astype(F32)).astype(BF16)
    return hi, lo


def _bias_build(rel_table_t, bucket, mask):
    H = rel_table_t.shape[0]
    n = bucket.shape[-1]

    def body(t_ref, bkt_ref, mask_ref, o_ref):
        onehot = (lax.broadcasted_iota(jnp.int32, (REL_BUCKETS, n), 0) == bkt_ref[0]).astype(BF16)
        t = t_ref[...]
        t1 = t.astype(BF16)
        r1 = t - t1.astype(F32)
        t2 = r1.astype(BF16)
        t3 = (r1 - t2.astype(F32)).astype(BF16)
        acc = jnp.dot(t1, onehot, preferred_element_type=F32)
        acc = acc + jnp.dot(t2, onehot, preferred_element_type=F32)
        acc = acc + jnp.dot(t3, onehot, preferred_element_type=F32)
        o_ref[0] = acc + mask_ref[0]

    return pl.pallas_call(
        body, grid=(3,),
        in_specs=[pl.BlockSpec((H, REL_BUCKETS), lambda b: (0, 0)),
                  pl.BlockSpec((1, 1, n), lambda b: (b, 0, 0)),
                  pl.BlockSpec((1, 1, n), lambda b: (b, 0, 0))],
        out_specs=pl.BlockSpec((1, H, n), lambda b: (b, 0, 0)),
        out_shape=jax.ShapeDtypeStruct((3, H, n), F32),
        compiler_params=_params(1), name="bias_build",
    )(rel_table_t, bucket, mask)


def _rel_grad(dbias, bucket):
    H = dbias.shape[1]
    n = bucket.shape[-1]
    dims = (((1,), (1,)), ((), ()))

    def body(d_ref, bkt_ref, o_ref):
        b = pl.program_id(0)
        onehot = (lax.broadcasted_iota(jnp.int32, (REL_BUCKETS, n), 0) == bkt_ref[0]).astype(BF16)
        d = d_ref[0]
        d1 = d.astype(BF16)
        r1 = d - d1.astype(F32)
        d2 = r1.astype(BF16)
        d3 = (r1 - d2.astype(F32)).astype(BF16)
        acc = lax.dot_general(d1, onehot, dims, preferred_element_type=F32)
        acc = acc + lax.dot_general(d2, onehot, dims, preferred_element_type=F32)
        acc = acc + lax.dot_general(d3, onehot, dims, preferred_element_type=F32)
        _accumulate(o_ref, b == 0, acc)

    return pl.pallas_call(
        body, grid=(3,),
        in_specs=[pl.BlockSpec((1, H, n), lambda b: (b, 0, 0)),
                  pl.BlockSpec((1, 1, n), lambda b: (b, 0, 0))],
        out_specs=pl.BlockSpec((H, REL_BUCKETS), lambda b: (0, 0)),
        out_shape=jax.ShapeDtypeStruct((H, REL_BUCKETS), F32),
        compiler_params=_params(1), name="rel_grad",
    )(dbias, bucket)


def _regroup(src, stage, dst, d, S, off=0):
    if d == 1:
        dst[off:off + S, :] = src.astype(dst.dtype)
        return
    stage[...] = src.astype(F32)
    L = S // d
    for r in range(d):
        dst[off + r * L:off + (r + 1) * L, :] = stage[pl.ds(r, L, stride=d), :].astype(dst.dtype)


def _ungroup(sub_ref, off, nat_ref, d, S, add):
    L = S // d
    for r in range(d):
        rows = pl.ds(0, S) if d == 1 else pl.ds(r, L, stride=d)
        val = sub_ref[off + r * L:off + (r + 1) * L, :]
        if add:
            nat_ref[rows, :] += val
        else:
            nat_ref[rows, :] = val


def _branch_keys(ks, vs, S, nb, g_idx):
    blk3 = (S // ATTN_BLOCK, ATTN_BLOCK, LANES)
    kc3 = ks[ATTN_BLOCK:ATTN_BLOCK + S, :].reshape(blk3)
    vc3 = vs[ATTN_BLOCK:ATTN_BLOCK + S, :].reshape(blk3)
    if nb == 1:
        return kc3, vc3, None
    kk3 = jnp.concatenate([ks[0:S, :].reshape(blk3), kc3], axis=1)
    vv3 = jnp.concatenate([vs[0:S, :].reshape(blk3), vc3], axis=1)
    col = lax.broadcasted_iota(jnp.int32, (1, 1, 2 * ATTN_BLOCK), 2)
    dead = jnp.logical_and((g_idx & (nb - 1)) == 0, col < ATTN_BLOCK)
    return kk3, vv3, dead


def _branch_scores(qe, kk3, b_ref, bi, e, dead):
    s = jnp.einsum("gqe,gke->gqk", qe, kk3, preferred_element_type=F32)
    if dead is None:
        return s + b_ref[bi, e, :, ATTN_BLOCK:]
    return jnp.where(dead, NEG_INF, s + b_ref[bi, e])


def _attention_fwd(qkv, bias_all, B, S, AW, bg=None):
    HP = AW // LANES
    G = S // ATTN_BLOCK
    blk3 = (G, ATTN_BLOCK, LANES)

    def body(refs, bg_hook):
        q_ref, k_ref, v_ref, b_ref, o_ref, lse_ref, stage, qs, ks, vs, ot, lt, on0, on1, on2, ln0, ln1, ln2 = refs
        bg_hook(False)
        head0 = lax.broadcasted_iota(jnp.int32, (1, 1, LANES), 2) < HEAD_DIM
        g_idx = lax.broadcasted_iota(jnp.int32, (G, 1, 1), 0)
        ks[0:ATTN_BLOCK, :] = jnp.zeros((ATTN_BLOCK, LANES), BF16)
        vs[0:ATTN_BLOCK, :] = jnp.zeros((ATTN_BLOCK, LANES), BF16)
        nat_o, nat_l = (on0, on1, on2), (ln0, ln1, ln2)
        for bi, (_, d) in enumerate(DILATED_CONFIGS):
            nb = S // d // ATTN_BLOCK
            _regroup(q_ref[0], stage, qs, d, S)
            _regroup(k_ref[0], stage, ks, d, S, ATTN_BLOCK)
            _regroup(v_ref[0], stage, vs, d, S, ATTN_BLOCK)
            q3 = qs[...].reshape(blk3) * QK_SCALE
            kk3, vv3, dead = _branch_keys(ks, vs, S, nb, g_idx)
            outs, lses = [], []
            for e in range(2):
                msk = head0 if e == 0 else jnp.logical_not(head0)
                qe = jnp.where(msk, q3, jnp.zeros_like(q3))
                s = _branch_scores(qe, kk3, b_ref, bi, e, dead)
                m = jnp.max(s, axis=-1, keepdims=True)
                p = jnp.exp(s - m)
                l = jnp.sum(p, axis=-1, keepdims=True)
                o = jnp.einsum("gqk,gke->gqe", p.astype(BF16), vv3, preferred_element_type=F32)
                outs.append(o / l)
                lses.append(jnp.broadcast_to(m + jnp.log(l), blk3))
            ot[...] = jnp.where(head0, outs[0], outs[1]).reshape(S, LANES)
            lt[...] = jnp.where(head0, lses[0], lses[1]).reshape(S, LANES)
            _ungroup(ot, 0, nat_o[bi], d, S, add=False)
            _ungroup(lt, 0, nat_l[bi], d, S, add=False)

        la, lb, lc = ln0[...], ln1[...], ln2[...]
        m = jnp.maximum(jnp.maximum(la, lb), lc)
        ea, eb, ec = jnp.exp(la - m), jnp.exp(lb - m), jnp.exp(lc - m)
        den = ea + eb + ec
        lse_ref[0] = m + jnp.log(den)
        o_ref[0] = (ea * on0[...] + eb * on1[...] + ec * on2[...]) / den
        bg_hook(True)

    blk = lambda off: pl.BlockSpec((1, S, LANES), lambda b, h: (b, 0, off + h))
    qv = qkv.reshape(B, S, 3 * AW)
    sub_f = pltpu.VMEM((S, LANES), F32)
    pad_b = pltpu.VMEM((S + ATTN_BLOCK, LANES), BF16)
    res = _hosted_call(
        body, bg, grid=(B, HP),
        in_specs=[blk(0), blk(HP), blk(2 * HP),
                  pl.BlockSpec((3, 2, ATTN_BLOCK, 2 * ATTN_BLOCK), lambda b, h: (0, h, 0, 0))],
        out_specs=[blk(0), blk(0)],
        out_shape=[jax.ShapeDtypeStruct((B, S, AW), F32)] * 2,
        scratch_shapes=[sub_f, pltpu.VMEM((S, LANES), BF16), pad_b, pad_b] + [sub_f] * 8,
        operands=[qv, qv, qv, bias_all], name="attention_fwd")
    return (res[0].reshape(B * S, AW), res[1].reshape(B * S, AW)) + tuple(res[2:])


def _attention_bwd(qkv, do, lse, dd, bias_all, B, S, AW, bg=None):
    HP = AW // LANES
    H = AW // HEAD_DIM
    G = S // ATTN_BLOCK
    blk3 = (G, ATTN_BLOCK, LANES)
    PAD = ATTN_BLOCK

    def body(refs, bg_hook):
        (q_ref, k_ref, v_ref, do_ref, lse_ref, dd_ref, b_ref,
         dq_ref, dk_ref, dv_ref, csq_ref, csk_ref, csv_ref, db_ref,
         stage, qs, ks, vs, gs, ls, ds_, tq, tk, tv, accq, acck, accv) = refs
        bg_hook(False)
        head0 = lax.broadcasted_iota(jnp.int32, (1, 1, LANES), 2) < HEAD_DIM
        g_idx = lax.broadcasted_iota(jnp.int32, (G, 1, 1), 0)
        first_b = pl.program_id(1) == 0

        @pl.when(first_b)
        def _():
            db_ref[...] = jnp.zeros_like(db_ref)

        ks[0:PAD, :] = jnp.zeros((PAD, LANES), BF16)
        vs[0:PAD, :] = jnp.zeros((PAD, LANES), BF16)
        tk[0:PAD, :] = jnp.zeros((PAD, LANES), F32)
        tv[0:PAD, :] = jnp.zeros((PAD, LANES), F32)
        for bi, (_, d) in enumerate(DILATED_CONFIGS):
            nb = S // d // ATTN_BLOCK
            _regroup(q_ref[0], stage, qs, d, S)
            _regroup(k_ref[0], stage, ks, d, S, PAD)
            _regroup(v_ref[0], stage, vs, d, S, PAD)
            _regroup(do_ref[0], stage, gs, d, S)
            _regroup(lse_ref[0], stage, ls, d, S)
            _regroup(dd_ref[0], stage, ds_, d, S)
            q3 = qs[...].reshape(blk3) * QK_SCALE
            do3 = gs[...].reshape(blk3)
            lse3 = ls[...].reshape(blk3)
            dd3 = ds_[...].reshape(blk3)
            kk3, vv3, dead = _branch_keys(ks, vs, S, nb, g_idx)
            dq = jnp.zeros(blk3, F32)
            dkk = jnp.zeros(kk3.shape, F32)
            dvv = jnp.zeros(kk3.shape, F32)
            for e in range(2):
                msk = head0 if e == 0 else jnp.logical_not(head0)
                c0 = e * HEAD_DIM
                qe = jnp.where(msk, q3, jnp.zeros_like(q3))
                doe = jnp.where(msk, do3, jnp.zeros_like(do3))
                ke = jnp.where(msk, kk3 * QK_SCALE, jnp.zeros_like(kk3))
                s = _branch_scores(qe, kk3, b_ref, bi, e, dead)
                p = jnp.exp(s - lse3[:, :, c0:c0 + 1])
                dp = jnp.einsum("gqe,gke->gqk", doe, vv3, preferred_element_type=F32)
                dsc = p * (dp - dd3[:, :, c0:c0 + 1])
                if dead is None:
                    db_ref[bi, e, :, ATTN_BLOCK:] += jnp.sum(dsc, axis=0)
                else:
                    db_ref[bi, e] += jnp.sum(dsc, axis=0)
                dsb = dsc.astype(BF16)
                dq = dq + jnp.einsum("gqk,gke->gqe", dsb, ke, preferred_element_type=F32)
                dkk = dkk + jnp.einsum("gqk,gqe->gke", dsb, qe, preferred_element_type=F32)
                dvv = dvv + jnp.einsum("gqk,gqe->gke", p.astype(BF16), doe, preferred_element_type=F32)
            tq[...] = dq.reshape(S, LANES)
            if dead is None:
                tk[PAD:PAD + S, :] = dkk.reshape(S, LANES)
                tv[PAD:PAD + S, :] = dvv.reshape(S, LANES)
            else:
                tk[PAD:PAD + S, :] = dkk[:, ATTN_BLOCK:, :].reshape(S, LANES)
                tv[PAD:PAD + S, :] = dvv[:, ATTN_BLOCK:, :].reshape(S, LANES)
                tk[0:S, :] += dkk[:, :ATTN_BLOCK, :].reshape(S, LANES)
                tv[0:S, :] += dvv[:, :ATTN_BLOCK, :].reshape(S, LANES)
            _ungroup(tq, 0, accq, d, S, add=bi > 0)
            _ungroup(tk, PAD, acck, d, S, add=bi > 0)
            _ungroup(tv, PAD, accv, d, S, add=bi > 0)

        for acc, out_ref, cs_ref in ((accq, dq_ref, csq_ref), (acck, dk_ref, csk_ref), (accv, dv_ref, csv_ref)):
            tot = acc[...]
            out_ref[0] = tot.astype(out_ref.dtype)
            _accumulate(cs_ref, first_b, jnp.sum(tot, axis=0, keepdims=True))
        bg_hook(True)

    blk = lambda off: pl.BlockSpec((1, S, LANES), lambda h, b: (b, 0, off + h))
    cs_spec = pl.BlockSpec((1, LANES), lambda h, b: (0, h))
    bias_spec = pl.BlockSpec((3, 2, ATTN_BLOCK, 2 * ATTN_BLOCK), lambda h, b: (0, h, 0, 0))
    qv = qkv.reshape(B, S, 3 * AW)
    view = lambda t: t.reshape(B, S, AW)
    sub_b = pltpu.VMEM((S, LANES), BF16)
    sub_f = pltpu.VMEM((S, LANES), F32)
    pad_b = pltpu.VMEM((S + PAD, LANES), BF16)
    pad_f = pltpu.VMEM((S + PAD, LANES), F32)
    res = _hosted_call(
        body, bg, grid=(HP, B),
        in_specs=[blk(0), blk(HP), blk(2 * HP), blk(0), blk(0), blk(0), bias_spec],
        out_specs=[blk(0), blk(0), blk(0), cs_spec, cs_spec, cs_spec, bias_spec],
        out_shape=[jax.ShapeDtypeStruct((B, S, AW), BF16)] * 3 + [jax.ShapeDtypeStruct((1, AW), F32)] * 3
        + [jax.ShapeDtypeStruct((3, H, ATTN_BLOCK, 2 * ATTN_BLOCK), F32)],
        scratch_shapes=[sub_f, sub_b, pad_b, pad_b, sub_b, sub_f, sub_f, sub_f, pad_f, pad_f, sub_f, sub_f, sub_f],
        operands=[qv, qv, qv, view(do), view(lse), view(dd), bias_all], name="attention_bwd")
    flat = lambda t: t.reshape(B * S, AW)
    return (flat(res[0]), flat(res[1]), flat(res[2]), res[3], res[4], res[5], res[6]) + tuple(res[7:])


class _RowShifts:
    def __init__(self, x, row, up):
        self.x, self.row, self.up, self.base = x, row, up, {0: x}

    def __call__(self, s):
        x = self.x
        n, c = x.shape
        r, whole = s % 8, s - s % 8
        if r not in self.base:
            if self.up:
                rolled = pltpu.roll(x, n - r, 0)
                tail = jnp.where(self.row[n - 8:] < n - r, rolled[n - 8:], 0.0)
                self.base[r] = jnp.concatenate([rolled[:n - 8], tail], axis=0)
            else:
                rolled = pltpu.roll(x, r, 0)
                head = jnp.where(self.row[:8] >= r, rolled[:8], 0.0)
                self.base[r] = jnp.concatenate([head, rolled[8:]], axis=0)
        y = self.base[r]
        if whole == 0:
            return y
        pad = jnp.zeros((whole, c), x.dtype)
        if self.up:
            return jnp.concatenate([y[whole:], pad], axis=0)
        return jnp.concatenate([pad, y[:n - whole]], axis=0)


def _conv_branch_fwd_math(a, g, w_ref, cb, lg, lb, row):
    sg = _sigmoid(g)
    u0 = a * sg
    u0_down = _RowShifts(u0, row, up=False)
    uc = jnp.zeros_like(u0) + cb
    for k in range(CONV_KERNEL):
        uc = uc + w_ref[k:k + 1, :] * u0_down(CONV_KERNEL - 1 - k)
    ul, xh, r = _ln_fwd(uc, lg, lb)
    su = _sigmoid(ul)
    u = ul * su
    return sg, u0_down, ul, xh, r, su, u


def _conv_fwd(ag, conv_w, conv_b, ln_g, ln_b, norm_g, B, S, CW):
    def body(a_ref, g_ref, w_ref, cb_ref, lg_ref, lb_ref, ng_ref, o_ref):
        row = lax.broadcasted_iota(jnp.int32, (S, CW), 0)
        _, _, _, _, _, _, u = _conv_branch_fwd_math(a_ref[0], g_ref[0], w_ref, cb_ref[...], lg_ref[...],
                                                    lb_ref[...], row)
        rr = lax.rsqrt(jnp.mean(u * u, axis=-1, keepdims=True) + LN_EPS)
        o_ref[0] = (u * rr * ng_ref[...]).astype(BF16)

    vec = pl.BlockSpec((1, CW), lambda b: (0, 0))
    out = pl.pallas_call(
        body, grid=(B,),
        in_specs=[pl.BlockSpec((1, S, CW), lambda b: (b, 0, 0)), pl.BlockSpec((1, S, CW), lambda b: (b, 0, 1)),
                  pl.BlockSpec((CONV_KERNEL, CW), lambda b: (0, 0)), vec, vec, vec, vec],
        out_specs=pl.BlockSpec((1, S, CW), lambda b: (b, 0, 0)),
        out_shape=jax.ShapeDtypeStruct((B, S, CW), BF16),
        compiler_params=_params(1), name="conv_fwd",
    )(ag.reshape(B, S, 2 * CW), ag.reshape(B, S, 2 * CW), conv_w, conv_b, ln_g, ln_b, norm_g)
    return out.reshape(B * S, CW)


def _conv_bwd(ag, dmc, conv_w, conv_b, ln_g, ln_b, norm_g, B, S, CW):
    def body(a_ref, g_ref, dm_ref, w_ref, cb_ref, lg_ref, lb_ref, ng_ref,
             dag_ref, dw_ref, dcb_ref, dlg_ref, dlb_ref, dng_ref):
        b = pl.program_id(0)
        row = lax.broadcasted_iota(jnp.int32, (S, CW), 0)
        a, g = a_ref[0], g_ref[0]
        sg, u0_down, ul, xh, r, su, u = _conv_branch_fwd_math(a, g, w_ref, cb_ref[...], lg_ref[...], lb_ref[...], row)
        rr = lax.rsqrt(jnp.mean(u * u, axis=-1, keepdims=True) + LN_EPS)
        dm = dm_ref[0]
        dxn = dm * ng_ref[...]
        du = rr * (dxn - u * (rr * rr) * jnp.mean(dxn * u, axis=-1, keepdims=True))
        dul = du * su * (1.0 + ul * (1.0 - su))
        duc = _ln_bwd(dul, xh, r, lg_ref[...])
        first = b == 0
        _accumulate(dng_ref, first, jnp.sum(dm * u * rr, axis=0, keepdims=True))
        _accumulate(dlg_ref, first, jnp.sum(dul * xh, axis=0, keepdims=True))
        _accumulate(dlb_ref, first, jnp.sum(dul, axis=0, keepdims=True))
        _accumulate(dcb_ref, first, jnp.sum(duc, axis=0, keepdims=True))

        @pl.when(first)
        def _():
            dw_ref[...] = jnp.zeros_like(dw_ref)

        duc_up = _RowShifts(duc, row, up=True)
        du0 = jnp.zeros_like(duc)
        for k in range(CONV_KERNEL):
            sh = CONV_KERNEL - 1 - k
            dw_ref[k:k + 1, :] += jnp.sum(duc * u0_down(sh), axis=0, keepdims=True)
            du0 = du0 + w_ref[k:k + 1, :] * duc_up(sh)
        dag_ref[0, :, :CW] = du0 * sg
        dag_ref[0, :, CW:] = du0 * a * sg * (1.0 - sg)

    vec = pl.BlockSpec((1, CW), lambda b: (0, 0))
    wspec = pl.BlockSpec((CONV_KERNEL, CW), lambda b: (0, 0))
    agv = ag.reshape(B, S, 2 * CW)
    res = pl.pallas_call(
        body, grid=(B,),
        in_specs=[pl.BlockSpec((1, S, CW), lambda b: (b, 0, 0)), pl.BlockSpec((1, S, CW), lambda b: (b, 0, 1)),
                  pl.BlockSpec((1, S, CW), lambda b: (b, 0, 0)), wspec, vec, vec, vec, vec],
        out_specs=[pl.BlockSpec((1, S, 2 * CW), lambda b: (b, 0, 0)), wspec, vec, vec, vec, vec],
        out_shape=[jax.ShapeDtypeStruct((B, S, 2 * CW), F32), jax.ShapeDtypeStruct((CONV_KERNEL, CW), F32)]
        + [jax.ShapeDtypeStruct((1, CW), F32)] * 4,
        compiler_params=_params(1), name="conv_bwd",
    )(agv, agv, dmc.reshape(B, S, CW), conv_w, conv_b, ln_g, ln_b, norm_g)
    return (res[0].reshape(B * S, 2 * CW),) + tuple(res[1:])


def _ffn_conv(x, w_ref, bias, row):
    down = x if isinstance(x, _RowShifts) else _RowShifts(x, row, up=False)
    y = jnp.zeros_like(down.x) + bias
    for k in range(FFN_CONV_KERNEL):
        y = y + w_ref[k:k + 1, :] * down(FFN_CONV_KERNEL - 1 - k)
    return y


def _ffn_specs(S, tc, nj, order):
    pick = (lambda b, j: (b, j)) if order == "bj" else (lambda j, b: (b, j))
    act = lambda off: pl.BlockSpec((1, S, tc), lambda *g: (pick(*g)[0], 0, off + pick(*g)[1]))
    cw = lambda off: pl.BlockSpec((FFN_CONV_KERNEL, tc), lambda *g: (0, off + pick(*g)[1]))
    cb = lambda off: pl.BlockSpec((1, tc), lambda *g: (0, off + pick(*g)[1]))
    return act, cw, cb


FFN_HALO = 16


def _half_sequences(S):
    if S < 8 * FFN_HALO:
        return [(0, S, 0, S)]
    h = S // 2
    return [(0, h + FFN_HALO, 0, h), (h - FFN_HALO, S, FFN_HALO, h)]


def _w_up_block_spec(w_up_sh, tc, off):
    _, D, cs = w_up_sh.shape
    assert cs % tc == 0
    bps = cs // tc
    return pl.BlockSpec((1, D, tc), lambda j: ((off + j) // bps, 0, (off + j) % bps))


def _ffn_fwd_fused(x1b, w_up_sh, cw, cb, B, S, DFF):
    tc = FFN_COLS
    nj = DFF // tc
    D = x1b.shape[1]

    def body(x_ref, wg_ref, wv_ref, cwg_ref, cwv_ref, cbg_ref, cbv_ref, o_ref, up_ref):
        w = jnp.concatenate([wg_ref[0], wv_ref[0]], axis=1)
        for b in range(B):
            for lo, hi, o0, on in _half_sequences(S):
                row = lax.broadcasted_iota(jnp.int32, (hi - lo, tc), 0)
                up = jnp.dot(x_ref[b, lo:hi, :], w, preferred_element_type=F32)
                up_ref[b, lo + o0:lo + o0 + on, :] = up[o0:o0 + on]
                gate = _ffn_conv(up[:, :tc], cwg_ref, cbg_ref[...], row)
                val = _ffn_conv(up[:, tc:], cwv_ref, cbv_ref[...], row)
                o_ref[b, lo + o0:lo + o0 + on, :] = (gate * _sigmoid(gate) * val).astype(BF16)[o0:o0 + on]

    cws = lambda off: pl.BlockSpec((FFN_CONV_KERNEL, tc), lambda j: (0, off + j))
    cbs = lambda off: pl.BlockSpec((1, tc), lambda j: (0, off + j))
    act, upre = pl.pallas_call(
        body, grid=(nj,),
        in_specs=[pl.BlockSpec((B, S, D), lambda j: (0, 0, 0), pipeline_mode=pl.Buffered(1)),
                  _w_up_block_spec(w_up_sh, tc, 0), _w_up_block_spec(w_up_sh, tc, nj),
                  cws(0), cws(nj), cbs(0), cbs(nj)],
        out_specs=[pl.BlockSpec((B, S, tc), lambda j: (0, 0, j)), pl.BlockSpec((B, S, 2 * tc), lambda j: (0, 0, j))],
        out_shape=[jax.ShapeDtypeStruct((B, S, DFF), BF16), jax.ShapeDtypeStruct((B, S, 2 * DFF), F32)],
        compiler_params=_params(1), name="ffn_fwd",
    )(x1b.reshape(B, S, D), w_up_sh, w_up_sh, cw, cw, cb, cb)
    return act.reshape(B * S, DFF), upre


def _ffn_bwd_fused(x1b, dz2b, upre, w_down, cw, cb, B, S, DFF):
    tc = FFN_COLS
    nj = DFF // tc
    D = x1b.shape[1]

    def body(x_ref, dz_ref, up_ref, wd_ref, cwg_ref, cwv_ref, cbg_ref, cbv_ref,
             dug_ref, duv_ref, dwu_ref, dwd_ref, dcw_ref, dcb_ref):
        first = pl.program_id(1) == 0
        dw_t = dwd = None
        dcb = [None, None]
        dcw = [[None] * FFN_CONV_KERNEL, [None] * FFN_CONV_KERNEL]
        add = lambda old, new: new if old is None else old + new
        for lo, hi, o0, on in _half_sequences(S):
            n = hi - lo
            own = slice(o0, o0 + on)
            row = lax.broadcasted_iota(jnp.int32, (n, tc), 0)
            x = x_ref[0, lo:hi, :]
            dz = dz_ref[0, lo:hi, :]
            ug = _RowShifts(up_ref[0, lo:hi, :tc], row, up=False)
            uv = _RowShifts(up_ref[0, lo:hi, tc:], row, up=False)
            gate = _ffn_conv(ug, cwg_ref, cbg_ref[...], row)
            val = _ffn_conv(uv, cwv_ref, cbv_ref[...], row)
            sg = _sigmoid(gate)
            act = (gate * sg * val).astype(BF16)
            dact = _dot(dz, wd_ref[...], "nt")
            dgate = dact * val * sg * (1.0 + gate * (1.0 - sg))
            dval = dact * gate * sg
            dupre = []
            for h, (dup, u_down, w_ref) in enumerate(((dgate, ug, cwg_ref), (dval, uv, cwv_ref))):
                dcb[h] = add(dcb[h], jnp.sum(dup[own], axis=0, keepdims=True))
                dup_up = _RowShifts(dup, row, up=True)
                acc = jnp.zeros_like(dup)
                for k in range(FFN_CONV_KERNEL):
                    sh = FFN_CONV_KERNEL - 1 - k
                    dcw[h][k] = add(dcw[h][k], jnp.sum((dup * u_down(sh))[own], axis=0, keepdims=True))
                    acc = acc + w_ref[k:k + 1, :] * dup_up(sh)
                dupre.append(acc.astype(BF16)[own])
            dug_ref[0, lo + o0:lo + o0 + on, :] = dupre[0]
            duv_ref[0, lo + o0:lo + o0 + on, :] = dupre[1]
            dw_t = add(dw_t, _dot(jnp.concatenate(dupre, axis=1), x[own], "tn"))
            dwd = add(dwd, _dot(act[own], dz[own], "tn"))
        _accumulate(dwu_ref.at[0], first, dw_t[:tc])
        _accumulate(dwu_ref.at[1], first, dw_t[tc:])
        _accumulate(dwd_ref, first, dwd)
        for h in range(2):
            _accumulate(dcb_ref.at[h], first, dcb[h])
            for k in range(FFN_CONV_KERNEL):
                _accumulate(dcw_ref.at[k, pl.ds(h, 1), :], first, dcw[h][k])

    act_s, cws, cbs = _ffn_specs(S, tc, nj, "jb")
    seq = pl.BlockSpec((1, S, D), lambda j, b: (b, 0, 0))
    res = pl.pallas_call(
        body, grid=(nj, B),
        in_specs=[seq, seq, pl.BlockSpec((1, S, 2 * tc), lambda j, b: (b, 0, j)),
                  pl.BlockSpec((tc, D), lambda j, b: (j, 0)), cws(0), cws(nj), cbs(0), cbs(nj)],
        out_specs=[act_s(0), act_s(0), pl.BlockSpec((2, tc, D), lambda j, b: (0, j, 0)),
                   pl.BlockSpec((tc, D), lambda j, b: (j, 0)),
                   pl.BlockSpec((FFN_CONV_KERNEL, 2, tc), lambda j, b: (0, 0, j)),
                   pl.BlockSpec((2, 1, tc), lambda j, b: (0, 0, j))],
        out_shape=[jax.ShapeDtypeStruct((B, S, DFF), BF16)] * 2
        + [jax.ShapeDtypeStruct((2, DFF, D), F32), jax.ShapeDtypeStruct((DFF, D), F32),
           jax.ShapeDtypeStruct((FFN_CONV_KERNEL, 2, DFF), F32), jax.ShapeDtypeStruct((2, 1, DFF), F32)],
        compiler_params=_params(2), name="ffn_bwd",
    )(x1b.reshape(B, S, D), dz2b.reshape(B, S, D), upre, w_down, cw, cw, cb, cb)
    flat = lambda t: t.reshape(B * S, DFF)
    return flat(res[0]), flat(res[1]), res[2], res[3], res[4], res[5]


def _dx1_ln1_bwd(dupre_g, dupre_v, w_up_sh, dz2, xh1, r1, ln1_g, tm, bg):
    T, D = dz2.shape
    NS, _, cs = w_up_sh.shape
    half = NS // 2
    DFF = dupre_g.shape[1]

    def body(refs, bg_hook):
        dug_ref, duv_ref, w_ref, dz2_ref, xh_ref, r_ref, g_ref, dz_ref, dzb_ref, dg_ref, db_ref = refs
        bg_hook(False)
        first = pl.program_id(0) == 0
        dg = db = None
        for rows in (slice(0, tm // 2), slice(tm // 2, tm)):
            dx1 = ALPHA * dz2_ref[rows, :]
            for k in range(NS):
                src = dug_ref if k < half else duv_ref
                c0 = (k % half) * cs
                dx1 = dx1 + _dot(src[rows, c0:c0 + cs], w_ref[k], "nt")
            xh = xh_ref[rows, :]
            dz = _ln_bwd(dx1, xh, r_ref[rows, 0:1], g_ref[...])
            dz_ref[rows, :] = dz
            dzb_ref[rows, :] = dz.astype(BF16)
            dg_h, db_h = jnp.sum(dx1 * xh, axis=0, keepdims=True), jnp.sum(dx1, axis=0, keepdims=True)
            dg, db = (dg_h, db_h) if dg is None else (dg + dg_h, db + db_h)
        _accumulate(dg_ref, first, dg)
        _accumulate(db_ref, first, db)
        bg_hook(True)

    row = pl.BlockSpec((tm, D), lambda i: (i, 0))
    vec = pl.BlockSpec((1, D), lambda i: (0, 0))
    du = pl.BlockSpec((tm, DFF), lambda i: (i, 0))
    return _hosted_call(
        body, bg, grid=(T // tm,),
        in_specs=[du, du, pl.BlockSpec((NS, D, cs), lambda i: (0, 0, 0), pipeline_mode=pl.Buffered(1)),
                  row, row, pl.BlockSpec((tm, LANES), lambda i: (i, 0)), vec],
        out_specs=[row, row, vec, vec],
        out_shape=[jax.ShapeDtypeStruct((T, D), F32), jax.ShapeDtypeStruct((T, D), BF16),
                   jax.ShapeDtypeStruct((1, D), F32), jax.ShapeDtypeStruct((1, D), F32)],
        scratch_shapes=[], operands=[dupre_g, dupre_v, w_up_sh, dz2, xh1, r1, ln1_g], name="mm_dx1_ln1_bwd")


def _dh_cat(dq, dk, dv, dag, tm):
    T, AW = dq.shape
    CW2 = dag.shape[1]
    W = 3 * AW + CW2

    def body(dq_ref, dk_ref, dv_ref, dag_ref, dh_ref, cs_ref):
        for c, ref in enumerate((dq_ref, dk_ref, dv_ref)):
            dh_ref[:, c * AW:(c + 1) * AW] = ref[...]
        dg = dag_ref[...]
        dh_ref[:, 3 * AW:] = dg.astype(BF16)
        _accumulate(cs_ref, pl.program_id(0) == 0, jnp.sum(dg, axis=0, keepdims=True))

    row = pl.BlockSpec((tm, AW), lambda i: (i, 0))
    return pl.pallas_call(
        body, grid=(T // tm,),
        in_specs=[row] * 3 + [pl.BlockSpec((tm, CW2), lambda i: (i, 0))],
        out_specs=[pl.BlockSpec((tm, W), lambda i: (i, 0)), pl.BlockSpec((1, CW2), lambda i: (0, 0))],
        out_shape=[jax.ShapeDtypeStruct((T, W), BF16), jax.ShapeDtypeStruct((1, CW2), F32)],
        compiler_params=_params(1), name="dh_cat",
    )(dq, dk, dv, dag)


def _local_step(x, target, rel_table, w_in_t, b_in, conv_w, conv_b, conv_ln_g, conv_ln_b, attn_norm_g,
                conv_norm_g, staged, ln1_g, ln1_b, ffn_cw, ffn_cb, ln2_g, ln2_b, ids):
    B, S, D = x.shape
    T = B * S
    AW = attn_norm_g.shape[-1]
    CW = conv_norm_g.shape[-1]
    H = AW // HEAD_DIM
    DFF = staged[2].shape[0] * staged[2].shape[1]
    INW = 3 * AW + 2 * CW
    xf = x.reshape(T, D)
    tf = target.reshape(T, D)
    tm = _row_tile(T, 512)
    tm_s = _row_tile(T, 256)

    bucket_np, mask_np = _bucket_tables()
    bucket = jnp.asarray(bucket_np)
    band_mask = jnp.asarray(mask_np)
    bias_all = _bias_build(rel_table.T, bucket, band_mask).reshape(3, H, ATTN_BLOCK, 2 * ATTN_BLOCK)

    def in_proj(n0, n, tn, rows, out_dtype, name):
        assert n0 % tn == 0 and n % tn == 0

        def epilogue(acc, i, j, extra_refs, out_refs):
            out_refs[0][...] = (acc + extra_refs[0][...]).astype(out_dtype)

        return _matmul_general(
            [(xf, (rows, D), lambda i, j, k: (i, 0)), (w_in_t, (tn, D), lambda i, j, k: (n0 // tn + j, 0))],
            lambda refs, i, j, k: _dot(refs[0][...], refs[1][...], "nt"),
            grid=(T // rows, n // tn, 1), tm=rows, tn=tn,
            extras=[(b_in, (1, tn), lambda i, j, k: (0, n0 // tn + j))],
            outs=[_plain_out(T, n, rows, tn, out_dtype)], epilogue=epilogue, name=name)[0]

    qkv = in_proj(0, 3 * AW, _col_tile(3 * AW, 1152), tm, BF16, "mm_qkv")
    ag = in_proj(3 * AW, 2 * CW, math.gcd(3 * AW, 2 * CW), _row_tile(T, 1024), F32, "mm_ag")

    attn, lse, w_out_g, w_up_sh, w_down_g = _attention_fwd(qkv, bias_all, B, S, AW, bg=_bg_gather(staged))
    w_out = w_out_g.reshape(D, D)
    w_down = w_down_g.reshape(DFF, D)
    mixed_c = _conv_fwd(ag, conv_w, conv_b, conv_ln_g, conv_ln_b, conv_norm_g, B, S, CW)

    def attn_rstd(a):
        return lax.rsqrt(jnp.mean(a * a, axis=-1, keepdims=True) + LN_EPS)

    def mixed_rows(attn_ref, mc_ref, gain_ref):
        a = attn_ref[...]
        return jnp.concatenate([(a * attn_rstd(a) * gain_ref[...]).astype(BF16), mc_ref[...]], axis=1)

    def ln1_epilogue(acc, i, j, extra_refs, out_refs):
        x_ref, g_ref, b_ref, a_ref = extra_refs
        x1, xh, r = _ln_fwd(acc + ALPHA * x_ref[...], g_ref[...], b_ref[...])
        out_refs[0][...] = x1
        out_refs[1][...] = x1.astype(BF16)
        out_refs[2][...] = xh
        out_refs[3][...] = jnp.broadcast_to(r, (tm_s, LANES))
        out_refs[4][...] = jnp.broadcast_to(attn_rstd(a_ref[...]), (tm_s, LANES))

    rowD = lambda i, j, k: (i, 0)
    vecD = lambda i, j, k: (0, 0)
    x1, x1b, xh1, r1, r_attn = _matmul_general(
        [(attn, (tm_s, AW), rowD), (mixed_c, (tm_s, CW), rowD), (attn_norm_g, (1, AW), vecD), (w_out, (D, D), vecD)],
        lambda refs, i, j, k: _dot(mixed_rows(refs[0], refs[1], refs[2]), refs[3][...], "nn"),
        grid=(T // tm_s, 1, 1), tm=tm_s, tn=D,
        extras=[(xf, (tm_s, D), rowD), (ln1_g, (1, D), vecD), (ln1_b, (1, D), vecD), (attn, (tm_s, AW), rowD)],
        outs=[((T, D), F32, (tm_s, D), rowD), ((T, D), BF16, (tm_s, D), rowD), ((T, D), F32, (tm_s, D), rowD),
              ((T, LANES), F32, (tm_s, LANES), rowD), ((T, LANES), F32, (tm_s, LANES), rowD)],
        epilogue=ln1_epilogue, name="mm_out_ln1")

    NS, _, cs = w_up_sh.shape
    half = NS // 2

    act, upre = _ffn_fwd_fused(x1b, w_up_sh, ffn_cw, ffn_cb, B, S, DFF)

    halves = [slice(0, tm // 2), slice(tm // 2, tm)]

    def ln2_epilogue(parts, i, j, extra_refs, out_refs):
        x1_ref, g_ref, b_ref, t_ref = extra_refs
        dz_ref, dzb_ref, loss_ref, dg_ref, db_ref = out_refs
        g = g_ref[...]
        sums = None
        for rows, acc in zip(halves, parts):
            y, xh, r = _ln_fwd(acc + ALPHA * x1_ref[rows, :], g, b_ref[...])
            diff = y - t_ref[rows, :]
            row_loss = jnp.sum(diff * diff, axis=1, keepdims=True)
            tile_loss = jnp.sum(row_loss, axis=0, keepdims=True) * (0.5 / D)
            dy = diff * (1.0 / D)
            dz = _ln_bwd(dy, xh, r, g)
            dz_ref[rows, :] = dz
            dzb_ref[rows, :] = dz.astype(BF16)
            vals = (jnp.broadcast_to(tile_loss, (1, LANES)), jnp.sum(dy * xh, axis=0, keepdims=True),
                    jnp.sum(dy, axis=0, keepdims=True))
            sums = vals if sums is None else tuple(a + b for a, b in zip(sums, vals))
        for ref, val in zip((loss_ref, dg_ref, db_ref), sums):
            _accumulate(ref, i == 0, val)

    dz2, dz2b, loss_part, d_ln2_g, d_ln2_b = _matmul_general(
        [(act, (tm, DFF), rowD), (w_down, (DFF, D), vecD)],
        lambda refs, i, j, k: tuple(_dot(refs[0][rows, :], refs[1][...], "nn") for rows in halves),
        grid=(T // tm, 1, 1), tm=tm, tn=D,
        extras=[(x1, (tm, D), rowD), (ln2_g, (1, D), vecD), (ln2_b, (1, D), vecD), (tf, (tm, D), rowD)],
        outs=[((T, D), F32, (tm, D), rowD), ((T, D), BF16, (tm, D), rowD),
              ((1, LANES), F32, (1, LANES), vecD), ((1, D), F32, (1, D), vecD), ((1, D), F32, (1, D), vecD)],
        epilogue=ln2_epilogue, name="mm_down_ln2_loss")

    dupre_g, dupre_v, d_w_up_t, d_w_down, d_ffn_cw2, d_ffn_cb2 = _ffn_bwd_fused(
        x1b, dz2b, upre, w_down, ffn_cw, ffn_cb, B, S, DFF)
    d_w_up_t = d_w_up_t.reshape(NS, cs, D)
    d_ffn_cw = d_ffn_cw2.reshape(FFN_CONV_KERNEL, 2 * DFF)
    d_ffn_cb = d_ffn_cb2.reshape(1, 2 * DFF)
    tk_t = _row_tile(T, 512)

    early = [d_w_up_t, d_w_down.reshape(NS, DFF // NS, D)]
    dz1, dz1b, d_ln1_g, d_ln1_b, *sib_e = _dx1_ln1_bwd(dupre_g, dupre_v, w_up_sh, dz2, xh1, r1, ln1_g, tm,
                                                       bg=_bg_sibling_exchange(early))
    chip_e = [_pair_sum(g, s, ids, name="pair_sum_" + n) for g, s, n in zip(early, sib_e, ("w_up", "w_down"))]

    def dw_out_epilogue(acc, i, j, extra_refs, out_refs):
        out_refs[0][...] = acc

    d_w_out = _matmul_general(
        [(attn, (tk_t, AW), lambda i, j, k: (k, 0)), (mixed_c, (tk_t, CW), lambda i, j, k: (k, 0)),
         (attn_norm_g, (1, AW), vecD), (dz1b, (tk_t, D), lambda i, j, k: (k, 0))],
        lambda refs, i, j, k: _dot(mixed_rows(refs[0], refs[1], refs[2]), refs[3][...], "tn"),
        grid=(1, 1, T // tk_t), tm=D, tn=D, outs=[_plain_out(D, D, D, D, F32)],
        epilogue=dw_out_epilogue, name="mm_dw_out")[0]
    early.append(d_w_out.reshape(NS, D // NS, D))
    def dmixed_epilogue(acc, i, j, extra_refs, out_refs):
        a_ref, r_ref, g_ref = extra_refs
        do_ref, dd_ref, dmc_ref, dg_ref = out_refs
        head_of = lambda axis: lax.broadcasted_iota(jnp.int32, (AW, AW), axis) // HEAD_DIM
        same_head = (head_of(0) == head_of(1)).astype(BF16)
        dm = acc[:, :AW]
        dmc_ref[...] = acc[:, AW:]
        a = a_ref[...]
        r = r_ref[:, 0:1]
        dxn = dm * g_ref[...]
        da = r * (dxn - a * (r * r) * jnp.mean(dxn * a, axis=-1, keepdims=True))
        do_ref[...] = da.astype(BF16)
        hi, lo = _split_hi_lo(da * a)
        dd_ref[...] = (jnp.dot(hi, same_head, preferred_element_type=F32)
                       + jnp.dot(lo, same_head, preferred_element_type=F32))
        _accumulate(dg_ref, i == 0, jnp.sum(dm * a * r, axis=0, keepdims=True))

    dattn, dd, dmc, d_attn_norm_g, sib_out = _matmul(
        dz1b, w_out, mode="nt", tm=tm, tn=D, tk=D,
        extras=[(attn, (tm, AW), rowD), (r_attn, (tm, LANES), rowD), (attn_norm_g, (1, AW), vecD)],
        outs=[((T, AW), BF16, (tm, AW), rowD), ((T, AW), F32, (tm, AW), rowD), ((T, CW), F32, (tm, CW), rowD),
              ((1, AW), F32, (1, AW), vecD)],
        epilogue=dmixed_epilogue, name="mm_dmixed", bg=_bg_sibling_exchange(early[2:]))
    sib_e.append(sib_out)
    chip_e.append(_pair_sum(early[2], sib_out, ids, name="pair_sum_w_out"))

    dag, d_conv_w, d_conv_b, d_conv_ln_g, d_conv_ln_b, d_conv_norm_g = _conv_bwd(
        ag, dmc, conv_w, conv_b, conv_ln_g, conv_ln_b, conv_norm_g, B, S, CW)

    dq, dk, dv, csq, csk, csv, dbias, *got_e = _attention_bwd(qkv, dattn, lse, dd, bias_all, B, S, AW,
                                                              bg=_bg_chip_exchange(chip_e))
    full_up, full_down, full_out = [_final_sum(g, s, r, ids, name="final_sum_" + n)
                                    for g, s, r, n in zip(early, sib_e, got_e, ("w_up", "w_down", "w_out"))]
    d_rel_table = _rel_grad(dbias.reshape(3, H, ATTN_BLOCK * 2 * ATTN_BLOCK), bucket).T
    dh, cs_ag = _dh_cat(dq, dk, dv, dag, tm_s)
    d_b_in = jnp.concatenate([csq, csk, csv, cs_ag], axis=1)

    d_w_in_t = _mm_plain(dh, xf, mode="tn", tm=_col_tile(INW, 1408), tn=D, tk=tk_t, out_dtype=F32, name="mm_dw_in")
    late = [d_w_in_t.reshape(NS, INW // NS, D)]
    sib_l = _sibling_exchange(late)
    chip_l = [_pair_sum(late[0], sib_l[0], ids, name="pair_sum_w_in")]
    small = dict(rel_table=d_rel_table, b_in=d_b_in, conv_w=d_conv_w, conv_b=d_conv_b, conv_ln_g=d_conv_ln_g,
                 conv_ln_b=d_conv_ln_b, attn_norm_g=d_attn_norm_g, conv_norm_g=d_conv_norm_g, ln1_g=d_ln1_g,
                 ln1_b=d_ln1_b, ffn_conv_w=d_ffn_cw, ffn_conv_b=d_ffn_cb, ln2_g=d_ln2_g, ln2_b=d_ln2_b)
    pack = _pack([loss_part] + [small[n] for n in SMALL_NAMES])

    def gx_epilogue(acc, i, j, extra_refs, out_refs):
        out_refs[0][...] = acc + ALPHA * extra_refs[0][...]

    grad_x, got_in, all_packs = _matmul(
        dh, w_in_t, mode="nn", tm=tm, tn=D, tk=INW, extras=[(dz1, (tm, D), rowD)],
        outs=[((T, D), F32, (tm, D), rowD)], epilogue=gx_epilogue, name="mm_grad_x",
        bg=_bg_chip_exchange(chip_l, pack))
    full_in = _final_sum(late[0], sib_l[0], got_in, ids, name="final_sum_w_in")
    return grad_x.reshape(B, S, D), [full_in, full_out, full_up, full_down], all_packs


def _place():
    return lax.axis_index("x"), lax.axis_index("y"), lax.axis_index("c")


CHIP_FLIPS = ((1, 0), (0, 1), (1, 1))


def _flip(v, f):
    return 1 - v if f else v


HBM_SPEC = pl.BlockSpec(memory_space=pl.ANY)
VMEM_SPEC = pl.BlockSpec(memory_space=pltpu.VMEM)
COMM_PARAMS = pltpu.CompilerParams(vmem_limit_bytes=VMEM_LIMIT)


def _gather_weights(big, small):
    nb, ns = len(big), len(small)

    def body(*refs):
        big_in = refs[:nb]
        small_in = refs[nb:nb + ns]
        big_out = refs[nb + ns:2 * nb + ns]
        small_out = refs[2 * nb + ns:2 * nb + 2 * ns]
        stages = refs[2 * nb + 2 * ns:3 * nb + 2 * ns]
        send_sems, recv_sems, local_sems = refs[3 * nb + 2 * ns:]
        x, y, c = _place()
        s_me = 2 * x + y
        sibling = (x, y, 1 - c)
        started, local_copies = [], []
        for a in range(nb):
            rh = big[a].shape[0] // 2
            lo = pl.multiple_of(c * rh, 16)
            stages[a][...] = big_in[a][pl.ds(lo, rh), :].astype(BF16)
            mine = big_out[a].at[s_me, pl.ds(lo, rh), :]
            loc = pltpu.make_async_copy(stages[a], mine, local_sems.at[a])
            loc.start()
            local_copies.append(loc)
            targets = [sibling] + [(_flip(x, fx), _flip(y, fy), c) for fx, fy in CHIP_FLIPS]
            for k, to in enumerate(targets):
                cp = pltpu.make_async_remote_copy(stages[a], mine, send_sems.at[a * 7 + k],
                                                  recv_sems.at[a * 7 + k], device_id=to, device_id_type=MESH)
                cp.start()
                started.append(cp)
        for a in range(ns):
            mine = small_out[a].at[s_me]
            loc = pltpu.make_async_copy(small_in[a], mine, local_sems.at[nb + a])
            loc.start()
            local_copies.append(loc)
            for k, (fx, fy) in enumerate(CHIP_FLIPS):
                cp = pltpu.make_async_remote_copy(small_in[a], mine, send_sems.at[nb * 7 + a * 3 + k],
                                                  recv_sems.at[nb * 7 + a * 3 + k],
                                                  device_id=(_flip(x, fx), _flip(y, fy), c), device_id_type=MESH)
                cp.start()
                started.append(cp)
        for a in range(nb):
            rh = big[a].shape[0] // 2
            lo = pl.multiple_of(c * rh, 16)
            for k, (fx, fy) in enumerate(CHIP_FLIPS):
                s_from = 2 * _flip(x, fx) + _flip(y, fy)
                got = big_out[a].at[s_from, pl.ds(lo, rh), :]
                pltpu.make_async_remote_copy(got, got, send_sems.at[a * 7 + 1 + k], recv_sems.at[a * 7 + 1 + k],
                                             device_id=sibling, device_id_type=MESH).wait_recv()
                fwd = pltpu.make_async_remote_copy(got, got, send_sems.at[a * 7 + 4 + k],
                                                   recv_sems.at[a * 7 + 4 + k], device_id=sibling,
                                                   device_id_type=MESH)
                fwd.start()
                started.append(fwd)
        for a in range(nb):
            rh = big[a].shape[0] // 2
            lo_sib = pl.multiple_of((1 - c) * rh, 16)
            for k in (0, 4, 5, 6):
                any_rows = big_out[a].at[s_me, pl.ds(lo_sib, rh), :]
                pltpu.make_async_remote_copy(any_rows, any_rows, send_sems.at[a * 7 + k], recv_sems.at[a * 7 + k],
                                             device_id=sibling, device_id_type=MESH).wait_recv()
        for a in range(ns):
            for k in range(3):
                pltpu.make_async_remote_copy(small_in[a], small_out[a].at[s_me], send_sems.at[nb * 7 + a * 3 + k],
                                             recv_sems.at[nb * 7 + a * 3 + k], device_id=sibling,
                                             device_id_type=MESH).wait_recv()
        for cp in started:
            cp.wait_send()
        for cp in local_copies:
            cp.wait()

    n_sem = nb * 7 + ns * 3
    out_shape = ([jax.ShapeDtypeStruct((N_SHARDS,) + w.shape, BF16) for w in big]
                 + [jax.ShapeDtypeStruct((N_SHARDS,) + w.shape, F32) for w in small])
    res = pl.pallas_call(
        body, in_specs=[VMEM_SPEC] * nb + [HBM_SPEC] * ns, out_specs=[HBM_SPEC] * (nb + ns),
        out_shape=out_shape,
        scratch_shapes=[pltpu.VMEM((w.shape[0] // 2, w.shape[1]), BF16) for w in big]
        + [pltpu.SemaphoreType.DMA((n_sem,)), pltpu.SemaphoreType.DMA((n_sem,)),
           pltpu.SemaphoreType.DMA((nb + ns,))],
        compiler_params=COMM_PARAMS, name="gather_weights",
    )(*big, *small)
    return res[:nb], res[nb:]


def _sibling_exchange(grads):
    n = len(grads)

    def body(*refs):
        g_in = refs[:n]
        got = refs[n:2 * n]
        send_sems, recv_sems = refs[2 * n:]
        x, y, c = _place()
        cps = []
        for a in range(n):
            rh = grads[a].shape[1] // 2
            lo = pl.multiple_of((1 - c) * rh, 8)
            cp = pltpu.make_async_remote_copy(g_in[a].at[:, pl.ds(lo, rh), :], got[a], send_sems.at[a],
                                              recv_sems.at[a], device_id=(x, y, 1 - c), device_id_type=MESH)
            cp.start()
            cps.append(cp)
        for cp in cps:
            cp.wait()

    return pl.pallas_call(
        body, in_specs=[HBM_SPEC] * n, out_specs=[HBM_SPEC] * n,
        out_shape=[jax.ShapeDtypeStruct((N_SHARDS, g.shape[1] // 2, g.shape[2]), F32) for g in grads],
        scratch_shapes=[pltpu.SemaphoreType.DMA((n,)), pltpu.SemaphoreType.DMA((n,))],
        compiler_params=COMM_PARAMS, name="sibling_exchange",
    )(*grads)


def _sibling_assemble(fulls):
    n = len(fulls)

    def body(*refs):
        full = refs[n:2 * n]
        send_sems, recv_sems = refs[2 * n:]
        x, y, c = _place()
        cps = []
        for a in range(n):
            rh = fulls[a].shape[0] // 2
            mine = full[a].at[pl.ds(pl.multiple_of(c * rh, 8), rh), :]
            cp = pltpu.make_async_remote_copy(mine, mine, send_sems.at[a], recv_sems.at[a],
                                              device_id=(x, y, 1 - c), device_id_type=MESH)
            cp.start()
            cps.append(cp)
        for cp in cps:
            cp.wait()

    return pl.pallas_call(
        body, in_specs=[HBM_SPEC] * n, out_specs=[HBM_SPEC] * n,
        out_shape=[jax.ShapeDtypeStruct(f.shape, F32) for f in fulls],
        input_output_aliases={a: a for a in range(n)},
        scratch_shapes=[pltpu.SemaphoreType.DMA((n,)), pltpu.SemaphoreType.DMA((n,))],
        compiler_params=COMM_PARAMS, name="sibling_assemble",
    )(*fulls)


def _remote(ref_src, ref_dst, send_sems, recv_sems, k, to):
    return pltpu.make_async_remote_copy(ref_src, ref_dst, send_sems.at[k], recv_sems.at[k], device_id=to,
                                        device_id_type=MESH)


def _stage_half(w, ids, name):
    R, C = w.shape
    rh = R // 2
    rt = _half_tile(rh)
    nt = rh // rt

    def body(ids_ref, w_ref, o_ref):
        o_ref[0] = w_ref[...].astype(BF16)

    grid_spec = pltpu.PrefetchScalarGridSpec(
        num_scalar_prefetch=1, grid=(nt,),
        in_specs=[pl.BlockSpec((rt, C), lambda i, ids: (ids[2] * nt + i, 0))],
        out_specs=pl.BlockSpec((1, rt, C), lambda i, ids: (2 * ids[0] + ids[1], ids[2] * nt + i, 0)))
    return pl.pallas_call(body, grid_spec=grid_spec, out_shape=jax.ShapeDtypeStruct((N_SHARDS, R, C), BF16),
                          compiler_params=_params(1), name=name)(ids, w)


def _bg_gather(staged):
    n = len(staged)

    def run(step, n_steps, ins, outs, send_sems, recv_sems, local_sems, post):
        x, y, c = _place()
        s_me = 2 * x + y
        sibling = (x, y, 1 - c)
        chips = [(_flip(x, fx), _flip(y, fy)) for fx, fy in CHIP_FLIPS]

        def rows(a, s, half):
            rh = staged[a].shape[1] // 2
            return outs[a].at[s, pl.ds(pl.multiple_of(half * rh, 16), rh), :]

        def copy(a, k, ref, to):
            return _remote(ref, ref, send_sems, recv_sems, a * 7 + k, to)

        if not post:
            @pl.when(step == 0)
            def _():
                for a in range(n):
                    mine = rows(a, s_me, c)
                    copy(a, 0, mine, sibling).start()
                    for k, (px, py) in enumerate(chips):
                        copy(a, 1 + k, mine, (px, py, c)).start()

            @pl.when(step == max(n_steps - 2, 0))
            def _():
                for a in range(n):
                    for k, (px, py) in enumerate(chips):
                        got = rows(a, 2 * px + py, c)
                        copy(a, 1 + k, got, sibling).wait_recv()
                        copy(a, 4 + k, got, sibling).start()
        else:
            @pl.when(step == n_steps - 1)
            def _():
                for a in range(n):
                    for k in (0, 4, 5, 6):
                        copy(a, k, rows(a, s_me, 1 - c), sibling).wait_recv()
                    for k in range(7):
                        copy(a, k, rows(a, s_me, c), sibling).wait_send()

    return _Background(staged, [jax.ShapeDtypeStruct(g.shape, g.dtype) for g in staged],
                       {a: a for a in range(n)}, 7 * n, run)


def _bg_sibling_exchange(grads):
    n = len(grads)

    def run(step, n_steps, ins, outs, send_sems, recv_sems, local_sems, post):
        x, y, c = _place()

        def copy(a):
            rh = grads[a].shape[1] // 2
            lo = pl.multiple_of((1 - c) * rh, 8)
            return _remote(ins[a].at[:, pl.ds(lo, rh), :], outs[a], send_sems, recv_sems, a, (x, y, 1 - c))

        if not post:
            @pl.when(step == 0)
            def _():
                for a in range(n):
                    copy(a).start()
        else:
            @pl.when(step == n_steps - 1)
            def _():
                for a in range(n):
                    copy(a).wait()

    return _Background(grads, [jax.ShapeDtypeStruct((N_SHARDS, g.shape[1] // 2, g.shape[2]), F32) for g in grads],
                       {}, n, run)


def _bg_chip_exchange(chip_parts, pack=None):
    n = len(chip_parts)

    def run(step, n_steps, ins, outs, send_sems, recv_sems, local_sems, post):
        x, y, c = _place()
        me = 4 * x + 2 * y + c

        def copies():
            cps = []
            for a in range(n):
                for k, (fx, fy) in enumerate(CHIP_FLIPS):
                    px, py = _flip(x, fx), _flip(y, fy)
                    cps.append(_remote(ins[a].at[2 * px + py], outs[a].at[k], send_sems, recv_sems, a * 3 + k,
                                       (px, py, c)))
            if pack is not None:
                for m in range(1, N_DEV):
                    to = (_flip(x, m & 4), _flip(y, m & 2), _flip(c, m & 1))
                    cps.append(_remote(ins[n], outs[n].at[me], send_sems, recv_sems, n * 3 + m - 1, to))
            return cps

        def local():
            return pltpu.make_async_copy(ins[n], outs[n].at[me], local_sems.at[0])

        if not post:
            @pl.when(step == 0)
            def _():
                for cp in copies():
                    cp.start()
                if pack is not None:
                    local().start()
        else:
            @pl.when(step == n_steps - 1)
            def _():
                for cp in copies():
                    cp.wait()
                if pack is not None:
                    local().wait()

    in_arrays = list(chip_parts) + ([pack] if pack is not None else [])
    out_shapes = [jax.ShapeDtypeStruct((3,) + p.shape[1:], BF16) for p in chip_parts]
    if pack is not None:
        out_shapes.append(jax.ShapeDtypeStruct((N_DEV, pack.shape[0], LANES), F32))
    return _Background(in_arrays, out_shapes, {}, n * 3 + N_DEV - 1, run)


def _half_tile(rh, mult=16, want=256):
    best = None
    for t in range(mult, min(rh, want) + 1, mult):
        if rh % t == 0:
            best = t
    return best if best is not None else rh


def _pair_sum(g, sib, ids, name):
    _, R, C = g.shape
    rh = R // 2
    rt = _half_tile(rh)
    nt = rh // rt

    def body(ids_ref, g_ref, s_ref, o_ref):
        o_ref[...] = (g_ref[...] + s_ref[...]).astype(BF16)

    grid_spec = pltpu.PrefetchScalarGridSpec(
        num_scalar_prefetch=1, grid=(N_SHARDS, nt),
        in_specs=[pl.BlockSpec((1, rt, C), lambda s, i, ids: (s, ids[2] * nt + i, 0)),
                  pl.BlockSpec((1, rt, C), lambda s, i, ids: (s, i, 0))],
        out_specs=pl.BlockSpec((1, rt, C), lambda s, i, ids: (s, i, 0)))
    return pl.pallas_call(body, grid_spec=grid_spec, out_shape=jax.ShapeDtypeStruct((N_SHARDS, rh, C), BF16),
                          compiler_params=_params(2), name=name)(ids, g, sib)


def _final_sum(g, sib, got, ids, name):
    _, R, C = g.shape
    rh = R // 2
    rt = _half_tile(rh)
    nt = rh // rt

    def body(ids_ref, g_ref, s_ref, r_ref, o_ref):
        tot = g_ref[0] + s_ref[0]
        for k in range(3):
            tot = tot + r_ref[k].astype(F32)
        o_ref[...] = tot

    grid_spec = pltpu.PrefetchScalarGridSpec(
        num_scalar_prefetch=1, grid=(nt,),
        in_specs=[pl.BlockSpec((1, rt, C), lambda i, ids: (2 * ids[0] + ids[1], ids[2] * nt + i, 0)),
                  pl.BlockSpec((1, rt, C), lambda i, ids: (2 * ids[0] + ids[1], i, 0)),
                  pl.BlockSpec((3, rt, C), lambda i, ids: (0, i, 0))],
        out_specs=pl.BlockSpec((rt, C), lambda i, ids: (ids[2] * nt + i, 0)))
    return pl.pallas_call(body, grid_spec=grid_spec, out_shape=jax.ShapeDtypeStruct((R, C), F32),
                          compiler_params=_params(1), name=name)(ids, g, sib, got)


def _sum_packs(all_packs):
    def body(p_ref, o_ref):
        tot = p_ref[0]
        for i in range(1, N_DEV):
            tot = tot + p_ref[i]
        o_ref[...] = tot

    return pl.pallas_call(body, in_specs=[VMEM_SPEC], out_specs=VMEM_SPEC,
                          out_shape=jax.ShapeDtypeStruct(all_packs.shape[1:], F32), name="sum_packs")(all_packs)


def _adamw(w, g, m, v, name, g_transposed=False):
    R, C = w.shape
    rt = _half_tile(R, mult=LANES if g_transposed else 8, want=256)

    def body(w_ref, g_ref, m_ref, v_ref, g_out_ref, d_ref, nm_ref, nv_ref):
        gg = g_ref[...].T if g_transposed else g_ref[...]
        g_out_ref[...] = gg
        d_ref[...], nm_ref[...], nv_ref[...] = _adamw_update(w_ref[...], gg, m_ref[...], v_ref[...])

    spec = pl.BlockSpec((rt, C), lambda i: (i, 0))
    g_spec = pl.BlockSpec((C, rt), lambda i: (0, i)) if g_transposed else spec
    return pl.pallas_call(body, grid=(R // rt,), in_specs=[spec, g_spec, spec, spec], out_specs=[spec] * 4,
                          out_shape=[jax.ShapeDtypeStruct((R, C), F32)] * 4,
                          compiler_params=_params(1), name=name)(w, g, m, v)


def _adamw_update(w, g, m, v):
    nm = ADAM_B1 * m + (1.0 - ADAM_B1) * g
    nv = ADAM_B2 * v + (1.0 - ADAM_B2) * (g * g)
    m_hat = nm / (1.0 - ADAM_B1 ** ADAM_STEP)
    v_hat = nv / (1.0 - ADAM_B2 ** ADAM_STEP)
    return -ADAM_LR * (m_hat / (jnp.sqrt(v_hat) + ADAM_EPS) + ADAM_WD * w), nm, nv


def _adamw_many(ws, gs, ms, vs, name):
    n = len(ws)

    def body(*refs):
        for i in range(n):
            d, nm, nv = _adamw_update(refs[i][...], refs[n + i][...], refs[2 * n + i][...], refs[3 * n + i][...])
            refs[4 * n + i][...] = d
            refs[5 * n + i][...] = nm
            refs[6 * n + i][...] = nv

    return pl.pallas_call(body, in_specs=[VMEM_SPEC] * (4 * n), out_specs=[VMEM_SPEC] * (3 * n),
                          out_shape=[jax.ShapeDtypeStruct(w.shape, F32) for w in ws] * 3, name=name,
                          )(*ws, *gs, *ms, *vs)


def _pack(pieces):
    rows = []
    for p in pieces:
        flat = p.reshape(-1)
        pad = (-flat.shape[0]) % LANES
        if pad:
            flat = jnp.concatenate([flat, jnp.zeros((pad,), F32)])
        rows.append(flat.reshape(-1, LANES))
    total = sum(r.shape[0] for r in rows)
    pad_rows = (-total) % 8
    if pad_rows:
        rows.append(jnp.zeros((pad_rows, LANES), F32))
    return jnp.concatenate(rows, axis=0)


def _unpack(buf, shapes):
    out, r0 = [], 0
    for shp in shapes:
        n = int(np.prod(shp))
        nr = -(-n // LANES)
        out.append(buf[r0:r0 + nr].reshape(-1)[:n].reshape(shp))
        r0 += nr
    return out


SMALL_NAMES = ("rel_table", "b_in", "conv_w", "conv_b", "conv_ln_g", "conv_ln_b", "attn_norm_g", "conv_norm_g",
               "ln1_g", "ln1_b", "ffn_conv_w", "ffn_conv_b", "ln2_g", "ln2_b")
BIG_NAMES = ("w_in", "w_out", "w_up", "w_down")
WEIGHT_ORDER = ("rel_table", "w_in", "b_in", "conv_w", "conv_b", "conv_ln_g", "conv_ln_b", "attn_norm_g",
                "conv_norm_g", "w_out", "ln1_g", "ln1_b", "w_up", "ffn_conv_w", "ffn_conv_b", "w_down",
                "ln2_g", "ln2_b")


def kernel(x, rel_table, w_in, b_in, conv_w, conv_b, conv_ln_g, conv_ln_b, attn_norm_g, conv_norm_g, w_out, ln1_g, ln1_b, w_up, ffn_conv_w, ffn_conv_b, w_down, ln2_g, ln2_b, loss_target, m_rel_table, m_w_in, m_b_in, m_conv_w, m_conv_b, m_conv_ln_g, m_conv_ln_b, m_attn_norm_g, m_conv_norm_g, m_w_out, m_ln1_g, m_ln1_b, m_w_up, m_ffn_conv_w, m_ffn_conv_b, m_w_down, m_ln2_g, m_ln2_b, v_rel_table, v_w_in, v_b_in, v_conv_w, v_conv_b, v_conv_ln_g, v_conv_ln_b, v_attn_norm_g, v_conv_norm_g, v_w_out, v_ln1_g, v_ln1_b, v_w_up, v_ffn_conv_w, v_ffn_conv_b, v_w_down, v_ln2_g, v_ln2_b):
    args = dict(locals())
    weights = {n: args[n] for n in WEIGHT_ORDER}
    moms = {n: args["m_" + n] for n in WEIGHT_ORDER}
    vels = {n: args["v_" + n] for n in WEIGHT_ORDER}
    xi, yi, ci = _place()
    ids = jnp.stack([xi, yi, ci]).astype(jnp.int32)
    shard = 2 * xi + yi
    D = x.shape[-1]
    DFF = w_down.shape[1] * N_SHARDS
    CW = conv_norm_g.shape[-1]

    tr = lambda t: jnp.transpose(t[0])
    (g_in,), (g_cw, g_fcw) = _gather_weights([tr(w_in)], [conv_w[0], ffn_conv_w[0]])
    cols = lambda t: jnp.transpose(t, (1, 0, 2)).reshape(t.shape[1], N_SHARDS * t.shape[2])
    staged = [_stage_half(w[0], ids, name="stage_" + n) for w, n in ((w_out, "w_out"), (w_up, "w_up"),
                                                                     (w_down, "w_down"))]

    grad_x, fulls, all_packs = _local_step(
        x, loss_target, rel_table, g_in.reshape(-1, D), b_in, cols(g_cw), conv_b, conv_ln_g, conv_ln_b, attn_norm_g,
        conv_norm_g, staged, ln1_g, ln1_b, cols(g_fcw), ffn_conv_b, ln2_g, ln2_b, ids)
    big_grads = dict(zip(BIG_NAMES, _sibling_assemble(fulls)))

    summed = _sum_packs(all_packs)
    full_shapes = {n: weights[n].shape for n in SMALL_NAMES}
    full_shapes["conv_w"] = (1, CONV_KERNEL, CW)
    full_shapes["ffn_conv_w"] = (1, FFN_CONV_KERNEL, 2 * DFF)
    un = _unpack(summed, [(1, LANES)] + [full_shapes[n] for n in SMALL_NAMES])
    loss = un[0][0, 0]
    small_grads = dict(zip(SMALL_NAMES, un[1:]))
    for n in ("conv_w", "ffn_conv_w"):
        width = weights[n].shape[-1]
        small_grads[n] = lax.dynamic_slice_in_dim(small_grads[n], shard * width, width, axis=2)

    grads, delta, new_m, new_v = {}, {}, {}, {}
    for n in BIG_NAMES:
        shp = weights[n].shape
        g2 = big_grads[n]
        if n == "w_in":
            res = [jnp.transpose(t) for t in _adamw(tr(weights[n]), g2, tr(moms[n]), tr(vels[n]), name="adamw_" + n)]
        else:
            res = _adamw(weights[n][0], g2, moms[n][0], vels[n][0], name="adamw_" + n, g_transposed=n == "w_up")
        grads[n], delta[n], new_m[n], new_v[n] = (t.reshape(shp) for t in res)
    pick = lambda src: [src[n] for n in SMALL_NAMES]
    small_out = _adamw_many(pick(weights), pick(small_grads), pick(moms), pick(vels), name="adamw_small")
    ns = len(SMALL_NAMES)
    for tgt, part in ((delta, small_out[:ns]), (new_m, small_out[ns:2 * ns]), (new_v, small_out[2 * ns:])):
        tgt.update(zip(SMALL_NAMES, part))
    grads.update(small_grads)

    return (loss, grad_x, *[grads[n] for n in WEIGHT_ORDER], *[delta[n] for n in WEIGHT_ORDER],
            *[new_m[n] for n in WEIGHT_ORDER], *[new_v[n] for n in WEIGHT_ORDER])
```

```python
import math

import numpy as np
import jax
import jax.numpy as jnp
from jax import lax
from jax.experimental import pallas as pl
from jax.experimental.pallas import tpu as pltpu

F32 = jnp.float32
BF16 = jnp.bfloat16
MESH = pl.DeviceIdType.MESH

HEAD_DIM = 64
LANES = 128
ATTN_BLOCK = 128
DILATED_CONFIGS = ((128, 1), (512, 4), (2048, 16))
CONV_KERNEL = 31
FFN_CONV_KERNEL = 3
REL_BUCKETS = 32
REL_MAX_DIST = 2048
DEPTH = 1
ALPHA = (2 * DEPTH) ** 0.25
LN_EPS = 1e-5
NEG_INF = -1e30
QK_SCALE = 1.0 / math.sqrt(HEAD_DIM)
ADAM_LR = 0.001
ADAM_B1 = 0.9
ADAM_B2 = 0.999
ADAM_EPS = 1e-08
ADAM_WD = 0.01
ADAM_STEP = 10
VMEM_LIMIT = 52 * 1024 * 1024
FFN_COLS = 128
N_SHARDS = 4
N_DEV = 8


def _params(n_axes):
    return pltpu.CompilerParams(dimension_semantics=("arbitrary",) * n_axes,
                                vmem_limit_bytes=VMEM_LIMIT)


MM_DIMS = {"nn": (((1,), (0,)), ((), ())), "nt": (((1,), (1,)), ((), ())), "tn": (((0,), (0,)), ((), ()))}


class _Background:
    def __init__(self, in_arrays, out_shapes, aliases, n_sems, run, n_local=1):
        self.in_arrays, self.out_shapes, self.aliases = list(in_arrays), list(out_shapes), dict(aliases)
        self.n_sems, self.n_local, self.run = n_sems, n_local, run

    def scratch(self):
        return [pltpu.SemaphoreType.DMA((self.n_sems,)), pltpu.SemaphoreType.DMA((self.n_sems,)),
                pltpu.SemaphoreType.DMA((self.n_local,))]


def _hosted_call(body, bg, *, grid, in_specs, out_specs, out_shape, scratch_shapes, operands, name):
    n_in, n_out, n_scr = len(in_specs), len(out_specs), len(scratch_shapes)
    if bg is None:
        return pl.pallas_call(lambda *refs: body(refs, lambda post: None), grid=grid, in_specs=in_specs,
                              out_specs=out_specs, out_shape=out_shape, scratch_shapes=scratch_shapes,
                              compiler_params=_params(len(grid)), name=name)(*operands)
    nb_in, nb_out = len(bg.in_arrays), len(bg.out_shapes)
    n_steps = int(np.prod(grid))

    def full_body(*refs):
        own = refs[:n_in] + refs[n_in + nb_in:n_in + nb_in + n_out] \
            + refs[n_in + nb_in + n_out + nb_out:n_in + nb_in + n_out + nb_out + n_scr]
        bg_in = refs[n_in:n_in + nb_in]
        bg_out = refs[n_in + nb_in + n_out:n_in + nb_in + n_out + nb_out]
        sems = refs[n_in + nb_in + n_out + nb_out + n_scr:]
        step = pl.program_id(0)
        for ax in range(1, len(grid)):
            step = step * grid[ax] + pl.program_id(ax)

        def hook(post):
            bg.run(step, n_steps, bg_in, bg_out, *sems, post)

        body(own, hook)

    res = pl.pallas_call(
        full_body, grid=grid, in_specs=list(in_specs) + [HBM_SPEC] * nb_in,
        out_specs=list(out_specs) + [HBM_SPEC] * nb_out, out_shape=list(out_shape) + bg.out_shapes,
        input_output_aliases={n_in + a: n_out + o for a, o in bg.aliases.items()},
        scratch_shapes=list(scratch_shapes) + bg.scratch(), compiler_params=_params(len(grid)), name=name,
    )(*operands, *bg.in_arrays)
    return res


def _matmul_general(ins, part_fn, *, grid, tm, tn, outs, epilogue, extras=(), name, bg=None):
    nk = grid[2]
    n_in, n_extra = len(ins), len(extras)

    def body(refs, bg_hook):
        in_refs = refs[:n_in]
        rest = refs[n_in:]
        extra_refs = rest[:n_extra]
        out_refs = rest[n_extra:n_extra + len(outs)]
        acc_ref = rest[-1]
        i, j, k = pl.program_id(0), pl.program_id(1), pl.program_id(2)
        bg_hook(False)
        part = part_fn(in_refs, i, j, k)
        if nk == 1:
            epilogue(part, i, j, extra_refs, out_refs)
        else:
            @pl.when(k == 0)
            def _():
                acc_ref[...] = part

            @pl.when(k > 0)
            def _():
                acc_ref[...] += part

            @pl.when(k == nk - 1)
            def _():
                epilogue(acc_ref[...], i, j, extra_refs, out_refs)
        bg_hook(True)

    in_specs = [pl.BlockSpec(bs, im) for (_, bs, im) in list(ins) + list(extras)]
    out_specs = [pl.BlockSpec(bs, im) for (_, _, bs, im) in outs]
    out_shape = [jax.ShapeDtypeStruct(s, d) for (s, d, _, _) in outs]
    return _hosted_call(body, bg, grid=grid, in_specs=in_specs, out_specs=out_specs, out_shape=out_shape,
                        scratch_shapes=[pltpu.VMEM((tm, tn), F32)],
                        operands=[e[0] for e in ins] + [e[0] for e in extras], name=name)


def _dot(a, b, mode):
    return lax.dot_general(a.astype(BF16), b.astype(BF16), MM_DIMS[mode], preferred_element_type=F32)


def _matmul(a, b, *, mode, tm, tn, tk, outs, epilogue, extras=(), name, bg=None):
    if mode == "tn":
        K, M = a.shape
        N = b.shape[1]
        ins = [(a, (tk, tm), lambda i, j, k: (k, i)), (b, (tk, tn), lambda i, j, k: (k, j))]
    elif mode == "nt":
        M, K = a.shape
        N = b.shape[0]
        ins = [(a, (tm, tk), lambda i, j, k: (i, k)), (b, (tn, tk), lambda i, j, k: (j, k))]
    else:
        M, K = a.shape
        N = b.shape[1]
        ins = [(a, (tm, tk), lambda i, j, k: (i, k)), (b, (tk, tn), lambda i, j, k: (k, j))]
    assert M % tm == 0 and N % tn == 0 and K % tk == 0, (name, M, N, K, tm, tn, tk)

    def part_fn(in_refs, i, j, k):
        return _dot(in_refs[0][...], in_refs[1][...], mode)

    return _matmul_general(ins, part_fn, grid=(M // tm, N // tn, K // tk), tm=tm, tn=tn, outs=outs,
                           epilogue=epilogue, extras=extras, name=name, bg=bg)


def _plain_out(M, N, tm, tn, dtype):
    return ((M, N), dtype, (tm, tn), lambda i, j, k: (i, j))


def _mm_plain(a, b, *, mode, tm, tn, tk, out_dtype, name, bias=None, bg=None):
    if mode == "tn":
        M, N = a.shape[1], b.shape[1]
    elif mode == "nt":
        M, N = a.shape[0], b.shape[0]
    else:
        M, N = a.shape[0], b.shape[1]
    extras = []
    if bias is not None:
        extras.append((bias, (1, tn), lambda i, j, k: (0, j)))

    def epilogue(acc, i, j, extra_refs, out_refs):
        if bias is not None:
            acc = acc + extra_refs[0][...]
        out_refs[0][...] = acc.astype(out_dtype)

    res = _matmul(a, b, mode=mode, tm=tm, tn=tn, tk=tk, outs=[_plain_out(M, N, tm, tn, out_dtype)],
                  epilogue=epilogue, extras=extras, name=name, bg=bg)
    return res[0] if bg is None else res


def _row_tile(T, want):
    t = min(T, want)
    while T % t:
        t //= 2
    return t


def _col_tile(N, want):
    if N <= want:
        return N
    best = None
    for c in range(LANES, want + 1, LANES):
        if N % c == 0:
            best = c
    return best if best is not None else N


def _accumulate(ref, first, val):
    @pl.when(first)
    def _():
        ref[...] = val

    @pl.when(jnp.logical_not(first))
    def _():
        ref[...] += val


def _ln_fwd(z, g, b):
    mu = jnp.mean(z, axis=-1, keepdims=True)
    zc = z - mu
    var = jnp.mean(zc * zc, axis=-1, keepdims=True)
    r = lax.rsqrt(var + LN_EPS)
    xh = zc * r
    return xh * g + b, xh, r


def _ln_bwd(dy, xh, r, g):
    dxh = dy * g
    m1 = jnp.mean(dxh, axis=-1, keepdims=True)
    m2 = jnp.mean(dxh * xh, axis=-1, keepdims=True)
    return r * (dxh - m1 - xh * m2)


def _sigmoid(x):
    return 1.0 / (1.0 + jnp.exp(-x))


def _bucket_tables():
    exact = REL_BUCKETS // 2
    qi = np.arange(ATTN_BLOCK)[:, None]
    kj = np.arange(2 * ATTN_BLOCK)[None, :]
    steps = qi + ATTN_BLOCK - kj
    buckets, masks = [], []
    for window, dilation in DILATED_CONFIGS:
        max_steps = window // dilation
        band = (steps >= 0) & (steps <= max_steps)
        dist = np.maximum(steps, 0) * dilation
        d_f = np.maximum(dist, 1).astype(np.float32)
        large = exact + (np.log(d_f / np.float32(exact)) / np.float32(math.log(REL_MAX_DIST / exact))
                         * np.float32(REL_BUCKETS - exact)).astype(np.int32)
        large = np.minimum(large, REL_BUCKETS - 1)
        bucket = np.where(dist < exact, dist, large).astype(np.int32)
        buckets.append(bucket.reshape(1, -1))
        masks.append(np.where(band, 0.0, NEG_INF).astype(np.float32).reshape(1, -1))
    return np.stack(buckets), np.stack(masks)


def _split_hi_lo(x):
    hi = x.astype(BF16)
    lo = (x - hi.astype(F32)).astype(BF16)
    return hi, lo


def _bias_build(rel_table_t, bucket, mask):
    H = rel_table_t.shape[0]
    n = bucket.shape[-1]

    def body(t_ref, bkt_ref, mask_ref, o_ref):
        onehot = (lax.broadcasted_iota(jnp.int32, (REL_BUCKETS, n), 0) == bkt_ref[0]).astype(BF16)
        t = t_ref[...]
        t1 = t.astype(BF16)
        r1 = t - t1.astype(F32)
        t2 = r1.astype(BF16)
        t3 = (r1 - t2.astype(F32)).astype(BF16)
        acc = jnp.dot(t1, onehot, preferred_element_type=F32)
        acc = acc + jnp.dot(t2, onehot, preferred_element_type=F32)
        acc = acc + jnp.dot(t3, onehot, preferred_element_type=F32)
        o_ref[0] = acc + mask_ref[0]

    return pl.pallas_call(
        body, grid=(3,),
        in_specs=[pl.BlockSpec((H, REL_BUCKETS), lambda b: (0, 0)),
                  pl.BlockSpec((1, 1, n), lambda b: (b, 0, 0)),
                  pl.BlockSpec((1, 1, n), lambda b: (b, 0, 0))],
        out_specs=pl.BlockSpec((1, H, n), lambda b: (b, 0, 0)),
        out_shape=jax.ShapeDtypeStruct((3, H, n), F32),
        compiler_params=_params(1), name="bias_build",
    )(rel_table_t, bucket, mask)


def _rel_grad(dbias, bucket):
    H = dbias.shape[1]
    n = bucket.shape[-1]
    dims = (((1,), (1,)), ((), ()))

    def body(d_ref, bkt_ref, o_ref):
        b = pl.program_id(0)
        onehot = (lax.broadcasted_iota(jnp.int32, (REL_BUCKETS, n), 0) == bkt_ref[0]).astype(BF16)
        d = d_ref[0]
        d1 = d.astype(BF16)
        r1 = d - d1.astype(F32)
        d2 = r1.astype(BF16)
        d3 = (r1 - d2.astype(F32)).astype(BF16)
        acc = lax.dot_general(d1, onehot, dims, preferred_element_type=F32)
        acc = acc + lax.dot_general(d2, onehot, dims, preferred_element_type=F32)
        acc = acc + lax.dot_general(d3, onehot, dims, preferred_element_type=F32)
        _accumulate(o_ref, b == 0, acc)

    return pl.pallas_call(
        body, grid=(3,),
        in_specs=[pl.BlockSpec((1, H, n), lambda b: (b, 0, 0)),
                  pl.BlockSpec((1, 1, n), lambda b: (b, 0, 0))],
        out_specs=pl.BlockSpec((H, REL_BUCKETS), lambda b: (0, 0)),
        out_shape=jax.ShapeDtypeStruct((H, REL_BUCKETS), F32),
        compiler_params=_params(1), name="rel_grad",
    )(dbias, bucket)


def _regroup(src, stage, dst, d, S, off=0):
    if d == 1:
        dst[off:off + S, :] = src.astype(dst.dtype)
        return
    stage[...] = src.astype(F32)
    L = S // d
    for r in range(d):
        dst[off + r * L:off + (r + 1) * L, :] = stage[pl.ds(r, L, stride=d), :].astype(dst.dtype)


def _ungroup(sub_ref, off, nat_ref, d, S, add):
    L = S // d
    for r in range(d):
        rows = pl.ds(0, S) if d == 1 else pl.ds(r, L, stride=d)
        val = sub_ref[off + r * L:off + (r + 1) * L, :]
        if add:
            nat_ref[rows, :] += val
        else:
            nat_ref[rows, :] = val


def _branch_keys(ks, vs, S, nb, g_idx):
    blk3 = (S // ATTN_BLOCK, ATTN_BLOCK, LANES)
    kc3 = ks[ATTN_BLOCK:ATTN_BLOCK + S, :].reshape(blk3)
    vc3 = vs[ATTN_BLOCK:ATTN_BLOCK + S, :].reshape(blk3)
    if nb == 1:
        return kc3, vc3, None
    kk3 = jnp.concatenate([ks[0:S, :].reshape(blk3), kc3], axis=1)
    vv3 = jnp.concatenate([vs[0:S, :].reshape(blk3), vc3], axis=1)
    col = lax.broadcasted_iota(jnp.int32, (1, 1, 2 * ATTN_BLOCK), 2)
    dead = jnp.logical_and((g_idx & (nb - 1)) == 0, col < ATTN_BLOCK)
    return kk3, vv3, dead


def _branch_scores(qe, kk3, b_ref, bi, e, dead):
    s = jnp.einsum("gqe,gke->gqk", qe, kk3, preferred_element_type=F32)
    if dead is None:
        return s + b_ref[bi, e, :, ATTN_BLOCK:]
    return jnp.where(dead, NEG_INF, s + b_ref[bi, e])


def _attention_fwd(qkv, bias_all, B, S, AW, bg=None):
    HP = AW // LANES
    G = S // ATTN_BLOCK
    blk3 = (G, ATTN_BLOCK, LANES)

    def body(refs, bg_hook):
        q_ref, k_ref, v_ref, b_ref, o_ref, lse_ref, stage, qs, ks, vs, ot, lt, on0, on1, on2, ln0, ln1, ln2 = refs
        bg_hook(False)
        head0 = lax.broadcasted_iota(jnp.int32, (1, 1, LANES), 2) < HEAD_DIM
        g_idx = lax.broadcasted_iota(jnp.int32, (G, 1, 1), 0)
        ks[0:ATTN_BLOCK, :] = jnp.zeros((ATTN_BLOCK, LANES), BF16)
        vs[0:ATTN_BLOCK, :] = jnp.zeros((ATTN_BLOCK, LANES), BF16)
        nat_o, nat_l = (on0, on1, on2), (ln0, ln1, ln2)
        for bi, (_, d) in enumerate(DILATED_CONFIGS):
            nb = S // d // ATTN_BLOCK
            _regroup(q_ref[0], stage, qs, d, S)
            _regroup(k_ref[0], stage, ks, d, S, ATTN_BLOCK)
            _regroup(v_ref[0], stage, vs, d, S, ATTN_BLOCK)
            q3 = qs[...].reshape(blk3) * QK_SCALE
            kk3, vv3, dead = _branch_keys(ks, vs, S, nb, g_idx)
            outs, lses = [], []
            for e in range(2):
                msk = head0 if e == 0 else jnp.logical_not(head0)
                qe = jnp.where(msk, q3, jnp.zeros_like(q3))
                s = _branch_scores(qe, kk3, b_ref, bi, e, dead)
                m = jnp.max(s, axis=-1, keepdims=True)
                p = jnp.exp(s - m)
                l = jnp.sum(p, axis=-1, keepdims=True)
                o = jnp.einsum("gqk,gke->gqe", p.astype(BF16), vv3, preferred_element_type=F32)
                outs.append(o / l)
                lses.append(jnp.broadcast_to(m + jnp.log(l), blk3))
            ot[...] = jnp.where(head0, outs[0], outs[1]).reshape(S, LANES)
            lt[...] = jnp.where(head0, lses[0], lses[1]).reshape(S, LANES)
            _ungroup(ot, 0, nat_o[bi], d, S, add=False)
            _ungroup(lt, 0, nat_l[bi], d, S, add=False)

        la, lb, lc = ln0[...], ln1[...], ln2[...]
        m = jnp.maximum(jnp.maximum(la, lb), lc)
        ea, eb, ec = jnp.exp(la - m), jnp.exp(lb - m), jnp.exp(lc - m)
        den = ea + eb + ec
        lse_ref[0] = m + jnp.log(den)
        o_ref[0] = (ea * on0[...] + eb * on1[...] + ec * on2[...]) / den
        bg_hook(True)

    blk = lambda off: pl.BlockSpec((1, S, LANES), lambda b, h: (b, 0, off + h))
    qv = qkv.reshape(B, S, 3 * AW)
    sub_f = pltpu.VMEM((S, LANES), F32)
    pad_b = pltpu.VMEM((S + ATTN_BLOCK, LANES), BF16)
    res = _hosted_call(
        body, bg, grid=(B, HP),
        in_specs=[blk(0), blk(HP), blk(2 * HP),
                  pl.BlockSpec((3, 2, ATTN_BLOCK, 2 * ATTN_BLOCK), lambda b, h: (0, h, 0, 0))],
        out_specs=[blk(0), blk(0)],
        out_shape=[jax.ShapeDtypeStruct((B, S, AW), F32)] * 2,
        scratch_shapes=[sub_f, pltpu.VMEM((S, LANES), BF16), pad_b, pad_b] + [sub_f] * 8,
        operands=[qv, qv, qv, bias_all], name="attention_fwd")
    return (res[0].reshape(B * S, AW), res[1].reshape(B * S, AW)) + tuple(res[2:])


def _attention_bwd(qkv, do, lse, dd, bias_all, B, S, AW, bg=None):
    HP = AW // LANES
    H = AW // HEAD_DIM
    G = S // ATTN_BLOCK
    blk3 = (G, ATTN_BLOCK, LANES)
    PAD = ATTN_BLOCK

    def body(refs, bg_hook):
        (q_ref, k_ref, v_ref, do_ref, lse_ref, dd_ref, b_ref,
         dq_ref, dk_ref, dv_ref, csq_ref, csk_ref, csv_ref, db_ref,
         stage, qs, ks, vs, gs, ls, ds_, tq, tk, tv, accq, acck, accv) = refs
        bg_hook(False)
        head0 = lax.broadcasted_iota(jnp.int32, (1, 1, LANES), 2) < HEAD_DIM
        g_idx = lax.broadcasted_iota(jnp.int32, (G, 1, 1), 0)
        first_b = pl.program_id(1) == 0

        @pl.when(first_b)
        def _():
            db_ref[...] = jnp.zeros_like(db_ref)

        ks[0:PAD, :] = jnp.zeros((PAD, LANES), BF16)
        vs[0:PAD, :] = jnp.zeros((PAD, LANES), BF16)
        tk[0:PAD, :] = jnp.zeros((PAD, LANES), F32)
        tv[0:PAD, :] = jnp.zeros((PAD, LANES), F32)
        for bi, (_, d) in enumerate(DILATED_CONFIGS):
            nb = S // d // ATTN_BLOCK
            _regroup(q_ref[0], stage, qs, d, S)
            _regroup(k_ref[0], stage, ks, d, S, PAD)
            _regroup(v_ref[0], stage, vs, d, S, PAD)
            _regroup(do_ref[0], stage, gs, d, S)
            _regroup(lse_ref[0], stage, ls, d, S)
            _regroup(dd_ref[0], stage, ds_, d, S)
            q3 = qs[...].reshape(blk3) * QK_SCALE
            do3 = gs[...].reshape(blk3)
            lse3 = ls[...].reshape(blk3)
            dd3 = ds_[...].reshape(blk3)
            kk3, vv3, dead = _branch_keys(ks, vs, S, nb, g_idx)
            dq = jnp.zeros(blk3, F32)
            dkk = jnp.zeros(kk3.shape, F32)
            dvv = jnp.zeros(kk3.shape, F32)
            for e in range(2):
                msk = head0 if e == 0 else jnp.logical_not(head0)
                c0 = e * HEAD_DIM
                qe = jnp.where(msk, q3, jnp.zeros_like(q3))
                doe = jnp.where(msk, do3, jnp.zeros_like(do3))
                ke = jnp.where(msk, kk3 * QK_SCALE, jnp.zeros_like(kk3))
                s = _branch_scores(qe, kk3, b_ref, bi, e, dead)
                p = jnp.exp(s - lse3[:, :, c0:c0 + 1])
                dp = jnp.einsum("gqe,gke->gqk", doe, vv3, preferred_element_type=F32)
                dsc = p * (dp - dd3[:, :, c0:c0 + 1])
                if dead is None:
                    db_ref[bi, e, :, ATTN_BLOCK:] += jnp.sum(dsc, axis=0)
                else:
                    db_ref[bi, e] += jnp.sum(dsc, axis=0)
                dsb = dsc.astype(BF16)
                dq = dq + jnp.einsum("gqk,gke->gqe", dsb, ke, preferred_element_type=F32)
                dkk = dkk + jnp.einsum("gqk,gqe->gke", dsb, qe, preferred_element_type=F32)
                dvv = dvv + jnp.einsum("gqk,gqe->gke", p.astype(BF16), doe, preferred_element_type=F32)
            tq[...] = dq.reshape(S, LANES)
            if dead is None:
                tk[PAD:PAD + S, :] = dkk.reshape(S, LANES)
                tv[PAD:PAD + S, :] = dvv.reshape(S, LANES)
            else:
                tk[PAD:PAD + S, :] = dkk[:, ATTN_BLOCK:, :].reshape(S, LANES)
                tv[PAD:PAD + S, :] = dvv[:, ATTN_BLOCK:, :].reshape(S, LANES)
                tk[0:S, :] += dkk[:, :ATTN_BLOCK, :].reshape(S, LANES)
                tv[0:S, :] += dvv[:, :ATTN_BLOCK, :].reshape(S, LANES)
            _ungroup(tq, 0, accq, d, S, add=bi > 0)
            _ungroup(tk, PAD, acck, d, S, add=bi > 0)
            _ungroup(tv, PAD, accv, d, S, add=bi > 0)

        for acc, out_ref, cs_ref in ((accq, dq_ref, csq_ref), (acck, dk_ref, csk_ref), (accv, dv_ref, csv_ref)):
            tot = acc[...]
            out_ref[0] = tot.astype(out_ref.dtype)
            _accumulate(cs_ref, first_b, jnp.sum(tot, axis=0, keepdims=True))
        bg_hook(True)

    blk = lambda off: pl.BlockSpec((1, S, LANES), lambda h, b: (b, 0, off + h))
    cs_spec = pl.BlockSpec((1, LANES), lambda h, b: (0, h))
    bias_spec = pl.BlockSpec((3, 2, ATTN_BLOCK, 2 * ATTN_BLOCK), lambda h, b: (0, h, 0, 0))
    qv = qkv.reshape(B, S, 3 * AW)
    view = lambda t: t.reshape(B, S, AW)
    sub_b = pltpu.VMEM((S, LANES), BF16)
    sub_f = pltpu.VMEM((S, LANES), F32)
    pad_b = pltpu.VMEM((S + PAD, LANES), BF16)
    pad_f = pltpu.VMEM((S + PAD, LANES), F32)
    res = _hosted_call(
        body, bg, grid=(HP, B),
        in_specs=[blk(0), blk(HP), blk(2 * HP), blk(0), blk(0), blk(0), bias_spec],
        out_specs=[blk(0), blk(0), blk(0), cs_spec, cs_spec, cs_spec, bias_spec],
        out_shape=[jax.ShapeDtypeStruct((B, S, AW), BF16)] * 3 + [jax.ShapeDtypeStruct((1, AW), F32)] * 3
        + [jax.ShapeDtypeStruct((3, H, ATTN_BLOCK, 2 * ATTN_BLOCK), F32)],
        scratch_shapes=[sub_f, sub_b, pad_b, pad_b, sub_b, sub_f, sub_f, sub_f, pad_f, pad_f, sub_f, sub_f, sub_f],
        operands=[qv, qv, qv, view(do), view(lse), view(dd), bias_all], name="attention_bwd")
    flat = lambda t: t.reshape(B * S, AW)
    return (flat(res[0]), flat(res[1]), flat(res[2]), res[3], res[4], res[5], res[6]) + tuple(res[7:])


class _RowShifts:
    def __init__(self, x, row, up):
        self.x, self.row, self.up, self.base = x, row, up, {0: x}

    def __call__(self, s):
        x = self.x
        n, c = x.shape
        r, whole = s % 8, s - s % 8
        if r not in self.base:
            if self.up:
                rolled = pltpu.roll(x, n - r, 0)
                tail = jnp.where(self.row[n - 8:] < n - r, rolled[n - 8:], 0.0)
                self.base[r] = jnp.concatenate([rolled[:n - 8], tail], axis=0)
            else:
                rolled = pltpu.roll(x, r, 0)
                head = jnp.where(self.row[:8] >= r, rolled[:8], 0.0)
                self.base[r] = jnp.concatenate([head, rolled[8:]], axis=0)
        y = self.base[r]
        if whole == 0:
            return y
        pad = jnp.zeros((whole, c), x.dtype)
        if self.up:
            return jnp.concatenate([y[whole:], pad], axis=0)
        return jnp.concatenate([pad, y[:n - whole]], axis=0)


def _conv_branch_fwd_math(a, g, w_ref, cb, lg, lb, row):
    sg = _sigmoid(g)
    u0 = a * sg
    u0_down = _RowShifts(u0, row, up=False)
    uc = jnp.zeros_like(u0) + cb
    for k in range(CONV_KERNEL):
        uc = uc + w_ref[k:k + 1, :] * u0_down(CONV_KERNEL - 1 - k)
    ul, xh, r = _ln_fwd(uc, lg, lb)
    su = _sigmoid(ul)
    u = ul * su
    return sg, u0_down, ul, xh, r, su, u


def _conv_fwd(ag, conv_w, conv_b, ln_g, ln_b, norm_g, B, S, CW):
    def body(a_ref, g_ref, w_ref, cb_ref, lg_ref, lb_ref, ng_ref, o_ref):
        row = lax.broadcasted_iota(jnp.int32, (S, CW), 0)
        _, _, _, _, _, _, u = _conv_branch_fwd_math(a_ref[0], g_ref[0], w_ref, cb_ref[...], lg_ref[...],
                                                    lb_ref[...], row)
        rr = lax.rsqrt(jnp.mean(u * u, axis=-1, keepdims=True) + LN_EPS)
        o_ref[0] = (u * rr * ng_ref[...]).astype(BF16)

    vec = pl.BlockSpec((1, CW), lambda b: (0, 0))
    out = pl.pallas_call(
        body, grid=(B,),
        in_specs=[pl.BlockSpec((1, S, CW), lambda b: (b, 0, 0)), pl.BlockSpec((1, S, CW), lambda b: (b, 0, 1)),
                  pl.BlockSpec((CONV_KERNEL, CW), lambda b: (0, 0)), vec, vec, vec, vec],
        out_specs=pl.BlockSpec((1, S, CW), lambda b: (b, 0, 0)),
        out_shape=jax.ShapeDtypeStruct((B, S, CW), BF16),
        compiler_params=_params(1), name="conv_fwd",
    )(ag.reshape(B, S, 2 * CW), ag.reshape(B, S, 2 * CW), conv_w, conv_b, ln_g, ln_b, norm_g)
    return out.reshape(B * S, CW)


def _conv_bwd(ag, dmc, conv_w, conv_b, ln_g, ln_b, norm_g, B, S, CW):
    def body(a_ref, g_ref, dm_ref, w_ref, cb_ref, lg_ref, lb_ref, ng_ref,
             dag_ref, dw_ref, dcb_ref, dlg_ref, dlb_ref, dng_ref):
        b = pl.program_id(0)
        row = lax.broadcasted_iota(jnp.int32, (S, CW), 0)
        a, g = a_ref[0], g_ref[0]
        sg, u0_down, ul, xh, r, su, u = _conv_branch_fwd_math(a, g, w_ref, cb_ref[...], lg_ref[...], lb_ref[...], row)
        rr = lax.rsqrt(jnp.mean(u * u, axis=-1, keepdims=True) + LN_EPS)
        dm = dm_ref[0]
        dxn = dm * ng_ref[...]
        du = rr * (dxn - u * (rr * rr) * jnp.mean(dxn * u, axis=-1, keepdims=True))
        dul = du * su * (1.0 + ul * (1.0 - su))
        duc = _ln_bwd(dul, xh, r, lg_ref[...])
        first = b == 0
        _accumulate(dng_ref, first, jnp.sum(dm * u * rr, axis=0, keepdims=True))
        _accumulate(dlg_ref, first, jnp.sum(dul * xh, axis=0, keepdims=True))
        _accumulate(dlb_ref, first, jnp.sum(dul, axis=0, keepdims=True))
        _accumulate(dcb_ref, first, jnp.sum(duc, axis=0, keepdims=True))

        @pl.when(first)
        def _():
            dw_ref[...] = jnp.zeros_like(dw_ref)

        duc_up = _RowShifts(duc, row, up=True)
        du0 = jnp.zeros_like(duc)
        for k in range(CONV_KERNEL):
            sh = CONV_KERNEL - 1 - k
            dw_ref[k:k + 1, :] += jnp.sum(duc * u0_down(sh), axis=0, keepdims=True)
            du0 = du0 + w_ref[k:k + 1, :] * duc_up(sh)
        dag_ref[0, :, :CW] = du0 * sg
        dag_ref[0, :, CW:] = du0 * a * sg * (1.0 - sg)

    vec = pl.BlockSpec((1, CW), lambda b: (0, 0))
    wspec = pl.BlockSpec((CONV_KERNEL, CW), lambda b: (0, 0))
    agv = ag.reshape(B, S, 2 * CW)
    res = pl.pallas_call(
        body, grid=(B,),
        in_specs=[pl.BlockSpec((1, S, CW), lambda b: (b, 0, 0)), pl.BlockSpec((1, S, CW), lambda b: (b, 0, 1)),
                  pl.BlockSpec((1, S, CW), lambda b: (b, 0, 0)), wspec, vec, vec, vec, vec],
        out_specs=[pl.BlockSpec((1, S, 2 * CW), lambda b: (b, 0, 0)), wspec, vec, vec, vec, vec],
        out_shape=[jax.ShapeDtypeStruct((B, S, 2 * CW), F32), jax.ShapeDtypeStruct((CONV_KERNEL, CW), F32)]
        + [jax.ShapeDtypeStruct((1, CW), F32)] * 4,
        compiler_params=_params(1), name="conv_bwd",
    )(agv, agv, dmc.reshape(B, S, CW), conv_w, conv_b, ln_g, ln_b, norm_g)
    return (res[0].reshape(B * S, 2 * CW),) + tuple(res[1:])


def _ffn_conv(x, w_ref, bias, row):
    down = x if isinstance(x, _RowShifts) else _RowShifts(x, row, up=False)
    y = jnp.zeros_like(down.x) + bias
    for k in range(FFN_CONV_KERNEL):
        y = y + w_ref[k:k + 1, :] * down(FFN_CONV_KERNEL - 1 - k)
    return y


def _ffn_specs(S, tc, nj, order):
    pick = (lambda b, j: (b, j)) if order == "bj" else (lambda j, b: (b, j))
    act = lambda off: pl.BlockSpec((1, S, tc), lambda *g: (pick(*g)[0], 0, off + pick(*g)[1]))
    cw = lambda off: pl.BlockSpec((FFN_CONV_KERNEL, tc), lambda *g: (0, off + pick(*g)[1]))
    cb = lambda off: pl.BlockSpec((1, tc), lambda *g: (0, off + pick(*g)[1]))
    return act, cw, cb


FFN_HALO = 16


def _half_sequences(S):
    if S < 8 * FFN_HALO:
        return [(0, S, 0, S)]
    h = S // 2
    return [(0, h + FFN_HALO, 0, h), (h - FFN_HALO, S, FFN_HALO, h)]


def _w_up_block_spec(w_up_sh, tc, off):
    _, D, cs = w_up_sh.shape
    assert cs % tc == 0
    bps = cs // tc
    return pl.BlockSpec((1, D, tc), lambda j: ((off + j) // bps, 0, (off + j) % bps))


def _ffn_fwd_fused(x1b, w_up_sh, cw, cb, B, S, DFF):
    tc = FFN_COLS
    nj = DFF // tc
    D = x1b.shape[1]

    def body(x_ref, wg_ref, wv_ref, cwg_ref, cwv_ref, cbg_ref, cbv_ref, o_ref, up_ref):
        w = jnp.concatenate([wg_ref[0], wv_ref[0]], axis=1)
        for b in range(B):
            for lo, hi, o0, on in _half_sequences(S):
                row = lax.broadcasted_iota(jnp.int32, (hi - lo, tc), 0)
                up = jnp.dot(x_ref[b, lo:hi, :], w, preferred_element_type=F32)
                up_ref[b, lo + o0:lo + o0 + on, :] = up[o0:o0 + on]
                gate = _ffn_conv(up[:, :tc], cwg_ref, cbg_ref[...], row)
                val = _ffn_conv(up[:, tc:], cwv_ref, cbv_ref[...], row)
                o_ref[b, lo + o0:lo + o0 + on, :] = (gate * _sigmoid(gate) * val).astype(BF16)[o0:o0 + on]

    cws = lambda off: pl.BlockSpec((FFN_CONV_KERNEL, tc), lambda j: (0, off + j))
    cbs = lambda off: pl.BlockSpec((1, tc), lambda j: (0, off + j))
    act, upre = pl.pallas_call(
        body, grid=(nj,),
        in_specs=[pl.BlockSpec((B, S, D), lambda j: (0, 0, 0), pipeline_mode=pl.Buffered(1)),
                  _w_up_block_spec(w_up_sh, tc, 0), _w_up_block_spec(w_up_sh, tc, nj),
                  cws(0), cws(nj), cbs(0), cbs(nj)],
        out_specs=[pl.BlockSpec((B, S, tc), lambda j: (0, 0, j)), pl.BlockSpec((B, S, 2 * tc), lambda j: (0, 0, j))],
        out_shape=[jax.ShapeDtypeStruct((B, S, DFF), BF16), jax.ShapeDtypeStruct((B, S, 2 * DFF), F32)],
        compiler_params=_params(1), name="ffn_fwd",
    )(x1b.reshape(B, S, D), w_up_sh, w_up_sh, cw, cw, cb, cb)
    return act.reshape(B * S, DFF), upre


def _ffn_bwd_fused(x1b, dz2b, upre, w_down, cw, cb, B, S, DFF):
    tc = FFN_COLS
    nj = DFF // tc
    D = x1b.shape[1]

    def body(x_ref, dz_ref, up_ref, wd_ref, cwg_ref, cwv_ref, cbg_ref, cbv_ref,
             dug_ref, duv_ref, dwu_ref, dwd_ref, dcw_ref, dcb_ref):
        first = pl.program_id(1) == 0
        dw_t = dwd = None
        dcb = [None, None]
        dcw = [[None] * FFN_CONV_KERNEL, [None] * FFN_CONV_KERNEL]
        add = lambda old, new: new if old is None else old + new
        for lo, hi, o0, on in _half_sequences(S):
            n = hi - lo
            own = slice(o0, o0 + on)
            row = lax.broadcasted_iota(jnp.int32, (n, tc), 0)
            x = x_ref[0, lo:hi, :]
            dz = dz_ref[0, lo:hi, :]
            ug = _RowShifts(up_ref[0, lo:hi, :tc], row, up=False)
            uv = _RowShifts(up_ref[0, lo:hi, tc:], row, up=False)
            gate = _ffn_conv(ug, cwg_ref, cbg_ref[...], row)
            val = _ffn_conv(uv, cwv_ref, cbv_ref[...], row)
            sg = _sigmoid(gate)
            act = (gate * sg * val).astype(BF16)
            dact = _dot(dz, wd_ref[...], "nt")
            dgate = dact * val * sg * (1.0 + gate * (1.0 - sg))
            dval = dact * gate * sg
            dupre = []
            for h, (dup, u_down, w_ref) in enumerate(((dgate, ug, cwg_ref), (dval, uv, cwv_ref))):
                dcb[h] = add(dcb[h], jnp.sum(dup[own], axis=0, keepdims=True))
                dup_up = _RowShifts(dup, row, up=True)
                acc = jnp.zeros_like(dup)
                for k in range(FFN_CONV_KERNEL):
                    sh = FFN_CONV_KERNEL - 1 - k
                    dcw[h][k] = add(dcw[h][k], jnp.sum((dup * u_down(sh))[own], axis=0, keepdims=True))
                    acc = acc + w_ref[k:k + 1, :] * dup_up(sh)
                dupre.append(acc.astype(BF16)[own])
            dug_ref[0, lo + o0:lo + o0 + on, :] = dupre[0]
            duv_ref[0, lo + o0:lo + o0 + on, :] = dupre[1]
            dw_t = add(dw_t, _dot(jnp.concatenate(dupre, axis=1), x[own], "tn"))
            dwd = add(dwd, _dot(act[own], dz[own], "tn"))
        _accumulate(dwu_ref.at[0], first, dw_t[:tc])
        _accumulate(dwu_ref.at[1], first, dw_t[tc:])
        _accumulate(dwd_ref, first, dwd)
        for h in range(2):
            _accumulate(dcb_ref.at[h], first, dcb[h])
            for k in range(FFN_CONV_KERNEL):
                _accumulate(dcw_ref.at[k, pl.ds(h, 1), :], first, dcw[h][k])

    act_s, cws, cbs = _ffn_specs(S, tc, nj, "jb")
    seq = pl.BlockSpec((1, S, D), lambda j, b: (b, 0, 0))
    res = pl.pallas_call(
        body, grid=(nj, B),
        in_specs=[seq, seq, pl.BlockSpec((1, S, 2 * tc), lambda j, b: (b, 0, j)),
                  pl.BlockSpec((tc, D), lambda j, b: (j, 0)), cws(0), cws(nj), cbs(0), cbs(nj)],
        out_specs=[act_s(0), act_s(0), pl.BlockSpec((2, tc, D), lambda j, b: (0, j, 0)),
                   pl.BlockSpec((tc, D), lambda j, b: (j, 0)),
                   pl.BlockSpec((FFN_CONV_KERNEL, 2, tc), lambda j, b: (0, 0, j)),
                   pl.BlockSpec((2, 1, tc), lambda j, b: (0, 0, j))],
        out_shape=[jax.ShapeDtypeStruct((B, S, DFF), BF16)] * 2
        + [jax.ShapeDtypeStruct((2, DFF, D), F32), jax.ShapeDtypeStruct((DFF, D), F32),
           jax.ShapeDtypeStruct((FFN_CONV_KERNEL, 2, DFF), F32), jax.ShapeDtypeStruct((2, 1, DFF), F32)],
        compiler_params=_params(2), name="ffn_bwd",
    )(x1b.reshape(B, S, D), dz2b.reshape(B, S, D), upre, w_down, cw, cw, cb, cb)
    flat = lambda t: t.reshape(B * S, DFF)
    return flat(res[0]), flat(res[1]), res[2], res[3], res[4], res[5]


def _dx1_ln1_bwd(dupre_g, dupre_v, w_up_sh, dz2, xh1, r1, ln1_g, tm, bg):
    T, D = dz2.shape
    NS, _, cs = w_up_sh.shape
    half = NS // 2
    DFF = dupre_g.shape[1]

    def body(refs, bg_hook):
        dug_ref, duv_ref, w_ref, dz2_ref, xh_ref, r_ref, g_ref, dz_ref, dzb_ref, dg_ref, db_ref = refs
        bg_hook(False)
        first = pl.program_id(0) == 0
        dg = db = None
        for rows in (slice(0, tm // 2), slice(tm // 2, tm)):
            dx1 = ALPHA * dz2_ref[rows, :]
            for k in range(NS):
                src = dug_ref if k < half else duv_ref
                c0 = (k % half) * cs
                dx1 = dx1 + _dot(src[rows, c0:c0 + cs], w_ref[k], "nt")
            xh = xh_ref[rows, :]
            dz = _ln_bwd(dx1, xh, r_ref[rows, 0:1], g_ref[...])
            dz_ref[rows, :] = dz
            dzb_ref[rows, :] = dz.astype(BF16)
            dg_h, db_h = jnp.sum(dx1 * xh, axis=0, keepdims=True), jnp.sum(dx1, axis=0, keepdims=True)
            dg, db = (dg_h, db_h) if dg is None else (dg + dg_h, db + db_h)
        _accumulate(dg_ref, first, dg)
        _accumulate(db_ref, first, db)
        bg_hook(True)

    row = pl.BlockSpec((tm, D), lambda i: (i, 0))
    vec = pl.BlockSpec((1, D), lambda i: (0, 0))
    du = pl.BlockSpec((tm, DFF), lambda i: (i, 0))
    return _hosted_call(
        body, bg, grid=(T // tm,),
        in_specs=[du, du, pl.BlockSpec((NS, D, cs), lambda i: (0, 0, 0), pipeline_mode=pl.Buffered(1)),
                  row, row, pl.BlockSpec((tm, LANES), lambda i: (i, 0)), vec],
        out_specs=[row, row, vec, vec],
        out_shape=[jax.ShapeDtypeStruct((T, D), F32), jax.ShapeDtypeStruct((T, D), BF16),
                   jax.ShapeDtypeStruct((1, D), F32), jax.ShapeDtypeStruct((1, D), F32)],
        scratch_shapes=[], operands=[dupre_g, dupre_v, w_up_sh, dz2, xh1, r1, ln1_g], name="mm_dx1_ln1_bwd")


def _dh_cat(dq, dk, dv, dag, tm):
    T, AW = dq.shape
    CW2 = dag.shape[1]
    W = 3 * AW + CW2

    def body(dq_ref, dk_ref, dv_ref, dag_ref, dh_ref, cs_ref):
        for c, ref in enumerate((dq_ref, dk_ref, dv_ref)):
            dh_ref[:, c * AW:(c + 1) * AW] = ref[...]
        dg = dag_ref[...]
        dh_ref[:, 3 * AW:] = dg.astype(BF16)
        _accumulate(cs_ref, pl.program_id(0) == 0, jnp.sum(dg, axis=0, keepdims=True))

    row = pl.BlockSpec((tm, AW), lambda i: (i, 0))
    return pl.pallas_call(
        body, grid=(T // tm,),
        in_specs=[row] * 3 + [pl.BlockSpec((tm, CW2), lambda i: (i, 0))],
        out_specs=[pl.BlockSpec((tm, W), lambda i: (i, 0)), pl.BlockSpec((1, CW2), lambda i: (0, 0))],
        out_shape=[jax.ShapeDtypeStruct((T, W), BF16), jax.ShapeDtypeStruct((1, CW2), F32)],
        compiler_params=_params(1), name="dh_cat",
    )(dq, dk, dv, dag)


def _local_step(x, target, rel_table, w_in_t, b_in, conv_w, conv_b, conv_ln_g, conv_ln_b, attn_norm_g,
                conv_norm_g, staged, ln1_g, ln1_b, ffn_cw, ffn_cb, ln2_g, ln2_b, ids):
    B, S, D = x.shape
    T = B * S
    AW = attn_norm_g.shape[-1]
    CW = conv_norm_g.shape[-1]
    H = AW // HEAD_DIM
    DFF = staged[2].shape[0] * staged[2].shape[1]
    INW = 3 * AW + 2 * CW
    xf = x.reshape(T, D)
    tf = target.reshape(T, D)
    tm = _row_tile(T, 512)
    tm_s = tm

    bucket_np, mask_np = _bucket_tables()
    bucket = jnp.asarray(bucket_np)
    band_mask = jnp.asarray(mask_np)
    bias_all = _bias_build(rel_table.T, bucket, band_mask).reshape(3, H, ATTN_BLOCK, 2 * ATTN_BLOCK)

    def in_proj(n0, n, tn, rows, out_dtype, name):
        assert n0 % tn == 0 and n % tn == 0

        def epilogue(acc, i, j, extra_refs, out_refs):
            out_refs[0][...] = (acc + extra_refs[0][...]).astype(out_dtype)

        return _matmul_general(
            [(xf, (rows, D), lambda i, j, k: (i, 0)), (w_in_t, (tn, D), lambda i, j, k: (n0 // tn + j, 0))],
            lambda refs, i, j, k: _dot(refs[0][...], refs[1][...], "nt"),
            grid=(T // rows, n // tn, 1), tm=rows, tn=tn,
            extras=[(b_in, (1, tn), lambda i, j, k: (0, n0 // tn + j))],
            outs=[_plain_out(T, n, rows, tn, out_dtype)], epilogue=epilogue, name=name)[0]

    qkv = in_proj(0, 3 * AW, _col_tile(3 * AW, 1152), tm, BF16, "mm_qkv")
    ag = in_proj(3 * AW, 2 * CW, math.gcd(3 * AW, 2 * CW), _row_tile(T, 1024), F32, "mm_ag")

    attn, lse, w_out_g, w_up_sh, w_down_g = _attention_fwd(qkv, bias_all, B, S, AW, bg=_bg_gather(staged))
    w_out = w_out_g.reshape(D, D)
    w_down = w_down_g.reshape(DFF, D)
    mixed_c = _conv_fwd(ag, conv_w, conv_b, conv_ln_g, conv_ln_b, conv_norm_g, B, S, CW)

    def attn_rstd(a):
        return lax.rsqrt(jnp.mean(a * a, axis=-1, keepdims=True) + LN_EPS)

    def mixed_rows(attn_ref, mc_ref, gain_ref):
        a = attn_ref[...]
        return jnp.concatenate([(a * attn_rstd(a) * gain_ref[...]).astype(BF16), mc_ref[...]], axis=1)

    def ln1_epilogue(acc, i, j, extra_refs, out_refs):
        x_ref, g_ref, b_ref, a_ref = extra_refs
        x1, xh, r = _ln_fwd(acc + ALPHA * x_ref[...], g_ref[...], b_ref[...])
        out_refs[0][...] = x1
        out_refs[1][...] = x1.astype(BF16)
        out_refs[2][...] = xh
        out_refs[3][...] = jnp.broadcast_to(r, (tm_s, LANES))
        out_refs[4][...] = jnp.broadcast_to(attn_rstd(a_ref[...]), (tm_s, LANES))

    rowD = lambda i, j, k: (i, 0)
    vecD = lambda i, j, k: (0, 0)
    x1, x1b, xh1, r1, r_attn = _matmul_general(
        [(attn, (tm_s, AW), rowD), (mixed_c, (tm_s, CW), rowD), (attn_norm_g, (1, AW), vecD), (w_out, (D, D), vecD)],
        lambda refs, i, j, k: _dot(mixed_rows(refs[0], refs[1], refs[2]), refs[3][...], "nn"),
        grid=(T // tm_s, 1, 1), tm=tm_s, tn=D,
        extras=[(xf, (tm_s, D), rowD), (ln1_g, (1, D), vecD), (ln1_b, (1, D), vecD), (attn, (tm_s, AW), rowD)],
        outs=[((T, D), F32, (tm_s, D), rowD), ((T, D), BF16, (tm_s, D), rowD), ((T, D), F32, (tm_s, D), rowD),
              ((T, LANES), F32, (tm_s, LANES), rowD), ((T, LANES), F32, (tm_s, LANES), rowD)],
        epilogue=ln1_epilogue, name="mm_out_ln1")

    NS, _, cs = w_up_sh.shape
    half = NS // 2

    act, upre = _ffn_fwd_fused(x1b, w_up_sh, ffn_cw, ffn_cb, B, S, DFF)

    halves = [slice(0, tm // 2), slice(tm // 2, tm)]

    def ln2_epilogue(parts, i, j, extra_refs, out_refs):
        x1_ref, g_ref, b_ref, t_ref = extra_refs
        dz_ref, dzb_ref, loss_ref, dg_ref, db_ref = out_refs
        g = g_ref[...]
        sums = None
        for rows, acc in zip(halves, parts):
            y, xh, r = _ln_fwd(acc + ALPHA * x1_ref[rows, :], g, b_ref[...])
            diff = y - t_ref[rows, :]
            row_loss = jnp.sum(diff * diff, axis=1, keepdims=True)
            tile_loss = jnp.sum(row_loss, axis=0, keepdims=True) * (0.5 / D)
            dy = diff * (1.0 / D)
            dz = _ln_bwd(dy, xh, r, g)
            dz_ref[rows, :] = dz
            dzb_ref[rows, :] = dz.astype(BF16)
            vals = (jnp.broadcast_to(tile_loss, (1, LANES)), jnp.sum(dy * xh, axis=0, keepdims=True),
                    jnp.sum(dy, axis=0, keepdims=True))
            sums = vals if sums is None else tuple(a + b for a, b in zip(sums, vals))
        for ref, val in zip((loss_ref, dg_ref, db_ref), sums):
            _accumulate(ref, i == 0, val)

    dz2, dz2b, loss_part, d_ln2_g, d_ln2_b = _matmul_general(
        [(act, (tm, DFF), rowD), (w_down, (DFF, D), vecD)],
        lambda refs, i, j, k: tuple(_dot(refs[0][rows, :], refs[1][...], "nn") for rows in halves),
        grid=(T // tm, 1, 1), tm=tm, tn=D,
        extras=[(x1, (tm, D), rowD), (ln2_g, (1, D), vecD), (ln2_b, (1, D), vecD), (tf, (tm, D), rowD)],
        outs=[((T, D), F32, (tm, D), rowD), ((T, D), BF16, (tm, D), rowD),
              ((1, LANES), F32, (1, LANES), vecD), ((1, D), F32, (1, D), vecD), ((1, D), F32, (1, D), vecD)],
        epilogue=ln2_epilogue, name="mm_down_ln2_loss")

    dupre_g, dupre_v, d_w_up_t, d_w_down, d_ffn_cw2, d_ffn_cb2 = _ffn_bwd_fused(
        x1b, dz2b, upre, w_down, ffn_cw, ffn_cb, B, S, DFF)
    d_w_up_t = d_w_up_t.reshape(NS, cs, D)
    d_ffn_cw = d_ffn_cw2.reshape(FFN_CONV_KERNEL, 2 * DFF)
    d_ffn_cb = d_ffn_cb2.reshape(1, 2 * DFF)
    tk_t = _row_tile(T, 512)

    early = [d_w_up_t, d_w_down.reshape(NS, DFF // NS, D)]
    dz1, dz1b, d_ln1_g, d_ln1_b, *sib_e = _dx1_ln1_bwd(dupre_g, dupre_v, w_up_sh, dz2, xh1, r1, ln1_g, tm,
                                                       bg=_bg_sibling_exchange(early))
    chip_e = [_pair_sum(g, s, ids, name="pair_sum_" + n) for g, s, n in zip(early, sib_e, ("w_up", "w_down"))]

    def dw_out_epilogue(acc, i, j, extra_refs, out_refs):
        out_refs[0][...] = acc

    d_w_out = _matmul_general(
        [(attn, (tk_t, AW), lambda i, j, k: (k, 0)), (mixed_c, (tk_t, CW), lambda i, j, k: (k, 0)),
         (attn_norm_g, (1, AW), vecD), (dz1b, (tk_t, D), lambda i, j, k: (k, 0))],
        lambda refs, i, j, k: _dot(mixed_rows(refs[0], refs[1], refs[2]), refs[3][...], "tn"),
        grid=(1, 1, T // tk_t), tm=D, tn=D, outs=[_plain_out(D, D, D, D, F32)],
        epilogue=dw_out_epilogue, name="mm_dw_out")[0]
    early.append(d_w_out.reshape(NS, D // NS, D))
    def dmixed_epilogue(acc, i, j, extra_refs, out_refs):
        a_ref, r_ref, g_ref = extra_refs
        do_ref, dd_ref, dmc_ref, dg_ref = out_refs
        head_of = lambda axis: lax.broadcasted_iota(jnp.int32, (AW, AW), axis) // HEAD_DIM
        same_head = (head_of(0) == head_of(1)).astype(BF16)
        dm = acc[:, :AW]
        dmc_ref[...] = acc[:, AW:]
        a = a_ref[...]
        r = r_ref[:, 0:1]
        dxn = dm * g_ref[...]
        da = r * (dxn - a * (r * r) * jnp.mean(dxn * a, axis=-1, keepdims=True))
        do_ref[...] = da.astype(BF16)
        hi, lo = _split_hi_lo(da * a)
        dd_ref[...] = (jnp.dot(hi, same_head, preferred_element_type=F32)
                       + jnp.dot(lo, same_head, preferred_element_type=F32))
        _accumulate(dg_ref, i == 0, jnp.sum(dm * a * r, axis=0, keepdims=True))

    dattn, dd, dmc, d_attn_norm_g, sib_out = _matmul(
        dz1b, w_out, mode="nt", tm=tm, tn=D, tk=D,
        extras=[(attn, (tm, AW), rowD), (r_attn, (tm, LANES), rowD), (attn_norm_g, (1, AW), vecD)],
        outs=[((T, AW), BF16, (tm, AW), rowD), ((T, AW), F32, (tm, AW), rowD), ((T, CW), F32, (tm, CW), rowD),
              ((1, AW), F32, (1, AW), vecD)],
        epilogue=dmixed_epilogue, name="mm_dmixed", bg=_bg_sibling_exchange(early[2:]))
    sib_e.append(sib_out)
    chip_e.append(_pair_sum(early[2], sib_out, ids, name="pair_sum_w_out"))

    dag, d_conv_w, d_conv_b, d_conv_ln_g, d_conv_ln_b, d_conv_norm_g = _conv_bwd(
        ag, dmc, conv_w, conv_b, conv_ln_g, conv_ln_b, conv_norm_g, B, S, CW)

    dq, dk, dv, csq, csk, csv, dbias, *got_e = _attention_bwd(qkv, dattn, lse, dd, bias_all, B, S, AW,
                                                              bg=_bg_chip_exchange(chip_e))
    full_up, full_down, full_out = [_final_sum(g, s, r, ids, name="final_sum_" + n)
                                    for g, s, r, n in zip(early, sib_e, got_e, ("w_up", "w_down", "w_out"))]
    d_rel_table = _rel_grad(dbias.reshape(3, H, ATTN_BLOCK * 2 * ATTN_BLOCK), bucket).T
    dh, cs_ag = _dh_cat(dq, dk, dv, dag, tm_s)
    d_b_in = jnp.concatenate([csq, csk, csv, cs_ag], axis=1)

    d_w_in_t = _mm_plain(dh, xf, mode="tn", tm=_col_tile(INW, 1408), tn=D, tk=tk_t, out_dtype=F32, name="mm_dw_in")
    late = [d_w_in_t.reshape(NS, INW // NS, D)]
    sib_l = _sibling_exchange(late)
    chip_l = [_pair_sum(late[0], sib_l[0], ids, name="pair_sum_w_in")]
    small = dict(rel_table=d_rel_table, b_in=d_b_in, conv_w=d_conv_w, conv_b=d_conv_b, conv_ln_g=d_conv_ln_g,
                 conv_ln_b=d_conv_ln_b, attn_norm_g=d_attn_norm_g, conv_norm_g=d_conv_norm_g, ln1_g=d_ln1_g,
                 ln1_b=d_ln1_b, ffn_conv_w=d_ffn_cw, ffn_conv_b=d_ffn_cb, ln2_g=d_ln2_g, ln2_b=d_ln2_b)
    pack = _pack([loss_part] + [small[n] for n in SMALL_NAMES])

    def gx_epilogue(acc, i, j, extra_refs, out_refs):
        out_refs[0][...] = acc + ALPHA * extra_refs[0][...]

    grad_x, got_in, all_packs = _matmul(
        dh, w_in_t, mode="nn", tm=tm, tn=D, tk=INW, extras=[(dz1, (tm, D), rowD)],
        outs=[((T, D), F32, (tm, D), rowD)], epilogue=gx_epilogue, name="mm_grad_x",
        bg=_bg_chip_exchange(chip_l, pack))
    full_in = _final_sum(late[0], sib_l[0], got_in, ids, name="final_sum_w_in")
    return grad_x.reshape(B, S, D), [full_in, full_out, full_up, full_down], all_packs


def _place():
    return lax.axis_index("x"), lax.axis_index("y"), lax.axis_index("c")


CHIP_FLIPS = ((1, 0), (0, 1), (1, 1))


def _flip(v, f):
    return 1 - v if f else v


HBM_SPEC = pl.BlockSpec(memory_space=pl.ANY)
VMEM_SPEC = pl.BlockSpec(memory_space=pltpu.VMEM)
COMM_PARAMS = pltpu.CompilerParams(vmem_limit_bytes=VMEM_LIMIT)


def _gather_weights(big, small):
    nb, ns = len(big), len(small)

    def body(*refs):
        big_in = refs[:nb]
        small_in = refs[nb:nb + ns]
        big_out = refs[nb + ns:2 * nb + ns]
        small_out = refs[2 * nb + ns:2 * nb + 2 * ns]
        stages = refs[2 * nb + 2 * ns:3 * nb + 2 * ns]
        send_sems, recv_sems, local_sems = refs[3 * nb + 2 * ns:]
        x, y, c = _place()
        s_me = 2 * x + y
        sibling = (x, y, 1 - c)
        started, local_copies = [], []
        for a in range(nb):
            rh = big[a].shape[0] // 2
            lo = pl.multiple_of(c * rh, 16)
            stages[a][...] = big_in[a][pl.ds(lo, rh), :].astype(BF16)
            mine = big_out[a].at[s_me, pl.ds(lo, rh), :]
            loc = pltpu.make_async_copy(stages[a], mine, local_sems.at[a])
            loc.start()
            local_copies.append(loc)
            targets = [sibling] + [(_flip(x, fx), _flip(y, fy), c) for fx, fy in CHIP_FLIPS]
            for k, to in enumerate(targets):
                cp = pltpu.make_async_remote_copy(stages[a], mine, send_sems.at[a * 7 + k],
                                                  recv_sems.at[a * 7 + k], device_id=to, device_id_type=MESH)
                cp.start()
                started.append(cp)
        for a in range(ns):
            mine = small_out[a].at[s_me]
            loc = pltpu.make_async_copy(small_in[a], mine, local_sems.at[nb + a])
            loc.start()
            local_copies.append(loc)
            for k, (fx, fy) in enumerate(CHIP_FLIPS):
                cp = pltpu.make_async_remote_copy(small_in[a], mine, send_sems.at[nb * 7 + a * 3 + k],
                                                  recv_sems.at[nb * 7 + a * 3 + k],
                                                  device_id=(_flip(x, fx), _flip(y, fy), c), device_id_type=MESH)
                cp.start()
                started.append(cp)
        for a in range(nb):
            rh = big[a].shape[0] // 2
            lo = pl.multiple_of(c * rh, 16)
            for k, (fx, fy) in enumerate(CHIP_FLIPS):
                s_from = 2 * _flip(x, fx) + _flip(y, fy)
                got = big_out[a].at[s_from, pl.ds(lo, rh), :]
                pltpu.make_async_remote_copy(got, got, send_sems.at[a * 7 + 1 + k], recv_sems.at[a * 7 + 1 + k],
                                             device_id=sibling, device_id_type=MESH).wait_recv()
                fwd = pltpu.make_async_remote_copy(got, got, send_sems.at[a * 7 + 4 + k],
                                                   recv_sems.at[a * 7 + 4 + k], device_id=sibling,
                                                   device_id_type=MESH)
                fwd.start()
                started.append(fwd)
        for a in range(nb):
            rh = big[a].shape[0] // 2
            lo_sib = pl.multiple_of((1 - c) * rh, 16)
            for k in (0, 4, 5, 6):
                any_rows = big_out[a].at[s_me, pl.ds(lo_sib, rh), :]
                pltpu.make_async_remote_copy(any_rows, any_rows, send_sems.at[a * 7 + k], recv_sems.at[a * 7 + k],
                                             device_id=sibling, device_id_type=MESH).wait_recv()
        for a in range(ns):
            for k in range(3):
                pltpu.make_async_remote_copy(small_in[a], small_out[a].at[s_me], send_sems.at[nb * 7 + a * 3 + k],
                                             recv_sems.at[nb * 7 + a * 3 + k], device_id=sibling,
                                             device_id_type=MESH).wait_recv()
        for cp in started:
            cp.wait_send()
        for cp in local_copies:
            cp.wait()

    n_sem = nb * 7 + ns * 3
    out_shape = ([jax.ShapeDtypeStruct((N_SHARDS,) + w.shape, BF16) for w in big]
                 + [jax.ShapeDtypeStruct((N_SHARDS,) + w.shape, F32) for w in small])
    res = pl.pallas_call(
        body, in_specs=[VMEM_SPEC] * nb + [HBM_SPEC] * ns, out_specs=[HBM_SPEC] * (nb + ns),
        out_shape=out_shape,
        scratch_shapes=[pltpu.VMEM((w.shape[0] // 2, w.shape[1]), BF16) for w in big]
        + [pltpu.SemaphoreType.DMA((n_sem,)), pltpu.SemaphoreType.DMA((n_sem,)),
           pltpu.SemaphoreType.DMA((nb + ns,))],
        compiler_params=COMM_PARAMS, name="gather_weights",
    )(*big, *small)
    return res[:nb], res[nb:]


def _sibling_exchange(grads):
    n = len(grads)

    def body(*refs):
        g_in = refs[:n]
        got = refs[n:2 * n]
        send_sems, recv_sems = refs[2 * n:]
        x, y, c = _place()
        cps = []
        for a in range(n):
            rh = grads[a].shape[1] // 2
            lo = pl.multiple_of((1 - c) * rh, 8)
            cp = pltpu.make_async_remote_copy(g_in[a].at[:, pl.ds(lo, rh), :], got[a], send_sems.at[a],
                                              recv_sems.at[a], device_id=(x, y, 1 - c), device_id_type=MESH)
            cp.start()
            cps.append(cp)
        for cp in cps:
            cp.wait()

    return pl.pallas_call(
        body, in_specs=[HBM_SPEC] * n, out_specs=[HBM_SPEC] * n,
        out_shape=[jax.ShapeDtypeStruct((N_SHARDS, g.shape[1] // 2, g.shape[2]), F32) for g in grads],
        scratch_shapes=[pltpu.SemaphoreType.DMA((n,)), pltpu.SemaphoreType.DMA((n,))],
        compiler_params=COMM_PARAMS, name="sibling_exchange",
    )(*grads)


def _sibling_assemble(fulls):
    n = len(fulls)

    def body(*refs):
        full = refs[n:2 * n]
        send_sems, recv_sems = refs[2 * n:]
        x, y, c = _place()
        cps = []
        for a in range(n):
            rh = fulls[a].shape[0] // 2
            mine = full[a].at[pl.ds(pl.multiple_of(c * rh, 8), rh), :]
            cp = pltpu.make_async_remote_copy(mine, mine, send_sems.at[a], recv_sems.at[a],
                                              device_id=(x, y, 1 - c), device_id_type=MESH)
            cp.start()
            cps.append(cp)
        for cp in cps:
            cp.wait()

    return pl.pallas_call(
        body, in_specs=[HBM_SPEC] * n, out_specs=[HBM_SPEC] * n,
        out_shape=[jax.ShapeDtypeStruct(f.shape, F32) for f in fulls],
        input_output_aliases={a: a for a in range(n)},
        scratch_shapes=[pltpu.SemaphoreType.DMA((n,)), pltpu.SemaphoreType.DMA((n,))],
        compiler_params=COMM_PARAMS, name="sibling_assemble",
    )(*fulls)


def _remote(ref_src, ref_dst, send_sems, recv_sems, k, to):
    return pltpu.make_async_remote_copy(ref_src, ref_dst, send_sems.at[k], recv_sems.at[k], device_id=to,
                                        device_id_type=MESH)


def _stage_half(w, ids, name):
    R, C = w.shape
    rh = R // 2
    rt = _half_tile(rh)
    nt = rh // rt

    def body(ids_ref, w_ref, o_ref):
        o_ref[0] = w_ref[...].astype(BF16)

    grid_spec = pltpu.PrefetchScalarGridSpec(
        num_scalar_prefetch=1, grid=(nt,),
        in_specs=[pl.BlockSpec((rt, C), lambda i, ids: (ids[2] * nt + i, 0))],
        out_specs=pl.BlockSpec((1, rt, C), lambda i, ids: (2 * ids[0] + ids[1], ids[2] * nt + i, 0)))
    return pl.pallas_call(body, grid_spec=grid_spec, out_shape=jax.ShapeDtypeStruct((N_SHARDS, R, C), BF16),
                          compiler_params=_params(1), name=name)(ids, w)


def _bg_gather(staged):
    n = len(staged)

    def run(step, n_steps, ins, outs, send_sems, recv_sems, local_sems, post):
        x, y, c = _place()
        s_me = 2 * x + y
        sibling = (x, y, 1 - c)
        chips = [(_flip(x, fx), _flip(y, fy)) for fx, fy in CHIP_FLIPS]

        def rows(a, s, half):
            rh = staged[a].shape[1] // 2
            return outs[a].at[s, pl.ds(pl.multiple_of(half * rh, 16), rh), :]

        def copy(a, k, ref, to):
            return _remote(ref, ref, send_sems, recv_sems, a * 7 + k, to)

        if not post:
            @pl.when(step == 0)
            def _():
                for a in range(n):
                    mine = rows(a, s_me, c)
                    copy(a, 0, mine, sibling).start()
                    for k, (px, py) in enumerate(chips):
                        copy(a, 1 + k, mine, (px, py, c)).start()

            @pl.when(step == max(n_steps - 2, 0))
            def _():
                for a in range(n):
                    for k, (px, py) in enumerate(chips):
                        got = rows(a, 2 * px + py, c)
                        copy(a, 1 + k, got, sibling).wait_recv()
                        copy(a, 4 + k, got, sibling).start()
        else:
            @pl.when(step == n_steps - 1)
            def _():
                for a in range(n):
                    for k in (0, 4, 5, 6):
                        copy(a, k, rows(a, s_me, 1 - c), sibling).wait_recv()
                    for k in range(7):
                        copy(a, k, rows(a, s_me, c), sibling).wait_send()

    return _Background(staged, [jax.ShapeDtypeStruct(g.shape, g.dtype) for g in staged],
                       {a: a for a in range(n)}, 7 * n, run)


def _bg_sibling_exchange(grads):
    n = len(grads)

    def run(step, n_steps, ins, outs, send_sems, recv_sems, local_sems, post):
        x, y, c = _place()

        def copy(a):
            rh = grads[a].shape[1] // 2
            lo = pl.multiple_of((1 - c) * rh, 8)
            return _remote(ins[a].at[:, pl.ds(lo, rh), :], outs[a], send_sems, recv_sems, a, (x, y, 1 - c))

        if not post:
            @pl.when(step == 0)
            def _():
                for a in range(n):
                    copy(a).start()
        else:
            @pl.when(step == n_steps - 1)
            def _():
                for a in range(n):
                    copy(a).wait()

    return _Background(grads, [jax.ShapeDtypeStruct((N_SHARDS, g.shape[1] // 2, g.shape[2]), F32) for g in grads],
                       {}, n, run)


def _bg_chip_exchange(chip_parts, pack=None):
    n = len(chip_parts)

    def run(step, n_steps, ins, outs, send_sems, recv_sems, local_sems, post):
        x, y, c = _place()
        me = 4 * x + 2 * y + c

        def copies():
            cps = []
            for a in range(n):
                for k, (fx, fy) in enumerate(CHIP_FLIPS):
                    px, py = _flip(x, fx), _flip(y, fy)
                    cps.append(_remote(ins[a].at[2 * px + py], outs[a].at[k], send_sems, recv_sems, a * 3 + k,
                                       (px, py, c)))
            if pack is not None:
                for m in range(1, N_DEV):
                    to = (_flip(x, m & 4), _flip(y, m & 2), _flip(c, m & 1))
                    cps.append(_remote(ins[n], outs[n].at[me], send_sems, recv_sems, n * 3 + m - 1, to))
            return cps

        def local():
            return pltpu.make_async_copy(ins[n], outs[n].at[me], local_sems.at[0])

        if not post:
            @pl.when(step == 0)
            def _():
                for cp in copies():
                    cp.start()
                if pack is not None:
                    local().start()
        else:
            @pl.when(step == n_steps - 1)
            def _():
                for cp in copies():
                    cp.wait()
                if pack is not None:
                    local().wait()

    in_arrays = list(chip_parts) + ([pack] if pack is not None else [])
    out_shapes = [jax.ShapeDtypeStruct((3,) + p.shape[1:], BF16) for p in chip_parts]
    if pack is not None:
        out_shapes.append(jax.ShapeDtypeStruct((N_DEV, pack.shape[0], LANES), F32))
    return _Background(in_arrays, out_shapes, {}, n * 3 + N_DEV - 1, run)


def _half_tile(rh, mult=16, want=256):
    best = None
    for t in range(mult, min(rh, want) + 1, mult):
        if rh % t == 0:
            best = t
    return best if best is not None else rh


def _pair_sum(g, sib, ids, name):
    _, R, C = g.shape
    rh = R // 2
    rt = _half_tile(rh)
    nt = rh // rt

    def body(ids_ref, g_ref, s_ref, o_ref):
        o_ref[...] = (g_ref[...] + s_ref[...]).astype(BF16)

    def other(j, ids):
        return j + (j >= 2 * ids[0] + ids[1]).astype(jnp.int32)

    grid_spec = pltpu.PrefetchScalarGridSpec(
        num_scalar_prefetch=1, grid=(N_SHARDS - 1, nt),
        in_specs=[pl.BlockSpec((1, rt, C), lambda j, i, ids: (other(j, ids), ids[2] * nt + i, 0)),
                  pl.BlockSpec((1, rt, C), lambda j, i, ids: (other(j, ids), i, 0))],
        out_specs=pl.BlockSpec((1, rt, C), lambda j, i, ids: (other(j, ids), i, 0)))
    return pl.pallas_call(body, grid_spec=grid_spec, out_shape=jax.ShapeDtypeStruct((N_SHARDS, rh, C), BF16),
                          compiler_params=_params(2), name=name)(ids, g, sib)


def _final_sum(g, sib, got, ids, name):
    _, R, C = g.shape
    rh = R // 2
    rt = _half_tile(rh)
    nt = rh // rt

    def body(ids_ref, g_ref, s_ref, r_ref, o_ref):
        tot = g_ref[0] + s_ref[0]
        for k in range(3):
            tot = tot + r_ref[k].astype(F32)
        o_ref[...] = tot

    grid_spec = pltpu.PrefetchScalarGridSpec(
        num_scalar_prefetch=1, grid=(nt,),
        in_specs=[pl.BlockSpec((1, rt, C), lambda i, ids: (2 * ids[0] + ids[1], ids[2] * nt + i, 0)),
                  pl.BlockSpec((1, rt, C), lambda i, ids: (2 * ids[0] + ids[1], i, 0)),
                  pl.BlockSpec((3, rt, C), lambda i, ids: (0, i, 0))],
        out_specs=pl.BlockSpec((rt, C), lambda i, ids: (ids[2] * nt + i, 0)))
    return pl.pallas_call(body, grid_spec=grid_spec, out_shape=jax.ShapeDtypeStruct((R, C), F32),
                          compiler_params=_params(1), name=name)(ids, g, sib, got)


def _sum_packs(all_packs):
    def body(p_ref, o_ref):
        tot = p_ref[0]
        for i in range(1, N_DEV):
            tot = tot + p_ref[i]
        o_ref[...] = tot

    return pl.pallas_call(body, in_specs=[VMEM_SPEC], out_specs=VMEM_SPEC,
                          out_shape=jax.ShapeDtypeStruct(all_packs.shape[1:], F32), name="sum_packs")(all_packs)


def _adamw(w, g, m, v, name, g_transposed=False):
    R, C = w.shape
    rt = _half_tile(R, mult=LANES if g_transposed else 8, want=256)

    def body(w_ref, g_ref, m_ref, v_ref, g_out_ref, d_ref, nm_ref, nv_ref):
        gg = g_ref[...].T if g_transposed else g_ref[...]
        g_out_ref[...] = gg
        d_ref[...], nm_ref[...], nv_ref[...] = _adamw_update(w_ref[...], gg, m_ref[...], v_ref[...])

    spec = pl.BlockSpec((rt, C), lambda i: (i, 0))
    g_spec = pl.BlockSpec((C, rt), lambda i: (0, i)) if g_transposed else spec
    return pl.pallas_call(body, grid=(R // rt,), in_specs=[spec, g_spec, spec, spec], out_specs=[spec] * 4,
                          out_shape=[jax.ShapeDtypeStruct((R, C), F32)] * 4,
                          compiler_params=_params(1), name=name)(w, g, m, v)


def _adamw_update(w, g, m, v):
    nm = ADAM_B1 * m + (1.0 - ADAM_B1) * g
    nv = ADAM_B2 * v + (1.0 - ADAM_B2) * (g * g)
    m_hat = nm / (1.0 - ADAM_B1 ** ADAM_STEP)
    v_hat = nv / (1.0 - ADAM_B2 ** ADAM_STEP)
    return -ADAM_LR * (m_hat / (jnp.sqrt(v_hat) + ADAM_EPS) + ADAM_WD * w), nm, nv


def _adamw_many(ws, gs, ms, vs, name):
    n = len(ws)

    def body(*refs):
        for i in range(n):
            d, nm, nv = _adamw_update(refs[i][...], refs[n + i][...], refs[2 * n + i][...], refs[3 * n + i][...])
            refs[4 * n + i][...] = d
            refs[5 * n + i][...] = nm
            refs[6 * n + i][...] = nv

    return pl.pallas_call(body, in_specs=[VMEM_SPEC] * (4 * n), out_specs=[VMEM_SPEC] * (3 * n),
                          out_shape=[jax.ShapeDtypeStruct(w.shape, F32) for w in ws] * 3, name=name,
                          )(*ws, *gs, *ms, *vs)


def _pack(pieces):
    rows = []
    for p in pieces:
        flat = p.reshape(-1)
        pad = (-flat.shape[0]) % LANES
        if pad:
            flat = jnp.concatenate([flat, jnp.zeros((pad,), F32)])
        rows.append(flat.reshape(-1, LANES))
    total = sum(r.shape[0] for r in rows)
    pad_rows = (-total) % 8
    if pad_rows:
        rows.append(jnp.zeros((pad_rows, LANES), F32))
    return jnp.concatenate(rows, axis=0)


def _unpack(buf, shapes):
    out, r0 = [], 0
    for shp in shapes:
        n = int(np.prod(shp))
        nr = -(-n // LANES)
        out.append(buf[r0:r0 + nr].reshape(-1)[:n].reshape(shp))
        r0 += nr
    return out


SMALL_NAMES = ("rel_table", "b_in", "conv_w", "conv_b", "conv_ln_g", "conv_ln_b", "attn_norm_g", "conv_norm_g",
               "ln1_g", "ln1_b", "ffn_conv_w", "ffn_conv_b", "ln2_g", "ln2_b")
BIG_NAMES = ("w_in", "w_out", "w_up", "w_down")
WEIGHT_ORDER = ("rel_table", "w_in", "b_in", "conv_w", "conv_b", "conv_ln_g", "conv_ln_b", "attn_norm_g",
                "conv_norm_g", "w_out", "ln1_g", "ln1_b", "w_up", "ffn_conv_w", "ffn_conv_b", "w_down",
                "ln2_g", "ln2_b")


def kernel(x, rel_table, w_in, b_in, conv_w, conv_b, conv_ln_g, conv_ln_b, attn_norm_g, conv_norm_g, w_out, ln1_g, ln1_b, w_up, ffn_conv_w, ffn_conv_b, w_down, ln2_g, ln2_b, loss_target, m_rel_table, m_w_in, m_b_in, m_conv_w, m_conv_b, m_conv_ln_g, m_conv_ln_b, m_attn_norm_g, m_conv_norm_g, m_w_out, m_ln1_g, m_ln1_b, m_w_up, m_ffn_conv_w, m_ffn_conv_b, m_w_down, m_ln2_g, m_ln2_b, v_rel_table, v_w_in, v_b_in, v_conv_w, v_conv_b, v_conv_ln_g, v_conv_ln_b, v_attn_norm_g, v_conv_norm_g, v_w_out, v_ln1_g, v_ln1_b, v_w_up, v_ffn_conv_w, v_ffn_conv_b, v_w_down, v_ln2_g, v_ln2_b):
    args = dict(locals())
    weights = {n: args[n] for n in WEIGHT_ORDER}
    moms = {n: args["m_" + n] for n in WEIGHT_ORDER}
    vels = {n: args["v_" + n] for n in WEIGHT_ORDER}
    xi, yi, ci = _place()
    ids = jnp.stack([xi, yi, ci]).astype(jnp.int32)
    shard = 2 * xi + yi
    D = x.shape[-1]
    DFF = w_down.shape[1] * N_SHARDS
    CW = conv_norm_g.shape[-1]

    tr = lambda t: jnp.transpose(t[0])
    (g_in,), (g_cw, g_fcw) = _gather_weights([tr(w_in)], [conv_w[0], ffn_conv_w[0]])
    cols = lambda t: jnp.transpose(t, (1, 0, 2)).reshape(t.shape[1], N_SHARDS * t.shape[2])
    staged = [_stage_half(w[0], ids, name="stage_" + n) for w, n in ((w_out, "w_out"), (w_up, "w_up"),
                                                                     (w_down, "w_down"))]

    grad_x, fulls, all_packs = _local_step(
        x, loss_target, rel_table, g_in.reshape(-1, D), b_in, cols(g_cw), conv_b, conv_ln_g, conv_ln_b, attn_norm_g,
        conv_norm_g, staged, ln1_g, ln1_b, cols(g_fcw), ffn_conv_b, ln2_g, ln2_b, ids)
    big_grads = dict(zip(BIG_NAMES, _sibling_assemble(fulls)))

    summed = _sum_packs(all_packs)
    full_shapes = {n: weights[n].shape for n in SMALL_NAMES}
    full_shapes["conv_w"] = (1, CONV_KERNEL, CW)
    full_shapes["ffn_conv_w"] = (1, FFN_CONV_KERNEL, 2 * DFF)
    un = _unpack(summed, [(1, LANES)] + [full_shapes[n] for n in SMALL_NAMES])
    loss = un[0][0, 0]
    small_grads = dict(zip(SMALL_NAMES, un[1:]))
    for n in ("conv_w", "ffn_conv_w"):
        width = weights[n].shape[-1]
        small_grads[n] = lax.dynamic_slice_in_dim(small_grads[n], shard * width, width, axis=2)

    grads, delta, new_m, new_v = {}, {}, {}, {}
    for n in BIG_NAMES:
        shp = weights[n].shape
        g2 = big_grads[n]
        if n == "w_in":
            res = [jnp.transpose(t) for t in _adamw(tr(weights[n]), g2, tr(moms[n]), tr(vels[n]), name="adamw_" + n)]
        else:
            res = _adamw(weights[n][0], g2, moms[n][0], vels[n][0], name="adamw_" + n, g_transposed=n == "w_up")
        grads[n], delta[n], new_m[n], new_v[n] = (t.reshape(shp) for t in res)
    pick = lambda src: [src[n] for n in SMALL_NAMES]
    small_out = _adamw_many(pick(weights), pick(small_grads), pick(moms), pick(vels), name="adamw_small")
    ns = len(SMALL_NAMES)
    for tgt, part in ((delta, small_out[:ns]), (new_m, small_out[ns:2 * ns]), (new_v, small_out[2 * ns:])):
        tgt.update(zip(SMALL_NAMES, part))
    grads.update(small_grads)

    return (loss, grad_x, *[grads[n] for n in WEIGHT_ORDER], *[delta[n] for n in WEIGHT_ORDER],
            *[new_m[n] for n in WEIGHT_ORDER], *[new_v[n] for n in WEIGHT_ORDER])
```

```python
import math

import numpy as np
import jax
import jax.numpy as jnp
from jax import lax
from jax.experimental import pallas as pl
from jax.experimental.pallas import tpu as pltpu

F32 = jnp.float32
BF16 = jnp.bfloat16
MESH = pl.DeviceIdType.MESH

HEAD_DIM = 64
LANES = 128
ATTN_BLOCK = 128
DILATED_CONFIGS = ((128, 1), (512, 4), (2048, 16))
CONV_KERNEL = 31
FFN_CONV_KERNEL = 3
REL_BUCKETS = 32
REL_MAX_DIST = 2048
DEPTH = 1
ALPHA = (2 * DEPTH) ** 0.25
LN_EPS = 1e-5
NEG_INF = -1e30
QK_SCALE = 1.0 / math.sqrt(HEAD_DIM)
ADAM_LR = 0.001
ADAM_B1 = 0.9
ADAM_B2 = 0.999
ADAM_EPS = 1e-08
ADAM_WD = 0.01
ADAM_STEP = 10
VMEM_LIMIT = 52 * 1024 * 1024
FFN_COLS = 128
N_SHARDS = 4
N_DEV = 8


def _params(n_axes):
    return pltpu.CompilerParams(dimension_semantics=("arbitrary",) * n_axes,
                                vmem_limit_bytes=VMEM_LIMIT)


MM_DIMS = {"nn": (((1,), (0,)), ((), ())), "nt": (((1,), (1,)), ((), ())), "tn": (((0,), (0,)), ((), ()))}


class _Background:
    def __init__(self, in_arrays, out_shapes, aliases, n_sems, run, n_local=1):
        self.in_arrays, self.out_shapes, self.aliases = list(in_arrays), list(out_shapes), dict(aliases)
        self.n_sems, self.n_local, self.run = n_sems, n_local, run

    def scratch(self):
        return [pltpu.SemaphoreType.DMA((self.n_sems,)), pltpu.SemaphoreType.DMA((self.n_sems,)),
                pltpu.SemaphoreType.DMA((self.n_local,))]


def _hosted_call(body, bg, *, grid, in_specs, out_specs, out_shape, scratch_shapes, operands, name):
    n_in, n_out, n_scr = len(in_specs), len(out_specs), len(scratch_shapes)
    if bg is None:
        return pl.pallas_call(lambda *refs: body(refs, lambda post: None), grid=grid, in_specs=in_specs,
                              out_specs=out_specs, out_shape=out_shape, scratch_shapes=scratch_shapes,
                              compiler_params=_params(len(grid)), name=name)(*operands)
    nb_in, nb_out = len(bg.in_arrays), len(bg.out_shapes)
    n_steps = int(np.prod(grid))

    def full_body(*refs):
        own = refs[:n_in] + refs[n_in + nb_in:n_in + nb_in + n_out] \
            + refs[n_in + nb_in + n_out + nb_out:n_in + nb_in + n_out + nb_out + n_scr]
        bg_in = refs[n_in:n_in + nb_in]
        bg_out = refs[n_in + nb_in + n_out:n_in + nb_in + n_out + nb_out]
        sems = refs[n_in + nb_in + n_out + nb_out + n_scr:]
        step = pl.program_id(0)
        for ax in range(1, len(grid)):
            step = step * grid[ax] + pl.program_id(ax)

        def hook(post):
            bg.run(step, n_steps, bg_in, bg_out, *sems, post)

        body(own, hook)

    res = pl.pallas_call(
        full_body, grid=grid, in_specs=list(in_specs) + [HBM_SPEC] * nb_in,
        out_specs=list(out_specs) + [HBM_SPEC] * nb_out, out_shape=list(out_shape) + bg.out_shapes,
        input_output_aliases={n_in + a: n_out + o for a, o in bg.aliases.items()},
        scratch_shapes=list(scratch_shapes) + bg.scratch(), compiler_params=_params(len(grid)), name=name,
    )(*operands, *bg.in_arrays)
    return res


def _matmul_general(ins, part_fn, *, grid, tm, tn, outs, epilogue, extras=(), name, bg=None):
    nk = grid[2]
    n_in, n_extra = len(ins), len(extras)

    def body(refs, bg_hook):
        in_refs = refs[:n_in]
        rest = refs[n_in:]
        extra_refs = rest[:n_extra]
        out_refs = rest[n_extra:n_extra + len(outs)]
        acc_ref = rest[-1]
        i, j, k = pl.program_id(0), pl.program_id(1), pl.program_id(2)
        bg_hook(False)
        part = part_fn(in_refs, i, j, k)
        if nk == 1:
            epilogue(part, i, j, extra_refs, out_refs)
        else:
            @pl.when(k == 0)
            def _():
                acc_ref[...] = part

            @pl.when(k > 0)
            def _():
                acc_ref[...] += part

            @pl.when(k == nk - 1)
            def _():
                epilogue(acc_ref[...], i, j, extra_refs, out_refs)
        bg_hook(True)

    in_specs = [pl.BlockSpec(bs, im) for (_, bs, im) in list(ins) + list(extras)]
    out_specs = [pl.BlockSpec(bs, im) for (_, _, bs, im) in outs]
    out_shape = [jax.ShapeDtypeStruct(s, d) for (s, d, _, _) in outs]
    return _hosted_call(body, bg, grid=grid, in_specs=in_specs, out_specs=out_specs, out_shape=out_shape,
                        scratch_shapes=[pltpu.VMEM((tm, tn), F32)],
                        operands=[e[0] for e in ins] + [e[0] for e in extras], name=name)


def _dot(a, b, mode):
    return lax.dot_general(a.astype(BF16), b.astype(BF16), MM_DIMS[mode], preferred_element_type=F32)


def _matmul(a, b, *, mode, tm, tn, tk, outs, epilogue, extras=(), name, bg=None):
    if mode == "tn":
        K, M = a.shape
        N = b.shape[1]
        ins = [(a, (tk, tm), lambda i, j, k: (k, i)), (b, (tk, tn), lambda i, j, k: (k, j))]
    elif mode == "nt":
        M, K = a.shape
        N = b.shape[0]
        ins = [(a, (tm, tk), lambda i, j, k: (i, k)), (b, (tn, tk), lambda i, j, k: (j, k))]
    else:
        M, K = a.shape
        N = b.shape[1]
        ins = [(a, (tm, tk), lambda i, j, k: (i, k)), (b, (tk, tn), lambda i, j, k: (k, j))]
    assert M % tm == 0 and N % tn == 0 and K % tk == 0, (name, M, N, K, tm, tn, tk)

    def part_fn(in_refs, i, j, k):
        return _dot(in_refs[0][...], in_refs[1][...], mode)

    return _matmul_general(ins, part_fn, grid=(M // tm, N // tn, K // tk), tm=tm, tn=tn, outs=outs,
                           epilogue=epilogue, extras=extras, name=name, bg=bg)


def _plain_out(M, N, tm, tn, dtype):
    return ((M, N), dtype, (tm, tn), lambda i, j, k: (i, j))


def _mm_plain(a, b, *, mode, tm, tn, tk, out_dtype, name, bias=None, bg=None):
    if mode == "tn":
        M, N = a.shape[1], b.shape[1]
    elif mode == "nt":
        M, N = a.shape[0], b.shape[0]
    else:
        M, N = a.shape[0], b.shape[1]
    extras = []
    if bias is not None:
        extras.append((bias, (1, tn), lambda i, j, k: (0, j)))

    def epilogue(acc, i, j, extra_refs, out_refs):
        if bias is not None:
            acc = acc + extra_refs[0][...]
        out_refs[0][...] = acc.astype(out_dtype)

    res = _matmul(a, b, mode=mode, tm=tm, tn=tn, tk=tk, outs=[_plain_out(M, N, tm, tn, out_dtype)],
                  epilogue=epilogue, extras=extras, name=name, bg=bg)
    return res[0] if bg is None else res


def _row_tile(T, want):
    t = min(T, want)
    while T % t:
        t //= 2
    return t


def _col_tile(N, want):
    if N <= want:
        return N
    best = None
    for c in range(LANES, want + 1, LANES):
        if N % c == 0:
            best = c
    return best if best is not None else N


def _accumulate(ref, first, val):
    @pl.when(first)
    def _():
        ref[...] = val

    @pl.when(jnp.logical_not(first))
    def _():
        ref[...] += val


def _ln_fwd(z, g, b):
    mu = jnp.mean(z, axis=-1, keepdims=True)
    zc = z - mu
    var = jnp.mean(zc * zc, axis=-1, keepdims=True)
    r = lax.rsqrt(var + LN_EPS)
    xh = zc * r
    return xh * g + b, xh, r


def _ln_bwd(dy, xh, r, g):
    dxh = dy * g
    m1 = jnp.mean(dxh, axis=-1, keepdims=True)
    m2 = jnp.mean(dxh * xh, axis=-1, keepdims=True)
    return r * (dxh - m1 - xh * m2)


def _sigmoid(x):
    return 0.5 * jnp.tanh(0.5 * x) + 0.5


def _bucket_tables():
    exact = REL_BUCKETS // 2
    qi = np.arange(ATTN_BLOCK)[:, None]
    kj = np.arange(2 * ATTN_BLOCK)[None, :]
    steps = qi + ATTN_BLOCK - kj
    buckets, masks = [], []
    for window, dilation in DILATED_CONFIGS:
        max_steps = window // dilation
        band = (steps >= 0) & (steps <= max_steps)
        dist = np.maximum(steps, 0) * dilation
        d_f = np.maximum(dist, 1).astype(np.float32)
        large = exact + (np.log(d_f / np.float32(exact)) / np.float32(math.log(REL_MAX_DIST / exact))
                         * np.float32(REL_BUCKETS - exact)).astype(np.int32)
        large = np.minimum(large, REL_BUCKETS - 1)
        bucket = np.where(dist < exact, dist, large).astype(np.int32)
        buckets.append(bucket.reshape(1, -1))
        masks.append(np.where(band, 0.0, NEG_INF).astype(np.float32).reshape(1, -1))
    return np.stack(buckets), np.stack(masks)


def _split_hi_lo(x):
    hi = x.astype(BF16)
    lo = (x - hi.astype(F32)).astype(BF16)
    return hi, lo


def _bias_build(rel_table_t, bucket, mask):
    H = rel_table_t.shape[0]
    n = bucket.shape[-1]

    def body(t_ref, bkt_ref, mask_ref, o_ref):
        onehot = (lax.broadcasted_iota(jnp.int32, (REL_BUCKETS, n), 0) == bkt_ref[0]).astype(BF16)
        t = t_ref[...]
        t1 = t.astype(BF16)
        r1 = t - t1.astype(F32)
        t2 = r1.astype(BF16)
        t3 = (r1 - t2.astype(F32)).astype(BF16)
        acc = jnp.dot(t1, onehot, preferred_element_type=F32)
        acc = acc + jnp.dot(t2, onehot, preferred_element_type=F32)
        acc = acc + jnp.dot(t3, onehot, preferred_element_type=F32)
        o_ref[0] = acc + mask_ref[0]

    return pl.pallas_call(
        body, grid=(3,),
        in_specs=[pl.BlockSpec((H, REL_BUCKETS), lambda b: (0, 0)),
                  pl.BlockSpec((1, 1, n), lambda b: (b, 0, 0)),
                  pl.BlockSpec((1, 1, n), lambda b: (b, 0, 0))],
        out_specs=pl.BlockSpec((1, H, n), lambda b: (b, 0, 0)),
        out_shape=jax.ShapeDtypeStruct((3, H, n), F32),
        compiler_params=_params(1), name="bias_build",
    )(rel_table_t, bucket, mask)


def _rel_grad(dbias, bucket):
    H = dbias.shape[1]
    n = bucket.shape[-1]
    dims = (((1,), (1,)), ((), ()))

    def body(d_ref, bkt_ref, o_ref):
        b = pl.program_id(0)
        onehot = (lax.broadcasted_iota(jnp.int32, (REL_BUCKETS, n), 0) == bkt_ref[0]).astype(BF16)
        d = d_ref[0]
        d1 = d.astype(BF16)
        r1 = d - d1.astype(F32)
        d2 = r1.astype(BF16)
        d3 = (r1 - d2.astype(F32)).astype(BF16)
        acc = lax.dot_general(d1, onehot, dims, preferred_element_type=F32)
        acc = acc + lax.dot_general(d2, onehot, dims, preferred_element_type=F32)
        acc = acc + lax.dot_general(d3, onehot, dims, preferred_element_type=F32)
        _accumulate(o_ref, b == 0, acc)

    return pl.pallas_call(
        body, grid=(3,),
        in_specs=[pl.BlockSpec((1, H, n), lambda b: (b, 0, 0)),
                  pl.BlockSpec((1, 1, n), lambda b: (b, 0, 0))],
        out_specs=pl.BlockSpec((H, REL_BUCKETS), lambda b: (0, 0)),
        out_shape=jax.ShapeDtypeStruct((H, REL_BUCKETS), F32),
        compiler_params=_params(1), name="rel_grad",
    )(dbias, bucket)


def _regroup(src, stage, dst, d, S, off=0):
    if d == 1:
        dst[off:off + S, :] = src.astype(dst.dtype)
        return
    stage[...] = src.astype(F32)
    L = S // d
    for r in range(d):
        dst[off + r * L:off + (r + 1) * L, :] = stage[pl.ds(r, L, stride=d), :].astype(dst.dtype)


def _ungroup(sub_ref, off, nat_ref, d, S, add):
    L = S // d
    for r in range(d):
        rows = pl.ds(0, S) if d == 1 else pl.ds(r, L, stride=d)
        val = sub_ref[off + r * L:off + (r + 1) * L, :]
        if add:
            nat_ref[rows, :] += val
        else:
            nat_ref[rows, :] = val


def _branch_keys(ks, vs, S, nb, g_idx):
    blk3 = (S // ATTN_BLOCK, ATTN_BLOCK, LANES)
    kc3 = ks[ATTN_BLOCK:ATTN_BLOCK + S, :].reshape(blk3)
    vc3 = vs[ATTN_BLOCK:ATTN_BLOCK + S, :].reshape(blk3)
    if nb == 1:
        return kc3, vc3, None
    kk3 = jnp.concatenate([ks[0:S, :].reshape(blk3), kc3], axis=1)
    vv3 = jnp.concatenate([vs[0:S, :].reshape(blk3), vc3], axis=1)
    col = lax.broadcasted_iota(jnp.int32, (1, 1, 2 * ATTN_BLOCK), 2)
    dead = jnp.logical_and((g_idx & (nb - 1)) == 0, col < ATTN_BLOCK)
    return kk3, vv3, dead


def _branch_scores(qe, kk3, b_ref, bi, e, dead):
    s = jnp.einsum("gqe,gke->gqk", qe, kk3, preferred_element_type=F32)
    if dead is None:
        return s + b_ref[bi, e, :, ATTN_BLOCK:]
    return jnp.where(dead, NEG_INF, s + b_ref[bi, e])


def _attention_fwd(qkv, bias_all, B, S, AW, bg=None):
    HP = AW // LANES
    G = S // ATTN_BLOCK
    blk3 = (G, ATTN_BLOCK, LANES)

    def body(refs, bg_hook):
        q_ref, k_ref, v_ref, b_ref, o_ref, lse_ref, stage, qs, ks, vs, ot, lt, on0, on1, on2, ln0, ln1, ln2 = refs
        bg_hook(False)
        head0 = lax.broadcasted_iota(jnp.int32, (1, 1, LANES), 2) < HEAD_DIM
        g_idx = lax.broadcasted_iota(jnp.int32, (G, 1, 1), 0)
        ks[0:ATTN_BLOCK, :] = jnp.zeros((ATTN_BLOCK, LANES), BF16)
        vs[0:ATTN_BLOCK, :] = jnp.zeros((ATTN_BLOCK, LANES), BF16)
        nat_o, nat_l = (on0, on1, on2), (ln0, ln1, ln2)
        for bi, (_, d) in enumerate(DILATED_CONFIGS):
            nb = S // d // ATTN_BLOCK
            _regroup(q_ref[0], stage, qs, d, S)
            _regroup(k_ref[0], stage, ks, d, S, ATTN_BLOCK)
            _regroup(v_ref[0], stage, vs, d, S, ATTN_BLOCK)
            q3 = qs[...].reshape(blk3) * QK_SCALE
            kk3, vv3, dead = _branch_keys(ks, vs, S, nb, g_idx)
            outs, lses = [], []
            for e in range(2):
                msk = head0 if e == 0 else jnp.logical_not(head0)
                qe = jnp.where(msk, q3, jnp.zeros_like(q3))
                s = _branch_scores(qe, kk3, b_ref, bi, e, dead)
                m = jnp.max(s, axis=-1, keepdims=True)
                p = jnp.exp(s - m)
                l = jnp.sum(p, axis=-1, keepdims=True)
                o = jnp.einsum("gqk,gke->gqe", p.astype(BF16), vv3, preferred_element_type=F32)
                outs.append(o / l)
                lses.append(jnp.broadcast_to(m + jnp.log(l), blk3))
            ot[...] = jnp.where(head0, outs[0], outs[1]).reshape(S, LANES)
            lt[...] = jnp.where(head0, lses[0], lses[1]).reshape(S, LANES)
            _ungroup(ot, 0, nat_o[bi], d, S, add=False)
            _ungroup(lt, 0, nat_l[bi], d, S, add=False)

        la, lb, lc = ln0[...], ln1[...], ln2[...]
        m = jnp.maximum(jnp.maximum(la, lb), lc)
        ea, eb, ec = jnp.exp(la - m), jnp.exp(lb - m), jnp.exp(lc - m)
        den = ea + eb + ec
        lse_ref[0] = m + jnp.log(den)
        o_ref[0] = (ea * on0[...] + eb * on1[...] + ec * on2[...]) / den
        bg_hook(True)

    blk = lambda off: pl.BlockSpec((1, S, LANES), lambda b, h: (b, 0, off + h))
    qv = qkv.reshape(B, S, 3 * AW)
    sub_f = pltpu.VMEM((S, LANES), F32)
    pad_b = pltpu.VMEM((S + ATTN_BLOCK, LANES), BF16)
    res = _hosted_call(
        body, bg, grid=(B, HP),
        in_specs=[blk(0), blk(HP), blk(2 * HP),
                  pl.BlockSpec((3, 2, ATTN_BLOCK, 2 * ATTN_BLOCK), lambda b, h: (0, h, 0, 0))],
        out_specs=[blk(0), blk(0)],
        out_shape=[jax.ShapeDtypeStruct((B, S, AW), F32)] * 2,
        scratch_shapes=[sub_f, pltpu.VMEM((S, LANES), BF16), pad_b, pad_b] + [sub_f] * 8,
        operands=[qv, qv, qv, bias_all], name="attention_fwd")
    return (res[0].reshape(B * S, AW), res[1].reshape(B * S, AW)) + tuple(res[2:])


def _attention_bwd(qkv, do, lse, dd, bias_all, B, S, AW, bg=None):
    HP = AW // LANES
    H = AW // HEAD_DIM
    G = S // ATTN_BLOCK
    blk3 = (G, ATTN_BLOCK, LANES)
    PAD = ATTN_BLOCK

    def body(refs, bg_hook):
        (q_ref, k_ref, v_ref, do_ref, lse_ref, dd_ref, b_ref,
         dq_ref, dk_ref, dv_ref, csq_ref, csk_ref, csv_ref, db_ref,
         stage, qs, ks, vs, gs, ls, ds_, tq, tk, tv, accq, acck, accv) = refs
        bg_hook(False)
        head0 = lax.broadcasted_iota(jnp.int32, (1, 1, LANES), 2) < HEAD_DIM
        g_idx = lax.broadcasted_iota(jnp.int32, (G, 1, 1), 0)
        first_b = pl.program_id(1) == 0

        @pl.when(first_b)
        def _():
            db_ref[...] = jnp.zeros_like(db_ref)

        ks[0:PAD, :] = jnp.zeros((PAD, LANES), BF16)
        vs[0:PAD, :] = jnp.zeros((PAD, LANES), BF16)
        tk[0:PAD, :] = jnp.zeros((PAD, LANES), F32)
        tv[0:PAD, :] = jnp.zeros((PAD, LANES), F32)
        for bi, (_, d) in enumerate(DILATED_CONFIGS):
            nb = S // d // ATTN_BLOCK
            _regroup(q_ref[0], stage, qs, d, S)
            _regroup(k_ref[0], stage, ks, d, S, PAD)
            _regroup(v_ref[0], stage, vs, d, S, PAD)
            _regroup(do_ref[0], stage, gs, d, S)
            _regroup(lse_ref[0], stage, ls, d, S)
            _regroup(dd_ref[0], stage, ds_, d, S)
            q3 = qs[...].reshape(blk3) * QK_SCALE
            do3 = gs[...].reshape(blk3)
            lse3 = ls[...].reshape(blk3)
            dd3 = ds_[...].reshape(blk3)
            kk3, vv3, dead = _branch_keys(ks, vs, S, nb, g_idx)
            dq = jnp.zeros(blk3, F32)
            dkk = jnp.zeros(kk3.shape, F32)
            dvv = jnp.zeros(kk3.shape, F32)
            for e in range(2):
                msk = head0 if e == 0 else jnp.logical_not(head0)
                c0 = e * HEAD_DIM
                qe = jnp.where(msk, q3, jnp.zeros_like(q3))
                doe = jnp.where(msk, do3, jnp.zeros_like(do3))
                ke = jnp.where(msk, kk3 * QK_SCALE, jnp.zeros_like(kk3))
                s = _branch_scores(qe, kk3, b_ref, bi, e, dead)
                p = jnp.exp(s - lse3[:, :, c0:c0 + 1])
                dp = jnp.einsum("gqe,gke->gqk", doe, vv3, preferred_element_type=F32)
                dsc = p * (dp - dd3[:, :, c0:c0 + 1])
                if dead is None:
                    db_ref[bi, e, :, ATTN_BLOCK:] += jnp.sum(dsc, axis=0)
                else:
                    db_ref[bi, e] += jnp.sum(dsc, axis=0)
                dsb = dsc.astype(BF16)
                dq = dq + jnp.einsum("gqk,gke->gqe", dsb, ke, preferred_element_type=F32)
                dkk = dkk + jnp.einsum("gqk,gqe->gke", dsb, qe, preferred_element_type=F32)
                dvv = dvv + jnp.einsum("gqk,gqe->gke", p.astype(BF16), doe, preferred_element_type=F32)
            tq[...] = dq.reshape(S, LANES)
            if dead is None:
                tk[PAD:PAD + S, :] = dkk.reshape(S, LANES)
                tv[PAD:PAD + S, :] = dvv.reshape(S, LANES)
            else:
                tk[PAD:PAD + S, :] = dkk[:, ATTN_BLOCK:, :].reshape(S, LANES)
                tv[PAD:PAD + S, :] = dvv[:, ATTN_BLOCK:, :].reshape(S, LANES)
                tk[0:S, :] += dkk[:, :ATTN_BLOCK, :].reshape(S, LANES)
                tv[0:S, :] += dvv[:, :ATTN_BLOCK, :].reshape(S, LANES)
            _ungroup(tq, 0, accq, d, S, add=bi > 0)
            _ungroup(tk, PAD, acck, d, S, add=bi > 0)
            _ungroup(tv, PAD, accv, d, S, add=bi > 0)

        for acc, out_ref, cs_ref in ((accq, dq_ref, csq_ref), (acck, dk_ref, csk_ref), (accv, dv_ref, csv_ref)):
            tot = acc[...]
            out_ref[0] = tot.astype(out_ref.dtype)
            _accumulate(cs_ref, first_b, jnp.sum(tot, axis=0, keepdims=True))
        bg_hook(True)

    blk = lambda off: pl.BlockSpec((1, S, LANES), lambda h, b: (b, 0, off + h))
    cs_spec = pl.BlockSpec((1, LANES), lambda h, b: (0, h))
    bias_spec = pl.BlockSpec((3, 2, ATTN_BLOCK, 2 * ATTN_BLOCK), lambda h, b: (0, h, 0, 0))
    qv = qkv.reshape(B, S, 3 * AW)
    view = lambda t: t.reshape(B, S, AW)
    sub_b = pltpu.VMEM((S, LANES), BF16)
    sub_f = pltpu.VMEM((S, LANES), F32)
    pad_b = pltpu.VMEM((S + PAD, LANES), BF16)
    pad_f = pltpu.VMEM((S + PAD, LANES), F32)
    res = _hosted_call(
        body, bg, grid=(HP, B),
        in_specs=[blk(0), blk(HP), blk(2 * HP), blk(0), blk(0), blk(0), bias_spec],
        out_specs=[blk(0), blk(0), blk(0), cs_spec, cs_spec, cs_spec, bias_spec],
        out_shape=[jax.ShapeDtypeStruct((B, S, AW), BF16)] * 3 + [jax.ShapeDtypeStruct((1, AW), F32)] * 3
        + [jax.ShapeDtypeStruct((3, H, ATTN_BLOCK, 2 * ATTN_BLOCK), F32)],
        scratch_shapes=[sub_f, sub_b, pad_b, pad_b, sub_b, sub_f, sub_f, sub_f, pad_f, pad_f, sub_f, sub_f, sub_f],
        operands=[qv, qv, qv, view(do), view(lse), view(dd), bias_all], name="attention_bwd")
    flat = lambda t: t.reshape(B * S, AW)
    return (flat(res[0]), flat(res[1]), flat(res[2]), res[3], res[4], res[5], res[6]) + tuple(res[7:])


class _RowShifts:
    def __init__(self, x, row, up):
        self.x, self.row, self.up, self.base = x, row, up, {0: x}

    def __call__(self, s):
        x = self.x
        n, c = x.shape
        r, whole = s % 8, s - s % 8
        if r not in self.base:
            if self.up:
                rolled = pltpu.roll(x, n - r, 0)
                tail = jnp.where(self.row[n - 8:] < n - r, rolled[n - 8:], 0.0)
                self.base[r] = jnp.concatenate([rolled[:n - 8], tail], axis=0)
            else:
                rolled = pltpu.roll(x, r, 0)
                head = jnp.where(self.row[:8] >= r, rolled[:8], 0.0)
                self.base[r] = jnp.concatenate([head, rolled[8:]], axis=0)
        y = self.base[r]
        if whole == 0:
            return y
        pad = jnp.zeros((whole, c), x.dtype)
        if self.up:
            return jnp.concatenate([y[whole:], pad], axis=0)
        return jnp.concatenate([pad, y[:n - whole]], axis=0)


def _conv_branch_fwd_math(a, g, w_ref, cb, lg, lb, row):
    sg = _sigmoid(g)
    u0 = a * sg
    u0_down = _RowShifts(u0, row, up=False)
    uc = jnp.zeros_like(u0) + cb
    for k in range(CONV_KERNEL):
        uc = uc + w_ref[k:k + 1, :] * u0_down(CONV_KERNEL - 1 - k)
    ul, xh, r = _ln_fwd(uc, lg, lb)
    su = _sigmoid(ul)
    u = ul * su
    return sg, u0_down, ul, xh, r, su, u


def _conv_fwd(ag, conv_w, conv_b, ln_g, ln_b, norm_g, B, S, CW):
    def body(a_ref, g_ref, w_ref, cb_ref, lg_ref, lb_ref, ng_ref, o_ref):
        row = lax.broadcasted_iota(jnp.int32, (S, CW), 0)
        _, _, _, _, _, _, u = _conv_branch_fwd_math(a_ref[0], g_ref[0], w_ref, cb_ref[...], lg_ref[...],
                                                    lb_ref[...], row)
        rr = lax.rsqrt(jnp.mean(u * u, axis=-1, keepdims=True) + LN_EPS)
        o_ref[0] = (u * rr * ng_ref[...]).astype(BF16)

    vec = pl.BlockSpec((1, CW), lambda b: (0, 0))
    out = pl.pallas_call(
        body, grid=(B,),
        in_specs=[pl.BlockSpec((1, S, CW), lambda b: (b, 0, 0)), pl.BlockSpec((1, S, CW), lambda b: (b, 0, 1)),
                  pl.BlockSpec((CONV_KERNEL, CW), lambda b: (0, 0)), vec, vec, vec, vec],
        out_specs=pl.BlockSpec((1, S, CW), lambda b: (b, 0, 0)),
        out_shape=jax.ShapeDtypeStruct((B, S, CW), BF16),
        compiler_params=_params(1), name="conv_fwd",
    )(ag.reshape(B, S, 2 * CW), ag.reshape(B, S, 2 * CW), conv_w, conv_b, ln_g, ln_b, norm_g)
    return out.reshape(B * S, CW)


def _conv_bwd(ag, dmc, conv_w, conv_b, ln_g, ln_b, norm_g, B, S, CW):
    def body(a_ref, g_ref, dm_ref, w_ref, cb_ref, lg_ref, lb_ref, ng_ref,
             dag_ref, dw_ref, dcb_ref, dlg_ref, dlb_ref, dng_ref):
        b = pl.program_id(0)
        row = lax.broadcasted_iota(jnp.int32, (S, CW), 0)
        a, g = a_ref[0], g_ref[0]
        sg, u0_down, ul, xh, r, su, u = _conv_branch_fwd_math(a, g, w_ref, cb_ref[...], lg_ref[...], lb_ref[...], row)
        rr = lax.rsqrt(jnp.mean(u * u, axis=-1, keepdims=True) + LN_EPS)
        dm = dm_ref[0]
        dxn = dm * ng_ref[...]
        du = rr * (dxn - u * (rr * rr) * jnp.mean(dxn * u, axis=-1, keepdims=True))
        dul = du * su * (1.0 + ul * (1.0 - su))
        duc = _ln_bwd(dul, xh, r, lg_ref[...])
        first = b == 0
        _accumulate(dng_ref, first, jnp.sum(dm * u * rr, axis=0, keepdims=True))
        _accumulate(dlg_ref, first, jnp.sum(dul * xh, axis=0, keepdims=True))
        _accumulate(dlb_ref, first, jnp.sum(dul, axis=0, keepdims=True))
        _accumulate(dcb_ref, first, jnp.sum(duc, axis=0, keepdims=True))

        @pl.when(first)
        def _():
            dw_ref[...] = jnp.zeros_like(dw_ref)

        duc_up = _RowShifts(duc, row, up=True)
        du0 = jnp.zeros_like(duc)
        for k in range(CONV_KERNEL):
            sh = CONV_KERNEL - 1 - k
            dw_ref[k:k + 1, :] += jnp.sum(duc * u0_down(sh), axis=0, keepdims=True)
            du0 = du0 + w_ref[k:k + 1, :] * duc_up(sh)
        dag_ref[0, :, :CW] = du0 * sg
        dag_ref[0, :, CW:] = du0 * a * sg * (1.0 - sg)

    vec = pl.BlockSpec((1, CW), lambda b: (0, 0))
    wspec = pl.BlockSpec((CONV_KERNEL, CW), lambda b: (0, 0))
    agv = ag.reshape(B, S, 2 * CW)
    res = pl.pallas_call(
        body, grid=(B,),
        in_specs=[pl.BlockSpec((1, S, CW), lambda b: (b, 0, 0)), pl.BlockSpec((1, S, CW), lambda b: (b, 0, 1)),
                  pl.BlockSpec((1, S, CW), lambda b: (b, 0, 0)), wspec, vec, vec, vec, vec],
        out_specs=[pl.BlockSpec((1, S, 2 * CW), lambda b: (b, 0, 0)), wspec, vec, vec, vec, vec],
        out_shape=[jax.ShapeDtypeStruct((B, S, 2 * CW), F32), jax.ShapeDtypeStruct((CONV_KERNEL, CW), F32)]
        + [jax.ShapeDtypeStruct((1, CW), F32)] * 4,
        compiler_params=_params(1), name="conv_bwd",
    )(agv, agv, dmc.reshape(B, S, CW), conv_w, conv_b, ln_g, ln_b, norm_g)
    return (res[0].reshape(B * S, 2 * CW),) + tuple(res[1:])


def _ffn_conv(x, w_ref, bias, row):
    down = x if isinstance(x, _RowShifts) else _RowShifts(x, row, up=False)
    y = jnp.zeros_like(down.x) + bias
    for k in range(FFN_CONV_KERNEL):
        y = y + w_ref[k:k + 1, :] * down(FFN_CONV_KERNEL - 1 - k)
    return y


def _ffn_specs(S, tc, nj, order):
    pick = (lambda b, j: (b, j)) if order == "bj" else (lambda j, b: (b, j))
    act = lambda off: pl.BlockSpec((1, S, tc), lambda *g: (pick(*g)[0], 0, off + pick(*g)[1]))
    cw = lambda off: pl.BlockSpec((FFN_CONV_KERNEL, tc), lambda *g: (0, off + pick(*g)[1]))
    cb = lambda off: pl.BlockSpec((1, tc), lambda *g: (0, off + pick(*g)[1]))
    return act, cw, cb


FFN_HALO = 16


def _half_sequences(S):
    if S < 8 * FFN_HALO:
        return [(0, S, 0, S)]
    h = S // 2
    return [(0, h + FFN_HALO, 0, h), (h - FFN_HALO, S, FFN_HALO, h)]


def _w_up_block_spec(w_up_sh, tc, off):
    _, D, cs = w_up_sh.shape
    assert cs % tc == 0
    bps = cs // tc
    return pl.BlockSpec((1, D, tc), lambda j: ((off + j) // bps, 0, (off + j) % bps))


def _ffn_fwd_fused(x1b, w_up_sh, cw, cb, B, S, DFF):
    tc = FFN_COLS
    nj = DFF // tc
    D = x1b.shape[1]

    def body(x_ref, wg_ref, wv_ref, cwg_ref, cwv_ref, cbg_ref, cbv_ref, o_ref, up_ref):
        w = jnp.concatenate([wg_ref[0], wv_ref[0]], axis=1)
        for b in range(B):
            for lo, hi, o0, on in _half_sequences(S):
                row = lax.broadcasted_iota(jnp.int32, (hi - lo, tc), 0)
                up = jnp.dot(x_ref[b, lo:hi, :], w, preferred_element_type=F32)
                up_ref[b, lo + o0:lo + o0 + on, :] = up[o0:o0 + on]
                gate = _ffn_conv(up[:, :tc], cwg_ref, cbg_ref[...], row)
                val = _ffn_conv(up[:, tc:], cwv_ref, cbv_ref[...], row)
                o_ref[b, lo + o0:lo + o0 + on, :] = (gate * _sigmoid(gate) * val).astype(BF16)[o0:o0 + on]

    cws = lambda off: pl.BlockSpec((FFN_CONV_KERNEL, tc), lambda j: (0, off + j))
    cbs = lambda off: pl.BlockSpec((1, tc), lambda j: (0, off + j))
    act, upre = pl.pallas_call(
        body, grid=(nj,),
        in_specs=[pl.BlockSpec((B, S, D), lambda j: (0, 0, 0), pipeline_mode=pl.Buffered(1)),
                  _w_up_block_spec(w_up_sh, tc, 0), _w_up_block_spec(w_up_sh, tc, nj),
                  cws(0), cws(nj), cbs(0), cbs(nj)],
        out_specs=[pl.BlockSpec((B, S, tc), lambda j: (0, 0, j)), pl.BlockSpec((B, S, 2 * tc), lambda j: (0, 0, j))],
        out_shape=[jax.ShapeDtypeStruct((B, S, DFF), BF16), jax.ShapeDtypeStruct((B, S, 2 * DFF), F32)],
        compiler_params=_params(1), name="ffn_fwd",
    )(x1b.reshape(B, S, D), w_up_sh, w_up_sh, cw, cw, cb, cb)
    return act.reshape(B * S, DFF), upre


def _ffn_bwd_fused(x1b, dz2b, upre, w_down, cw, cb, B, S, DFF):
    tc = FFN_COLS
    nj = DFF // tc
    D = x1b.shape[1]

    def body(x_ref, dz_ref, up_ref, wd_ref, cwg_ref, cwv_ref, cbg_ref, cbv_ref,
             dug_ref, duv_ref, dwu_ref, dwd_ref, dcw_ref, dcb_ref):
        first = pl.program_id(1) == 0
        dw_t = dwd = None
        dcb = [None, None]
        dcw = [[None] * FFN_CONV_KERNEL, [None] * FFN_CONV_KERNEL]
        add = lambda old, new: new if old is None else old + new
        for lo, hi, o0, on in _half_sequences(S):
            n = hi - lo
            own = slice(o0, o0 + on)
            row = lax.broadcasted_iota(jnp.int32, (n, tc), 0)
            x = x_ref[0, lo:hi, :]
            dz = dz_ref[0, lo:hi, :]
            ug = _RowShifts(up_ref[0, lo:hi, :tc], row, up=False)
            uv = _RowShifts(up_ref[0, lo:hi, tc:], row, up=False)
            gate = _ffn_conv(ug, cwg_ref, cbg_ref[...], row)
            val = _ffn_conv(uv, cwv_ref, cbv_ref[...], row)
            sg = _sigmoid(gate)
            act = (gate * sg * val).astype(BF16)
            dact = _dot(dz, wd_ref[...], "nt")
            dgate = dact * val * sg * (1.0 + gate * (1.0 - sg))
            dval = dact * gate * sg
            dupre = []
            for h, (dup, u_down, w_ref) in enumerate(((dgate, ug, cwg_ref), (dval, uv, cwv_ref))):
                dcb[h] = add(dcb[h], jnp.sum(dup[own], axis=0, keepdims=True))
                dup_up = _RowShifts(dup, row, up=True)
                acc = jnp.zeros_like(dup)
                for k in range(FFN_CONV_KERNEL):
                    sh = FFN_CONV_KERNEL - 1 - k
                    dcw[h][k] = add(dcw[h][k], jnp.sum((dup * u_down(sh))[own], axis=0, keepdims=True))
                    acc = acc + w_ref[k:k + 1, :] * dup_up(sh)
                dupre.append(acc.astype(BF16)[own])
            dug_ref[0, lo + o0:lo + o0 + on, :] = dupre[0]
            duv_ref[0, lo + o0:lo + o0 + on, :] = dupre[1]
            dw_t = add(dw_t, _dot(jnp.concatenate(dupre, axis=1), x[own], "tn"))
            dwd = add(dwd, _dot(act[own], dz[own], "tn"))
        _accumulate(dwu_ref.at[0], first, dw_t[:tc])
        _accumulate(dwu_ref.at[1], first, dw_t[tc:])
        _accumulate(dwd_ref, first, dwd)
        for h in range(2):
            _accumulate(dcb_ref.at[h], first, dcb[h])
            for k in range(FFN_CONV_KERNEL):
                _accumulate(dcw_ref.at[k, pl.ds(h, 1), :], first, dcw[h][k])

    act_s, cws, cbs = _ffn_specs(S, tc, nj, "jb")
    seq = pl.BlockSpec((1, S, D), lambda j, b: (b, 0, 0))
    res = pl.pallas_call(
        body, grid=(nj, B),
        in_specs=[seq, seq, pl.BlockSpec((1, S, 2 * tc), lambda j, b: (b, 0, j)),
                  pl.BlockSpec((tc, D), lambda j, b: (j, 0)), cws(0), cws(nj), cbs(0), cbs(nj)],
        out_specs=[act_s(0), act_s(0), pl.BlockSpec((2, tc, D), lambda j, b: (0, j, 0)),
                   pl.BlockSpec((tc, D), lambda j, b: (j, 0)),
                   pl.BlockSpec((FFN_CONV_KERNEL, 2, tc), lambda j, b: (0, 0, j)),
                   pl.BlockSpec((2, 1, tc), lambda j, b: (0, 0, j))],
        out_shape=[jax.ShapeDtypeStruct((B, S, DFF), BF16)] * 2
        + [jax.ShapeDtypeStruct((2, DFF, D), F32), jax.ShapeDtypeStruct((DFF, D), F32),
           jax.ShapeDtypeStruct((FFN_CONV_KERNEL, 2, DFF), F32), jax.ShapeDtypeStruct((2, 1, DFF), F32)],
        compiler_params=_params(2), name="ffn_bwd",
    )(x1b.reshape(B, S, D), dz2b.reshape(B, S, D), upre, w_down, cw, cw, cb, cb)
    flat = lambda t: t.reshape(B * S, DFF)
    return flat(res[0]), flat(res[1]), res[2], res[3], res[4], res[5]


def _dx1_ln1_bwd(dupre_g, dupre_v, w_up_sh, dz2, xh1, r1, ln1_g, tm, bg):
    T, D = dz2.shape
    NS, _, cs = w_up_sh.shape
    half = NS // 2
    DFF = dupre_g.shape[1]

    def body(refs, bg_hook):
        dug_ref, duv_ref, w_ref, dz2_ref, xh_ref, r_ref, g_ref, dz_ref, dzb_ref, dg_ref, db_ref = refs
        bg_hook(False)
        first = pl.program_id(0) == 0
        dg = db = None
        for rows in (slice(0, tm // 2), slice(tm // 2, tm)):
            dx1 = ALPHA * dz2_ref[rows, :]
            for k in range(NS):
                src = dug_ref if k < half else duv_ref
                c0 = (k % half) * cs
                dx1 = dx1 + _dot(src[rows, c0:c0 + cs], w_ref[k], "nt")
            xh = xh_ref[rows, :]
            dz = _ln_bwd(dx1, xh, r_ref[rows, 0:1], g_ref[...])
            dz_ref[rows, :] = dz
            dzb_ref[rows, :] = dz.astype(BF16)
            dg_h, db_h = jnp.sum(dx1 * xh, axis=0, keepdims=True), jnp.sum(dx1, axis=0, keepdims=True)
            dg, db = (dg_h, db_h) if dg is None else (dg + dg_h, db + db_h)
        _accumulate(dg_ref, first, dg)
        _accumulate(db_ref, first, db)
        bg_hook(True)

    row = pl.BlockSpec((tm, D), lambda i: (i, 0))
    vec = pl.BlockSpec((1, D), lambda i: (0, 0))
    du = pl.BlockSpec((tm, DFF), lambda i: (i, 0))
    return _hosted_call(
        body, bg, grid=(T // tm,),
        in_specs=[du, du, pl.BlockSpec((NS, D, cs), lambda i: (0, 0, 0), pipeline_mode=pl.Buffered(1)),
                  row, row, pl.BlockSpec((tm, LANES), lambda i: (i, 0)), vec],
        out_specs=[row, row, vec, vec],
        out_shape=[jax.ShapeDtypeStruct((T, D), F32), jax.ShapeDtypeStruct((T, D), BF16),
                   jax.ShapeDtypeStruct((1, D), F32), jax.ShapeDtypeStruct((1, D), F32)],
        scratch_shapes=[], operands=[dupre_g, dupre_v, w_up_sh, dz2, xh1, r1, ln1_g], name="mm_dx1_ln1_bwd")


def _dh_cat(dq, dk, dv, dag, tm):
    T, AW = dq.shape
    CW2 = dag.shape[1]
    W = 3 * AW + CW2

    def body(dq_ref, dk_ref, dv_ref, dag_ref, dh_ref, cs_ref):
        for c, ref in enumerate((dq_ref, dk_ref, dv_ref)):
            dh_ref[:, c * AW:(c + 1) * AW] = ref[...]
        dg = dag_ref[...]
        dh_ref[:, 3 * AW:] = dg.astype(BF16)
        _accumulate(cs_ref, pl.program_id(0) == 0, jnp.sum(dg, axis=0, keepdims=True))

    row = pl.BlockSpec((tm, AW), lambda i: (i, 0))
    return pl.pallas_call(
        body, grid=(T // tm,),
        in_specs=[row] * 3 + [pl.BlockSpec((tm, CW2), lambda i: (i, 0))],
        out_specs=[pl.BlockSpec((tm, W), lambda i: (i, 0)), pl.BlockSpec((1, CW2), lambda i: (0, 0))],
        out_shape=[jax.ShapeDtypeStruct((T, W), BF16), jax.ShapeDtypeStruct((1, CW2), F32)],
        compiler_params=_params(1), name="dh_cat",
    )(dq, dk, dv, dag)


def _local_step(x, target, rel_table, w_in_t, b_in, conv_w, conv_b, conv_ln_g, conv_ln_b, attn_norm_g,
                conv_norm_g, staged, ln1_g, ln1_b, ffn_cw, ffn_cb, ln2_g, ln2_b, ids):
    B, S, D = x.shape
    T = B * S
    AW = attn_norm_g.shape[-1]
    CW = conv_norm_g.shape[-1]
    H = AW // HEAD_DIM
    DFF = staged[2].shape[0] * staged[2].shape[1]
    INW = 3 * AW + 2 * CW
    xf = x.reshape(T, D)
    tf = target.reshape(T, D)
    tm = _row_tile(T, 512)
    tm_s = tm

    bucket_np, mask_np = _bucket_tables()
    bucket = jnp.asarray(bucket_np)
    band_mask = jnp.asarray(mask_np)
    bias_all = _bias_build(rel_table.T, bucket, band_mask).reshape(3, H, ATTN_BLOCK, 2 * ATTN_BLOCK)

    def in_proj(n0, n, tn, rows, out_dtype, name):
        assert n0 % tn == 0 and n % tn == 0

        def epilogue(acc, i, j, extra_refs, out_refs):
            out_refs[0][...] = (acc + extra_refs[0][...]).astype(out_dtype)

        return _matmul_general(
            [(xf, (rows, D), lambda i, j, k: (i, 0)), (w_in_t, (tn, D), lambda i, j, k: (n0 // tn + j, 0))],
            lambda refs, i, j, k: _dot(refs[0][...], refs[1][...], "nt"),
            grid=(T // rows, n // tn, 1), tm=rows, tn=tn,
            extras=[(b_in, (1, tn), lambda i, j, k: (0, n0 // tn + j))],
            outs=[_plain_out(T, n, rows, tn, out_dtype)], epilogue=epilogue, name=name)[0]

    qkv = in_proj(0, 3 * AW, 3 * AW, tm, BF16, "mm_qkv")
    ag = in_proj(3 * AW, 2 * CW, math.gcd(3 * AW, 2 * CW), _row_tile(T, 1024), F32, "mm_ag")

    attn, lse, w_out_g, w_up_sh, w_down_g = _attention_fwd(qkv, bias_all, B, S, AW, bg=_bg_gather(staged))
    w_out = w_out_g.reshape(D, D)
    w_down = w_down_g.reshape(DFF, D)
    mixed_c = _conv_fwd(ag, conv_w, conv_b, conv_ln_g, conv_ln_b, conv_norm_g, B, S, CW)

    def attn_rstd(a):
        return lax.rsqrt(jnp.mean(a * a, axis=-1, keepdims=True) + LN_EPS)

    def mixed_rows(attn_ref, mc_ref, gain_ref):
        a = attn_ref[...]
        return jnp.concatenate([(a * attn_rstd(a) * gain_ref[...]).astype(BF16), mc_ref[...]], axis=1)

    def ln1_epilogue(acc, i, j, extra_refs, out_refs):
        x_ref, g_ref, b_ref, a_ref = extra_refs
        x1, xh, r = _ln_fwd(acc + ALPHA * x_ref[...], g_ref[...], b_ref[...])
        out_refs[0][...] = x1
        out_refs[1][...] = x1.astype(BF16)
        out_refs[2][...] = xh
        out_refs[3][...] = jnp.broadcast_to(r, (tm_s, LANES))
        out_refs[4][...] = jnp.broadcast_to(attn_rstd(a_ref[...]), (tm_s, LANES))

    rowD = lambda i, j, k: (i, 0)
    vecD = lambda i, j, k: (0, 0)
    x1, x1b, xh1, r1, r_attn = _matmul_general(
        [(attn, (tm_s, AW), rowD), (mixed_c, (tm_s, CW), rowD), (attn_norm_g, (1, AW), vecD), (w_out, (D, D), vecD)],
        lambda refs, i, j, k: _dot(mixed_rows(refs[0], refs[1], refs[2]), refs[3][...], "nn"),
        grid=(T // tm_s, 1, 1), tm=tm_s, tn=D,
        extras=[(xf, (tm_s, D), rowD), (ln1_g, (1, D), vecD), (ln1_b, (1, D), vecD), (attn, (tm_s, AW), rowD)],
        outs=[((T, D), F32, (tm_s, D), rowD), ((T, D), BF16, (tm_s, D), rowD), ((T, D), F32, (tm_s, D), rowD),
              ((T, LANES), F32, (tm_s, LANES), rowD), ((T, LANES), F32, (tm_s, LANES), rowD)],
        epilogue=ln1_epilogue, name="mm_out_ln1")

    NS, _, cs = w_up_sh.shape
    half = NS // 2

    act, upre = _ffn_fwd_fused(x1b, w_up_sh, ffn_cw, ffn_cb, B, S, DFF)

    halves = [slice(0, tm // 2), slice(tm // 2, tm)]

    def ln2_epilogue(parts, i, j, extra_refs, out_refs):
        x1_ref, g_ref, b_ref, t_ref = extra_refs
        dz_ref, dzb_ref, loss_ref, dg_ref, db_ref = out_refs
        g = g_ref[...]
        sums = None
        for rows, acc in zip(halves, parts):
            y, xh, r = _ln_fwd(acc + ALPHA * x1_ref[rows, :], g, b_ref[...])
            diff = y - t_ref[rows, :]
            row_loss = jnp.sum(diff * diff, axis=1, keepdims=True)
            tile_loss = jnp.sum(row_loss, axis=0, keepdims=True) * (0.5 / D)
            dy = diff * (1.0 / D)
            dz = _ln_bwd(dy, xh, r, g)
            dz_ref[rows, :] = dz
            dzb_ref[rows, :] = dz.astype(BF16)
            vals = (jnp.broadcast_to(tile_loss, (1, LANES)), jnp.sum(dy * xh, axis=0, keepdims=True),
                    jnp.sum(dy, axis=0, keepdims=True))
            sums = vals if sums is None else tuple(a + b for a, b in zip(sums, vals))
        for ref, val in zip((loss_ref, dg_ref, db_ref), sums):
            _accumulate(ref, i == 0, val)

    dz2, dz2b, loss_part, d_ln2_g, d_ln2_b = _matmul_general(
        [(act, (tm, DFF), rowD), (w_down, (DFF, D), vecD)],
        lambda refs, i, j, k: tuple(_dot(refs[0][rows, :], refs[1][...], "nn") for rows in halves),
        grid=(T // tm, 1, 1), tm=tm, tn=D,
        extras=[(x1, (tm, D), rowD), (ln2_g, (1, D), vecD), (ln2_b, (1, D), vecD), (tf, (tm, D), rowD)],
        outs=[((T, D), F32, (tm, D), rowD), ((T, D), BF16, (tm, D), rowD),
              ((1, LANES), F32, (1, LANES), vecD), ((1, D), F32, (1, D), vecD), ((1, D), F32, (1, D), vecD)],
        epilogue=ln2_epilogue, name="mm_down_ln2_loss")

    dupre_g, dupre_v, d_w_up_t, d_w_down, d_ffn_cw2, d_ffn_cb2 = _ffn_bwd_fused(
        x1b, dz2b, upre, w_down, ffn_cw, ffn_cb, B, S, DFF)
    d_w_up_t = d_w_up_t.reshape(NS, cs, D)
    d_ffn_cw = d_ffn_cw2.reshape(FFN_CONV_KERNEL, 2 * DFF)
    d_ffn_cb = d_ffn_cb2.reshape(1, 2 * DFF)
    tk_t = _row_tile(T, 512)

    early = [d_w_up_t, d_w_down.reshape(NS, DFF // NS, D)]
    dz1, dz1b, d_ln1_g, d_ln1_b, *sib_e = _dx1_ln1_bwd(dupre_g, dupre_v, w_up_sh, dz2, xh1, r1, ln1_g, tm,
                                                       bg=_bg_sibling_exchange(early))
    chip_e = [_pair_sum(g, s, ids, name="pair_sum_" + n) for g, s, n in zip(early, sib_e, ("w_up", "w_down"))]

    def dw_out_epilogue(acc, i, j, extra_refs, out_refs):
        out_refs[0][...] = acc

    d_w_out = _matmul_general(
        [(attn, (tk_t, AW), lambda i, j, k: (k, 0)), (mixed_c, (tk_t, CW), lambda i, j, k: (k, 0)),
         (attn_norm_g, (1, AW), vecD), (dz1b, (tk_t, D), lambda i, j, k: (k, 0))],
        lambda refs, i, j, k: _dot(mixed_rows(refs[0], refs[1], refs[2]), refs[3][...], "tn"),
        grid=(1, 1, T // tk_t), tm=D, tn=D, outs=[_plain_out(D, D, D, D, F32)],
        epilogue=dw_out_epilogue, name="mm_dw_out")[0]
    early.append(d_w_out.reshape(NS, D // NS, D))
    def dmixed_epilogue(acc, i, j, extra_refs, out_refs):
        a_ref, r_ref, g_ref = extra_refs
        do_ref, dd_ref, dmc_ref, dg_ref = out_refs
        head_of = lambda axis: lax.broadcasted_iota(jnp.int32, (AW, AW), axis) // HEAD_DIM
        same_head = (head_of(0) == head_of(1)).astype(BF16)
        dm = acc[:, :AW]
        dmc_ref[...] = acc[:, AW:]
        a = a_ref[...]
        r = r_ref[:, 0:1]
        dxn = dm * g_ref[...]
        da = r * (dxn - a * (r * r) * jnp.mean(dxn * a, axis=-1, keepdims=True))
        do_ref[...] = da.astype(BF16)
        hi, lo = _split_hi_lo(da * a)
        dd_ref[...] = (jnp.dot(hi, same_head, preferred_element_type=F32)
                       + jnp.dot(lo, same_head, preferred_element_type=F32))
        _accumulate(dg_ref, i == 0, jnp.sum(dm * a * r, axis=0, keepdims=True))

    dattn, dd, dmc, d_attn_norm_g, sib_out = _matmul(
        dz1b, w_out, mode="nt", tm=tm, tn=D, tk=D,
        extras=[(attn, (tm, AW), rowD), (r_attn, (tm, LANES), rowD), (attn_norm_g, (1, AW), vecD)],
        outs=[((T, AW), BF16, (tm, AW), rowD), ((T, AW), F32, (tm, AW), rowD), ((T, CW), F32, (tm, CW), rowD),
              ((1, AW), F32, (1, AW), vecD)],
        epilogue=dmixed_epilogue, name="mm_dmixed", bg=_bg_sibling_exchange(early[2:]))
    sib_e.append(sib_out)
    chip_e.append(_pair_sum(early[2], sib_out, ids, name="pair_sum_w_out"))

    dag, d_conv_w, d_conv_b, d_conv_ln_g, d_conv_ln_b, d_conv_norm_g = _conv_bwd(
        ag, dmc, conv_w, conv_b, conv_ln_g, conv_ln_b, conv_norm_g, B, S, CW)

    dq, dk, dv, csq, csk, csv, dbias, *got_e = _attention_bwd(qkv, dattn, lse, dd, bias_all, B, S, AW,
                                                              bg=_bg_chip_exchange(chip_e))
    full_up, full_down, full_out = [_final_sum(g, s, r, ids, name="final_sum_" + n)
                                    for g, s, r, n in zip(early, sib_e, got_e, ("w_up", "w_down", "w_out"))]
    d_rel_table = _rel_grad(dbias.reshape(3, H, ATTN_BLOCK * 2 * ATTN_BLOCK), bucket).T
    dh, cs_ag = _dh_cat(dq, dk, dv, dag, tm_s)
    d_b_in = jnp.concatenate([csq, csk, csv, cs_ag], axis=1)

    d_w_in_t = _mm_plain(dh, xf, mode="tn", tm=_col_tile(INW, 1408), tn=D, tk=tk_t, out_dtype=F32, name="mm_dw_in")
    late = [d_w_in_t.reshape(NS, INW // NS, D)]
    sib_l = _sibling_exchange(late)
    chip_l = [_pair_sum(late[0], sib_l[0], ids, name="pair_sum_w_in")]
    small = dict(rel_table=d_rel_table, b_in=d_b_in, conv_w=d_conv_w, conv_b=d_conv_b, conv_ln_g=d_conv_ln_g,
                 conv_ln_b=d_conv_ln_b, attn_norm_g=d_attn_norm_g, conv_norm_g=d_conv_norm_g, ln1_g=d_ln1_g,
                 ln1_b=d_ln1_b, ffn_conv_w=d_ffn_cw, ffn_conv_b=d_ffn_cb, ln2_g=d_ln2_g, ln2_b=d_ln2_b)
    pack = _pack([loss_part] + [small[n] for n in SMALL_NAMES])

    def gx_epilogue(acc, i, j, extra_refs, out_refs):
        out_refs[0][...] = acc + ALPHA * extra_refs[0][...]

    grad_x, got_in, all_packs = _matmul(
        dh, w_in_t, mode="nn", tm=tm, tn=D, tk=INW, extras=[(dz1, (tm, D), rowD)],
        outs=[((T, D), F32, (tm, D), rowD)], epilogue=gx_epilogue, name="mm_grad_x",
        bg=_bg_chip_exchange(chip_l, pack))
    full_in = _final_sum(late[0], sib_l[0], got_in, ids, name="final_sum_w_in")
    return grad_x.reshape(B, S, D), [full_in, full_out, full_up, full_down], all_packs


def _place():
    return lax.axis_index("x"), lax.axis_index("y"), lax.axis_index("c")


CHIP_FLIPS = ((1, 0), (0, 1), (1, 1))


def _flip(v, f):
    return 1 - v if f else v


HBM_SPEC = pl.BlockSpec(memory_space=pl.ANY)
VMEM_SPEC = pl.BlockSpec(memory_space=pltpu.VMEM)
COMM_PARAMS = pltpu.CompilerParams(vmem_limit_bytes=VMEM_LIMIT)


def _gather_weights(big, small):
    nb, ns = len(big), len(small)

    def body(*refs):
        big_in = refs[:nb]
        small_in = refs[nb:nb + ns]
        big_out = refs[nb + ns:2 * nb + ns]
        small_out = refs[2 * nb + ns:2 * nb + 2 * ns]
        stages = refs[2 * nb + 2 * ns:3 * nb + 2 * ns]
        send_sems, recv_sems, local_sems = refs[3 * nb + 2 * ns:]
        x, y, c = _place()
        s_me = 2 * x + y
        sibling = (x, y, 1 - c)
        started, local_copies = [], []
        for a in range(nb):
            rh = big[a].shape[0] // 2
            lo = pl.multiple_of(c * rh, 16)
            stages[a][...] = big_in[a][pl.ds(lo, rh), :].astype(BF16)
            mine = big_out[a].at[s_me, pl.ds(lo, rh), :]
            loc = pltpu.make_async_copy(stages[a], mine, local_sems.at[a])
            loc.start()
            local_copies.append(loc)
            targets = [sibling] + [(_flip(x, fx), _flip(y, fy), c) for fx, fy in CHIP_FLIPS]
            for k, to in enumerate(targets):
                cp = pltpu.make_async_remote_copy(stages[a], mine, send_sems.at[a * 7 + k],
                                                  recv_sems.at[a * 7 + k], device_id=to, device_id_type=MESH)
                cp.start()
                started.append(cp)
        for a in range(ns):
            mine = small_out[a].at[s_me]
            loc = pltpu.make_async_copy(small_in[a], mine, local_sems.at[nb + a])
            loc.start()
            local_copies.append(loc)
            for k, (fx, fy) in enumerate(CHIP_FLIPS):
                cp = pltpu.make_async_remote_copy(small_in[a], mine, send_sems.at[nb * 7 + a * 3 + k],
                                                  recv_sems.at[nb * 7 + a * 3 + k],
                                                  device_id=(_flip(x, fx), _flip(y, fy), c), device_id_type=MESH)
                cp.start()
                started.append(cp)
        for a in range(nb):
            rh = big[a].shape[0] // 2
            lo = pl.multiple_of(c * rh, 16)
            for k, (fx, fy) in enumerate(CHIP_FLIPS):
                s_from = 2 * _flip(x, fx) + _flip(y, fy)
                got = big_out[a].at[s_from, pl.ds(lo, rh), :]
                pltpu.make_async_remote_copy(got, got, send_sems.at[a * 7 + 1 + k], recv_sems.at[a * 7 + 1 + k],
                                             device_id=sibling, device_id_type=MESH).wait_recv()
                fwd = pltpu.make_async_remote_copy(got, got, send_sems.at[a * 7 + 4 + k],
                                                   recv_sems.at[a * 7 + 4 + k], device_id=sibling,
                                                   device_id_type=MESH)
                fwd.start()
                started.append(fwd)
        for a in range(nb):
            rh = big[a].shape[0] // 2
            lo_sib = pl.multiple_of((1 - c) * rh, 16)
            for k in (0, 4, 5, 6):
                any_rows = big_out[a].at[s_me, pl.ds(lo_sib, rh), :]
                pltpu.make_async_remote_copy(any_rows, any_rows, send_sems.at[a * 7 + k], recv_sems.at[a * 7 + k],
                                             device_id=sibling, device_id_type=MESH).wait_recv()
        for a in range(ns):
            for k in range(3):
                pltpu.make_async_remote_copy(small_in[a], small_out[a].at[s_me], send_sems.at[nb * 7 + a * 3 + k],
                                             recv_sems.at[nb * 7 + a * 3 + k], device_id=sibling,
                                             device_id_type=MESH).wait_recv()
        for cp in started:
            cp.wait_send()
        for cp in local_copies:
            cp.wait()

    n_sem = nb * 7 + ns * 3
    out_shape = ([jax.ShapeDtypeStruct((N_SHARDS,) + w.shape, BF16) for w in big]
                 + [jax.ShapeDtypeStruct((N_SHARDS,) + w.shape, F32) for w in small])
    res = pl.pallas_call(
        body, in_specs=[VMEM_SPEC] * nb + [HBM_SPEC] * ns, out_specs=[HBM_SPEC] * (nb + ns),
        out_shape=out_shape,
        scratch_shapes=[pltpu.VMEM((w.shape[0] // 2, w.shape[1]), BF16) for w in big]
        + [pltpu.SemaphoreType.DMA((n_sem,)), pltpu.SemaphoreType.DMA((n_sem,)),
           pltpu.SemaphoreType.DMA((nb + ns,))],
        compiler_params=COMM_PARAMS, name="gather_weights",
    )(*big, *small)
    return res[:nb], res[nb:]


def _sibling_exchange(grads):
    n = len(grads)

    def body(*refs):
        g_in = refs[:n]
        got = refs[n:2 * n]
        send_sems, recv_sems = refs[2 * n:]
        x, y, c = _place()
        cps = []
        for a in range(n):
            rh = grads[a].shape[1] // 2
            lo = pl.multiple_of((1 - c) * rh, 8)
            cp = pltpu.make_async_remote_copy(g_in[a].at[:, pl.ds(lo, rh), :], got[a], send_sems.at[a],
                                              recv_sems.at[a], device_id=(x, y, 1 - c), device_id_type=MESH)
            cp.start()
            cps.append(cp)
        for cp in cps:
            cp.wait()

    return pl.pallas_call(
        body, in_specs=[HBM_SPEC] * n, out_specs=[HBM_SPEC] * n,
        out_shape=[jax.ShapeDtypeStruct((N_SHARDS, g.shape[1] // 2, g.shape[2]), F32) for g in grads],
        scratch_shapes=[pltpu.SemaphoreType.DMA((n,)), pltpu.SemaphoreType.DMA((n,))],
        compiler_params=COMM_PARAMS, name="sibling_exchange",
    )(*grads)


def _sibling_assemble(fulls):
    n = len(fulls)

    def body(*refs):
        full = refs[n:2 * n]
        send_sems, recv_sems = refs[2 * n:]
        x, y, c = _place()
        cps = []
        for a in range(n):
            rh = fulls[a].shape[0] // 2
            mine = full[a].at[pl.ds(pl.multiple_of(c * rh, 8), rh), :]
            cp = pltpu.make_async_remote_copy(mine, mine, send_sems.at[a], recv_sems.at[a],
                                              device_id=(x, y, 1 - c), device_id_type=MESH)
            cp.start()
            cps.append(cp)
        for cp in cps:
            cp.wait()

    return pl.pallas_call(
        body, in_specs=[HBM_SPEC] * n, out_specs=[HBM_SPEC] * n,
        out_shape=[jax.ShapeDtypeStruct(f.shape, F32) for f in fulls],
        input_output_aliases={a: a for a in range(n)},
        scratch_shapes=[pltpu.SemaphoreType.DMA((n,)), pltpu.SemaphoreType.DMA((n,))],
        compiler_params=COMM_PARAMS, name="sibling_assemble",
    )(*fulls)


def _remote(ref_src, ref_dst, send_sems, recv_sems, k, to):
    return pltpu.make_async_remote_copy(ref_src, ref_dst, send_sems.at[k], recv_sems.at[k], device_id=to,
                                        device_id_type=MESH)


def _stage_half(w, ids, name):
    R, C = w.shape
    rh = R // 2
    rt = _half_tile(rh)
    nt = rh // rt

    def body(ids_ref, w_ref, o_ref):
        o_ref[0] = w_ref[...].astype(BF16)

    grid_spec = pltpu.PrefetchScalarGridSpec(
        num_scalar_prefetch=1, grid=(nt,),
        in_specs=[pl.BlockSpec((rt, C), lambda i, ids: (ids[2] * nt + i, 0))],
        out_specs=pl.BlockSpec((1, rt, C), lambda i, ids: (2 * ids[0] + ids[1], ids[2] * nt + i, 0)))
    return pl.pallas_call(body, grid_spec=grid_spec, out_shape=jax.ShapeDtypeStruct((N_SHARDS, R, C), BF16),
                          compiler_params=_params(1), name=name)(ids, w)


def _bg_gather(staged):
    n = len(staged)

    def run(step, n_steps, ins, outs, send_sems, recv_sems, local_sems, post):
        x, y, c = _place()
        s_me = 2 * x + y
        sibling = (x, y, 1 - c)
        chips = [(_flip(x, fx), _flip(y, fy)) for fx, fy in CHIP_FLIPS]

        def rows(a, s, half):
            rh = staged[a].shape[1] // 2
            return outs[a].at[s, pl.ds(pl.multiple_of(half * rh, 16), rh), :]

        def copy(a, k, ref, to):
            return _remote(ref, ref, send_sems, recv_sems, a * 7 + k, to)

        if not post:
            @pl.when(step == 0)
            def _():
                for a in range(n):
                    mine = rows(a, s_me, c)
                    copy(a, 0, mine, sibling).start()
                    for k, (px, py) in enumerate(chips):
                        copy(a, 1 + k, mine, (px, py, c)).start()

            @pl.when(step == max(n_steps - 2, 0))
            def _():
                for a in range(n):
                    for k, (px, py) in enumerate(chips):
                        got = rows(a, 2 * px + py, c)
                        copy(a, 1 + k, got, sibling).wait_recv()
                        copy(a, 4 + k, got, sibling).start()
        else:
            @pl.when(step == n_steps - 1)
            def _():
                for a in range(n):
                    for k in (0, 4, 5, 6):
                        copy(a, k, rows(a, s_me, 1 - c), sibling).wait_recv()
                    for k in range(7):
                        copy(a, k, rows(a, s_me, c), sibling).wait_send()

    return _Background(staged, [jax.ShapeDtypeStruct(g.shape, g.dtype) for g in staged],
                       {a: a for a in range(n)}, 7 * n, run)


def _bg_sibling_exchange(grads):
    n = len(grads)

    def run(step, n_steps, ins, outs, send_sems, recv_sems, local_sems, post):
        x, y, c = _place()

        def copy(a):
            rh = grads[a].shape[1] // 2
            lo = pl.multiple_of((1 - c) * rh, 8)
            return _remote(ins[a].at[:, pl.ds(lo, rh), :], outs[a], send_sems, recv_sems, a, (x, y, 1 - c))

        if not post:
            @pl.when(step == 0)
            def _():
                for a in range(n):
                    copy(a).start()
        else:
            @pl.when(step == n_steps - 1)
            def _():
                for a in range(n):
                    copy(a).wait()

    return _Background(grads, [jax.ShapeDtypeStruct((N_SHARDS, g.shape[1] // 2, g.shape[2]), F32) for g in grads],
                       {}, n, run)


def _bg_chip_exchange(chip_parts, pack=None):
    n = len(chip_parts)

    def run(step, n_steps, ins, outs, send_sems, recv_sems, local_sems, post):
        x, y, c = _place()
        me = 4 * x + 2 * y + c

        def copies():
            cps = []
            for a in range(n):
                for k, (fx, fy) in enumerate(CHIP_FLIPS):
                    px, py = _flip(x, fx), _flip(y, fy)
                    cps.append(_remote(ins[a].at[2 * px + py], outs[a].at[k], send_sems, recv_sems, a * 3 + k,
                                       (px, py, c)))
            if pack is not None:
                for m in range(1, N_DEV):
                    to = (_flip(x, m & 4), _flip(y, m & 2), _flip(c, m & 1))
                    cps.append(_remote(ins[n], outs[n].at[me], send_sems, recv_sems, n * 3 + m - 1, to))
            return cps

        def local():
            return pltpu.make_async_copy(ins[n], outs[n].at[me], local_sems.at[0])

        if not post:
            @pl.when(step == 0)
            def _():
                for cp in copies():
                    cp.start()
                if pack is not None:
                    local().start()
        else:
            @pl.when(step == n_steps - 1)
            def _():
                for cp in copies():
                    cp.wait()
                if pack is not None:
                    local().wait()

    in_arrays = list(chip_parts) + ([pack] if pack is not None else [])
    out_shapes = [jax.ShapeDtypeStruct((3,) + p.shape[1:], BF16) for p in chip_parts]
    if pack is not None:
        out_shapes.append(jax.ShapeDtypeStruct((N_DEV, pack.shape[0], LANES), F32))
    return _Background(in_arrays, out_shapes, {}, n * 3 + N_DEV - 1, run)


def _half_tile(rh, mult=16, want=256):
    best = None
    for t in range(mult, min(rh, want) + 1, mult):
        if rh % t == 0:
            best = t
    return best if best is not None else rh


def _pair_sum(g, sib, ids, name):
    _, R, C = g.shape
    rh = R // 2
    rt = _half_tile(rh)
    nt = rh // rt

    def body(ids_ref, g_ref, s_ref, o_ref):
        o_ref[...] = (g_ref[...] + s_ref[...]).astype(BF16)

    def other(j, ids):
        return j + (j >= 2 * ids[0] + ids[1]).astype(jnp.int32)

    grid_spec = pltpu.PrefetchScalarGridSpec(
        num_scalar_prefetch=1, grid=(N_SHARDS - 1, nt),
        in_specs=[pl.BlockSpec((1, rt, C), lambda j, i, ids: (other(j, ids), ids[2] * nt + i, 0)),
                  pl.BlockSpec((1, rt, C), lambda j, i, ids: (other(j, ids), i, 0))],
        out_specs=pl.BlockSpec((1, rt, C), lambda j, i, ids: (other(j, ids), i, 0)))
    return pl.pallas_call(body, grid_spec=grid_spec, out_shape=jax.ShapeDtypeStruct((N_SHARDS, rh, C), BF16),
                          compiler_params=_params(2), name=name)(ids, g, sib)


def _final_sum(g, sib, got, ids, name):
    _, R, C = g.shape
    rh = R // 2
    rt = _half_tile(rh)
    nt = rh // rt

    def body(ids_ref, g_ref, s_ref, r_ref, o_ref):
        tot = g_ref[0] + s_ref[0]
        for k in range(3):
            tot = tot + r_ref[k].astype(F32)
        o_ref[...] = tot

    grid_spec = pltpu.PrefetchScalarGridSpec(
        num_scalar_prefetch=1, grid=(nt,),
        in_specs=[pl.BlockSpec((1, rt, C), lambda i, ids: (2 * ids[0] + ids[1], ids[2] * nt + i, 0)),
                  pl.BlockSpec((1, rt, C), lambda i, ids: (2 * ids[0] + ids[1], i, 0)),
                  pl.BlockSpec((3, rt, C), lambda i, ids: (0, i, 0))],
        out_specs=pl.BlockSpec((rt, C), lambda i, ids: (ids[2] * nt + i, 0)))
    return pl.pallas_call(body, grid_spec=grid_spec, out_shape=jax.ShapeDtypeStruct((R, C), F32),
                          compiler_params=_params(1), name=name)(ids, g, sib, got)


def _sum_packs(all_packs):
    def body(p_ref, o_ref):
        tot = p_ref[0]
        for i in range(1, N_DEV):
            tot = tot + p_ref[i]
        o_ref[...] = tot

    return pl.pallas_call(body, in_specs=[VMEM_SPEC], out_specs=VMEM_SPEC,
                          out_shape=jax.ShapeDtypeStruct(all_packs.shape[1:], F32), name="sum_packs")(all_packs)


def _adamw(w, g, m, v, name, g_transposed=False):
    R, C = w.shape
    rt = _half_tile(R, mult=LANES if g_transposed else 8, want=256)

    def body(w_ref, g_ref, m_ref, v_ref, g_out_ref, d_ref, nm_ref, nv_ref):
        gg = g_ref[...].T if g_transposed else g_ref[...]
        g_out_ref[...] = gg
        d_ref[...], nm_ref[...], nv_ref[...] = _adamw_update(w_ref[...], gg, m_ref[...], v_ref[...])

    spec = pl.BlockSpec((rt, C), lambda i: (i, 0))
    g_spec = pl.BlockSpec((C, rt), lambda i: (0, i)) if g_transposed else spec
    return pl.pallas_call(body, grid=(R // rt,), in_specs=[spec, g_spec, spec, spec], out_specs=[spec] * 4,
                          out_shape=[jax.ShapeDtypeStruct((R, C), F32)] * 4,
                          compiler_params=_params(1), name=name)(w, g, m, v)


def _adamw_update(w, g, m, v):
    nm = ADAM_B1 * m + (1.0 - ADAM_B1) * g
    nv = ADAM_B2 * v + (1.0 - ADAM_B2) * (g * g)
    m_hat = nm / (1.0 - ADAM_B1 ** ADAM_STEP)
    v_hat = nv / (1.0 - ADAM_B2 ** ADAM_STEP)
    return -ADAM_LR * (m_hat / (jnp.sqrt(v_hat) + ADAM_EPS) + ADAM_WD * w), nm, nv


def _adamw_many(ws, gs, ms, vs, name):
    n = len(ws)

    def body(*refs):
        for i in range(n):
            d, nm, nv = _adamw_update(refs[i][...], refs[n + i][...], refs[2 * n + i][...], refs[3 * n + i][...])
            refs[4 * n + i][...] = d
            refs[5 * n + i][...] = nm
            refs[6 * n + i][...] = nv

    return pl.pallas_call(body, in_specs=[VMEM_SPEC] * (4 * n), out_specs=[VMEM_SPEC] * (3 * n),
                          out_shape=[jax.ShapeDtypeStruct(w.shape, F32) for w in ws] * 3, name=name,
                          )(*ws, *gs, *ms, *vs)


def _pack(pieces):
    rows = []
    for p in pieces:
        flat = p.reshape(-1)
        pad = (-flat.shape[0]) % LANES
        if pad:
            flat = jnp.concatenate([flat, jnp.zeros((pad,), F32)])
        rows.append(flat.reshape(-1, LANES))
    total = sum(r.shape[0] for r in rows)
    pad_rows = (-total) % 8
    if pad_rows:
        rows.append(jnp.zeros((pad_rows, LANES), F32))
    return jnp.concatenate(rows, axis=0)


def _unpack(buf, shapes):
    out, r0 = [], 0
    for shp in shapes:
        n = int(np.prod(shp))
        nr = -(-n // LANES)
        out.append(buf[r0:r0 + nr].reshape(-1)[:n].reshape(shp))
        r0 += nr
    return out


SMALL_NAMES = ("rel_table", "b_in", "conv_w", "conv_b", "conv_ln_g", "conv_ln_b", "attn_norm_g", "conv_norm_g",
               "ln1_g", "ln1_b", "ffn_conv_w", "ffn_conv_b", "ln2_g", "ln2_b")
BIG_NAMES = ("w_in", "w_out", "w_up", "w_down")
WEIGHT_ORDER = ("rel_table", "w_in", "b_in", "conv_w", "conv_b", "conv_ln_g", "conv_ln_b", "attn_norm_g",
                "conv_norm_g", "w_out", "ln1_g", "ln1_b", "w_up", "ffn_conv_w", "ffn_conv_b", "w_down",
                "ln2_g", "ln2_b")


def kernel(x, rel_table, w_in, b_in, conv_w, conv_b, conv_ln_g, conv_ln_b, attn_norm_g, conv_norm_g, w_out, ln1_g, ln1_b, w_up, ffn_conv_w, ffn_conv_b, w_down, ln2_g, ln2_b, loss_target, m_rel_table, m_w_in, m_b_in, m_conv_w, m_conv_b, m_conv_ln_g, m_conv_ln_b, m_attn_norm_g, m_conv_norm_g, m_w_out, m_ln1_g, m_ln1_b, m_w_up, m_ffn_conv_w, m_ffn_conv_b, m_w_down, m_ln2_g, m_ln2_b, v_rel_table, v_w_in, v_b_in, v_conv_w, v_conv_b, v_conv_ln_g, v_conv_ln_b, v_attn_norm_g, v_conv_norm_g, v_w_out, v_ln1_g, v_ln1_b, v_w_up, v_ffn_conv_w, v_ffn_conv_b, v_w_down, v_ln2_g, v_ln2_b):
    args = dict(locals())
    weights = {n: args[n] for n in WEIGHT_ORDER}
    moms = {n: args["m_" + n] for n in WEIGHT_ORDER}
    vels = {n: args["v_" + n] for n in WEIGHT_ORDER}
    xi, yi, ci = _place()
    ids = jnp.stack([xi, yi, ci]).astype(jnp.int32)
    shard = 2 * xi + yi
    D = x.shape[-1]
    DFF = w_down.shape[1] * N_SHARDS
    CW = conv_norm_g.shape[-1]

    tr = lambda t: jnp.transpose(t[0])
    (g_in,), (g_cw, g_fcw) = _gather_weights([tr(w_in)], [conv_w[0], ffn_conv_w[0]])
    cols = lambda t: jnp.transpose(t, (1, 0, 2)).reshape(t.shape[1], N_SHARDS * t.shape[2])
    staged = [_stage_half(w[0], ids, name="stage_" + n) for w, n in ((w_out, "w_out"), (w_up, "w_up"),
                                                                     (w_down, "w_down"))]

    grad_x, fulls, all_packs = _local_step(
        x, loss_target, rel_table, g_in.reshape(-1, D), b_in, cols(g_cw), conv_b, conv_ln_g, conv_ln_b, attn_norm_g,
        conv_norm_g, staged, ln1_g, ln1_b, cols(g_fcw), ffn_conv_b, ln2_g, ln2_b, ids)
    big_grads = dict(zip(BIG_NAMES, _sibling_assemble(fulls)))

    summed = _sum_packs(all_packs)
    full_shapes = {n: weights[n].shape for n in SMALL_NAMES}
    full_shapes["conv_w"] = (1, CONV_KERNEL, CW)
    full_shapes["ffn_conv_w"] = (1, FFN_CONV_KERNEL, 2 * DFF)
    un = _unpack(summed, [(1, LANES)] + [full_shapes[n] for n in SMALL_NAMES])
    loss = un[0][0, 0]
    small_grads = dict(zip(SMALL_NAMES, un[1:]))
    for n in ("conv_w", "ffn_conv_w"):
        width = weights[n].shape[-1]
        small_grads[n] = lax.dynamic_slice_in_dim(small_grads[n], shard * width, width, axis=2)

    grads, delta, new_m, new_v = {}, {}, {}, {}
    for n in BIG_NAMES:
        shp = weights[n].shape
        g2 = big_grads[n]
        if n == "w_in":
            res = [jnp.transpose(t) for t in _adamw(tr(weights[n]), g2, tr(moms[n]), tr(vels[n]), name="adamw_" + n)]
        else:
            res = _adamw(weights[n][0], g2, moms[n][0], vels[n][0], name="adamw_" + n, g_transposed=n == "w_up")
        grads[n], delta[n], new_m[n], new_v[n] = (t.reshape(shp) for t in res)
    pick = lambda src: [src[n] for n in SMALL_NAMES]
    small_out = _adamw_many(pick(weights), pick(small_grads), pick(moms), pick(vels), name="adamw_small")
    ns = len(SMALL_NAMES)
    for tgt, part in ((delta, small_out[:ns]), (new_m, small_out[ns:2 * ns]), (new_v, small_out[2 * ns:])):
        tgt.update(zip(SMALL_NAMES, part))
    grads.update(small_grads)

    return (loss, grad_x, *[grads[n] for n in WEIGHT_ORDER], *[delta[n] for n in WEIGHT_ORDER],
            *[new_m[n] for n in WEIGHT_ORDER], *[new_v[n] for n in WEIGHT_ORDER])
```

```python
import math

import numpy as np
import jax
import jax.numpy as jnp
from jax import lax
from jax.experimental import pallas as pl
from jax.experimental.pallas import tpu as pltpu

F32 = jnp.float32
BF16 = jnp.bfloat16
MESH = pl.DeviceIdType.MESH

HEAD_DIM = 64
LANES = 128
ATTN_BLOCK = 128
DILATED_CONFIGS = ((128, 1), (512, 4), (2048, 16))
CONV_KERNEL = 31
FFN_CONV_KERNEL = 3
REL_BUCKETS = 32
REL_MAX_DIST = 2048
DEPTH = 1
ALPHA = (2 * DEPTH) ** 0.25
LN_EPS = 1e-5
NEG_INF = -1e30
QK_SCALE = 1.0 / math.sqrt(HEAD_DIM)
ADAM_LR = 0.001
ADAM_B1 = 0.9
ADAM_B2 = 0.999
ADAM_EPS = 1e-08
ADAM_WD = 0.01
ADAM_STEP = 10
VMEM_LIMIT = 52 * 1024 * 1024
FFN_COLS = 128
N_SHARDS = 4
N_DEV = 8


def _params(n_axes):
    return pltpu.CompilerParams(dimension_semantics=("arbitrary",) * n_axes,
                                vmem_limit_bytes=VMEM_LIMIT)


MM_DIMS = {"nn": (((1,), (0,)), ((), ())), "nt": (((1,), (1,)), ((), ())), "tn": (((0,), (0,)), ((), ()))}


class _Background:
    def __init__(self, in_arrays, out_shapes, aliases, n_sems, run, n_local=1):
        self.in_arrays, self.out_shapes, self.aliases = list(in_arrays), list(out_shapes), dict(aliases)
        self.n_sems, self.n_local, self.run = n_sems, n_local, run

    def scratch(self):
        return [pltpu.SemaphoreType.DMA((self.n_sems,)), pltpu.SemaphoreType.DMA((self.n_sems,)),
                pltpu.SemaphoreType.DMA((self.n_local,))]


def _hosted_call(body, bg, *, grid, in_specs, out_specs, out_shape, scratch_shapes, operands, name):
    n_in, n_out, n_scr = len(in_specs), len(out_specs), len(scratch_shapes)
    if bg is None:
        return pl.pallas_call(lambda *refs: body(refs, lambda post: None), grid=grid, in_specs=in_specs,
                              out_specs=out_specs, out_shape=out_shape, scratch_shapes=scratch_shapes,
                              compiler_params=_params(len(grid)), name=name)(*operands)
    nb_in, nb_out = len(bg.in_arrays), len(bg.out_shapes)
    n_steps = int(np.prod(grid))

    def full_body(*refs):
        own = refs[:n_in] + refs[n_in + nb_in:n_in + nb_in + n_out] \
            + refs[n_in + nb_in + n_out + nb_out:n_in + nb_in + n_out + nb_out + n_scr]
        bg_in = refs[n_in:n_in + nb_in]
        bg_out = refs[n_in + nb_in + n_out:n_in + nb_in + n_out + nb_out]
        sems = refs[n_in + nb_in + n_out + nb_out + n_scr:]
        step = pl.program_id(0)
        for ax in range(1, len(grid)):
            step = step * grid[ax] + pl.program_id(ax)

        def hook(post):
            bg.run(step, n_steps, bg_in, bg_out, *sems, post)

        body(own, hook)

    res = pl.pallas_call(
        full_body, grid=grid, in_specs=list(in_specs) + [HBM_SPEC] * nb_in,
        out_specs=list(out_specs) + [HBM_SPEC] * nb_out, out_shape=list(out_shape) + bg.out_shapes,
        input_output_aliases={n_in + a: n_out + o for a, o in bg.aliases.items()},
        scratch_shapes=list(scratch_shapes) + bg.scratch(), compiler_params=_params(len(grid)), name=name,
    )(*operands, *bg.in_arrays)
    return res


def _matmul_general(ins, part_fn, *, grid, tm, tn, outs, epilogue, extras=(), name, bg=None):
    nk = grid[2]
    n_in, n_extra = len(ins), len(extras)

    def body(refs, bg_hook):
        in_refs = refs[:n_in]
        rest = refs[n_in:]
        extra_refs = rest[:n_extra]
        out_refs = rest[n_extra:n_extra + len(outs)]
        acc_ref = rest[-1]
        i, j, k = pl.program_id(0), pl.program_id(1), pl.program_id(2)
        bg_hook(False)
        part = part_fn(in_refs, i, j, k)
        if nk == 1:
            epilogue(part, i, j, extra_refs, out_refs)
        else:
            @pl.when(k == 0)
            def _():
                acc_ref[...] = part

            @pl.when(k > 0)
            def _():
                acc_ref[...] += part

            @pl.when(k == nk - 1)
            def _():
                epilogue(acc_ref[...], i, j, extra_refs, out_refs)
        bg_hook(True)

    in_specs = [pl.BlockSpec(bs, im) for (_, bs, im) in list(ins) + list(extras)]
    out_specs = [pl.BlockSpec(bs, im) for (_, _, bs, im) in outs]
    out_shape = [jax.ShapeDtypeStruct(s, d) for (s, d, _, _) in outs]
    return _hosted_call(body, bg, grid=grid, in_specs=in_specs, out_specs=out_specs, out_shape=out_shape,
                        scratch_shapes=[pltpu.VMEM((tm, tn), F32)],
                        operands=[e[0] for e in ins] + [e[0] for e in extras], name=name)


def _dot(a, b, mode):
    return lax.dot_general(a.astype(BF16), b.astype(BF16), MM_DIMS[mode], preferred_element_type=F32)


def _matmul(a, b, *, mode, tm, tn, tk, outs, epilogue, extras=(), name, bg=None):
    if mode == "tn":
        K, M = a.shape
        N = b.shape[1]
        ins = [(a, (tk, tm), lambda i, j, k: (k, i)), (b, (tk, tn), lambda i, j, k: (k, j))]
    elif mode == "nt":
        M, K = a.shape
        N = b.shape[0]
        ins = [(a, (tm, tk), lambda i, j, k: (i, k)), (b, (tn, tk), lambda i, j, k: (j, k))]
    else:
        M, K = a.shape
        N = b.shape[1]
        ins = [(a, (tm, tk), lambda i, j, k: (i, k)), (b, (tk, tn), lambda i, j, k: (k, j))]
    assert M % tm == 0 and N % tn == 0 and K % tk == 0, (name, M, N, K, tm, tn, tk)

    def part_fn(in_refs, i, j, k):
        return _dot(in_refs[0][...], in_refs[1][...], mode)

    return _matmul_general(ins, part_fn, grid=(M // tm, N // tn, K // tk), tm=tm, tn=tn, outs=outs,
                           epilogue=epilogue, extras=extras, name=name, bg=bg)


def _plain_out(M, N, tm, tn, dtype):
    return ((M, N), dtype, (tm, tn), lambda i, j, k: (i, j))


def _mm_plain(a, b, *, mode, tm, tn, tk, out_dtype, name, bias=None, bg=None):
    if mode == "tn":
        M, N = a.shape[1], b.shape[1]
    elif mode == "nt":
        M, N = a.shape[0], b.shape[0]
    else:
        M, N = a.shape[0], b.shape[1]
    extras = []
    if bias is not None:
        extras.append((bias, (1, tn), lambda i, j, k: (0, j)))

    def epilogue(acc, i, j, extra_refs, out_refs):
        if bias is not None:
            acc = acc + extra_refs[0][...]
        out_refs[0][...] = acc.astype(out_dtype)

    res = _matmul(a, b, mode=mode, tm=tm, tn=tn, tk=tk, outs=[_plain_out(M, N, tm, tn, out_dtype)],
                  epilogue=epilogue, extras=extras, name=name, bg=bg)
    return res[0] if bg is None else res


def _row_tile(T, want):
    t = min(T, want)
    while T % t:
        t //= 2
    return t


def _col_tile(N, want):
    if N <= want:
        return N
    best = None
    for c in range(LANES, want + 1, LANES):
        if N % c == 0:
            best = c
    return best if best is not None else N


def _accumulate(ref, first, val):
    @pl.when(first)
    def _():
        ref[...] = val

    @pl.when(jnp.logical_not(first))
    def _():
        ref[...] += val


def _ln_fwd(z, g, b):
    mu = jnp.mean(z, axis=-1, keepdims=True)
    zc = z - mu
    var = jnp.mean(zc * zc, axis=-1, keepdims=True)
    r = lax.rsqrt(var + LN_EPS)
    xh = zc * r
    return xh * g + b, xh, r


def _ln_bwd(dy, xh, r, g):
    dxh = dy * g
    m1 = jnp.mean(dxh, axis=-1, keepdims=True)
    m2 = jnp.mean(dxh * xh, axis=-1, keepdims=True)
    return r * (dxh - m1 - xh * m2)


def _sigmoid(x):
    return 0.5 * jnp.tanh(0.5 * x) + 0.5


def _bucket_tables():
    exact = REL_BUCKETS // 2
    qi = np.arange(ATTN_BLOCK)[:, None]
    kj = np.arange(2 * ATTN_BLOCK)[None, :]
    steps = qi + ATTN_BLOCK - kj
    buckets, masks = [], []
    for window, dilation in DILATED_CONFIGS:
        max_steps = window // dilation
        band = (steps >= 0) & (steps <= max_steps)
        dist = np.maximum(steps, 0) * dilation
        d_f = np.maximum(dist, 1).astype(np.float32)
        large = exact + (np.log(d_f / np.float32(exact)) / np.float32(math.log(REL_MAX_DIST / exact))
                         * np.float32(REL_BUCKETS - exact)).astype(np.int32)
        large = np.minimum(large, REL_BUCKETS - 1)
        bucket = np.where(dist < exact, dist, large).astype(np.int32)
        buckets.append(bucket.reshape(1, -1))
        masks.append(np.where(band, 0.0, NEG_INF).astype(np.float32).reshape(1, -1))
    return np.stack(buckets), np.stack(masks)


def _split_hi_lo(x):
    hi = x.astype(BF16)
    lo = (x - hi.astype(F32)).astype(BF16)
    return hi, lo


def _bias_build(rel_table_t, bucket, mask):
    H = rel_table_t.shape[0]
    n = bucket.shape[-1]

    def body(t_ref, bkt_ref, mask_ref, o_ref):
        onehot = (lax.broadcasted_iota(jnp.int32, (REL_BUCKETS, n), 0) == bkt_ref[0]).astype(BF16)
        t = t_ref[...]
        t1 = t.astype(BF16)
        r1 = t - t1.astype(F32)
        t2 = r1.astype(BF16)
        t3 = (r1 - t2.astype(F32)).astype(BF16)
        acc = jnp.dot(t1, onehot, preferred_element_type=F32)
        acc = acc + jnp.dot(t2, onehot, preferred_element_type=F32)
        acc = acc + jnp.dot(t3, onehot, preferred_element_type=F32)
        o_ref[0] = acc + mask_ref[0]

    return pl.pallas_call(
        body, grid=(3,),
        in_specs=[pl.BlockSpec((H, REL_BUCKETS), lambda b: (0, 0)),
                  pl.BlockSpec((1, 1, n), lambda b: (b, 0, 0)),
                  pl.BlockSpec((1, 1, n), lambda b: (b, 0, 0))],
        out_specs=pl.BlockSpec((1, H, n), lambda b: (b, 0, 0)),
        out_shape=jax.ShapeDtypeStruct((3, H, n), F32),
        compiler_params=_params(1), name="bias_build",
    )(rel_table_t, bucket, mask)


def _rel_grad(dbias, bucket):
    H = dbias.shape[1]
    n = bucket.shape[-1]
    dims = (((1,), (1,)), ((), ()))

    def body(d_ref, bkt_ref, o_ref):
        b = pl.program_id(0)
        onehot = (lax.broadcasted_iota(jnp.int32, (REL_BUCKETS, n), 0) == bkt_ref[0]).astype(BF16)
        d = d_ref[0]
        d1 = d.astype(BF16)
        r1 = d - d1.astype(F32)
        d2 = r1.astype(BF16)
        d3 = (r1 - d2.astype(F32)).astype(BF16)
        acc = lax.dot_general(d1, onehot, dims, preferred_element_type=F32)
        acc = acc + lax.dot_general(d2, onehot, dims, preferred_element_type=F32)
        acc = acc + lax.dot_general(d3, onehot, dims, preferred_element_type=F32)
        _accumulate(o_ref, b == 0, acc)

    return pl.pallas_call(
        body, grid=(3,),
        in_specs=[pl.BlockSpec((1, H, n), lambda b: (b, 0, 0)),
                  pl.BlockSpec((1, 1, n), lambda b: (b, 0, 0))],
        out_specs=pl.BlockSpec((H, REL_BUCKETS), lambda b: (0, 0)),
        out_shape=jax.ShapeDtypeStruct((H, REL_BUCKETS), F32),
        compiler_params=_params(1), name="rel_grad",
    )(dbias, bucket)


def _regroup(src, stage, dst, d, S, off=0):
    if d == 1:
        dst[off:off + S, :] = src.astype(dst.dtype)
        return
    stage[...] = src.astype(F32)
    L = S // d
    for r in range(d):
        dst[off + r * L:off + (r + 1) * L, :] = stage[pl.ds(r, L, stride=d), :].astype(dst.dtype)


def _ungroup(sub_ref, off, nat_ref, d, S, add):
    L = S // d
    for r in range(d):
        rows = pl.ds(0, S) if d == 1 else pl.ds(r, L, stride=d)
        val = sub_ref[off + r * L:off + (r + 1) * L, :]
        if add:
            nat_ref[rows, :] += val
        else:
            nat_ref[rows, :] = val


def _branch_keys(ks, vs, S, nb, g_idx):
    blk3 = (S // ATTN_BLOCK, ATTN_BLOCK, LANES)
    kc3 = ks[ATTN_BLOCK:ATTN_BLOCK + S, :].reshape(blk3)
    vc3 = vs[ATTN_BLOCK:ATTN_BLOCK + S, :].reshape(blk3)
    if nb == 1:
        return kc3, vc3, None
    kk3 = jnp.concatenate([ks[0:S, :].reshape(blk3), kc3], axis=1)
    vv3 = jnp.concatenate([vs[0:S, :].reshape(blk3), vc3], axis=1)
    col = lax.broadcasted_iota(jnp.int32, (1, 1, 2 * ATTN_BLOCK), 2)
    dead = jnp.logical_and((g_idx & (nb - 1)) == 0, col < ATTN_BLOCK)
    return kk3, vv3, dead


def _branch_scores(qe, kk3, b_ref, bi, e, dead):
    s = jnp.einsum("gqe,gke->gqk", qe, kk3, preferred_element_type=F32)
    if dead is None:
        return s + b_ref[bi, e, :, ATTN_BLOCK:]
    return jnp.where(dead, NEG_INF, s + b_ref[bi, e])


def _attention_fwd(qkv, bias_all, B, S, AW, bg=None):
    HP = AW // LANES
    G = S // ATTN_BLOCK
    blk3 = (G, ATTN_BLOCK, LANES)

    def body(refs, bg_hook):
        q_ref, k_ref, v_ref, b_ref, o_ref, lse_ref, stage, qs, ks, vs, ot, lt, on0, on1, on2, ln0, ln1, ln2 = refs
        bg_hook(False)
        head0 = lax.broadcasted_iota(jnp.int32, (1, 1, LANES), 2) < HEAD_DIM
        g_idx = lax.broadcasted_iota(jnp.int32, (G, 1, 1), 0)
        ks[0:ATTN_BLOCK, :] = jnp.zeros((ATTN_BLOCK, LANES), BF16)
        vs[0:ATTN_BLOCK, :] = jnp.zeros((ATTN_BLOCK, LANES), BF16)
        nat_o, nat_l = (on0, on1, on2), (ln0, ln1, ln2)
        for bi, (_, d) in enumerate(DILATED_CONFIGS):
            nb = S // d // ATTN_BLOCK
            _regroup(q_ref[0], stage, qs, d, S)
            _regroup(k_ref[0], stage, ks, d, S, ATTN_BLOCK)
            _regroup(v_ref[0], stage, vs, d, S, ATTN_BLOCK)
            q3 = qs[...].reshape(blk3) * QK_SCALE
            kk3, vv3, dead = _branch_keys(ks, vs, S, nb, g_idx)
            outs, lses = [], []
            for e in range(2):
                msk = head0 if e == 0 else jnp.logical_not(head0)
                qe = jnp.where(msk, q3, jnp.zeros_like(q3))
                s = _branch_scores(qe, kk3, b_ref, bi, e, dead)
                m = jnp.max(s, axis=-1, keepdims=True)
                p = jnp.exp(s - m)
                l = jnp.sum(p, axis=-1, keepdims=True)
                o = jnp.einsum("gqk,gke->gqe", p.astype(BF16), vv3, preferred_element_type=F32)
                outs.append(o / l)
                lses.append(jnp.broadcast_to(m + jnp.log(l), blk3))
            ot[...] = jnp.where(head0, outs[0], outs[1]).reshape(S, LANES)
            lt[...] = jnp.where(head0, lses[0], lses[1]).reshape(S, LANES)
            _ungroup(ot, 0, nat_o[bi], d, S, add=False)
            _ungroup(lt, 0, nat_l[bi], d, S, add=False)

        la, lb, lc = ln0[...], ln1[...], ln2[...]
        m = jnp.maximum(jnp.maximum(la, lb), lc)
        ea, eb, ec = jnp.exp(la - m), jnp.exp(lb - m), jnp.exp(lc - m)
        den = ea + eb + ec
        lse_ref[0] = m + jnp.log(den)
        o_ref[0] = (ea * on0[...] + eb * on1[...] + ec * on2[...]) / den
        bg_hook(True)

    blk = lambda off: pl.BlockSpec((1, S, LANES), lambda b, h: (b, 0, off + h))
    qv = qkv.reshape(B, S, 3 * AW)
    sub_f = pltpu.VMEM((S, LANES), F32)
    pad_b = pltpu.VMEM((S + ATTN_BLOCK, LANES), BF16)
    res = _hosted_call(
        body, bg, grid=(B, HP),
        in_specs=[blk(0), blk(HP), blk(2 * HP),
                  pl.BlockSpec((3, 2, ATTN_BLOCK, 2 * ATTN_BLOCK), lambda b, h: (0, h, 0, 0))],
        out_specs=[blk(0), blk(0)],
        out_shape=[jax.ShapeDtypeStruct((B, S, AW), F32)] * 2,
        scratch_shapes=[sub_f, pltpu.VMEM((S, LANES), BF16), pad_b, pad_b] + [sub_f] * 8,
        operands=[qv, qv, qv, bias_all], name="attention_fwd")
    return (res[0].reshape(B * S, AW), res[1].reshape(B * S, AW)) + tuple(res[2:])


def _attention_bwd(qkv, do, lse, dd, bias_all, B, S, AW, bg=None):
    HP = AW // LANES
    H = AW // HEAD_DIM
    G = S // ATTN_BLOCK
    blk3 = (G, ATTN_BLOCK, LANES)
    PAD = ATTN_BLOCK

    def body(refs, bg_hook):
        (q_ref, k_ref, v_ref, do_ref, lse_ref, dd_ref, b_ref,
         dq_ref, dk_ref, dv_ref, csq_ref, csk_ref, csv_ref, db_ref,
         stage, qs, ks, vs, gs, ls, ds_, tq, tk, tv, accq, acck, accv) = refs
        bg_hook(False)
        head0 = lax.broadcasted_iota(jnp.int32, (1, 1, LANES), 2) < HEAD_DIM
        g_idx = lax.broadcasted_iota(jnp.int32, (G, 1, 1), 0)
        first_b = pl.program_id(1) == 0

        @pl.when(first_b)
        def _():
            db_ref[...] = jnp.zeros_like(db_ref)

        ks[0:PAD, :] = jnp.zeros((PAD, LANES), BF16)
        vs[0:PAD, :] = jnp.zeros((PAD, LANES), BF16)
        tk[0:PAD, :] = jnp.zeros((PAD, LANES), F32)
        tv[0:PAD, :] = jnp.zeros((PAD, LANES), F32)
        for bi, (_, d) in enumerate(DILATED_CONFIGS):
            nb = S // d // ATTN_BLOCK
            _regroup(q_ref[0], stage, qs, d, S)
            _regroup(k_ref[0], stage, ks, d, S, PAD)
            _regroup(v_ref[0], stage, vs, d, S, PAD)
            _regroup(do_ref[0], stage, gs, d, S)
            _regroup(lse_ref[0], stage, ls, d, S)
            _regroup(dd_ref[0], stage, ds_, d, S)
            q3 = qs[...].reshape(blk3) * QK_SCALE
            do3 = gs[...].reshape(blk3)
            lse3 = ls[...].reshape(blk3)
            dd3 = ds_[...].reshape(blk3)
            kk3, vv3, dead = _branch_keys(ks, vs, S, nb, g_idx)
            dq = jnp.zeros(blk3, F32)
            dkk = jnp.zeros(kk3.shape, F32)
            dvv = jnp.zeros(kk3.shape, F32)
            for e in range(2):
                msk = head0 if e == 0 else jnp.logical_not(head0)
                c0 = e * HEAD_DIM
                qe = jnp.where(msk, q3, jnp.zeros_like(q3))
                doe = jnp.where(msk, do3, jnp.zeros_like(do3))
                ke = jnp.where(msk, kk3 * QK_SCALE, jnp.zeros_like(kk3))
                s = _branch_scores(qe, kk3, b_ref, bi, e, dead)
                p = jnp.exp(s - lse3[:, :, c0:c0 + 1])
                dp = jnp.einsum("gqe,gke->gqk", doe, vv3, preferred_element_type=F32)
                dsc = p * (dp - dd3[:, :, c0:c0 + 1])
                if dead is None:
                    db_ref[bi, e, :, ATTN_BLOCK:] += jnp.sum(dsc, axis=0)
                else:
                    db_ref[bi, e] += jnp.sum(dsc, axis=0)
                dsb = dsc.astype(BF16)
                dq = dq + jnp.einsum("gqk,gke->gqe", dsb, ke, preferred_element_type=F32)
                dkk = dkk + jnp.einsum("gqk,gqe->gke", dsb, qe, preferred_element_type=F32)
                dvv = dvv + jnp.einsum("gqk,gqe->gke", p.astype(BF16), doe, preferred_element_type=F32)
            tq[...] = dq.reshape(S, LANES)
            if dead is None:
                tk[PAD:PAD + S, :] = dkk.reshape(S, LANES)
                tv[PAD:PAD + S, :] = dvv.reshape(S, LANES)
            else:
                tk[PAD:PAD + S, :] = dkk[:, ATTN_BLOCK:, :].reshape(S, LANES)
                tv[PAD:PAD + S, :] = dvv[:, ATTN_BLOCK:, :].reshape(S, LANES)
                tk[0:S, :] += dkk[:, :ATTN_BLOCK, :].reshape(S, LANES)
                tv[0:S, :] += dvv[:, :ATTN_BLOCK, :].reshape(S, LANES)
            _ungroup(tq, 0, accq, d, S, add=bi > 0)
            _ungroup(tk, PAD, acck, d, S, add=bi > 0)
            _ungroup(tv, PAD, accv, d, S, add=bi > 0)

        for acc, out_ref, cs_ref in ((accq, dq_ref, csq_ref), (acck, dk_ref, csk_ref), (accv, dv_ref, csv_ref)):
            tot = acc[...]
            out_ref[0] = tot.astype(out_ref.dtype)
            _accumulate(cs_ref, first_b, jnp.sum(tot, axis=0, keepdims=True))
        bg_hook(True)

    blk = lambda off: pl.BlockSpec((1, S, LANES), lambda h, b: (b, 0, off + h))
    cs_spec = pl.BlockSpec((1, LANES), lambda h, b: (0, h))
    bias_spec = pl.BlockSpec((3, 2, ATTN_BLOCK, 2 * ATTN_BLOCK), lambda h, b: (0, h, 0, 0))
    qv = qkv.reshape(B, S, 3 * AW)
    view = lambda t: t.reshape(B, S, AW)
    sub_b = pltpu.VMEM((S, LANES), BF16)
    sub_f = pltpu.VMEM((S, LANES), F32)
    pad_b = pltpu.VMEM((S + PAD, LANES), BF16)
    pad_f = pltpu.VMEM((S + PAD, LANES), F32)
    res = _hosted_call(
        body, bg, grid=(HP, B),
        in_specs=[blk(0), blk(HP), blk(2 * HP), blk(0), blk(0), blk(0), bias_spec],
        out_specs=[blk(0), blk(0), blk(0), cs_spec, cs_spec, cs_spec, bias_spec],
        out_shape=[jax.ShapeDtypeStruct((B, S, AW), BF16)] * 3 + [jax.ShapeDtypeStruct((1, AW), F32)] * 3
        + [jax.ShapeDtypeStruct((3, H, ATTN_BLOCK, 2 * ATTN_BLOCK), F32)],
        scratch_shapes=[sub_f, sub_b, pad_b, pad_b, sub_b, sub_f, sub_f, sub_f, pad_f, pad_f, sub_f, sub_f, sub_f],
        operands=[qv, qv, qv, view(do), view(lse), view(dd), bias_all], name="attention_bwd")
    flat = lambda t: t.reshape(B * S, AW)
    return (flat(res[0]), flat(res[1]), flat(res[2]), res[3], res[4], res[5], res[6]) + tuple(res[7:])


class _RowShifts:
    def __init__(self, x, row, up):
        self.x, self.row, self.up, self.base = x, row, up, {0: x}

    def __call__(self, s):
        x = self.x
        n, c = x.shape
        r, whole = s % 8, s - s % 8
        if r not in self.base:
            if self.up:
                rolled = pltpu.roll(x, n - r, 0)
                tail = jnp.where(self.row[n - 8:] < n - r, rolled[n - 8:], 0.0)
                self.base[r] = jnp.concatenate([rolled[:n - 8], tail], axis=0)
            else:
                rolled = pltpu.roll(x, r, 0)
                head = jnp.where(self.row[:8] >= r, rolled[:8], 0.0)
                self.base[r] = jnp.concatenate([head, rolled[8:]], axis=0)
        y = self.base[r]
        if whole == 0:
            return y
        pad = jnp.zeros((whole, c), x.dtype)
        if self.up:
            return jnp.concatenate([y[whole:], pad], axis=0)
        return jnp.concatenate([pad, y[:n - whole]], axis=0)


def _conv_branch_fwd_math(a, g, w_ref, cb, lg, lb, row):
    sg = _sigmoid(g)
    u0 = a * sg
    u0_down = _RowShifts(u0, row, up=False)
    uc = jnp.zeros_like(u0) + cb
    for k in range(CONV_KERNEL):
        uc = uc + w_ref[k:k + 1, :] * u0_down(CONV_KERNEL - 1 - k)
    ul, xh, r = _ln_fwd(uc, lg, lb)
    su = _sigmoid(ul)
    u = ul * su
    return sg, u0_down, ul, xh, r, su, u


def _conv_fwd(ag, conv_w, conv_b, ln_g, ln_b, norm_g, B, S, CW):
    def body(a_ref, g_ref, w_ref, cb_ref, lg_ref, lb_ref, ng_ref, o_ref):
        row = lax.broadcasted_iota(jnp.int32, (S, CW), 0)
        _, _, _, _, _, _, u = _conv_branch_fwd_math(a_ref[0], g_ref[0], w_ref, cb_ref[...], lg_ref[...],
                                                    lb_ref[...], row)
        rr = lax.rsqrt(jnp.mean(u * u, axis=-1, keepdims=True) + LN_EPS)
        o_ref[0] = (u * rr * ng_ref[...]).astype(BF16)

    vec = pl.BlockSpec((1, CW), lambda b: (0, 0))
    out = pl.pallas_call(
        body, grid=(B,),
        in_specs=[pl.BlockSpec((1, S, CW), lambda b: (b, 0, 0)), pl.BlockSpec((1, S, CW), lambda b: (b, 0, 1)),
                  pl.BlockSpec((CONV_KERNEL, CW), lambda b: (0, 0)), vec, vec, vec, vec],
        out_specs=pl.BlockSpec((1, S, CW), lambda b: (b, 0, 0)),
        out_shape=jax.ShapeDtypeStruct((B, S, CW), BF16),
        compiler_params=_params(1), name="conv_fwd",
    )(ag.reshape(B, S, 2 * CW), ag.reshape(B, S, 2 * CW), conv_w, conv_b, ln_g, ln_b, norm_g)
    return out.reshape(B * S, CW)


def _conv_bwd(ag, dmc, conv_w, conv_b, ln_g, ln_b, norm_g, B, S, CW):
    def body(a_ref, g_ref, dm_ref, w_ref, cb_ref, lg_ref, lb_ref, ng_ref,
             dag_ref, dw_ref, dcb_ref, dlg_ref, dlb_ref, dng_ref):
        b = pl.program_id(0)
        row = lax.broadcasted_iota(jnp.int32, (S, CW), 0)
        a, g = a_ref[0], g_ref[0]
        sg, u0_down, ul, xh, r, su, u = _conv_branch_fwd_math(a, g, w_ref, cb_ref[...], lg_ref[...], lb_ref[...], row)
        rr = lax.rsqrt(jnp.mean(u * u, axis=-1, keepdims=True) + LN_EPS)
        dm = dm_ref[0]
        dxn = dm * ng_ref[...]
        du = rr * (dxn - u * (rr * rr) * jnp.mean(dxn * u, axis=-1, keepdims=True))
        dul = du * su * (1.0 + ul * (1.0 - su))
        duc = _ln_bwd(dul, xh, r, lg_ref[...])
        first = b == 0
        _accumulate(dng_ref, first, jnp.sum(dm * u * rr, axis=0, keepdims=True))
        _accumulate(dlg_ref, first, jnp.sum(dul * xh, axis=0, keepdims=True))
        _accumulate(dlb_ref, first, jnp.sum(dul, axis=0, keepdims=True))
        _accumulate(dcb_ref, first, jnp.sum(duc, axis=0, keepdims=True))

        @pl.when(first)
        def _():
            dw_ref[...] = jnp.zeros_like(dw_ref)

        duc_up = _RowShifts(duc, row, up=True)
        du0 = jnp.zeros_like(duc)
        for k in range(CONV_KERNEL):
            sh = CONV_KERNEL - 1 - k
            dw_ref[k:k + 1, :] += jnp.sum(duc * u0_down(sh), axis=0, keepdims=True)
            du0 = du0 + w_ref[k:k + 1, :] * duc_up(sh)
        dag_ref[0, :, :CW] = du0 * sg
        dag_ref[0, :, CW:] = du0 * a * sg * (1.0 - sg)

    vec = pl.BlockSpec((1, CW), lambda b: (0, 0))
    wspec = pl.BlockSpec((CONV_KERNEL, CW), lambda b: (0, 0))
    agv = ag.reshape(B, S, 2 * CW)
    res = pl.pallas_call(
        body, grid=(B,),
        in_specs=[pl.BlockSpec((1, S, CW), lambda b: (b, 0, 0)), pl.BlockSpec((1, S, CW), lambda b: (b, 0, 1)),
                  pl.BlockSpec((1, S, CW), lambda b: (b, 0, 0)), wspec, vec, vec, vec, vec],
        out_specs=[pl.BlockSpec((1, S, 2 * CW), lambda b: (b, 0, 0)), wspec, vec, vec, vec, vec],
        out_shape=[jax.ShapeDtypeStruct((B, S, 2 * CW), F32), jax.ShapeDtypeStruct((CONV_KERNEL, CW), F32)]
        + [jax.ShapeDtypeStruct((1, CW), F32)] * 4,
        compiler_params=_params(1), name="conv_bwd",
    )(agv, agv, dmc.reshape(B, S, CW), conv_w, conv_b, ln_g, ln_b, norm_g)
    return (res[0].reshape(B * S, 2 * CW),) + tuple(res[1:])


def _ffn_conv(x, w_ref, bias, row):
    down = x if isinstance(x, _RowShifts) else _RowShifts(x, row, up=False)
    y = jnp.zeros_like(down.x) + bias
    for k in range(FFN_CONV_KERNEL):
        y = y + w_ref[k:k + 1, :] * down(FFN_CONV_KERNEL - 1 - k)
    return y


def _ffn_specs(S, tc, nj, order):
    pick = (lambda b, j: (b, j)) if order == "bj" else (lambda j, b: (b, j))
    act = lambda off: pl.BlockSpec((1, S, tc), lambda *g: (pick(*g)[0], 0, off + pick(*g)[1]))
    cw = lambda off: pl.BlockSpec((FFN_CONV_KERNEL, tc), lambda *g: (0, off + pick(*g)[1]))
    cb = lambda off: pl.BlockSpec((1, tc), lambda *g: (0, off + pick(*g)[1]))
    return act, cw, cb


FFN_HALO = 16


def _half_sequences(S):
    if S < 8 * FFN_HALO:
        return [(0, S, 0, S)]
    h = S // 2
    return [(0, h + FFN_HALO, 0, h), (h - FFN_HALO, S, FFN_HALO, h)]


def _w_up_block_spec(w_up_sh, tc, off):
    _, D, cs = w_up_sh.shape
    assert cs % tc == 0
    bps = cs // tc
    return pl.BlockSpec((1, D, tc), lambda j: ((off + j) // bps, 0, (off + j) % bps))


def _ffn_fwd_fused(x1b, w_up_sh, cw, cb, B, S, DFF):
    tc = FFN_COLS
    nj = DFF // tc
    D = x1b.shape[1]

    def body(x_ref, wg_ref, wv_ref, cwg_ref, cwv_ref, cbg_ref, cbv_ref, o_ref, up_ref):
        w = jnp.concatenate([wg_ref[0], wv_ref[0]], axis=1)
        for b in range(B):
            for lo, hi, o0, on in _half_sequences(S):
                row = lax.broadcasted_iota(jnp.int32, (hi - lo, tc), 0)
                up = jnp.dot(x_ref[b, lo:hi, :], w, preferred_element_type=F32)
                up_ref[b, lo + o0:lo + o0 + on, :] = up[o0:o0 + on]
                gate = _ffn_conv(up[:, :tc], cwg_ref, cbg_ref[...], row)
                val = _ffn_conv(up[:, tc:], cwv_ref, cbv_ref[...], row)
                o_ref[b, lo + o0:lo + o0 + on, :] = (gate * _sigmoid(gate) * val).astype(BF16)[o0:o0 + on]

    cws = lambda off: pl.BlockSpec((FFN_CONV_KERNEL, tc), lambda j: (0, off + j))
    cbs = lambda off: pl.BlockSpec((1, tc), lambda j: (0, off + j))
    act, upre = pl.pallas_call(
        body, grid=(nj,),
        in_specs=[pl.BlockSpec((B, S, D), lambda j: (0, 0, 0), pipeline_mode=pl.Buffered(1)),
                  _w_up_block_spec(w_up_sh, tc, 0), _w_up_block_spec(w_up_sh, tc, nj),
                  cws(0), cws(nj), cbs(0), cbs(nj)],
        out_specs=[pl.BlockSpec((B, S, tc), lambda j: (0, 0, j)), pl.BlockSpec((B, S, 2 * tc), lambda j: (0, 0, j))],
        out_shape=[jax.ShapeDtypeStruct((B, S, DFF), BF16), jax.ShapeDtypeStruct((B, S, 2 * DFF), F32)],
        compiler_params=_params(1), name="ffn_fwd",
    )(x1b.reshape(B, S, D), w_up_sh, w_up_sh, cw, cw, cb, cb)
    return act.reshape(B * S, DFF), upre


def _ffn_bwd_fused(x1b, dz2b, upre, w_down, cw, cb, B, S, DFF):
    tc = FFN_COLS
    nj = DFF // tc
    D = x1b.shape[1]

    def body(x_ref, dz_ref, up_ref, wd_ref, cwg_ref, cwv_ref, cbg_ref, cbv_ref,
             dug_ref, duv_ref, dwu_ref, dwd_ref, dcw_ref, dcb_ref):
        first = pl.program_id(1) == 0
        dw_t = dwd = None
        dcb = [None, None]
        dcw = [[None] * FFN_CONV_KERNEL, [None] * FFN_CONV_KERNEL]
        add = lambda old, new: new if old is None else old + new
        for lo, hi, o0, on in _half_sequences(S):
            n = hi - lo
            own = slice(o0, o0 + on)
            row = lax.broadcasted_iota(jnp.int32, (n, tc), 0)
            x = x_ref[0, lo:hi, :]
            dz = dz_ref[0, lo:hi, :]
            ug = _RowShifts(up_ref[0, lo:hi, :tc], row, up=False)
            uv = _RowShifts(up_ref[0, lo:hi, tc:], row, up=False)
            gate = _ffn_conv(ug, cwg_ref, cbg_ref[...], row)
            val = _ffn_conv(uv, cwv_ref, cbv_ref[...], row)
            sg = _sigmoid(gate)
            act = (gate * sg * val).astype(BF16)
            dact = _dot(dz, wd_ref[...], "nt")
            dgate = dact * val * sg * (1.0 + gate * (1.0 - sg))
            dval = dact * gate * sg
            dupre = []
            for h, (dup, u_down, w_ref) in enumerate(((dgate, ug, cwg_ref), (dval, uv, cwv_ref))):
                dcb[h] = add(dcb[h], jnp.sum(dup[own], axis=0, keepdims=True))
                dup_up = _RowShifts(dup, row, up=True)
                acc = jnp.zeros_like(dup)
                for k in range(FFN_CONV_KERNEL):
                    sh = FFN_CONV_KERNEL - 1 - k
                    dcw[h][k] = add(dcw[h][k], jnp.sum((dup * u_down(sh))[own], axis=0, keepdims=True))
                    acc = acc + w_ref[k:k + 1, :] * dup_up(sh)
                dupre.append(acc.astype(BF16)[own])
            dug_ref[0, lo + o0:lo + o0 + on, :] = dupre[0]
            duv_ref[0, lo + o0:lo + o0 + on, :] = dupre[1]
            dw_t = add(dw_t, _dot(jnp.concatenate(dupre, axis=1), x[own], "tn"))
            dwd = add(dwd, _dot(act[own], dz[own], "tn"))
        _accumulate(dwu_ref.at[0], first, dw_t[:tc])
        _accumulate(dwu_ref.at[1], first, dw_t[tc:])
        _accumulate(dwd_ref, first, dwd)
        for h in range(2):
            _accumulate(dcb_ref.at[h], first, dcb[h])
            for k in range(FFN_CONV_KERNEL):
                _accumulate(dcw_ref.at[k, pl.ds(h, 1), :], first, dcw[h][k])

    act_s, cws, cbs = _ffn_specs(S, tc, nj, "jb")
    seq = pl.BlockSpec((1, S, D), lambda j, b: (b, 0, 0))
    res = pl.pallas_call(
        body, grid=(nj, B),
        in_specs=[seq, seq, pl.BlockSpec((1, S, 2 * tc), lambda j, b: (b, 0, j)),
                  pl.BlockSpec((tc, D), lambda j, b: (j, 0)), cws(0), cws(nj), cbs(0), cbs(nj)],
        out_specs=[act_s(0), act_s(0), pl.BlockSpec((2, tc, D), lambda j, b: (0, j, 0)),
                   pl.BlockSpec((tc, D), lambda j, b: (j, 0)),
                   pl.BlockSpec((FFN_CONV_KERNEL, 2, tc), lambda j, b: (0, 0, j)),
                   pl.BlockSpec((2, 1, tc), lambda j, b: (0, 0, j))],
        out_shape=[jax.ShapeDtypeStruct((B, S, DFF), BF16)] * 2
        + [jax.ShapeDtypeStruct((2, DFF, D), F32), jax.ShapeDtypeStruct((DFF, D), F32),
           jax.ShapeDtypeStruct((FFN_CONV_KERNEL, 2, DFF), F32), jax.ShapeDtypeStruct((2, 1, DFF), F32)],
        compiler_params=_params(2), name="ffn_bwd",
    )(x1b.reshape(B, S, D), dz2b.reshape(B, S, D), upre, w_down, cw, cw, cb, cb)
    flat = lambda t: t.reshape(B * S, DFF)
    return flat(res[0]), flat(res[1]), res[2], res[3], res[4], res[5]


def _dx1_ln1_bwd(dupre_g, dupre_v, w_up_sh, dz2, xh1, r1, ln1_g, tm, bg):
    T, D = dz2.shape
    NS, _, cs = w_up_sh.shape
    half = NS // 2
    DFF = dupre_g.shape[1]

    def body(refs, bg_hook):
        dug_ref, duv_ref, w_ref, dz2_ref, xh_ref, r_ref, g_ref, dz_ref, dzb_ref, dg_ref, db_ref = refs
        bg_hook(False)
        first = pl.program_id(0) == 0
        dg = db = None
        for rows in (slice(0, tm // 2), slice(tm // 2, tm)):
            dx1 = ALPHA * dz2_ref[rows, :]
            for k in range(NS):
                src = dug_ref if k < half else duv_ref
                c0 = (k % half) * cs
                dx1 = dx1 + _dot(src[rows, c0:c0 + cs], w_ref[k], "nt")
            xh = xh_ref[rows, :]
            dz = _ln_bwd(dx1, xh, r_ref[rows, 0:1], g_ref[...])
            dz_ref[rows, :] = dz
            dzb_ref[rows, :] = dz.astype(BF16)
            dg_h, db_h = jnp.sum(dx1 * xh, axis=0, keepdims=True), jnp.sum(dx1, axis=0, keepdims=True)
            dg, db = (dg_h, db_h) if dg is None else (dg + dg_h, db + db_h)
        _accumulate(dg_ref, first, dg)
        _accumulate(db_ref, first, db)
        bg_hook(True)

    row = pl.BlockSpec((tm, D), lambda i: (i, 0))
    vec = pl.BlockSpec((1, D), lambda i: (0, 0))
    du = pl.BlockSpec((tm, DFF), lambda i: (i, 0))
    return _hosted_call(
        body, bg, grid=(T // tm,),
        in_specs=[du, du, pl.BlockSpec((NS, D, cs), lambda i: (0, 0, 0), pipeline_mode=pl.Buffered(1)),
                  row, row, pl.BlockSpec((tm, LANES), lambda i: (i, 0)), vec],
        out_specs=[row, row, vec, vec],
        out_shape=[jax.ShapeDtypeStruct((T, D), F32), jax.ShapeDtypeStruct((T, D), BF16),
                   jax.ShapeDtypeStruct((1, D), F32), jax.ShapeDtypeStruct((1, D), F32)],
        scratch_shapes=[], operands=[dupre_g, dupre_v, w_up_sh, dz2, xh1, r1, ln1_g], name="mm_dx1_ln1_bwd")


def _dh_cat(dq, dk, dv, dag, tm):
    T, AW = dq.shape
    CW2 = dag.shape[1]
    W = 3 * AW + CW2

    def body(dq_ref, dk_ref, dv_ref, dag_ref, dh_ref, cs_ref):
        for c, ref in enumerate((dq_ref, dk_ref, dv_ref)):
            dh_ref[:, c * AW:(c + 1) * AW] = ref[...]
        dg = dag_ref[...]
        dh_ref[:, 3 * AW:] = dg.astype(BF16)
        _accumulate(cs_ref, pl.program_id(0) == 0, jnp.sum(dg, axis=0, keepdims=True))

    row = pl.BlockSpec((tm, AW), lambda i: (i, 0))
    return pl.pallas_call(
        body, grid=(T // tm,),
        in_specs=[row] * 3 + [pl.BlockSpec((tm, CW2), lambda i: (i, 0))],
        out_specs=[pl.BlockSpec((tm, W), lambda i: (i, 0)), pl.BlockSpec((1, CW2), lambda i: (0, 0))],
        out_shape=[jax.ShapeDtypeStruct((T, W), BF16), jax.ShapeDtypeStruct((1, CW2), F32)],
        compiler_params=_params(1), name="dh_cat",
    )(dq, dk, dv, dag)


def _local_step(x, target, rel_table, w_in_t, b_in, conv_w, conv_b, conv_ln_g, conv_ln_b, attn_norm_g,
                conv_norm_g, staged, ln1_g, ln1_b, ffn_cw, ffn_cb, ln2_g, ln2_b, ids):
    B, S, D = x.shape
    T = B * S
    AW = attn_norm_g.shape[-1]
    CW = conv_norm_g.shape[-1]
    H = AW // HEAD_DIM
    DFF = staged[2].shape[0] * staged[2].shape[1]
    INW = 3 * AW + 2 * CW
    xf = x.reshape(T, D)
    tf = target.reshape(T, D)
    tm = _row_tile(T, 512)
    tm_s = tm

    bucket_np, mask_np = _bucket_tables()
    bucket = jnp.asarray(bucket_np)
    band_mask = jnp.asarray(mask_np)
    bias_all = _bias_build(rel_table.T, bucket, band_mask).reshape(3, H, ATTN_BLOCK, 2 * ATTN_BLOCK)

    rowD = lambda i, j, k: (i, 0)
    vecD = lambda i, j, k: (0, 0)

    def in_proj_epilogue(acc, i, j, extra_refs, out_refs):
        h = acc + extra_refs[0][...]
        out_refs[0][...] = h[:, :3 * AW].astype(BF16)
        out_refs[1][...] = h[:, 3 * AW:]

    qkv, ag = _matmul_general(
        [(xf, (tm, D), rowD), (w_in_t, (INW, D), vecD)],
        lambda refs, i, j, k: _dot(refs[0][...], refs[1][...], "nt"),
        grid=(T // tm, 1, 1), tm=tm, tn=INW, extras=[(b_in, (1, INW), vecD)],
        outs=[((T, 3 * AW), BF16, (tm, 3 * AW), rowD), ((T, 2 * CW), F32, (tm, 2 * CW), rowD)],
        epilogue=in_proj_epilogue, name="mm_in")

    attn, lse, w_out_g, w_up_sh, w_down_g = _attention_fwd(qkv, bias_all, B, S, AW, bg=_bg_gather(staged))
    w_out = w_out_g.reshape(D, D)
    w_down = w_down_g.reshape(DFF, D)
    mixed_c = _conv_fwd(ag, conv_w, conv_b, conv_ln_g, conv_ln_b, conv_norm_g, B, S, CW)

    def attn_rstd(a):
        return lax.rsqrt(jnp.mean(a * a, axis=-1, keepdims=True) + LN_EPS)

    def mixed_rows(attn_ref, mc_ref, gain_ref):
        a = attn_ref[...]
        return jnp.concatenate([(a * attn_rstd(a) * gain_ref[...]).astype(BF16), mc_ref[...]], axis=1)

    def ln1_epilogue(acc, i, j, extra_refs, out_refs):
        x_ref, g_ref, b_ref, a_ref = extra_refs
        x1, xh, r = _ln_fwd(acc + ALPHA * x_ref[...], g_ref[...], b_ref[...])
        out_refs[0][...] = x1
        out_refs[1][...] = x1.astype(BF16)
        out_refs[2][...] = xh
        out_refs[3][...] = jnp.broadcast_to(r, (tm_s, LANES))
        out_refs[4][...] = jnp.broadcast_to(attn_rstd(a_ref[...]), (tm_s, LANES))

    x1, x1b, xh1, r1, r_attn = _matmul_general(
        [(attn, (tm_s, AW), rowD), (mixed_c, (tm_s, CW), rowD), (attn_norm_g, (1, AW), vecD), (w_out, (D, D), vecD)],
        lambda refs, i, j, k: _dot(mixed_rows(refs[0], refs[1], refs[2]), refs[3][...], "nn"),
        grid=(T // tm_s, 1, 1), tm=tm_s, tn=D,
        extras=[(xf, (tm_s, D), rowD), (ln1_g, (1, D), vecD), (ln1_b, (1, D), vecD), (attn, (tm_s, AW), rowD)],
        outs=[((T, D), F32, (tm_s, D), rowD), ((T, D), BF16, (tm_s, D), rowD), ((T, D), F32, (tm_s, D), rowD),
              ((T, LANES), F32, (tm_s, LANES), rowD), ((T, LANES), F32, (tm_s, LANES), rowD)],
        epilogue=ln1_epilogue, name="mm_out_ln1")

    NS, _, cs = w_up_sh.shape
    half = NS // 2

    act, upre = _ffn_fwd_fused(x1b, w_up_sh, ffn_cw, ffn_cb, B, S, DFF)

    halves = [slice(0, tm // 2), slice(tm // 2, tm)]

    def ln2_epilogue(parts, i, j, extra_refs, out_refs):
        x1_ref, g_ref, b_ref, t_ref = extra_refs
        dz_ref, dzb_ref, loss_ref, dg_ref, db_ref = out_refs
        g = g_ref[...]
        sums = None
        for rows, acc in zip(halves, parts):
            y, xh, r = _ln_fwd(acc + ALPHA * x1_ref[rows, :], g, b_ref[...])
            diff = y - t_ref[rows, :]
            row_loss = jnp.sum(diff * diff, axis=1, keepdims=True)
            tile_loss = jnp.sum(row_loss, axis=0, keepdims=True) * (0.5 / D)
            dy = diff * (1.0 / D)
            dz = _ln_bwd(dy, xh, r, g)
            dz_ref[rows, :] = dz
            dzb_ref[rows, :] = dz.astype(BF16)
            vals = (jnp.broadcast_to(tile_loss, (1, LANES)), jnp.sum(dy * xh, axis=0, keepdims=True),
                    jnp.sum(dy, axis=0, keepdims=True))
            sums = vals if sums is None else tuple(a + b for a, b in zip(sums, vals))
        for ref, val in zip((loss_ref, dg_ref, db_ref), sums):
            _accumulate(ref, i == 0, val)

    dz2, dz2b, loss_part, d_ln2_g, d_ln2_b = _matmul_general(
        [(act, (tm, DFF), rowD), (w_down, (DFF, D), vecD)],
        lambda refs, i, j, k: tuple(_dot(refs[0][rows, :], refs[1][...], "nn") for rows in halves),
        grid=(T // tm, 1, 1), tm=tm, tn=D,
        extras=[(x1, (tm, D), rowD), (ln2_g, (1, D), vecD), (ln2_b, (1, D), vecD), (tf, (tm, D), rowD)],
        outs=[((T, D), F32, (tm, D), rowD), ((T, D), BF16, (tm, D), rowD),
              ((1, LANES), F32, (1, LANES), vecD), ((1, D), F32, (1, D), vecD), ((1, D), F32, (1, D), vecD)],
        epilogue=ln2_epilogue, name="mm_down_ln2_loss")

    dupre_g, dupre_v, d_w_up_t, d_w_down, d_ffn_cw2, d_ffn_cb2 = _ffn_bwd_fused(
        x1b, dz2b, upre, w_down, ffn_cw, ffn_cb, B, S, DFF)
    d_w_up_t = d_w_up_t.reshape(NS, cs, D)
    d_ffn_cw = d_ffn_cw2.reshape(FFN_CONV_KERNEL, 2 * DFF)
    d_ffn_cb = d_ffn_cb2.reshape(1, 2 * DFF)
    tk_t = _row_tile(T, 512)

    early = [d_w_up_t, d_w_down.reshape(NS, DFF // NS, D)]
    dz1, dz1b, d_ln1_g, d_ln1_b, *sib_e = _dx1_ln1_bwd(dupre_g, dupre_v, w_up_sh, dz2, xh1, r1, ln1_g, tm,
                                                       bg=_bg_sibling_exchange(early))
    chip_e = [_pair_sum(g, s, ids, name="pair_sum_" + n) for g, s, n in zip(early, sib_e, ("w_up", "w_down"))]

    def dw_out_epilogue(acc, i, j, extra_refs, out_refs):
        out_refs[0][...] = acc

    d_w_out = _matmul_general(
        [(attn, (tk_t, AW), lambda i, j, k: (k, 0)), (mixed_c, (tk_t, CW), lambda i, j, k: (k, 0)),
         (attn_norm_g, (1, AW), vecD), (dz1b, (tk_t, D), lambda i, j, k: (k, 0))],
        lambda refs, i, j, k: _dot(mixed_rows(refs[0], refs[1], refs[2]), refs[3][...], "tn"),
        grid=(1, 1, T // tk_t), tm=D, tn=D, outs=[_plain_out(D, D, D, D, F32)],
        epilogue=dw_out_epilogue, name="mm_dw_out")[0]
    early.append(d_w_out.reshape(NS, D // NS, D))
    def dmixed_epilogue(acc, i, j, extra_refs, out_refs):
        a_ref, r_ref, g_ref = extra_refs
        do_ref, dd_ref, dmc_ref, dg_ref = out_refs
        head_of = lambda axis: lax.broadcasted_iota(jnp.int32, (AW, AW), axis) // HEAD_DIM
        same_head = (head_of(0) == head_of(1)).astype(BF16)
        dm = acc[:, :AW]
        dmc_ref[...] = acc[:, AW:]
        a = a_ref[...]
        r = r_ref[:, 0:1]
        dxn = dm * g_ref[...]
        da = r * (dxn - a * (r * r) * jnp.mean(dxn * a, axis=-1, keepdims=True))
        do_ref[...] = da.astype(BF16)
        hi, lo = _split_hi_lo(da * a)
        dd_ref[...] = (jnp.dot(hi, same_head, preferred_element_type=F32)
                       + jnp.dot(lo, same_head, preferred_element_type=F32))
        _accumulate(dg_ref, i == 0, jnp.sum(dm * a * r, axis=0, keepdims=True))

    dattn, dd, dmc, d_attn_norm_g, sib_out = _matmul(
        dz1b, w_out, mode="nt", tm=tm, tn=D, tk=D,
        extras=[(attn, (tm, AW), rowD), (r_attn, (tm, LANES), rowD), (attn_norm_g, (1, AW), vecD)],
        outs=[((T, AW), BF16, (tm, AW), rowD), ((T, AW), F32, (tm, AW), rowD), ((T, CW), F32, (tm, CW), rowD),
              ((1, AW), F32, (1, AW), vecD)],
        epilogue=dmixed_epilogue, name="mm_dmixed", bg=_bg_sibling_exchange(early[2:]))
    sib_e.append(sib_out)
    chip_e.append(_pair_sum(early[2], sib_out, ids, name="pair_sum_w_out"))

    dag, d_conv_w, d_conv_b, d_conv_ln_g, d_conv_ln_b, d_conv_norm_g = _conv_bwd(
        ag, dmc, conv_w, conv_b, conv_ln_g, conv_ln_b, conv_norm_g, B, S, CW)

    dq, dk, dv, csq, csk, csv, dbias, *got_e = _attention_bwd(qkv, dattn, lse, dd, bias_all, B, S, AW,
                                                              bg=_bg_chip_exchange(chip_e))
    full_up, full_down, full_out = [_final_sum(g, s, r, ids, name="final_sum_" + n)
                                    for g, s, r, n in zip(early, sib_e, got_e, ("w_up", "w_down", "w_out"))]
    d_rel_table = _rel_grad(dbias.reshape(3, H, ATTN_BLOCK * 2 * ATTN_BLOCK), bucket).T
    dh, cs_ag = _dh_cat(dq, dk, dv, dag, tm_s)
    d_b_in = jnp.concatenate([csq, csk, csv, cs_ag], axis=1)

    d_w_in_t = _mm_plain(dh, xf, mode="tn", tm=_col_tile(INW, 1408), tn=D, tk=tk_t, out_dtype=F32, name="mm_dw_in")
    late = [d_w_in_t.reshape(NS, INW // NS, D)]
    sib_l = _sibling_exchange(late)
    chip_l = [_pair_sum(late[0], sib_l[0], ids, name="pair_sum_w_in")]
    small = dict(rel_table=d_rel_table, b_in=d_b_in, conv_w=d_conv_w, conv_b=d_conv_b, conv_ln_g=d_conv_ln_g,
                 conv_ln_b=d_conv_ln_b, attn_norm_g=d_attn_norm_g, conv_norm_g=d_conv_norm_g, ln1_g=d_ln1_g,
                 ln1_b=d_ln1_b, ffn_conv_w=d_ffn_cw, ffn_conv_b=d_ffn_cb, ln2_g=d_ln2_g, ln2_b=d_ln2_b)
    pack = _pack([loss_part] + [small[n] for n in SMALL_NAMES])

    def gx_epilogue(acc, i, j, extra_refs, out_refs):
        out_refs[0][...] = acc + ALPHA * extra_refs[0][...]

    grad_x, got_in, all_packs = _matmul(
        dh, w_in_t, mode="nn", tm=tm, tn=D, tk=INW, extras=[(dz1, (tm, D), rowD)],
        outs=[((T, D), F32, (tm, D), rowD)], epilogue=gx_epilogue, name="mm_grad_x",
        bg=_bg_chip_exchange(chip_l, pack))
    full_in = _final_sum(late[0], sib_l[0], got_in, ids, name="final_sum_w_in")
    return grad_x.reshape(B, S, D), [full_in, full_out, full_up, full_down], all_packs


def _place():
    return lax.axis_index("x"), lax.axis_index("y"), lax.axis_index("c")


CHIP_FLIPS = ((1, 0), (0, 1), (1, 1))


def _flip(v, f):
    return 1 - v if f else v


HBM_SPEC = pl.BlockSpec(memory_space=pl.ANY)
VMEM_SPEC = pl.BlockSpec(memory_space=pltpu.VMEM)
COMM_PARAMS = pltpu.CompilerParams(vmem_limit_bytes=VMEM_LIMIT)


def _gather_weights(big, small):
    nb, ns = len(big), len(small)

    def body(*refs):
        big_in = refs[:nb]
        small_in = refs[nb:nb + ns]
        big_out = refs[nb + ns:2 * nb + ns]
        small_out = refs[2 * nb + ns:2 * nb + 2 * ns]
        stages = refs[2 * nb + 2 * ns:3 * nb + 2 * ns]
        send_sems, recv_sems, local_sems = refs[3 * nb + 2 * ns:]
        x, y, c = _place()
        s_me = 2 * x + y
        sibling = (x, y, 1 - c)
        started, local_copies = [], []
        for a in range(nb):
            rh = big[a].shape[0] // 2
            lo = pl.multiple_of(c * rh, 16)
            stages[a][...] = big_in[a][pl.ds(lo, rh), :].astype(BF16)
            mine = big_out[a].at[s_me, pl.ds(lo, rh), :]
            loc = pltpu.make_async_copy(stages[a], mine, local_sems.at[a])
            loc.start()
            local_copies.append(loc)
            targets = [sibling] + [(_flip(x, fx), _flip(y, fy), c) for fx, fy in CHIP_FLIPS]
            for k, to in enumerate(targets):
                cp = pltpu.make_async_remote_copy(stages[a], mine, send_sems.at[a * 7 + k],
                                                  recv_sems.at[a * 7 + k], device_id=to, device_id_type=MESH)
                cp.start()
                started.append(cp)
        for a in range(ns):
            mine = small_out[a].at[s_me]
            loc = pltpu.make_async_copy(small_in[a], mine, local_sems.at[nb + a])
            loc.start()
            local_copies.append(loc)
            for k, (fx, fy) in enumerate(CHIP_FLIPS):
                cp = pltpu.make_async_remote_copy(small_in[a], mine, send_sems.at[nb * 7 + a * 3 + k],
                                                  recv_sems.at[nb * 7 + a * 3 + k],
                                                  device_id=(_flip(x, fx), _flip(y, fy), c), device_id_type=MESH)
                cp.start()
                started.append(cp)
        for a in range(nb):
            rh = big[a].shape[0] // 2
            lo = pl.multiple_of(c * rh, 16)
            for k, (fx, fy) in enumerate(CHIP_FLIPS):
                s_from = 2 * _flip(x, fx) + _flip(y, fy)
                got = big_out[a].at[s_from, pl.ds(lo, rh), :]
                pltpu.make_async_remote_copy(got, got, send_sems.at[a * 7 + 1 + k], recv_sems.at[a * 7 + 1 + k],
                                             device_id=sibling, device_id_type=MESH).wait_recv()
                fwd = pltpu.make_async_remote_copy(got, got, send_sems.at[a * 7 + 4 + k],
                                                   recv_sems.at[a * 7 + 4 + k], device_id=sibling,
                                                   device_id_type=MESH)
                fwd.start()
                started.append(fwd)
        for a in range(nb):
            rh = big[a].shape[0] // 2
            lo_sib = pl.multiple_of((1 - c) * rh, 16)
            for k in (0, 4, 5, 6):
                any_rows = big_out[a].at[s_me, pl.ds(lo_sib, rh), :]
                pltpu.make_async_remote_copy(any_rows, any_rows, send_sems.at[a * 7 + k], recv_sems.at[a * 7 + k],
                                             device_id=sibling, device_id_type=MESH).wait_recv()
        for a in range(ns):
            for k in range(3):
                pltpu.make_async_remote_copy(small_in[a], small_out[a].at[s_me], send_sems.at[nb * 7 + a * 3 + k],
                                             recv_sems.at[nb * 7 + a * 3 + k], device_id=sibling,
                                             device_id_type=MESH).wait_recv()
        for cp in started:
            cp.wait_send()
        for cp in local_copies:
            cp.wait()

    n_sem = nb * 7 + ns * 3
    out_shape = ([jax.ShapeDtypeStruct((N_SHARDS,) + w.shape, BF16) for w in big]
                 + [jax.ShapeDtypeStruct((N_SHARDS,) + w.shape, F32) for w in small])
    res = pl.pallas_call(
        body, in_specs=[VMEM_SPEC] * nb + [HBM_SPEC] * ns, out_specs=[HBM_SPEC] * (nb + ns),
        out_shape=out_shape,
        scratch_shapes=[pltpu.VMEM((w.shape[0] // 2, w.shape[1]), BF16) for w in big]
        + [pltpu.SemaphoreType.DMA((n_sem,)), pltpu.SemaphoreType.DMA((n_sem,)),
           pltpu.SemaphoreType.DMA((nb + ns,))],
        compiler_params=COMM_PARAMS, name="gather_weights",
    )(*big, *small)
    return res[:nb], res[nb:]


def _sibling_exchange(grads):
    n = len(grads)

    def body(*refs):
        g_in = refs[:n]
        got = refs[n:2 * n]
        send_sems, recv_sems = refs[2 * n:]
        x, y, c = _place()
        cps = []
        for a in range(n):
            rh = grads[a].shape[1] // 2
            lo = pl.multiple_of((1 - c) * rh, 8)
            cp = pltpu.make_async_remote_copy(g_in[a].at[:, pl.ds(lo, rh), :], got[a], send_sems.at[a],
                                              recv_sems.at[a], device_id=(x, y, 1 - c), device_id_type=MESH)
            cp.start()
            cps.append(cp)
        for cp in cps:
            cp.wait()

    return pl.pallas_call(
        body, in_specs=[HBM_SPEC] * n, out_specs=[HBM_SPEC] * n,
        out_shape=[jax.ShapeDtypeStruct((N_SHARDS, g.shape[1] // 2, g.shape[2]), F32) for g in grads],
        scratch_shapes=[pltpu.SemaphoreType.DMA((n,)), pltpu.SemaphoreType.DMA((n,))],
        compiler_params=COMM_PARAMS, name="sibling_exchange",
    )(*grads)


def _sibling_assemble(fulls):
    n = len(fulls)

    def body(*refs):
        full = refs[n:2 * n]
        send_sems, recv_sems = refs[2 * n:]
        x, y, c = _place()
        cps = []
        for a in range(n):
            rh = fulls[a].shape[0] // 2
            mine = full[a].at[pl.ds(pl.multiple_of(c * rh, 8), rh), :]
            cp = pltpu.make_async_remote_copy(mine, mine, send_sems.at[a], recv_sems.at[a],
                                              device_id=(x, y, 1 - c), device_id_type=MESH)
            cp.start()
            cps.append(cp)
        for cp in cps:
            cp.wait()

    return pl.pallas_call(
        body, in_specs=[HBM_SPEC] * n, out_specs=[HBM_SPEC] * n,
        out_shape=[jax.ShapeDtypeStruct(f.shape, F32) for f in fulls],
        input_output_aliases={a: a for a in range(n)},
        scratch_shapes=[pltpu.SemaphoreType.DMA((n,)), pltpu.SemaphoreType.DMA((n,))],
        compiler_params=COMM_PARAMS, name="sibling_assemble",
    )(*fulls)


def _remote(ref_src, ref_dst, send_sems, recv_sems, k, to):
    return pltpu.make_async_remote_copy(ref_src, ref_dst, send_sems.at[k], recv_sems.at[k], device_id=to,
                                        device_id_type=MESH)


def _stage_half(w, ids, name):
    R, C = w.shape
    rh = R // 2
    rt = _half_tile(rh)
    nt = rh // rt

    def body(ids_ref, w_ref, o_ref):
        o_ref[0] = w_ref[...].astype(BF16)

    grid_spec = pltpu.PrefetchScalarGridSpec(
        num_scalar_prefetch=1, grid=(nt,),
        in_specs=[pl.BlockSpec((rt, C), lambda i, ids: (ids[2] * nt + i, 0))],
        out_specs=pl.BlockSpec((1, rt, C), lambda i, ids: (2 * ids[0] + ids[1], ids[2] * nt + i, 0)))
    return pl.pallas_call(body, grid_spec=grid_spec, out_shape=jax.ShapeDtypeStruct((N_SHARDS, R, C), BF16),
                          compiler_params=_params(1), name=name)(ids, w)


def _bg_gather(staged):
    n = len(staged)

    def run(step, n_steps, ins, outs, send_sems, recv_sems, local_sems, post):
        x, y, c = _place()
        s_me = 2 * x + y
        sibling = (x, y, 1 - c)
        chips = [(_flip(x, fx), _flip(y, fy)) for fx, fy in CHIP_FLIPS]

        def rows(a, s, half):
            rh = staged[a].shape[1] // 2
            return outs[a].at[s, pl.ds(pl.multiple_of(half * rh, 16), rh), :]

        def copy(a, k, ref, to):
            return _remote(ref, ref, send_sems, recv_sems, a * 7 + k, to)

        if not post:
            @pl.when(step == 0)
            def _():
                for a in range(n):
                    mine = rows(a, s_me, c)
                    copy(a, 0, mine, sibling).start()
                    for k, (px, py) in enumerate(chips):
                        copy(a, 1 + k, mine, (px, py, c)).start()

            @pl.when(step == max(n_steps - 2, 0))
            def _():
                for a in range(n):
                    for k, (px, py) in enumerate(chips):
                        got = rows(a, 2 * px + py, c)
                        copy(a, 1 + k, got, sibling).wait_recv()
                        copy(a, 4 + k, got, sibling).start()
        else:
            @pl.when(step == n_steps - 1)
            def _():
                for a in range(n):
                    for k in (0, 4, 5, 6):
                        copy(a, k, rows(a, s_me, 1 - c), sibling).wait_recv()
                    for k in range(7):
                        copy(a, k, rows(a, s_me, c), sibling).wait_send()

    return _Background(staged, [jax.ShapeDtypeStruct(g.shape, g.dtype) for g in staged],
                       {a: a for a in range(n)}, 7 * n, run)


def _bg_sibling_exchange(grads):
    n = len(grads)

    def run(step, n_steps, ins, outs, send_sems, recv_sems, local_sems, post):
        x, y, c = _place()

        def copy(a):
            rh = grads[a].shape[1] // 2
            lo = pl.multiple_of((1 - c) * rh, 8)
            return _remote(ins[a].at[:, pl.ds(lo, rh), :], outs[a], send_sems, recv_sems, a, (x, y, 1 - c))

        if not post:
            @pl.when(step == 0)
            def _():
                for a in range(n):
                    copy(a).start()
        else:
            @pl.when(step == n_steps - 1)
            def _():
                for a in range(n):
                    copy(a).wait()

    return _Background(grads, [jax.ShapeDtypeStruct((N_SHARDS, g.shape[1] // 2, g.shape[2]), F32) for g in grads],
                       {}, n, run)


def _bg_chip_exchange(chip_parts, pack=None):
    n = len(chip_parts)

    def run(step, n_steps, ins, outs, send_sems, recv_sems, local_sems, post):
        x, y, c = _place()
        me = 4 * x + 2 * y + c

        def copies():
            cps = []
            for a in range(n):
                for k, (fx, fy) in enumerate(CHIP_FLIPS):
                    px, py = _flip(x, fx), _flip(y, fy)
                    cps.append(_remote(ins[a].at[2 * px + py], outs[a].at[k], send_sems, recv_sems, a * 3 + k,
                                       (px, py, c)))
            if pack is not None:
                for m in range(1, N_DEV):
                    to = (_flip(x, m & 4), _flip(y, m & 2), _flip(c, m & 1))
                    cps.append(_remote(ins[n], outs[n].at[me], send_sems, recv_sems, n * 3 + m - 1, to))
            return cps

        def local():
            return pltpu.make_async_copy(ins[n], outs[n].at[me], local_sems.at[0])

        if not post:
            @pl.when(step == 0)
            def _():
                for cp in copies():
                    cp.start()
                if pack is not None:
                    local().start()
        else:
            @pl.when(step == n_steps - 1)
            def _():
                for cp in copies():
                    cp.wait()
                if pack is not None:
                    local().wait()

    in_arrays = list(chip_parts) + ([pack] if pack is not None else [])
    out_shapes = [jax.ShapeDtypeStruct((3,) + p.shape[1:], BF16) for p in chip_parts]
    if pack is not None:
        out_shapes.append(jax.ShapeDtypeStruct((N_DEV, pack.shape[0], LANES), F32))
    return _Background(in_arrays, out_shapes, {}, n * 3 + N_DEV - 1, run)


def _half_tile(rh, mult=16, want=256):
    best = None
    for t in range(mult, min(rh, want) + 1, mult):
        if rh % t == 0:
            best = t
    return best if best is not None else rh


def _pair_sum(g, sib, ids, name):
    _, R, C = g.shape
    rh = R // 2
    rt = _half_tile(rh)
    nt = rh // rt

    def body(ids_ref, g_ref, s_ref, o_ref):
        o_ref[...] = (g_ref[...] + s_ref[...]).astype(BF16)

    def other(j, ids):
        return j + (j >= 2 * ids[0] + ids[1]).astype(jnp.int32)

    grid_spec = pltpu.PrefetchScalarGridSpec(
        num_scalar_prefetch=1, grid=(N_SHARDS - 1, nt),
        in_specs=[pl.BlockSpec((1, rt, C), lambda j, i, ids: (other(j, ids), ids[2] * nt + i, 0)),
                  pl.BlockSpec((1, rt, C), lambda j, i, ids: (other(j, ids), i, 0))],
        out_specs=pl.BlockSpec((1, rt, C), lambda j, i, ids: (other(j, ids), i, 0)))
    return pl.pallas_call(body, grid_spec=grid_spec, out_shape=jax.ShapeDtypeStruct((N_SHARDS, rh, C), BF16),
                          compiler_params=_params(2), name=name)(ids, g, sib)


def _final_sum(g, sib, got, ids, name):
    _, R, C = g.shape
    rh = R // 2
    rt = _half_tile(rh)
    nt = rh // rt

    def body(ids_ref, g_ref, s_ref, r_ref, o_ref):
        tot = g_ref[0] + s_ref[0]
        for k in range(3):
            tot = tot + r_ref[k].astype(F32)
        o_ref[...] = tot

    grid_spec = pltpu.PrefetchScalarGridSpec(
        num_scalar_prefetch=1, grid=(nt,),
        in_specs=[pl.BlockSpec((1, rt, C), lambda i, ids: (2 * ids[0] + ids[1], ids[2] * nt + i, 0)),
                  pl.BlockSpec((1, rt, C), lambda i, ids: (2 * ids[0] + ids[1], i, 0)),
                  pl.BlockSpec((3, rt, C), lambda i, ids: (0, i, 0))],
        out_specs=pl.BlockSpec((rt, C), lambda i, ids: (ids[2] * nt + i, 0)))
    return pl.pallas_call(body, grid_spec=grid_spec, out_shape=jax.ShapeDtypeStruct((R, C), F32),
                          compiler_params=_params(1), name=name)(ids, g, sib, got)


def _sum_packs(all_packs):
    def body(p_ref, o_ref):
        tot = p_ref[0]
        for i in range(1, N_DEV):
            tot = tot + p_ref[i]
        o_ref[...] = tot

    return pl.pallas_call(body, in_specs=[VMEM_SPEC], out_specs=VMEM_SPEC,
                          out_shape=jax.ShapeDtypeStruct(all_packs.shape[1:], F32), name="sum_packs")(all_packs)


def _adamw(w, g, m, v, name, g_transposed=False):
    R, C = w.shape
    rt = _half_tile(R, mult=LANES if g_transposed else 8, want=256)

    def body(w_ref, g_ref, m_ref, v_ref, g_out_ref, d_ref, nm_ref, nv_ref):
        gg = g_ref[...].T if g_transposed else g_ref[...]
        g_out_ref[...] = gg
        d_ref[...], nm_ref[...], nv_ref[...] = _adamw_update(w_ref[...], gg, m_ref[...], v_ref[...])

    spec = pl.BlockSpec((rt, C), lambda i: (i, 0))
    g_spec = pl.BlockSpec((C, rt), lambda i: (0, i)) if g_transposed else spec
    return pl.pallas_call(body, grid=(R // rt,), in_specs=[spec, g_spec, spec, spec], out_specs=[spec] * 4,
                          out_shape=[jax.ShapeDtypeStruct((R, C), F32)] * 4,
                          compiler_params=_params(1), name=name)(w, g, m, v)


def _adamw_update(w, g, m, v):
    nm = ADAM_B1 * m + (1.0 - ADAM_B1) * g
    nv = ADAM_B2 * v + (1.0 - ADAM_B2) * (g * g)
    m_hat = nm / (1.0 - ADAM_B1 ** ADAM_STEP)
    v_hat = nv / (1.0 - ADAM_B2 ** ADAM_STEP)
    return -ADAM_LR * (m_hat / (jnp.sqrt(v_hat) + ADAM_EPS) + ADAM_WD * w), nm, nv


def _adamw_many(ws, gs, ms, vs, name):
    n = len(ws)

    def body(*refs):
        for i in range(n):
            d, nm, nv = _adamw_update(refs[i][...], refs[n + i][...], refs[2 * n + i][...], refs[3 * n + i][...])
            refs[4 * n + i][...] = d
            refs[5 * n + i][...] = nm
            refs[6 * n + i][...] = nv

    return pl.pallas_call(body, in_specs=[VMEM_SPEC] * (4 * n), out_specs=[VMEM_SPEC] * (3 * n),
                          out_shape=[jax.ShapeDtypeStruct(w.shape, F32) for w in ws] * 3, name=name,
                          )(*ws, *gs, *ms, *vs)


def _pack(pieces):
    rows = []
    for p in pieces:
        flat = p.reshape(-1)
        pad = (-flat.shape[0]) % LANES
        if pad:
            flat = jnp.concatenate([flat, jnp.zeros((pad,), F32)])
        rows.append(flat.reshape(-1, LANES))
    total = sum(r.shape[0] for r in rows)
    pad_rows = (-total) % 8
    if pad_rows:
        rows.append(jnp.zeros((pad_rows, LANES), F32))
    return jnp.concatenate(rows, axis=0)


def _unpack(buf, shapes):
    out, r0 = [], 0
    for shp in shapes:
        n = int(np.prod(shp))
        nr = -(-n // LANES)
        out.append(buf[r0:r0 + nr].reshape(-1)[:n].reshape(shp))
        r0 += nr
    return out


SMALL_NAMES = ("rel_table", "b_in", "conv_w", "conv_b", "conv_ln_g", "conv_ln_b", "attn_norm_g", "conv_norm_g",
               "ln1_g", "ln1_b", "ffn_conv_w", "ffn_conv_b", "ln2_g", "ln2_b")
BIG_NAMES = ("w_in", "w_out", "w_up", "w_down")
WEIGHT_ORDER = ("rel_table", "w_in", "b_in", "conv_w", "conv_b", "conv_ln_g", "conv_ln_b", "attn_norm_g",
                "conv_norm_g", "w_out", "ln1_g", "ln1_b", "w_up", "ffn_conv_w", "ffn_conv_b", "w_down",
                "ln2_g", "ln2_b")


def kernel(x, rel_table, w_in, b_in, conv_w, conv_b, conv_ln_g, conv_ln_b, attn_norm_g, conv_norm_g, w_out, ln1_g, ln1_b, w_up, ffn_conv_w, ffn_conv_b, w_down, ln2_g, ln2_b, loss_target, m_rel_table, m_w_in, m_b_in, m_conv_w, m_conv_b, m_conv_ln_g, m_conv_ln_b, m_attn_norm_g, m_conv_norm_g, m_w_out, m_ln1_g, m_ln1_b, m_w_up, m_ffn_conv_w, m_ffn_conv_b, m_w_down, m_ln2_g, m_ln2_b, v_rel_table, v_w_in, v_b_in, v_conv_w, v_conv_b, v_conv_ln_g, v_conv_ln_b, v_attn_norm_g, v_conv_norm_g, v_w_out, v_ln1_g, v_ln1_b, v_w_up, v_ffn_conv_w, v_ffn_conv_b, v_w_down, v_ln2_g, v_ln2_b):
    args = dict(locals())
    weights = {n: args[n] for n in WEIGHT_ORDER}
    moms = {n: args["m_" + n] for n in WEIGHT_ORDER}
    vels = {n: args["v_" + n] for n in WEIGHT_ORDER}
    xi, yi, ci = _place()
    ids = jnp.stack([xi, yi, ci]).astype(jnp.int32)
    shard = 2 * xi + yi
    D = x.shape[-1]
    DFF = w_down.shape[1] * N_SHARDS
    CW = conv_norm_g.shape[-1]

    tr = lambda t: jnp.transpose(t[0])
    (g_in,), (g_cw, g_fcw) = _gather_weights([tr(w_in)], [conv_w[0], ffn_conv_w[0]])
    cols = lambda t: jnp.transpose(t, (1, 0, 2)).reshape(t.shape[1], N_SHARDS * t.shape[2])
    staged = [_stage_half(w[0], ids, name="stage_" + n) for w, n in ((w_out, "w_out"), (w_up, "w_up"),
                                                                     (w_down, "w_down"))]

    grad_x, fulls, all_packs = _local_step(
        x, loss_target, rel_table, g_in.reshape(-1, D), b_in, cols(g_cw), conv_b, conv_ln_g, conv_ln_b, attn_norm_g,
        conv_norm_g, staged, ln1_g, ln1_b, cols(g_fcw), ffn_conv_b, ln2_g, ln2_b, ids)
    big_grads = dict(zip(BIG_NAMES, _sibling_assemble(fulls)))

    summed = _sum_packs(all_packs)
    full_shapes = {n: weights[n].shape for n in SMALL_NAMES}
    full_shapes["conv_w"] = (1, CONV_KERNEL, CW)
    full_shapes["ffn_conv_w"] = (1, FFN_CONV_KERNEL, 2 * DFF)
    un = _unpack(summed, [(1, LANES)] + [full_shapes[n] for n in SMALL_NAMES])
    loss = un[0][0, 0]
    small_grads = dict(zip(SMALL_NAMES, un[1:]))
    for n in ("conv_w", "ffn_conv_w"):
        width = weights[n].shape[-1]
        small_grads[n] = lax.dynamic_slice_in_dim(small_grads[n], shard * width, width, axis=2)

    grads, delta, new_m, new_v = {}, {}, {}, {}
    for n in BIG_NAMES:
        shp = weights[n].shape
        g2 = big_grads[n]
        if n == "w_in":
            res = [jnp.transpose(t) for t in _adamw(tr(weights[n]), g2, tr(moms[n]), tr(vels[n]), name="adamw_" + n)]
        else:
            res = _adamw(weights[n][0], g2, moms[n][0], vels[n][0], name="adamw_" + n, g_transposed=n == "w_up")
        grads[n], delta[n], new_m[n], new_v[n] = (t.reshape(shp) for t in res)
    pick = lambda src: [src[n] for n in SMALL_NAMES]
    small_out = _adamw_many(pick(weights), pick(small_grads), pick(moms), pick(vels), name="adamw_small")
    ns = len(SMALL_NAMES)
    for tgt, part in ((delta, small_out[:ns]), (new_m, small_out[ns:2 * ns]), (new_v, small_out[2 * ns:])):
        tgt.update(zip(SMALL_NAMES, part))
    grads.update(small_grads)

    return (loss, grad_x, *[grads[n] for n in WEIGHT_ORDER], *[delta[n] for n in WEIGHT_ORDER],
            *[new_m[n] for n in WEIGHT_ORDER], *[new_v[n] for n in WEIGHT_ORDER])
```

```python
import math

import numpy as np
import jax
import jax.numpy as jnp
from jax import lax
from jax.experimental import pallas as pl
from jax.experimental.pallas import tpu as pltpu

F32 = jnp.float32
BF16 = jnp.bfloat16
MESH = pl.DeviceIdType.MESH

HEAD_DIM = 64
LANES = 128
ATTN_BLOCK = 128
DILATED_CONFIGS = ((128, 1), (512, 4), (2048, 16))
CONV_KERNEL = 31
FFN_CONV_KERNEL = 3
REL_BUCKETS = 32
REL_MAX_DIST = 2048
DEPTH = 1
ALPHA = (2 * DEPTH) ** 0.25
LN_EPS = 1e-5
NEG_INF = -1e30
QK_SCALE = 1.0 / math.sqrt(HEAD_DIM)
ADAM_LR = 0.001
ADAM_B1 = 0.9
ADAM_B2 = 0.999
ADAM_EPS = 1e-08
ADAM_WD = 0.01
ADAM_STEP = 10
VMEM_LIMIT = 52 * 1024 * 1024
FFN_COLS = 128
N_SHARDS = 4
N_DEV = 8


def _params(n_axes):
    return pltpu.CompilerParams(dimension_semantics=("arbitrary",) * n_axes,
                                vmem_limit_bytes=VMEM_LIMIT)


MM_DIMS = {"nn": (((1,), (0,)), ((), ())), "nt": (((1,), (1,)), ((), ())), "tn": (((0,), (0,)), ((), ()))}


class _Background:
    def __init__(self, in_arrays, out_shapes, aliases, n_sems, run, n_local=1):
        self.in_arrays, self.out_shapes, self.aliases = list(in_arrays), list(out_shapes), dict(aliases)
        self.n_sems, self.n_local, self.run = n_sems, n_local, run

    def scratch(self):
        return [pltpu.SemaphoreType.DMA((self.n_sems,)), pltpu.SemaphoreType.DMA((self.n_sems,)),
                pltpu.SemaphoreType.DMA((self.n_local,))]


def _hosted_call(body, bg, *, grid, in_specs, out_specs, out_shape, scratch_shapes, operands, name):
    n_in, n_out, n_scr = len(in_specs), len(out_specs), len(scratch_shapes)
    if bg is None:
        return pl.pallas_call(lambda *refs: body(refs, lambda post: None), grid=grid, in_specs=in_specs,
                              out_specs=out_specs, out_shape=out_shape, scratch_shapes=scratch_shapes,
                              compiler_params=_params(len(grid)), name=name)(*operands)
    nb_in, nb_out = len(bg.in_arrays), len(bg.out_shapes)
    n_steps = int(np.prod(grid))

    def full_body(*refs):
        own = refs[:n_in] + refs[n_in + nb_in:n_in + nb_in + n_out] \
            + refs[n_in + nb_in + n_out + nb_out:n_in + nb_in + n_out + nb_out + n_scr]
        bg_in = refs[n_in:n_in + nb_in]
        bg_out = refs[n_in + nb_in + n_out:n_in + nb_in + n_out + nb_out]
        sems = refs[n_in + nb_in + n_out + nb_out + n_scr:]
        step = pl.program_id(0)
        for ax in range(1, len(grid)):
            step = step * grid[ax] + pl.program_id(ax)

        def hook(post):
            bg.run(step, n_steps, bg_in, bg_out, *sems, post)

        body(own, hook)

    res = pl.pallas_call(
        full_body, grid=grid, in_specs=list(in_specs) + [HBM_SPEC] * nb_in,
        out_specs=list(out_specs) + [HBM_SPEC] * nb_out, out_shape=list(out_shape) + bg.out_shapes,
        input_output_aliases={n_in + a: n_out + o for a, o in bg.aliases.items()},
        scratch_shapes=list(scratch_shapes) + bg.scratch(), compiler_params=_params(len(grid)), name=name,
    )(*operands, *bg.in_arrays)
    return res


def _matmul_general(ins, part_fn, *, grid, tm, tn, outs, epilogue, extras=(), name, bg=None):
    nk = grid[2]
    n_in, n_extra = len(ins), len(extras)

    def body(refs, bg_hook):
        in_refs = refs[:n_in]
        rest = refs[n_in:]
        extra_refs = rest[:n_extra]
        out_refs = rest[n_extra:n_extra + len(outs)]
        acc_ref = rest[-1]
        i, j, k = pl.program_id(0), pl.program_id(1), pl.program_id(2)
        bg_hook(False)
        part = part_fn(in_refs, i, j, k)
        if nk == 1:
            epilogue(part, i, j, extra_refs, out_refs)
        else:
            @pl.when(k == 0)
            def _():
                acc_ref[...] = part

            @pl.when(k > 0)
            def _():
                acc_ref[...] += part

            @pl.when(k == nk - 1)
            def _():
                epilogue(acc_ref[...], i, j, extra_refs, out_refs)
        bg_hook(True)

    in_specs = [pl.BlockSpec(bs, im) for (_, bs, im) in list(ins) + list(extras)]
    out_specs = [pl.BlockSpec(bs, im) for (_, _, bs, im) in outs]
    out_shape = [jax.ShapeDtypeStruct(s, d) for (s, d, _, _) in outs]
    return _hosted_call(body, bg, grid=grid, in_specs=in_specs, out_specs=out_specs, out_shape=out_shape,
                        scratch_shapes=[pltpu.VMEM((tm, tn), F32)],
                        operands=[e[0] for e in ins] + [e[0] for e in extras], name=name)


def _dot(a, b, mode):
    return lax.dot_general(a.astype(BF16), b.astype(BF16), MM_DIMS[mode], preferred_element_type=F32)


def _matmul(a, b, *, mode, tm, tn, tk, outs, epilogue, extras=(), name, bg=None):
    if mode == "tn":
        K, M = a.shape
        N = b.shape[1]
        ins = [(a, (tk, tm), lambda i, j, k: (k, i)), (b, (tk, tn), lambda i, j, k: (k, j))]
    elif mode == "nt":
        M, K = a.shape
        N = b.shape[0]
        ins = [(a, (tm, tk), lambda i, j, k: (i, k)), (b, (tn, tk), lambda i, j, k: (j, k))]
    else:
        M, K = a.shape
        N = b.shape[1]
        ins = [(a, (tm, tk), lambda i, j, k: (i, k)), (b, (tk, tn), lambda i, j, k: (k, j))]
    assert M % tm == 0 and N % tn == 0 and K % tk == 0, (name, M, N, K, tm, tn, tk)

    def part_fn(in_refs, i, j, k):
        return _dot(in_refs[0][...], in_refs[1][...], mode)

    return _matmul_general(ins, part_fn, grid=(M // tm, N // tn, K // tk), tm=tm, tn=tn, outs=outs,
                           epilogue=epilogue, extras=extras, name=name, bg=bg)


def _plain_out(M, N, tm, tn, dtype):
    return ((M, N), dtype, (tm, tn), lambda i, j, k: (i, j))


def _mm_plain(a, b, *, mode, tm, tn, tk, out_dtype, name, bias=None, bg=None):
    if mode == "tn":
        M, N = a.shape[1], b.shape[1]
    elif mode == "nt":
        M, N = a.shape[0], b.shape[0]
    else:
        M, N = a.shape[0], b.shape[1]
    extras = []
    if bias is not None:
        extras.append((bias, (1, tn), lambda i, j, k: (0, j)))

    def epilogue(acc, i, j, extra_refs, out_refs):
        if bias is not None:
            acc = acc + extra_refs[0][...]
        out_refs[0][...] = acc.astype(out_dtype)

    res = _matmul(a, b, mode=mode, tm=tm, tn=tn, tk=tk, outs=[_plain_out(M, N, tm, tn, out_dtype)],
                  epilogue=epilogue, extras=extras, name=name, bg=bg)
    return res[0] if bg is None else res


def _row_tile(T, want):
    t = min(T, want)
    while T % t:
        t //= 2
    return t


def _col_tile(N, want):
    if N <= want:
        return N
    best = None
    for c in range(LANES, want + 1, LANES):
        if N % c == 0:
            best = c
    return best if best is not None else N


def _accumulate(ref, first, val):
    @pl.when(first)
    def _():
        ref[...] = val

    @pl.when(jnp.logical_not(first))
    def _():
        ref[...] += val


def _ln_fwd(z, g, b):
    mu = jnp.mean(z, axis=-1, keepdims=True)
    zc = z - mu
    var = jnp.mean(zc * zc, axis=-1, keepdims=True)
    r = lax.rsqrt(var + LN_EPS)
    xh = zc * r
    return xh * g + b, xh, r


def _ln_bwd(dy, xh, r, g):
    dxh = dy * g
    m1 = jnp.mean(dxh, axis=-1, keepdims=True)
    m2 = jnp.mean(dxh * xh, axis=-1, keepdims=True)
    return r * (dxh - m1 - xh * m2)


def _sigmoid(x):
    return 0.5 * jnp.tanh(0.5 * x) + 0.5


def _bucket_tables():
    exact = REL_BUCKETS // 2
    qi = np.arange(ATTN_BLOCK)[:, None]
    kj = np.arange(2 * ATTN_BLOCK)[None, :]
    steps = qi + ATTN_BLOCK - kj
    buckets, masks = [], []
    for window, dilation in DILATED_CONFIGS:
        max_steps = window // dilation
        band = (steps >= 0) & (steps <= max_steps)
        dist = np.maximum(steps, 0) * dilation
        d_f = np.maximum(dist, 1).astype(np.float32)
        large = exact + (np.log(d_f / np.float32(exact)) / np.float32(math.log(REL_MAX_DIST / exact))
                         * np.float32(REL_BUCKETS - exact)).astype(np.int32)
        large = np.minimum(large, REL_BUCKETS - 1)
        bucket = np.where(dist < exact, dist, large).astype(np.int32)
        buckets.append(bucket.reshape(1, -1))
        masks.append(np.where(band, 0.0, NEG_INF).astype(np.float32).reshape(1, -1))
    return np.stack(buckets), np.stack(masks)


def _split_hi_lo(x):
    hi = x.astype(BF16)
    lo = (x - hi.astype(F32)).astype(BF16)
    return hi, lo


def _bias_build(rel_table_t, bucket, mask):
    H = rel_table_t.shape[0]
    n = bucket.shape[-1]

    def body(t_ref, bkt_ref, mask_ref, o_ref):
        onehot = (lax.broadcasted_iota(jnp.int32, (REL_BUCKETS, n), 0) == bkt_ref[0]).astype(BF16)
        t = t_ref[...]
        t1 = t.astype(BF16)
        r1 = t - t1.astype(F32)
        t2 = r1.astype(BF16)
        t3 = (r1 - t2.astype(F32)).astype(BF16)
        acc = jnp.dot(t1, onehot, preferred_element_type=F32)
        acc = acc + jnp.dot(t2, onehot, preferred_element_type=F32)
        acc = acc + jnp.dot(t3, onehot, preferred_element_type=F32)
        o_ref[0] = acc + mask_ref[0]

    return pl.pallas_call(
        body, grid=(3,),
        in_specs=[pl.BlockSpec((H, REL_BUCKETS), lambda b: (0, 0)),
                  pl.BlockSpec((1, 1, n), lambda b: (b, 0, 0)),
                  pl.BlockSpec((1, 1, n), lambda b: (b, 0, 0))],
        out_specs=pl.BlockSpec((1, H, n), lambda b: (b, 0, 0)),
        out_shape=jax.ShapeDtypeStruct((3, H, n), F32),
        compiler_params=_params(1), name="bias_build",
    )(rel_table_t, bucket, mask)


def _rel_grad(dbias, bucket):
    H = dbias.shape[1]
    n = bucket.shape[-1]
    dims = (((1,), (1,)), ((), ()))

    def body(d_ref, bkt_ref, o_ref):
        b = pl.program_id(0)
        onehot = (lax.broadcasted_iota(jnp.int32, (REL_BUCKETS, n), 0) == bkt_ref[0]).astype(BF16)
        d = d_ref[0]
        d1 = d.astype(BF16)
        r1 = d - d1.astype(F32)
        d2 = r1.astype(BF16)
        d3 = (r1 - d2.astype(F32)).astype(BF16)
        acc = lax.dot_general(d1, onehot, dims, preferred_element_type=F32)
        acc = acc + lax.dot_general(d2, onehot, dims, preferred_element_type=F32)
        acc = acc + lax.dot_general(d3, onehot, dims, preferred_element_type=F32)
        _accumulate(o_ref, b == 0, acc)

    return pl.pallas_call(
        body, grid=(3,),
        in_specs=[pl.BlockSpec((1, H, n), lambda b: (b, 0, 0)),
                  pl.BlockSpec((1, 1, n), lambda b: (b, 0, 0))],
        out_specs=pl.BlockSpec((H, REL_BUCKETS), lambda b: (0, 0)),
        out_shape=jax.ShapeDtypeStruct((H, REL_BUCKETS), F32),
        compiler_params=_params(1), name="rel_grad",
    )(dbias, bucket)


def _regroup(src, stage, dst, d, S, off=0):
    if d == 1:
        dst[off:off + S, :] = src.astype(dst.dtype)
        return
    stage[...] = src.astype(F32)
    L = S // d
    for r in range(d):
        dst[off + r * L:off + (r + 1) * L, :] = stage[pl.ds(r, L, stride=d), :].astype(dst.dtype)


def _ungroup(sub_ref, off, nat_ref, d, S, add):
    L = S // d
    for r in range(d):
        rows = pl.ds(0, S) if d == 1 else pl.ds(r, L, stride=d)
        val = sub_ref[off + r * L:off + (r + 1) * L, :]
        if add:
            nat_ref[rows, :] += val
        else:
            nat_ref[rows, :] = val


def _branch_keys(ks, vs, S, nb, g_idx):
    blk3 = (S // ATTN_BLOCK, ATTN_BLOCK, LANES)
    kc3 = ks[ATTN_BLOCK:ATTN_BLOCK + S, :].reshape(blk3)
    vc3 = vs[ATTN_BLOCK:ATTN_BLOCK + S, :].reshape(blk3)
    if nb == 1:
        return kc3, vc3, None
    kk3 = jnp.concatenate([ks[0:S, :].reshape(blk3), kc3], axis=1)
    vv3 = jnp.concatenate([vs[0:S, :].reshape(blk3), vc3], axis=1)
    col = lax.broadcasted_iota(jnp.int32, (1, 1, 2 * ATTN_BLOCK), 2)
    dead = jnp.logical_and((g_idx & (nb - 1)) == 0, col < ATTN_BLOCK)
    return kk3, vv3, dead


def _branch_scores(qe, kk3, b_ref, bi, e, dead):
    s = jnp.einsum("gqe,gke->gqk", qe, kk3, preferred_element_type=F32)
    if dead is None:
        return s + b_ref[bi, e, :, ATTN_BLOCK:]
    return jnp.where(dead, NEG_INF, s + b_ref[bi, e])


def _attention_fwd(qkv, bias_all, B, S, AW, bg=None):
    HP = AW // LANES
    G = S // ATTN_BLOCK
    blk3 = (G, ATTN_BLOCK, LANES)

    def body(refs, bg_hook):
        q_ref, k_ref, v_ref, b_ref, o_ref, lse_ref, stage, qs, ks, vs, ot, lt, on0, on1, on2, ln0, ln1, ln2 = refs
        bg_hook(False)
        head0 = lax.broadcasted_iota(jnp.int32, (1, 1, LANES), 2) < HEAD_DIM
        g_idx = lax.broadcasted_iota(jnp.int32, (G, 1, 1), 0)
        ks[0:ATTN_BLOCK, :] = jnp.zeros((ATTN_BLOCK, LANES), BF16)
        vs[0:ATTN_BLOCK, :] = jnp.zeros((ATTN_BLOCK, LANES), BF16)
        nat_o, nat_l = (on0, on1, on2), (ln0, ln1, ln2)
        for bi, (_, d) in enumerate(DILATED_CONFIGS):
            nb = S // d // ATTN_BLOCK
            _regroup(q_ref[0], stage, qs, d, S)
            _regroup(k_ref[0], stage, ks, d, S, ATTN_BLOCK)
            _regroup(v_ref[0], stage, vs, d, S, ATTN_BLOCK)
            q3 = qs[...].reshape(blk3) * QK_SCALE
            kk3, vv3, dead = _branch_keys(ks, vs, S, nb, g_idx)
            outs, lses = [], []
            for e in range(2):
                msk = head0 if e == 0 else jnp.logical_not(head0)
                qe = jnp.where(msk, q3, jnp.zeros_like(q3))
                s = _branch_scores(qe, kk3, b_ref, bi, e, dead)
                m = jnp.max(s, axis=-1, keepdims=True)
                p = jnp.exp(s - m)
                l = jnp.sum(p, axis=-1, keepdims=True)
                o = jnp.einsum("gqk,gke->gqe", p.astype(BF16), vv3, preferred_element_type=F32)
                outs.append(o / l)
                lses.append(jnp.broadcast_to(m + jnp.log(l), blk3))
            ot[...] = jnp.where(head0, outs[0], outs[1]).reshape(S, LANES)
            lt[...] = jnp.where(head0, lses[0], lses[1]).reshape(S, LANES)
            _ungroup(ot, 0, nat_o[bi], d, S, add=False)
            _ungroup(lt, 0, nat_l[bi], d, S, add=False)

        la, lb, lc = ln0[...], ln1[...], ln2[...]
        m = jnp.maximum(jnp.maximum(la, lb), lc)
        ea, eb, ec = jnp.exp(la - m), jnp.exp(lb - m), jnp.exp(lc - m)
        den = ea + eb + ec
        lse_ref[0] = m + jnp.log(den)
        o_ref[0] = (ea * on0[...] + eb * on1[...] + ec * on2[...]) / den
        bg_hook(True)

    blk = lambda off: pl.BlockSpec((1, S, LANES), lambda b, h: (b, 0, off + h))
    qv = qkv.reshape(B, S, 3 * AW)
    sub_f = pltpu.VMEM((S, LANES), F32)
    pad_b = pltpu.VMEM((S + ATTN_BLOCK, LANES), BF16)
    res = _hosted_call(
        body, bg, grid=(B, HP),
        in_specs=[blk(0), blk(HP), blk(2 * HP),
                  pl.BlockSpec((3, 2, ATTN_BLOCK, 2 * ATTN_BLOCK), lambda b, h: (0, h, 0, 0))],
        out_specs=[blk(0), blk(0)],
        out_shape=[jax.ShapeDtypeStruct((B, S, AW), F32)] * 2,
        scratch_shapes=[sub_f, pltpu.VMEM((S, LANES), BF16), pad_b, pad_b] + [sub_f] * 8,
        operands=[qv, qv, qv, bias_all], name="attention_fwd")
    return (res[0].reshape(B * S, AW), res[1].reshape(B * S, AW)) + tuple(res[2:])


def _attention_bwd(qkv, do, lse, dd, bias_all, B, S, AW, bg=None):
    HP = AW // LANES
    H = AW // HEAD_DIM
    G = S // ATTN_BLOCK
    blk3 = (G, ATTN_BLOCK, LANES)
    PAD = ATTN_BLOCK

    def body(refs, bg_hook):
        (q_ref, k_ref, v_ref, do_ref, lse_ref, dd_ref, b_ref,
         dq_ref, dk_ref, dv_ref, csq_ref, csk_ref, csv_ref, db_ref,
         stage, qs, ks, vs, gs, ls, ds_, tq, tk, tv, accq, acck, accv) = refs
        bg_hook(False)
        head0 = lax.broadcasted_iota(jnp.int32, (1, 1, LANES), 2) < HEAD_DIM
        g_idx = lax.broadcasted_iota(jnp.int32, (G, 1, 1), 0)
        first_b = pl.program_id(1) == 0

        @pl.when(first_b)
        def _():
            db_ref[...] = jnp.zeros_like(db_ref)

        ks[0:PAD, :] = jnp.zeros((PAD, LANES), BF16)
        vs[0:PAD, :] = jnp.zeros((PAD, LANES), BF16)
        tk[0:PAD, :] = jnp.zeros((PAD, LANES), F32)
        tv[0:PAD, :] = jnp.zeros((PAD, LANES), F32)
        for bi, (_, d) in enumerate(DILATED_CONFIGS):
            nb = S // d // ATTN_BLOCK
            _regroup(q_ref[0], stage, qs, d, S)
            _regroup(k_ref[0], stage, ks, d, S, PAD)
            _regroup(v_ref[0], stage, vs, d, S, PAD)
            _regroup(do_ref[0], stage, gs, d, S)
            _regroup(lse_ref[0], stage, ls, d, S)
            _regroup(dd_ref[0], stage, ds_, d, S)
            q3 = qs[...].reshape(blk3) * QK_SCALE
            do3 = gs[...].reshape(blk3)
            lse3 = ls[...].reshape(blk3)
            dd3 = ds_[...].reshape(blk3)
            kk3, vv3, dead = _branch_keys(ks, vs, S, nb, g_idx)
            dq = jnp.zeros(blk3, F32)
            dkk = jnp.zeros(kk3.shape, F32)
            dvv = jnp.zeros(kk3.shape, F32)
            for e in range(2):
                msk = head0 if e == 0 else jnp.logical_not(head0)
                c0 = e * HEAD_DIM
                qe = jnp.where(msk, q3, jnp.zeros_like(q3))
                doe = jnp.where(msk, do3, jnp.zeros_like(do3))
                ke = jnp.where(msk, kk3 * QK_SCALE, jnp.zeros_like(kk3))
                s = _branch_scores(qe, kk3, b_ref, bi, e, dead)
                p = jnp.exp(s - lse3[:, :, c0:c0 + 1])
                dp = jnp.einsum("gqe,gke->gqk", doe, vv3, preferred_element_type=F32)
                dsc = p * (dp - dd3[:, :, c0:c0 + 1])
                if dead is None:
                    db_ref[bi, e, :, ATTN_BLOCK:] += jnp.sum(dsc, axis=0)
                else:
                    db_ref[bi, e] += jnp.sum(dsc, axis=0)
                dsb = dsc.astype(BF16)
                dq = dq + jnp.einsum("gqk,gke->gqe", dsb, ke, preferred_element_type=F32)
                dkk = dkk + jnp.einsum("gqk,gqe->gke", dsb, qe, preferred_element_type=F32)
                dvv = dvv + jnp.einsum("gqk,gqe->gke", p.astype(BF16), doe, preferred_element_type=F32)
            tq[...] = dq.reshape(S, LANES)
            if dead is None:
                tk[PAD:PAD + S, :] = dkk.reshape(S, LANES)
                tv[PAD:PAD + S, :] = dvv.reshape(S, LANES)
            else:
                tk[PAD:PAD + S, :] = dkk[:, ATTN_BLOCK:, :].reshape(S, LANES)
                tv[PAD:PAD + S, :] = dvv[:, ATTN_BLOCK:, :].reshape(S, LANES)
                tk[0:S, :] += dkk[:, :ATTN_BLOCK, :].reshape(S, LANES)
                tv[0:S, :] += dvv[:, :ATTN_BLOCK, :].reshape(S, LANES)
            _ungroup(tq, 0, accq, d, S, add=bi > 0)
            _ungroup(tk, PAD, acck, d, S, add=bi > 0)
            _ungroup(tv, PAD, accv, d, S, add=bi > 0)

        for acc, out_ref, cs_ref in ((accq, dq_ref, csq_ref), (acck, dk_ref, csk_ref), (accv, dv_ref, csv_ref)):
            tot = acc[...]
            out_ref[0] = tot.astype(out_ref.dtype)
            _accumulate(cs_ref, first_b, jnp.sum(tot, axis=0, keepdims=True))
        bg_hook(True)

    blk = lambda off: pl.BlockSpec((1, S, LANES), lambda h, b: (b, 0, off + h))
    cs_spec = pl.BlockSpec((1, LANES), lambda h, b: (0, h))
    bias_spec = pl.BlockSpec((3, 2, ATTN_BLOCK, 2 * ATTN_BLOCK), lambda h, b: (0, h, 0, 0))
    qv = qkv.reshape(B, S, 3 * AW)
    view = lambda t: t.reshape(B, S, AW)
    sub_b = pltpu.VMEM((S, LANES), BF16)
    sub_f = pltpu.VMEM((S, LANES), F32)
    pad_b = pltpu.VMEM((S + PAD, LANES), BF16)
    pad_f = pltpu.VMEM((S + PAD, LANES), F32)
    res = _hosted_call(
        body, bg, grid=(HP, B),
        in_specs=[blk(0), blk(HP), blk(2 * HP), blk(0), blk(0), blk(0), bias_spec],
        out_specs=[blk(0), blk(0), blk(0), cs_spec, cs_spec, cs_spec, bias_spec],
        out_shape=[jax.ShapeDtypeStruct((B, S, AW), BF16)] * 3 + [jax.ShapeDtypeStruct((1, AW), F32)] * 3
        + [jax.ShapeDtypeStruct((3, H, ATTN_BLOCK, 2 * ATTN_BLOCK), F32)],
        scratch_shapes=[sub_f, sub_b, pad_b, pad_b, sub_b, sub_f, sub_f, sub_f, pad_f, pad_f, sub_f, sub_f, sub_f],
        operands=[qv, qv, qv, view(do), view(lse), view(dd), bias_all], name="attention_bwd")
    flat = lambda t: t.reshape(B * S, AW)
    return (flat(res[0]), flat(res[1]), flat(res[2]), res[3], res[4], res[5], res[6]) + tuple(res[7:])


class _RowShifts:
    def __init__(self, x, row, up):
        self.x, self.row, self.up, self.base = x, row, up, {0: x}

    def __call__(self, s):
        x = self.x
        n, c = x.shape
        r, whole = s % 8, s - s % 8
        if r not in self.base:
            if self.up:
                rolled = pltpu.roll(x, n - r, 0)
                tail = jnp.where(self.row[n - 8:] < n - r, rolled[n - 8:], 0.0)
                self.base[r] = jnp.concatenate([rolled[:n - 8], tail], axis=0)
            else:
                rolled = pltpu.roll(x, r, 0)
                head = jnp.where(self.row[:8] >= r, rolled[:8], 0.0)
                self.base[r] = jnp.concatenate([head, rolled[8:]], axis=0)
        y = self.base[r]
        if whole == 0:
            return y
        pad = jnp.zeros((whole, c), x.dtype)
        if self.up:
            return jnp.concatenate([y[whole:], pad], axis=0)
        return jnp.concatenate([pad, y[:n - whole]], axis=0)


def _conv_branch_fwd_math(a, g, w_ref, cb, lg, lb, row):
    sg = _sigmoid(g)
    u0 = a * sg
    u0_down = _RowShifts(u0, row, up=False)
    uc = jnp.zeros_like(u0) + cb
    for k in range(CONV_KERNEL):
        uc = uc + w_ref[k:k + 1, :] * u0_down(CONV_KERNEL - 1 - k)
    ul, xh, r = _ln_fwd(uc, lg, lb)
    su = _sigmoid(ul)
    u = ul * su
    return sg, u0_down, ul, xh, r, su, u


def _conv_fwd(ag, conv_w, conv_b, ln_g, ln_b, norm_g, B, S, CW):
    def body(a_ref, g_ref, w_ref, cb_ref, lg_ref, lb_ref, ng_ref, o_ref):
        row = lax.broadcasted_iota(jnp.int32, (S, CW), 0)
        _, _, _, _, _, _, u = _conv_branch_fwd_math(a_ref[0], g_ref[0], w_ref, cb_ref[...], lg_ref[...],
                                                    lb_ref[...], row)
        rr = lax.rsqrt(jnp.mean(u * u, axis=-1, keepdims=True) + LN_EPS)
        o_ref[0] = (u * rr * ng_ref[...]).astype(BF16)

    vec = pl.BlockSpec((1, CW), lambda b: (0, 0))
    out = pl.pallas_call(
        body, grid=(B,),
        in_specs=[pl.BlockSpec((1, S, CW), lambda b: (b, 0, 0)), pl.BlockSpec((1, S, CW), lambda b: (b, 0, 1)),
                  pl.BlockSpec((CONV_KERNEL, CW), lambda b: (0, 0)), vec, vec, vec, vec],
        out_specs=pl.BlockSpec((1, S, CW), lambda b: (b, 0, 0)),
        out_shape=jax.ShapeDtypeStruct((B, S, CW), BF16),
        compiler_params=_params(1), name="conv_fwd",
    )(ag.reshape(B, S, 2 * CW), ag.reshape(B, S, 2 * CW), conv_w, conv_b, ln_g, ln_b, norm_g)
    return out.reshape(B * S, CW)


def _conv_bwd(ag, dmc, conv_w, conv_b, ln_g, ln_b, norm_g, B, S, CW):
    def body(a_ref, g_ref, dm_ref, w_ref, cb_ref, lg_ref, lb_ref, ng_ref,
             dag_ref, dw_ref, dcb_ref, dlg_ref, dlb_ref, dng_ref):
        b = pl.program_id(0)
        row = lax.broadcasted_iota(jnp.int32, (S, CW), 0)
        a, g = a_ref[0], g_ref[0]
        sg, u0_down, ul, xh, r, su, u = _conv_branch_fwd_math(a, g, w_ref, cb_ref[...], lg_ref[...], lb_ref[...], row)
        rr = lax.rsqrt(jnp.mean(u * u, axis=-1, keepdims=True) + LN_EPS)
        dm = dm_ref[0]
        dxn = dm * ng_ref[...]
        du = rr * (dxn - u * (rr * rr) * jnp.mean(dxn * u, axis=-1, keepdims=True))
        dul = du * su * (1.0 + ul * (1.0 - su))
        duc = _ln_bwd(dul, xh, r, lg_ref[...])
        first = b == 0
        _accumulate(dng_ref, first, jnp.sum(dm * u * rr, axis=0, keepdims=True))
        _accumulate(dlg_ref, first, jnp.sum(dul * xh, axis=0, keepdims=True))
        _accumulate(dlb_ref, first, jnp.sum(dul, axis=0, keepdims=True))
        _accumulate(dcb_ref, first, jnp.sum(duc, axis=0, keepdims=True))

        @pl.when(first)
        def _():
            dw_ref[...] = jnp.zeros_like(dw_ref)

        duc_up = _RowShifts(duc, row, up=True)
        du0 = jnp.zeros_like(duc)
        for k in range(CONV_KERNEL):
            sh = CONV_KERNEL - 1 - k
            dw_ref[k:k + 1, :] += jnp.sum(duc * u0_down(sh), axis=0, keepdims=True)
            du0 = du0 + w_ref[k:k + 1, :] * duc_up(sh)
        dag_ref[0, :, :CW] = du0 * sg
        dag_ref[0, :, CW:] = du0 * a * sg * (1.0 - sg)

    vec = pl.BlockSpec((1, CW), lambda b: (0, 0))
    wspec = pl.BlockSpec((CONV_KERNEL, CW), lambda b: (0, 0))
    agv = ag.reshape(B, S, 2 * CW)
    res = pl.pallas_call(
        body, grid=(B,),
        in_specs=[pl.BlockSpec((1, S, CW), lambda b: (b, 0, 0)), pl.BlockSpec((1, S, CW), lambda b: (b, 0, 1)),
                  pl.BlockSpec((1, S, CW), lambda b: (b, 0, 0)), wspec, vec, vec, vec, vec],
        out_specs=[pl.BlockSpec((1, S, 2 * CW), lambda b: (b, 0, 0)), wspec, vec, vec, vec, vec],
        out_shape=[jax.ShapeDtypeStruct((B, S, 2 * CW), F32), jax.ShapeDtypeStruct((CONV_KERNEL, CW), F32)]
        + [jax.ShapeDtypeStruct((1, CW), F32)] * 4,
        compiler_params=_params(1), name="conv_bwd",
    )(agv, agv, dmc.reshape(B, S, CW), conv_w, conv_b, ln_g, ln_b, norm_g)
    return (res[0].reshape(B * S, 2 * CW),) + tuple(res[1:])


def _ffn_conv(x, w_ref, bias, row):
    down = x if isinstance(x, _RowShifts) else _RowShifts(x, row, up=False)
    y = jnp.zeros_like(down.x) + bias
    for k in range(FFN_CONV_KERNEL):
        y = y + w_ref[k:k + 1, :] * down(FFN_CONV_KERNEL - 1 - k)
    return y


def _ffn_specs(S, tc, nj, order):
    pick = (lambda b, j: (b, j)) if order == "bj" else (lambda j, b: (b, j))
    act = lambda off: pl.BlockSpec((1, S, tc), lambda *g: (pick(*g)[0], 0, off + pick(*g)[1]))
    cw = lambda off: pl.BlockSpec((FFN_CONV_KERNEL, tc), lambda *g: (0, off + pick(*g)[1]))
    cb = lambda off: pl.BlockSpec((1, tc), lambda *g: (0, off + pick(*g)[1]))
    return act, cw, cb


FFN_HALO = 16


def _half_sequences(S):
    if S < 8 * FFN_HALO:
        return [(0, S, 0, S)]
    h = S // 2
    return [(0, h + FFN_HALO, 0, h), (h - FFN_HALO, S, FFN_HALO, h)]


def _w_up_block_spec(w_up_sh, tc, off):
    _, D, cs = w_up_sh.shape
    assert cs % tc == 0
    bps = cs // tc
    return pl.BlockSpec((1, D, tc), lambda j: ((off + j) // bps, 0, (off + j) % bps))


def _ffn_fwd_fused(x1b, w_up_sh, cw, cb, B, S, DFF):
    tc = FFN_COLS
    nj = DFF // tc
    D = x1b.shape[1]

    def body(x_ref, wg_ref, wv_ref, cwg_ref, cwv_ref, cbg_ref, cbv_ref, o_ref, up_ref):
        w = jnp.concatenate([wg_ref[0], wv_ref[0]], axis=1)
        for b in range(B):
            for lo, hi, o0, on in _half_sequences(S):
                row = lax.broadcasted_iota(jnp.int32, (hi - lo, tc), 0)
                up = jnp.dot(x_ref[b, lo:hi, :], w, preferred_element_type=F32)
                up_ref[b, lo + o0:lo + o0 + on, :] = up[o0:o0 + on]
                gate = _ffn_conv(up[:, :tc], cwg_ref, cbg_ref[...], row)
                val = _ffn_conv(up[:, tc:], cwv_ref, cbv_ref[...], row)
                o_ref[b, lo + o0:lo + o0 + on, :] = (gate * _sigmoid(gate) * val).astype(BF16)[o0:o0 + on]

    cws = lambda off: pl.BlockSpec((FFN_CONV_KERNEL, tc), lambda j: (0, off + j))
    cbs = lambda off: pl.BlockSpec((1, tc), lambda j: (0, off + j))
    act, upre = pl.pallas_call(
        body, grid=(nj,),
        in_specs=[pl.BlockSpec((B, S, D), lambda j: (0, 0, 0), pipeline_mode=pl.Buffered(1)),
                  _w_up_block_spec(w_up_sh, tc, 0), _w_up_block_spec(w_up_sh, tc, nj),
                  cws(0), cws(nj), cbs(0), cbs(nj)],
        out_specs=[pl.BlockSpec((B, S, tc), lambda j: (0, 0, j)), pl.BlockSpec((B, S, 2 * tc), lambda j: (0, 0, j))],
        out_shape=[jax.ShapeDtypeStruct((B, S, DFF), BF16), jax.ShapeDtypeStruct((B, S, 2 * DFF), F32)],
        compiler_params=_params(1), name="ffn_fwd",
    )(x1b.reshape(B, S, D), w_up_sh, w_up_sh, cw, cw, cb, cb)
    return act.reshape(B * S, DFF), upre


def _ffn_bwd_fused(x1b, dz2b, upre, w_down, cw, cb, B, S, DFF):
    tc = FFN_COLS
    nj = DFF // tc
    D = x1b.shape[1]

    def body(x_ref, dz_ref, up_ref, wd_ref, cwg_ref, cwv_ref, cbg_ref, cbv_ref,
             dug_ref, duv_ref, dwu_ref, dwd_ref, dcw_ref, dcb_ref):
        first = pl.program_id(1) == 0
        dw_t = dwd = None
        dcb = [None, None]
        dcw = [[None] * FFN_CONV_KERNEL, [None] * FFN_CONV_KERNEL]
        add = lambda old, new: new if old is None else old + new
        for lo, hi, o0, on in _half_sequences(S):
            n = hi - lo
            own = slice(o0, o0 + on)
            row = lax.broadcasted_iota(jnp.int32, (n, tc), 0)
            x = x_ref[0, lo:hi, :]
            dz = dz_ref[0, lo:hi, :]
            ug = _RowShifts(up_ref[0, lo:hi, :tc], row, up=False)
            uv = _RowShifts(up_ref[0, lo:hi, tc:], row, up=False)
            gate = _ffn_conv(ug, cwg_ref, cbg_ref[...], row)
            val = _ffn_conv(uv, cwv_ref, cbv_ref[...], row)
            sg = _sigmoid(gate)
            act = (gate * sg * val).astype(BF16)
            dact = _dot(dz, wd_ref[...], "nt")
            dgate = dact * val * sg * (1.0 + gate * (1.0 - sg))
            dval = dact * gate * sg
            dupre = []
            for h, (dup, u_down, w_ref) in enumerate(((dgate, ug, cwg_ref), (dval, uv, cwv_ref))):
                dcb[h] = add(dcb[h], jnp.sum(dup[own], axis=0, keepdims=True))
                dup_up = _RowShifts(dup, row, up=True)
                acc = jnp.zeros_like(dup)
                for k in range(FFN_CONV_KERNEL):
                    sh = FFN_CONV_KERNEL - 1 - k
                    dcw[h][k] = add(dcw[h][k], jnp.sum((dup * u_down(sh))[own], axis=0, keepdims=True))
                    acc = acc + w_ref[k:k + 1, :] * dup_up(sh)
                dupre.append(acc.astype(BF16)[own])
            dug_ref[0, lo + o0:lo + o0 + on, :] = dupre[0]
            duv_ref[0, lo + o0:lo + o0 + on, :] = dupre[1]
            dw_t = add(dw_t, _dot(jnp.concatenate(dupre, axis=1), x[own], "tn"))
            dwd = add(dwd, _dot(act[own], dz[own], "tn"))
        _accumulate(dwu_ref.at[0], first, dw_t[:tc])
        _accumulate(dwu_ref.at[1], first, dw_t[tc:])
        _accumulate(dwd_ref, first, dwd)
        for h in range(2):
            _accumulate(dcb_ref.at[h], first, dcb[h])
            for k in range(FFN_CONV_KERNEL):
                _accumulate(dcw_ref.at[k, pl.ds(h, 1), :], first, dcw[h][k])

    act_s, cws, cbs = _ffn_specs(S, tc, nj, "jb")
    seq = pl.BlockSpec((1, S, D), lambda j, b: (b, 0, 0))
    res = pl.pallas_call(
        body, grid=(nj, B),
        in_specs=[seq, seq, pl.BlockSpec((1, S, 2 * tc), lambda j, b: (b, 0, j)),
                  pl.BlockSpec((tc, D), lambda j, b: (j, 0)), cws(0), cws(nj), cbs(0), cbs(nj)],
        out_specs=[act_s(0), act_s(0), pl.BlockSpec((2, tc, D), lambda j, b: (0, j, 0)),
                   pl.BlockSpec((tc, D), lambda j, b: (j, 0)),
                   pl.BlockSpec((FFN_CONV_KERNEL, 2, tc), lambda j, b: (0, 0, j)),
                   pl.BlockSpec((2, 1, tc), lambda j, b: (0, 0, j))],
        out_shape=[jax.ShapeDtypeStruct((B, S, DFF), BF16)] * 2
        + [jax.ShapeDtypeStruct((2, DFF, D), F32), jax.ShapeDtypeStruct((DFF, D), F32),
           jax.ShapeDtypeStruct((FFN_CONV_KERNEL, 2, DFF), F32), jax.ShapeDtypeStruct((2, 1, DFF), F32)],
        compiler_params=_params(2), name="ffn_bwd",
    )(x1b.reshape(B, S, D), dz2b.reshape(B, S, D), upre, w_down, cw, cw, cb, cb)
    flat = lambda t: t.reshape(B * S, DFF)
    return flat(res[0]), flat(res[1]), res[2], res[3], res[4], res[5]


def _dx1_ln1_bwd(dupre_g, dupre_v, w_up_sh, dz2, xh1, r1, ln1_g, tm, bg):
    T, D = dz2.shape
    NS, _, cs = w_up_sh.shape
    half = NS // 2
    DFF = dupre_g.shape[1]

    def body(refs, bg_hook):
        dug_ref, duv_ref, w_ref, dz2_ref, xh_ref, r_ref, g_ref, dz_ref, dzb_ref, dg_ref, db_ref = refs
        bg_hook(False)
        first = pl.program_id(0) == 0
        dg = db = None
        for rows in (slice(0, tm // 2), slice(tm // 2, tm)):
            dx1 = ALPHA * dz2_ref[rows, :]
            for k in range(NS):
                src = dug_ref if k < half else duv_ref
                c0 = (k % half) * cs
                dx1 = dx1 + _dot(src[rows, c0:c0 + cs], w_ref[k], "nt")
            xh = xh_ref[rows, :]
            dz = _ln_bwd(dx1, xh, r_ref[rows, 0:1], g_ref[...])
            dz_ref[rows, :] = dz
            dzb_ref[rows, :] = dz.astype(BF16)
            dg_h, db_h = jnp.sum(dx1 * xh, axis=0, keepdims=True), jnp.sum(dx1, axis=0, keepdims=True)
            dg, db = (dg_h, db_h) if dg is None else (dg + dg_h, db + db_h)
        _accumulate(dg_ref, first, dg)
        _accumulate(db_ref, first, db)
        bg_hook(True)

    row = pl.BlockSpec((tm, D), lambda i: (i, 0))
    vec = pl.BlockSpec((1, D), lambda i: (0, 0))
    du = pl.BlockSpec((tm, DFF), lambda i: (i, 0))
    return _hosted_call(
        body, bg, grid=(T // tm,),
        in_specs=[du, du, pl.BlockSpec((NS, D, cs), lambda i: (0, 0, 0), pipeline_mode=pl.Buffered(1)),
                  row, row, pl.BlockSpec((tm, LANES), lambda i: (i, 0)), vec],
        out_specs=[row, row, vec, vec],
        out_shape=[jax.ShapeDtypeStruct((T, D), F32), jax.ShapeDtypeStruct((T, D), BF16),
                   jax.ShapeDtypeStruct((1, D), F32), jax.ShapeDtypeStruct((1, D), F32)],
        scratch_shapes=[], operands=[dupre_g, dupre_v, w_up_sh, dz2, xh1, r1, ln1_g], name="mm_dx1_ln1_bwd")


def _dh_cat(dq, dk, dv, dag, tm):
    T, AW = dq.shape
    CW2 = dag.shape[1]
    W = 3 * AW + CW2

    def body(dq_ref, dk_ref, dv_ref, dag_ref, dh_ref, cs_ref):
        for c, ref in enumerate((dq_ref, dk_ref, dv_ref)):
            dh_ref[:, c * AW:(c + 1) * AW] = ref[...]
        dg = dag_ref[...]
        dh_ref[:, 3 * AW:] = dg.astype(BF16)
        _accumulate(cs_ref, pl.program_id(0) == 0, jnp.sum(dg, axis=0, keepdims=True))

    row = pl.BlockSpec((tm, AW), lambda i: (i, 0))
    return pl.pallas_call(
        body, grid=(T // tm,),
        in_specs=[row] * 3 + [pl.BlockSpec((tm, CW2), lambda i: (i, 0))],
        out_specs=[pl.BlockSpec((tm, W), lambda i: (i, 0)), pl.BlockSpec((1, CW2), lambda i: (0, 0))],
        out_shape=[jax.ShapeDtypeStruct((T, W), BF16), jax.ShapeDtypeStruct((1, CW2), F32)],
        compiler_params=_params(1), name="dh_cat",
    )(dq, dk, dv, dag)


def _local_step(x, target, rel_table, w_in_t, b_in, conv_w, conv_b, conv_ln_g, conv_ln_b, attn_norm_g,
                conv_norm_g, staged, ln1_g, ln1_b, ffn_cw, ffn_cb, ln2_g, ln2_b, ids):
    B, S, D = x.shape
    T = B * S
    AW = attn_norm_g.shape[-1]
    CW = conv_norm_g.shape[-1]
    H = AW // HEAD_DIM
    DFF = staged[2].shape[0] * staged[2].shape[1]
    INW = 3 * AW + 2 * CW
    xf = x.reshape(T, D)
    tf = target.reshape(T, D)
    tm = _row_tile(T, 512)
    tm_s = tm

    bucket_np, mask_np = _bucket_tables()
    bucket = jnp.asarray(bucket_np)
    band_mask = jnp.asarray(mask_np)
    bias_all = _bias_build(rel_table.T, bucket, band_mask).reshape(3, H, ATTN_BLOCK, 2 * ATTN_BLOCK)

    rowD = lambda i, j, k: (i, 0)
    vecD = lambda i, j, k: (0, 0)

    def in_proj_epilogue(acc, i, j, extra_refs, out_refs):
        h = acc + extra_refs[0][...]
        out_refs[0][...] = h[:, :3 * AW].astype(BF16)
        out_refs[1][...] = h[:, 3 * AW:]

    qkv, ag = _matmul_general(
        [(xf, (tm, D), rowD), (w_in_t, (INW, D), vecD)],
        lambda refs, i, j, k: _dot(refs[0][...], refs[1][...], "nt"),
        grid=(T // tm, 1, 1), tm=tm, tn=INW, extras=[(b_in, (1, INW), vecD)],
        outs=[((T, 3 * AW), BF16, (tm, 3 * AW), rowD), ((T, 2 * CW), F32, (tm, 2 * CW), rowD)],
        epilogue=in_proj_epilogue, name="mm_in")

    attn, lse, w_out_g, w_up_sh, w_down_g = _attention_fwd(qkv, bias_all, B, S, AW, bg=_bg_gather(staged))
    w_out = w_out_g.reshape(D, D)
    w_down = w_down_g.reshape(DFF, D)
    mixed_c = _conv_fwd(ag, conv_w, conv_b, conv_ln_g, conv_ln_b, conv_norm_g, B, S, CW)

    def attn_rstd(a):
        return lax.rsqrt(jnp.mean(a * a, axis=-1, keepdims=True) + LN_EPS)

    def mixed_rows(attn_ref, mc_ref, gain_ref, rows=slice(None)):
        a = attn_ref[rows, :]
        return jnp.concatenate([(a * attn_rstd(a) * gain_ref[...]).astype(BF16), mc_ref[rows, :]], axis=1)

    halves = [slice(0, tm // 2), slice(tm // 2, tm)]

    def ln1_epilogue(parts, i, j, extra_refs, out_refs):
        x_ref, g_ref, b_ref, a_ref = extra_refs
        for rows, acc in zip(halves, parts):
            x1, xh, r = _ln_fwd(acc + ALPHA * x_ref[rows, :], g_ref[...], b_ref[...])
            out_refs[0][rows, :] = x1
            out_refs[1][rows, :] = x1.astype(BF16)
            out_refs[2][rows, :] = xh
            out_refs[3][rows, :] = jnp.broadcast_to(r, (tm // 2, LANES))
            out_refs[4][rows, :] = jnp.broadcast_to(attn_rstd(a_ref[rows, :]), (tm // 2, LANES))

    x1, x1b, xh1, r1, r_attn = _matmul_general(
        [(attn, (tm_s, AW), rowD), (mixed_c, (tm_s, CW), rowD), (attn_norm_g, (1, AW), vecD), (w_out, (D, D), vecD)],
        lambda refs, i, j, k: tuple(_dot(mixed_rows(refs[0], refs[1], refs[2], rows), refs[3][...], "nn")
                                    for rows in halves),
        grid=(T // tm_s, 1, 1), tm=tm_s, tn=D,
        extras=[(xf, (tm_s, D), rowD), (ln1_g, (1, D), vecD), (ln1_b, (1, D), vecD), (attn, (tm_s, AW), rowD)],
        outs=[((T, D), F32, (tm_s, D), rowD), ((T, D), BF16, (tm_s, D), rowD), ((T, D), F32, (tm_s, D), rowD),
              ((T, LANES), F32, (tm_s, LANES), rowD), ((T, LANES), F32, (tm_s, LANES), rowD)],
        epilogue=ln1_epilogue, name="mm_out_ln1")

    NS, _, cs = w_up_sh.shape
    half = NS // 2

    act, upre = _ffn_fwd_fused(x1b, w_up_sh, ffn_cw, ffn_cb, B, S, DFF)

    def ln2_epilogue(parts, i, j, extra_refs, out_refs):
        x1_ref, g_ref, b_ref, t_ref = extra_refs
        dz_ref, dzb_ref, loss_ref, dg_ref, db_ref = out_refs
        g = g_ref[...]
        sums = None
        for rows, acc in zip(halves, parts):
            y, xh, r = _ln_fwd(acc + ALPHA * x1_ref[rows, :], g, b_ref[...])
            diff = y - t_ref[rows, :]
            row_loss = jnp.sum(diff * diff, axis=1, keepdims=True)
            tile_loss = jnp.sum(row_loss, axis=0, keepdims=True) * (0.5 / D)
            dy = diff * (1.0 / D)
            dz = _ln_bwd(dy, xh, r, g)
            dz_ref[rows, :] = dz
            dzb_ref[rows, :] = dz.astype(BF16)
            vals = (jnp.broadcast_to(tile_loss, (1, LANES)), jnp.sum(dy * xh, axis=0, keepdims=True),
                    jnp.sum(dy, axis=0, keepdims=True))
            sums = vals if sums is None else tuple(a + b for a, b in zip(sums, vals))
        for ref, val in zip((loss_ref, dg_ref, db_ref), sums):
            _accumulate(ref, i == 0, val)

    dz2, dz2b, loss_part, d_ln2_g, d_ln2_b = _matmul_general(
        [(act, (tm, DFF), rowD), (w_down, (DFF, D), vecD)],
        lambda refs, i, j, k: tuple(_dot(refs[0][rows, :], refs[1][...], "nn") for rows in halves),
        grid=(T // tm, 1, 1), tm=tm, tn=D,
        extras=[(x1, (tm, D), rowD), (ln2_g, (1, D), vecD), (ln2_b, (1, D), vecD), (tf, (tm, D), rowD)],
        outs=[((T, D), F32, (tm, D), rowD), ((T, D), BF16, (tm, D), rowD),
              ((1, LANES), F32, (1, LANES), vecD), ((1, D), F32, (1, D), vecD), ((1, D), F32, (1, D), vecD)],
        epilogue=ln2_epilogue, name="mm_down_ln2_loss")

    dupre_g, dupre_v, d_w_up_t, d_w_down, d_ffn_cw2, d_ffn_cb2 = _ffn_bwd_fused(
        x1b, dz2b, upre, w_down, ffn_cw, ffn_cb, B, S, DFF)
    d_w_up_t = d_w_up_t.reshape(NS, cs, D)
    d_ffn_cw = d_ffn_cw2.reshape(FFN_CONV_KERNEL, 2 * DFF)
    d_ffn_cb = d_ffn_cb2.reshape(1, 2 * DFF)
    tk_t = _row_tile(T, 512)

    early = [d_w_up_t, d_w_down.reshape(NS, DFF // NS, D)]
    dz1, dz1b, d_ln1_g, d_ln1_b, *sib_e = _dx1_ln1_bwd(dupre_g, dupre_v, w_up_sh, dz2, xh1, r1, ln1_g, tm,
                                                       bg=_bg_sibling_exchange(early))
    chip_e = [_pair_sum(g, s, ids, name="pair_sum_" + n) for g, s, n in zip(early, sib_e, ("w_up", "w_down"))]

    def dw_out_epilogue(acc, i, j, extra_refs, out_refs):
        out_refs[0][...] = acc

    d_w_out = _matmul_general(
        [(attn, (tk_t, AW), lambda i, j, k: (k, 0)), (mixed_c, (tk_t, CW), lambda i, j, k: (k, 0)),
         (attn_norm_g, (1, AW), vecD), (dz1b, (tk_t, D), lambda i, j, k: (k, 0))],
        lambda refs, i, j, k: _dot(mixed_rows(refs[0], refs[1], refs[2]), refs[3][...], "tn"),
        grid=(1, 1, T // tk_t), tm=D, tn=D, outs=[_plain_out(D, D, D, D, F32)],
        epilogue=dw_out_epilogue, name="mm_dw_out")[0]
    early.append(d_w_out.reshape(NS, D // NS, D))
    def dmixed_epilogue(parts, i, j, extra_refs, out_refs):
        a_ref, r_ref, g_ref = extra_refs
        do_ref, dd_ref, dmc_ref, dg_ref = out_refs
        head_of = lambda axis: lax.broadcasted_iota(jnp.int32, (AW, AW), axis) // HEAD_DIM
        same_head = (head_of(0) == head_of(1)).astype(BF16)
        dg = None
        for rows, acc in zip(halves, parts):
            dm = acc[:, :AW]
            dmc_ref[rows, :] = acc[:, AW:]
            a = a_ref[rows, :]
            r = r_ref[rows, 0:1]
            dxn = dm * g_ref[...]
            da = r * (dxn - a * (r * r) * jnp.mean(dxn * a, axis=-1, keepdims=True))
            do_ref[rows, :] = da.astype(BF16)
            hi, lo = _split_hi_lo(da * a)
            dd_ref[rows, :] = (jnp.dot(hi, same_head, preferred_element_type=F32)
                               + jnp.dot(lo, same_head, preferred_element_type=F32))
            dg_h = jnp.sum(dm * a * r, axis=0, keepdims=True)
            dg = dg_h if dg is None else dg + dg_h
        _accumulate(dg_ref, i == 0, dg)

    dattn, dd, dmc, d_attn_norm_g, sib_out = _matmul_general(
        [(dz1b, (tm, D), rowD), (w_out, (D, D), vecD)],
        lambda refs, i, j, k: tuple(_dot(refs[0][rows, :], refs[1][...], "nt") for rows in halves),
        grid=(T // tm, 1, 1), tm=tm, tn=D,
        extras=[(attn, (tm, AW), rowD), (r_attn, (tm, LANES), rowD), (attn_norm_g, (1, AW), vecD)],
        outs=[((T, AW), BF16, (tm, AW), rowD), ((T, AW), F32, (tm, AW), rowD), ((T, CW), F32, (tm, CW), rowD),
              ((1, AW), F32, (1, AW), vecD)],
        epilogue=dmixed_epilogue, name="mm_dmixed", bg=_bg_sibling_exchange(early[2:]))
    sib_e.append(sib_out)
    chip_e.append(_pair_sum(early[2], sib_out, ids, name="pair_sum_w_out"))

    dag, d_conv_w, d_conv_b, d_conv_ln_g, d_conv_ln_b, d_conv_norm_g = _conv_bwd(
        ag, dmc, conv_w, conv_b, conv_ln_g, conv_ln_b, conv_norm_g, B, S, CW)

    dq, dk, dv, csq, csk, csv, dbias, *got_e = _attention_bwd(qkv, dattn, lse, dd, bias_all, B, S, AW,
                                                              bg=_bg_chip_exchange(chip_e))
    full_up, full_down, full_out = [_final_sum(g, s, r, ids, name="final_sum_" + n)
                                    for g, s, r, n in zip(early, sib_e, got_e, ("w_up", "w_down", "w_out"))]
    d_rel_table = _rel_grad(dbias.reshape(3, H, ATTN_BLOCK * 2 * ATTN_BLOCK), bucket).T
    dh, cs_ag = _dh_cat(dq, dk, dv, dag, tm_s)
    d_b_in = jnp.concatenate([csq, csk, csv, cs_ag], axis=1)

    d_w_in_t = _mm_plain(dh, xf, mode="tn", tm=_col_tile(INW, 1408), tn=D, tk=tk_t, out_dtype=F32, name="mm_dw_in")
    late = [d_w_in_t.reshape(NS, INW // NS, D)]
    sib_l = _sibling_exchange(late)
    chip_l = [_pair_sum(late[0], sib_l[0], ids, name="pair_sum_w_in")]
    small = dict(rel_table=d_rel_table, b_in=d_b_in, conv_w=d_conv_w, conv_b=d_conv_b, conv_ln_g=d_conv_ln_g,
                 conv_ln_b=d_conv_ln_b, attn_norm_g=d_attn_norm_g, conv_norm_g=d_conv_norm_g, ln1_g=d_ln1_g,
                 ln1_b=d_ln1_b, ffn_conv_w=d_ffn_cw, ffn_conv_b=d_ffn_cb, ln2_g=d_ln2_g, ln2_b=d_ln2_b)
    pack = _pack([loss_part] + [small[n] for n in SMALL_NAMES])

    def gx_epilogue(acc, i, j, extra_refs, out_refs):
        out_refs[0][...] = acc + ALPHA * extra_refs[0][...]

    grad_x, got_in, all_packs = _matmul(
        dh, w_in_t, mode="nn", tm=tm, tn=D, tk=INW, extras=[(dz1, (tm, D), rowD)],
        outs=[((T, D), F32, (tm, D), rowD)], epilogue=gx_epilogue, name="mm_grad_x",
        bg=_bg_chip_exchange(chip_l, pack))
    full_in = _final_sum(late[0], sib_l[0], got_in, ids, name="final_sum_w_in")
    return grad_x.reshape(B, S, D), [full_in, full_out, full_up, full_down], all_packs


def _place():
    return lax.axis_index("x"), lax.axis_index("y"), lax.axis_index("c")


CHIP_FLIPS = ((1, 0), (0, 1), (1, 1))


def _flip(v, f):
    return 1 - v if f else v


HBM_SPEC = pl.BlockSpec(memory_space=pl.ANY)
VMEM_SPEC = pl.BlockSpec(memory_space=pltpu.VMEM)
COMM_PARAMS = pltpu.CompilerParams(vmem_limit_bytes=VMEM_LIMIT)


def _gather_weights(big, small):
    nb, ns = len(big), len(small)

    def body(*refs):
        big_in = refs[:nb]
        small_in = refs[nb:nb + ns]
        big_out = refs[nb + ns:2 * nb + ns]
        small_out = refs[2 * nb + ns:2 * nb + 2 * ns]
        stages = refs[2 * nb + 2 * ns:3 * nb + 2 * ns]
        send_sems, recv_sems, local_sems = refs[3 * nb + 2 * ns:]
        x, y, c = _place()
        s_me = 2 * x + y
        sibling = (x, y, 1 - c)
        started, local_copies = [], []
        for a in range(nb):
            rh = big[a].shape[0] // 2
            lo = pl.multiple_of(c * rh, 16)
            stages[a][...] = big_in[a][pl.ds(lo, rh), :].astype(BF16)
            mine = big_out[a].at[s_me, pl.ds(lo, rh), :]
            loc = pltpu.make_async_copy(stages[a], mine, local_sems.at[a])
            loc.start()
            local_copies.append(loc)
            targets = [sibling] + [(_flip(x, fx), _flip(y, fy), c) for fx, fy in CHIP_FLIPS]
            for k, to in enumerate(targets):
                cp = pltpu.make_async_remote_copy(stages[a], mine, send_sems.at[a * 7 + k],
                                                  recv_sems.at[a * 7 + k], device_id=to, device_id_type=MESH)
                cp.start()
                started.append(cp)
        for a in range(ns):
            mine = small_out[a].at[s_me]
            loc = pltpu.make_async_copy(small_in[a], mine, local_sems.at[nb + a])
            loc.start()
            local_copies.append(loc)
            for k, (fx, fy) in enumerate(CHIP_FLIPS):
                cp = pltpu.make_async_remote_copy(small_in[a], mine, send_sems.at[nb * 7 + a * 3 + k],
                                                  recv_sems.at[nb * 7 + a * 3 + k],
                                                  device_id=(_flip(x, fx), _flip(y, fy), c), device_id_type=MESH)
                cp.start()
                started.append(cp)
        for a in range(nb):
            rh = big[a].shape[0] // 2
            lo = pl.multiple_of(c * rh, 16)
            for k, (fx, fy) in enumerate(CHIP_FLIPS):
                s_from = 2 * _flip(x, fx) + _flip(y, fy)
                got = big_out[a].at[s_from, pl.ds(lo, rh), :]
                pltpu.make_async_remote_copy(got, got, send_sems.at[a * 7 + 1 + k], recv_sems.at[a * 7 + 1 + k],
                                             device_id=sibling, device_id_type=MESH).wait_recv()
                fwd = pltpu.make_async_remote_copy(got, got, send_sems.at[a * 7 + 4 + k],
                                                   recv_sems.at[a * 7 + 4 + k], device_id=sibling,
                                                   device_id_type=MESH)
                fwd.start()
                started.append(fwd)
        for a in range(nb):
            rh = big[a].shape[0] // 2
            lo_sib = pl.multiple_of((1 - c) * rh, 16)
            for k in (0, 4, 5, 6):
                any_rows = big_out[a].at[s_me, pl.ds(lo_sib, rh), :]
                pltpu.make_async_remote_copy(any_rows, any_rows, send_sems.at[a * 7 + k], recv_sems.at[a * 7 + k],
                                             device_id=sibling, device_id_type=MESH).wait_recv()
        for a in range(ns):
            for k in range(3):
                pltpu.make_async_remote_copy(small_in[a], small_out[a].at[s_me], send_sems.at[nb * 7 + a * 3 + k],
                                             recv_sems.at[nb * 7 + a * 3 + k], device_id=sibling,
                                             device_id_type=MESH).wait_recv()
        for cp in started:
            cp.wait_send()
        for cp in local_copies:
            cp.wait()

    n_sem = nb * 7 + ns * 3
    out_shape = ([jax.ShapeDtypeStruct((N_SHARDS,) + w.shape, BF16) for w in big]
                 + [jax.ShapeDtypeStruct((N_SHARDS,) + w.shape, F32) for w in small])
    res = pl.pallas_call(
        body, in_specs=[VMEM_SPEC] * nb + [HBM_SPEC] * ns, out_specs=[HBM_SPEC] * (nb + ns),
        out_shape=out_shape,
        scratch_shapes=[pltpu.VMEM((w.shape[0] // 2, w.shape[1]), BF16) for w in big]
        + [pltpu.SemaphoreType.DMA((n_sem,)), pltpu.SemaphoreType.DMA((n_sem,)),
           pltpu.SemaphoreType.DMA((nb + ns,))],
        compiler_params=COMM_PARAMS, name="gather_weights",
    )(*big, *small)
    return res[:nb], res[nb:]


def _sibling_exchange(grads):
    n = len(grads)

    def body(*refs):
        g_in = refs[:n]
        got = refs[n:2 * n]
        send_sems, recv_sems = refs[2 * n:]
        x, y, c = _place()
        cps = []
        for a in range(n):
            rh = grads[a].shape[1] // 2
            lo = pl.multiple_of((1 - c) * rh, 8)
            cp = pltpu.make_async_remote_copy(g_in[a].at[:, pl.ds(lo, rh), :], got[a], send_sems.at[a],
                                              recv_sems.at[a], device_id=(x, y, 1 - c), device_id_type=MESH)
            cp.start()
            cps.append(cp)
        for cp in cps:
            cp.wait()

    return pl.pallas_call(
        body, in_specs=[HBM_SPEC] * n, out_specs=[HBM_SPEC] * n,
        out_shape=[jax.ShapeDtypeStruct((N_SHARDS, g.shape[1] // 2, g.shape[2]), F32) for g in grads],
        scratch_shapes=[pltpu.SemaphoreType.DMA((n,)), pltpu.SemaphoreType.DMA((n,))],
        compiler_params=COMM_PARAMS, name="sibling_exchange",
    )(*grads)


def _sibling_assemble(fulls):
    n = len(fulls)

    def body(*refs):
        full = refs[n:2 * n]
        send_sems, recv_sems = refs[2 * n:]
        x, y, c = _place()
        cps = []
        for a in range(n):
            rh = fulls[a].shape[0] // 2
            mine = full[a].at[pl.ds(pl.multiple_of(c * rh, 8), rh), :]
            cp = pltpu.make_async_remote_copy(mine, mine, send_sems.at[a], recv_sems.at[a],
                                              device_id=(x, y, 1 - c), device_id_type=MESH)
            cp.start()
            cps.append(cp)
        for cp in cps:
            cp.wait()

    return pl.pallas_call(
        body, in_specs=[HBM_SPEC] * n, out_specs=[HBM_SPEC] * n,
        out_shape=[jax.ShapeDtypeStruct(f.shape, F32) for f in fulls],
        input_output_aliases={a: a for a in range(n)},
        scratch_shapes=[pltpu.SemaphoreType.DMA((n,)), pltpu.SemaphoreType.DMA((n,))],
        compiler_params=COMM_PARAMS, name="sibling_assemble",
    )(*fulls)


def _remote(ref_src, ref_dst, send_sems, recv_sems, k, to):
    return pltpu.make_async_remote_copy(ref_src, ref_dst, send_sems.at[k], recv_sems.at[k], device_id=to,
                                        device_id_type=MESH)


def _stage_half(w, ids, name):
    R, C = w.shape
    rh = R // 2
    rt = _half_tile(rh)
    nt = rh // rt

    def body(ids_ref, w_ref, o_ref):
        o_ref[0] = w_ref[...].astype(BF16)

    grid_spec = pltpu.PrefetchScalarGridSpec(
        num_scalar_prefetch=1, grid=(nt,),
        in_specs=[pl.BlockSpec((rt, C), lambda i, ids: (ids[2] * nt + i, 0))],
        out_specs=pl.BlockSpec((1, rt, C), lambda i, ids: (2 * ids[0] + ids[1], ids[2] * nt + i, 0)))
    return pl.pallas_call(body, grid_spec=grid_spec, out_shape=jax.ShapeDtypeStruct((N_SHARDS, R, C), BF16),
                          compiler_params=_params(1), name=name)(ids, w)


def _bg_gather(staged):
    n = len(staged)

    def run(step, n_steps, ins, outs, send_sems, recv_sems, local_sems, post):
        x, y, c = _place()
        s_me = 2 * x + y
        sibling = (x, y, 1 - c)
        chips = [(_flip(x, fx), _flip(y, fy)) for fx, fy in CHIP_FLIPS]

        def rows(a, s, half):
            rh = staged[a].shape[1] // 2
            return outs[a].at[s, pl.ds(pl.multiple_of(half * rh, 16), rh), :]

        def copy(a, k, ref, to):
            return _remote(ref, ref, send_sems, recv_sems, a * 7 + k, to)

        if not post:
            @pl.when(step == 0)
            def _():
                for a in range(n):
                    mine = rows(a, s_me, c)
                    copy(a, 0, mine, sibling).start()
                    for k, (px, py) in enumerate(chips):
                        copy(a, 1 + k, mine, (px, py, c)).start()

            @pl.when(step == max(n_steps - 2, 0))
            def _():
                for a in range(n):
                    for k, (px, py) in enumerate(chips):
                        got = rows(a, 2 * px + py, c)
                        copy(a, 1 + k, got, sibling).wait_recv()
                        copy(a, 4 + k, got, sibling).start()
        else:
            @pl.when(step == n_steps - 1)
            def _():
                for a in range(n):
                    for k in (0, 4, 5, 6):
                        copy(a, k, rows(a, s_me, 1 - c), sibling).wait_recv()
                    for k in range(7):
                        copy(a, k, rows(a, s_me, c), sibling).wait_send()

    return _Background(staged, [jax.ShapeDtypeStruct(g.shape, g.dtype) for g in staged],
                       {a: a for a in range(n)}, 7 * n, run)


def _bg_sibling_exchange(grads):
    n = len(grads)

    def run(step, n_steps, ins, outs, send_sems, recv_sems, local_sems, post):
        x, y, c = _place()

        def copy(a):
            rh = grads[a].shape[1] // 2
            lo = pl.multiple_of((1 - c) * rh, 8)
            return _remote(ins[a].at[:, pl.ds(lo, rh), :], outs[a], send_sems, recv_sems, a, (x, y, 1 - c))

        if not post:
            @pl.when(step == 0)
            def _():
                for a in range(n):
                    copy(a).start()
        else:
            @pl.when(step == n_steps - 1)
            def _():
                for a in range(n):
                    copy(a).wait()

    return _Background(grads, [jax.ShapeDtypeStruct((N_SHARDS, g.shape[1] // 2, g.shape[2]), F32) for g in grads],
                       {}, n, run)


def _bg_chip_exchange(chip_parts, pack=None):
    n = len(chip_parts)

    def run(step, n_steps, ins, outs, send_sems, recv_sems, local_sems, post):
        x, y, c = _place()
        me = 4 * x + 2 * y + c

        def copies():
            cps = []
            for a in range(n):
                for k, (fx, fy) in enumerate(CHIP_FLIPS):
                    px, py = _flip(x, fx), _flip(y, fy)
                    cps.append(_remote(ins[a].at[2 * px + py], outs[a].at[k], send_sems, recv_sems, a * 3 + k,
                                       (px, py, c)))
            if pack is not None:
                for m in range(1, N_DEV):
                    to = (_flip(x, m & 4), _flip(y, m & 2), _flip(c, m & 1))
                    cps.append(_remote(ins[n], outs[n].at[me], send_sems, recv_sems, n * 3 + m - 1, to))
            return cps

        def local():
            return pltpu.make_async_copy(ins[n], outs[n].at[me], local_sems.at[0])

        if not post:
            @pl.when(step == 0)
            def _():
                for cp in copies():
                    cp.start()
                if pack is not None:
                    local().start()
        else:
            @pl.when(step == n_steps - 1)
            def _():
                for cp in copies():
                    cp.wait()
                if pack is not None:
                    local().wait()

    in_arrays = list(chip_parts) + ([pack] if pack is not None else [])
    out_shapes = [jax.ShapeDtypeStruct((3,) + p.shape[1:], BF16) for p in chip_parts]
    if pack is not None:
        out_shapes.append(jax.ShapeDtypeStruct((N_DEV, pack.shape[0], LANES), F32))
    return _Background(in_arrays, out_shapes, {}, n * 3 + N_DEV - 1, run)


def _half_tile(rh, mult=16, want=256):
    best = None
    for t in range(mult, min(rh, want) + 1, mult):
        if rh % t == 0:
            best = t
    return best if best is not None else rh


def _pair_sum(g, sib, ids, name):
    _, R, C = g.shape
    rh = R // 2
    rt = _half_tile(rh)
    nt = rh // rt

    def body(ids_ref, g_ref, s_ref, o_ref):
        o_ref[...] = (g_ref[...] + s_ref[...]).astype(BF16)

    def other(j, ids):
        return j + (j >= 2 * ids[0] + ids[1]).astype(jnp.int32)

    grid_spec = pltpu.PrefetchScalarGridSpec(
        num_scalar_prefetch=1, grid=(N_SHARDS - 1, nt),
        in_specs=[pl.BlockSpec((1, rt, C), lambda j, i, ids: (other(j, ids), ids[2] * nt + i, 0)),
                  pl.BlockSpec((1, rt, C), lambda j, i, ids: (other(j, ids), i, 0))],
        out_specs=pl.BlockSpec((1, rt, C), lambda j, i, ids: (other(j, ids), i, 0)))
    return pl.pallas_call(body, grid_spec=grid_spec, out_shape=jax.ShapeDtypeStruct((N_SHARDS, rh, C), BF16),
                          compiler_params=_params(2), name=name)(ids, g, sib)


def _final_sum(g, sib, got, ids, name):
    _, R, C = g.shape
    rh = R // 2
    rt = _half_tile(rh)
    nt = rh // rt

    def body(ids_ref, g_ref, s_ref, r_ref, o_ref):
        tot = g_ref[0] + s_ref[0]
        for k in range(3):
            tot = tot + r_ref[k].astype(F32)
        o_ref[...] = tot

    grid_spec = pltpu.PrefetchScalarGridSpec(
        num_scalar_prefetch=1, grid=(nt,),
        in_specs=[pl.BlockSpec((1, rt, C), lambda i, ids: (2 * ids[0] + ids[1], ids[2] * nt + i, 0)),
                  pl.BlockSpec((1, rt, C), lambda i, ids: (2 * ids[0] + ids[1], i, 0)),
                  pl.BlockSpec((3, rt, C), lambda i, ids: (0, i, 0))],
        out_specs=pl.BlockSpec((rt, C), lambda i, ids: (ids[2] * nt + i, 0)))
    return pl.pallas_call(body, grid_spec=grid_spec, out_shape=jax.ShapeDtypeStruct((R, C), F32),
                          compiler_params=_params(1), name=name)(ids, g, sib, got)


def _sum_packs(all_packs):
    def body(p_ref, o_ref):
        tot = p_ref[0]
        for i in range(1, N_DEV):
            tot = tot + p_ref[i]
        o_ref[...] = tot

    return pl.pallas_call(body, in_specs=[VMEM_SPEC], out_specs=VMEM_SPEC,
                          out_shape=jax.ShapeDtypeStruct(all_packs.shape[1:], F32), name="sum_packs")(all_packs)


def _adamw(w, g, m, v, name, g_transposed=False):
    R, C = w.shape
    rt = _half_tile(R, mult=LANES if g_transposed else 8, want=256)

    def body(w_ref, g_ref, m_ref, v_ref, g_out_ref, d_ref, nm_ref, nv_ref):
        gg = g_ref[...].T if g_transposed else g_ref[...]
        g_out_ref[...] = gg
        d_ref[...], nm_ref[...], nv_ref[...] = _adamw_update(w_ref[...], gg, m_ref[...], v_ref[...])

    spec = pl.BlockSpec((rt, C), lambda i: (i, 0))
    g_spec = pl.BlockSpec((C, rt), lambda i: (0, i)) if g_transposed else spec
    return pl.pallas_call(body, grid=(R // rt,), in_specs=[spec, g_spec, spec, spec], out_specs=[spec] * 4,
                          out_shape=[jax.ShapeDtypeStruct((R, C), F32)] * 4,
                          compiler_params=_params(1), name=name)(w, g, m, v)


def _adamw_update(w, g, m, v):
    nm = ADAM_B1 * m + (1.0 - ADAM_B1) * g
    nv = ADAM_B2 * v + (1.0 - ADAM_B2) * (g * g)
    m_hat = nm / (1.0 - ADAM_B1 ** ADAM_STEP)
    v_hat = nv / (1.0 - ADAM_B2 ** ADAM_STEP)
    return -ADAM_LR * (m_hat / (jnp.sqrt(v_hat) + ADAM_EPS) + ADAM_WD * w), nm, nv


def _adamw_many(ws, gs, ms, vs, name):
    n = len(ws)

    def body(*refs):
        for i in range(n):
            d, nm, nv = _adamw_update(refs[i][...], refs[n + i][...], refs[2 * n + i][...], refs[3 * n + i][...])
            refs[4 * n + i][...] = d
            refs[5 * n + i][...] = nm
            refs[6 * n + i][...] = nv

    return pl.pallas_call(body, in_specs=[VMEM_SPEC] * (4 * n), out_specs=[VMEM_SPEC] * (3 * n),
                          out_shape=[jax.ShapeDtypeStruct(w.shape, F32) for w in ws] * 3, name=name,
                          )(*ws, *gs, *ms, *vs)


def _pack(pieces):
    rows = []
    for p in pieces:
        flat = p.reshape(-1)
        pad = (-flat.shape[0]) % LANES
        if pad:
            flat = jnp.concatenate([flat, jnp.zeros((pad,), F32)])
        rows.append(flat.reshape(-1, LANES))
    total = sum(r.shape[0] for r in rows)
    pad_rows = (-total) % 8
    if pad_rows:
        rows.append(jnp.zeros((pad_rows, LANES), F32))
    return jnp.concatenate(rows, axis=0)


def _unpack(buf, shapes):
    out, r0 = [], 0
    for shp in shapes:
        n = int(np.prod(shp))
        nr = -(-n // LANES)
        out.append(buf[r0:r0 + nr].reshape(-1)[:n].reshape(shp))
        r0 += nr
    return out


SMALL_NAMES = ("rel_table", "b_in", "conv_w", "conv_b", "conv_ln_g", "conv_ln_b", "attn_norm_g", "conv_norm_g",
               "ln1_g", "ln1_b", "ffn_conv_w", "ffn_conv_b", "ln2_g", "ln2_b")
BIG_NAMES = ("w_in", "w_out", "w_up", "w_down")
WEIGHT_ORDER = ("rel_table", "w_in", "b_in", "conv_w", "conv_b", "conv_ln_g", "conv_ln_b", "attn_norm_g",
                "conv_norm_g", "w_out", "ln1_g", "ln1_b", "w_up", "ffn_conv_w", "ffn_conv_b", "w_down",
                "ln2_g", "ln2_b")


def kernel(x, rel_table, w_in, b_in, conv_w, conv_b, conv_ln_g, conv_ln_b, attn_norm_g, conv_norm_g, w_out, ln1_g, ln1_b, w_up, ffn_conv_w, ffn_conv_b, w_down, ln2_g, ln2_b, loss_target, m_rel_table, m_w_in, m_b_in, m_conv_w, m_conv_b, m_conv_ln_g, m_conv_ln_b, m_attn_norm_g, m_conv_norm_g, m_w_out, m_ln1_g, m_ln1_b, m_w_up, m_ffn_conv_w, m_ffn_conv_b, m_w_down, m_ln2_g, m_ln2_b, v_rel_table, v_w_in, v_b_in, v_conv_w, v_conv_b, v_conv_ln_g, v_conv_ln_b, v_attn_norm_g, v_conv_norm_g, v_w_out, v_ln1_g, v_ln1_b, v_w_up, v_ffn_conv_w, v_ffn_conv_b, v_w_down, v_ln2_g, v_ln2_b):
    args = dict(locals())
    weights = {n: args[n] for n in WEIGHT_ORDER}
    moms = {n: args["m_" + n] for n in WEIGHT_ORDER}
    vels = {n: args["v_" + n] for n in WEIGHT_ORDER}
    xi, yi, ci = _place()
    ids = jnp.stack([xi, yi, ci]).astype(jnp.int32)
    shard = 2 * xi + yi
    D = x.shape[-1]
    DFF = w_down.shape[1] * N_SHARDS
    CW = conv_norm_g.shape[-1]

    tr = lambda t: jnp.transpose(t[0])
    (g_in,), (g_cw, g_fcw) = _gather_weights([tr(w_in)], [conv_w[0], ffn_conv_w[0]])
    cols = lambda t: jnp.transpose(t, (1, 0, 2)).reshape(t.shape[1], N_SHARDS * t.shape[2])
    staged = [_stage_half(w[0], ids, name="stage_" + n) for w, n in ((w_out, "w_out"), (w_up, "w_up"),
                                                                     (w_down, "w_down"))]

    grad_x, fulls, all_packs = _local_step(
        x, loss_target, rel_table, g_in.reshape(-1, D), b_in, cols(g_cw), conv_b, conv_ln_g, conv_ln_b, attn_norm_g,
        conv_norm_g, staged, ln1_g, ln1_b, cols(g_fcw), ffn_conv_b, ln2_g, ln2_b, ids)
    big_grads = dict(zip(BIG_NAMES, _sibling_assemble(fulls)))

    summed = _sum_packs(all_packs)
    full_shapes = {n: weights[n].shape for n in SMALL_NAMES}
    full_shapes["conv_w"] = (1, CONV_KERNEL, CW)
    full_shapes["ffn_conv_w"] = (1, FFN_CONV_KERNEL, 2 * DFF)
    un = _unpack(summed, [(1, LANES)] + [full_shapes[n] for n in SMALL_NAMES])
    loss = un[0][0, 0]
    small_grads = dict(zip(SMALL_NAMES, un[1:]))
    for n in ("conv_w", "ffn_conv_w"):
        width = weights[n].shape[-1]
        small_grads[n] = lax.dynamic_slice_in_dim(small_grads[n], shard * width, width, axis=2)

    grads, delta, new_m, new_v = {}, {}, {}, {}
    for n in BIG_NAMES:
        shp = weights[n].shape
        g2 = big_grads[n]
        if n == "w_in":
            res = [jnp.transpose(t) for t in _adamw(tr(weights[n]), g2, tr(moms[n]), tr(vels[n]), name="adamw_" + n)]
        else:
            res = _adamw(weights[n][0], g2, moms[n][0], vels[n][0], name="adamw_" + n, g_transposed=n == "w_up")
        grads[n], delta[n], new_m[n], new_v[n] = (t.reshape(shp) for t in res)
    pick = lambda src: [src[n] for n in SMALL_NAMES]
    small_out = _adamw_many(pick(weights), pick(small_grads), pick(moms), pick(vels), name="adamw_small")
    ns = len(SMALL_NAMES)
    for tgt, part in ((delta, small_out[:ns]), (new_m, small_out[ns:2 * ns]), (new_v, small_out[2 * ns:])):
        tgt.update(zip(SMALL_NAMES, part))
    grads.update(small_grads)

    return (loss, grad_x, *[grads[n] for n in WEIGHT_ORDER], *[delta[n] for n in WEIGHT_ORDER],
            *[new_m[n] for n in WEIGHT_ORDER], *[new_v[n] for n in WEIGHT_ORDER])
```

```python
import math

import numpy as np
import jax
import jax.numpy as jnp
from jax import lax
from jax.experimental import pallas as pl
from jax.experimental.pallas import tpu as pltpu

F32 = jnp.float32
BF16 = jnp.bfloat16
MESH = pl.DeviceIdType.MESH

HEAD_DIM = 64
LANES = 128
ATTN_BLOCK = 128
DILATED_CONFIGS = ((128, 1), (512, 4), (2048, 16))
CONV_KERNEL = 31
FFN_CONV_KERNEL = 3
REL_BUCKETS = 32
REL_MAX_DIST = 2048
DEPTH = 1
ALPHA = (2 * DEPTH) ** 0.25
LN_EPS = 1e-5
NEG_INF = -1e30
QK_SCALE = 1.0 / math.sqrt(HEAD_DIM)
ADAM_LR = 0.001
ADAM_B1 = 0.9
ADAM_B2 = 0.999
ADAM_EPS = 1e-08
ADAM_WD = 0.01
ADAM_STEP = 10
VMEM_LIMIT = 52 * 1024 * 1024
FFN_COLS = 128
N_SHARDS = 4
N_DEV = 8


def _params(n_axes):
    return pltpu.CompilerParams(dimension_semantics=("arbitrary",) * n_axes,
                                vmem_limit_bytes=VMEM_LIMIT)


MM_DIMS = {"nn": (((1,), (0,)), ((), ())), "nt": (((1,), (1,)), ((), ())), "tn": (((0,), (0,)), ((), ()))}


class _Background:
    def __init__(self, in_arrays, out_shapes, aliases, n_sems, run, n_local=1):
        self.in_arrays, self.out_shapes, self.aliases = list(in_arrays), list(out_shapes), dict(aliases)
        self.n_sems, self.n_local, self.run = n_sems, n_local, run

    def scratch(self):
        return [pltpu.SemaphoreType.DMA((self.n_sems,)), pltpu.SemaphoreType.DMA((self.n_sems,)),
                pltpu.SemaphoreType.DMA((self.n_local,))]


def _hosted_call(body, bg, *, grid, in_specs, out_specs, out_shape, scratch_shapes, operands, name):
    n_in, n_out, n_scr = len(in_specs), len(out_specs), len(scratch_shapes)
    if bg is None:
        return pl.pallas_call(lambda *refs: body(refs, lambda post: None), grid=grid, in_specs=in_specs,
                              out_specs=out_specs, out_shape=out_shape, scratch_shapes=scratch_shapes,
                              compiler_params=_params(len(grid)), name=name)(*operands)
    nb_in, nb_out = len(bg.in_arrays), len(bg.out_shapes)
    n_steps = int(np.prod(grid))

    def full_body(*refs):
        own = refs[:n_in] + refs[n_in + nb_in:n_in + nb_in + n_out] \
            + refs[n_in + nb_in + n_out + nb_out:n_in + nb_in + n_out + nb_out + n_scr]
        bg_in = refs[n_in:n_in + nb_in]
        bg_out = refs[n_in + nb_in + n_out:n_in + nb_in + n_out + nb_out]
        sems = refs[n_in + nb_in + n_out + nb_out + n_scr:]
        step = pl.program_id(0)
        for ax in range(1, len(grid)):
            step = step * grid[ax] + pl.program_id(ax)

        def hook(post):
            bg.run(step, n_steps, bg_in, bg_out, *sems, post)

        body(own, hook)

    res = pl.pallas_call(
        full_body, grid=grid, in_specs=list(in_specs) + [HBM_SPEC] * nb_in,
        out_specs=list(out_specs) + [HBM_SPEC] * nb_out, out_shape=list(out_shape) + bg.out_shapes,
        input_output_aliases={n_in + a: n_out + o for a, o in bg.aliases.items()},
        scratch_shapes=list(scratch_shapes) + bg.scratch(), compiler_params=_params(len(grid)), name=name,
    )(*operands, *bg.in_arrays)
    return res


def _matmul_general(ins, part_fn, *, grid, tm, tn, outs, epilogue, extras=(), name, bg=None):
    nk = grid[2]
    n_in, n_extra = len(ins), len(extras)

    def body(refs, bg_hook):
        in_refs = refs[:n_in]
        rest = refs[n_in:]
        extra_refs = rest[:n_extra]
        out_refs = rest[n_extra:n_extra + len(outs)]
        acc_ref = rest[-1]
        i, j, k = pl.program_id(0), pl.program_id(1), pl.program_id(2)
        bg_hook(False)
        part = part_fn(in_refs, i, j, k)
        if nk == 1:
            epilogue(part, i, j, extra_refs, out_refs)
        else:
            @pl.when(k == 0)
            def _():
                acc_ref[...] = part

            @pl.when(k > 0)
            def _():
                acc_ref[...] += part

            @pl.when(k == nk - 1)
            def _():
                epilogue(acc_ref[...], i, j, extra_refs, out_refs)
        bg_hook(True)

    in_specs = [pl.BlockSpec(bs, im) for (_, bs, im) in list(ins) + list(extras)]
    out_specs = [pl.BlockSpec(bs, im) for (_, _, bs, im) in outs]
    out_shape = [jax.ShapeDtypeStruct(s, d) for (s, d, _, _) in outs]
    return _hosted_call(body, bg, grid=grid, in_specs=in_specs, out_specs=out_specs, out_shape=out_shape,
                        scratch_shapes=[pltpu.VMEM((tm, tn), F32)],
                        operands=[e[0] for e in ins] + [e[0] for e in extras], name=name)


def _dot(a, b, mode):
    return lax.dot_general(a.astype(BF16), b.astype(BF16), MM_DIMS[mode], preferred_element_type=F32)


def _matmul(a, b, *, mode, tm, tn, tk, outs, epilogue, extras=(), name, bg=None):
    if mode == "tn":
        K, M = a.shape
        N = b.shape[1]
        ins = [(a, (tk, tm), lambda i, j, k: (k, i)), (b, (tk, tn), lambda i, j, k: (k, j))]
    elif mode == "nt":
        M, K = a.shape
        N = b.shape[0]
        ins = [(a, (tm, tk), lambda i, j, k: (i, k)), (b, (tn, tk), lambda i, j, k: (j, k))]
    else:
        M, K = a.shape
        N = b.shape[1]
        ins = [(a, (tm, tk), lambda i, j, k: (i, k)), (b, (tk, tn), lambda i, j, k: (k, j))]
    assert M % tm == 0 and N % tn == 0 and K % tk == 0, (name, M, N, K, tm, tn, tk)

    def part_fn(in_refs, i, j, k):
        return _dot(in_refs[0][...], in_refs[1][...], mode)

    return _matmul_general(ins, part_fn, grid=(M // tm, N // tn, K // tk), tm=tm, tn=tn, outs=outs,
                           epilogue=epilogue, extras=extras, name=name, bg=bg)


def _plain_out(M, N, tm, tn, dtype):
    return ((M, N), dtype, (tm, tn), lambda i, j, k: (i, j))


def _mm_plain(a, b, *, mode, tm, tn, tk, out_dtype, name, bias=None, bg=None):
    if mode == "tn":
        M, N = a.shape[1], b.shape[1]
    elif mode == "nt":
        M, N = a.shape[0], b.shape[0]
    else:
        M, N = a.shape[0], b.shape[1]
    extras = []
    if bias is not None:
        extras.append((bias, (1, tn), lambda i, j, k: (0, j)))

    def epilogue(acc, i, j, extra_refs, out_refs):
        if bias is not None:
            acc = acc + extra_refs[0][...]
        out_refs[0][...] = acc.astype(out_dtype)

    res = _matmul(a, b, mode=mode, tm=tm, tn=tn, tk=tk, outs=[_plain_out(M, N, tm, tn, out_dtype)],
                  epilogue=epilogue, extras=extras, name=name, bg=bg)
    return res[0] if bg is None else res


def _row_tile(T, want):
    t = min(T, want)
    while T % t:
        t //= 2
    return t


def _col_tile(N, want):
    if N <= want:
        return N
    best = None
    for c in range(LANES, want + 1, LANES):
        if N % c == 0:
            best = c
    return best if best is not None else N


def _accumulate(ref, first, val):
    @pl.when(first)
    def _():
        ref[...] = val

    @pl.when(jnp.logical_not(first))
    def _():
        ref[...] += val


def _ln_fwd(z, g, b):
    mu = jnp.mean(z, axis=-1, keepdims=True)
    zc = z - mu
    var = jnp.mean(zc * zc, axis=-1, keepdims=True)
    r = lax.rsqrt(var + LN_EPS)
    xh = zc * r
    return xh * g + b, xh, r


def _ln_bwd(dy, xh, r, g):
    dxh = dy * g
    m1 = jnp.mean(dxh, axis=-1, keepdims=True)
    m2 = jnp.mean(dxh * xh, axis=-1, keepdims=True)
    return r * (dxh - m1 - xh * m2)


def _sigmoid(x):
    return 0.5 * jnp.tanh(0.5 * x) + 0.5


def _bucket_tables():
    exact = REL_BUCKETS // 2
    qi = np.arange(ATTN_BLOCK)[:, None]
    kj = np.arange(2 * ATTN_BLOCK)[None, :]
    steps = qi + ATTN_BLOCK - kj
    buckets, masks = [], []
    for window, dilation in DILATED_CONFIGS:
        max_steps = window // dilation
        band = (steps >= 0) & (steps <= max_steps)
        dist = np.maximum(steps, 0) * dilation
        d_f = np.maximum(dist, 1).astype(np.float32)
        large = exact + (np.log(d_f / np.float32(exact)) / np.float32(math.log(REL_MAX_DIST / exact))
                         * np.float32(REL_BUCKETS - exact)).astype(np.int32)
        large = np.minimum(large, REL_BUCKETS - 1)
        bucket = np.where(dist < exact, dist, large).astype(np.int32)
        buckets.append(bucket.reshape(1, -1))
        masks.append(np.where(band, 0.0, NEG_INF).astype(np.float32).reshape(1, -1))
    return np.stack(buckets), np.stack(masks)


def _split_hi_lo(x):
    hi = x.astype(BF16)
    lo = (x - hi.astype(F32)).astype(BF16)
    return hi, lo


def _bias_build(rel_table_t, bucket, mask):
    H = rel_table_t.shape[0]
    n = bucket.shape[-1]

    def body(t_ref, bkt_ref, mask_ref, o_ref):
        onehot = (lax.broadcasted_iota(jnp.int32, (REL_BUCKETS, n), 0) == bkt_ref[0]).astype(BF16)
        t = t_ref[...]
        t1 = t.astype(BF16)
        r1 = t - t1.astype(F32)
        t2 = r1.astype(BF16)
        t3 = (r1 - t2.astype(F32)).astype(BF16)
        acc = jnp.dot(t1, onehot, preferred_element_type=F32)
        acc = acc + jnp.dot(t2, onehot, preferred_element_type=F32)
        acc = acc + jnp.dot(t3, onehot, preferred_element_type=F32)
        o_ref[0] = acc + mask_ref[0]

    return pl.pallas_call(
        body, grid=(3,),
        in_specs=[pl.BlockSpec((H, REL_BUCKETS), lambda b: (0, 0)),
                  pl.BlockSpec((1, 1, n), lambda b: (b, 0, 0)),
                  pl.BlockSpec((1, 1, n), lambda b: (b, 0, 0))],
        out_specs=pl.BlockSpec((1, H, n), lambda b: (b, 0, 0)),
        out_shape=jax.ShapeDtypeStruct((3, H, n), F32),
        compiler_params=_params(1), name="bias_build",
    )(rel_table_t, bucket, mask)


def _rel_grad(dbias, bucket):
    H = dbias.shape[1]
    n = bucket.shape[-1]
    dims = (((1,), (1,)), ((), ()))

    def body(d_ref, bkt_ref, o_ref):
        b = pl.program_id(0)
        onehot = (lax.broadcasted_iota(jnp.int32, (REL_BUCKETS, n), 0) == bkt_ref[0]).astype(BF16)
        d = d_ref[0]
        d1 = d.astype(BF16)
        r1 = d - d1.astype(F32)
        d2 = r1.astype(BF16)
        d3 = (r1 - d2.astype(F32)).astype(BF16)
        acc = lax.dot_general(d1, onehot, dims, preferred_element_type=F32)
        acc = acc + lax.dot_general(d2, onehot, dims, preferred_element_type=F32)
        acc = acc + lax.dot_general(d3, onehot, dims, preferred_element_type=F32)
        _accumulate(o_ref, b == 0, acc)

    return pl.pallas_call(
        body, grid=(3,),
        in_specs=[pl.BlockSpec((1, H, n), lambda b: (b, 0, 0)),
                  pl.BlockSpec((1, 1, n), lambda b: (b, 0, 0))],
        out_specs=pl.BlockSpec((H, REL_BUCKETS), lambda b: (0, 0)),
        out_shape=jax.ShapeDtypeStruct((H, REL_BUCKETS), F32),
        compiler_params=_params(1), name="rel_grad",
    )(dbias, bucket)


def _regroup(src, stage, dst, d, S, off=0):
    if d == 1:
        dst[off:off + S, :] = src.astype(dst.dtype)
        return
    stage[...] = src.astype(F32)
    L = S // d
    for r in range(d):
        dst[off + r * L:off + (r + 1) * L, :] = stage[pl.ds(r, L, stride=d), :].astype(dst.dtype)


def _ungroup(sub_ref, off, nat_ref, d, S, add):
    L = S // d
    for r in range(d):
        rows = pl.ds(0, S) if d == 1 else pl.ds(r, L, stride=d)
        val = sub_ref[off + r * L:off + (r + 1) * L, :]
        if add:
            nat_ref[rows, :] += val
        else:
            nat_ref[rows, :] = val


def _branch_keys(ks, vs, S, nb, g_idx):
    blk3 = (S // ATTN_BLOCK, ATTN_BLOCK, LANES)
    kc3 = ks[ATTN_BLOCK:ATTN_BLOCK + S, :].reshape(blk3)
    vc3 = vs[ATTN_BLOCK:ATTN_BLOCK + S, :].reshape(blk3)
    if nb == 1:
        return kc3, vc3, None
    kk3 = jnp.concatenate([ks[0:S, :].reshape(blk3), kc3], axis=1)
    vv3 = jnp.concatenate([vs[0:S, :].reshape(blk3), vc3], axis=1)
    col = lax.broadcasted_iota(jnp.int32, (1, 1, 2 * ATTN_BLOCK), 2)
    dead = jnp.logical_and((g_idx & (nb - 1)) == 0, col < ATTN_BLOCK)
    return kk3, vv3, dead


def _branch_scores(qe, kk3, b_ref, bi, e, dead):
    s = jnp.einsum("gqe,gke->gqk", qe, kk3, preferred_element_type=F32)
    if dead is None:
        return s + b_ref[bi, e, :, ATTN_BLOCK:]
    return jnp.where(dead, NEG_INF, s + b_ref[bi, e])


def _attention_fwd(qkv, bias_all, B, S, AW, bg=None):
    HP = AW // LANES
    G = S // ATTN_BLOCK
    blk3 = (G, ATTN_BLOCK, LANES)

    def body(refs, bg_hook):
        q_ref, k_ref, v_ref, b_ref, o_ref, lse_ref, stage, qs, ks, vs, ot, lt, on0, on1, on2, ln0, ln1, ln2 = refs
        bg_hook(False)
        head0 = lax.broadcasted_iota(jnp.int32, (1, 1, LANES), 2) < HEAD_DIM
        g_idx = lax.broadcasted_iota(jnp.int32, (G, 1, 1), 0)
        ks[0:ATTN_BLOCK, :] = jnp.zeros((ATTN_BLOCK, LANES), BF16)
        vs[0:ATTN_BLOCK, :] = jnp.zeros((ATTN_BLOCK, LANES), BF16)
        nat_o, nat_l = (on0, on1, on2), (ln0, ln1, ln2)
        for bi, (_, d) in enumerate(DILATED_CONFIGS):
            nb = S // d // ATTN_BLOCK
            _regroup(q_ref[0], stage, qs, d, S)
            _regroup(k_ref[0], stage, ks, d, S, ATTN_BLOCK)
            _regroup(v_ref[0], stage, vs, d, S, ATTN_BLOCK)
            q3 = qs[...].reshape(blk3) * QK_SCALE
            kk3, vv3, dead = _branch_keys(ks, vs, S, nb, g_idx)
            outs, lses = [], []
            for e in range(2):
                msk = head0 if e == 0 else jnp.logical_not(head0)
                qe = jnp.where(msk, q3, jnp.zeros_like(q3))
                s = _branch_scores(qe, kk3, b_ref, bi, e, dead)
                m = jnp.max(s, axis=-1, keepdims=True)
                p = jnp.exp(s - m)
                l = jnp.sum(p, axis=-1, keepdims=True)
                o = jnp.einsum("gqk,gke->gqe", p.astype(BF16), vv3, preferred_element_type=F32)
                outs.append(o / l)
                lses.append(jnp.broadcast_to(m + jnp.log(l), blk3))
            ot[...] = jnp.where(head0, outs[0], outs[1]).reshape(S, LANES)
            lt[...] = jnp.where(head0, lses[0], lses[1]).reshape(S, LANES)
            _ungroup(ot, 0, nat_o[bi], d, S, add=False)
            _ungroup(lt, 0, nat_l[bi], d, S, add=False)

        la, lb, lc = ln0[...], ln1[...], ln2[...]
        m = jnp.maximum(jnp.maximum(la, lb), lc)
        ea, eb, ec = jnp.exp(la - m), jnp.exp(lb - m), jnp.exp(lc - m)
        den = ea + eb + ec
        lse_ref[0] = m + jnp.log(den)
        o_ref[0] = (ea * on0[...] + eb * on1[...] + ec * on2[...]) / den
        bg_hook(True)

    blk = lambda off: pl.BlockSpec((1, S, LANES), lambda b, h: (b, 0, off + h))
    qv = qkv.reshape(B, S, 3 * AW)
    sub_f = pltpu.VMEM((S, LANES), F32)
    pad_b = pltpu.VMEM((S + ATTN_BLOCK, LANES), BF16)
    res = _hosted_call(
        body, bg, grid=(B, HP),
        in_specs=[blk(0), blk(HP), blk(2 * HP),
                  pl.BlockSpec((3, 2, ATTN_BLOCK, 2 * ATTN_BLOCK), lambda b, h: (0, h, 0, 0))],
        out_specs=[blk(0), blk(0)],
        out_shape=[jax.ShapeDtypeStruct((B, S, AW), F32)] * 2,
        scratch_shapes=[sub_f, pltpu.VMEM((S, LANES), BF16), pad_b, pad_b] + [sub_f] * 8,
        operands=[qv, qv, qv, bias_all], name="attention_fwd")
    return (res[0].reshape(B * S, AW), res[1].reshape(B * S, AW)) + tuple(res[2:])


def _attention_bwd(qkv, do, lse, dd, bias_all, B, S, AW, bg=None):
    HP = AW // LANES
    H = AW // HEAD_DIM
    G = S // ATTN_BLOCK
    blk3 = (G, ATTN_BLOCK, LANES)
    PAD = ATTN_BLOCK

    def body(refs, bg_hook):
        (q_ref, k_ref, v_ref, do_ref, lse_ref, dd_ref, b_ref,
         dq_ref, dk_ref, dv_ref, csq_ref, csk_ref, csv_ref, db_ref,
         stage, qs, ks, vs, gs, ls, ds_, tq, tk, tv, accq, acck, accv) = refs
        bg_hook(False)
        head0 = lax.broadcasted_iota(jnp.int32, (1, 1, LANES), 2) < HEAD_DIM
        g_idx = lax.broadcasted_iota(jnp.int32, (G, 1, 1), 0)
        first_b = pl.program_id(1) == 0

        @pl.when(first_b)
        def _():
            db_ref[...] = jnp.zeros_like(db_ref)

        ks[0:PAD, :] = jnp.zeros((PAD, LANES), BF16)
        vs[0:PAD, :] = jnp.zeros((PAD, LANES), BF16)
        tk[0:PAD, :] = jnp.zeros((PAD, LANES), F32)
        tv[0:PAD, :] = jnp.zeros((PAD, LANES), F32)
        for bi, (_, d) in enumerate(DILATED_CONFIGS):
            nb = S // d // ATTN_BLOCK
            _regroup(q_ref[0], stage, qs, d, S)
            _regroup(k_ref[0], stage, ks, d, S, PAD)
            _regroup(v_ref[0], stage, vs, d, S, PAD)
            _regroup(do_ref[0], stage, gs, d, S)
            _regroup(lse_ref[0], stage, ls, d, S)
            _regroup(dd_ref[0], stage, ds_, d, S)
            q3 = qs[...].reshape(blk3) * QK_SCALE
            do3 = gs[...].reshape(blk3)
            lse3 = ls[...].reshape(blk3)
            dd3 = ds_[...].reshape(blk3)
            kk3, vv3, dead = _branch_keys(ks, vs, S, nb, g_idx)
            dq = jnp.zeros(blk3, F32)
            dkk = jnp.zeros(kk3.shape, F32)
            dvv = jnp.zeros(kk3.shape, F32)
            for e in range(2):
                msk = head0 if e == 0 else jnp.logical_not(head0)
                c0 = e * HEAD_DIM
                qe = jnp.where(msk, q3, jnp.zeros_like(q3))
                doe = jnp.where(msk, do3, jnp.zeros_like(do3))
                ke = jnp.where(msk, kk3 * QK_SCALE, jnp.zeros_like(kk3))
                s = _branch_scores(qe, kk3, b_ref, bi, e, dead)
                p = jnp.exp(s - lse3[:, :, c0:c0 + 1])
                dp = jnp.einsum("gqe,gke->gqk", doe, vv3, preferred_element_type=F32)
                dsc = p * (dp - dd3[:, :, c0:c0 + 1])
                if dead is None:
                    db_ref[bi, e, :, ATTN_BLOCK:] += jnp.sum(dsc, axis=0)
                else:
                    db_ref[bi, e] += jnp.sum(dsc, axis=0)
                dsb = dsc.astype(BF16)
                dq = dq + jnp.einsum("gqk,gke->gqe", dsb, ke, preferred_element_type=F32)
                dkk = dkk + jnp.einsum("gqk,gqe->gke", dsb, qe, preferred_element_type=F32)
                dvv = dvv + jnp.einsum("gqk,gqe->gke", p.astype(BF16), doe, preferred_element_type=F32)
            tq[...] = dq.reshape(S, LANES)
            if dead is None:
                tk[PAD:PAD + S, :] = dkk.reshape(S, LANES)
                tv[PAD:PAD + S, :] = dvv.reshape(S, LANES)
            else:
                tk[PAD:PAD + S, :] = dkk[:, ATTN_BLOCK:, :].reshape(S, LANES)
                tv[PAD:PAD + S, :] = dvv[:, ATTN_BLOCK:, :].reshape(S, LANES)
                tk[0:S, :] += dkk[:, :ATTN_BLOCK, :].reshape(S, LANES)
                tv[0:S, :] += dvv[:, :ATTN_BLOCK, :].reshape(S, LANES)
            _ungroup(tq, 0, accq, d, S, add=bi > 0)
            _ungroup(tk, PAD, acck, d, S, add=bi > 0)
            _ungroup(tv, PAD, accv, d, S, add=bi > 0)

        for acc, out_ref, cs_ref in ((accq, dq_ref, csq_ref), (acck, dk_ref, csk_ref), (accv, dv_ref, csv_ref)):
            tot = acc[...]
            out_ref[0] = tot.astype(out_ref.dtype)
            _accumulate(cs_ref, first_b, jnp.sum(tot, axis=0, keepdims=True))
        bg_hook(True)

    blk = lambda off: pl.BlockSpec((1, S, LANES), lambda h, b: (b, 0, off + h))
    cs_spec = pl.BlockSpec((1, LANES), lambda h, b: (0, h))
    bias_spec = pl.BlockSpec((3, 2, ATTN_BLOCK, 2 * ATTN_BLOCK), lambda h, b: (0, h, 0, 0))
    qv = qkv.reshape(B, S, 3 * AW)
    view = lambda t: t.reshape(B, S, AW)
    sub_b = pltpu.VMEM((S, LANES), BF16)
    sub_f = pltpu.VMEM((S, LANES), F32)
    pad_b = pltpu.VMEM((S + PAD, LANES), BF16)
    pad_f = pltpu.VMEM((S + PAD, LANES), F32)
    res = _hosted_call(
        body, bg, grid=(HP, B),
        in_specs=[blk(0), blk(HP), blk(2 * HP), blk(0), blk(0), blk(0), bias_spec],
        out_specs=[blk(0), blk(0), blk(0), cs_spec, cs_spec, cs_spec, bias_spec],
        out_shape=[jax.ShapeDtypeStruct((B, S, AW), BF16)] * 3 + [jax.ShapeDtypeStruct((1, AW), F32)] * 3
        + [jax.ShapeDtypeStruct((3, H, ATTN_BLOCK, 2 * ATTN_BLOCK), F32)],
        scratch_shapes=[sub_f, sub_b, pad_b, pad_b, sub_b, sub_f, sub_f, sub_f, pad_f, pad_f, sub_f, sub_f, sub_f],
        operands=[qv, qv, qv, view(do), view(lse), view(dd), bias_all], name="attention_bwd")
    flat = lambda t: t.reshape(B * S, AW)
    return (flat(res[0]), flat(res[1]), flat(res[2]), res[3], res[4], res[5], res[6]) + tuple(res[7:])


class _RowShifts:
    def __init__(self, x, row, up):
        self.x, self.row, self.up, self.base = x, row, up, {0: x}

    def __call__(self, s):
        x = self.x
        n, c = x.shape
        r, whole = s % 8, s - s % 8
        if r not in self.base:
            if self.up:
                rolled = pltpu.roll(x, n - r, 0)
                tail = jnp.where(self.row[n - 8:] < n - r, rolled[n - 8:], 0.0)
                self.base[r] = jnp.concatenate([rolled[:n - 8], tail], axis=0)
            else:
                rolled = pltpu.roll(x, r, 0)
                head = jnp.where(self.row[:8] >= r, rolled[:8], 0.0)
                self.base[r] = jnp.concatenate([head, rolled[8:]], axis=0)
        y = self.base[r]
        if whole == 0:
            return y
        pad = jnp.zeros((whole, c), x.dtype)
        if self.up:
            return jnp.concatenate([y[whole:], pad], axis=0)
        return jnp.concatenate([pad, y[:n - whole]], axis=0)


def _conv_branch_fwd_math(a, g, w_ref, cb, lg, lb, row):
    sg = _sigmoid(g)
    u0 = a * sg
    u0_down = _RowShifts(u0, row, up=False)
    uc = jnp.zeros_like(u0) + cb
    for k in range(CONV_KERNEL):
        uc = uc + w_ref[k:k + 1, :] * u0_down(CONV_KERNEL - 1 - k)
    ul, xh, r = _ln_fwd(uc, lg, lb)
    su = _sigmoid(ul)
    u = ul * su
    return sg, u0_down, ul, xh, r, su, u


def _conv_fwd(ag, conv_w, conv_b, ln_g, ln_b, norm_g, B, S, CW):
    def body(a_ref, g_ref, w_ref, cb_ref, lg_ref, lb_ref, ng_ref, o_ref):
        row = lax.broadcasted_iota(jnp.int32, (S, CW), 0)
        _, _, _, _, _, _, u = _conv_branch_fwd_math(a_ref[0], g_ref[0], w_ref, cb_ref[...], lg_ref[...],
                                                    lb_ref[...], row)
        rr = lax.rsqrt(jnp.mean(u * u, axis=-1, keepdims=True) + LN_EPS)
        o_ref[0] = (u * rr * ng_ref[...]).astype(BF16)

    vec = pl.BlockSpec((1, CW), lambda b: (0, 0))
    out = pl.pallas_call(
        body, grid=(B,),
        in_specs=[pl.BlockSpec((1, S, CW), lambda b: (b, 0, 0)), pl.BlockSpec((1, S, CW), lambda b: (b, 0, 1)),
                  pl.BlockSpec((CONV_KERNEL, CW), lambda b: (0, 0)), vec, vec, vec, vec],
        out_specs=pl.BlockSpec((1, S, CW), lambda b: (b, 0, 0)),
        out_shape=jax.ShapeDtypeStruct((B, S, CW), BF16),
        compiler_params=_params(1), name="conv_fwd",
    )(ag.reshape(B, S, 2 * CW), ag.reshape(B, S, 2 * CW), conv_w, conv_b, ln_g, ln_b, norm_g)
    return out.reshape(B * S, CW)


def _conv_bwd(ag, dmc, conv_w, conv_b, ln_g, ln_b, norm_g, B, S, CW):
    def body(a_ref, g_ref, dm_ref, w_ref, cb_ref, lg_ref, lb_ref, ng_ref,
             dag_ref, dw_ref, dcb_ref, dlg_ref, dlb_ref, dng_ref):
        b = pl.program_id(0)
        row = lax.broadcasted_iota(jnp.int32, (S, CW), 0)
        a, g = a_ref[0], g_ref[0]
        sg, u0_down, ul, xh, r, su, u = _conv_branch_fwd_math(a, g, w_ref, cb_ref[...], lg_ref[...], lb_ref[...], row)
        rr = lax.rsqrt(jnp.mean(u * u, axis=-1, keepdims=True) + LN_EPS)
        dm = dm_ref[0]
        dxn = dm * ng_ref[...]
        du = rr * (dxn - u * (rr * rr) * jnp.mean(dxn * u, axis=-1, keepdims=True))
        dul = du * su * (1.0 + ul * (1.0 - su))
        duc = _ln_bwd(dul, xh, r, lg_ref[...])
        first = b == 0
        _accumulate(dng_ref, first, jnp.sum(dm * u * rr, axis=0, keepdims=True))
        _accumulate(dlg_ref, first, jnp.sum(dul * xh, axis=0, keepdims=True))
        _accumulate(dlb_ref, first, jnp.sum(dul, axis=0, keepdims=True))
        _accumulate(dcb_ref, first, jnp.sum(duc, axis=0, keepdims=True))

        @pl.when(first)
        def _():
            dw_ref[...] = jnp.zeros_like(dw_ref)

        duc_up = _RowShifts(duc, row, up=True)
        du0 = jnp.zeros_like(duc)
        for k in range(CONV_KERNEL):
            sh = CONV_KERNEL - 1 - k
            dw_ref[k:k + 1, :] += jnp.sum(duc * u0_down(sh), axis=0, keepdims=True)
            du0 = du0 + w_ref[k:k + 1, :] * duc_up(sh)
        dag_ref[0, :, :CW] = du0 * sg
        dag_ref[0, :, CW:] = du0 * a * sg * (1.0 - sg)

    vec = pl.BlockSpec((1, CW), lambda b: (0, 0))
    wspec = pl.BlockSpec((CONV_KERNEL, CW), lambda b: (0, 0))
    agv = ag.reshape(B, S, 2 * CW)
    res = pl.pallas_call(
        body, grid=(B,),
        in_specs=[pl.BlockSpec((1, S, CW), lambda b: (b, 0, 0)), pl.BlockSpec((1, S, CW), lambda b: (b, 0, 1)),
                  pl.BlockSpec((1, S, CW), lambda b: (b, 0, 0)), wspec, vec, vec, vec, vec],
        out_specs=[pl.BlockSpec((1, S, 2 * CW), lambda b: (b, 0, 0)), wspec, vec, vec, vec, vec],
        out_shape=[jax.ShapeDtypeStruct((B, S, 2 * CW), F32), jax.ShapeDtypeStruct((CONV_KERNEL, CW), F32)]
        + [jax.ShapeDtypeStruct((1, CW), F32)] * 4,
        compiler_params=_params(1), name="conv_bwd",
    )(agv, agv, dmc.reshape(B, S, CW), conv_w, conv_b, ln_g, ln_b, norm_g)
    return (res[0].reshape(B * S, 2 * CW),) + tuple(res[1:])


def _ffn_conv(x, w_ref, bias, row):
    down = x if isinstance(x, _RowShifts) else _RowShifts(x, row, up=False)
    y = jnp.zeros_like(down.x) + bias
    for k in range(FFN_CONV_KERNEL):
        y = y + w_ref[k:k + 1, :] * down(FFN_CONV_KERNEL - 1 - k)
    return y


def _ffn_specs(S, tc, nj, order):
    pick = (lambda b, j: (b, j)) if order == "bj" else (lambda j, b: (b, j))
    act = lambda off: pl.BlockSpec((1, S, tc), lambda *g: (pick(*g)[0], 0, off + pick(*g)[1]))
    cw = lambda off: pl.BlockSpec((FFN_CONV_KERNEL, tc), lambda *g: (0, off + pick(*g)[1]))
    cb = lambda off: pl.BlockSpec((1, tc), lambda *g: (0, off + pick(*g)[1]))
    return act, cw, cb


FFN_HALO = 16


def _half_sequences(S):
    if S < 8 * FFN_HALO:
        return [(0, S, 0, S)]
    h = S // 2
    return [(0, h + FFN_HALO, 0, h), (h - FFN_HALO, S, FFN_HALO, h)]


def _w_up_block_spec(w_up_sh, tc, off):
    _, D, cs = w_up_sh.shape
    assert cs % tc == 0
    bps = cs // tc
    return pl.BlockSpec((1, D, tc), lambda j: ((off + j) // bps, 0, (off + j) % bps))


def _ffn_fwd_fused(x1b, w_up_sh, cw, cb, B, S, DFF):
    tc = FFN_COLS
    nj = DFF // tc
    D = x1b.shape[1]

    def body(x_ref, wg_ref, wv_ref, cwg_ref, cwv_ref, cbg_ref, cbv_ref, o_ref, up_ref):
        w = jnp.concatenate([wg_ref[0], wv_ref[0]], axis=1)
        for b in range(B):
            for lo, hi, o0, on in _half_sequences(S):
                row = lax.broadcasted_iota(jnp.int32, (hi - lo, tc), 0)
                up = jnp.dot(x_ref[b, lo:hi, :], w, preferred_element_type=F32)
                up_ref[b, lo + o0:lo + o0 + on, :] = up[o0:o0 + on]
                gate = _ffn_conv(up[:, :tc], cwg_ref, cbg_ref[...], row)
                val = _ffn_conv(up[:, tc:], cwv_ref, cbv_ref[...], row)
                o_ref[b, lo + o0:lo + o0 + on, :] = (gate * _sigmoid(gate) * val).astype(BF16)[o0:o0 + on]

    cws = lambda off: pl.BlockSpec((FFN_CONV_KERNEL, tc), lambda j: (0, off + j))
    cbs = lambda off: pl.BlockSpec((1, tc), lambda j: (0, off + j))
    act, upre = pl.pallas_call(
        body, grid=(nj,),
        in_specs=[pl.BlockSpec((B, S, D), lambda j: (0, 0, 0), pipeline_mode=pl.Buffered(1)),
                  _w_up_block_spec(w_up_sh, tc, 0), _w_up_block_spec(w_up_sh, tc, nj),
                  cws(0), cws(nj), cbs(0), cbs(nj)],
        out_specs=[pl.BlockSpec((B, S, tc), lambda j: (0, 0, j)), pl.BlockSpec((B, S, 2 * tc), lambda j: (0, 0, j))],
        out_shape=[jax.ShapeDtypeStruct((B, S, DFF), BF16), jax.ShapeDtypeStruct((B, S, 2 * DFF), F32)],
        compiler_params=_params(1), name="ffn_fwd",
    )(x1b.reshape(B, S, D), w_up_sh, w_up_sh, cw, cw, cb, cb)
    return act.reshape(B * S, DFF), upre


def _ffn_bwd_fused(x1b, dz2b, upre, w_down, cw, cb, B, S, DFF):
    tc = FFN_COLS
    nj = DFF // tc
    D = x1b.shape[1]

    def body(x_ref, dz_ref, up_ref, wd_ref, cwg_ref, cwv_ref, cbg_ref, cbv_ref,
             dug_ref, duv_ref, dwu_ref, dwd_ref, dcw_ref, dcb_ref):
        first = pl.program_id(1) == 0
        dw_t = dwd = None
        dcb = [None, None]
        dcw = [[None] * FFN_CONV_KERNEL, [None] * FFN_CONV_KERNEL]
        add = lambda old, new: new if old is None else old + new
        for lo, hi, o0, on in _half_sequences(S):
            n = hi - lo
            own = slice(o0, o0 + on)
            row = lax.broadcasted_iota(jnp.int32, (n, tc), 0)
            x = x_ref[0, lo:hi, :]
            dz = dz_ref[0, lo:hi, :]
            ug = _RowShifts(up_ref[0, lo:hi, :tc], row, up=False)
            uv = _RowShifts(up_ref[0, lo:hi, tc:], row, up=False)
            gate = _ffn_conv(ug, cwg_ref, cbg_ref[...], row)
            val = _ffn_conv(uv, cwv_ref, cbv_ref[...], row)
            sg = _sigmoid(gate)
            act = (gate * sg * val).astype(BF16)
            dact = _dot(dz, wd_ref[...], "nt")
            dgate = dact * val * sg * (1.0 + gate * (1.0 - sg))
            dval = dact * gate * sg
            dupre = []
            for h, (dup, u_down, w_ref) in enumerate(((dgate, ug, cwg_ref), (dval, uv, cwv_ref))):
                dcb[h] = add(dcb[h], jnp.sum(dup[own], axis=0, keepdims=True))
                dup_up = _RowShifts(dup, row, up=True)
                acc = jnp.zeros_like(dup)
                for k in range(FFN_CONV_KERNEL):
                    sh = FFN_CONV_KERNEL - 1 - k
                    dcw[h][k] = add(dcw[h][k], jnp.sum((dup * u_down(sh))[own], axis=0, keepdims=True))
                    acc = acc + w_ref[k:k + 1, :] * dup_up(sh)
                dupre.append(acc.astype(BF16)[own])
            dug_ref[0, lo + o0:lo + o0 + on, :] = dupre[0]
            duv_ref[0, lo + o0:lo + o0 + on, :] = dupre[1]
            dw_t = add(dw_t, _dot(jnp.concatenate(dupre, axis=1), x[own], "tn"))
            dwd = add(dwd, _dot(act[own], dz[own], "tn"))
        _accumulate(dwu_ref.at[0], first, dw_t[:tc])
        _accumulate(dwu_ref.at[1], first, dw_t[tc:])
        _accumulate(dwd_ref, first, dwd)
        for h in range(2):
            _accumulate(dcb_ref.at[h], first, dcb[h])
            for k in range(FFN_CONV_KERNEL):
                _accumulate(dcw_ref.at[k, pl.ds(h, 1), :], first, dcw[h][k])

    act_s, cws, cbs = _ffn_specs(S, tc, nj, "jb")
    seq = pl.BlockSpec((1, S, D), lambda j, b: (b, 0, 0))
    res = pl.pallas_call(
        body, grid=(nj, B),
        in_specs=[seq, seq, pl.BlockSpec((1, S, 2 * tc), lambda j, b: (b, 0, j)),
                  pl.BlockSpec((tc, D), lambda j, b: (j, 0)), cws(0), cws(nj), cbs(0), cbs(nj)],
        out_specs=[act_s(0), act_s(0), pl.BlockSpec((2, tc, D), lambda j, b: (0, j, 0)),
                   pl.BlockSpec((tc, D), lambda j, b: (j, 0)),
                   pl.BlockSpec((FFN_CONV_KERNEL, 2, tc), lambda j, b: (0, 0, j)),
                   pl.BlockSpec((2, 1, tc), lambda j, b: (0, 0, j))],
        out_shape=[jax.ShapeDtypeStruct((B, S, DFF), BF16)] * 2
        + [jax.ShapeDtypeStruct((2, DFF, D), F32), jax.ShapeDtypeStruct((DFF, D), F32),
           jax.ShapeDtypeStruct((FFN_CONV_KERNEL, 2, DFF), F32), jax.ShapeDtypeStruct((2, 1, DFF), F32)],
        compiler_params=_params(2), name="ffn_bwd",
    )(x1b.reshape(B, S, D), dz2b.reshape(B, S, D), upre, w_down, cw, cw, cb, cb)
    flat = lambda t: t.reshape(B * S, DFF)
    return flat(res[0]), flat(res[1]), res[2], res[3], res[4], res[5]


def _dx1_ln1_bwd(dupre_g, dupre_v, w_up_sh, dz2, xh1, r1, ln1_g, tm, bg):
    T, D = dz2.shape
    NS, _, cs = w_up_sh.shape
    half = NS // 2
    DFF = dupre_g.shape[1]

    def body(refs, bg_hook):
        dug_ref, duv_ref, w_ref, dz2_ref, xh_ref, r_ref, g_ref, dz_ref, dzb_ref, dg_ref, db_ref = refs
        bg_hook(False)
        first = pl.program_id(0) == 0
        dg = db = None
        for rows in (slice(0, tm // 2), slice(tm // 2, tm)):
            dx1 = ALPHA * dz2_ref[rows, :]
            for k in range(NS):
                src = dug_ref if k < half else duv_ref
                c0 = (k % half) * cs
                dx1 = dx1 + _dot(src[rows, c0:c0 + cs], w_ref[k], "nt")
            xh = xh_ref[rows, :]
            dz = _ln_bwd(dx1, xh, r_ref[rows, 0:1], g_ref[...])
            dz_ref[rows, :] = dz
            dzb_ref[rows, :] = dz.astype(BF16)
            dg_h, db_h = jnp.sum(dx1 * xh, axis=0, keepdims=True), jnp.sum(dx1, axis=0, keepdims=True)
            dg, db = (dg_h, db_h) if dg is None else (dg + dg_h, db + db_h)
        _accumulate(dg_ref, first, dg)
        _accumulate(db_ref, first, db)
        bg_hook(True)

    row = pl.BlockSpec((tm, D), lambda i: (i, 0))
    vec = pl.BlockSpec((1, D), lambda i: (0, 0))
    du = pl.BlockSpec((tm, DFF), lambda i: (i, 0))
    return _hosted_call(
        body, bg, grid=(T // tm,),
        in_specs=[du, du, pl.BlockSpec((NS, D, cs), lambda i: (0, 0, 0), pipeline_mode=pl.Buffered(1)),
                  row, row, pl.BlockSpec((tm, LANES), lambda i: (i, 0)), vec],
        out_specs=[row, row, vec, vec],
        out_shape=[jax.ShapeDtypeStruct((T, D), F32), jax.ShapeDtypeStruct((T, D), BF16),
                   jax.ShapeDtypeStruct((1, D), F32), jax.ShapeDtypeStruct((1, D), F32)],
        scratch_shapes=[], operands=[dupre_g, dupre_v, w_up_sh, dz2, xh1, r1, ln1_g], name="mm_dx1_ln1_bwd")


def _dh_cat(dq, dk, dv, dag, tm):
    T, AW = dq.shape
    CW2 = dag.shape[1]
    W = 3 * AW + CW2

    def body(dq_ref, dk_ref, dv_ref, dag_ref, dh_ref, cs_ref):
        for c, ref in enumerate((dq_ref, dk_ref, dv_ref)):
            dh_ref[:, c * AW:(c + 1) * AW] = ref[...]
        dg = dag_ref[...]
        dh_ref[:, 3 * AW:] = dg.astype(BF16)
        _accumulate(cs_ref, pl.program_id(0) == 0, jnp.sum(dg, axis=0, keepdims=True))

    row = pl.BlockSpec((tm, AW), lambda i: (i, 0))
    return pl.pallas_call(
        body, grid=(T // tm,),
        in_specs=[row] * 3 + [pl.BlockSpec((tm, CW2), lambda i: (i, 0))],
        out_specs=[pl.BlockSpec((tm, W), lambda i: (i, 0)), pl.BlockSpec((1, CW2), lambda i: (0, 0))],
        out_shape=[jax.ShapeDtypeStruct((T, W), BF16), jax.ShapeDtypeStruct((1, CW2), F32)],
        compiler_params=_params(1), name="dh_cat",
    )(dq, dk, dv, dag)


def _local_step(x, target, rel_table, w_in_t, b_in, conv_w, conv_b, conv_ln_g, conv_ln_b, attn_norm_g,
                conv_norm_g, staged, ln1_g, ln1_b, ffn_cw, ffn_cb, ln2_g, ln2_b, ids):
    B, S, D = x.shape
    T = B * S
    AW = attn_norm_g.shape[-1]
    CW = conv_norm_g.shape[-1]
    H = AW // HEAD_DIM
    DFF = staged[2].shape[0] * staged[2].shape[1]
    INW = 3 * AW + 2 * CW
    xf = x.reshape(T, D)
    tf = target.reshape(T, D)
    tm = _row_tile(T, 512)
    tm_s = tm

    bucket_np, mask_np = _bucket_tables()
    bucket = jnp.asarray(bucket_np)
    band_mask = jnp.asarray(mask_np)
    bias_all = _bias_build(rel_table.T, bucket, band_mask).reshape(3, H, ATTN_BLOCK, 2 * ATTN_BLOCK)

    rowD = lambda i, j, k: (i, 0)
    vecD = lambda i, j, k: (0, 0)

    def in_proj_epilogue(acc, i, j, extra_refs, out_refs):
        h = acc + extra_refs[0][...]
        out_refs[0][...] = h[:, :3 * AW].astype(BF16)
        out_refs[1][...] = h[:, 3 * AW:]

    qkv, ag = _matmul_general(
        [(xf, (tm, D), rowD), (w_in_t, (INW, D), vecD)],
        lambda refs, i, j, k: _dot(refs[0][...], refs[1][...], "nt"),
        grid=(T // tm, 1, 1), tm=tm, tn=INW, extras=[(b_in, (1, INW), vecD)],
        outs=[((T, 3 * AW), BF16, (tm, 3 * AW), rowD), ((T, 2 * CW), F32, (tm, 2 * CW), rowD)],
        epilogue=in_proj_epilogue, name="mm_in")

    attn, lse, w_out_g, w_up_sh, w_down_g = _attention_fwd(qkv, bias_all, B, S, AW, bg=_bg_gather(staged))
    w_out = w_out_g.reshape(D, D)
    w_down = w_down_g.reshape(DFF, D)
    mixed_c = _conv_fwd(ag, conv_w, conv_b, conv_ln_g, conv_ln_b, conv_norm_g, B, S, CW)

    def attn_rstd(a):
        return lax.rsqrt(jnp.mean(a * a, axis=-1, keepdims=True) + LN_EPS)

    def mixed_rows(attn_ref, mc_ref, gain_ref, rows=slice(None)):
        a = attn_ref[rows, :]
        return jnp.concatenate([(a * attn_rstd(a) * gain_ref[...]).astype(BF16), mc_ref[rows, :]], axis=1)

    halves = [slice(0, tm // 2), slice(tm // 2, tm)]

    def ln1_epilogue(parts, i, j, extra_refs, out_refs):
        x_ref, g_ref, b_ref, a_ref = extra_refs
        for rows, acc in zip(halves, parts):
            x1, xh, r = _ln_fwd(acc + ALPHA * x_ref[rows, :], g_ref[...], b_ref[...])
            out_refs[0][rows, :] = x1
            out_refs[1][rows, :] = x1.astype(BF16)
            out_refs[2][rows, :] = xh
            out_refs[3][rows, :] = jnp.broadcast_to(r, (tm // 2, LANES))
            out_refs[4][rows, :] = jnp.broadcast_to(attn_rstd(a_ref[rows, :]), (tm // 2, LANES))

    x1, x1b, xh1, r1, r_attn = _matmul_general(
        [(attn, (tm_s, AW), rowD), (mixed_c, (tm_s, CW), rowD), (attn_norm_g, (1, AW), vecD), (w_out, (D, D), vecD)],
        lambda refs, i, j, k: tuple(_dot(mixed_rows(refs[0], refs[1], refs[2], rows), refs[3][...], "nn")
                                    for rows in halves),
        grid=(T // tm_s, 1, 1), tm=tm_s, tn=D,
        extras=[(xf, (tm_s, D), rowD), (ln1_g, (1, D), vecD), (ln1_b, (1, D), vecD), (attn, (tm_s, AW), rowD)],
        outs=[((T, D), F32, (tm_s, D), rowD), ((T, D), BF16, (tm_s, D), rowD), ((T, D), F32, (tm_s, D), rowD),
              ((T, LANES), F32, (tm_s, LANES), rowD), ((T, LANES), F32, (tm_s, LANES), rowD)],
        epilogue=ln1_epilogue, name="mm_out_ln1")

    NS, _, cs = w_up_sh.shape
    half = NS // 2

    act, upre = _ffn_fwd_fused(x1b, w_up_sh, ffn_cw, ffn_cb, B, S, DFF)

    def ln2_epilogue(parts, i, j, extra_refs, out_refs):
        x1_ref, g_ref, b_ref, t_ref = extra_refs
        dz_ref, dzb_ref, loss_ref, dg_ref, db_ref = out_refs
        g = g_ref[...]
        sums = None
        for rows, acc in zip(halves, parts):
            y, xh, r = _ln_fwd(acc + ALPHA * x1_ref[rows, :], g, b_ref[...])
            diff = y - t_ref[rows, :]
            row_loss = jnp.sum(diff * diff, axis=1, keepdims=True)
            tile_loss = jnp.sum(row_loss, axis=0, keepdims=True) * (0.5 / D)
            dy = diff * (1.0 / D)
            dz = _ln_bwd(dy, xh, r, g)
            dz_ref[rows, :] = dz
            dzb_ref[rows, :] = dz.astype(BF16)
            vals = (jnp.broadcast_to(tile_loss, (1, LANES)), jnp.sum(dy * xh, axis=0, keepdims=True),
                    jnp.sum(dy, axis=0, keepdims=True))
            sums = vals if sums is None else tuple(a + b for a, b in zip(sums, vals))
        for ref, val in zip((loss_ref, dg_ref, db_ref), sums):
            _accumulate(ref, i == 0, val)

    dz2, dz2b, loss_part, d_ln2_g, d_ln2_b = _matmul_general(
        [(act, (tm, DFF), rowD), (w_down, (DFF, D), vecD)],
        lambda refs, i, j, k: tuple(_dot(refs[0][rows, :], refs[1][...], "nn") for rows in halves),
        grid=(T // tm, 1, 1), tm=tm, tn=D,
        extras=[(x1, (tm, D), rowD), (ln2_g, (1, D), vecD), (ln2_b, (1, D), vecD), (tf, (tm, D), rowD)],
        outs=[((T, D), F32, (tm, D), rowD), ((T, D), BF16, (tm, D), rowD),
              ((1, LANES), F32, (1, LANES), vecD), ((1, D), F32, (1, D), vecD), ((1, D), F32, (1, D), vecD)],
        epilogue=ln2_epilogue, name="mm_down_ln2_loss")

    dupre_g, dupre_v, d_w_up_t, d_w_down, d_ffn_cw2, d_ffn_cb2 = _ffn_bwd_fused(
        x1b, dz2b, upre, w_down, ffn_cw, ffn_cb, B, S, DFF)
    d_w_up_t = d_w_up_t.reshape(NS, cs, D)
    d_ffn_cw = d_ffn_cw2.reshape(FFN_CONV_KERNEL, 2 * DFF)
    d_ffn_cb = d_ffn_cb2.reshape(1, 2 * DFF)
    tk_t = _row_tile(T, 512)

    early = [d_w_up_t, d_w_down.reshape(NS, DFF // NS, D)]
    dz1, dz1b, d_ln1_g, d_ln1_b, *sib_e = _dx1_ln1_bwd(dupre_g, dupre_v, w_up_sh, dz2, xh1, r1, ln1_g, tm,
                                                       bg=_bg_sibling_exchange(early))
    chip_e = [_pair_sum(g, s, ids, name="pair_sum_" + n) for g, s, n in zip(early, sib_e, ("w_up", "w_down"))]

    def dw_out_epilogue(acc, i, j, extra_refs, out_refs):
        out_refs[0][...] = acc

    d_w_out = _matmul_general(
        [(attn, (tk_t, AW), lambda i, j, k: (k, 0)), (mixed_c, (tk_t, CW), lambda i, j, k: (k, 0)),
         (attn_norm_g, (1, AW), vecD), (dz1b, (tk_t, D), lambda i, j, k: (k, 0))],
        lambda refs, i, j, k: _dot(mixed_rows(refs[0], refs[1], refs[2]), refs[3][...], "tn"),
        grid=(1, 1, T // tk_t), tm=D, tn=D, outs=[_plain_out(D, D, D, D, F32)],
        epilogue=dw_out_epilogue, name="mm_dw_out")[0]
    early.append(d_w_out.reshape(NS, D // NS, D))
    def dmixed_epilogue(parts, i, j, extra_refs, out_refs):
        a_ref, r_ref, g_ref = extra_refs
        do_ref, dd_ref, dmc_ref, dg_ref = out_refs
        head_of = lambda axis: lax.broadcasted_iota(jnp.int32, (AW, AW), axis) // HEAD_DIM
        same_head = (head_of(0) == head_of(1)).astype(BF16)
        dg = None
        for rows, acc in zip(halves, parts):
            dm = acc[:, :AW]
            dmc_ref[rows, :] = acc[:, AW:]
            a = a_ref[rows, :]
            r = r_ref[rows, 0:1]
            dxn = dm * g_ref[...]
            da = r * (dxn - a * (r * r) * jnp.mean(dxn * a, axis=-1, keepdims=True))
            do_ref[rows, :] = da.astype(BF16)
            hi, lo = _split_hi_lo(da * a)
            dd_ref[rows, :] = (jnp.dot(hi, same_head, preferred_element_type=F32)
                               + jnp.dot(lo, same_head, preferred_element_type=F32))
            dg_h = jnp.sum(dm * a * r, axis=0, keepdims=True)
            dg = dg_h if dg is None else dg + dg_h
        _accumulate(dg_ref, i == 0, dg)

    dattn, dd, dmc, d_attn_norm_g, sib_out = _matmul_general(
        [(dz1b, (tm, D), rowD), (w_out, (D, D), vecD)],
        lambda refs, i, j, k: tuple(_dot(refs[0][rows, :], refs[1][...], "nt") for rows in halves),
        grid=(T // tm, 1, 1), tm=tm, tn=D,
        extras=[(attn, (tm, AW), rowD), (r_attn, (tm, LANES), rowD), (attn_norm_g, (1, AW), vecD)],
        outs=[((T, AW), BF16, (tm, AW), rowD), ((T, AW), F32, (tm, AW), rowD), ((T, CW), F32, (tm, CW), rowD),
              ((1, AW), F32, (1, AW), vecD)],
        epilogue=dmixed_epilogue, name="mm_dmixed", bg=_bg_sibling_exchange(early[2:]))
    sib_e.append(sib_out)
    chip_e.append(_pair_sum(early[2], sib_out, ids, name="pair_sum_w_out"))

    dag, d_conv_w, d_conv_b, d_conv_ln_g, d_conv_ln_b, d_conv_norm_g = _conv_bwd(
        ag, dmc, conv_w, conv_b, conv_ln_g, conv_ln_b, conv_norm_g, B, S, CW)

    dq, dk, dv, csq, csk, csv, dbias, *got_e = _attention_bwd(qkv, dattn, lse, dd, bias_all, B, S, AW,
                                                              bg=_bg_chip_exchange(chip_e))
    full_up, full_down, full_out = [_final_sum(g, s, r, ids, name="final_sum_" + n)
                                    for g, s, r, n in zip(early, sib_e, got_e, ("w_up", "w_down", "w_out"))]
    d_rel_table = _rel_grad(dbias.reshape(3, H, ATTN_BLOCK * 2 * ATTN_BLOCK), bucket).T
    dh, cs_ag = _dh_cat(dq, dk, dv, dag, tm_s)
    d_b_in = jnp.concatenate([csq, csk, csv, cs_ag], axis=1)

    d_w_in_t = _mm_plain(dh, xf, mode="tn", tm=_col_tile(INW, 1408), tn=D, tk=tk_t, out_dtype=F32, name="mm_dw_in")
    late = [d_w_in_t.reshape(NS, INW // NS, D)]
    sib_l = _sibling_exchange(late)
    chip_l = [_pair_sum(late[0], sib_l[0], ids, name="pair_sum_w_in")]
    small = dict(rel_table=d_rel_table, b_in=d_b_in, conv_w=d_conv_w, conv_b=d_conv_b, conv_ln_g=d_conv_ln_g,
                 conv_ln_b=d_conv_ln_b, attn_norm_g=d_attn_norm_g, conv_norm_g=d_conv_norm_g, ln1_g=d_ln1_g,
                 ln1_b=d_ln1_b, ffn_conv_w=d_ffn_cw, ffn_conv_b=d_ffn_cb, ln2_g=d_ln2_g, ln2_b=d_ln2_b)
    pack = _pack([loss_part] + [small[n] for n in SMALL_NAMES])

    def gx_epilogue(acc, i, j, extra_refs, out_refs):
        out_refs[0][...] = acc + ALPHA * extra_refs[0][...]

    grad_x, got_in, all_packs = _matmul(
        dh, w_in_t, mode="nn", tm=tm, tn=D, tk=INW, extras=[(dz1, (tm, D), rowD)],
        outs=[((T, D), F32, (tm, D), rowD)], epilogue=gx_epilogue, name="mm_grad_x",
        bg=_bg_chip_exchange(chip_l, pack))
    full_in = _final_sum(late[0], sib_l[0], got_in, ids, name="final_sum_w_in")
    return grad_x.reshape(B, S, D), [full_in, full_out, full_up, full_down], all_packs


def _place():
    return lax.axis_index("x"), lax.axis_index("y"), lax.axis_index("c")


CHIP_FLIPS = ((1, 0), (0, 1), (1, 1))


def _flip(v, f):
    return 1 - v if f else v


HBM_SPEC = pl.BlockSpec(memory_space=pl.ANY)
VMEM_SPEC = pl.BlockSpec(memory_space=pltpu.VMEM)
COMM_PARAMS = pltpu.CompilerParams(vmem_limit_bytes=VMEM_LIMIT)


def _gather_weights(big, small):
    nb, ns = len(big), len(small)

    def body(*refs):
        big_in = refs[:nb]
        small_in = refs[nb:nb + ns]
        big_out = refs[nb + ns:2 * nb + ns]
        small_out = refs[2 * nb + ns:2 * nb + 2 * ns]
        stages = refs[2 * nb + 2 * ns:3 * nb + 2 * ns]
        send_sems, recv_sems, local_sems = refs[3 * nb + 2 * ns:]
        x, y, c = _place()
        s_me = 2 * x + y
        sibling = (x, y, 1 - c)
        started, local_copies = [], []
        for a in range(nb):
            rh = big[a].shape[0] // 2
            lo = pl.multiple_of(c * rh, 16)
            stages[a][...] = big_in[a][pl.ds(lo, rh), :].astype(BF16)
            mine = big_out[a].at[s_me, pl.ds(lo, rh), :]
            loc = pltpu.make_async_copy(stages[a], mine, local_sems.at[a])
            loc.start()
            local_copies.append(loc)
            targets = [sibling] + [(_flip(x, fx), _flip(y, fy), c) for fx, fy in CHIP_FLIPS]
            for k, to in enumerate(targets):
                cp = pltpu.make_async_remote_copy(stages[a], mine, send_sems.at[a * 7 + k],
                                                  recv_sems.at[a * 7 + k], device_id=to, device_id_type=MESH)
                cp.start()
                started.append(cp)
        for a in range(ns):
            mine = small_out[a].at[s_me]
            loc = pltpu.make_async_copy(small_in[a], mine, local_sems.at[nb + a])
            loc.start()
            local_copies.append(loc)
            for k, (fx, fy) in enumerate(CHIP_FLIPS):
                cp = pltpu.make_async_remote_copy(small_in[a], mine, send_sems.at[nb * 7 + a * 3 + k],
                                                  recv_sems.at[nb * 7 + a * 3 + k],
                                                  device_id=(_flip(x, fx), _flip(y, fy), c), device_id_type=MESH)
                cp.start()
                started.append(cp)
        for a in range(nb):
            rh = big[a].shape[0] // 2
            lo = pl.multiple_of(c * rh, 16)
            for k, (fx, fy) in enumerate(CHIP_FLIPS):
                s_from = 2 * _flip(x, fx) + _flip(y, fy)
                got = big_out[a].at[s_from, pl.ds(lo, rh), :]
                pltpu.make_async_remote_copy(got, got, send_sems.at[a * 7 + 1 + k], recv_sems.at[a * 7 + 1 + k],
                                             device_id=sibling, device_id_type=MESH).wait_recv()
                fwd = pltpu.make_async_remote_copy(got, got, send_sems.at[a * 7 + 4 + k],
                                                   recv_sems.at[a * 7 + 4 + k], device_id=sibling,
                                                   device_id_type=MESH)
                fwd.start()
                started.append(fwd)
        for a in range(nb):
            rh = big[a].shape[0] // 2
            lo_sib = pl.multiple_of((1 - c) * rh, 16)
            for k in (0, 4, 5, 6):
                any_rows = big_out[a].at[s_me, pl.ds(lo_sib, rh), :]
                pltpu.make_async_remote_copy(any_rows, any_rows, send_sems.at[a * 7 + k], recv_sems.at[a * 7 + k],
                                             device_id=sibling, device_id_type=MESH).wait_recv()
        for a in range(ns):
            for k in range(3):
                pltpu.make_async_remote_copy(small_in[a], small_out[a].at[s_me], send_sems.at[nb * 7 + a * 3 + k],
                                             recv_sems.at[nb * 7 + a * 3 + k], device_id=sibling,
                                             device_id_type=MESH).wait_recv()
        for cp in started:
            cp.wait_send()
        for cp in local_copies:
            cp.wait()

    n_sem = nb * 7 + ns * 3
    out_shape = ([jax.ShapeDtypeStruct((N_SHARDS,) + w.shape, BF16) for w in big]
                 + [jax.ShapeDtypeStruct((N_SHARDS,) + w.shape, F32) for w in small])
    res = pl.pallas_call(
        body, in_specs=[VMEM_SPEC] * nb + [HBM_SPEC] * ns, out_specs=[HBM_SPEC] * (nb + ns),
        out_shape=out_shape,
        scratch_shapes=[pltpu.VMEM((w.shape[0] // 2, w.shape[1]), BF16) for w in big]
        + [pltpu.SemaphoreType.DMA((n_sem,)), pltpu.SemaphoreType.DMA((n_sem,)),
           pltpu.SemaphoreType.DMA((nb + ns,))],
        compiler_params=COMM_PARAMS, name="gather_weights",
    )(*big, *small)
    return res[:nb], res[nb:]


def _sibling_exchange(grads):
    n = len(grads)

    def body(*refs):
        g_in = refs[:n]
        got = refs[n:2 * n]
        send_sems, recv_sems = refs[2 * n:]
        x, y, c = _place()
        cps = []
        for a in range(n):
            rh = grads[a].shape[1] // 2
            lo = pl.multiple_of((1 - c) * rh, 8)
            cp = pltpu.make_async_remote_copy(g_in[a].at[:, pl.ds(lo, rh), :], got[a], send_sems.at[a],
                                              recv_sems.at[a], device_id=(x, y, 1 - c), device_id_type=MESH)
            cp.start()
            cps.append(cp)
        for cp in cps:
            cp.wait()

    return pl.pallas_call(
        body, in_specs=[HBM_SPEC] * n, out_specs=[HBM_SPEC] * n,
        out_shape=[jax.ShapeDtypeStruct((N_SHARDS, g.shape[1] // 2, g.shape[2]), F32) for g in grads],
        scratch_shapes=[pltpu.SemaphoreType.DMA((n,)), pltpu.SemaphoreType.DMA((n,))],
        compiler_params=COMM_PARAMS, name="sibling_exchange",
    )(*grads)


def _sibling_assemble(fulls):
    n = len(fulls)

    def body(*refs):
        full = refs[n:2 * n]
        send_sems, recv_sems = refs[2 * n:]
        x, y, c = _place()
        cps = []
        for a in range(n):
            rh = fulls[a].shape[0] // 2
            mine = full[a].at[pl.ds(pl.multiple_of(c * rh, 8), rh), :]
            cp = pltpu.make_async_remote_copy(mine, mine, send_sems.at[a], recv_sems.at[a],
                                              device_id=(x, y, 1 - c), device_id_type=MESH)
            cp.start()
            cps.append(cp)
        for cp in cps:
            cp.wait()

    return pl.pallas_call(
        body, in_specs=[HBM_SPEC] * n, out_specs=[HBM_SPEC] * n,
        out_shape=[jax.ShapeDtypeStruct(f.shape, F32) for f in fulls],
        input_output_aliases={a: a for a in range(n)},
        scratch_shapes=[pltpu.SemaphoreType.DMA((n,)), pltpu.SemaphoreType.DMA((n,))],
        compiler_params=COMM_PARAMS, name="sibling_assemble",
    )(*fulls)


def _remote(ref_src, ref_dst, send_sems, recv_sems, k, to):
    return pltpu.make_async_remote_copy(ref_src, ref_dst, send_sems.at[k], recv_sems.at[k], device_id=to,
                                        device_id_type=MESH)


def _stage_half(w, ids, name):
    R, C = w.shape
    rh = R // 2
    rt = _half_tile(rh)
    nt = rh // rt

    def body(ids_ref, w_ref, o_ref):
        o_ref[0] = w_ref[...].astype(BF16)

    grid_spec = pltpu.PrefetchScalarGridSpec(
        num_scalar_prefetch=1, grid=(nt,),
        in_specs=[pl.BlockSpec((rt, C), lambda i, ids: (ids[2] * nt + i, 0))],
        out_specs=pl.BlockSpec((1, rt, C), lambda i, ids: (2 * ids[0] + ids[1], ids[2] * nt + i, 0)))
    return pl.pallas_call(body, grid_spec=grid_spec, out_shape=jax.ShapeDtypeStruct((N_SHARDS, R, C), BF16),
                          compiler_params=_params(1), name=name)(ids, w)


def _bg_gather(staged):
    n = len(staged)

    def run(step, n_steps, ins, outs, send_sems, recv_sems, local_sems, post):
        x, y, c = _place()
        s_me = 2 * x + y
        sibling = (x, y, 1 - c)
        chips = [(_flip(x, fx), _flip(y, fy)) for fx, fy in CHIP_FLIPS]

        def rows(a, s, half):
            rh = staged[a].shape[1] // 2
            return outs[a].at[s, pl.ds(pl.multiple_of(half * rh, 16), rh), :]

        def copy(a, k, ref, to):
            return _remote(ref, ref, send_sems, recv_sems, a * 7 + k, to)

        if not post:
            @pl.when(step == 0)
            def _():
                for a in range(n):
                    mine = rows(a, s_me, c)
                    copy(a, 0, mine, sibling).start()
                    for k, (px, py) in enumerate(chips):
                        copy(a, 1 + k, mine, (px, py, c)).start()

            @pl.when(step == max(n_steps - 2, 0))
            def _():
                for a in range(n):
                    for k, (px, py) in enumerate(chips):
                        got = rows(a, 2 * px + py, c)
                        copy(a, 1 + k, got, sibling).wait_recv()
                        copy(a, 4 + k, got, sibling).start()
        else:
            @pl.when(step == n_steps - 1)
            def _():
                for a in range(n):
                    for k in (0, 4, 5, 6):
                        copy(a, k, rows(a, s_me, 1 - c), sibling).wait_recv()
                    for k in range(7):
                        copy(a, k, rows(a, s_me, c), sibling).wait_send()

    return _Background(staged, [jax.ShapeDtypeStruct(g.shape, g.dtype) for g in staged],
                       {a: a for a in range(n)}, 7 * n, run)


def _bg_sibling_exchange(grads):
    n = len(grads)

    def run(step, n_steps, ins, outs, send_sems, recv_sems, local_sems, post):
        x, y, c = _place()

        def copy(a):
            rh = grads[a].shape[1] // 2
            lo = pl.multiple_of((1 - c) * rh, 8)
            return _remote(ins[a].at[:, pl.ds(lo, rh), :], outs[a], send_sems, recv_sems, a, (x, y, 1 - c))

        if not post:
            @pl.when(step == 0)
            def _():
                for a in range(n):
                    copy(a).start()
        else:
            @pl.when(step == n_steps - 1)
            def _():
                for a in range(n):
                    copy(a).wait()

    return _Background(grads, [jax.ShapeDtypeStruct((N_SHARDS, g.shape[1] // 2, g.shape[2]), F32) for g in grads],
                       {}, n, run)


def _bg_chip_exchange(chip_parts, pack=None):
    n = len(chip_parts)

    def run(step, n_steps, ins, outs, send_sems, recv_sems, local_sems, post):
        x, y, c = _place()
        me = 4 * x + 2 * y + c

        def copies():
            cps = []
            for a in range(n):
                for k, (fx, fy) in enumerate(CHIP_FLIPS):
                    px, py = _flip(x, fx), _flip(y, fy)
                    cps.append(_remote(ins[a].at[2 * px + py], outs[a].at[k], send_sems, recv_sems, a * 3 + k,
                                       (px, py, c)))
            if pack is not None:
                for m in range(1, N_DEV):
                    to = (_flip(x, m & 4), _flip(y, m & 2), _flip(c, m & 1))
                    cps.append(_remote(ins[n], outs[n].at[me], send_sems, recv_sems, n * 3 + m - 1, to))
            return cps

        def local():
            return pltpu.make_async_copy(ins[n], outs[n].at[me], local_sems.at[0])

        if not post:
            @pl.when(step == 0)
            def _():
                for cp in copies():
                    cp.start()
                if pack is not None:
                    local().start()
        else:
            @pl.when(step == n_steps - 1)
            def _():
                for cp in copies():
                    cp.wait()
                if pack is not None:
                    local().wait()

    in_arrays = list(chip_parts) + ([pack] if pack is not None else [])
    out_shapes = [jax.ShapeDtypeStruct((3,) + p.shape[1:], BF16) for p in chip_parts]
    if pack is not None:
        out_shapes.append(jax.ShapeDtypeStruct((N_DEV, pack.shape[0], LANES), F32))
    return _Background(in_arrays, out_shapes, {}, n * 3 + N_DEV - 1, run)


def _half_tile(rh, mult=16, want=512):
    best = None
    for t in range(mult, min(rh, want) + 1, mult):
        if rh % t == 0:
            best = t
    return best if best is not None else rh


def _pair_sum(g, sib, ids, name):
    _, R, C = g.shape
    rh = R // 2
    rt = _half_tile(rh)
    nt = rh // rt

    def body(ids_ref, g_ref, s_ref, o_ref):
        o_ref[...] = (g_ref[...] + s_ref[...]).astype(BF16)

    def other(j, ids):
        return j + (j >= 2 * ids[0] + ids[1]).astype(jnp.int32)

    grid_spec = pltpu.PrefetchScalarGridSpec(
        num_scalar_prefetch=1, grid=(N_SHARDS - 1, nt),
        in_specs=[pl.BlockSpec((1, rt, C), lambda j, i, ids: (other(j, ids), ids[2] * nt + i, 0)),
                  pl.BlockSpec((1, rt, C), lambda j, i, ids: (other(j, ids), i, 0))],
        out_specs=pl.BlockSpec((1, rt, C), lambda j, i, ids: (other(j, ids), i, 0)))
    return pl.pallas_call(body, grid_spec=grid_spec, out_shape=jax.ShapeDtypeStruct((N_SHARDS, rh, C), BF16),
                          compiler_params=_params(2), name=name)(ids, g, sib)


def _final_sum(g, sib, got, ids, name):
    _, R, C = g.shape
    rh = R // 2
    rt = _half_tile(rh)
    nt = rh // rt

    def body(ids_ref, g_ref, s_ref, r_ref, o_ref):
        tot = g_ref[0] + s_ref[0]
        for k in range(3):
            tot = tot + r_ref[k].astype(F32)
        o_ref[...] = tot

    grid_spec = pltpu.PrefetchScalarGridSpec(
        num_scalar_prefetch=1, grid=(nt,),
        in_specs=[pl.BlockSpec((1, rt, C), lambda i, ids: (2 * ids[0] + ids[1], ids[2] * nt + i, 0)),
                  pl.BlockSpec((1, rt, C), lambda i, ids: (2 * ids[0] + ids[1], i, 0)),
                  pl.BlockSpec((3, rt, C), lambda i, ids: (0, i, 0))],
        out_specs=pl.BlockSpec((rt, C), lambda i, ids: (ids[2] * nt + i, 0)))
    return pl.pallas_call(body, grid_spec=grid_spec, out_shape=jax.ShapeDtypeStruct((R, C), F32),
                          compiler_params=_params(1), name=name)(ids, g, sib, got)


def _sum_packs(all_packs):
    def body(p_ref, o_ref):
        tot = p_ref[0]
        for i in range(1, N_DEV):
            tot = tot + p_ref[i]
        o_ref[...] = tot

    return pl.pallas_call(body, in_specs=[VMEM_SPEC], out_specs=VMEM_SPEC,
                          out_shape=jax.ShapeDtypeStruct(all_packs.shape[1:], F32), name="sum_packs")(all_packs)


def _adamw(w, g, m, v, name, g_transposed=False):
    R, C = w.shape
    rt = _half_tile(R, mult=LANES if g_transposed else 8, want=256)

    def body(w_ref, g_ref, m_ref, v_ref, g_out_ref, d_ref, nm_ref, nv_ref):
        gg = g_ref[...].T if g_transposed else g_ref[...]
        g_out_ref[...] = gg
        d_ref[...], nm_ref[...], nv_ref[...] = _adamw_update(w_ref[...], gg, m_ref[...], v_ref[...])

    spec = pl.BlockSpec((rt, C), lambda i: (i, 0))
    g_spec = pl.BlockSpec((C, rt), lambda i: (0, i)) if g_transposed else spec
    return pl.pallas_call(body, grid=(R // rt,), in_specs=[spec, g_spec, spec, spec], out_specs=[spec] * 4,
                          out_shape=[jax.ShapeDtypeStruct((R, C), F32)] * 4,
                          compiler_params=_params(1), name=name)(w, g, m, v)


def _adamw_update(w, g, m, v):
    nm = ADAM_B1 * m + (1.0 - ADAM_B1) * g
    nv = ADAM_B2 * v + (1.0 - ADAM_B2) * (g * g)
    m_hat = nm / (1.0 - ADAM_B1 ** ADAM_STEP)
    v_hat = nv / (1.0 - ADAM_B2 ** ADAM_STEP)
    return -ADAM_LR * (m_hat / (jnp.sqrt(v_hat) + ADAM_EPS) + ADAM_WD * w), nm, nv


def _adamw_many(ws, gs, ms, vs, name):
    n = len(ws)

    def body(*refs):
        for i in range(n):
            d, nm, nv = _adamw_update(refs[i][...], refs[n + i][...], refs[2 * n + i][...], refs[3 * n + i][...])
            refs[4 * n + i][...] = d
            refs[5 * n + i][...] = nm
            refs[6 * n + i][...] = nv

    return pl.pallas_call(body, in_specs=[VMEM_SPEC] * (4 * n), out_specs=[VMEM_SPEC] * (3 * n),
                          out_shape=[jax.ShapeDtypeStruct(w.shape, F32) for w in ws] * 3, name=name,
                          )(*ws, *gs, *ms, *vs)


def _pack(pieces):
    rows = []
    for p in pieces:
        flat = p.reshape(-1)
        pad = (-flat.shape[0]) % LANES
        if pad:
            flat = jnp.concatenate([flat, jnp.zeros((pad,), F32)])
        rows.append(flat.reshape(-1, LANES))
    total = sum(r.shape[0] for r in rows)
    pad_rows = (-total) % 8
    if pad_rows:
        rows.append(jnp.zeros((pad_rows, LANES), F32))
    return jnp.concatenate(rows, axis=0)


def _unpack(buf, shapes):
    out, r0 = [], 0
    for shp in shapes:
        n = int(np.prod(shp))
        nr = -(-n // LANES)
        out.append(buf[r0:r0 + nr].reshape(-1)[:n].reshape(shp))
        r0 += nr
    return out


SMALL_NAMES = ("rel_table", "b_in", "conv_w", "conv_b", "conv_ln_g", "conv_ln_b", "attn_norm_g", "conv_norm_g",
               "ln1_g", "ln1_b", "ffn_conv_w", "ffn_conv_b", "ln2_g", "ln2_b")
BIG_NAMES = ("w_in", "w_out", "w_up", "w_down")
WEIGHT_ORDER = ("rel_table", "w_in", "b_in", "conv_w", "conv_b", "conv_ln_g", "conv_ln_b", "attn_norm_g",
                "conv_norm_g", "w_out", "ln1_g", "ln1_b", "w_up", "ffn_conv_w", "ffn_conv_b", "w_down",
                "ln2_g", "ln2_b")


def kernel(x, rel_table, w_in, b_in, conv_w, conv_b, conv_ln_g, conv_ln_b, attn_norm_g, conv_norm_g, w_out, ln1_g, ln1_b, w_up, ffn_conv_w, ffn_conv_b, w_down, ln2_g, ln2_b, loss_target, m_rel_table, m_w_in, m_b_in, m_conv_w, m_conv_b, m_conv_ln_g, m_conv_ln_b, m_attn_norm_g, m_conv_norm_g, m_w_out, m_ln1_g, m_ln1_b, m_w_up, m_ffn_conv_w, m_ffn_conv_b, m_w_down, m_ln2_g, m_ln2_b, v_rel_table, v_w_in, v_b_in, v_conv_w, v_conv_b, v_conv_ln_g, v_conv_ln_b, v_attn_norm_g, v_conv_norm_g, v_w_out, v_ln1_g, v_ln1_b, v_w_up, v_ffn_conv_w, v_ffn_conv_b, v_w_down, v_ln2_g, v_ln2_b):
    args = dict(locals())
    weights = {n: args[n] for n in WEIGHT_ORDER}
    moms = {n: args["m_" + n] for n in WEIGHT_ORDER}
    vels = {n: args["v_" + n] for n in WEIGHT_ORDER}
    xi, yi, ci = _place()
    ids = jnp.stack([xi, yi, ci]).astype(jnp.int32)
    shard = 2 * xi + yi
    D = x.shape[-1]
    DFF = w_down.shape[1] * N_SHARDS
    CW = conv_norm_g.shape[-1]

    tr = lambda t: jnp.transpose(t[0])
    (g_in,), (g_cw, g_fcw) = _gather_weights([tr(w_in)], [conv_w[0], ffn_conv_w[0]])
    cols = lambda t: jnp.transpose(t, (1, 0, 2)).reshape(t.shape[1], N_SHARDS * t.shape[2])
    staged = [_stage_half(w[0], ids, name="stage_" + n) for w, n in ((w_out, "w_out"), (w_up, "w_up"),
                                                                     (w_down, "w_down"))]

    grad_x, fulls, all_packs = _local_step(
        x, loss_target, rel_table, g_in.reshape(-1, D), b_in, cols(g_cw), conv_b, conv_ln_g, conv_ln_b, attn_norm_g,
        conv_norm_g, staged, ln1_g, ln1_b, cols(g_fcw), ffn_conv_b, ln2_g, ln2_b, ids)
    big_grads = dict(zip(BIG_NAMES, _sibling_assemble(fulls)))

    summed = _sum_packs(all_packs)
    full_shapes = {n: weights[n].shape for n in SMALL_NAMES}
    full_shapes["conv_w"] = (1, CONV_KERNEL, CW)
    full_shapes["ffn_conv_w"] = (1, FFN_CONV_KERNEL, 2 * DFF)
    un = _unpack(summed, [(1, LANES)] + [full_shapes[n] for n in SMALL_NAMES])
    loss = un[0][0, 0]
    small_grads = dict(zip(SMALL_NAMES, un[1:]))
    for n in ("conv_w", "ffn_conv_w"):
        width = weights[n].shape[-1]
        small_grads[n] = lax.dynamic_slice_in_dim(small_grads[n], shard * width, width, axis=2)

    grads, delta, new_m, new_v = {}, {}, {}, {}
    for n in BIG_NAMES:
        shp = weights[n].shape
        g2 = big_grads[n]
        if n == "w_in":
            res = [jnp.transpose(t) for t in _adamw(tr(weights[n]), g2, tr(moms[n]), tr(vels[n]), name="adamw_" + n)]
        else:
            res = _adamw(weights[n][0], g2, moms[n][0], vels[n][0], name="adamw_" + n, g_transposed=n == "w_up")
        grads[n], delta[n], new_m[n], new_v[n] = (t.reshape(shp) for t in res)
    pick = lambda src: [src[n] for n in SMALL_NAMES]
    small_out = _adamw_many(pick(weights), pick(small_grads), pick(moms), pick(vels), name="adamw_small")
    ns = len(SMALL_NAMES)
    for tgt, part in ((delta, small_out[:ns]), (new_m, small_out[ns:2 * ns]), (new_v, small_out[2 * ns:])):
        tgt.update(zip(SMALL_NAMES, part))
    grads.update(small_grads)

    return (loss, grad_x, *[grads[n] for n in WEIGHT_ORDER], *[delta[n] for n in WEIGHT_ORDER],
            *[new_m[n] for n in WEIGHT_ORDER], *[new_v[n] for n in WEIGHT_ORDER])
```

```python
import math

import numpy as np
import jax
import jax.numpy as jnp
from jax import lax
from jax.experimental import pallas as pl
from jax.experimental.pallas import tpu as pltpu

F32 = jnp.float32
BF16 = jnp.bfloat16
MESH = pl.DeviceIdType.MESH

HEAD_DIM = 64
LANES = 128
ATTN_BLOCK = 128
DILATED_CONFIGS = ((128, 1), (512, 4), (2048, 16))
CONV_KERNEL = 31
FFN_CONV_KERNEL = 3
REL_BUCKETS = 32
REL_MAX_DIST = 2048
DEPTH = 1
ALPHA = (2 * DEPTH) ** 0.25
LN_EPS = 1e-5
NEG_INF = -1e30
QK_SCALE = 1.0 / math.sqrt(HEAD_DIM)
ADAM_LR = 0.001
ADAM_B1 = 0.9
ADAM_B2 = 0.999
ADAM_EPS = 1e-08
ADAM_WD = 0.01
ADAM_STEP = 10
VMEM_LIMIT = 52 * 1024 * 1024
FFN_COLS = 128
N_SHARDS = 4
N_DEV = 8


def _params(n_axes):
    return pltpu.CompilerParams(dimension_semantics=("arbitrary",) * n_axes,
                                vmem_limit_bytes=VMEM_LIMIT)


MM_DIMS = {"nn": (((1,), (0,)), ((), ())), "nt": (((1,), (1,)), ((), ())), "tn": (((0,), (0,)), ((), ()))}


class _Background:
    def __init__(self, in_arrays, out_shapes, aliases, n_sems, run, n_local=1):
        self.in_arrays, self.out_shapes, self.aliases = list(in_arrays), list(out_shapes), dict(aliases)
        self.n_sems, self.n_local, self.run = n_sems, n_local, run

    def scratch(self):
        return [pltpu.SemaphoreType.DMA((self.n_sems,)), pltpu.SemaphoreType.DMA((self.n_sems,)),
                pltpu.SemaphoreType.DMA((self.n_local,))]


def _hosted_call(body, bg, *, grid, in_specs, out_specs, out_shape, scratch_shapes, operands, name):
    n_in, n_out, n_scr = len(in_specs), len(out_specs), len(scratch_shapes)
    if bg is None:
        return pl.pallas_call(lambda *refs: body(refs, lambda post: None), grid=grid, in_specs=in_specs,
                              out_specs=out_specs, out_shape=out_shape, scratch_shapes=scratch_shapes,
                              compiler_params=_params(len(grid)), name=name)(*operands)
    nb_in, nb_out = len(bg.in_arrays), len(bg.out_shapes)
    n_steps = int(np.prod(grid))

    def full_body(*refs):
        own = refs[:n_in] + refs[n_in + nb_in:n_in + nb_in + n_out] \
            + refs[n_in + nb_in + n_out + nb_out:n_in + nb_in + n_out + nb_out + n_scr]
        bg_in = refs[n_in:n_in + nb_in]
        bg_out = refs[n_in + nb_in + n_out:n_in + nb_in + n_out + nb_out]
        sems = refs[n_in + nb_in + n_out + nb_out + n_scr:]
        step = pl.program_id(0)
        for ax in range(1, len(grid)):
            step = step * grid[ax] + pl.program_id(ax)

        def hook(post):
            bg.run(step, n_steps, bg_in, bg_out, *sems, post)

        body(own, hook)

    res = pl.pallas_call(
        full_body, grid=grid, in_specs=list(in_specs) + [HBM_SPEC] * nb_in,
        out_specs=list(out_specs) + [HBM_SPEC] * nb_out, out_shape=list(out_shape) + bg.out_shapes,
        input_output_aliases={n_in + a: n_out + o for a, o in bg.aliases.items()},
        scratch_shapes=list(scratch_shapes) + bg.scratch(), compiler_params=_params(len(grid)), name=name,
    )(*operands, *bg.in_arrays)
    return res


def _matmul_general(ins, part_fn, *, grid, tm, tn, outs, epilogue, extras=(), name, bg=None):
    nk = grid[2]
    n_in, n_extra = len(ins), len(extras)

    def body(refs, bg_hook):
        in_refs = refs[:n_in]
        rest = refs[n_in:]
        extra_refs = rest[:n_extra]
        out_refs = rest[n_extra:n_extra + len(outs)]
        acc_ref = rest[-1]
        i, j, k = pl.program_id(0), pl.program_id(1), pl.program_id(2)
        bg_hook(False)
        part = part_fn(in_refs, i, j, k)
        if nk == 1:
            epilogue(part, i, j, extra_refs, out_refs)
        else:
            @pl.when(k == 0)
            def _():
                acc_ref[...] = part

            @pl.when(k > 0)
            def _():
                acc_ref[...] += part

            @pl.when(k == nk - 1)
            def _():
                epilogue(acc_ref[...], i, j, extra_refs, out_refs)
        bg_hook(True)

    in_specs = [pl.BlockSpec(bs, im) for (_, bs, im) in list(ins) + list(extras)]
    out_specs = [pl.BlockSpec(bs, im) for (_, _, bs, im) in outs]
    out_shape = [jax.ShapeDtypeStruct(s, d) for (s, d, _, _) in outs]
    return _hosted_call(body, bg, grid=grid, in_specs=in_specs, out_specs=out_specs, out_shape=out_shape,
                        scratch_shapes=[pltpu.VMEM((tm, tn), F32)],
                        operands=[e[0] for e in ins] + [e[0] for e in extras], name=name)


def _dot(a, b, mode):
    return lax.dot_general(a.astype(BF16), b.astype(BF16), MM_DIMS[mode], preferred_element_type=F32)


def _matmul(a, b, *, mode, tm, tn, tk, outs, epilogue, extras=(), name, bg=None):
    if mode == "tn":
        K, M = a.shape
        N = b.shape[1]
        ins = [(a, (tk, tm), lambda i, j, k: (k, i)), (b, (tk, tn), lambda i, j, k: (k, j))]
    elif mode == "nt":
        M, K = a.shape
        N = b.shape[0]
        ins = [(a, (tm, tk), lambda i, j, k: (i, k)), (b, (tn, tk), lambda i, j, k: (j, k))]
    else:
        M, K = a.shape
        N = b.shape[1]
        ins = [(a, (tm, tk), lambda i, j, k: (i, k)), (b, (tk, tn), lambda i, j, k: (k, j))]
    assert M % tm == 0 and N % tn == 0 and K % tk == 0, (name, M, N, K, tm, tn, tk)

    def part_fn(in_refs, i, j, k):
        return _dot(in_refs[0][...], in_refs[1][...], mode)

    return _matmul_general(ins, part_fn, grid=(M // tm, N // tn, K // tk), tm=tm, tn=tn, outs=outs,
                           epilogue=epilogue, extras=extras, name=name, bg=bg)


def _plain_out(M, N, tm, tn, dtype):
    return ((M, N), dtype, (tm, tn), lambda i, j, k: (i, j))


def _mm_plain(a, b, *, mode, tm, tn, tk, out_dtype, name, bias=None, bg=None):
    if mode == "tn":
        M, N = a.shape[1], b.shape[1]
    elif mode == "nt":
        M, N = a.shape[0], b.shape[0]
    else:
        M, N = a.shape[0], b.shape[1]
    extras = []
    if bias is not None:
        extras.append((bias, (1, tn), lambda i, j, k: (0, j)))

    def epilogue(acc, i, j, extra_refs, out_refs):
        if bias is not None:
            acc = acc + extra_refs[0][...]
        out_refs[0][...] = acc.astype(out_dtype)

    res = _matmul(a, b, mode=mode, tm=tm, tn=tn, tk=tk, outs=[_plain_out(M, N, tm, tn, out_dtype)],
                  epilogue=epilogue, extras=extras, name=name, bg=bg)
    return res[0] if bg is None else res


def _row_tile(T, want):
    t = min(T, want)
    while T % t:
        t //= 2
    return t


def _col_tile(N, want):
    if N <= want:
        return N
    best = None
    for c in range(LANES, want + 1, LANES):
        if N % c == 0:
            best = c
    return best if best is not None else N


def _accumulate(ref, first, val):
    @pl.when(first)
    def _():
        ref[...] = val

    @pl.when(jnp.logical_not(first))
    def _():
        ref[...] += val


def _ln_fwd(z, g, b):
    mu = jnp.mean(z, axis=-1, keepdims=True)
    zc = z - mu
    var = jnp.mean(zc * zc, axis=-1, keepdims=True)
    r = lax.rsqrt(var + LN_EPS)
    xh = zc * r
    return xh * g + b, xh, r


def _ln_bwd(dy, xh, r, g):
    dxh = dy * g
    m1 = jnp.mean(dxh, axis=-1, keepdims=True)
    m2 = jnp.mean(dxh * xh, axis=-1, keepdims=True)
    return r * (dxh - m1 - xh * m2)


def _sigmoid(x):
    return 0.5 * jnp.tanh(0.5 * x) + 0.5


def _bucket_tables():
    exact = REL_BUCKETS // 2
    qi = np.arange(ATTN_BLOCK)[:, None]
    kj = np.arange(2 * ATTN_BLOCK)[None, :]
    steps = qi + ATTN_BLOCK - kj
    buckets, masks = [], []
    for window, dilation in DILATED_CONFIGS:
        max_steps = window // dilation
        band = (steps >= 0) & (steps <= max_steps)
        dist = np.maximum(steps, 0) * dilation
        d_f = np.maximum(dist, 1).astype(np.float32)
        large = exact + (np.log(d_f / np.float32(exact)) / np.float32(math.log(REL_MAX_DIST / exact))
                         * np.float32(REL_BUCKETS - exact)).astype(np.int32)
        large = np.minimum(large, REL_BUCKETS - 1)
        bucket = np.where(dist < exact, dist, large).astype(np.int32)
        buckets.append(bucket.reshape(1, -1))
        masks.append(np.where(band, 0.0, NEG_INF).astype(np.float32).reshape(1, -1))
    return np.stack(buckets), np.stack(masks)


def _split_hi_lo(x):
    hi = x.astype(BF16)
    lo = (x - hi.astype(F32)).astype(BF16)
    return hi, lo


def _bias_build(rel_table_t, bucket, mask):
    H = rel_table_t.shape[0]
    n = bucket.shape[-1]

    def body(t_ref, bkt_ref, mask_ref, o_ref):
        onehot = (lax.broadcasted_iota(jnp.int32, (REL_BUCKETS, n), 0) == bkt_ref[0]).astype(BF16)
        t = t_ref[...]
        t1 = t.astype(BF16)
        r1 = t - t1.astype(F32)
        t2 = r1.astype(BF16)
        t3 = (r1 - t2.astype(F32)).astype(BF16)
        acc = jnp.dot(t1, onehot, preferred_element_type=F32)
        acc = acc + jnp.dot(t2, onehot, preferred_element_type=F32)
        acc = acc + jnp.dot(t3, onehot, preferred_element_type=F32)
        o_ref[0] = acc + mask_ref[0]

    return pl.pallas_call(
        body, grid=(3,),
        in_specs=[pl.BlockSpec((H, REL_BUCKETS), lambda b: (0, 0)),
                  pl.BlockSpec((1, 1, n), lambda b: (b, 0, 0)),
                  pl.BlockSpec((1, 1, n), lambda b: (b, 0, 0))],
        out_specs=pl.BlockSpec((1, H, n), lambda b: (b, 0, 0)),
        out_shape=jax.ShapeDtypeStruct((3, H, n), F32),
        compiler_params=_params(1), name="bias_build",
    )(rel_table_t, bucket, mask)


def _rel_grad(dbias, bucket):
    H = dbias.shape[1]
    n = bucket.shape[-1]
    dims = (((1,), (1,)), ((), ()))

    def body(d_ref, bkt_ref, o_ref):
        b = pl.program_id(0)
        onehot = (lax.broadcasted_iota(jnp.int32, (REL_BUCKETS, n), 0) == bkt_ref[0]).astype(BF16)
        d = d_ref[0]
        d1 = d.astype(BF16)
        r1 = d - d1.astype(F32)
        d2 = r1.astype(BF16)
        d3 = (r1 - d2.astype(F32)).astype(BF16)
        acc = lax.dot_general(d1, onehot, dims, preferred_element_type=F32)
        acc = acc + lax.dot_general(d2, onehot, dims, preferred_element_type=F32)
        acc = acc + lax.dot_general(d3, onehot, dims, preferred_element_type=F32)
        _accumulate(o_ref, b == 0, acc)

    return pl.pallas_call(
        body, grid=(3,),
        in_specs=[pl.BlockSpec((1, H, n), lambda b: (b, 0, 0)),
                  pl.BlockSpec((1, 1, n), lambda b: (b, 0, 0))],
        out_specs=pl.BlockSpec((H, REL_BUCKETS), lambda b: (0, 0)),
        out_shape=jax.ShapeDtypeStruct((H, REL_BUCKETS), F32),
        compiler_params=_params(1), name="rel_grad",
    )(dbias, bucket)


def _regroup(src, stage, dst, d, S, off=0):
    if d == 1:
        dst[off:off + S, :] = src.astype(dst.dtype)
        return
    stage[...] = src.astype(F32)
    L = S // d
    for r in range(d):
        dst[off + r * L:off + (r + 1) * L, :] = stage[pl.ds(r, L, stride=d), :].astype(dst.dtype)


def _ungroup(sub_ref, off, nat_ref, d, S, add):
    L = S // d
    for r in range(d):
        rows = pl.ds(0, S) if d == 1 else pl.ds(r, L, stride=d)
        val = sub_ref[off + r * L:off + (r + 1) * L, :]
        if add:
            nat_ref[rows, :] += val
        else:
            nat_ref[rows, :] = val


def _branch_keys(ks, vs, S, nb, g_idx):
    blk3 = (S // ATTN_BLOCK, ATTN_BLOCK, LANES)
    kc3 = ks[ATTN_BLOCK:ATTN_BLOCK + S, :].reshape(blk3)
    vc3 = vs[ATTN_BLOCK:ATTN_BLOCK + S, :].reshape(blk3)
    if nb == 1:
        return kc3, vc3, None
    kk3 = jnp.concatenate([ks[0:S, :].reshape(blk3), kc3], axis=1)
    vv3 = jnp.concatenate([vs[0:S, :].reshape(blk3), vc3], axis=1)
    col = lax.broadcasted_iota(jnp.int32, (1, 1, 2 * ATTN_BLOCK), 2)
    dead = jnp.logical_and((g_idx & (nb - 1)) == 0, col < ATTN_BLOCK)
    return kk3, vv3, dead


def _branch_scores(qe, kk3, b_ref, bi, e, dead):
    s = jnp.einsum("gqe,gke->gqk", qe, kk3, preferred_element_type=F32)
    if dead is None:
        return s + b_ref[bi, e, :, ATTN_BLOCK:]
    return jnp.where(dead, NEG_INF, s + b_ref[bi, e])


def _attention_fwd(qkv, bias_all, B, S, AW, bg=None):
    HP = AW // LANES
    G = S // ATTN_BLOCK
    blk3 = (G, ATTN_BLOCK, LANES)

    def body(refs, bg_hook):
        q_ref, k_ref, v_ref, b_ref, o_ref, lse_ref, stage, qs, ks, vs, ot, lt, on0, on1, on2, ln0, ln1, ln2 = refs
        bg_hook(False)
        head0 = lax.broadcasted_iota(jnp.int32, (1, 1, LANES), 2) < HEAD_DIM
        g_idx = lax.broadcasted_iota(jnp.int32, (G, 1, 1), 0)
        ks[0:ATTN_BLOCK, :] = jnp.zeros((ATTN_BLOCK, LANES), BF16)
        vs[0:ATTN_BLOCK, :] = jnp.zeros((ATTN_BLOCK, LANES), BF16)
        nat_o, nat_l = (on0, on1, on2), (ln0, ln1, ln2)
        for bi, (_, d) in enumerate(DILATED_CONFIGS):
            nb = S // d // ATTN_BLOCK
            _regroup(q_ref[0], stage, qs, d, S)
            _regroup(k_ref[0], stage, ks, d, S, ATTN_BLOCK)
            _regroup(v_ref[0], stage, vs, d, S, ATTN_BLOCK)
            q3 = qs[...].reshape(blk3) * QK_SCALE
            kk3, vv3, dead = _branch_keys(ks, vs, S, nb, g_idx)
            outs, lses = [], []
            for e in range(2):
                msk = head0 if e == 0 else jnp.logical_not(head0)
                qe = jnp.where(msk, q3, jnp.zeros_like(q3))
                s = _branch_scores(qe, kk3, b_ref, bi, e, dead)
                m = jnp.max(s, axis=-1, keepdims=True)
                p = jnp.exp(s - m)
                l = jnp.sum(p, axis=-1, keepdims=True)
                o = jnp.einsum("gqk,gke->gqe", p.astype(BF16), vv3, preferred_element_type=F32)
                outs.append(o / l)
                lses.append(jnp.broadcast_to(m + jnp.log(l), blk3))
            ot[...] = jnp.where(head0, outs[0], outs[1]).reshape(S, LANES)
            lt[...] = jnp.where(head0, lses[0], lses[1]).reshape(S, LANES)
            _ungroup(ot, 0, nat_o[bi], d, S, add=False)
            _ungroup(lt, 0, nat_l[bi], d, S, add=False)

        la, lb, lc = ln0[...], ln1[...], ln2[...]
        m = jnp.maximum(jnp.maximum(la, lb), lc)
        ea, eb, ec = jnp.exp(la - m), jnp.exp(lb - m), jnp.exp(lc - m)
        den = ea + eb + ec
        lse_ref[0] = m + jnp.log(den)
        o_ref[0] = (ea * on0[...] + eb * on1[...] + ec * on2[...]) / den
        bg_hook(True)

    blk = lambda off: pl.BlockSpec((1, S, LANES), lambda b, h: (b, 0, off + h))
    qv = qkv.reshape(B, S, 3 * AW)
    sub_f = pltpu.VMEM((S, LANES), F32)
    pad_b = pltpu.VMEM((S + ATTN_BLOCK, LANES), BF16)
    res = _hosted_call(
        body, bg, grid=(B, HP),
        in_specs=[blk(0), blk(HP), blk(2 * HP),
                  pl.BlockSpec((3, 2, ATTN_BLOCK, 2 * ATTN_BLOCK), lambda b, h: (0, h, 0, 0))],
        out_specs=[blk(0), blk(0)],
        out_shape=[jax.ShapeDtypeStruct((B, S, AW), F32)] * 2,
        scratch_shapes=[sub_f, pltpu.VMEM((S, LANES), BF16), pad_b, pad_b] + [sub_f] * 8,
        operands=[qv, qv, qv, bias_all], name="attention_fwd")
    return (res[0].reshape(B * S, AW), res[1].reshape(B * S, AW)) + tuple(res[2:])


def _attention_bwd(qkv, do, lse, dd, bias_all, B, S, AW, bg=None):
    HP = AW // LANES
    H = AW // HEAD_DIM
    G = S // ATTN_BLOCK
    blk3 = (G, ATTN_BLOCK, LANES)
    PAD = ATTN_BLOCK

    def body(refs, bg_hook):
        (q_ref, k_ref, v_ref, do_ref, lse_ref, dd_ref, b_ref,
         dq_ref, dk_ref, dv_ref, csq_ref, csk_ref, csv_ref, db_ref,
         stage, qs, ks, vs, gs, ls, ds_, tq, tk, tv, accq, acck, accv) = refs
        bg_hook(False)
        head0 = lax.broadcasted_iota(jnp.int32, (1, 1, LANES), 2) < HEAD_DIM
        g_idx = lax.broadcasted_iota(jnp.int32, (G, 1, 1), 0)
        first_b = pl.program_id(1) == 0

        @pl.when(first_b)
        def _():
            db_ref[...] = jnp.zeros_like(db_ref)

        ks[0:PAD, :] = jnp.zeros((PAD, LANES), BF16)
        vs[0:PAD, :] = jnp.zeros((PAD, LANES), BF16)
        tk[0:PAD, :] = jnp.zeros((PAD, LANES), F32)
        tv[0:PAD, :] = jnp.zeros((PAD, LANES), F32)
        for bi, (_, d) in enumerate(DILATED_CONFIGS):
            nb = S // d // ATTN_BLOCK
            _regroup(q_ref[0], stage, qs, d, S)
            _regroup(k_ref[0], stage, ks, d, S, PAD)
            _regroup(v_ref[0], stage, vs, d, S, PAD)
            _regroup(do_ref[0], stage, gs, d, S)
            _regroup(lse_ref[0], stage, ls, d, S)
            _regroup(dd_ref[0], stage, ds_, d, S)
            q3 = qs[...].reshape(blk3) * QK_SCALE
            do3 = gs[...].reshape(blk3)
            lse3 = ls[...].reshape(blk3)
            dd3 = ds_[...].reshape(blk3)
            kk3, vv3, dead = _branch_keys(ks, vs, S, nb, g_idx)
            dq = jnp.zeros(blk3, F32)
            dkk = jnp.zeros(kk3.shape, F32)
            dvv = jnp.zeros(kk3.shape, F32)
            for e in range(2):
                msk = head0 if e == 0 else jnp.logical_not(head0)
                c0 = e * HEAD_DIM
                qe = jnp.where(msk, q3, jnp.zeros_like(q3))
                doe = jnp.where(msk, do3, jnp.zeros_like(do3))
                ke = jnp.where(msk, kk3 * QK_SCALE, jnp.zeros_like(kk3))
                s = _branch_scores(qe, kk3, b_ref, bi, e, dead)
                p = jnp.exp(s - lse3[:, :, c0:c0 + 1])
                dp = jnp.einsum("gqe,gke->gqk", doe, vv3, preferred_element_type=F32)
                dsc = p * (dp - dd3[:, :, c0:c0 + 1])
                if dead is None:
                    db_ref[bi, e, :, ATTN_BLOCK:] += jnp.sum(dsc, axis=0)
                else:
                    db_ref[bi, e] += jnp.sum(dsc, axis=0)
                dsb = dsc.astype(BF16)
                dq = dq + jnp.einsum("gqk,gke->gqe", dsb, ke, preferred_element_type=F32)
                dkk = dkk + jnp.einsum("gqk,gqe->gke", dsb, qe, preferred_element_type=F32)
                dvv = dvv + jnp.einsum("gqk,gqe->gke", p.astype(BF16), doe, preferred_element_type=F32)
            tq[...] = dq.reshape(S, LANES)
            if dead is None:
                tk[PAD:PAD + S, :] = dkk.reshape(S, LANES)
                tv[PAD:PAD + S, :] = dvv.reshape(S, LANES)
            else:
                tk[PAD:PAD + S, :] = dkk[:, ATTN_BLOCK:, :].reshape(S, LANES)
                tv[PAD:PAD + S, :] = dvv[:, ATTN_BLOCK:, :].reshape(S, LANES)
                tk[0:S, :] += dkk[:, :ATTN_BLOCK, :].reshape(S, LANES)
                tv[0:S, :] += dvv[:, :ATTN_BLOCK, :].reshape(S, LANES)
            _ungroup(tq, 0, accq, d, S, add=bi > 0)
            _ungroup(tk, PAD, acck, d, S, add=bi > 0)
            _ungroup(tv, PAD, accv, d, S, add=bi > 0)

        for acc, out_ref, cs_ref in ((accq, dq_ref, csq_ref), (acck, dk_ref, csk_ref), (accv, dv_ref, csv_ref)):
            tot = acc[...]
            out_ref[0] = tot.astype(out_ref.dtype)
            _accumulate(cs_ref, first_b, jnp.sum(tot, axis=0, keepdims=True))
        bg_hook(True)

    blk = lambda off: pl.BlockSpec((1, S, LANES), lambda h, b: (b, 0, off + h))
    cs_spec = pl.BlockSpec((1, LANES), lambda h, b: (0, h))
    bias_spec = pl.BlockSpec((3, 2, ATTN_BLOCK, 2 * ATTN_BLOCK), lambda h, b: (0, h, 0, 0))
    qv = qkv.reshape(B, S, 3 * AW)
    view = lambda t: t.reshape(B, S, AW)
    sub_b = pltpu.VMEM((S, LANES), BF16)
    sub_f = pltpu.VMEM((S, LANES), F32)
    pad_b = pltpu.VMEM((S + PAD, LANES), BF16)
    pad_f = pltpu.VMEM((S + PAD, LANES), F32)
    res = _hosted_call(
        body, bg, grid=(HP, B),
        in_specs=[blk(0), blk(HP), blk(2 * HP), blk(0), blk(0), blk(0), bias_spec],
        out_specs=[blk(0), blk(0), blk(0), cs_spec, cs_spec, cs_spec, bias_spec],
        out_shape=[jax.ShapeDtypeStruct((B, S, AW), BF16)] * 3 + [jax.ShapeDtypeStruct((1, AW), F32)] * 3
        + [jax.ShapeDtypeStruct((3, H, ATTN_BLOCK, 2 * ATTN_BLOCK), F32)],
        scratch_shapes=[sub_f, sub_b, pad_b, pad_b, sub_b, sub_f, sub_f, sub_f, pad_f, pad_f, sub_f, sub_f, sub_f],
        operands=[qv, qv, qv, view(do), view(lse), view(dd), bias_all], name="attention_bwd")
    flat = lambda t: t.reshape(B * S, AW)
    return (flat(res[0]), flat(res[1]), flat(res[2]), res[3], res[4], res[5], res[6]) + tuple(res[7:])


class _RowShifts:
    def __init__(self, x, row, up):
        self.x, self.row, self.up, self.base = x, row, up, {0: x}

    def __call__(self, s):
        x = self.x
        n, c = x.shape
        r, whole = s % 8, s - s % 8
        if r not in self.base:
            if self.up:
                rolled = pltpu.roll(x, n - r, 0)
                tail = jnp.where(self.row[n - 8:] < n - r, rolled[n - 8:], 0.0)
                self.base[r] = jnp.concatenate([rolled[:n - 8], tail], axis=0)
            else:
                rolled = pltpu.roll(x, r, 0)
                head = jnp.where(self.row[:8] >= r, rolled[:8], 0.0)
                self.base[r] = jnp.concatenate([head, rolled[8:]], axis=0)
        y = self.base[r]
        if whole == 0:
            return y
        pad = jnp.zeros((whole, c), x.dtype)
        if self.up:
            return jnp.concatenate([y[whole:], pad], axis=0)
        return jnp.concatenate([pad, y[:n - whole]], axis=0)


def _conv_branch_fwd_math(a, g, w_ref, cb, lg, lb, row):
    sg = _sigmoid(g)
    u0 = a * sg
    u0_down = _RowShifts(u0, row, up=False)
    uc = jnp.zeros_like(u0) + cb
    for k in range(CONV_KERNEL):
        uc = uc + w_ref[k:k + 1, :] * u0_down(CONV_KERNEL - 1 - k)
    ul, xh, r = _ln_fwd(uc, lg, lb)
    su = _sigmoid(ul)
    u = ul * su
    return sg, u0_down, ul, xh, r, su, u


def _conv_fwd(ag, conv_w, conv_b, ln_g, ln_b, norm_g, B, S, CW):
    def body(a_ref, g_ref, w_ref, cb_ref, lg_ref, lb_ref, ng_ref, o_ref):
        row = lax.broadcasted_iota(jnp.int32, (S, CW), 0)
        _, _, _, _, _, _, u = _conv_branch_fwd_math(a_ref[0], g_ref[0], w_ref, cb_ref[...], lg_ref[...],
                                                    lb_ref[...], row)
        rr = lax.rsqrt(jnp.mean(u * u, axis=-1, keepdims=True) + LN_EPS)
        o_ref[0] = (u * rr * ng_ref[...]).astype(BF16)

    vec = pl.BlockSpec((1, CW), lambda b: (0, 0))
    out = pl.pallas_call(
        body, grid=(B,),
        in_specs=[pl.BlockSpec((1, S, CW), lambda b: (b, 0, 0)), pl.BlockSpec((1, S, CW), lambda b: (b, 0, 1)),
                  pl.BlockSpec((CONV_KERNEL, CW), lambda b: (0, 0)), vec, vec, vec, vec],
        out_specs=pl.BlockSpec((1, S, CW), lambda b: (b, 0, 0)),
        out_shape=jax.ShapeDtypeStruct((B, S, CW), BF16),
        compiler_params=_params(1), name="conv_fwd",
    )(ag.reshape(B, S, 2 * CW), ag.reshape(B, S, 2 * CW), conv_w, conv_b, ln_g, ln_b, norm_g)
    return out.reshape(B * S, CW)


def _conv_bwd(ag, dmc, conv_w, conv_b, ln_g, ln_b, norm_g, B, S, CW):
    def body(a_ref, g_ref, dm_ref, w_ref, cb_ref, lg_ref, lb_ref, ng_ref,
             dag_ref, dw_ref, dcb_ref, dlg_ref, dlb_ref, dng_ref):
        b = pl.program_id(0)
        row = lax.broadcasted_iota(jnp.int32, (S, CW), 0)
        a, g = a_ref[0], g_ref[0]
        sg, u0_down, ul, xh, r, su, u = _conv_branch_fwd_math(a, g, w_ref, cb_ref[...], lg_ref[...], lb_ref[...], row)
        rr = lax.rsqrt(jnp.mean(u * u, axis=-1, keepdims=True) + LN_EPS)
        dm = dm_ref[0]
        dxn = dm * ng_ref[...]
        du = rr * (dxn - u * (rr * rr) * jnp.mean(dxn * u, axis=-1, keepdims=True))
        dul = du * su * (1.0 + ul * (1.0 - su))
        duc = _ln_bwd(dul, xh, r, lg_ref[...])
        first = b == 0
        _accumulate(dng_ref, first, jnp.sum(dm * u * rr, axis=0, keepdims=True))
        _accumulate(dlg_ref, first, jnp.sum(dul * xh, axis=0, keepdims=True))
        _accumulate(dlb_ref, first, jnp.sum(dul, axis=0, keepdims=True))
        _accumulate(dcb_ref, first, jnp.sum(duc, axis=0, keepdims=True))

        @pl.when(first)
        def _():
            dw_ref[...] = jnp.zeros_like(dw_ref)

        duc_up = _RowShifts(duc, row, up=True)
        du0 = jnp.zeros_like(duc)
        for k in range(CONV_KERNEL):
            sh = CONV_KERNEL - 1 - k
            dw_ref[k:k + 1, :] += jnp.sum(duc * u0_down(sh), axis=0, keepdims=True)
            du0 = du0 + w_ref[k:k + 1, :] * duc_up(sh)
        dag_ref[0, :, :CW] = du0 * sg
        dag_ref[0, :, CW:] = du0 * a * sg * (1.0 - sg)

    vec = pl.BlockSpec((1, CW), lambda b: (0, 0))
    wspec = pl.BlockSpec((CONV_KERNEL, CW), lambda b: (0, 0))
    agv = ag.reshape(B, S, 2 * CW)
    res = pl.pallas_call(
        body, grid=(B,),
        in_specs=[pl.BlockSpec((1, S, CW), lambda b: (b, 0, 0)), pl.BlockSpec((1, S, CW), lambda b: (b, 0, 1)),
                  pl.BlockSpec((1, S, CW), lambda b: (b, 0, 0)), wspec, vec, vec, vec, vec],
        out_specs=[pl.BlockSpec((1, S, 2 * CW), lambda b: (b, 0, 0)), wspec, vec, vec, vec, vec],
        out_shape=[jax.ShapeDtypeStruct((B, S, 2 * CW), F32), jax.ShapeDtypeStruct((CONV_KERNEL, CW), F32)]
        + [jax.ShapeDtypeStruct((1, CW), F32)] * 4,
        compiler_params=_params(1), name="conv_bwd",
    )(agv, agv, dmc.reshape(B, S, CW), conv_w, conv_b, ln_g, ln_b, norm_g)
    return (res[0].reshape(B * S, 2 * CW),) + tuple(res[1:])


def _ffn_conv(x, w_ref, bias, row):
    down = x if isinstance(x, _RowShifts) else _RowShifts(x, row, up=False)
    y = jnp.zeros_like(down.x) + bias
    for k in range(FFN_CONV_KERNEL):
        y = y + w_ref[k:k + 1, :] * down(FFN_CONV_KERNEL - 1 - k)
    return y


def _ffn_specs(S, tc, nj, order):
    pick = (lambda b, j: (b, j)) if order == "bj" else (lambda j, b: (b, j))
    act = lambda off: pl.BlockSpec((1, S, tc), lambda *g: (pick(*g)[0], 0, off + pick(*g)[1]))
    cw = lambda off: pl.BlockSpec((FFN_CONV_KERNEL, tc), lambda *g: (0, off + pick(*g)[1]))
    cb = lambda off: pl.BlockSpec((1, tc), lambda *g: (0, off + pick(*g)[1]))
    return act, cw, cb


FFN_HALO = 16


def _half_sequences(S):
    if S < 8 * FFN_HALO:
        return [(0, S, 0, S)]
    h = S // 2
    return [(0, h + FFN_HALO, 0, h), (h - FFN_HALO, S, FFN_HALO, h)]


def _w_up_block_spec(w_up_sh, tc, off):
    _, D, cs = w_up_sh.shape
    assert cs % tc == 0
    bps = cs // tc
    return pl.BlockSpec((1, D, tc), lambda j: ((off + j) // bps, 0, (off + j) % bps))


def _ffn_fwd_fused(x1b, w_up_sh, cw, cb, B, S, DFF):
    tc = FFN_COLS
    nj = DFF // tc
    D = x1b.shape[1]

    def body(x_ref, wg_ref, wv_ref, cwg_ref, cwv_ref, cbg_ref, cbv_ref, o_ref, up_ref):
        w = jnp.concatenate([wg_ref[0], wv_ref[0]], axis=1)
        for b in range(B):
            for lo, hi, o0, on in _half_sequences(S):
                row = lax.broadcasted_iota(jnp.int32, (hi - lo, tc), 0)
                up = jnp.dot(x_ref[b, lo:hi, :], w, preferred_element_type=F32)
                up_ref[b, lo + o0:lo + o0 + on, :] = up[o0:o0 + on]
                gate = _ffn_conv(up[:, :tc], cwg_ref, cbg_ref[...], row)
                val = _ffn_conv(up[:, tc:], cwv_ref, cbv_ref[...], row)
                o_ref[b, lo + o0:lo + o0 + on, :] = (gate * _sigmoid(gate) * val).astype(BF16)[o0:o0 + on]

    cws = lambda off: pl.BlockSpec((FFN_CONV_KERNEL, tc), lambda j: (0, off + j))
    cbs = lambda off: pl.BlockSpec((1, tc), lambda j: (0, off + j))
    act, upre = pl.pallas_call(
        body, grid=(nj,),
        in_specs=[pl.BlockSpec((B, S, D), lambda j: (0, 0, 0), pipeline_mode=pl.Buffered(1)),
                  _w_up_block_spec(w_up_sh, tc, 0), _w_up_block_spec(w_up_sh, tc, nj),
                  cws(0), cws(nj), cbs(0), cbs(nj)],
        out_specs=[pl.BlockSpec((B, S, tc), lambda j: (0, 0, j)), pl.BlockSpec((B, S, 2 * tc), lambda j: (0, 0, j))],
        out_shape=[jax.ShapeDtypeStruct((B, S, DFF), BF16), jax.ShapeDtypeStruct((B, S, 2 * DFF), F32)],
        compiler_params=_params(1), name="ffn_fwd",
    )(x1b.reshape(B, S, D), w_up_sh, w_up_sh, cw, cw, cb, cb)
    return act.reshape(B * S, DFF), upre


def _ffn_bwd_fused(x1b, dz2b, upre, w_down, cw, cb, B, S, DFF):
    tc = FFN_COLS
    nj = DFF // tc
    D = x1b.shape[1]

    def body(x_ref, dz_ref, up_ref, wd_ref, cwg_ref, cwv_ref, cbg_ref, cbv_ref,
             dug_ref, duv_ref, dwu_ref, dwd_ref, dcw_ref, dcb_ref):
        first = pl.program_id(1) == 0
        dw_t = dwd = None
        dcb = [None, None]
        dcw = [[None] * FFN_CONV_KERNEL, [None] * FFN_CONV_KERNEL]
        add = lambda old, new: new if old is None else old + new
        for lo, hi, o0, on in _half_sequences(S):
            n = hi - lo
            own = slice(o0, o0 + on)
            row = lax.broadcasted_iota(jnp.int32, (n, tc), 0)
            x = x_ref[0, lo:hi, :]
            dz = dz_ref[0, lo:hi, :]
            ug = _RowShifts(up_ref[0, lo:hi, :tc], row, up=False)
            uv = _RowShifts(up_ref[0, lo:hi, tc:], row, up=False)
            gate = _ffn_conv(ug, cwg_ref, cbg_ref[...], row)
            val = _ffn_conv(uv, cwv_ref, cbv_ref[...], row)
            sg = _sigmoid(gate)
            act = (gate * sg * val).astype(BF16)
            dact = _dot(dz, wd_ref[...], "nt")
            dgate = dact * val * sg * (1.0 + gate * (1.0 - sg))
            dval = dact * gate * sg
            dupre = []
            for h, (dup, u_down, w_ref) in enumerate(((dgate, ug, cwg_ref), (dval, uv, cwv_ref))):
                dcb[h] = add(dcb[h], jnp.sum(dup[own], axis=0, keepdims=True))
                dup_up = _RowShifts(dup, row, up=True)
                acc = jnp.zeros_like(dup)
                for k in range(FFN_CONV_KERNEL):
                    sh = FFN_CONV_KERNEL - 1 - k
                    dcw[h][k] = add(dcw[h][k], jnp.sum((dup * u_down(sh))[own], axis=0, keepdims=True))
                    acc = acc + w_ref[k:k + 1, :] * dup_up(sh)
                dupre.append(acc.astype(BF16)[own])
            dug_ref[0, lo + o0:lo + o0 + on, :] = dupre[0]
            duv_ref[0, lo + o0:lo + o0 + on, :] = dupre[1]
            dw_t = add(dw_t, _dot(jnp.concatenate(dupre, axis=1), x[own], "tn"))
            dwd = add(dwd, _dot(act[own], dz[own], "tn"))
        _accumulate(dwu_ref.at[0], first, dw_t[:tc])
        _accumulate(dwu_ref.at[1], first, dw_t[tc:])
        _accumulate(dwd_ref, first, dwd)
        for h in range(2):
            _accumulate(dcb_ref.at[h], first, dcb[h])
            for k in range(FFN_CONV_KERNEL):
                _accumulate(dcw_ref.at[k, pl.ds(h, 1), :], first, dcw[h][k])

    act_s, cws, cbs = _ffn_specs(S, tc, nj, "jb")
    seq = pl.BlockSpec((1, S, D), lambda j, b: (b, 0, 0))
    res = pl.pallas_call(
        body, grid=(nj, B),
        in_specs=[seq, seq, pl.BlockSpec((1, S, 2 * tc), lambda j, b: (b, 0, j)),
                  pl.BlockSpec((tc, D), lambda j, b: (j, 0)), cws(0), cws(nj), cbs(0), cbs(nj)],
        out_specs=[act_s(0), act_s(0), pl.BlockSpec((2, tc, D), lambda j, b: (0, j, 0)),
                   pl.BlockSpec((tc, D), lambda j, b: (j, 0)),
                   pl.BlockSpec((FFN_CONV_KERNEL, 2, tc), lambda j, b: (0, 0, j)),
                   pl.BlockSpec((2, 1, tc), lambda j, b: (0, 0, j))],
        out_shape=[jax.ShapeDtypeStruct((B, S, DFF), BF16)] * 2
        + [jax.ShapeDtypeStruct((2, DFF, D), F32), jax.ShapeDtypeStruct((DFF, D), F32),
           jax.ShapeDtypeStruct((FFN_CONV_KERNEL, 2, DFF), F32), jax.ShapeDtypeStruct((2, 1, DFF), F32)],
        compiler_params=_params(2), name="ffn_bwd",
    )(x1b.reshape(B, S, D), dz2b.reshape(B, S, D), upre, w_down, cw, cw, cb, cb)
    flat = lambda t: t.reshape(B * S, DFF)
    return flat(res[0]), flat(res[1]), res[2], res[3], res[4], res[5]


def _dx1_ln1_bwd(dupre_g, dupre_v, w_up_sh, dz2, xh1, r1, ln1_g, tm, bg):
    T, D = dz2.shape
    NS, _, cs = w_up_sh.shape
    half = NS // 2
    DFF = dupre_g.shape[1]

    def body(refs, bg_hook):
        dug_ref, duv_ref, w_ref, dz2_ref, xh_ref, r_ref, g_ref, dz_ref, dzb_ref, dg_ref, db_ref = refs
        bg_hook(False)
        first = pl.program_id(0) == 0
        dg = db = None
        for rows in (slice(0, tm // 2), slice(tm // 2, tm)):
            dx1 = ALPHA * dz2_ref[rows, :]
            for k in range(NS):
                src = dug_ref if k < half else duv_ref
                c0 = (k % half) * cs
                dx1 = dx1 + _dot(src[rows, c0:c0 + cs], w_ref[k], "nt")
            xh = xh_ref[rows, :]
            dz = _ln_bwd(dx1, xh, r_ref[rows, 0:1], g_ref[...])
            dz_ref[rows, :] = dz
            dzb_ref[rows, :] = dz.astype(BF16)
            dg_h, db_h = jnp.sum(dx1 * xh, axis=0, keepdims=True), jnp.sum(dx1, axis=0, keepdims=True)
            dg, db = (dg_h, db_h) if dg is None else (dg + dg_h, db + db_h)
        _accumulate(dg_ref, first, dg)
        _accumulate(db_ref, first, db)
        bg_hook(True)

    row = pl.BlockSpec((tm, D), lambda i: (i, 0))
    vec = pl.BlockSpec((1, D), lambda i: (0, 0))
    du = pl.BlockSpec((tm, DFF), lambda i: (i, 0))
    return _hosted_call(
        body, bg, grid=(T // tm,),
        in_specs=[du, du, pl.BlockSpec((NS, D, cs), lambda i: (0, 0, 0), pipeline_mode=pl.Buffered(1)),
                  row, row, pl.BlockSpec((tm, LANES), lambda i: (i, 0)), vec],
        out_specs=[row, row, vec, vec],
        out_shape=[jax.ShapeDtypeStruct((T, D), F32), jax.ShapeDtypeStruct((T, D), BF16),
                   jax.ShapeDtypeStruct((1, D), F32), jax.ShapeDtypeStruct((1, D), F32)],
        scratch_shapes=[], operands=[dupre_g, dupre_v, w_up_sh, dz2, xh1, r1, ln1_g], name="mm_dx1_ln1_bwd")


def _dh_cat(dq, dk, dv, dag, tm):
    T, AW = dq.shape
    CW2 = dag.shape[1]
    W = 3 * AW + CW2

    def body(dq_ref, dk_ref, dv_ref, dag_ref, dh_ref, cs_ref):
        for c, ref in enumerate((dq_ref, dk_ref, dv_ref)):
            dh_ref[:, c * AW:(c + 1) * AW] = ref[...]
        dg = dag_ref[...]
        dh_ref[:, 3 * AW:] = dg.astype(BF16)
        _accumulate(cs_ref, pl.program_id(0) == 0, jnp.sum(dg, axis=0, keepdims=True))

    row = pl.BlockSpec((tm, AW), lambda i: (i, 0))
    return pl.pallas_call(
        body, grid=(T // tm,),
        in_specs=[row] * 3 + [pl.BlockSpec((tm, CW2), lambda i: (i, 0))],
        out_specs=[pl.BlockSpec((tm, W), lambda i: (i, 0)), pl.BlockSpec((1, CW2), lambda i: (0, 0))],
        out_shape=[jax.ShapeDtypeStruct((T, W), BF16), jax.ShapeDtypeStruct((1, CW2), F32)],
        compiler_params=_params(1), name="dh_cat",
    )(dq, dk, dv, dag)


def _local_step(x, target, rel_table, w_in_t, b_in, conv_w, conv_b, conv_ln_g, conv_ln_b, attn_norm_g,
                conv_norm_g, staged, ln1_g, ln1_b, ffn_cw, ffn_cb, ln2_g, ln2_b, ids):
    B, S, D = x.shape
    T = B * S
    AW = attn_norm_g.shape[-1]
    CW = conv_norm_g.shape[-1]
    H = AW // HEAD_DIM
    DFF = staged[2].shape[0] * staged[2].shape[1]
    INW = 3 * AW + 2 * CW
    xf = x.reshape(T, D)
    tf = target.reshape(T, D)
    tm = _row_tile(T, 512)
    tm_s = tm

    bucket_np, mask_np = _bucket_tables()
    bucket = jnp.asarray(bucket_np)
    band_mask = jnp.asarray(mask_np)
    bias_all = _bias_build(rel_table.T, bucket, band_mask).reshape(3, H, ATTN_BLOCK, 2 * ATTN_BLOCK)

    rowD = lambda i, j, k: (i, 0)
    vecD = lambda i, j, k: (0, 0)

    def in_proj_epilogue(acc, i, j, extra_refs, out_refs):
        h = acc + extra_refs[0][...]
        out_refs[0][...] = h[:, :3 * AW].astype(BF16)
        out_refs[1][...] = h[:, 3 * AW:]

    qkv, ag = _matmul_general(
        [(xf, (tm, D), rowD), (w_in_t, (INW, D), vecD)],
        lambda refs, i, j, k: _dot(refs[0][...], refs[1][...], "nt"),
        grid=(T // tm, 1, 1), tm=tm, tn=INW, extras=[(b_in, (1, INW), vecD)],
        outs=[((T, 3 * AW), BF16, (tm, 3 * AW), rowD), ((T, 2 * CW), F32, (tm, 2 * CW), rowD)],
        epilogue=in_proj_epilogue, name="mm_in")

    attn, lse, w_out_g, w_up_sh, w_down_g = _attention_fwd(qkv, bias_all, B, S, AW, bg=_bg_gather(staged))
    w_out = w_out_g.reshape(D, D)
    w_down = w_down_g.reshape(DFF, D)
    mixed_c = _conv_fwd(ag, conv_w, conv_b, conv_ln_g, conv_ln_b, conv_norm_g, B, S, CW)

    def attn_rstd(a):
        return lax.rsqrt(jnp.mean(a * a, axis=-1, keepdims=True) + LN_EPS)

    def mixed_rows(attn_ref, mc_ref, gain_ref, rows=slice(None)):
        a = attn_ref[rows, :]
        return jnp.concatenate([(a * attn_rstd(a) * gain_ref[...]).astype(BF16), mc_ref[rows, :]], axis=1)

    halves = [slice(0, tm // 2), slice(tm // 2, tm)]

    def ln1_epilogue(parts, i, j, extra_refs, out_refs):
        x_ref, g_ref, b_ref, a_ref = extra_refs
        for rows, acc in zip(halves, parts):
            x1, xh, r = _ln_fwd(acc + ALPHA * x_ref[rows, :], g_ref[...], b_ref[...])
            out_refs[0][rows, :] = x1
            out_refs[1][rows, :] = x1.astype(BF16)
            out_refs[2][rows, :] = xh
            out_refs[3][rows, :] = jnp.broadcast_to(r, (tm // 2, LANES))
            out_refs[4][rows, :] = jnp.broadcast_to(attn_rstd(a_ref[rows, :]), (tm // 2, LANES))

    x1, x1b, xh1, r1, r_attn = _matmul_general(
        [(attn, (tm_s, AW), rowD), (mixed_c, (tm_s, CW), rowD), (attn_norm_g, (1, AW), vecD), (w_out, (D, D), vecD)],
        lambda refs, i, j, k: tuple(_dot(mixed_rows(refs[0], refs[1], refs[2], rows), refs[3][...], "nn")
                                    for rows in halves),
        grid=(T // tm_s, 1, 1), tm=tm_s, tn=D,
        extras=[(xf, (tm_s, D), rowD), (ln1_g, (1, D), vecD), (ln1_b, (1, D), vecD), (attn, (tm_s, AW), rowD)],
        outs=[((T, D), F32, (tm_s, D), rowD), ((T, D), BF16, (tm_s, D), rowD), ((T, D), F32, (tm_s, D), rowD),
              ((T, LANES), F32, (tm_s, LANES), rowD), ((T, LANES), F32, (tm_s, LANES), rowD)],
        epilogue=ln1_epilogue, name="mm_out_ln1")

    NS, _, cs = w_up_sh.shape
    half = NS // 2

    act, upre = _ffn_fwd_fused(x1b, w_up_sh, ffn_cw, ffn_cb, B, S, DFF)

    def ln2_epilogue(parts, i, j, extra_refs, out_refs):
        x1_ref, g_ref, b_ref, t_ref = extra_refs
        dz_ref, dzb_ref, loss_ref, dg_ref, db_ref = out_refs
        g = g_ref[...]
        sums = None
        for rows, acc in zip(halves, parts):
            y, xh, r = _ln_fwd(acc + ALPHA * x1_ref[rows, :], g, b_ref[...])
            diff = y - t_ref[rows, :]
            row_loss = jnp.sum(diff * diff, axis=1, keepdims=True)
            tile_loss = jnp.sum(row_loss, axis=0, keepdims=True) * (0.5 / D)
            dy = diff * (1.0 / D)
            dz = _ln_bwd(dy, xh, r, g)
            dz_ref[rows, :] = dz
            dzb_ref[rows, :] = dz.astype(BF16)
            vals = (jnp.broadcast_to(tile_loss, (1, LANES)), jnp.sum(dy * xh, axis=0, keepdims=True),
                    jnp.sum(dy, axis=0, keepdims=True))
            sums = vals if sums is None else tuple(a + b for a, b in zip(sums, vals))
        for ref, val in zip((loss_ref, dg_ref, db_ref), sums):
            _accumulate(ref, i == 0, val)

    dz2, dz2b, loss_part, d_ln2_g, d_ln2_b = _matmul_general(
        [(act, (tm, DFF), rowD), (w_down, (DFF, D), vecD)],
        lambda refs, i, j, k: tuple(_dot(refs[0][rows, :], refs[1][...], "nn") for rows in halves),
        grid=(T // tm, 1, 1), tm=tm, tn=D,
        extras=[(x1, (tm, D), rowD), (ln2_g, (1, D), vecD), (ln2_b, (1, D), vecD), (tf, (tm, D), rowD)],
        outs=[((T, D), F32, (tm, D), rowD), ((T, D), BF16, (tm, D), rowD),
              ((1, LANES), F32, (1, LANES), vecD), ((1, D), F32, (1, D), vecD), ((1, D), F32, (1, D), vecD)],
        epilogue=ln2_epilogue, name="mm_down_ln2_loss")

    dupre_g, dupre_v, d_w_up_t, d_w_down, d_ffn_cw2, d_ffn_cb2 = _ffn_bwd_fused(
        x1b, dz2b, upre, w_down, ffn_cw, ffn_cb, B, S, DFF)
    d_w_up_t = d_w_up_t.reshape(NS, cs, D)
    d_ffn_cw = d_ffn_cw2.reshape(FFN_CONV_KERNEL, 2 * DFF)
    d_ffn_cb = d_ffn_cb2.reshape(1, 2 * DFF)
    tk_t = _row_tile(T, 1024)

    early = [d_w_up_t, d_w_down.reshape(NS, DFF // NS, D)]
    dz1, dz1b, d_ln1_g, d_ln1_b, *sib_e = _dx1_ln1_bwd(dupre_g, dupre_v, w_up_sh, dz2, xh1, r1, ln1_g, tm,
                                                       bg=_bg_sibling_exchange(early))
    chip_e = [_pair_sum(g, s, ids, name="pair_sum_" + n) for g, s, n in zip(early, sib_e, ("w_up", "w_down"))]

    def dw_out_epilogue(acc, i, j, extra_refs, out_refs):
        out_refs[0][...] = acc

    d_w_out = _matmul_general(
        [(attn, (tk_t, AW), lambda i, j, k: (k, 0)), (mixed_c, (tk_t, CW), lambda i, j, k: (k, 0)),
         (attn_norm_g, (1, AW), vecD), (dz1b, (tk_t, D), lambda i, j, k: (k, 0))],
        lambda refs, i, j, k: _dot(mixed_rows(refs[0], refs[1], refs[2]), refs[3][...], "tn"),
        grid=(1, 1, T // tk_t), tm=D, tn=D, outs=[_plain_out(D, D, D, D, F32)],
        epilogue=dw_out_epilogue, name="mm_dw_out")[0]
    early.append(d_w_out.reshape(NS, D // NS, D))
    def dmixed_epilogue(parts, i, j, extra_refs, out_refs):
        a_ref, r_ref, g_ref = extra_refs
        do_ref, dd_ref, dmc_ref, dg_ref = out_refs
        head_of = lambda axis: lax.broadcasted_iota(jnp.int32, (AW, AW), axis) // HEAD_DIM
        same_head = (head_of(0) == head_of(1)).astype(BF16)
        dg = None
        for rows, acc in zip(halves, parts):
            dm = acc[:, :AW]
            dmc_ref[rows, :] = acc[:, AW:]
            a = a_ref[rows, :]
            r = r_ref[rows, 0:1]
            dxn = dm * g_ref[...]
            da = r * (dxn - a * (r * r) * jnp.mean(dxn * a, axis=-1, keepdims=True))
            do_ref[rows, :] = da.astype(BF16)
            hi, lo = _split_hi_lo(da * a)
            dd_ref[rows, :] = (jnp.dot(hi, same_head, preferred_element_type=F32)
                               + jnp.dot(lo, same_head, preferred_element_type=F32))
            dg_h = jnp.sum(dm * a * r, axis=0, keepdims=True)
            dg = dg_h if dg is None else dg + dg_h
        _accumulate(dg_ref, i == 0, dg)

    dattn, dd, dmc, d_attn_norm_g, sib_out = _matmul_general(
        [(dz1b, (tm, D), rowD), (w_out, (D, D), vecD)],
        lambda refs, i, j, k: tuple(_dot(refs[0][rows, :], refs[1][...], "nt") for rows in halves),
        grid=(T // tm, 1, 1), tm=tm, tn=D,
        extras=[(attn, (tm, AW), rowD), (r_attn, (tm, LANES), rowD), (attn_norm_g, (1, AW), vecD)],
        outs=[((T, AW), BF16, (tm, AW), rowD), ((T, AW), F32, (tm, AW), rowD), ((T, CW), F32, (tm, CW), rowD),
              ((1, AW), F32, (1, AW), vecD)],
        epilogue=dmixed_epilogue, name="mm_dmixed", bg=_bg_sibling_exchange(early[2:]))
    sib_e.append(sib_out)
    chip_e.append(_pair_sum(early[2], sib_out, ids, name="pair_sum_w_out"))

    dag, d_conv_w, d_conv_b, d_conv_ln_g, d_conv_ln_b, d_conv_norm_g = _conv_bwd(
        ag, dmc, conv_w, conv_b, conv_ln_g, conv_ln_b, conv_norm_g, B, S, CW)

    dq, dk, dv, csq, csk, csv, dbias, *got_e = _attention_bwd(qkv, dattn, lse, dd, bias_all, B, S, AW,
                                                              bg=_bg_chip_exchange(chip_e))
    full_up, full_down, full_out = [_final_sum(g, s, r, ids, name="final_sum_" + n)
                                    for g, s, r, n in zip(early, sib_e, got_e, ("w_up", "w_down", "w_out"))]
    d_rel_table = _rel_grad(dbias.reshape(3, H, ATTN_BLOCK * 2 * ATTN_BLOCK), bucket).T
    dh, cs_ag = _dh_cat(dq, dk, dv, dag, tm_s)
    d_b_in = jnp.concatenate([csq, csk, csv, cs_ag], axis=1)

    d_w_in_t = _mm_plain(dh, xf, mode="tn", tm=_col_tile(INW, 1408), tn=D, tk=tk_t, out_dtype=F32, name="mm_dw_in")
    late = [d_w_in_t.reshape(NS, INW // NS, D)]
    sib_l = _sibling_exchange(late)
    chip_l = [_pair_sum(late[0], sib_l[0], ids, name="pair_sum_w_in")]
    small = dict(rel_table=d_rel_table, b_in=d_b_in, conv_w=d_conv_w, conv_b=d_conv_b, conv_ln_g=d_conv_ln_g,
                 conv_ln_b=d_conv_ln_b, attn_norm_g=d_attn_norm_g, conv_norm_g=d_conv_norm_g, ln1_g=d_ln1_g,
                 ln1_b=d_ln1_b, ffn_conv_w=d_ffn_cw, ffn_conv_b=d_ffn_cb, ln2_g=d_ln2_g, ln2_b=d_ln2_b)
    pack = _pack([loss_part] + [small[n] for n in SMALL_NAMES])

    def gx_epilogue(acc, i, j, extra_refs, out_refs):
        out_refs[0][...] = acc + ALPHA * extra_refs[0][...]

    grad_x, got_in, all_packs = _matmul(
        dh, w_in_t, mode="nn", tm=tm, tn=D, tk=INW, extras=[(dz1, (tm, D), rowD)],
        outs=[((T, D), F32, (tm, D), rowD)], epilogue=gx_epilogue, name="mm_grad_x",
        bg=_bg_chip_exchange(chip_l, pack))
    full_in = _final_sum(late[0], sib_l[0], got_in, ids, name="final_sum_w_in")
    return grad_x.reshape(B, S, D), [full_in, full_out, full_up, full_down], all_packs


def _place():
    return lax.axis_index("x"), lax.axis_index("y"), lax.axis_index("c")


CHIP_FLIPS = ((1, 0), (0, 1), (1, 1))


def _flip(v, f):
    return 1 - v if f else v


HBM_SPEC = pl.BlockSpec(memory_space=pl.ANY)
VMEM_SPEC = pl.BlockSpec(memory_space=pltpu.VMEM)
COMM_PARAMS = pltpu.CompilerParams(vmem_limit_bytes=VMEM_LIMIT)


def _gather_weights(big, small):
    nb, ns = len(big), len(small)

    def body(*refs):
        big_in = refs[:nb]
        small_in = refs[nb:nb + ns]
        big_out = refs[nb + ns:2 * nb + ns]
        small_out = refs[2 * nb + ns:2 * nb + 2 * ns]
        stages = refs[2 * nb + 2 * ns:3 * nb + 2 * ns]
        send_sems, recv_sems, local_sems = refs[3 * nb + 2 * ns:]
        x, y, c = _place()
        s_me = 2 * x + y
        sibling = (x, y, 1 - c)
        started, local_copies = [], []
        for a in range(nb):
            rh = big[a].shape[0] // 2
            lo = pl.multiple_of(c * rh, 16)
            stages[a][...] = big_in[a][pl.ds(lo, rh), :].astype(BF16)
            mine = big_out[a].at[s_me, pl.ds(lo, rh), :]
            loc = pltpu.make_async_copy(stages[a], mine, local_sems.at[a])
            loc.start()
            local_copies.append(loc)
            targets = [sibling] + [(_flip(x, fx), _flip(y, fy), c) for fx, fy in CHIP_FLIPS]
            for k, to in enumerate(targets):
                cp = pltpu.make_async_remote_copy(stages[a], mine, send_sems.at[a * 7 + k],
                                                  recv_sems.at[a * 7 + k], device_id=to, device_id_type=MESH)
                cp.start()
                started.append(cp)
        for a in range(ns):
            mine = small_out[a].at[s_me]
            loc = pltpu.make_async_copy(small_in[a], mine, local_sems.at[nb + a])
            loc.start()
            local_copies.append(loc)
            for k, (fx, fy) in enumerate(CHIP_FLIPS):
                cp = pltpu.make_async_remote_copy(small_in[a], mine, send_sems.at[nb * 7 + a * 3 + k],
                                                  recv_sems.at[nb * 7 + a * 3 + k],
                                                  device_id=(_flip(x, fx), _flip(y, fy), c), device_id_type=MESH)
                cp.start()
                started.append(cp)
        for a in range(nb):
            rh = big[a].shape[0] // 2
            lo = pl.multiple_of(c * rh, 16)
            for k, (fx, fy) in enumerate(CHIP_FLIPS):
                s_from = 2 * _flip(x, fx) + _flip(y, fy)
                got = big_out[a].at[s_from, pl.ds(lo, rh), :]
                pltpu.make_async_remote_copy(got, got, send_sems.at[a * 7 + 1 + k], recv_sems.at[a * 7 + 1 + k],
                                             device_id=sibling, device_id_type=MESH).wait_recv()
                fwd = pltpu.make_async_remote_copy(got, got, send_sems.at[a * 7 + 4 + k],
                                                   recv_sems.at[a * 7 + 4 + k], device_id=sibling,
                                                   device_id_type=MESH)
                fwd.start()
                started.append(fwd)
        for a in range(nb):
            rh = big[a].shape[0] // 2
            lo_sib = pl.multiple_of((1 - c) * rh, 16)
            for k in (0, 4, 5, 6):
                any_rows = big_out[a].at[s_me, pl.ds(lo_sib, rh), :]
                pltpu.make_async_remote_copy(any_rows, any_rows, send_sems.at[a * 7 + k], recv_sems.at[a * 7 + k],
                                             device_id=sibling, device_id_type=MESH).wait_recv()
        for a in range(ns):
            for k in range(3):
                pltpu.make_async_remote_copy(small_in[a], small_out[a].at[s_me], send_sems.at[nb * 7 + a * 3 + k],
                                             recv_sems.at[nb * 7 + a * 3 + k], device_id=sibling,
                                             device_id_type=MESH).wait_recv()
        for cp in started:
            cp.wait_send()
        for cp in local_copies:
            cp.wait()

    n_sem = nb * 7 + ns * 3
    out_shape = ([jax.ShapeDtypeStruct((N_SHARDS,) + w.shape, BF16) for w in big]
                 + [jax.ShapeDtypeStruct((N_SHARDS,) + w.shape, F32) for w in small])
    res = pl.pallas_call(
        body, in_specs=[VMEM_SPEC] * nb + [HBM_SPEC] * ns, out_specs=[HBM_SPEC] * (nb + ns),
        out_shape=out_shape,
        scratch_shapes=[pltpu.VMEM((w.shape[0] // 2, w.shape[1]), BF16) for w in big]
        + [pltpu.SemaphoreType.DMA((n_sem,)), pltpu.SemaphoreType.DMA((n_sem,)),
           pltpu.SemaphoreType.DMA((nb + ns,))],
        compiler_params=COMM_PARAMS, name="gather_weights",
    )(*big, *small)
    return res[:nb], res[nb:]


def _sibling_exchange(grads):
    n = len(grads)

    def body(*refs):
        g_in = refs[:n]
        got = refs[n:2 * n]
        send_sems, recv_sems = refs[2 * n:]
        x, y, c = _place()
        cps = []
        for a in range(n):
            rh = grads[a].shape[1] // 2
            lo = pl.multiple_of((1 - c) * rh, 8)
            cp = pltpu.make_async_remote_copy(g_in[a].at[:, pl.ds(lo, rh), :], got[a], send_sems.at[a],
                                              recv_sems.at[a], device_id=(x, y, 1 - c), device_id_type=MESH)
            cp.start()
            cps.append(cp)
        for cp in cps:
            cp.wait()

    return pl.pallas_call(
        body, in_specs=[HBM_SPEC] * n, out_specs=[HBM_SPEC] * n,
        out_shape=[jax.ShapeDtypeStruct((N_SHARDS, g.shape[1] // 2, g.shape[2]), F32) for g in grads],
        scratch_shapes=[pltpu.SemaphoreType.DMA((n,)), pltpu.SemaphoreType.DMA((n,))],
        compiler_params=COMM_PARAMS, name="sibling_exchange",
    )(*grads)


def _sibling_assemble(fulls):
    n = len(fulls)

    def body(*refs):
        full = refs[n:2 * n]
        send_sems, recv_sems = refs[2 * n:]
        x, y, c = _place()
        cps = []
        for a in range(n):
            rh = fulls[a].shape[0] // 2
            mine = full[a].at[pl.ds(pl.multiple_of(c * rh, 8), rh), :]
            cp = pltpu.make_async_remote_copy(mine, mine, send_sems.at[a], recv_sems.at[a],
                                              device_id=(x, y, 1 - c), device_id_type=MESH)
            cp.start()
            cps.append(cp)
        for cp in cps:
            cp.wait()

    return pl.pallas_call(
        body, in_specs=[HBM_SPEC] * n, out_specs=[HBM_SPEC] * n,
        out_shape=[jax.ShapeDtypeStruct(f.shape, F32) for f in fulls],
        input_output_aliases={a: a for a in range(n)},
        scratch_shapes=[pltpu.SemaphoreType.DMA((n,)), pltpu.SemaphoreType.DMA((n,))],
        compiler_params=COMM_PARAMS, name="sibling_assemble",
    )(*fulls)


def _remote(ref_src, ref_dst, send_sems, recv_sems, k, to):
    return pltpu.make_async_remote_copy(ref_src, ref_dst, send_sems.at[k], recv_sems.at[k], device_id=to,
                                        device_id_type=MESH)


def _stage_half(w, ids, name):
    R, C = w.shape
    rh = R // 2
    rt = _half_tile(rh)
    nt = rh // rt

    def body(ids_ref, w_ref, o_ref):
        o_ref[0] = w_ref[...].astype(BF16)

    grid_spec = pltpu.PrefetchScalarGridSpec(
        num_scalar_prefetch=1, grid=(nt,),
        in_specs=[pl.BlockSpec((rt, C), lambda i, ids: (ids[2] * nt + i, 0))],
        out_specs=pl.BlockSpec((1, rt, C), lambda i, ids: (2 * ids[0] + ids[1], ids[2] * nt + i, 0)))
    return pl.pallas_call(body, grid_spec=grid_spec, out_shape=jax.ShapeDtypeStruct((N_SHARDS, R, C), BF16),
                          compiler_params=_params(1), name=name)(ids, w)


def _bg_gather(staged):
    n = len(staged)

    def run(step, n_steps, ins, outs, send_sems, recv_sems, local_sems, post):
        x, y, c = _place()
        s_me = 2 * x + y
        sibling = (x, y, 1 - c)
        chips = [(_flip(x, fx), _flip(y, fy)) for fx, fy in CHIP_FLIPS]

        def rows(a, s, half):
            rh = staged[a].shape[1] // 2
            return outs[a].at[s, pl.ds(pl.multiple_of(half * rh, 16), rh), :]

        def copy(a, k, ref, to):
            return _remote(ref, ref, send_sems, recv_sems, a * 7 + k, to)

        if not post:
            @pl.when(step == 0)
            def _():
                for a in range(n):
                    mine = rows(a, s_me, c)
                    copy(a, 0, mine, sibling).start()
                    for k, (px, py) in enumerate(chips):
                        copy(a, 1 + k, mine, (px, py, c)).start()

            @pl.when(step == max(n_steps - 2, 0))
            def _():
                for a in range(n):
                    for k, (px, py) in enumerate(chips):
                        got = rows(a, 2 * px + py, c)
                        copy(a, 1 + k, got, sibling).wait_recv()
                        copy(a, 4 + k, got, sibling).start()
        else:
            @pl.when(step == n_steps - 1)
            def _():
                for a in range(n):
                    for k in (0, 4, 5, 6):
                        copy(a, k, rows(a, s_me, 1 - c), sibling).wait_recv()
                    for k in range(7):
                        copy(a, k, rows(a, s_me, c), sibling).wait_send()

    return _Background(staged, [jax.ShapeDtypeStruct(g.shape, g.dtype) for g in staged],
                       {a: a for a in range(n)}, 7 * n, run)


def _bg_sibling_exchange(grads):
    n = len(grads)

    def run(step, n_steps, ins, outs, send_sems, recv_sems, local_sems, post):
        x, y, c = _place()

        def copy(a):
            rh = grads[a].shape[1] // 2
            lo = pl.multiple_of((1 - c) * rh, 8)
            return _remote(ins[a].at[:, pl.ds(lo, rh), :], outs[a], send_sems, recv_sems, a, (x, y, 1 - c))

        if not post:
            @pl.when(step == 0)
            def _():
                for a in range(n):
                    copy(a).start()
        else:
            @pl.when(step == n_steps - 1)
            def _():
                for a in range(n):
                    copy(a).wait()

    return _Background(grads, [jax.ShapeDtypeStruct((N_SHARDS, g.shape[1] // 2, g.shape[2]), F32) for g in grads],
                       {}, n, run)


def _bg_chip_exchange(chip_parts, pack=None):
    n = len(chip_parts)

    def run(step, n_steps, ins, outs, send_sems, recv_sems, local_sems, post):
        x, y, c = _place()
        me = 4 * x + 2 * y + c

        def copies():
            cps = []
            for a in range(n):
                for k, (fx, fy) in enumerate(CHIP_FLIPS):
                    px, py = _flip(x, fx), _flip(y, fy)
                    cps.append(_remote(ins[a].at[2 * px + py], outs[a].at[k], send_sems, recv_sems, a * 3 + k,
                                       (px, py, c)))
            if pack is not None:
                for m in range(1, N_DEV):
                    to = (_flip(x, m & 4), _flip(y, m & 2), _flip(c, m & 1))
                    cps.append(_remote(ins[n], outs[n].at[me], send_sems, recv_sems, n * 3 + m - 1, to))
            return cps

        def local():
            return pltpu.make_async_copy(ins[n], outs[n].at[me], local_sems.at[0])

        if not post:
            @pl.when(step == 0)
            def _():
                for cp in copies():
                    cp.start()
                if pack is not None:
                    local().start()
        else:
            @pl.when(step == n_steps - 1)
            def _():
                for cp in copies():
                    cp.wait()
                if pack is not None:
                    local().wait()

    in_arrays = list(chip_parts) + ([pack] if pack is not None else [])
    out_shapes = [jax.ShapeDtypeStruct((3,) + p.shape[1:], BF16) for p in chip_parts]
    if pack is not None:
        out_shapes.append(jax.ShapeDtypeStruct((N_DEV, pack.shape[0], LANES), F32))
    return _Background(in_arrays, out_shapes, {}, n * 3 + N_DEV - 1, run)


def _half_tile(rh, mult=16, want=512):
    best = None
    for t in range(mult, min(rh, want) + 1, mult):
        if rh % t == 0:
            best = t
    return best if best is not None else rh


def _pair_sum(g, sib, ids, name):
    _, R, C = g.shape
    rh = R // 2
    rt = _half_tile(rh)
    nt = rh // rt

    def body(ids_ref, g_ref, s_ref, o_ref):
        o_ref[...] = (g_ref[...] + s_ref[...]).astype(BF16)

    def other(j, ids):
        return j + (j >= 2 * ids[0] + ids[1]).astype(jnp.int32)

    grid_spec = pltpu.PrefetchScalarGridSpec(
        num_scalar_prefetch=1, grid=(N_SHARDS - 1, nt),
        in_specs=[pl.BlockSpec((1, rt, C), lambda j, i, ids: (other(j, ids), ids[2] * nt + i, 0)),
                  pl.BlockSpec((1, rt, C), lambda j, i, ids: (other(j, ids), i, 0))],
        out_specs=pl.BlockSpec((1, rt, C), lambda j, i, ids: (other(j, ids), i, 0)))
    return pl.pallas_call(body, grid_spec=grid_spec, out_shape=jax.ShapeDtypeStruct((N_SHARDS, rh, C), BF16),
                          compiler_params=_params(2), name=name)(ids, g, sib)


def _final_sum(g, sib, got, ids, name):
    _, R, C = g.shape
    rh = R // 2
    rt = _half_tile(rh)
    nt = rh // rt

    def body(ids_ref, g_ref, s_ref, r_ref, o_ref):
        tot = g_ref[0] + s_ref[0]
        for k in range(3):
            tot = tot + r_ref[k].astype(F32)
        o_ref[...] = tot

    grid_spec = pltpu.PrefetchScalarGridSpec(
        num_scalar_prefetch=1, grid=(nt,),
        in_specs=[pl.BlockSpec((1, rt, C), lambda i, ids: (2 * ids[0] + ids[1], ids[2] * nt + i, 0)),
                  pl.BlockSpec((1, rt, C), lambda i, ids: (2 * ids[0] + ids[1], i, 0)),
                  pl.BlockSpec((3, rt, C), lambda i, ids: (0, i, 0))],
        out_specs=pl.BlockSpec((rt, C), lambda i, ids: (ids[2] * nt + i, 0)))
    return pl.pallas_call(body, grid_spec=grid_spec, out_shape=jax.ShapeDtypeStruct((R, C), F32),
                          compiler_params=_params(1), name=name)(ids, g, sib, got)


def _sum_packs(all_packs):
    def body(p_ref, o_ref):
        tot = p_ref[0]
        for i in range(1, N_DEV):
            tot = tot + p_ref[i]
        o_ref[...] = tot

    return pl.pallas_call(body, in_specs=[VMEM_SPEC], out_specs=VMEM_SPEC,
                          out_shape=jax.ShapeDtypeStruct(all_packs.shape[1:], F32), name="sum_packs")(all_packs)


def _adamw(w, g, m, v, name, g_transposed=False):
    R, C = w.shape
    rt = _half_tile(R, mult=LANES if g_transposed else 8, want=256)

    def body(w_ref, g_ref, m_ref, v_ref, g_out_ref, d_ref, nm_ref, nv_ref):
        gg = g_ref[...].T if g_transposed else g_ref[...]
        g_out_ref[...] = gg
        d_ref[...], nm_ref[...], nv_ref[...] = _adamw_update(w_ref[...], gg, m_ref[...], v_ref[...])

    spec = pl.BlockSpec((rt, C), lambda i: (i, 0))
    g_spec = pl.BlockSpec((C, rt), lambda i: (0, i)) if g_transposed else spec
    return pl.pallas_call(body, grid=(R // rt,), in_specs=[spec, g_spec, spec, spec], out_specs=[spec] * 4,
                          out_shape=[jax.ShapeDtypeStruct((R, C), F32)] * 4,
                          compiler_params=_params(1), name=name)(w, g, m, v)


def _adamw_update(w, g, m, v):
    nm = ADAM_B1 * m + (1.0 - ADAM_B1) * g
    nv = ADAM_B2 * v + (1.0 - ADAM_B2) * (g * g)
    m_hat = nm / (1.0 - ADAM_B1 ** ADAM_STEP)
    v_hat = nv / (1.0 - ADAM_B2 ** ADAM_STEP)
    return -ADAM_LR * (m_hat / (jnp.sqrt(v_hat) + ADAM_EPS) + ADAM_WD * w), nm, nv


def _adamw_many(ws, gs, ms, vs, name):
    n = len(ws)

    def body(*refs):
        for i in range(n):
            d, nm, nv = _adamw_update(refs[i][...], refs[n + i][...], refs[2 * n + i][...], refs[3 * n + i][...])
            refs[4 * n + i][...] = d
            refs[5 * n + i][...] = nm
            refs[6 * n + i][...] = nv

    return pl.pallas_call(body, in_specs=[VMEM_SPEC] * (4 * n), out_specs=[VMEM_SPEC] * (3 * n),
                          out_shape=[jax.ShapeDtypeStruct(w.shape, F32) for w in ws] * 3, name=name,
                          )(*ws, *gs, *ms, *vs)


def _pack(pieces):
    rows = []
    for p in pieces:
        flat = p.reshape(-1)
        pad = (-flat.shape[0]) % LANES
        if pad:
            flat = jnp.concatenate([flat, jnp.zeros((pad,), F32)])
        rows.append(flat.reshape(-1, LANES))
    total = sum(r.shape[0] for r in rows)
    pad_rows = (-total) % 8
    if pad_rows:
        rows.append(jnp.zeros((pad_rows, LANES), F32))
    return jnp.concatenate(rows, axis=0)


def _unpack(buf, shapes):
    out, r0 = [], 0
    for shp in shapes:
        n = int(np.prod(shp))
        nr = -(-n // LANES)
        out.append(buf[r0:r0 + nr].reshape(-1)[:n].reshape(shp))
        r0 += nr
    return out


SMALL_NAMES = ("rel_table", "b_in", "conv_w", "conv_b", "conv_ln_g", "conv_ln_b", "attn_norm_g", "conv_norm_g",
               "ln1_g", "ln1_b", "ffn_conv_w", "ffn_conv_b", "ln2_g", "ln2_b")
BIG_NAMES = ("w_in", "w_out", "w_up", "w_down")
WEIGHT_ORDER = ("rel_table", "w_in", "b_in", "conv_w", "conv_b", "conv_ln_g", "conv_ln_b", "attn_norm_g",
                "conv_norm_g", "w_out", "ln1_g", "ln1_b", "w_up", "ffn_conv_w", "ffn_conv_b", "w_down",
                "ln2_g", "ln2_b")


def kernel(x, rel_table, w_in, b_in, conv_w, conv_b, conv_ln_g, conv_ln_b, attn_norm_g, conv_norm_g, w_out, ln1_g, ln1_b, w_up, ffn_conv_w, ffn_conv_b, w_down, ln2_g, ln2_b, loss_target, m_rel_table, m_w_in, m_b_in, m_conv_w, m_conv_b, m_conv_ln_g, m_conv_ln_b, m_attn_norm_g, m_conv_norm_g, m_w_out, m_ln1_g, m_ln1_b, m_w_up, m_ffn_conv_w, m_ffn_conv_b, m_w_down, m_ln2_g, m_ln2_b, v_rel_table, v_w_in, v_b_in, v_conv_w, v_conv_b, v_conv_ln_g, v_conv_ln_b, v_attn_norm_g, v_conv_norm_g, v_w_out, v_ln1_g, v_ln1_b, v_w_up, v_ffn_conv_w, v_ffn_conv_b, v_w_down, v_ln2_g, v_ln2_b):
    args = dict(locals())
    weights = {n: args[n] for n in WEIGHT_ORDER}
    moms = {n: args["m_" + n] for n in WEIGHT_ORDER}
    vels = {n: args["v_" + n] for n in WEIGHT_ORDER}
    xi, yi, ci = _place()
    ids = jnp.stack([xi, yi, ci]).astype(jnp.int32)
    shard = 2 * xi + yi
    D = x.shape[-1]
    DFF = w_down.shape[1] * N_SHARDS
    CW = conv_norm_g.shape[-1]

    tr = lambda t: jnp.transpose(t[0])
    (g_in,), (g_cw, g_fcw) = _gather_weights([tr(w_in)], [conv_w[0], ffn_conv_w[0]])
    cols = lambda t: jnp.transpose(t, (1, 0, 2)).reshape(t.shape[1], N_SHARDS * t.shape[2])
    staged = [_stage_half(w[0], ids, name="stage_" + n) for w, n in ((w_out, "w_out"), (w_up, "w_up"),
                                                                     (w_down, "w_down"))]

    grad_x, fulls, all_packs = _local_step(
        x, loss_target, rel_table, g_in.reshape(-1, D), b_in, cols(g_cw), conv_b, conv_ln_g, conv_ln_b, attn_norm_g,
        conv_norm_g, staged, ln1_g, ln1_b, cols(g_fcw), ffn_conv_b, ln2_g, ln2_b, ids)
    big_grads = dict(zip(BIG_NAMES, _sibling_assemble(fulls)))

    summed = _sum_packs(all_packs)
    full_shapes = {n: weights[n].shape for n in SMALL_NAMES}
    full_shapes["conv_w"] = (1, CONV_KERNEL, CW)
    full_shapes["ffn_conv_w"] = (1, FFN_CONV_KERNEL, 2 * DFF)
    un = _unpack(summed, [(1, LANES)] + [full_shapes[n] for n in SMALL_NAMES])
    loss = un[0][0, 0]
    small_grads = dict(zip(SMALL_NAMES, un[1:]))
    for n in ("conv_w", "ffn_conv_w"):
        width = weights[n].shape[-1]
        small_grads[n] = lax.dynamic_slice_in_dim(small_grads[n], shard * width, width, axis=2)

    grads, delta, new_m, new_v = {}, {}, {}, {}
    for n in BIG_NAMES:
        shp = weights[n].shape
        g2 = big_grads[n]
        if n == "w_in":
            res = [jnp.transpose(t) for t in _adamw(tr(weights[n]), g2, tr(moms[n]), tr(vels[n]), name="adamw_" + n)]
        else:
            res = _adamw(weights[n][0], g2, moms[n][0], vels[n][0], name="adamw_" + n, g_transposed=n == "w_up")
        grads[n], delta[n], new_m[n], new_v[n] = (t.reshape(shp) for t in res)
    pick = lambda src: [src[n] for n in SMALL_NAMES]
    small_out = _adamw_many(pick(weights), pick(small_grads), pick(moms), pick(vels), name="adamw_small")
    ns = len(SMALL_NAMES)
    for tgt, part in ((delta, small_out[:ns]), (new_m, small_out[ns:2 * ns]), (new_v, small_out[2 * ns:])):
        tgt.update(zip(SMALL_NAMES, part))
    grads.update(small_grads)

    return (loss, grad_x, *[grads[n] for n in WEIGHT_ORDER], *[delta[n] for n in WEIGHT_ORDER],
            *[new_m[n] for n in WEIGHT_ORDER], *[new_v[n] for n in WEIGHT_ORDER])
```

```python
import math

import numpy as np
import jax
import jax.numpy as jnp
from jax import lax
from jax.experimental import pallas as pl
from jax.experimental.pallas import tpu as pltpu

F32 = jnp.float32
BF16 = jnp.bfloat16
MESH = pl.DeviceIdType.MESH

HEAD_DIM = 64
LANES = 128
ATTN_BLOCK = 128
DILATED_CONFIGS = ((128, 1), (512, 4), (2048, 16))
CONV_KERNEL = 31
FFN_CONV_KERNEL = 3
REL_BUCKETS = 32
REL_MAX_DIST = 2048
DEPTH = 1
ALPHA = (2 * DEPTH) ** 0.25
LN_EPS = 1e-5
NEG_INF = -1e30
QK_SCALE = 1.0 / math.sqrt(HEAD_DIM)
ADAM_LR = 0.001
ADAM_B1 = 0.9
ADAM_B2 = 0.999
ADAM_EPS = 1e-08
ADAM_WD = 0.01
ADAM_STEP = 10
VMEM_LIMIT = 52 * 1024 * 1024
FFN_COLS = 128
N_SHARDS = 4
N_DEV = 8


def _params(n_axes):
    return pltpu.CompilerParams(dimension_semantics=("arbitrary",) * n_axes,
                                vmem_limit_bytes=VMEM_LIMIT)


MM_DIMS = {"nn": (((1,), (0,)), ((), ())), "nt": (((1,), (1,)), ((), ())), "tn": (((0,), (0,)), ((), ()))}


class _Background:
    def __init__(self, in_arrays, out_shapes, aliases, n_sems, run, n_local=1):
        self.in_arrays, self.out_shapes, self.aliases = list(in_arrays), list(out_shapes), dict(aliases)
        self.n_sems, self.n_local, self.run = n_sems, n_local, run

    def scratch(self):
        return [pltpu.SemaphoreType.DMA((self.n_sems,)), pltpu.SemaphoreType.DMA((self.n_sems,)),
                pltpu.SemaphoreType.DMA((self.n_local,))]


def _hosted_call(body, bg, *, grid, in_specs, out_specs, out_shape, scratch_shapes, operands, name):
    n_in, n_out, n_scr = len(in_specs), len(out_specs), len(scratch_shapes)
    if bg is None:
        return pl.pallas_call(lambda *refs: body(refs, lambda post: None), grid=grid, in_specs=in_specs,
                              out_specs=out_specs, out_shape=out_shape, scratch_shapes=scratch_shapes,
                              compiler_params=_params(len(grid)), name=name)(*operands)
    nb_in, nb_out = len(bg.in_arrays), len(bg.out_shapes)
    n_steps = int(np.prod(grid))

    def full_body(*refs):
        own = refs[:n_in] + refs[n_in + nb_in:n_in + nb_in + n_out] \
            + refs[n_in + nb_in + n_out + nb_out:n_in + nb_in + n_out + nb_out + n_scr]
        bg_in = refs[n_in:n_in + nb_in]
        bg_out = refs[n_in + nb_in + n_out:n_in + nb_in + n_out + nb_out]
        sems = refs[n_in + nb_in + n_out + nb_out + n_scr:]
        step = pl.program_id(0)
        for ax in range(1, len(grid)):
            step = step * grid[ax] + pl.program_id(ax)

        def hook(post):
            bg.run(step, n_steps, bg_in, bg_out, *sems, post)

        body(own, hook)

    res = pl.pallas_call(
        full_body, grid=grid, in_specs=list(in_specs) + [HBM_SPEC] * nb_in,
        out_specs=list(out_specs) + [HBM_SPEC] * nb_out, out_shape=list(out_shape) + bg.out_shapes,
        input_output_aliases={n_in + a: n_out + o for a, o in bg.aliases.items()},
        scratch_shapes=list(scratch_shapes) + bg.scratch(), compiler_params=_params(len(grid)), name=name,
    )(*operands, *bg.in_arrays)
    return res


def _matmul_general(ins, part_fn, *, grid, tm, tn, outs, epilogue, extras=(), name, bg=None):
    nk = grid[2]
    n_in, n_extra = len(ins), len(extras)

    def body(refs, bg_hook):
        in_refs = refs[:n_in]
        rest = refs[n_in:]
        extra_refs = rest[:n_extra]
        out_refs = rest[n_extra:n_extra + len(outs)]
        acc_ref = rest[-1]
        i, j, k = pl.program_id(0), pl.program_id(1), pl.program_id(2)
        bg_hook(False)
        part = part_fn(in_refs, i, j, k)
        if nk == 1:
            epilogue(part, i, j, extra_refs, out_refs)
        else:
            @pl.when(k == 0)
            def _():
                acc_ref[...] = part

            @pl.when(k > 0)
            def _():
                acc_ref[...] += part

            @pl.when(k == nk - 1)
            def _():
                epilogue(acc_ref[...], i, j, extra_refs, out_refs)
        bg_hook(True)

    in_specs = [pl.BlockSpec(bs, im) for (_, bs, im) in list(ins) + list(extras)]
    out_specs = [pl.BlockSpec(bs, im) for (_, _, bs, im) in outs]
    out_shape = [jax.ShapeDtypeStruct(s, d) for (s, d, _, _) in outs]
    return _hosted_call(body, bg, grid=grid, in_specs=in_specs, out_specs=out_specs, out_shape=out_shape,
                        scratch_shapes=[pltpu.VMEM((tm, tn), F32)],
                        operands=[e[0] for e in ins] + [e[0] for e in extras], name=name)


def _dot(a, b, mode):
    return lax.dot_general(a.astype(BF16), b.astype(BF16), MM_DIMS[mode], preferred_element_type=F32)


def _matmul(a, b, *, mode, tm, tn, tk, outs, epilogue, extras=(), name, bg=None):
    if mode == "tn":
        K, M = a.shape
        N = b.shape[1]
        ins = [(a, (tk, tm), lambda i, j, k: (k, i)), (b, (tk, tn), lambda i, j, k: (k, j))]
    elif mode == "nt":
        M, K = a.shape
        N = b.shape[0]
        ins = [(a, (tm, tk), lambda i, j, k: (i, k)), (b, (tn, tk), lambda i, j, k: (j, k))]
    else:
        M, K = a.shape
        N = b.shape[1]
        ins = [(a, (tm, tk), lambda i, j, k: (i, k)), (b, (tk, tn), lambda i, j, k: (k, j))]
    assert M % tm == 0 and N % tn == 0 and K % tk == 0, (name, M, N, K, tm, tn, tk)

    def part_fn(in_refs, i, j, k):
        return _dot(in_refs[0][...], in_refs[1][...], mode)

    return _matmul_general(ins, part_fn, grid=(M // tm, N // tn, K // tk), tm=tm, tn=tn, outs=outs,
                           epilogue=epilogue, extras=extras, name=name, bg=bg)


def _plain_out(M, N, tm, tn, dtype):
    return ((M, N), dtype, (tm, tn), lambda i, j, k: (i, j))


def _mm_plain(a, b, *, mode, tm, tn, tk, out_dtype, name, bias=None, bg=None):
    if mode == "tn":
        M, N = a.shape[1], b.shape[1]
    elif mode == "nt":
        M, N = a.shape[0], b.shape[0]
    else:
        M, N = a.shape[0], b.shape[1]
    extras = []
    if bias is not None:
        extras.append((bias, (1, tn), lambda i, j, k: (0, j)))

    def epilogue(acc, i, j, extra_refs, out_refs):
        if bias is not None:
            acc = acc + extra_refs[0][...]
        out_refs[0][...] = acc.astype(out_dtype)

    res = _matmul(a, b, mode=mode, tm=tm, tn=tn, tk=tk, outs=[_plain_out(M, N, tm, tn, out_dtype)],
                  epilogue=epilogue, extras=extras, name=name, bg=bg)
    return res[0] if bg is None else res


def _row_tile(T, want):
    t = min(T, want)
    while T % t:
        t //= 2
    return t


def _col_tile(N, want):
    if N <= want:
        return N
    best = None
    for c in range(LANES, want + 1, LANES):
        if N % c == 0:
            best = c
    return best if best is not None else N


def _accumulate(ref, first, val):
    @pl.when(first)
    def _():
        ref[...] = val

    @pl.when(jnp.logical_not(first))
    def _():
        ref[...] += val


def _ln_fwd(z, g, b):
    mu = jnp.mean(z, axis=-1, keepdims=True)
    zc = z - mu
    var = jnp.mean(zc * zc, axis=-1, keepdims=True)
    r = lax.rsqrt(var + LN_EPS)
    xh = zc * r
    return xh * g + b, xh, r


def _ln_bwd(dy, xh, r, g):
    dxh = dy * g
    m1 = jnp.mean(dxh, axis=-1, keepdims=True)
    m2 = jnp.mean(dxh * xh, axis=-1, keepdims=True)
    return r * (dxh - m1 - xh * m2)


def _sigmoid(x):
    return 0.5 * jnp.tanh(0.5 * x) + 0.5


def _bucket_tables():
    exact = REL_BUCKETS // 2
    qi = np.arange(ATTN_BLOCK)[:, None]
    kj = np.arange(2 * ATTN_BLOCK)[None, :]
    steps = qi + ATTN_BLOCK - kj
    buckets, masks = [], []
    for window, dilation in DILATED_CONFIGS:
        max_steps = window // dilation
        band = (steps >= 0) & (steps <= max_steps)
        dist = np.maximum(steps, 0) * dilation
        d_f = np.maximum(dist, 1).astype(np.float32)
        large = exact + (np.log(d_f / np.float32(exact)) / np.float32(math.log(REL_MAX_DIST / exact))
                         * np.float32(REL_BUCKETS - exact)).astype(np.int32)
        large = np.minimum(large, REL_BUCKETS - 1)
        bucket = np.where(dist < exact, dist, large).astype(np.int32)
        buckets.append(bucket.reshape(1, -1))
        masks.append(np.where(band, 0.0, NEG_INF).astype(np.float32).reshape(1, -1))
    return np.stack(buckets), np.stack(masks)


def _split_hi_lo(x):
    hi = x.astype(BF16)
    lo = (x - hi.astype(F32)).astype(BF16)
    return hi, lo


def _bias_build(rel_table_t, bucket, mask):
    H = rel_table_t.shape[0]
    n = bucket.shape[-1]

    def body(t_ref, bkt_ref, mask_ref, o_ref):
        onehot = (lax.broadcasted_iota(jnp.int32, (REL_BUCKETS, n), 0) == bkt_ref[0]).astype(BF16)
        t = t_ref[...]
        t1 = t.astype(BF16)
        r1 = t - t1.astype(F32)
        t2 = r1.astype(BF16)
        t3 = (r1 - t2.astype(F32)).astype(BF16)
        acc = jnp.dot(t1, onehot, preferred_element_type=F32)
        acc = acc + jnp.dot(t2, onehot, preferred_element_type=F32)
        acc = acc + jnp.dot(t3, onehot, preferred_element_type=F32)
        o_ref[0] = acc + mask_ref[0]

    return pl.pallas_call(
        body, grid=(3,),
        in_specs=[pl.BlockSpec((H, REL_BUCKETS), lambda b: (0, 0)),
                  pl.BlockSpec((1, 1, n), lambda b: (b, 0, 0)),
                  pl.BlockSpec((1, 1, n), lambda b: (b, 0, 0))],
        out_specs=pl.BlockSpec((1, H, n), lambda b: (b, 0, 0)),
        out_shape=jax.ShapeDtypeStruct((3, H, n), F32),
        compiler_params=_params(1), name="bias_build",
    )(rel_table_t, bucket, mask)


def _rel_grad(dbias, bucket):
    H = dbias.shape[1]
    n = bucket.shape[-1]
    dims = (((1,), (1,)), ((), ()))

    def body(d_ref, bkt_ref, o_ref):
        b = pl.program_id(0)
        onehot = (lax.broadcasted_iota(jnp.int32, (REL_BUCKETS, n), 0) == bkt_ref[0]).astype(BF16)
        d = d_ref[0]
        d1 = d.astype(BF16)
        r1 = d - d1.astype(F32)
        d2 = r1.astype(BF16)
        d3 = (r1 - d2.astype(F32)).astype(BF16)
        acc = lax.dot_general(d1, onehot, dims, preferred_element_type=F32)
        acc = acc + lax.dot_general(d2, onehot, dims, preferred_element_type=F32)
        acc = acc + lax.dot_general(d3, onehot, dims, preferred_element_type=F32)
        _accumulate(o_ref, b == 0, acc)

    return pl.pallas_call(
        body, grid=(3,),
        in_specs=[pl.BlockSpec((1, H, n), lambda b: (b, 0, 0)),
                  pl.BlockSpec((1, 1, n), lambda b: (b, 0, 0))],
        out_specs=pl.BlockSpec((H, REL_BUCKETS), lambda b: (0, 0)),
        out_shape=jax.ShapeDtypeStruct((H, REL_BUCKETS), F32),
        compiler_params=_params(1), name="rel_grad",
    )(dbias, bucket)


def _regroup(src, stage, dst, d, S, off=0):
    if d == 1:
        dst[off:off + S, :] = src.astype(dst.dtype)
        return
    stage[...] = src.astype(F32)
    L = S // d
    for r in range(d):
        dst[off + r * L:off + (r + 1) * L, :] = stage[pl.ds(r, L, stride=d), :].astype(dst.dtype)


def _ungroup(sub_ref, off, nat_ref, d, S, add):
    L = S // d
    for r in range(d):
        rows = pl.ds(0, S) if d == 1 else pl.ds(r, L, stride=d)
        val = sub_ref[off + r * L:off + (r + 1) * L, :]
        if add:
            nat_ref[rows, :] += val
        else:
            nat_ref[rows, :] = val


def _branch_keys(ks, vs, S, nb, g_idx):
    blk3 = (S // ATTN_BLOCK, ATTN_BLOCK, LANES)
    kc3 = ks[ATTN_BLOCK:ATTN_BLOCK + S, :].reshape(blk3)
    vc3 = vs[ATTN_BLOCK:ATTN_BLOCK + S, :].reshape(blk3)
    if nb == 1:
        return kc3, vc3, None
    kk3 = jnp.concatenate([ks[0:S, :].reshape(blk3), kc3], axis=1)
    vv3 = jnp.concatenate([vs[0:S, :].reshape(blk3), vc3], axis=1)
    col = lax.broadcasted_iota(jnp.int32, (1, 1, 2 * ATTN_BLOCK), 2)
    dead = jnp.logical_and((g_idx & (nb - 1)) == 0, col < ATTN_BLOCK)
    return kk3, vv3, dead


def _branch_scores(qe, kk3, b_ref, bi, e, dead):
    s = jnp.einsum("gqe,gke->gqk", qe, kk3, preferred_element_type=F32)
    if dead is None:
        return s + b_ref[bi, e, :, ATTN_BLOCK:]
    return jnp.where(dead, NEG_INF, s + b_ref[bi, e])


def _attention_fwd(qkv, bias_all, B, S, AW, bg=None):
    HP = AW // LANES
    G = S // ATTN_BLOCK
    blk3 = (G, ATTN_BLOCK, LANES)

    def body(refs, bg_hook):
        q_ref, k_ref, v_ref, b_ref, o_ref, lse_ref, stage, qs, ks, vs, ot, lt, on0, on1, on2, ln0, ln1, ln2 = refs
        bg_hook(False)
        head0 = lax.broadcasted_iota(jnp.int32, (1, 1, LANES), 2) < HEAD_DIM
        g_idx = lax.broadcasted_iota(jnp.int32, (G, 1, 1), 0)
        ks[0:ATTN_BLOCK, :] = jnp.zeros((ATTN_BLOCK, LANES), BF16)
        vs[0:ATTN_BLOCK, :] = jnp.zeros((ATTN_BLOCK, LANES), BF16)
        nat_o, nat_l = (on0, on1, on2), (ln0, ln1, ln2)
        for bi, (_, d) in enumerate(DILATED_CONFIGS):
            nb = S // d // ATTN_BLOCK
            _regroup(q_ref[0], stage, qs, d, S)
            _regroup(k_ref[0], stage, ks, d, S, ATTN_BLOCK)
            _regroup(v_ref[0], stage, vs, d, S, ATTN_BLOCK)
            q3 = qs[...].reshape(blk3) * QK_SCALE
            kk3, vv3, dead = _branch_keys(ks, vs, S, nb, g_idx)
            outs, lses = [], []
            for e in range(2):
                msk = head0 if e == 0 else jnp.logical_not(head0)
                qe = jnp.where(msk, q3, jnp.zeros_like(q3))
                s = _branch_scores(qe, kk3, b_ref, bi, e, dead)
                m = jnp.max(s, axis=-1, keepdims=True)
                p = jnp.exp(s - m)
                l = jnp.sum(p, axis=-1, keepdims=True)
                o = jnp.einsum("gqk,gke->gqe", p.astype(BF16), vv3, preferred_element_type=F32)
                outs.append(o / l)
                lses.append(jnp.broadcast_to(m + jnp.log(l), blk3))
            ot[...] = jnp.where(head0, outs[0], outs[1]).reshape(S, LANES)
            lt[...] = jnp.where(head0, lses[0], lses[1]).reshape(S, LANES)
            _ungroup(ot, 0, nat_o[bi], d, S, add=False)
            _ungroup(lt, 0, nat_l[bi], d, S, add=False)

        la, lb, lc = ln0[...], ln1[...], ln2[...]
        m = jnp.maximum(jnp.maximum(la, lb), lc)
        ea, eb, ec = jnp.exp(la - m), jnp.exp(lb - m), jnp.exp(lc - m)
        den = ea + eb + ec
        lse_ref[0] = m + jnp.log(den)
        o_ref[0] = (ea * on0[...] + eb * on1[...] + ec * on2[...]) / den
        bg_hook(True)

    blk = lambda off: pl.BlockSpec((1, S, LANES), lambda b, h: (b, 0, off + h))
    qv = qkv.reshape(B, S, 3 * AW)
    sub_f = pltpu.VMEM((S, LANES), F32)
    pad_b = pltpu.VMEM((S + ATTN_BLOCK, LANES), BF16)
    res = _hosted_call(
        body, bg, grid=(B, HP),
        in_specs=[blk(0), blk(HP), blk(2 * HP),
                  pl.BlockSpec((3, 2, ATTN_BLOCK, 2 * ATTN_BLOCK), lambda b, h: (0, h, 0, 0))],
        out_specs=[blk(0), blk(0)],
        out_shape=[jax.ShapeDtypeStruct((B, S, AW), F32)] * 2,
        scratch_shapes=[sub_f, pltpu.VMEM((S, LANES), BF16), pad_b, pad_b] + [sub_f] * 8,
        operands=[qv, qv, qv, bias_all], name="attention_fwd")
    return (res[0].reshape(B * S, AW), res[1].reshape(B * S, AW)) + tuple(res[2:])


def _attention_bwd(qkv, do, lse, dd, bias_all, B, S, AW, bg=None):
    HP = AW // LANES
    H = AW // HEAD_DIM
    G = S // ATTN_BLOCK
    blk3 = (G, ATTN_BLOCK, LANES)
    PAD = ATTN_BLOCK

    def body(refs, bg_hook):
        (q_ref, k_ref, v_ref, do_ref, lse_ref, dd_ref, b_ref,
         dq_ref, dk_ref, dv_ref, csq_ref, csk_ref, csv_ref, db_ref,
         stage, qs, ks, vs, gs, ls, ds_, tq, tk, tv, accq, acck, accv) = refs
        bg_hook(False)
        head0 = lax.broadcasted_iota(jnp.int32, (1, 1, LANES), 2) < HEAD_DIM
        g_idx = lax.broadcasted_iota(jnp.int32, (G, 1, 1), 0)
        first_b = pl.program_id(1) == 0

        @pl.when(first_b)
        def _():
            db_ref[...] = jnp.zeros_like(db_ref)

        ks[0:PAD, :] = jnp.zeros((PAD, LANES), BF16)
        vs[0:PAD, :] = jnp.zeros((PAD, LANES), BF16)
        tk[0:PAD, :] = jnp.zeros((PAD, LANES), F32)
        tv[0:PAD, :] = jnp.zeros((PAD, LANES), F32)
        for bi, (_, d) in enumerate(DILATED_CONFIGS):
            nb = S // d // ATTN_BLOCK
            _regroup(q_ref[0], stage, qs, d, S)
            _regroup(k_ref[0], stage, ks, d, S, PAD)
            _regroup(v_ref[0], stage, vs, d, S, PAD)
            _regroup(do_ref[0], stage, gs, d, S)
            _regroup(lse_ref[0], stage, ls, d, S)
            _regroup(dd_ref[0], stage, ds_, d, S)
            q3 = qs[...].reshape(blk3) * QK_SCALE
            do3 = gs[...].reshape(blk3)
            lse3 = ls[...].reshape(blk3)
            dd3 = ds_[...].reshape(blk3)
            kk3, vv3, dead = _branch_keys(ks, vs, S, nb, g_idx)
            dq_c, dkk_c, dvv_c = [], [], []
            half_g = G // 2 if G % 2 == 0 else G
            for g0 in range(0, G, half_g):
                sl = slice(g0, g0 + half_g)
                dead_c = None if dead is None else dead[sl]
                dq = dkk = dvv = None
                for e in range(2):
                    msk = head0 if e == 0 else jnp.logical_not(head0)
                    c0 = e * HEAD_DIM
                    qe = jnp.where(msk, q3[sl], jnp.zeros_like(q3[sl]))
                    doe = jnp.where(msk, do3[sl], jnp.zeros_like(do3[sl]))
                    ke = jnp.where(msk, kk3[sl] * QK_SCALE, jnp.zeros_like(kk3[sl]))
                    s = _branch_scores(qe, kk3[sl], b_ref, bi, e, dead_c)
                    p = jnp.exp(s - lse3[sl, :, c0:c0 + 1])
                    dp = jnp.einsum("gqe,gke->gqk", doe, vv3[sl], preferred_element_type=F32)
                    dsc = p * (dp - dd3[sl, :, c0:c0 + 1])
                    if dead is None:
                        db_ref[bi, e, :, ATTN_BLOCK:] += jnp.sum(dsc, axis=0)
                    else:
                        db_ref[bi, e] += jnp.sum(dsc, axis=0)
                    dsb = dsc.astype(BF16)
                    dq_e = jnp.einsum("gqk,gke->gqe", dsb, ke, preferred_element_type=F32)
                    dk_e = jnp.einsum("gqk,gqe->gke", dsb, qe, preferred_element_type=F32)
                    dv_e = jnp.einsum("gqk,gqe->gke", p.astype(BF16), doe, preferred_element_type=F32)
                    dq, dkk, dvv = (dq_e, dk_e, dv_e) if dq is None else (dq + dq_e, dkk + dk_e, dvv + dv_e)
                dq_c.append(dq)
                dkk_c.append(dkk)
                dvv_c.append(dvv)
            dq, dkk, dvv = (jnp.concatenate(t, axis=0) for t in (dq_c, dkk_c, dvv_c))
            tq[...] = dq.reshape(S, LANES)
            if dead is None:
                tk[PAD:PAD + S, :] = dkk.reshape(S, LANES)
                tv[PAD:PAD + S, :] = dvv.reshape(S, LANES)
            else:
                tk[PAD:PAD + S, :] = dkk[:, ATTN_BLOCK:, :].reshape(S, LANES)
                tv[PAD:PAD + S, :] = dvv[:, ATTN_BLOCK:, :].reshape(S, LANES)
                tk[0:S, :] += dkk[:, :ATTN_BLOCK, :].reshape(S, LANES)
                tv[0:S, :] += dvv[:, :ATTN_BLOCK, :].reshape(S, LANES)
            _ungroup(tq, 0, accq, d, S, add=bi > 0)
            _ungroup(tk, PAD, acck, d, S, add=bi > 0)
            _ungroup(tv, PAD, accv, d, S, add=bi > 0)

        for acc, out_ref, cs_ref in ((accq, dq_ref, csq_ref), (acck, dk_ref, csk_ref), (accv, dv_ref, csv_ref)):
            tot = acc[...]
            out_ref[0] = tot.astype(out_ref.dtype)
            _accumulate(cs_ref, first_b, jnp.sum(tot, axis=0, keepdims=True))
        bg_hook(True)

    blk = lambda off: pl.BlockSpec((1, S, LANES), lambda h, b: (b, 0, off + h))
    cs_spec = pl.BlockSpec((1, LANES), lambda h, b: (0, h))
    bias_spec = pl.BlockSpec((3, 2, ATTN_BLOCK, 2 * ATTN_BLOCK), lambda h, b: (0, h, 0, 0))
    qv = qkv.reshape(B, S, 3 * AW)
    view = lambda t: t.reshape(B, S, AW)
    sub_b = pltpu.VMEM((S, LANES), BF16)
    sub_f = pltpu.VMEM((S, LANES), F32)
    pad_b = pltpu.VMEM((S + PAD, LANES), BF16)
    pad_f = pltpu.VMEM((S + PAD, LANES), F32)
    res = _hosted_call(
        body, bg, grid=(HP, B),
        in_specs=[blk(0), blk(HP), blk(2 * HP), blk(0), blk(0), blk(0), bias_spec],
        out_specs=[blk(0), blk(0), blk(0), cs_spec, cs_spec, cs_spec, bias_spec],
        out_shape=[jax.ShapeDtypeStruct((B, S, AW), BF16)] * 3 + [jax.ShapeDtypeStruct((1, AW), F32)] * 3
        + [jax.ShapeDtypeStruct((3, H, ATTN_BLOCK, 2 * ATTN_BLOCK), F32)],
        scratch_shapes=[sub_f, sub_b, pad_b, pad_b, sub_b, sub_f, sub_f, sub_f, pad_f, pad_f, sub_f, sub_f, sub_f],
        operands=[qv, qv, qv, view(do), view(lse), view(dd), bias_all], name="attention_bwd")
    flat = lambda t: t.reshape(B * S, AW)
    return (flat(res[0]), flat(res[1]), flat(res[2]), res[3], res[4], res[5], res[6]) + tuple(res[7:])


class _RowShifts:
    def __init__(self, x, row, up):
        self.x, self.row, self.up, self.base = x, row, up, {0: x}

    def __call__(self, s):
        x = self.x
        n, c = x.shape
        r, whole = s % 8, s - s % 8
        if r not in self.base:
            if self.up:
                rolled = pltpu.roll(x, n - r, 0)
                tail = jnp.where(self.row[n - 8:] < n - r, rolled[n - 8:], 0.0)
                self.base[r] = jnp.concatenate([rolled[:n - 8], tail], axis=0)
            else:
                rolled = pltpu.roll(x, r, 0)
                head = jnp.where(self.row[:8] >= r, rolled[:8], 0.0)
                self.base[r] = jnp.concatenate([head, rolled[8:]], axis=0)
        y = self.base[r]
        if whole == 0:
            return y
        pad = jnp.zeros((whole, c), x.dtype)
        if self.up:
            return jnp.concatenate([y[whole:], pad], axis=0)
        return jnp.concatenate([pad, y[:n - whole]], axis=0)


def _conv_branch_fwd_math(a, g, w_ref, cb, lg, lb, row):
    sg = _sigmoid(g)
    u0 = a * sg
    u0_down = _RowShifts(u0, row, up=False)
    uc = jnp.zeros_like(u0) + cb
    for k in range(CONV_KERNEL):
        uc = uc + w_ref[k:k + 1, :] * u0_down(CONV_KERNEL - 1 - k)
    ul, xh, r = _ln_fwd(uc, lg, lb)
    su = _sigmoid(ul)
    u = ul * su
    return sg, u0_down, ul, xh, r, su, u


def _conv_fwd(ag, conv_w, conv_b, ln_g, ln_b, norm_g, B, S, CW):
    def body(a_ref, g_ref, w_ref, cb_ref, lg_ref, lb_ref, ng_ref, o_ref):
        row = lax.broadcasted_iota(jnp.int32, (S, CW), 0)
        _, _, _, _, _, _, u = _conv_branch_fwd_math(a_ref[0], g_ref[0], w_ref, cb_ref[...], lg_ref[...],
                                                    lb_ref[...], row)
        rr = lax.rsqrt(jnp.mean(u * u, axis=-1, keepdims=True) + LN_EPS)
        o_ref[0] = (u * rr * ng_ref[...]).astype(BF16)

    vec = pl.BlockSpec((1, CW), lambda b: (0, 0))
    out = pl.pallas_call(
        body, grid=(B,),
        in_specs=[pl.BlockSpec((1, S, CW), lambda b: (b, 0, 0)), pl.BlockSpec((1, S, CW), lambda b: (b, 0, 1)),
                  pl.BlockSpec((CONV_KERNEL, CW), lambda b: (0, 0)), vec, vec, vec, vec],
        out_specs=pl.BlockSpec((1, S, CW), lambda b: (b, 0, 0)),
        out_shape=jax.ShapeDtypeStruct((B, S, CW), BF16),
        compiler_params=_params(1), name="conv_fwd",
    )(ag.reshape(B, S, 2 * CW), ag.reshape(B, S, 2 * CW), conv_w, conv_b, ln_g, ln_b, norm_g)
    return out.reshape(B * S, CW)


def _conv_bwd(ag, dmc, conv_w, conv_b, ln_g, ln_b, norm_g, B, S, CW):
    def body(a_ref, g_ref, dm_ref, w_ref, cb_ref, lg_ref, lb_ref, ng_ref,
             dag_ref, dw_ref, dcb_ref, dlg_ref, dlb_ref, dng_ref):
        b = pl.program_id(0)
        row = lax.broadcasted_iota(jnp.int32, (S, CW), 0)
        a, g = a_ref[0], g_ref[0]
        sg, u0_down, ul, xh, r, su, u = _conv_branch_fwd_math(a, g, w_ref, cb_ref[...], lg_ref[...], lb_ref[...], row)
        rr = lax.rsqrt(jnp.mean(u * u, axis=-1, keepdims=True) + LN_EPS)
        dm = dm_ref[0]
        dxn = dm * ng_ref[...]
        du = rr * (dxn - u * (rr * rr) * jnp.mean(dxn * u, axis=-1, keepdims=True))
        dul = du * su * (1.0 + ul * (1.0 - su))
        duc = _ln_bwd(dul, xh, r, lg_ref[...])
        first = b == 0
        _accumulate(dng_ref, first, jnp.sum(dm * u * rr, axis=0, keepdims=True))
        _accumulate(dlg_ref, first, jnp.sum(dul * xh, axis=0, keepdims=True))
        _accumulate(dlb_ref, first, jnp.sum(dul, axis=0, keepdims=True))
        _accumulate(dcb_ref, first, jnp.sum(duc, axis=0, keepdims=True))

        @pl.when(first)
        def _():
            dw_ref[...] = jnp.zeros_like(dw_ref)

        duc_up = _RowShifts(duc, row, up=True)
        du0 = jnp.zeros_like(duc)
        for k in range(CONV_KERNEL):
            sh = CONV_KERNEL - 1 - k
            dw_ref[k:k + 1, :] += jnp.sum(duc * u0_down(sh), axis=0, keepdims=True)
            du0 = du0 + w_ref[k:k + 1, :] * duc_up(sh)
        dag_ref[0, :, :CW] = du0 * sg
        dag_ref[0, :, CW:] = du0 * a * sg * (1.0 - sg)

    vec = pl.BlockSpec((1, CW), lambda b: (0, 0))
    wspec = pl.BlockSpec((CONV_KERNEL, CW), lambda b: (0, 0))
    agv = ag.reshape(B, S, 2 * CW)
    res = pl.pallas_call(
        body, grid=(B,),
        in_specs=[pl.BlockSpec((1, S, CW), lambda b: (b, 0, 0)), pl.BlockSpec((1, S, CW), lambda b: (b, 0, 1)),
                  pl.BlockSpec((1, S, CW), lambda b: (b, 0, 0)), wspec, vec, vec, vec, vec],
        out_specs=[pl.BlockSpec((1, S, 2 * CW), lambda b: (b, 0, 0)), wspec, vec, vec, vec, vec],
        out_shape=[jax.ShapeDtypeStruct((B, S, 2 * CW), F32), jax.ShapeDtypeStruct((CONV_KERNEL, CW), F32)]
        + [jax.ShapeDtypeStruct((1, CW), F32)] * 4,
        compiler_params=_params(1), name="conv_bwd",
    )(agv, agv, dmc.reshape(B, S, CW), conv_w, conv_b, ln_g, ln_b, norm_g)
    return (res[0].reshape(B * S, 2 * CW),) + tuple(res[1:])


def _ffn_conv(x, w_ref, bias, row):
    down = x if isinstance(x, _RowShifts) else _RowShifts(x, row, up=False)
    y = jnp.zeros_like(down.x) + bias
    for k in range(FFN_CONV_KERNEL):
        y = y + w_ref[k:k + 1, :] * down(FFN_CONV_KERNEL - 1 - k)
    return y


def _ffn_specs(S, tc, nj, order):
    pick = (lambda b, j: (b, j)) if order == "bj" else (lambda j, b: (b, j))
    act = lambda off: pl.BlockSpec((1, S, tc), lambda *g: (pick(*g)[0], 0, off + pick(*g)[1]))
    cw = lambda off: pl.BlockSpec((FFN_CONV_KERNEL, tc), lambda *g: (0, off + pick(*g)[1]))
    cb = lambda off: pl.BlockSpec((1, tc), lambda *g: (0, off + pick(*g)[1]))
    return act, cw, cb


FFN_HALO = 16


def _half_sequences(S):
    if S < 8 * FFN_HALO:
        return [(0, S, 0, S)]
    h = S // 2
    return [(0, h + FFN_HALO, 0, h), (h - FFN_HALO, S, FFN_HALO, h)]


def _w_up_block_spec(w_up_sh, tc, off):
    _, D, cs = w_up_sh.shape
    assert cs % tc == 0
    bps = cs // tc
    return pl.BlockSpec((1, D, tc), lambda j: ((off + j) // bps, 0, (off + j) % bps))


def _ffn_fwd_fused(x1b, w_up_sh, cw, cb, B, S, DFF):
    tc = FFN_COLS
    nj = DFF // tc
    D = x1b.shape[1]

    def body(x_ref, wg_ref, wv_ref, cwg_ref, cwv_ref, cbg_ref, cbv_ref, o_ref, up_ref):
        w = jnp.concatenate([wg_ref[0], wv_ref[0]], axis=1)
        for b in range(B):
            for lo, hi, o0, on in _half_sequences(S):
                row = lax.broadcasted_iota(jnp.int32, (hi - lo, tc), 0)
                up = jnp.dot(x_ref[b, lo:hi, :], w, preferred_element_type=F32)
                up_ref[b, lo + o0:lo + o0 + on, :] = up[o0:o0 + on]
                gate = _ffn_conv(up[:, :tc], cwg_ref, cbg_ref[...], row)
                val = _ffn_conv(up[:, tc:], cwv_ref, cbv_ref[...], row)
                o_ref[b, lo + o0:lo + o0 + on, :] = (gate * _sigmoid(gate) * val).astype(BF16)[o0:o0 + on]

    cws = lambda off: pl.BlockSpec((FFN_CONV_KERNEL, tc), lambda j: (0, off + j))
    cbs = lambda off: pl.BlockSpec((1, tc), lambda j: (0, off + j))
    act, upre = pl.pallas_call(
        body, grid=(nj,),
        in_specs=[pl.BlockSpec((B, S, D), lambda j: (0, 0, 0), pipeline_mode=pl.Buffered(1)),
                  _w_up_block_spec(w_up_sh, tc, 0), _w_up_block_spec(w_up_sh, tc, nj),
                  cws(0), cws(nj), cbs(0), cbs(nj)],
        out_specs=[pl.BlockSpec((B, S, tc), lambda j: (0, 0, j)), pl.BlockSpec((B, S, 2 * tc), lambda j: (0, 0, j))],
        out_shape=[jax.ShapeDtypeStruct((B, S, DFF), BF16), jax.ShapeDtypeStruct((B, S, 2 * DFF), F32)],
        compiler_params=_params(1), name="ffn_fwd",
    )(x1b.reshape(B, S, D), w_up_sh, w_up_sh, cw, cw, cb, cb)
    return act.reshape(B * S, DFF), upre


def _ffn_bwd_fused(x1b, dz2b, upre, w_down, cw, cb, B, S, DFF):
    tc = FFN_COLS
    nj = DFF // tc
    D = x1b.shape[1]

    def body(x_ref, dz_ref, up_ref, wd_ref, cwg_ref, cwv_ref, cbg_ref, cbv_ref,
             dug_ref, duv_ref, dwu_ref, dwd_ref, dcw_ref, dcb_ref):
        first = pl.program_id(1) == 0
        dw_t = dwd = None
        dcb = [None, None]
        dcw = [[None] * FFN_CONV_KERNEL, [None] * FFN_CONV_KERNEL]
        add = lambda old, new: new if old is None else old + new
        for lo, hi, o0, on in _half_sequences(S):
            n = hi - lo
            own = slice(o0, o0 + on)
            row = lax.broadcasted_iota(jnp.int32, (n, tc), 0)
            x = x_ref[0, lo:hi, :]
            dz = dz_ref[0, lo:hi, :]
            ug = _RowShifts(up_ref[0, lo:hi, :tc], row, up=False)
            uv = _RowShifts(up_ref[0, lo:hi, tc:], row, up=False)
            gate = _ffn_conv(ug, cwg_ref, cbg_ref[...], row)
            val = _ffn_conv(uv, cwv_ref, cbv_ref[...], row)
            sg = _sigmoid(gate)
            act = (gate * sg * val).astype(BF16)
            dact = _dot(dz, wd_ref[...], "nt")
            dgate = dact * val * sg * (1.0 + gate * (1.0 - sg))
            dval = dact * gate * sg
            dupre = []
            for h, (dup, u_down, w_ref) in enumerate(((dgate, ug, cwg_ref), (dval, uv, cwv_ref))):
                dcb[h] = add(dcb[h], jnp.sum(dup[own], axis=0, keepdims=True))
                dup_up = _RowShifts(dup, row, up=True)
                acc = jnp.zeros_like(dup)
                for k in range(FFN_CONV_KERNEL):
                    sh = FFN_CONV_KERNEL - 1 - k
                    dcw[h][k] = add(dcw[h][k], jnp.sum((dup * u_down(sh))[own], axis=0, keepdims=True))
                    acc = acc + w_ref[k:k + 1, :] * dup_up(sh)
                dupre.append(acc.astype(BF16)[own])
            dug_ref[0, lo + o0:lo + o0 + on, :] = dupre[0]
            duv_ref[0, lo + o0:lo + o0 + on, :] = dupre[1]
            dw_t = add(dw_t, _dot(jnp.concatenate(dupre, axis=1), x[own], "tn"))
            dwd = add(dwd, _dot(act[own], dz[own], "tn"))
        _accumulate(dwu_ref.at[0], first, dw_t[:tc])
        _accumulate(dwu_ref.at[1], first, dw_t[tc:])
        _accumulate(dwd_ref, first, dwd)
        for h in range(2):
            _accumulate(dcb_ref.at[h], first, dcb[h])
            for k in range(FFN_CONV_KERNEL):
                _accumulate(dcw_ref.at[k, pl.ds(h, 1), :], first, dcw[h][k])

    act_s, cws, cbs = _ffn_specs(S, tc, nj, "jb")
    seq = pl.BlockSpec((1, S, D), lambda j, b: (b, 0, 0))
    res = pl.pallas_call(
        body, grid=(nj, B),
        in_specs=[seq, seq, pl.BlockSpec((1, S, 2 * tc), lambda j, b: (b, 0, j)),
                  pl.BlockSpec((tc, D), lambda j, b: (j, 0)), cws(0), cws(nj), cbs(0), cbs(nj)],
        out_specs=[act_s(0), act_s(0), pl.BlockSpec((2, tc, D), lambda j, b: (0, j, 0)),
                   pl.BlockSpec((tc, D), lambda j, b: (j, 0)),
                   pl.BlockSpec((FFN_CONV_KERNEL, 2, tc), lambda j, b: (0, 0, j)),
                   pl.BlockSpec((2, 1, tc), lambda j, b: (0, 0, j))],
        out_shape=[jax.ShapeDtypeStruct((B, S, DFF), BF16)] * 2
        + [jax.ShapeDtypeStruct((2, DFF, D), F32), jax.ShapeDtypeStruct((DFF, D), F32),
           jax.ShapeDtypeStruct((FFN_CONV_KERNEL, 2, DFF), F32), jax.ShapeDtypeStruct((2, 1, DFF), F32)],
        compiler_params=_params(2), name="ffn_bwd",
    )(x1b.reshape(B, S, D), dz2b.reshape(B, S, D), upre, w_down, cw, cw, cb, cb)
    flat = lambda t: t.reshape(B * S, DFF)
    return flat(res[0]), flat(res[1]), res[2], res[3], res[4], res[5]


def _dx1_ln1_bwd(dupre_g, dupre_v, w_up_sh, dz2, xh1, r1, ln1_g, tm, bg):
    T, D = dz2.shape
    NS, _, cs = w_up_sh.shape
    half = NS // 2
    DFF = dupre_g.shape[1]

    def body(refs, bg_hook):
        dug_ref, duv_ref, w_ref, dz2_ref, xh_ref, r_ref, g_ref, dz_ref, dzb_ref, dg_ref, db_ref = refs
        bg_hook(False)
        first = pl.program_id(0) == 0
        dg = db = None
        for rows in (slice(0, tm // 2), slice(tm // 2, tm)):
            dx1 = ALPHA * dz2_ref[rows, :]
            for k in range(NS):
                src = dug_ref if k < half else duv_ref
                c0 = (k % half) * cs
                dx1 = dx1 + _dot(src[rows, c0:c0 + cs], w_ref[k], "nt")
            xh = xh_ref[rows, :]
            dz = _ln_bwd(dx1, xh, r_ref[rows, 0:1], g_ref[...])
            dz_ref[rows, :] = dz
            dzb_ref[rows, :] = dz.astype(BF16)
            dg_h, db_h = jnp.sum(dx1 * xh, axis=0, keepdims=True), jnp.sum(dx1, axis=0, keepdims=True)
            dg, db = (dg_h, db_h) if dg is None else (dg + dg_h, db + db_h)
        _accumulate(dg_ref, first, dg)
        _accumulate(db_ref, first, db)
        bg_hook(True)

    row = pl.BlockSpec((tm, D), lambda i: (i, 0))
    vec = pl.BlockSpec((1, D), lambda i: (0, 0))
    du = pl.BlockSpec((tm, DFF), lambda i: (i, 0))
    return _hosted_call(
        body, bg, grid=(T // tm,),
        in_specs=[du, du, pl.BlockSpec((NS, D, cs), lambda i: (0, 0, 0), pipeline_mode=pl.Buffered(1)),
                  row, row, pl.BlockSpec((tm, LANES), lambda i: (i, 0)), vec],
        out_specs=[row, row, vec, vec],
        out_shape=[jax.ShapeDtypeStruct((T, D), F32), jax.ShapeDtypeStruct((T, D), BF16),
                   jax.ShapeDtypeStruct((1, D), F32), jax.ShapeDtypeStruct((1, D), F32)],
        scratch_shapes=[], operands=[dupre_g, dupre_v, w_up_sh, dz2, xh1, r1, ln1_g], name="mm_dx1_ln1_bwd")


def _dh_cat(dq, dk, dv, dag, tm):
    T, AW = dq.shape
    CW2 = dag.shape[1]
    W = 3 * AW + CW2

    def body(dq_ref, dk_ref, dv_ref, dag_ref, dh_ref, cs_ref):
        for c, ref in enumerate((dq_ref, dk_ref, dv_ref)):
            dh_ref[:, c * AW:(c + 1) * AW] = ref[...]
        dg = dag_ref[...]
        dh_ref[:, 3 * AW:] = dg.astype(BF16)
        _accumulate(cs_ref, pl.program_id(0) == 0, jnp.sum(dg, axis=0, keepdims=True))

    row = pl.BlockSpec((tm, AW), lambda i: (i, 0))
    return pl.pallas_call(
        body, grid=(T // tm,),
        in_specs=[row] * 3 + [pl.BlockSpec((tm, CW2), lambda i: (i, 0))],
        out_specs=[pl.BlockSpec((tm, W), lambda i: (i, 0)), pl.BlockSpec((1, CW2), lambda i: (0, 0))],
        out_shape=[jax.ShapeDtypeStruct((T, W), BF16), jax.ShapeDtypeStruct((1, CW2), F32)],
        compiler_params=_params(1), name="dh_cat",
    )(dq, dk, dv, dag)


def _local_step(x, target, rel_table, w_in_t, b_in, conv_w, conv_b, conv_ln_g, conv_ln_b, attn_norm_g,
                conv_norm_g, staged, ln1_g, ln1_b, ffn_cw, ffn_cb, ln2_g, ln2_b, ids):
    B, S, D = x.shape
    T = B * S
    AW = attn_norm_g.shape[-1]
    CW = conv_norm_g.shape[-1]
    H = AW // HEAD_DIM
    DFF = staged[2].shape[0] * staged[2].shape[1]
    INW = 3 * AW + 2 * CW
    xf = x.reshape(T, D)
    tf = target.reshape(T, D)
    tm = _row_tile(T, 512)
    tm_s = tm

    bucket_np, mask_np = _bucket_tables()
    bucket = jnp.asarray(bucket_np)
    band_mask = jnp.asarray(mask_np)
    bias_all = _bias_build(rel_table.T, bucket, band_mask).reshape(3, H, ATTN_BLOCK, 2 * ATTN_BLOCK)

    rowD = lambda i, j, k: (i, 0)
    vecD = lambda i, j, k: (0, 0)

    def in_proj_epilogue(acc, i, j, extra_refs, out_refs):
        h = acc + extra_refs[0][...]
        out_refs[0][...] = h[:, :3 * AW].astype(BF16)
        out_refs[1][...] = h[:, 3 * AW:]

    qkv, ag = _matmul_general(
        [(xf, (tm, D), rowD), (w_in_t, (INW, D), vecD)],
        lambda refs, i, j, k: _dot(refs[0][...], refs[1][...], "nt"),
        grid=(T // tm, 1, 1), tm=tm, tn=INW, extras=[(b_in, (1, INW), vecD)],
        outs=[((T, 3 * AW), BF16, (tm, 3 * AW), rowD), ((T, 2 * CW), F32, (tm, 2 * CW), rowD)],
        epilogue=in_proj_epilogue, name="mm_in")

    attn, lse, w_out_g, w_up_sh, w_down_g = _attention_fwd(qkv, bias_all, B, S, AW, bg=_bg_gather(staged))
    w_out = w_out_g.reshape(D, D)
    w_down = w_down_g.reshape(DFF, D)
    mixed_c = _conv_fwd(ag, conv_w, conv_b, conv_ln_g, conv_ln_b, conv_norm_g, B, S, CW)

    def attn_rstd(a):
        return lax.rsqrt(jnp.mean(a * a, axis=-1, keepdims=True) + LN_EPS)

    def mixed_rows(attn_ref, mc_ref, gain_ref, rows=slice(None)):
        a = attn_ref[rows, :]
        return jnp.concatenate([(a * attn_rstd(a) * gain_ref[...]).astype(BF16), mc_ref[rows, :]], axis=1)

    halves = [slice(0, tm // 2), slice(tm // 2, tm)]

    def ln1_epilogue(parts, i, j, extra_refs, out_refs):
        x_ref, g_ref, b_ref, a_ref = extra_refs
        for rows, acc in zip(halves, parts):
            x1, xh, r = _ln_fwd(acc + ALPHA * x_ref[rows, :], g_ref[...], b_ref[...])
            out_refs[0][rows, :] = x1
            out_refs[1][rows, :] = x1.astype(BF16)
            out_refs[2][rows, :] = xh
            out_refs[3][rows, :] = jnp.broadcast_to(r, (tm // 2, LANES))
            out_refs[4][rows, :] = jnp.broadcast_to(attn_rstd(a_ref[rows, :]), (tm // 2, LANES))

    x1, x1b, xh1, r1, r_attn = _matmul_general(
        [(attn, (tm_s, AW), rowD), (mixed_c, (tm_s, CW), rowD), (attn_norm_g, (1, AW), vecD), (w_out, (D, D), vecD)],
        lambda refs, i, j, k: tuple(_dot(mixed_rows(refs[0], refs[1], refs[2], rows), refs[3][...], "nn")
                                    for rows in halves),
        grid=(T // tm_s, 1, 1), tm=tm_s, tn=D,
        extras=[(xf, (tm_s, D), rowD), (ln1_g, (1, D), vecD), (ln1_b, (1, D), vecD), (attn, (tm_s, AW), rowD)],
        outs=[((T, D), F32, (tm_s, D), rowD), ((T, D), BF16, (tm_s, D), rowD), ((T, D), F32, (tm_s, D), rowD),
              ((T, LANES), F32, (tm_s, LANES), rowD), ((T, LANES), F32, (tm_s, LANES), rowD)],
        epilogue=ln1_epilogue, name="mm_out_ln1")

    NS, _, cs = w_up_sh.shape
    half = NS // 2

    act, upre = _ffn_fwd_fused(x1b, w_up_sh, ffn_cw, ffn_cb, B, S, DFF)

    def ln2_epilogue(parts, i, j, extra_refs, out_refs):
        x1_ref, g_ref, b_ref, t_ref = extra_refs
        dz_ref, dzb_ref, loss_ref, dg_ref, db_ref = out_refs
        g = g_ref[...]
        sums = None
        for rows, acc in zip(halves, parts):
            y, xh, r = _ln_fwd(acc + ALPHA * x1_ref[rows, :], g, b_ref[...])
            diff = y - t_ref[rows, :]
            row_loss = jnp.sum(diff * diff, axis=1, keepdims=True)
            tile_loss = jnp.sum(row_loss, axis=0, keepdims=True) * (0.5 / D)
            dy = diff * (1.0 / D)
            dz = _ln_bwd(dy, xh, r, g)
            dz_ref[rows, :] = dz
            dzb_ref[rows, :] = dz.astype(BF16)
            vals = (jnp.broadcast_to(tile_loss, (1, LANES)), jnp.sum(dy * xh, axis=0, keepdims=True),
                    jnp.sum(dy, axis=0, keepdims=True))
            sums = vals if sums is None else tuple(a + b for a, b in zip(sums, vals))
        for ref, val in zip((loss_ref, dg_ref, db_ref), sums):
            _accumulate(ref, i == 0, val)

    dz2, dz2b, loss_part, d_ln2_g, d_ln2_b = _matmul_general(
        [(act, (tm, DFF), rowD), (w_down, (DFF, D), vecD)],
        lambda refs, i, j, k: tuple(_dot(refs[0][rows, :], refs[1][...], "nn") for rows in halves),
        grid=(T // tm, 1, 1), tm=tm, tn=D,
        extras=[(x1, (tm, D), rowD), (ln2_g, (1, D), vecD), (ln2_b, (1, D), vecD), (tf, (tm, D), rowD)],
        outs=[((T, D), F32, (tm, D), rowD), ((T, D), BF16, (tm, D), rowD),
              ((1, LANES), F32, (1, LANES), vecD), ((1, D), F32, (1, D), vecD), ((1, D), F32, (1, D), vecD)],
        epilogue=ln2_epilogue, name="mm_down_ln2_loss")

    dupre_g, dupre_v, d_w_up_t, d_w_down, d_ffn_cw2, d_ffn_cb2 = _ffn_bwd_fused(
        x1b, dz2b, upre, w_down, ffn_cw, ffn_cb, B, S, DFF)
    d_w_up_t = d_w_up_t.reshape(NS, cs, D)
    d_ffn_cw = d_ffn_cw2.reshape(FFN_CONV_KERNEL, 2 * DFF)
    d_ffn_cb = d_ffn_cb2.reshape(1, 2 * DFF)
    tk_t = _row_tile(T, 1024)

    early = [d_w_up_t, d_w_down.reshape(NS, DFF // NS, D)]
    dz1, dz1b, d_ln1_g, d_ln1_b, *sib_e = _dx1_ln1_bwd(dupre_g, dupre_v, w_up_sh, dz2, xh1, r1, ln1_g, tm,
                                                       bg=_bg_sibling_exchange(early))
    chip_e = [_pair_sum(g, s, ids, name="pair_sum_" + n) for g, s, n in zip(early, sib_e, ("w_up", "w_down"))]

    def dw_out_epilogue(acc, i, j, extra_refs, out_refs):
        out_refs[0][...] = acc

    d_w_out = _matmul_general(
        [(attn, (tk_t, AW), lambda i, j, k: (k, 0)), (mixed_c, (tk_t, CW), lambda i, j, k: (k, 0)),
         (attn_norm_g, (1, AW), vecD), (dz1b, (tk_t, D), lambda i, j, k: (k, 0))],
        lambda refs, i, j, k: _dot(mixed_rows(refs[0], refs[1], refs[2]), refs[3][...], "tn"),
        grid=(1, 1, T // tk_t), tm=D, tn=D, outs=[_plain_out(D, D, D, D, F32)],
        epilogue=dw_out_epilogue, name="mm_dw_out")[0]
    early.append(d_w_out.reshape(NS, D // NS, D))
    def dmixed_epilogue(parts, i, j, extra_refs, out_refs):
        a_ref, r_ref, g_ref = extra_refs
        do_ref, dd_ref, dmc_ref, dg_ref = out_refs
        head_of = lambda axis: lax.broadcasted_iota(jnp.int32, (AW, AW), axis) // HEAD_DIM
        same_head = (head_of(0) == head_of(1)).astype(BF16)
        dg = None
        for rows, acc in zip(halves, parts):
            dm = acc[:, :AW]
            dmc_ref[rows, :] = acc[:, AW:]
            a = a_ref[rows, :]
            r = r_ref[rows, 0:1]
            dxn = dm * g_ref[...]
            da = r * (dxn - a * (r * r) * jnp.mean(dxn * a, axis=-1, keepdims=True))
            do_ref[rows, :] = da.astype(BF16)
            hi, lo = _split_hi_lo(da * a)
            dd_ref[rows, :] = (jnp.dot(hi, same_head, preferred_element_type=F32)
                               + jnp.dot(lo, same_head, preferred_element_type=F32))
            dg_h = jnp.sum(dm * a * r, axis=0, keepdims=True)
            dg = dg_h if dg is None else dg + dg_h
        _accumulate(dg_ref, i == 0, dg)

    dattn, dd, dmc, d_attn_norm_g, sib_out = _matmul_general(
        [(dz1b, (tm, D), rowD), (w_out, (D, D), vecD)],
        lambda refs, i, j, k: tuple(_dot(refs[0][rows, :], refs[1][...], "nt") for rows in halves),
        grid=(T // tm, 1, 1), tm=tm, tn=D,
        extras=[(attn, (tm, AW), rowD), (r_attn, (tm, LANES), rowD), (attn_norm_g, (1, AW), vecD)],
        outs=[((T, AW), BF16, (tm, AW), rowD), ((T, AW), F32, (tm, AW), rowD), ((T, CW), F32, (tm, CW), rowD),
              ((1, AW), F32, (1, AW), vecD)],
        epilogue=dmixed_epilogue, name="mm_dmixed", bg=_bg_sibling_exchange(early[2:]))
    sib_e.append(sib_out)
    chip_e.append(_pair_sum(early[2], sib_out, ids, name="pair_sum_w_out"))

    dag, d_conv_w, d_conv_b, d_conv_ln_g, d_conv_ln_b, d_conv_norm_g = _conv_bwd(
        ag, dmc, conv_w, conv_b, conv_ln_g, conv_ln_b, conv_norm_g, B, S, CW)

    dq, dk, dv, csq, csk, csv, dbias, *got_e = _attention_bwd(qkv, dattn, lse, dd, bias_all, B, S, AW,
                                                              bg=_bg_chip_exchange(chip_e))
    full_up, full_down, full_out = [_final_sum(g, s, r, ids, name="final_sum_" + n)
                                    for g, s, r, n in zip(early, sib_e, got_e, ("w_up", "w_down", "w_out"))]
    d_rel_table = _rel_grad(dbias.reshape(3, H, ATTN_BLOCK * 2 * ATTN_BLOCK), bucket).T
    dh, cs_ag = _dh_cat(dq, dk, dv, dag, tm_s)
    d_b_in = jnp.concatenate([csq, csk, csv, cs_ag], axis=1)

    d_w_in_t = _mm_plain(dh, xf, mode="tn", tm=_col_tile(INW, 1408), tn=D, tk=tk_t, out_dtype=F32, name="mm_dw_in")
    late = [d_w_in_t.reshape(NS, INW // NS, D)]
    sib_l = _sibling_exchange(late)
    chip_l = [_pair_sum(late[0], sib_l[0], ids, name="pair_sum_w_in")]
    small = dict(rel_table=d_rel_table, b_in=d_b_in, conv_w=d_conv_w, conv_b=d_conv_b, conv_ln_g=d_conv_ln_g,
                 conv_ln_b=d_conv_ln_b, attn_norm_g=d_attn_norm_g, conv_norm_g=d_conv_norm_g, ln1_g=d_ln1_g,
                 ln1_b=d_ln1_b, ffn_conv_w=d_ffn_cw, ffn_conv_b=d_ffn_cb, ln2_g=d_ln2_g, ln2_b=d_ln2_b)
    pack = _pack([loss_part] + [small[n] for n in SMALL_NAMES])

    def gx_epilogue(acc, i, j, extra_refs, out_refs):
        out_refs[0][...] = acc + ALPHA * extra_refs[0][...]

    grad_x, got_in, all_packs = _matmul(
        dh, w_in_t, mode="nn", tm=tm, tn=D, tk=INW, extras=[(dz1, (tm, D), rowD)],
        outs=[((T, D), F32, (tm, D), rowD)], epilogue=gx_epilogue, name="mm_grad_x",
        bg=_bg_chip_exchange(chip_l, pack))
    full_in = _final_sum(late[0], sib_l[0], got_in, ids, name="final_sum_w_in")
    return grad_x.reshape(B, S, D), [full_in, full_out, full_up, full_down], all_packs


def _place():
    return lax.axis_index("x"), lax.axis_index("y"), lax.axis_index("c")


CHIP_FLIPS = ((1, 0), (0, 1), (1, 1))


def _flip(v, f):
    return 1 - v if f else v


HBM_SPEC = pl.BlockSpec(memory_space=pl.ANY)
VMEM_SPEC = pl.BlockSpec(memory_space=pltpu.VMEM)
COMM_PARAMS = pltpu.CompilerParams(vmem_limit_bytes=VMEM_LIMIT)


def _gather_weights(big, small):
    nb, ns = len(big), len(small)

    def body(*refs):
        big_in = refs[:nb]
        small_in = refs[nb:nb + ns]
        big_out = refs[nb + ns:2 * nb + ns]
        small_out = refs[2 * nb + ns:2 * nb + 2 * ns]
        stages = refs[2 * nb + 2 * ns:3 * nb + 2 * ns]
        send_sems, recv_sems, local_sems = refs[3 * nb + 2 * ns:]
        x, y, c = _place()
        s_me = 2 * x + y
        sibling = (x, y, 1 - c)
        started, local_copies = [], []
        for a in range(nb):
            rh = big[a].shape[0] // 2
            lo = pl.multiple_of(c * rh, 16)
            stages[a][...] = big_in[a][pl.ds(lo, rh), :].astype(BF16)
            mine = big_out[a].at[s_me, pl.ds(lo, rh), :]
            loc = pltpu.make_async_copy(stages[a], mine, local_sems.at[a])
            loc.start()
            local_copies.append(loc)
            targets = [sibling] + [(_flip(x, fx), _flip(y, fy), c) for fx, fy in CHIP_FLIPS]
            for k, to in enumerate(targets):
                cp = pltpu.make_async_remote_copy(stages[a], mine, send_sems.at[a * 7 + k],
                                                  recv_sems.at[a * 7 + k], device_id=to, device_id_type=MESH)
                cp.start()
                started.append(cp)
        for a in range(ns):
            mine = small_out[a].at[s_me]
            loc = pltpu.make_async_copy(small_in[a], mine, local_sems.at[nb + a])
            loc.start()
            local_copies.append(loc)
            for k, (fx, fy) in enumerate(CHIP_FLIPS):
                cp = pltpu.make_async_remote_copy(small_in[a], mine, send_sems.at[nb * 7 + a * 3 + k],
                                                  recv_sems.at[nb * 7 + a * 3 + k],
                                                  device_id=(_flip(x, fx), _flip(y, fy), c), device_id_type=MESH)
                cp.start()
                started.append(cp)
        for a in range(nb):
            rh = big[a].shape[0] // 2
            lo = pl.multiple_of(c * rh, 16)
            for k, (fx, fy) in enumerate(CHIP_FLIPS):
                s_from = 2 * _flip(x, fx) + _flip(y, fy)
                got = big_out[a].at[s_from, pl.ds(lo, rh), :]
                pltpu.make_async_remote_copy(got, got, send_sems.at[a * 7 + 1 + k], recv_sems.at[a * 7 + 1 + k],
                                             device_id=sibling, device_id_type=MESH).wait_recv()
                fwd = pltpu.make_async_remote_copy(got, got, send_sems.at[a * 7 + 4 + k],
                                                   recv_sems.at[a * 7 + 4 + k], device_id=sibling,
                                                   device_id_type=MESH)
                fwd.start()
                started.append(fwd)
        for a in range(nb):
            rh = big[a].shape[0] // 2
            lo_sib = pl.multiple_of((1 - c) * rh, 16)
            for k in (0, 4, 5, 6):
                any_rows = big_out[a].at[s_me, pl.ds(lo_sib, rh), :]
                pltpu.make_async_remote_copy(any_rows, any_rows, send_sems.at[a * 7 + k], recv_sems.at[a * 7 + k],
                                             device_id=sibling, device_id_type=MESH).wait_recv()
        for a in range(ns):
            for k in range(3):
                pltpu.make_async_remote_copy(small_in[a], small_out[a].at[s_me], send_sems.at[nb * 7 + a * 3 + k],
                                             recv_sems.at[nb * 7 + a * 3 + k], device_id=sibling,
                                             device_id_type=MESH).wait_recv()
        for cp in started:
            cp.wait_send()
        for cp in local_copies:
            cp.wait()

    n_sem = nb * 7 + ns * 3
    out_shape = ([jax.ShapeDtypeStruct((N_SHARDS,) + w.shape, BF16) for w in big]
                 + [jax.ShapeDtypeStruct((N_SHARDS,) + w.shape, F32) for w in small])
    res = pl.pallas_call(
        body, in_specs=[VMEM_SPEC] * nb + [HBM_SPEC] * ns, out_specs=[HBM_SPEC] * (nb + ns),
        out_shape=out_shape,
        scratch_shapes=[pltpu.VMEM((w.shape[0] // 2, w.shape[1]), BF16) for w in big]
        + [pltpu.SemaphoreType.DMA((n_sem,)), pltpu.SemaphoreType.DMA((n_sem,)),
           pltpu.SemaphoreType.DMA((nb + ns,))],
        compiler_params=COMM_PARAMS, name="gather_weights",
    )(*big, *small)
    return res[:nb], res[nb:]


def _sibling_exchange(grads):
    n = len(grads)

    def body(*refs):
        g_in = refs[:n]
        got = refs[n:2 * n]
        send_sems, recv_sems = refs[2 * n:]
        x, y, c = _place()
        cps = []
        for a in range(n):
            rh = grads[a].shape[1] // 2
            lo = pl.multiple_of((1 - c) * rh, 8)
            cp = pltpu.make_async_remote_copy(g_in[a].at[:, pl.ds(lo, rh), :], got[a], send_sems.at[a],
                                              recv_sems.at[a], device_id=(x, y, 1 - c), device_id_type=MESH)
            cp.start()
            cps.append(cp)
        for cp in cps:
            cp.wait()

    return pl.pallas_call(
        body, in_specs=[HBM_SPEC] * n, out_specs=[HBM_SPEC] * n,
        out_shape=[jax.ShapeDtypeStruct((N_SHARDS, g.shape[1] // 2, g.shape[2]), F32) for g in grads],
        scratch_shapes=[pltpu.SemaphoreType.DMA((n,)), pltpu.SemaphoreType.DMA((n,))],
        compiler_params=COMM_PARAMS, name="sibling_exchange",
    )(*grads)


def _sibling_assemble(fulls):
    n = len(fulls)

    def body(*refs):
        full = refs[n:2 * n]
        send_sems, recv_sems = refs[2 * n:]
        x, y, c = _place()
        cps = []
        for a in range(n):
            rh = fulls[a].shape[0] // 2
            mine = full[a].at[pl.ds(pl.multiple_of(c * rh, 8), rh), :]
            cp = pltpu.make_async_remote_copy(mine, mine, send_sems.at[a], recv_sems.at[a],
                                              device_id=(x, y, 1 - c), device_id_type=MESH)
            cp.start()
            cps.append(cp)
        for cp in cps:
            cp.wait()

    return pl.pallas_call(
        body, in_specs=[HBM_SPEC] * n, out_specs=[HBM_SPEC] * n,
        out_shape=[jax.ShapeDtypeStruct(f.shape, F32) for f in fulls],
        input_output_aliases={a: a for a in range(n)},
        scratch_shapes=[pltpu.SemaphoreType.DMA((n,)), pltpu.SemaphoreType.DMA((n,))],
        compiler_params=COMM_PARAMS, name="sibling_assemble",
    )(*fulls)


def _remote(ref_src, ref_dst, send_sems, recv_sems, k, to):
    return pltpu.make_async_remote_copy(ref_src, ref_dst, send_sems.at[k], recv_sems.at[k], device_id=to,
                                        device_id_type=MESH)


def _stage_half(w, ids, name):
    R, C = w.shape
    rh = R // 2
    rt = _half_tile(rh)
    nt = rh // rt

    def body(ids_ref, w_ref, o_ref):
        o_ref[0] = w_ref[...].astype(BF16)

    grid_spec = pltpu.PrefetchScalarGridSpec(
        num_scalar_prefetch=1, grid=(nt,),
        in_specs=[pl.BlockSpec((rt, C), lambda i, ids: (ids[2] * nt + i, 0))],
        out_specs=pl.BlockSpec((1, rt, C), lambda i, ids: (2 * ids[0] + ids[1], ids[2] * nt + i, 0)))
    return pl.pallas_call(body, grid_spec=grid_spec, out_shape=jax.ShapeDtypeStruct((N_SHARDS, R, C), BF16),
                          compiler_params=_params(1), name=name)(ids, w)


def _bg_gather(staged):
    n = len(staged)

    def run(step, n_steps, ins, outs, send_sems, recv_sems, local_sems, post):
        x, y, c = _place()
        s_me = 2 * x + y
        sibling = (x, y, 1 - c)
        chips = [(_flip(x, fx), _flip(y, fy)) for fx, fy in CHIP_FLIPS]

        def rows(a, s, half):
            rh = staged[a].shape[1] // 2
            return outs[a].at[s, pl.ds(pl.multiple_of(half * rh, 16), rh), :]

        def copy(a, k, ref, to):
            return _remote(ref, ref, send_sems, recv_sems, a * 7 + k, to)

        if not post:
            @pl.when(step == 0)
            def _():
                for a in range(n):
                    mine = rows(a, s_me, c)
                    copy(a, 0, mine, sibling).start()
                    for k, (px, py) in enumerate(chips):
                        copy(a, 1 + k, mine, (px, py, c)).start()

            @pl.when(step == max(n_steps - 2, 0))
            def _():
                for a in range(n):
                    for k, (px, py) in enumerate(chips):
                        got = rows(a, 2 * px + py, c)
                        copy(a, 1 + k, got, sibling).wait_recv()
                        copy(a, 4 + k, got, sibling).start()
        else:
            @pl.when(step == n_steps - 1)
            def _():
                for a in range(n):
                    for k in (0, 4, 5, 6):
                        copy(a, k, rows(a, s_me, 1 - c), sibling).wait_recv()
                    for k in range(7):
                        copy(a, k, rows(a, s_me, c), sibling).wait_send()

    return _Background(staged, [jax.ShapeDtypeStruct(g.shape, g.dtype) for g in staged],
                       {a: a for a in range(n)}, 7 * n, run)


def _bg_sibling_exchange(grads):
    n = len(grads)

    def run(step, n_steps, ins, outs, send_sems, recv_sems, local_sems, post):
        x, y, c = _place()

        def copy(a):
            rh = grads[a].shape[1] // 2
            lo = pl.multiple_of((1 - c) * rh, 8)
            return _remote(ins[a].at[:, pl.ds(lo, rh), :], outs[a], send_sems, recv_sems, a, (x, y, 1 - c))

        if not post:
            @pl.when(step == 0)
            def _():
                for a in range(n):
                    copy(a).start()
        else:
            @pl.when(step == n_steps - 1)
            def _():
                for a in range(n):
                    copy(a).wait()

    return _Background(grads, [jax.ShapeDtypeStruct((N_SHARDS, g.shape[1] // 2, g.shape[2]), F32) for g in grads],
                       {}, n, run)


def _bg_chip_exchange(chip_parts, pack=None):
    n = len(chip_parts)

    def run(step, n_steps, ins, outs, send_sems, recv_sems, local_sems, post):
        x, y, c = _place()
        me = 4 * x + 2 * y + c

        def copies():
            cps = []
            for a in range(n):
                for k, (fx, fy) in enumerate(CHIP_FLIPS):
                    px, py = _flip(x, fx), _flip(y, fy)
                    cps.append(_remote(ins[a].at[2 * px + py], outs[a].at[k], send_sems, recv_sems, a * 3 + k,
                                       (px, py, c)))
            if pack is not None:
                for m in range(1, N_DEV):
                    to = (_flip(x, m & 4), _flip(y, m & 2), _flip(c, m & 1))
                    cps.append(_remote(ins[n], outs[n].at[me], send_sems, recv_sems, n * 3 + m - 1, to))
            return cps

        def local():
            return pltpu.make_async_copy(ins[n], outs[n].at[me], local_sems.at[0])

        if not post:
            @pl.when(step == 0)
            def _():
                for cp in copies():
                    cp.start()
                if pack is not None:
                    local().start()
        else:
            @pl.when(step == n_steps - 1)
            def _():
                for cp in copies():
                    cp.wait()
                if pack is not None:
                    local().wait()

    in_arrays = list(chip_parts) + ([pack] if pack is not None else [])
    out_shapes = [jax.ShapeDtypeStruct((3,) + p.shape[1:], BF16) for p in chip_parts]
    if pack is not None:
        out_shapes.append(jax.ShapeDtypeStruct((N_DEV, pack.shape[0], LANES), F32))
    return _Background(in_arrays, out_shapes, {}, n * 3 + N_DEV - 1, run)


def _half_tile(rh, mult=16, want=512):
    best = None
    for t in range(mult, min(rh, want) + 1, mult):
        if rh % t == 0:
            best = t
    return best if best is not None else rh


def _pair_sum(g, sib, ids, name):
    _, R, C = g.shape
    rh = R // 2
    rt = _half_tile(rh)
    nt = rh // rt

    def body(ids_ref, g_ref, s_ref, o_ref):
        o_ref[...] = (g_ref[...] + s_ref[...]).astype(BF16)

    def other(j, ids):
        return j + (j >= 2 * ids[0] + ids[1]).astype(jnp.int32)

    grid_spec = pltpu.PrefetchScalarGridSpec(
        num_scalar_prefetch=1, grid=(N_SHARDS - 1, nt),
        in_specs=[pl.BlockSpec((1, rt, C), lambda j, i, ids: (other(j, ids), ids[2] * nt + i, 0)),
                  pl.BlockSpec((1, rt, C), lambda j, i, ids: (other(j, ids), i, 0))],
        out_specs=pl.BlockSpec((1, rt, C), lambda j, i, ids: (other(j, ids), i, 0)))
    return pl.pallas_call(body, grid_spec=grid_spec, out_shape=jax.ShapeDtypeStruct((N_SHARDS, rh, C), BF16),
                          compiler_params=_params(2), name=name)(ids, g, sib)


def _final_sum(g, sib, got, ids, name):
    _, R, C = g.shape
    rh = R // 2
    rt = _half_tile(rh)
    nt = rh // rt

    def body(ids_ref, g_ref, s_ref, r_ref, o_ref):
        tot = g_ref[0] + s_ref[0]
        for k in range(3):
            tot = tot + r_ref[k].astype(F32)
        o_ref[...] = tot

    grid_spec = pltpu.PrefetchScalarGridSpec(
        num_scalar_prefetch=1, grid=(nt,),
        in_specs=[pl.BlockSpec((1, rt, C), lambda i, ids: (2 * ids[0] + ids[1], ids[2] * nt + i, 0)),
                  pl.BlockSpec((1, rt, C), lambda i, ids: (2 * ids[0] + ids[1], i, 0)),
                  pl.BlockSpec((3, rt, C), lambda i, ids: (0, i, 0))],
        out_specs=pl.BlockSpec((rt, C), lambda i, ids: (ids[2] * nt + i, 0)))
    return pl.pallas_call(body, grid_spec=grid_spec, out_shape=jax.ShapeDtypeStruct((R, C), F32),
                          compiler_params=_params(1), name=name)(ids, g, sib, got)


def _sum_packs(all_packs):
    def body(p_ref, o_ref):
        tot = p_ref[0]
        for i in range(1, N_DEV):
            tot = tot + p_ref[i]
        o_ref[...] = tot

    return pl.pallas_call(body, in_specs=[VMEM_SPEC], out_specs=VMEM_SPEC,
                          out_shape=jax.ShapeDtypeStruct(all_packs.shape[1:], F32), name="sum_packs")(all_packs)


def _adamw(w, g, m, v, name, g_transposed=False):
    R, C = w.shape
    rt = _half_tile(R, mult=LANES if g_transposed else 8, want=256)

    def body(w_ref, g_ref, m_ref, v_ref, g_out_ref, d_ref, nm_ref, nv_ref):
        gg = g_ref[...].T if g_transposed else g_ref[...]
        g_out_ref[...] = gg
        d_ref[...], nm_ref[...], nv_ref[...] = _adamw_update(w_ref[...], gg, m_ref[...], v_ref[...])

    spec = pl.BlockSpec((rt, C), lambda i: (i, 0))
    g_spec = pl.BlockSpec((C, rt), lambda i: (0, i)) if g_transposed else spec
    return pl.pallas_call(body, grid=(R // rt,), in_specs=[spec, g_spec, spec, spec], out_specs=[spec] * 4,
                          out_shape=[jax.ShapeDtypeStruct((R, C), F32)] * 4,
                          compiler_params=_params(1), name=name)(w, g, m, v)


def _adamw_update(w, g, m, v):
    nm = ADAM_B1 * m + (1.0 - ADAM_B1) * g
    nv = ADAM_B2 * v + (1.0 - ADAM_B2) * (g * g)
    m_hat = nm / (1.0 - ADAM_B1 ** ADAM_STEP)
    v_hat = nv / (1.0 - ADAM_B2 ** ADAM_STEP)
    return -ADAM_LR * (m_hat / (jnp.sqrt(v_hat) + ADAM_EPS) + ADAM_WD * w), nm, nv


def _adamw_many(ws, gs, ms, vs, name):
    n = len(ws)

    def body(*refs):
        for i in range(n):
            d, nm, nv = _adamw_update(refs[i][...], refs[n + i][...], refs[2 * n + i][...], refs[3 * n + i][...])
            refs[4 * n + i][...] = d
            refs[5 * n + i][...] = nm
            refs[6 * n + i][...] = nv

    return pl.pallas_call(body, in_specs=[VMEM_SPEC] * (4 * n), out_specs=[VMEM_SPEC] * (3 * n),
                          out_shape=[jax.ShapeDtypeStruct(w.shape, F32) for w in ws] * 3, name=name,
                          )(*ws, *gs, *ms, *vs)


def _pack(pieces):
    rows = []
    for p in pieces:
        flat = p.reshape(-1)
        pad = (-flat.shape[0]) % LANES
        if pad:
            flat = jnp.concatenate([flat, jnp.zeros((pad,), F32)])
        rows.append(flat.reshape(-1, LANES))
    total = sum(r.shape[0] for r in rows)
    pad_rows = (-total) % 8
    if pad_rows:
        rows.append(jnp.zeros((pad_rows, LANES), F32))
    return jnp.concatenate(rows, axis=0)


def _unpack(buf, shapes):
    out, r0 = [], 0
    for shp in shapes:
        n = int(np.prod(shp))
        nr = -(-n // LANES)
        out.append(buf[r0:r0 + nr].reshape(-1)[:n].reshape(shp))
        r0 += nr
    return out


SMALL_NAMES = ("rel_table", "b_in", "conv_w", "conv_b", "conv_ln_g", "conv_ln_b", "attn_norm_g", "conv_norm_g",
               "ln1_g", "ln1_b", "ffn_conv_w", "ffn_conv_b", "ln2_g", "ln2_b")
BIG_NAMES = ("w_in", "w_out", "w_up", "w_down")
WEIGHT_ORDER = ("rel_table", "w_in", "b_in", "conv_w", "conv_b", "conv_ln_g", "conv_ln_b", "attn_norm_g",
                "conv_norm_g", "w_out", "ln1_g", "ln1_b", "w_up", "ffn_conv_w", "ffn_conv_b", "w_down",
                "ln2_g", "ln2_b")


def kernel(x, rel_table, w_in, b_in, conv_w, conv_b, conv_ln_g, conv_ln_b, attn_norm_g, conv_norm_g, w_out, ln1_g, ln1_b, w_up, ffn_conv_w, ffn_conv_b, w_down, ln2_g, ln2_b, loss_target, m_rel_table, m_w_in, m_b_in, m_conv_w, m_conv_b, m_conv_ln_g, m_conv_ln_b, m_attn_norm_g, m_conv_norm_g, m_w_out, m_ln1_g, m_ln1_b, m_w_up, m_ffn_conv_w, m_ffn_conv_b, m_w_down, m_ln2_g, m_ln2_b, v_rel_table, v_w_in, v_b_in, v_conv_w, v_conv_b, v_conv_ln_g, v_conv_ln_b, v_attn_norm_g, v_conv_norm_g, v_w_out, v_ln1_g, v_ln1_b, v_w_up, v_ffn_conv_w, v_ffn_conv_b, v_w_down, v_ln2_g, v_ln2_b):
    args = dict(locals())
    weights = {n: args[n] for n in WEIGHT_ORDER}
    moms = {n: args["m_" + n] for n in WEIGHT_ORDER}
    vels = {n: args["v_" + n] for n in WEIGHT_ORDER}
    xi, yi, ci = _place()
    ids = jnp.stack([xi, yi, ci]).astype(jnp.int32)
    shard = 2 * xi + yi
    D = x.shape[-1]
    DFF = w_down.shape[1] * N_SHARDS
    CW = conv_norm_g.shape[-1]

    tr = lambda t: jnp.transpose(t[0])
    (g_in,), (g_cw, g_fcw) = _gather_weights([tr(w_in)], [conv_w[0], ffn_conv_w[0]])
    cols = lambda t: jnp.transpose(t, (1, 0, 2)).reshape(t.shape[1], N_SHARDS * t.shape[2])
    staged = [_stage_half(w[0], ids, name="stage_" + n) for w, n in ((w_out, "w_out"), (w_up, "w_up"),
                                                                     (w_down, "w_down"))]

    grad_x, fulls, all_packs = _local_step(
        x, loss_target, rel_table, g_in.reshape(-1, D), b_in, cols(g_cw), conv_b, conv_ln_g, conv_ln_b, attn_norm_g,
        conv_norm_g, staged, ln1_g, ln1_b, cols(g_fcw), ffn_conv_b, ln2_g, ln2_b, ids)
    big_grads = dict(zip(BIG_NAMES, _sibling_assemble(fulls)))

    summed = _sum_packs(all_packs)
    full_shapes = {n: weights[n].shape for n in SMALL_NAMES}
    full_shapes["conv_w"] = (1, CONV_KERNEL, CW)
    full_shapes["ffn_conv_w"] = (1, FFN_CONV_KERNEL, 2 * DFF)
    un = _unpack(summed, [(1, LANES)] + [full_shapes[n] for n in SMALL_NAMES])
    loss = un[0][0, 0]
    small_grads = dict(zip(SMALL_NAMES, un[1:]))
    for n in ("conv_w", "ffn_conv_w"):
        width = weights[n].shape[-1]
        small_grads[n] = lax.dynamic_slice_in_dim(small_grads[n], shard * width, width, axis=2)

    grads, delta, new_m, new_v = {}, {}, {}, {}
    for n in BIG_NAMES:
        shp = weights[n].shape
        g2 = big_grads[n]
        if n == "w_in":
            res = [jnp.transpose(t) for t in _adamw(tr(weights[n]), g2, tr(moms[n]), tr(vels[n]), name="adamw_" + n)]
        else:
            res = _adamw(weights[n][0], g2, moms[n][0], vels[n][0], name="adamw_" + n, g_transposed=n == "w_up")
        grads[n], delta[n], new_m[n], new_v[n] = (t.reshape(shp) for t in res)
    pick = lambda src: [src[n] for n in SMALL_NAMES]
    small_out = _adamw_many(pick(weights), pick(small_grads), pick(moms), pick(vels), name="adamw_small")
    ns = len(SMALL_NAMES)
    for tgt, part in ((delta, small_out[:ns]), (new_m, small_out[ns:2 * ns]), (new_v, small_out[2 * ns:])):
        tgt.update(zip(SMALL_NAMES, part))
    grads.update(small_grads)

    return (loss, grad_x, *[grads[n] for n in WEIGHT_ORDER], *[delta[n] for n in WEIGHT_ORDER],
            *[new_m[n] for n in WEIGHT_ORDER], *[new_v[n] for n in WEIGHT_ORDER])
```
